```python
import jax, jax.numpy as jnp
from jax import lax
import numpy as np

D_MODEL = 1024
BATCH = 8
SEQ = 4096
DEPTH = 1

MLA_HEADS = 4
V_HEAD_DIM = D_MODEL // (2 * MLA_HEADS)
QK_NOPE_DIM = V_HEAD_DIM
QK_ROPE_DIM = V_HEAD_DIM // 2
Q_LORA_RANK = D_MODEL // 4
KV_LORA_RANK = D_MODEL // 8
MLA_WIDTH = MLA_HEADS * V_HEAD_DIM
POOL_WIDTH = D_MODEL - MLA_WIDTH
POOL_WINDOWS = (2, 4, 8, 16)
POOL_GROUPS = len(POOL_WINDOWS)
POOL_CH = POOL_WIDTH // POOL_GROUPS
IN_PROJ_DIM = Q_LORA_RANK + KV_LORA_RANK + QK_ROPE_DIM + POOL_WIDTH
D_FF = ((8 * D_MODEL // 3 + 255) // 256) * 256
N_MEM = 256
MEM_HEADS = 4
MEM_HEAD_DIM = D_MODEL // MEM_HEADS
ROPE_BASE = 10000.0
RMS_EPS = 1e-6
BLOCK_Q = 128

kernel_name = "hymba_mla_pool_macaron_memxattn"


def rmsnorm(x, g):
    xf = x.astype(jnp.float32)
    y = xf * lax.rsqrt(jnp.mean(xf * xf, axis=-1, keepdims=True) + RMS_EPS)
    return (y * g.astype(jnp.float32)).astype(x.dtype)


def swiglu(x, w_gate, w_up, w_down):
    return (jax.nn.silu(x @ w_gate) * (x @ w_up)) @ w_down


def rope(x, positions):
    d = x.shape[-1]
    freqs = 1.0 / (ROPE_BASE ** (jnp.arange(0, d, 2, dtype=jnp.float32) / d))
    ang = positions.astype(jnp.float32)[..., None] * freqs
    cos = jnp.cos(ang)[:, :, None, :]
    sin = jnp.sin(ang)[:, :, None, :]
    xf = x.astype(jnp.float32)
    x1, x2 = xf[..., : d // 2], xf[..., d // 2:]
    out = jnp.concatenate([x1 * cos - x2 * sin, x1 * sin + x2 * cos], axis=-1)
    return out.astype(x.dtype)


def causal_block_attention(q, k, v, scale):
    B, S, H, Dqk = q.shape
    nblk = S // BLOCK_Q
    qb = q.reshape(B, nblk, BLOCK_Q, H, Dqk).transpose(1, 0, 2, 3, 4)
    key_pos = jnp.arange(S)

    def one_block(args):
        qi, i = args
        s = jnp.einsum('bqhd,bkhd->bhqk', qi, k).astype(jnp.float32) * scale
        q_pos = i * BLOCK_Q + jnp.arange(BLOCK_Q)
        mask = key_pos[None, :] <= q_pos[:, None]
        s = jnp.where(mask[None, None], s, -jnp.inf)
        p = jax.nn.softmax(s, axis=-1).astype(v.dtype)
        return jnp.einsum('bhqk,bkhd->bqhd', p, v)

    out = lax.map(one_block, (qb, jnp.arange(nblk)))
    return out.transpose(1, 0, 2, 3, 4).reshape(B, S, H, v.shape[-1])


def mla_group(c_q, c_kv, k_rope, positions, q_norm, w_q_up, kv_norm, w_kv_up):
    B, S, _ = c_q.shape
    q = (rmsnorm(c_q, q_norm) @ w_q_up).reshape(B, S, MLA_HEADS, QK_NOPE_DIM + QK_ROPE_DIM)
    q_nope, q_pe = q[..., :QK_NOPE_DIM], q[..., QK_NOPE_DIM:]
    q_pe = rope(q_pe, positions)
    kv = (rmsnorm(c_kv, kv_norm) @ w_kv_up).reshape(B, S, MLA_HEADS, QK_NOPE_DIM + V_HEAD_DIM)
    k_nope, v = kv[..., :QK_NOPE_DIM], kv[..., QK_NOPE_DIM:]
    k_pe = rope(k_rope[:, :, None, :], positions)
    k_pe = jnp.broadcast_to(k_pe, (B, S, MLA_HEADS, QK_ROPE_DIM))
    q_full = jnp.concatenate([q_nope, q_pe], axis=-1)
    k_full = jnp.concatenate([k_nope, k_pe], axis=-1)
    scale = (QK_NOPE_DIM + QK_ROPE_DIM) ** -0.5
    o = causal_block_attention(q_full, k_full, v, scale)
    return o.reshape(B, S, MLA_WIDTH)


def pool_group(z, pool_w, pool_scale):
    B, S, _ = z.shape
    zg = z.reshape(B, S, POOL_GROUPS, POOL_CH)
    cs = jnp.cumsum(zg.astype(jnp.float32), axis=1)
    cs = jnp.pad(cs, ((0, 0), (1, 0), (0, 0), (0, 0)))
    t1 = jnp.arange(1, S + 1, dtype=jnp.float32)
    pooled = []
    for g, w in enumerate(POOL_WINDOWS):
        hi = cs[:, 1:, g]
        lo = jnp.pad(cs[:, : S + 1 - w, g], ((0, 0), (w - 1, 0), (0, 0)))
        count = jnp.minimum(t1, float(w))[None, :, None]
        pooled.append((hi - lo) / count)
    mean = jnp.stack(pooled, axis=2).astype(z.dtype)
    y = jnp.einsum('bsgc,gcd->bsgd', mean - zg, pool_w).reshape(B, S, POOL_WIDTH)
    return y * pool_scale


def memory_cross_attention(hn, memn, w_mq, w_mkv, w_mo):
    B, S, _ = hn.shape
    q = (hn @ w_mq).reshape(B, S, MEM_HEADS, MEM_HEAD_DIM)
    kv = (memn @ w_mkv).reshape(B, N_MEM, 2, MEM_HEADS, MEM_HEAD_DIM)
    k, v = kv[:, :, 0], kv[:, :, 1]
    s = jnp.einsum('bshd,bmhd->bhsm', q, k).astype(jnp.float32) * (MEM_HEAD_DIM ** -0.5)
    p = jax.nn.softmax(s, axis=-1).astype(v.dtype)
    o = jnp.einsum('bhsm,bmhd->bshd', p, v).reshape(B, S, D_MODEL)
    return o @ w_mo


def _fwd_setup_inputs(seed: int = 0) -> dict:
    key = jax.random.key(seed)
    ks = iter(jax.random.split(key, 32))

    def dense(shape, fan_in, mult=1.0):
        return jax.random.normal(next(ks), shape, jnp.float32) * (mult * fan_in ** -0.5)

    def gain(shape):
        return 1.0 + 0.02 * jax.random.normal(next(ks), shape, jnp.float32)

    L = DEPTH
    x = jax.random.normal(next(ks), (BATCH, SEQ, D_MODEL), jnp.float32)
    mem = jax.random.normal(next(ks), (BATCH, N_MEM, D_MODEL), jnp.float32)
    offset = jax.random.randint(next(ks), (BATCH, 1), 0, 1024, dtype=jnp.int32)
    positions = (jnp.arange(SEQ, dtype=jnp.int32)[None, :] + offset).astype(jnp.int32)
    return {
        "x": x,
        "mem": mem,
        "positions": positions,
        "ffn1_norm": gain((L, D_MODEL)),
        "ffn1_w_gate": dense((L, D_MODEL, D_FF), D_MODEL),
        "ffn1_w_up": dense((L, D_MODEL, D_FF), D_MODEL),
        "ffn1_w_down": dense((L, D_FF, D_MODEL), D_FF),
        "mix_norm": gain((L, D_MODEL)),
        "w_in": dense((L, D_MODEL, IN_PROJ_DIM), D_MODEL),
        "q_norm": gain((L, Q_LORA_RANK)),
        "w_q_up": dense((L, Q_LORA_RANK, MLA_HEADS * (QK_NOPE_DIM + QK_ROPE_DIM)), Q_LORA_RANK),
        "kv_norm": gain((L, KV_LORA_RANK)),
        "w_kv_up": dense((L, KV_LORA_RANK, MLA_HEADS * (QK_NOPE_DIM + V_HEAD_DIM)), KV_LORA_RANK),
        "pool_w": dense((L, POOL_GROUPS, POOL_CH, POOL_CH), POOL_CH),
        "pool_scale": 0.5 + 0.05 * jax.random.normal(next(ks), (L, POOL_WIDTH), jnp.float32),
        "w_out": dense((L, MLA_WIDTH + POOL_WIDTH, D_MODEL), MLA_WIDTH + POOL_WIDTH),
        "xattn_norm": gain((L, D_MODEL)),
        "mem_norm": gain((L, D_MODEL)),
        "w_mq": dense((L, D_MODEL, D_MODEL), D_MODEL),
        "w_mkv": dense((L, D_MODEL, 2 * D_MODEL), D_MODEL),
        "w_mo": dense((L, D_MODEL, D_MODEL), D_MODEL),
        "ffn2_norm": gain((L, D_MODEL)),
        "ffn2_w_gate": dense((L, D_MODEL, D_FF), D_MODEL),
        "ffn2_w_up": dense((L, D_MODEL, D_FF), D_MODEL),
        "ffn2_w_down": dense((L, D_FF, D_MODEL), D_FF),
        "final_norm": gain((D_MODEL,)),
    }


def _fwd_reference(x, mem, positions, ffn1_norm, ffn1_w_gate, ffn1_w_up, ffn1_w_down, mix_norm, w_in,
              q_norm, w_q_up, kv_norm, w_kv_up, pool_w, pool_scale, w_out, xattn_norm, mem_norm,
              w_mq, w_mkv, w_mo, ffn2_norm, ffn2_w_gate, ffn2_w_up, ffn2_w_down, final_norm):
    h = x
    split_pts = [Q_LORA_RANK, Q_LORA_RANK + KV_LORA_RANK, Q_LORA_RANK + KV_LORA_RANK + QK_ROPE_DIM]
    for l in range(DEPTH):
        h = h + 0.5 * swiglu(rmsnorm(h, ffn1_norm[l]), ffn1_w_gate[l], ffn1_w_up[l], ffn1_w_down[l])
        z = rmsnorm(h, mix_norm[l]) @ w_in[l]
        c_q, c_kv, k_rope, z_pool = jnp.split(z, split_pts, axis=-1)
        a = mla_group(c_q, c_kv, k_rope, positions, q_norm[l], w_q_up[l], kv_norm[l], w_kv_up[l])
        p = pool_group(z_pool, pool_w[l], pool_scale[l])
        h = h + jnp.concatenate([a, p], axis=-1) @ w_out[l]
        h = h + memory_cross_attention(rmsnorm(h, xattn_norm[l]), rmsnorm(mem, mem_norm[l]),
                                       w_mq[l], w_mkv[l], w_mo[l])
        h = h + 0.5 * swiglu(rmsnorm(h, ffn2_norm[l]), ffn2_w_gate[l], ffn2_w_up[l], ffn2_w_down[l])
    return rmsnorm(h, final_norm)


import jax as _jax
import jax.numpy as _jnp

TWIN_FORMAT = 'train_step'
FWD_PARAMS = ['x', 'mem', 'positions', 'ffn1_norm', 'ffn1_w_gate', 'ffn1_w_up', 'ffn1_w_down', 'mix_norm', 'w_in', 'q_norm', 'w_q_up', 'kv_norm', 'w_kv_up', 'pool_w', 'pool_scale', 'w_out', 'xattn_norm', 'mem_norm', 'w_mq', 'w_mkv', 'w_mo', 'ffn2_norm', 'ffn2_w_gate', 'ffn2_w_up', 'ffn2_w_down', 'final_norm']
TWIN_WEIGHTS = ['ffn1_norm', 'ffn1_w_gate', 'ffn1_w_up', 'ffn1_w_down', 'mix_norm', 'w_in', 'q_norm', 'w_q_up', 'kv_norm', 'w_kv_up', 'pool_w', 'pool_scale', 'w_out', 'xattn_norm', 'mem_norm', 'w_mq', 'w_mkv', 'w_mo', 'ffn2_norm', 'ffn2_w_gate', 'ffn2_w_up', 'ffn2_w_down', 'final_norm']
TWIN_DIFF_INPUT = 'x'
TWIN_INPUTS = ['x', 'mem', 'positions', 'ffn1_norm', 'ffn1_w_gate', 'ffn1_w_up', 'ffn1_w_down', 'mix_norm', 'w_in', 'q_norm', 'w_q_up', 'kv_norm', 'w_kv_up', 'pool_w', 'pool_scale', 'w_out', 'xattn_norm', 'mem_norm', 'w_mq', 'w_mkv', 'w_mo', 'ffn2_norm', 'ffn2_w_gate', 'ffn2_w_up', 'ffn2_w_down', 'final_norm', 'loss_target', 'm_ffn1_norm', 'm_ffn1_w_gate', 'm_ffn1_w_up', 'm_ffn1_w_down', 'm_mix_norm', 'm_w_in', 'm_q_norm', 'm_w_q_up', 'm_kv_norm', 'm_w_kv_up', 'm_pool_w', 'm_pool_scale', 'm_w_out', 'm_xattn_norm', 'm_mem_norm', 'm_w_mq', 'm_w_mkv', 'm_w_mo', 'm_ffn2_norm', 'm_ffn2_w_gate', 'm_ffn2_w_up', 'm_ffn2_w_down', 'm_final_norm', 'v_ffn1_norm', 'v_ffn1_w_gate', 'v_ffn1_w_up', 'v_ffn1_w_down', 'v_mix_norm', 'v_w_in', 'v_q_norm', 'v_w_q_up', 'v_kv_norm', 'v_w_kv_up', 'v_pool_w', 'v_pool_scale', 'v_w_out', 'v_xattn_norm', 'v_mem_norm', 'v_w_mq', 'v_w_mkv', 'v_w_mo', 'v_ffn2_norm', 'v_ffn2_w_gate', 'v_ffn2_w_up', 'v_ffn2_w_down', 'v_final_norm']
TWIN_OUTPUTS = ['loss', 'grad_x', 'grad_ffn1_norm', 'grad_ffn1_w_gate', 'grad_ffn1_w_up', 'grad_ffn1_w_down', 'grad_mix_norm', 'grad_w_in', 'grad_q_norm', 'grad_w_q_up', 'grad_kv_norm', 'grad_w_kv_up', 'grad_pool_w', 'grad_pool_scale', 'grad_w_out', 'grad_xattn_norm', 'grad_mem_norm', 'grad_w_mq', 'grad_w_mkv', 'grad_w_mo', 'grad_ffn2_norm', 'grad_ffn2_w_gate', 'grad_ffn2_w_up', 'grad_ffn2_w_down', 'grad_final_norm', 'delta_ffn1_norm', 'delta_ffn1_w_gate', 'delta_ffn1_w_up', 'delta_ffn1_w_down', 'delta_mix_norm', 'delta_w_in', 'delta_q_norm', 'delta_w_q_up', 'delta_kv_norm', 'delta_w_kv_up', 'delta_pool_w', 'delta_pool_scale', 'delta_w_out', 'delta_xattn_norm', 'delta_mem_norm', 'delta_w_mq', 'delta_w_mkv', 'delta_w_mo', 'delta_ffn2_norm', 'delta_ffn2_w_gate', 'delta_ffn2_w_up', 'delta_ffn2_w_down', 'delta_final_norm', 'new_m_ffn1_norm', 'new_m_ffn1_w_gate', 'new_m_ffn1_w_up', 'new_m_ffn1_w_down', 'new_m_mix_norm', 'new_m_w_in', 'new_m_q_norm', 'new_m_w_q_up', 'new_m_kv_norm', 'new_m_w_kv_up', 'new_m_pool_w', 'new_m_pool_scale', 'new_m_w_out', 'new_m_xattn_norm', 'new_m_mem_norm', 'new_m_w_mq', 'new_m_w_mkv', 'new_m_w_mo', 'new_m_ffn2_norm', 'new_m_ffn2_w_gate', 'new_m_ffn2_w_up', 'new_m_ffn2_w_down', 'new_m_final_norm', 'new_v_ffn1_norm', 'new_v_ffn1_w_gate', 'new_v_ffn1_w_up', 'new_v_ffn1_w_down', 'new_v_mix_norm', 'new_v_w_in', 'new_v_q_norm', 'new_v_w_q_up', 'new_v_kv_norm', 'new_v_w_kv_up', 'new_v_pool_w', 'new_v_pool_scale', 'new_v_w_out', 'new_v_xattn_norm', 'new_v_mem_norm', 'new_v_w_mq', 'new_v_w_mkv', 'new_v_w_mo', 'new_v_ffn2_norm', 'new_v_ffn2_w_gate', 'new_v_ffn2_w_up', 'new_v_ffn2_w_down', 'new_v_final_norm']
TWIN_LEAF_KINDS = {'loss': 'loss', 'grad_x': 'grad_x', 'grad_ffn1_norm': 'grad_w', 'grad_ffn1_w_gate': 'grad_w', 'grad_ffn1_w_up': 'grad_w', 'grad_ffn1_w_down': 'grad_w', 'grad_mix_norm': 'grad_w', 'grad_w_in': 'grad_w', 'grad_q_norm': 'grad_w', 'grad_w_q_up': 'grad_w', 'grad_kv_norm': 'grad_w', 'grad_w_kv_up': 'grad_w', 'grad_pool_w': 'grad_w', 'grad_pool_scale': 'grad_w', 'grad_w_out': 'grad_w', 'grad_xattn_norm': 'grad_w', 'grad_mem_norm': 'grad_w', 'grad_w_mq': 'grad_w', 'grad_w_mkv': 'grad_w', 'grad_w_mo': 'grad_w', 'grad_ffn2_norm': 'grad_w', 'grad_ffn2_w_gate': 'grad_w', 'grad_ffn2_w_up': 'grad_w', 'grad_ffn2_w_down': 'grad_w', 'grad_final_norm': 'grad_w', 'delta_ffn1_norm': 'delta_w', 'delta_ffn1_w_gate': 'delta_w', 'delta_ffn1_w_up': 'delta_w', 'delta_ffn1_w_down': 'delta_w', 'delta_mix_norm': 'delta_w', 'delta_w_in': 'delta_w', 'delta_q_norm': 'delta_w', 'delta_w_q_up': 'delta_w', 'delta_kv_norm': 'delta_w', 'delta_w_kv_up': 'delta_w', 'delta_pool_w': 'delta_w', 'delta_pool_scale': 'delta_w', 'delta_w_out': 'delta_w', 'delta_xattn_norm': 'delta_w', 'delta_mem_norm': 'delta_w', 'delta_w_mq': 'delta_w', 'delta_w_mkv': 'delta_w', 'delta_w_mo': 'delta_w', 'delta_ffn2_norm': 'delta_w', 'delta_ffn2_w_gate': 'delta_w', 'delta_ffn2_w_up': 'delta_w', 'delta_ffn2_w_down': 'delta_w', 'delta_final_norm': 'delta_w', 'new_m_ffn1_norm': 'new_m', 'new_m_ffn1_w_gate': 'new_m', 'new_m_ffn1_w_up': 'new_m', 'new_m_ffn1_w_down': 'new_m', 'new_m_mix_norm': 'new_m', 'new_m_w_in': 'new_m', 'new_m_q_norm': 'new_m', 'new_m_w_q_up': 'new_m', 'new_m_kv_norm': 'new_m', 'new_m_w_kv_up': 'new_m', 'new_m_pool_w': 'new_m', 'new_m_pool_scale': 'new_m', 'new_m_w_out': 'new_m', 'new_m_xattn_norm': 'new_m', 'new_m_mem_norm': 'new_m', 'new_m_w_mq': 'new_m', 'new_m_w_mkv': 'new_m', 'new_m_w_mo': 'new_m', 'new_m_ffn2_norm': 'new_m', 'new_m_ffn2_w_gate': 'new_m', 'new_m_ffn2_w_up': 'new_m', 'new_m_ffn2_w_down': 'new_m', 'new_m_final_norm': 'new_m', 'new_v_ffn1_norm': 'new_v', 'new_v_ffn1_w_gate': 'new_v', 'new_v_ffn1_w_up': 'new_v', 'new_v_ffn1_w_down': 'new_v', 'new_v_mix_norm': 'new_v', 'new_v_w_in': 'new_v', 'new_v_q_norm': 'new_v', 'new_v_w_q_up': 'new_v', 'new_v_kv_norm': 'new_v', 'new_v_w_kv_up': 'new_v', 'new_v_pool_w': 'new_v', 'new_v_pool_scale': 'new_v', 'new_v_w_out': 'new_v', 'new_v_xattn_norm': 'new_v', 'new_v_mem_norm': 'new_v', 'new_v_w_mq': 'new_v', 'new_v_w_mkv': 'new_v', 'new_v_w_mo': 'new_v', 'new_v_ffn2_norm': 'new_v', 'new_v_ffn2_w_gate': 'new_v', 'new_v_ffn2_w_up': 'new_v', 'new_v_ffn2_w_down': 'new_v', 'new_v_final_norm': 'new_v'}


def _forward(args):
    return _fwd_reference(*[args[k] for k in FWD_PARAMS])


def _output_shape():
    def fwd():
        inp = _fwd_setup_inputs(0)
        return _fwd_reference(*[inp[k] for k in FWD_PARAMS])
    out = _jax.eval_shape(fwd)
    return out.shape, out.dtype

N_MICROBATCH = 1
ADAM_LR = 0.001
ADAM_B1 = 0.9
ADAM_B2 = 0.999
ADAM_EPS = 1e-08
ADAM_WD = 0.01
ADAM_STEP = 10
PER_EXAMPLE_BATCH_AXIS = {'x': 0, 'mem': 0, 'positions': 0, 'loss_target': 0}
SHARED_INPUTS = []
_WEIGHT_DTYPES = {'ffn1_norm': _jnp.float32, 'ffn1_w_gate': _jnp.float32, 'ffn1_w_up': _jnp.float32, 'ffn1_w_down': _jnp.float32, 'mix_norm': _jnp.float32, 'w_in': _jnp.float32, 'q_norm': _jnp.float32, 'w_q_up': _jnp.float32, 'kv_norm': _jnp.float32, 'w_kv_up': _jnp.float32, 'pool_w': _jnp.float32, 'pool_scale': _jnp.float32, 'w_out': _jnp.float32, 'xattn_norm': _jnp.float32, 'mem_norm': _jnp.float32, 'w_mq': _jnp.float32, 'w_mkv': _jnp.float32, 'w_mo': _jnp.float32, 'ffn2_norm': _jnp.float32, 'ffn2_w_gate': _jnp.float32, 'ffn2_w_up': _jnp.float32, 'ffn2_w_down': _jnp.float32, 'final_norm': _jnp.float32}
MOMENT_SCALE = {'ffn1_norm': 7.855726e-02, 'ffn1_w_gate': 3.290371e-02, 'ffn1_w_up': 3.191869e-02, 'ffn1_w_down': 5.293351e-02, 'mix_norm': 6.320890e-02, 'w_in': 6.705520e-02, 'q_norm': 4.449869e-02, 'w_q_up': 2.515412e-02, 'kv_norm': 1.018135e-01, 'w_kv_up': 3.180212e-02, 'pool_w': 7.541789e-02, 'pool_scale': 1.759898e-01, 'w_out': 5.814012e-02, 'xattn_norm': 1.926912e-02, 'mem_norm': 2.858513e-02, 'w_mq': 1.770204e-02, 'w_mkv': 1.784018e-02, 'w_mo': 1.789090e-02, 'ffn2_norm': 6.858749e-02, 'ffn2_w_gate': 2.905666e-02, 'ffn2_w_up': 2.817373e-02, 'ffn2_w_down': 4.687665e-02, 'final_norm': 3.199521e+01}


def _to_microbatches(a, axis):
    t = _jnp.moveaxis(a, axis, 0)
    t = t.reshape((N_MICROBATCH, t.shape[0] // N_MICROBATCH) + t.shape[1:])
    return _jnp.moveaxis(t, 1, axis + 1)


def setup_inputs(seed: int = 0) -> dict:
    inp = _fwd_setup_inputs(seed)
    key = _jax.random.fold_in(_jax.random.key(seed), 7919)
    shape, _ = _output_shape()
    out = dict(inp)
    out["loss_target"] = _jax.random.normal(_jax.random.fold_in(key, 0), shape, _jnp.float32)
    for i, name in enumerate(TWIN_WEIGHTS):
        w = inp[name].astype(_jnp.float32)
        if MOMENT_SCALE is None:
            s = _jnp.sqrt(_jnp.mean(_jnp.square(w)) + 1e-30)
        else:
            s = MOMENT_SCALE[name]
        km, kv = _jax.random.split(_jax.random.fold_in(key, i + 1))
        out[name] = w
        out["m_" + name] = s * _jax.random.normal(km, w.shape, _jnp.float32)
        out["v_" + name] = (s * s) * _jax.random.uniform(kv, w.shape, _jnp.float32, 0.5, 1.5)
    if N_MICROBATCH > 1:
        for name, axis in PER_EXAMPLE_BATCH_AXIS.items():
            out[name] = _to_microbatches(out[name], axis)
    return {'x': out['x'], 'mem': out['mem'], 'positions': out['positions'], 'ffn1_norm': out['ffn1_norm'], 'ffn1_w_gate': out['ffn1_w_gate'], 'ffn1_w_up': out['ffn1_w_up'], 'ffn1_w_down': out['ffn1_w_down'], 'mix_norm': out['mix_norm'], 'w_in': out['w_in'], 'q_norm': out['q_norm'], 'w_q_up': out['w_q_up'], 'kv_norm': out['kv_norm'], 'w_kv_up': out['w_kv_up'], 'pool_w': out['pool_w'], 'pool_scale': out['pool_scale'], 'w_out': out['w_out'], 'xattn_norm': out['xattn_norm'], 'mem_norm': out['mem_norm'], 'w_mq': out['w_mq'], 'w_mkv': out['w_mkv'], 'w_mo': out['w_mo'], 'ffn2_norm': out['ffn2_norm'], 'ffn2_w_gate': out['ffn2_w_gate'], 'ffn2_w_up': out['ffn2_w_up'], 'ffn2_w_down': out['ffn2_w_down'], 'final_norm': out['final_norm'], 'loss_target': out['loss_target'], 'm_ffn1_norm': out['m_ffn1_norm'], 'm_ffn1_w_gate': out['m_ffn1_w_gate'], 'm_ffn1_w_up': out['m_ffn1_w_up'], 'm_ffn1_w_down': out['m_ffn1_w_down'], 'm_mix_norm': out['m_mix_norm'], 'm_w_in': out['m_w_in'], 'm_q_norm': out['m_q_norm'], 'm_w_q_up': out['m_w_q_up'], 'm_kv_norm': out['m_kv_norm'], 'm_w_kv_up': out['m_w_kv_up'], 'm_pool_w': out['m_pool_w'], 'm_pool_scale': out['m_pool_scale'], 'm_w_out': out['m_w_out'], 'm_xattn_norm': out['m_xattn_norm'], 'm_mem_norm': out['m_mem_norm'], 'm_w_mq': out['m_w_mq'], 'm_w_mkv': out['m_w_mkv'], 'm_w_mo': out['m_w_mo'], 'm_ffn2_norm': out['m_ffn2_norm'], 'm_ffn2_w_gate': out['m_ffn2_w_gate'], 'm_ffn2_w_up': out['m_ffn2_w_up'], 'm_ffn2_w_down': out['m_ffn2_w_down'], 'm_final_norm': out['m_final_norm'], 'v_ffn1_norm': out['v_ffn1_norm'], 'v_ffn1_w_gate': out['v_ffn1_w_gate'], 'v_ffn1_w_up': out['v_ffn1_w_up'], 'v_ffn1_w_down': out['v_ffn1_w_down'], 'v_mix_norm': out['v_mix_norm'], 'v_w_in': out['v_w_in'], 'v_q_norm': out['v_q_norm'], 'v_w_q_up': out['v_w_q_up'], 'v_kv_norm': out['v_kv_norm'], 'v_w_kv_up': out['v_w_kv_up'], 'v_pool_w': out['v_pool_w'], 'v_pool_scale': out['v_pool_scale'], 'v_w_out': out['v_w_out'], 'v_xattn_norm': out['v_xattn_norm'], 'v_mem_norm': out['v_mem_norm'], 'v_w_mq': out['v_w_mq'], 'v_w_mkv': out['v_w_mkv'], 'v_w_mo': out['v_w_mo'], 'v_ffn2_norm': out['v_ffn2_norm'], 'v_ffn2_w_gate': out['v_ffn2_w_gate'], 'v_ffn2_w_up': out['v_ffn2_w_up'], 'v_ffn2_w_down': out['v_ffn2_w_down'], 'v_final_norm': out['v_final_norm']}


def _loss(weights, diff, rest, loss_target):
    with _jax.named_scope("forward"):
        args = {**rest, TWIN_DIFF_INPUT: diff, **{k: w.astype(_WEIGHT_DTYPES[k]) for k, w in weights.items()}}
        y = _forward(args)
    with _jax.named_scope("loss_head"):
        err = _jnp.square(y.astype(_jnp.float32) - loss_target)
        return 0.5 * _jnp.sum(_jnp.mean(err, axis=-1)) if err.ndim else 0.5 * err


def _adamw(w, g, m, v):
    m = ADAM_B1 * m + (1.0 - ADAM_B1) * g
    v = ADAM_B2 * v + (1.0 - ADAM_B2) * _jnp.square(g)
    m_hat = m / (1.0 - ADAM_B1 ** ADAM_STEP)
    v_hat = v / (1.0 - ADAM_B2 ** ADAM_STEP)
    delta = -ADAM_LR * (m_hat / (_jnp.sqrt(v_hat) + ADAM_EPS) + ADAM_WD * w)
    return delta, m, v


def reference(x, mem, positions, ffn1_norm, ffn1_w_gate, ffn1_w_up, ffn1_w_down, mix_norm, w_in, q_norm, w_q_up, kv_norm, w_kv_up, pool_w, pool_scale, w_out, xattn_norm, mem_norm, w_mq, w_mkv, w_mo, ffn2_norm, ffn2_w_gate, ffn2_w_up, ffn2_w_down, final_norm, loss_target, m_ffn1_norm, m_ffn1_w_gate, m_ffn1_w_up, m_ffn1_w_down, m_mix_norm, m_w_in, m_q_norm, m_w_q_up, m_kv_norm, m_w_kv_up, m_pool_w, m_pool_scale, m_w_out, m_xattn_norm, m_mem_norm, m_w_mq, m_w_mkv, m_w_mo, m_ffn2_norm, m_ffn2_w_gate, m_ffn2_w_up, m_ffn2_w_down, m_final_norm, v_ffn1_norm, v_ffn1_w_gate, v_ffn1_w_up, v_ffn1_w_down, v_mix_norm, v_w_in, v_q_norm, v_w_q_up, v_kv_norm, v_w_kv_up, v_pool_w, v_pool_scale, v_w_out, v_xattn_norm, v_mem_norm, v_w_mq, v_w_mkv, v_w_mo, v_ffn2_norm, v_ffn2_w_gate, v_ffn2_w_up, v_ffn2_w_down, v_final_norm):
    given = dict(x=x, mem=mem, positions=positions, ffn1_norm=ffn1_norm, ffn1_w_gate=ffn1_w_gate, ffn1_w_up=ffn1_w_up, ffn1_w_down=ffn1_w_down, mix_norm=mix_norm, w_in=w_in, q_norm=q_norm, w_q_up=w_q_up, kv_norm=kv_norm, w_kv_up=w_kv_up, pool_w=pool_w, pool_scale=pool_scale, w_out=w_out, xattn_norm=xattn_norm, mem_norm=mem_norm, w_mq=w_mq, w_mkv=w_mkv, w_mo=w_mo, ffn2_norm=ffn2_norm, ffn2_w_gate=ffn2_w_gate, ffn2_w_up=ffn2_w_up, ffn2_w_down=ffn2_w_down, final_norm=final_norm, loss_target=loss_target, m_ffn1_norm=m_ffn1_norm, m_ffn1_w_gate=m_ffn1_w_gate, m_ffn1_w_up=m_ffn1_w_up, m_ffn1_w_down=m_ffn1_w_down, m_mix_norm=m_mix_norm, m_w_in=m_w_in, m_q_norm=m_q_norm, m_w_q_up=m_w_q_up, m_kv_norm=m_kv_norm, m_w_kv_up=m_w_kv_up, m_pool_w=m_pool_w, m_pool_scale=m_pool_scale, m_w_out=m_w_out, m_xattn_norm=m_xattn_norm, m_mem_norm=m_mem_norm, m_w_mq=m_w_mq, m_w_mkv=m_w_mkv, m_w_mo=m_w_mo, m_ffn2_norm=m_ffn2_norm, m_ffn2_w_gate=m_ffn2_w_gate, m_ffn2_w_up=m_ffn2_w_up, m_ffn2_w_down=m_ffn2_w_down, m_final_norm=m_final_norm, v_ffn1_norm=v_ffn1_norm, v_ffn1_w_gate=v_ffn1_w_gate, v_ffn1_w_up=v_ffn1_w_up, v_ffn1_w_down=v_ffn1_w_down, v_mix_norm=v_mix_norm, v_w_in=v_w_in, v_q_norm=v_q_norm, v_w_q_up=v_w_q_up, v_kv_norm=v_kv_norm, v_w_kv_up=v_w_kv_up, v_pool_w=v_pool_w, v_pool_scale=v_pool_scale, v_w_out=v_w_out, v_xattn_norm=v_xattn_norm, v_mem_norm=v_mem_norm, v_w_mq=v_w_mq, v_w_mkv=v_w_mkv, v_w_mo=v_w_mo, v_ffn2_norm=v_ffn2_norm, v_ffn2_w_gate=v_ffn2_w_gate, v_ffn2_w_up=v_ffn2_w_up, v_ffn2_w_down=v_ffn2_w_down, v_final_norm=v_final_norm)
    weights = {n: given[n] for n in TWIN_WEIGHTS}
    shared = {n: given[n] for n in SHARED_INPUTS}
    per_example = {n: given[n] for n in ['x', 'mem', 'positions']}
    grad_fn = _jax.value_and_grad(_loss, argnums=(0, 1))

    def one_microbatch(ex, loss_target):
        ex = dict(ex)
        diff = ex.pop(TWIN_DIFF_INPUT)
        return grad_fn(weights, diff, {**shared, **ex}, loss_target)

    if N_MICROBATCH == 1:
        loss, (grad_w, grad_x) = one_microbatch(per_example, given["loss_target"])
    else:
        def body(carry, xs):
            loss_sum, grad_sum = carry
            l_k, (gw_k, gx_k) = one_microbatch(xs[0], xs[1])
            with _jax.named_scope("update"):
                return (loss_sum + l_k, _jax.tree.map(_jnp.add, grad_sum, gw_k)), gx_k

        init = (_jnp.zeros((), _jnp.float32), _jax.tree.map(_jnp.zeros_like, weights))
        (loss, grad_w), grad_x = _jax.lax.scan(body, init, (per_example, given["loss_target"]))
    with _jax.named_scope("update"):
        delta_w, new_m, new_v = {}, {}, {}
        for n in TWIN_WEIGHTS:
            delta_w[n], new_m[n], new_v[n] = _adamw(weights[n], grad_w[n], given["m_" + n], given["v_" + n])
    return (loss, grad_x, *[grad_w[n] for n in TWIN_WEIGHTS], *[delta_w[n] for n in TWIN_WEIGHTS],
            *[new_m[n] for n in TWIN_WEIGHTS], *[new_v[n] for n in TWIN_WEIGHTS])
```

```python
import functools

import jax
import jax.numpy as jnp
from jax import lax
from jax.experimental import pallas as pl
from jax.experimental.pallas import tpu as pltpu

F32 = jnp.float32
BF16 = jnp.bfloat16

D_MODEL = 1024
D_FF = 2816
N_CHIPS = 4
FF_SHARD = D_FF // N_CHIPS
MLA_HEADS = 4
Q_LORA = 256
KV_LORA = 128
ROPE_DIM = 64
HEAD_QK = 256
HEAD_V = 128
POOL_GROUPS = 4
POOL_CH = 128
MEM_HEADS = 4
MEM_HEAD_DIM = 256
RMS_EPS = 1e-6
ROPE_BASE = 10000.0
MLA_SCALE = (128 + 64) ** -0.5
MEM_SCALE = MEM_HEAD_DIM ** -0.5

ADAM_LR = 0.001
ADAM_B1 = 0.9
ADAM_B2 = 0.999
ADAM_EPS = 1e-08
ADAM_WD = 0.01
ADAM_STEP = 10

V7X_VMEM_LIMIT_BYTES = 56 * 1024 * 1024

NN = ((1,), (0,))
NT = ((1,), (1,))
TN = ((0,), (0,))


def _params(*sem):
    return pltpu.CompilerParams(dimension_semantics=sem, vmem_limit_bytes=V7X_VMEM_LIMIT_BYTES)


def _dot(a, b, dims):
    return lax.dot_general(a.astype(BF16), b.astype(BF16), (dims, ((), ())), preferred_element_type=F32)


_MAX_ROW_BLOCK = 1024
_ATT_BLOCK = 512


def _row_block(s, want=1024):
    return min(want, s, _MAX_ROW_BLOCK)


def _matmul(name, grid, terms, extras, outs, epilogue, acc_shape):
    nt, ne, no = len(terms), len(extras), len(outs)
    nk = grid[-1]
    dims = [t[4] for t in terms]

    def body(*refs):
        a_refs, b_refs = refs[:nt], refs[nt:2 * nt]
        e_refs = refs[2 * nt:2 * nt + ne]
        o_refs = refs[2 * nt + ne:2 * nt + ne + no]
        part = None
        for a, b, d in zip(a_refs, b_refs, dims):
            t = _dot(a[...], b[...], d)
            part = t if part is None else part + t

        def finish(acc):
            vals = epilogue(acc, *[e[...] for e in e_refs])
            for o, val in zip(o_refs, vals):
                o[...] = val.astype(o.dtype)

        if nk == 1:
            finish(part)
        else:
            acc_ref = refs[-1]
            k = pl.program_id(len(grid) - 1)

            @pl.when(k == 0)
            def _():
                acc_ref[...] = part

            @pl.when(k > 0)
            def _():
                acc_ref[...] += part

            @pl.when(k == nk - 1)
            def _():
                finish(acc_ref[...])

    in_specs = [t[1] for t in terms] + [t[3] for t in terms] + [e[1] for e in extras]
    args = [t[0] for t in terms] + [t[2] for t in terms] + [e[0] for e in extras]
    sem = ("parallel",) * (len(grid) - 1) + ("arbitrary",)
    return pl.pallas_call(
        body, name=name, grid=grid, in_specs=in_specs,
        out_specs=[o[1] for o in outs], out_shape=[o[0] for o in outs],
        scratch_shapes=[pltpu.VMEM(acc_shape, F32)] if nk > 1 else [],
        compiler_params=_params(*sem),
    )(*args)


def _ident(acc):
    return (acc,)


def _rmsnorm_fwd(name, x, gain, width, col_block=0):
    s = x.shape[0]
    bm = _row_block(s)

    def body(x_ref, g_ref, o_ref):
        xf = x_ref[...]
        r = lax.rsqrt(jnp.mean(xf * xf, axis=-1, keepdims=True) + RMS_EPS)
        o_ref[...] = ((xf * r) * g_ref[...]).astype(o_ref.dtype)

    return pl.pallas_call(
        body, name=name, grid=(s // bm,),
        in_specs=[pl.BlockSpec((bm, width), lambda i: (i, col_block)), pl.BlockSpec((1, width), lambda i: (0, 0))],
        out_specs=pl.BlockSpec((bm, width), lambda i: (i, 0)),
        out_shape=jax.ShapeDtypeStruct((s, width), BF16),
        compiler_params=_params("parallel"),
    )(x, gain)


def _rms_bwd_math(dy, xf, g, width):
    r = lax.rsqrt(jnp.mean(xf * xf, axis=-1, keepdims=True) + RMS_EPS)
    dyg = dy * g
    dot = jnp.sum(dyg * xf, axis=-1, keepdims=True)
    dx = r * dyg - xf * ((r * r * r) * (dot * (1.0 / width)))
    dgain = jnp.sum(dy * (xf * r), axis=0, keepdims=True)
    return dx, dgain


def _rmsnorm_bwd(name, dy, x, gain, width, col_block=0, dres=None, out_dtype=F32):
    s = x.shape[0]
    bm = _row_block(s)
    has_res = dres is not None

    def body(*refs):
        if has_res:
            dy_ref, x_ref, g_ref, r_ref, dx_ref, dg_ref = refs
        else:
            dy_ref, x_ref, g_ref, dx_ref, dg_ref = refs
        dx, dgain = _rms_bwd_math(dy_ref[...].astype(F32), x_ref[...], g_ref[...], width)
        if has_res:
            dx = dx + r_ref[...]
        dx_ref[...] = dx.astype(dx_ref.dtype)

        @pl.when(pl.program_id(0) == 0)
        def _():
            dg_ref[...] = dgain

        @pl.when(pl.program_id(0) > 0)
        def _():
            dg_ref[...] += dgain

    row = pl.BlockSpec((bm, width), lambda i: (i, 0))
    in_specs = [row, pl.BlockSpec((bm, width), lambda i: (i, col_block)), pl.BlockSpec((1, width), lambda i: (0, 0))]
    args = [dy, x, gain]
    if has_res:
        in_specs.append(row)
        args.append(dres)
    return pl.pallas_call(
        body, name=name, grid=(s // bm,), in_specs=in_specs,
        out_specs=[row, pl.BlockSpec((1, width), lambda i: (0, 0))],
        out_shape=[jax.ShapeDtypeStruct((s, width), out_dtype), jax.ShapeDtypeStruct((1, width), F32)],
        compiler_params=_params("arbitrary"),
    )(*args)


def _loss_and_final_norm(h, gain, target):
    s, d = h.shape
    bm = _row_block(s, 512)

    def body(h_ref, g_ref, t_ref, dh_ref, loss_ref, dg_ref):
        xf = h_ref[...]
        g = g_ref[...]
        r = lax.rsqrt(jnp.mean(xf * xf, axis=-1, keepdims=True) + RMS_EPS)
        err = (xf * r) * g - t_ref[...]
        part = 0.5 * jnp.sum(jnp.mean(err * err, axis=-1, keepdims=True), axis=0, keepdims=True)
        dx, dgain = _rms_bwd_math(err * (1.0 / d), xf, g, d)
        dh_ref[...] = dx

        @pl.when(pl.program_id(0) == 0)
        def _():
            dg_ref[...] = dgain
            loss_ref[...] = jnp.broadcast_to(part, loss_ref.shape)

        @pl.when(pl.program_id(0) > 0)
        def _():
            dg_ref[...] += dgain
            loss_ref[...] += jnp.broadcast_to(part, loss_ref.shape)

    row = pl.BlockSpec((bm, d), lambda i: (i, 0))
    vec = pl.BlockSpec((1, d), lambda i: (0, 0))
    return pl.pallas_call(
        body, name="loss_final_norm", grid=(s // bm,), in_specs=[row, vec, row],
        out_specs=[row, pl.BlockSpec((1, 128), lambda i: (0, 0)), vec],
        out_shape=[jax.ShapeDtypeStruct((s, d), F32), jax.ShapeDtypeStruct((1, 128), F32),
                   jax.ShapeDtypeStruct((1, d), F32)],
        compiler_params=_params("arbitrary"),
    )(h, gain, target)


def _ffn_up(name, n, wg, wu):
    s = n.shape[0]
    bm = _row_block(s)

    def body(n_ref, wg_ref, wu_ref, g_ref, u_ref, a_ref):
        x = n_ref[...]
        g = _dot(x, wg_ref[...], NN)
        u = _dot(x, wu_ref[...], NN)
        g_ref[...] = g.astype(BF16)
        u_ref[...] = u.astype(BF16)
        a_ref[...] = ((g * jax.nn.sigmoid(g)) * u).astype(BF16)

    w_spec = pl.BlockSpec((None, D_MODEL, FF_SHARD), lambda j, i: (j, 0, 0))
    o_spec = pl.BlockSpec((None, bm, FF_SHARD), lambda j, i: (j, i, 0))
    shp = jax.ShapeDtypeStruct((N_CHIPS, s, FF_SHARD), BF16)
    return pl.pallas_call(
        body, name=name, grid=(N_CHIPS, s // bm),
        in_specs=[pl.BlockSpec((bm, D_MODEL), lambda j, i: (i, 0)), w_spec, w_spec],
        out_specs=[o_spec, o_spec, o_spec], out_shape=[shp, shp, shp],
        compiler_params=_params("parallel", "parallel"),
    )(n, wg, wu)


def _ffn_down(name, a, wd, res):
    s = a.shape[1]
    bm = _row_block(s)
    row = pl.BlockSpec((bm, D_MODEL), lambda i, j: (i, 0))
    return _matmul(
        name, (s // bm, N_CHIPS),
        [(a, pl.BlockSpec((None, bm, FF_SHARD), lambda i, j: (j, i, 0)),
          wd, pl.BlockSpec((None, FF_SHARD, D_MODEL), lambda i, j: (j, 0, 0)), NN)],
        [(res, row)], [(jax.ShapeDtypeStruct((s, D_MODEL), F32), row)],
        lambda acc, r: (r + 0.5 * acc,), (bm, D_MODEL))[0]


def _ffn_bwd(tag, dh, n, g, u, a, wg, wu, wd):
    s = dh.shape[0]
    bm = _row_block(s)
    bk = _row_block(s)
    nk = s // bk

    def act_bwd(acc, gv, uv):
        da = 0.5 * acc
        gf, uf = gv.astype(F32), uv.astype(F32)
        sg = jax.nn.sigmoid(gf)
        dg = da * uf * (sg * (1.0 + gf * (1.0 - sg)))
        du = da * (gf * sg)
        return dg, du

    slab = pl.BlockSpec((None, bm, FF_SHARD), lambda j, i, k: (j, i, 0))
    shp = jax.ShapeDtypeStruct((N_CHIPS, s, FF_SHARD), BF16)
    dg, du = _matmul(
        tag + "_dact", (N_CHIPS, s // bm, 1),
        [(dh, pl.BlockSpec((bm, D_MODEL), lambda j, i, k: (i, 0)),
          wd, pl.BlockSpec((None, FF_SHARD, D_MODEL), lambda j, i, k: (j, 0, 0)), NT)],
        [(g, slab), (u, slab)], [(shp, slab), (shp, slab)], act_bwd, None)

    dwd = _matmul(
        tag + "_dwd", (N_CHIPS, nk),
        [(a, pl.BlockSpec((None, bk, FF_SHARD), lambda j, k: (j, k, 0)),
          dh, pl.BlockSpec((bk, D_MODEL), lambda j, k: (k, 0)), TN)],
        [], [(jax.ShapeDtypeStruct((N_CHIPS, FF_SHARD, D_MODEL), BF16),
              pl.BlockSpec((None, FF_SHARD, D_MODEL), lambda j, k: (j, 0, 0)))],
        lambda acc: (0.5 * acc,), (FF_SHARD, D_MODEL))[0]

    row = pl.BlockSpec((bm, D_MODEL), lambda i, j: (i, 0))
    a_slab = pl.BlockSpec((None, bm, FF_SHARD), lambda i, j: (j, i, 0))
    w_slab = pl.BlockSpec((None, D_MODEL, FF_SHARD), lambda i, j: (j, 0, 0))
    dn = _matmul(
        tag + "_dn", (s // bm, N_CHIPS),
        [(dg, a_slab, wg, w_slab, NT), (du, a_slab, wu, w_slab, NT)],
        [], [(jax.ShapeDtypeStruct((s, D_MODEL), F32), row)], _ident, (bm, D_MODEL))[0]

    def dw_up(nm, dact):
        return _matmul(
            nm, (N_CHIPS, nk),
            [(n, pl.BlockSpec((bk, D_MODEL), lambda j, k: (k, 0)),
              dact, pl.BlockSpec((None, bk, FF_SHARD), lambda j, k: (j, k, 0)), TN)],
            [], [(jax.ShapeDtypeStruct((N_CHIPS, D_MODEL, FF_SHARD), BF16),
                  pl.BlockSpec((None, D_MODEL, FF_SHARD), lambda j, k: (j, 0, 0)))],
            _ident, (D_MODEL, FF_SHARD))[0]

    return dn, dw_up(tag + "_dwg", dg), dw_up(tag + "_dwu", du), dwd


def _mm_nn(name, a, b, out_dtype, res=None):
    s, k = a.shape
    nn = b.shape[1]
    bm = _row_block(s)
    row = pl.BlockSpec((bm, nn), lambda i, kk: (i, 0))
    extras = [(res, row)] if res is not None else []
    epi = (lambda acc, r: (r + acc,)) if res is not None else _ident
    return _matmul(
        name, (s // bm, 1),
        [(a, pl.BlockSpec((bm, k), lambda i, kk: (i, 0)), b, pl.BlockSpec((k, nn), lambda i, kk: (0, 0)), NN)],
        extras, [(jax.ShapeDtypeStruct((s, nn), out_dtype), row)], epi, None)[0]


def _mm_nt(name, a, b, out_dtype):
    s, nn = a.shape
    k = b.shape[0]
    bm = _row_block(s)
    return _matmul(
        name, (s // bm, 1),
        [(a, pl.BlockSpec((bm, nn), lambda i, kk: (i, 0)), b, pl.BlockSpec((k, nn), lambda i, kk: (0, 0)), NT)],
        [], [(jax.ShapeDtypeStruct((s, k), out_dtype), pl.BlockSpec((bm, k), lambda i, kk: (i, 0)))], _ident, None)[0]


def _mm_tn(name, a, b, out_dtype=BF16):
    s, k = a.shape
    nn = b.shape[1]
    bk = _row_block(s)
    return _matmul(
        name, (s // bk,),
        [(a, pl.BlockSpec((bk, k), lambda kk: (kk, 0)), b, pl.BlockSpec((bk, nn), lambda kk: (kk, 0)), TN)],
        [], [(jax.ShapeDtypeStruct((k, nn), out_dtype), pl.BlockSpec((k, nn), lambda kk: (0, 0)))],
        _ident, (k, nn))[0]


def _mm_heads_fwd(name, a, w, out_dtype):
    s, k = a.shape
    nh, _, nn = w.shape
    bm = _row_block(s)
    return _matmul(
        name, (nh, s // bm, 1),
        [(a, pl.BlockSpec((bm, k), lambda h, i, kk: (i, 0)), w, pl.BlockSpec((None, k, nn), lambda h, i, kk: (h, 0, 0)), NN)],
        [], [(jax.ShapeDtypeStruct((s, nh * nn), out_dtype), pl.BlockSpec((bm, nn), lambda h, i, kk: (i, h)))],
        _ident, None)[0]


def _mm_heads_bwd(name, dy, a, w):
    s, k = a.shape
    nh, _, nn = w.shape
    bm = _row_block(s)
    bk = _row_block(s)
    da = _matmul(
        name + "_dx", (s // bm, nh),
        [(dy, pl.BlockSpec((bm, nn), lambda i, h: (i, h)), w, pl.BlockSpec((None, k, nn), lambda i, h: (h, 0, 0)), NT)],
        [], [(jax.ShapeDtypeStruct((s, k), F32), pl.BlockSpec((bm, k), lambda i, h: (i, 0)))], _ident, (bm, k))[0]
    dw = _matmul(
        name + "_dw", (nh, s // bk),
        [(a, pl.BlockSpec((bk, k), lambda h, kk: (kk, 0)), dy, pl.BlockSpec((bk, nn), lambda h, kk: (kk, h)), TN)],
        [], [(jax.ShapeDtypeStruct((nh, k, nn), BF16), pl.BlockSpec((None, k, nn), lambda h, kk: (h, 0, 0)))],
        _ident, (k, nn))[0]
    return da, dw


def _rope_tables(positions):
    half = ROPE_DIM // 2
    freqs = 1.0 / (ROPE_BASE ** (jnp.arange(0, ROPE_DIM, 2, dtype=F32) / ROPE_DIM))
    ang = positions.astype(F32)[:, None] * freqs
    cos, sin = jnp.cos(ang), jnp.sin(ang)
    z = jnp.zeros_like(cos)
    tc = jnp.concatenate([cos, cos, z, z], axis=-1)
    ta = jnp.concatenate([-sin, z, z, z], axis=-1)
    tb = jnp.concatenate([z, sin, z, z], axis=-1)
    assert tc.shape[-1] == 4 * half
    return tc, ta, tb


def _rope(x, tc, ta, tb):
    return x * tc + pltpu.roll(x, 96, 1) * ta + pltpu.roll(x, 32, 1) * tb


def _rope_t(dy, tc, ta, tb):
    return dy * tc + pltpu.roll(dy * ta, 32, 1) + pltpu.roll(dy * tb, 96, 1)


def _q_rope(q, tc, ta, tb, transpose):
    s = q.shape[0]
    bm = _row_block(s, 512)
    rot = _rope_t if transpose else _rope

    def body(q_ref, tc_ref, ta_ref, tb_ref, o_ref):
        c, a, b = tc_ref[...], ta_ref[...], tb_ref[...]
        for h in range(MLA_HEADS):
            lo = h * HEAD_QK
            o_ref[:, lo:lo + 128] = q_ref[:, lo:lo + 128].astype(BF16)
            o_ref[:, lo + 128:lo + 256] = rot(q_ref[:, lo + 128:lo + 256], c, a, b).astype(BF16)

    row = pl.BlockSpec((bm, MLA_HEADS * HEAD_QK), lambda i: (i, 0))
    tab = pl.BlockSpec((bm, 128), lambda i: (i, 0))
    return pl.pallas_call(
        body, name="q_rope_t" if transpose else "q_rope", grid=(s // bm,), in_specs=[row, tab, tab, tab],
        out_specs=row, out_shape=jax.ShapeDtypeStruct((s, MLA_HEADS * HEAD_QK), BF16),
        compiler_params=_params("parallel"),
    )(q, tc, ta, tb)


def _kv_assemble(kv, z, tc, ta, tb):
    s = kv.shape[0]
    bm = _row_block(s, 512)

    def body(kv_ref, kr_ref, tc_ref, ta_ref, tb_ref, k_ref, v_ref):
        kpe = _rope(kr_ref[...], tc_ref[...], ta_ref[...], tb_ref[...]).astype(BF16)
        for h in range(MLA_HEADS):
            lo = h * 256
            k_ref[:, lo:lo + 128] = kv_ref[:, lo:lo + 128].astype(BF16)
            k_ref[:, lo + 128:lo + 256] = kpe
            v_ref[:, h * 128:(h + 1) * 128] = kv_ref[:, lo + 128:lo + 256].astype(BF16)

    row = pl.BlockSpec((bm, 1024), lambda i: (i, 0))
    tab = pl.BlockSpec((bm, 128), lambda i: (i, 0))
    return pl.pallas_call(
        body, name="kv_assemble", grid=(s // bm,),
        in_specs=[row, pl.BlockSpec((bm, 128), lambda i: (i, 3)), tab, tab, tab],
        out_specs=[row, pl.BlockSpec((bm, 512), lambda i: (i, 0))],
        out_shape=[jax.ShapeDtypeStruct((s, 1024), BF16), jax.ShapeDtypeStruct((s, 512), BF16)],
        compiler_params=_params("parallel"),
    )(kv, z, tc, ta, tb)


def _kv_assemble_bwd(dk, dv, tc, ta, tb):
    s = dk.shape[0]
    bm = _row_block(s, 512)

    def body(dk_ref, dv_ref, tc_ref, ta_ref, tb_ref, dkv_ref, dkr_ref):
        dpe = None
        for h in range(MLA_HEADS):
            lo = h * 256
            dkv_ref[:, lo:lo + 128] = dk_ref[:, lo:lo + 128].astype(BF16)
            dkv_ref[:, lo + 128:lo + 256] = dv_ref[:, h * 128:(h + 1) * 128].astype(BF16)
            t = dk_ref[:, lo + 128:lo + 256]
            dpe = t if dpe is None else dpe + t
        dkr_ref[...] = _rope_t(dpe, tc_ref[...], ta_ref[...], tb_ref[...])

    row = pl.BlockSpec((bm, 1024), lambda i: (i, 0))
    tab = pl.BlockSpec((bm, 128), lambda i: (i, 0))
    return pl.pallas_call(
        body, name="kv_assemble_bwd", grid=(s // bm,),
        in_specs=[row, pl.BlockSpec((bm, 512), lambda i: (i, 0)), tab, tab, tab],
        out_specs=[row, tab],
        out_shape=[jax.ShapeDtypeStruct((s, 1024), BF16), jax.ShapeDtypeStruct((s, 128), F32)],
        compiler_params=_params("parallel"),
    )(dk, dv, tc, ta, tb)


def _causal_mask(s, row0, col0):
    rows = row0 + lax.broadcasted_iota(jnp.int32, s.shape, 0)
    cols = col0 + lax.broadcasted_iota(jnp.int32, s.shape, 1)
    return jnp.where(cols <= rows, s, -jnp.inf)


def _attn_fwd(name, q, k, k_off, v, v_off, nh, dq, dv, scale, causal, blk):
    sq, sk = q.shape[0], k.shape[0]
    bq = min(blk, sq)
    bk = min(blk, sk)
    nkv = sk // bk
    assert not causal or (sq == sk and bq == bk)

    def body(q_ref, k_ref, v_ref, o_ref, lse_ref):
        qi = pl.program_id(1)
        qv = q_ref[...]

        def step(j, carry, masked):
            m, l, acc = carry
            rows = pl.ds(pl.multiple_of(j * bk, bk), bk)
            s = _dot(qv, k_ref[rows, :], NT) * scale
            if masked:
                s = _causal_mask(s, qi * bq, j * bk)
            m_new = jnp.maximum(m, jnp.max(s, axis=-1, keepdims=True))
            alpha = jnp.exp(m - m_new)
            p = jnp.exp(s - m_new)
            l = alpha * l + jnp.sum(p, axis=-1, keepdims=True)
            acc = alpha * acc + _dot(p, v_ref[rows, :], NN)
            return m_new, l, acc

        init = (jnp.full((bq, 1), -jnp.inf, F32), jnp.zeros((bq, 1), F32), jnp.zeros((bq, dv), F32))
        if causal:
            carry = lax.fori_loop(0, qi, lambda j, c: step(j, c, False), init)
            m, l, acc = step(qi, carry, True)
        else:
            m, l, acc = lax.fori_loop(0, nkv, lambda j, c: step(j, c, False), init)
        o_ref[...] = (acc / l).astype(o_ref.dtype)
        lse_ref[...] = m + jnp.log(l)

    return pl.pallas_call(
        body, name=name, grid=(nh, sq // bq),
        in_specs=[pl.BlockSpec((bq, dq), lambda h, i: (i, h)),
                  pl.BlockSpec((sk, dq), lambda h, i: (0, k_off + h)),
                  pl.BlockSpec((sk, dv), lambda h, i: (0, v_off + h))],
        out_specs=[pl.BlockSpec((bq, dv), lambda h, i: (i, h)), pl.BlockSpec((None, bq, 1), lambda h, i: (h, i, 0))],
        out_shape=[jax.ShapeDtypeStruct((sq, nh * dv), BF16), jax.ShapeDtypeStruct((nh, sq, 1), F32)],
        compiler_params=_params("parallel", "parallel"),
    )(q, k, v)


def _attn_delta(name, do, do_off, o, nh, dv):
    s = o.shape[0]
    bm = _row_block(s, 512)

    def body(do_ref, o_ref, d_ref):
        d_ref[...] = jnp.sum(do_ref[...].astype(F32) * o_ref[...].astype(F32), axis=-1, keepdims=True)

    return pl.pallas_call(
        body, name=name, grid=(nh, s // bm),
        in_specs=[pl.BlockSpec((bm, dv), lambda h, i: (i, do_off + h)), pl.BlockSpec((bm, dv), lambda h, i: (i, h))],
        out_specs=pl.BlockSpec((None, bm, 1), lambda h, i: (h, i, 0)),
        out_shape=jax.ShapeDtypeStruct((nh, s, 1), F32),
        compiler_params=_params("parallel", "parallel"),
    )(do, o)


def _attn_bwd(name, q, k, k_off, v, v_off, do, do_off, lse, delta, nh, dq, dv, scale, causal, blk):
    sq, sk = q.shape[0], k.shape[0]
    bq = min(blk, sq)
    bk = min(blk, sk)
    nq = sq // bq
    assert not causal or (sq == sk and bq == bk)

    def body(q_ref, k_ref, v_ref, do_ref, lse_ref, dl_ref, dq_ref, dk_ref, dv_ref, dk_acc, dv_acc):
        j = pl.program_id(1)

        @pl.when(j == 0)
        def _():
            dq_ref[...] = jnp.zeros_like(dq_ref)

        dk_acc[...] = jnp.zeros_like(dk_acc)
        dv_acc[...] = jnp.zeros_like(dv_acc)
        kv = k_ref[...]
        vv = v_ref[...]

        def step(i, masked):
            rows = pl.ds(pl.multiple_of(i * bq, bq), bq)
            qv = q_ref[rows, :]
            dov = do_ref[rows, :].astype(BF16)
            s = _dot(qv, kv, NT) * scale
            if masked:
                s = _causal_mask(s, i * bq, j * bk)
            p = jnp.exp(s - lse_ref[rows, :])
            dp = _dot(dov, vv, NT)
            ds = (p * (dp - dl_ref[rows, :]) * scale).astype(BF16)
            dv_acc[...] += _dot(p, dov, TN)
            dk_acc[...] += _dot(ds, qv, TN)
            dq_ref[rows, :] += _dot(ds, kv, NN)

        if causal:
            step(j, True)

            def loop(i, c):
                step(i, False)
                return c

            lax.fori_loop(j + 1, nq, loop, 0)
        else:
            def loop(i, c):
                step(i, False)
                return c

            lax.fori_loop(0, nq, loop, 0)
        dk_ref[...] = dk_acc[...]
        dv_ref[...] = dv_acc[...]

    stat = pl.BlockSpec((None, sq, 1), lambda h, j: (h, 0, 0))
    return pl.pallas_call(
        body, name=name, grid=(nh, sk // bk),
        in_specs=[pl.BlockSpec((sq, dq), lambda h, j: (0, h)),
                  pl.BlockSpec((bk, dq), lambda h, j: (j, k_off + h)),
                  pl.BlockSpec((bk, dv), lambda h, j: (j, v_off + h)),
                  pl.BlockSpec((sq, dv), lambda h, j: (0, do_off + h)), stat, stat],
        out_specs=[pl.BlockSpec((sq, dq), lambda h, j: (0, h)),
                   pl.BlockSpec((bk, dq), lambda h, j: (j, h)),
                   pl.BlockSpec((bk, dv), lambda h, j: (j, h))],
        out_shape=[jax.ShapeDtypeStruct((sq, nh * dq), F32), jax.ShapeDtypeStruct((sk, nh * dq), F32),
                   jax.ShapeDtypeStruct((sk, nh * dv), F32)],
        scratch_shapes=[pltpu.VMEM((bk, dq), F32), pltpu.VMEM((bk, dv), F32)],
        compiler_params=_params("parallel", "arbitrary"),
    )(q, k, v, do, lse, delta)


def _pool_diff(z, g):
    s = z.shape[0]
    t = lax.broadcasted_iota(jnp.int32, z.shape, 0)
    acc = z
    sums = []
    for k in (1, 2, 4, 8):
        acc = acc + jnp.where(t >= k, pltpu.roll(acc, k, 0), 0.0)
        sums.append(acc)
    win = jnp.where(g == 0, sums[0], jnp.where(g == 1, sums[1], jnp.where(g == 2, sums[2], sums[3])))
    w = lax.shift_left(jnp.int32(2), g)
    count = jnp.minimum(t + 1, w).astype(F32)
    del s
    return win / count - z, count


def _pool_fwd(z, pool_w, pool_scale):
    s = z.shape[0]

    def body(z_ref, w_ref, sc_ref, o_ref):
        diff, _ = _pool_diff(z_ref[...], pl.program_id(0))
        o_ref[...] = (_dot(diff, w_ref[...], NN) * sc_ref[...]).astype(o_ref.dtype)

    return pl.pallas_call(
        body, name="pool_fwd", grid=(POOL_GROUPS,),
        in_specs=[pl.BlockSpec((s, POOL_CH), lambda g: (0, 4 + g)),
                  pl.BlockSpec((None, POOL_CH, POOL_CH), lambda g: (g, 0, 0)),
                  pl.BlockSpec((1, POOL_CH), lambda g: (0, g))],
        out_specs=pl.BlockSpec((s, POOL_CH), lambda g: (0, g)),
        out_shape=jax.ShapeDtypeStruct((s, POOL_GROUPS * POOL_CH), BF16),
        compiler_params=_params("parallel"),
    )(z, pool_w, pool_scale)


def _pool_bwd(dcat, z, pool_w, pool_scale):
    s = z.shape[0]

    def body(dp_ref, z_ref, w_ref, sc_ref, dz_ref, dw_ref, dsc_ref):
        g = pl.program_id(0)
        diff, count = _pool_diff(z_ref[...], g)
        dpf = dp_ref[...].astype(F32)
        u = _dot(diff, w_ref[...], NN)
        dsc_ref[...] = jnp.sum(dpf * u, axis=0, keepdims=True)
        du = (dpf * sc_ref[...]).astype(BF16)
        dw_ref[...] = _dot(diff, du, TN)
        ddiff = _dot(du, w_ref[...], NT)
        t = lax.broadcasted_iota(jnp.int32, ddiff.shape, 0)
        acc = ddiff / count
        sums = []
        for k in (1, 2, 4, 8):
            acc = acc + jnp.where(t < s - k, pltpu.roll(acc, s - k, 0), 0.0)
            sums.append(acc)
        win = jnp.where(g == 0, sums[0], jnp.where(g == 1, sums[1], jnp.where(g == 2, sums[2], sums[3])))
        dz_ref[...] = win - ddiff

    return pl.pallas_call(
        body, name="pool_bwd", grid=(POOL_GROUPS,),
        in_specs=[pl.BlockSpec((s, POOL_CH), lambda g: (0, 4 + g)),
                  pl.BlockSpec((s, POOL_CH), lambda g: (0, 4 + g)),
                  pl.BlockSpec((None, POOL_CH, POOL_CH), lambda g: (g, 0, 0)),
                  pl.BlockSpec((1, POOL_CH), lambda g: (0, g))],
        out_specs=[pl.BlockSpec((s, POOL_CH), lambda g: (0, g)),
                   pl.BlockSpec((None, POOL_CH, POOL_CH), lambda g: (g, 0, 0)),
                   pl.BlockSpec((1, POOL_CH), lambda g: (0, g))],
        out_shape=[jax.ShapeDtypeStruct((s, POOL_GROUPS * POOL_CH), F32),
                   jax.ShapeDtypeStruct((POOL_GROUPS, POOL_CH, POOL_CH), F32),
                   jax.ShapeDtypeStruct((1, POOL_GROUPS * POOL_CH), F32)],
        compiler_params=_params("parallel"),
    )(dcat, z, pool_w, pool_scale)


def _local_step(x, mem, positions, target, w):
    tc, ta, tb = _rope_tables(positions)
    blk = _ATT_BLOCK

    n1 = _rmsnorm_fwd("ffn1_norm", x, w["ffn1_norm"], D_MODEL)
    g1, u1, a1 = _ffn_up("ffn1_up", n1, w["ffn1_w_gate"], w["ffn1_w_up"])
    h1 = _ffn_down("ffn1_down", a1, w["ffn1_w_down"], x)

    n2 = _rmsnorm_fwd("mix_norm", h1, w["mix_norm"], D_MODEL)
    z = _mm_nn("w_in", n2, w["w_in"], F32)
    qn = _rmsnorm_fwd("q_norm", z, w["q_norm"], Q_LORA, 0)
    kvn = _rmsnorm_fwd("kv_norm", z, w["kv_norm"], KV_LORA, 2)
    qp = _mm_heads_fwd("q_up", qn, w["w_q_up"], F32)
    kvp = _mm_heads_fwd("kv_up", kvn, w["w_kv_up"], F32)
    qf = _q_rope(qp, tc, ta, tb, False)
    kf, vf = _kv_assemble(kvp, z, tc, ta, tb)
    att, lse = _attn_fwd("mla_fwd", qf, kf, 0, vf, 0, MLA_HEADS, HEAD_QK, HEAD_V, MLA_SCALE, True, blk)
    pool = _pool_fwd(z, w["pool_w"], w["pool_scale"])
    s = x.shape[0]
    bm = _row_block(s)
    row = pl.BlockSpec((bm, D_MODEL), lambda i, k: (i, 0))
    half = pl.BlockSpec((bm, 512), lambda i, k: (i, 0))
    h2 = _matmul(
        "w_out", (s // bm, 1),
        [(att, half, w["w_out"], pl.BlockSpec((512, D_MODEL), lambda i, k: (0, 0)), NN),
         (pool, half, w["w_out"], pl.BlockSpec((512, D_MODEL), lambda i, k: (1, 0)), NN)],
        [(h1, row)], [(jax.ShapeDtypeStruct((s, D_MODEL), F32), row)], lambda acc, r: (r + acc,), None)[0]

    n3 = _rmsnorm_fwd("xattn_norm", h2, w["xattn_norm"], D_MODEL)
    memn = _rmsnorm_fwd("mem_norm", mem, w["mem_norm"], D_MODEL)
    qm = _mm_nn("w_mq", n3, w["w_mq"], BF16)
    kvm = _mm_heads_fwd("w_mkv", memn, w["w_mkv"], BF16)
    om, lse_m = _attn_fwd("xattn_fwd", qm, kvm, 0, kvm, MEM_HEADS, MEM_HEADS, MEM_HEAD_DIM, MEM_HEAD_DIM,
                          MEM_SCALE, False, blk)
    h3 = _mm_nn("w_mo", om, w["w_mo"], F32, res=h2)

    n4 = _rmsnorm_fwd("ffn2_norm", h3, w["ffn2_norm"], D_MODEL)
    g2, u2, a2 = _ffn_up("ffn2_up", n4, w["ffn2_w_gate"], w["ffn2_w_up"])
    h4 = _ffn_down("ffn2_down", a2, w["ffn2_w_down"], h3)

    dh4, loss_vec, d_final = _loss_and_final_norm(h4, w["final_norm"], target)
    grads = {"final_norm": d_final}

    dn4, grads["ffn2_w_gate"], grads["ffn2_w_up"], grads["ffn2_w_down"] = _ffn_bwd(
        "ffn2", dh4, n4, g2, u2, a2, w["ffn2_w_gate"], w["ffn2_w_up"], w["ffn2_w_down"])
    dh3, grads["ffn2_norm"] = _rmsnorm_bwd("ffn2_norm_bwd", dn4, h3, w["ffn2_norm"], D_MODEL, dres=dh4)

    dom = _mm_nt("w_mo_dx", dh3, w["w_mo"], BF16)
    grads["w_mo"] = _mm_tn("w_mo_dw", om, dh3)
    delta_m = _attn_delta("xattn_delta", dom, 0, om, MEM_HEADS, MEM_HEAD_DIM)
    dqm, dkm, dvm = _attn_bwd("xattn_bwd", qm, kvm, 0, kvm, MEM_HEADS, dom, 0, lse_m, delta_m, MEM_HEADS,
                              MEM_HEAD_DIM, MEM_HEAD_DIM, MEM_SCALE, False, blk)
    dkvm = jnp.concatenate([dkm, dvm], axis=1).astype(BF16)
    dn3 = _mm_nt("w_mq_dx", dqm, w["w_mq"], F32)
    grads["w_mq"] = _mm_tn("w_mq_dw", n3, dqm)
    dmemn, grads["w_mkv"] = _mm_heads_bwd("w_mkv", dkvm, memn, w["w_mkv"])
    _, grads["mem_norm"] = _rmsnorm_bwd("mem_norm_bwd", dmemn, mem, w["mem_norm"], D_MODEL, out_dtype=BF16)
    dh2, grads["xattn_norm"] = _rmsnorm_bwd("xattn_norm_bwd", dn3, h2, w["xattn_norm"], D_MODEL, dres=dh3)

    dcat = _mm_nt("w_out_dx", dh2, w["w_out"], BF16)
    grads["w_out"] = jnp.concatenate([_mm_tn("w_out_dw_a", att, dh2), _mm_tn("w_out_dw_p", pool, dh2)], axis=0)
    dzp, grads["pool_w"], grads["pool_scale"] = _pool_bwd(dcat, z, w["pool_w"], w["pool_scale"])
    delta = _attn_delta("mla_delta", dcat, 0, att, MLA_HEADS, HEAD_V)
    dqf, dkf, dvf = _attn_bwd("mla_bwd", qf, kf, 0, vf, 0, dcat, 0, lse, delta, MLA_HEADS, HEAD_QK, HEAD_V,
                              MLA_SCALE, True, blk)
    dqp = _q_rope(dqf, tc, ta, tb, True)
    dkvp, dkr = _kv_assemble_bwd(dkf, dvf, tc, ta, tb)
    dqn, grads["w_q_up"] = _mm_heads_bwd("q_up", dqp, qn, w["w_q_up"])
    dkvn, grads["w_kv_up"] = _mm_heads_bwd("kv_up", dkvp, kvn, w["w_kv_up"])
    dcq, grads["q_norm"] = _rmsnorm_bwd("q_norm_bwd", dqn, z, w["q_norm"], Q_LORA, 0, out_dtype=BF16)
    dckv, grads["kv_norm"] = _rmsnorm_bwd("kv_norm_bwd", dkvn, z, w["kv_norm"], KV_LORA, 2, out_dtype=BF16)
    dz = jnp.concatenate([dcq, dckv, dkr.astype(BF16), dzp.astype(BF16)], axis=1)
    dn2 = _mm_nt("w_in_dx", dz, w["w_in"], F32)
    grads["w_in"] = _mm_tn("w_in_dw", n2, dz)
    dh1, grads["mix_norm"] = _rmsnorm_bwd("mix_norm_bwd", dn2, h1, w["mix_norm"], D_MODEL, dres=dh2)

    dn1, grads["ffn1_w_gate"], grads["ffn1_w_up"], grads["ffn1_w_down"] = _ffn_bwd(
        "ffn1", dh1, n1, g1, u1, a1, w["ffn1_w_gate"], w["ffn1_w_up"], w["ffn1_w_down"])
    dx, grads["ffn1_norm"] = _rmsnorm_bwd("ffn1_norm_bwd", dn1, x, w["ffn1_norm"], D_MODEL, dres=dh1)
    return loss_vec[0, 0], dx, grads


_MESH = pl.DeviceIdType.MESH
_ANY = pl.BlockSpec(memory_space=pl.ANY)


def _mesh_pos():
    x, y, c = lax.axis_index("x"), lax.axis_index("y"), lax.axis_index("c")
    chips = [(1 - x, y), (x, 1 - y), (1 - x, 1 - y)]
    chip_ids = [2 * cx + cy for cx, cy in chips]
    return x, y, c, 2 * x + y, chips, chip_ids


def _all_gather_weights(shards):
    n = len(shards)

    def body(*refs):
        ins, outs = refs[:n], refs[n:2 * n]
        send, recv, loc = refs[2 * n:]
        x, y, c, me, chips, chip_ids = _mesh_pos()
        sib = (x, y, 1 - c)

        def halves(k):
            hr = ins[k].shape[0] // 2
            return pl.ds(pl.multiple_of(c * hr, 16), hr), pl.ds(pl.multiple_of((1 - c) * hr, 16), hr)

        def remote(src, dst, k, j, dev):
            return pltpu.make_async_remote_copy(src_ref=src, dst_ref=dst, send_sem=send.at[k, j],
                                                recv_sem=recv.at[k, j], device_id=dev, device_id_type=_MESH)

        started = []
        local = []
        for k in range(n):
            mine, _ = halves(k)
            cp = pltpu.make_async_copy(ins[k], outs[k].at[me], loc.at[k])
            cp.start()
            local.append(cp)
            for j, (cx, cy) in enumerate(chips):
                cp = remote(ins[k].at[mine], outs[k].at[me, mine], k, j, (cx, cy, c))
                cp.start()
                started.append(cp)
        for k in range(n):
            mine, _ = halves(k)
            for j in range(3):
                land = outs[k].at[chip_ids[j], mine]
                remote(land, land, k, j, sib).wait_recv()
                cp = remote(land, land, k, 3 + j, sib)
                cp.start()
                started.append(cp)
        for k in range(n):
            _, other = halves(k)
            for j in range(3):
                land = outs[k].at[chip_ids[j], other]
                remote(land, land, k, 3 + j, sib).wait_recv()
        for cp in started:
            cp.wait_send()
        for cp in local:
            cp.wait()

    return pl.pallas_call(
        body, name="all_gather_weights", in_specs=[_ANY] * n, out_specs=[_ANY] * n,
        out_shape=[jax.ShapeDtypeStruct((N_CHIPS,) + s.shape, s.dtype) for s in shards],
        scratch_shapes=[pltpu.SemaphoreType.DMA((n, 6)), pltpu.SemaphoreType.DMA((n, 6)),
                        pltpu.SemaphoreType.DMA((n,))],
        compiler_params=pltpu.CompilerParams(vmem_limit_bytes=V7X_VMEM_LIMIT_BYTES),
    )(*shards)


_RS_CHUNK = 32


def _reduce_scatter(name, grads):
    n = len(grads)

    def body(*refs):
        gs, outs = refs[:n], refs[n:2 * n]
        own, r1, r2, fin = (refs[(2 + i) * n:(3 + i) * n] for i in range(4))
        a_send, a_recv, b_send, b_recv, c_send, c_recv, l_in, l_out = refs[6 * n:]
        x, y, c, me, chips, chip_ids = _mesh_pos()
        sib = (x, y, 1 - c)

        def halves(k):
            hr = gs[k].shape[1] // 2
            return hr, pl.ds(pl.multiple_of(c * hr, 16), hr), pl.ds(pl.multiple_of((1 - c) * hr, 16), hr)

        def remote(src, dst, ssem, rsem, dev):
            return pltpu.make_async_remote_copy(src_ref=src, dst_ref=dst, send_sem=ssem, recv_sem=rsem,
                                                device_id=dev, device_id_type=_MESH)

        sends, locals_in = [], []
        for k in range(n):
            hr, mine, other = halves(k)
            cp = remote(gs[k].at[:, other, :], r1[k], a_send.at[k], a_recv.at[k], sib)
            cp.start()
            sends.append(cp)
            cp = pltpu.make_async_copy(gs[k].at[:, mine, :], own[k], l_in.at[k])
            cp.start()
            locals_in.append(cp)

        for k in range(n):
            hr, mine, other = halves(k)
            locals_in[k].wait()
            remote(r1[k], r1[k], a_send.at[k], a_recv.at[k], sib).wait_recv()
            for j in range(N_CHIPS):
                def add(i, carry, k=k, j=j):
                    rows = pl.ds(pl.multiple_of(i * _RS_CHUNK, _RS_CHUNK), _RS_CHUNK)
                    own[k][j, rows, :] = (own[k][j, rows, :].astype(F32) + r1[k][j, rows, :].astype(F32)).astype(BF16)
                    return carry

                lax.fori_loop(0, hr // _RS_CHUNK, add, 0)
            for j, (cx, cy) in enumerate(chips):
                cp = remote(own[k].at[chip_ids[j]], r2[k].at[j], b_send.at[k, j], b_recv.at[k, j], (cx, cy, c))
                cp.start()
                sends.append(cp)

        locals_out = []
        for k in range(n):
            hr, mine, other = halves(k)
            for j in range(3):
                remote(r2[k].at[j], r2[k].at[j], b_send.at[k, j], b_recv.at[k, j], sib).wait_recv()

            def total(i, carry, k=k):
                rows = pl.ds(pl.multiple_of(i * _RS_CHUNK, _RS_CHUNK), _RS_CHUNK)
                acc = own[k][me, rows, :].astype(F32)
                for j in range(3):
                    acc = acc + r2[k][j, rows, :].astype(F32)
                fin[k][rows, :] = acc
                return carry

            lax.fori_loop(0, hr // _RS_CHUNK, total, 0)
            cp = remote(fin[k], outs[k].at[mine, :], c_send.at[k], c_recv.at[k], sib)
            cp.start()
            sends.append(cp)
            cp = pltpu.make_async_copy(fin[k], outs[k].at[mine, :], l_out.at[k])
            cp.start()
            locals_out.append(cp)

        for k in range(n):
            hr, mine, other = halves(k)
            land = outs[k].at[other, :]
            remote(land, land, c_send.at[k], c_recv.at[k], sib).wait_recv()
        for cp in sends:
            cp.wait_send()
        for cp in locals_out:
            cp.wait()

    scratch = []
    for g in grads:
        scratch.append(pltpu.VMEM((N_CHIPS, g.shape[1] // 2, g.shape[2]), BF16))
    for g in grads:
        scratch.append(pltpu.VMEM((N_CHIPS, g.shape[1] // 2, g.shape[2]), BF16))
    for g in grads:
        scratch.append(pltpu.VMEM((3, g.shape[1] // 2, g.shape[2]), BF16))
    for g in grads:
        scratch.append(pltpu.VMEM((g.shape[1] // 2, g.shape[2]), F32))
    dma = pltpu.SemaphoreType.DMA
    scratch += [dma((n,)), dma((n,)), dma((n, 3)), dma((n, 3)), dma((n,)), dma((n,)), dma((n,)), dma((n,))]
    return pl.pallas_call(
        body, name=name, in_specs=[_ANY] * n, out_specs=[_ANY] * n,
        out_shape=[jax.ShapeDtypeStruct(g.shape[1:], F32) for g in grads],
        scratch_shapes=scratch,
        compiler_params=pltpu.CompilerParams(vmem_limit_bytes=V7X_VMEM_LIMIT_BYTES),
    )(*grads)


def _adamw_math(w, g, m, v):
    m = ADAM_B1 * m + (1.0 - ADAM_B1) * g
    v = ADAM_B2 * v + (1.0 - ADAM_B2) * (g * g)
    m_hat = m / (1.0 - ADAM_B1 ** ADAM_STEP)
    v_hat = v / (1.0 - ADAM_B2 ** ADAM_STEP)
    delta = -ADAM_LR * (m_hat / (jnp.sqrt(v_hat) + ADAM_EPS) + ADAM_WD * w)
    return delta, m, v


def _adamw(name, w, g, m, v):
    r, c = w.shape
    br = r
    while br * c * 4 > (1 << 20) and br % 16 == 0:
        br //= 2

    def body(w_ref, g_ref, m_ref, v_ref, d_ref, nm_ref, nv_ref):
        d, nm, nv = _adamw_math(w_ref[...], g_ref[...], m_ref[...], v_ref[...])
        d_ref[...] = d
        nm_ref[...] = nm
        nv_ref[...] = nv

    spec = pl.BlockSpec((br, c), lambda i: (i, 0))
    shp = jax.ShapeDtypeStruct((r, c), F32)
    return pl.pallas_call(
        body, name=name, grid=(r // br,), in_specs=[spec] * 4, out_specs=[spec] * 3, out_shape=[shp] * 3,
        compiler_params=_params("parallel"),
    )(w, g, m, v)


_SMALL_ROWS = 80


def _small_allreduce_adamw(gpack, wpack, mpack, vpack):
    def body(g_ref, w_ref, m_ref, v_ref, go_ref, d_ref, nm_ref, nv_ref, buf, send, recv):
        x, y, c = lax.axis_index("x"), lax.axis_index("y"), lax.axis_index("c")
        me = 4 * x + 2 * y + c
        buf[me] = g_ref[...]
        copies = []
        for rel in range(1, 8):
            fx, fy, fc = (rel >> 2) & 1, (rel >> 1) & 1, rel & 1
            dev = ((1 - x) if fx else x, (1 - y) if fy else y, (1 - c) if fc else c)
            cp = pltpu.make_async_remote_copy(src_ref=g_ref, dst_ref=buf.at[me], send_sem=send.at[rel - 1],
                                              recv_sem=recv.at[rel - 1], device_id=dev, device_id_type=_MESH)
            cp.start()
            copies.append(cp)
        for cp in copies:
            cp.wait_recv()
        for cp in copies:
            cp.wait_send()
        total = buf[0]
        for i in range(1, 8):
            total = total + buf[i]
        go_ref[...] = total
        d, nm, nv = _adamw_math(w_ref[...], total, m_ref[...], v_ref[...])
        d_ref[...] = d
        nm_ref[...] = nm
        nv_ref[...] = nv

    vm = pl.BlockSpec(memory_space=pltpu.VMEM)
    shp = jax.ShapeDtypeStruct((_SMALL_ROWS, D_MODEL), F32)
    return pl.pallas_call(
        body, name="small_allreduce_adamw", in_specs=[vm] * 4, out_specs=[vm] * 4, out_shape=[shp] * 4,
        scratch_shapes=[pltpu.VMEM((8, _SMALL_ROWS, D_MODEL), F32), pltpu.SemaphoreType.DMA((7,)),
                        pltpu.SemaphoreType.DMA((7,))],
        compiler_params=pltpu.CompilerParams(vmem_limit_bytes=V7X_VMEM_LIMIT_BYTES),
    )(gpack, wpack, mpack, vpack)


_SMALL_VECTORS = ("ffn1_norm", "mix_norm", "xattn_norm", "mem_norm", "ffn2_norm", "final_norm", "q_norm",
                  "kv_norm", "pool_scale")


def _pack_small(d):
    rows = []
    for n in _SMALL_VECTORS:
        v = d[n].reshape(1, -1).astype(F32)
        rows.append(jnp.pad(v, ((0, 0), (0, D_MODEL - v.shape[1]))))
    rows.append(jnp.zeros((16 - len(_SMALL_VECTORS), D_MODEL), F32))
    rows.append(d["pool_w"].reshape(64, D_MODEL).astype(F32))
    return jnp.concatenate(rows, axis=0)


def _unpack_small(pack, like):
    out = {}
    for i, n in enumerate(_SMALL_VECTORS):
        out[n] = pack[i, :like[n].size].reshape(like[n].shape)
    out["pool_w"] = pack[16:].reshape(like["pool_w"].shape)
    return out


_WEIGHTS = ("ffn1_norm", "ffn1_w_gate", "ffn1_w_up", "ffn1_w_down", "mix_norm", "w_in", "q_norm", "w_q_up",
            "kv_norm", "w_kv_up", "pool_w", "pool_scale", "w_out", "xattn_norm", "mem_norm", "w_mq", "w_mkv",
            "w_mo", "ffn2_norm", "ffn2_w_gate", "ffn2_w_up", "ffn2_w_down", "final_norm")
_SHARDED = ("ffn1_w_gate", "ffn1_w_up", "ffn1_w_down", "w_in", "w_q_up", "w_kv_up", "w_out", "w_mq", "w_mkv",
            "w_mo", "ffn2_w_gate", "ffn2_w_up", "ffn2_w_down")
_RS_GROUPS = (("ffn2_w_gate", "ffn2_w_up", "ffn2_w_down"),
              ("w_mo", "w_mq", "w_mkv", "w_out", "w_q_up", "w_kv_up", "w_in"),
              ("ffn1_w_gate", "ffn1_w_up", "ffn1_w_down"))
W_IN_SPLIT = Q_LORA + KV_LORA + ROPE_DIM


def _pad_shard(name, a):
    if name == "w_in":
        return jnp.concatenate([a[:, :W_IN_SPLIT], jnp.zeros((a.shape[0], 64), a.dtype), a[:, W_IN_SPLIT:]], axis=1)
    if name == "w_q_up":
        return jnp.pad(a, ((0, 0), (0, 64)))
    return a


def _unpad_shard(name, a):
    if name == "w_in":
        return jnp.concatenate([a[:, :W_IN_SPLIT], a[:, W_IN_SPLIT + 64:]], axis=1)
    if name == "w_q_up":
        return a[:, :192]
    return a


def kernel(x, mem, positions, ffn1_norm, ffn1_w_gate, ffn1_w_up, ffn1_w_down, mix_norm, w_in, q_norm, w_q_up, kv_norm, w_kv_up, pool_w, pool_scale, w_out, xattn_norm, mem_norm, w_mq, w_mkv, w_mo, ffn2_norm, ffn2_w_gate, ffn2_w_up, ffn2_w_down, final_norm, loss_target, m_ffn1_norm, m_ffn1_w_gate, m_ffn1_w_up, m_ffn1_w_down, m_mix_norm, m_w_in, m_q_norm, m_w_q_up, m_kv_norm, m_w_kv_up, m_pool_w, m_pool_scale, m_w_out, m_xattn_norm, m_mem_norm, m_w_mq, m_w_mkv, m_w_mo, m_ffn2_norm, m_ffn2_w_gate, m_ffn2_w_up, m_ffn2_w_down, m_final_norm, v_ffn1_norm, v_ffn1_w_gate, v_ffn1_w_up, v_ffn1_w_down, v_mix_norm, v_w_in, v_q_norm, v_w_q_up, v_kv_norm, v_w_kv_up, v_pool_w, v_pool_scale, v_w_out, v_xattn_norm, v_mem_norm, v_w_mq, v_w_mkv, v_w_mo, v_ffn2_norm, v_ffn2_w_gate, v_ffn2_w_up, v_ffn2_w_down, v_final_norm):
    wts = dict(zip(_WEIGHTS, (ffn1_norm, ffn1_w_gate, ffn1_w_up, ffn1_w_down, mix_norm, w_in, q_norm, w_q_up, kv_norm, w_kv_up, pool_w, pool_scale, w_out, xattn_norm, mem_norm, w_mq, w_mkv, w_mo, ffn2_norm, ffn2_w_gate, ffn2_w_up, ffn2_w_down, final_norm)))
    mom = dict(zip(_WEIGHTS, (m_ffn1_norm, m_ffn1_w_gate, m_ffn1_w_up, m_ffn1_w_down, m_mix_norm, m_w_in, m_q_norm, m_w_q_up, m_kv_norm, m_w_kv_up, m_pool_w, m_pool_scale, m_w_out, m_xattn_norm, m_mem_norm, m_w_mq, m_w_mkv, m_w_mo, m_ffn2_norm, m_ffn2_w_gate, m_ffn2_w_up, m_ffn2_w_down, m_final_norm)))
    var = dict(zip(_WEIGHTS, (v_ffn1_norm, v_ffn1_w_gate, v_ffn1_w_up, v_ffn1_w_down, v_mix_norm, v_w_in, v_q_norm, v_w_q_up, v_kv_norm, v_w_kv_up, v_pool_w, v_pool_scale, v_w_out, v_xattn_norm, v_mem_norm, v_w_mq, v_w_mkv, v_w_mo, v_ffn2_norm, v_ffn2_w_gate, v_ffn2_w_up, v_ffn2_w_down, v_final_norm)))
    small = [n for n in _WEIGHTS if n not in _SHARDED]

    shards = [_pad_shard(n, wts[n][0]).astype(BF16) for n in _SHARDED]
    full = dict(zip(_SHARDED, _all_gather_weights(shards)))
    w = {n: wts[n].reshape(1, -1) for n in _SMALL_VECTORS}
    w["pool_w"] = pool_w[0].astype(BF16)
    for n in _SHARDED:
        w[n] = full[n]
    for n in ("w_in", "w_out", "w_mq", "w_mo"):
        w[n] = full[n].reshape(D_MODEL, D_MODEL)

    loss_local, dx, grads = _local_step(x[0], mem[0], positions[0], loss_target[0], w)
    loss = lax.psum(loss_local, ("x", "y", "c"))

    for n in ("w_in", "w_out", "w_mq", "w_mo"):
        grads[n] = grads[n].reshape(N_CHIPS, D_MODEL // N_CHIPS, D_MODEL)
    g_shard = {}
    for gi, group in enumerate(_RS_GROUPS):
        red = _reduce_scatter(f"reduce_scatter_{gi}", [grads[n] for n in group])
        for n, r in zip(group, red):
            g_shard[n] = _unpad_shard(n, r)
    gpack, dpack, mpack, vpack = _small_allreduce_adamw(
        _pack_small({n: grads[n] for n in small}), _pack_small({n: wts[n] for n in small}),
        _pack_small({n: mom[n] for n in small}), _pack_small({n: var[n] for n in small}))
    like = {n: wts[n] for n in small}
    g_out, d_out, m_out, v_out = (_unpack_small(p, like) for p in (gpack, dpack, mpack, vpack))

    for n in _SHARDED:
        d, nm, nv = _adamw("adamw_" + n, wts[n][0], g_shard[n], mom[n][0], var[n][0])
        g_out[n] = g_shard[n][None]
        d_out[n], m_out[n], v_out[n] = d[None], nm[None], nv[None]

    return (loss, dx[None], *[g_out[n] for n in _WEIGHTS], *[d_out[n] for n in _WEIGHTS],
            *[m_out[n] for n in _WEIGHTS], *[v_out[n] for n in _WEIGHTS])
```

```python
import functools

import jax
import jax.numpy as jnp
from jax import lax
from jax.experimental import pallas as pl
from jax.experimental.pallas import tpu as pltpu

F32 = jnp.float32
BF16 = jnp.bfloat16

D_MODEL = 1024
D_FF = 2816
N_CHIPS = 4
FF_SHARD = D_FF // N_CHIPS
MLA_HEADS = 4
Q_LORA = 256
KV_LORA = 128
ROPE_DIM = 64
HEAD_QK = 256
HEAD_V = 128
POOL_GROUPS = 4
POOL_CH = 128
MEM_HEADS = 4
MEM_HEAD_DIM = 256
RMS_EPS = 1e-6
ROPE_BASE = 10000.0
MLA_SCALE = (128 + 64) ** -0.5
MEM_SCALE = MEM_HEAD_DIM ** -0.5

ADAM_LR = 0.001
ADAM_B1 = 0.9
ADAM_B2 = 0.999
ADAM_EPS = 1e-08
ADAM_WD = 0.01
ADAM_STEP = 10

V7X_VMEM_LIMIT_BYTES = 56 * 1024 * 1024

NN = ((1,), (0,))
NT = ((1,), (1,))
TN = ((0,), (0,))


def _params(*sem):
    return pltpu.CompilerParams(dimension_semantics=sem, vmem_limit_bytes=V7X_VMEM_LIMIT_BYTES)


_MESH = pl.DeviceIdType.MESH
_ANY = pl.BlockSpec(memory_space=pl.ANY)


class _Stage:
    def __init__(self, ins, outs, n_remote, n_local, copies, aliases=None):
        self.ins, self.outs, self.n_remote, self.n_local = list(ins), list(outs), n_remote, n_local
        self.copies, self.aliases = copies, dict(aliases or {})
        self.results = None

    def descriptors(self, in_refs, out_refs, send, recv, loc):
        ds, ri, li = [], 0, 0
        for src, dst, dev in self.copies(in_refs, out_refs):
            if dev is None:
                ds.append(pltpu.make_async_copy(src, dst, loc.at[li]))
                li += 1
            else:
                ds.append(pltpu.make_async_remote_copy(src_ref=src, dst_ref=dst, send_sem=send.at[ri],
                                                       recv_sem=recv.at[ri], device_id=dev, device_id_type=_MESH))
                ri += 1
        assert ri == self.n_remote and li == self.n_local
        return ds


_PENDING = {}


def _host(name, stage):
    _PENDING.setdefault(name, []).append(stage)
    return stage


_PLAN = None


def _call(body, **kw):
    if _PLAN is not None:
        _PLAN.pre(kw["name"])
    res = _call_hosting(body, **kw)
    if _PLAN is not None:
        _PLAN.post(kw["name"])
    return res


def _call_hosting(body, *, name, grid, in_specs, out_specs, out_shape, sem, args, scratch_shapes=()):
    stages = _PENDING.pop(name, [])
    scratch_shapes = list(scratch_shapes)
    if not stages:
        return pl.pallas_call(body, name=name, grid=grid, in_specs=in_specs, out_specs=out_specs,
                              out_shape=out_shape, scratch_shapes=scratch_shapes,
                              compiler_params=_params(*sem))(*args)
    ni, no, ns = len(in_specs), len(out_shape), len(scratch_shapes)
    c_ins = [a for st in stages for a in st.ins]
    c_outs = [o for st in stages for o in st.outs]
    nci, nco = len(c_ins), len(c_outs)
    aliases, io, oo = {}, 0, 0
    for st in stages:
        for i, j in st.aliases.items():
            aliases[ni + io + i] = no + oo + j
        io += len(st.ins)
        oo += len(st.outs)
    dma = pltpu.SemaphoreType.DMA
    sems = []
    for st in stages:
        sems += [dma((max(st.n_remote, 1),)), dma((max(st.n_remote, 1),)), dma((max(st.n_local, 1),))]

    def wrapped(*refs):
        ins, cin = refs[:ni], refs[ni:ni + nci]
        outs, cout = refs[ni + nci:ni + nci + no], refs[ni + nci + no:ni + nci + no + nco]
        scr = refs[ni + nci + no + nco:ni + nci + no + nco + ns]
        sem_refs = refs[ni + nci + no + nco + ns:]
        first = pl.program_id(0) == 0
        last = pl.program_id(0) == grid[0] - 1
        for ax in range(1, len(grid)):
            first = jnp.logical_and(first, pl.program_id(ax) == 0)
            last = jnp.logical_and(last, pl.program_id(ax) == grid[ax] - 1)

        def descriptors():
            ds, io, oo = [], 0, 0
            for si, st in enumerate(stages):
                ds += st.descriptors(cin[io:io + len(st.ins)], cout[oo:oo + len(st.outs)], *sem_refs[3 * si:3 * si + 3])
                io += len(st.ins)
                oo += len(st.outs)
            return ds

        @pl.when(first)
        def _():
            for d in descriptors():
                d.start()

        body(*ins, *outs, *scr)

        @pl.when(last)
        def _():
            for d in descriptors():
                d.wait()

    res = pl.pallas_call(
        wrapped, name=name, grid=grid, in_specs=list(in_specs) + [_ANY] * nci,
        out_specs=list(out_specs) + [_ANY] * nco, out_shape=list(out_shape) + c_outs,
        scratch_shapes=scratch_shapes + sems, input_output_aliases=aliases,
        compiler_params=_params(*(("arbitrary",) * len(grid))))(*args, *c_ins)
    oo = no
    for st in stages:
        st.results = list(res[oo:oo + len(st.outs)])
        oo += len(st.outs)
    return list(res[:no])


def _dot(a, b, dims):
    return lax.dot_general(a.astype(BF16), b.astype(BF16), (dims, ((), ())), preferred_element_type=F32)


_MAX_ROW_BLOCK = 1024
_ATT_BLOCK = 512


def _row_block(s, want=1024):
    return min(want, s, _MAX_ROW_BLOCK)


def _matmul(name, grid, terms, extras, outs, epilogue, acc_shape):
    nt, ne, no = len(terms), len(extras), len(outs)
    nk = grid[-1]
    dims = [t[4] for t in terms]

    def body(*refs):
        a_refs, b_refs = refs[:nt], refs[nt:2 * nt]
        e_refs = refs[2 * nt:2 * nt + ne]
        o_refs = refs[2 * nt + ne:2 * nt + ne + no]
        part = None
        for a, b, d in zip(a_refs, b_refs, dims):
            t = _dot(a[...], b[...], d)
            part = t if part is None else part + t

        def finish(acc):
            vals = epilogue(acc, *[e[...] for e in e_refs])
            for o, val in zip(o_refs, vals):
                o[...] = val.astype(o.dtype)

        if nk == 1:
            finish(part)
        else:
            acc_ref = refs[-1]
            k = pl.program_id(len(grid) - 1)

            @pl.when(k == 0)
            def _():
                acc_ref[...] = part

            @pl.when(k > 0)
            def _():
                acc_ref[...] += part

            @pl.when(k == nk - 1)
            def _():
                finish(acc_ref[...])

    in_specs = [t[1] for t in terms] + [t[3] for t in terms] + [e[1] for e in extras]
    args = [t[0] for t in terms] + [t[2] for t in terms] + [e[0] for e in extras]
    sem = ("parallel",) * (len(grid) - 1) + ("arbitrary",)
    return _call(
        body, name=name, grid=grid, in_specs=in_specs,
        out_specs=[o[1] for o in outs], out_shape=[o[0] for o in outs],
        scratch_shapes=[pltpu.VMEM(acc_shape, F32)] if nk > 1 else [], sem=sem, args=args)


def _ident(acc):
    return (acc,)


def _rmsnorm_fwd(name, x, gain, width, col_block=0):
    s = x.shape[0]
    bm = _row_block(s)

    def body(x_ref, g_ref, o_ref):
        xf = x_ref[...]
        r = lax.rsqrt(jnp.mean(xf * xf, axis=-1, keepdims=True) + RMS_EPS)
        o_ref[...] = ((xf * r) * g_ref[...]).astype(o_ref.dtype)

    return pl.pallas_call(
        body, name=name, grid=(s // bm,),
        in_specs=[pl.BlockSpec((bm, width), lambda i: (i, col_block)), pl.BlockSpec((1, width), lambda i: (0, 0))],
        out_specs=pl.BlockSpec((bm, width), lambda i: (i, 0)),
        out_shape=jax.ShapeDtypeStruct((s, width), BF16),
        compiler_params=_params("parallel"),
    )(x, gain)


def _rms_bwd_math(dy, xf, g, width):
    r = lax.rsqrt(jnp.mean(xf * xf, axis=-1, keepdims=True) + RMS_EPS)
    dyg = dy * g
    dot = jnp.sum(dyg * xf, axis=-1, keepdims=True)
    dx = r * dyg - xf * ((r * r * r) * (dot * (1.0 / width)))
    dgain = jnp.sum(dy * (xf * r), axis=0, keepdims=True)
    return dx, dgain


def _rmsnorm_bwd(name, dy, x, gain, width, col_block=0, dres=None, out_dtype=F32):
    s = x.shape[0]
    bm = _row_block(s)
    has_res = dres is not None

    def body(*refs):
        if has_res:
            dy_ref, x_ref, g_ref, r_ref, dx_ref, dg_ref = refs
        else:
            dy_ref, x_ref, g_ref, dx_ref, dg_ref = refs
        dx, dgain = _rms_bwd_math(dy_ref[...].astype(F32), x_ref[...], g_ref[...], width)
        if has_res:
            dx = dx + r_ref[...]
        dx_ref[...] = dx.astype(dx_ref.dtype)

        @pl.when(pl.program_id(0) == 0)
        def _():
            dg_ref[...] = dgain

        @pl.when(pl.program_id(0) > 0)
        def _():
            dg_ref[...] += dgain

    row = pl.BlockSpec((bm, width), lambda i: (i, 0))
    in_specs = [row, pl.BlockSpec((bm, width), lambda i: (i, col_block)), pl.BlockSpec((1, width), lambda i: (0, 0))]
    args = [dy, x, gain]
    if has_res:
        in_specs.append(row)
        args.append(dres)
    return _call(
        body, name=name, grid=(s // bm,), in_specs=in_specs,
        out_specs=[row, pl.BlockSpec((1, width), lambda i: (0, 0))],
        out_shape=[jax.ShapeDtypeStruct((s, width), out_dtype), jax.ShapeDtypeStruct((1, width), F32)],
        sem=("arbitrary",), args=args)


def _loss_and_final_norm(h, gain, target):
    s, d = h.shape
    bm = _row_block(s, 512)

    def body(h_ref, g_ref, t_ref, dh_ref, loss_ref, dg_ref):
        xf = h_ref[...]
        g = g_ref[...]
        r = lax.rsqrt(jnp.mean(xf * xf, axis=-1, keepdims=True) + RMS_EPS)
        err = (xf * r) * g - t_ref[...]
        part = 0.5 * jnp.sum(jnp.mean(err * err, axis=-1, keepdims=True), axis=0, keepdims=True)
        dx, dgain = _rms_bwd_math(err * (1.0 / d), xf, g, d)
        dh_ref[...] = dx

        @pl.when(pl.program_id(0) == 0)
        def _():
            dg_ref[...] = dgain
            loss_ref[...] = jnp.broadcast_to(part, loss_ref.shape)

        @pl.when(pl.program_id(0) > 0)
        def _():
            dg_ref[...] += dgain
            loss_ref[...] += jnp.broadcast_to(part, loss_ref.shape)

    row = pl.BlockSpec((bm, d), lambda i: (i, 0))
    vec = pl.BlockSpec((1, d), lambda i: (0, 0))
    return pl.pallas_call(
        body, name="loss_final_norm", grid=(s // bm,), in_specs=[row, vec, row],
        out_specs=[row, pl.BlockSpec((1, 128), lambda i: (0, 0)), vec],
        out_shape=[jax.ShapeDtypeStruct((s, d), F32), jax.ShapeDtypeStruct((1, 128), F32),
                   jax.ShapeDtypeStruct((1, d), F32)],
        compiler_params=_params("arbitrary"),
    )(h, gain, target)


def _ffn_up(name, n, wg, wu):
    s = n.shape[0]
    bm = _row_block(s)

    def body(n_ref, wg_ref, wu_ref, g_ref, u_ref, a_ref):
        x = n_ref[...]
        g = _dot(x, wg_ref[...], NN)
        u = _dot(x, wu_ref[...], NN)
        g_ref[...] = g.astype(BF16)
        u_ref[...] = u.astype(BF16)
        a_ref[...] = ((g * jax.nn.sigmoid(g)) * u).astype(BF16)

    w_spec = pl.BlockSpec((None, D_MODEL, FF_SHARD), lambda j, i: (j, 0, 0))
    o_spec = pl.BlockSpec((None, bm, FF_SHARD), lambda j, i: (j, i, 0))
    shp = jax.ShapeDtypeStruct((N_CHIPS, s, FF_SHARD), BF16)
    return _call(
        body, name=name, grid=(N_CHIPS, s // bm),
        in_specs=[pl.BlockSpec((bm, D_MODEL), lambda j, i: (i, 0)), w_spec, w_spec],
        out_specs=[o_spec, o_spec, o_spec], out_shape=[shp, shp, shp],
        sem=("parallel", "parallel"), args=[n, wg, wu])


def _ffn_down(name, a, wd, res):
    s = a.shape[1]
    bm = _row_block(s)
    row = pl.BlockSpec((bm, D_MODEL), lambda i, j: (i, 0))
    return _matmul(
        name, (s // bm, N_CHIPS),
        [(a, pl.BlockSpec((None, bm, FF_SHARD), lambda i, j: (j, i, 0)),
          wd, pl.BlockSpec((None, FF_SHARD, D_MODEL), lambda i, j: (j, 0, 0)), NN)],
        [(res, row)], [(jax.ShapeDtypeStruct((s, D_MODEL), F32), row)],
        lambda acc, r: (r + 0.5 * acc,), (bm, D_MODEL))[0]


def _ffn_bwd(tag, dh, n, g, u, a, wg, wu, wd, grads):
    s = dh.shape[0]
    bm = _row_block(s)
    bk = _row_block(s)
    nk = s // bk

    def act_bwd(acc, gv, uv):
        da = 0.5 * acc
        gf, uf = gv.astype(F32), uv.astype(F32)
        sg = jax.nn.sigmoid(gf)
        dg = da * uf * (sg * (1.0 + gf * (1.0 - sg)))
        du = da * (gf * sg)
        return dg, du

    slab = pl.BlockSpec((None, bm, FF_SHARD), lambda j, i, k: (j, i, 0))
    shp = jax.ShapeDtypeStruct((N_CHIPS, s, FF_SHARD), BF16)
    dg, du = _matmul(
        tag + "_dact", (N_CHIPS, s // bm, 1),
        [(dh, pl.BlockSpec((bm, D_MODEL), lambda j, i, k: (i, 0)),
          wd, pl.BlockSpec((None, FF_SHARD, D_MODEL), lambda j, i, k: (j, 0, 0)), NT)],
        [(g, slab), (u, slab)], [(shp, slab), (shp, slab)], act_bwd, None)

    grads[tag + "_w_down"] = _matmul(
        tag + "_dwd", (N_CHIPS, nk),
        [(a, pl.BlockSpec((None, bk, FF_SHARD), lambda j, k: (j, k, 0)),
          dh, pl.BlockSpec((bk, D_MODEL), lambda j, k: (k, 0)), TN)],
        [], [(jax.ShapeDtypeStruct((N_CHIPS, FF_SHARD, D_MODEL), BF16),
              pl.BlockSpec((None, FF_SHARD, D_MODEL), lambda j, k: (j, 0, 0)))],
        lambda acc: (0.5 * acc,), (FF_SHARD, D_MODEL))[0]

    def dw_up(nm, dact):
        return _matmul(
            nm, (N_CHIPS, nk),
            [(n, pl.BlockSpec((bk, D_MODEL), lambda j, k: (k, 0)),
              dact, pl.BlockSpec((None, bk, FF_SHARD), lambda j, k: (j, k, 0)), TN)],
            [], [(jax.ShapeDtypeStruct((N_CHIPS, D_MODEL, FF_SHARD), BF16),
                  pl.BlockSpec((None, D_MODEL, FF_SHARD), lambda j, k: (j, 0, 0)))],
            _ident, (D_MODEL, FF_SHARD))[0]

    grads[tag + "_w_gate"] = dw_up(tag + "_dwg", dg)
    grads[tag + "_w_up"] = dw_up(tag + "_dwu", du)

    row = pl.BlockSpec((bm, D_MODEL), lambda i, j: (i, 0))
    a_slab = pl.BlockSpec((None, bm, FF_SHARD), lambda i, j: (j, i, 0))
    w_slab = pl.BlockSpec((None, D_MODEL, FF_SHARD), lambda i, j: (j, 0, 0))
    return _matmul(
        tag + "_dn", (s // bm, N_CHIPS),
        [(dg, a_slab, wg, w_slab, NT), (du, a_slab, wu, w_slab, NT)],
        [], [(jax.ShapeDtypeStruct((s, D_MODEL), F32), row)], _ident, (bm, D_MODEL))[0]


def _mm_nn(name, a, b, out_dtype, res=None):
    s, k = a.shape
    nn = b.shape[1]
    bm = _row_block(s)
    row = pl.BlockSpec((bm, nn), lambda i, kk: (i, 0))
    extras = [(res, row)] if res is not None else []
    epi = (lambda acc, r: (r + acc,)) if res is not None else _ident
    return _matmul(
        name, (s // bm, 1),
        [(a, pl.BlockSpec((bm, k), lambda i, kk: (i, 0)), b, pl.BlockSpec((k, nn), lambda i, kk: (0, 0)), NN)],
        extras, [(jax.ShapeDtypeStruct((s, nn), out_dtype), row)], epi, None)[0]


def _mm_nt(name, a, b, out_dtype):
    s, nn = a.shape
    k = b.shape[0]
    bm = _row_block(s)
    return _matmul(
        name, (s // bm, 1),
        [(a, pl.BlockSpec((bm, nn), lambda i, kk: (i, 0)), b, pl.BlockSpec((k, nn), lambda i, kk: (0, 0)), NT)],
        [], [(jax.ShapeDtypeStruct((s, k), out_dtype), pl.BlockSpec((bm, k), lambda i, kk: (i, 0)))], _ident, None)[0]


def _mm_tn(name, a, b, out_dtype=BF16):
    s, k = a.shape
    nn = b.shape[1]
    bk = _row_block(s)
    return _matmul(
        name, (s // bk,),
        [(a, pl.BlockSpec((bk, k), lambda kk: (kk, 0)), b, pl.BlockSpec((bk, nn), lambda kk: (kk, 0)), TN)],
        [], [(jax.ShapeDtypeStruct((k, nn), out_dtype), pl.BlockSpec((k, nn), lambda kk: (0, 0)))],
        _ident, (k, nn))[0]


def _mm_heads_fwd(name, a, w, out_dtype):
    s, k = a.shape
    nh, _, nn = w.shape
    bm = _row_block(s)
    return _matmul(
        name, (nh, s // bm, 1),
        [(a, pl.BlockSpec((bm, k), lambda h, i, kk: (i, 0)), w, pl.BlockSpec((None, k, nn), lambda h, i, kk: (h, 0, 0)), NN)],
        [], [(jax.ShapeDtypeStruct((s, nh * nn), out_dtype), pl.BlockSpec((bm, nn), lambda h, i, kk: (i, h)))],
        _ident, None)[0]


def _mm_heads_bwd(name, dy, a, w):
    s, k = a.shape
    nh, _, nn = w.shape
    bm = _row_block(s)
    bk = _row_block(s)
    da = _matmul(
        name + "_dx", (s // bm, nh),
        [(dy, pl.BlockSpec((bm, nn), lambda i, h: (i, h)), w, pl.BlockSpec((None, k, nn), lambda i, h: (h, 0, 0)), NT)],
        [], [(jax.ShapeDtypeStruct((s, k), F32), pl.BlockSpec((bm, k), lambda i, h: (i, 0)))], _ident, (bm, k))[0]
    dw = _matmul(
        name + "_dw", (nh, s // bk),
        [(a, pl.BlockSpec((bk, k), lambda h, kk: (kk, 0)), dy, pl.BlockSpec((bk, nn), lambda h, kk: (kk, h)), TN)],
        [], [(jax.ShapeDtypeStruct((nh, k, nn), BF16), pl.BlockSpec((None, k, nn), lambda h, kk: (h, 0, 0)))],
        _ident, (k, nn))[0]
    return da, dw


def _rope_tables(positions):
    half = ROPE_DIM // 2
    freqs = 1.0 / (ROPE_BASE ** (jnp.arange(0, ROPE_DIM, 2, dtype=F32) / ROPE_DIM))
    ang = positions.astype(F32)[:, None] * freqs
    cos, sin = jnp.cos(ang), jnp.sin(ang)
    z = jnp.zeros_like(cos)
    tc = jnp.concatenate([cos, cos, z, z], axis=-1)
    ta = jnp.concatenate([-sin, z, z, z], axis=-1)
    tb = jnp.concatenate([z, sin, z, z], axis=-1)
    assert tc.shape[-1] == 4 * half
    return tc, ta, tb


def _rope(x, tc, ta, tb):
    return x * tc + pltpu.roll(x, 96, 1) * ta + pltpu.roll(x, 32, 1) * tb


def _rope_t(dy, tc, ta, tb):
    return dy * tc + pltpu.roll(dy * ta, 32, 1) + pltpu.roll(dy * tb, 96, 1)


def _q_rope(q, tc, ta, tb, transpose):
    s = q.shape[0]
    bm = _row_block(s, 512)
    rot = _rope_t if transpose else _rope

    def body(q_ref, tc_ref, ta_ref, tb_ref, o_ref):
        c, a, b = tc_ref[...], ta_ref[...], tb_ref[...]
        for h in range(MLA_HEADS):
            lo = h * HEAD_QK
            o_ref[:, lo:lo + 128] = q_ref[:, lo:lo + 128].astype(BF16)
            o_ref[:, lo + 128:lo + 256] = rot(q_ref[:, lo + 128:lo + 256], c, a, b).astype(BF16)

    row = pl.BlockSpec((bm, MLA_HEADS * HEAD_QK), lambda i: (i, 0))
    tab = pl.BlockSpec((bm, 128), lambda i: (i, 0))
    return pl.pallas_call(
        body, name="q_rope_t" if transpose else "q_rope", grid=(s // bm,), in_specs=[row, tab, tab, tab],
        out_specs=row, out_shape=jax.ShapeDtypeStruct((s, MLA_HEADS * HEAD_QK), BF16),
        compiler_params=_params("parallel"),
    )(q, tc, ta, tb)


def _kv_assemble(kv, z, tc, ta, tb):
    s = kv.shape[0]
    bm = _row_block(s, 512)

    def body(kv_ref, kr_ref, tc_ref, ta_ref, tb_ref, k_ref, v_ref):
        kpe = _rope(kr_ref[...], tc_ref[...], ta_ref[...], tb_ref[...]).astype(BF16)
        for h in range(MLA_HEADS):
            lo = h * 256
            k_ref[:, lo:lo + 128] = kv_ref[:, lo:lo + 128].astype(BF16)
            k_ref[:, lo + 128:lo + 256] = kpe
            v_ref[:, h * 128:(h + 1) * 128] = kv_ref[:, lo + 128:lo + 256].astype(BF16)

    row = pl.BlockSpec((bm, 1024), lambda i: (i, 0))
    tab = pl.BlockSpec((bm, 128), lambda i: (i, 0))
    return pl.pallas_call(
        body, name="kv_assemble", grid=(s // bm,),
        in_specs=[row, pl.BlockSpec((bm, 128), lambda i: (i, 3)), tab, tab, tab],
        out_specs=[row, pl.BlockSpec((bm, 512), lambda i: (i, 0))],
        out_shape=[jax.ShapeDtypeStruct((s, 1024), BF16), jax.ShapeDtypeStruct((s, 512), BF16)],
        compiler_params=_params("parallel"),
    )(kv, z, tc, ta, tb)


def _kv_assemble_bwd(dk, dv, tc, ta, tb):
    s = dk.shape[0]
    bm = _row_block(s, 512)

    def body(dk_ref, dv_ref, tc_ref, ta_ref, tb_ref, dkv_ref, dkr_ref):
        dpe = None
        for h in range(MLA_HEADS):
            lo = h * 256
            dkv_ref[:, lo:lo + 128] = dk_ref[:, lo:lo + 128].astype(BF16)
            dkv_ref[:, lo + 128:lo + 256] = dv_ref[:, h * 128:(h + 1) * 128].astype(BF16)
            t = dk_ref[:, lo + 128:lo + 256]
            dpe = t if dpe is None else dpe + t
        dkr_ref[...] = _rope_t(dpe, tc_ref[...], ta_ref[...], tb_ref[...])

    row = pl.BlockSpec((bm, 1024), lambda i: (i, 0))
    tab = pl.BlockSpec((bm, 128), lambda i: (i, 0))
    return pl.pallas_call(
        body, name="kv_assemble_bwd", grid=(s // bm,),
        in_specs=[row, pl.BlockSpec((bm, 512), lambda i: (i, 0)), tab, tab, tab],
        out_specs=[row, tab],
        out_shape=[jax.ShapeDtypeStruct((s, 1024), BF16), jax.ShapeDtypeStruct((s, 128), F32)],
        compiler_params=_params("parallel"),
    )(dk, dv, tc, ta, tb)


def _causal_mask(s, row0, col0):
    rows = row0 + lax.broadcasted_iota(jnp.int32, s.shape, 0)
    cols = col0 + lax.broadcasted_iota(jnp.int32, s.shape, 1)
    return jnp.where(cols <= rows, s, -jnp.inf)


def _attn_fwd(name, q, k, k_off, v, v_off, nh, dq, dv, scale, causal, blk):
    sq, sk = q.shape[0], k.shape[0]
    bq = min(blk, sq)
    bk = min(blk, sk)
    nkv = sk // bk
    assert not causal or (sq == sk and bq == bk)

    def body(q_ref, k_ref, v_ref, o_ref, lse_ref):
        qi = pl.program_id(1)
        qv = q_ref[...]

        def step(j, carry, masked):
            m, l, acc = carry
            rows = pl.ds(pl.multiple_of(j * bk, bk), bk)
            s = _dot(qv, k_ref[rows, :], NT) * scale
            if masked:
                s = _causal_mask(s, qi * bq, j * bk)
            m_new = jnp.maximum(m, jnp.max(s, axis=-1, keepdims=True))
            alpha = jnp.exp(m - m_new)
            p = jnp.exp(s - m_new)
            l = alpha * l + jnp.sum(p, axis=-1, keepdims=True)
            acc = alpha * acc + _dot(p, v_ref[rows, :], NN)
            return m_new, l, acc

        init = (jnp.full((bq, 1), -jnp.inf, F32), jnp.zeros((bq, 1), F32), jnp.zeros((bq, dv), F32))
        if causal:
            carry = lax.fori_loop(0, qi, lambda j, c: step(j, c, False), init)
            m, l, acc = step(qi, carry, True)
        else:
            m, l, acc = lax.fori_loop(0, nkv, lambda j, c: step(j, c, False), init)
        o_ref[...] = (acc / l).astype(o_ref.dtype)
        lse_ref[...] = m + jnp.log(l)

    return _call(
        body, name=name, grid=(nh, sq // bq),
        in_specs=[pl.BlockSpec((bq, dq), lambda h, i: (i, h)),
                  pl.BlockSpec((sk, dq), lambda h, i: (0, k_off + h)),
                  pl.BlockSpec((sk, dv), lambda h, i: (0, v_off + h))],
        out_specs=[pl.BlockSpec((bq, dv), lambda h, i: (i, h)), pl.BlockSpec((None, bq, 1), lambda h, i: (h, i, 0))],
        out_shape=[jax.ShapeDtypeStruct((sq, nh * dv), BF16), jax.ShapeDtypeStruct((nh, sq, 1), F32)],
        sem=("parallel", "parallel"), args=[q, k, v])


def _attn_delta(name, do, do_off, o, nh, dv):
    s = o.shape[0]
    bm = _row_block(s, 512)

    def body(do_ref, o_ref, d_ref):
        d_ref[...] = jnp.sum(do_ref[...].astype(F32) * o_ref[...].astype(F32), axis=-1, keepdims=True)

    return pl.pallas_call(
        body, name=name, grid=(nh, s // bm),
        in_specs=[pl.BlockSpec((bm, dv), lambda h, i: (i, do_off + h)), pl.BlockSpec((bm, dv), lambda h, i: (i, h))],
        out_specs=pl.BlockSpec((None, bm, 1), lambda h, i: (h, i, 0)),
        out_shape=jax.ShapeDtypeStruct((nh, s, 1), F32),
        compiler_params=_params("parallel", "parallel"),
    )(do, o)


def _attn_bwd(name, q, k, k_off, v, v_off, do, do_off, lse, delta, nh, dq, dv, scale, causal, blk):
    sq, sk = q.shape[0], k.shape[0]
    bq = min(blk, sq)
    bk = min(blk, sk)
    nq = sq // bq
    assert not causal or (sq == sk and bq == bk)

    def body(q_ref, k_ref, v_ref, do_ref, lse_ref, dl_ref, dq_ref, dk_ref, dv_ref, dk_acc, dv_acc):
        j = pl.program_id(1)

        @pl.when(j == 0)
        def _():
            dq_ref[...] = jnp.zeros_like(dq_ref)

        dk_acc[...] = jnp.zeros_like(dk_acc)
        dv_acc[...] = jnp.zeros_like(dv_acc)
        kv = k_ref[...]
        vv = v_ref[...]

        def step(i, masked):
            rows = pl.ds(pl.multiple_of(i * bq, bq), bq)
            qv = q_ref[rows, :]
            dov = do_ref[rows, :].astype(BF16)
            s = _dot(qv, kv, NT) * scale
            if masked:
                s = _causal_mask(s, i * bq, j * bk)
            p = jnp.exp(s - lse_ref[rows, :])
            dp = _dot(dov, vv, NT)
            ds = (p * (dp - dl_ref[rows, :]) * scale).astype(BF16)
            dv_acc[...] += _dot(p, dov, TN)
            dk_acc[...] += _dot(ds, qv, TN)
            dq_ref[rows, :] += _dot(ds, kv, NN)

        if causal:
            step(j, True)

            def loop(i, c):
                step(i, False)
                return c

            lax.fori_loop(j + 1, nq, loop, 0)
        else:
            def loop(i, c):
                step(i, False)
                return c

            lax.fori_loop(0, nq, loop, 0)
        dk_ref[...] = dk_acc[...]
        dv_ref[...] = dv_acc[...]

    stat = pl.BlockSpec((None, sq, 1), lambda h, j: (h, 0, 0))
    return _call(
        body, name=name, grid=(nh, sk // bk),
        in_specs=[pl.BlockSpec((sq, dq), lambda h, j: (0, h)),
                  pl.BlockSpec((bk, dq), lambda h, j: (j, k_off + h)),
                  pl.BlockSpec((bk, dv), lambda h, j: (j, v_off + h)),
                  pl.BlockSpec((sq, dv), lambda h, j: (0, do_off + h)), stat, stat],
        out_specs=[pl.BlockSpec((sq, dq), lambda h, j: (0, h)),
                   pl.BlockSpec((bk, dq), lambda h, j: (j, h)),
                   pl.BlockSpec((bk, dv), lambda h, j: (j, h))],
        out_shape=[jax.ShapeDtypeStruct((sq, nh * dq), F32), jax.ShapeDtypeStruct((sk, nh * dq), F32),
                   jax.ShapeDtypeStruct((sk, nh * dv), F32)],
        scratch_shapes=[pltpu.VMEM((bk, dq), F32), pltpu.VMEM((bk, dv), F32)],
        sem=("parallel", "arbitrary"), args=[q, k, v, do, lse, delta])


def _pool_diff(z, g):
    s = z.shape[0]
    t = lax.broadcasted_iota(jnp.int32, z.shape, 0)
    acc = z
    sums = []
    for k in (1, 2, 4, 8):
        acc = acc + jnp.where(t >= k, pltpu.roll(acc, k, 0), 0.0)
        sums.append(acc)
    win = jnp.where(g == 0, sums[0], jnp.where(g == 1, sums[1], jnp.where(g == 2, sums[2], sums[3])))
    w = lax.shift_left(jnp.int32(2), g)
    count = jnp.minimum(t + 1, w).astype(F32)
    del s
    return win / count - z, count


def _pool_fwd(z, pool_w, pool_scale):
    s = z.shape[0]

    def body(z_ref, w_ref, sc_ref, o_ref):
        diff, _ = _pool_diff(z_ref[...], pl.program_id(0))
        o_ref[...] = (_dot(diff, w_ref[...], NN) * sc_ref[...]).astype(o_ref.dtype)

    return pl.pallas_call(
        body, name="pool_fwd", grid=(POOL_GROUPS,),
        in_specs=[pl.BlockSpec((s, POOL_CH), lambda g: (0, 4 + g)),
                  pl.BlockSpec((None, POOL_CH, POOL_CH), lambda g: (g, 0, 0)),
                  pl.BlockSpec((1, POOL_CH), lambda g: (0, g))],
        out_specs=pl.BlockSpec((s, POOL_CH), lambda g: (0, g)),
        out_shape=jax.ShapeDtypeStruct((s, POOL_GROUPS * POOL_CH), BF16),
        compiler_params=_params("parallel"),
    )(z, pool_w, pool_scale)


def _pool_bwd(dcat, z, pool_w, pool_scale):
    s = z.shape[0]

    def body(dp_ref, z_ref, w_ref, sc_ref, dz_ref, dw_ref, dsc_ref):
        g = pl.program_id(0)
        diff, count = _pool_diff(z_ref[...], g)
        dpf = dp_ref[...].astype(F32)
        u = _dot(diff, w_ref[...], NN)
        dsc_ref[...] = jnp.sum(dpf * u, axis=0, keepdims=True)
        du = (dpf * sc_ref[...]).astype(BF16)
        dw_ref[...] = _dot(diff, du, TN)
        ddiff = _dot(du, w_ref[...], NT)
        t = lax.broadcasted_iota(jnp.int32, ddiff.shape, 0)
        acc = ddiff / count
        sums = []
        for k in (1, 2, 4, 8):
            acc = acc + jnp.where(t < s - k, pltpu.roll(acc, s - k, 0), 0.0)
            sums.append(acc)
        win = jnp.where(g == 0, sums[0], jnp.where(g == 1, sums[1], jnp.where(g == 2, sums[2], sums[3])))
        dz_ref[...] = win - ddiff

    return pl.pallas_call(
        body, name="pool_bwd", grid=(POOL_GROUPS,),
        in_specs=[pl.BlockSpec((s, POOL_CH), lambda g: (0, 4 + g)),
                  pl.BlockSpec((s, POOL_CH), lambda g: (0, 4 + g)),
                  pl.BlockSpec((None, POOL_CH, POOL_CH), lambda g: (g, 0, 0)),
                  pl.BlockSpec((1, POOL_CH), lambda g: (0, g))],
        out_specs=[pl.BlockSpec((s, POOL_CH), lambda g: (0, g)),
                   pl.BlockSpec((None, POOL_CH, POOL_CH), lambda g: (g, 0, 0)),
                   pl.BlockSpec((1, POOL_CH), lambda g: (0, g))],
        out_shape=[jax.ShapeDtypeStruct((s, POOL_GROUPS * POOL_CH), F32),
                   jax.ShapeDtypeStruct((POOL_GROUPS, POOL_CH, POOL_CH), F32),
                   jax.ShapeDtypeStruct((1, POOL_GROUPS * POOL_CH), F32)],
        compiler_params=_params("parallel"),
    )(dcat, z, pool_w, pool_scale)


def _local_step(x, mem, positions, target, w, grads):
    tc, ta, tb = _rope_tables(positions)
    blk = _ATT_BLOCK

    n1 = _rmsnorm_fwd("ffn1_norm", x, w["ffn1_norm"], D_MODEL)
    g1, u1, a1 = _ffn_up("ffn1_up", n1, w["ffn1_w_gate"], w["ffn1_w_up"])
    h1 = _ffn_down("ffn1_down", a1, w["ffn1_w_down"], x)

    n2 = _rmsnorm_fwd("mix_norm", h1, w["mix_norm"], D_MODEL)
    z = _mm_nn("w_in", n2, w["w_in"], F32)
    qn = _rmsnorm_fwd("q_norm", z, w["q_norm"], Q_LORA, 0)
    kvn = _rmsnorm_fwd("kv_norm", z, w["kv_norm"], KV_LORA, 2)
    qp = _mm_heads_fwd("q_up", qn, w["w_q_up"], F32)
    kvp = _mm_heads_fwd("kv_up", kvn, w["w_kv_up"], F32)
    qf = _q_rope(qp, tc, ta, tb, False)
    kf, vf = _kv_assemble(kvp, z, tc, ta, tb)
    att, lse = _attn_fwd("mla_fwd", qf, kf, 0, vf, 0, MLA_HEADS, HEAD_QK, HEAD_V, MLA_SCALE, True, blk)
    pool = _pool_fwd(z, w["pool_w"], w["pool_scale"])
    s = x.shape[0]
    bm = _row_block(s)
    row = pl.BlockSpec((bm, D_MODEL), lambda i, k: (i, 0))
    half = pl.BlockSpec((bm, 512), lambda i, k: (i, 0))
    h2 = _matmul(
        "w_out", (s // bm, 1),
        [(att, half, w["w_out"], pl.BlockSpec((512, D_MODEL), lambda i, k: (0, 0)), NN),
         (pool, half, w["w_out"], pl.BlockSpec((512, D_MODEL), lambda i, k: (1, 0)), NN)],
        [(h1, row)], [(jax.ShapeDtypeStruct((s, D_MODEL), F32), row)], lambda acc, r: (r + acc,), None)[0]

    n3 = _rmsnorm_fwd("xattn_norm", h2, w["xattn_norm"], D_MODEL)
    memn = _rmsnorm_fwd("mem_norm", mem, w["mem_norm"], D_MODEL)
    qm = _mm_nn("w_mq", n3, w["w_mq"], BF16)
    kvm = _mm_heads_fwd("w_mkv", memn, w["w_mkv"], BF16)
    om, lse_m = _attn_fwd("xattn_fwd", qm, kvm, 0, kvm, MEM_HEADS, MEM_HEADS, MEM_HEAD_DIM, MEM_HEAD_DIM,
                          MEM_SCALE, False, blk)
    h3 = _mm_nn("w_mo", om, w["w_mo"], F32, res=h2)

    n4 = _rmsnorm_fwd("ffn2_norm", h3, w["ffn2_norm"], D_MODEL)
    g2, u2, a2 = _ffn_up("ffn2_up", n4, w["ffn2_w_gate"], w["ffn2_w_up"])
    h4 = _ffn_down("ffn2_down", a2, w["ffn2_w_down"], h3)

    dh4, loss_vec, d_final = _loss_and_final_norm(h4, w["final_norm"], target)
    grads["final_norm"] = d_final

    dn4 = _ffn_bwd("ffn2", dh4, n4, g2, u2, a2, w["ffn2_w_gate"], w["ffn2_w_up"], w["ffn2_w_down"], grads)
    dh3, grads["ffn2_norm"] = _rmsnorm_bwd("ffn2_norm_bwd", dn4, h3, w["ffn2_norm"], D_MODEL, dres=dh4)

    dom = _mm_nt("w_mo_dx", dh3, w["w_mo"], BF16)
    grads["w_mo"] = _mm_tn("w_mo_dw", om, dh3)
    delta_m = _attn_delta("xattn_delta", dom, 0, om, MEM_HEADS, MEM_HEAD_DIM)
    dqm, dkm, dvm = _attn_bwd("xattn_bwd", qm, kvm, 0, kvm, MEM_HEADS, dom, 0, lse_m, delta_m, MEM_HEADS,
                              MEM_HEAD_DIM, MEM_HEAD_DIM, MEM_SCALE, False, blk)
    dkvm = jnp.concatenate([dkm, dvm], axis=1).astype(BF16)
    dn3 = _mm_nt("w_mq_dx", dqm, w["w_mq"], F32)
    grads["w_mq"] = _mm_tn("w_mq_dw", n3, dqm)
    dmemn, grads["w_mkv"] = _mm_heads_bwd("w_mkv", dkvm, memn, w["w_mkv"])
    _, grads["mem_norm"] = _rmsnorm_bwd("mem_norm_bwd", dmemn, mem, w["mem_norm"], D_MODEL, out_dtype=BF16)
    dh2, grads["xattn_norm"] = _rmsnorm_bwd("xattn_norm_bwd", dn3, h2, w["xattn_norm"], D_MODEL, dres=dh3)

    dcat = _mm_nt("w_out_dx", dh2, w["w_out"], BF16)
    grads["w_out"] = jnp.concatenate([_mm_tn("w_out_dw_a", att, dh2), _mm_tn("w_out_dw_p", pool, dh2)], axis=0)
    dzp, grads["pool_w"], grads["pool_scale"] = _pool_bwd(dcat, z, w["pool_w"], w["pool_scale"])
    delta = _attn_delta("mla_delta", dcat, 0, att, MLA_HEADS, HEAD_V)
    dqf, dkf, dvf = _attn_bwd("mla_bwd", qf, kf, 0, vf, 0, dcat, 0, lse, delta, MLA_HEADS, HEAD_QK, HEAD_V,
                              MLA_SCALE, True, blk)
    dqp = _q_rope(dqf, tc, ta, tb, True)
    dkvp, dkr = _kv_assemble_bwd(dkf, dvf, tc, ta, tb)
    dqn, grads["w_q_up"] = _mm_heads_bwd("q_up", dqp, qn, w["w_q_up"])
    dkvn, grads["w_kv_up"] = _mm_heads_bwd("kv_up", dkvp, kvn, w["w_kv_up"])
    dcq, grads["q_norm"] = _rmsnorm_bwd("q_norm_bwd", dqn, z, w["q_norm"], Q_LORA, 0, out_dtype=BF16)
    dckv, grads["kv_norm"] = _rmsnorm_bwd("kv_norm_bwd", dkvn, z, w["kv_norm"], KV_LORA, 2, out_dtype=BF16)
    dz = jnp.concatenate([dcq, dckv, dkr.astype(BF16), dzp.astype(BF16)], axis=1)
    dn2 = _mm_nt("w_in_dx", dz, w["w_in"], F32)
    grads["w_in"] = _mm_tn("w_in_dw", n2, dz)
    dh1, grads["mix_norm"] = _rmsnorm_bwd("mix_norm_bwd", dn2, h1, w["mix_norm"], D_MODEL, dres=dh2)

    dn1 = _ffn_bwd("ffn1", dh1, n1, g1, u1, a1, w["ffn1_w_gate"], w["ffn1_w_up"], w["ffn1_w_down"], grads)
    dx, grads["ffn1_norm"] = _rmsnorm_bwd("ffn1_norm_bwd", dn1, x, w["ffn1_norm"], D_MODEL, dres=dh1)
    return loss_vec[0, 0], dx


def _mesh_pos():
    x, y, c = lax.axis_index("x"), lax.axis_index("y"), lax.axis_index("c")
    chips = [(1 - x, y), (x, 1 - y), (1 - x, 1 - y)]
    chip_ids = [2 * cx + cy for cx, cy in chips]
    return x, y, c, 2 * x + y, chips, chip_ids


def _half_rows(c, rows):
    hr = rows // 2
    return pl.ds(pl.multiple_of(c * hr, 16), hr), pl.ds(pl.multiple_of((1 - c) * hr, 16), hr)


def _ag_ici_stage(shards):
    n = len(shards)

    def copies(ins, outs):
        x, y, c, me, chips, _ = _mesh_pos()
        out = []
        for k in range(n):
            mine, _ = _half_rows(c, ins[k].shape[0])
            out.append((ins[k], outs[k].at[me], None))
            for cx, cy in chips:
                out.append((ins[k].at[mine], outs[k].at[me, mine], (cx, cy, c)))
        return out

    return _Stage(shards, [jax.ShapeDtypeStruct((N_CHIPS,) + s.shape, s.dtype) for s in shards], 3 * n, n, copies)


def _ag_d2d_stage(fulls):
    n = len(fulls)

    def copies(ins, outs):
        x, y, c, me, _, chip_ids = _mesh_pos()
        out = []
        for k in range(n):
            mine, _ = _half_rows(c, ins[k].shape[1])
            for j in range(3):
                out.append((ins[k].at[chip_ids[j], mine], outs[k].at[chip_ids[j], mine], (x, y, 1 - c)))
        return out

    return _Stage(fulls, [jax.ShapeDtypeStruct(f.shape, f.dtype) for f in fulls], 3 * n, 0, copies,
                  aliases={k: k for k in range(n)})


def _rs_swap_stage(grads):
    n = len(grads)

    def copies(ins, outs):
        x, y, c, _, _, _ = _mesh_pos()
        out = []
        for k in range(n):
            _, other = _half_rows(c, ins[k].shape[1])
            out.append((ins[k].at[:, other, :], outs[k], (x, y, 1 - c)))
        return out

    return _Stage(grads, [jax.ShapeDtypeStruct((N_CHIPS, g.shape[1] // 2, g.shape[2]), g.dtype) for g in grads],
                  n, 0, copies)


def _rs_scatter_stage(sums):
    n = len(sums)

    def copies(ins, outs):
        x, y, c, me, chips, chip_ids = _mesh_pos()
        out = []
        for k in range(n):
            mine, _ = _half_rows(c, 2 * ins[k].shape[1])
            out.append((ins[k].at[me], outs[k].at[0, mine, :], None))
            for j, (cx, cy) in enumerate(chips):
                out.append((ins[k].at[chip_ids[j]], outs[k].at[1 + j, mine, :], (cx, cy, c)))
        return out

    return _Stage(sums, [jax.ShapeDtypeStruct((N_CHIPS, 2 * s.shape[1], s.shape[2]), s.dtype) for s in sums],
                  3 * n, n, copies)


def _rs_mirror_stage(parts):
    n = len(parts)

    def copies(ins, outs):
        x, y, c, _, _, _ = _mesh_pos()
        out = []
        for k in range(n):
            mine, _ = _half_rows(c, ins[k].shape[1])
            out.append((ins[k].at[:, mine, :], outs[k].at[:, mine, :], (x, y, 1 - c)))
        return out

    return _Stage(parts, [jax.ShapeDtypeStruct(p.shape, p.dtype) for p in parts], n, 0, copies,
                  aliases={k: k for k in range(n)})


def _pair_add(name, g, r1, core):
    _, rows, cols = g.shape
    hr = rows // 2

    def body(c_ref, g_ref, r_ref, o_ref):
        o_ref[...] = (g_ref[...].astype(F32) + r_ref[...].astype(F32)).astype(BF16)

    half = pl.BlockSpec((None, hr, cols), lambda j, c: (j, 0, 0))
    return pl.pallas_call(
        body, name=name,
        grid_spec=pltpu.PrefetchScalarGridSpec(
            num_scalar_prefetch=1, grid=(N_CHIPS,),
            in_specs=[pl.BlockSpec((None, hr, cols), lambda j, c: (j, c[0], 0)), half], out_specs=half),
        out_shape=jax.ShapeDtypeStruct((N_CHIPS, hr, cols), BF16),
        compiler_params=_params("parallel"),
    )(core, g, r1)


def _all_gather_weights(shards):
    n = len(shards)

    def body(*refs):
        ins, outs = refs[:n], refs[n:2 * n]
        send, recv, loc = refs[2 * n:]
        x, y, c, me, chips, chip_ids = _mesh_pos()
        sib = (x, y, 1 - c)

        def halves(k):
            hr = ins[k].shape[0] // 2
            return pl.ds(pl.multiple_of(c * hr, 16), hr), pl.ds(pl.multiple_of((1 - c) * hr, 16), hr)

        def remote(src, dst, k, j, dev):
            return pltpu.make_async_remote_copy(src_ref=src, dst_ref=dst, send_sem=send.at[k, j],
                                                recv_sem=recv.at[k, j], device_id=dev, device_id_type=_MESH)

        started = []
        local = []
        for k in range(n):
            mine, _ = halves(k)
            cp = pltpu.make_async_copy(ins[k], outs[k].at[me], loc.at[k])
            cp.start()
            local.append(cp)
            for j, (cx, cy) in enumerate(chips):
                cp = remote(ins[k].at[mine], outs[k].at[me, mine], k, j, (cx, cy, c))
                cp.start()
                started.append(cp)
        for k in range(n):
            mine, _ = halves(k)
            for j in range(3):
                land = outs[k].at[chip_ids[j], mine]
                remote(land, land, k, j, sib).wait_recv()
                cp = remote(land, land, k, 3 + j, sib)
                cp.start()
                started.append(cp)
        for k in range(n):
            _, other = halves(k)
            for j in range(3):
                land = outs[k].at[chip_ids[j], other]
                remote(land, land, k, 3 + j, sib).wait_recv()
        for cp in started:
            cp.wait_send()
        for cp in local:
            cp.wait()

    return pl.pallas_call(
        body, name="all_gather_weights", in_specs=[_ANY] * n, out_specs=[_ANY] * n,
        out_shape=[jax.ShapeDtypeStruct((N_CHIPS,) + s.shape, s.dtype) for s in shards],
        scratch_shapes=[pltpu.SemaphoreType.DMA((n, 6)), pltpu.SemaphoreType.DMA((n, 6)),
                        pltpu.SemaphoreType.DMA((n,))],
        compiler_params=pltpu.CompilerParams(vmem_limit_bytes=V7X_VMEM_LIMIT_BYTES),
    )(*shards)


_RS_CHUNK = 32


def _reduce_scatter(name, grads):
    n = len(grads)

    def body(*refs):
        gs, outs = refs[:n], refs[n:2 * n]
        own, r1, r2, fin = (refs[(2 + i) * n:(3 + i) * n] for i in range(4))
        a_send, a_recv, b_send, b_recv, c_send, c_recv, l_in, l_out = refs[6 * n:]
        x, y, c, me, chips, chip_ids = _mesh_pos()
        sib = (x, y, 1 - c)

        def halves(k):
            hr = gs[k].shape[1] // 2
            return hr, pl.ds(pl.multiple_of(c * hr, 16), hr), pl.ds(pl.multiple_of((1 - c) * hr, 16), hr)

        def remote(src, dst, ssem, rsem, dev):
            return pltpu.make_async_remote_copy(src_ref=src, dst_ref=dst, send_sem=ssem, recv_sem=rsem,
                                                device_id=dev, device_id_type=_MESH)

        sends, locals_in = [], []
        for k in range(n):
            hr, mine, other = halves(k)
            cp = remote(gs[k].at[:, other, :], r1[k], a_send.at[k], a_recv.at[k], sib)
            cp.start()
            sends.append(cp)
            cp = pltpu.make_async_copy(gs[k].at[:, mine, :], own[k], l_in.at[k])
            cp.start()
            locals_in.append(cp)

        for k in range(n):
            hr, mine, other = halves(k)
            locals_in[k].wait()
            remote(r1[k], r1[k], a_send.at[k], a_recv.at[k], sib).wait_recv()
            for j in range(N_CHIPS):
                def add(i, carry, k=k, j=j):
                    rows = pl.ds(pl.multiple_of(i * _RS_CHUNK, _RS_CHUNK), _RS_CHUNK)
                    own[k][j, rows, :] = (own[k][j, rows, :].astype(F32) + r1[k][j, rows, :].astype(F32)).astype(BF16)
                    return carry

                lax.fori_loop(0, hr // _RS_CHUNK, add, 0)
            for j, (cx, cy) in enumerate(chips):
                cp = remote(own[k].at[chip_ids[j]], r2[k].at[j], b_send.at[k, j], b_recv.at[k, j], (cx, cy, c))
                cp.start()
                sends.append(cp)

        locals_out = []
        for k in range(n):
            hr, mine, other = halves(k)
            for j in range(3):
                remote(r2[k].at[j], r2[k].at[j], b_send.at[k, j], b_recv.at[k, j], sib).wait_recv()

            def total(i, carry, k=k):
                rows = pl.ds(pl.multiple_of(i * _RS_CHUNK, _RS_CHUNK), _RS_CHUNK)
                acc = own[k][me, rows, :].astype(F32)
                for j in range(3):
                    acc = acc + r2[k][j, rows, :].astype(F32)
                fin[k][rows, :] = acc
                return carry

            lax.fori_loop(0, hr // _RS_CHUNK, total, 0)
            cp = remote(fin[k], outs[k].at[mine, :], c_send.at[k], c_recv.at[k], sib)
            cp.start()
            sends.append(cp)
            cp = pltpu.make_async_copy(fin[k], outs[k].at[mine, :], l_out.at[k])
            cp.start()
            locals_out.append(cp)

        for k in range(n):
            hr, mine, other = halves(k)
            land = outs[k].at[other, :]
            remote(land, land, c_send.at[k], c_recv.at[k], sib).wait_recv()
        for cp in sends:
            cp.wait_send()
        for cp in locals_out:
            cp.wait()

    scratch = []
    for g in grads:
        scratch.append(pltpu.VMEM((N_CHIPS, g.shape[1] // 2, g.shape[2]), BF16))
    for g in grads:
        scratch.append(pltpu.VMEM((N_CHIPS, g.shape[1] // 2, g.shape[2]), BF16))
    for g in grads:
        scratch.append(pltpu.VMEM((3, g.shape[1] // 2, g.shape[2]), BF16))
    for g in grads:
        scratch.append(pltpu.VMEM((g.shape[1] // 2, g.shape[2]), F32))
    dma = pltpu.SemaphoreType.DMA
    scratch += [dma((n,)), dma((n,)), dma((n, 3)), dma((n, 3)), dma((n,)), dma((n,)), dma((n,)), dma((n,))]
    return pl.pallas_call(
        body, name=name, in_specs=[_ANY] * n, out_specs=[_ANY] * n,
        out_shape=[jax.ShapeDtypeStruct(g.shape[1:], F32) for g in grads],
        scratch_shapes=scratch,
        compiler_params=pltpu.CompilerParams(vmem_limit_bytes=V7X_VMEM_LIMIT_BYTES),
    )(*grads)


def _adamw_math(w, g, m, v):
    m = ADAM_B1 * m + (1.0 - ADAM_B1) * g
    v = ADAM_B2 * v + (1.0 - ADAM_B2) * (g * g)
    m_hat = m / (1.0 - ADAM_B1 ** ADAM_STEP)
    v_hat = v / (1.0 - ADAM_B2 ** ADAM_STEP)
    delta = -ADAM_LR * (m_hat / (jnp.sqrt(v_hat) + ADAM_EPS) + ADAM_WD * w)
    return delta, m, v


def _adamw_sum(name, w, parts, m, v):
    r, c = w.shape
    br = r
    while br * c * 4 > (1 << 20) and br % 32 == 0:
        br //= 2

    def body(w_ref, p_ref, m_ref, v_ref, g_ref, d_ref, nm_ref, nv_ref):
        g = p_ref[0].astype(F32)
        for j in range(1, N_CHIPS):
            g = g + p_ref[j].astype(F32)
        d, nm, nv = _adamw_math(w_ref[...], g, m_ref[...], v_ref[...])
        g_ref[...] = g
        d_ref[...] = d
        nm_ref[...] = nm
        nv_ref[...] = nv

    spec = pl.BlockSpec((br, c), lambda i: (i, 0))
    shp = jax.ShapeDtypeStruct((r, c), F32)
    return _call(
        body, name=name, grid=(r // br,),
        in_specs=[spec, pl.BlockSpec((N_CHIPS, br, c), lambda i: (0, i, 0)), spec, spec],
        out_specs=[spec] * 4, out_shape=[shp] * 4, sem=("parallel",), args=[w, parts, m, v])


_SMALL_ROWS = 80


def _small_allreduce_adamw(gpack, wpack, mpack, vpack):
    def body(g_ref, w_ref, m_ref, v_ref, go_ref, d_ref, nm_ref, nv_ref, buf, send, recv):
        x, y, c = lax.axis_index("x"), lax.axis_index("y"), lax.axis_index("c")
        me = 4 * x + 2 * y + c
        buf[me] = g_ref[...]
        copies = []
        for rel in range(1, 8):
            fx, fy, fc = (rel >> 2) & 1, (rel >> 1) & 1, rel & 1
            dev = ((1 - x) if fx else x, (1 - y) if fy else y, (1 - c) if fc else c)
            cp = pltpu.make_async_remote_copy(src_ref=g_ref, dst_ref=buf.at[me], send_sem=send.at[rel - 1],
                                              recv_sem=recv.at[rel - 1], device_id=dev, device_id_type=_MESH)
            cp.start()
            copies.append(cp)
        for cp in copies:
            cp.wait_recv()
        for cp in copies:
            cp.wait_send()
        total = buf[0]
        for i in range(1, 8):
            total = total + buf[i]
        go_ref[...] = total
        d, nm, nv = _adamw_math(w_ref[...], total, m_ref[...], v_ref[...])
        d_ref[...] = d
        nm_ref[...] = nm
        nv_ref[...] = nv

    vm = pl.BlockSpec(memory_space=pltpu.VMEM)
    shp = jax.ShapeDtypeStruct((_SMALL_ROWS, D_MODEL), F32)
    return pl.pallas_call(
        body, name="small_allreduce_adamw", in_specs=[vm] * 4, out_specs=[vm] * 4, out_shape=[shp] * 4,
        scratch_shapes=[pltpu.VMEM((8, _SMALL_ROWS, D_MODEL), F32), pltpu.SemaphoreType.DMA((7,)),
                        pltpu.SemaphoreType.DMA((7,))],
        compiler_params=pltpu.CompilerParams(vmem_limit_bytes=V7X_VMEM_LIMIT_BYTES),
    )(gpack, wpack, mpack, vpack)


_SMALL_VECTORS = ("ffn1_norm", "mix_norm", "xattn_norm", "mem_norm", "ffn2_norm", "final_norm", "q_norm",
                  "kv_norm", "pool_scale")


def _pack_small(d):
    rows = []
    for n in _SMALL_VECTORS:
        v = d[n].reshape(1, -1).astype(F32)
        rows.append(jnp.pad(v, ((0, 0), (0, D_MODEL - v.shape[1]))))
    rows.append(jnp.zeros((16 - len(_SMALL_VECTORS), D_MODEL), F32))
    rows.append(d["pool_w"].reshape(64, D_MODEL).astype(F32))
    return jnp.concatenate(rows, axis=0)


def _unpack_small(pack, like):
    out = {}
    for i, n in enumerate(_SMALL_VECTORS):
        out[n] = pack[i, :like[n].size].reshape(like[n].shape)
    out["pool_w"] = pack[16:].reshape(like["pool_w"].shape)
    return out


_WEIGHTS = ("ffn1_norm", "ffn1_w_gate", "ffn1_w_up", "ffn1_w_down", "mix_norm", "w_in", "q_norm", "w_q_up",
            "kv_norm", "w_kv_up", "pool_w", "pool_scale", "w_out", "xattn_norm", "mem_norm", "w_mq", "w_mkv",
            "w_mo", "ffn2_norm", "ffn2_w_gate", "ffn2_w_up", "ffn2_w_down", "final_norm")
_SHARDED = ("ffn1_w_gate", "ffn1_w_up", "ffn1_w_down", "w_in", "w_q_up", "w_kv_up", "w_out", "w_mq", "w_mkv",
            "w_mo", "ffn2_w_gate", "ffn2_w_up", "ffn2_w_down")
_RS_GROUPS = (("ffn2_w_gate", "ffn2_w_up", "ffn2_w_down"),
              ("w_mo", "w_mq", "w_mkv", "w_out", "w_q_up", "w_kv_up", "w_in"),
              ("ffn1_w_gate", "ffn1_w_up", "ffn1_w_down"))
W_IN_SPLIT = Q_LORA + KV_LORA + ROPE_DIM


def _pad_shard(name, a):
    if name == "w_in":
        return jnp.concatenate([a[:, :W_IN_SPLIT], jnp.zeros((a.shape[0], 64), a.dtype), a[:, W_IN_SPLIT:]], axis=1)
    if name == "w_q_up":
        return jnp.pad(a, ((0, 0), (0, 64)))
    return a


def _unpad_shard(name, a):
    if name == "w_in":
        return jnp.concatenate([a[..., :W_IN_SPLIT], a[..., W_IN_SPLIT + 64:]], axis=-1)
    if name == "w_q_up":
        return a[..., :192]
    return a


def _stacked(g):
    return g if g.ndim == 3 else g.reshape(N_CHIPS, g.shape[0] // N_CHIPS, g.shape[1])


class _Plan:
    AG_UNITS = (
        (("w_in", "w_q_up", "w_kv_up", "w_out", "w_mq", "w_mkv", "w_mo"), "ffn1_up", "ffn1_down"),
        (("ffn2_w_gate", "ffn2_w_up", "ffn2_w_down"), "mla_fwd", "w_out"),
    )
    RS_UNITS = (
        (("ffn2_w_gate", "ffn2_w_up", "ffn2_w_down"), "ffn2_norm_bwd", "mla_bwd", "q_up_dx"),
        (("w_mo", "w_mq", "w_mkv", "w_out", "w_q_up", "w_kv_up", "w_in"), "mix_norm_bwd", "ffn1_dact", "ffn1_dwd"),
        (("ffn1_w_down",), "ffn1_dwg", "ffn1_dwu", "ffn1_norm_bwd"),
        (("ffn1_w_gate",), "ffn1_dwu", "ffn1_dn", "ffn1_norm_bwd"),
        (("ffn1_w_up",), "ffn1_dn", "ffn1_norm_bwd", "adamw_ffn2_w_gate"),
    )

    def __init__(self, shards, w, grads, core):
        self.shards, self.w, self.grads, self.core = shards, w, grads, core
        self.parts = {}
        self.ag = [[None, None] for _ in self.AG_UNITS]
        self.rs = [[None, None, None, None] for _ in self.RS_UNITS]

    def pre(self, name):
        for i, (names, h1, h2) in enumerate(self.AG_UNITS):
            if name == h1:
                self.ag[i][0] = _host(name, _ag_ici_stage([self.shards[n] for n in names]))
            if name == h2:
                self.ag[i][1] = _host(name, _ag_d2d_stage(self.ag[i][0].results))
        for i, (names, h1, h2, h3) in enumerate(self.RS_UNITS):
            if name == h1:
                self.rs[i][0] = _host(name, _rs_swap_stage([_stacked(self.grads[n]) for n in names]))
            if name == h2:
                self.rs[i][2] = _host(name, _rs_scatter_stage(self.rs[i][1]))
            if name == h3:
                self.rs[i][3] = _host(name, _rs_mirror_stage(self.rs[i][2].results))

    def post(self, name):
        for i, (names, h1, h2) in enumerate(self.AG_UNITS):
            if name == h2:
                for n, f in zip(names, self.ag[i][1].results):
                    self.w[n] = _full_weight(n, f)
        for i, (names, h1, h2, h3) in enumerate(self.RS_UNITS):
            if name == h1:
                self.rs[i][1] = [_pair_add("pair_add_" + n, _stacked(self.grads[n]), r1, self.core)
                                 for n, r1 in zip(names, self.rs[i][0].results)]
            if name == h3:
                for n, p in zip(names, self.rs[i][3].results):
                    self.parts[n] = p


def _full_weight(name, stacked):
    if name in ("w_in", "w_out", "w_mq", "w_mo"):
        return stacked.reshape(D_MODEL, D_MODEL)
    return stacked


def kernel(x, mem, positions, ffn1_norm, ffn1_w_gate, ffn1_w_up, ffn1_w_down, mix_norm, w_in, q_norm, w_q_up, kv_norm, w_kv_up, pool_w, pool_scale, w_out, xattn_norm, mem_norm, w_mq, w_mkv, w_mo, ffn2_norm, ffn2_w_gate, ffn2_w_up, ffn2_w_down, final_norm, loss_target, m_ffn1_norm, m_ffn1_w_gate, m_ffn1_w_up, m_ffn1_w_down, m_mix_norm, m_w_in, m_q_norm, m_w_q_up, m_kv_norm, m_w_kv_up, m_pool_w, m_pool_scale, m_w_out, m_xattn_norm, m_mem_norm, m_w_mq, m_w_mkv, m_w_mo, m_ffn2_norm, m_ffn2_w_gate, m_ffn2_w_up, m_ffn2_w_down, m_final_norm, v_ffn1_norm, v_ffn1_w_gate, v_ffn1_w_up, v_ffn1_w_down, v_mix_norm, v_w_in, v_q_norm, v_w_q_up, v_kv_norm, v_w_kv_up, v_pool_w, v_pool_scale, v_w_out, v_xattn_norm, v_mem_norm, v_w_mq, v_w_mkv, v_w_mo, v_ffn2_norm, v_ffn2_w_gate, v_ffn2_w_up, v_ffn2_w_down, v_final_norm):
    wts = dict(zip(_WEIGHTS, (ffn1_norm, ffn1_w_gate, ffn1_w_up, ffn1_w_down, mix_norm, w_in, q_norm, w_q_up, kv_norm, w_kv_up, pool_w, pool_scale, w_out, xattn_norm, mem_norm, w_mq, w_mkv, w_mo, ffn2_norm, ffn2_w_gate, ffn2_w_up, ffn2_w_down, final_norm)))
    mom = dict(zip(_WEIGHTS, (m_ffn1_norm, m_ffn1_w_gate, m_ffn1_w_up, m_ffn1_w_down, m_mix_norm, m_w_in, m_q_norm, m_w_q_up, m_kv_norm, m_w_kv_up, m_pool_w, m_pool_scale, m_w_out, m_xattn_norm, m_mem_norm, m_w_mq, m_w_mkv, m_w_mo, m_ffn2_norm, m_ffn2_w_gate, m_ffn2_w_up, m_ffn2_w_down, m_final_norm)))
    var = dict(zip(_WEIGHTS, (v_ffn1_norm, v_ffn1_w_gate, v_ffn1_w_up, v_ffn1_w_down, v_mix_norm, v_w_in, v_q_norm, v_w_q_up, v_kv_norm, v_w_kv_up, v_pool_w, v_pool_scale, v_w_out, v_xattn_norm, v_mem_norm, v_w_mq, v_w_mkv, v_w_mo, v_ffn2_norm, v_ffn2_w_gate, v_ffn2_w_up, v_ffn2_w_down, v_final_norm)))
    small = [n for n in _WEIGHTS if n not in _SHARDED]

    global _PLAN
    shards = {n: _pad_shard(n, wts[n][0]).astype(BF16) for n in _SHARDED}
    w = {n: wts[n].reshape(1, -1) for n in _SMALL_VECTORS}
    w["pool_w"] = pool_w[0].astype(BF16)
    grads = {}
    core = lax.axis_index("c").astype(jnp.int32).reshape(1)
    plan = _Plan(shards, w, grads, core)
    _PLAN = plan
    try:
        first = ("ffn1_w_gate", "ffn1_w_up", "ffn1_w_down")
        for n, f in zip(first, _all_gather_weights([shards[n] for n in first])):
            w[n] = f

        loss_local, dx = _local_step(x[0], mem[0], positions[0], loss_target[0], w, grads)
        loss = lax.psum(loss_local, ("x", "y", "c"))

        gpack, dpack, mpack, vpack = _small_allreduce_adamw(
            _pack_small({n: grads[n] for n in small}), _pack_small({n: wts[n] for n in small}),
            _pack_small({n: mom[n] for n in small}), _pack_small({n: var[n] for n in small}))
        like = {n: wts[n] for n in small}
        g_out, d_out, m_out, v_out = (_unpack_small(p, like) for p in (gpack, dpack, mpack, vpack))

        for names in [u[0] for u in _Plan.RS_UNITS]:
            for n in names:
                g, d, nm, nv = _adamw_sum("adamw_" + n, wts[n][0], _unpad_shard(n, plan.parts[n]), mom[n][0],
                                          var[n][0])
                g_out[n], d_out[n], m_out[n], v_out[n] = g[None], d[None], nm[None], nv[None]
    finally:
        _PLAN = None
        _PENDING.clear()

    return (loss, dx[None], *[g_out[n] for n in _WEIGHTS], *[d_out[n] for n in _WEIGHTS],
            *[m_out[n] for n in _WEIGHTS], *[v_out[n] for n in _WEIGHTS])
```

```python
import functools

import jax
import jax.numpy as jnp
from jax import lax
from jax.experimental import pallas as pl
from jax.experimental.pallas import tpu as pltpu

F32 = jnp.float32
BF16 = jnp.bfloat16

D_MODEL = 1024
D_FF = 2816
N_CHIPS = 4
FF_SHARD = D_FF // N_CHIPS
MLA_HEADS = 4
Q_LORA = 256
KV_LORA = 128
ROPE_DIM = 64
HEAD_QK = 256
HEAD_V = 128
POOL_GROUPS = 4
POOL_CH = 128
MEM_HEADS = 4
MEM_HEAD_DIM = 256
RMS_EPS = 1e-6
ROPE_BASE = 10000.0
MLA_SCALE = (128 + 64) ** -0.5
MEM_SCALE = MEM_HEAD_DIM ** -0.5

ADAM_LR = 0.001
ADAM_B1 = 0.9
ADAM_B2 = 0.999
ADAM_EPS = 1e-08
ADAM_WD = 0.01
ADAM_STEP = 10

V7X_VMEM_LIMIT_BYTES = 56 * 1024 * 1024

NN = ((1,), (0,))
NT = ((1,), (1,))
TN = ((0,), (0,))


def _params(*sem):
    return pltpu.CompilerParams(dimension_semantics=sem, vmem_limit_bytes=V7X_VMEM_LIMIT_BYTES)


_MESH = pl.DeviceIdType.MESH
_ANY = pl.BlockSpec(memory_space=pl.ANY)


class _Stage:
    def __init__(self, ins, outs, n_remote, n_local, copies, aliases=None):
        self.ins, self.outs, self.n_remote, self.n_local = list(ins), list(outs), n_remote, n_local
        self.copies, self.aliases = copies, dict(aliases or {})
        self.results = None

    def descriptors(self, in_refs, out_refs, send, recv, loc):
        ds, ri, li = [], 0, 0
        for src, dst, dev in self.copies(in_refs, out_refs):
            if dev is None:
                ds.append(pltpu.make_async_copy(src, dst, loc.at[li]))
                li += 1
            else:
                ds.append(pltpu.make_async_remote_copy(src_ref=src, dst_ref=dst, send_sem=send.at[ri],
                                                       recv_sem=recv.at[ri], device_id=dev, device_id_type=_MESH))
                ri += 1
        assert ri == self.n_remote and li == self.n_local
        return ds


_PENDING = {}


def _host(name, stage):
    _PENDING.setdefault(name, []).append(stage)
    return stage


_PLAN = None


def _call(body, **kw):
    if _PLAN is not None:
        _PLAN.pre(kw["name"])
    res = _call_hosting(body, **kw)
    if _PLAN is not None:
        _PLAN.post(kw["name"])
    return res


def _call_hosting(body, *, name, grid, in_specs, out_specs, out_shape, sem, args, scratch_shapes=()):
    stages = _PENDING.pop(name, [])
    scratch_shapes = list(scratch_shapes)
    if not stages:
        return pl.pallas_call(body, name=name, grid=grid, in_specs=in_specs, out_specs=out_specs,
                              out_shape=out_shape, scratch_shapes=scratch_shapes,
                              compiler_params=_params(*sem))(*args)
    ni, no, ns = len(in_specs), len(out_shape), len(scratch_shapes)
    c_ins = [a for st in stages for a in st.ins]
    c_outs = [o for st in stages for o in st.outs]
    nci, nco = len(c_ins), len(c_outs)
    aliases, io, oo = {}, 0, 0
    for st in stages:
        for i, j in st.aliases.items():
            aliases[ni + io + i] = no + oo + j
        io += len(st.ins)
        oo += len(st.outs)
    dma = pltpu.SemaphoreType.DMA
    sems = []
    for st in stages:
        sems += [dma((max(st.n_remote, 1),)), dma((max(st.n_remote, 1),)), dma((max(st.n_local, 1),))]

    def wrapped(*refs):
        ins, cin = refs[:ni], refs[ni:ni + nci]
        outs, cout = refs[ni + nci:ni + nci + no], refs[ni + nci + no:ni + nci + no + nco]
        scr = refs[ni + nci + no + nco:ni + nci + no + nco + ns]
        sem_refs = refs[ni + nci + no + nco + ns:]
        first = pl.program_id(0) == 0
        last = pl.program_id(0) == grid[0] - 1
        for ax in range(1, len(grid)):
            first = jnp.logical_and(first, pl.program_id(ax) == 0)
            last = jnp.logical_and(last, pl.program_id(ax) == grid[ax] - 1)

        def descriptors():
            ds, io, oo = [], 0, 0
            for si, st in enumerate(stages):
                ds += st.descriptors(cin[io:io + len(st.ins)], cout[oo:oo + len(st.outs)], *sem_refs[3 * si:3 * si + 3])
                io += len(st.ins)
                oo += len(st.outs)
            return ds

        @pl.when(first)
        def _():
            for d in descriptors():
                d.start()

        body(*ins, *outs, *scr)

        @pl.when(last)
        def _():
            for d in descriptors():
                d.wait()

    res = pl.pallas_call(
        wrapped, name=name, grid=grid, in_specs=list(in_specs) + [_ANY] * nci,
        out_specs=list(out_specs) + [_ANY] * nco, out_shape=list(out_shape) + c_outs,
        scratch_shapes=scratch_shapes + sems, input_output_aliases=aliases,
        compiler_params=_params(*(("arbitrary",) * len(grid))))(*args, *c_ins)
    oo = no
    for st in stages:
        st.results = list(res[oo:oo + len(st.outs)])
        oo += len(st.outs)
    return list(res[:no])


def _dot(a, b, dims):
    return lax.dot_general(a.astype(BF16), b.astype(BF16), (dims, ((), ())), preferred_element_type=F32)


_MAX_ROW_BLOCK = 1024
_ATT_BLOCK = 512


_MAX_REDUCE_BLOCK = 2048


def _row_block(s, want=1024):
    return min(want, s, _MAX_ROW_BLOCK)


def _reduce_block(s):
    return min(s, _MAX_REDUCE_BLOCK)


def _matmul(name, grid, terms, extras, outs, epilogue, acc_shape):
    nt, ne, no = len(terms), len(extras), len(outs)
    nk = grid[-1]
    dims = [t[4] for t in terms]

    def body(*refs):
        a_refs, b_refs = refs[:nt], refs[nt:2 * nt]
        e_refs = refs[2 * nt:2 * nt + ne]
        o_refs = refs[2 * nt + ne:2 * nt + ne + no]
        def finish(acc):
            vals = epilogue(acc, *[e[...] for e in e_refs])
            for o, val in zip(o_refs, vals):
                o[...] = val.astype(o.dtype)

        if nk == 1:
            part = None
            for a, b, d in zip(a_refs, b_refs, dims):
                t = _dot(a[...], b[...], d)
                part = t if part is None else part + t
            finish(part)
        else:
            acc_ref = refs[-1]
            k = pl.program_id(len(grid) - 1)

            @pl.when(k == 0)
            def _():
                acc_ref[...] = jnp.zeros_like(acc_ref)

            for a, b, d in zip(a_refs, b_refs, dims):
                acc_ref[...] += _dot(a[...], b[...], d)

            @pl.when(k == nk - 1)
            def _():
                finish(acc_ref[...])

    in_specs = [t[1] for t in terms] + [t[3] for t in terms] + [e[1] for e in extras]
    args = [t[0] for t in terms] + [t[2] for t in terms] + [e[0] for e in extras]
    sem = ("parallel",) * (len(grid) - 1) + ("arbitrary",)
    return _call(
        body, name=name, grid=grid, in_specs=in_specs,
        out_specs=[o[1] for o in outs], out_shape=[o[0] for o in outs],
        scratch_shapes=[pltpu.VMEM(acc_shape, F32)] if nk > 1 else [], sem=sem, args=args)


def _ident(acc):
    return (acc,)


def _rmsnorm_fwd(name, x, gain, width, col_block=0):
    s = x.shape[0]
    bm = _row_block(s)

    def body(x_ref, g_ref, o_ref):
        xf = x_ref[...]
        r = lax.rsqrt(jnp.mean(xf * xf, axis=-1, keepdims=True) + RMS_EPS)
        o_ref[...] = ((xf * r) * g_ref[...]).astype(o_ref.dtype)

    return pl.pallas_call(
        body, name=name, grid=(s // bm,),
        in_specs=[pl.BlockSpec((bm, width), lambda i: (i, col_block)), pl.BlockSpec((1, width), lambda i: (0, 0))],
        out_specs=pl.BlockSpec((bm, width), lambda i: (i, 0)),
        out_shape=jax.ShapeDtypeStruct((s, width), BF16),
        compiler_params=_params("parallel"),
    )(x, gain)


def _rms_bwd_math(dy, xf, g, width):
    r = lax.rsqrt(jnp.mean(xf * xf, axis=-1, keepdims=True) + RMS_EPS)
    dyg = dy * g
    dot = jnp.sum(dyg * xf, axis=-1, keepdims=True)
    dx = r * dyg - xf * ((r * r * r) * (dot * (1.0 / width)))
    dgain = jnp.sum(dy * (xf * r), axis=0, keepdims=True)
    return dx, dgain


def _rmsnorm_bwd(name, dy, x, gain, width, col_block=0, dres=None, out_dtype=F32):
    s = x.shape[0]
    bm = _row_block(s)
    has_res = dres is not None

    def body(*refs):
        if has_res:
            dy_ref, x_ref, g_ref, r_ref, dx_ref, dg_ref, dxb_ref = refs
        else:
            dy_ref, x_ref, g_ref, dx_ref, dg_ref = refs
        dx, dgain = _rms_bwd_math(dy_ref[...].astype(F32), x_ref[...], g_ref[...], width)
        if has_res:
            dx = dx + r_ref[...]
            dxb_ref[...] = dx.astype(BF16)
        dx_ref[...] = dx.astype(dx_ref.dtype)

        @pl.when(pl.program_id(0) == 0)
        def _():
            dg_ref[...] = dgain

        @pl.when(pl.program_id(0) > 0)
        def _():
            dg_ref[...] += dgain

    row = pl.BlockSpec((bm, width), lambda i: (i, 0))
    in_specs = [row, pl.BlockSpec((bm, width), lambda i: (i, col_block)), pl.BlockSpec((1, width), lambda i: (0, 0))]
    args = [dy, x, gain]
    out_specs = [row, pl.BlockSpec((1, width), lambda i: (0, 0))]
    out_shape = [jax.ShapeDtypeStruct((s, width), out_dtype), jax.ShapeDtypeStruct((1, width), F32)]
    if has_res:
        in_specs.append(row)
        args.append(dres)
        out_specs.append(row)
        out_shape.append(jax.ShapeDtypeStruct((s, width), BF16))
    return _call(body, name=name, grid=(s // bm,), in_specs=in_specs, out_specs=out_specs, out_shape=out_shape,
                 sem=("arbitrary",), args=args)


def _loss_and_final_norm(h, gain, target):
    s, d = h.shape
    bm = _row_block(s, 512)

    def body(h_ref, g_ref, t_ref, dh_ref, dhb_ref, loss_ref, dg_ref):
        xf = h_ref[...]
        g = g_ref[...]
        r = lax.rsqrt(jnp.mean(xf * xf, axis=-1, keepdims=True) + RMS_EPS)
        err = (xf * r) * g - t_ref[...]
        part = 0.5 * jnp.sum(jnp.mean(err * err, axis=-1, keepdims=True), axis=0, keepdims=True)
        dx, dgain = _rms_bwd_math(err * (1.0 / d), xf, g, d)
        dh_ref[...] = dx
        dhb_ref[...] = dx.astype(BF16)

        @pl.when(pl.program_id(0) == 0)
        def _():
            dg_ref[...] = dgain
            loss_ref[...] = jnp.broadcast_to(part, loss_ref.shape)

        @pl.when(pl.program_id(0) > 0)
        def _():
            dg_ref[...] += dgain
            loss_ref[...] += jnp.broadcast_to(part, loss_ref.shape)

    row = pl.BlockSpec((bm, d), lambda i: (i, 0))
    vec = pl.BlockSpec((1, d), lambda i: (0, 0))
    return pl.pallas_call(
        body, name="loss_final_norm", grid=(s // bm,), in_specs=[row, vec, row],
        out_specs=[row, row, pl.BlockSpec((1, 128), lambda i: (0, 0)), vec],
        out_shape=[jax.ShapeDtypeStruct((s, d), F32), jax.ShapeDtypeStruct((s, d), BF16),
                   jax.ShapeDtypeStruct((1, 128), F32),
                   jax.ShapeDtypeStruct((1, d), F32)],
        compiler_params=_params("arbitrary"),
    )(h, gain, target)


def _ffn_up(name, n, wg, wu):
    s = n.shape[0]
    bm = _row_block(s)

    def body(n_ref, wg_ref, wu_ref, a_ref, dadu_ref, dadg_ref):
        x = n_ref[...]
        g = _dot(x, wg_ref[...], NN)
        u = _dot(x, wu_ref[...], NN)
        sg = jax.nn.sigmoid(g)
        silu = g * sg
        a_ref[...] = (silu * u).astype(BF16)
        dadu_ref[...] = silu.astype(BF16)
        dadg_ref[...] = (u * (sg * (1.0 + g * (1.0 - sg)))).astype(BF16)

    w_spec = pl.BlockSpec((None, D_MODEL, FF_SHARD), lambda j, i: (j, 0, 0))
    o_spec = pl.BlockSpec((None, bm, FF_SHARD), lambda j, i: (j, i, 0))
    shp = jax.ShapeDtypeStruct((N_CHIPS, s, FF_SHARD), BF16)
    return _call(
        body, name=name, grid=(N_CHIPS, s // bm),
        in_specs=[pl.BlockSpec((bm, D_MODEL), lambda j, i: (i, 0)), w_spec, w_spec],
        out_specs=[o_spec, o_spec, o_spec], out_shape=[shp, shp, shp],
        sem=("parallel", "parallel"), args=[n, wg, wu])


def _ffn_down(name, a, wd, res):
    s = a.shape[1]
    bm = _row_block(s, 512)
    row = pl.BlockSpec((bm, D_MODEL), lambda i, k: (i, 0))
    terms = [(a, pl.BlockSpec((None, bm, FF_SHARD), lambda i, k, j=j: (j, i, 0)),
              wd, pl.BlockSpec((None, FF_SHARD, D_MODEL), lambda i, k, j=j: (j, 0, 0)), NN) for j in range(N_CHIPS)]
    return _matmul(
        name, (s // bm, 1), terms, [(res, row)], [(jax.ShapeDtypeStruct((s, D_MODEL), F32), row)],
        lambda acc, r: (r + 0.5 * acc,), None)[0]


def _ffn_bwd(tag, dh, n, dadg, dadu, a, wg, wu, wd, grads):
    s = dh.shape[0]
    bm = _row_block(s)
    bk = _reduce_block(s)
    nk = s // bk

    def act_bwd(acc, dg_da, du_da):
        da = 0.5 * acc
        return da * dg_da.astype(F32), da * du_da.astype(F32)

    slab = pl.BlockSpec((None, bm, FF_SHARD), lambda j, i, k: (j, i, 0))
    shp = jax.ShapeDtypeStruct((N_CHIPS, s, FF_SHARD), BF16)
    dg, du = _matmul(
        tag + "_dact", (N_CHIPS, s // bm, 1),
        [(dh, pl.BlockSpec((bm, D_MODEL), lambda j, i, k: (i, 0)),
          wd, pl.BlockSpec((None, FF_SHARD, D_MODEL), lambda j, i, k: (j, 0, 0)), NT)],
        [(dadg, slab), (dadu, slab)], [(shp, slab), (shp, slab)], act_bwd, None)

    grads[tag + "_w_down"] = _matmul(
        tag + "_dwd", (N_CHIPS, nk),
        [(a, pl.BlockSpec((None, bk, FF_SHARD), lambda j, k: (j, k, 0)),
          dh, pl.BlockSpec((bk, D_MODEL), lambda j, k: (k, 0)), TN)],
        [], [(jax.ShapeDtypeStruct((N_CHIPS, FF_SHARD, D_MODEL), BF16),
              pl.BlockSpec((None, FF_SHARD, D_MODEL), lambda j, k: (j, 0, 0)))],
        lambda acc: (0.5 * acc,), (FF_SHARD, D_MODEL))[0]

    def dw_up(nm, dact):
        return _matmul(
            nm, (N_CHIPS, nk),
            [(n, pl.BlockSpec((bk, D_MODEL), lambda j, k: (k, 0)),
              dact, pl.BlockSpec((None, bk, FF_SHARD), lambda j, k: (j, k, 0)), TN)],
            [], [(jax.ShapeDtypeStruct((N_CHIPS, D_MODEL, FF_SHARD), BF16),
                  pl.BlockSpec((None, D_MODEL, FF_SHARD), lambda j, k: (j, 0, 0)))],
            _ident, (D_MODEL, FF_SHARD))[0]

    grads[tag + "_w_gate"] = dw_up(tag + "_dwg", dg)
    grads[tag + "_w_up"] = dw_up(tag + "_dwu", du)

    bn = _row_block(s, 512)
    row = pl.BlockSpec((bn, D_MODEL), lambda i, k: (i, 0))
    terms = []
    for j in range(N_CHIPS):
        a_slab = pl.BlockSpec((None, bn, FF_SHARD), lambda i, k, j=j: (j, i, 0))
        w_slab = pl.BlockSpec((None, D_MODEL, FF_SHARD), lambda i, k, j=j: (j, 0, 0))
        terms += [(dg, a_slab, wg, w_slab, NT), (du, a_slab, wu, w_slab, NT)]
    return _matmul(tag + "_dn", (s // bn, 1), terms, [], [(jax.ShapeDtypeStruct((s, D_MODEL), F32), row)],
                   _ident, None)[0]


def _mm_nn(name, a, b, out_dtype, res=None):
    s, k = a.shape
    nn = b.shape[1]
    bm = _row_block(s)
    row = pl.BlockSpec((bm, nn), lambda i, kk: (i, 0))
    extras = [(res, row)] if res is not None else []
    epi = (lambda acc, r: (r + acc,)) if res is not None else _ident
    return _matmul(
        name, (s // bm, 1),
        [(a, pl.BlockSpec((bm, k), lambda i, kk: (i, 0)), b, pl.BlockSpec((k, nn), lambda i, kk: (0, 0)), NN)],
        extras, [(jax.ShapeDtypeStruct((s, nn), out_dtype), row)], epi, None)[0]


def _mm_nt(name, a, b, out_dtype):
    s, nn = a.shape
    k = b.shape[0]
    bm = _row_block(s)
    return _matmul(
        name, (s // bm, 1),
        [(a, pl.BlockSpec((bm, nn), lambda i, kk: (i, 0)), b, pl.BlockSpec((k, nn), lambda i, kk: (0, 0)), NT)],
        [], [(jax.ShapeDtypeStruct((s, k), out_dtype), pl.BlockSpec((bm, k), lambda i, kk: (i, 0)))], _ident, None)[0]


def _mm_tn(name, a, b, out_dtype=BF16):
    s, k = a.shape
    nn = b.shape[1]
    bk = _reduce_block(s)
    return _matmul(
        name, (s // bk,),
        [(a, pl.BlockSpec((bk, k), lambda kk: (kk, 0)), b, pl.BlockSpec((bk, nn), lambda kk: (kk, 0)), TN)],
        [], [(jax.ShapeDtypeStruct((k, nn), out_dtype), pl.BlockSpec((k, nn), lambda kk: (0, 0)))],
        _ident, (k, nn))[0]


def _mm_heads_fwd(name, a, w, out_dtype):
    s, k = a.shape
    nh, _, nn = w.shape
    bm = _row_block(s)
    return _matmul(
        name, (nh, s // bm, 1),
        [(a, pl.BlockSpec((bm, k), lambda h, i, kk: (i, 0)), w, pl.BlockSpec((None, k, nn), lambda h, i, kk: (h, 0, 0)), NN)],
        [], [(jax.ShapeDtypeStruct((s, nh * nn), out_dtype), pl.BlockSpec((bm, nn), lambda h, i, kk: (i, h)))],
        _ident, None)[0]


def _mm_heads_bwd(name, dy, a, w):
    s, k = a.shape
    nh, _, nn = w.shape
    bm = _row_block(s)
    bk = _reduce_block(s)
    da = _matmul(
        name + "_dx", (s // bm, nh),
        [(dy, pl.BlockSpec((bm, nn), lambda i, h: (i, h)), w, pl.BlockSpec((None, k, nn), lambda i, h: (h, 0, 0)), NT)],
        [], [(jax.ShapeDtypeStruct((s, k), F32), pl.BlockSpec((bm, k), lambda i, h: (i, 0)))], _ident, (bm, k))[0]
    dw = _matmul(
        name + "_dw", (nh, s // bk),
        [(a, pl.BlockSpec((bk, k), lambda h, kk: (kk, 0)), dy, pl.BlockSpec((bk, nn), lambda h, kk: (kk, h)), TN)],
        [], [(jax.ShapeDtypeStruct((nh, k, nn), BF16), pl.BlockSpec((None, k, nn), lambda h, kk: (h, 0, 0)))],
        _ident, (k, nn))[0]
    return da, dw


def _rope_tables(positions):
    half = ROPE_DIM // 2
    freqs = 1.0 / (ROPE_BASE ** (jnp.arange(0, ROPE_DIM, 2, dtype=F32) / ROPE_DIM))
    ang = positions.astype(F32)[:, None] * freqs
    cos, sin = jnp.cos(ang), jnp.sin(ang)
    z = jnp.zeros_like(cos)
    tc = jnp.concatenate([cos, cos, z, z], axis=-1)
    ta = jnp.concatenate([-sin, z, z, z], axis=-1)
    tb = jnp.concatenate([z, sin, z, z], axis=-1)
    assert tc.shape[-1] == 4 * half
    return tc, ta, tb


def _rope(x, tc, ta, tb):
    return x * tc + pltpu.roll(x, 96, 1) * ta + pltpu.roll(x, 32, 1) * tb


def _rope_t(dy, tc, ta, tb):
    return dy * tc + pltpu.roll(dy * ta, 32, 1) + pltpu.roll(dy * tb, 96, 1)


def _q_rope(q, tc, ta, tb, transpose):
    s = q.shape[0]
    bm = _row_block(s, 512)
    rot = _rope_t if transpose else _rope

    def body(q_ref, tc_ref, ta_ref, tb_ref, o_ref):
        c, a, b = tc_ref[...], ta_ref[...], tb_ref[...]
        for h in range(MLA_HEADS):
            lo = h * HEAD_QK
            o_ref[:, lo:lo + 128] = q_ref[:, lo:lo + 128].astype(BF16)
            o_ref[:, lo + 128:lo + 256] = rot(q_ref[:, lo + 128:lo + 256], c, a, b).astype(BF16)

    row = pl.BlockSpec((bm, MLA_HEADS * HEAD_QK), lambda i: (i, 0))
    tab = pl.BlockSpec((bm, 128), lambda i: (i, 0))
    return pl.pallas_call(
        body, name="q_rope_t" if transpose else "q_rope", grid=(s // bm,), in_specs=[row, tab, tab, tab],
        out_specs=row, out_shape=jax.ShapeDtypeStruct((s, MLA_HEADS * HEAD_QK), BF16),
        compiler_params=_params("parallel"),
    )(q, tc, ta, tb)


def _kv_assemble(kv, z, tc, ta, tb):
    s = kv.shape[0]
    bm = _row_block(s, 512)

    def body(kv_ref, kr_ref, tc_ref, ta_ref, tb_ref, k_ref, v_ref):
        kpe = _rope(kr_ref[...], tc_ref[...], ta_ref[...], tb_ref[...]).astype(BF16)
        for h in range(MLA_HEADS):
            lo = h * 256
            k_ref[:, lo:lo + 128] = kv_ref[:, lo:lo + 128].astype(BF16)
            k_ref[:, lo + 128:lo + 256] = kpe
            v_ref[:, h * 128:(h + 1) * 128] = kv_ref[:, lo + 128:lo + 256].astype(BF16)

    row = pl.BlockSpec((bm, 1024), lambda i: (i, 0))
    tab = pl.BlockSpec((bm, 128), lambda i: (i, 0))
    return pl.pallas_call(
        body, name="kv_assemble", grid=(s // bm,),
        in_specs=[row, pl.BlockSpec((bm, 128), lambda i: (i, 3)), tab, tab, tab],
        out_specs=[row, pl.BlockSpec((bm, 512), lambda i: (i, 0))],
        out_shape=[jax.ShapeDtypeStruct((s, 1024), BF16), jax.ShapeDtypeStruct((s, 512), BF16)],
        compiler_params=_params("parallel"),
    )(kv, z, tc, ta, tb)


def _kv_assemble_bwd(dk, dv, tc, ta, tb):
    s = dk.shape[0]
    bm = _row_block(s, 512)

    def body(dk_ref, dv_ref, tc_ref, ta_ref, tb_ref, dkv_ref, dkr_ref):
        dpe = None
        for h in range(MLA_HEADS):
            lo = h * 256
            dkv_ref[:, lo:lo + 128] = dk_ref[:, lo:lo + 128].astype(BF16)
            dkv_ref[:, lo + 128:lo + 256] = dv_ref[:, h * 128:(h + 1) * 128].astype(BF16)
            t = dk_ref[:, lo + 128:lo + 256]
            dpe = t if dpe is None else dpe + t
        dkr_ref[...] = _rope_t(dpe, tc_ref[...], ta_ref[...], tb_ref[...])

    row = pl.BlockSpec((bm, 1024), lambda i: (i, 0))
    tab = pl.BlockSpec((bm, 128), lambda i: (i, 0))
    return pl.pallas_call(
        body, name="kv_assemble_bwd", grid=(s // bm,),
        in_specs=[row, pl.BlockSpec((bm, 512), lambda i: (i, 0)), tab, tab, tab],
        out_specs=[row, tab],
        out_shape=[jax.ShapeDtypeStruct((s, 1024), BF16), jax.ShapeDtypeStruct((s, 128), F32)],
        compiler_params=_params("parallel"),
    )(dk, dv, tc, ta, tb)


def _causal_mask(s, row0, col0):
    rows = row0 + lax.broadcasted_iota(jnp.int32, s.shape, 0)
    cols = col0 + lax.broadcasted_iota(jnp.int32, s.shape, 1)
    return jnp.where(cols <= rows, s, -jnp.inf)


def _attn_fwd(name, q, k, k_off, v, v_off, nh, dq, dv, scale, causal, blk):
    sq, sk = q.shape[0], k.shape[0]
    bq = min(blk, sq)
    bk = min(blk, sk)
    nkv = sk // bk
    assert not causal or (sq == sk and bq == bk)

    def body(q_ref, k_ref, v_ref, o_ref, lse_ref):
        qi = pl.program_id(1)
        qv = q_ref[...]

        def step(j, carry, masked):
            m, l, acc = carry
            rows = pl.ds(pl.multiple_of(j * bk, bk), bk)
            s = _dot(qv, k_ref[rows, :], NT) * scale
            if masked:
                s = _causal_mask(s, qi * bq, j * bk)
            m_new = jnp.maximum(m, jnp.max(s, axis=-1, keepdims=True))
            alpha = jnp.exp(m - m_new)
            p = jnp.exp(s - m_new)
            l = alpha * l + jnp.sum(p, axis=-1, keepdims=True)
            acc = alpha * acc + _dot(p, v_ref[rows, :], NN)
            return m_new, l, acc

        init = (jnp.full((bq, 1), -jnp.inf, F32), jnp.zeros((bq, 1), F32), jnp.zeros((bq, dv), F32))
        if causal:
            carry = lax.fori_loop(0, qi, lambda j, c: step(j, c, False), init)
            m, l, acc = step(qi, carry, True)
        else:
            m, l, acc = lax.fori_loop(0, nkv, lambda j, c: step(j, c, False), init)
        o_ref[...] = (acc / l).astype(o_ref.dtype)
        lse_ref[...] = m + jnp.log(l)

    return _call(
        body, name=name, grid=(nh, sq // bq),
        in_specs=[pl.BlockSpec((bq, dq), lambda h, i: (i, h)),
                  pl.BlockSpec((sk, dq), lambda h, i: (0, k_off + h)),
                  pl.BlockSpec((sk, dv), lambda h, i: (0, v_off + h))],
        out_specs=[pl.BlockSpec((bq, dv), lambda h, i: (i, h)), pl.BlockSpec((None, bq, 1), lambda h, i: (h, i, 0))],
        out_shape=[jax.ShapeDtypeStruct((sq, nh * dv), BF16), jax.ShapeDtypeStruct((nh, sq, 1), F32)],
        sem=("parallel", "parallel"), args=[q, k, v])


def _attn_delta(name, do, do_off, o, nh, dv):
    s = o.shape[0]
    bm = _row_block(s, 512)

    def body(do_ref, o_ref, d_ref):
        d_ref[...] = jnp.sum(do_ref[...].astype(F32) * o_ref[...].astype(F32), axis=-1, keepdims=True)

    return pl.pallas_call(
        body, name=name, grid=(nh, s // bm),
        in_specs=[pl.BlockSpec((bm, dv), lambda h, i: (i, do_off + h)), pl.BlockSpec((bm, dv), lambda h, i: (i, h))],
        out_specs=pl.BlockSpec((None, bm, 1), lambda h, i: (h, i, 0)),
        out_shape=jax.ShapeDtypeStruct((nh, s, 1), F32),
        compiler_params=_params("parallel", "parallel"),
    )(do, o)


def _attn_bwd(name, q, k, k_off, v, v_off, do, do_off, lse, delta, nh, dq, dv, scale, causal, blk):
    sq, sk = q.shape[0], k.shape[0]
    bq = min(blk, sq)
    bk = min(blk, sk)
    nq = sq // bq
    assert not causal or (sq == sk and bq == bk)

    def body(q_ref, k_ref, v_ref, do_ref, lse_ref, dl_ref, dq_ref, dk_ref, dv_ref, dk_acc, dv_acc):
        j = pl.program_id(1)

        @pl.when(j == 0)
        def _():
            dq_ref[...] = jnp.zeros_like(dq_ref)

        dk_acc[...] = jnp.zeros_like(dk_acc)
        dv_acc[...] = jnp.zeros_like(dv_acc)
        kv = k_ref[...]
        vv = v_ref[...]

        def step(i, masked):
            rows = pl.ds(pl.multiple_of(i * bq, bq), bq)
            qv = q_ref[rows, :]
            dov = do_ref[rows, :].astype(BF16)
            s = _dot(qv, kv, NT) * scale
            if masked:
                s = _causal_mask(s, i * bq, j * bk)
            p = jnp.exp(s - lse_ref[rows, :])
            dp = _dot(dov, vv, NT)
            ds = (p * (dp - dl_ref[rows, :]) * scale).astype(BF16)
            dv_acc[...] += _dot(p, dov, TN)
            dk_acc[...] += _dot(ds, qv, TN)
            dq_ref[rows, :] += _dot(ds, kv, NN)

        if causal:
            step(j, True)

            def loop(i, c):
                step(i, False)
                return c

            lax.fori_loop(j + 1, nq, loop, 0)
        else:
            def loop(i, c):
                step(i, False)
                return c

            lax.fori_loop(0, nq, loop, 0)
        dk_ref[...] = dk_acc[...]
        dv_ref[...] = dv_acc[...]

    stat = pl.BlockSpec((None, sq, 1), lambda h, j: (h, 0, 0))
    return _call(
        body, name=name, grid=(nh, sk // bk),
        in_specs=[pl.BlockSpec((sq, dq), lambda h, j: (0, h)),
                  pl.BlockSpec((bk, dq), lambda h, j: (j, k_off + h)),
                  pl.BlockSpec((bk, dv), lambda h, j: (j, v_off + h)),
                  pl.BlockSpec((sq, dv), lambda h, j: (0, do_off + h)), stat, stat],
        out_specs=[pl.BlockSpec((sq, dq), lambda h, j: (0, h)),
                   pl.BlockSpec((bk, dq), lambda h, j: (j, h)),
                   pl.BlockSpec((bk, dv), lambda h, j: (j, h))],
        out_shape=[jax.ShapeDtypeStruct((sq, nh * dq), F32), jax.ShapeDtypeStruct((sk, nh * dq), F32),
                   jax.ShapeDtypeStruct((sk, nh * dv), F32)],
        scratch_shapes=[pltpu.VMEM((bk, dq), F32), pltpu.VMEM((bk, dv), F32)],
        sem=("parallel", "arbitrary"), args=[q, k, v, do, lse, delta])


def _pool_diff(z, g):
    s = z.shape[0]
    t = lax.broadcasted_iota(jnp.int32, z.shape, 0)
    acc = z
    sums = []
    for k in (1, 2, 4, 8):
        acc = acc + jnp.where(t >= k, pltpu.roll(acc, k, 0), 0.0)
        sums.append(acc)
    win = jnp.where(g == 0, sums[0], jnp.where(g == 1, sums[1], jnp.where(g == 2, sums[2], sums[3])))
    w = lax.shift_left(jnp.int32(2), g)
    count = jnp.minimum(t + 1, w).astype(F32)
    del s
    return win / count - z, count


def _pool_fwd(z, pool_w, pool_scale):
    s = z.shape[0]

    def body(z_ref, w_ref, sc_ref, o_ref):
        diff, _ = _pool_diff(z_ref[...], pl.program_id(0))
        o_ref[...] = (_dot(diff, w_ref[...], NN) * sc_ref[...]).astype(o_ref.dtype)

    return pl.pallas_call(
        body, name="pool_fwd", grid=(POOL_GROUPS,),
        in_specs=[pl.BlockSpec((s, POOL_CH), lambda g: (0, 4 + g)),
                  pl.BlockSpec((None, POOL_CH, POOL_CH), lambda g: (g, 0, 0)),
                  pl.BlockSpec((1, POOL_CH), lambda g: (0, g))],
        out_specs=pl.BlockSpec((s, POOL_CH), lambda g: (0, g)),
        out_shape=jax.ShapeDtypeStruct((s, POOL_GROUPS * POOL_CH), BF16),
        compiler_params=_params("parallel"),
    )(z, pool_w, pool_scale)


def _pool_bwd(dcat, z, pool_w, pool_scale):
    s = z.shape[0]

    def body(dp_ref, z_ref, w_ref, sc_ref, dz_ref, dw_ref, dsc_ref):
        g = pl.program_id(0)
        diff, count = _pool_diff(z_ref[...], g)
        dpf = dp_ref[...].astype(F32)
        u = _dot(diff, w_ref[...], NN)
        dsc_ref[...] = jnp.sum(dpf * u, axis=0, keepdims=True)
        du = (dpf * sc_ref[...]).astype(BF16)
        dw_ref[...] = _dot(diff, du, TN)
        ddiff = _dot(du, w_ref[...], NT)
        t = lax.broadcasted_iota(jnp.int32, ddiff.shape, 0)
        acc = ddiff / count
        sums = []
        for k in (1, 2, 4, 8):
            acc = acc + jnp.where(t < s - k, pltpu.roll(acc, s - k, 0), 0.0)
            sums.append(acc)
        win = jnp.where(g == 0, sums[0], jnp.where(g == 1, sums[1], jnp.where(g == 2, sums[2], sums[3])))
        dz_ref[...] = win - ddiff

    return pl.pallas_call(
        body, name="pool_bwd", grid=(POOL_GROUPS,),
        in_specs=[pl.BlockSpec((s, POOL_CH), lambda g: (0, 4 + g)),
                  pl.BlockSpec((s, POOL_CH), lambda g: (0, 4 + g)),
                  pl.BlockSpec((None, POOL_CH, POOL_CH), lambda g: (g, 0, 0)),
                  pl.BlockSpec((1, POOL_CH), lambda g: (0, g))],
        out_specs=[pl.BlockSpec((s, POOL_CH), lambda g: (0, g)),
                   pl.BlockSpec((None, POOL_CH, POOL_CH), lambda g: (g, 0, 0)),
                   pl.BlockSpec((1, POOL_CH), lambda g: (0, g))],
        out_shape=[jax.ShapeDtypeStruct((s, POOL_GROUPS * POOL_CH), F32),
                   jax.ShapeDtypeStruct((POOL_GROUPS, POOL_CH, POOL_CH), F32),
                   jax.ShapeDtypeStruct((1, POOL_GROUPS * POOL_CH), F32)],
        compiler_params=_params("parallel"),
    )(dcat, z, pool_w, pool_scale)


def _local_step(x, mem, positions, target, w, grads):
    tc, ta, tb = _rope_tables(positions)
    blk = _ATT_BLOCK

    n1 = _rmsnorm_fwd("ffn1_norm", x, w["ffn1_norm"], D_MODEL)
    a1, dadu1, dadg1 = _ffn_up("ffn1_up", n1, w["ffn1_w_gate"], w["ffn1_w_up"])
    h1 = _ffn_down("ffn1_down", a1, w["ffn1_w_down"], x)

    n2 = _rmsnorm_fwd("mix_norm", h1, w["mix_norm"], D_MODEL)
    z = _mm_nn("w_in", n2, w["w_in"], F32)
    qn = _rmsnorm_fwd("q_norm", z, w["q_norm"], Q_LORA, 0)
    kvn = _rmsnorm_fwd("kv_norm", z, w["kv_norm"], KV_LORA, 2)
    qp = _mm_heads_fwd("q_up", qn, w["w_q_up"], F32)
    kvp = _mm_heads_fwd("kv_up", kvn, w["w_kv_up"], F32)
    qf = _q_rope(qp, tc, ta, tb, False)
    kf, vf = _kv_assemble(kvp, z, tc, ta, tb)
    att, lse = _attn_fwd("mla_fwd", qf, kf, 0, vf, 0, MLA_HEADS, HEAD_QK, HEAD_V, MLA_SCALE, True, blk)
    pool = _pool_fwd(z, w["pool_w"], w["pool_scale"])
    s = x.shape[0]
    bm = _row_block(s)
    row = pl.BlockSpec((bm, D_MODEL), lambda i, k: (i, 0))
    half = pl.BlockSpec((bm, 512), lambda i, k: (i, 0))
    h2 = _matmul(
        "w_out", (s // bm, 1),
        [(att, half, w["w_out"], pl.BlockSpec((512, D_MODEL), lambda i, k: (0, 0)), NN),
         (pool, half, w["w_out"], pl.BlockSpec((512, D_MODEL), lambda i, k: (1, 0)), NN)],
        [(h1, row)], [(jax.ShapeDtypeStruct((s, D_MODEL), F32), row)], lambda acc, r: (r + acc,), None)[0]

    n3 = _rmsnorm_fwd("xattn_norm", h2, w["xattn_norm"], D_MODEL)
    memn = _rmsnorm_fwd("mem_norm", mem, w["mem_norm"], D_MODEL)
    qm = _mm_nn("w_mq", n3, w["w_mq"], BF16)
    kvm = _mm_heads_fwd("w_mkv", memn, w["w_mkv"], BF16)
    om, lse_m = _attn_fwd("xattn_fwd", qm, kvm, 0, kvm, MEM_HEADS, MEM_HEADS, MEM_HEAD_DIM, MEM_HEAD_DIM,
                          MEM_SCALE, False, blk)
    h3 = _mm_nn("w_mo", om, w["w_mo"], F32, res=h2)

    n4 = _rmsnorm_fwd("ffn2_norm", h3, w["ffn2_norm"], D_MODEL)
    a2, dadu2, dadg2 = _ffn_up("ffn2_up", n4, w["ffn2_w_gate"], w["ffn2_w_up"])
    h4 = _ffn_down("ffn2_down", a2, w["ffn2_w_down"], h3)

    dh4, dh4b, loss_vec, d_final = _loss_and_final_norm(h4, w["final_norm"], target)
    grads["final_norm"] = d_final

    dn4 = _ffn_bwd("ffn2", dh4b, n4, dadg2, dadu2, a2, w["ffn2_w_gate"], w["ffn2_w_up"], w["ffn2_w_down"], grads)
    dh3, grads["ffn2_norm"], dh3b = _rmsnorm_bwd("ffn2_norm_bwd", dn4, h3, w["ffn2_norm"], D_MODEL, dres=dh4)

    dom = _mm_nt("w_mo_dx", dh3b, w["w_mo"], BF16)
    grads["w_mo"] = _mm_tn("w_mo_dw", om, dh3b)
    delta_m = _attn_delta("xattn_delta", dom, 0, om, MEM_HEADS, MEM_HEAD_DIM)
    dqm, dkm, dvm = _attn_bwd("xattn_bwd", qm, kvm, 0, kvm, MEM_HEADS, dom, 0, lse_m, delta_m, MEM_HEADS,
                              MEM_HEAD_DIM, MEM_HEAD_DIM, MEM_SCALE, False, blk)
    dkvm = jnp.concatenate([dkm, dvm], axis=1).astype(BF16)
    dn3 = _mm_nt("w_mq_dx", dqm, w["w_mq"], F32)
    grads["w_mq"] = _mm_tn("w_mq_dw", n3, dqm)
    dmemn, grads["w_mkv"] = _mm_heads_bwd("w_mkv", dkvm, memn, w["w_mkv"])
    _, grads["mem_norm"] = _rmsnorm_bwd("mem_norm_bwd", dmemn, mem, w["mem_norm"], D_MODEL, out_dtype=BF16)
    dh2, grads["xattn_norm"], dh2b = _rmsnorm_bwd("xattn_norm_bwd", dn3, h2, w["xattn_norm"], D_MODEL, dres=dh3)

    dcat = _mm_nt("w_out_dx", dh2b, w["w_out"], BF16)
    grads["w_out"] = jnp.concatenate([_mm_tn("w_out_dw_a", att, dh2b), _mm_tn("w_out_dw_p", pool, dh2b)], axis=0)
    dzp, grads["pool_w"], grads["pool_scale"] = _pool_bwd(dcat, z, w["pool_w"], w["pool_scale"])
    delta = _attn_delta("mla_delta", dcat, 0, att, MLA_HEADS, HEAD_V)
    dqf, dkf, dvf = _attn_bwd("mla_bwd", qf, kf, 0, vf, 0, dcat, 0, lse, delta, MLA_HEADS, HEAD_QK, HEAD_V,
                              MLA_SCALE, True, blk)
    dqp = _q_rope(dqf, tc, ta, tb, True)
    dkvp, dkr = _kv_assemble_bwd(dkf, dvf, tc, ta, tb)
    dqn, grads["w_q_up"] = _mm_heads_bwd("q_up", dqp, qn, w["w_q_up"])
    dkvn, grads["w_kv_up"] = _mm_heads_bwd("kv_up", dkvp, kvn, w["w_kv_up"])
    dcq, grads["q_norm"] = _rmsnorm_bwd("q_norm_bwd", dqn, z, w["q_norm"], Q_LORA, 0, out_dtype=BF16)
    dckv, grads["kv_norm"] = _rmsnorm_bwd("kv_norm_bwd", dkvn, z, w["kv_norm"], KV_LORA, 2, out_dtype=BF16)
    dz = jnp.concatenate([dcq, dckv, dkr.astype(BF16), dzp.astype(BF16)], axis=1)
    dn2 = _mm_nt("w_in_dx", dz, w["w_in"], F32)
    grads["w_in"] = _mm_tn("w_in_dw", n2, dz)
    dh1, grads["mix_norm"], dh1b = _rmsnorm_bwd("mix_norm_bwd", dn2, h1, w["mix_norm"], D_MODEL, dres=dh2)

    dn1 = _ffn_bwd("ffn1", dh1b, n1, dadg1, dadu1, a1, w["ffn1_w_gate"], w["ffn1_w_up"], w["ffn1_w_down"], grads)
    dx, grads["ffn1_norm"], _ = _rmsnorm_bwd("ffn1_norm_bwd", dn1, x, w["ffn1_norm"], D_MODEL, dres=dh1)
    return loss_vec[0, 0], dx


def _mesh_pos():
    x, y, c = lax.axis_index("x"), lax.axis_index("y"), lax.axis_index("c")
    chips = [(1 - x, y), (x, 1 - y), (1 - x, 1 - y)]
    chip_ids = [2 * cx + cy for cx, cy in chips]
    return x, y, c, 2 * x + y, chips, chip_ids


def _half_rows(c, rows):
    hr = rows // 2
    return pl.ds(pl.multiple_of(c * hr, 16), hr), pl.ds(pl.multiple_of((1 - c) * hr, 16), hr)


def _ag_ici_stage(shards):
    n = len(shards)

    def copies(ins, outs):
        x, y, c, me, chips, _ = _mesh_pos()
        out = []
        for k in range(n):
            mine, _ = _half_rows(c, ins[k].shape[0])
            out.append((ins[k], outs[k].at[me], None))
            for cx, cy in chips:
                out.append((ins[k].at[mine], outs[k].at[me, mine], (cx, cy, c)))
        return out

    return _Stage(shards, [jax.ShapeDtypeStruct((N_CHIPS,) + s.shape, s.dtype) for s in shards], 3 * n, n, copies)


def _ag_d2d_stage(fulls):
    n = len(fulls)

    def copies(ins, outs):
        x, y, c, me, _, chip_ids = _mesh_pos()
        out = []
        for k in range(n):
            mine, _ = _half_rows(c, ins[k].shape[1])
            for j in range(3):
                out.append((ins[k].at[chip_ids[j], mine], outs[k].at[chip_ids[j], mine], (x, y, 1 - c)))
        return out

    return _Stage(fulls, [jax.ShapeDtypeStruct(f.shape, f.dtype) for f in fulls], 3 * n, 0, copies,
                  aliases={k: k for k in range(n)})


def _rs_swap_stage(grads):
    n = len(grads)

    def copies(ins, outs):
        x, y, c, _, _, _ = _mesh_pos()
        out = []
        for k in range(n):
            _, other = _half_rows(c, ins[k].shape[1])
            out.append((ins[k].at[:, other, :], outs[k], (x, y, 1 - c)))
        return out

    return _Stage(grads, [jax.ShapeDtypeStruct((N_CHIPS, g.shape[1] // 2, g.shape[2]), g.dtype) for g in grads],
                  n, 0, copies)


def _rs_scatter_stage(sums):
    n = len(sums)

    def copies(ins, outs):
        x, y, c, me, chips, chip_ids = _mesh_pos()
        out = []
        for k in range(n):
            mine, _ = _half_rows(c, 2 * ins[k].shape[1])
            out.append((ins[k].at[me], outs[k].at[0, mine, :], None))
            for j, (cx, cy) in enumerate(chips):
                out.append((ins[k].at[chip_ids[j]], outs[k].at[1 + j, mine, :], (cx, cy, c)))
        return out

    return _Stage(sums, [jax.ShapeDtypeStruct((N_CHIPS, 2 * s.shape[1], s.shape[2]), s.dtype) for s in sums],
                  3 * n, n, copies)


def _rs_mirror_stage(parts):
    n = len(parts)

    def copies(ins, outs):
        x, y, c, _, _, _ = _mesh_pos()
        out = []
        for k in range(n):
            mine, _ = _half_rows(c, ins[k].shape[1])
            out.append((ins[k].at[:, mine, :], outs[k].at[:, mine, :], (x, y, 1 - c)))
        return out

    return _Stage(parts, [jax.ShapeDtypeStruct(p.shape, p.dtype) for p in parts], n, 0, copies,
                  aliases={k: k for k in range(n)})


def _pair_add(name, g, r1, core):
    _, rows, cols = g.shape
    hr = rows // 2

    def body(c_ref, g_ref, r_ref, o_ref):
        o_ref[...] = (g_ref[...].astype(F32) + r_ref[...].astype(F32)).astype(BF16)

    half = pl.BlockSpec((None, hr, cols), lambda j, c: (j, 0, 0))
    return pl.pallas_call(
        body, name=name,
        grid_spec=pltpu.PrefetchScalarGridSpec(
            num_scalar_prefetch=1, grid=(N_CHIPS,),
            in_specs=[pl.BlockSpec((None, hr, cols), lambda j, c: (j, c[0], 0)), half], out_specs=half),
        out_shape=jax.ShapeDtypeStruct((N_CHIPS, hr, cols), BF16),
        compiler_params=_params("parallel"),
    )(core, g, r1)


def _all_gather_weights(shards):
    n = len(shards)

    def body(*refs):
        ins, outs = refs[:n], refs[n:2 * n]
        send, recv, loc = refs[2 * n:]
        x, y, c, me, chips, chip_ids = _mesh_pos()
        sib = (x, y, 1 - c)

        def halves(k):
            hr = ins[k].shape[0] // 2
            return pl.ds(pl.multiple_of(c * hr, 16), hr), pl.ds(pl.multiple_of((1 - c) * hr, 16), hr)

        def remote(src, dst, k, j, dev):
            return pltpu.make_async_remote_copy(src_ref=src, dst_ref=dst, send_sem=send.at[k, j],
                                                recv_sem=recv.at[k, j], device_id=dev, device_id_type=_MESH)

        started = []
        local = []
        for k in range(n):
            mine, _ = halves(k)
            cp = pltpu.make_async_copy(ins[k], outs[k].at[me], loc.at[k])
            cp.start()
            local.append(cp)
            for j, (cx, cy) in enumerate(chips):
                cp = remote(ins[k].at[mine], outs[k].at[me, mine], k, j, (cx, cy, c))
                cp.start()
                started.append(cp)
        for k in range(n):
            mine, _ = halves(k)
            for j in range(3):
                land = outs[k].at[chip_ids[j], mine]
                remote(land, land, k, j, sib).wait_recv()
                cp = remote(land, land, k, 3 + j, sib)
                cp.start()
                started.append(cp)
        for k in range(n):
            _, other = halves(k)
            for j in range(3):
                land = outs[k].at[chip_ids[j], other]
                remote(land, land, k, 3 + j, sib).wait_recv()
        for cp in started:
            cp.wait_send()
        for cp in local:
            cp.wait()

    return pl.pallas_call(
        body, name="all_gather_weights", in_specs=[_ANY] * n, out_specs=[_ANY] * n,
        out_shape=[jax.ShapeDtypeStruct((N_CHIPS,) + s.shape, s.dtype) for s in shards],
        scratch_shapes=[pltpu.SemaphoreType.DMA((n, 6)), pltpu.SemaphoreType.DMA((n, 6)),
                        pltpu.SemaphoreType.DMA((n,))],
        compiler_params=pltpu.CompilerParams(vmem_limit_bytes=V7X_VMEM_LIMIT_BYTES),
    )(*shards)


_RS_CHUNK = 32


def _reduce_scatter(name, grads):
    n = len(grads)

    def body(*refs):
        gs, outs = refs[:n], refs[n:2 * n]
        own, r1, r2, fin = (refs[(2 + i) * n:(3 + i) * n] for i in range(4))
        a_send, a_recv, b_send, b_recv, c_send, c_recv, l_in, l_out = refs[6 * n:]
        x, y, c, me, chips, chip_ids = _mesh_pos()
        sib = (x, y, 1 - c)

        def halves(k):
            hr = gs[k].shape[1] // 2
            return hr, pl.ds(pl.multiple_of(c * hr, 16), hr), pl.ds(pl.multiple_of((1 - c) * hr, 16), hr)

        def remote(src, dst, ssem, rsem, dev):
            return pltpu.make_async_remote_copy(src_ref=src, dst_ref=dst, send_sem=ssem, recv_sem=rsem,
                                                device_id=dev, device_id_type=_MESH)

        sends, locals_in = [], []
        for k in range(n):
            hr, mine, other = halves(k)
            cp = remote(gs[k].at[:, other, :], r1[k], a_send.at[k], a_recv.at[k], sib)
            cp.start()
            sends.append(cp)
            cp = pltpu.make_async_copy(gs[k].at[:, mine, :], own[k], l_in.at[k])
            cp.start()
            locals_in.append(cp)

        for k in range(n):
            hr, mine, other = halves(k)
            locals_in[k].wait()
            remote(r1[k], r1[k], a_send.at[k], a_recv.at[k], sib).wait_recv()
            for j in range(N_CHIPS):
                def add(i, carry, k=k, j=j):
                    rows = pl.ds(pl.multiple_of(i * _RS_CHUNK, _RS_CHUNK), _RS_CHUNK)
                    own[k][j, rows, :] = (own[k][j, rows, :].astype(F32) + r1[k][j, rows, :].astype(F32)).astype(BF16)
                    return carry

                lax.fori_loop(0, hr // _RS_CHUNK, add, 0)
            for j, (cx, cy) in enumerate(chips):
                cp = remote(own[k].at[chip_ids[j]], r2[k].at[j], b_send.at[k, j], b_recv.at[k, j], (cx, cy, c))
                cp.start()
                sends.append(cp)

        locals_out = []
        for k in range(n):
            hr, mine, other = halves(k)
            for j in range(3):
                remote(r2[k].at[j], r2[k].at[j], b_send.at[k, j], b_recv.at[k, j], sib).wait_recv()

            def total(i, carry, k=k):
                rows = pl.ds(pl.multiple_of(i * _RS_CHUNK, _RS_CHUNK), _RS_CHUNK)
                acc = own[k][me, rows, :].astype(F32)
                for j in range(3):
                    acc = acc + r2[k][j, rows, :].astype(F32)
                fin[k][rows, :] = acc
                return carry

            lax.fori_loop(0, hr // _RS_CHUNK, total, 0)
            cp = remote(fin[k], outs[k].at[mine, :], c_send.at[k], c_recv.at[k], sib)
            cp.start()
            sends.append(cp)
            cp = pltpu.make_async_copy(fin[k], outs[k].at[mine, :], l_out.at[k])
            cp.start()
            locals_out.append(cp)

        for k in range(n):
            hr, mine, other = halves(k)
            land = outs[k].at[other, :]
            remote(land, land, c_send.at[k], c_recv.at[k], sib).wait_recv()
        for cp in sends:
            cp.wait_send()
        for cp in locals_out:
            cp.wait()

    scratch = []
    for g in grads:
        scratch.append(pltpu.VMEM((N_CHIPS, g.shape[1] // 2, g.shape[2]), BF16))
    for g in grads:
        scratch.append(pltpu.VMEM((N_CHIPS, g.shape[1] // 2, g.shape[2]), BF16))
    for g in grads:
        scratch.append(pltpu.VMEM((3, g.shape[1] // 2, g.shape[2]), BF16))
    for g in grads:
        scratch.append(pltpu.VMEM((g.shape[1] // 2, g.shape[2]), F32))
    dma = pltpu.SemaphoreType.DMA
    scratch += [dma((n,)), dma((n,)), dma((n, 3)), dma((n, 3)), dma((n,)), dma((n,)), dma((n,)), dma((n,))]
    return pl.pallas_call(
        body, name=name, in_specs=[_ANY] * n, out_specs=[_ANY] * n,
        out_shape=[jax.ShapeDtypeStruct(g.shape[1:], F32) for g in grads],
        scratch_shapes=scratch,
        compiler_params=pltpu.CompilerParams(vmem_limit_bytes=V7X_VMEM_LIMIT_BYTES),
    )(*grads)


def _adamw_math(w, g, m, v):
    m = ADAM_B1 * m + (1.0 - ADAM_B1) * g
    v = ADAM_B2 * v + (1.0 - ADAM_B2) * (g * g)
    m_hat = m / (1.0 - ADAM_B1 ** ADAM_STEP)
    v_hat = v / (1.0 - ADAM_B2 ** ADAM_STEP)
    delta = -ADAM_LR * (m_hat / (jnp.sqrt(v_hat) + ADAM_EPS) + ADAM_WD * w)
    return delta, m, v


def _adamw_sum(name, w, parts, m, v):
    r, c = w.shape
    br = r
    while br * c * 4 > (1 << 20) and br % 32 == 0:
        br //= 2

    def body(w_ref, p_ref, m_ref, v_ref, g_ref, d_ref, nm_ref, nv_ref):
        g = p_ref[0].astype(F32)
        for j in range(1, N_CHIPS):
            g = g + p_ref[j].astype(F32)
        d, nm, nv = _adamw_math(w_ref[...], g, m_ref[...], v_ref[...])
        g_ref[...] = g
        d_ref[...] = d
        nm_ref[...] = nm
        nv_ref[...] = nv

    spec = pl.BlockSpec((br, c), lambda i: (i, 0))
    shp = jax.ShapeDtypeStruct((r, c), F32)
    return _call(
        body, name=name, grid=(r // br,),
        in_specs=[spec, pl.BlockSpec((N_CHIPS, br, c), lambda i: (0, i, 0)), spec, spec],
        out_specs=[spec] * 4, out_shape=[shp] * 4, sem=("parallel",), args=[w, parts, m, v])


_SMALL_ROWS = 80


def _small_allreduce_adamw(gpack, wpack, mpack, vpack):
    def body(g_ref, w_ref, m_ref, v_ref, go_ref, d_ref, nm_ref, nv_ref, buf, send, recv):
        x, y, c = lax.axis_index("x"), lax.axis_index("y"), lax.axis_index("c")
        me = 4 * x + 2 * y + c
        buf[me] = g_ref[...]
        copies = []
        for rel in range(1, 8):
            fx, fy, fc = (rel >> 2) & 1, (rel >> 1) & 1, rel & 1
            dev = ((1 - x) if fx else x, (1 - y) if fy else y, (1 - c) if fc else c)
            cp = pltpu.make_async_remote_copy(src_ref=g_ref, dst_ref=buf.at[me], send_sem=send.at[rel - 1],
                                              recv_sem=recv.at[rel - 1], device_id=dev, device_id_type=_MESH)
            cp.start()
            copies.append(cp)
        for cp in copies:
            cp.wait_recv()
        for cp in copies:
            cp.wait_send()
        total = buf[0]
        for i in range(1, 8):
            total = total + buf[i]
        go_ref[...] = total
        d, nm, nv = _adamw_math(w_ref[...], total, m_ref[...], v_ref[...])
        d_ref[...] = d
        nm_ref[...] = nm
        nv_ref[...] = nv

    vm = pl.BlockSpec(memory_space=pltpu.VMEM)
    shp = jax.ShapeDtypeStruct((_SMALL_ROWS, D_MODEL), F32)
    return pl.pallas_call(
        body, name="small_allreduce_adamw", in_specs=[vm] * 4, out_specs=[vm] * 4, out_shape=[shp] * 4,
        scratch_shapes=[pltpu.VMEM((8, _SMALL_ROWS, D_MODEL), F32), pltpu.SemaphoreType.DMA((7,)),
                        pltpu.SemaphoreType.DMA((7,))],
        compiler_params=pltpu.CompilerParams(vmem_limit_bytes=V7X_VMEM_LIMIT_BYTES),
    )(gpack, wpack, mpack, vpack)


_SMALL_VECTORS = ("ffn1_norm", "mix_norm", "xattn_norm", "mem_norm", "ffn2_norm", "final_norm", "q_norm",
                  "kv_norm", "pool_scale")


def _pack_small(d):
    rows = []
    for n in _SMALL_VECTORS:
        v = d[n].reshape(1, -1).astype(F32)
        rows.append(jnp.pad(v, ((0, 0), (0, D_MODEL - v.shape[1]))))
    rows.append(jnp.zeros((16 - len(_SMALL_VECTORS), D_MODEL), F32))
    rows.append(d["pool_w"].reshape(64, D_MODEL).astype(F32))
    return jnp.concatenate(rows, axis=0)


def _unpack_small(pack, like):
    out = {}
    for i, n in enumerate(_SMALL_VECTORS):
        out[n] = pack[i, :like[n].size].reshape(like[n].shape)
    out["pool_w"] = pack[16:].reshape(like["pool_w"].shape)
    return out


_WEIGHTS = ("ffn1_norm", "ffn1_w_gate", "ffn1_w_up", "ffn1_w_down", "mix_norm", "w_in", "q_norm", "w_q_up",
            "kv_norm", "w_kv_up", "pool_w", "pool_scale", "w_out", "xattn_norm", "mem_norm", "w_mq", "w_mkv",
            "w_mo", "ffn2_norm", "ffn2_w_gate", "ffn2_w_up", "ffn2_w_down", "final_norm")
_SHARDED = ("ffn1_w_gate", "ffn1_w_up", "ffn1_w_down", "w_in", "w_q_up", "w_kv_up", "w_out", "w_mq", "w_mkv",
            "w_mo", "ffn2_w_gate", "ffn2_w_up", "ffn2_w_down")
_RS_GROUPS = (("ffn2_w_gate", "ffn2_w_up", "ffn2_w_down"),
              ("w_mo", "w_mq", "w_mkv", "w_out", "w_q_up", "w_kv_up", "w_in"),
              ("ffn1_w_gate", "ffn1_w_up", "ffn1_w_down"))
W_IN_SPLIT = Q_LORA + KV_LORA + ROPE_DIM


def _pad_shard(name, a):
    if name == "w_in":
        return jnp.concatenate([a[:, :W_IN_SPLIT], jnp.zeros((a.shape[0], 64), a.dtype), a[:, W_IN_SPLIT:]], axis=1)
    if name == "w_q_up":
        return jnp.pad(a, ((0, 0), (0, 64)))
    return a


def _unpad_shard(name, a):
    if name == "w_in":
        return jnp.concatenate([a[..., :W_IN_SPLIT], a[..., W_IN_SPLIT + 64:]], axis=-1)
    if name == "w_q_up":
        return a[..., :192]
    return a


def _stacked(g):
    return g if g.ndim == 3 else g.reshape(N_CHIPS, g.shape[0] // N_CHIPS, g.shape[1])


class _Plan:
    AG_UNITS = (
        (("w_in", "w_q_up", "w_kv_up", "w_out", "w_mq", "w_mkv", "w_mo"), "ffn1_up", "ffn1_down"),
        (("ffn2_w_gate", "ffn2_w_up", "ffn2_w_down"), "mla_fwd", "w_out"),
    )
    RS_UNITS = (
        (("ffn2_w_gate", "ffn2_w_up", "ffn2_w_down"), "ffn2_norm_bwd", "mla_bwd", "q_up_dx"),
        (("w_mo", "w_mq", "w_mkv", "w_out", "w_q_up", "w_kv_up", "w_in"), "mix_norm_bwd", "ffn1_dact", "ffn1_dwd"),
        (("ffn1_w_down",), "ffn1_dwg", "ffn1_dwu", "ffn1_norm_bwd"),
        (("ffn1_w_gate",), "ffn1_dwu", "ffn1_dn", "ffn1_norm_bwd"),
        (("ffn1_w_up",), "ffn1_dn", "ffn1_norm_bwd", "adamw_ffn2_w_gate"),
    )

    def __init__(self, shards, w, grads, core):
        self.shards, self.w, self.grads, self.core = shards, w, grads, core
        self.parts = {}
        self.ag = [[None, None] for _ in self.AG_UNITS]
        self.rs = [[None, None, None, None] for _ in self.RS_UNITS]

    def pre(self, name):
        for i, (names, h1, h2) in enumerate(self.AG_UNITS):
            if name == h1:
                self.ag[i][0] = _host(name, _ag_ici_stage([self.shards[n] for n in names]))
            if name == h2:
                self.ag[i][1] = _host(name, _ag_d2d_stage(self.ag[i][0].results))
        for i, (names, h1, h2, h3) in enumerate(self.RS_UNITS):
            if name == h1:
                self.rs[i][0] = _host(name, _rs_swap_stage([_stacked(self.grads[n]) for n in names]))
            if name == h2:
                self.rs[i][2] = _host(name, _rs_scatter_stage(self.rs[i][1]))
            if name == h3:
                self.rs[i][3] = _host(name, _rs_mirror_stage(self.rs[i][2].results))

    def post(self, name):
        for i, (names, h1, h2) in enumerate(self.AG_UNITS):
            if name == h2:
                for n, f in zip(names, self.ag[i][1].results):
                    self.w[n] = _full_weight(n, f)
        for i, (names, h1, h2, h3) in enumerate(self.RS_UNITS):
            if name == h1:
                self.rs[i][1] = [_pair_add("pair_add_" + n, _stacked(self.grads[n]), r1, self.core)
                                 for n, r1 in zip(names, self.rs[i][0].results)]
            if name == h3:
                for n, p in zip(names, self.rs[i][3].results):
                    self.parts[n] = p


def _full_weight(name, stacked):
    if name in ("w_in", "w_out", "w_mq", "w_mo"):
        return stacked.reshape(D_MODEL, D_MODEL)
    return stacked


def kernel(x, mem, positions, ffn1_norm, ffn1_w_gate, ffn1_w_up, ffn1_w_down, mix_norm, w_in, q_norm, w_q_up, kv_norm, w_kv_up, pool_w, pool_scale, w_out, xattn_norm, mem_norm, w_mq, w_mkv, w_mo, ffn2_norm, ffn2_w_gate, ffn2_w_up, ffn2_w_down, final_norm, loss_target, m_ffn1_norm, m_ffn1_w_gate, m_ffn1_w_up, m_ffn1_w_down, m_mix_norm, m_w_in, m_q_norm, m_w_q_up, m_kv_norm, m_w_kv_up, m_pool_w, m_pool_scale, m_w_out, m_xattn_norm, m_mem_norm, m_w_mq, m_w_mkv, m_w_mo, m_ffn2_norm, m_ffn2_w_gate, m_ffn2_w_up, m_ffn2_w_down, m_final_norm, v_ffn1_norm, v_ffn1_w_gate, v_ffn1_w_up, v_ffn1_w_down, v_mix_norm, v_w_in, v_q_norm, v_w_q_up, v_kv_norm, v_w_kv_up, v_pool_w, v_pool_scale, v_w_out, v_xattn_norm, v_mem_norm, v_w_mq, v_w_mkv, v_w_mo, v_ffn2_norm, v_ffn2_w_gate, v_ffn2_w_up, v_ffn2_w_down, v_final_norm):
    wts = dict(zip(_WEIGHTS, (ffn1_norm, ffn1_w_gate, ffn1_w_up, ffn1_w_down, mix_norm, w_in, q_norm, w_q_up, kv_norm, w_kv_up, pool_w, pool_scale, w_out, xattn_norm, mem_norm, w_mq, w_mkv, w_mo, ffn2_norm, ffn2_w_gate, ffn2_w_up, ffn2_w_down, final_norm)))
    mom = dict(zip(_WEIGHTS, (m_ffn1_norm, m_ffn1_w_gate, m_ffn1_w_up, m_ffn1_w_down, m_mix_norm, m_w_in, m_q_norm, m_w_q_up, m_kv_norm, m_w_kv_up, m_pool_w, m_pool_scale, m_w_out, m_xattn_norm, m_mem_norm, m_w_mq, m_w_mkv, m_w_mo, m_ffn2_norm, m_ffn2_w_gate, m_ffn2_w_up, m_ffn2_w_down, m_final_norm)))
    var = dict(zip(_WEIGHTS, (v_ffn1_norm, v_ffn1_w_gate, v_ffn1_w_up, v_ffn1_w_down, v_mix_norm, v_w_in, v_q_norm, v_w_q_up, v_kv_norm, v_w_kv_up, v_pool_w, v_pool_scale, v_w_out, v_xattn_norm, v_mem_norm, v_w_mq, v_w_mkv, v_w_mo, v_ffn2_norm, v_ffn2_w_gate, v_ffn2_w_up, v_ffn2_w_down, v_final_norm)))
    small = [n for n in _WEIGHTS if n not in _SHARDED]

    global _PLAN
    shards = {n: _pad_shard(n, wts[n][0]).astype(BF16) for n in _SHARDED}
    w = {n: wts[n].reshape(1, -1) for n in _SMALL_VECTORS}
    w["pool_w"] = pool_w[0].astype(BF16)
    grads = {}
    core = lax.axis_index("c").astype(jnp.int32).reshape(1)
    plan = _Plan(shards, w, grads, core)
    _PLAN = plan
    try:
        first = ("ffn1_w_gate", "ffn1_w_up", "ffn1_w_down")
        for n, f in zip(first, _all_gather_weights([shards[n] for n in first])):
            w[n] = f

        loss_local, dx = _local_step(x[0], mem[0], positions[0], loss_target[0], w, grads)
        loss = lax.psum(loss_local, ("x", "y", "c"))

        gpack, dpack, mpack, vpack = _small_allreduce_adamw(
            _pack_small({n: grads[n] for n in small}), _pack_small({n: wts[n] for n in small}),
            _pack_small({n: mom[n] for n in small}), _pack_small({n: var[n] for n in small}))
        like = {n: wts[n] for n in small}
        g_out, d_out, m_out, v_out = (_unpack_small(p, like) for p in (gpack, dpack, mpack, vpack))

        for names in [u[0] for u in _Plan.RS_UNITS]:
            for n in names:
                g, d, nm, nv = _adamw_sum("adamw_" + n, wts[n][0], _unpad_shard(n, plan.parts[n]), mom[n][0],
                                          var[n][0])
                g_out[n], d_out[n], m_out[n], v_out[n] = g[None], d[None], nm[None], nv[None]
    finally:
        _PLAN = None
        _PENDING.clear()

    return (loss, dx[None], *[g_out[n] for n in _WEIGHTS], *[d_out[n] for n in _WEIGHTS],
            *[m_out[n] for n in _WEIGHTS], *[v_out[n] for n in _WEIGHTS])
```

```python
import functools

import jax
import jax.numpy as jnp
from jax import lax
from jax.experimental import pallas as pl
from jax.experimental.pallas import tpu as pltpu

F32 = jnp.float32
BF16 = jnp.bfloat16

D_MODEL = 1024
D_FF = 2816
N_CHIPS = 4
FF_SHARD = D_FF // N_CHIPS
MLA_HEADS = 4
Q_LORA = 256
KV_LORA = 128
ROPE_DIM = 64
HEAD_QK = 256
HEAD_V = 128
POOL_GROUPS = 4
POOL_CH = 128
MEM_HEADS = 4
MEM_HEAD_DIM = 256
RMS_EPS = 1e-6
ROPE_BASE = 10000.0
MLA_SCALE = (128 + 64) ** -0.5
MEM_SCALE = MEM_HEAD_DIM ** -0.5

ADAM_LR = 0.001
ADAM_B1 = 0.9
ADAM_B2 = 0.999
ADAM_EPS = 1e-08
ADAM_WD = 0.01
ADAM_STEP = 10

V7X_VMEM_LIMIT_BYTES = 56 * 1024 * 1024

NN = ((1,), (0,))
NT = ((1,), (1,))
TN = ((0,), (0,))


def _params(*sem):
    return pltpu.CompilerParams(dimension_semantics=sem, vmem_limit_bytes=V7X_VMEM_LIMIT_BYTES)


_MESH = pl.DeviceIdType.MESH
_ANY = pl.BlockSpec(memory_space=pl.ANY)


class _Stage:
    def __init__(self, ins, outs, n_remote, n_local, copies, aliases=None):
        self.ins, self.outs, self.n_remote, self.n_local = list(ins), list(outs), n_remote, n_local
        self.copies, self.aliases = copies, dict(aliases or {})
        self.results = None

    def descriptors(self, in_refs, out_refs, send, recv, loc):
        ds, ri, li = [], 0, 0
        for src, dst, dev in self.copies(in_refs, out_refs):
            if dev is None:
                ds.append(pltpu.make_async_copy(src, dst, loc.at[li]))
                li += 1
            else:
                ds.append(pltpu.make_async_remote_copy(src_ref=src, dst_ref=dst, send_sem=send.at[ri],
                                                       recv_sem=recv.at[ri], device_id=dev, device_id_type=_MESH))
                ri += 1
        assert ri == self.n_remote and li == self.n_local
        return ds


_PENDING = {}


def _host(name, stage):
    _PENDING.setdefault(name, []).append(stage)
    return stage


_PLAN = None


def _call(body, **kw):
    if _PLAN is not None:
        _PLAN.pre(kw["name"])
    res = _call_hosting(body, **kw)
    if _PLAN is not None:
        _PLAN.post(kw["name"])
    return res


def _call_hosting(body, *, name, grid, in_specs, out_specs, out_shape, sem, args, scratch_shapes=(), aliases=None):
    stages = _PENDING.pop(name, [])
    scratch_shapes = list(scratch_shapes)
    if not stages:
        return pl.pallas_call(body, name=name, grid=grid, in_specs=in_specs, out_specs=out_specs,
                              out_shape=out_shape, scratch_shapes=scratch_shapes,
                              input_output_aliases=dict(aliases or {}), compiler_params=_params(*sem))(*args)
    ni, no, ns = len(in_specs), len(out_shape), len(scratch_shapes)
    c_ins = [a for st in stages for a in st.ins]
    c_outs = [o for st in stages for o in st.outs]
    nci, nco = len(c_ins), len(c_outs)
    aliases, io, oo = dict(aliases or {}), 0, 0
    for st in stages:
        for i, j in st.aliases.items():
            aliases[ni + io + i] = no + oo + j
        io += len(st.ins)
        oo += len(st.outs)
    dma = pltpu.SemaphoreType.DMA
    sems = []
    for st in stages:
        sems += [dma((max(st.n_remote, 1),)), dma((max(st.n_remote, 1),)), dma((max(st.n_local, 1),))]

    def wrapped(*refs):
        ins, cin = refs[:ni], refs[ni:ni + nci]
        outs, cout = refs[ni + nci:ni + nci + no], refs[ni + nci + no:ni + nci + no + nco]
        scr = refs[ni + nci + no + nco:ni + nci + no + nco + ns]
        sem_refs = refs[ni + nci + no + nco + ns:]
        first = pl.program_id(0) == 0
        last = pl.program_id(0) == grid[0] - 1
        for ax in range(1, len(grid)):
            first = jnp.logical_and(first, pl.program_id(ax) == 0)
            last = jnp.logical_and(last, pl.program_id(ax) == grid[ax] - 1)

        def descriptors():
            ds, io, oo = [], 0, 0
            for si, st in enumerate(stages):
                ds += st.descriptors(cin[io:io + len(st.ins)], cout[oo:oo + len(st.outs)], *sem_refs[3 * si:3 * si + 3])
                io += len(st.ins)
                oo += len(st.outs)
            return ds

        @pl.when(first)
        def _():
            for d in descriptors():
                d.start()

        body(*ins, *outs, *scr)

        @pl.when(last)
        def _():
            for d in descriptors():
                d.wait()

    res = pl.pallas_call(
        wrapped, name=name, grid=grid, in_specs=list(in_specs) + [_ANY] * nci,
        out_specs=list(out_specs) + [_ANY] * nco, out_shape=list(out_shape) + c_outs,
        scratch_shapes=scratch_shapes + sems, input_output_aliases=aliases,
        compiler_params=_params(*(("arbitrary",) * len(grid))))(*args, *c_ins)
    oo = no
    for st in stages:
        st.results = list(res[oo:oo + len(st.outs)])
        oo += len(st.outs)
    return list(res[:no])


def _dot(a, b, dims):
    return lax.dot_general(a.astype(BF16), b.astype(BF16), (dims, ((), ())), preferred_element_type=F32)


_MAX_ROW_BLOCK = 1024
_ATT_BLOCK = 512


_MAX_REDUCE_BLOCK = 2048


def _row_block(s, want=1024):
    return min(want, s, _MAX_ROW_BLOCK)


def _reduce_block(s):
    return min(s, _MAX_REDUCE_BLOCK)


def _matmul(name, grid, terms, extras, outs, epilogue, acc_shape, fill=None):
    nt, ne, no = len(terms), len(extras), len(outs)
    nk = grid[-1]
    dims = [t[4] for t in terms]

    def body(*refs):
        a_refs, b_refs = refs[:nt], refs[nt:2 * nt]
        e_refs = refs[2 * nt:2 * nt + ne]
        first_out = 2 * nt + ne + (fill is not None)
        o_refs = refs[first_out:first_out + no]

        def finish(acc):
            vals = epilogue(acc, *[e[...] for e in e_refs])
            for o, val in zip(o_refs, vals):
                o[...] = val.astype(o.dtype)

        if nk == 1:
            part = None
            for a, b, d in zip(a_refs, b_refs, dims):
                t = _dot(a[...], b[...], d)
                part = t if part is None else part + t
            finish(part)
        else:
            acc_ref = refs[-1]
            k = pl.program_id(len(grid) - 1)

            @pl.when(k == 0)
            def _():
                acc_ref[...] = jnp.zeros_like(acc_ref)

            for a, b, d in zip(a_refs, b_refs, dims):
                acc_ref[...] += _dot(a[...], b[...], d)

            @pl.when(k == nk - 1)
            def _():
                finish(acc_ref[...])

    in_specs = [t[1] for t in terms] + [t[3] for t in terms] + [e[1] for e in extras]
    args = [t[0] for t in terms] + [t[2] for t in terms] + [e[0] for e in extras]
    sem = ("parallel",) * (len(grid) - 1) + ("arbitrary",)
    aliases = None
    if fill is not None:
        assert ne == 0
        in_specs, args, aliases = in_specs + [_ANY], args + [fill], {len(in_specs): 0}
    return _call(
        body, name=name, grid=grid, in_specs=in_specs,
        out_specs=[o[1] for o in outs], out_shape=[o[0] for o in outs],
        scratch_shapes=[pltpu.VMEM(acc_shape, F32)] if nk > 1 else [], sem=sem, args=args, aliases=aliases)


def _ident(acc):
    return (acc,)


def _rmsnorm_fwd(name, x, gain, width, col_block=0):
    s = x.shape[0]
    bm = _row_block(s)

    def body(x_ref, g_ref, o_ref):
        xf = x_ref[...]
        r = lax.rsqrt(jnp.mean(xf * xf, axis=-1, keepdims=True) + RMS_EPS)
        o_ref[...] = ((xf * r) * g_ref[...]).astype(o_ref.dtype)

    return pl.pallas_call(
        body, name=name, grid=(s // bm,),
        in_specs=[pl.BlockSpec((bm, width), lambda i: (i, col_block)), pl.BlockSpec((1, width), lambda i: (0, 0))],
        out_specs=pl.BlockSpec((bm, width), lambda i: (i, 0)),
        out_shape=jax.ShapeDtypeStruct((s, width), BF16),
        compiler_params=_params("parallel"),
    )(x, gain)


def _rms_bwd_math(dy, xf, g, width):
    r = lax.rsqrt(jnp.mean(xf * xf, axis=-1, keepdims=True) + RMS_EPS)
    dyg = dy * g
    dot = jnp.sum(dyg * xf, axis=-1, keepdims=True)
    dx = r * dyg - xf * ((r * r * r) * (dot * (1.0 / width)))
    dgain = jnp.sum(dy * (xf * r), axis=0, keepdims=True)
    return dx, dgain


def _rmsnorm_bwd(name, dy, x, gain, width, col_block=0, dres=None, out_dtype=F32):
    s = x.shape[0]
    bm = _row_block(s)
    has_res = dres is not None

    def body(*refs):
        if has_res:
            dy_ref, x_ref, g_ref, r_ref, dx_ref, dg_ref, dxb_ref = refs
        else:
            dy_ref, x_ref, g_ref, dx_ref, dg_ref = refs
        dx, dgain = _rms_bwd_math(dy_ref[...].astype(F32), x_ref[...], g_ref[...], width)
        if has_res:
            dx = dx + r_ref[...]
            dxb_ref[...] = dx.astype(BF16)
        dx_ref[...] = dx.astype(dx_ref.dtype)

        @pl.when(pl.program_id(0) == 0)
        def _():
            dg_ref[...] = dgain

        @pl.when(pl.program_id(0) > 0)
        def _():
            dg_ref[...] += dgain

    row = pl.BlockSpec((bm, width), lambda i: (i, 0))
    in_specs = [row, pl.BlockSpec((bm, width), lambda i: (i, col_block)), pl.BlockSpec((1, width), lambda i: (0, 0))]
    args = [dy, x, gain]
    out_specs = [row, pl.BlockSpec((1, width), lambda i: (0, 0))]
    out_shape = [jax.ShapeDtypeStruct((s, width), out_dtype), jax.ShapeDtypeStruct((1, width), F32)]
    if has_res:
        in_specs.append(row)
        args.append(dres)
        out_specs.append(row)
        out_shape.append(jax.ShapeDtypeStruct((s, width), BF16))
    return _call(body, name=name, grid=(s // bm,), in_specs=in_specs, out_specs=out_specs, out_shape=out_shape,
                 sem=("arbitrary",), args=args)


def _loss_and_final_norm(h, gain, target):
    s, d = h.shape
    bm = _row_block(s, 512)

    def body(h_ref, g_ref, t_ref, dh_ref, dhb_ref, loss_ref, dg_ref):
        xf = h_ref[...]
        g = g_ref[...]
        r = lax.rsqrt(jnp.mean(xf * xf, axis=-1, keepdims=True) + RMS_EPS)
        err = (xf * r) * g - t_ref[...]
        part = 0.5 * jnp.sum(jnp.mean(err * err, axis=-1, keepdims=True), axis=0, keepdims=True)
        dx, dgain = _rms_bwd_math(err * (1.0 / d), xf, g, d)
        dh_ref[...] = dx
        dhb_ref[...] = dx.astype(BF16)

        @pl.when(pl.program_id(0) == 0)
        def _():
            dg_ref[...] = dgain
            loss_ref[...] = jnp.broadcast_to(part, loss_ref.shape)

        @pl.when(pl.program_id(0) > 0)
        def _():
            dg_ref[...] += dgain
            loss_ref[...] += jnp.broadcast_to(part, loss_ref.shape)

    row = pl.BlockSpec((bm, d), lambda i: (i, 0))
    vec = pl.BlockSpec((1, d), lambda i: (0, 0))
    return pl.pallas_call(
        body, name="loss_final_norm", grid=(s // bm,), in_specs=[row, vec, row],
        out_specs=[row, row, pl.BlockSpec((1, 128), lambda i: (0, 0)), vec],
        out_shape=[jax.ShapeDtypeStruct((s, d), F32), jax.ShapeDtypeStruct((s, d), BF16),
                   jax.ShapeDtypeStruct((1, 128), F32),
                   jax.ShapeDtypeStruct((1, d), F32)],
        compiler_params=_params("arbitrary"),
    )(h, gain, target)


def _ffn_up(name, n, wg, wu):
    s = n.shape[0]
    bm = _row_block(s)

    def body(n_ref, wg_ref, wu_ref, a_ref, dadu_ref, dadg_ref):
        x = n_ref[...]
        g = _dot(x, wg_ref[...], NT)
        u = _dot(x, wu_ref[...], NT)
        sg = jax.nn.sigmoid(g)
        silu = g * sg
        a_ref[...] = (silu * u).astype(BF16)
        dadu_ref[...] = silu.astype(BF16)
        dadg_ref[...] = (u * (sg * (1.0 + g * (1.0 - sg)))).astype(BF16)

    w_spec = pl.BlockSpec((None, FF_SHARD, D_MODEL), lambda j, i: (j, 0, 0))
    o_spec = pl.BlockSpec((None, bm, FF_SHARD), lambda j, i: (j, i, 0))
    shp = jax.ShapeDtypeStruct((N_CHIPS, s, FF_SHARD), BF16)
    return _call(
        body, name=name, grid=(N_CHIPS, s // bm),
        in_specs=[pl.BlockSpec((bm, D_MODEL), lambda j, i: (i, 0)), w_spec, w_spec],
        out_specs=[o_spec, o_spec, o_spec], out_shape=[shp, shp, shp],
        sem=("parallel", "parallel"), args=[n, wg, wu])


def _ffn_down(name, a, wd, res):
    s = a.shape[1]
    bm = _row_block(s, 512)
    row = pl.BlockSpec((bm, D_MODEL), lambda i, k: (i, 0))
    terms = [(a, pl.BlockSpec((None, bm, FF_SHARD), lambda i, k, j=j: (j, i, 0)),
              wd, pl.BlockSpec((None, FF_SHARD, D_MODEL), lambda i, k, j=j: (j, 0, 0)), NN) for j in range(N_CHIPS)]
    return _matmul(
        name, (s // bm, 1), terms, [(res, row)], [(jax.ShapeDtypeStruct((s, D_MODEL), F32), row)],
        lambda acc, r: (r + 0.5 * acc,), None)[0]


def _ffn_bwd(tag, dh, n, dadg, dadu, a, wg, wu, wd, grads):
    s = dh.shape[0]
    bm = _row_block(s)
    bk = _reduce_block(s)
    nk = s // bk

    def act_bwd(acc, dg_da, du_da):
        da = 0.5 * acc
        return da * dg_da.astype(F32), da * du_da.astype(F32)

    slab = pl.BlockSpec((None, bm, FF_SHARD), lambda j, i, k: (j, i, 0))
    shp = jax.ShapeDtypeStruct((N_CHIPS, s, FF_SHARD), BF16)
    dg, du = _matmul(
        tag + "_dact", (N_CHIPS, s // bm, 1),
        [(dh, pl.BlockSpec((bm, D_MODEL), lambda j, i, k: (i, 0)),
          wd, pl.BlockSpec((None, FF_SHARD, D_MODEL), lambda j, i, k: (j, 0, 0)), NT)],
        [(dadg, slab), (dadu, slab)], [(shp, slab), (shp, slab)], act_bwd, None)

    grads[tag + "_w_down"] = _matmul(
        tag + "_dwd", (N_CHIPS, nk),
        [(a, pl.BlockSpec((None, bk, FF_SHARD), lambda j, k: (j, k, 0)),
          dh, pl.BlockSpec((bk, D_MODEL), lambda j, k: (k, 0)), TN)],
        [], [(jax.ShapeDtypeStruct((N_CHIPS, FF_SHARD, D_MODEL), BF16),
              pl.BlockSpec((None, FF_SHARD, D_MODEL), lambda j, k: (j, 0, 0)))],
        lambda acc: (0.5 * acc,), (FF_SHARD, D_MODEL))[0]

    def dw_up(nm, dact):
        return _matmul(
            nm, (N_CHIPS, nk),
            [(dact, pl.BlockSpec((None, bk, FF_SHARD), lambda j, k: (j, k, 0)),
              n, pl.BlockSpec((bk, D_MODEL), lambda j, k: (k, 0)), TN)],
            [], [(jax.ShapeDtypeStruct((N_CHIPS, FF_SHARD, D_MODEL), BF16),
                  pl.BlockSpec((None, FF_SHARD, D_MODEL), lambda j, k: (j, 0, 0)))],
            _ident, (FF_SHARD, D_MODEL))[0]

    grads[tag + "_w_gate"] = dw_up(tag + "_dwg", dg)
    grads[tag + "_w_up"] = dw_up(tag + "_dwu", du)

    bn = _row_block(s, 512)
    steps = s // bn // 2
    dn = None
    for part, off in (("_dn_a", 0), ("_dn_b", steps)):
        row = pl.BlockSpec((bn, D_MODEL), lambda i, k, off=off: (i + off, 0))
        terms = []
        for j in range(N_CHIPS):
            a_slab = pl.BlockSpec((None, bn, FF_SHARD), lambda i, k, j=j, off=off: (j, i + off, 0))
            w_slab = pl.BlockSpec((None, FF_SHARD, D_MODEL), lambda i, k, j=j: (j, 0, 0))
            terms += [(dg, a_slab, wg, w_slab, NN), (du, a_slab, wu, w_slab, NN)]
        dn = _matmul(tag + part, (steps, 1), terms, [], [(jax.ShapeDtypeStruct((s, D_MODEL), F32), row)],
                     _ident, None, fill=dn)[0]
    return dn


def _mm_nn(name, a, b, out_dtype, res=None):
    s, k = a.shape
    nn = b.shape[1]
    bm = _row_block(s)
    row = pl.BlockSpec((bm, nn), lambda i, kk: (i, 0))
    extras = [(res, row)] if res is not None else []
    epi = (lambda acc, r: (r + acc,)) if res is not None else _ident
    return _matmul(
        name, (s // bm, 1),
        [(a, pl.BlockSpec((bm, k), lambda i, kk: (i, 0)), b, pl.BlockSpec((k, nn), lambda i, kk: (0, 0)), NN)],
        extras, [(jax.ShapeDtypeStruct((s, nn), out_dtype), row)], epi, None)[0]


def _mm_nt(name, a, b, out_dtype):
    s, nn = a.shape
    k = b.shape[0]
    bm = _row_block(s)
    return _matmul(
        name, (s // bm, 1),
        [(a, pl.BlockSpec((bm, nn), lambda i, kk: (i, 0)), b, pl.BlockSpec((k, nn), lambda i, kk: (0, 0)), NT)],
        [], [(jax.ShapeDtypeStruct((s, k), out_dtype), pl.BlockSpec((bm, k), lambda i, kk: (i, 0)))], _ident, None)[0]


def _mm_tn(name, a, b, out_dtype=BF16):
    s, k = a.shape
    nn = b.shape[1]
    bk = _reduce_block(s)
    return _matmul(
        name, (s // bk,),
        [(a, pl.BlockSpec((bk, k), lambda kk: (kk, 0)), b, pl.BlockSpec((bk, nn), lambda kk: (kk, 0)), TN)],
        [], [(jax.ShapeDtypeStruct((k, nn), out_dtype), pl.BlockSpec((k, nn), lambda kk: (0, 0)))],
        _ident, (k, nn))[0]


def _mm_heads_fwd(name, a, w, out_dtype, w_transposed=False):
    s, k = a.shape
    nh = w.shape[0]
    nn = w.shape[1] if w_transposed else w.shape[2]
    bm = _row_block(s)
    return _matmul(
        name, (nh, s // bm, 1),
        [(a, pl.BlockSpec((bm, k), lambda h, i, kk: (i, 0)),
          w, pl.BlockSpec((None,) + w.shape[1:], lambda h, i, kk: (h, 0, 0)), NT if w_transposed else NN)],
        [], [(jax.ShapeDtypeStruct((s, nh * nn), out_dtype), pl.BlockSpec((bm, nn), lambda h, i, kk: (i, h)))],
        _ident, None)[0]


def _mm_heads_bwd(name, dy, a, w, w_transposed=False):
    s, k = a.shape
    nh = w.shape[0]
    nn = w.shape[1] if w_transposed else w.shape[2]
    bm = _row_block(s)
    bk = _reduce_block(s)
    w_spec = pl.BlockSpec((None,) + w.shape[1:], lambda i, h: (h, 0, 0))
    da = _matmul(
        name + "_dx", (s // bm, nh),
        [(dy, pl.BlockSpec((bm, nn), lambda i, h: (i, h)), w, w_spec, NN if w_transposed else NT)],
        [], [(jax.ShapeDtypeStruct((s, k), F32), pl.BlockSpec((bm, k), lambda i, h: (i, 0)))], _ident, (bm, k))[0]
    a_term = (a, pl.BlockSpec((bk, k), lambda h, kk: (kk, 0)))
    dy_term = (dy, pl.BlockSpec((bk, nn), lambda h, kk: (kk, h)))
    lhs, rhs = (dy_term, a_term) if w_transposed else (a_term, dy_term)
    dw = _matmul(
        name + "_dw", (nh, s // bk), [lhs + rhs + (TN,)],
        [], [(jax.ShapeDtypeStruct(w.shape, BF16), pl.BlockSpec((None,) + w.shape[1:], lambda h, kk: (h, 0, 0)))],
        _ident, w.shape[1:])[0]
    return da, dw


def _w_in_fwd(n, w_t):
    s = n.shape[0]
    bm = _row_block(s)
    nh, nout, kin = w_t.shape
    terms = [(n, pl.BlockSpec((bm, kin), lambda i, k, j=j: (i, j)),
              w_t, pl.BlockSpec((None, nout, kin), lambda i, k, j=j: (j, 0, 0)), NT) for j in range(nh)]
    row = pl.BlockSpec((bm, nout), lambda i, k: (i, 0))
    return _matmul("w_in", (s // bm, 1), terms, [], [(jax.ShapeDtypeStruct((s, nout), F32), row)], _ident, None)[0]


def _w_in_dw(dz, n):
    s, nout = dz.shape
    kin = n.shape[1] // N_CHIPS
    bk = _reduce_block(s)
    return _matmul(
        "w_in_dw", (N_CHIPS, s // bk),
        [(dz, pl.BlockSpec((bk, nout), lambda j, k: (k, 0)), n, pl.BlockSpec((bk, kin), lambda j, k: (k, j)), TN)],
        [], [(jax.ShapeDtypeStruct((N_CHIPS, nout, kin), BF16), pl.BlockSpec((None, nout, kin), lambda j, k: (j, 0, 0)))],
        _ident, (nout, kin))[0]


def _rope_tables(positions):
    half = ROPE_DIM // 2
    freqs = 1.0 / (ROPE_BASE ** (jnp.arange(0, ROPE_DIM, 2, dtype=F32) / ROPE_DIM))
    ang = positions.astype(F32)[:, None] * freqs
    cos, sin = jnp.cos(ang), jnp.sin(ang)
    z = jnp.zeros_like(cos)
    tc = jnp.concatenate([cos, cos, z, z], axis=-1)
    ta = jnp.concatenate([-sin, z, z, z], axis=-1)
    tb = jnp.concatenate([z, sin, z, z], axis=-1)
    assert tc.shape[-1] == 4 * half
    return tc, ta, tb


def _rope(x, tc, ta, tb):
    return x * tc + pltpu.roll(x, 96, 1) * ta + pltpu.roll(x, 32, 1) * tb


def _rope_t(dy, tc, ta, tb):
    return dy * tc + pltpu.roll(dy * ta, 32, 1) + pltpu.roll(dy * tb, 96, 1)


def _q_rope(q, tc, ta, tb, transpose):
    s = q.shape[0]
    bm = _row_block(s, 512)
    rot = _rope_t if transpose else _rope

    def body(q_ref, tc_ref, ta_ref, tb_ref, o_ref):
        c, a, b = tc_ref[...], ta_ref[...], tb_ref[...]
        for h in range(MLA_HEADS):
            lo = h * HEAD_QK
            o_ref[:, lo:lo + 128] = q_ref[:, lo:lo + 128].astype(BF16)
            o_ref[:, lo + 128:lo + 256] = rot(q_ref[:, lo + 128:lo + 256], c, a, b).astype(BF16)

    row = pl.BlockSpec((bm, MLA_HEADS * HEAD_QK), lambda i: (i, 0))
    tab = pl.BlockSpec((bm, 128), lambda i: (i, 0))
    return pl.pallas_call(
        body, name="q_rope_t" if transpose else "q_rope", grid=(s // bm,), in_specs=[row, tab, tab, tab],
        out_specs=row, out_shape=jax.ShapeDtypeStruct((s, MLA_HEADS * HEAD_QK), BF16),
        compiler_params=_params("parallel"),
    )(q, tc, ta, tb)


def _kv_assemble(kv, z, tc, ta, tb):
    s = kv.shape[0]
    bm = _row_block(s, 512)

    def body(kv_ref, kr_ref, tc_ref, ta_ref, tb_ref, k_ref, v_ref):
        kpe = _rope(kr_ref[...], tc_ref[...], ta_ref[...], tb_ref[...]).astype(BF16)
        for h in range(MLA_HEADS):
            lo = h * 256
            k_ref[:, lo:lo + 128] = kv_ref[:, lo:lo + 128].astype(BF16)
            k_ref[:, lo + 128:lo + 256] = kpe
            v_ref[:, h * 128:(h + 1) * 128] = kv_ref[:, lo + 128:lo + 256].astype(BF16)

    row = pl.BlockSpec((bm, 1024), lambda i: (i, 0))
    tab = pl.BlockSpec((bm, 128), lambda i: (i, 0))
    return pl.pallas_call(
        body, name="kv_assemble", grid=(s // bm,),
        in_specs=[row, pl.BlockSpec((bm, 128), lambda i: (i, 3)), tab, tab, tab],
        out_specs=[row, pl.BlockSpec((bm, 512), lambda i: (i, 0))],
        out_shape=[jax.ShapeDtypeStruct((s, 1024), BF16), jax.ShapeDtypeStruct((s, 512), BF16)],
        compiler_params=_params("parallel"),
    )(kv, z, tc, ta, tb)


def _kv_assemble_bwd(dk, dv, tc, ta, tb):
    s = dk.shape[0]
    bm = _row_block(s, 512)

    def body(dk_ref, dv_ref, tc_ref, ta_ref, tb_ref, dkv_ref, dkr_ref):
        dpe = None
        for h in range(MLA_HEADS):
            lo = h * 256
            dkv_ref[:, lo:lo + 128] = dk_ref[:, lo:lo + 128].astype(BF16)
            dkv_ref[:, lo + 128:lo + 256] = dv_ref[:, h * 128:(h + 1) * 128].astype(BF16)
            t = dk_ref[:, lo + 128:lo + 256]
            dpe = t if dpe is None else dpe + t
        dkr_ref[...] = _rope_t(dpe, tc_ref[...], ta_ref[...], tb_ref[...])

    row = pl.BlockSpec((bm, 1024), lambda i: (i, 0))
    tab = pl.BlockSpec((bm, 128), lambda i: (i, 0))
    return pl.pallas_call(
        body, name="kv_assemble_bwd", grid=(s // bm,),
        in_specs=[row, pl.BlockSpec((bm, 512), lambda i: (i, 0)), tab, tab, tab],
        out_specs=[row, tab],
        out_shape=[jax.ShapeDtypeStruct((s, 1024), BF16), jax.ShapeDtypeStruct((s, 128), F32)],
        compiler_params=_params("parallel"),
    )(dk, dv, tc, ta, tb)


def _causal_mask(s, row0, col0):
    rows = row0 + lax.broadcasted_iota(jnp.int32, s.shape, 0)
    cols = col0 + lax.broadcasted_iota(jnp.int32, s.shape, 1)
    return jnp.where(cols <= rows, s, -jnp.inf)


def _attn_fwd(name, q, k, k_off, v, v_off, nh, dq, dv, scale, causal, blk):
    sq, sk = q.shape[0], k.shape[0]
    bq = min(blk, sq)
    bk = min(blk, sk)
    nkv = sk // bk
    assert not causal or (sq == sk and bq == bk)

    def body(q_ref, k_ref, v_ref, o_ref, lse_ref):
        qi = pl.program_id(1)
        qv = q_ref[...]

        def step(j, carry, masked):
            m, l, acc = carry
            rows = pl.ds(pl.multiple_of(j * bk, bk), bk)
            s = _dot(qv, k_ref[rows, :], NT) * scale
            if masked:
                s = _causal_mask(s, qi * bq, j * bk)
            m_new = jnp.maximum(m, jnp.max(s, axis=-1, keepdims=True))
            alpha = jnp.exp(m - m_new)
            p = jnp.exp(s - m_new)
            l = alpha * l + jnp.sum(p, axis=-1, keepdims=True)
            acc = alpha * acc + _dot(p, v_ref[rows, :], NN)
            return m_new, l, acc

        init = (jnp.full((bq, 1), -jnp.inf, F32), jnp.zeros((bq, 1), F32), jnp.zeros((bq, dv), F32))
        if causal:
            carry = lax.fori_loop(0, qi, lambda j, c: step(j, c, False), init)
            m, l, acc = step(qi, carry, True)
        else:
            m, l, acc = lax.fori_loop(0, nkv, lambda j, c: step(j, c, False), init)
        o_ref[...] = (acc / l).astype(o_ref.dtype)
        lse_ref[...] = m + jnp.log(l)

    return _call(
        body, name=name, grid=(nh, sq // bq),
        in_specs=[pl.BlockSpec((bq, dq), lambda h, i: (i, h)),
                  pl.BlockSpec((sk, dq), lambda h, i: (0, k_off + h)),
                  pl.BlockSpec((sk, dv), lambda h, i: (0, v_off + h))],
        out_specs=[pl.BlockSpec((bq, dv), lambda h, i: (i, h)), pl.BlockSpec((None, bq, 1), lambda h, i: (h, i, 0))],
        out_shape=[jax.ShapeDtypeStruct((sq, nh * dv), BF16), jax.ShapeDtypeStruct((nh, sq, 1), F32)],
        sem=("parallel", "parallel"), args=[q, k, v])


def _attn_delta(name, do, do_off, o, nh, dv):
    s = o.shape[0]
    bm = _row_block(s, 512)

    def body(do_ref, o_ref, d_ref):
        d_ref[...] = jnp.sum(do_ref[...].astype(F32) * o_ref[...].astype(F32), axis=-1, keepdims=True)

    return pl.pallas_call(
        body, name=name, grid=(nh, s // bm),
        in_specs=[pl.BlockSpec((bm, dv), lambda h, i: (i, do_off + h)), pl.BlockSpec((bm, dv), lambda h, i: (i, h))],
        out_specs=pl.BlockSpec((None, bm, 1), lambda h, i: (h, i, 0)),
        out_shape=jax.ShapeDtypeStruct((nh, s, 1), F32),
        compiler_params=_params("parallel", "parallel"),
    )(do, o)


def _attn_bwd(name, q, k, k_off, v, v_off, do, do_off, lse, delta, nh, dq, dv, scale, causal, blk):
    sq, sk = q.shape[0], k.shape[0]
    bq = min(blk, sq)
    bk = min(blk, sk)
    nq = sq // bq
    assert not causal or (sq == sk and bq == bk)

    def body(q_ref, k_ref, v_ref, do_ref, lse_ref, dl_ref, dq_ref, dk_ref, dv_ref, dk_acc, dv_acc):
        j = pl.program_id(1)

        @pl.when(j == 0)
        def _():
            dq_ref[...] = jnp.zeros_like(dq_ref)

        dk_acc[...] = jnp.zeros_like(dk_acc)
        dv_acc[...] = jnp.zeros_like(dv_acc)
        kv = k_ref[...]
        vv = v_ref[...]

        def step(i, masked):
            rows = pl.ds(pl.multiple_of(i * bq, bq), bq)
            qv = q_ref[rows, :]
            dov = do_ref[rows, :].astype(BF16)
            s = _dot(qv, kv, NT) * scale
            if masked:
                s = _causal_mask(s, i * bq, j * bk)
            p = jnp.exp(s - lse_ref[rows, :])
            dp = _dot(dov, vv, NT)
            ds = (p * (dp - dl_ref[rows, :]) * scale).astype(BF16)
            dv_acc[...] += _dot(p, dov, TN)
            dk_acc[...] += _dot(ds, qv, TN)
            dq_ref[rows, :] += _dot(ds, kv, NN)

        if causal:
            step(j, True)

            def loop(i, c):
                step(i, False)
                return c

            lax.fori_loop(j + 1, nq, loop, 0)
        else:
            def loop(i, c):
                step(i, False)
                return c

            lax.fori_loop(0, nq, loop, 0)
        dk_ref[...] = dk_acc[...]
        dv_ref[...] = dv_acc[...]

    stat = pl.BlockSpec((None, sq, 1), lambda h, j: (h, 0, 0))
    return _call(
        body, name=name, grid=(nh, sk // bk),
        in_specs=[pl.BlockSpec((sq, dq), lambda h, j: (0, h)),
                  pl.BlockSpec((bk, dq), lambda h, j: (j, k_off + h)),
                  pl.BlockSpec((bk, dv), lambda h, j: (j, v_off + h)),
                  pl.BlockSpec((sq, dv), lambda h, j: (0, do_off + h)), stat, stat],
        out_specs=[pl.BlockSpec((sq, dq), lambda h, j: (0, h)),
                   pl.BlockSpec((bk, dq), lambda h, j: (j, h)),
                   pl.BlockSpec((bk, dv), lambda h, j: (j, h))],
        out_shape=[jax.ShapeDtypeStruct((sq, nh * dq), F32), jax.ShapeDtypeStruct((sk, nh * dq), F32),
                   jax.ShapeDtypeStruct((sk, nh * dv), F32)],
        scratch_shapes=[pltpu.VMEM((bk, dq), F32), pltpu.VMEM((bk, dv), F32)],
        sem=("parallel", "arbitrary"), args=[q, k, v, do, lse, delta])


def _pool_diff(z, g):
    s = z.shape[0]
    t = lax.broadcasted_iota(jnp.int32, z.shape, 0)
    acc = z
    sums = []
    for k in (1, 2, 4, 8):
        acc = acc + jnp.where(t >= k, pltpu.roll(acc, k, 0), 0.0)
        sums.append(acc)
    win = jnp.where(g == 0, sums[0], jnp.where(g == 1, sums[1], jnp.where(g == 2, sums[2], sums[3])))
    w = lax.shift_left(jnp.int32(2), g)
    count = jnp.minimum(t + 1, w).astype(F32)
    del s
    return win / count - z, count


def _pool_fwd(z, pool_w, pool_scale):
    s = z.shape[0]

    def body(z_ref, w_ref, sc_ref, o_ref):
        diff, _ = _pool_diff(z_ref[...], pl.program_id(0))
        o_ref[...] = (_dot(diff, w_ref[...], NN) * sc_ref[...]).astype(o_ref.dtype)

    return pl.pallas_call(
        body, name="pool_fwd", grid=(POOL_GROUPS,),
        in_specs=[pl.BlockSpec((s, POOL_CH), lambda g: (0, 4 + g)),
                  pl.BlockSpec((None, POOL_CH, POOL_CH), lambda g: (g, 0, 0)),
                  pl.BlockSpec((1, POOL_CH), lambda g: (0, g))],
        out_specs=pl.BlockSpec((s, POOL_CH), lambda g: (0, g)),
        out_shape=jax.ShapeDtypeStruct((s, POOL_GROUPS * POOL_CH), BF16),
        compiler_params=_params("parallel"),
    )(z, pool_w, pool_scale)


def _pool_bwd(dcat, z, pool_w, pool_scale):
    s = z.shape[0]

    def body(dp_ref, z_ref, w_ref, sc_ref, dz_ref, dw_ref, dsc_ref):
        g = pl.program_id(0)
        diff, count = _pool_diff(z_ref[...], g)
        dpf = dp_ref[...].astype(F32)
        u = _dot(diff, w_ref[...], NN)
        dsc_ref[...] = jnp.sum(dpf * u, axis=0, keepdims=True)
        du = (dpf * sc_ref[...]).astype(BF16)
        dw_ref[...] = _dot(diff, du, TN)
        ddiff = _dot(du, w_ref[...], NT)
        t = lax.broadcasted_iota(jnp.int32, ddiff.shape, 0)
        acc = ddiff / count
        sums = []
        for k in (1, 2, 4, 8):
            acc = acc + jnp.where(t < s - k, pltpu.roll(acc, s - k, 0), 0.0)
            sums.append(acc)
        win = jnp.where(g == 0, sums[0], jnp.where(g == 1, sums[1], jnp.where(g == 2, sums[2], sums[3])))
        dz_ref[...] = win - ddiff

    return pl.pallas_call(
        body, name="pool_bwd", grid=(POOL_GROUPS,),
        in_specs=[pl.BlockSpec((s, POOL_CH), lambda g: (0, 4 + g)),
                  pl.BlockSpec((s, POOL_CH), lambda g: (0, 4 + g)),
                  pl.BlockSpec((None, POOL_CH, POOL_CH), lambda g: (g, 0, 0)),
                  pl.BlockSpec((1, POOL_CH), lambda g: (0, g))],
        out_specs=[pl.BlockSpec((s, POOL_CH), lambda g: (0, g)),
                   pl.BlockSpec((None, POOL_CH, POOL_CH), lambda g: (g, 0, 0)),
                   pl.BlockSpec((1, POOL_CH), lambda g: (0, g))],
        out_shape=[jax.ShapeDtypeStruct((s, POOL_GROUPS * POOL_CH), F32),
                   jax.ShapeDtypeStruct((POOL_GROUPS, POOL_CH, POOL_CH), F32),
                   jax.ShapeDtypeStruct((1, POOL_GROUPS * POOL_CH), F32)],
        compiler_params=_params("parallel"),
    )(dcat, z, pool_w, pool_scale)


def _local_step(x, mem, positions, target, w, grads):
    tc, ta, tb = _rope_tables(positions)
    blk = _ATT_BLOCK

    n1 = _rmsnorm_fwd("ffn1_norm", x, w["ffn1_norm"], D_MODEL)
    a1, dadu1, dadg1 = _ffn_up("ffn1_up", n1, w["ffn1_w_gate"], w["ffn1_w_up"])
    h1 = _ffn_down("ffn1_down", a1, w["ffn1_w_down"], x)

    n2 = _rmsnorm_fwd("mix_norm", h1, w["mix_norm"], D_MODEL)
    z = _w_in_fwd(n2, w["w_in"])
    qn = _rmsnorm_fwd("q_norm", z, w["q_norm"], Q_LORA, 0)
    kvn = _rmsnorm_fwd("kv_norm", z, w["kv_norm"], KV_LORA, 2)
    qp = _mm_heads_fwd("q_up", qn, w["w_q_up"], F32, w_transposed=True)
    kvp = _mm_heads_fwd("kv_up", kvn, w["w_kv_up"], F32)
    qf = _q_rope(qp, tc, ta, tb, False)
    kf, vf = _kv_assemble(kvp, z, tc, ta, tb)
    att, lse = _attn_fwd("mla_fwd", qf, kf, 0, vf, 0, MLA_HEADS, HEAD_QK, HEAD_V, MLA_SCALE, True, blk)
    pool = _pool_fwd(z, w["pool_w"], w["pool_scale"])
    s = x.shape[0]
    bm = _row_block(s)
    row = pl.BlockSpec((bm, D_MODEL), lambda i, k: (i, 0))
    half = pl.BlockSpec((bm, 512), lambda i, k: (i, 0))
    h2 = _matmul(
        "w_out", (s // bm, 1),
        [(att, half, w["w_out"], pl.BlockSpec((512, D_MODEL), lambda i, k: (0, 0)), NN),
         (pool, half, w["w_out"], pl.BlockSpec((512, D_MODEL), lambda i, k: (1, 0)), NN)],
        [(h1, row)], [(jax.ShapeDtypeStruct((s, D_MODEL), F32), row)], lambda acc, r: (r + acc,), None)[0]

    n3 = _rmsnorm_fwd("xattn_norm", h2, w["xattn_norm"], D_MODEL)
    memn = _rmsnorm_fwd("mem_norm", mem, w["mem_norm"], D_MODEL)
    qm = _mm_nn("w_mq", n3, w["w_mq"], BF16)
    kvm = _mm_heads_fwd("w_mkv", memn, w["w_mkv"], BF16)
    om, lse_m = _attn_fwd("xattn_fwd", qm, kvm, 0, kvm, MEM_HEADS, MEM_HEADS, MEM_HEAD_DIM, MEM_HEAD_DIM,
                          MEM_SCALE, False, blk)
    h3 = _mm_nn("w_mo", om, w["w_mo"], F32, res=h2)

    n4 = _rmsnorm_fwd("ffn2_norm", h3, w["ffn2_norm"], D_MODEL)
    a2, dadu2, dadg2 = _ffn_up("ffn2_up", n4, w["ffn2_w_gate"], w["ffn2_w_up"])
    h4 = _ffn_down("ffn2_down", a2, w["ffn2_w_down"], h3)

    dh4, dh4b, loss_vec, d_final = _loss_and_final_norm(h4, w["final_norm"], target)
    grads["final_norm"] = d_final

    dn4 = _ffn_bwd("ffn2", dh4b, n4, dadg2, dadu2, a2, w["ffn2_w_gate"], w["ffn2_w_up"], w["ffn2_w_down"], grads)
    dh3, grads["ffn2_norm"], dh3b = _rmsnorm_bwd("ffn2_norm_bwd", dn4, h3, w["ffn2_norm"], D_MODEL, dres=dh4)

    dom = _mm_nt("w_mo_dx", dh3b, w["w_mo"], BF16)
    grads["w_mo"] = _mm_tn("w_mo_dw", om, dh3b)
    delta_m = _attn_delta("xattn_delta", dom, 0, om, MEM_HEADS, MEM_HEAD_DIM)
    dqm, dkm, dvm = _attn_bwd("xattn_bwd", qm, kvm, 0, kvm, MEM_HEADS, dom, 0, lse_m, delta_m, MEM_HEADS,
                              MEM_HEAD_DIM, MEM_HEAD_DIM, MEM_SCALE, False, blk)
    dkvm = jnp.concatenate([dkm, dvm], axis=1).astype(BF16)
    dn3 = _mm_nt("w_mq_dx", dqm, w["w_mq"], F32)
    grads["w_mq"] = _mm_tn("w_mq_dw", n3, dqm)
    dmemn, grads["w_mkv"] = _mm_heads_bwd("w_mkv", dkvm, memn, w["w_mkv"])
    _, grads["mem_norm"] = _rmsnorm_bwd("mem_norm_bwd", dmemn, mem, w["mem_norm"], D_MODEL, out_dtype=BF16)
    dh2, grads["xattn_norm"], dh2b = _rmsnorm_bwd("xattn_norm_bwd", dn3, h2, w["xattn_norm"], D_MODEL, dres=dh3)

    dcat = _mm_nt("w_out_dx", dh2b, w["w_out"], BF16)
    grads["w_out"] = jnp.concatenate([_mm_tn("w_out_dw_a", att, dh2b), _mm_tn("w_out_dw_p", pool, dh2b)], axis=0)
    dzp, grads["pool_w"], grads["pool_scale"] = _pool_bwd(dcat, z, w["pool_w"], w["pool_scale"])
    delta = _attn_delta("mla_delta", dcat, 0, att, MLA_HEADS, HEAD_V)
    dqf, dkf, dvf = _attn_bwd("mla_bwd", qf, kf, 0, vf, 0, dcat, 0, lse, delta, MLA_HEADS, HEAD_QK, HEAD_V,
                              MLA_SCALE, True, blk)
    dqp = _q_rope(dqf, tc, ta, tb, True)
    dkvp, dkr = _kv_assemble_bwd(dkf, dvf, tc, ta, tb)
    dqn, grads["w_q_up"] = _mm_heads_bwd("q_up", dqp, qn, w["w_q_up"], w_transposed=True)
    dkvn, grads["w_kv_up"] = _mm_heads_bwd("kv_up", dkvp, kvn, w["w_kv_up"])
    dcq, grads["q_norm"] = _rmsnorm_bwd("q_norm_bwd", dqn, z, w["q_norm"], Q_LORA, 0, out_dtype=BF16)
    dckv, grads["kv_norm"] = _rmsnorm_bwd("kv_norm_bwd", dkvn, z, w["kv_norm"], KV_LORA, 2, out_dtype=BF16)
    dz = jnp.concatenate([dcq, dckv, dkr.astype(BF16), dzp.astype(BF16)], axis=1)
    dn2 = _mm_heads_fwd("w_in_dx", dz, w["w_in"], F32)
    grads["w_in"] = _w_in_dw(dz, n2)
    dh1, grads["mix_norm"], dh1b = _rmsnorm_bwd("mix_norm_bwd", dn2, h1, w["mix_norm"], D_MODEL, dres=dh2)

    dn1 = _ffn_bwd("ffn1", dh1b, n1, dadg1, dadu1, a1, w["ffn1_w_gate"], w["ffn1_w_up"], w["ffn1_w_down"], grads)
    dx, grads["ffn1_norm"], _ = _rmsnorm_bwd("ffn1_norm_bwd", dn1, x, w["ffn1_norm"], D_MODEL, dres=dh1)
    return loss_vec[0, 0], dx


def _mesh_pos():
    x, y, c = lax.axis_index("x"), lax.axis_index("y"), lax.axis_index("c")
    chips = [(1 - x, y), (x, 1 - y), (1 - x, 1 - y)]
    chip_ids = [2 * cx + cy for cx, cy in chips]
    return x, y, c, 2 * x + y, chips, chip_ids


def _half_rows(c, rows):
    hr = rows // 2
    return pl.ds(pl.multiple_of(c * hr, 16), hr), pl.ds(pl.multiple_of((1 - c) * hr, 16), hr)


def _ag_ici_stage(shards):
    n = len(shards)

    def copies(ins, outs):
        x, y, c, me, chips, _ = _mesh_pos()
        out = []
        for k in range(n):
            mine, _ = _half_rows(c, ins[k].shape[0])
            out.append((ins[k], outs[k].at[me], None))
            for cx, cy in chips:
                out.append((ins[k].at[mine], outs[k].at[me, mine], (cx, cy, c)))
        return out

    return _Stage(shards, [jax.ShapeDtypeStruct((N_CHIPS,) + s.shape, s.dtype) for s in shards], 3 * n, n, copies)


def _ag_d2d_stage(fulls):
    n = len(fulls)

    def copies(ins, outs):
        x, y, c, me, _, chip_ids = _mesh_pos()
        out = []
        for k in range(n):
            mine, _ = _half_rows(c, ins[k].shape[1])
            for j in range(3):
                out.append((ins[k].at[chip_ids[j], mine], outs[k].at[chip_ids[j], mine], (x, y, 1 - c)))
        return out

    return _Stage(fulls, [jax.ShapeDtypeStruct(f.shape, f.dtype) for f in fulls], 3 * n, 0, copies,
                  aliases={k: k for k in range(n)})


def _rs_swap_stage(grads):
    n = len(grads)

    def copies(ins, outs):
        x, y, c, _, _, _ = _mesh_pos()
        out = []
        for k in range(n):
            _, other = _half_rows(c, ins[k].shape[1])
            out.append((ins[k].at[:, other, :], outs[k], (x, y, 1 - c)))
        return out

    return _Stage(grads, [jax.ShapeDtypeStruct((N_CHIPS, g.shape[1] // 2, g.shape[2]), g.dtype) for g in grads],
                  n, 0, copies)


def _rs_scatter_stage(sums):
    n = len(sums)

    def copies(ins, outs):
        x, y, c, me, chips, chip_ids = _mesh_pos()
        out = []
        for k in range(n):
            mine, _ = _half_rows(c, 2 * ins[k].shape[1])
            out.append((ins[k].at[me], outs[k].at[0, mine, :], None))
            for j, (cx, cy) in enumerate(chips):
                out.append((ins[k].at[chip_ids[j]], outs[k].at[1 + j, mine, :], (cx, cy, c)))
        return out

    return _Stage(sums, [jax.ShapeDtypeStruct((N_CHIPS, 2 * s.shape[1], s.shape[2]), s.dtype) for s in sums],
                  3 * n, n, copies)


def _rs_mirror_stage(parts):
    n = len(parts)

    def copies(ins, outs):
        x, y, c, _, _, _ = _mesh_pos()
        out = []
        for k in range(n):
            mine, _ = _half_rows(c, ins[k].shape[1])
            out.append((ins[k].at[:, mine, :], outs[k].at[:, mine, :], (x, y, 1 - c)))
        return out

    return _Stage(parts, [jax.ShapeDtypeStruct(p.shape, p.dtype) for p in parts], n, 0, copies,
                  aliases={k: k for k in range(n)})


def _pair_add(name, g, r1, core):
    _, rows, cols = g.shape
    hr = rows // 2

    def body(c_ref, g_ref, r_ref, o_ref):
        o_ref[...] = (g_ref[...].astype(F32) + r_ref[...].astype(F32)).astype(BF16)

    half = pl.BlockSpec((None, hr, cols), lambda j, c: (j, 0, 0))
    return pl.pallas_call(
        body, name=name,
        grid_spec=pltpu.PrefetchScalarGridSpec(
            num_scalar_prefetch=1, grid=(N_CHIPS,),
            in_specs=[pl.BlockSpec((None, hr, cols), lambda j, c: (j, c[0], 0)), half], out_specs=half),
        out_shape=jax.ShapeDtypeStruct((N_CHIPS, hr, cols), BF16),
        compiler_params=_params("parallel"),
    )(core, g, r1)


def _all_gather_weights(shards):
    n = len(shards)

    def body(*refs):
        ins, outs = refs[:n], refs[n:2 * n]
        send, recv, loc = refs[2 * n:]
        x, y, c, me, chips, chip_ids = _mesh_pos()
        sib = (x, y, 1 - c)

        def halves(k):
            hr = ins[k].shape[0] // 2
            return pl.ds(pl.multiple_of(c * hr, 16), hr), pl.ds(pl.multiple_of((1 - c) * hr, 16), hr)

        def remote(src, dst, k, j, dev):
            return pltpu.make_async_remote_copy(src_ref=src, dst_ref=dst, send_sem=send.at[k, j],
                                                recv_sem=recv.at[k, j], device_id=dev, device_id_type=_MESH)

        started = []
        local = []
        for k in range(n):
            mine, _ = halves(k)
            cp = pltpu.make_async_copy(ins[k], outs[k].at[me], loc.at[k])
            cp.start()
            local.append(cp)
            for j, (cx, cy) in enumerate(chips):
                cp = remote(ins[k].at[mine], outs[k].at[me, mine], k, j, (cx, cy, c))
                cp.start()
                started.append(cp)
        for k in range(n):
            mine, _ = halves(k)
            for j in range(3):
                land = outs[k].at[chip_ids[j], mine]
                remote(land, land, k, j, sib).wait_recv()
                cp = remote(land, land, k, 3 + j, sib)
                cp.start()
                started.append(cp)
        for k in range(n):
            _, other = halves(k)
            for j in range(3):
                land = outs[k].at[chip_ids[j], other]
                remote(land, land, k, 3 + j, sib).wait_recv()
        for cp in started:
            cp.wait_send()
        for cp in local:
            cp.wait()

    return pl.pallas_call(
        body, name="all_gather_weights", in_specs=[_ANY] * n, out_specs=[_ANY] * n,
        out_shape=[jax.ShapeDtypeStruct((N_CHIPS,) + s.shape, s.dtype) for s in shards],
        scratch_shapes=[pltpu.SemaphoreType.DMA((n, 6)), pltpu.SemaphoreType.DMA((n, 6)),
                        pltpu.SemaphoreType.DMA((n,))],
        compiler_params=pltpu.CompilerParams(vmem_limit_bytes=V7X_VMEM_LIMIT_BYTES),
    )(*shards)


_RS_CHUNK = 32


def _reduce_scatter(name, grads):
    n = len(grads)

    def body(*refs):
        gs, outs = refs[:n], refs[n:2 * n]
        own, r1, r2, fin = (refs[(2 + i) * n:(3 + i) * n] for i in range(4))
        a_send, a_recv, b_send, b_recv, c_send, c_recv, l_in, l_out = refs[6 * n:]
        x, y, c, me, chips, chip_ids = _mesh_pos()
        sib = (x, y, 1 - c)

        def halves(k):
            hr = gs[k].shape[1] // 2
            return hr, pl.ds(pl.multiple_of(c * hr, 16), hr), pl.ds(pl.multiple_of((1 - c) * hr, 16), hr)

        def remote(src, dst, ssem, rsem, dev):
            return pltpu.make_async_remote_copy(src_ref=src, dst_ref=dst, send_sem=ssem, recv_sem=rsem,
                                                device_id=dev, device_id_type=_MESH)

        sends, locals_in = [], []
        for k in range(n):
            hr, mine, other = halves(k)
            cp = remote(gs[k].at[:, other, :], r1[k], a_send.at[k], a_recv.at[k], sib)
            cp.start()
            sends.append(cp)
            cp = pltpu.make_async_copy(gs[k].at[:, mine, :], own[k], l_in.at[k])
            cp.start()
            locals_in.append(cp)

        for k in range(n):
            hr, mine, other = halves(k)
            locals_in[k].wait()
            remote(r1[k], r1[k], a_send.at[k], a_recv.at[k], sib).wait_recv()
            for j in range(N_CHIPS):
                def add(i, carry, k=k, j=j):
                    rows = pl.ds(pl.multiple_of(i * _RS_CHUNK, _RS_CHUNK), _RS_CHUNK)
                    own[k][j, rows, :] = (own[k][j, rows, :].astype(F32) + r1[k][j, rows, :].astype(F32)).astype(BF16)
                    return carry

                lax.fori_loop(0, hr // _RS_CHUNK, add, 0)
            for j, (cx, cy) in enumerate(chips):
                cp = remote(own[k].at[chip_ids[j]], r2[k].at[j], b_send.at[k, j], b_recv.at[k, j], (cx, cy, c))
                cp.start()
                sends.append(cp)

        locals_out = []
        for k in range(n):
            hr, mine, other = halves(k)
            for j in range(3):
                remote(r2[k].at[j], r2[k].at[j], b_send.at[k, j], b_recv.at[k, j], sib).wait_recv()

            def total(i, carry, k=k):
                rows = pl.ds(pl.multiple_of(i * _RS_CHUNK, _RS_CHUNK), _RS_CHUNK)
                acc = own[k][me, rows, :].astype(F32)
                for j in range(3):
                    acc = acc + r2[k][j, rows, :].astype(F32)
                fin[k][rows, :] = acc
                return carry

            lax.fori_loop(0, hr // _RS_CHUNK, total, 0)
            cp = remote(fin[k], outs[k].at[mine, :], c_send.at[k], c_recv.at[k], sib)
            cp.start()
            sends.append(cp)
            cp = pltpu.make_async_copy(fin[k], outs[k].at[mine, :], l_out.at[k])
            cp.start()
            locals_out.append(cp)

        for k in range(n):
            hr, mine, other = halves(k)
            land = outs[k].at[other, :]
            remote(land, land, c_send.at[k], c_recv.at[k], sib).wait_recv()
        for cp in sends:
            cp.wait_send()
        for cp in locals_out:
            cp.wait()

    scratch = []
    for g in grads:
        scratch.append(pltpu.VMEM((N_CHIPS, g.shape[1] // 2, g.shape[2]), BF16))
    for g in grads:
        scratch.append(pltpu.VMEM((N_CHIPS, g.shape[1] // 2, g.shape[2]), BF16))
    for g in grads:
        scratch.append(pltpu.VMEM((3, g.shape[1] // 2, g.shape[2]), BF16))
    for g in grads:
        scratch.append(pltpu.VMEM((g.shape[1] // 2, g.shape[2]), F32))
    dma = pltpu.SemaphoreType.DMA
    scratch += [dma((n,)), dma((n,)), dma((n, 3)), dma((n, 3)), dma((n,)), dma((n,)), dma((n,)), dma((n,))]
    return pl.pallas_call(
        body, name=name, in_specs=[_ANY] * n, out_specs=[_ANY] * n,
        out_shape=[jax.ShapeDtypeStruct(g.shape[1:], F32) for g in grads],
        scratch_shapes=scratch,
        compiler_params=pltpu.CompilerParams(vmem_limit_bytes=V7X_VMEM_LIMIT_BYTES),
    )(*grads)


def _adamw_math(w, g, m, v):
    m = ADAM_B1 * m + (1.0 - ADAM_B1) * g
    v = ADAM_B2 * v + (1.0 - ADAM_B2) * (g * g)
    m_hat = m / (1.0 - ADAM_B1 ** ADAM_STEP)
    v_hat = v / (1.0 - ADAM_B2 ** ADAM_STEP)
    delta = -ADAM_LR * (m_hat / (jnp.sqrt(v_hat) + ADAM_EPS) + ADAM_WD * w)
    return delta, m, v


def _adamw_sum(name, w, parts, m, v):
    r, c = w.shape
    br = r
    while br * c * 4 > (1 << 20) and br % 32 == 0:
        br //= 2

    def body(w_ref, p_ref, m_ref, v_ref, g_ref, d_ref, nm_ref, nv_ref):
        g = p_ref[0].astype(F32)
        for j in range(1, N_CHIPS):
            g = g + p_ref[j].astype(F32)
        d, nm, nv = _adamw_math(w_ref[...], g, m_ref[...], v_ref[...])
        g_ref[...] = g
        d_ref[...] = d
        nm_ref[...] = nm
        nv_ref[...] = nv

    spec = pl.BlockSpec((br, c), lambda i: (i, 0))
    shp = jax.ShapeDtypeStruct((r, c), F32)
    return _call(
        body, name=name, grid=(r // br,),
        in_specs=[spec, pl.BlockSpec((N_CHIPS, br, c), lambda i: (0, i, 0)), spec, spec],
        out_specs=[spec] * 4, out_shape=[shp] * 4, sem=("parallel",), args=[w, parts, m, v])


_SMALL_ROWS = 80


def _small_allreduce_adamw(gpack, wpack, mpack, vpack):
    def body(g_ref, w_ref, m_ref, v_ref, go_ref, d_ref, nm_ref, nv_ref, buf, send, recv):
        x, y, c = lax.axis_index("x"), lax.axis_index("y"), lax.axis_index("c")
        me = 4 * x + 2 * y + c
        buf[me] = g_ref[...]
        copies = []
        for rel in range(1, 8):
            fx, fy, fc = (rel >> 2) & 1, (rel >> 1) & 1, rel & 1
            dev = ((1 - x) if fx else x, (1 - y) if fy else y, (1 - c) if fc else c)
            cp = pltpu.make_async_remote_copy(src_ref=g_ref, dst_ref=buf.at[me], send_sem=send.at[rel - 1],
                                              recv_sem=recv.at[rel - 1], device_id=dev, device_id_type=_MESH)
            cp.start()
            copies.append(cp)
        for cp in copies:
            cp.wait_recv()
        for cp in copies:
            cp.wait_send()
        total = buf[0]
        for i in range(1, 8):
            total = total + buf[i]
        go_ref[...] = total
        d, nm, nv = _adamw_math(w_ref[...], total, m_ref[...], v_ref[...])
        d_ref[...] = d
        nm_ref[...] = nm
        nv_ref[...] = nv

    vm = pl.BlockSpec(memory_space=pltpu.VMEM)
    shp = jax.ShapeDtypeStruct((_SMALL_ROWS, D_MODEL), F32)
    return pl.pallas_call(
        body, name="small_allreduce_adamw", in_specs=[vm] * 4, out_specs=[vm] * 4, out_shape=[shp] * 4,
        scratch_shapes=[pltpu.VMEM((8, _SMALL_ROWS, D_MODEL), F32), pltpu.SemaphoreType.DMA((7,)),
                        pltpu.SemaphoreType.DMA((7,))],
        compiler_params=pltpu.CompilerParams(vmem_limit_bytes=V7X_VMEM_LIMIT_BYTES),
    )(gpack, wpack, mpack, vpack)


_SMALL_VECTORS = ("ffn1_norm", "mix_norm", "xattn_norm", "mem_norm", "ffn2_norm", "final_norm", "q_norm",
                  "kv_norm", "pool_scale")


def _pack_small(d):
    rows = []
    for n in _SMALL_VECTORS:
        v = d[n].reshape(1, -1).astype(F32)
        rows.append(jnp.pad(v, ((0, 0), (0, D_MODEL - v.shape[1]))))
    rows.append(jnp.zeros((16 - len(_SMALL_VECTORS), D_MODEL), F32))
    rows.append(d["pool_w"].reshape(64, D_MODEL).astype(F32))
    return jnp.concatenate(rows, axis=0)


def _unpack_small(pack, like):
    out = {}
    for i, n in enumerate(_SMALL_VECTORS):
        out[n] = pack[i, :like[n].size].reshape(like[n].shape)
    out["pool_w"] = pack[16:].reshape(like["pool_w"].shape)
    return out


_WEIGHTS = ("ffn1_norm", "ffn1_w_gate", "ffn1_w_up", "ffn1_w_down", "mix_norm", "w_in", "q_norm", "w_q_up",
            "kv_norm", "w_kv_up", "pool_w", "pool_scale", "w_out", "xattn_norm", "mem_norm", "w_mq", "w_mkv",
            "w_mo", "ffn2_norm", "ffn2_w_gate", "ffn2_w_up", "ffn2_w_down", "final_norm")
_SHARDED = ("ffn1_w_gate", "ffn1_w_up", "ffn1_w_down", "w_in", "w_q_up", "w_kv_up", "w_out", "w_mq", "w_mkv",
            "w_mo", "ffn2_w_gate", "ffn2_w_up", "ffn2_w_down")
_RS_GROUPS = (("ffn2_w_gate", "ffn2_w_up", "ffn2_w_down"),
              ("w_mo", "w_mq", "w_mkv", "w_out", "w_q_up", "w_kv_up", "w_in"),
              ("ffn1_w_gate", "ffn1_w_up", "ffn1_w_down"))
W_IN_SPLIT = Q_LORA + KV_LORA + ROPE_DIM


_TRANSPOSED = ("ffn1_w_gate", "ffn1_w_up", "ffn2_w_gate", "ffn2_w_up", "w_in", "w_q_up")


def _local_view(name, a):
    return jnp.swapaxes(a, 1, 2)[0] if name in _TRANSPOSED else a[0]


def _global_view(name, a):
    return jnp.swapaxes(a[None], 1, 2) if name in _TRANSPOSED else a[None]


def _pad_shard(name, a):
    if name == "w_in":
        return jnp.concatenate([a[:W_IN_SPLIT], jnp.zeros((64, a.shape[1]), a.dtype), a[W_IN_SPLIT:]], axis=0)
    if name == "w_q_up":
        return jnp.pad(a, ((0, 64), (0, 0)))
    return a


def _unpad_shard(name, a):
    if name == "w_in":
        return jnp.concatenate([a[:, :W_IN_SPLIT], a[:, W_IN_SPLIT + 64:]], axis=1)
    if name == "w_q_up":
        return a[:, :192]
    return a


def _stacked(g):
    return g if g.ndim == 3 else g.reshape(N_CHIPS, g.shape[0] // N_CHIPS, g.shape[1])


class _Plan:
    AG_UNITS = (
        (("w_in", "w_q_up", "w_kv_up", "w_out", "w_mq", "w_mkv", "w_mo"), "ffn1_up", "ffn1_down"),
        (("ffn2_w_gate", "ffn2_w_up", "ffn2_w_down"), "mla_fwd", "w_out"),
    )
    RS_UNITS = (
        (("ffn2_w_gate", "ffn2_w_up", "ffn2_w_down"), "ffn2_norm_bwd", "mla_bwd", "q_up_dx"),
        (("w_mo", "w_mq", "w_mkv", "w_out", "w_q_up", "w_kv_up", "w_in"), "mix_norm_bwd", "ffn1_dact", "ffn1_dwd"),
        (("ffn1_w_down",), "ffn1_dwg", "ffn1_dwu", "ffn1_dn_a"),
        (("ffn1_w_gate",), "ffn1_dwu", "ffn1_dn_a", "ffn1_dn_b"),
        (("ffn1_w_up",), "ffn1_dn_a", "ffn1_dn_b", "adamw_w_kv_up"),
    )
    ADAMW_ORDER = ("w_kv_up", "ffn2_w_gate", "ffn2_w_up", "ffn2_w_down", "w_mo", "w_mq", "w_mkv", "w_out", "w_q_up",
                   "w_in", "ffn1_w_down", "ffn1_w_gate", "ffn1_w_up")

    def __init__(self, shards, w, grads, core):
        self.shards, self.w, self.grads, self.core = shards, w, grads, core
        self.parts = {}
        self.ag = [[None, None] for _ in self.AG_UNITS]
        self.rs = [[None, None, None, None] for _ in self.RS_UNITS]

    def pre(self, name):
        for i, (names, h1, h2) in enumerate(self.AG_UNITS):
            if name == h1:
                self.ag[i][0] = _host(name, _ag_ici_stage([self.shards[n] for n in names]))
            if name == h2:
                self.ag[i][1] = _host(name, _ag_d2d_stage(self.ag[i][0].results))
        for i, (names, h1, h2, h3) in enumerate(self.RS_UNITS):
            if name == h1:
                self.rs[i][0] = _host(name, _rs_swap_stage([_stacked(self.grads[n]) for n in names]))
            if name == h2:
                self.rs[i][2] = _host(name, _rs_scatter_stage(self.rs[i][1]))
            if name == h3:
                self.rs[i][3] = _host(name, _rs_mirror_stage(self.rs[i][2].results))

    def post(self, name):
        for i, (names, h1, h2) in enumerate(self.AG_UNITS):
            if name == h2:
                for n, f in zip(names, self.ag[i][1].results):
                    self.w[n] = _full_weight(n, f)
        for i, (names, h1, h2, h3) in enumerate(self.RS_UNITS):
            if name == h1:
                self.rs[i][1] = [_pair_add("pair_add_" + n, _stacked(self.grads[n]), r1, self.core)
                                 for n, r1 in zip(names, self.rs[i][0].results)]
            if name == h3:
                for n, p in zip(names, self.rs[i][3].results):
                    self.parts[n] = p


def _full_weight(name, stacked):
    if name in ("w_out", "w_mq", "w_mo"):
        return stacked.reshape(D_MODEL, D_MODEL)
    return stacked


def kernel(x, mem, positions, ffn1_norm, ffn1_w_gate, ffn1_w_up, ffn1_w_down, mix_norm, w_in, q_norm, w_q_up, kv_norm, w_kv_up, pool_w, pool_scale, w_out, xattn_norm, mem_norm, w_mq, w_mkv, w_mo, ffn2_norm, ffn2_w_gate, ffn2_w_up, ffn2_w_down, final_norm, loss_target, m_ffn1_norm, m_ffn1_w_gate, m_ffn1_w_up, m_ffn1_w_down, m_mix_norm, m_w_in, m_q_norm, m_w_q_up, m_kv_norm, m_w_kv_up, m_pool_w, m_pool_scale, m_w_out, m_xattn_norm, m_mem_norm, m_w_mq, m_w_mkv, m_w_mo, m_ffn2_norm, m_ffn2_w_gate, m_ffn2_w_up, m_ffn2_w_down, m_final_norm, v_ffn1_norm, v_ffn1_w_gate, v_ffn1_w_up, v_ffn1_w_down, v_mix_norm, v_w_in, v_q_norm, v_w_q_up, v_kv_norm, v_w_kv_up, v_pool_w, v_pool_scale, v_w_out, v_xattn_norm, v_mem_norm, v_w_mq, v_w_mkv, v_w_mo, v_ffn2_norm, v_ffn2_w_gate, v_ffn2_w_up, v_ffn2_w_down, v_final_norm):
    wts = dict(zip(_WEIGHTS, (ffn1_norm, ffn1_w_gate, ffn1_w_up, ffn1_w_down, mix_norm, w_in, q_norm, w_q_up, kv_norm, w_kv_up, pool_w, pool_scale, w_out, xattn_norm, mem_norm, w_mq, w_mkv, w_mo, ffn2_norm, ffn2_w_gate, ffn2_w_up, ffn2_w_down, final_norm)))
    mom = dict(zip(_WEIGHTS, (m_ffn1_norm, m_ffn1_w_gate, m_ffn1_w_up, m_ffn1_w_down, m_mix_norm, m_w_in, m_q_norm, m_w_q_up, m_kv_norm, m_w_kv_up, m_pool_w, m_pool_scale, m_w_out, m_xattn_norm, m_mem_norm, m_w_mq, m_w_mkv, m_w_mo, m_ffn2_norm, m_ffn2_w_gate, m_ffn2_w_up, m_ffn2_w_down, m_final_norm)))
    var = dict(zip(_WEIGHTS, (v_ffn1_norm, v_ffn1_w_gate, v_ffn1_w_up, v_ffn1_w_down, v_mix_norm, v_w_in, v_q_norm, v_w_q_up, v_kv_norm, v_w_kv_up, v_pool_w, v_pool_scale, v_w_out, v_xattn_norm, v_mem_norm, v_w_mq, v_w_mkv, v_w_mo, v_ffn2_norm, v_ffn2_w_gate, v_ffn2_w_up, v_ffn2_w_down, v_final_norm)))
    small = [n for n in _WEIGHTS if n not in _SHARDED]

    global _PLAN
    shards = {n: _pad_shard(n, _local_view(n, wts[n])).astype(BF16) for n in _SHARDED}
    w = {n: wts[n].reshape(1, -1) for n in _SMALL_VECTORS}
    w["pool_w"] = pool_w[0].astype(BF16)
    grads = {}
    core = lax.axis_index("c").astype(jnp.int32).reshape(1)
    plan = _Plan(shards, w, grads, core)
    _PLAN = plan
    try:
        first = ("ffn1_w_gate", "ffn1_w_up", "ffn1_w_down")
        for n, f in zip(first, _all_gather_weights([shards[n] for n in first])):
            w[n] = f

        loss_local, dx = _local_step(x[0], mem[0], positions[0], loss_target[0], w, grads)
        loss = lax.psum(loss_local, ("x", "y", "c"))

        gpack, dpack, mpack, vpack = _small_allreduce_adamw(
            _pack_small({n: grads[n] for n in small}), _pack_small({n: wts[n] for n in small}),
            _pack_small({n: mom[n] for n in small}), _pack_small({n: var[n] for n in small}))
        like = {n: wts[n] for n in small}
        g_out, d_out, m_out, v_out = (_unpack_small(p, like) for p in (gpack, dpack, mpack, vpack))

        for n in _Plan.ADAMW_ORDER:
            res = _adamw_sum("adamw_" + n, _local_view(n, wts[n]), _unpad_shard(n, plan.parts[n]),
                             _local_view(n, mom[n]), _local_view(n, var[n]))
            g_out[n], d_out[n], m_out[n], v_out[n] = (_global_view(n, r) for r in res)
    finally:
        _PLAN = None
        _PENDING.clear()

    return (loss, dx[None], *[g_out[n] for n in _WEIGHTS], *[d_out[n] for n in _WEIGHTS],
            *[m_out[n] for n in _WEIGHTS], *[v_out[n] for n in _WEIGHTS])
```

```python
import functools

import jax
import jax.numpy as jnp
from jax import lax
from jax.experimental import pallas as pl
from jax.experimental.pallas import tpu as pltpu

F32 = jnp.float32
BF16 = jnp.bfloat16

D_MODEL = 1024
D_FF = 2816
N_CHIPS = 4
FF_SHARD = D_FF // N_CHIPS
MLA_HEADS = 4
Q_LORA = 256
KV_LORA = 128
ROPE_DIM = 64
HEAD_QK = 256
HEAD_V = 128
POOL_GROUPS = 4
POOL_CH = 128
MEM_HEADS = 4
MEM_HEAD_DIM = 256
RMS_EPS = 1e-6
ROPE_BASE = 10000.0
MLA_SCALE = (128 + 64) ** -0.5
MEM_SCALE = MEM_HEAD_DIM ** -0.5

ADAM_LR = 0.001
ADAM_B1 = 0.9
ADAM_B2 = 0.999
ADAM_EPS = 1e-08
ADAM_WD = 0.01
ADAM_STEP = 10

V7X_VMEM_LIMIT_BYTES = 56 * 1024 * 1024

NN = ((1,), (0,))
NT = ((1,), (1,))
TN = ((0,), (0,))


def _params(*sem):
    return pltpu.CompilerParams(dimension_semantics=sem, vmem_limit_bytes=V7X_VMEM_LIMIT_BYTES)


_MESH = pl.DeviceIdType.MESH
_ANY = pl.BlockSpec(memory_space=pl.ANY)


class _Stage:
    def __init__(self, ins, outs, n_remote, n_local, copies, aliases=None):
        self.ins, self.outs, self.n_remote, self.n_local = list(ins), list(outs), n_remote, n_local
        self.copies, self.aliases = copies, dict(aliases or {})
        self.results = None

    def descriptors(self, in_refs, out_refs, send, recv, loc):
        ds, ri, li = [], 0, 0
        for src, dst, dev in self.copies(in_refs, out_refs):
            if dev is None:
                ds.append(pltpu.make_async_copy(src, dst, loc.at[li]))
                li += 1
            else:
                ds.append(pltpu.make_async_remote_copy(src_ref=src, dst_ref=dst, send_sem=send.at[ri],
                                                       recv_sem=recv.at[ri], device_id=dev, device_id_type=_MESH))
                ri += 1
        assert ri == self.n_remote and li == self.n_local
        return ds


_PENDING = {}


def _host(name, stage):
    _PENDING.setdefault(name, []).append(stage)
    return stage


_PLAN = None


def _call(body, **kw):
    if _PLAN is not None:
        _PLAN.pre(kw["name"])
    res = _call_hosting(body, **kw)
    if _PLAN is not None:
        _PLAN.post(kw["name"])
    return res


def _call_hosting(body, *, name, grid, in_specs, out_specs, out_shape, sem, args, scratch_shapes=(), aliases=None):
    stages = _PENDING.pop(name, [])
    scratch_shapes = list(scratch_shapes)
    if not stages:
        return pl.pallas_call(body, name=name, grid=grid, in_specs=in_specs, out_specs=out_specs,
                              out_shape=out_shape, scratch_shapes=scratch_shapes,
                              input_output_aliases=dict(aliases or {}), compiler_params=_params(*sem))(*args)
    ni, no, ns = len(in_specs), len(out_shape), len(scratch_shapes)
    c_ins = [a for st in stages for a in st.ins]
    c_outs = [o for st in stages for o in st.outs]
    nci, nco = len(c_ins), len(c_outs)
    aliases, io, oo = dict(aliases or {}), 0, 0
    for st in stages:
        for i, j in st.aliases.items():
            aliases[ni + io + i] = no + oo + j
        io += len(st.ins)
        oo += len(st.outs)
    dma = pltpu.SemaphoreType.DMA
    sems = []
    for st in stages:
        sems += [dma((max(st.n_remote, 1),)), dma((max(st.n_remote, 1),)), dma((max(st.n_local, 1),))]

    def wrapped(*refs):
        ins, cin = refs[:ni], refs[ni:ni + nci]
        outs, cout = refs[ni + nci:ni + nci + no], refs[ni + nci + no:ni + nci + no + nco]
        scr = refs[ni + nci + no + nco:ni + nci + no + nco + ns]
        sem_refs = refs[ni + nci + no + nco + ns:]
        first = pl.program_id(0) == 0
        last = pl.program_id(0) == grid[0] - 1
        for ax in range(1, len(grid)):
            first = jnp.logical_and(first, pl.program_id(ax) == 0)
            last = jnp.logical_and(last, pl.program_id(ax) == grid[ax] - 1)

        def descriptors():
            ds, io, oo = [], 0, 0
            for si, st in enumerate(stages):
                ds += st.descriptors(cin[io:io + len(st.ins)], cout[oo:oo + len(st.outs)], *sem_refs[3 * si:3 * si + 3])
                io += len(st.ins)
                oo += len(st.outs)
            return ds

        @pl.when(first)
        def _():
            for d in descriptors():
                d.start()

        body(*ins, *outs, *scr)

        @pl.when(last)
        def _():
            for d in descriptors():
                d.wait()

    res = pl.pallas_call(
        wrapped, name=name, grid=grid, in_specs=list(in_specs) + [_ANY] * nci,
        out_specs=list(out_specs) + [_ANY] * nco, out_shape=list(out_shape) + c_outs,
        scratch_shapes=scratch_shapes + sems, input_output_aliases=aliases,
        compiler_params=_params(*(("arbitrary",) * len(grid))))(*args, *c_ins)
    oo = no
    for st in stages:
        st.results = list(res[oo:oo + len(st.outs)])
        oo += len(st.outs)
    return list(res[:no])


def _dot(a, b, dims):
    return lax.dot_general(a.astype(BF16), b.astype(BF16), (dims, ((), ())), preferred_element_type=F32)


_MAX_ROW_BLOCK = 1024
_ATT_BLOCK = 512


_MAX_REDUCE_BLOCK = 2048


def _row_block(s, want=1024):
    return min(want, s, _MAX_ROW_BLOCK)


def _reduce_block(s):
    return min(s, _MAX_REDUCE_BLOCK)


def _matmul(name, grid, terms, extras, outs, epilogue, acc_shape, fill=None):
    nt, ne, no = len(terms), len(extras), len(outs)
    nk = grid[-1]
    dims = [t[4] for t in terms]

    def body(*refs):
        a_refs, b_refs = refs[:nt], refs[nt:2 * nt]
        e_refs = refs[2 * nt:2 * nt + ne]
        first_out = 2 * nt + ne + (fill is not None)
        o_refs = refs[first_out:first_out + no]

        def finish(acc):
            vals = epilogue(acc, *[e[...] for e in e_refs])
            for o, val in zip(o_refs, vals):
                o[...] = val.astype(o.dtype)

        if nk == 1:
            part = None
            for a, b, d in zip(a_refs, b_refs, dims):
                t = _dot(a[...], b[...], d)
                part = t if part is None else part + t
            finish(part)
        else:
            acc_ref = refs[-1]
            k = pl.program_id(len(grid) - 1)

            @pl.when(k == 0)
            def _():
                acc_ref[...] = jnp.zeros_like(acc_ref)

            for a, b, d in zip(a_refs, b_refs, dims):
                acc_ref[...] += _dot(a[...], b[...], d)

            @pl.when(k == nk - 1)
            def _():
                finish(acc_ref[...])

    in_specs = [t[1] for t in terms] + [t[3] for t in terms] + [e[1] for e in extras]
    args = [t[0] for t in terms] + [t[2] for t in terms] + [e[0] for e in extras]
    sem = ("parallel",) * (len(grid) - 1) + ("arbitrary",)
    aliases = None
    if fill is not None:
        assert ne == 0
        in_specs, args, aliases = in_specs + [_ANY], args + [fill], {len(in_specs): 0}
    return _call(
        body, name=name, grid=grid, in_specs=in_specs,
        out_specs=[o[1] for o in outs], out_shape=[o[0] for o in outs],
        scratch_shapes=[pltpu.VMEM(acc_shape, F32)] if nk > 1 else [], sem=sem, args=args, aliases=aliases)


def _ident(acc):
    return (acc,)


def _rmsnorm_fwd(name, x, gain, width, col_block=0):
    s = x.shape[0]
    bm = _row_block(s)

    def body(x_ref, g_ref, o_ref):
        xf = x_ref[...]
        r = lax.rsqrt(jnp.mean(xf * xf, axis=-1, keepdims=True) + RMS_EPS)
        o_ref[...] = ((xf * r) * g_ref[...]).astype(o_ref.dtype)

    return pl.pallas_call(
        body, name=name, grid=(s // bm,),
        in_specs=[pl.BlockSpec((bm, width), lambda i: (i, col_block)), pl.BlockSpec((1, width), lambda i: (0, 0))],
        out_specs=pl.BlockSpec((bm, width), lambda i: (i, 0)),
        out_shape=jax.ShapeDtypeStruct((s, width), BF16),
        compiler_params=_params("parallel"),
    )(x, gain)


def _rms_bwd_math(dy, xf, g, width):
    r = lax.rsqrt(jnp.mean(xf * xf, axis=-1, keepdims=True) + RMS_EPS)
    dyg = dy * g
    dot = jnp.sum(dyg * xf, axis=-1, keepdims=True)
    dx = r * dyg - xf * ((r * r * r) * (dot * (1.0 / width)))
    dgain = jnp.sum(dy * (xf * r), axis=0, keepdims=True)
    return dx, dgain


def _rmsnorm_bwd(name, dy, x, gain, width, col_block=0, dres=None, out_dtype=F32):
    s = x.shape[0]
    bm = _row_block(s)
    has_res = dres is not None

    def body(*refs):
        if has_res:
            dy_ref, x_ref, g_ref, r_ref, dx_ref, dg_ref, dxb_ref = refs
        else:
            dy_ref, x_ref, g_ref, dx_ref, dg_ref = refs
        dx, dgain = _rms_bwd_math(dy_ref[...].astype(F32), x_ref[...], g_ref[...], width)
        if has_res:
            dx = dx + r_ref[...]
            dxb_ref[...] = dx.astype(BF16)
        dx_ref[...] = dx.astype(dx_ref.dtype)

        @pl.when(pl.program_id(0) == 0)
        def _():
            dg_ref[...] = dgain

        @pl.when(pl.program_id(0) > 0)
        def _():
            dg_ref[...] += dgain

    row = pl.BlockSpec((bm, width), lambda i: (i, 0))
    in_specs = [row, pl.BlockSpec((bm, width), lambda i: (i, col_block)), pl.BlockSpec((1, width), lambda i: (0, 0))]
    args = [dy, x, gain]
    out_specs = [row, pl.BlockSpec((1, width), lambda i: (0, 0))]
    out_shape = [jax.ShapeDtypeStruct((s, width), out_dtype), jax.ShapeDtypeStruct((1, width), F32)]
    if has_res:
        in_specs.append(row)
        args.append(dres)
        out_specs.append(row)
        out_shape.append(jax.ShapeDtypeStruct((s, width), BF16))
    return _call(body, name=name, grid=(s // bm,), in_specs=in_specs, out_specs=out_specs, out_shape=out_shape,
                 sem=("arbitrary",), args=args)


def _loss_and_final_norm(h, gain, target):
    s, d = h.shape
    bm = _row_block(s, 512)

    def body(h_ref, g_ref, t_ref, dh_ref, dhb_ref, loss_ref, dg_ref):
        xf = h_ref[...]
        g = g_ref[...]
        r = lax.rsqrt(jnp.mean(xf * xf, axis=-1, keepdims=True) + RMS_EPS)
        err = (xf * r) * g - t_ref[...]
        part = 0.5 * jnp.sum(jnp.mean(err * err, axis=-1, keepdims=True), axis=0, keepdims=True)
        dx, dgain = _rms_bwd_math(err * (1.0 / d), xf, g, d)
        dh_ref[...] = dx
        dhb_ref[...] = dx.astype(BF16)

        @pl.when(pl.program_id(0) == 0)
        def _():
            dg_ref[...] = dgain
            loss_ref[...] = jnp.broadcast_to(part, loss_ref.shape)

        @pl.when(pl.program_id(0) > 0)
        def _():
            dg_ref[...] += dgain
            loss_ref[...] += jnp.broadcast_to(part, loss_ref.shape)

    row = pl.BlockSpec((bm, d), lambda i: (i, 0))
    vec = pl.BlockSpec((1, d), lambda i: (0, 0))
    return pl.pallas_call(
        body, name="loss_final_norm", grid=(s // bm,), in_specs=[row, vec, row],
        out_specs=[row, row, pl.BlockSpec((1, 128), lambda i: (0, 0)), vec],
        out_shape=[jax.ShapeDtypeStruct((s, d), F32), jax.ShapeDtypeStruct((s, d), BF16),
                   jax.ShapeDtypeStruct((1, 128), F32),
                   jax.ShapeDtypeStruct((1, d), F32)],
        compiler_params=_params("arbitrary"),
    )(h, gain, target)


def _ffn_up(name, n, wg, wu):
    s = n.shape[0]
    bm = _row_block(s)

    def body(n_ref, wg_ref, wu_ref, a_ref, dadu_ref, dadg_ref):
        x = n_ref[...]
        g = _dot(x, wg_ref[...], NT)
        u = _dot(x, wu_ref[...], NT)
        sg = jax.nn.sigmoid(g)
        silu = g * sg
        a_ref[...] = (silu * u).astype(BF16)
        dadu_ref[...] = silu.astype(BF16)
        dadg_ref[...] = (u * (sg * (1.0 + g * (1.0 - sg)))).astype(BF16)

    w_spec = pl.BlockSpec((None, FF_SHARD, D_MODEL), lambda j, i: (j, 0, 0))
    o_spec = pl.BlockSpec((None, bm, FF_SHARD), lambda j, i: (j, i, 0))
    shp = jax.ShapeDtypeStruct((N_CHIPS, s, FF_SHARD), BF16)
    return _call(
        body, name=name, grid=(N_CHIPS, s // bm),
        in_specs=[pl.BlockSpec((bm, D_MODEL), lambda j, i: (i, 0)), w_spec, w_spec],
        out_specs=[o_spec, o_spec, o_spec], out_shape=[shp, shp, shp],
        sem=("parallel", "parallel"), args=[n, wg, wu])


def _residual_epilogue(alpha, with_norm):
    if not with_norm:
        return lambda acc, r: (r + alpha * acc,)

    def epilogue(acc, r, g):
        h = r + alpha * acc
        rs = lax.rsqrt(jnp.mean(h * h, axis=-1, keepdims=True) + RMS_EPS)
        return h, (h * rs) * g

    return epilogue


def _residual_outs(s, bm, gain):
    row = pl.BlockSpec((bm, D_MODEL), lambda i, k: (i, 0))
    outs = [(jax.ShapeDtypeStruct((s, D_MODEL), F32), row)]
    if gain is None:
        return [], outs
    return [(gain, pl.BlockSpec((1, D_MODEL), lambda i, k: (0, 0)))], outs + [(jax.ShapeDtypeStruct((s, D_MODEL), BF16), row)]


def _ffn_down(name, a, wd, res, gain=None):
    s = a.shape[1]
    bm = _row_block(s, 512)
    row = pl.BlockSpec((bm, D_MODEL), lambda i, k: (i, 0))
    terms = [(a, pl.BlockSpec((None, bm, FF_SHARD), lambda i, k, j=j: (j, i, 0)),
              wd, pl.BlockSpec((None, FF_SHARD, D_MODEL), lambda i, k, j=j: (j, 0, 0)), NN) for j in range(N_CHIPS)]
    extras, outs = _residual_outs(s, bm, gain)
    res_out = _matmul(name, (s // bm, 1), terms, [(res, row)] + extras, outs,
                      _residual_epilogue(0.5, gain is not None), None)
    return res_out if gain is not None else res_out[0]


def _ffn_bwd(tag, dh, n, dadg, dadu, a, wg, wu, wd, grads):
    s = dh.shape[0]
    bm = _row_block(s)
    bk = _reduce_block(s)
    nk = s // bk

    def act_bwd(acc, dg_da, du_da):
        da = 0.5 * acc
        return da * dg_da.astype(F32), da * du_da.astype(F32)

    slab = pl.BlockSpec((None, bm, FF_SHARD), lambda j, i, k: (j, i, 0))
    shp = jax.ShapeDtypeStruct((N_CHIPS, s, FF_SHARD), BF16)
    dg, du = _matmul(
        tag + "_dact", (N_CHIPS, s // bm, 1),
        [(dh, pl.BlockSpec((bm, D_MODEL), lambda j, i, k: (i, 0)),
          wd, pl.BlockSpec((None, FF_SHARD, D_MODEL), lambda j, i, k: (j, 0, 0)), NT)],
        [(dadg, slab), (dadu, slab)], [(shp, slab), (shp, slab)], act_bwd, None)

    grads[tag + "_w_down"] = _matmul(
        tag + "_dwd", (N_CHIPS, nk),
        [(a, pl.BlockSpec((None, bk, FF_SHARD), lambda j, k: (j, k, 0)),
          dh, pl.BlockSpec((bk, D_MODEL), lambda j, k: (k, 0)), TN)],
        [], [(jax.ShapeDtypeStruct((N_CHIPS, FF_SHARD, D_MODEL), BF16),
              pl.BlockSpec((None, FF_SHARD, D_MODEL), lambda j, k: (j, 0, 0)))],
        lambda acc: (0.5 * acc,), (FF_SHARD, D_MODEL))[0]

    def dw_up(nm, dact):
        return _matmul(
            nm, (N_CHIPS, nk),
            [(dact, pl.BlockSpec((None, bk, FF_SHARD), lambda j, k: (j, k, 0)),
              n, pl.BlockSpec((bk, D_MODEL), lambda j, k: (k, 0)), TN)],
            [], [(jax.ShapeDtypeStruct((N_CHIPS, FF_SHARD, D_MODEL), BF16),
                  pl.BlockSpec((None, FF_SHARD, D_MODEL), lambda j, k: (j, 0, 0)))],
            _ident, (FF_SHARD, D_MODEL))[0]

    grads[tag + "_w_gate"] = dw_up(tag + "_dwg", dg)
    grads[tag + "_w_up"] = dw_up(tag + "_dwu", du)

    bn = _row_block(s, 512)
    steps = s // bn // 2
    dn = None
    for part, off in (("_dn_a", 0), ("_dn_b", steps)):
        row = pl.BlockSpec((bn, D_MODEL), lambda i, k, off=off: (i + off, 0))
        terms = []
        for j in range(N_CHIPS):
            a_slab = pl.BlockSpec((None, bn, FF_SHARD), lambda i, k, j=j, off=off: (j, i + off, 0))
            w_slab = pl.BlockSpec((None, FF_SHARD, D_MODEL), lambda i, k, j=j: (j, 0, 0))
            terms += [(dg, a_slab, wg, w_slab, NN), (du, a_slab, wu, w_slab, NN)]
        dn = _matmul(tag + part, (steps, 1), terms, [], [(jax.ShapeDtypeStruct((s, D_MODEL), F32), row)],
                     _ident, None, fill=dn)[0]
    return dn


def _mm_nn(name, a, b, out_dtype, res=None, gain=None):
    s, k = a.shape
    nn = b.shape[1]
    bm = _row_block(s)
    row = pl.BlockSpec((bm, nn), lambda i, kk: (i, 0))
    term = [(a, pl.BlockSpec((bm, k), lambda i, kk: (i, 0)), b, pl.BlockSpec((k, nn), lambda i, kk: (0, 0)), NN)]
    if res is None:
        return _matmul(name, (s // bm, 1), term, [], [(jax.ShapeDtypeStruct((s, nn), out_dtype), row)], _ident, None)[0]
    extras, outs = _residual_outs(s, bm, gain)
    res_out = _matmul(name, (s // bm, 1), term, [(res, row)] + extras, outs,
                      _residual_epilogue(1.0, gain is not None), None)
    return res_out if gain is not None else res_out[0]


def _mm_nt(name, a, b, out_dtype, attn_out=None, nh=0, dv=0):
    s, nn = a.shape
    k = b.shape[0]
    bm = _row_block(s)
    term = [(a, pl.BlockSpec((bm, nn), lambda i, kk: (i, 0)), b, pl.BlockSpec((k, nn), lambda i, kk: (0, 0)), NT)]
    out = (jax.ShapeDtypeStruct((s, k), out_dtype), pl.BlockSpec((bm, k), lambda i, kk: (i, 0)))
    if attn_out is None:
        return _matmul(name, (s // bm, 1), term, [], [out], _ident, None)[0]

    def with_delta(acc, o):
        do = acc.astype(out_dtype).astype(F32)
        cols = [jnp.sum(do[:, h * dv:(h + 1) * dv] * o[:, h * dv:(h + 1) * dv].astype(F32), axis=-1, keepdims=True)
                for h in range(nh)]
        return acc, jnp.stack(cols, axis=0)

    return _matmul(
        name, (s // bm, 1), term, [(attn_out, pl.BlockSpec((bm, nh * dv), lambda i, kk: (i, 0)))],
        [out, (jax.ShapeDtypeStruct((nh, s, 1), F32), pl.BlockSpec((nh, bm, 1), lambda i, kk: (0, i, 0)))],
        with_delta, None)


def _mm_tn(name, a, b, out_dtype=BF16):
    s, k = a.shape
    nn = b.shape[1]
    bk = _reduce_block(s)
    return _matmul(
        name, (s // bk,),
        [(a, pl.BlockSpec((bk, k), lambda kk: (kk, 0)), b, pl.BlockSpec((bk, nn), lambda kk: (kk, 0)), TN)],
        [], [(jax.ShapeDtypeStruct((k, nn), out_dtype), pl.BlockSpec((k, nn), lambda kk: (0, 0)))],
        _ident, (k, nn))[0]


def _mm_heads_fwd(name, a, w, out_dtype, w_transposed=False):
    s, k = a.shape
    nh = w.shape[0]
    nn = w.shape[1] if w_transposed else w.shape[2]
    bm = _row_block(s)
    return _matmul(
        name, (nh, s // bm, 1),
        [(a, pl.BlockSpec((bm, k), lambda h, i, kk: (i, 0)),
          w, pl.BlockSpec((None,) + w.shape[1:], lambda h, i, kk: (h, 0, 0)), NT if w_transposed else NN)],
        [], [(jax.ShapeDtypeStruct((s, nh * nn), out_dtype), pl.BlockSpec((bm, nn), lambda h, i, kk: (i, h)))],
        _ident, None)[0]


def _mm_heads_bwd(name, dy, a, w, w_transposed=False):
    s, k = a.shape
    nh = w.shape[0]
    nn = w.shape[1] if w_transposed else w.shape[2]
    bm = _row_block(s)
    bk = _reduce_block(s)
    w_spec = pl.BlockSpec((None,) + w.shape[1:], lambda i, h: (h, 0, 0))
    da = _matmul(
        name + "_dx", (s // bm, nh),
        [(dy, pl.BlockSpec((bm, nn), lambda i, h: (i, h)), w, w_spec, NN if w_transposed else NT)],
        [], [(jax.ShapeDtypeStruct((s, k), F32), pl.BlockSpec((bm, k), lambda i, h: (i, 0)))], _ident, (bm, k))[0]
    a_term = (a, pl.BlockSpec((bk, k), lambda h, kk: (kk, 0)))
    dy_term = (dy, pl.BlockSpec((bk, nn), lambda h, kk: (kk, h)))
    lhs, rhs = (dy_term, a_term) if w_transposed else (a_term, dy_term)
    dw = _matmul(
        name + "_dw", (nh, s // bk), [lhs + rhs + (TN,)],
        [], [(jax.ShapeDtypeStruct(w.shape, BF16), pl.BlockSpec((None,) + w.shape[1:], lambda h, kk: (h, 0, 0)))],
        _ident, w.shape[1:])[0]
    return da, dw


def _w_in_fwd(n, w_t):
    s = n.shape[0]
    bm = _row_block(s)
    nh, nout, kin = w_t.shape
    terms = [(n, pl.BlockSpec((bm, kin), lambda i, k, j=j: (i, j)),
              w_t, pl.BlockSpec((None, nout, kin), lambda i, k, j=j: (j, 0, 0)), NT) for j in range(nh)]
    row = pl.BlockSpec((bm, nout), lambda i, k: (i, 0))
    return _matmul("w_in", (s // bm, 1), terms, [], [(jax.ShapeDtypeStruct((s, nout), F32), row)], _ident, None)[0]


def _w_in_dw(dz, n):
    s, nout = dz.shape
    kin = n.shape[1] // N_CHIPS
    bk = _reduce_block(s)
    return _matmul(
        "w_in_dw", (N_CHIPS, s // bk),
        [(dz, pl.BlockSpec((bk, nout), lambda j, k: (k, 0)), n, pl.BlockSpec((bk, kin), lambda j, k: (k, j)), TN)],
        [], [(jax.ShapeDtypeStruct((N_CHIPS, nout, kin), BF16), pl.BlockSpec((None, nout, kin), lambda j, k: (j, 0, 0)))],
        _ident, (nout, kin))[0]


def _rope_tables(positions):
    half = ROPE_DIM // 2
    freqs = 1.0 / (ROPE_BASE ** (jnp.arange(0, ROPE_DIM, 2, dtype=F32) / ROPE_DIM))
    ang = positions.astype(F32)[:, None] * freqs
    cos, sin = jnp.cos(ang), jnp.sin(ang)
    z = jnp.zeros_like(cos)
    tc = jnp.concatenate([cos, cos, z, z], axis=-1)
    ta = jnp.concatenate([-sin, z, z, z], axis=-1)
    tb = jnp.concatenate([z, sin, z, z], axis=-1)
    assert tc.shape[-1] == 4 * half
    return tc, ta, tb


def _rope(x, tc, ta, tb):
    return x * tc + pltpu.roll(x, 96, 1) * ta + pltpu.roll(x, 32, 1) * tb


def _rope_t(dy, tc, ta, tb):
    return dy * tc + pltpu.roll(dy * ta, 32, 1) + pltpu.roll(dy * tb, 96, 1)


def _q_rope(q, tc, ta, tb, transpose):
    s = q.shape[0]
    bm = _row_block(s, 512)
    rot = _rope_t if transpose else _rope

    def body(q_ref, tc_ref, ta_ref, tb_ref, o_ref):
        c, a, b = tc_ref[...], ta_ref[...], tb_ref[...]
        for h in range(MLA_HEADS):
            lo = h * HEAD_QK
            o_ref[:, lo:lo + 128] = q_ref[:, lo:lo + 128].astype(BF16)
            o_ref[:, lo + 128:lo + 256] = rot(q_ref[:, lo + 128:lo + 256], c, a, b).astype(BF16)

    row = pl.BlockSpec((bm, MLA_HEADS * HEAD_QK), lambda i: (i, 0))
    tab = pl.BlockSpec((bm, 128), lambda i: (i, 0))
    return pl.pallas_call(
        body, name="q_rope_t" if transpose else "q_rope", grid=(s // bm,), in_specs=[row, tab, tab, tab],
        out_specs=row, out_shape=jax.ShapeDtypeStruct((s, MLA_HEADS * HEAD_QK), BF16),
        compiler_params=_params("parallel"),
    )(q, tc, ta, tb)


def _kv_assemble(kv, z, tc, ta, tb):
    s = kv.shape[0]
    bm = _row_block(s, 512)

    def body(kv_ref, kr_ref, tc_ref, ta_ref, tb_ref, k_ref, v_ref):
        kpe = _rope(kr_ref[...], tc_ref[...], ta_ref[...], tb_ref[...]).astype(BF16)
        for h in range(MLA_HEADS):
            lo = h * 256
            k_ref[:, lo:lo + 128] = kv_ref[:, lo:lo + 128].astype(BF16)
            k_ref[:, lo + 128:lo + 256] = kpe
            v_ref[:, h * 128:(h + 1) * 128] = kv_ref[:, lo + 128:lo + 256].astype(BF16)

    row = pl.BlockSpec((bm, 1024), lambda i: (i, 0))
    tab = pl.BlockSpec((bm, 128), lambda i: (i, 0))
    return pl.pallas_call(
        body, name="kv_assemble", grid=(s // bm,),
        in_specs=[row, pl.BlockSpec((bm, 128), lambda i: (i, 3)), tab, tab, tab],
        out_specs=[row, pl.BlockSpec((bm, 512), lambda i: (i, 0))],
        out_shape=[jax.ShapeDtypeStruct((s, 1024), BF16), jax.ShapeDtypeStruct((s, 512), BF16)],
        compiler_params=_params("parallel"),
    )(kv, z, tc, ta, tb)


def _kv_assemble_bwd(dk, dv, tc, ta, tb):
    s = dk.shape[0]
    bm = _row_block(s, 512)

    def body(dk_ref, dv_ref, tc_ref, ta_ref, tb_ref, dkv_ref, dkr_ref):
        dpe = None
        for h in range(MLA_HEADS):
            lo = h * 256
            dkv_ref[:, lo:lo + 128] = dk_ref[:, lo:lo + 128].astype(BF16)
            dkv_ref[:, lo + 128:lo + 256] = dv_ref[:, h * 128:(h + 1) * 128].astype(BF16)
            t = dk_ref[:, lo + 128:lo + 256]
            dpe = t if dpe is None else dpe + t
        dkr_ref[...] = _rope_t(dpe, tc_ref[...], ta_ref[...], tb_ref[...])

    row = pl.BlockSpec((bm, 1024), lambda i: (i, 0))
    tab = pl.BlockSpec((bm, 128), lambda i: (i, 0))
    return pl.pallas_call(
        body, name="kv_assemble_bwd", grid=(s // bm,),
        in_specs=[row, pl.BlockSpec((bm, 512), lambda i: (i, 0)), tab, tab, tab],
        out_specs=[row, tab],
        out_shape=[jax.ShapeDtypeStruct((s, 1024), BF16), jax.ShapeDtypeStruct((s, 128), F32)],
        compiler_params=_params("parallel"),
    )(dk, dv, tc, ta, tb)


def _causal_mask(s, row0, col0):
    rows = row0 + lax.broadcasted_iota(jnp.int32, s.shape, 0)
    cols = col0 + lax.broadcasted_iota(jnp.int32, s.shape, 1)
    return jnp.where(cols <= rows, s, -jnp.inf)


def _attn_fwd(name, q, k, k_off, v, v_off, nh, dq, dv, scale, causal, blk):
    sq, sk = q.shape[0], k.shape[0]
    bq = min(blk, sq)
    bk = min(blk, sk)
    nkv = sk // bk
    assert not causal or (sq == sk and bq == bk)

    hq = bq
    log2e = 1.4426950408889634
    c2 = scale * log2e

    def body(q_ref, k_ref, v_ref, o_ref, lse_ref):
        qi = pl.program_id(1)
        qs = (q_ref[...],)

        def step(j, carry, masked):
            rows = pl.ds(pl.multiple_of(j * bk, bk), bk)
            kb, vb = k_ref[rows, :], v_ref[rows, :]
            out = []
            for t, (m, l, acc) in enumerate(carry):
                s = _dot(qs[t], kb, NT) * c2
                if masked:
                    s = _causal_mask(s, qi * bq + t * hq, j * bk)
                m_new = jnp.maximum(m, jnp.max(s, axis=-1, keepdims=True))
                alpha = jnp.exp2(m - m_new)
                p = jnp.exp2(s - m_new)
                l = alpha * l + jnp.sum(p, axis=-1, keepdims=True)
                acc = alpha * acc + _dot(p, vb, NN)
                out.append((m_new, l, acc))
            return tuple(out)

        one = (jnp.full((hq, 1), -jnp.inf, F32), jnp.zeros((hq, 1), F32), jnp.zeros((hq, dv), F32))
        init = (one,)
        if causal:
            carry = lax.fori_loop(0, qi, lambda j, c: step(j, c, False), init)
            fin = step(qi, carry, True)
        else:
            fin = lax.fori_loop(0, nkv, lambda j, c: step(j, c, False), init)
        for t, (m, l, acc) in enumerate(fin):
            o_ref[t * hq:(t + 1) * hq, :] = (acc / l).astype(o_ref.dtype)
            lse_ref[t * hq:(t + 1) * hq, :] = m * (1.0 / log2e) + jnp.log(l)

    return _call(
        body, name=name, grid=(nh, sq // bq),
        in_specs=[pl.BlockSpec((bq, dq), lambda h, i: (i, h)),
                  pl.BlockSpec((sk, dq), lambda h, i: (0, k_off + h)),
                  pl.BlockSpec((sk, dv), lambda h, i: (0, v_off + h))],
        out_specs=[pl.BlockSpec((bq, dv), lambda h, i: (i, h)), pl.BlockSpec((None, bq, 1), lambda h, i: (h, i, 0))],
        out_shape=[jax.ShapeDtypeStruct((sq, nh * dv), BF16), jax.ShapeDtypeStruct((nh, sq, 1), F32)],
        sem=("parallel", "parallel"), args=[q, k, v])


def _attn_delta(name, do, do_off, o, nh, dv):
    s = o.shape[0]
    bm = _row_block(s, 512)

    def body(do_ref, o_ref, d_ref):
        d_ref[...] = jnp.sum(do_ref[...].astype(F32) * o_ref[...].astype(F32), axis=-1, keepdims=True)

    return pl.pallas_call(
        body, name=name, grid=(nh, s // bm),
        in_specs=[pl.BlockSpec((bm, dv), lambda h, i: (i, do_off + h)), pl.BlockSpec((bm, dv), lambda h, i: (i, h))],
        out_specs=pl.BlockSpec((None, bm, 1), lambda h, i: (h, i, 0)),
        out_shape=jax.ShapeDtypeStruct((nh, s, 1), F32),
        compiler_params=_params("parallel", "parallel"),
    )(do, o)


def _attn_bwd(name, q, k, k_off, v, v_off, do, do_off, lse, delta, nh, dq, dv, scale, causal, blk):
    sq, sk = q.shape[0], k.shape[0]
    bq = min(blk, sq)
    bk = min(blk, sk)
    nq = sq // bq
    assert not causal or (sq == sk and bq == bk)

    def body(q_ref, k_ref, v_ref, do_ref, lse_ref, dl_ref, dq_ref, dk_ref, dv_ref, dk_acc, dv_acc):
        j = pl.program_id(1)

        @pl.when(j == 0)
        def _():
            dq_ref[...] = jnp.zeros_like(dq_ref)

        dk_acc[...] = jnp.zeros_like(dk_acc)
        dv_acc[...] = jnp.zeros_like(dv_acc)
        kv = k_ref[...]
        vv = v_ref[...]

        def step(i, masked):
            rows = pl.ds(pl.multiple_of(i * bq, bq), bq)
            qv = q_ref[rows, :]
            dov = do_ref[rows, :].astype(BF16)
            s = _dot(qv, kv, NT) * scale
            if masked:
                s = _causal_mask(s, i * bq, j * bk)
            p = jnp.exp(s - lse_ref[rows, :])
            dp = _dot(dov, vv, NT)
            ds = (p * (dp - dl_ref[rows, :]) * scale).astype(BF16)
            dv_acc[...] += _dot(p, dov, TN)
            dk_acc[...] += _dot(ds, qv, TN)
            dq_ref[rows, :] += _dot(ds, kv, NN)

        if causal:
            step(j, True)

            def loop(i, c):
                step(i, False)
                return c

            lax.fori_loop(j + 1, nq, loop, 0)
        else:
            def loop(i, c):
                step(i, False)
                return c

            lax.fori_loop(0, nq, loop, 0)
        dk_ref[...] = dk_acc[...]
        dv_ref[...] = dv_acc[...]

    stat = pl.BlockSpec((None, sq, 1), lambda h, j: (h, 0, 0))
    return _call(
        body, name=name, grid=(nh, sk // bk),
        in_specs=[pl.BlockSpec((sq, dq), lambda h, j: (0, h)),
                  pl.BlockSpec((bk, dq), lambda h, j: (j, k_off + h)),
                  pl.BlockSpec((bk, dv), lambda h, j: (j, v_off + h)),
                  pl.BlockSpec((sq, dv), lambda h, j: (0, do_off + h)), stat, stat],
        out_specs=[pl.BlockSpec((sq, dq), lambda h, j: (0, h)),
                   pl.BlockSpec((bk, dq), lambda h, j: (j, h)),
                   pl.BlockSpec((bk, dv), lambda h, j: (j, h))],
        out_shape=[jax.ShapeDtypeStruct((sq, nh * dq), F32), jax.ShapeDtypeStruct((sk, nh * dq), F32),
                   jax.ShapeDtypeStruct((sk, nh * dv), F32)],
        scratch_shapes=[pltpu.VMEM((bk, dq), F32), pltpu.VMEM((bk, dv), F32)],
        sem=("parallel", "arbitrary"), args=[q, k, v, do, lse, delta])


def _pool_diff(z, g):
    s = z.shape[0]
    t = lax.broadcasted_iota(jnp.int32, z.shape, 0)
    acc = z
    sums = []
    for k in (1, 2, 4, 8):
        acc = acc + jnp.where(t >= k, pltpu.roll(acc, k, 0), 0.0)
        sums.append(acc)
    win = jnp.where(g == 0, sums[0], jnp.where(g == 1, sums[1], jnp.where(g == 2, sums[2], sums[3])))
    w = lax.shift_left(jnp.int32(2), g)
    count = jnp.minimum(t + 1, w).astype(F32)
    del s
    return win / count - z, count


def _pool_fwd(z, pool_w, pool_scale):
    s = z.shape[0]

    def body(z_ref, w_ref, sc_ref, o_ref):
        diff, _ = _pool_diff(z_ref[...], pl.program_id(0))
        o_ref[...] = (_dot(diff, w_ref[...], NN) * sc_ref[...]).astype(o_ref.dtype)

    return _call(
        body, name="pool_fwd", grid=(POOL_GROUPS,),
        in_specs=[pl.BlockSpec((s, POOL_CH), lambda g: (0, 4 + g)),
                  pl.BlockSpec((None, POOL_CH, POOL_CH), lambda g: (g, 0, 0)),
                  pl.BlockSpec((1, POOL_CH), lambda g: (0, g))],
        out_specs=[pl.BlockSpec((s, POOL_CH), lambda g: (0, g))],
        out_shape=[jax.ShapeDtypeStruct((s, POOL_GROUPS * POOL_CH), BF16)],
        sem=("parallel",), args=[z, pool_w, pool_scale])[0]


def _pool_bwd(dcat, z, pool_w, pool_scale):
    s = z.shape[0]

    def body(dp_ref, z_ref, w_ref, sc_ref, dz_ref, dw_ref, dsc_ref):
        g = pl.program_id(0)
        diff, count = _pool_diff(z_ref[...], g)
        dpf = dp_ref[...].astype(F32)
        u = _dot(diff, w_ref[...], NN)
        dsc_ref[...] = jnp.sum(dpf * u, axis=0, keepdims=True)
        du = (dpf * sc_ref[...]).astype(BF16)
        dw_ref[...] = _dot(diff, du, TN)
        ddiff = _dot(du, w_ref[...], NT)
        t = lax.broadcasted_iota(jnp.int32, ddiff.shape, 0)
        acc = ddiff / count
        sums = []
        for k in (1, 2, 4, 8):
            acc = acc + jnp.where(t < s - k, pltpu.roll(acc, s - k, 0), 0.0)
            sums.append(acc)
        win = jnp.where(g == 0, sums[0], jnp.where(g == 1, sums[1], jnp.where(g == 2, sums[2], sums[3])))
        dz_ref[...] = win - ddiff

    return pl.pallas_call(
        body, name="pool_bwd", grid=(POOL_GROUPS,),
        in_specs=[pl.BlockSpec((s, POOL_CH), lambda g: (0, 4 + g)),
                  pl.BlockSpec((s, POOL_CH), lambda g: (0, 4 + g)),
                  pl.BlockSpec((None, POOL_CH, POOL_CH), lambda g: (g, 0, 0)),
                  pl.BlockSpec((1, POOL_CH), lambda g: (0, g))],
        out_specs=[pl.BlockSpec((s, POOL_CH), lambda g: (0, g)),
                   pl.BlockSpec((None, POOL_CH, POOL_CH), lambda g: (g, 0, 0)),
                   pl.BlockSpec((1, POOL_CH), lambda g: (0, g))],
        out_shape=[jax.ShapeDtypeStruct((s, POOL_GROUPS * POOL_CH), F32),
                   jax.ShapeDtypeStruct((POOL_GROUPS, POOL_CH, POOL_CH), F32),
                   jax.ShapeDtypeStruct((1, POOL_GROUPS * POOL_CH), F32)],
        compiler_params=_params("parallel"),
    )(dcat, z, pool_w, pool_scale)


def _local_step(x, mem, positions, target, w, grads):
    tc, ta, tb = _rope_tables(positions)
    blk = _ATT_BLOCK

    n1 = _rmsnorm_fwd("ffn1_norm", x, w["ffn1_norm"], D_MODEL)
    a1, dadu1, dadg1 = _ffn_up("ffn1_up", n1, w["ffn1_w_gate"], w["ffn1_w_up"])
    h1, n2 = _ffn_down("ffn1_down", a1, w["ffn1_w_down"], x, w["mix_norm"])
    z = _w_in_fwd(n2, w["w_in"])
    qn = _rmsnorm_fwd("q_norm", z, w["q_norm"], Q_LORA, 0)
    kvn = _rmsnorm_fwd("kv_norm", z, w["kv_norm"], KV_LORA, 2)
    qp = _mm_heads_fwd("q_up", qn, w["w_q_up"], F32, w_transposed=True)
    kvp = _mm_heads_fwd("kv_up", kvn, w["w_kv_up"], F32)
    qf = _q_rope(qp, tc, ta, tb, False)
    kf, vf = _kv_assemble(kvp, z, tc, ta, tb)
    att, lse = _attn_fwd("mla_fwd", qf, kf, 0, vf, 0, MLA_HEADS, HEAD_QK, HEAD_V, MLA_SCALE, True, blk)
    pool = _pool_fwd(z, w["pool_w"], w["pool_scale"])
    s = x.shape[0]
    bm = _row_block(s)
    row = pl.BlockSpec((bm, D_MODEL), lambda i, k: (i, 0))
    half = pl.BlockSpec((bm, 512), lambda i, k: (i, 0))
    h2, n3 = _matmul(
        "w_out", (s // bm, 1),
        [(att, half, w["w_out"], pl.BlockSpec((512, D_MODEL), lambda i, k: (0, 0)), NN),
         (pool, half, w["w_out"], pl.BlockSpec((512, D_MODEL), lambda i, k: (1, 0)), NN)],
        [(h1, row)] + _residual_outs(s, bm, w["xattn_norm"])[0], _residual_outs(s, bm, w["xattn_norm"])[1],
        _residual_epilogue(1.0, True), None)
    memn = _rmsnorm_fwd("mem_norm", mem, w["mem_norm"], D_MODEL)
    qm = _mm_nn("w_mq", n3, w["w_mq"], BF16)
    kvm = _mm_heads_fwd("w_mkv", memn, w["w_mkv"], BF16)
    om, lse_m = _attn_fwd("xattn_fwd", qm, kvm, 0, kvm, MEM_HEADS, MEM_HEADS, MEM_HEAD_DIM, MEM_HEAD_DIM,
                          MEM_SCALE, False, blk)
    h3, n4 = _mm_nn("w_mo", om, w["w_mo"], F32, res=h2, gain=w["ffn2_norm"])
    a2, dadu2, dadg2 = _ffn_up("ffn2_up", n4, w["ffn2_w_gate"], w["ffn2_w_up"])
    h4 = _ffn_down("ffn2_down", a2, w["ffn2_w_down"], h3)

    dh4, dh4b, loss_vec, d_final = _loss_and_final_norm(h4, w["final_norm"], target)
    grads["final_norm"] = d_final

    dn4 = _ffn_bwd("ffn2", dh4b, n4, dadg2, dadu2, a2, w["ffn2_w_gate"], w["ffn2_w_up"], w["ffn2_w_down"], grads)
    dh3, grads["ffn2_norm"], dh3b = _rmsnorm_bwd("ffn2_norm_bwd", dn4, h3, w["ffn2_norm"], D_MODEL, dres=dh4)

    dom, delta_m = _mm_nt("w_mo_dx", dh3b, w["w_mo"], BF16, attn_out=om, nh=MEM_HEADS, dv=MEM_HEAD_DIM)
    grads["w_mo"] = _mm_tn("w_mo_dw", om, dh3b)
    dqm, dkm, dvm = _attn_bwd("xattn_bwd", qm, kvm, 0, kvm, MEM_HEADS, dom, 0, lse_m, delta_m, MEM_HEADS,
                              MEM_HEAD_DIM, MEM_HEAD_DIM, MEM_SCALE, False, blk)
    dkvm = jnp.concatenate([dkm, dvm], axis=1).astype(BF16)
    dn3 = _mm_nt("w_mq_dx", dqm, w["w_mq"], F32)
    grads["w_mq"] = _mm_tn("w_mq_dw", n3, dqm)
    dmemn, grads["w_mkv"] = _mm_heads_bwd("w_mkv", dkvm, memn, w["w_mkv"])
    _, grads["mem_norm"] = _rmsnorm_bwd("mem_norm_bwd", dmemn, mem, w["mem_norm"], D_MODEL, out_dtype=BF16)
    dh2, grads["xattn_norm"], dh2b = _rmsnorm_bwd("xattn_norm_bwd", dn3, h2, w["xattn_norm"], D_MODEL, dres=dh3)

    dcat, delta = _mm_nt("w_out_dx", dh2b, w["w_out"], BF16, attn_out=att, nh=MLA_HEADS, dv=HEAD_V)
    grads["w_out"] = jnp.concatenate([_mm_tn("w_out_dw_a", att, dh2b), _mm_tn("w_out_dw_p", pool, dh2b)], axis=0)
    dzp, grads["pool_w"], grads["pool_scale"] = _pool_bwd(dcat, z, w["pool_w"], w["pool_scale"])
    dqf, dkf, dvf = _attn_bwd("mla_bwd", qf, kf, 0, vf, 0, dcat, 0, lse, delta, MLA_HEADS, HEAD_QK, HEAD_V,
                              MLA_SCALE, True, blk)
    dqp = _q_rope(dqf, tc, ta, tb, True)
    dkvp, dkr = _kv_assemble_bwd(dkf, dvf, tc, ta, tb)
    dqn, grads["w_q_up"] = _mm_heads_bwd("q_up", dqp, qn, w["w_q_up"], w_transposed=True)
    dkvn, grads["w_kv_up"] = _mm_heads_bwd("kv_up", dkvp, kvn, w["w_kv_up"])
    dcq, grads["q_norm"] = _rmsnorm_bwd("q_norm_bwd", dqn, z, w["q_norm"], Q_LORA, 0, out_dtype=BF16)
    dckv, grads["kv_norm"] = _rmsnorm_bwd("kv_norm_bwd", dkvn, z, w["kv_norm"], KV_LORA, 2, out_dtype=BF16)
    dz = jnp.concatenate([dcq, dckv, dkr.astype(BF16), dzp.astype(BF16)], axis=1)
    dn2 = _mm_heads_fwd("w_in_dx", dz, w["w_in"], F32)
    grads["w_in"] = _w_in_dw(dz, n2)
    dh1, grads["mix_norm"], dh1b = _rmsnorm_bwd("mix_norm_bwd", dn2, h1, w["mix_norm"], D_MODEL, dres=dh2)

    dn1 = _ffn_bwd("ffn1", dh1b, n1, dadg1, dadu1, a1, w["ffn1_w_gate"], w["ffn1_w_up"], w["ffn1_w_down"], grads)
    dx, grads["ffn1_norm"], _ = _rmsnorm_bwd("ffn1_norm_bwd", dn1, x, w["ffn1_norm"], D_MODEL, dres=dh1)
    return loss_vec[0, 0], dx


def _mesh_pos():
    x, y, c = lax.axis_index("x"), lax.axis_index("y"), lax.axis_index("c")
    chips = [(1 - x, y), (x, 1 - y), (1 - x, 1 - y)]
    chip_ids = [2 * cx + cy for cx, cy in chips]
    return x, y, c, 2 * x + y, chips, chip_ids


def _half_rows(c, rows):
    hr = rows // 2
    return pl.ds(pl.multiple_of(c * hr, 16), hr), pl.ds(pl.multiple_of((1 - c) * hr, 16), hr)


def _ag_ici_stage(shards):
    n = len(shards)

    def copies(ins, outs):
        x, y, c, me, chips, _ = _mesh_pos()
        out = []
        for k in range(n):
            mine, _ = _half_rows(c, ins[k].shape[0])
            out.append((ins[k], outs[k].at[me], None))
            for cx, cy in chips:
                out.append((ins[k].at[mine], outs[k].at[me, mine], (cx, cy, c)))
        return out

    return _Stage(shards, [jax.ShapeDtypeStruct((N_CHIPS,) + s.shape, s.dtype) for s in shards], 3 * n, n, copies)


def _ag_d2d_stage(fulls):
    n = len(fulls)

    def copies(ins, outs):
        x, y, c, me, _, chip_ids = _mesh_pos()
        out = []
        for k in range(n):
            mine, _ = _half_rows(c, ins[k].shape[1])
            for j in range(3):
                out.append((ins[k].at[chip_ids[j], mine], outs[k].at[chip_ids[j], mine], (x, y, 1 - c)))
        return out

    return _Stage(fulls, [jax.ShapeDtypeStruct(f.shape, f.dtype) for f in fulls], 3 * n, 0, copies,
                  aliases={k: k for k in range(n)})


def _rs_swap_stage(grads):
    n = len(grads)

    def copies(ins, outs):
        x, y, c, _, _, _ = _mesh_pos()
        out = []
        for k in range(n):
            _, other = _half_rows(c, ins[k].shape[1])
            out.append((ins[k].at[:, other, :], outs[k], (x, y, 1 - c)))
        return out

    return _Stage(grads, [jax.ShapeDtypeStruct((N_CHIPS, g.shape[1] // 2, g.shape[2]), g.dtype) for g in grads],
                  n, 0, copies)


def _rs_scatter_stage(sums):
    n = len(sums)

    def copies(ins, outs):
        x, y, c, me, chips, chip_ids = _mesh_pos()
        out = []
        for k in range(n):
            mine, _ = _half_rows(c, 2 * ins[k].shape[1])
            out.append((ins[k].at[me], outs[k].at[0, mine, :], None))
            for j, (cx, cy) in enumerate(chips):
                out.append((ins[k].at[chip_ids[j]], outs[k].at[1 + j, mine, :], (cx, cy, c)))
        return out

    return _Stage(sums, [jax.ShapeDtypeStruct((N_CHIPS, 2 * s.shape[1], s.shape[2]), s.dtype) for s in sums],
                  3 * n, n, copies)


def _rs_mirror_stage(parts):
    n = len(parts)

    def copies(ins, outs):
        x, y, c, _, _, _ = _mesh_pos()
        out = []
        for k in range(n):
            mine, _ = _half_rows(c, ins[k].shape[1])
            out.append((ins[k].at[:, mine, :], outs[k].at[:, mine, :], (x, y, 1 - c)))
        return out

    return _Stage(parts, [jax.ShapeDtypeStruct(p.shape, p.dtype) for p in parts], n, 0, copies,
                  aliases={k: k for k in range(n)})


def _pair_add(name, g, r1, core):
    _, rows, cols = g.shape
    hr = rows // 2

    def body(c_ref, g_ref, r_ref, o_ref):
        o_ref[...] = (g_ref[...].astype(F32) + r_ref[...].astype(F32)).astype(BF16)

    half = pl.BlockSpec((None, hr, cols), lambda j, c: (j, 0, 0))
    return pl.pallas_call(
        body, name=name,
        grid_spec=pltpu.PrefetchScalarGridSpec(
            num_scalar_prefetch=1, grid=(N_CHIPS,),
            in_specs=[pl.BlockSpec((None, hr, cols), lambda j, c: (j, c[0], 0)), half], out_specs=half),
        out_shape=jax.ShapeDtypeStruct((N_CHIPS, hr, cols), BF16),
        compiler_params=_params("parallel"),
    )(core, g, r1)


def _all_gather_weights(shards):
    n = len(shards)

    def body(*refs):
        ins, outs = refs[:n], refs[n:2 * n]
        send, recv, loc = refs[2 * n:]
        x, y, c, me, chips, chip_ids = _mesh_pos()
        sib = (x, y, 1 - c)

        def halves(k):
            hr = ins[k].shape[0] // 2
            return pl.ds(pl.multiple_of(c * hr, 16), hr), pl.ds(pl.multiple_of((1 - c) * hr, 16), hr)

        def remote(src, dst, k, j, dev):
            return pltpu.make_async_remote_copy(src_ref=src, dst_ref=dst, send_sem=send.at[k, j],
                                                recv_sem=recv.at[k, j], device_id=dev, device_id_type=_MESH)

        started = []
        local = []
        for k in range(n):
            mine, _ = halves(k)
            cp = pltpu.make_async_copy(ins[k], outs[k].at[me], loc.at[k])
            cp.start()
            local.append(cp)
            for j, (cx, cy) in enumerate(chips):
                cp = remote(ins[k].at[mine], outs[k].at[me, mine], k, j, (cx, cy, c))
                cp.start()
                started.append(cp)
        for k in range(n):
            mine, _ = halves(k)
            for j in range(3):
                land = outs[k].at[chip_ids[j], mine]
                remote(land, land, k, j, sib).wait_recv()
                cp = remote(land, land, k, 3 + j, sib)
                cp.start()
                started.append(cp)
        for k in range(n):
            _, other = halves(k)
            for j in range(3):
                land = outs[k].at[chip_ids[j], other]
                remote(land, land, k, 3 + j, sib).wait_recv()
        for cp in started:
            cp.wait_send()
        for cp in local:
            cp.wait()

    return pl.pallas_call(
        body, name="all_gather_weights", in_specs=[_ANY] * n, out_specs=[_ANY] * n,
        out_shape=[jax.ShapeDtypeStruct((N_CHIPS,) + s.shape, s.dtype) for s in shards],
        scratch_shapes=[pltpu.SemaphoreType.DMA((n, 6)), pltpu.SemaphoreType.DMA((n, 6)),
                        pltpu.SemaphoreType.DMA((n,))],
        compiler_params=pltpu.CompilerParams(vmem_limit_bytes=V7X_VMEM_LIMIT_BYTES),
    )(*shards)


_RS_CHUNK = 32


def _reduce_scatter(name, grads):
    n = len(grads)

    def body(*refs):
        gs, outs = refs[:n], refs[n:2 * n]
        own, r1, r2, fin = (refs[(2 + i) * n:(3 + i) * n] for i in range(4))
        a_send, a_recv, b_send, b_recv, c_send, c_recv, l_in, l_out = refs[6 * n:]
        x, y, c, me, chips, chip_ids = _mesh_pos()
        sib = (x, y, 1 - c)

        def halves(k):
            hr = gs[k].shape[1] // 2
            return hr, pl.ds(pl.multiple_of(c * hr, 16), hr), pl.ds(pl.multiple_of((1 - c) * hr, 16), hr)

        def remote(src, dst, ssem, rsem, dev):
            return pltpu.make_async_remote_copy(src_ref=src, dst_ref=dst, send_sem=ssem, recv_sem=rsem,
                                                device_id=dev, device_id_type=_MESH)

        sends, locals_in = [], []
        for k in range(n):
            hr, mine, other = halves(k)
            cp = remote(gs[k].at[:, other, :], r1[k], a_send.at[k], a_recv.at[k], sib)
            cp.start()
            sends.append(cp)
            cp = pltpu.make_async_copy(gs[k].at[:, mine, :], own[k], l_in.at[k])
            cp.start()
            locals_in.append(cp)

        for k in range(n):
            hr, mine, other = halves(k)
            locals_in[k].wait()
            remote(r1[k], r1[k], a_send.at[k], a_recv.at[k], sib).wait_recv()
            for j in range(N_CHIPS):
                def add(i, carry, k=k, j=j):
                    rows = pl.ds(pl.multiple_of(i * _RS_CHUNK, _RS_CHUNK), _RS_CHUNK)
                    own[k][j, rows, :] = (own[k][j, rows, :].astype(F32) + r1[k][j, rows, :].astype(F32)).astype(BF16)
                    return carry

                lax.fori_loop(0, hr // _RS_CHUNK, add, 0)
            for j, (cx, cy) in enumerate(chips):
                cp = remote(own[k].at[chip_ids[j]], r2[k].at[j], b_send.at[k, j], b_recv.at[k, j], (cx, cy, c))
                cp.start()
                sends.append(cp)

        locals_out = []
        for k in range(n):
            hr, mine, other = halves(k)
            for j in range(3):
                remote(r2[k].at[j], r2[k].at[j], b_send.at[k, j], b_recv.at[k, j], sib).wait_recv()

            def total(i, carry, k=k):
                rows = pl.ds(pl.multiple_of(i * _RS_CHUNK, _RS_CHUNK), _RS_CHUNK)
                acc = own[k][me, rows, :].astype(F32)
                for j in range(3):
                    acc = acc + r2[k][j, rows, :].astype(F32)
                fin[k][rows, :] = acc
                return carry

            lax.fori_loop(0, hr // _RS_CHUNK, total, 0)
            cp = remote(fin[k], outs[k].at[mine, :], c_send.at[k], c_recv.at[k], sib)
            cp.start()
            sends.append(cp)
            cp = pltpu.make_async_copy(fin[k], outs[k].at[mine, :], l_out.at[k])
            cp.start()
            locals_out.append(cp)

        for k in range(n):
            hr, mine, other = halves(k)
            land = outs[k].at[other, :]
            remote(land, land, c_send.at[k], c_recv.at[k], sib).wait_recv()
        for cp in sends:
            cp.wait_send()
        for cp in locals_out:
            cp.wait()

    scratch = []
    for g in grads:
        scratch.append(pltpu.VMEM((N_CHIPS, g.shape[1] // 2, g.shape[2]), BF16))
    for g in grads:
        scratch.append(pltpu.VMEM((N_CHIPS, g.shape[1] // 2, g.shape[2]), BF16))
    for g in grads:
        scratch.append(pltpu.VMEM((3, g.shape[1] // 2, g.shape[2]), BF16))
    for g in grads:
        scratch.append(pltpu.VMEM((g.shape[1] // 2, g.shape[2]), F32))
    dma = pltpu.SemaphoreType.DMA
    scratch += [dma((n,)), dma((n,)), dma((n, 3)), dma((n, 3)), dma((n,)), dma((n,)), dma((n,)), dma((n,))]
    return pl.pallas_call(
        body, name=name, in_specs=[_ANY] * n, out_specs=[_ANY] * n,
        out_shape=[jax.ShapeDtypeStruct(g.shape[1:], F32) for g in grads],
        scratch_shapes=scratch,
        compiler_params=pltpu.CompilerParams(vmem_limit_bytes=V7X_VMEM_LIMIT_BYTES),
    )(*grads)


def _adamw_math(w, g, m, v):
    m = ADAM_B1 * m + (1.0 - ADAM_B1) * g
    v = ADAM_B2 * v + (1.0 - ADAM_B2) * (g * g)
    m_hat = m / (1.0 - ADAM_B1 ** ADAM_STEP)
    v_hat = v / (1.0 - ADAM_B2 ** ADAM_STEP)
    delta = -ADAM_LR * (m_hat / (jnp.sqrt(v_hat) + ADAM_EPS) + ADAM_WD * w)
    return delta, m, v


def _adamw_sum(name, w, parts, m, v):
    r, c = w.shape
    br = r
    while br * c * 4 > (1 << 20) and br % 32 == 0:
        br //= 2

    def body(w_ref, p_ref, m_ref, v_ref, g_ref, d_ref, nm_ref, nv_ref):
        g = p_ref[0].astype(F32)
        for j in range(1, N_CHIPS):
            g = g + p_ref[j].astype(F32)
        d, nm, nv = _adamw_math(w_ref[...], g, m_ref[...], v_ref[...])
        g_ref[...] = g
        d_ref[...] = d
        nm_ref[...] = nm
        nv_ref[...] = nv

    spec = pl.BlockSpec((br, c), lambda i: (i, 0))
    shp = jax.ShapeDtypeStruct((r, c), F32)
    return _call(
        body, name=name, grid=(r // br,),
        in_specs=[spec, pl.BlockSpec((N_CHIPS, br, c), lambda i: (0, i, 0)), spec, spec],
        out_specs=[spec] * 4, out_shape=[shp] * 4, sem=("parallel",), args=[w, parts, m, v])


_SMALL_ROWS = 80


def _small_allreduce_adamw(gpack, wpack, mpack, vpack):
    def body(g_ref, w_ref, m_ref, v_ref, go_ref, d_ref, nm_ref, nv_ref, buf, send, recv):
        x, y, c = lax.axis_index("x"), lax.axis_index("y"), lax.axis_index("c")
        me = 4 * x + 2 * y + c
        buf[me] = g_ref[...]
        copies = []
        for rel in range(1, 8):
            fx, fy, fc = (rel >> 2) & 1, (rel >> 1) & 1, rel & 1
            dev = ((1 - x) if fx else x, (1 - y) if fy else y, (1 - c) if fc else c)
            cp = pltpu.make_async_remote_copy(src_ref=g_ref, dst_ref=buf.at[me], send_sem=send.at[rel - 1],
                                              recv_sem=recv.at[rel - 1], device_id=dev, device_id_type=_MESH)
            cp.start()
            copies.append(cp)
        for cp in copies:
            cp.wait_recv()
        for cp in copies:
            cp.wait_send()
        total = buf[0]
        for i in range(1, 8):
            total = total + buf[i]
        go_ref[...] = total
        d, nm, nv = _adamw_math(w_ref[...], total, m_ref[...], v_ref[...])
        d_ref[...] = d
        nm_ref[...] = nm
        nv_ref[...] = nv

    vm = pl.BlockSpec(memory_space=pltpu.VMEM)
    shp = jax.ShapeDtypeStruct((_SMALL_ROWS, D_MODEL), F32)
    return pl.pallas_call(
        body, name="small_allreduce_adamw", in_specs=[vm] * 4, out_specs=[vm] * 4, out_shape=[shp] * 4,
        scratch_shapes=[pltpu.VMEM((8, _SMALL_ROWS, D_MODEL), F32), pltpu.SemaphoreType.DMA((7,)),
                        pltpu.SemaphoreType.DMA((7,))],
        compiler_params=pltpu.CompilerParams(vmem_limit_bytes=V7X_VMEM_LIMIT_BYTES),
    )(gpack, wpack, mpack, vpack)


_SMALL_VECTORS = ("ffn1_norm", "mix_norm", "xattn_norm", "mem_norm", "ffn2_norm", "final_norm", "q_norm",
                  "kv_norm", "pool_scale")


def _pack_small(d):
    rows = []
    for n in _SMALL_VECTORS:
        v = d[n].reshape(1, -1).astype(F32)
        rows.append(jnp.pad(v, ((0, 0), (0, D_MODEL - v.shape[1]))))
    rows.append(jnp.zeros((16 - len(_SMALL_VECTORS), D_MODEL), F32))
    rows.append(d["pool_w"].reshape(64, D_MODEL).astype(F32))
    return jnp.concatenate(rows, axis=0)


def _unpack_small(pack, like):
    out = {}
    for i, n in enumerate(_SMALL_VECTORS):
        out[n] = pack[i, :like[n].size].reshape(like[n].shape)
    out["pool_w"] = pack[16:].reshape(like["pool_w"].shape)
    return out


_WEIGHTS = ("ffn1_norm", "ffn1_w_gate", "ffn1_w_up", "ffn1_w_down", "mix_norm", "w_in", "q_norm", "w_q_up",
            "kv_norm", "w_kv_up", "pool_w", "pool_scale", "w_out", "xattn_norm", "mem_norm", "w_mq", "w_mkv",
            "w_mo", "ffn2_norm", "ffn2_w_gate", "ffn2_w_up", "ffn2_w_down", "final_norm")
_SHARDED = ("ffn1_w_gate", "ffn1_w_up", "ffn1_w_down", "w_in", "w_q_up", "w_kv_up", "w_out", "w_mq", "w_mkv",
            "w_mo", "ffn2_w_gate", "ffn2_w_up", "ffn2_w_down")
_RS_GROUPS = (("ffn2_w_gate", "ffn2_w_up", "ffn2_w_down"),
              ("w_mo", "w_mq", "w_mkv", "w_out", "w_q_up", "w_kv_up", "w_in"),
              ("ffn1_w_gate", "ffn1_w_up", "ffn1_w_down"))
W_IN_SPLIT = Q_LORA + KV_LORA + ROPE_DIM


_TRANSPOSED = ("ffn1_w_gate", "ffn1_w_up", "ffn2_w_gate", "ffn2_w_up", "w_in", "w_q_up")


def _local_view(name, a):
    return jnp.swapaxes(a, 1, 2)[0] if name in _TRANSPOSED else a[0]


def _global_view(name, a):
    return jnp.swapaxes(a[None], 1, 2) if name in _TRANSPOSED else a[None]


def _pad_shard(name, a):
    if name == "w_in":
        return jnp.concatenate([a[:W_IN_SPLIT], jnp.zeros((64, a.shape[1]), a.dtype), a[W_IN_SPLIT:]], axis=0)
    if name == "w_q_up":
        return jnp.pad(a, ((0, 64), (0, 0)))
    return a


def _unpad_shard(name, a):
    if name == "w_in":
        return jnp.concatenate([a[:, :W_IN_SPLIT], a[:, W_IN_SPLIT + 64:]], axis=1)
    if name == "w_q_up":
        return a[:, :192]
    return a


def _stacked(g):
    return g if g.ndim == 3 else g.reshape(N_CHIPS, g.shape[0] // N_CHIPS, g.shape[1])


class _Plan:
    AG_UNITS = (
        (("w_in", "w_q_up", "w_kv_up", "ffn2_w_gate"), "ffn1_up", "ffn1_down"),
        (("ffn2_w_up",), "ffn1_down", "w_in"),
        (("w_out", "w_mq", "w_mkv", "w_mo", "ffn2_w_down"), "mla_fwd", "pool_fwd"),
    )
    RS_UNITS = (
        (("ffn2_w_gate", "ffn2_w_up", "ffn2_w_down"), "ffn2_norm_bwd", "mla_bwd", "q_up_dx"),
        (("w_mo", "w_mq", "w_mkv"), "w_out_dx", "mla_bwd", "q_up_dx"),
        (("w_out", "w_q_up", "w_kv_up", "w_in"), "mix_norm_bwd", "ffn1_dact", "ffn1_dwd"),
        (("ffn1_w_down",), "ffn1_dwg", "ffn1_dwu", "ffn1_dn_a"),
        (("ffn1_w_gate",), "ffn1_dwu", "ffn1_dn_a", "ffn1_dn_b"),
        (("ffn1_w_up",), "ffn1_dn_a", "ffn1_dn_b", "adamw_w_kv_up"),
    )
    ADAMW_ORDER = ("w_kv_up", "ffn2_w_gate", "ffn2_w_up", "ffn2_w_down", "w_mo", "w_mq", "w_mkv", "w_out", "w_q_up",
                   "w_in", "ffn1_w_down", "ffn1_w_gate", "ffn1_w_up")

    def __init__(self, shards, w, grads, core):
        self.shards, self.w, self.grads, self.core = shards, w, grads, core
        self.parts = {}
        self.ag = [[None, None] for _ in self.AG_UNITS]
        self.rs = [[None, None, None, None] for _ in self.RS_UNITS]

    def pre(self, name):
        for i, (names, h1, h2) in enumerate(self.AG_UNITS):
            if name == h1:
                self.ag[i][0] = _host(name, _ag_ici_stage([self.shards[n] for n in names]))
            if name == h2:
                self.ag[i][1] = _host(name, _ag_d2d_stage(self.ag[i][0].results))
        for i, (names, h1, h2, h3) in enumerate(self.RS_UNITS):
            if name == h1:
                self.rs[i][0] = _host(name, _rs_swap_stage([_stacked(self.grads[n]) for n in names]))
            if name == h2:
                self.rs[i][2] = _host(name, _rs_scatter_stage(self.rs[i][1]))
            if name == h3:
                self.rs[i][3] = _host(name, _rs_mirror_stage(self.rs[i][2].results))

    def post(self, name):
        for i, (names, h1, h2) in enumerate(self.AG_UNITS):
            if name == h2:
                for n, f in zip(names, self.ag[i][1].results):
                    self.w[n] = _full_weight(n, f)
        for i, (names, h1, h2, h3) in enumerate(self.RS_UNITS):
            if name == h1:
                self.rs[i][1] = [_pair_add("pair_add_" + n, _stacked(self.grads[n]), r1, self.core)
                                 for n, r1 in zip(names, self.rs[i][0].results)]
            if name == h3:
                for n, p in zip(names, self.rs[i][3].results):
                    self.parts[n] = p


def _full_weight(name, stacked):
    if name in ("w_out", "w_mq", "w_mo"):
        return stacked.reshape(D_MODEL, D_MODEL)
    return stacked


def kernel(x, mem, positions, ffn1_norm, ffn1_w_gate, ffn1_w_up, ffn1_w_down, mix_norm, w_in, q_norm, w_q_up, kv_norm, w_kv_up, pool_w, pool_scale, w_out, xattn_norm, mem_norm, w_mq, w_mkv, w_mo, ffn2_norm, ffn2_w_gate, ffn2_w_up, ffn2_w_down, final_norm, loss_target, m_ffn1_norm, m_ffn1_w_gate, m_ffn1_w_up, m_ffn1_w_down, m_mix_norm, m_w_in, m_q_norm, m_w_q_up, m_kv_norm, m_w_kv_up, m_pool_w, m_pool_scale, m_w_out, m_xattn_norm, m_mem_norm, m_w_mq, m_w_mkv, m_w_mo, m_ffn2_norm, m_ffn2_w_gate, m_ffn2_w_up, m_ffn2_w_down, m_final_norm, v_ffn1_norm, v_ffn1_w_gate, v_ffn1_w_up, v_ffn1_w_down, v_mix_norm, v_w_in, v_q_norm, v_w_q_up, v_kv_norm, v_w_kv_up, v_pool_w, v_pool_scale, v_w_out, v_xattn_norm, v_mem_norm, v_w_mq, v_w_mkv, v_w_mo, v_ffn2_norm, v_ffn2_w_gate, v_ffn2_w_up, v_ffn2_w_down, v_final_norm):
    wts = dict(zip(_WEIGHTS, (ffn1_norm, ffn1_w_gate, ffn1_w_up, ffn1_w_down, mix_norm, w_in, q_norm, w_q_up, kv_norm, w_kv_up, pool_w, pool_scale, w_out, xattn_norm, mem_norm, w_mq, w_mkv, w_mo, ffn2_norm, ffn2_w_gate, ffn2_w_up, ffn2_w_down, final_norm)))
    mom = dict(zip(_WEIGHTS, (m_ffn1_norm, m_ffn1_w_gate, m_ffn1_w_up, m_ffn1_w_down, m_mix_norm, m_w_in, m_q_norm, m_w_q_up, m_kv_norm, m_w_kv_up, m_pool_w, m_pool_scale, m_w_out, m_xattn_norm, m_mem_norm, m_w_mq, m_w_mkv, m_w_mo, m_ffn2_norm, m_ffn2_w_gate, m_ffn2_w_up, m_ffn2_w_down, m_final_norm)))
    var = dict(zip(_WEIGHTS, (v_ffn1_norm, v_ffn1_w_gate, v_ffn1_w_up, v_ffn1_w_down, v_mix_norm, v_w_in, v_q_norm, v_w_q_up, v_kv_norm, v_w_kv_up, v_pool_w, v_pool_scale, v_w_out, v_xattn_norm, v_mem_norm, v_w_mq, v_w_mkv, v_w_mo, v_ffn2_norm, v_ffn2_w_gate, v_ffn2_w_up, v_ffn2_w_down, v_final_norm)))
    small = [n for n in _WEIGHTS if n not in _SHARDED]

    global _PLAN
    shards = {n: _pad_shard(n, _local_view(n, wts[n])).astype(BF16) for n in _SHARDED}
    w = {n: wts[n].reshape(1, -1) for n in _SMALL_VECTORS}
    w["pool_w"] = pool_w[0].astype(BF16)
    grads = {}
    core = lax.axis_index("c").astype(jnp.int32).reshape(1)
    plan = _Plan(shards, w, grads, core)
    _PLAN = plan
    try:
        first = ("ffn1_w_gate", "ffn1_w_up", "ffn1_w_down")
        for n, f in zip(first, _all_gather_weights([shards[n] for n in first])):
            w[n] = f

        loss_local, dx = _local_step(x[0], mem[0], positions[0], loss_target[0], w, grads)
        loss = lax.psum(loss_local, ("x", "y", "c"))

        gpack, dpack, mpack, vpack = _small_allreduce_adamw(
            _pack_small({n: grads[n] for n in small}), _pack_small({n: wts[n] for n in small}),
            _pack_small({n: mom[n] for n in small}), _pack_small({n: var[n] for n in small}))
        like = {n: wts[n] for n in small}
        g_out, d_out, m_out, v_out = (_unpack_small(p, like) for p in (gpack, dpack, mpack, vpack))

        for n in _Plan.ADAMW_ORDER:
            res = _adamw_sum("adamw_" + n, _local_view(n, wts[n]), _unpad_shard(n, plan.parts[n]),
                             _local_view(n, mom[n]), _local_view(n, var[n]))
            g_out[n], d_out[n], m_out[n], v_out[n] = (_global_view(n, r) for r in res)
    finally:
        _PLAN = None
        _PENDING.clear()

    return (loss, dx[None], *[g_out[n] for n in _WEIGHTS], *[d_out[n] for n in _WEIGHTS],
            *[m_out[n] for n in _WEIGHTS], *[v_out[n] for n in _WEIGHTS])
```

```python
import functools

import jax
import jax.numpy as jnp
from jax import lax
from jax.experimental import pallas as pl
from jax.experimental.pallas import tpu as pltpu

F32 = jnp.float32
BF16 = jnp.bfloat16

D_MODEL = 1024
D_FF = 2816
N_CHIPS = 4
FF_SHARD = D_FF // N_CHIPS
MLA_HEADS = 4
Q_LORA = 256
KV_LORA = 128
ROPE_DIM = 64
HEAD_QK = 256
HEAD_V = 128
POOL_GROUPS = 4
POOL_CH = 128
MEM_HEADS = 4
MEM_HEAD_DIM = 256
RMS_EPS = 1e-6
ROPE_BASE = 10000.0
MLA_SCALE = (128 + 64) ** -0.5
MEM_SCALE = MEM_HEAD_DIM ** -0.5

ADAM_LR = 0.001
ADAM_B1 = 0.9
ADAM_B2 = 0.999
ADAM_EPS = 1e-08
ADAM_WD = 0.01
ADAM_STEP = 10

V7X_VMEM_LIMIT_BYTES = 56 * 1024 * 1024

NN = ((1,), (0,))
NT = ((1,), (1,))
TN = ((0,), (0,))


def _params(*sem):
    return pltpu.CompilerParams(dimension_semantics=sem, vmem_limit_bytes=V7X_VMEM_LIMIT_BYTES)


_MESH = pl.DeviceIdType.MESH
_ANY = pl.BlockSpec(memory_space=pl.ANY)


class _Stage:
    def __init__(self, ins, outs, n_remote, n_local, copies, aliases=None):
        self.ins, self.outs, self.n_remote, self.n_local = list(ins), list(outs), n_remote, n_local
        self.copies, self.aliases = copies, dict(aliases or {})
        self.results = None

    def descriptors(self, in_refs, out_refs, send, recv, loc):
        ds, ri, li = [], 0, 0
        for src, dst, dev in self.copies(in_refs, out_refs):
            if dev is None:
                ds.append(pltpu.make_async_copy(src, dst, loc.at[li]))
                li += 1
            else:
                ds.append(pltpu.make_async_remote_copy(src_ref=src, dst_ref=dst, send_sem=send.at[ri],
                                                       recv_sem=recv.at[ri], device_id=dev, device_id_type=_MESH))
                ri += 1
        assert ri == self.n_remote and li == self.n_local
        return ds


_PENDING = {}


def _host(name, stage):
    _PENDING.setdefault(name, []).append(stage)
    return stage


_PLAN = None


def _call(body, **kw):
    if _PLAN is not None:
        _PLAN.pre(kw["name"])
    res = _call_hosting(body, **kw)
    if _PLAN is not None:
        _PLAN.post(kw["name"])
    return res


def _call_hosting(body, *, name, grid, in_specs, out_specs, out_shape, sem, args, scratch_shapes=(), aliases=None):
    stages = _PENDING.pop(name, [])
    scratch_shapes = list(scratch_shapes)
    if not stages:
        return pl.pallas_call(body, name=name, grid=grid, in_specs=in_specs, out_specs=out_specs,
                              out_shape=out_shape, scratch_shapes=scratch_shapes,
                              input_output_aliases=dict(aliases or {}), compiler_params=_params(*sem))(*args)
    ni, no, ns = len(in_specs), len(out_shape), len(scratch_shapes)
    c_ins = [a for st in stages for a in st.ins]
    c_outs = [o for st in stages for o in st.outs]
    nci, nco = len(c_ins), len(c_outs)
    aliases, io, oo = dict(aliases or {}), 0, 0
    for st in stages:
        for i, j in st.aliases.items():
            aliases[ni + io + i] = no + oo + j
        io += len(st.ins)
        oo += len(st.outs)
    dma = pltpu.SemaphoreType.DMA
    sems = []
    for st in stages:
        sems += [dma((max(st.n_remote, 1),)), dma((max(st.n_remote, 1),)), dma((max(st.n_local, 1),))]

    def wrapped(*refs):
        ins, cin = refs[:ni], refs[ni:ni + nci]
        outs, cout = refs[ni + nci:ni + nci + no], refs[ni + nci + no:ni + nci + no + nco]
        scr = refs[ni + nci + no + nco:ni + nci + no + nco + ns]
        sem_refs = refs[ni + nci + no + nco + ns:]
        first = pl.program_id(0) == 0
        last = pl.program_id(0) == grid[0] - 1
        for ax in range(1, len(grid)):
            first = jnp.logical_and(first, pl.program_id(ax) == 0)
            last = jnp.logical_and(last, pl.program_id(ax) == grid[ax] - 1)

        def descriptors():
            ds, io, oo = [], 0, 0
            for si, st in enumerate(stages):
                ds += st.descriptors(cin[io:io + len(st.ins)], cout[oo:oo + len(st.outs)], *sem_refs[3 * si:3 * si + 3])
                io += len(st.ins)
                oo += len(st.outs)
            return ds

        @pl.when(first)
        def _():
            for d in descriptors():
                d.start()

        body(*ins, *outs, *scr)

        @pl.when(last)
        def _():
            for d in descriptors():
                d.wait()

    res = pl.pallas_call(
        wrapped, name=name, grid=grid, in_specs=list(in_specs) + [_ANY] * nci,
        out_specs=list(out_specs) + [_ANY] * nco, out_shape=list(out_shape) + c_outs,
        scratch_shapes=scratch_shapes + sems, input_output_aliases=aliases,
        compiler_params=_params(*(("arbitrary",) * len(grid))))(*args, *c_ins)
    oo = no
    for st in stages:
        st.results = list(res[oo:oo + len(st.outs)])
        oo += len(st.outs)
    return list(res[:no])


def _dot(a, b, dims):
    return lax.dot_general(a.astype(BF16), b.astype(BF16), (dims, ((), ())), preferred_element_type=F32)


_MAX_ROW_BLOCK = 1024
_ATT_BLOCK = 512


_MAX_REDUCE_BLOCK = 2048


def _row_block(s, want=1024):
    return min(want, s, _MAX_ROW_BLOCK)


def _reduce_block(s):
    return min(s, _MAX_REDUCE_BLOCK)


def _matmul(name, grid, terms, extras, outs, epilogue, acc_shape, fill=None):
    nt, ne, no = len(terms), len(extras), len(outs)
    nk = grid[-1]
    dims = [t[4] for t in terms]

    def body(*refs):
        a_refs, b_refs = refs[:nt], refs[nt:2 * nt]
        e_refs = refs[2 * nt:2 * nt + ne]
        first_out = 2 * nt + ne + (fill is not None)
        o_refs = refs[first_out:first_out + no]

        def finish(acc):
            vals = epilogue(acc, *[e[...] for e in e_refs])
            for o, val in zip(o_refs, vals):
                o[...] = val.astype(o.dtype)

        if nk == 1:
            part = None
            for a, b, d in zip(a_refs, b_refs, dims):
                t = _dot(a[...], b[...], d)
                part = t if part is None else part + t
            finish(part)
        else:
            acc_ref = refs[-1]
            k = pl.program_id(len(grid) - 1)

            @pl.when(k == 0)
            def _():
                acc_ref[...] = jnp.zeros_like(acc_ref)

            for a, b, d in zip(a_refs, b_refs, dims):
                acc_ref[...] += _dot(a[...], b[...], d)

            @pl.when(k == nk - 1)
            def _():
                finish(acc_ref[...])

    in_specs = [t[1] for t in terms] + [t[3] for t in terms] + [e[1] for e in extras]
    args = [t[0] for t in terms] + [t[2] for t in terms] + [e[0] for e in extras]
    sem = ("parallel",) * (len(grid) - 1) + ("arbitrary",)
    aliases = None
    if fill is not None:
        assert ne == 0
        in_specs, args, aliases = in_specs + [_ANY], args + [fill], {len(in_specs): 0}
    return _call(
        body, name=name, grid=grid, in_specs=in_specs,
        out_specs=[o[1] for o in outs], out_shape=[o[0] for o in outs],
        scratch_shapes=[pltpu.VMEM(acc_shape, F32)] if nk > 1 else [], sem=sem, args=args, aliases=aliases)


def _ident(acc):
    return (acc,)


def _rmsnorm_fwd(name, x, gain, width, col_block=0):
    s = x.shape[0]
    bm = _row_block(s)

    def body(x_ref, g_ref, o_ref):
        xf = x_ref[...]
        r = lax.rsqrt(jnp.mean(xf * xf, axis=-1, keepdims=True) + RMS_EPS)
        o_ref[...] = ((xf * r) * g_ref[...]).astype(o_ref.dtype)

    return pl.pallas_call(
        body, name=name, grid=(s // bm,),
        in_specs=[pl.BlockSpec((bm, width), lambda i: (i, col_block)), pl.BlockSpec((1, width), lambda i: (0, 0))],
        out_specs=pl.BlockSpec((bm, width), lambda i: (i, 0)),
        out_shape=jax.ShapeDtypeStruct((s, width), BF16),
        compiler_params=_params("parallel"),
    )(x, gain)


def _rms_bwd_math(dy, xf, g, width):
    r = lax.rsqrt(jnp.mean(xf * xf, axis=-1, keepdims=True) + RMS_EPS)
    dyg = dy * g
    dot = jnp.sum(dyg * xf, axis=-1, keepdims=True)
    dx = r * dyg - xf * ((r * r * r) * (dot * (1.0 / width)))
    dgain = jnp.sum(dy * (xf * r), axis=0, keepdims=True)
    return dx, dgain


def _rmsnorm_bwd(name, dy, x, gain, width, col_block=0, dres=None, out_dtype=F32):
    s = x.shape[0]
    bm = _row_block(s)
    has_res = dres is not None

    def body(*refs):
        if has_res:
            dy_ref, x_ref, g_ref, r_ref, dx_ref, dg_ref, dxb_ref = refs
        else:
            dy_ref, x_ref, g_ref, dx_ref, dg_ref = refs
        dx, dgain = _rms_bwd_math(dy_ref[...].astype(F32), x_ref[...], g_ref[...], width)
        if has_res:
            dx = dx + r_ref[...]
            dxb_ref[...] = dx.astype(BF16)
        dx_ref[...] = dx.astype(dx_ref.dtype)

        @pl.when(pl.program_id(0) == 0)
        def _():
            dg_ref[...] = dgain

        @pl.when(pl.program_id(0) > 0)
        def _():
            dg_ref[...] += dgain

    row = pl.BlockSpec((bm, width), lambda i: (i, 0))
    in_specs = [row, pl.BlockSpec((bm, width), lambda i: (i, col_block)), pl.BlockSpec((1, width), lambda i: (0, 0))]
    args = [dy, x, gain]
    out_specs = [row, pl.BlockSpec((1, width), lambda i: (0, 0))]
    out_shape = [jax.ShapeDtypeStruct((s, width), out_dtype), jax.ShapeDtypeStruct((1, width), F32)]
    if has_res:
        in_specs.append(row)
        args.append(dres)
        out_specs.append(row)
        out_shape.append(jax.ShapeDtypeStruct((s, width), BF16))
    return _call(body, name=name, grid=(s // bm,), in_specs=in_specs, out_specs=out_specs, out_shape=out_shape,
                 sem=("arbitrary",), args=args)


def _loss_and_final_norm(h, gain, target):
    s, d = h.shape
    bm = _row_block(s, 512)

    def body(h_ref, g_ref, t_ref, dh_ref, dhb_ref, loss_ref, dg_ref):
        xf = h_ref[...]
        g = g_ref[...]
        r = lax.rsqrt(jnp.mean(xf * xf, axis=-1, keepdims=True) + RMS_EPS)
        err = (xf * r) * g - t_ref[...]
        part = 0.5 * jnp.sum(jnp.mean(err * err, axis=-1, keepdims=True), axis=0, keepdims=True)
        dx, dgain = _rms_bwd_math(err * (1.0 / d), xf, g, d)
        dh_ref[...] = dx
        dhb_ref[...] = dx.astype(BF16)

        @pl.when(pl.program_id(0) == 0)
        def _():
            dg_ref[...] = dgain
            loss_ref[...] = jnp.broadcast_to(part, loss_ref.shape)

        @pl.when(pl.program_id(0) > 0)
        def _():
            dg_ref[...] += dgain
            loss_ref[...] += jnp.broadcast_to(part, loss_ref.shape)

    row = pl.BlockSpec((bm, d), lambda i: (i, 0))
    vec = pl.BlockSpec((1, d), lambda i: (0, 0))
    return pl.pallas_call(
        body, name="loss_final_norm", grid=(s // bm,), in_specs=[row, vec, row],
        out_specs=[row, row, pl.BlockSpec((1, 128), lambda i: (0, 0)), vec],
        out_shape=[jax.ShapeDtypeStruct((s, d), F32), jax.ShapeDtypeStruct((s, d), BF16),
                   jax.ShapeDtypeStruct((1, 128), F32),
                   jax.ShapeDtypeStruct((1, d), F32)],
        compiler_params=_params("arbitrary"),
    )(h, gain, target)


def _ffn_up(name, n, wg, wu):
    s = n.shape[0]
    bm = _row_block(s)

    def body(n_ref, wg_ref, wu_ref, a_ref, dadu_ref, dadg_ref):
        x = n_ref[...]
        g = _dot(x, wg_ref[...], NT)
        u = _dot(x, wu_ref[...], NT)
        sg = jax.nn.sigmoid(g)
        silu = g * sg
        a_ref[...] = (silu * u).astype(BF16)
        dadu_ref[...] = silu.astype(BF16)
        dadg_ref[...] = (u * (sg * (1.0 + g * (1.0 - sg)))).astype(BF16)

    w_spec = pl.BlockSpec((None, FF_SHARD, D_MODEL), lambda j, i: (j, 0, 0))
    o_spec = pl.BlockSpec((None, bm, FF_SHARD), lambda j, i: (j, i, 0))
    shp = jax.ShapeDtypeStruct((N_CHIPS, s, FF_SHARD), BF16)
    return _call(
        body, name=name, grid=(N_CHIPS, s // bm),
        in_specs=[pl.BlockSpec((bm, D_MODEL), lambda j, i: (i, 0)), w_spec, w_spec],
        out_specs=[o_spec, o_spec, o_spec], out_shape=[shp, shp, shp],
        sem=("parallel", "parallel"), args=[n, wg, wu])


def _residual_epilogue(alpha, with_norm):
    if not with_norm:
        return lambda acc, r: (r + alpha * acc,)

    def epilogue(acc, r, g):
        h = r + alpha * acc
        rs = lax.rsqrt(jnp.mean(h * h, axis=-1, keepdims=True) + RMS_EPS)
        return h, (h * rs) * g

    return epilogue


def _residual_outs(s, bm, gain):
    row = pl.BlockSpec((bm, D_MODEL), lambda i, k: (i, 0))
    outs = [(jax.ShapeDtypeStruct((s, D_MODEL), F32), row)]
    if gain is None:
        return [], outs
    return [(gain, pl.BlockSpec((1, D_MODEL), lambda i, k: (0, 0)))], outs + [(jax.ShapeDtypeStruct((s, D_MODEL), BF16), row)]


def _ffn_down(name, a, wd, res, gain=None):
    s = a.shape[1]
    bm = _row_block(s, 512)
    row = pl.BlockSpec((bm, D_MODEL), lambda i, k: (i, 0))
    terms = [(a, pl.BlockSpec((None, bm, FF_SHARD), lambda i, k, j=j: (j, i, 0)),
              wd, pl.BlockSpec((None, FF_SHARD, D_MODEL), lambda i, k, j=j: (j, 0, 0)), NN) for j in range(N_CHIPS)]
    extras, outs = _residual_outs(s, bm, gain)
    res_out = _matmul(name, (s // bm, 1), terms, [(res, row)] + extras, outs,
                      _residual_epilogue(0.5, gain is not None), None)
    return res_out if gain is not None else res_out[0]


def _ffn_bwd(tag, dh, n, dadg, dadu, a, wg, wu, wd, grads):
    s = dh.shape[0]
    bm = _row_block(s)
    bk = _reduce_block(s)
    nk = s // bk

    def act_bwd(acc, dg_da, du_da):
        da = 0.5 * acc
        return da * dg_da.astype(F32), da * du_da.astype(F32)

    slab = pl.BlockSpec((None, bm, FF_SHARD), lambda j, i, k: (j, i, 0))
    shp = jax.ShapeDtypeStruct((N_CHIPS, s, FF_SHARD), BF16)
    dg, du = _matmul(
        tag + "_dact", (N_CHIPS, s // bm, 1),
        [(dh, pl.BlockSpec((bm, D_MODEL), lambda j, i, k: (i, 0)),
          wd, pl.BlockSpec((None, FF_SHARD, D_MODEL), lambda j, i, k: (j, 0, 0)), NT)],
        [(dadg, slab), (dadu, slab)], [(shp, slab), (shp, slab)], act_bwd, None)

    grads[tag + "_w_down"] = _matmul(
        tag + "_dwd", (N_CHIPS, nk),
        [(a, pl.BlockSpec((None, bk, FF_SHARD), lambda j, k: (j, k, 0)),
          dh, pl.BlockSpec((bk, D_MODEL), lambda j, k: (k, 0)), TN)],
        [], [(jax.ShapeDtypeStruct((N_CHIPS, FF_SHARD, D_MODEL), BF16),
              pl.BlockSpec((None, FF_SHARD, D_MODEL), lambda j, k: (j, 0, 0)))],
        lambda acc: (0.5 * acc,), (FF_SHARD, D_MODEL))[0]

    def dw_up(nm, dact):
        return _matmul(
            nm, (N_CHIPS, nk),
            [(dact, pl.BlockSpec((None, bk, FF_SHARD), lambda j, k: (j, k, 0)),
              n, pl.BlockSpec((bk, D_MODEL), lambda j, k: (k, 0)), TN)],
            [], [(jax.ShapeDtypeStruct((N_CHIPS, FF_SHARD, D_MODEL), BF16),
                  pl.BlockSpec((None, FF_SHARD, D_MODEL), lambda j, k: (j, 0, 0)))],
            _ident, (FF_SHARD, D_MODEL))[0]

    grads[tag + "_w_gate"] = dw_up(tag + "_dwg", dg)
    grads[tag + "_w_up"] = dw_up(tag + "_dwu", du)

    bn = _row_block(s, 512)
    steps = s // bn // 2
    dn = None
    for part, off in (("_dn_a", 0), ("_dn_b", steps)):
        row = pl.BlockSpec((bn, D_MODEL), lambda i, k, off=off: (i + off, 0))
        terms = []
        for j in range(N_CHIPS):
            a_slab = pl.BlockSpec((None, bn, FF_SHARD), lambda i, k, j=j, off=off: (j, i + off, 0))
            w_slab = pl.BlockSpec((None, FF_SHARD, D_MODEL), lambda i, k, j=j: (j, 0, 0))
            terms += [(dg, a_slab, wg, w_slab, NN), (du, a_slab, wu, w_slab, NN)]
        dn = _matmul(tag + part, (steps, 1), terms, [], [(jax.ShapeDtypeStruct((s, D_MODEL), F32), row)],
                     _ident, None, fill=dn)[0]
    return dn


def _mm_nn(name, a, b, out_dtype, res=None, gain=None):
    s, k = a.shape
    nn = b.shape[1]
    bm = _row_block(s)
    row = pl.BlockSpec((bm, nn), lambda i, kk: (i, 0))
    term = [(a, pl.BlockSpec((bm, k), lambda i, kk: (i, 0)), b, pl.BlockSpec((k, nn), lambda i, kk: (0, 0)), NN)]
    if res is None:
        return _matmul(name, (s // bm, 1), term, [], [(jax.ShapeDtypeStruct((s, nn), out_dtype), row)], _ident, None)[0]
    extras, outs = _residual_outs(s, bm, gain)
    res_out = _matmul(name, (s // bm, 1), term, [(res, row)] + extras, outs,
                      _residual_epilogue(1.0, gain is not None), None)
    return res_out if gain is not None else res_out[0]


def _mm_nt(name, a, b, out_dtype, attn_out=None, nh=0, dv=0):
    s, nn = a.shape
    k = b.shape[0]
    bm = _row_block(s)
    term = [(a, pl.BlockSpec((bm, nn), lambda i, kk: (i, 0)), b, pl.BlockSpec((k, nn), lambda i, kk: (0, 0)), NT)]
    out = (jax.ShapeDtypeStruct((s, k), out_dtype), pl.BlockSpec((bm, k), lambda i, kk: (i, 0)))
    if attn_out is None:
        return _matmul(name, (s // bm, 1), term, [], [out], _ident, None)[0]

    def with_delta(acc, o):
        do = acc.astype(out_dtype).astype(F32)
        cols = [jnp.sum(do[:, h * dv:(h + 1) * dv] * o[:, h * dv:(h + 1) * dv].astype(F32), axis=-1, keepdims=True)
                for h in range(nh)]
        return acc, jnp.stack(cols, axis=0)

    return _matmul(
        name, (s // bm, 1), term, [(attn_out, pl.BlockSpec((bm, nh * dv), lambda i, kk: (i, 0)))],
        [out, (jax.ShapeDtypeStruct((nh, s, 1), F32), pl.BlockSpec((nh, bm, 1), lambda i, kk: (0, i, 0)))],
        with_delta, None)


def _mm_tn(name, a, b, out_dtype=BF16):
    s, k = a.shape
    nn = b.shape[1]
    bk = _reduce_block(s)
    return _matmul(
        name, (s // bk,),
        [(a, pl.BlockSpec((bk, k), lambda kk: (kk, 0)), b, pl.BlockSpec((bk, nn), lambda kk: (kk, 0)), TN)],
        [], [(jax.ShapeDtypeStruct((k, nn), out_dtype), pl.BlockSpec((k, nn), lambda kk: (0, 0)))],
        _ident, (k, nn))[0]


def _mm_heads_fwd(name, a, w, out_dtype, w_transposed=False):
    s, k = a.shape
    nh = w.shape[0]
    nn = w.shape[1] if w_transposed else w.shape[2]
    bm = _row_block(s)
    return _matmul(
        name, (nh, s // bm, 1),
        [(a, pl.BlockSpec((bm, k), lambda h, i, kk: (i, 0)),
          w, pl.BlockSpec((None,) + w.shape[1:], lambda h, i, kk: (h, 0, 0)), NT if w_transposed else NN)],
        [], [(jax.ShapeDtypeStruct((s, nh * nn), out_dtype), pl.BlockSpec((bm, nn), lambda h, i, kk: (i, h)))],
        _ident, None)[0]


def _mm_heads_bwd(name, dy, a, w, w_transposed=False):
    s, k = a.shape
    nh = w.shape[0]
    nn = w.shape[1] if w_transposed else w.shape[2]
    bm = _row_block(s)
    bk = _reduce_block(s)
    w_spec = pl.BlockSpec((None,) + w.shape[1:], lambda i, h: (h, 0, 0))
    da = _matmul(
        name + "_dx", (s // bm, nh),
        [(dy, pl.BlockSpec((bm, nn), lambda i, h: (i, h)), w, w_spec, NN if w_transposed else NT)],
        [], [(jax.ShapeDtypeStruct((s, k), F32), pl.BlockSpec((bm, k), lambda i, h: (i, 0)))], _ident, (bm, k))[0]
    a_term = (a, pl.BlockSpec((bk, k), lambda h, kk: (kk, 0)))
    dy_term = (dy, pl.BlockSpec((bk, nn), lambda h, kk: (kk, h)))
    lhs, rhs = (dy_term, a_term) if w_transposed else (a_term, dy_term)
    dw = _matmul(
        name + "_dw", (nh, s // bk), [lhs + rhs + (TN,)],
        [], [(jax.ShapeDtypeStruct(w.shape, BF16), pl.BlockSpec((None,) + w.shape[1:], lambda h, kk: (h, 0, 0)))],
        _ident, w.shape[1:])[0]
    return da, dw


def _w_in_fwd(n, w_t):
    s = n.shape[0]
    bm = _row_block(s)
    nh, nout, kin = w_t.shape
    terms = [(n, pl.BlockSpec((bm, kin), lambda i, k, j=j: (i, j)),
              w_t, pl.BlockSpec((None, nout, kin), lambda i, k, j=j: (j, 0, 0)), NT) for j in range(nh)]
    row = pl.BlockSpec((bm, nout), lambda i, k: (i, 0))
    return _matmul("w_in", (s // bm, 1), terms, [], [(jax.ShapeDtypeStruct((s, nout), F32), row)], _ident, None)[0]


def _w_in_dw(dz, n):
    s, nout = dz.shape
    kin = n.shape[1] // N_CHIPS
    bk = _reduce_block(s)
    return _matmul(
        "w_in_dw", (N_CHIPS, s // bk),
        [(dz, pl.BlockSpec((bk, nout), lambda j, k: (k, 0)), n, pl.BlockSpec((bk, kin), lambda j, k: (k, j)), TN)],
        [], [(jax.ShapeDtypeStruct((N_CHIPS, nout, kin), BF16), pl.BlockSpec((None, nout, kin), lambda j, k: (j, 0, 0)))],
        _ident, (nout, kin))[0]


def _rope_tables(positions):
    half = ROPE_DIM // 2
    freqs = 1.0 / (ROPE_BASE ** (jnp.arange(0, ROPE_DIM, 2, dtype=F32) / ROPE_DIM))
    ang = positions.astype(F32)[:, None] * freqs
    cos, sin = jnp.cos(ang), jnp.sin(ang)
    z = jnp.zeros_like(cos)
    tc = jnp.concatenate([cos, cos, z, z], axis=-1)
    ta = jnp.concatenate([-sin, z, z, z], axis=-1)
    tb = jnp.concatenate([z, sin, z, z], axis=-1)
    assert tc.shape[-1] == 4 * half
    return tc, ta, tb


def _rope(x, tc, ta, tb):
    return x * tc + pltpu.roll(x, 96, 1) * ta + pltpu.roll(x, 32, 1) * tb


def _rope_t(dy, tc, ta, tb):
    return dy * tc + pltpu.roll(dy * ta, 32, 1) + pltpu.roll(dy * tb, 96, 1)


def _q_rope(q, tc, ta, tb, transpose):
    s = q.shape[0]
    bm = _row_block(s, 512)
    rot = _rope_t if transpose else _rope

    def body(q_ref, tc_ref, ta_ref, tb_ref, o_ref):
        c, a, b = tc_ref[...], ta_ref[...], tb_ref[...]
        for h in range(MLA_HEADS):
            lo = h * HEAD_QK
            o_ref[:, lo:lo + 128] = q_ref[:, lo:lo + 128].astype(BF16)
            o_ref[:, lo + 128:lo + 256] = rot(q_ref[:, lo + 128:lo + 256], c, a, b).astype(BF16)

    row = pl.BlockSpec((bm, MLA_HEADS * HEAD_QK), lambda i: (i, 0))
    tab = pl.BlockSpec((bm, 128), lambda i: (i, 0))
    return pl.pallas_call(
        body, name="q_rope_t" if transpose else "q_rope", grid=(s // bm,), in_specs=[row, tab, tab, tab],
        out_specs=row, out_shape=jax.ShapeDtypeStruct((s, MLA_HEADS * HEAD_QK), BF16),
        compiler_params=_params("parallel"),
    )(q, tc, ta, tb)


def _kv_assemble(kv, z, tc, ta, tb):
    s = kv.shape[0]
    bm = _row_block(s, 512)

    def body(kv_ref, kr_ref, tc_ref, ta_ref, tb_ref, k_ref, v_ref):
        kpe = _rope(kr_ref[...], tc_ref[...], ta_ref[...], tb_ref[...]).astype(BF16)
        for h in range(MLA_HEADS):
            lo = h * 256
            k_ref[:, lo:lo + 128] = kv_ref[:, lo:lo + 128].astype(BF16)
            k_ref[:, lo + 128:lo + 256] = kpe
            v_ref[:, h * 128:(h + 1) * 128] = kv_ref[:, lo + 128:lo + 256].astype(BF16)

    row = pl.BlockSpec((bm, 1024), lambda i: (i, 0))
    tab = pl.BlockSpec((bm, 128), lambda i: (i, 0))
    return pl.pallas_call(
        body, name="kv_assemble", grid=(s // bm,),
        in_specs=[row, pl.BlockSpec((bm, 128), lambda i: (i, 3)), tab, tab, tab],
        out_specs=[row, pl.BlockSpec((bm, 512), lambda i: (i, 0))],
        out_shape=[jax.ShapeDtypeStruct((s, 1024), BF16), jax.ShapeDtypeStruct((s, 512), BF16)],
        compiler_params=_params("parallel"),
    )(kv, z, tc, ta, tb)


def _kv_assemble_bwd(dk, dv, tc, ta, tb):
    s = dk.shape[0]
    bm = _row_block(s, 512)

    def body(dk_ref, dv_ref, tc_ref, ta_ref, tb_ref, dkv_ref, dkr_ref):
        dpe = None
        for h in range(MLA_HEADS):
            lo = h * 256
            dkv_ref[:, lo:lo + 128] = dk_ref[:, lo:lo + 128].astype(BF16)
            dkv_ref[:, lo + 128:lo + 256] = dv_ref[:, h * 128:(h + 1) * 128].astype(BF16)
            t = dk_ref[:, lo + 128:lo + 256]
            dpe = t if dpe is None else dpe + t
        dkr_ref[...] = _rope_t(dpe, tc_ref[...], ta_ref[...], tb_ref[...])

    row = pl.BlockSpec((bm, 1024), lambda i: (i, 0))
    tab = pl.BlockSpec((bm, 128), lambda i: (i, 0))
    return pl.pallas_call(
        body, name="kv_assemble_bwd", grid=(s // bm,),
        in_specs=[row, pl.BlockSpec((bm, 512), lambda i: (i, 0)), tab, tab, tab],
        out_specs=[row, tab],
        out_shape=[jax.ShapeDtypeStruct((s, 1024), BF16), jax.ShapeDtypeStruct((s, 128), F32)],
        compiler_params=_params("parallel"),
    )(dk, dv, tc, ta, tb)


def _norm_bf16(x, g):
    r = lax.rsqrt(jnp.mean(x * x, axis=-1, keepdims=True) + RMS_EPS)
    return ((x * r) * g).astype(BF16)


def _qkv_prep(z, q_gain, kv_gain, wq_t, wkv, tc, ta, tb):
    s = z.shape[0]
    bm = _row_block(s, 512)

    def body(zq_ref, zkv_ref, zkr_ref, qg_ref, kvg_ref, wq_ref, wkv_ref, tc_ref, ta_ref, tb_ref,
             qn_ref, kvn_ref, q_ref, k_ref, v_ref):
        c, a, b = tc_ref[...], ta_ref[...], tb_ref[...]
        qn = _norm_bf16(zq_ref[...], qg_ref[...])
        kvn = _norm_bf16(zkv_ref[...], kvg_ref[...])
        qn_ref[...] = qn
        kvn_ref[...] = kvn
        kpe = _rope(zkr_ref[...], c, a, b).astype(BF16)
        for h in range(MLA_HEADS):
            lo = h * HEAD_QK
            qp = _dot(qn, wq_ref[h], NT)
            q_ref[:, lo:lo + 128] = qp[:, :128].astype(BF16)
            q_ref[:, lo + 128:lo + 256] = _rope(qp[:, 128:], c, a, b).astype(BF16)
            kv = _dot(kvn, wkv_ref[h], NN)
            k_ref[:, lo:lo + 128] = kv[:, :128].astype(BF16)
            k_ref[:, lo + 128:lo + 256] = kpe
            v_ref[:, h * HEAD_V:(h + 1) * HEAD_V] = kv[:, 128:].astype(BF16)

    def cols(width, blk):
        return pl.BlockSpec((bm, width), lambda i: (i, blk))

    def whole(a):
        return pl.BlockSpec(a.shape, lambda i: (0,) * a.ndim)

    tab = cols(128, 0)
    return pl.pallas_call(
        body, name="qkv_prep", grid=(s // bm,),
        in_specs=[cols(Q_LORA, 0), cols(KV_LORA, 2), cols(128, 3), whole(q_gain), whole(kv_gain), whole(wq_t),
                  whole(wkv), tab, tab, tab],
        out_specs=[cols(Q_LORA, 0), cols(KV_LORA, 0), cols(1024, 0), cols(1024, 0), cols(512, 0)],
        out_shape=[jax.ShapeDtypeStruct((s, Q_LORA), BF16), jax.ShapeDtypeStruct((s, KV_LORA), BF16),
                   jax.ShapeDtypeStruct((s, 1024), BF16), jax.ShapeDtypeStruct((s, 1024), BF16),
                   jax.ShapeDtypeStruct((s, 512), BF16)],
        compiler_params=_params("parallel"),
    )(z, z, z, q_gain, kv_gain, wq_t, wkv, tc, ta, tb)


def _qkv_prep_bwd(dq, dk, dv, z, qn, kvn, q_gain, kv_gain, wq_t, wkv, tc, ta, tb):
    s = z.shape[0]
    bm = _row_block(s, 512)
    nsteps = s // bm

    def body(dq_ref, dk_ref, dv_ref, zq_ref, zkv_ref, qn_ref, kvn_ref, qg_ref, kvg_ref, wq_ref, wkv_ref,
             tc_ref, ta_ref, tb_ref, dz_ref, dqg_ref, dkvg_ref, dwq_ref, dwkv_ref, wq_acc, wkv_acc):
        i = pl.program_id(0)
        c, a, b = tc_ref[...], ta_ref[...], tb_ref[...]

        @pl.when(i == 0)
        def _():
            wq_acc[...] = jnp.zeros_like(wq_acc)
            wkv_acc[...] = jnp.zeros_like(wkv_acc)

        qn, kvn = qn_ref[...], kvn_ref[...]
        dqn = jnp.zeros((bm, Q_LORA), F32)
        dkvn = jnp.zeros((bm, KV_LORA), F32)
        dpe = jnp.zeros((bm, 128), F32)
        for h in range(MLA_HEADS):
            lo = h * HEAD_QK
            dqp = jnp.concatenate([dq_ref[:, lo:lo + 128].astype(BF16),
                                   _rope_t(dq_ref[:, lo + 128:lo + 256], c, a, b).astype(BF16)], axis=1)
            dqn = dqn + _dot(dqp, wq_ref[h], NN)
            wq_acc[h] += _dot(dqp, qn, TN)
            dkv = jnp.concatenate([dk_ref[:, lo:lo + 128].astype(BF16),
                                   dv_ref[:, h * HEAD_V:(h + 1) * HEAD_V].astype(BF16)], axis=1)
            dkvn = dkvn + _dot(dkv, wkv_ref[h], NT)
            wkv_acc[h] += _dot(kvn, dkv, TN)
            dpe = dpe + dk_ref[:, lo + 128:lo + 256]
        dcq, dqg = _rms_bwd_math(dqn, zq_ref[...], qg_ref[...], Q_LORA)
        dckv, dkvg = _rms_bwd_math(dkvn, zkv_ref[...], kvg_ref[...], KV_LORA)
        dz_ref[:, 0:Q_LORA] = dcq.astype(BF16)
        dz_ref[:, Q_LORA:Q_LORA + KV_LORA] = dckv.astype(BF16)
        dz_ref[:, Q_LORA + KV_LORA:512] = _rope_t(dpe, c, a, b).astype(BF16)

        @pl.when(i == 0)
        def _():
            dqg_ref[...] = dqg
            dkvg_ref[...] = dkvg

        @pl.when(i > 0)
        def _():
            dqg_ref[...] += dqg
            dkvg_ref[...] += dkvg

        @pl.when(i == nsteps - 1)
        def _():
            dwq_ref[...] = wq_acc[...].astype(BF16)
            dwkv_ref[...] = wkv_acc[...].astype(BF16)

    def cols(width, blk):
        return pl.BlockSpec((bm, width), lambda i: (i, blk))

    def whole(shape):
        return pl.BlockSpec(shape, lambda i: (0,) * len(shape))

    tab = cols(128, 0)
    return _call(
        body, name="qkv_prep_bwd", grid=(nsteps,),
        in_specs=[cols(1024, 0), cols(1024, 0), cols(512, 0), cols(Q_LORA, 0), cols(KV_LORA, 2), cols(Q_LORA, 0),
                  cols(KV_LORA, 0), whole(q_gain.shape), whole(kv_gain.shape), whole(wq_t.shape), whole(wkv.shape),
                  tab, tab, tab],
        out_specs=[cols(512, 0), whole(q_gain.shape), whole(kv_gain.shape), whole(wq_t.shape), whole(wkv.shape)],
        out_shape=[jax.ShapeDtypeStruct((s, 512), BF16), jax.ShapeDtypeStruct(q_gain.shape, F32),
                   jax.ShapeDtypeStruct(kv_gain.shape, F32), jax.ShapeDtypeStruct(wq_t.shape, BF16),
                   jax.ShapeDtypeStruct(wkv.shape, BF16)],
        scratch_shapes=[pltpu.VMEM(wq_t.shape, F32), pltpu.VMEM(wkv.shape, F32)],
        sem=("arbitrary",), args=[dq, dk, dv, z, z, qn, kvn, q_gain, kv_gain, wq_t, wkv, tc, ta, tb])


def _causal_mask(s, row0, col0):
    rows = row0 + lax.broadcasted_iota(jnp.int32, s.shape, 0)
    cols = col0 + lax.broadcasted_iota(jnp.int32, s.shape, 1)
    return jnp.where(cols <= rows, s, -jnp.inf)


def _attn_fwd(name, q, k, k_off, v, v_off, nh, dq, dv, scale, causal, blk):
    sq, sk = q.shape[0], k.shape[0]
    bq = min(blk, sq)
    bk = min(blk, sk)
    nkv = sk // bk
    assert not causal or (sq == sk and bq == bk)

    hq = bq
    log2e = 1.4426950408889634
    c2 = scale * log2e

    def body(q_ref, k_ref, v_ref, o_ref, lse_ref):
        qi = pl.program_id(1)
        qs = (q_ref[...],)

        def step(j, carry, masked):
            rows = pl.ds(pl.multiple_of(j * bk, bk), bk)
            kb, vb = k_ref[rows, :], v_ref[rows, :]
            out = []
            for t, (m, l, acc) in enumerate(carry):
                s = _dot(qs[t], kb, NT) * c2
                if masked:
                    s = _causal_mask(s, qi * bq + t * hq, j * bk)
                m_new = jnp.maximum(m, jnp.max(s, axis=-1, keepdims=True))
                alpha = jnp.exp2(m - m_new)
                p = jnp.exp2(s - m_new)
                l = alpha * l + jnp.sum(p, axis=-1, keepdims=True)
                acc = alpha * acc + _dot(p, vb, NN)
                out.append((m_new, l, acc))
            return tuple(out)

        one = (jnp.full((hq, 1), -jnp.inf, F32), jnp.zeros((hq, 1), F32), jnp.zeros((hq, dv), F32))
        init = (one,)
        if causal:
            carry = lax.fori_loop(0, qi, lambda j, c: step(j, c, False), init)
            fin = step(qi, carry, True)
        else:
            fin = lax.fori_loop(0, nkv, lambda j, c: step(j, c, False), init)
        for t, (m, l, acc) in enumerate(fin):
            o_ref[t * hq:(t + 1) * hq, :] = (acc / l).astype(o_ref.dtype)
            lse_ref[t * hq:(t + 1) * hq, :] = m * (1.0 / log2e) + jnp.log(l)

    return _call(
        body, name=name, grid=(nh, sq // bq),
        in_specs=[pl.BlockSpec((bq, dq), lambda h, i: (i, h)),
                  pl.BlockSpec((sk, dq), lambda h, i: (0, k_off + h)),
                  pl.BlockSpec((sk, dv), lambda h, i: (0, v_off + h))],
        out_specs=[pl.BlockSpec((bq, dv), lambda h, i: (i, h)), pl.BlockSpec((None, bq, 1), lambda h, i: (h, i, 0))],
        out_shape=[jax.ShapeDtypeStruct((sq, nh * dv), BF16), jax.ShapeDtypeStruct((nh, sq, 1), F32)],
        sem=("parallel", "parallel"), args=[q, k, v])


def _attn_delta(name, do, do_off, o, nh, dv):
    s = o.shape[0]
    bm = _row_block(s, 512)

    def body(do_ref, o_ref, d_ref):
        d_ref[...] = jnp.sum(do_ref[...].astype(F32) * o_ref[...].astype(F32), axis=-1, keepdims=True)

    return pl.pallas_call(
        body, name=name, grid=(nh, s // bm),
        in_specs=[pl.BlockSpec((bm, dv), lambda h, i: (i, do_off + h)), pl.BlockSpec((bm, dv), lambda h, i: (i, h))],
        out_specs=pl.BlockSpec((None, bm, 1), lambda h, i: (h, i, 0)),
        out_shape=jax.ShapeDtypeStruct((nh, s, 1), F32),
        compiler_params=_params("parallel", "parallel"),
    )(do, o)


def _attn_bwd(name, q, k, k_off, v, v_off, do, do_off, lse, delta, nh, dq, dv, scale, causal, blk):
    sq, sk = q.shape[0], k.shape[0]
    bq = min(blk, sq)
    bk = min(blk, sk)
    nq = sq // bq
    assert not causal or (sq == sk and bq == bk)

    def body(q_ref, k_ref, v_ref, do_ref, lse_ref, dl_ref, dq_ref, dk_ref, dv_ref, dk_acc, dv_acc):
        j = pl.program_id(1)

        @pl.when(j == 0)
        def _():
            dq_ref[...] = jnp.zeros_like(dq_ref)

        dk_acc[...] = jnp.zeros_like(dk_acc)
        dv_acc[...] = jnp.zeros_like(dv_acc)
        kv = k_ref[...]
        vv = v_ref[...]

        def step(i, masked):
            rows = pl.ds(pl.multiple_of(i * bq, bq), bq)
            qv = q_ref[rows, :]
            dov = do_ref[rows, :].astype(BF16)
            s = _dot(qv, kv, NT) * scale
            if masked:
                s = _causal_mask(s, i * bq, j * bk)
            p = jnp.exp(s - lse_ref[rows, :])
            dp = _dot(dov, vv, NT)
            ds = (p * (dp - dl_ref[rows, :]) * scale).astype(BF16)
            dv_acc[...] += _dot(p, dov, TN)
            dk_acc[...] += _dot(ds, qv, TN)
            dq_ref[rows, :] += _dot(ds, kv, NN)

        if causal:
            step(j, True)

            def loop(i, c):
                step(i, False)
                return c

            lax.fori_loop(j + 1, nq, loop, 0)
        else:
            def loop(i, c):
                step(i, False)
                return c

            lax.fori_loop(0, nq, loop, 0)
        dk_ref[...] = dk_acc[...]
        dv_ref[...] = dv_acc[...]

    stat = pl.BlockSpec((None, sq, 1), lambda h, j: (h, 0, 0))
    return _call(
        body, name=name, grid=(nh, sk // bk),
        in_specs=[pl.BlockSpec((sq, dq), lambda h, j: (0, h)),
                  pl.BlockSpec((bk, dq), lambda h, j: (j, k_off + h)),
                  pl.BlockSpec((bk, dv), lambda h, j: (j, v_off + h)),
                  pl.BlockSpec((sq, dv), lambda h, j: (0, do_off + h)), stat, stat],
        out_specs=[pl.BlockSpec((sq, dq), lambda h, j: (0, h)),
                   pl.BlockSpec((bk, dq), lambda h, j: (j, h)),
                   pl.BlockSpec((bk, dv), lambda h, j: (j, h))],
        out_shape=[jax.ShapeDtypeStruct((sq, nh * dq), F32), jax.ShapeDtypeStruct((sk, nh * dq), F32),
                   jax.ShapeDtypeStruct((sk, nh * dv), F32)],
        scratch_shapes=[pltpu.VMEM((bk, dq), F32), pltpu.VMEM((bk, dv), F32)],
        sem=("parallel", "arbitrary"), args=[q, k, v, do, lse, delta])


def _pool_diff(z, g):
    s = z.shape[0]
    t = lax.broadcasted_iota(jnp.int32, z.shape, 0)
    acc = z
    sums = []
    for k in (1, 2, 4, 8):
        acc = acc + jnp.where(t >= k, pltpu.roll(acc, k, 0), 0.0)
        sums.append(acc)
    win = jnp.where(g == 0, sums[0], jnp.where(g == 1, sums[1], jnp.where(g == 2, sums[2], sums[3])))
    w = lax.shift_left(jnp.int32(2), g)
    count = jnp.minimum(t + 1, w).astype(F32)
    del s
    return win / count - z, count


def _pool_fwd(z, pool_w, pool_scale):
    s = z.shape[0]

    def body(z_ref, w_ref, sc_ref, o_ref):
        diff, _ = _pool_diff(z_ref[...], pl.program_id(0))
        o_ref[...] = (_dot(diff, w_ref[...], NN) * sc_ref[...]).astype(o_ref.dtype)

    return _call(
        body, name="pool_fwd", grid=(POOL_GROUPS,),
        in_specs=[pl.BlockSpec((s, POOL_CH), lambda g: (0, 4 + g)),
                  pl.BlockSpec((None, POOL_CH, POOL_CH), lambda g: (g, 0, 0)),
                  pl.BlockSpec((1, POOL_CH), lambda g: (0, g))],
        out_specs=[pl.BlockSpec((s, POOL_CH), lambda g: (0, g))],
        out_shape=[jax.ShapeDtypeStruct((s, POOL_GROUPS * POOL_CH), BF16)],
        sem=("parallel",), args=[z, pool_w, pool_scale])[0]


def _pool_bwd(dcat, z, pool_w, pool_scale):
    s = z.shape[0]

    def body(dp_ref, z_ref, w_ref, sc_ref, dz_ref, dw_ref, dsc_ref):
        g = pl.program_id(0)
        diff, count = _pool_diff(z_ref[...], g)
        dpf = dp_ref[...].astype(F32)
        u = _dot(diff, w_ref[...], NN)
        dsc_ref[...] = jnp.sum(dpf * u, axis=0, keepdims=True)
        du = (dpf * sc_ref[...]).astype(BF16)
        dw_ref[...] = _dot(diff, du, TN)
        ddiff = _dot(du, w_ref[...], NT)
        t = lax.broadcasted_iota(jnp.int32, ddiff.shape, 0)
        acc = ddiff / count
        sums = []
        for k in (1, 2, 4, 8):
            acc = acc + jnp.where(t < s - k, pltpu.roll(acc, s - k, 0), 0.0)
            sums.append(acc)
        win = jnp.where(g == 0, sums[0], jnp.where(g == 1, sums[1], jnp.where(g == 2, sums[2], sums[3])))
        dz_ref[...] = win - ddiff

    return pl.pallas_call(
        body, name="pool_bwd", grid=(POOL_GROUPS,),
        in_specs=[pl.BlockSpec((s, POOL_CH), lambda g: (0, 4 + g)),
                  pl.BlockSpec((s, POOL_CH), lambda g: (0, 4 + g)),
                  pl.BlockSpec((None, POOL_CH, POOL_CH), lambda g: (g, 0, 0)),
                  pl.BlockSpec((1, POOL_CH), lambda g: (0, g))],
        out_specs=[pl.BlockSpec((s, POOL_CH), lambda g: (0, g)),
                   pl.BlockSpec((None, POOL_CH, POOL_CH), lambda g: (g, 0, 0)),
                   pl.BlockSpec((1, POOL_CH), lambda g: (0, g))],
        out_shape=[jax.ShapeDtypeStruct((s, POOL_GROUPS * POOL_CH), F32),
                   jax.ShapeDtypeStruct((POOL_GROUPS, POOL_CH, POOL_CH), F32),
                   jax.ShapeDtypeStruct((1, POOL_GROUPS * POOL_CH), F32)],
        compiler_params=_params("parallel"),
    )(dcat, z, pool_w, pool_scale)


def _local_step(x, mem, positions, target, w, grads):
    tc, ta, tb = _rope_tables(positions)
    blk = _ATT_BLOCK

    n1 = _rmsnorm_fwd("ffn1_norm", x, w["ffn1_norm"], D_MODEL)
    a1, dadu1, dadg1 = _ffn_up("ffn1_up", n1, w["ffn1_w_gate"], w["ffn1_w_up"])
    h1, n2 = _ffn_down("ffn1_down", a1, w["ffn1_w_down"], x, w["mix_norm"])
    z = _w_in_fwd(n2, w["w_in"])
    qn, kvn, qf, kf, vf = _qkv_prep(z, w["q_norm"], w["kv_norm"], w["w_q_up"], w["w_kv_up"], tc, ta, tb)
    att, lse = _attn_fwd("mla_fwd", qf, kf, 0, vf, 0, MLA_HEADS, HEAD_QK, HEAD_V, MLA_SCALE, True, blk)
    pool = _pool_fwd(z, w["pool_w"], w["pool_scale"])
    s = x.shape[0]
    bm = _row_block(s)
    row = pl.BlockSpec((bm, D_MODEL), lambda i, k: (i, 0))
    half = pl.BlockSpec((bm, 512), lambda i, k: (i, 0))
    h2, n3 = _matmul(
        "w_out", (s // bm, 1),
        [(att, half, w["w_out"], pl.BlockSpec((512, D_MODEL), lambda i, k: (0, 0)), NN),
         (pool, half, w["w_out"], pl.BlockSpec((512, D_MODEL), lambda i, k: (1, 0)), NN)],
        [(h1, row)] + _residual_outs(s, bm, w["xattn_norm"])[0], _residual_outs(s, bm, w["xattn_norm"])[1],
        _residual_epilogue(1.0, True), None)
    memn = _rmsnorm_fwd("mem_norm", mem, w["mem_norm"], D_MODEL)
    qm = _mm_nn("w_mq", n3, w["w_mq"], BF16)
    kvm = _mm_heads_fwd("w_mkv", memn, w["w_mkv"], BF16)
    om, lse_m = _attn_fwd("xattn_fwd", qm, kvm, 0, kvm, MEM_HEADS, MEM_HEADS, MEM_HEAD_DIM, MEM_HEAD_DIM,
                          MEM_SCALE, False, blk)
    h3, n4 = _mm_nn("w_mo", om, w["w_mo"], F32, res=h2, gain=w["ffn2_norm"])
    a2, dadu2, dadg2 = _ffn_up("ffn2_up", n4, w["ffn2_w_gate"], w["ffn2_w_up"])
    h4 = _ffn_down("ffn2_down", a2, w["ffn2_w_down"], h3)

    dh4, dh4b, loss_vec, d_final = _loss_and_final_norm(h4, w["final_norm"], target)
    grads["final_norm"] = d_final

    dn4 = _ffn_bwd("ffn2", dh4b, n4, dadg2, dadu2, a2, w["ffn2_w_gate"], w["ffn2_w_up"], w["ffn2_w_down"], grads)
    dh3, grads["ffn2_norm"], dh3b = _rmsnorm_bwd("ffn2_norm_bwd", dn4, h3, w["ffn2_norm"], D_MODEL, dres=dh4)

    dom, delta_m = _mm_nt("w_mo_dx", dh3b, w["w_mo"], BF16, attn_out=om, nh=MEM_HEADS, dv=MEM_HEAD_DIM)
    grads["w_mo"] = _mm_tn("w_mo_dw", om, dh3b)
    dqm, dkm, dvm = _attn_bwd("xattn_bwd", qm, kvm, 0, kvm, MEM_HEADS, dom, 0, lse_m, delta_m, MEM_HEADS,
                              MEM_HEAD_DIM, MEM_HEAD_DIM, MEM_SCALE, False, blk)
    dkvm = jnp.concatenate([dkm, dvm], axis=1).astype(BF16)
    dn3 = _mm_nt("w_mq_dx", dqm, w["w_mq"], F32)
    grads["w_mq"] = _mm_tn("w_mq_dw", n3, dqm)
    dmemn, grads["w_mkv"] = _mm_heads_bwd("w_mkv", dkvm, memn, w["w_mkv"])
    _, grads["mem_norm"] = _rmsnorm_bwd("mem_norm_bwd", dmemn, mem, w["mem_norm"], D_MODEL, out_dtype=BF16)
    dh2, grads["xattn_norm"], dh2b = _rmsnorm_bwd("xattn_norm_bwd", dn3, h2, w["xattn_norm"], D_MODEL, dres=dh3)

    dcat, delta = _mm_nt("w_out_dx", dh2b, w["w_out"], BF16, attn_out=att, nh=MLA_HEADS, dv=HEAD_V)
    grads["w_out"] = jnp.concatenate([_mm_tn("w_out_dw_a", att, dh2b), _mm_tn("w_out_dw_p", pool, dh2b)], axis=0)
    dzp, grads["pool_w"], grads["pool_scale"] = _pool_bwd(dcat, z, w["pool_w"], w["pool_scale"])
    dqf, dkf, dvf = _attn_bwd("mla_bwd", qf, kf, 0, vf, 0, dcat, 0, lse, delta, MLA_HEADS, HEAD_QK, HEAD_V,
                              MLA_SCALE, True, blk)
    dz_lat, grads["q_norm"], grads["kv_norm"], grads["w_q_up"], grads["w_kv_up"] = _qkv_prep_bwd(
        dqf, dkf, dvf, z, qn, kvn, w["q_norm"], w["kv_norm"], w["w_q_up"], w["w_kv_up"], tc, ta, tb)
    dz = jnp.concatenate([dz_lat, dzp.astype(BF16)], axis=1)
    dn2 = _mm_heads_fwd("w_in_dx", dz, w["w_in"], F32)
    grads["w_in"] = _w_in_dw(dz, n2)
    dh1, grads["mix_norm"], dh1b = _rmsnorm_bwd("mix_norm_bwd", dn2, h1, w["mix_norm"], D_MODEL, dres=dh2)

    dn1 = _ffn_bwd("ffn1", dh1b, n1, dadg1, dadu1, a1, w["ffn1_w_gate"], w["ffn1_w_up"], w["ffn1_w_down"], grads)
    dx, grads["ffn1_norm"], _ = _rmsnorm_bwd("ffn1_norm_bwd", dn1, x, w["ffn1_norm"], D_MODEL, dres=dh1)
    return loss_vec[0, 0], dx


def _mesh_pos():
    x, y, c = lax.axis_index("x"), lax.axis_index("y"), lax.axis_index("c")
    chips = [(1 - x, y), (x, 1 - y), (1 - x, 1 - y)]
    chip_ids = [2 * cx + cy for cx, cy in chips]
    return x, y, c, 2 * x + y, chips, chip_ids


def _half_rows(c, rows):
    hr = rows // 2
    return pl.ds(pl.multiple_of(c * hr, 16), hr), pl.ds(pl.multiple_of((1 - c) * hr, 16), hr)


def _ag_ici_stage(shards):
    n = len(shards)

    def copies(ins, outs):
        x, y, c, me, chips, _ = _mesh_pos()
        out = []
        for k in range(n):
            mine, _ = _half_rows(c, ins[k].shape[0])
            out.append((ins[k], outs[k].at[me], None))
            for cx, cy in chips:
                out.append((ins[k].at[mine], outs[k].at[me, mine], (cx, cy, c)))
        return out

    return _Stage(shards, [jax.ShapeDtypeStruct((N_CHIPS,) + s.shape, s.dtype) for s in shards], 3 * n, n, copies)


def _ag_d2d_stage(fulls):
    n = len(fulls)

    def copies(ins, outs):
        x, y, c, me, _, chip_ids = _mesh_pos()
        out = []
        for k in range(n):
            mine, _ = _half_rows(c, ins[k].shape[1])
            for j in range(3):
                out.append((ins[k].at[chip_ids[j], mine], outs[k].at[chip_ids[j], mine], (x, y, 1 - c)))
        return out

    return _Stage(fulls, [jax.ShapeDtypeStruct(f.shape, f.dtype) for f in fulls], 3 * n, 0, copies,
                  aliases={k: k for k in range(n)})


def _rs_swap_stage(grads):
    n = len(grads)

    def copies(ins, outs):
        x, y, c, _, _, _ = _mesh_pos()
        out = []
        for k in range(n):
            _, other = _half_rows(c, ins[k].shape[1])
            out.append((ins[k].at[:, other, :], outs[k], (x, y, 1 - c)))
        return out

    return _Stage(grads, [jax.ShapeDtypeStruct((N_CHIPS, g.shape[1] // 2, g.shape[2]), g.dtype) for g in grads],
                  n, 0, copies)


def _rs_scatter_stage(sums):
    n = len(sums)

    def copies(ins, outs):
        x, y, c, me, chips, chip_ids = _mesh_pos()
        out = []
        for k in range(n):
            mine, _ = _half_rows(c, 2 * ins[k].shape[1])
            out.append((ins[k].at[me], outs[k].at[0, mine, :], None))
            for j, (cx, cy) in enumerate(chips):
                out.append((ins[k].at[chip_ids[j]], outs[k].at[1 + j, mine, :], (cx, cy, c)))
        return out

    return _Stage(sums, [jax.ShapeDtypeStruct((N_CHIPS, 2 * s.shape[1], s.shape[2]), s.dtype) for s in sums],
                  3 * n, n, copies)


def _rs_mirror_stage(parts):
    n = len(parts)

    def copies(ins, outs):
        x, y, c, _, _, _ = _mesh_pos()
        out = []
        for k in range(n):
            mine, _ = _half_rows(c, ins[k].shape[1])
            out.append((ins[k].at[:, mine, :], outs[k].at[:, mine, :], (x, y, 1 - c)))
        return out

    return _Stage(parts, [jax.ShapeDtypeStruct(p.shape, p.dtype) for p in parts], n, 0, copies,
                  aliases={k: k for k in range(n)})


def _pair_add(name, g, r1, core):
    _, rows, cols = g.shape
    hr = rows // 2

    def body(c_ref, g_ref, r_ref, o_ref):
        o_ref[...] = (g_ref[...].astype(F32) + r_ref[...].astype(F32)).astype(BF16)

    half = pl.BlockSpec((None, hr, cols), lambda j, c: (j, 0, 0))
    return pl.pallas_call(
        body, name=name,
        grid_spec=pltpu.PrefetchScalarGridSpec(
            num_scalar_prefetch=1, grid=(N_CHIPS,),
            in_specs=[pl.BlockSpec((None, hr, cols), lambda j, c: (j, c[0], 0)), half], out_specs=half),
        out_shape=jax.ShapeDtypeStruct((N_CHIPS, hr, cols), BF16),
        compiler_params=_params("parallel"),
    )(core, g, r1)


def _all_gather_weights(shards):
    n = len(shards)

    def body(*refs):
        ins, outs = refs[:n], refs[n:2 * n]
        send, recv, loc = refs[2 * n:]
        x, y, c, me, chips, chip_ids = _mesh_pos()
        sib = (x, y, 1 - c)

        def halves(k):
            hr = ins[k].shape[0] // 2
            return pl.ds(pl.multiple_of(c * hr, 16), hr), pl.ds(pl.multiple_of((1 - c) * hr, 16), hr)

        def remote(src, dst, k, j, dev):
            return pltpu.make_async_remote_copy(src_ref=src, dst_ref=dst, send_sem=send.at[k, j],
                                                recv_sem=recv.at[k, j], device_id=dev, device_id_type=_MESH)

        started = []
        local = []
        for k in range(n):
            mine, _ = halves(k)
            cp = pltpu.make_async_copy(ins[k], outs[k].at[me], loc.at[k])
            cp.start()
            local.append(cp)
            for j, (cx, cy) in enumerate(chips):
                cp = remote(ins[k].at[mine], outs[k].at[me, mine], k, j, (cx, cy, c))
                cp.start()
                started.append(cp)
        for k in range(n):
            mine, _ = halves(k)
            for j in range(3):
                land = outs[k].at[chip_ids[j], mine]
                remote(land, land, k, j, sib).wait_recv()
                cp = remote(land, land, k, 3 + j, sib)
                cp.start()
                started.append(cp)
        for k in range(n):
            _, other = halves(k)
            for j in range(3):
                land = outs[k].at[chip_ids[j], other]
                remote(land, land, k, 3 + j, sib).wait_recv()
        for cp in started:
            cp.wait_send()
        for cp in local:
            cp.wait()

    return pl.pallas_call(
        body, name="all_gather_weights", in_specs=[_ANY] * n, out_specs=[_ANY] * n,
        out_shape=[jax.ShapeDtypeStruct((N_CHIPS,) + s.shape, s.dtype) for s in shards],
        scratch_shapes=[pltpu.SemaphoreType.DMA((n, 6)), pltpu.SemaphoreType.DMA((n, 6)),
                        pltpu.SemaphoreType.DMA((n,))],
        compiler_params=pltpu.CompilerParams(vmem_limit_bytes=V7X_VMEM_LIMIT_BYTES),
    )(*shards)


_RS_CHUNK = 32


def _reduce_scatter(name, grads):
    n = len(grads)

    def body(*refs):
        gs, outs = refs[:n], refs[n:2 * n]
        own, r1, r2, fin = (refs[(2 + i) * n:(3 + i) * n] for i in range(4))
        a_send, a_recv, b_send, b_recv, c_send, c_recv, l_in, l_out = refs[6 * n:]
        x, y, c, me, chips, chip_ids = _mesh_pos()
        sib = (x, y, 1 - c)

        def halves(k):
            hr = gs[k].shape[1] // 2
            return hr, pl.ds(pl.multiple_of(c * hr, 16), hr), pl.ds(pl.multiple_of((1 - c) * hr, 16), hr)

        def remote(src, dst, ssem, rsem, dev):
            return pltpu.make_async_remote_copy(src_ref=src, dst_ref=dst, send_sem=ssem, recv_sem=rsem,
                                                device_id=dev, device_id_type=_MESH)

        sends, locals_in = [], []
        for k in range(n):
            hr, mine, other = halves(k)
            cp = remote(gs[k].at[:, other, :], r1[k], a_send.at[k], a_recv.at[k], sib)
            cp.start()
            sends.append(cp)
            cp = pltpu.make_async_copy(gs[k].at[:, mine, :], own[k], l_in.at[k])
            cp.start()
            locals_in.append(cp)

        for k in range(n):
            hr, mine, other = halves(k)
            locals_in[k].wait()
            remote(r1[k], r1[k], a_send.at[k], a_recv.at[k], sib).wait_recv()
            for j in range(N_CHIPS):
                def add(i, carry, k=k, j=j):
                    rows = pl.ds(pl.multiple_of(i * _RS_CHUNK, _RS_CHUNK), _RS_CHUNK)
                    own[k][j, rows, :] = (own[k][j, rows, :].astype(F32) + r1[k][j, rows, :].astype(F32)).astype(BF16)
                    return carry

                lax.fori_loop(0, hr // _RS_CHUNK, add, 0)
            for j, (cx, cy) in enumerate(chips):
                cp = remote(own[k].at[chip_ids[j]], r2[k].at[j], b_send.at[k, j], b_recv.at[k, j], (cx, cy, c))
                cp.start()
                sends.append(cp)

        locals_out = []
        for k in range(n):
            hr, mine, other = halves(k)
            for j in range(3):
                remote(r2[k].at[j], r2[k].at[j], b_send.at[k, j], b_recv.at[k, j], sib).wait_recv()

            def total(i, carry, k=k):
                rows = pl.ds(pl.multiple_of(i * _RS_CHUNK, _RS_CHUNK), _RS_CHUNK)
                acc = own[k][me, rows, :].astype(F32)
                for j in range(3):
                    acc = acc + r2[k][j, rows, :].astype(F32)
                fin[k][rows, :] = acc
                return carry

            lax.fori_loop(0, hr // _RS_CHUNK, total, 0)
            cp = remote(fin[k], outs[k].at[mine, :], c_send.at[k], c_recv.at[k], sib)
            cp.start()
            sends.append(cp)
            cp = pltpu.make_async_copy(fin[k], outs[k].at[mine, :], l_out.at[k])
            cp.start()
            locals_out.append(cp)

        for k in range(n):
            hr, mine, other = halves(k)
            land = outs[k].at[other, :]
            remote(land, land, c_send.at[k], c_recv.at[k], sib).wait_recv()
        for cp in sends:
            cp.wait_send()
        for cp in locals_out:
            cp.wait()

    scratch = []
    for g in grads:
        scratch.append(pltpu.VMEM((N_CHIPS, g.shape[1] // 2, g.shape[2]), BF16))
    for g in grads:
        scratch.append(pltpu.VMEM((N_CHIPS, g.shape[1] // 2, g.shape[2]), BF16))
    for g in grads:
        scratch.append(pltpu.VMEM((3, g.shape[1] // 2, g.shape[2]), BF16))
    for g in grads:
        scratch.append(pltpu.VMEM((g.shape[1] // 2, g.shape[2]), F32))
    dma = pltpu.SemaphoreType.DMA
    scratch += [dma((n,)), dma((n,)), dma((n, 3)), dma((n, 3)), dma((n,)), dma((n,)), dma((n,)), dma((n,))]
    return pl.pallas_call(
        body, name=name, in_specs=[_ANY] * n, out_specs=[_ANY] * n,
        out_shape=[jax.ShapeDtypeStruct(g.shape[1:], F32) for g in grads],
        scratch_shapes=scratch,
        compiler_params=pltpu.CompilerParams(vmem_limit_bytes=V7X_VMEM_LIMIT_BYTES),
    )(*grads)


def _adamw_math(w, g, m, v):
    m = ADAM_B1 * m + (1.0 - ADAM_B1) * g
    v = ADAM_B2 * v + (1.0 - ADAM_B2) * (g * g)
    m_hat = m / (1.0 - ADAM_B1 ** ADAM_STEP)
    v_hat = v / (1.0 - ADAM_B2 ** ADAM_STEP)
    delta = -ADAM_LR * (m_hat / (jnp.sqrt(v_hat) + ADAM_EPS) + ADAM_WD * w)
    return delta, m, v


def _adamw_sum(name, w, parts, m, v):
    r, c = w.shape
    br = r
    while br * c * 4 > (1 << 20) and br % 32 == 0:
        br //= 2

    def body(w_ref, p_ref, m_ref, v_ref, g_ref, d_ref, nm_ref, nv_ref):
        g = p_ref[0].astype(F32)
        for j in range(1, N_CHIPS):
            g = g + p_ref[j].astype(F32)
        d, nm, nv = _adamw_math(w_ref[...], g, m_ref[...], v_ref[...])
        g_ref[...] = g
        d_ref[...] = d
        nm_ref[...] = nm
        nv_ref[...] = nv

    spec = pl.BlockSpec((br, c), lambda i: (i, 0))
    shp = jax.ShapeDtypeStruct((r, c), F32)
    return _call(
        body, name=name, grid=(r // br,),
        in_specs=[spec, pl.BlockSpec((N_CHIPS, br, c), lambda i: (0, i, 0)), spec, spec],
        out_specs=[spec] * 4, out_shape=[shp] * 4, sem=("parallel",), args=[w, parts, m, v])


_SMALL_ROWS = 80


def _small_allreduce_adamw(gpack, wpack, mpack, vpack):
    def body(g_ref, w_ref, m_ref, v_ref, go_ref, d_ref, nm_ref, nv_ref, buf, send, recv):
        x, y, c = lax.axis_index("x"), lax.axis_index("y"), lax.axis_index("c")
        me = 4 * x + 2 * y + c
        buf[me] = g_ref[...]
        copies = []
        for rel in range(1, 8):
            fx, fy, fc = (rel >> 2) & 1, (rel >> 1) & 1, rel & 1
            dev = ((1 - x) if fx else x, (1 - y) if fy else y, (1 - c) if fc else c)
            cp = pltpu.make_async_remote_copy(src_ref=g_ref, dst_ref=buf.at[me], send_sem=send.at[rel - 1],
                                              recv_sem=recv.at[rel - 1], device_id=dev, device_id_type=_MESH)
            cp.start()
            copies.append(cp)
        for cp in copies:
            cp.wait_recv()
        for cp in copies:
            cp.wait_send()
        total = buf[0]
        for i in range(1, 8):
            total = total + buf[i]
        go_ref[...] = total
        d, nm, nv = _adamw_math(w_ref[...], total, m_ref[...], v_ref[...])
        d_ref[...] = d
        nm_ref[...] = nm
        nv_ref[...] = nv

    vm = pl.BlockSpec(memory_space=pltpu.VMEM)
    shp = jax.ShapeDtypeStruct((_SMALL_ROWS, D_MODEL), F32)
    return pl.pallas_call(
        body, name="small_allreduce_adamw", in_specs=[vm] * 4, out_specs=[vm] * 4, out_shape=[shp] * 4,
        scratch_shapes=[pltpu.VMEM((8, _SMALL_ROWS, D_MODEL), F32), pltpu.SemaphoreType.DMA((7,)),
                        pltpu.SemaphoreType.DMA((7,))],
        compiler_params=pltpu.CompilerParams(vmem_limit_bytes=V7X_VMEM_LIMIT_BYTES),
    )(gpack, wpack, mpack, vpack)


_SMALL_VECTORS = ("ffn1_norm", "mix_norm", "xattn_norm", "mem_norm", "ffn2_norm", "final_norm", "q_norm",
                  "kv_norm", "pool_scale")


def _pack_small(d):
    rows = []
    for n in _SMALL_VECTORS:
        v = d[n].reshape(1, -1).astype(F32)
        rows.append(jnp.pad(v, ((0, 0), (0, D_MODEL - v.shape[1]))))
    rows.append(jnp.zeros((16 - len(_SMALL_VECTORS), D_MODEL), F32))
    rows.append(d["pool_w"].reshape(64, D_MODEL).astype(F32))
    return jnp.concatenate(rows, axis=0)


def _unpack_small(pack, like):
    out = {}
    for i, n in enumerate(_SMALL_VECTORS):
        out[n] = pack[i, :like[n].size].reshape(like[n].shape)
    out["pool_w"] = pack[16:].reshape(like["pool_w"].shape)
    return out


_WEIGHTS = ("ffn1_norm", "ffn1_w_gate", "ffn1_w_up", "ffn1_w_down", "mix_norm", "w_in", "q_norm", "w_q_up",
            "kv_norm", "w_kv_up", "pool_w", "pool_scale", "w_out", "xattn_norm", "mem_norm", "w_mq", "w_mkv",
            "w_mo", "ffn2_norm", "ffn2_w_gate", "ffn2_w_up", "ffn2_w_down", "final_norm")
_SHARDED = ("ffn1_w_gate", "ffn1_w_up", "ffn1_w_down", "w_in", "w_q_up", "w_kv_up", "w_out", "w_mq", "w_mkv",
            "w_mo", "ffn2_w_gate", "ffn2_w_up", "ffn2_w_down")
_RS_GROUPS = (("ffn2_w_gate", "ffn2_w_up", "ffn2_w_down"),
              ("w_mo", "w_mq", "w_mkv", "w_out", "w_q_up", "w_kv_up", "w_in"),
              ("ffn1_w_gate", "ffn1_w_up", "ffn1_w_down"))
W_IN_SPLIT = Q_LORA + KV_LORA + ROPE_DIM


_TRANSPOSED = ("ffn1_w_gate", "ffn1_w_up", "ffn2_w_gate", "ffn2_w_up", "w_in", "w_q_up")


def _local_view(name, a):
    return jnp.swapaxes(a, 1, 2)[0] if name in _TRANSPOSED else a[0]


def _global_view(name, a):
    return jnp.swapaxes(a[None], 1, 2) if name in _TRANSPOSED else a[None]


def _pad_shard(name, a):
    if name == "w_in":
        return jnp.concatenate([a[:W_IN_SPLIT], jnp.zeros((64, a.shape[1]), a.dtype), a[W_IN_SPLIT:]], axis=0)
    if name == "w_q_up":
        return jnp.pad(a, ((0, 64), (0, 0)))
    return a


def _unpad_shard(name, a):
    if name == "w_in":
        return jnp.concatenate([a[:, :W_IN_SPLIT], a[:, W_IN_SPLIT + 64:]], axis=1)
    if name == "w_q_up":
        return a[:, :192]
    return a


def _stacked(g):
    return g if g.ndim == 3 else g.reshape(N_CHIPS, g.shape[0] // N_CHIPS, g.shape[1])


class _Plan:
    AG_UNITS = (
        (("w_in", "w_q_up", "w_kv_up", "ffn2_w_gate"), "ffn1_up", "ffn1_down"),
        (("ffn2_w_up",), "ffn1_down", "w_in"),
        (("w_out", "w_mq", "w_mkv", "w_mo", "ffn2_w_down"), "mla_fwd", "pool_fwd"),
    )
    RS_UNITS = (
        (("ffn2_w_gate", "ffn2_w_up", "ffn2_w_down"), "ffn2_norm_bwd", "mla_bwd", "qkv_prep_bwd"),
        (("w_mo", "w_mq", "w_mkv"), "w_out_dx", "mla_bwd", "qkv_prep_bwd"),
        (("w_out", "w_q_up", "w_kv_up", "w_in"), "mix_norm_bwd", "ffn1_dact", "ffn1_dwd"),
        (("ffn1_w_down",), "ffn1_dwg", "ffn1_dwu", "ffn1_dn_a"),
        (("ffn1_w_gate",), "ffn1_dwu", "ffn1_dn_a", "ffn1_dn_b"),
        (("ffn1_w_up",), "ffn1_dn_a", "ffn1_dn_b", "adamw_w_kv_up"),
    )
    ADAMW_ORDER = ("w_kv_up", "ffn2_w_gate", "ffn2_w_up", "ffn2_w_down", "w_mo", "w_mq", "w_mkv", "w_out", "w_q_up",
                   "w_in", "ffn1_w_down", "ffn1_w_gate", "ffn1_w_up")

    def __init__(self, shards, w, grads, core):
        self.shards, self.w, self.grads, self.core = shards, w, grads, core
        self.parts = {}
        self.ag = [[None, None] for _ in self.AG_UNITS]
        self.rs = [[None, None, None, None] for _ in self.RS_UNITS]

    def pre(self, name):
        for i, (names, h1, h2) in enumerate(self.AG_UNITS):
            if name == h1:
                self.ag[i][0] = _host(name, _ag_ici_stage([self.shards[n] for n in names]))
            if name == h2:
                self.ag[i][1] = _host(name, _ag_d2d_stage(self.ag[i][0].results))
        for i, (names, h1, h2, h3) in enumerate(self.RS_UNITS):
            if name == h1:
                self.rs[i][0] = _host(name, _rs_swap_stage([_stacked(self.grads[n]) for n in names]))
            if name == h2:
                self.rs[i][2] = _host(name, _rs_scatter_stage(self.rs[i][1]))
            if name == h3:
                self.rs[i][3] = _host(name, _rs_mirror_stage(self.rs[i][2].results))

    def post(self, name):
        for i, (names, h1, h2) in enumerate(self.AG_UNITS):
            if name == h2:
                for n, f in zip(names, self.ag[i][1].results):
                    self.w[n] = _full_weight(n, f)
        for i, (names, h1, h2, h3) in enumerate(self.RS_UNITS):
            if name == h1:
                self.rs[i][1] = [_pair_add("pair_add_" + n, _stacked(self.grads[n]), r1, self.core)
                                 for n, r1 in zip(names, self.rs[i][0].results)]
            if name == h3:
                for n, p in zip(names, self.rs[i][3].results):
                    self.parts[n] = p


def _full_weight(name, stacked):
    if name in ("w_out", "w_mq", "w_mo"):
        return stacked.reshape(D_MODEL, D_MODEL)
    return stacked


def kernel(x, mem, positions, ffn1_norm, ffn1_w_gate, ffn1_w_up, ffn1_w_down, mix_norm, w_in, q_norm, w_q_up, kv_norm, w_kv_up, pool_w, pool_scale, w_out, xattn_norm, mem_norm, w_mq, w_mkv, w_mo, ffn2_norm, ffn2_w_gate, ffn2_w_up, ffn2_w_down, final_norm, loss_target, m_ffn1_norm, m_ffn1_w_gate, m_ffn1_w_up, m_ffn1_w_down, m_mix_norm, m_w_in, m_q_norm, m_w_q_up, m_kv_norm, m_w_kv_up, m_pool_w, m_pool_scale, m_w_out, m_xattn_norm, m_mem_norm, m_w_mq, m_w_mkv, m_w_mo, m_ffn2_norm, m_ffn2_w_gate, m_ffn2_w_up, m_ffn2_w_down, m_final_norm, v_ffn1_norm, v_ffn1_w_gate, v_ffn1_w_up, v_ffn1_w_down, v_mix_norm, v_w_in, v_q_norm, v_w_q_up, v_kv_norm, v_w_kv_up, v_pool_w, v_pool_scale, v_w_out, v_xattn_norm, v_mem_norm, v_w_mq, v_w_mkv, v_w_mo, v_ffn2_norm, v_ffn2_w_gate, v_ffn2_w_up, v_ffn2_w_down, v_final_norm):
    wts = dict(zip(_WEIGHTS, (ffn1_norm, ffn1_w_gate, ffn1_w_up, ffn1_w_down, mix_norm, w_in, q_norm, w_q_up, kv_norm, w_kv_up, pool_w, pool_scale, w_out, xattn_norm, mem_norm, w_mq, w_mkv, w_mo, ffn2_norm, ffn2_w_gate, ffn2_w_up, ffn2_w_down, final_norm)))
    mom = dict(zip(_WEIGHTS, (m_ffn1_norm, m_ffn1_w_gate, m_ffn1_w_up, m_ffn1_w_down, m_mix_norm, m_w_in, m_q_norm, m_w_q_up, m_kv_norm, m_w_kv_up, m_pool_w, m_pool_scale, m_w_out, m_xattn_norm, m_mem_norm, m_w_mq, m_w_mkv, m_w_mo, m_ffn2_norm, m_ffn2_w_gate, m_ffn2_w_up, m_ffn2_w_down, m_final_norm)))
    var = dict(zip(_WEIGHTS, (v_ffn1_norm, v_ffn1_w_gate, v_ffn1_w_up, v_ffn1_w_down, v_mix_norm, v_w_in, v_q_norm, v_w_q_up, v_kv_norm, v_w_kv_up, v_pool_w, v_pool_scale, v_w_out, v_xattn_norm, v_mem_norm, v_w_mq, v_w_mkv, v_w_mo, v_ffn2_norm, v_ffn2_w_gate, v_ffn2_w_up, v_ffn2_w_down, v_final_norm)))
    small = [n for n in _WEIGHTS if n not in _SHARDED]

    global _PLAN
    shards = {n: _pad_shard(n, _local_view(n, wts[n])).astype(BF16) for n in _SHARDED}
    w = {n: wts[n].reshape(1, -1) for n in _SMALL_VECTORS}
    w["pool_w"] = pool_w[0].astype(BF16)
    grads = {}
    core = lax.axis_index("c").astype(jnp.int32).reshape(1)
    plan = _Plan(shards, w, grads, core)
    _PLAN = plan
    try:
        first = ("ffn1_w_gate", "ffn1_w_up", "ffn1_w_down")
        for n, f in zip(first, _all_gather_weights([shards[n] for n in first])):
            w[n] = f

        loss_local, dx = _local_step(x[0], mem[0], positions[0], loss_target[0], w, grads)
        loss = lax.psum(loss_local, ("x", "y", "c"))

        gpack, dpack, mpack, vpack = _small_allreduce_adamw(
            _pack_small({n: grads[n] for n in small}), _pack_small({n: wts[n] for n in small}),
            _pack_small({n: mom[n] for n in small}), _pack_small({n: var[n] for n in small}))
        like = {n: wts[n] for n in small}
        g_out, d_out, m_out, v_out = (_unpack_small(p, like) for p in (gpack, dpack, mpack, vpack))

        for n in _Plan.ADAMW_ORDER:
            res = _adamw_sum("adamw_" + n, _local_view(n, wts[n]), _unpad_shard(n, plan.parts[n]),
                             _local_view(n, mom[n]), _local_view(n, var[n]))
            g_out[n], d_out[n], m_out[n], v_out[n] = (_global_view(n, r) for r in res)
    finally:
        _PLAN = None
        _PENDING.clear()

    return (loss, dx[None], *[g_out[n] for n in _WEIGHTS], *[d_out[n] for n in _WEIGHTS],
            *[m_out[n] for n in _WEIGHTS], *[v_out[n] for n in _WEIGHTS])
```

```python
import functools

import jax
import jax.numpy as jnp
from jax import lax
from jax.experimental import pallas as pl
from jax.experimental.pallas import tpu as pltpu

F32 = jnp.float32
BF16 = jnp.bfloat16

D_MODEL = 1024
D_FF = 2816
N_CHIPS = 4
FF_SHARD = D_FF // N_CHIPS
MLA_HEADS = 4
Q_LORA = 256
KV_LORA = 128
ROPE_DIM = 64
HEAD_QK = 256
HEAD_V = 128
POOL_GROUPS = 4
POOL_CH = 128
MEM_HEADS = 4
MEM_HEAD_DIM = 256
RMS_EPS = 1e-6
ROPE_BASE = 10000.0
MLA_SCALE = (128 + 64) ** -0.5
MEM_SCALE = MEM_HEAD_DIM ** -0.5

ADAM_LR = 0.001
ADAM_B1 = 0.9
ADAM_B2 = 0.999
ADAM_EPS = 1e-08
ADAM_WD = 0.01
ADAM_STEP = 10

V7X_VMEM_LIMIT_BYTES = 56 * 1024 * 1024

NN = ((1,), (0,))
NT = ((1,), (1,))
TN = ((0,), (0,))


def _params(*sem):
    return pltpu.CompilerParams(dimension_semantics=sem, vmem_limit_bytes=V7X_VMEM_LIMIT_BYTES)


_MESH = pl.DeviceIdType.MESH
_ANY = pl.BlockSpec(memory_space=pl.ANY)


class _Stage:
    def __init__(self, ins, outs, n_remote, n_local, copies, aliases=None):
        self.ins, self.outs, self.n_remote, self.n_local = list(ins), list(outs), n_remote, n_local
        self.copies, self.aliases = copies, dict(aliases or {})
        self.results = None

    def descriptors(self, in_refs, out_refs, send, recv, loc):
        ds, ri, li = [], 0, 0
        for src, dst, dev in self.copies(in_refs, out_refs):
            if dev is None:
                ds.append(pltpu.make_async_copy(src, dst, loc.at[li]))
                li += 1
            else:
                ds.append(pltpu.make_async_remote_copy(src_ref=src, dst_ref=dst, send_sem=send.at[ri],
                                                       recv_sem=recv.at[ri], device_id=dev, device_id_type=_MESH))
                ri += 1
        assert ri == self.n_remote and li == self.n_local
        return ds


_PENDING = {}


def _host(name, stage):
    _PENDING.setdefault(name, []).append(stage)
    return stage


_PLAN = None


def _call(body, **kw):
    if _PLAN is not None:
        _PLAN.pre(kw["name"])
    res = _call_hosting(body, **kw)
    if _PLAN is not None:
        _PLAN.post(kw["name"])
    return res


def _call_hosting(body, *, name, grid, in_specs, out_specs, out_shape, sem, args, scratch_shapes=(), aliases=None):
    stages = _PENDING.pop(name, [])
    scratch_shapes = list(scratch_shapes)
    if not stages:
        return pl.pallas_call(body, name=name, grid=grid, in_specs=in_specs, out_specs=out_specs,
                              out_shape=out_shape, scratch_shapes=scratch_shapes,
                              input_output_aliases=dict(aliases or {}), compiler_params=_params(*sem))(*args)
    ni, no, ns = len(in_specs), len(out_shape), len(scratch_shapes)
    c_ins = [a for st in stages for a in st.ins]
    c_outs = [o for st in stages for o in st.outs]
    nci, nco = len(c_ins), len(c_outs)
    aliases, io, oo = dict(aliases or {}), 0, 0
    for st in stages:
        for i, j in st.aliases.items():
            aliases[ni + io + i] = no + oo + j
        io += len(st.ins)
        oo += len(st.outs)
    dma = pltpu.SemaphoreType.DMA
    sems = []
    for st in stages:
        sems += [dma((max(st.n_remote, 1),)), dma((max(st.n_remote, 1),)), dma((max(st.n_local, 1),))]

    def wrapped(*refs):
        ins, cin = refs[:ni], refs[ni:ni + nci]
        outs, cout = refs[ni + nci:ni + nci + no], refs[ni + nci + no:ni + nci + no + nco]
        scr = refs[ni + nci + no + nco:ni + nci + no + nco + ns]
        sem_refs = refs[ni + nci + no + nco + ns:]
        first = pl.program_id(0) == 0
        last = pl.program_id(0) == grid[0] - 1
        for ax in range(1, len(grid)):
            first = jnp.logical_and(first, pl.program_id(ax) == 0)
            last = jnp.logical_and(last, pl.program_id(ax) == grid[ax] - 1)

        def descriptors():
            ds, io, oo = [], 0, 0
            for si, st in enumerate(stages):
                ds += st.descriptors(cin[io:io + len(st.ins)], cout[oo:oo + len(st.outs)], *sem_refs[3 * si:3 * si + 3])
                io += len(st.ins)
                oo += len(st.outs)
            return ds

        @pl.when(first)
        def _():
            for d in descriptors():
                d.start()

        body(*ins, *outs, *scr)

        @pl.when(last)
        def _():
            for d in descriptors():
                d.wait()

    res = pl.pallas_call(
        wrapped, name=name, grid=grid, in_specs=list(in_specs) + [_ANY] * nci,
        out_specs=list(out_specs) + [_ANY] * nco, out_shape=list(out_shape) + c_outs,
        scratch_shapes=scratch_shapes + sems, input_output_aliases=aliases,
        compiler_params=_params(*(("arbitrary",) * len(grid))))(*args, *c_ins)
    oo = no
    for st in stages:
        st.results = list(res[oo:oo + len(st.outs)])
        oo += len(st.outs)
    return list(res[:no])


def _dot(a, b, dims):
    return lax.dot_general(a.astype(BF16), b.astype(BF16), (dims, ((), ())), preferred_element_type=F32)


_MAX_ROW_BLOCK = 1024
_ATT_BLOCK = 512


_MAX_REDUCE_BLOCK = 2048


def _row_block(s, want=512):
    return min(want, s, _MAX_ROW_BLOCK)


def _reduce_block(s):
    return min(s, _MAX_REDUCE_BLOCK)


def _matmul(name, grid, terms, extras, outs, epilogue, acc_shape, fill=(), summed=()):
    nt, ne, no, nf = len(terms), len(extras), len(outs), len(fill)
    nk = grid[-1]
    dims = [t[4] for t in terms]

    def body(*refs):
        a_refs, b_refs = refs[:nt], refs[nt:2 * nt]
        e_refs = refs[2 * nt:2 * nt + ne]
        o_refs = refs[2 * nt + ne + nf:2 * nt + ne + nf + no]

        def finish(acc):
            vals = epilogue(acc, *[e[...] for e in e_refs])
            for idx, (o, val) in enumerate(zip(o_refs, vals)):
                if idx in summed:
                    @pl.when(pl.program_id(0) == 0)
                    def _(o=o, val=val):
                        o[...] = val.astype(o.dtype)

                    @pl.when(pl.program_id(0) > 0)
                    def _(o=o, val=val):
                        o[...] += val.astype(o.dtype)
                else:
                    o[...] = val.astype(o.dtype)

        if nk == 1:
            part = None
            for a, b, d in zip(a_refs, b_refs, dims):
                t = _dot(a[...], b[...], d)
                part = t if part is None else part + t
            finish(part)
        else:
            acc_ref = refs[-1]
            k = pl.program_id(len(grid) - 1)

            @pl.when(k == 0)
            def _():
                acc_ref[...] = jnp.zeros_like(acc_ref)

            for a, b, d in zip(a_refs, b_refs, dims):
                acc_ref[...] += _dot(a[...], b[...], d)

            @pl.when(k == nk - 1)
            def _():
                finish(acc_ref[...])

    in_specs = [t[1] for t in terms] + [t[3] for t in terms] + [e[1] for e in extras] + [_ANY] * nf
    args = [t[0] for t in terms] + [t[2] for t in terms] + [e[0] for e in extras] + list(fill)
    sem = ("arbitrary" if summed else "parallel",) * (len(grid) - 1) + ("arbitrary",)
    aliases = {2 * nt + ne + i: i for i in range(nf)}
    return _call(
        body, name=name, grid=grid, in_specs=in_specs,
        out_specs=[o[1] for o in outs], out_shape=[o[0] for o in outs],
        scratch_shapes=[pltpu.VMEM(acc_shape, F32)] if nk > 1 else [], sem=sem, args=args, aliases=aliases)


def _ident(acc):
    return (acc,)


def _rmsnorm_fwd(name, x, gain, width, col_block=0):
    s = x.shape[0]
    bm = _row_block(s)

    def body(x_ref, g_ref, o_ref):
        xf = x_ref[...]
        r = lax.rsqrt(jnp.mean(xf * xf, axis=-1, keepdims=True) + RMS_EPS)
        o_ref[...] = ((xf * r) * g_ref[...]).astype(o_ref.dtype)

    return pl.pallas_call(
        body, name=name, grid=(s // bm,),
        in_specs=[pl.BlockSpec((bm, width), lambda i: (i, col_block)), pl.BlockSpec((1, width), lambda i: (0, 0))],
        out_specs=pl.BlockSpec((bm, width), lambda i: (i, 0)),
        out_shape=jax.ShapeDtypeStruct((s, width), BF16),
        compiler_params=_params("parallel"),
    )(x, gain)


def _rms_bwd_math(dy, xf, g, width):
    r = lax.rsqrt(jnp.mean(xf * xf, axis=-1, keepdims=True) + RMS_EPS)
    dyg = dy * g
    dot = jnp.sum(dyg * xf, axis=-1, keepdims=True)
    dx = r * dyg - xf * ((r * r * r) * (dot * (1.0 / width)))
    dgain = jnp.sum(dy * (xf * r), axis=0, keepdims=True)
    return dx, dgain


def _rmsnorm_bwd(name, dy, x, gain, width, col_block=0, dres=None, out_dtype=F32):
    s = x.shape[0]
    bm = _row_block(s)
    has_res = dres is not None

    def body(*refs):
        if has_res:
            dy_ref, x_ref, g_ref, r_ref, dx_ref, dg_ref, dxb_ref = refs
        else:
            dy_ref, x_ref, g_ref, dx_ref, dg_ref = refs
        dx, dgain = _rms_bwd_math(dy_ref[...].astype(F32), x_ref[...], g_ref[...], width)
        if has_res:
            dx = dx + r_ref[...]
            dxb_ref[...] = dx.astype(BF16)
        dx_ref[...] = dx.astype(dx_ref.dtype)

        @pl.when(pl.program_id(0) == 0)
        def _():
            dg_ref[...] = dgain

        @pl.when(pl.program_id(0) > 0)
        def _():
            dg_ref[...] += dgain

    row = pl.BlockSpec((bm, width), lambda i: (i, 0))
    in_specs = [row, pl.BlockSpec((bm, width), lambda i: (i, col_block)), pl.BlockSpec((1, width), lambda i: (0, 0))]
    args = [dy, x, gain]
    out_specs = [row, pl.BlockSpec((1, width), lambda i: (0, 0))]
    out_shape = [jax.ShapeDtypeStruct((s, width), out_dtype), jax.ShapeDtypeStruct((1, width), F32)]
    if has_res:
        in_specs.append(row)
        args.append(dres)
        out_specs.append(row)
        out_shape.append(jax.ShapeDtypeStruct((s, width), BF16))
    return _call(body, name=name, grid=(s // bm,), in_specs=in_specs, out_specs=out_specs, out_shape=out_shape,
                 sem=("arbitrary",), args=args)


def _loss_and_final_norm(h, gain, target):
    s, d = h.shape
    bm = _row_block(s, 512)

    def body(h_ref, g_ref, t_ref, dh_ref, dhb_ref, loss_ref, dg_ref):
        xf = h_ref[...]
        g = g_ref[...]
        r = lax.rsqrt(jnp.mean(xf * xf, axis=-1, keepdims=True) + RMS_EPS)
        err = (xf * r) * g - t_ref[...]
        part = 0.5 * jnp.sum(jnp.mean(err * err, axis=-1, keepdims=True), axis=0, keepdims=True)
        dx, dgain = _rms_bwd_math(err * (1.0 / d), xf, g, d)
        dh_ref[...] = dx
        dhb_ref[...] = dx.astype(BF16)

        @pl.when(pl.program_id(0) == 0)
        def _():
            dg_ref[...] = dgain
            loss_ref[...] = jnp.broadcast_to(part, loss_ref.shape)

        @pl.when(pl.program_id(0) > 0)
        def _():
            dg_ref[...] += dgain
            loss_ref[...] += jnp.broadcast_to(part, loss_ref.shape)

    row = pl.BlockSpec((bm, d), lambda i: (i, 0))
    vec = pl.BlockSpec((1, d), lambda i: (0, 0))
    return pl.pallas_call(
        body, name="loss_final_norm", grid=(s // bm,), in_specs=[row, vec, row],
        out_specs=[row, row, pl.BlockSpec((1, 128), lambda i: (0, 0)), vec],
        out_shape=[jax.ShapeDtypeStruct((s, d), F32), jax.ShapeDtypeStruct((s, d), BF16),
                   jax.ShapeDtypeStruct((1, 128), F32),
                   jax.ShapeDtypeStruct((1, d), F32)],
        compiler_params=_params("arbitrary"),
    )(h, gain, target)


def _ffn_up(name, n, wg, wu):
    s = n.shape[0]
    bm = _row_block(s)

    def body(n_ref, wg_ref, wu_ref, a_ref, dadu_ref, dadg_ref):
        x = n_ref[...]
        g = _dot(x, wg_ref[...], NT)
        u = _dot(x, wu_ref[...], NT)
        sg = jax.nn.sigmoid(g)
        silu = g * sg
        a_ref[...] = (silu * u).astype(BF16)
        dadu_ref[...] = silu.astype(BF16)
        dadg_ref[...] = (u * (sg * (1.0 + g * (1.0 - sg)))).astype(BF16)

    w_spec = pl.BlockSpec((None, FF_SHARD, D_MODEL), lambda j, i: (j, 0, 0))
    o_spec = pl.BlockSpec((None, bm, FF_SHARD), lambda j, i: (j, i, 0))
    shp = jax.ShapeDtypeStruct((N_CHIPS, s, FF_SHARD), BF16)
    return _call(
        body, name=name, grid=(N_CHIPS, s // bm),
        in_specs=[pl.BlockSpec((bm, D_MODEL), lambda j, i: (i, 0)), w_spec, w_spec],
        out_specs=[o_spec, o_spec, o_spec], out_shape=[shp, shp, shp],
        sem=("parallel", "parallel"), args=[n, wg, wu])


def _residual_epilogue(alpha, with_norm):
    if not with_norm:
        return lambda acc, r: (r + alpha * acc,)

    def epilogue(acc, r, g):
        h = r + alpha * acc
        rs = lax.rsqrt(jnp.mean(h * h, axis=-1, keepdims=True) + RMS_EPS)
        return h, (h * rs) * g

    return epilogue


def _residual_outs(s, bm, gain):
    row = pl.BlockSpec((bm, D_MODEL), lambda i, k: (i, 0))
    outs = [(jax.ShapeDtypeStruct((s, D_MODEL), F32), row)]
    if gain is None:
        return [], outs
    return [(gain, pl.BlockSpec((1, D_MODEL), lambda i, k: (0, 0)))], outs + [(jax.ShapeDtypeStruct((s, D_MODEL), BF16), row)]


def _ffn_down(name, a, wd, res, gain=None):
    s = a.shape[1]
    bm = _row_block(s, 512)
    row = pl.BlockSpec((bm, D_MODEL), lambda i, k: (i, 0))
    terms = [(a, pl.BlockSpec((None, bm, FF_SHARD), lambda i, k, j=j: (j, i, 0)),
              wd, pl.BlockSpec((None, FF_SHARD, D_MODEL), lambda i, k, j=j: (j, 0, 0)), NN) for j in range(N_CHIPS)]
    extras, outs = _residual_outs(s, bm, gain)
    res_out = _matmul(name, (s // bm, 1), terms, [(res, row)] + extras, outs,
                      _residual_epilogue(0.5, gain is not None), None)
    return res_out if gain is not None else res_out[0]


def _norm_bwd_epilogue(width):
    def epilogue(acc, h, g, dres):
        dx, dgain = _rms_bwd_math(acc, h, g, width)
        dx = dx + dres
        return dx, dx, dgain

    return epilogue


def _norm_bwd_operands(s, bm, h, gain, dres):
    row = pl.BlockSpec((bm, D_MODEL), lambda i, k: (i, 0))
    vec = pl.BlockSpec((1, D_MODEL), lambda i, k: (0, 0))
    extras = [(h, row), (gain, vec), (dres, row)]
    outs = [(jax.ShapeDtypeStruct((s, D_MODEL), F32), row), (jax.ShapeDtypeStruct((s, D_MODEL), BF16), row),
            (jax.ShapeDtypeStruct((1, D_MODEL), F32), vec)]
    return extras, outs, (2,)


def _ffn_bwd(tag, dh, n, dadg, dadu, a, wg, wu, wd, grads, norm_bwd=None):
    s = dh.shape[0]
    bm = _row_block(s)
    bk = _reduce_block(s)
    nk = s // bk

    def act_bwd(acc, dg_da, du_da):
        da = 0.5 * acc
        return da * dg_da.astype(F32), da * du_da.astype(F32)

    slab = pl.BlockSpec((None, bm, FF_SHARD), lambda j, i, k: (j, i, 0))
    shp = jax.ShapeDtypeStruct((N_CHIPS, s, FF_SHARD), BF16)
    dg, du = _matmul(
        tag + "_dact", (N_CHIPS, s // bm, 1),
        [(dh, pl.BlockSpec((bm, D_MODEL), lambda j, i, k: (i, 0)),
          wd, pl.BlockSpec((None, FF_SHARD, D_MODEL), lambda j, i, k: (j, 0, 0)), NT)],
        [(dadg, slab), (dadu, slab)], [(shp, slab), (shp, slab)], act_bwd, None)

    grads[tag + "_w_down"] = _matmul(
        tag + "_dwd", (N_CHIPS, nk),
        [(a, pl.BlockSpec((None, bk, FF_SHARD), lambda j, k: (j, k, 0)),
          dh, pl.BlockSpec((bk, D_MODEL), lambda j, k: (k, 0)), TN)],
        [], [(jax.ShapeDtypeStruct((N_CHIPS, FF_SHARD, D_MODEL), BF16),
              pl.BlockSpec((None, FF_SHARD, D_MODEL), lambda j, k: (j, 0, 0)))],
        lambda acc: (0.5 * acc,), (FF_SHARD, D_MODEL))[0]

    def dw_up(nm, dact):
        return _matmul(
            nm, (N_CHIPS, nk),
            [(dact, pl.BlockSpec((None, bk, FF_SHARD), lambda j, k: (j, k, 0)),
              n, pl.BlockSpec((bk, D_MODEL), lambda j, k: (k, 0)), TN)],
            [], [(jax.ShapeDtypeStruct((N_CHIPS, FF_SHARD, D_MODEL), BF16),
                  pl.BlockSpec((None, FF_SHARD, D_MODEL), lambda j, k: (j, 0, 0)))],
            _ident, (FF_SHARD, D_MODEL))[0]

    grads[tag + "_w_gate"] = dw_up(tag + "_dwg", dg)
    grads[tag + "_w_up"] = dw_up(tag + "_dwu", du)

    bn = _row_block(s, 512)
    steps = s // bn // 2
    prev, dgain = (), None
    for part, off in (("_dn_a", 0), ("_dn_b", steps)):
        row = pl.BlockSpec((bn, D_MODEL), lambda i, k, off=off: (i + off, 0))
        terms = []
        for j in range(N_CHIPS):
            a_slab = pl.BlockSpec((None, bn, FF_SHARD), lambda i, k, j=j, off=off: (j, i + off, 0))
            w_slab = pl.BlockSpec((None, FF_SHARD, D_MODEL), lambda i, k, j=j: (j, 0, 0))
            terms += [(dg, a_slab, wg, w_slab, NN), (du, a_slab, wu, w_slab, NN)]
        if norm_bwd is None:
            prev = _matmul(tag + part, (steps, 1), terms, [], [(jax.ShapeDtypeStruct((s, D_MODEL), F32), row)],
                           _ident, None, fill=prev)
            continue
        h, gain, dres = norm_bwd
        vec = pl.BlockSpec((1, D_MODEL), lambda i, k: (0, 0))
        res = _matmul(
            tag + part, (steps, 1), terms, [(h, row), (gain, vec), (dres, row)],
            [(jax.ShapeDtypeStruct((s, D_MODEL), F32), row), (jax.ShapeDtypeStruct((s, D_MODEL), BF16), row),
             (jax.ShapeDtypeStruct((1, D_MODEL), F32), vec)],
            _norm_bwd_epilogue(D_MODEL), None, fill=prev, summed=(2,))
        prev = res[:2]
        dgain = res[2] if dgain is None else dgain + res[2]
    return prev[0] if norm_bwd is None else (prev[0], prev[1], dgain)


def _mm_nn(name, a, b, out_dtype, res=None, gain=None):
    s, k = a.shape
    nn = b.shape[1]
    bm = _row_block(s)
    row = pl.BlockSpec((bm, nn), lambda i, kk: (i, 0))
    term = [(a, pl.BlockSpec((bm, k), lambda i, kk: (i, 0)), b, pl.BlockSpec((k, nn), lambda i, kk: (0, 0)), NN)]
    if res is None:
        return _matmul(name, (s // bm, 1), term, [], [(jax.ShapeDtypeStruct((s, nn), out_dtype), row)], _ident, None)[0]
    extras, outs = _residual_outs(s, bm, gain)
    res_out = _matmul(name, (s // bm, 1), term, [(res, row)] + extras, outs,
                      _residual_epilogue(1.0, gain is not None), None)
    return res_out if gain is not None else res_out[0]


def _mm_nt(name, a, b, out_dtype, attn_out=None, nh=0, dv=0):
    s, nn = a.shape
    k = b.shape[0]
    bm = _row_block(s)
    term = [(a, pl.BlockSpec((bm, nn), lambda i, kk: (i, 0)), b, pl.BlockSpec((k, nn), lambda i, kk: (0, 0)), NT)]
    out = (jax.ShapeDtypeStruct((s, k), out_dtype), pl.BlockSpec((bm, k), lambda i, kk: (i, 0)))
    if attn_out is None:
        return _matmul(name, (s // bm, 1), term, [], [out], _ident, None)[0]

    def with_delta(acc, o):
        do = acc.astype(out_dtype).astype(F32)
        cols = [jnp.sum(do[:, h * dv:(h + 1) * dv] * o[:, h * dv:(h + 1) * dv].astype(F32), axis=-1, keepdims=True)
                for h in range(nh)]
        return acc, jnp.stack(cols, axis=0)

    return _matmul(
        name, (s // bm, 1), term, [(attn_out, pl.BlockSpec((bm, nh * dv), lambda i, kk: (i, 0)))],
        [out, (jax.ShapeDtypeStruct((nh, s, 1), F32), pl.BlockSpec((nh, bm, 1), lambda i, kk: (0, i, 0)))],
        with_delta, None)


def _mm_nt_norm_bwd(name, a, b, h, gain, dres):
    s, nn = a.shape
    bm = _row_block(s)
    extras, outs, summed = _norm_bwd_operands(s, bm, h, gain, dres)
    return _matmul(
        name, (s // bm, 1),
        [(a, pl.BlockSpec((bm, nn), lambda i, kk: (i, 0)), b, pl.BlockSpec(b.shape, lambda i, kk: (0, 0)), NT)],
        extras, outs, _norm_bwd_epilogue(D_MODEL), None, summed=summed)


def _w_in_dx_norm_bwd(dz, w_t, h, gain, dres):
    s = dz.shape[0]
    bm = _row_block(s)
    epilogue = _norm_bwd_epilogue(D_MODEL)

    def body(dz_ref, w_ref, h_ref, g_ref, r_ref, dx_ref, dxb_ref, dg_ref):
        dzv = dz_ref[...]
        dn = jnp.concatenate([_dot(dzv, w_ref[j], NN) for j in range(N_CHIPS)], axis=1)
        dx, _, dgain = epilogue(dn, h_ref[...], g_ref[...], r_ref[...])
        dx_ref[...] = dx
        dxb_ref[...] = dx.astype(BF16)

        @pl.when(pl.program_id(0) == 0)
        def _():
            dg_ref[...] = dgain

        @pl.when(pl.program_id(0) > 0)
        def _():
            dg_ref[...] += dgain

    row = pl.BlockSpec((bm, D_MODEL), lambda i: (i, 0))
    vec = pl.BlockSpec((1, D_MODEL), lambda i: (0, 0))
    return _call(
        body, name="w_in_dx", grid=(s // bm,),
        in_specs=[row, pl.BlockSpec(w_t.shape, lambda i: (0, 0, 0)), row, vec, row],
        out_specs=[row, row, vec],
        out_shape=[jax.ShapeDtypeStruct((s, D_MODEL), F32), jax.ShapeDtypeStruct((s, D_MODEL), BF16),
                   jax.ShapeDtypeStruct((1, D_MODEL), F32)],
        sem=("arbitrary",), args=[dz, w_t, h, gain, dres])


def _mm_tn(name, a, b, out_dtype=BF16):
    s, k = a.shape
    nn = b.shape[1]
    bk = _row_block(s)
    return _matmul(
        name, (s // bk,),
        [(a, pl.BlockSpec((bk, k), lambda kk: (kk, 0)), b, pl.BlockSpec((bk, nn), lambda kk: (kk, 0)), TN)],
        [], [(jax.ShapeDtypeStruct((k, nn), out_dtype), pl.BlockSpec((k, nn), lambda kk: (0, 0)))],
        _ident, (k, nn))[0]


def _mm_heads_fwd(name, a, w, out_dtype, w_transposed=False):
    s, k = a.shape
    nh = w.shape[0]
    nn = w.shape[1] if w_transposed else w.shape[2]
    bm = _row_block(s)
    return _matmul(
        name, (nh, s // bm, 1),
        [(a, pl.BlockSpec((bm, k), lambda h, i, kk: (i, 0)),
          w, pl.BlockSpec((None,) + w.shape[1:], lambda h, i, kk: (h, 0, 0)), NT if w_transposed else NN)],
        [], [(jax.ShapeDtypeStruct((s, nh * nn), out_dtype), pl.BlockSpec((bm, nn), lambda h, i, kk: (i, h)))],
        _ident, None)[0]


def _mm_heads_bwd(name, dy, a, w, w_transposed=False):
    s, k = a.shape
    nh = w.shape[0]
    nn = w.shape[1] if w_transposed else w.shape[2]
    bm = _row_block(s)
    bk = _row_block(s)
    w_spec = pl.BlockSpec((None,) + w.shape[1:], lambda i, h: (h, 0, 0))
    da = _matmul(
        name + "_dx", (s // bm, nh),
        [(dy, pl.BlockSpec((bm, nn), lambda i, h: (i, h)), w, w_spec, NN if w_transposed else NT)],
        [], [(jax.ShapeDtypeStruct((s, k), F32), pl.BlockSpec((bm, k), lambda i, h: (i, 0)))], _ident, (bm, k))[0]
    a_term = (a, pl.BlockSpec((bk, k), lambda h, kk: (kk, 0)))
    dy_term = (dy, pl.BlockSpec((bk, nn), lambda h, kk: (kk, h)))
    lhs, rhs = (dy_term, a_term) if w_transposed else (a_term, dy_term)
    dw = _matmul(
        name + "_dw", (nh, s // bk), [lhs + rhs + (TN,)],
        [], [(jax.ShapeDtypeStruct(w.shape, BF16), pl.BlockSpec((None,) + w.shape[1:], lambda h, kk: (h, 0, 0)))],
        _ident, w.shape[1:])[0]
    return da, dw


def _w_in_fwd(n, w_t):
    s = n.shape[0]
    bm = _row_block(s)
    nh, nout, kin = w_t.shape
    terms = [(n, pl.BlockSpec((bm, kin), lambda i, k, j=j: (i, j)),
              w_t, pl.BlockSpec((None, nout, kin), lambda i, k, j=j: (j, 0, 0)), NT) for j in range(nh)]
    row = pl.BlockSpec((bm, nout), lambda i, k: (i, 0))
    return _matmul("w_in", (s // bm, 1), terms, [], [(jax.ShapeDtypeStruct((s, nout), F32), row)], _ident, None)[0]


def _w_in_dw(dz, n):
    s, nout = dz.shape
    kin = n.shape[1] // N_CHIPS
    bk = _row_block(s)
    return _matmul(
        "w_in_dw", (N_CHIPS, s // bk),
        [(dz, pl.BlockSpec((bk, nout), lambda j, k: (k, 0)), n, pl.BlockSpec((bk, kin), lambda j, k: (k, j)), TN)],
        [], [(jax.ShapeDtypeStruct((N_CHIPS, nout, kin), BF16), pl.BlockSpec((None, nout, kin), lambda j, k: (j, 0, 0)))],
        _ident, (nout, kin))[0]


def _rope_tables(positions):
    half = ROPE_DIM // 2
    freqs = 1.0 / (ROPE_BASE ** (jnp.arange(0, ROPE_DIM, 2, dtype=F32) / ROPE_DIM))
    ang = positions.astype(F32)[:, None] * freqs
    cos, sin = jnp.cos(ang), jnp.sin(ang)
    z = jnp.zeros_like(cos)
    tc = jnp.concatenate([cos, cos, z, z], axis=-1)
    ta = jnp.concatenate([-sin, z, z, z], axis=-1)
    tb = jnp.concatenate([z, sin, z, z], axis=-1)
    assert tc.shape[-1] == 4 * half
    return tc, ta, tb


def _rope(x, tc, ta, tb):
    return x * tc + pltpu.roll(x, 96, 1) * ta + pltpu.roll(x, 32, 1) * tb


def _rope_t(dy, tc, ta, tb):
    return dy * tc + pltpu.roll(dy * ta, 32, 1) + pltpu.roll(dy * tb, 96, 1)


def _q_rope(q, tc, ta, tb, transpose):
    s = q.shape[0]
    bm = _row_block(s, 512)
    rot = _rope_t if transpose else _rope

    def body(q_ref, tc_ref, ta_ref, tb_ref, o_ref):
        c, a, b = tc_ref[...], ta_ref[...], tb_ref[...]
        for h in range(MLA_HEADS):
            lo = h * HEAD_QK
            o_ref[:, lo:lo + 128] = q_ref[:, lo:lo + 128].astype(BF16)
            o_ref[:, lo + 128:lo + 256] = rot(q_ref[:, lo + 128:lo + 256], c, a, b).astype(BF16)

    row = pl.BlockSpec((bm, MLA_HEADS * HEAD_QK), lambda i: (i, 0))
    tab = pl.BlockSpec((bm, 128), lambda i: (i, 0))
    return pl.pallas_call(
        body, name="q_rope_t" if transpose else "q_rope", grid=(s // bm,), in_specs=[row, tab, tab, tab],
        out_specs=row, out_shape=jax.ShapeDtypeStruct((s, MLA_HEADS * HEAD_QK), BF16),
        compiler_params=_params("parallel"),
    )(q, tc, ta, tb)


def _kv_assemble(kv, z, tc, ta, tb):
    s = kv.shape[0]
    bm = _row_block(s, 512)

    def body(kv_ref, kr_ref, tc_ref, ta_ref, tb_ref, k_ref, v_ref):
        kpe = _rope(kr_ref[...], tc_ref[...], ta_ref[...], tb_ref[...]).astype(BF16)
        for h in range(MLA_HEADS):
            lo = h * 256
            k_ref[:, lo:lo + 128] = kv_ref[:, lo:lo + 128].astype(BF16)
            k_ref[:, lo + 128:lo + 256] = kpe
            v_ref[:, h * 128:(h + 1) * 128] = kv_ref[:, lo + 128:lo + 256].astype(BF16)

    row = pl.BlockSpec((bm, 1024), lambda i: (i, 0))
    tab = pl.BlockSpec((bm, 128), lambda i: (i, 0))
    return pl.pallas_call(
        body, name="kv_assemble", grid=(s // bm,),
        in_specs=[row, pl.BlockSpec((bm, 128), lambda i: (i, 3)), tab, tab, tab],
        out_specs=[row, pl.BlockSpec((bm, 512), lambda i: (i, 0))],
        out_shape=[jax.ShapeDtypeStruct((s, 1024), BF16), jax.ShapeDtypeStruct((s, 512), BF16)],
        compiler_params=_params("parallel"),
    )(kv, z, tc, ta, tb)


def _kv_assemble_bwd(dk, dv, tc, ta, tb):
    s = dk.shape[0]
    bm = _row_block(s, 512)

    def body(dk_ref, dv_ref, tc_ref, ta_ref, tb_ref, dkv_ref, dkr_ref):
        dpe = None
        for h in range(MLA_HEADS):
            lo = h * 256
            dkv_ref[:, lo:lo + 128] = dk_ref[:, lo:lo + 128].astype(BF16)
            dkv_ref[:, lo + 128:lo + 256] = dv_ref[:, h * 128:(h + 1) * 128].astype(BF16)
            t = dk_ref[:, lo + 128:lo + 256]
            dpe = t if dpe is None else dpe + t
        dkr_ref[...] = _rope_t(dpe, tc_ref[...], ta_ref[...], tb_ref[...])

    row = pl.BlockSpec((bm, 1024), lambda i: (i, 0))
    tab = pl.BlockSpec((bm, 128), lambda i: (i, 0))
    return pl.pallas_call(
        body, name="kv_assemble_bwd", grid=(s // bm,),
        in_specs=[row, pl.BlockSpec((bm, 512), lambda i: (i, 0)), tab, tab, tab],
        out_specs=[row, tab],
        out_shape=[jax.ShapeDtypeStruct((s, 1024), BF16), jax.ShapeDtypeStruct((s, 128), F32)],
        compiler_params=_params("parallel"),
    )(dk, dv, tc, ta, tb)


def _norm_bf16(x, g):
    r = lax.rsqrt(jnp.mean(x * x, axis=-1, keepdims=True) + RMS_EPS)
    return ((x * r) * g).astype(BF16)


def _qkv_prep(z, q_gain, kv_gain, wq_t, wkv, tc, ta, tb):
    s = z.shape[0]
    bm = _row_block(s, 512)

    def body(zq_ref, zkv_ref, zkr_ref, qg_ref, kvg_ref, wq_ref, wkv_ref, tc_ref, ta_ref, tb_ref,
             qn_ref, kvn_ref, q_ref, k_ref, v_ref):
        c, a, b = tc_ref[...], ta_ref[...], tb_ref[...]
        qn = _norm_bf16(zq_ref[...], qg_ref[...])
        kvn = _norm_bf16(zkv_ref[...], kvg_ref[...])
        qn_ref[...] = qn
        kvn_ref[...] = kvn
        kpe = _rope(zkr_ref[...], c, a, b).astype(BF16)
        for h in range(MLA_HEADS):
            lo = h * HEAD_QK
            qp = _dot(qn, wq_ref[h], NT)
            q_ref[:, lo:lo + 128] = qp[:, :128].astype(BF16)
            q_ref[:, lo + 128:lo + 256] = _rope(qp[:, 128:], c, a, b).astype(BF16)
            kv = _dot(kvn, wkv_ref[h], NN)
            k_ref[:, lo:lo + 128] = kv[:, :128].astype(BF16)
            k_ref[:, lo + 128:lo + 256] = kpe
            v_ref[:, h * HEAD_V:(h + 1) * HEAD_V] = kv[:, 128:].astype(BF16)

    def cols(width, blk):
        return pl.BlockSpec((bm, width), lambda i: (i, blk))

    def whole(a):
        return pl.BlockSpec(a.shape, lambda i: (0,) * a.ndim)

    tab = cols(128, 0)
    return pl.pallas_call(
        body, name="qkv_prep", grid=(s // bm,),
        in_specs=[cols(Q_LORA, 0), cols(KV_LORA, 2), cols(128, 3), whole(q_gain), whole(kv_gain), whole(wq_t),
                  whole(wkv), tab, tab, tab],
        out_specs=[cols(Q_LORA, 0), cols(KV_LORA, 0), cols(1024, 0), cols(1024, 0), cols(512, 0)],
        out_shape=[jax.ShapeDtypeStruct((s, Q_LORA), BF16), jax.ShapeDtypeStruct((s, KV_LORA), BF16),
                   jax.ShapeDtypeStruct((s, 1024), BF16), jax.ShapeDtypeStruct((s, 1024), BF16),
                   jax.ShapeDtypeStruct((s, 512), BF16)],
        compiler_params=_params("parallel"),
    )(z, z, z, q_gain, kv_gain, wq_t, wkv, tc, ta, tb)


def _qkv_prep_bwd(dq, dk, dv, z, qn, kvn, q_gain, kv_gain, wq_t, wkv, tc, ta, tb):
    s = z.shape[0]
    bm = _row_block(s, 512)
    nsteps = s // bm

    def body(dq_ref, dk_ref, dv_ref, zq_ref, zkv_ref, qn_ref, kvn_ref, qg_ref, kvg_ref, wq_ref, wkv_ref,
             tc_ref, ta_ref, tb_ref, dz_ref, dqg_ref, dkvg_ref, dwq_ref, dwkv_ref, wq_acc, wkv_acc):
        i = pl.program_id(0)
        c, a, b = tc_ref[...], ta_ref[...], tb_ref[...]

        @pl.when(i == 0)
        def _():
            wq_acc[...] = jnp.zeros_like(wq_acc)
            wkv_acc[...] = jnp.zeros_like(wkv_acc)

        qn, kvn = qn_ref[...], kvn_ref[...]
        dqn = jnp.zeros((bm, Q_LORA), F32)
        dkvn = jnp.zeros((bm, KV_LORA), F32)
        dpe = jnp.zeros((bm, 128), F32)
        for h in range(MLA_HEADS):
            lo = h * HEAD_QK
            dqp = jnp.concatenate([dq_ref[:, lo:lo + 128].astype(BF16),
                                   _rope_t(dq_ref[:, lo + 128:lo + 256], c, a, b).astype(BF16)], axis=1)
            dqn = dqn + _dot(dqp, wq_ref[h], NN)
            wq_acc[h] += _dot(dqp, qn, TN)
            dkv = jnp.concatenate([dk_ref[:, lo:lo + 128].astype(BF16),
                                   dv_ref[:, h * HEAD_V:(h + 1) * HEAD_V].astype(BF16)], axis=1)
            dkvn = dkvn + _dot(dkv, wkv_ref[h], NT)
            wkv_acc[h] += _dot(kvn, dkv, TN)
            dpe = dpe + dk_ref[:, lo + 128:lo + 256]
        dcq, dqg = _rms_bwd_math(dqn, zq_ref[...], qg_ref[...], Q_LORA)
        dckv, dkvg = _rms_bwd_math(dkvn, zkv_ref[...], kvg_ref[...], KV_LORA)
        dz_ref[:, 0:Q_LORA] = dcq.astype(BF16)
        dz_ref[:, Q_LORA:Q_LORA + KV_LORA] = dckv.astype(BF16)
        dz_ref[:, Q_LORA + KV_LORA:512] = _rope_t(dpe, c, a, b).astype(BF16)

        @pl.when(i == 0)
        def _():
            dqg_ref[...] = dqg
            dkvg_ref[...] = dkvg

        @pl.when(i > 0)
        def _():
            dqg_ref[...] += dqg
            dkvg_ref[...] += dkvg

        @pl.when(i == nsteps - 1)
        def _():
            dwq_ref[...] = wq_acc[...].astype(BF16)
            dwkv_ref[...] = wkv_acc[...].astype(BF16)

    def cols(width, blk):
        return pl.BlockSpec((bm, width), lambda i: (i, blk))

    def whole(shape):
        return pl.BlockSpec(shape, lambda i: (0,) * len(shape))

    tab = cols(128, 0)
    return _call(
        body, name="qkv_prep_bwd", grid=(nsteps,),
        in_specs=[cols(1024, 0), cols(1024, 0), cols(512, 0), cols(Q_LORA, 0), cols(KV_LORA, 2), cols(Q_LORA, 0),
                  cols(KV_LORA, 0), whole(q_gain.shape), whole(kv_gain.shape), whole(wq_t.shape), whole(wkv.shape),
                  tab, tab, tab],
        out_specs=[cols(512, 0), whole(q_gain.shape), whole(kv_gain.shape), whole(wq_t.shape), whole(wkv.shape)],
        out_shape=[jax.ShapeDtypeStruct((s, 512), BF16), jax.ShapeDtypeStruct(q_gain.shape, F32),
                   jax.ShapeDtypeStruct(kv_gain.shape, F32), jax.ShapeDtypeStruct(wq_t.shape, BF16),
                   jax.ShapeDtypeStruct(wkv.shape, BF16)],
        scratch_shapes=[pltpu.VMEM(wq_t.shape, F32), pltpu.VMEM(wkv.shape, F32)],
        sem=("arbitrary",), args=[dq, dk, dv, z, z, qn, kvn, q_gain, kv_gain, wq_t, wkv, tc, ta, tb])


def _causal_mask(s, row0, col0):
    rows = row0 + lax.broadcasted_iota(jnp.int32, s.shape, 0)
    cols = col0 + lax.broadcasted_iota(jnp.int32, s.shape, 1)
    return jnp.where(cols <= rows, s, -jnp.inf)


def _attn_fwd(name, q, k, k_off, v, v_off, nh, dq, dv, scale, causal, blk):
    sq, sk = q.shape[0], k.shape[0]
    bq = min(blk, sq)
    bk = min(blk, sk)
    nkv = sk // bk
    assert not causal or (sq == sk and bq == bk)

    hq = bq
    log2e = 1.4426950408889634
    c2 = scale * log2e

    def body(q_ref, k_ref, v_ref, o_ref, lse_ref):
        qi = pl.program_id(1)
        qs = (q_ref[...],)

        def step(j, carry, masked):
            rows = pl.ds(pl.multiple_of(j * bk, bk), bk)
            kb, vb = k_ref[rows, :], v_ref[rows, :]
            out = []
            for t, (m, l, acc) in enumerate(carry):
                s = _dot(qs[t], kb, NT) * c2
                if masked:
                    s = _causal_mask(s, qi * bq + t * hq, j * bk)
                m_new = jnp.maximum(m, jnp.max(s, axis=-1, keepdims=True))
                alpha = jnp.exp2(m - m_new)
                p = jnp.exp2(s - m_new)
                l = alpha * l + jnp.sum(p, axis=-1, keepdims=True)
                acc = alpha * acc + _dot(p, vb, NN)
                out.append((m_new, l, acc))
            return tuple(out)

        one = (jnp.full((hq, 1), -jnp.inf, F32), jnp.zeros((hq, 1), F32), jnp.zeros((hq, dv), F32))
        init = (one,)
        if causal:
            carry = lax.fori_loop(0, qi, lambda j, c: step(j, c, False), init)
            fin = step(qi, carry, True)
        else:
            fin = lax.fori_loop(0, nkv, lambda j, c: step(j, c, False), init)
        for t, (m, l, acc) in enumerate(fin):
            o_ref[t * hq:(t + 1) * hq, :] = (acc / l).astype(o_ref.dtype)
            lse_ref[t * hq:(t + 1) * hq, :] = m * (1.0 / log2e) + jnp.log(l)

    return _call(
        body, name=name, grid=(nh, sq // bq),
        in_specs=[pl.BlockSpec((bq, dq), lambda h, i: (i, h)),
                  pl.BlockSpec((sk, dq), lambda h, i: (0, k_off + h)),
                  pl.BlockSpec((sk, dv), lambda h, i: (0, v_off + h))],
        out_specs=[pl.BlockSpec((bq, dv), lambda h, i: (i, h)), pl.BlockSpec((None, bq, 1), lambda h, i: (h, i, 0))],
        out_shape=[jax.ShapeDtypeStruct((sq, nh * dv), BF16), jax.ShapeDtypeStruct((nh, sq, 1), F32)],
        sem=("parallel", "parallel"), args=[q, k, v])


def _attn_delta(name, do, do_off, o, nh, dv):
    s = o.shape[0]
    bm = _row_block(s, 512)

    def body(do_ref, o_ref, d_ref):
        d_ref[...] = jnp.sum(do_ref[...].astype(F32) * o_ref[...].astype(F32), axis=-1, keepdims=True)

    return pl.pallas_call(
        body, name=name, grid=(nh, s // bm),
        in_specs=[pl.BlockSpec((bm, dv), lambda h, i: (i, do_off + h)), pl.BlockSpec((bm, dv), lambda h, i: (i, h))],
        out_specs=pl.BlockSpec((None, bm, 1), lambda h, i: (h, i, 0)),
        out_shape=jax.ShapeDtypeStruct((nh, s, 1), F32),
        compiler_params=_params("parallel", "parallel"),
    )(do, o)


def _attn_bwd(name, q, k, k_off, v, v_off, do, do_off, lse, delta, nh, dq, dv, scale, causal, blk):
    sq, sk = q.shape[0], k.shape[0]
    bq = min(blk, sq)
    bk = min(blk, sk)
    nq = sq // bq
    assert not causal or (sq == sk and bq == bk)

    def body(q_ref, k_ref, v_ref, do_ref, lse_ref, dl_ref, dq_ref, dk_ref, dv_ref, dk_acc, dv_acc):
        j = pl.program_id(1)

        @pl.when(j == 0)
        def _():
            dq_ref[...] = jnp.zeros_like(dq_ref)

        dk_acc[...] = jnp.zeros_like(dk_acc)
        dv_acc[...] = jnp.zeros_like(dv_acc)
        kv = k_ref[...]
        vv = v_ref[...]

        def step(i, masked):
            rows = pl.ds(pl.multiple_of(i * bq, bq), bq)
            qv = q_ref[rows, :]
            dov = do_ref[rows, :].astype(BF16)
            s = _dot(qv, kv, NT) * scale
            if masked:
                s = _causal_mask(s, i * bq, j * bk)
            p = jnp.exp(s - lse_ref[rows, :])
            dp = _dot(dov, vv, NT)
            ds = (p * (dp - dl_ref[rows, :]) * scale).astype(BF16)
            dv_acc[...] += _dot(p, dov, TN)
            dk_acc[...] += _dot(ds, qv, TN)
            dq_ref[rows, :] += _dot(ds, kv, NN)

        if causal:
            step(j, True)

            def loop(i, c):
                step(i, False)
                return c

            lax.fori_loop(j + 1, nq, loop, 0)
        else:
            def loop(i, c):
                step(i, False)
                return c

            lax.fori_loop(0, nq, loop, 0)
        dk_ref[...] = dk_acc[...]
        dv_ref[...] = dv_acc[...]

    stat = pl.BlockSpec((None, sq, 1), lambda h, j: (h, 0, 0))
    return _call(
        body, name=name, grid=(nh, sk // bk),
        in_specs=[pl.BlockSpec((sq, dq), lambda h, j: (0, h)),
                  pl.BlockSpec((bk, dq), lambda h, j: (j, k_off + h)),
                  pl.BlockSpec((bk, dv), lambda h, j: (j, v_off + h)),
                  pl.BlockSpec((sq, dv), lambda h, j: (0, do_off + h)), stat, stat],
        out_specs=[pl.BlockSpec((sq, dq), lambda h, j: (0, h)),
                   pl.BlockSpec((bk, dq), lambda h, j: (j, h)),
                   pl.BlockSpec((bk, dv), lambda h, j: (j, h))],
        out_shape=[jax.ShapeDtypeStruct((sq, nh * dq), F32), jax.ShapeDtypeStruct((sk, nh * dq), F32),
                   jax.ShapeDtypeStruct((sk, nh * dv), F32)],
        scratch_shapes=[pltpu.VMEM((bk, dq), F32), pltpu.VMEM((bk, dv), F32)],
        sem=("parallel", "arbitrary"), args=[q, k, v, do, lse, delta])


def _pool_diff(z, g):
    s = z.shape[0]
    t = lax.broadcasted_iota(jnp.int32, z.shape, 0)
    acc = z
    sums = []
    for k in (1, 2, 4, 8):
        acc = acc + jnp.where(t >= k, pltpu.roll(acc, k, 0), 0.0)
        sums.append(acc)
    win = jnp.where(g == 0, sums[0], jnp.where(g == 1, sums[1], jnp.where(g == 2, sums[2], sums[3])))
    w = lax.shift_left(jnp.int32(2), g)
    count = jnp.minimum(t + 1, w).astype(F32)
    del s
    return win / count - z, count


def _pool_fwd(z, pool_w, pool_scale):
    s = z.shape[0]

    def body(z_ref, w_ref, sc_ref, o_ref):
        diff, _ = _pool_diff(z_ref[...], pl.program_id(0))
        o_ref[...] = (_dot(diff, w_ref[...], NN) * sc_ref[...]).astype(o_ref.dtype)

    return _call(
        body, name="pool_fwd", grid=(POOL_GROUPS,),
        in_specs=[pl.BlockSpec((s, POOL_CH), lambda g: (0, 4 + g)),
                  pl.BlockSpec((None, POOL_CH, POOL_CH), lambda g: (g, 0, 0)),
                  pl.BlockSpec((1, POOL_CH), lambda g: (0, g))],
        out_specs=[pl.BlockSpec((s, POOL_CH), lambda g: (0, g))],
        out_shape=[jax.ShapeDtypeStruct((s, POOL_GROUPS * POOL_CH), BF16)],
        sem=("parallel",), args=[z, pool_w, pool_scale])[0]


def _pool_bwd(dcat, z, pool_w, pool_scale):
    s = z.shape[0]

    def body(dp_ref, z_ref, w_ref, sc_ref, dz_ref, dw_ref, dsc_ref):
        g = pl.program_id(0)
        diff, count = _pool_diff(z_ref[...], g)
        dpf = dp_ref[...].astype(F32)
        u = _dot(diff, w_ref[...], NN)
        dsc_ref[...] = jnp.sum(dpf * u, axis=0, keepdims=True)
        du = (dpf * sc_ref[...]).astype(BF16)
        dw_ref[...] = _dot(diff, du, TN)
        ddiff = _dot(du, w_ref[...], NT)
        t = lax.broadcasted_iota(jnp.int32, ddiff.shape, 0)
        acc = ddiff / count
        sums = []
        for k in (1, 2, 4, 8):
            acc = acc + jnp.where(t < s - k, pltpu.roll(acc, s - k, 0), 0.0)
            sums.append(acc)
        win = jnp.where(g == 0, sums[0], jnp.where(g == 1, sums[1], jnp.where(g == 2, sums[2], sums[3])))
        dz_ref[...] = win - ddiff

    return pl.pallas_call(
        body, name="pool_bwd", grid=(POOL_GROUPS,),
        in_specs=[pl.BlockSpec((s, POOL_CH), lambda g: (0, 4 + g)),
                  pl.BlockSpec((s, POOL_CH), lambda g: (0, 4 + g)),
                  pl.BlockSpec((None, POOL_CH, POOL_CH), lambda g: (g, 0, 0)),
                  pl.BlockSpec((1, POOL_CH), lambda g: (0, g))],
        out_specs=[pl.BlockSpec((s, POOL_CH), lambda g: (0, g)),
                   pl.BlockSpec((None, POOL_CH, POOL_CH), lambda g: (g, 0, 0)),
                   pl.BlockSpec((1, POOL_CH), lambda g: (0, g))],
        out_shape=[jax.ShapeDtypeStruct((s, POOL_GROUPS * POOL_CH), F32),
                   jax.ShapeDtypeStruct((POOL_GROUPS, POOL_CH, POOL_CH), F32),
                   jax.ShapeDtypeStruct((1, POOL_GROUPS * POOL_CH), F32)],
        compiler_params=_params("parallel"),
    )(dcat, z, pool_w, pool_scale)


def _local_step(x, mem, positions, target, w, grads):
    tc, ta, tb = _rope_tables(positions)
    blk = _ATT_BLOCK

    n1 = _rmsnorm_fwd("ffn1_norm", x, w["ffn1_norm"], D_MODEL)
    a1, dadu1, dadg1 = _ffn_up("ffn1_up", n1, w["ffn1_w_gate"], w["ffn1_w_up"])
    h1, n2 = _ffn_down("ffn1_down", a1, w["ffn1_w_down"], x, w["mix_norm"])
    z = _w_in_fwd(n2, w["w_in"])
    qn, kvn, qf, kf, vf = _qkv_prep(z, w["q_norm"], w["kv_norm"], w["w_q_up"], w["w_kv_up"], tc, ta, tb)
    att, lse = _attn_fwd("mla_fwd", qf, kf, 0, vf, 0, MLA_HEADS, HEAD_QK, HEAD_V, MLA_SCALE, True, blk)
    pool = _pool_fwd(z, w["pool_w"], w["pool_scale"])
    s = x.shape[0]
    bm = _row_block(s)
    row = pl.BlockSpec((bm, D_MODEL), lambda i, k: (i, 0))
    half = pl.BlockSpec((bm, 512), lambda i, k: (i, 0))
    h2, n3 = _matmul(
        "w_out", (s // bm, 1),
        [(att, half, w["w_out"], pl.BlockSpec((512, D_MODEL), lambda i, k: (0, 0)), NN),
         (pool, half, w["w_out"], pl.BlockSpec((512, D_MODEL), lambda i, k: (1, 0)), NN)],
        [(h1, row)] + _residual_outs(s, bm, w["xattn_norm"])[0], _residual_outs(s, bm, w["xattn_norm"])[1],
        _residual_epilogue(1.0, True), None)
    memn = _rmsnorm_fwd("mem_norm", mem, w["mem_norm"], D_MODEL)
    qm = _mm_nn("w_mq", n3, w["w_mq"], BF16)
    kvm = _mm_heads_fwd("w_mkv", memn, w["w_mkv"], BF16)
    om, lse_m = _attn_fwd("xattn_fwd", qm, kvm, 0, kvm, MEM_HEADS, MEM_HEADS, MEM_HEAD_DIM, MEM_HEAD_DIM,
                          MEM_SCALE, False, blk)
    h3, n4 = _mm_nn("w_mo", om, w["w_mo"], F32, res=h2, gain=w["ffn2_norm"])
    a2, dadu2, dadg2 = _ffn_up("ffn2_up", n4, w["ffn2_w_gate"], w["ffn2_w_up"])
    h4 = _ffn_down("ffn2_down", a2, w["ffn2_w_down"], h3)

    dh4, dh4b, loss_vec, d_final = _loss_and_final_norm(h4, w["final_norm"], target)
    grads["final_norm"] = d_final

    dh3, dh3b, grads["ffn2_norm"] = _ffn_bwd("ffn2", dh4b, n4, dadg2, dadu2, a2, w["ffn2_w_gate"], w["ffn2_w_up"],
                                             w["ffn2_w_down"], grads, norm_bwd=(h3, w["ffn2_norm"], dh4))

    dom, delta_m = _mm_nt("w_mo_dx", dh3b, w["w_mo"], BF16, attn_out=om, nh=MEM_HEADS, dv=MEM_HEAD_DIM)
    grads["w_mo"] = _mm_tn("w_mo_dw", om, dh3b)
    dqm, dkm, dvm = _attn_bwd("xattn_bwd", qm, kvm, 0, kvm, MEM_HEADS, dom, 0, lse_m, delta_m, MEM_HEADS,
                              MEM_HEAD_DIM, MEM_HEAD_DIM, MEM_SCALE, False, blk)
    dkvm = jnp.concatenate([dkm, dvm], axis=1).astype(BF16)
    dh2, dh2b, grads["xattn_norm"] = _mm_nt_norm_bwd("w_mq_dx", dqm, w["w_mq"], h2, w["xattn_norm"], dh3)
    grads["w_mq"] = _mm_tn("w_mq_dw", n3, dqm)
    dmemn, grads["w_mkv"] = _mm_heads_bwd("w_mkv", dkvm, memn, w["w_mkv"])
    _, grads["mem_norm"] = _rmsnorm_bwd("mem_norm_bwd", dmemn, mem, w["mem_norm"], D_MODEL, out_dtype=BF16)

    dcat, delta = _mm_nt("w_out_dx", dh2b, w["w_out"], BF16, attn_out=att, nh=MLA_HEADS, dv=HEAD_V)
    grads["w_out"] = jnp.concatenate([_mm_tn("w_out_dw_a", att, dh2b), _mm_tn("w_out_dw_p", pool, dh2b)], axis=0)
    dzp, grads["pool_w"], grads["pool_scale"] = _pool_bwd(dcat, z, w["pool_w"], w["pool_scale"])
    dqf, dkf, dvf = _attn_bwd("mla_bwd", qf, kf, 0, vf, 0, dcat, 0, lse, delta, MLA_HEADS, HEAD_QK, HEAD_V,
                              MLA_SCALE, True, blk)
    dz_lat, grads["q_norm"], grads["kv_norm"], grads["w_q_up"], grads["w_kv_up"] = _qkv_prep_bwd(
        dqf, dkf, dvf, z, qn, kvn, w["q_norm"], w["kv_norm"], w["w_q_up"], w["w_kv_up"], tc, ta, tb)
    dz = jnp.concatenate([dz_lat, dzp.astype(BF16)], axis=1)
    grads["w_in"] = _w_in_dw(dz, n2)
    dh1, dh1b, grads["mix_norm"] = _w_in_dx_norm_bwd(dz, w["w_in"], h1, w["mix_norm"], dh2)

    dn1 = _ffn_bwd("ffn1", dh1b, n1, dadg1, dadu1, a1, w["ffn1_w_gate"], w["ffn1_w_up"], w["ffn1_w_down"], grads)
    dx, grads["ffn1_norm"], _ = _rmsnorm_bwd("ffn1_norm_bwd", dn1, x, w["ffn1_norm"], D_MODEL, dres=dh1)
    return loss_vec[0, 0], dx


def _mesh_pos():
    x, y, c = lax.axis_index("x"), lax.axis_index("y"), lax.axis_index("c")
    chips = [(1 - x, y), (x, 1 - y), (1 - x, 1 - y)]
    chip_ids = [2 * cx + cy for cx, cy in chips]
    return x, y, c, 2 * x + y, chips, chip_ids


def _half_rows(c, rows):
    hr = rows // 2
    return pl.ds(pl.multiple_of(c * hr, 16), hr), pl.ds(pl.multiple_of((1 - c) * hr, 16), hr)


def _ag_ici_stage(shards):
    n = len(shards)

    def copies(ins, outs):
        x, y, c, me, chips, _ = _mesh_pos()
        out = []
        for k in range(n):
            mine, _ = _half_rows(c, ins[k].shape[0])
            out.append((ins[k], outs[k].at[me], None))
            for cx, cy in chips:
                out.append((ins[k].at[mine], outs[k].at[me, mine], (cx, cy, c)))
        return out

    return _Stage(shards, [jax.ShapeDtypeStruct((N_CHIPS,) + s.shape, s.dtype) for s in shards], 3 * n, n, copies)


def _ag_d2d_stage(fulls):
    n = len(fulls)

    def copies(ins, outs):
        x, y, c, me, _, chip_ids = _mesh_pos()
        out = []
        for k in range(n):
            mine, _ = _half_rows(c, ins[k].shape[1])
            for j in range(3):
                out.append((ins[k].at[chip_ids[j], mine], outs[k].at[chip_ids[j], mine], (x, y, 1 - c)))
        return out

    return _Stage(fulls, [jax.ShapeDtypeStruct(f.shape, f.dtype) for f in fulls], 3 * n, 0, copies,
                  aliases={k: k for k in range(n)})


def _rs_swap_stage(grads):
    n = len(grads)

    def copies(ins, outs):
        x, y, c, _, _, _ = _mesh_pos()
        out = []
        for k in range(n):
            _, other = _half_rows(c, ins[k].shape[1])
            out.append((ins[k].at[:, other, :], outs[k], (x, y, 1 - c)))
        return out

    return _Stage(grads, [jax.ShapeDtypeStruct((N_CHIPS, g.shape[1] // 2, g.shape[2]), g.dtype) for g in grads],
                  n, 0, copies)


def _rs_scatter_stage(sums):
    n = len(sums)

    def copies(ins, outs):
        x, y, c, me, chips, chip_ids = _mesh_pos()
        out = []
        for k in range(n):
            mine, _ = _half_rows(c, 2 * ins[k].shape[1])
            out.append((ins[k].at[me], outs[k].at[0, mine, :], None))
            for j, (cx, cy) in enumerate(chips):
                out.append((ins[k].at[chip_ids[j]], outs[k].at[1 + j, mine, :], (cx, cy, c)))
        return out

    return _Stage(sums, [jax.ShapeDtypeStruct((N_CHIPS, 2 * s.shape[1], s.shape[2]), s.dtype) for s in sums],
                  3 * n, n, copies)


def _rs_mirror_stage(parts):
    n = len(parts)

    def copies(ins, outs):
        x, y, c, _, _, _ = _mesh_pos()
        out = []
        for k in range(n):
            mine, _ = _half_rows(c, ins[k].shape[1])
            out.append((ins[k].at[:, mine, :], outs[k].at[:, mine, :], (x, y, 1 - c)))
        return out

    return _Stage(parts, [jax.ShapeDtypeStruct(p.shape, p.dtype) for p in parts], n, 0, copies,
                  aliases={k: k for k in range(n)})


def _pair_add(name, g, r1, core):
    _, rows, cols = g.shape
    hr = rows // 2

    def body(c_ref, g_ref, r_ref, o_ref):
        o_ref[...] = (g_ref[...].astype(F32) + r_ref[...].astype(F32)).astype(BF16)

    half = pl.BlockSpec((None, hr, cols), lambda j, c: (j, 0, 0))
    return pl.pallas_call(
        body, name=name,
        grid_spec=pltpu.PrefetchScalarGridSpec(
            num_scalar_prefetch=1, grid=(N_CHIPS,),
            in_specs=[pl.BlockSpec((None, hr, cols), lambda j, c: (j, c[0], 0)), half], out_specs=half),
        out_shape=jax.ShapeDtypeStruct((N_CHIPS, hr, cols), BF16),
        compiler_params=_params("parallel"),
    )(core, g, r1)


def _all_gather_weights(shards):
    n = len(shards)

    def body(*refs):
        ins, outs = refs[:n], refs[n:2 * n]
        send, recv, loc = refs[2 * n:]
        x, y, c, me, chips, chip_ids = _mesh_pos()
        sib = (x, y, 1 - c)

        def halves(k):
            hr = ins[k].shape[0] // 2
            return pl.ds(pl.multiple_of(c * hr, 16), hr), pl.ds(pl.multiple_of((1 - c) * hr, 16), hr)

        def remote(src, dst, k, j, dev):
            return pltpu.make_async_remote_copy(src_ref=src, dst_ref=dst, send_sem=send.at[k, j],
                                                recv_sem=recv.at[k, j], device_id=dev, device_id_type=_MESH)

        started = []
        local = []
        for k in range(n):
            mine, _ = halves(k)
            cp = pltpu.make_async_copy(ins[k], outs[k].at[me], loc.at[k])
            cp.start()
            local.append(cp)
            for j, (cx, cy) in enumerate(chips):
                cp = remote(ins[k].at[mine], outs[k].at[me, mine], k, j, (cx, cy, c))
                cp.start()
                started.append(cp)
        for k in range(n):
            mine, _ = halves(k)
            for j in range(3):
                land = outs[k].at[chip_ids[j], mine]
                remote(land, land, k, j, sib).wait_recv()
                cp = remote(land, land, k, 3 + j, sib)
                cp.start()
                started.append(cp)
        for k in range(n):
            _, other = halves(k)
            for j in range(3):
                land = outs[k].at[chip_ids[j], other]
                remote(land, land, k, 3 + j, sib).wait_recv()
        for cp in started:
            cp.wait_send()
        for cp in local:
            cp.wait()

    return pl.pallas_call(
        body, name="all_gather_weights", in_specs=[_ANY] * n, out_specs=[_ANY] * n,
        out_shape=[jax.ShapeDtypeStruct((N_CHIPS,) + s.shape, s.dtype) for s in shards],
        scratch_shapes=[pltpu.SemaphoreType.DMA((n, 6)), pltpu.SemaphoreType.DMA((n, 6)),
                        pltpu.SemaphoreType.DMA((n,))],
        compiler_params=pltpu.CompilerParams(vmem_limit_bytes=V7X_VMEM_LIMIT_BYTES),
    )(*shards)


_RS_CHUNK = 32


def _reduce_scatter(name, grads):
    n = len(grads)

    def body(*refs):
        gs, outs = refs[:n], refs[n:2 * n]
        own, r1, r2, fin = (refs[(2 + i) * n:(3 + i) * n] for i in range(4))
        a_send, a_recv, b_send, b_recv, c_send, c_recv, l_in, l_out = refs[6 * n:]
        x, y, c, me, chips, chip_ids = _mesh_pos()
        sib = (x, y, 1 - c)

        def halves(k):
            hr = gs[k].shape[1] // 2
            return hr, pl.ds(pl.multiple_of(c * hr, 16), hr), pl.ds(pl.multiple_of((1 - c) * hr, 16), hr)

        def remote(src, dst, ssem, rsem, dev):
            return pltpu.make_async_remote_copy(src_ref=src, dst_ref=dst, send_sem=ssem, recv_sem=rsem,
                                                device_id=dev, device_id_type=_MESH)

        sends, locals_in = [], []
        for k in range(n):
            hr, mine, other = halves(k)
            cp = remote(gs[k].at[:, other, :], r1[k], a_send.at[k], a_recv.at[k], sib)
            cp.start()
            sends.append(cp)
            cp = pltpu.make_async_copy(gs[k].at[:, mine, :], own[k], l_in.at[k])
            cp.start()
            locals_in.append(cp)

        for k in range(n):
            hr, mine, other = halves(k)
            locals_in[k].wait()
            remote(r1[k], r1[k], a_send.at[k], a_recv.at[k], sib).wait_recv()
            for j in range(N_CHIPS):
                def add(i, carry, k=k, j=j):
                    rows = pl.ds(pl.multiple_of(i * _RS_CHUNK, _RS_CHUNK), _RS_CHUNK)
                    own[k][j, rows, :] = (own[k][j, rows, :].astype(F32) + r1[k][j, rows, :].astype(F32)).astype(BF16)
                    return carry

                lax.fori_loop(0, hr // _RS_CHUNK, add, 0)
            for j, (cx, cy) in enumerate(chips):
                cp = remote(own[k].at[chip_ids[j]], r2[k].at[j], b_send.at[k, j], b_recv.at[k, j], (cx, cy, c))
                cp.start()
                sends.append(cp)

        locals_out = []
        for k in range(n):
            hr, mine, other = halves(k)
            for j in range(3):
                remote(r2[k].at[j], r2[k].at[j], b_send.at[k, j], b_recv.at[k, j], sib).wait_recv()

            def total(i, carry, k=k):
                rows = pl.ds(pl.multiple_of(i * _RS_CHUNK, _RS_CHUNK), _RS_CHUNK)
                acc = own[k][me, rows, :].astype(F32)
                for j in range(3):
                    acc = acc + r2[k][j, rows, :].astype(F32)
                fin[k][rows, :] = acc
                return carry

            lax.fori_loop(0, hr // _RS_CHUNK, total, 0)
            cp = remote(fin[k], outs[k].at[mine, :], c_send.at[k], c_recv.at[k], sib)
            cp.start()
            sends.append(cp)
            cp = pltpu.make_async_copy(fin[k], outs[k].at[mine, :], l_out.at[k])
            cp.start()
            locals_out.append(cp)

        for k in range(n):
            hr, mine, other = halves(k)
            land = outs[k].at[other, :]
            remote(land, land, c_send.at[k], c_recv.at[k], sib).wait_recv()
        for cp in sends:
            cp.wait_send()
        for cp in locals_out:
            cp.wait()

    scratch = []
    for g in grads:
        scratch.append(pltpu.VMEM((N_CHIPS, g.shape[1] // 2, g.shape[2]), BF16))
    for g in grads:
        scratch.append(pltpu.VMEM((N_CHIPS, g.shape[1] // 2, g.shape[2]), BF16))
    for g in grads:
        scratch.append(pltpu.VMEM((3, g.shape[1] // 2, g.shape[2]), BF16))
    for g in grads:
        scratch.append(pltpu.VMEM((g.shape[1] // 2, g.shape[2]), F32))
    dma = pltpu.SemaphoreType.DMA
    scratch += [dma((n,)), dma((n,)), dma((n, 3)), dma((n, 3)), dma((n,)), dma((n,)), dma((n,)), dma((n,))]
    return pl.pallas_call(
        body, name=name, in_specs=[_ANY] * n, out_specs=[_ANY] * n,
        out_shape=[jax.ShapeDtypeStruct(g.shape[1:], F32) for g in grads],
        scratch_shapes=scratch,
        compiler_params=pltpu.CompilerParams(vmem_limit_bytes=V7X_VMEM_LIMIT_BYTES),
    )(*grads)


def _adamw_math(w, g, m, v):
    m = ADAM_B1 * m + (1.0 - ADAM_B1) * g
    v = ADAM_B2 * v + (1.0 - ADAM_B2) * (g * g)
    m_hat = m / (1.0 - ADAM_B1 ** ADAM_STEP)
    v_hat = v / (1.0 - ADAM_B2 ** ADAM_STEP)
    delta = -ADAM_LR * (m_hat / (jnp.sqrt(v_hat) + ADAM_EPS) + ADAM_WD * w)
    return delta, m, v


def _adamw_sum(name, w, parts, m, v):
    r, c = w.shape
    br = r
    while br * c * 4 > (1 << 20) and br % 32 == 0:
        br //= 2

    def body(w_ref, p_ref, m_ref, v_ref, g_ref, d_ref, nm_ref, nv_ref):
        g = p_ref[0].astype(F32)
        for j in range(1, N_CHIPS):
            g = g + p_ref[j].astype(F32)
        d, nm, nv = _adamw_math(w_ref[...], g, m_ref[...], v_ref[...])
        g_ref[...] = g
        d_ref[...] = d
        nm_ref[...] = nm
        nv_ref[...] = nv

    spec = pl.BlockSpec((br, c), lambda i: (i, 0))
    shp = jax.ShapeDtypeStruct((r, c), F32)
    return _call(
        body, name=name, grid=(r // br,),
        in_specs=[spec, pl.BlockSpec((N_CHIPS, br, c), lambda i: (0, i, 0)), spec, spec],
        out_specs=[spec] * 4, out_shape=[shp] * 4, sem=("parallel",), args=[w, parts, m, v])


_SMALL_ROWS = 80


def _small_allreduce_adamw(gpack, wpack, mpack, vpack):
    def body(g_ref, w_ref, m_ref, v_ref, go_ref, d_ref, nm_ref, nv_ref, buf, send, recv):
        x, y, c = lax.axis_index("x"), lax.axis_index("y"), lax.axis_index("c")
        me = 4 * x + 2 * y + c
        buf[me] = g_ref[...]
        copies = []
        for rel in range(1, 8):
            fx, fy, fc = (rel >> 2) & 1, (rel >> 1) & 1, rel & 1
            dev = ((1 - x) if fx else x, (1 - y) if fy else y, (1 - c) if fc else c)
            cp = pltpu.make_async_remote_copy(src_ref=g_ref, dst_ref=buf.at[me], send_sem=send.at[rel - 1],
                                              recv_sem=recv.at[rel - 1], device_id=dev, device_id_type=_MESH)
            cp.start()
            copies.append(cp)
        for cp in copies:
            cp.wait_recv()
        for cp in copies:
            cp.wait_send()
        total = buf[0]
        for i in range(1, 8):
            total = total + buf[i]
        go_ref[...] = total
        d, nm, nv = _adamw_math(w_ref[...], total, m_ref[...], v_ref[...])
        d_ref[...] = d
        nm_ref[...] = nm
        nv_ref[...] = nv

    vm = pl.BlockSpec(memory_space=pltpu.VMEM)
    shp = jax.ShapeDtypeStruct((_SMALL_ROWS, D_MODEL), F32)
    return pl.pallas_call(
        body, name="small_allreduce_adamw", in_specs=[vm] * 4, out_specs=[vm] * 4, out_shape=[shp] * 4,
        scratch_shapes=[pltpu.VMEM((8, _SMALL_ROWS, D_MODEL), F32), pltpu.SemaphoreType.DMA((7,)),
                        pltpu.SemaphoreType.DMA((7,))],
        compiler_params=pltpu.CompilerParams(vmem_limit_bytes=V7X_VMEM_LIMIT_BYTES),
    )(gpack, wpack, mpack, vpack)


_SMALL_VECTORS = ("ffn1_norm", "mix_norm", "xattn_norm", "mem_norm", "ffn2_norm", "final_norm", "q_norm",
                  "kv_norm", "pool_scale")


def _pack_small(d):
    rows = []
    for n in _SMALL_VECTORS:
        v = d[n].reshape(1, -1).astype(F32)
        rows.append(jnp.pad(v, ((0, 0), (0, D_MODEL - v.shape[1]))))
    rows.append(jnp.zeros((16 - len(_SMALL_VECTORS), D_MODEL), F32))
    rows.append(d["pool_w"].reshape(64, D_MODEL).astype(F32))
    return jnp.concatenate(rows, axis=0)


def _unpack_small(pack, like):
    out = {}
    for i, n in enumerate(_SMALL_VECTORS):
        out[n] = pack[i, :like[n].size].reshape(like[n].shape)
    out["pool_w"] = pack[16:].reshape(like["pool_w"].shape)
    return out


_WEIGHTS = ("ffn1_norm", "ffn1_w_gate", "ffn1_w_up", "ffn1_w_down", "mix_norm", "w_in", "q_norm", "w_q_up",
            "kv_norm", "w_kv_up", "pool_w", "pool_scale", "w_out", "xattn_norm", "mem_norm", "w_mq", "w_mkv",
            "w_mo", "ffn2_norm", "ffn2_w_gate", "ffn2_w_up", "ffn2_w_down", "final_norm")
_SHARDED = ("ffn1_w_gate", "ffn1_w_up", "ffn1_w_down", "w_in", "w_q_up", "w_kv_up", "w_out", "w_mq", "w_mkv",
            "w_mo", "ffn2_w_gate", "ffn2_w_up", "ffn2_w_down")
_RS_GROUPS = (("ffn2_w_gate", "ffn2_w_up", "ffn2_w_down"),
              ("w_mo", "w_mq", "w_mkv", "w_out", "w_q_up", "w_kv_up", "w_in"),
              ("ffn1_w_gate", "ffn1_w_up", "ffn1_w_down"))
W_IN_SPLIT = Q_LORA + KV_LORA + ROPE_DIM


_TRANSPOSED = ("ffn1_w_gate", "ffn1_w_up", "ffn2_w_gate", "ffn2_w_up", "w_in", "w_q_up")


def _local_view(name, a):
    return jnp.swapaxes(a, 1, 2)[0] if name in _TRANSPOSED else a[0]


def _global_view(name, a):
    return jnp.swapaxes(a[None], 1, 2) if name in _TRANSPOSED else a[None]


def _pad_shard(name, a):
    if name == "w_in":
        return jnp.concatenate([a[:W_IN_SPLIT], jnp.zeros((64, a.shape[1]), a.dtype), a[W_IN_SPLIT:]], axis=0)
    if name == "w_q_up":
        return jnp.pad(a, ((0, 64), (0, 0)))
    return a


def _unpad_shard(name, a):
    if name == "w_in":
        return jnp.concatenate([a[:, :W_IN_SPLIT], a[:, W_IN_SPLIT + 64:]], axis=1)
    if name == "w_q_up":
        return a[:, :192]
    return a


def _stacked(g):
    return g if g.ndim == 3 else g.reshape(N_CHIPS, g.shape[0] // N_CHIPS, g.shape[1])


class _Plan:
    AG_UNITS = (
        (("w_in", "w_q_up", "w_kv_up", "ffn2_w_gate"), "ffn1_up", "ffn1_down"),
        (("ffn2_w_up",), "ffn1_down", "w_in"),
        (("w_out", "w_mq", "w_mkv", "w_mo", "ffn2_w_down"), "mla_fwd", "pool_fwd"),
    )
    RS_UNITS = (
        (("ffn2_w_gate", "ffn2_w_up", "ffn2_w_down"), "ffn2_dn_a", "mla_bwd", "qkv_prep_bwd"),
        (("w_mo", "w_mq", "w_mkv"), "w_out_dx", "mla_bwd", "qkv_prep_bwd"),
        (("w_out", "w_q_up", "w_kv_up", "w_in"), "w_in_dx", "ffn1_dact", "ffn1_dwd"),
        (("ffn1_w_down",), "ffn1_dwg", "ffn1_dwu", "ffn1_dn_a"),
        (("ffn1_w_gate",), "ffn1_dwu", "ffn1_dn_a", "ffn1_dn_b"),
        (("ffn1_w_up",), "ffn1_dn_a", "ffn1_dn_b", "adamw_w_kv_up"),
    )
    ADAMW_ORDER = ("w_kv_up", "ffn2_w_gate", "ffn2_w_up", "ffn2_w_down", "w_mo", "w_mq", "w_mkv", "w_out", "w_q_up",
                   "w_in", "ffn1_w_down", "ffn1_w_gate", "ffn1_w_up")

    def __init__(self, shards, w, grads, core):
        self.shards, self.w, self.grads, self.core = shards, w, grads, core
        self.parts = {}
        self.ag = [[None, None] for _ in self.AG_UNITS]
        self.rs = [[None, None, None, None] for _ in self.RS_UNITS]

    def pre(self, name):
        for i, (names, h1, h2) in enumerate(self.AG_UNITS):
            if name == h1:
                self.ag[i][0] = _host(name, _ag_ici_stage([self.shards[n] for n in names]))
            if name == h2:
                self.ag[i][1] = _host(name, _ag_d2d_stage(self.ag[i][0].results))
        for i, (names, h1, h2, h3) in enumerate(self.RS_UNITS):
            if name == h1:
                self.rs[i][0] = _host(name, _rs_swap_stage([_stacked(self.grads[n]) for n in names]))
            if name == h2:
                self.rs[i][2] = _host(name, _rs_scatter_stage(self.rs[i][1]))
            if name == h3:
                self.rs[i][3] = _host(name, _rs_mirror_stage(self.rs[i][2].results))

    def post(self, name):
        for i, (names, h1, h2) in enumerate(self.AG_UNITS):
            if name == h2:
                for n, f in zip(names, self.ag[i][1].results):
                    self.w[n] = _full_weight(n, f)
        for i, (names, h1, h2, h3) in enumerate(self.RS_UNITS):
            if name == h1:
                self.rs[i][1] = [_pair_add("pair_add_" + n, _stacked(self.grads[n]), r1, self.core)
                                 for n, r1 in zip(names, self.rs[i][0].results)]
            if name == h3:
                for n, p in zip(names, self.rs[i][3].results):
                    self.parts[n] = p


def _full_weight(name, stacked):
    if name in ("w_out", "w_mq", "w_mo"):
        return stacked.reshape(D_MODEL, D_MODEL)
    return stacked


def kernel(x, mem, positions, ffn1_norm, ffn1_w_gate, ffn1_w_up, ffn1_w_down, mix_norm, w_in, q_norm, w_q_up, kv_norm, w_kv_up, pool_w, pool_scale, w_out, xattn_norm, mem_norm, w_mq, w_mkv, w_mo, ffn2_norm, ffn2_w_gate, ffn2_w_up, ffn2_w_down, final_norm, loss_target, m_ffn1_norm, m_ffn1_w_gate, m_ffn1_w_up, m_ffn1_w_down, m_mix_norm, m_w_in, m_q_norm, m_w_q_up, m_kv_norm, m_w_kv_up, m_pool_w, m_pool_scale, m_w_out, m_xattn_norm, m_mem_norm, m_w_mq, m_w_mkv, m_w_mo, m_ffn2_norm, m_ffn2_w_gate, m_ffn2_w_up, m_ffn2_w_down, m_final_norm, v_ffn1_norm, v_ffn1_w_gate, v_ffn1_w_up, v_ffn1_w_down, v_mix_norm, v_w_in, v_q_norm, v_w_q_up, v_kv_norm, v_w_kv_up, v_pool_w, v_pool_scale, v_w_out, v_xattn_norm, v_mem_norm, v_w_mq, v_w_mkv, v_w_mo, v_ffn2_norm, v_ffn2_w_gate, v_ffn2_w_up, v_ffn2_w_down, v_final_norm):
    wts = dict(zip(_WEIGHTS, (ffn1_norm, ffn1_w_gate, ffn1_w_up, ffn1_w_down, mix_norm, w_in, q_norm, w_q_up, kv_norm, w_kv_up, pool_w, pool_scale, w_out, xattn_norm, mem_norm, w_mq, w_mkv, w_mo, ffn2_norm, ffn2_w_gate, ffn2_w_up, ffn2_w_down, final_norm)))
    mom = dict(zip(_WEIGHTS, (m_ffn1_norm, m_ffn1_w_gate, m_ffn1_w_up, m_ffn1_w_down, m_mix_norm, m_w_in, m_q_norm, m_w_q_up, m_kv_norm, m_w_kv_up, m_pool_w, m_pool_scale, m_w_out, m_xattn_norm, m_mem_norm, m_w_mq, m_w_mkv, m_w_mo, m_ffn2_norm, m_ffn2_w_gate, m_ffn2_w_up, m_ffn2_w_down, m_final_norm)))
    var = dict(zip(_WEIGHTS, (v_ffn1_norm, v_ffn1_w_gate, v_ffn1_w_up, v_ffn1_w_down, v_mix_norm, v_w_in, v_q_norm, v_w_q_up, v_kv_norm, v_w_kv_up, v_pool_w, v_pool_scale, v_w_out, v_xattn_norm, v_mem_norm, v_w_mq, v_w_mkv, v_w_mo, v_ffn2_norm, v_ffn2_w_gate, v_ffn2_w_up, v_ffn2_w_down, v_final_norm)))
    small = [n for n in _WEIGHTS if n not in _SHARDED]

    global _PLAN
    shards = {n: _pad_shard(n, _local_view(n, wts[n])).astype(BF16) for n in _SHARDED}
    w = {n: wts[n].reshape(1, -1) for n in _SMALL_VECTORS}
    w["pool_w"] = pool_w[0].astype(BF16)
    grads = {}
    core = lax.axis_index("c").astype(jnp.int32).reshape(1)
    plan = _Plan(shards, w, grads, core)
    _PLAN = plan
    try:
        first = ("ffn1_w_gate", "ffn1_w_up", "ffn1_w_down")
        for n, f in zip(first, _all_gather_weights([shards[n] for n in first])):
            w[n] = f

        loss_local, dx = _local_step(x[0], mem[0], positions[0], loss_target[0], w, grads)
        loss = lax.psum(loss_local, ("x", "y", "c"))

        gpack, dpack, mpack, vpack = _small_allreduce_adamw(
            _pack_small({n: grads[n] for n in small}), _pack_small({n: wts[n] for n in small}),
            _pack_small({n: mom[n] for n in small}), _pack_small({n: var[n] for n in small}))
        like = {n: wts[n] for n in small}
        g_out, d_out, m_out, v_out = (_unpack_small(p, like) for p in (gpack, dpack, mpack, vpack))

        for n in _Plan.ADAMW_ORDER:
            res = _adamw_sum("adamw_" + n, _local_view(n, wts[n]), _unpad_shard(n, plan.parts[n]),
                             _local_view(n, mom[n]), _local_view(n, var[n]))
            g_out[n], d_out[n], m_out[n], v_out[n] = (_global_view(n, r) for r in res)
    finally:
        _PLAN = None
        _PENDING.clear()

    return (loss, dx[None], *[g_out[n] for n in _WEIGHTS], *[d_out[n] for n in _WEIGHTS],
            *[m_out[n] for n in _WEIGHTS], *[v_out[n] for n in _WEIGHTS])
```

```python
import functools

import jax
import jax.numpy as jnp
from jax import lax
from jax.experimental import pallas as pl
from jax.experimental.pallas import tpu as pltpu

F32 = jnp.float32
BF16 = jnp.bfloat16

D_MODEL = 1024
D_FF = 2816
N_CHIPS = 4
FF_SHARD = D_FF // N_CHIPS
MLA_HEADS = 4
Q_LORA = 256
KV_LORA = 128
ROPE_DIM = 64
HEAD_QK = 256
HEAD_V = 128
POOL_GROUPS = 4
POOL_CH = 128
MEM_HEADS = 4
MEM_HEAD_DIM = 256
RMS_EPS = 1e-6
ROPE_BASE = 10000.0
MLA_SCALE = (128 + 64) ** -0.5
MEM_SCALE = MEM_HEAD_DIM ** -0.5

ADAM_LR = 0.001
ADAM_B1 = 0.9
ADAM_B2 = 0.999
ADAM_EPS = 1e-08
ADAM_WD = 0.01
ADAM_STEP = 10

V7X_VMEM_LIMIT_BYTES = 56 * 1024 * 1024

NN = ((1,), (0,))
NT = ((1,), (1,))
TN = ((0,), (0,))


def _params(*sem):
    return pltpu.CompilerParams(dimension_semantics=sem, vmem_limit_bytes=V7X_VMEM_LIMIT_BYTES)


_MESH = pl.DeviceIdType.MESH
_ANY = pl.BlockSpec(memory_space=pl.ANY)


class _Stage:
    def __init__(self, ins, outs, n_remote, n_local, copies, aliases=None):
        self.ins, self.outs, self.n_remote, self.n_local = list(ins), list(outs), n_remote, n_local
        self.copies, self.aliases = copies, dict(aliases or {})
        self.results = None

    def descriptors(self, in_refs, out_refs, send, recv, loc):
        ds, ri, li = [], 0, 0
        for src, dst, dev in self.copies(in_refs, out_refs):
            if dev is None:
                ds.append(pltpu.make_async_copy(src, dst, loc.at[li]))
                li += 1
            else:
                ds.append(pltpu.make_async_remote_copy(src_ref=src, dst_ref=dst, send_sem=send.at[ri],
                                                       recv_sem=recv.at[ri], device_id=dev, device_id_type=_MESH))
                ri += 1
        assert ri == self.n_remote and li == self.n_local
        return ds


_PENDING = {}


def _host(name, stage):
    _PENDING.setdefault(name, []).append(stage)
    return stage


_PLAN = None


def _call(body, **kw):
    if _PLAN is not None:
        _PLAN.pre(kw["name"])
    res = _call_hosting(body, **kw)
    if _PLAN is not None:
        _PLAN.post(kw["name"])
    return res


def _call_hosting(body, *, name, grid, in_specs, out_specs, out_shape, sem, args, scratch_shapes=(), aliases=None):
    stages = _PENDING.pop(name, [])
    scratch_shapes = list(scratch_shapes)
    if not stages:
        return pl.pallas_call(body, name=name, grid=grid, in_specs=in_specs, out_specs=out_specs,
                              out_shape=out_shape, scratch_shapes=scratch_shapes,
                              input_output_aliases=dict(aliases or {}), compiler_params=_params(*sem))(*args)
    ni, no, ns = len(in_specs), len(out_shape), len(scratch_shapes)
    c_ins = [a for st in stages for a in st.ins]
    c_outs = [o for st in stages for o in st.outs]
    nci, nco = len(c_ins), len(c_outs)
    aliases, io, oo = dict(aliases or {}), 0, 0
    for st in stages:
        for i, j in st.aliases.items():
            aliases[ni + io + i] = no + oo + j
        io += len(st.ins)
        oo += len(st.outs)
    dma = pltpu.SemaphoreType.DMA
    sems = []
    for st in stages:
        sems += [dma((max(st.n_remote, 1),)), dma((max(st.n_remote, 1),)), dma((max(st.n_local, 1),))]

    def wrapped(*refs):
        ins, cin = refs[:ni], refs[ni:ni + nci]
        outs, cout = refs[ni + nci:ni + nci + no], refs[ni + nci + no:ni + nci + no + nco]
        scr = refs[ni + nci + no + nco:ni + nci + no + nco + ns]
        sem_refs = refs[ni + nci + no + nco + ns:]
        first = pl.program_id(0) == 0
        last = pl.program_id(0) == grid[0] - 1
        for ax in range(1, len(grid)):
            first = jnp.logical_and(first, pl.program_id(ax) == 0)
            last = jnp.logical_and(last, pl.program_id(ax) == grid[ax] - 1)

        def descriptors():
            ds, io, oo = [], 0, 0
            for si, st in enumerate(stages):
                ds += st.descriptors(cin[io:io + len(st.ins)], cout[oo:oo + len(st.outs)], *sem_refs[3 * si:3 * si + 3])
                io += len(st.ins)
                oo += len(st.outs)
            return ds

        @pl.when(first)
        def _():
            for d in descriptors():
                d.start()

        body(*ins, *outs, *scr)

        @pl.when(last)
        def _():
            for d in descriptors():
                d.wait()

    res = pl.pallas_call(
        wrapped, name=name, grid=grid, in_specs=list(in_specs) + [_ANY] * nci,
        out_specs=list(out_specs) + [_ANY] * nco, out_shape=list(out_shape) + c_outs,
        scratch_shapes=scratch_shapes + sems, input_output_aliases=aliases,
        compiler_params=_params(*(("arbitrary",) * len(grid))))(*args, *c_ins)
    oo = no
    for st in stages:
        st.results = list(res[oo:oo + len(st.outs)])
        oo += len(st.outs)
    return list(res[:no])


def _dot(a, b, dims):
    return lax.dot_general(a.astype(BF16), b.astype(BF16), (dims, ((), ())), preferred_element_type=F32)


_MAX_ROW_BLOCK = 1024
_ATT_BLOCK = 512


_MAX_REDUCE_BLOCK = 2048


def _row_block(s, want=1024):
    return min(want, s, _MAX_ROW_BLOCK)


def _reduce_block(s):
    return min(s, _MAX_REDUCE_BLOCK)


def _matmul(name, grid, terms, extras, outs, epilogue, acc_shape, fill=(), summed=()):
    nt, ne, no, nf = len(terms), len(extras), len(outs), len(fill)
    nk = grid[-1]
    dims = [t[4] for t in terms]

    def body(*refs):
        a_refs, b_refs = refs[:nt], refs[nt:2 * nt]
        e_refs = refs[2 * nt:2 * nt + ne]
        o_refs = refs[2 * nt + ne + nf:2 * nt + ne + nf + no]

        def finish(acc):
            vals = epilogue(acc, *[e[...] for e in e_refs])
            for idx, (o, val) in enumerate(zip(o_refs, vals)):
                if idx in summed:
                    @pl.when(pl.program_id(0) == 0)
                    def _(o=o, val=val):
                        o[...] = val.astype(o.dtype)

                    @pl.when(pl.program_id(0) > 0)
                    def _(o=o, val=val):
                        o[...] += val.astype(o.dtype)
                else:
                    o[...] = val.astype(o.dtype)

        if nk == 1:
            part = None
            for a, b, d in zip(a_refs, b_refs, dims):
                t = _dot(a[...], b[...], d)
                part = t if part is None else part + t
            finish(part)
        else:
            acc_ref = refs[-1]
            k = pl.program_id(len(grid) - 1)

            @pl.when(k == 0)
            def _():
                acc_ref[...] = jnp.zeros_like(acc_ref)

            for a, b, d in zip(a_refs, b_refs, dims):
                acc_ref[...] += _dot(a[...], b[...], d)

            @pl.when(k == nk - 1)
            def _():
                finish(acc_ref[...])

    in_specs = [t[1] for t in terms] + [t[3] for t in terms] + [e[1] for e in extras] + [_ANY] * nf
    args = [t[0] for t in terms] + [t[2] for t in terms] + [e[0] for e in extras] + list(fill)
    sem = ("arbitrary" if summed else "parallel",) * (len(grid) - 1) + ("arbitrary",)
    aliases = {2 * nt + ne + i: i for i in range(nf)}
    return _call(
        body, name=name, grid=grid, in_specs=in_specs,
        out_specs=[o[1] for o in outs], out_shape=[o[0] for o in outs],
        scratch_shapes=[pltpu.VMEM(acc_shape, F32)] if nk > 1 else [], sem=sem, args=args, aliases=aliases)


def _ident(acc):
    return (acc,)


def _rmsnorm_fwd(name, x, gain, width, col_block=0):
    s = x.shape[0]
    bm = _row_block(s)

    def body(x_ref, g_ref, o_ref):
        xf = x_ref[...]
        r = lax.rsqrt(jnp.mean(xf * xf, axis=-1, keepdims=True) + RMS_EPS)
        o_ref[...] = ((xf * r) * g_ref[...]).astype(o_ref.dtype)

    return pl.pallas_call(
        body, name=name, grid=(s // bm,),
        in_specs=[pl.BlockSpec((bm, width), lambda i: (i, col_block)), pl.BlockSpec((1, width), lambda i: (0, 0))],
        out_specs=pl.BlockSpec((bm, width), lambda i: (i, 0)),
        out_shape=jax.ShapeDtypeStruct((s, width), BF16),
        compiler_params=_params("parallel"),
    )(x, gain)


def _rms_bwd_math(dy, xf, g, width):
    r = lax.rsqrt(jnp.mean(xf * xf, axis=-1, keepdims=True) + RMS_EPS)
    dyg = dy * g
    dot = jnp.sum(dyg * xf, axis=-1, keepdims=True)
    dx = r * dyg - xf * ((r * r * r) * (dot * (1.0 / width)))
    dgain = jnp.sum(dy * (xf * r), axis=0, keepdims=True)
    return dx, dgain


def _rmsnorm_bwd(name, dy, x, gain, width, col_block=0, dres=None, out_dtype=F32):
    s = x.shape[0]
    bm = _row_block(s)
    has_res = dres is not None

    def body(*refs):
        if has_res:
            dy_ref, x_ref, g_ref, r_ref, dx_ref, dg_ref, dxb_ref = refs
        else:
            dy_ref, x_ref, g_ref, dx_ref, dg_ref = refs
        dx, dgain = _rms_bwd_math(dy_ref[...].astype(F32), x_ref[...], g_ref[...], width)
        if has_res:
            dx = dx + r_ref[...]
            dxb_ref[...] = dx.astype(BF16)
        dx_ref[...] = dx.astype(dx_ref.dtype)

        @pl.when(pl.program_id(0) == 0)
        def _():
            dg_ref[...] = dgain

        @pl.when(pl.program_id(0) > 0)
        def _():
            dg_ref[...] += dgain

    row = pl.BlockSpec((bm, width), lambda i: (i, 0))
    in_specs = [row, pl.BlockSpec((bm, width), lambda i: (i, col_block)), pl.BlockSpec((1, width), lambda i: (0, 0))]
    args = [dy, x, gain]
    out_specs = [row, pl.BlockSpec((1, width), lambda i: (0, 0))]
    out_shape = [jax.ShapeDtypeStruct((s, width), out_dtype), jax.ShapeDtypeStruct((1, width), F32)]
    if has_res:
        in_specs.append(row)
        args.append(dres)
        out_specs.append(row)
        out_shape.append(jax.ShapeDtypeStruct((s, width), BF16))
    return _call(body, name=name, grid=(s // bm,), in_specs=in_specs, out_specs=out_specs, out_shape=out_shape,
                 sem=("arbitrary",), args=args)


def _loss_and_final_norm(h, gain, target):
    s, d = h.shape
    bm = _row_block(s, 512)

    def body(h_ref, g_ref, t_ref, dh_ref, dhb_ref, loss_ref, dg_ref):
        xf = h_ref[...]
        g = g_ref[...]
        r = lax.rsqrt(jnp.mean(xf * xf, axis=-1, keepdims=True) + RMS_EPS)
        err = (xf * r) * g - t_ref[...]
        part = 0.5 * jnp.sum(jnp.mean(err * err, axis=-1, keepdims=True), axis=0, keepdims=True)
        dx, dgain = _rms_bwd_math(err * (1.0 / d), xf, g, d)
        dh_ref[...] = dx
        dhb_ref[...] = dx.astype(BF16)

        @pl.when(pl.program_id(0) == 0)
        def _():
            dg_ref[...] = dgain
            loss_ref[...] = jnp.broadcast_to(part, loss_ref.shape)

        @pl.when(pl.program_id(0) > 0)
        def _():
            dg_ref[...] += dgain
            loss_ref[...] += jnp.broadcast_to(part, loss_ref.shape)

    row = pl.BlockSpec((bm, d), lambda i: (i, 0))
    vec = pl.BlockSpec((1, d), lambda i: (0, 0))
    return pl.pallas_call(
        body, name="loss_final_norm", grid=(s // bm,), in_specs=[row, vec, row],
        out_specs=[row, row, pl.BlockSpec((1, 128), lambda i: (0, 0)), vec],
        out_shape=[jax.ShapeDtypeStruct((s, d), F32), jax.ShapeDtypeStruct((s, d), BF16),
                   jax.ShapeDtypeStruct((1, 128), F32),
                   jax.ShapeDtypeStruct((1, d), F32)],
        compiler_params=_params("arbitrary"),
    )(h, gain, target)


def _ffn_up(name, n, wg, wu):
    s = n.shape[0]
    bm = _row_block(s)

    def body(n_ref, wg_ref, wu_ref, a_ref, dadu_ref, dadg_ref):
        x = n_ref[...]
        g = _dot(x, wg_ref[...], NT)
        u = _dot(x, wu_ref[...], NT)
        sg = jax.nn.sigmoid(g)
        silu = g * sg
        a_ref[...] = (silu * u).astype(BF16)
        dadu_ref[...] = silu.astype(BF16)
        dadg_ref[...] = (u * (sg * (1.0 + g * (1.0 - sg)))).astype(BF16)

    w_spec = pl.BlockSpec((None, FF_SHARD, D_MODEL), lambda j, i: (j, 0, 0))
    o_spec = pl.BlockSpec((None, bm, FF_SHARD), lambda j, i: (j, i, 0))
    shp = jax.ShapeDtypeStruct((N_CHIPS, s, FF_SHARD), BF16)
    return _call(
        body, name=name, grid=(N_CHIPS, s // bm),
        in_specs=[pl.BlockSpec((bm, D_MODEL), lambda j, i: (i, 0)), w_spec, w_spec],
        out_specs=[o_spec, o_spec, o_spec], out_shape=[shp, shp, shp],
        sem=("parallel", "parallel"), args=[n, wg, wu])


def _residual_epilogue(alpha, with_norm):
    if not with_norm:
        return lambda acc, r: (r + alpha * acc,)

    def epilogue(acc, r, g):
        h = r + alpha * acc
        rs = lax.rsqrt(jnp.mean(h * h, axis=-1, keepdims=True) + RMS_EPS)
        return h, (h * rs) * g

    return epilogue


def _residual_outs(s, bm, gain):
    row = pl.BlockSpec((bm, D_MODEL), lambda i, k: (i, 0))
    outs = [(jax.ShapeDtypeStruct((s, D_MODEL), F32), row)]
    if gain is None:
        return [], outs
    return [(gain, pl.BlockSpec((1, D_MODEL), lambda i, k: (0, 0)))], outs + [(jax.ShapeDtypeStruct((s, D_MODEL), BF16), row)]


def _ffn_down(name, a, wd, res, gain=None):
    s = a.shape[1]
    bm = _row_block(s, 512)
    row = pl.BlockSpec((bm, D_MODEL), lambda i, k: (i, 0))
    terms = [(a, pl.BlockSpec((None, bm, FF_SHARD), lambda i, k, j=j: (j, i, 0)),
              wd, pl.BlockSpec((None, FF_SHARD, D_MODEL), lambda i, k, j=j: (j, 0, 0)), NN) for j in range(N_CHIPS)]
    extras, outs = _residual_outs(s, bm, gain)
    res_out = _matmul(name, (s // bm, 1), terms, [(res, row)] + extras, outs,
                      _residual_epilogue(0.5, gain is not None), None)
    return res_out if gain is not None else res_out[0]


def _norm_bwd_epilogue(width):
    def epilogue(acc, h, g, dres):
        dx, dgain = _rms_bwd_math(acc, h, g, width)
        dx = dx + dres
        return dx, dx, dgain

    return epilogue


def _norm_bwd_operands(s, bm, h, gain, dres):
    row = pl.BlockSpec((bm, D_MODEL), lambda i, k: (i, 0))
    vec = pl.BlockSpec((1, D_MODEL), lambda i, k: (0, 0))
    extras = [(h, row), (gain, vec), (dres, row)]
    outs = [(jax.ShapeDtypeStruct((s, D_MODEL), F32), row), (jax.ShapeDtypeStruct((s, D_MODEL), BF16), row),
            (jax.ShapeDtypeStruct((1, D_MODEL), F32), vec)]
    return extras, outs, (2,)


def _ffn_bwd(tag, dh, n, dadg, dadu, a, wg, wu, wd, grads, norm_bwd=None):
    s = dh.shape[0]
    bm = _row_block(s)
    bk = _reduce_block(s)
    nk = s // bk

    def act_bwd(acc, dg_da, du_da):
        da = 0.5 * acc
        return da * dg_da.astype(F32), da * du_da.astype(F32)

    slab = pl.BlockSpec((None, bm, FF_SHARD), lambda j, i, k: (j, i, 0))
    shp = jax.ShapeDtypeStruct((N_CHIPS, s, FF_SHARD), BF16)
    dg, du = _matmul(
        tag + "_dact", (N_CHIPS, s // bm, 1),
        [(dh, pl.BlockSpec((bm, D_MODEL), lambda j, i, k: (i, 0)),
          wd, pl.BlockSpec((None, FF_SHARD, D_MODEL), lambda j, i, k: (j, 0, 0)), NT)],
        [(dadg, slab), (dadu, slab)], [(shp, slab), (shp, slab)], act_bwd, None)

    grads[tag + "_w_down"] = _matmul(
        tag + "_dwd", (N_CHIPS, nk),
        [(a, pl.BlockSpec((None, bk, FF_SHARD), lambda j, k: (j, k, 0)),
          dh, pl.BlockSpec((bk, D_MODEL), lambda j, k: (k, 0)), TN)],
        [], [(jax.ShapeDtypeStruct((N_CHIPS, FF_SHARD, D_MODEL), BF16),
              pl.BlockSpec((None, FF_SHARD, D_MODEL), lambda j, k: (j, 0, 0)))],
        lambda acc: (0.5 * acc,), (FF_SHARD, D_MODEL))[0]

    def dw_up(nm, dact):
        return _matmul(
            nm, (N_CHIPS, nk),
            [(dact, pl.BlockSpec((None, bk, FF_SHARD), lambda j, k: (j, k, 0)),
              n, pl.BlockSpec((bk, D_MODEL), lambda j, k: (k, 0)), TN)],
            [], [(jax.ShapeDtypeStruct((N_CHIPS, FF_SHARD, D_MODEL), BF16),
                  pl.BlockSpec((None, FF_SHARD, D_MODEL), lambda j, k: (j, 0, 0)))],
            _ident, (FF_SHARD, D_MODEL))[0]

    grads[tag + "_w_gate"] = dw_up(tag + "_dwg", dg)
    grads[tag + "_w_up"] = dw_up(tag + "_dwu", du)

    bn = _row_block(s, 512)
    steps = s // bn // 2
    prev, dgain = (), None
    for part, off in (("_dn_a", 0), ("_dn_b", steps)):
        row = pl.BlockSpec((bn, D_MODEL), lambda i, k, off=off: (i + off, 0))
        terms = []
        for j in range(N_CHIPS):
            a_slab = pl.BlockSpec((None, bn, FF_SHARD), lambda i, k, j=j, off=off: (j, i + off, 0))
            w_slab = pl.BlockSpec((None, FF_SHARD, D_MODEL), lambda i, k, j=j: (j, 0, 0))
            terms += [(dg, a_slab, wg, w_slab, NN), (du, a_slab, wu, w_slab, NN)]
        if norm_bwd is None:
            prev = _matmul(tag + part, (steps, 1), terms, [], [(jax.ShapeDtypeStruct((s, D_MODEL), F32), row)],
                           _ident, None, fill=prev)
            continue
        h, gain, dres = norm_bwd
        vec = pl.BlockSpec((1, D_MODEL), lambda i, k: (0, 0))
        res = _matmul(
            tag + part, (steps, 1), terms, [(h, row), (gain, vec), (dres, row)],
            [(jax.ShapeDtypeStruct((s, D_MODEL), F32), row), (jax.ShapeDtypeStruct((s, D_MODEL), BF16), row),
             (jax.ShapeDtypeStruct((1, D_MODEL), F32), vec)],
            _norm_bwd_epilogue(D_MODEL), None, fill=prev, summed=(2,))
        prev = res[:2]
        dgain = res[2] if dgain is None else dgain + res[2]
    return prev[0] if norm_bwd is None else (prev[0], prev[1], dgain)


def _mm_nn(name, a, b, out_dtype, res=None, gain=None):
    s, k = a.shape
    nn = b.shape[1]
    bm = _row_block(s)
    row = pl.BlockSpec((bm, nn), lambda i, kk: (i, 0))
    term = [(a, pl.BlockSpec((bm, k), lambda i, kk: (i, 0)), b, pl.BlockSpec((k, nn), lambda i, kk: (0, 0)), NN)]
    if res is None:
        return _matmul(name, (s // bm, 1), term, [], [(jax.ShapeDtypeStruct((s, nn), out_dtype), row)], _ident, None)[0]
    extras, outs = _residual_outs(s, bm, gain)
    res_out = _matmul(name, (s // bm, 1), term, [(res, row)] + extras, outs,
                      _residual_epilogue(1.0, gain is not None), None)
    return res_out if gain is not None else res_out[0]


def _mm_nt(name, a, b, out_dtype, attn_out=None, nh=0, dv=0):
    s, nn = a.shape
    k = b.shape[0]
    bm = _row_block(s)
    term = [(a, pl.BlockSpec((bm, nn), lambda i, kk: (i, 0)), b, pl.BlockSpec((k, nn), lambda i, kk: (0, 0)), NT)]
    out = (jax.ShapeDtypeStruct((s, k), out_dtype), pl.BlockSpec((bm, k), lambda i, kk: (i, 0)))
    if attn_out is None:
        return _matmul(name, (s // bm, 1), term, [], [out], _ident, None)[0]

    def with_delta(acc, o):
        do = acc.astype(out_dtype).astype(F32)
        cols = [jnp.sum(do[:, h * dv:(h + 1) * dv] * o[:, h * dv:(h + 1) * dv].astype(F32), axis=-1, keepdims=True)
                for h in range(nh)]
        return acc, jnp.stack(cols, axis=0)

    return _matmul(
        name, (s // bm, 1), term, [(attn_out, pl.BlockSpec((bm, nh * dv), lambda i, kk: (i, 0)))],
        [out, (jax.ShapeDtypeStruct((nh, s, 1), F32), pl.BlockSpec((nh, bm, 1), lambda i, kk: (0, i, 0)))],
        with_delta, None)


def _mm_nt_norm_bwd(name, a, b, h, gain, dres):
    s, nn = a.shape
    bm = _row_block(s, 512)
    extras, outs, summed = _norm_bwd_operands(s, bm, h, gain, dres)
    return _matmul(
        name, (s // bm, 1),
        [(a, pl.BlockSpec((bm, nn), lambda i, kk: (i, 0)), b, pl.BlockSpec(b.shape, lambda i, kk: (0, 0)), NT)],
        extras, outs, _norm_bwd_epilogue(D_MODEL), None, summed=summed)


def _w_in_dx_norm_bwd(dz, w_t, h, gain, dres):
    s = dz.shape[0]
    bm = _row_block(s, 512)
    epilogue = _norm_bwd_epilogue(D_MODEL)

    def body(dz_ref, w_ref, h_ref, g_ref, r_ref, dx_ref, dxb_ref, dg_ref):
        dzv = dz_ref[...]
        dn = jnp.concatenate([_dot(dzv, w_ref[j], NN) for j in range(N_CHIPS)], axis=1)
        dx, _, dgain = epilogue(dn, h_ref[...], g_ref[...], r_ref[...])
        dx_ref[...] = dx
        dxb_ref[...] = dx.astype(BF16)

        @pl.when(pl.program_id(0) == 0)
        def _():
            dg_ref[...] = dgain

        @pl.when(pl.program_id(0) > 0)
        def _():
            dg_ref[...] += dgain

    row = pl.BlockSpec((bm, D_MODEL), lambda i: (i, 0))
    vec = pl.BlockSpec((1, D_MODEL), lambda i: (0, 0))
    return _call(
        body, name="w_in_dx", grid=(s // bm,),
        in_specs=[row, pl.BlockSpec(w_t.shape, lambda i: (0, 0, 0)), row, vec, row],
        out_specs=[row, row, vec],
        out_shape=[jax.ShapeDtypeStruct((s, D_MODEL), F32), jax.ShapeDtypeStruct((s, D_MODEL), BF16),
                   jax.ShapeDtypeStruct((1, D_MODEL), F32)],
        sem=("arbitrary",), args=[dz, w_t, h, gain, dres])


def _mm_tn(name, a, b, out_dtype=BF16):
    s, k = a.shape
    nn = b.shape[1]
    bk = _reduce_block(s)
    return _matmul(
        name, (s // bk,),
        [(a, pl.BlockSpec((bk, k), lambda kk: (kk, 0)), b, pl.BlockSpec((bk, nn), lambda kk: (kk, 0)), TN)],
        [], [(jax.ShapeDtypeStruct((k, nn), out_dtype), pl.BlockSpec((k, nn), lambda kk: (0, 0)))],
        _ident, (k, nn))[0]


def _mm_heads_fwd(name, a, w, out_dtype, w_transposed=False):
    s, k = a.shape
    nh = w.shape[0]
    nn = w.shape[1] if w_transposed else w.shape[2]
    bm = _row_block(s)
    return _matmul(
        name, (nh, s // bm, 1),
        [(a, pl.BlockSpec((bm, k), lambda h, i, kk: (i, 0)),
          w, pl.BlockSpec((None,) + w.shape[1:], lambda h, i, kk: (h, 0, 0)), NT if w_transposed else NN)],
        [], [(jax.ShapeDtypeStruct((s, nh * nn), out_dtype), pl.BlockSpec((bm, nn), lambda h, i, kk: (i, h)))],
        _ident, None)[0]


def _mm_heads_bwd(name, dy, a, w, w_transposed=False):
    s, k = a.shape
    nh = w.shape[0]
    nn = w.shape[1] if w_transposed else w.shape[2]
    bm = _row_block(s)
    bk = _reduce_block(s)
    w_spec = pl.BlockSpec((None,) + w.shape[1:], lambda i, h: (h, 0, 0))
    da = _matmul(
        name + "_dx", (s // bm, nh),
        [(dy, pl.BlockSpec((bm, nn), lambda i, h: (i, h)), w, w_spec, NN if w_transposed else NT)],
        [], [(jax.ShapeDtypeStruct((s, k), F32), pl.BlockSpec((bm, k), lambda i, h: (i, 0)))], _ident, (bm, k))[0]
    a_term = (a, pl.BlockSpec((bk, k), lambda h, kk: (kk, 0)))
    dy_term = (dy, pl.BlockSpec((bk, nn), lambda h, kk: (kk, h)))
    lhs, rhs = (dy_term, a_term) if w_transposed else (a_term, dy_term)
    dw = _matmul(
        name + "_dw", (nh, s // bk), [lhs + rhs + (TN,)],
        [], [(jax.ShapeDtypeStruct(w.shape, BF16), pl.BlockSpec((None,) + w.shape[1:], lambda h, kk: (h, 0, 0)))],
        _ident, w.shape[1:])[0]
    return da, dw


def _w_in_fwd(n, w_t):
    s = n.shape[0]
    bm = _row_block(s)
    nh, nout, kin = w_t.shape
    terms = [(n, pl.BlockSpec((bm, kin), lambda i, k, j=j: (i, j)),
              w_t, pl.BlockSpec((None, nout, kin), lambda i, k, j=j: (j, 0, 0)), NT) for j in range(nh)]
    row = pl.BlockSpec((bm, nout), lambda i, k: (i, 0))
    return _matmul("w_in", (s // bm, 1), terms, [], [(jax.ShapeDtypeStruct((s, nout), F32), row)], _ident, None)[0]


def _w_in_dw(dz, n):
    s, nout = dz.shape
    kin = n.shape[1] // N_CHIPS
    bk = _reduce_block(s)
    return _matmul(
        "w_in_dw", (N_CHIPS, s // bk),
        [(dz, pl.BlockSpec((bk, nout), lambda j, k: (k, 0)), n, pl.BlockSpec((bk, kin), lambda j, k: (k, j)), TN)],
        [], [(jax.ShapeDtypeStruct((N_CHIPS, nout, kin), BF16), pl.BlockSpec((None, nout, kin), lambda j, k: (j, 0, 0)))],
        _ident, (nout, kin))[0]


def _rope_tables(positions):
    half = ROPE_DIM // 2
    freqs = 1.0 / (ROPE_BASE ** (jnp.arange(0, ROPE_DIM, 2, dtype=F32) / ROPE_DIM))
    ang = positions.astype(F32)[:, None] * freqs
    cos, sin = jnp.cos(ang), jnp.sin(ang)
    z = jnp.zeros_like(cos)
    tc = jnp.concatenate([cos, cos, z, z], axis=-1)
    ta = jnp.concatenate([-sin, z, z, z], axis=-1)
    tb = jnp.concatenate([z, sin, z, z], axis=-1)
    assert tc.shape[-1] == 4 * half
    return tc, ta, tb


def _rope(x, tc, ta, tb):
    return x * tc + pltpu.roll(x, 96, 1) * ta + pltpu.roll(x, 32, 1) * tb


def _rope_t(dy, tc, ta, tb):
    return dy * tc + pltpu.roll(dy * ta, 32, 1) + pltpu.roll(dy * tb, 96, 1)


def _q_rope(q, tc, ta, tb, transpose):
    s = q.shape[0]
    bm = _row_block(s, 512)
    rot = _rope_t if transpose else _rope

    def body(q_ref, tc_ref, ta_ref, tb_ref, o_ref):
        c, a, b = tc_ref[...], ta_ref[...], tb_ref[...]
        for h in range(MLA_HEADS):
            lo = h * HEAD_QK
            o_ref[:, lo:lo + 128] = q_ref[:, lo:lo + 128].astype(BF16)
            o_ref[:, lo + 128:lo + 256] = rot(q_ref[:, lo + 128:lo + 256], c, a, b).astype(BF16)

    row = pl.BlockSpec((bm, MLA_HEADS * HEAD_QK), lambda i: (i, 0))
    tab = pl.BlockSpec((bm, 128), lambda i: (i, 0))
    return pl.pallas_call(
        body, name="q_rope_t" if transpose else "q_rope", grid=(s // bm,), in_specs=[row, tab, tab, tab],
        out_specs=row, out_shape=jax.ShapeDtypeStruct((s, MLA_HEADS * HEAD_QK), BF16),
        compiler_params=_params("parallel"),
    )(q, tc, ta, tb)


def _kv_assemble(kv, z, tc, ta, tb):
    s = kv.shape[0]
    bm = _row_block(s, 512)

    def body(kv_ref, kr_ref, tc_ref, ta_ref, tb_ref, k_ref, v_ref):
        kpe = _rope(kr_ref[...], tc_ref[...], ta_ref[...], tb_ref[...]).astype(BF16)
        for h in range(MLA_HEADS):
            lo = h * 256
            k_ref[:, lo:lo + 128] = kv_ref[:, lo:lo + 128].astype(BF16)
            k_ref[:, lo + 128:lo + 256] = kpe
            v_ref[:, h * 128:(h + 1) * 128] = kv_ref[:, lo + 128:lo + 256].astype(BF16)

    row = pl.BlockSpec((bm, 1024), lambda i: (i, 0))
    tab = pl.BlockSpec((bm, 128), lambda i: (i, 0))
    return pl.pallas_call(
        body, name="kv_assemble", grid=(s // bm,),
        in_specs=[row, pl.BlockSpec((bm, 128), lambda i: (i, 3)), tab, tab, tab],
        out_specs=[row, pl.BlockSpec((bm, 512), lambda i: (i, 0))],
        out_shape=[jax.ShapeDtypeStruct((s, 1024), BF16), jax.ShapeDtypeStruct((s, 512), BF16)],
        compiler_params=_params("parallel"),
    )(kv, z, tc, ta, tb)


def _kv_assemble_bwd(dk, dv, tc, ta, tb):
    s = dk.shape[0]
    bm = _row_block(s, 512)

    def body(dk_ref, dv_ref, tc_ref, ta_ref, tb_ref, dkv_ref, dkr_ref):
        dpe = None
        for h in range(MLA_HEADS):
            lo = h * 256
            dkv_ref[:, lo:lo + 128] = dk_ref[:, lo:lo + 128].astype(BF16)
            dkv_ref[:, lo + 128:lo + 256] = dv_ref[:, h * 128:(h + 1) * 128].astype(BF16)
            t = dk_ref[:, lo + 128:lo + 256]
            dpe = t if dpe is None else dpe + t
        dkr_ref[...] = _rope_t(dpe, tc_ref[...], ta_ref[...], tb_ref[...])

    row = pl.BlockSpec((bm, 1024), lambda i: (i, 0))
    tab = pl.BlockSpec((bm, 128), lambda i: (i, 0))
    return pl.pallas_call(
        body, name="kv_assemble_bwd", grid=(s // bm,),
        in_specs=[row, pl.BlockSpec((bm, 512), lambda i: (i, 0)), tab, tab, tab],
        out_specs=[row, tab],
        out_shape=[jax.ShapeDtypeStruct((s, 1024), BF16), jax.ShapeDtypeStruct((s, 128), F32)],
        compiler_params=_params("parallel"),
    )(dk, dv, tc, ta, tb)


def _norm_bf16(x, g):
    r = lax.rsqrt(jnp.mean(x * x, axis=-1, keepdims=True) + RMS_EPS)
    return ((x * r) * g).astype(BF16)


def _qkv_prep(z, q_gain, kv_gain, wq_t, wkv, tc, ta, tb):
    s = z.shape[0]
    bm = _row_block(s, 512)

    def body(zq_ref, zkv_ref, zkr_ref, qg_ref, kvg_ref, wq_ref, wkv_ref, tc_ref, ta_ref, tb_ref,
             qn_ref, kvn_ref, q_ref, k_ref, v_ref):
        c, a, b = tc_ref[...], ta_ref[...], tb_ref[...]
        qn = _norm_bf16(zq_ref[...], qg_ref[...])
        kvn = _norm_bf16(zkv_ref[...], kvg_ref[...])
        qn_ref[...] = qn
        kvn_ref[...] = kvn
        kpe = _rope(zkr_ref[...], c, a, b).astype(BF16)
        for h in range(MLA_HEADS):
            lo = h * HEAD_QK
            qp = _dot(qn, wq_ref[h], NT)
            q_ref[:, lo:lo + 128] = qp[:, :128].astype(BF16)
            q_ref[:, lo + 128:lo + 256] = _rope(qp[:, 128:], c, a, b).astype(BF16)
            kv = _dot(kvn, wkv_ref[h], NN)
            k_ref[:, lo:lo + 128] = kv[:, :128].astype(BF16)
            k_ref[:, lo + 128:lo + 256] = kpe
            v_ref[:, h * HEAD_V:(h + 1) * HEAD_V] = kv[:, 128:].astype(BF16)

    def cols(width, blk):
        return pl.BlockSpec((bm, width), lambda i: (i, blk))

    def whole(a):
        return pl.BlockSpec(a.shape, lambda i: (0,) * a.ndim)

    tab = cols(128, 0)
    return pl.pallas_call(
        body, name="qkv_prep", grid=(s // bm,),
        in_specs=[cols(Q_LORA, 0), cols(KV_LORA, 2), cols(128, 3), whole(q_gain), whole(kv_gain), whole(wq_t),
                  whole(wkv), tab, tab, tab],
        out_specs=[cols(Q_LORA, 0), cols(KV_LORA, 0), cols(1024, 0), cols(1024, 0), cols(512, 0)],
        out_shape=[jax.ShapeDtypeStruct((s, Q_LORA), BF16), jax.ShapeDtypeStruct((s, KV_LORA), BF16),
                   jax.ShapeDtypeStruct((s, 1024), BF16), jax.ShapeDtypeStruct((s, 1024), BF16),
                   jax.ShapeDtypeStruct((s, 512), BF16)],
        compiler_params=_params("parallel"),
    )(z, z, z, q_gain, kv_gain, wq_t, wkv, tc, ta, tb)


def _qkv_prep_bwd(dq, dk, dv, z, qn, kvn, q_gain, kv_gain, wq_t, wkv, tc, ta, tb):
    s = z.shape[0]
    bm = _row_block(s, 512)
    nsteps = s // bm

    def body(dq_ref, dk_ref, dv_ref, zq_ref, zkv_ref, qn_ref, kvn_ref, qg_ref, kvg_ref, wq_ref, wkv_ref,
             tc_ref, ta_ref, tb_ref, dz_ref, dqg_ref, dkvg_ref, dwq_ref, dwkv_ref, wq_acc, wkv_acc):
        i = pl.program_id(0)
        c, a, b = tc_ref[...], ta_ref[...], tb_ref[...]

        @pl.when(i == 0)
        def _():
            wq_acc[...] = jnp.zeros_like(wq_acc)
            wkv_acc[...] = jnp.zeros_like(wkv_acc)

        qn, kvn = qn_ref[...], kvn_ref[...]
        dqn = jnp.zeros((bm, Q_LORA), F32)
        dkvn = jnp.zeros((bm, KV_LORA), F32)
        dpe = jnp.zeros((bm, 128), F32)
        for h in range(MLA_HEADS):
            lo = h * HEAD_QK
            dqp = jnp.concatenate([dq_ref[:, lo:lo + 128].astype(BF16),
                                   _rope_t(dq_ref[:, lo + 128:lo + 256], c, a, b).astype(BF16)], axis=1)
            dqn = dqn + _dot(dqp, wq_ref[h], NN)
            wq_acc[h] += _dot(dqp, qn, TN)
            dkv = jnp.concatenate([dk_ref[:, lo:lo + 128].astype(BF16),
                                   dv_ref[:, h * HEAD_V:(h + 1) * HEAD_V].astype(BF16)], axis=1)
            dkvn = dkvn + _dot(dkv, wkv_ref[h], NT)
            wkv_acc[h] += _dot(kvn, dkv, TN)
            dpe = dpe + dk_ref[:, lo + 128:lo + 256]
        dcq, dqg = _rms_bwd_math(dqn, zq_ref[...], qg_ref[...], Q_LORA)
        dckv, dkvg = _rms_bwd_math(dkvn, zkv_ref[...], kvg_ref[...], KV_LORA)
        dz_ref[:, 0:Q_LORA] = dcq.astype(BF16)
        dz_ref[:, Q_LORA:Q_LORA + KV_LORA] = dckv.astype(BF16)
        dz_ref[:, Q_LORA + KV_LORA:512] = _rope_t(dpe, c, a, b).astype(BF16)

        @pl.when(i == 0)
        def _():
            dqg_ref[...] = dqg
            dkvg_ref[...] = dkvg

        @pl.when(i > 0)
        def _():
            dqg_ref[...] += dqg
            dkvg_ref[...] += dkvg

        @pl.when(i == nsteps - 1)
        def _():
            dwq_ref[...] = wq_acc[...].astype(BF16)
            dwkv_ref[...] = wkv_acc[...].astype(BF16)

    def cols(width, blk):
        return pl.BlockSpec((bm, width), lambda i: (i, blk))

    def whole(shape):
        return pl.BlockSpec(shape, lambda i: (0,) * len(shape))

    tab = cols(128, 0)
    return _call(
        body, name="qkv_prep_bwd", grid=(nsteps,),
        in_specs=[cols(1024, 0), cols(1024, 0), cols(512, 0), cols(Q_LORA, 0), cols(KV_LORA, 2), cols(Q_LORA, 0),
                  cols(KV_LORA, 0), whole(q_gain.shape), whole(kv_gain.shape), whole(wq_t.shape), whole(wkv.shape),
                  tab, tab, tab],
        out_specs=[cols(512, 0), whole(q_gain.shape), whole(kv_gain.shape), whole(wq_t.shape), whole(wkv.shape)],
        out_shape=[jax.ShapeDtypeStruct((s, 512), BF16), jax.ShapeDtypeStruct(q_gain.shape, F32),
                   jax.ShapeDtypeStruct(kv_gain.shape, F32), jax.ShapeDtypeStruct(wq_t.shape, BF16),
                   jax.ShapeDtypeStruct(wkv.shape, BF16)],
        scratch_shapes=[pltpu.VMEM(wq_t.shape, F32), pltpu.VMEM(wkv.shape, F32)],
        sem=("arbitrary",), args=[dq, dk, dv, z, z, qn, kvn, q_gain, kv_gain, wq_t, wkv, tc, ta, tb])


def _causal_mask(s, row0, col0):
    rows = row0 + lax.broadcasted_iota(jnp.int32, s.shape, 0)
    cols = col0 + lax.broadcasted_iota(jnp.int32, s.shape, 1)
    return jnp.where(cols <= rows, s, -jnp.inf)


def _attn_fwd(name, q, k, k_off, v, v_off, nh, dq, dv, scale, causal, blk):
    sq, sk = q.shape[0], k.shape[0]
    bq = min(blk, sq)
    bk = min(blk, sk)
    nkv = sk // bk
    assert not causal or (sq == sk and bq == bk)

    hq = bq
    log2e = 1.4426950408889634
    c2 = scale * log2e

    def body(q_ref, k_ref, v_ref, o_ref, lse_ref):
        qi = pl.program_id(1)
        qs = (q_ref[...],)

        def step(j, carry, masked):
            rows = pl.ds(pl.multiple_of(j * bk, bk), bk)
            kb, vb = k_ref[rows, :], v_ref[rows, :]
            out = []
            for t, (m, l, acc) in enumerate(carry):
                s = _dot(qs[t], kb, NT) * c2
                if masked:
                    s = _causal_mask(s, qi * bq + t * hq, j * bk)
                m_new = jnp.maximum(m, jnp.max(s, axis=-1, keepdims=True))
                alpha = jnp.exp2(m - m_new)
                p = jnp.exp2(s - m_new)
                l = alpha * l + jnp.sum(p, axis=-1, keepdims=True)
                acc = alpha * acc + _dot(p, vb, NN)
                out.append((m_new, l, acc))
            return tuple(out)

        one = (jnp.full((hq, 1), -jnp.inf, F32), jnp.zeros((hq, 1), F32), jnp.zeros((hq, dv), F32))
        init = (one,)
        if causal:
            carry = lax.fori_loop(0, qi, lambda j, c: step(j, c, False), init)
            fin = step(qi, carry, True)
        else:
            fin = lax.fori_loop(0, nkv, lambda j, c: step(j, c, False), init)
        for t, (m, l, acc) in enumerate(fin):
            o_ref[t * hq:(t + 1) * hq, :] = (acc / l).astype(o_ref.dtype)
            lse_ref[t * hq:(t + 1) * hq, :] = m * (1.0 / log2e) + jnp.log(l)

    return _call(
        body, name=name, grid=(nh, sq // bq),
        in_specs=[pl.BlockSpec((bq, dq), lambda h, i: (i, h)),
                  pl.BlockSpec((sk, dq), lambda h, i: (0, k_off + h)),
                  pl.BlockSpec((sk, dv), lambda h, i: (0, v_off + h))],
        out_specs=[pl.BlockSpec((bq, dv), lambda h, i: (i, h)), pl.BlockSpec((None, bq, 1), lambda h, i: (h, i, 0))],
        out_shape=[jax.ShapeDtypeStruct((sq, nh * dv), BF16), jax.ShapeDtypeStruct((nh, sq, 1), F32)],
        sem=("parallel", "parallel"), args=[q, k, v])


def _attn_delta(name, do, do_off, o, nh, dv):
    s = o.shape[0]
    bm = _row_block(s, 512)

    def body(do_ref, o_ref, d_ref):
        d_ref[...] = jnp.sum(do_ref[...].astype(F32) * o_ref[...].astype(F32), axis=-1, keepdims=True)

    return pl.pallas_call(
        body, name=name, grid=(nh, s // bm),
        in_specs=[pl.BlockSpec((bm, dv), lambda h, i: (i, do_off + h)), pl.BlockSpec((bm, dv), lambda h, i: (i, h))],
        out_specs=pl.BlockSpec((None, bm, 1), lambda h, i: (h, i, 0)),
        out_shape=jax.ShapeDtypeStruct((nh, s, 1), F32),
        compiler_params=_params("parallel", "parallel"),
    )(do, o)


def _attn_bwd(name, q, k, k_off, v, v_off, do, do_off, lse, delta, nh, dq, dv, scale, causal, blk):
    sq, sk = q.shape[0], k.shape[0]
    bq = min(blk, sq)
    bk = min(blk, sk)
    nq = sq // bq
    assert not causal or (sq == sk and bq == bk)

    def body(q_ref, k_ref, v_ref, do_ref, lse_ref, dl_ref, dq_ref, dk_ref, dv_ref, dk_acc, dv_acc):
        j = pl.program_id(1)

        @pl.when(j == 0)
        def _():
            dq_ref[...] = jnp.zeros_like(dq_ref)

        dk_acc[...] = jnp.zeros_like(dk_acc)
        dv_acc[...] = jnp.zeros_like(dv_acc)
        kv = k_ref[...]
        vv = v_ref[...]

        def step(i, masked):
            rows = pl.ds(pl.multiple_of(i * bq, bq), bq)
            qv = q_ref[rows, :]
            dov = do_ref[rows, :].astype(BF16)
            s = _dot(qv, kv, NT) * scale
            if masked:
                s = _causal_mask(s, i * bq, j * bk)
            p = jnp.exp(s - lse_ref[rows, :])
            dp = _dot(dov, vv, NT)
            ds = (p * (dp - dl_ref[rows, :]) * scale).astype(BF16)
            dv_acc[...] += _dot(p, dov, TN)
            dk_acc[...] += _dot(ds, qv, TN)
            dq_ref[rows, :] += _dot(ds, kv, NN)

        if causal:
            step(j, True)

            def loop(i, c):
                step(i, False)
                return c

            lax.fori_loop(j + 1, nq, loop, 0)
        else:
            def loop(i, c):
                step(i, False)
                return c

            lax.fori_loop(0, nq, loop, 0)
        dk_ref[...] = dk_acc[...]
        dv_ref[...] = dv_acc[...]

    stat = pl.BlockSpec((None, sq, 1), lambda h, j: (h, 0, 0))
    return _call(
        body, name=name, grid=(nh, sk // bk),
        in_specs=[pl.BlockSpec((sq, dq), lambda h, j: (0, h)),
                  pl.BlockSpec((bk, dq), lambda h, j: (j, k_off + h)),
                  pl.BlockSpec((bk, dv), lambda h, j: (j, v_off + h)),
                  pl.BlockSpec((sq, dv), lambda h, j: (0, do_off + h)), stat, stat],
        out_specs=[pl.BlockSpec((sq, dq), lambda h, j: (0, h)),
                   pl.BlockSpec((bk, dq), lambda h, j: (j, h)),
                   pl.BlockSpec((bk, dv), lambda h, j: (j, h))],
        out_shape=[jax.ShapeDtypeStruct((sq, nh * dq), F32), jax.ShapeDtypeStruct((sk, nh * dq), F32),
                   jax.ShapeDtypeStruct((sk, nh * dv), F32)],
        scratch_shapes=[pltpu.VMEM((bk, dq), F32), pltpu.VMEM((bk, dv), F32)],
        sem=("parallel", "arbitrary"), args=[q, k, v, do, lse, delta])


def _pool_diff(z, g):
    s = z.shape[0]
    t = lax.broadcasted_iota(jnp.int32, z.shape, 0)
    acc = z
    sums = []
    for k in (1, 2, 4, 8):
        acc = acc + jnp.where(t >= k, pltpu.roll(acc, k, 0), 0.0)
        sums.append(acc)
    win = jnp.where(g == 0, sums[0], jnp.where(g == 1, sums[1], jnp.where(g == 2, sums[2], sums[3])))
    w = lax.shift_left(jnp.int32(2), g)
    count = jnp.minimum(t + 1, w).astype(F32)
    del s
    return win / count - z, count


def _pool_fwd(z, pool_w, pool_scale):
    s = z.shape[0]

    def body(z_ref, w_ref, sc_ref, o_ref):
        diff, _ = _pool_diff(z_ref[...], pl.program_id(0))
        o_ref[...] = (_dot(diff, w_ref[...], NN) * sc_ref[...]).astype(o_ref.dtype)

    return _call(
        body, name="pool_fwd", grid=(POOL_GROUPS,),
        in_specs=[pl.BlockSpec((s, POOL_CH), lambda g: (0, 4 + g)),
                  pl.BlockSpec((None, POOL_CH, POOL_CH), lambda g: (g, 0, 0)),
                  pl.BlockSpec((1, POOL_CH), lambda g: (0, g))],
        out_specs=[pl.BlockSpec((s, POOL_CH), lambda g: (0, g))],
        out_shape=[jax.ShapeDtypeStruct((s, POOL_GROUPS * POOL_CH), BF16)],
        sem=("parallel",), args=[z, pool_w, pool_scale])[0]


def _pool_bwd(dcat, z, pool_w, pool_scale):
    s = z.shape[0]

    def body(dp_ref, z_ref, w_ref, sc_ref, dz_ref, dw_ref, dsc_ref):
        g = pl.program_id(0)
        diff, count = _pool_diff(z_ref[...], g)
        dpf = dp_ref[...].astype(F32)
        u = _dot(diff, w_ref[...], NN)
        dsc_ref[...] = jnp.sum(dpf * u, axis=0, keepdims=True)
        du = (dpf * sc_ref[...]).astype(BF16)
        dw_ref[...] = _dot(diff, du, TN)
        ddiff = _dot(du, w_ref[...], NT)
        t = lax.broadcasted_iota(jnp.int32, ddiff.shape, 0)
        acc = ddiff / count
        sums = []
        for k in (1, 2, 4, 8):
            acc = acc + jnp.where(t < s - k, pltpu.roll(acc, s - k, 0), 0.0)
            sums.append(acc)
        win = jnp.where(g == 0, sums[0], jnp.where(g == 1, sums[1], jnp.where(g == 2, sums[2], sums[3])))
        dz_ref[...] = win - ddiff

    return pl.pallas_call(
        body, name="pool_bwd", grid=(POOL_GROUPS,),
        in_specs=[pl.BlockSpec((s, POOL_CH), lambda g: (0, 4 + g)),
                  pl.BlockSpec((s, POOL_CH), lambda g: (0, 4 + g)),
                  pl.BlockSpec((None, POOL_CH, POOL_CH), lambda g: (g, 0, 0)),
                  pl.BlockSpec((1, POOL_CH), lambda g: (0, g))],
        out_specs=[pl.BlockSpec((s, POOL_CH), lambda g: (0, g)),
                   pl.BlockSpec((None, POOL_CH, POOL_CH), lambda g: (g, 0, 0)),
                   pl.BlockSpec((1, POOL_CH), lambda g: (0, g))],
        out_shape=[jax.ShapeDtypeStruct((s, POOL_GROUPS * POOL_CH), F32),
                   jax.ShapeDtypeStruct((POOL_GROUPS, POOL_CH, POOL_CH), F32),
                   jax.ShapeDtypeStruct((1, POOL_GROUPS * POOL_CH), F32)],
        compiler_params=_params("parallel"),
    )(dcat, z, pool_w, pool_scale)


def _local_step(x, mem, positions, target, w, grads):
    tc, ta, tb = _rope_tables(positions)
    blk = _ATT_BLOCK

    n1 = _rmsnorm_fwd("ffn1_norm", x, w["ffn1_norm"], D_MODEL)
    a1, dadu1, dadg1 = _ffn_up("ffn1_up", n1, w["ffn1_w_gate"], w["ffn1_w_up"])
    h1, n2 = _ffn_down("ffn1_down", a1, w["ffn1_w_down"], x, w["mix_norm"])
    z = _w_in_fwd(n2, w["w_in"])
    qn, kvn, qf, kf, vf = _qkv_prep(z, w["q_norm"], w["kv_norm"], w["w_q_up"], w["w_kv_up"], tc, ta, tb)
    att, lse = _attn_fwd("mla_fwd", qf, kf, 0, vf, 0, MLA_HEADS, HEAD_QK, HEAD_V, MLA_SCALE, True, blk)
    pool = _pool_fwd(z, w["pool_w"], w["pool_scale"])
    s = x.shape[0]
    bm = _row_block(s)
    row = pl.BlockSpec((bm, D_MODEL), lambda i, k: (i, 0))
    half = pl.BlockSpec((bm, 512), lambda i, k: (i, 0))
    h2, n3 = _matmul(
        "w_out", (s // bm, 1),
        [(att, half, w["w_out"], pl.BlockSpec((512, D_MODEL), lambda i, k: (0, 0)), NN),
         (pool, half, w["w_out"], pl.BlockSpec((512, D_MODEL), lambda i, k: (1, 0)), NN)],
        [(h1, row)] + _residual_outs(s, bm, w["xattn_norm"])[0], _residual_outs(s, bm, w["xattn_norm"])[1],
        _residual_epilogue(1.0, True), None)
    memn = _rmsnorm_fwd("mem_norm", mem, w["mem_norm"], D_MODEL)
    qm = _mm_nn("w_mq", n3, w["w_mq"], BF16)
    kvm = _mm_heads_fwd("w_mkv", memn, w["w_mkv"], BF16)
    om, lse_m = _attn_fwd("xattn_fwd", qm, kvm, 0, kvm, MEM_HEADS, MEM_HEADS, MEM_HEAD_DIM, MEM_HEAD_DIM,
                          MEM_SCALE, False, blk)
    h3, n4 = _mm_nn("w_mo", om, w["w_mo"], F32, res=h2, gain=w["ffn2_norm"])
    a2, dadu2, dadg2 = _ffn_up("ffn2_up", n4, w["ffn2_w_gate"], w["ffn2_w_up"])
    h4 = _ffn_down("ffn2_down", a2, w["ffn2_w_down"], h3)

    dh4, dh4b, loss_vec, d_final = _loss_and_final_norm(h4, w["final_norm"], target)
    grads["final_norm"] = d_final

    dh3, dh3b, grads["ffn2_norm"] = _ffn_bwd("ffn2", dh4b, n4, dadg2, dadu2, a2, w["ffn2_w_gate"], w["ffn2_w_up"],
                                             w["ffn2_w_down"], grads, norm_bwd=(h3, w["ffn2_norm"], dh4))

    dom, delta_m = _mm_nt("w_mo_dx", dh3b, w["w_mo"], BF16, attn_out=om, nh=MEM_HEADS, dv=MEM_HEAD_DIM)
    grads["w_mo"] = _mm_tn("w_mo_dw", om, dh3b)
    dqm, dkm, dvm = _attn_bwd("xattn_bwd", qm, kvm, 0, kvm, MEM_HEADS, dom, 0, lse_m, delta_m, MEM_HEADS,
                              MEM_HEAD_DIM, MEM_HEAD_DIM, MEM_SCALE, False, blk)
    dkvm = jnp.concatenate([dkm, dvm], axis=1).astype(BF16)
    dh2, dh2b, grads["xattn_norm"] = _mm_nt_norm_bwd("w_mq_dx", dqm, w["w_mq"], h2, w["xattn_norm"], dh3)
    grads["w_mq"] = _mm_tn("w_mq_dw", n3, dqm)
    dmemn, grads["w_mkv"] = _mm_heads_bwd("w_mkv", dkvm, memn, w["w_mkv"])
    _, grads["mem_norm"] = _rmsnorm_bwd("mem_norm_bwd", dmemn, mem, w["mem_norm"], D_MODEL, out_dtype=BF16)

    dcat, delta = _mm_nt("w_out_dx", dh2b, w["w_out"], BF16, attn_out=att, nh=MLA_HEADS, dv=HEAD_V)
    grads["w_out"] = jnp.concatenate([_mm_tn("w_out_dw_a", att, dh2b), _mm_tn("w_out_dw_p", pool, dh2b)], axis=0)
    dzp, grads["pool_w"], grads["pool_scale"] = _pool_bwd(dcat, z, w["pool_w"], w["pool_scale"])
    dqf, dkf, dvf = _attn_bwd("mla_bwd", qf, kf, 0, vf, 0, dcat, 0, lse, delta, MLA_HEADS, HEAD_QK, HEAD_V,
                              MLA_SCALE, True, blk)
    dz_lat, grads["q_norm"], grads["kv_norm"], grads["w_q_up"], grads["w_kv_up"] = _qkv_prep_bwd(
        dqf, dkf, dvf, z, qn, kvn, w["q_norm"], w["kv_norm"], w["w_q_up"], w["w_kv_up"], tc, ta, tb)
    dz = jnp.concatenate([dz_lat, dzp.astype(BF16)], axis=1)
    grads["w_in"] = _w_in_dw(dz, n2)
    dh1, dh1b, grads["mix_norm"] = _w_in_dx_norm_bwd(dz, w["w_in"], h1, w["mix_norm"], dh2)

    dn1 = _ffn_bwd("ffn1", dh1b, n1, dadg1, dadu1, a1, w["ffn1_w_gate"], w["ffn1_w_up"], w["ffn1_w_down"], grads)
    dx, grads["ffn1_norm"], _ = _rmsnorm_bwd("ffn1_norm_bwd", dn1, x, w["ffn1_norm"], D_MODEL, dres=dh1)
    return loss_vec[0, 0], dx


def _mesh_pos():
    x, y, c = lax.axis_index("x"), lax.axis_index("y"), lax.axis_index("c")
    chips = [(1 - x, y), (x, 1 - y), (1 - x, 1 - y)]
    chip_ids = [2 * cx + cy for cx, cy in chips]
    return x, y, c, 2 * x + y, chips, chip_ids


def _half_rows(c, rows):
    hr = rows // 2
    return pl.ds(pl.multiple_of(c * hr, 16), hr), pl.ds(pl.multiple_of((1 - c) * hr, 16), hr)


def _ag_ici_stage(shards):
    n = len(shards)

    def copies(ins, outs):
        x, y, c, me, chips, _ = _mesh_pos()
        out = []
        for k in range(n):
            mine, _ = _half_rows(c, ins[k].shape[0])
            out.append((ins[k], outs[k].at[me], None))
            for cx, cy in chips:
                out.append((ins[k].at[mine], outs[k].at[me, mine], (cx, cy, c)))
        return out

    return _Stage(shards, [jax.ShapeDtypeStruct((N_CHIPS,) + s.shape, s.dtype) for s in shards], 3 * n, n, copies)


def _ag_d2d_stage(fulls):
    n = len(fulls)

    def copies(ins, outs):
        x, y, c, me, _, chip_ids = _mesh_pos()
        out = []
        for k in range(n):
            mine, _ = _half_rows(c, ins[k].shape[1])
            for j in range(3):
                out.append((ins[k].at[chip_ids[j], mine], outs[k].at[chip_ids[j], mine], (x, y, 1 - c)))
        return out

    return _Stage(fulls, [jax.ShapeDtypeStruct(f.shape, f.dtype) for f in fulls], 3 * n, 0, copies,
                  aliases={k: k for k in range(n)})


def _rs_swap_stage(grads):
    n = len(grads)

    def copies(ins, outs):
        x, y, c, _, _, _ = _mesh_pos()
        out = []
        for k in range(n):
            _, other = _half_rows(c, ins[k].shape[1])
            out.append((ins[k].at[:, other, :], outs[k], (x, y, 1 - c)))
        return out

    return _Stage(grads, [jax.ShapeDtypeStruct((N_CHIPS, g.shape[1] // 2, g.shape[2]), g.dtype) for g in grads],
                  n, 0, copies)


def _rs_scatter_stage(sums):
    n = len(sums)

    def copies(ins, outs):
        x, y, c, me, chips, chip_ids = _mesh_pos()
        out = []
        for k in range(n):
            mine, _ = _half_rows(c, 2 * ins[k].shape[1])
            out.append((ins[k].at[me], outs[k].at[0, mine, :], None))
            for j, (cx, cy) in enumerate(chips):
                out.append((ins[k].at[chip_ids[j]], outs[k].at[1 + j, mine, :], (cx, cy, c)))
        return out

    return _Stage(sums, [jax.ShapeDtypeStruct((N_CHIPS, 2 * s.shape[1], s.shape[2]), s.dtype) for s in sums],
                  3 * n, n, copies)


def _rs_mirror_stage(parts):
    n = len(parts)

    def copies(ins, outs):
        x, y, c, _, _, _ = _mesh_pos()
        out = []
        for k in range(n):
            mine, _ = _half_rows(c, ins[k].shape[1])
            out.append((ins[k].at[:, mine, :], outs[k].at[:, mine, :], (x, y, 1 - c)))
        return out

    return _Stage(parts, [jax.ShapeDtypeStruct(p.shape, p.dtype) for p in parts], n, 0, copies,
                  aliases={k: k for k in range(n)})


def _pair_add(name, g, r1, core):
    _, rows, cols = g.shape
    hr = rows // 2

    def body(c_ref, g_ref, r_ref, o_ref):
        o_ref[...] = (g_ref[...].astype(F32) + r_ref[...].astype(F32)).astype(BF16)

    half = pl.BlockSpec((None, hr, cols), lambda j, c: (j, 0, 0))
    return pl.pallas_call(
        body, name=name,
        grid_spec=pltpu.PrefetchScalarGridSpec(
            num_scalar_prefetch=1, grid=(N_CHIPS,),
            in_specs=[pl.BlockSpec((None, hr, cols), lambda j, c: (j, c[0], 0)), half], out_specs=half),
        out_shape=jax.ShapeDtypeStruct((N_CHIPS, hr, cols), BF16),
        compiler_params=_params("parallel"),
    )(core, g, r1)


def _all_gather_weights(shards):
    n = len(shards)

    def body(*refs):
        ins, outs = refs[:n], refs[n:2 * n]
        send, recv, loc = refs[2 * n:]
        x, y, c, me, chips, chip_ids = _mesh_pos()
        sib = (x, y, 1 - c)

        def halves(k):
            hr = ins[k].shape[0] // 2
            return pl.ds(pl.multiple_of(c * hr, 16), hr), pl.ds(pl.multiple_of((1 - c) * hr, 16), hr)

        def remote(src, dst, k, j, dev):
            return pltpu.make_async_remote_copy(src_ref=src, dst_ref=dst, send_sem=send.at[k, j],
                                                recv_sem=recv.at[k, j], device_id=dev, device_id_type=_MESH)

        started = []
        local = []
        for k in range(n):
            mine, _ = halves(k)
            cp = pltpu.make_async_copy(ins[k], outs[k].at[me], loc.at[k])
            cp.start()
            local.append(cp)
            for j, (cx, cy) in enumerate(chips):
                cp = remote(ins[k].at[mine], outs[k].at[me, mine], k, j, (cx, cy, c))
                cp.start()
                started.append(cp)
        for k in range(n):
            mine, _ = halves(k)
            for j in range(3):
                land = outs[k].at[chip_ids[j], mine]
                remote(land, land, k, j, sib).wait_recv()
                cp = remote(land, land, k, 3 + j, sib)
                cp.start()
                started.append(cp)
        for k in range(n):
            _, other = halves(k)
            for j in range(3):
                land = outs[k].at[chip_ids[j], other]
                remote(land, land, k, 3 + j, sib).wait_recv()
        for cp in started:
            cp.wait_send()
        for cp in local:
            cp.wait()

    return pl.pallas_call(
        body, name="all_gather_weights", in_specs=[_ANY] * n, out_specs=[_ANY] * n,
        out_shape=[jax.ShapeDtypeStruct((N_CHIPS,) + s.shape, s.dtype) for s in shards],
        scratch_shapes=[pltpu.SemaphoreType.DMA((n, 6)), pltpu.SemaphoreType.DMA((n, 6)),
                        pltpu.SemaphoreType.DMA((n,))],
        compiler_params=pltpu.CompilerParams(vmem_limit_bytes=V7X_VMEM_LIMIT_BYTES),
    )(*shards)


_RS_CHUNK = 32


def _reduce_scatter(name, grads):
    n = len(grads)

    def body(*refs):
        gs, outs = refs[:n], refs[n:2 * n]
        own, r1, r2, fin = (refs[(2 + i) * n:(3 + i) * n] for i in range(4))
        a_send, a_recv, b_send, b_recv, c_send, c_recv, l_in, l_out = refs[6 * n:]
        x, y, c, me, chips, chip_ids = _mesh_pos()
        sib = (x, y, 1 - c)

        def halves(k):
            hr = gs[k].shape[1] // 2
            return hr, pl.ds(pl.multiple_of(c * hr, 16), hr), pl.ds(pl.multiple_of((1 - c) * hr, 16), hr)

        def remote(src, dst, ssem, rsem, dev):
            return pltpu.make_async_remote_copy(src_ref=src, dst_ref=dst, send_sem=ssem, recv_sem=rsem,
                                                device_id=dev, device_id_type=_MESH)

        sends, locals_in = [], []
        for k in range(n):
            hr, mine, other = halves(k)
            cp = remote(gs[k].at[:, other, :], r1[k], a_send.at[k], a_recv.at[k], sib)
            cp.start()
            sends.append(cp)
            cp = pltpu.make_async_copy(gs[k].at[:, mine, :], own[k], l_in.at[k])
            cp.start()
            locals_in.append(cp)

        for k in range(n):
            hr, mine, other = halves(k)
            locals_in[k].wait()
            remote(r1[k], r1[k], a_send.at[k], a_recv.at[k], sib).wait_recv()
            for j in range(N_CHIPS):
                def add(i, carry, k=k, j=j):
                    rows = pl.ds(pl.multiple_of(i * _RS_CHUNK, _RS_CHUNK), _RS_CHUNK)
                    own[k][j, rows, :] = (own[k][j, rows, :].astype(F32) + r1[k][j, rows, :].astype(F32)).astype(BF16)
                    return carry

                lax.fori_loop(0, hr // _RS_CHUNK, add, 0)
            for j, (cx, cy) in enumerate(chips):
                cp = remote(own[k].at[chip_ids[j]], r2[k].at[j], b_send.at[k, j], b_recv.at[k, j], (cx, cy, c))
                cp.start()
                sends.append(cp)

        locals_out = []
        for k in range(n):
            hr, mine, other = halves(k)
            for j in range(3):
                remote(r2[k].at[j], r2[k].at[j], b_send.at[k, j], b_recv.at[k, j], sib).wait_recv()

            def total(i, carry, k=k):
                rows = pl.ds(pl.multiple_of(i * _RS_CHUNK, _RS_CHUNK), _RS_CHUNK)
                acc = own[k][me, rows, :].astype(F32)
                for j in range(3):
                    acc = acc + r2[k][j, rows, :].astype(F32)
                fin[k][rows, :] = acc
                return carry

            lax.fori_loop(0, hr // _RS_CHUNK, total, 0)
            cp = remote(fin[k], outs[k].at[mine, :], c_send.at[k], c_recv.at[k], sib)
            cp.start()
            sends.append(cp)
            cp = pltpu.make_async_copy(fin[k], outs[k].at[mine, :], l_out.at[k])
            cp.start()
            locals_out.append(cp)

        for k in range(n):
            hr, mine, other = halves(k)
            land = outs[k].at[other, :]
            remote(land, land, c_send.at[k], c_recv.at[k], sib).wait_recv()
        for cp in sends:
            cp.wait_send()
        for cp in locals_out:
            cp.wait()

    scratch = []
    for g in grads:
        scratch.append(pltpu.VMEM((N_CHIPS, g.shape[1] // 2, g.shape[2]), BF16))
    for g in grads:
        scratch.append(pltpu.VMEM((N_CHIPS, g.shape[1] // 2, g.shape[2]), BF16))
    for g in grads:
        scratch.append(pltpu.VMEM((3, g.shape[1] // 2, g.shape[2]), BF16))
    for g in grads:
        scratch.append(pltpu.VMEM((g.shape[1] // 2, g.shape[2]), F32))
    dma = pltpu.SemaphoreType.DMA
    scratch += [dma((n,)), dma((n,)), dma((n, 3)), dma((n, 3)), dma((n,)), dma((n,)), dma((n,)), dma((n,))]
    return pl.pallas_call(
        body, name=name, in_specs=[_ANY] * n, out_specs=[_ANY] * n,
        out_shape=[jax.ShapeDtypeStruct(g.shape[1:], F32) for g in grads],
        scratch_shapes=scratch,
        compiler_params=pltpu.CompilerParams(vmem_limit_bytes=V7X_VMEM_LIMIT_BYTES),
    )(*grads)


def _adamw_math(w, g, m, v):
    m = ADAM_B1 * m + (1.0 - ADAM_B1) * g
    v = ADAM_B2 * v + (1.0 - ADAM_B2) * (g * g)
    m_hat = m / (1.0 - ADAM_B1 ** ADAM_STEP)
    v_hat = v / (1.0 - ADAM_B2 ** ADAM_STEP)
    delta = -ADAM_LR * (m_hat / (jnp.sqrt(v_hat) + ADAM_EPS) + ADAM_WD * w)
    return delta, m, v


def _adamw_sum(name, w, parts, m, v):
    r, c = w.shape
    br = r
    while br * c * 4 > (1 << 20) and br % 32 == 0:
        br //= 2

    def body(w_ref, p_ref, m_ref, v_ref, g_ref, d_ref, nm_ref, nv_ref):
        g = p_ref[0].astype(F32)
        for j in range(1, N_CHIPS):
            g = g + p_ref[j].astype(F32)
        d, nm, nv = _adamw_math(w_ref[...], g, m_ref[...], v_ref[...])
        g_ref[...] = g
        d_ref[...] = d
        nm_ref[...] = nm
        nv_ref[...] = nv

    spec = pl.BlockSpec((br, c), lambda i: (i, 0))
    shp = jax.ShapeDtypeStruct((r, c), F32)
    return _call(
        body, name=name, grid=(r // br,),
        in_specs=[spec, pl.BlockSpec((N_CHIPS, br, c), lambda i: (0, i, 0)), spec, spec],
        out_specs=[spec] * 4, out_shape=[shp] * 4, sem=("parallel",), args=[w, parts, m, v])


_SMALL_ROWS = 80


def _small_allreduce_adamw(gpack, wpack, mpack, vpack):
    half = _SMALL_ROWS // 2

    def body(g_ref, w_ref, m_ref, v_ref, go_ref, d_ref, nm_ref, nv_ref, sib_buf, chip_sum, buf, send, recv):
        x, y, c, me, chips, chip_ids = _mesh_pos()
        sib = (x, y, 1 - c)
        mine = pl.ds(pl.multiple_of(c * half, 8), half)

        def remote(src, dst, k, dev):
            return pltpu.make_async_remote_copy(src_ref=src, dst_ref=dst, send_sem=send.at[k], recv_sem=recv.at[k],
                                                device_id=dev, device_id_type=_MESH)

        swap = remote(g_ref, sib_buf, 0, sib)
        swap.start()
        swap.wait()
        chip_sum[...] = g_ref[...] + sib_buf[...]
        buf[me] = chip_sum[...]
        sends = [remote(chip_sum.at[mine], buf.at[me, mine], 1 + j, (cx, cy, c)) for j, (cx, cy) in enumerate(chips)]
        for cp in sends:
            cp.start()
        for cp in sends:
            cp.wait()
        mirrors = [remote(buf.at[chip_ids[j], mine], buf.at[chip_ids[j], mine], 4 + j, sib) for j in range(3)]
        for cp in mirrors:
            cp.start()
        for cp in mirrors:
            cp.wait()
        total = buf[0]
        for i in range(1, N_CHIPS):
            total = total + buf[i]
        go_ref[...] = total
        d, nm, nv = _adamw_math(w_ref[...], total, m_ref[...], v_ref[...])
        d_ref[...] = d
        nm_ref[...] = nm
        nv_ref[...] = nv

    vm = pl.BlockSpec(memory_space=pltpu.VMEM)
    shp = jax.ShapeDtypeStruct((_SMALL_ROWS, D_MODEL), F32)
    return pl.pallas_call(
        body, name="small_allreduce_adamw", in_specs=[vm] * 4, out_specs=[vm] * 4, out_shape=[shp] * 4,
        scratch_shapes=[pltpu.VMEM((_SMALL_ROWS, D_MODEL), F32), pltpu.VMEM((_SMALL_ROWS, D_MODEL), F32),
                        pltpu.VMEM((N_CHIPS, _SMALL_ROWS, D_MODEL), F32), pltpu.SemaphoreType.DMA((7,)),
                        pltpu.SemaphoreType.DMA((7,))],
        compiler_params=pltpu.CompilerParams(vmem_limit_bytes=V7X_VMEM_LIMIT_BYTES),
    )(gpack, wpack, mpack, vpack)


_SMALL_VECTORS = ("ffn1_norm", "mix_norm", "xattn_norm", "mem_norm", "ffn2_norm", "final_norm", "q_norm",
                  "kv_norm", "pool_scale")


_LOSS_ROW = 9


def _pack_small(d, scalar=None):
    rows = []
    for n in _SMALL_VECTORS:
        v = d[n].reshape(1, -1).astype(F32)
        rows.append(jnp.pad(v, ((0, 0), (0, D_MODEL - v.shape[1]))))
    assert len(rows) == _LOSS_ROW
    extra = jnp.zeros((1, D_MODEL), F32) if scalar is None else jnp.pad(scalar.reshape(1, 1), ((0, 0), (0, D_MODEL - 1)))
    rows.append(extra)
    rows.append(jnp.zeros((16 - len(rows), D_MODEL), F32))
    rows.append(d["pool_w"].reshape(64, D_MODEL).astype(F32))
    return jnp.concatenate(rows, axis=0)


def _unpack_small(pack, like):
    out = {}
    for i, n in enumerate(_SMALL_VECTORS):
        out[n] = pack[i, :like[n].size].reshape(like[n].shape)
    out["pool_w"] = pack[16:].reshape(like["pool_w"].shape)
    return out


_WEIGHTS = ("ffn1_norm", "ffn1_w_gate", "ffn1_w_up", "ffn1_w_down", "mix_norm", "w_in", "q_norm", "w_q_up",
            "kv_norm", "w_kv_up", "pool_w", "pool_scale", "w_out", "xattn_norm", "mem_norm", "w_mq", "w_mkv",
            "w_mo", "ffn2_norm", "ffn2_w_gate", "ffn2_w_up", "ffn2_w_down", "final_norm")
_SHARDED = ("ffn1_w_gate", "ffn1_w_up", "ffn1_w_down", "w_in", "w_q_up", "w_kv_up", "w_out", "w_mq", "w_mkv",
            "w_mo", "ffn2_w_gate", "ffn2_w_up", "ffn2_w_down")
_RS_GROUPS = (("ffn2_w_gate", "ffn2_w_up", "ffn2_w_down"),
              ("w_mo", "w_mq", "w_mkv", "w_out", "w_q_up", "w_kv_up", "w_in"),
              ("ffn1_w_gate", "ffn1_w_up", "ffn1_w_down"))
W_IN_SPLIT = Q_LORA + KV_LORA + ROPE_DIM


_TRANSPOSED = ("ffn1_w_gate", "ffn1_w_up", "ffn2_w_gate", "ffn2_w_up", "w_in", "w_q_up")


def _local_view(name, a):
    return jnp.swapaxes(a, 1, 2)[0] if name in _TRANSPOSED else a[0]


def _global_view(name, a):
    return jnp.swapaxes(a[None], 1, 2) if name in _TRANSPOSED else a[None]


def _pad_shard(name, a):
    if name == "w_in":
        return jnp.concatenate([a[:W_IN_SPLIT], jnp.zeros((64, a.shape[1]), a.dtype), a[W_IN_SPLIT:]], axis=0)
    if name == "w_q_up":
        return jnp.pad(a, ((0, 64), (0, 0)))
    return a


def _unpad_shard(name, a):
    if name == "w_in":
        return jnp.concatenate([a[:, :W_IN_SPLIT], a[:, W_IN_SPLIT + 64:]], axis=1)
    if name == "w_q_up":
        return a[:, :192]
    return a


def _stacked(g):
    return g if g.ndim == 3 else g.reshape(N_CHIPS, g.shape[0] // N_CHIPS, g.shape[1])


class _Plan:
    AG_UNITS = (
        (("w_in", "w_q_up", "w_kv_up", "ffn2_w_gate"), "ffn1_up", "ffn1_down"),
        (("ffn2_w_up",), "ffn1_down", "w_in"),
        (("w_out", "w_mq", "w_mkv", "w_mo", "ffn2_w_down"), "mla_fwd", "pool_fwd"),
    )
    RS_UNITS = (
        (("ffn2_w_gate", "ffn2_w_up", "ffn2_w_down"), "ffn2_dn_a", "mla_bwd", "qkv_prep_bwd"),
        (("w_mo", "w_mq", "w_mkv"), "w_out_dx", "mla_bwd", "qkv_prep_bwd"),
        (("w_out", "w_q_up", "w_kv_up", "w_in"), "w_in_dx", "ffn1_dact", "ffn1_dwd"),
        (("ffn1_w_down",), "ffn1_dwg", "ffn1_dwu", "ffn1_dn_a"),
        (("ffn1_w_gate",), "ffn1_dwu", "ffn1_dn_a", "ffn1_dn_b"),
        (("ffn1_w_up",), "ffn1_dn_a", "ffn1_dn_b", "adamw_w_kv_up"),
    )
    ADAMW_ORDER = ("w_kv_up", "ffn2_w_gate", "ffn2_w_up", "ffn2_w_down", "w_mo", "w_mq", "w_mkv", "w_out", "w_q_up",
                   "w_in", "ffn1_w_down", "ffn1_w_gate", "ffn1_w_up")

    def __init__(self, shards, w, grads, core):
        self.shards, self.w, self.grads, self.core = shards, w, grads, core
        self.parts = {}
        self.ag = [[None, None] for _ in self.AG_UNITS]
        self.rs = [[None, None, None, None] for _ in self.RS_UNITS]

    def pre(self, name):
        for i, (names, h1, h2) in enumerate(self.AG_UNITS):
            if name == h1:
                self.ag[i][0] = _host(name, _ag_ici_stage([self.shards[n] for n in names]))
            if name == h2:
                self.ag[i][1] = _host(name, _ag_d2d_stage(self.ag[i][0].results))
        for i, (names, h1, h2, h3) in enumerate(self.RS_UNITS):
            if name == h1:
                self.rs[i][0] = _host(name, _rs_swap_stage([_stacked(self.grads[n]) for n in names]))
            if name == h2:
                self.rs[i][2] = _host(name, _rs_scatter_stage(self.rs[i][1]))
            if name == h3:
                self.rs[i][3] = _host(name, _rs_mirror_stage(self.rs[i][2].results))

    def post(self, name):
        for i, (names, h1, h2) in enumerate(self.AG_UNITS):
            if name == h2:
                for n, f in zip(names, self.ag[i][1].results):
                    self.w[n] = _full_weight(n, f)
        for i, (names, h1, h2, h3) in enumerate(self.RS_UNITS):
            if name == h1:
                self.rs[i][1] = [_pair_add("pair_add_" + n, _stacked(self.grads[n]), r1, self.core)
                                 for n, r1 in zip(names, self.rs[i][0].results)]
            if name == h3:
                for n, p in zip(names, self.rs[i][3].results):
                    self.parts[n] = p


def _full_weight(name, stacked):
    if name in ("w_out", "w_mq", "w_mo"):
        return stacked.reshape(D_MODEL, D_MODEL)
    return stacked


def kernel(x, mem, positions, ffn1_norm, ffn1_w_gate, ffn1_w_up, ffn1_w_down, mix_norm, w_in, q_norm, w_q_up, kv_norm, w_kv_up, pool_w, pool_scale, w_out, xattn_norm, mem_norm, w_mq, w_mkv, w_mo, ffn2_norm, ffn2_w_gate, ffn2_w_up, ffn2_w_down, final_norm, loss_target, m_ffn1_norm, m_ffn1_w_gate, m_ffn1_w_up, m_ffn1_w_down, m_mix_norm, m_w_in, m_q_norm, m_w_q_up, m_kv_norm, m_w_kv_up, m_pool_w, m_pool_scale, m_w_out, m_xattn_norm, m_mem_norm, m_w_mq, m_w_mkv, m_w_mo, m_ffn2_norm, m_ffn2_w_gate, m_ffn2_w_up, m_ffn2_w_down, m_final_norm, v_ffn1_norm, v_ffn1_w_gate, v_ffn1_w_up, v_ffn1_w_down, v_mix_norm, v_w_in, v_q_norm, v_w_q_up, v_kv_norm, v_w_kv_up, v_pool_w, v_pool_scale, v_w_out, v_xattn_norm, v_mem_norm, v_w_mq, v_w_mkv, v_w_mo, v_ffn2_norm, v_ffn2_w_gate, v_ffn2_w_up, v_ffn2_w_down, v_final_norm):
    wts = dict(zip(_WEIGHTS, (ffn1_norm, ffn1_w_gate, ffn1_w_up, ffn1_w_down, mix_norm, w_in, q_norm, w_q_up, kv_norm, w_kv_up, pool_w, pool_scale, w_out, xattn_norm, mem_norm, w_mq, w_mkv, w_mo, ffn2_norm, ffn2_w_gate, ffn2_w_up, ffn2_w_down, final_norm)))
    mom = dict(zip(_WEIGHTS, (m_ffn1_norm, m_ffn1_w_gate, m_ffn1_w_up, m_ffn1_w_down, m_mix_norm, m_w_in, m_q_norm, m_w_q_up, m_kv_norm, m_w_kv_up, m_pool_w, m_pool_scale, m_w_out, m_xattn_norm, m_mem_norm, m_w_mq, m_w_mkv, m_w_mo, m_ffn2_norm, m_ffn2_w_gate, m_ffn2_w_up, m_ffn2_w_down, m_final_norm)))
    var = dict(zip(_WEIGHTS, (v_ffn1_norm, v_ffn1_w_gate, v_ffn1_w_up, v_ffn1_w_down, v_mix_norm, v_w_in, v_q_norm, v_w_q_up, v_kv_norm, v_w_kv_up, v_pool_w, v_pool_scale, v_w_out, v_xattn_norm, v_mem_norm, v_w_mq, v_w_mkv, v_w_mo, v_ffn2_norm, v_ffn2_w_gate, v_ffn2_w_up, v_ffn2_w_down, v_final_norm)))
    small = [n for n in _WEIGHTS if n not in _SHARDED]

    global _PLAN
    shards = {n: _pad_shard(n, _local_view(n, wts[n])).astype(BF16) for n in _SHARDED}
    w = {n: wts[n].reshape(1, -1) for n in _SMALL_VECTORS}
    w["pool_w"] = pool_w[0].astype(BF16)
    grads = {}
    core = lax.axis_index("c").astype(jnp.int32).reshape(1)
    plan = _Plan(shards, w, grads, core)
    _PLAN = plan
    try:
        first = ("ffn1_w_gate", "ffn1_w_up", "ffn1_w_down")
        for n, f in zip(first, _all_gather_weights([shards[n] for n in first])):
            w[n] = f

        loss_local, dx = _local_step(x[0], mem[0], positions[0], loss_target[0], w, grads)

        gpack, dpack, mpack, vpack = _small_allreduce_adamw(
            _pack_small({n: grads[n] for n in small}, loss_local), _pack_small({n: wts[n] for n in small}),
            _pack_small({n: mom[n] for n in small}), _pack_small({n: var[n] for n in small}))
        like = {n: wts[n] for n in small}
        g_out, d_out, m_out, v_out = (_unpack_small(p, like) for p in (gpack, dpack, mpack, vpack))
        loss = gpack[_LOSS_ROW, 0]

        for n in _Plan.ADAMW_ORDER:
            res = _adamw_sum("adamw_" + n, _local_view(n, wts[n]), _unpad_shard(n, plan.parts[n]),
                             _local_view(n, mom[n]), _local_view(n, var[n]))
            g_out[n], d_out[n], m_out[n], v_out[n] = (_global_view(n, r) for r in res)
    finally:
        _PLAN = None
        _PENDING.clear()

    return (loss, dx[None], *[g_out[n] for n in _WEIGHTS], *[d_out[n] for n in _WEIGHTS],
            *[m_out[n] for n in _WEIGHTS], *[v_out[n] for n in _WEIGHTS])
```

```python
import functools

import jax
import jax.numpy as jnp
from jax import lax
from jax.experimental import pallas as pl
from jax.experimental.pallas import tpu as pltpu

F32 = jnp.float32
BF16 = jnp.bfloat16

D_MODEL = 1024
D_FF = 2816
N_CHIPS = 4
FF_SHARD = D_FF // N_CHIPS
MLA_HEADS = 4
Q_LORA = 256
KV_LORA = 128
ROPE_DIM = 64
HEAD_QK = 256
HEAD_V = 128
POOL_GROUPS = 4
POOL_CH = 128
MEM_HEADS = 4
MEM_HEAD_DIM = 256
RMS_EPS = 1e-6
ROPE_BASE = 10000.0
MLA_SCALE = (128 + 64) ** -0.5
MEM_SCALE = MEM_HEAD_DIM ** -0.5

ADAM_LR = 0.001
ADAM_B1 = 0.9
ADAM_B2 = 0.999
ADAM_EPS = 1e-08
ADAM_WD = 0.01
ADAM_STEP = 10

V7X_VMEM_LIMIT_BYTES = 56 * 1024 * 1024

NN = ((1,), (0,))
NT = ((1,), (1,))
TN = ((0,), (0,))


def _params(*sem):
    return pltpu.CompilerParams(dimension_semantics=sem, vmem_limit_bytes=V7X_VMEM_LIMIT_BYTES)


_MESH = pl.DeviceIdType.MESH
_ANY = pl.BlockSpec(memory_space=pl.ANY)


class _Stage:
    def __init__(self, ins, outs, n_remote, n_local, copies, aliases=None):
        self.ins, self.outs, self.n_remote, self.n_local = list(ins), list(outs), n_remote, n_local
        self.copies, self.aliases = copies, dict(aliases or {})
        self.results = None

    def descriptors(self, in_refs, out_refs, send, recv, loc):
        ds, ri, li = [], 0, 0
        for src, dst, dev in self.copies(in_refs, out_refs):
            if dev is None:
                ds.append(pltpu.make_async_copy(src, dst, loc.at[li]))
                li += 1
            else:
                ds.append(pltpu.make_async_remote_copy(src_ref=src, dst_ref=dst, send_sem=send.at[ri],
                                                       recv_sem=recv.at[ri], device_id=dev, device_id_type=_MESH))
                ri += 1
        assert ri == self.n_remote and li == self.n_local
        return ds


_PENDING = {}


def _host(name, stage):
    _PENDING.setdefault(name, []).append(stage)
    return stage


_PLAN = None


def _call(body, **kw):
    if _PLAN is not None:
        _PLAN.pre(kw["name"])
    res = _call_hosting(body, **kw)
    if _PLAN is not None:
        _PLAN.post(kw["name"])
    return res


def _call_hosting(body, *, name, grid, in_specs, out_specs, out_shape, sem, args, scratch_shapes=(), aliases=None):
    stages = _PENDING.pop(name, [])
    scratch_shapes = list(scratch_shapes)
    if not stages:
        return pl.pallas_call(body, name=name, grid=grid, in_specs=in_specs, out_specs=out_specs,
                              out_shape=out_shape, scratch_shapes=scratch_shapes,
                              input_output_aliases=dict(aliases or {}), compiler_params=_params(*sem))(*args)
    ni, no, ns = len(in_specs), len(out_shape), len(scratch_shapes)
    c_ins = [a for st in stages for a in st.ins]
    c_outs = [o for st in stages for o in st.outs]
    nci, nco = len(c_ins), len(c_outs)
    aliases, io, oo = dict(aliases or {}), 0, 0
    for st in stages:
        for i, j in st.aliases.items():
            aliases[ni + io + i] = no + oo + j
        io += len(st.ins)
        oo += len(st.outs)
    dma = pltpu.SemaphoreType.DMA
    sems = []
    for st in stages:
        sems += [dma((max(st.n_remote, 1),)), dma((max(st.n_remote, 1),)), dma((max(st.n_local, 1),))]

    def wrapped(*refs):
        ins, cin = refs[:ni], refs[ni:ni + nci]
        outs, cout = refs[ni + nci:ni + nci + no], refs[ni + nci + no:ni + nci + no + nco]
        scr = refs[ni + nci + no + nco:ni + nci + no + nco + ns]
        sem_refs = refs[ni + nci + no + nco + ns:]
        first = pl.program_id(0) == 0
        last = pl.program_id(0) == grid[0] - 1
        for ax in range(1, len(grid)):
            first = jnp.logical_and(first, pl.program_id(ax) == 0)
            last = jnp.logical_and(last, pl.program_id(ax) == grid[ax] - 1)

        def descriptors():
            ds, io, oo = [], 0, 0
            for si, st in enumerate(stages):
                ds += st.descriptors(cin[io:io + len(st.ins)], cout[oo:oo + len(st.outs)], *sem_refs[3 * si:3 * si + 3])
                io += len(st.ins)
                oo += len(st.outs)
            return ds

        body(*ins, *outs, *scr)

        @pl.when(first)
        def _():
            for d in descriptors():
                d.start()

        @pl.when(last)
        def _():
            for d in descriptors():
                d.wait()

    res = pl.pallas_call(
        wrapped, name=name, grid=grid, in_specs=list(in_specs) + [_ANY] * nci,
        out_specs=list(out_specs) + [_ANY] * nco, out_shape=list(out_shape) + c_outs,
        scratch_shapes=scratch_shapes + sems, input_output_aliases=aliases,
        compiler_params=_params(*(("arbitrary",) * len(grid))))(*args, *c_ins)
    oo = no
    for st in stages:
        st.results = list(res[oo:oo + len(st.outs)])
        oo += len(st.outs)
    return list(res[:no])


def _dot(a, b, dims):
    return lax.dot_general(a.astype(BF16), b.astype(BF16), (dims, ((), ())), preferred_element_type=F32)


_MAX_ROW_BLOCK = 1024
_ATT_BLOCK = 512


_MAX_REDUCE_BLOCK = 2048


def _row_block(s, want=1024):
    return min(want, s, _MAX_ROW_BLOCK)


def _reduce_block(s):
    return min(s, _MAX_REDUCE_BLOCK)


def _matmul(name, grid, terms, extras, outs, epilogue, acc_shape, fill=(), summed=()):
    nt, ne, no, nf = len(terms), len(extras), len(outs), len(fill)
    nk = grid[-1]
    dims = [t[4] for t in terms]

    def body(*refs):
        a_refs, b_refs = refs[:nt], refs[nt:2 * nt]
        e_refs = refs[2 * nt:2 * nt + ne]
        o_refs = refs[2 * nt + ne + nf:2 * nt + ne + nf + no]

        def finish(acc):
            vals = epilogue(acc, *[e[...] for e in e_refs])
            for idx, (o, val) in enumerate(zip(o_refs, vals)):
                if idx in summed:
                    @pl.when(pl.program_id(0) == 0)
                    def _(o=o, val=val):
                        o[...] = val.astype(o.dtype)

                    @pl.when(pl.program_id(0) > 0)
                    def _(o=o, val=val):
                        o[...] += val.astype(o.dtype)
                else:
                    o[...] = val.astype(o.dtype)

        if nk == 1:
            part = None
            for a, b, d in zip(a_refs, b_refs, dims):
                t = _dot(a[...], b[...], d)
                part = t if part is None else part + t
            finish(part)
        else:
            acc_ref = refs[-1]
            k = pl.program_id(len(grid) - 1)

            @pl.when(k == 0)
            def _():
                acc_ref[...] = jnp.zeros_like(acc_ref)

            for a, b, d in zip(a_refs, b_refs, dims):
                acc_ref[...] += _dot(a[...], b[...], d)

            @pl.when(k == nk - 1)
            def _():
                finish(acc_ref[...])

    in_specs = [t[1] for t in terms] + [t[3] for t in terms] + [e[1] for e in extras] + [_ANY] * nf
    args = [t[0] for t in terms] + [t[2] for t in terms] + [e[0] for e in extras] + list(fill)
    sem = ("arbitrary" if summed else "parallel",) * (len(grid) - 1) + ("arbitrary",)
    aliases = {2 * nt + ne + i: i for i in range(nf)}
    return _call(
        body, name=name, grid=grid, in_specs=in_specs,
        out_specs=[o[1] for o in outs], out_shape=[o[0] for o in outs],
        scratch_shapes=[pltpu.VMEM(acc_shape, F32)] if nk > 1 else [], sem=sem, args=args, aliases=aliases)


def _ident(acc):
    return (acc,)


def _rmsnorm_fwd(name, x, gain, width, col_block=0):
    s = x.shape[0]
    bm = _row_block(s)

    def body(x_ref, g_ref, o_ref):
        xf = x_ref[...]
        r = lax.rsqrt(jnp.mean(xf * xf, axis=-1, keepdims=True) + RMS_EPS)
        o_ref[...] = ((xf * r) * g_ref[...]).astype(o_ref.dtype)

    return pl.pallas_call(
        body, name=name, grid=(s // bm,),
        in_specs=[pl.BlockSpec((bm, width), lambda i: (i, col_block)), pl.BlockSpec((1, width), lambda i: (0, 0))],
        out_specs=pl.BlockSpec((bm, width), lambda i: (i, 0)),
        out_shape=jax.ShapeDtypeStruct((s, width), BF16),
        compiler_params=_params("parallel"),
    )(x, gain)


def _rms_bwd_math(dy, xf, g, width):
    r = lax.rsqrt(jnp.mean(xf * xf, axis=-1, keepdims=True) + RMS_EPS)
    dyg = dy * g
    dot = jnp.sum(dyg * xf, axis=-1, keepdims=True)
    dx = r * dyg - xf * ((r * r * r) * (dot * (1.0 / width)))
    dgain = jnp.sum(dy * (xf * r), axis=0, keepdims=True)
    return dx, dgain


def _rmsnorm_bwd(name, dy, x, gain, width, col_block=0, dres=None, out_dtype=F32):
    s = x.shape[0]
    bm = _row_block(s)
    has_res = dres is not None

    def body(*refs):
        if has_res:
            dy_ref, x_ref, g_ref, r_ref, dx_ref, dg_ref, dxb_ref = refs
        else:
            dy_ref, x_ref, g_ref, dx_ref, dg_ref = refs
        dx, dgain = _rms_bwd_math(dy_ref[...].astype(F32), x_ref[...], g_ref[...], width)
        if has_res:
            dx = dx + r_ref[...]
            dxb_ref[...] = dx.astype(BF16)
        dx_ref[...] = dx.astype(dx_ref.dtype)

        @pl.when(pl.program_id(0) == 0)
        def _():
            dg_ref[...] = dgain

        @pl.when(pl.program_id(0) > 0)
        def _():
            dg_ref[...] += dgain

    row = pl.BlockSpec((bm, width), lambda i: (i, 0))
    in_specs = [row, pl.BlockSpec((bm, width), lambda i: (i, col_block)), pl.BlockSpec((1, width), lambda i: (0, 0))]
    args = [dy, x, gain]
    out_specs = [row, pl.BlockSpec((1, width), lambda i: (0, 0))]
    out_shape = [jax.ShapeDtypeStruct((s, width), out_dtype), jax.ShapeDtypeStruct((1, width), F32)]
    if has_res:
        in_specs.append(row)
        args.append(dres)
        out_specs.append(row)
        out_shape.append(jax.ShapeDtypeStruct((s, width), BF16))
    return _call(body, name=name, grid=(s // bm,), in_specs=in_specs, out_specs=out_specs, out_shape=out_shape,
                 sem=("arbitrary",), args=args)


def _loss_and_final_norm(h, gain, target):
    s, d = h.shape
    bm = _row_block(s, 512)

    def body(h_ref, g_ref, t_ref, dh_ref, dhb_ref, loss_ref, dg_ref):
        xf = h_ref[...]
        g = g_ref[...]
        r = lax.rsqrt(jnp.mean(xf * xf, axis=-1, keepdims=True) + RMS_EPS)
        err = (xf * r) * g - t_ref[...]
        part = 0.5 * jnp.sum(jnp.mean(err * err, axis=-1, keepdims=True), axis=0, keepdims=True)
        dx, dgain = _rms_bwd_math(err * (1.0 / d), xf, g, d)
        dh_ref[...] = dx
        dhb_ref[...] = dx.astype(BF16)

        @pl.when(pl.program_id(0) == 0)
        def _():
            dg_ref[...] = dgain
            loss_ref[...] = jnp.broadcast_to(part, loss_ref.shape)

        @pl.when(pl.program_id(0) > 0)
        def _():
            dg_ref[...] += dgain
            loss_ref[...] += jnp.broadcast_to(part, loss_ref.shape)

    row = pl.BlockSpec((bm, d), lambda i: (i, 0))
    vec = pl.BlockSpec((1, d), lambda i: (0, 0))
    return pl.pallas_call(
        body, name="loss_final_norm", grid=(s // bm,), in_specs=[row, vec, row],
        out_specs=[row, row, pl.BlockSpec((1, 128), lambda i: (0, 0)), vec],
        out_shape=[jax.ShapeDtypeStruct((s, d), F32), jax.ShapeDtypeStruct((s, d), BF16),
                   jax.ShapeDtypeStruct((1, 128), F32),
                   jax.ShapeDtypeStruct((1, d), F32)],
        compiler_params=_params("arbitrary"),
    )(h, gain, target)


def _ffn_up(name, n, wg, wu):
    s = n.shape[0]
    bm = _row_block(s)

    def body(n_ref, wg_ref, wu_ref, a_ref, dadu_ref, dadg_ref):
        x = n_ref[...]
        g = _dot(x, wg_ref[...], NT)
        u = _dot(x, wu_ref[...], NT)
        sg = jax.nn.sigmoid(g)
        silu = g * sg
        a_ref[...] = (silu * u).astype(BF16)
        dadu_ref[...] = silu.astype(BF16)
        dadg_ref[...] = (u * (sg * (1.0 + g * (1.0 - sg)))).astype(BF16)

    w_spec = pl.BlockSpec((None, FF_SHARD, D_MODEL), lambda j, i: (j, 0, 0))
    o_spec = pl.BlockSpec((None, bm, FF_SHARD), lambda j, i: (j, i, 0))
    shp = jax.ShapeDtypeStruct((N_CHIPS, s, FF_SHARD), BF16)
    return _call(
        body, name=name, grid=(N_CHIPS, s // bm),
        in_specs=[pl.BlockSpec((bm, D_MODEL), lambda j, i: (i, 0)), w_spec, w_spec],
        out_specs=[o_spec, o_spec, o_spec], out_shape=[shp, shp, shp],
        sem=("parallel", "parallel"), args=[n, wg, wu])


def _ffn1_up_gather(n, g_sh, u_sh, d_sh):
    s = n.shape[0]
    bm = _row_block(s)
    nrb = s // bm
    rows, cols = g_sh.shape
    rels = (1, 2, 3)

    def body(n_ref, gs, us, ds, a_ref, dadu_ref, dadg_ref, wg, wu, wd, gbuf, ubuf, send, recv, fsend, frecv, loc, ld):
        r, i = pl.program_id(0), pl.program_id(1)
        x, y, c = lax.axis_index("x"), lax.axis_index("y"), lax.axis_index("c")
        sib = (x, y, 1 - c)
        mine, _ = _half_rows(c, rows)
        shards, fulls, bufs = (gs, us, ds), (wg, wu, wd), (gbuf, ubuf)

        def ici(k, rel, dev=sib):
            return pltpu.make_async_remote_copy(
                src_ref=shards[k].at[mine], dst_ref=fulls[k].at[rel, mine], send_sem=send.at[k, rel - 1],
                recv_sem=recv.at[k, rel - 1], device_id=dev, device_id_type=_MESH)

        def peer(rel):
            return ((1 - x) if rel & 2 else x, (1 - y) if rel & 1 else y, c)

        def fwd(k, rel):
            return pltpu.make_async_remote_copy(
                src_ref=fulls[k].at[rel, mine], dst_ref=fulls[k].at[rel, mine], send_sem=fsend.at[k, rel - 1],
                recv_sem=frecv.at[k, rel - 1], device_id=sib, device_id_type=_MESH)

        def own(k):
            return pltpu.make_async_copy(shards[k], fulls[k].at[0], loc.at[k])

        def load(k, src):
            return pltpu.make_async_copy(src, bufs[k], ld.at[k])

        @pl.when(jnp.logical_and(r == 0, i == 0))
        def _():
            for k in range(3):
                own(k).start()
            for rel in (1, 2, 3):
                for k in (0, 1):
                    ici(k, rel, peer(rel)).start()
            for rel in rels:
                ici(2, rel, peer(rel)).start()
            for k in (0, 1):
                load(k, shards[k]).start()
            for k in (0, 1):
                load(k, shards[k]).wait()

        @pl.when(jnp.logical_and(r > 0, i == 0))
        def _():
            for k in (0, 1):
                ici(k, r).wait_recv()
                fwd(k, r).start()
            for k in (0, 1):
                fwd(k, r).wait_recv()
                load(k, fulls[k].at[r]).start()
            for k in (0, 1):
                load(k, fulls[k].at[r]).wait()

        xv = n_ref[...]
        g = _dot(xv, gbuf[...], NT)
        u = _dot(xv, ubuf[...], NT)
        sg = jax.nn.sigmoid(g)
        silu = g * sg
        a_ref[...] = (silu * u).astype(BF16)
        dadu_ref[...] = silu.astype(BF16)
        dadg_ref[...] = (u * (sg * (1.0 + g * (1.0 - sg)))).astype(BF16)

        @pl.when(jnp.logical_and(r == 3, i == nrb - 1))
        def _():
            for rel in rels:
                ici(2, rel).wait_recv()
                fwd(2, rel).start()
            for rel in rels:
                fwd(2, rel).wait_recv()
            for k in range(3):
                for rel in rels:
                    ici(k, rel).wait_send()
                    fwd(k, rel).wait_send()
                own(k).wait()

    o_spec = pl.BlockSpec((None, bm, FF_SHARD), lambda r, i: (r, i, 0))
    act = jax.ShapeDtypeStruct((N_CHIPS, s, FF_SHARD), BF16)
    full = jax.ShapeDtypeStruct((N_CHIPS, rows, cols), BF16)
    dma = pltpu.SemaphoreType.DMA
    return _call(
        body, name="ffn1_up", grid=(N_CHIPS, nrb),
        in_specs=[pl.BlockSpec((bm, D_MODEL), lambda r, i: (i, 0)), _ANY, _ANY, _ANY],
        out_specs=[o_spec, o_spec, o_spec, _ANY, _ANY, _ANY], out_shape=[act, act, act, full, full, full],
        scratch_shapes=[pltpu.VMEM((rows, cols), BF16), pltpu.VMEM((rows, cols), BF16), dma((3, 3)), dma((3, 3)),
                        dma((3, 3)), dma((3, 3)), dma((3,)), dma((2,))],
        sem=("arbitrary", "arbitrary"), args=[n, g_sh, u_sh, d_sh])


def _residual_epilogue(alpha, with_norm):
    if not with_norm:
        return lambda acc, r: (r + alpha * acc,)

    def epilogue(acc, r, g):
        h = r + alpha * acc
        rs = lax.rsqrt(jnp.mean(h * h, axis=-1, keepdims=True) + RMS_EPS)
        return h, (h * rs) * g

    return epilogue


def _residual_outs(s, bm, gain):
    row = pl.BlockSpec((bm, D_MODEL), lambda i, k: (i, 0))
    outs = [(jax.ShapeDtypeStruct((s, D_MODEL), F32), row)]
    if gain is None:
        return [], outs
    return [(gain, pl.BlockSpec((1, D_MODEL), lambda i, k: (0, 0)))], outs + [(jax.ShapeDtypeStruct((s, D_MODEL), BF16), row)]


def _ffn_down(name, a, wd, res, gain=None):
    s = a.shape[1]
    bm = _row_block(s, 512)
    row = pl.BlockSpec((bm, D_MODEL), lambda i, k: (i, 0))
    terms = [(a, pl.BlockSpec((None, bm, FF_SHARD), lambda i, k, j=j: (j, i, 0)),
              wd, pl.BlockSpec((None, FF_SHARD, D_MODEL), lambda i, k, j=j: (j, 0, 0)), NN) for j in range(N_CHIPS)]
    extras, outs = _residual_outs(s, bm, gain)
    res_out = _matmul(name, (s // bm, 1), terms, [(res, row)] + extras, outs,
                      _residual_epilogue(0.5, gain is not None), None)
    return res_out if gain is not None else res_out[0]


def _norm_bwd_epilogue(width):
    def epilogue(acc, h, g, dres):
        dx, dgain = _rms_bwd_math(acc, h, g, width)
        dx = dx + dres
        return dx, dx, dgain

    return epilogue


def _norm_bwd_operands(s, bm, h, gain, dres):
    row = pl.BlockSpec((bm, D_MODEL), lambda i, k: (i, 0))
    vec = pl.BlockSpec((1, D_MODEL), lambda i, k: (0, 0))
    extras = [(h, row), (gain, vec), (dres, row)]
    outs = [(jax.ShapeDtypeStruct((s, D_MODEL), F32), row), (jax.ShapeDtypeStruct((s, D_MODEL), BF16), row),
            (jax.ShapeDtypeStruct((1, D_MODEL), F32), vec)]
    return extras, outs, (2,)


def _ffn_bwd(tag, dh, n, dadg, dadu, a, wg, wu, wd, grads, norm_bwd=None):
    s = dh.shape[0]
    bm = _row_block(s)
    bk = _reduce_block(s)
    nk = s // bk

    def act_bwd(acc, dg_da, du_da):
        da = 0.5 * acc
        return da * dg_da.astype(F32), da * du_da.astype(F32)

    slab = pl.BlockSpec((None, bm, FF_SHARD), lambda j, i, k: (j, i, 0))
    shp = jax.ShapeDtypeStruct((N_CHIPS, s, FF_SHARD), BF16)
    dg, du = _matmul(
        tag + "_dact", (N_CHIPS, s // bm, 1),
        [(dh, pl.BlockSpec((bm, D_MODEL), lambda j, i, k: (i, 0)),
          wd, pl.BlockSpec((None, FF_SHARD, D_MODEL), lambda j, i, k: (j, 0, 0)), NT)],
        [(dadg, slab), (dadu, slab)], [(shp, slab), (shp, slab)], act_bwd, None)

    grads[tag + "_w_down"] = _matmul(
        tag + "_dwd", (N_CHIPS, nk),
        [(a, pl.BlockSpec((None, bk, FF_SHARD), lambda j, k: (j, k, 0)),
          dh, pl.BlockSpec((bk, D_MODEL), lambda j, k: (k, 0)), TN)],
        [], [(jax.ShapeDtypeStruct((N_CHIPS, FF_SHARD, D_MODEL), BF16),
              pl.BlockSpec((None, FF_SHARD, D_MODEL), lambda j, k: (j, 0, 0)))],
        lambda acc: (0.5 * acc,), (FF_SHARD, D_MODEL))[0]

    def dw_up(nm, dact):
        return _matmul(
            nm, (N_CHIPS, nk),
            [(dact, pl.BlockSpec((None, bk, FF_SHARD), lambda j, k: (j, k, 0)),
              n, pl.BlockSpec((bk, D_MODEL), lambda j, k: (k, 0)), TN)],
            [], [(jax.ShapeDtypeStruct((N_CHIPS, FF_SHARD, D_MODEL), BF16),
                  pl.BlockSpec((None, FF_SHARD, D_MODEL), lambda j, k: (j, 0, 0)))],
            _ident, (FF_SHARD, D_MODEL))[0]

    grads[tag + "_w_gate"] = dw_up(tag + "_dwg", dg)
    grads[tag + "_w_up"] = dw_up(tag + "_dwu", du)

    bn = _row_block(s, 512)
    steps = s // bn // 2
    prev, dgain = (), None
    for part, off in (("_dn_a", 0), ("_dn_b", steps)):
        row = pl.BlockSpec((bn, D_MODEL), lambda i, k, off=off: (i + off, 0))
        terms = []
        for j in range(N_CHIPS):
            a_slab = pl.BlockSpec((None, bn, FF_SHARD), lambda i, k, j=j, off=off: (j, i + off, 0))
            w_slab = pl.BlockSpec((None, FF_SHARD, D_MODEL), lambda i, k, j=j: (j, 0, 0))
            terms += [(dg, a_slab, wg, w_slab, NN), (du, a_slab, wu, w_slab, NN)]
        if norm_bwd is None:
            prev = _matmul(tag + part, (steps, 1), terms, [], [(jax.ShapeDtypeStruct((s, D_MODEL), F32), row)],
                           _ident, None, fill=prev)
            continue
        h, gain, dres = norm_bwd
        vec = pl.BlockSpec((1, D_MODEL), lambda i, k: (0, 0))
        res = _matmul(
            tag + part, (steps, 1), terms, [(h, row), (gain, vec), (dres, row)],
            [(jax.ShapeDtypeStruct((s, D_MODEL), F32), row), (jax.ShapeDtypeStruct((s, D_MODEL), BF16), row),
             (jax.ShapeDtypeStruct((1, D_MODEL), F32), vec)],
            _norm_bwd_epilogue(D_MODEL), None, fill=prev, summed=(2,))
        prev = res[:2]
        dgain = res[2] if dgain is None else dgain + res[2]
    return prev[0] if norm_bwd is None else (prev[0], prev[1], dgain)


def _mm_nn(name, a, b, out_dtype, res=None, gain=None):
    s, k = a.shape
    nn = b.shape[1]
    bm = _row_block(s)
    row = pl.BlockSpec((bm, nn), lambda i, kk: (i, 0))
    term = [(a, pl.BlockSpec((bm, k), lambda i, kk: (i, 0)), b, pl.BlockSpec((k, nn), lambda i, kk: (0, 0)), NN)]
    if res is None:
        return _matmul(name, (s // bm, 1), term, [], [(jax.ShapeDtypeStruct((s, nn), out_dtype), row)], _ident, None)[0]
    extras, outs = _residual_outs(s, bm, gain)
    res_out = _matmul(name, (s // bm, 1), term, [(res, row)] + extras, outs,
                      _residual_epilogue(1.0, gain is not None), None)
    return res_out if gain is not None else res_out[0]


def _mm_nt(name, a, b, out_dtype, attn_out=None, nh=0, dv=0):
    s, nn = a.shape
    k = b.shape[0]
    bm = _row_block(s)
    term = [(a, pl.BlockSpec((bm, nn), lambda i, kk: (i, 0)), b, pl.BlockSpec((k, nn), lambda i, kk: (0, 0)), NT)]
    out = (jax.ShapeDtypeStruct((s, k), out_dtype), pl.BlockSpec((bm, k), lambda i, kk: (i, 0)))
    if attn_out is None:
        return _matmul(name, (s // bm, 1), term, [], [out], _ident, None)[0]

    def with_delta(acc, o):
        do = acc.astype(out_dtype).astype(F32)
        cols = [jnp.sum(do[:, h * dv:(h + 1) * dv] * o[:, h * dv:(h + 1) * dv].astype(F32), axis=-1, keepdims=True)
                for h in range(nh)]
        return acc, jnp.stack(cols, axis=0)

    return _matmul(
        name, (s // bm, 1), term, [(attn_out, pl.BlockSpec((bm, nh * dv), lambda i, kk: (i, 0)))],
        [out, (jax.ShapeDtypeStruct((nh, s, 1), F32), pl.BlockSpec((nh, bm, 1), lambda i, kk: (0, i, 0)))],
        with_delta, None)


def _mm_nt_norm_bwd(name, a, b, h, gain, dres):
    s, nn = a.shape
    bm = _row_block(s, 512)
    extras, outs, summed = _norm_bwd_operands(s, bm, h, gain, dres)
    return _matmul(
        name, (s // bm, 1),
        [(a, pl.BlockSpec((bm, nn), lambda i, kk: (i, 0)), b, pl.BlockSpec(b.shape, lambda i, kk: (0, 0)), NT)],
        extras, outs, _norm_bwd_epilogue(D_MODEL), None, summed=summed)


def _w_in_dx_norm_bwd(dz, w_t, h, gain, dres):
    s = dz.shape[0]
    bm = _row_block(s, 512)
    epilogue = _norm_bwd_epilogue(D_MODEL)

    def body(dz_ref, w_ref, h_ref, g_ref, r_ref, dx_ref, dxb_ref, dg_ref):
        dzv = dz_ref[...]
        dn = jnp.concatenate([_dot(dzv, w_ref[j], NN) for j in range(N_CHIPS)], axis=1)
        dx, _, dgain = epilogue(dn, h_ref[...], g_ref[...], r_ref[...])
        dx_ref[...] = dx
        dxb_ref[...] = dx.astype(BF16)

        @pl.when(pl.program_id(0) == 0)
        def _():
            dg_ref[...] = dgain

        @pl.when(pl.program_id(0) > 0)
        def _():
            dg_ref[...] += dgain

    row = pl.BlockSpec((bm, D_MODEL), lambda i: (i, 0))
    vec = pl.BlockSpec((1, D_MODEL), lambda i: (0, 0))
    return _call(
        body, name="w_in_dx", grid=(s // bm,),
        in_specs=[row, pl.BlockSpec(w_t.shape, lambda i: (0, 0, 0)), row, vec, row],
        out_specs=[row, row, vec],
        out_shape=[jax.ShapeDtypeStruct((s, D_MODEL), F32), jax.ShapeDtypeStruct((s, D_MODEL), BF16),
                   jax.ShapeDtypeStruct((1, D_MODEL), F32)],
        sem=("arbitrary",), args=[dz, w_t, h, gain, dres])


def _mm_tn(name, a, b, out_dtype=BF16):
    s, k = a.shape
    nn = b.shape[1]
    bk = _reduce_block(s)
    return _matmul(
        name, (s // bk,),
        [(a, pl.BlockSpec((bk, k), lambda kk: (kk, 0)), b, pl.BlockSpec((bk, nn), lambda kk: (kk, 0)), TN)],
        [], [(jax.ShapeDtypeStruct((k, nn), out_dtype), pl.BlockSpec((k, nn), lambda kk: (0, 0)))],
        _ident, (k, nn))[0]


def _mm_heads_fwd(name, a, w, out_dtype, w_transposed=False):
    s, k = a.shape
    nh = w.shape[0]
    nn = w.shape[1] if w_transposed else w.shape[2]
    bm = _row_block(s)
    return _matmul(
        name, (nh, s // bm, 1),
        [(a, pl.BlockSpec((bm, k), lambda h, i, kk: (i, 0)),
          w, pl.BlockSpec((None,) + w.shape[1:], lambda h, i, kk: (h, 0, 0)), NT if w_transposed else NN)],
        [], [(jax.ShapeDtypeStruct((s, nh * nn), out_dtype), pl.BlockSpec((bm, nn), lambda h, i, kk: (i, h)))],
        _ident, None)[0]


def _mm_heads_bwd(name, dy, a, w, w_transposed=False):
    s, k = a.shape
    nh = w.shape[0]
    nn = w.shape[1] if w_transposed else w.shape[2]
    bm = _row_block(s)
    bk = _reduce_block(s)
    w_spec = pl.BlockSpec((None,) + w.shape[1:], lambda i, h: (h, 0, 0))
    da = _matmul(
        name + "_dx", (s // bm, nh),
        [(dy, pl.BlockSpec((bm, nn), lambda i, h: (i, h)), w, w_spec, NN if w_transposed else NT)],
        [], [(jax.ShapeDtypeStruct((s, k), F32), pl.BlockSpec((bm, k), lambda i, h: (i, 0)))], _ident, (bm, k))[0]
    a_term = (a, pl.BlockSpec((bk, k), lambda h, kk: (kk, 0)))
    dy_term = (dy, pl.BlockSpec((bk, nn), lambda h, kk: (kk, h)))
    lhs, rhs = (dy_term, a_term) if w_transposed else (a_term, dy_term)
    dw = _matmul(
        name + "_dw", (nh, s // bk), [lhs + rhs + (TN,)],
        [], [(jax.ShapeDtypeStruct(w.shape, BF16), pl.BlockSpec((None,) + w.shape[1:], lambda h, kk: (h, 0, 0)))],
        _ident, w.shape[1:])[0]
    return da, dw


def _w_in_fwd(n, w_t):
    s = n.shape[0]
    bm = _row_block(s)
    nh, nout, kin = w_t.shape
    terms = [(n, pl.BlockSpec((bm, kin), lambda i, k, j=j: (i, j)),
              w_t, pl.BlockSpec((None, nout, kin), lambda i, k, j=j: (j, 0, 0)), NT) for j in range(nh)]
    row = pl.BlockSpec((bm, nout), lambda i, k: (i, 0))
    return _matmul("w_in", (s // bm, 1), terms, [], [(jax.ShapeDtypeStruct((s, nout), F32), row)], _ident, None)[0]


def _w_in_dw(dz, n):
    s, nout = dz.shape
    kin = n.shape[1] // N_CHIPS
    bk = _reduce_block(s)
    return _matmul(
        "w_in_dw", (N_CHIPS, s // bk),
        [(dz, pl.BlockSpec((bk, nout), lambda j, k: (k, 0)), n, pl.BlockSpec((bk, kin), lambda j, k: (k, j)), TN)],
        [], [(jax.ShapeDtypeStruct((N_CHIPS, nout, kin), BF16), pl.BlockSpec((None, nout, kin), lambda j, k: (j, 0, 0)))],
        _ident, (nout, kin))[0]


def _rope_tables(positions):
    half = ROPE_DIM // 2
    freqs = 1.0 / (ROPE_BASE ** (jnp.arange(0, ROPE_DIM, 2, dtype=F32) / ROPE_DIM))
    ang = positions.astype(F32)[:, None] * freqs
    cos, sin = jnp.cos(ang), jnp.sin(ang)
    z = jnp.zeros_like(cos)
    tc = jnp.concatenate([cos, cos, z, z], axis=-1)
    ta = jnp.concatenate([-sin, z, z, z], axis=-1)
    tb = jnp.concatenate([z, sin, z, z], axis=-1)
    assert tc.shape[-1] == 4 * half
    return tc, ta, tb


def _rope(x, tc, ta, tb):
    return x * tc + pltpu.roll(x, 96, 1) * ta + pltpu.roll(x, 32, 1) * tb


def _rope_t(dy, tc, ta, tb):
    return dy * tc + pltpu.roll(dy * ta, 32, 1) + pltpu.roll(dy * tb, 96, 1)


def _q_rope(q, tc, ta, tb, transpose):
    s = q.shape[0]
    bm = _row_block(s, 512)
    rot = _rope_t if transpose else _rope

    def body(q_ref, tc_ref, ta_ref, tb_ref, o_ref):
        c, a, b = tc_ref[...], ta_ref[...], tb_ref[...]
        for h in range(MLA_HEADS):
            lo = h * HEAD_QK
            o_ref[:, lo:lo + 128] = q_ref[:, lo:lo + 128].astype(BF16)
            o_ref[:, lo + 128:lo + 256] = rot(q_ref[:, lo + 128:lo + 256], c, a, b).astype(BF16)

    row = pl.BlockSpec((bm, MLA_HEADS * HEAD_QK), lambda i: (i, 0))
    tab = pl.BlockSpec((bm, 128), lambda i: (i, 0))
    return pl.pallas_call(
        body, name="q_rope_t" if transpose else "q_rope", grid=(s // bm,), in_specs=[row, tab, tab, tab],
        out_specs=row, out_shape=jax.ShapeDtypeStruct((s, MLA_HEADS * HEAD_QK), BF16),
        compiler_params=_params("parallel"),
    )(q, tc, ta, tb)


def _kv_assemble(kv, z, tc, ta, tb):
    s = kv.shape[0]
    bm = _row_block(s, 512)

    def body(kv_ref, kr_ref, tc_ref, ta_ref, tb_ref, k_ref, v_ref):
        kpe = _rope(kr_ref[...], tc_ref[...], ta_ref[...], tb_ref[...]).astype(BF16)
        for h in range(MLA_HEADS):
            lo = h * 256
            k_ref[:, lo:lo + 128] = kv_ref[:, lo:lo + 128].astype(BF16)
            k_ref[:, lo + 128:lo + 256] = kpe
            v_ref[:, h * 128:(h + 1) * 128] = kv_ref[:, lo + 128:lo + 256].astype(BF16)

    row = pl.BlockSpec((bm, 1024), lambda i: (i, 0))
    tab = pl.BlockSpec((bm, 128), lambda i: (i, 0))
    return pl.pallas_call(
        body, name="kv_assemble", grid=(s // bm,),
        in_specs=[row, pl.BlockSpec((bm, 128), lambda i: (i, 3)), tab, tab, tab],
        out_specs=[row, pl.BlockSpec((bm, 512), lambda i: (i, 0))],
        out_shape=[jax.ShapeDtypeStruct((s, 1024), BF16), jax.ShapeDtypeStruct((s, 512), BF16)],
        compiler_params=_params("parallel"),
    )(kv, z, tc, ta, tb)


def _kv_assemble_bwd(dk, dv, tc, ta, tb):
    s = dk.shape[0]
    bm = _row_block(s, 512)

    def body(dk_ref, dv_ref, tc_ref, ta_ref, tb_ref, dkv_ref, dkr_ref):
        dpe = None
        for h in range(MLA_HEADS):
            lo = h * 256
            dkv_ref[:, lo:lo + 128] = dk_ref[:, lo:lo + 128].astype(BF16)
            dkv_ref[:, lo + 128:lo + 256] = dv_ref[:, h * 128:(h + 1) * 128].astype(BF16)
            t = dk_ref[:, lo + 128:lo + 256]
            dpe = t if dpe is None else dpe + t
        dkr_ref[...] = _rope_t(dpe, tc_ref[...], ta_ref[...], tb_ref[...])

    row = pl.BlockSpec((bm, 1024), lambda i: (i, 0))
    tab = pl.BlockSpec((bm, 128), lambda i: (i, 0))
    return pl.pallas_call(
        body, name="kv_assemble_bwd", grid=(s // bm,),
        in_specs=[row, pl.BlockSpec((bm, 512), lambda i: (i, 0)), tab, tab, tab],
        out_specs=[row, tab],
        out_shape=[jax.ShapeDtypeStruct((s, 1024), BF16), jax.ShapeDtypeStruct((s, 128), F32)],
        compiler_params=_params("parallel"),
    )(dk, dv, tc, ta, tb)


def _norm_bf16(x, g):
    r = lax.rsqrt(jnp.mean(x * x, axis=-1, keepdims=True) + RMS_EPS)
    return ((x * r) * g).astype(BF16)


def _qkv_prep(z, q_gain, kv_gain, wq_t, wkv, tc, ta, tb):
    s = z.shape[0]
    bm = _row_block(s, 512)

    def body(zq_ref, zkv_ref, zkr_ref, qg_ref, kvg_ref, wq_ref, wkv_ref, tc_ref, ta_ref, tb_ref,
             qn_ref, kvn_ref, q_ref, k_ref, v_ref):
        c, a, b = tc_ref[...], ta_ref[...], tb_ref[...]
        qn = _norm_bf16(zq_ref[...], qg_ref[...])
        kvn = _norm_bf16(zkv_ref[...], kvg_ref[...])
        qn_ref[...] = qn
        kvn_ref[...] = kvn
        kpe = _rope(zkr_ref[...], c, a, b).astype(BF16)
        for h in range(MLA_HEADS):
            lo = h * HEAD_QK
            qp = _dot(qn, wq_ref[h], NT)
            q_ref[:, lo:lo + 128] = qp[:, :128].astype(BF16)
            q_ref[:, lo + 128:lo + 256] = _rope(qp[:, 128:], c, a, b).astype(BF16)
            kv = _dot(kvn, wkv_ref[h], NN)
            k_ref[:, lo:lo + 128] = kv[:, :128].astype(BF16)
            k_ref[:, lo + 128:lo + 256] = kpe
            v_ref[:, h * HEAD_V:(h + 1) * HEAD_V] = kv[:, 128:].astype(BF16)

    def cols(width, blk):
        return pl.BlockSpec((bm, width), lambda i: (i, blk))

    def whole(a):
        return pl.BlockSpec(a.shape, lambda i: (0,) * a.ndim)

    tab = cols(128, 0)
    return pl.pallas_call(
        body, name="qkv_prep", grid=(s // bm,),
        in_specs=[cols(Q_LORA, 0), cols(KV_LORA, 2), cols(128, 3), whole(q_gain), whole(kv_gain), whole(wq_t),
                  whole(wkv), tab, tab, tab],
        out_specs=[cols(Q_LORA, 0), cols(KV_LORA, 0), cols(1024, 0), cols(1024, 0), cols(512, 0)],
        out_shape=[jax.ShapeDtypeStruct((s, Q_LORA), BF16), jax.ShapeDtypeStruct((s, KV_LORA), BF16),
                   jax.ShapeDtypeStruct((s, 1024), BF16), jax.ShapeDtypeStruct((s, 1024), BF16),
                   jax.ShapeDtypeStruct((s, 512), BF16)],
        compiler_params=_params("parallel"),
    )(z, z, z, q_gain, kv_gain, wq_t, wkv, tc, ta, tb)


def _qkv_prep_bwd(dq, dk, dv, z, qn, kvn, q_gain, kv_gain, wq_t, wkv, tc, ta, tb):
    s = z.shape[0]
    bm = _row_block(s, 512)
    nsteps = s // bm

    def body(dq_ref, dk_ref, dv_ref, zq_ref, zkv_ref, qn_ref, kvn_ref, qg_ref, kvg_ref, wq_ref, wkv_ref,
             tc_ref, ta_ref, tb_ref, dz_ref, dqg_ref, dkvg_ref, dwq_ref, dwkv_ref, wq_acc, wkv_acc):
        i = pl.program_id(0)
        c, a, b = tc_ref[...], ta_ref[...], tb_ref[...]

        @pl.when(i == 0)
        def _():
            wq_acc[...] = jnp.zeros_like(wq_acc)
            wkv_acc[...] = jnp.zeros_like(wkv_acc)

        qn, kvn = qn_ref[...], kvn_ref[...]
        dqn = jnp.zeros((bm, Q_LORA), F32)
        dkvn = jnp.zeros((bm, KV_LORA), F32)
        dpe = jnp.zeros((bm, 128), F32)
        for h in range(MLA_HEADS):
            lo = h * HEAD_QK
            dqp = jnp.concatenate([dq_ref[:, lo:lo + 128].astype(BF16),
                                   _rope_t(dq_ref[:, lo + 128:lo + 256], c, a, b).astype(BF16)], axis=1)
            dqn = dqn + _dot(dqp, wq_ref[h], NN)
            wq_acc[h] += _dot(dqp, qn, TN)
            dkv = jnp.concatenate([dk_ref[:, lo:lo + 128].astype(BF16),
                                   dv_ref[:, h * HEAD_V:(h + 1) * HEAD_V].astype(BF16)], axis=1)
            dkvn = dkvn + _dot(dkv, wkv_ref[h], NT)
            wkv_acc[h] += _dot(kvn, dkv, TN)
            dpe = dpe + dk_ref[:, lo + 128:lo + 256]
        dcq, dqg = _rms_bwd_math(dqn, zq_ref[...], qg_ref[...], Q_LORA)
        dckv, dkvg = _rms_bwd_math(dkvn, zkv_ref[...], kvg_ref[...], KV_LORA)
        dz_ref[:, 0:Q_LORA] = dcq.astype(BF16)
        dz_ref[:, Q_LORA:Q_LORA + KV_LORA] = dckv.astype(BF16)
        dz_ref[:, Q_LORA + KV_LORA:512] = _rope_t(dpe, c, a, b).astype(BF16)

        @pl.when(i == 0)
        def _():
            dqg_ref[...] = dqg
            dkvg_ref[...] = dkvg

        @pl.when(i > 0)
        def _():
            dqg_ref[...] += dqg
            dkvg_ref[...] += dkvg

        @pl.when(i == nsteps - 1)
        def _():
            dwq_ref[...] = wq_acc[...].astype(BF16)
            dwkv_ref[...] = wkv_acc[...].astype(BF16)

    def cols(width, blk):
        return pl.BlockSpec((bm, width), lambda i: (i, blk))

    def whole(shape):
        return pl.BlockSpec(shape, lambda i: (0,) * len(shape))

    tab = cols(128, 0)
    return _call(
        body, name="qkv_prep_bwd", grid=(nsteps,),
        in_specs=[cols(1024, 0), cols(1024, 0), cols(512, 0), cols(Q_LORA, 0), cols(KV_LORA, 2), cols(Q_LORA, 0),
                  cols(KV_LORA, 0), whole(q_gain.shape), whole(kv_gain.shape), whole(wq_t.shape), whole(wkv.shape),
                  tab, tab, tab],
        out_specs=[cols(512, 0), whole(q_gain.shape), whole(kv_gain.shape), whole(wq_t.shape), whole(wkv.shape)],
        out_shape=[jax.ShapeDtypeStruct((s, 512), BF16), jax.ShapeDtypeStruct(q_gain.shape, F32),
                   jax.ShapeDtypeStruct(kv_gain.shape, F32), jax.ShapeDtypeStruct(wq_t.shape, BF16),
                   jax.ShapeDtypeStruct(wkv.shape, BF16)],
        scratch_shapes=[pltpu.VMEM(wq_t.shape, F32), pltpu.VMEM(wkv.shape, F32)],
        sem=("arbitrary",), args=[dq, dk, dv, z, z, qn, kvn, q_gain, kv_gain, wq_t, wkv, tc, ta, tb])


def _causal_mask(s, row0, col0):
    rows = row0 + lax.broadcasted_iota(jnp.int32, s.shape, 0)
    cols = col0 + lax.broadcasted_iota(jnp.int32, s.shape, 1)
    return jnp.where(cols <= rows, s, -jnp.inf)


def _attn_fwd(name, q, k, k_off, v, v_off, nh, dq, dv, scale, causal, blk):
    sq, sk = q.shape[0], k.shape[0]
    bq = min(blk, sq)
    bk = min(blk, sk)
    nkv = sk // bk
    assert not causal or (sq == sk and bq == bk)

    hq = bq
    log2e = 1.4426950408889634
    c2 = scale * log2e

    def body(q_ref, k_ref, v_ref, o_ref, lse_ref):
        qi = pl.program_id(1)
        qs = (q_ref[...],)

        def step(j, carry, masked):
            rows = pl.ds(pl.multiple_of(j * bk, bk), bk)
            kb, vb = k_ref[rows, :], v_ref[rows, :]
            out = []
            for t, (m, l, acc) in enumerate(carry):
                s = _dot(qs[t], kb, NT) * c2
                if masked:
                    s = _causal_mask(s, qi * bq + t * hq, j * bk)
                m_new = jnp.maximum(m, jnp.max(s, axis=-1, keepdims=True))
                alpha = jnp.exp2(m - m_new)
                p = jnp.exp2(s - m_new)
                l = alpha * l + jnp.sum(p, axis=-1, keepdims=True)
                acc = alpha * acc + _dot(p, vb, NN)
                out.append((m_new, l, acc))
            return tuple(out)

        one = (jnp.full((hq, 1), -jnp.inf, F32), jnp.zeros((hq, 1), F32), jnp.zeros((hq, dv), F32))
        init = (one,)
        if causal:
            carry = lax.fori_loop(0, qi, lambda j, c: step(j, c, False), init)
            fin = step(qi, carry, True)
        else:
            fin = lax.fori_loop(0, nkv, lambda j, c: step(j, c, False), init)
        for t, (m, l, acc) in enumerate(fin):
            o_ref[t * hq:(t + 1) * hq, :] = (acc / l).astype(o_ref.dtype)
            lse_ref[t * hq:(t + 1) * hq, :] = m * (1.0 / log2e) + jnp.log(l)

    return _call(
        body, name=name, grid=(nh, sq // bq),
        in_specs=[pl.BlockSpec((bq, dq), lambda h, i: (i, h)),
                  pl.BlockSpec((sk, dq), lambda h, i: (0, k_off + h)),
                  pl.BlockSpec((sk, dv), lambda h, i: (0, v_off + h))],
        out_specs=[pl.BlockSpec((bq, dv), lambda h, i: (i, h)), pl.BlockSpec((None, bq, 1), lambda h, i: (h, i, 0))],
        out_shape=[jax.ShapeDtypeStruct((sq, nh * dv), BF16), jax.ShapeDtypeStruct((nh, sq, 1), F32)],
        sem=("parallel", "parallel"), args=[q, k, v])


def _attn_delta(name, do, do_off, o, nh, dv):
    s = o.shape[0]
    bm = _row_block(s, 512)

    def body(do_ref, o_ref, d_ref):
        d_ref[...] = jnp.sum(do_ref[...].astype(F32) * o_ref[...].astype(F32), axis=-1, keepdims=True)

    return pl.pallas_call(
        body, name=name, grid=(nh, s // bm),
        in_specs=[pl.BlockSpec((bm, dv), lambda h, i: (i, do_off + h)), pl.BlockSpec((bm, dv), lambda h, i: (i, h))],
        out_specs=pl.BlockSpec((None, bm, 1), lambda h, i: (h, i, 0)),
        out_shape=jax.ShapeDtypeStruct((nh, s, 1), F32),
        compiler_params=_params("parallel", "parallel"),
    )(do, o)


def _attn_bwd(name, q, k, k_off, v, v_off, do, do_off, lse, delta, nh, dq, dv, scale, causal, blk):
    sq, sk = q.shape[0], k.shape[0]
    bq = min(blk, sq)
    bk = min(blk, sk)
    nq = sq // bq
    assert not causal or (sq == sk and bq == bk)

    def body(q_ref, k_ref, v_ref, do_ref, lse_ref, dl_ref, dq_ref, dk_ref, dv_ref, dk_acc, dv_acc):
        j = pl.program_id(1)

        @pl.when(j == 0)
        def _():
            dq_ref[...] = jnp.zeros_like(dq_ref)

        dk_acc[...] = jnp.zeros_like(dk_acc)
        dv_acc[...] = jnp.zeros_like(dv_acc)
        kv = k_ref[...]
        vv = v_ref[...]

        def step(i, masked):
            rows = pl.ds(pl.multiple_of(i * bq, bq), bq)
            qv = q_ref[rows, :]
            dov = do_ref[rows, :].astype(BF16)
            s = _dot(qv, kv, NT) * scale
            if masked:
                s = _causal_mask(s, i * bq, j * bk)
            p = jnp.exp(s - lse_ref[rows, :])
            dp = _dot(dov, vv, NT)
            ds = (p * (dp - dl_ref[rows, :]) * scale).astype(BF16)
            dv_acc[...] += _dot(p, dov, TN)
            dk_acc[...] += _dot(ds, qv, TN)
            dq_ref[rows, :] += _dot(ds, kv, NN)

        if causal:
            step(j, True)

            def loop(i, c):
                step(i, False)
                return c

            lax.fori_loop(j + 1, nq, loop, 0)
        else:
            def loop(i, c):
                step(i, False)
                return c

            lax.fori_loop(0, nq, loop, 0)
        dk_ref[...] = dk_acc[...]
        dv_ref[...] = dv_acc[...]

    stat = pl.BlockSpec((None, sq, 1), lambda h, j: (h, 0, 0))
    return _call(
        body, name=name, grid=(nh, sk // bk),
        in_specs=[pl.BlockSpec((sq, dq), lambda h, j: (0, h)),
                  pl.BlockSpec((bk, dq), lambda h, j: (j, k_off + h)),
                  pl.BlockSpec((bk, dv), lambda h, j: (j, v_off + h)),
                  pl.BlockSpec((sq, dv), lambda h, j: (0, do_off + h)), stat, stat],
        out_specs=[pl.BlockSpec((sq, dq), lambda h, j: (0, h)),
                   pl.BlockSpec((bk, dq), lambda h, j: (j, h)),
                   pl.BlockSpec((bk, dv), lambda h, j: (j, h))],
        out_shape=[jax.ShapeDtypeStruct((sq, nh * dq), F32), jax.ShapeDtypeStruct((sk, nh * dq), F32),
                   jax.ShapeDtypeStruct((sk, nh * dv), F32)],
        scratch_shapes=[pltpu.VMEM((bk, dq), F32), pltpu.VMEM((bk, dv), F32)],
        sem=("parallel", "arbitrary"), args=[q, k, v, do, lse, delta])


def _pool_diff(z, g):
    s = z.shape[0]
    t = lax.broadcasted_iota(jnp.int32, z.shape, 0)
    acc = z
    sums = []
    for k in (1, 2, 4, 8):
        acc = acc + jnp.where(t >= k, pltpu.roll(acc, k, 0), 0.0)
        sums.append(acc)
    win = jnp.where(g == 0, sums[0], jnp.where(g == 1, sums[1], jnp.where(g == 2, sums[2], sums[3])))
    w = lax.shift_left(jnp.int32(2), g)
    count = jnp.minimum(t + 1, w).astype(F32)
    del s
    return win / count - z, count


def _pool_fwd(z, pool_w, pool_scale):
    s = z.shape[0]

    def body(z_ref, w_ref, sc_ref, o_ref):
        diff, _ = _pool_diff(z_ref[...], pl.program_id(0))
        o_ref[...] = (_dot(diff, w_ref[...], NN) * sc_ref[...]).astype(o_ref.dtype)

    return _call(
        body, name="pool_fwd", grid=(POOL_GROUPS,),
        in_specs=[pl.BlockSpec((s, POOL_CH), lambda g: (0, 4 + g)),
                  pl.BlockSpec((None, POOL_CH, POOL_CH), lambda g: (g, 0, 0)),
                  pl.BlockSpec((1, POOL_CH), lambda g: (0, g))],
        out_specs=[pl.BlockSpec((s, POOL_CH), lambda g: (0, g))],
        out_shape=[jax.ShapeDtypeStruct((s, POOL_GROUPS * POOL_CH), BF16)],
        sem=("parallel",), args=[z, pool_w, pool_scale])[0]


def _pool_bwd(dcat, z, pool_w, pool_scale):
    s = z.shape[0]

    def body(dp_ref, z_ref, w_ref, sc_ref, dz_ref, dw_ref, dsc_ref):
        g = pl.program_id(0)
        diff, count = _pool_diff(z_ref[...], g)
        dpf = dp_ref[...].astype(F32)
        u = _dot(diff, w_ref[...], NN)
        dsc_ref[...] = jnp.sum(dpf * u, axis=0, keepdims=True)
        du = (dpf * sc_ref[...]).astype(BF16)
        dw_ref[...] = _dot(diff, du, TN)
        ddiff = _dot(du, w_ref[...], NT)
        t = lax.broadcasted_iota(jnp.int32, ddiff.shape, 0)
        acc = ddiff / count
        sums = []
        for k in (1, 2, 4, 8):
            acc = acc + jnp.where(t < s - k, pltpu.roll(acc, s - k, 0), 0.0)
            sums.append(acc)
        win = jnp.where(g == 0, sums[0], jnp.where(g == 1, sums[1], jnp.where(g == 2, sums[2], sums[3])))
        dz_ref[...] = win - ddiff

    return pl.pallas_call(
        body, name="pool_bwd", grid=(POOL_GROUPS,),
        in_specs=[pl.BlockSpec((s, POOL_CH), lambda g: (0, 4 + g)),
                  pl.BlockSpec((s, POOL_CH), lambda g: (0, 4 + g)),
                  pl.BlockSpec((None, POOL_CH, POOL_CH), lambda g: (g, 0, 0)),
                  pl.BlockSpec((1, POOL_CH), lambda g: (0, g))],
        out_specs=[pl.BlockSpec((s, POOL_CH), lambda g: (0, g)),
                   pl.BlockSpec((None, POOL_CH, POOL_CH), lambda g: (g, 0, 0)),
                   pl.BlockSpec((1, POOL_CH), lambda g: (0, g))],
        out_shape=[jax.ShapeDtypeStruct((s, POOL_GROUPS * POOL_CH), F32),
                   jax.ShapeDtypeStruct((POOL_GROUPS, POOL_CH, POOL_CH), F32),
                   jax.ShapeDtypeStruct((1, POOL_GROUPS * POOL_CH), F32)],
        compiler_params=_params("parallel"),
    )(dcat, z, pool_w, pool_scale)


def _local_step(x, mem, positions, target, w, grads):
    tc, ta, tb = _rope_tables(positions)
    blk = _ATT_BLOCK

    n1 = _rmsnorm_fwd("ffn1_norm", x, w["ffn1_norm"], D_MODEL)
    if "ffn1_shards" in w:
        a1, dadu1, dadg1, w["ffn1_w_gate"], w["ffn1_w_up"], w["ffn1_w_down"] = _ffn1_up_gather(n1, *w["ffn1_shards"])
    else:
        a1, dadu1, dadg1 = _ffn_up("ffn1_up", n1, w["ffn1_w_gate"], w["ffn1_w_up"])
    h1, n2 = _ffn_down("ffn1_down", a1, w["ffn1_w_down"], x, w["mix_norm"])
    z = _w_in_fwd(n2, w["w_in"])
    qn, kvn, qf, kf, vf = _qkv_prep(z, w["q_norm"], w["kv_norm"], w["w_q_up"], w["w_kv_up"], tc, ta, tb)
    att, lse = _attn_fwd("mla_fwd", qf, kf, 0, vf, 0, MLA_HEADS, HEAD_QK, HEAD_V, MLA_SCALE, True, blk)
    pool = _pool_fwd(z, w["pool_w"], w["pool_scale"])
    s = x.shape[0]
    bm = _row_block(s)
    row = pl.BlockSpec((bm, D_MODEL), lambda i, k: (i, 0))
    half = pl.BlockSpec((bm, 512), lambda i, k: (i, 0))
    h2, n3 = _matmul(
        "w_out", (s // bm, 1),
        [(att, half, w["w_out"], pl.BlockSpec((512, D_MODEL), lambda i, k: (0, 0)), NN),
         (pool, half, w["w_out"], pl.BlockSpec((512, D_MODEL), lambda i, k: (1, 0)), NN)],
        [(h1, row)] + _residual_outs(s, bm, w["xattn_norm"])[0], _residual_outs(s, bm, w["xattn_norm"])[1],
        _residual_epilogue(1.0, True), None)
    memn = _rmsnorm_fwd("mem_norm", mem, w["mem_norm"], D_MODEL)
    qm = _mm_nn("w_mq", n3, w["w_mq"], BF16)
    kvm = _mm_heads_fwd("w_mkv", memn, w["w_mkv"], BF16)
    om, lse_m = _attn_fwd("xattn_fwd", qm, kvm, 0, kvm, MEM_HEADS, MEM_HEADS, MEM_HEAD_DIM, MEM_HEAD_DIM,
                          MEM_SCALE, False, blk)
    h3, n4 = _mm_nn("w_mo", om, w["w_mo"], F32, res=h2, gain=w["ffn2_norm"])
    a2, dadu2, dadg2 = _ffn_up("ffn2_up", n4, w["ffn2_w_gate"], w["ffn2_w_up"])
    h4 = _ffn_down("ffn2_down", a2, w["ffn2_w_down"], h3)

    dh4, dh4b, loss_vec, d_final = _loss_and_final_norm(h4, w["final_norm"], target)
    grads["final_norm"] = d_final

    dh3, dh3b, grads["ffn2_norm"] = _ffn_bwd("ffn2", dh4b, n4, dadg2, dadu2, a2, w["ffn2_w_gate"], w["ffn2_w_up"],
                                             w["ffn2_w_down"], grads, norm_bwd=(h3, w["ffn2_norm"], dh4))

    dom, delta_m = _mm_nt("w_mo_dx", dh3b, w["w_mo"], BF16, attn_out=om, nh=MEM_HEADS, dv=MEM_HEAD_DIM)
    grads["w_mo"] = _mm_tn("w_mo_dw", om, dh3b)
    dqm, dkm, dvm = _attn_bwd("xattn_bwd", qm, kvm, 0, kvm, MEM_HEADS, dom, 0, lse_m, delta_m, MEM_HEADS,
                              MEM_HEAD_DIM, MEM_HEAD_DIM, MEM_SCALE, False, blk)
    dkvm = jnp.concatenate([dkm, dvm], axis=1).astype(BF16)
    dh2, dh2b, grads["xattn_norm"] = _mm_nt_norm_bwd("w_mq_dx", dqm, w["w_mq"], h2, w["xattn_norm"], dh3)
    grads["w_mq"] = _mm_tn("w_mq_dw", n3, dqm)
    dmemn, grads["w_mkv"] = _mm_heads_bwd("w_mkv", dkvm, memn, w["w_mkv"])
    _, grads["mem_norm"] = _rmsnorm_bwd("mem_norm_bwd", dmemn, mem, w["mem_norm"], D_MODEL, out_dtype=BF16)

    dcat, delta = _mm_nt("w_out_dx", dh2b, w["w_out"], BF16, attn_out=att, nh=MLA_HEADS, dv=HEAD_V)
    grads["w_out"] = jnp.concatenate([_mm_tn("w_out_dw_a", att, dh2b), _mm_tn("w_out_dw_p", pool, dh2b)], axis=0)
    dzp, grads["pool_w"], grads["pool_scale"] = _pool_bwd(dcat, z, w["pool_w"], w["pool_scale"])
    dqf, dkf, dvf = _attn_bwd("mla_bwd", qf, kf, 0, vf, 0, dcat, 0, lse, delta, MLA_HEADS, HEAD_QK, HEAD_V,
                              MLA_SCALE, True, blk)
    dz_lat, grads["q_norm"], grads["kv_norm"], grads["w_q_up"], grads["w_kv_up"] = _qkv_prep_bwd(
        dqf, dkf, dvf, z, qn, kvn, w["q_norm"], w["kv_norm"], w["w_q_up"], w["w_kv_up"], tc, ta, tb)
    dz = jnp.concatenate([dz_lat, dzp.astype(BF16)], axis=1)
    grads["w_in"] = _w_in_dw(dz, n2)
    dh1, dh1b, grads["mix_norm"] = _w_in_dx_norm_bwd(dz, w["w_in"], h1, w["mix_norm"], dh2)

    dn1 = _ffn_bwd("ffn1", dh1b, n1, dadg1, dadu1, a1, w["ffn1_w_gate"], w["ffn1_w_up"], w["ffn1_w_down"], grads)
    dx, grads["ffn1_norm"], _ = _rmsnorm_bwd("ffn1_norm_bwd", dn1, x, w["ffn1_norm"], D_MODEL, dres=dh1)
    return loss_vec[0, 0], dx


def _mesh_pos():
    x, y, c = lax.axis_index("x"), lax.axis_index("y"), lax.axis_index("c")
    chips = [(1 - x, y), (x, 1 - y), (1 - x, 1 - y)]
    chip_ids = [2 * cx + cy for cx, cy in chips]
    return x, y, c, 2 * x + y, chips, chip_ids


def _half_rows(c, rows):
    hr = rows // 2
    return pl.ds(pl.multiple_of(c * hr, 16), hr), pl.ds(pl.multiple_of((1 - c) * hr, 16), hr)


def _ag_ici_stage(shards):
    n = len(shards)

    def copies(ins, outs):
        x, y, c, me, chips, _ = _mesh_pos()
        out = []
        for k in range(n):
            mine, _ = _half_rows(c, ins[k].shape[0])
            out.append((ins[k], outs[k].at[me], None))
            for cx, cy in chips:
                out.append((ins[k].at[mine], outs[k].at[me, mine], (cx, cy, c)))
        return out

    return _Stage(shards, [jax.ShapeDtypeStruct((N_CHIPS,) + s.shape, s.dtype) for s in shards], 3 * n, n, copies)


def _ag_d2d_stage(fulls):
    n = len(fulls)

    def copies(ins, outs):
        x, y, c, me, _, chip_ids = _mesh_pos()
        out = []
        for k in range(n):
            mine, _ = _half_rows(c, ins[k].shape[1])
            for j in range(3):
                out.append((ins[k].at[chip_ids[j], mine], outs[k].at[chip_ids[j], mine], (x, y, 1 - c)))
        return out

    return _Stage(fulls, [jax.ShapeDtypeStruct(f.shape, f.dtype) for f in fulls], 3 * n, 0, copies,
                  aliases={k: k for k in range(n)})


def _rs_swap_stage(grads):
    n = len(grads)

    def copies(ins, outs):
        x, y, c, _, _, _ = _mesh_pos()
        out = []
        for k in range(n):
            _, other = _half_rows(c, ins[k].shape[1])
            out.append((ins[k].at[:, other, :], outs[k], (x, y, 1 - c)))
        return out

    return _Stage(grads, [jax.ShapeDtypeStruct((N_CHIPS, g.shape[1] // 2, g.shape[2]), g.dtype) for g in grads],
                  n, 0, copies)


_REL_OF_PEER = (2, 1, 3)


def _rs_scatter_stage(sums, relative=False):
    n = len(sums)

    def copies(ins, outs):
        x, y, c, me, chips, chip_ids = _mesh_pos()
        out = []
        for k in range(n):
            mine, _ = _half_rows(c, 2 * ins[k].shape[1])
            out.append((ins[k].at[0 if relative else me], outs[k].at[0, mine, :], None))
            for j, (cx, cy) in enumerate(chips):
                slab = _REL_OF_PEER[j] if relative else chip_ids[j]
                out.append((ins[k].at[slab], outs[k].at[1 + j, mine, :], (cx, cy, c)))
        return out

    return _Stage(sums, [jax.ShapeDtypeStruct((N_CHIPS, 2 * s.shape[1], s.shape[2]), s.dtype) for s in sums],
                  3 * n, n, copies)


def _rs_mirror_stage(parts):
    n = len(parts)

    def copies(ins, outs):
        x, y, c, _, _, _ = _mesh_pos()
        out = []
        for k in range(n):
            mine, _ = _half_rows(c, ins[k].shape[1])
            out.append((ins[k].at[:, mine, :], outs[k].at[:, mine, :], (x, y, 1 - c)))
        return out

    return _Stage(parts, [jax.ShapeDtypeStruct(p.shape, p.dtype) for p in parts], n, 0, copies,
                  aliases={k: k for k in range(n)})


def _pair_add(name, g, r1, core):
    _, rows, cols = g.shape
    hr = rows // 2

    def body(c_ref, g_ref, r_ref, o_ref):
        o_ref[...] = (g_ref[...].astype(F32) + r_ref[...].astype(F32)).astype(BF16)

    half = pl.BlockSpec((None, hr, cols), lambda j, c: (j, 0, 0))
    return pl.pallas_call(
        body, name=name,
        grid_spec=pltpu.PrefetchScalarGridSpec(
            num_scalar_prefetch=1, grid=(N_CHIPS,),
            in_specs=[pl.BlockSpec((None, hr, cols), lambda j, c: (j, c[0], 0)), half], out_specs=half),
        out_shape=jax.ShapeDtypeStruct((N_CHIPS, hr, cols), BF16),
        compiler_params=_params("parallel"),
    )(core, g, r1)


def _all_gather_weights(shards):
    n = len(shards)

    def body(*refs):
        ins, outs = refs[:n], refs[n:2 * n]
        send, recv, loc = refs[2 * n:]
        x, y, c, me, chips, chip_ids = _mesh_pos()
        sib = (x, y, 1 - c)

        def halves(k):
            hr = ins[k].shape[0] // 2
            return pl.ds(pl.multiple_of(c * hr, 16), hr), pl.ds(pl.multiple_of((1 - c) * hr, 16), hr)

        def remote(src, dst, k, j, dev):
            return pltpu.make_async_remote_copy(src_ref=src, dst_ref=dst, send_sem=send.at[k, j],
                                                recv_sem=recv.at[k, j], device_id=dev, device_id_type=_MESH)

        started = []
        local = []
        for k in range(n):
            mine, _ = halves(k)
            cp = pltpu.make_async_copy(ins[k], outs[k].at[me], loc.at[k])
            cp.start()
            local.append(cp)
            for j, (cx, cy) in enumerate(chips):
                cp = remote(ins[k].at[mine], outs[k].at[me, mine], k, j, (cx, cy, c))
                cp.start()
                started.append(cp)
        for k in range(n):
            mine, _ = halves(k)
            for j in range(3):
                land = outs[k].at[chip_ids[j], mine]
                remote(land, land, k, j, sib).wait_recv()
                cp = remote(land, land, k, 3 + j, sib)
                cp.start()
                started.append(cp)
        for k in range(n):
            _, other = halves(k)
            for j in range(3):
                land = outs[k].at[chip_ids[j], other]
                remote(land, land, k, 3 + j, sib).wait_recv()
        for cp in started:
            cp.wait_send()
        for cp in local:
            cp.wait()

    return pl.pallas_call(
        body, name="all_gather_weights", in_specs=[_ANY] * n, out_specs=[_ANY] * n,
        out_shape=[jax.ShapeDtypeStruct((N_CHIPS,) + s.shape, s.dtype) for s in shards],
        scratch_shapes=[pltpu.SemaphoreType.DMA((n, 6)), pltpu.SemaphoreType.DMA((n, 6)),
                        pltpu.SemaphoreType.DMA((n,))],
        compiler_params=pltpu.CompilerParams(vmem_limit_bytes=V7X_VMEM_LIMIT_BYTES),
    )(*shards)


_RS_CHUNK = 32


def _reduce_scatter(name, grads):
    n = len(grads)

    def body(*refs):
        gs, outs = refs[:n], refs[n:2 * n]
        own, r1, r2, fin = (refs[(2 + i) * n:(3 + i) * n] for i in range(4))
        a_send, a_recv, b_send, b_recv, c_send, c_recv, l_in, l_out = refs[6 * n:]
        x, y, c, me, chips, chip_ids = _mesh_pos()
        sib = (x, y, 1 - c)

        def halves(k):
            hr = gs[k].shape[1] // 2
            return hr, pl.ds(pl.multiple_of(c * hr, 16), hr), pl.ds(pl.multiple_of((1 - c) * hr, 16), hr)

        def remote(src, dst, ssem, rsem, dev):
            return pltpu.make_async_remote_copy(src_ref=src, dst_ref=dst, send_sem=ssem, recv_sem=rsem,
                                                device_id=dev, device_id_type=_MESH)

        sends, locals_in = [], []
        for k in range(n):
            hr, mine, other = halves(k)
            cp = remote(gs[k].at[:, other, :], r1[k], a_send.at[k], a_recv.at[k], sib)
            cp.start()
            sends.append(cp)
            cp = pltpu.make_async_copy(gs[k].at[:, mine, :], own[k], l_in.at[k])
            cp.start()
            locals_in.append(cp)

        for k in range(n):
            hr, mine, other = halves(k)
            locals_in[k].wait()
            remote(r1[k], r1[k], a_send.at[k], a_recv.at[k], sib).wait_recv()
            for j in range(N_CHIPS):
                def add(i, carry, k=k, j=j):
                    rows = pl.ds(pl.multiple_of(i * _RS_CHUNK, _RS_CHUNK), _RS_CHUNK)
                    own[k][j, rows, :] = (own[k][j, rows, :].astype(F32) + r1[k][j, rows, :].astype(F32)).astype(BF16)
                    return carry

                lax.fori_loop(0, hr // _RS_CHUNK, add, 0)
            for j, (cx, cy) in enumerate(chips):
                cp = remote(own[k].at[chip_ids[j]], r2[k].at[j], b_send.at[k, j], b_recv.at[k, j], (cx, cy, c))
                cp.start()
                sends.append(cp)

        locals_out = []
        for k in range(n):
            hr, mine, other = halves(k)
            for j in range(3):
                remote(r2[k].at[j], r2[k].at[j], b_send.at[k, j], b_recv.at[k, j], sib).wait_recv()

            def total(i, carry, k=k):
                rows = pl.ds(pl.multiple_of(i * _RS_CHUNK, _RS_CHUNK), _RS_CHUNK)
                acc = own[k][me, rows, :].astype(F32)
                for j in range(3):
                    acc = acc + r2[k][j, rows, :].astype(F32)
                fin[k][rows, :] = acc
                return carry

            lax.fori_loop(0, hr // _RS_CHUNK, total, 0)
            cp = remote(fin[k], outs[k].at[mine, :], c_send.at[k], c_recv.at[k], sib)
            cp.start()
            sends.append(cp)
            cp = pltpu.make_async_copy(fin[k], outs[k].at[mine, :], l_out.at[k])
            cp.start()
            locals_out.append(cp)

        for k in range(n):
            hr, mine, other = halves(k)
            land = outs[k].at[other, :]
            remote(land, land, c_send.at[k], c_recv.at[k], sib).wait_recv()
        for cp in sends:
            cp.wait_send()
        for cp in locals_out:
            cp.wait()

    scratch = []
    for g in grads:
        scratch.append(pltpu.VMEM((N_CHIPS, g.shape[1] // 2, g.shape[2]), BF16))
    for g in grads:
        scratch.append(pltpu.VMEM((N_CHIPS, g.shape[1] // 2, g.shape[2]), BF16))
    for g in grads:
        scratch.append(pltpu.VMEM((3, g.shape[1] // 2, g.shape[2]), BF16))
    for g in grads:
        scratch.append(pltpu.VMEM((g.shape[1] // 2, g.shape[2]), F32))
    dma = pltpu.SemaphoreType.DMA
    scratch += [dma((n,)), dma((n,)), dma((n, 3)), dma((n, 3)), dma((n,)), dma((n,)), dma((n,)), dma((n,))]
    return pl.pallas_call(
        body, name=name, in_specs=[_ANY] * n, out_specs=[_ANY] * n,
        out_shape=[jax.ShapeDtypeStruct(g.shape[1:], F32) for g in grads],
        scratch_shapes=scratch,
        compiler_params=pltpu.CompilerParams(vmem_limit_bytes=V7X_VMEM_LIMIT_BYTES),
    )(*grads)


def _adamw_math(w, g, m, v):
    m = ADAM_B1 * m + (1.0 - ADAM_B1) * g
    v = ADAM_B2 * v + (1.0 - ADAM_B2) * (g * g)
    m_hat = m / (1.0 - ADAM_B1 ** ADAM_STEP)
    v_hat = v / (1.0 - ADAM_B2 ** ADAM_STEP)
    delta = -ADAM_LR * (m_hat / (jnp.sqrt(v_hat) + ADAM_EPS) + ADAM_WD * w)
    return delta, m, v


def _adamw_sum(name, w, parts, m, v):
    r, c = w.shape
    br = r
    while br * c * 4 > (1 << 20) and br % 32 == 0:
        br //= 2

    def body(w_ref, p_ref, m_ref, v_ref, g_ref, d_ref, nm_ref, nv_ref):
        g = p_ref[0].astype(F32)
        for j in range(1, N_CHIPS):
            g = g + p_ref[j].astype(F32)
        d, nm, nv = _adamw_math(w_ref[...], g, m_ref[...], v_ref[...])
        g_ref[...] = g
        d_ref[...] = d
        nm_ref[...] = nm
        nv_ref[...] = nv

    spec = pl.BlockSpec((br, c), lambda i: (i, 0))
    shp = jax.ShapeDtypeStruct((r, c), F32)
    return _call(
        body, name=name, grid=(r // br,),
        in_specs=[spec, pl.BlockSpec((N_CHIPS, br, c), lambda i: (0, i, 0)), spec, spec],
        out_specs=[spec] * 4, out_shape=[shp] * 4, sem=("parallel",), args=[w, parts, m, v])


_SMALL_ROWS = 80


def _small_allreduce_adamw(gpack, wpack, mpack, vpack):
    half = _SMALL_ROWS // 2

    def body(g_ref, w_ref, m_ref, v_ref, go_ref, d_ref, nm_ref, nv_ref, sib_buf, chip_sum, buf, send, recv):
        x, y, c, me, chips, chip_ids = _mesh_pos()
        sib = (x, y, 1 - c)
        mine = pl.ds(pl.multiple_of(c * half, 8), half)

        def remote(src, dst, k, dev):
            return pltpu.make_async_remote_copy(src_ref=src, dst_ref=dst, send_sem=send.at[k], recv_sem=recv.at[k],
                                                device_id=dev, device_id_type=_MESH)

        swap = remote(g_ref, sib_buf, 0, sib)
        swap.start()
        swap.wait()
        chip_sum[...] = g_ref[...] + sib_buf[...]
        buf[me] = chip_sum[...]
        sends = [remote(chip_sum.at[mine], buf.at[me, mine], 1 + j, (cx, cy, c)) for j, (cx, cy) in enumerate(chips)]
        for cp in sends:
            cp.start()
        for cp in sends:
            cp.wait()
        mirrors = [remote(buf.at[chip_ids[j], mine], buf.at[chip_ids[j], mine], 4 + j, sib) for j in range(3)]
        for cp in mirrors:
            cp.start()
        for cp in mirrors:
            cp.wait()
        total = buf[0]
        for i in range(1, N_CHIPS):
            total = total + buf[i]
        go_ref[...] = total
        d, nm, nv = _adamw_math(w_ref[...], total, m_ref[...], v_ref[...])
        d_ref[...] = d
        nm_ref[...] = nm
        nv_ref[...] = nv

    vm = pl.BlockSpec(memory_space=pltpu.VMEM)
    shp = jax.ShapeDtypeStruct((_SMALL_ROWS, D_MODEL), F32)
    return pl.pallas_call(
        body, name="small_allreduce_adamw", in_specs=[vm] * 4, out_specs=[vm] * 4, out_shape=[shp] * 4,
        scratch_shapes=[pltpu.VMEM((_SMALL_ROWS, D_MODEL), F32), pltpu.VMEM((_SMALL_ROWS, D_MODEL), F32),
                        pltpu.VMEM((N_CHIPS, _SMALL_ROWS, D_MODEL), F32), pltpu.SemaphoreType.DMA((7,)),
                        pltpu.SemaphoreType.DMA((7,))],
        compiler_params=pltpu.CompilerParams(vmem_limit_bytes=V7X_VMEM_LIMIT_BYTES),
    )(gpack, wpack, mpack, vpack)


_SMALL_VECTORS = ("ffn1_norm", "mix_norm", "xattn_norm", "mem_norm", "ffn2_norm", "final_norm", "q_norm",
                  "kv_norm", "pool_scale")


_LOSS_ROW = 9


def _pack_small(d, scalar=None):
    rows = []
    for n in _SMALL_VECTORS:
        v = d[n].reshape(1, -1).astype(F32)
        rows.append(jnp.pad(v, ((0, 0), (0, D_MODEL - v.shape[1]))))
    assert len(rows) == _LOSS_ROW
    extra = jnp.zeros((1, D_MODEL), F32) if scalar is None else jnp.pad(scalar.reshape(1, 1), ((0, 0), (0, D_MODEL - 1)))
    rows.append(extra)
    rows.append(jnp.zeros((16 - len(rows), D_MODEL), F32))
    rows.append(d["pool_w"].reshape(64, D_MODEL).astype(F32))
    return jnp.concatenate(rows, axis=0)


def _unpack_small(pack, like):
    out = {}
    for i, n in enumerate(_SMALL_VECTORS):
        out[n] = pack[i, :like[n].size].reshape(like[n].shape)
    out["pool_w"] = pack[16:].reshape(like["pool_w"].shape)
    return out


_WEIGHTS = ("ffn1_norm", "ffn1_w_gate", "ffn1_w_up", "ffn1_w_down", "mix_norm", "w_in", "q_norm", "w_q_up",
            "kv_norm", "w_kv_up", "pool_w", "pool_scale", "w_out", "xattn_norm", "mem_norm", "w_mq", "w_mkv",
            "w_mo", "ffn2_norm", "ffn2_w_gate", "ffn2_w_up", "ffn2_w_down", "final_norm")
_SHARDED = ("ffn1_w_gate", "ffn1_w_up", "ffn1_w_down", "w_in", "w_q_up", "w_kv_up", "w_out", "w_mq", "w_mkv",
            "w_mo", "ffn2_w_gate", "ffn2_w_up", "ffn2_w_down")
_RS_GROUPS = (("ffn2_w_gate", "ffn2_w_up", "ffn2_w_down"),
              ("w_mo", "w_mq", "w_mkv", "w_out", "w_q_up", "w_kv_up", "w_in"),
              ("ffn1_w_gate", "ffn1_w_up", "ffn1_w_down"))
W_IN_SPLIT = Q_LORA + KV_LORA + ROPE_DIM


_FFN1 = ("ffn1_w_gate", "ffn1_w_up", "ffn1_w_down")
_TRANSPOSED = ("ffn1_w_gate", "ffn1_w_up", "ffn2_w_gate", "ffn2_w_up", "w_in", "w_q_up")


def _local_view(name, a):
    return jnp.swapaxes(a, 1, 2)[0] if name in _TRANSPOSED else a[0]


def _global_view(name, a):
    return jnp.swapaxes(a[None], 1, 2) if name in _TRANSPOSED else a[None]


def _pad_shard(name, a):
    if name == "w_in":
        return jnp.concatenate([a[:W_IN_SPLIT], jnp.zeros((64, a.shape[1]), a.dtype), a[W_IN_SPLIT:]], axis=0)
    if name == "w_q_up":
        return jnp.pad(a, ((0, 64), (0, 0)))
    return a


def _unpad_shard(name, a):
    if name == "w_in":
        return jnp.concatenate([a[:, :W_IN_SPLIT], a[:, W_IN_SPLIT + 64:]], axis=1)
    if name == "w_q_up":
        return a[:, :192]
    return a


def _stacked(g):
    return g if g.ndim == 3 else g.reshape(N_CHIPS, g.shape[0] // N_CHIPS, g.shape[1])


class _Plan:
    AG_UNITS = (
        (("w_in", "w_q_up", "w_kv_up", "ffn2_w_gate"), "ffn1_up", "ffn1_down"),
        (("ffn2_w_up",), "ffn1_down", "w_in"),
        (("w_out", "w_mq", "w_mkv", "w_mo", "ffn2_w_down"), "mla_fwd", "pool_fwd"),
    )
    RS_UNITS = (
        (("ffn2_w_gate", "ffn2_w_up", "ffn2_w_down"), "ffn2_dn_a", "mla_bwd", "qkv_prep_bwd"),
        (("w_mo", "w_mq", "w_mkv"), "w_out_dx", "mla_bwd", "qkv_prep_bwd"),
        (("w_out", "w_q_up", "w_kv_up", "w_in"), "w_in_dx", "ffn1_dact", "ffn1_dwd"),
        (("ffn1_w_down",), "ffn1_dwg", "ffn1_dwu", "ffn1_dn_a"),
        (("ffn1_w_gate",), "ffn1_dwu", "ffn1_dn_a", "ffn1_dn_b"),
        (("ffn1_w_up",), "ffn1_dn_a", "ffn1_dn_b", "adamw_w_kv_up"),
    )
    ADAMW_ORDER = ("w_kv_up", "ffn2_w_gate", "ffn2_w_up", "ffn2_w_down", "w_mo", "w_mq", "w_mkv", "w_out", "w_q_up",
                   "w_in", "ffn1_w_down", "ffn1_w_gate", "ffn1_w_up")

    def __init__(self, shards, w, grads, core):
        self.shards, self.w, self.grads, self.core = shards, w, grads, core
        self.parts = {}
        self.ag = [[None, None] for _ in self.AG_UNITS]
        self.rs = [[None, None, None, None] for _ in self.RS_UNITS]

    def pre(self, name):
        for i, (names, h1, h2) in enumerate(self.AG_UNITS):
            if name == h1:
                self.ag[i][0] = _host(name, _ag_ici_stage([self.shards[n] for n in names]))
            if name == h2:
                self.ag[i][1] = _host(name, _ag_d2d_stage(self.ag[i][0].results))
        for i, (names, h1, h2, h3) in enumerate(self.RS_UNITS):
            if name == h1:
                self.rs[i][0] = _host(name, _rs_swap_stage([_stacked(self.grads[n]) for n in names]))
            if name == h2:
                self.rs[i][2] = _host(name, _rs_scatter_stage(self.rs[i][1], relative=names[0] in _FFN1))
            if name == h3:
                self.rs[i][3] = _host(name, _rs_mirror_stage(self.rs[i][2].results))

    def post(self, name):
        for i, (names, h1, h2) in enumerate(self.AG_UNITS):
            if name == h2:
                for n, f in zip(names, self.ag[i][1].results):
                    self.w[n] = _full_weight(n, f)
        for i, (names, h1, h2, h3) in enumerate(self.RS_UNITS):
            if name == h1:
                self.rs[i][1] = [_pair_add("pair_add_" + n, _stacked(self.grads[n]), r1, self.core)
                                 for n, r1 in zip(names, self.rs[i][0].results)]
            if name == h3:
                for n, p in zip(names, self.rs[i][3].results):
                    self.parts[n] = p


def _full_weight(name, stacked):
    if name in ("w_out", "w_mq", "w_mo"):
        return stacked.reshape(D_MODEL, D_MODEL)
    return stacked


def kernel(x, mem, positions, ffn1_norm, ffn1_w_gate, ffn1_w_up, ffn1_w_down, mix_norm, w_in, q_norm, w_q_up, kv_norm, w_kv_up, pool_w, pool_scale, w_out, xattn_norm, mem_norm, w_mq, w_mkv, w_mo, ffn2_norm, ffn2_w_gate, ffn2_w_up, ffn2_w_down, final_norm, loss_target, m_ffn1_norm, m_ffn1_w_gate, m_ffn1_w_up, m_ffn1_w_down, m_mix_norm, m_w_in, m_q_norm, m_w_q_up, m_kv_norm, m_w_kv_up, m_pool_w, m_pool_scale, m_w_out, m_xattn_norm, m_mem_norm, m_w_mq, m_w_mkv, m_w_mo, m_ffn2_norm, m_ffn2_w_gate, m_ffn2_w_up, m_ffn2_w_down, m_final_norm, v_ffn1_norm, v_ffn1_w_gate, v_ffn1_w_up, v_ffn1_w_down, v_mix_norm, v_w_in, v_q_norm, v_w_q_up, v_kv_norm, v_w_kv_up, v_pool_w, v_pool_scale, v_w_out, v_xattn_norm, v_mem_norm, v_w_mq, v_w_mkv, v_w_mo, v_ffn2_norm, v_ffn2_w_gate, v_ffn2_w_up, v_ffn2_w_down, v_final_norm):
    wts = dict(zip(_WEIGHTS, (ffn1_norm, ffn1_w_gate, ffn1_w_up, ffn1_w_down, mix_norm, w_in, q_norm, w_q_up, kv_norm, w_kv_up, pool_w, pool_scale, w_out, xattn_norm, mem_norm, w_mq, w_mkv, w_mo, ffn2_norm, ffn2_w_gate, ffn2_w_up, ffn2_w_down, final_norm)))
    mom = dict(zip(_WEIGHTS, (m_ffn1_norm, m_ffn1_w_gate, m_ffn1_w_up, m_ffn1_w_down, m_mix_norm, m_w_in, m_q_norm, m_w_q_up, m_kv_norm, m_w_kv_up, m_pool_w, m_pool_scale, m_w_out, m_xattn_norm, m_mem_norm, m_w_mq, m_w_mkv, m_w_mo, m_ffn2_norm, m_ffn2_w_gate, m_ffn2_w_up, m_ffn2_w_down, m_final_norm)))
    var = dict(zip(_WEIGHTS, (v_ffn1_norm, v_ffn1_w_gate, v_ffn1_w_up, v_ffn1_w_down, v_mix_norm, v_w_in, v_q_norm, v_w_q_up, v_kv_norm, v_w_kv_up, v_pool_w, v_pool_scale, v_w_out, v_xattn_norm, v_mem_norm, v_w_mq, v_w_mkv, v_w_mo, v_ffn2_norm, v_ffn2_w_gate, v_ffn2_w_up, v_ffn2_w_down, v_final_norm)))
    small = [n for n in _WEIGHTS if n not in _SHARDED]

    global _PLAN
    shards = {n: _pad_shard(n, _local_view(n, wts[n])).astype(BF16) for n in _SHARDED}
    w = {n: wts[n].reshape(1, -1) for n in _SMALL_VECTORS}
    w["pool_w"] = pool_w[0].astype(BF16)
    grads = {}
    core = lax.axis_index("c").astype(jnp.int32).reshape(1)
    plan = _Plan(shards, w, grads, core)
    _PLAN = plan
    try:
        w["ffn1_shards"] = tuple(shards[n] for n in _FFN1)

        loss_local, dx = _local_step(x[0], mem[0], positions[0], loss_target[0], w, grads)

        gpack, dpack, mpack, vpack = _small_allreduce_adamw(
            _pack_small({n: grads[n] for n in small}, loss_local), _pack_small({n: wts[n] for n in small}),
            _pack_small({n: mom[n] for n in small}), _pack_small({n: var[n] for n in small}))
        like = {n: wts[n] for n in small}
        g_out, d_out, m_out, v_out = (_unpack_small(p, like) for p in (gpack, dpack, mpack, vpack))
        loss = gpack[_LOSS_ROW, 0]

        for n in _Plan.ADAMW_ORDER:
            res = _adamw_sum("adamw_" + n, _local_view(n, wts[n]), _unpad_shard(n, plan.parts[n]),
                             _local_view(n, mom[n]), _local_view(n, var[n]))
            g_out[n], d_out[n], m_out[n], v_out[n] = (_global_view(n, r) for r in res)
    finally:
        _PLAN = None
        _PENDING.clear()

    return (loss, dx[None], *[g_out[n] for n in _WEIGHTS], *[d_out[n] for n in _WEIGHTS],
            *[m_out[n] for n in _WEIGHTS], *[v_out[n] for n in _WEIGHTS])
```

```python
import functools

import jax
import jax.numpy as jnp
from jax import lax
from jax.experimental import pallas as pl
from jax.experimental.pallas import tpu as pltpu

F32 = jnp.float32
BF16 = jnp.bfloat16

D_MODEL = 1024
D_FF = 2816
N_CHIPS = 4
FF_SHARD = D_FF // N_CHIPS
MLA_HEADS = 4
Q_LORA = 256
KV_LORA = 128
ROPE_DIM = 64
HEAD_QK = 256
HEAD_V = 128
POOL_GROUPS = 4
POOL_CH = 128
MEM_HEADS = 4
MEM_HEAD_DIM = 256
RMS_EPS = 1e-6
ROPE_BASE = 10000.0
MLA_SCALE = (128 + 64) ** -0.5
MEM_SCALE = MEM_HEAD_DIM ** -0.5

ADAM_LR = 0.001
ADAM_B1 = 0.9
ADAM_B2 = 0.999
ADAM_EPS = 1e-08
ADAM_WD = 0.01
ADAM_STEP = 10

V7X_VMEM_LIMIT_BYTES = 56 * 1024 * 1024

NN = ((1,), (0,))
NT = ((1,), (1,))
TN = ((0,), (0,))


def _params(*sem):
    return pltpu.CompilerParams(dimension_semantics=sem, vmem_limit_bytes=V7X_VMEM_LIMIT_BYTES)


_MESH = pl.DeviceIdType.MESH
_ANY = pl.BlockSpec(memory_space=pl.ANY)


class _Stage:
    def __init__(self, ins, outs, n_remote, n_local, copies, aliases=None):
        self.ins, self.outs, self.n_remote, self.n_local = list(ins), list(outs), n_remote, n_local
        self.copies, self.aliases = copies, dict(aliases or {})
        self.results = None
        self.start_step = 0
        self.then = None

    def descriptors(self, in_refs, out_refs, send, recv, loc):
        ds, ri, li = [], 0, 0
        for src, dst, dev in self.copies(in_refs, out_refs):
            if dev is None:
                ds.append(pltpu.make_async_copy(src, dst, loc.at[li]))
                li += 1
            else:
                ds.append(pltpu.make_async_remote_copy(src_ref=src, dst_ref=dst, send_sem=send.at[ri],
                                                       recv_sem=recv.at[ri], device_id=dev, device_id_type=_MESH))
                ri += 1
        assert ri == self.n_remote and li == self.n_local
        return ds


_PENDING = {}


def _host(name, stage):
    _PENDING.setdefault(name, []).append(stage)
    return stage


_PLAN = None


def _call(body, **kw):
    if _PLAN is not None:
        _PLAN.pre(kw["name"])
    res = _call_hosting(body, **kw)
    if _PLAN is not None:
        _PLAN.post(kw["name"])
    return res


def _call_hosting(body, *, name, grid, in_specs, out_specs, out_shape, sem, args, scratch_shapes=(), aliases=None):
    stages = _PENDING.pop(name, [])
    scratch_shapes = list(scratch_shapes)
    if not stages:
        return pl.pallas_call(body, name=name, grid=grid, in_specs=in_specs, out_specs=out_specs,
                              out_shape=out_shape, scratch_shapes=scratch_shapes,
                              input_output_aliases=dict(aliases or {}), compiler_params=_params(*sem))(*args)
    ni, no, ns = len(in_specs), len(out_shape), len(scratch_shapes)
    c_ins = [a for st in stages for a in st.ins]
    c_outs = [o for st in stages for o in st.outs]
    nci, nco = len(c_ins), len(c_outs)
    aliases, io, oo = dict(aliases or {}), 0, 0
    for st in stages:
        for i, j in st.aliases.items():
            aliases[ni + io + i] = no + oo + j
        io += len(st.ins)
        oo += len(st.outs)
    dma = pltpu.SemaphoreType.DMA
    sems = []
    for st in stages:
        sems += [dma((max(st.n_remote, 1),)), dma((max(st.n_remote, 1),)), dma((max(st.n_local, 1),))]
    followers = [st.then for st in stages if st.then is not None]
    for st in followers:
        sems += [dma((max(st.n_remote, 1),)), dma((max(st.n_remote, 1),)), dma((max(st.n_local, 1),))]

    def wrapped(*refs):
        ins, cin = refs[:ni], refs[ni:ni + nci]
        outs, cout = refs[ni + nci:ni + nci + no], refs[ni + nci + no:ni + nci + no + nco]
        scr = refs[ni + nci + no + nco:ni + nci + no + nco + ns]
        sem_refs = refs[ni + nci + no + nco + ns:]
        step = pl.program_id(0)
        last = pl.program_id(0) == grid[0] - 1
        for ax in range(1, len(grid)):
            step = step * grid[ax] + pl.program_id(ax)
            last = jnp.logical_and(last, pl.program_id(ax) == grid[ax] - 1)

        def descriptors(si):
            io = sum(len(st.ins) for st in stages[:si])
            oo = sum(len(st.outs) for st in stages[:si])
            st = stages[si]
            return st.descriptors(cin[io:io + len(st.ins)], cout[oo:oo + len(st.outs)], *sem_refs[3 * si:3 * si + 3])

        def follower_descriptors(fi):
            si = [k for k, st in enumerate(stages) if st.then is not None][fi]
            oo = sum(len(st.outs) for st in stages[:si])
            bufs = cout[oo:oo + len(stages[si].outs)]
            k0 = 3 * (len(stages) + fi)
            return followers[fi].descriptors(bufs, bufs, *sem_refs[k0:k0 + 3])

        body(*ins, *outs, *scr)

        for si, st in enumerate(stages):
            @pl.when(step == st.start_step)
            def _(si=si):
                for d in descriptors(si):
                    d.start()

        @pl.when(last)
        def _():
            for si in range(len(stages)):
                for d in descriptors(si):
                    d.wait()
            for fi in range(len(followers)):
                for d in follower_descriptors(fi):
                    d.start()
            for fi in range(len(followers)):
                for d in follower_descriptors(fi):
                    d.wait()

    res = pl.pallas_call(
        wrapped, name=name, grid=grid, in_specs=list(in_specs) + [_ANY] * nci,
        out_specs=list(out_specs) + [_ANY] * nco, out_shape=list(out_shape) + c_outs,
        scratch_shapes=scratch_shapes + sems, input_output_aliases=aliases,
        compiler_params=_params(*(("arbitrary",) * len(grid))))(*args, *c_ins)
    oo = no
    for st in stages:
        st.results = list(res[oo:oo + len(st.outs)])
        oo += len(st.outs)
    return list(res[:no])


def _dot(a, b, dims):
    return lax.dot_general(a.astype(BF16), b.astype(BF16), (dims, ((), ())), preferred_element_type=F32)


_MAX_ROW_BLOCK = 1024
_ATT_BLOCK = 512


_MAX_REDUCE_BLOCK = 2048


def _row_block(s, want=1024):
    return min(want, s, _MAX_ROW_BLOCK)


def _reduce_block(s):
    return min(s, _MAX_REDUCE_BLOCK)


def _matmul(name, grid, terms, extras, outs, epilogue, acc_shape, fill=(), summed=()):
    nt, ne, no, nf = len(terms), len(extras), len(outs), len(fill)
    nk = grid[-1]
    dims = [t[4] for t in terms]

    def body(*refs):
        a_refs, b_refs = refs[:nt], refs[nt:2 * nt]
        e_refs = refs[2 * nt:2 * nt + ne]
        o_refs = refs[2 * nt + ne + nf:2 * nt + ne + nf + no]

        def finish(acc):
            vals = epilogue(acc, *[e[...] for e in e_refs])
            for idx, (o, val) in enumerate(zip(o_refs, vals)):
                if idx in summed:
                    @pl.when(pl.program_id(0) == 0)
                    def _(o=o, val=val):
                        o[...] = val.astype(o.dtype)

                    @pl.when(pl.program_id(0) > 0)
                    def _(o=o, val=val):
                        o[...] += val.astype(o.dtype)
                else:
                    o[...] = val.astype(o.dtype)

        if nk == 1:
            part = None
            for a, b, d in zip(a_refs, b_refs, dims):
                t = _dot(a[...], b[...], d)
                part = t if part is None else part + t
            finish(part)
        else:
            acc_ref = refs[-1]
            k = pl.program_id(len(grid) - 1)

            @pl.when(k == 0)
            def _():
                acc_ref[...] = jnp.zeros_like(acc_ref)

            for a, b, d in zip(a_refs, b_refs, dims):
                acc_ref[...] += _dot(a[...], b[...], d)

            @pl.when(k == nk - 1)
            def _():
                finish(acc_ref[...])

    in_specs = [t[1] for t in terms] + [t[3] for t in terms] + [e[1] for e in extras] + [_ANY] * nf
    args = [t[0] for t in terms] + [t[2] for t in terms] + [e[0] for e in extras] + list(fill)
    sem = ("arbitrary" if summed else "parallel",) * (len(grid) - 1) + ("arbitrary",)
    aliases = {2 * nt + ne + i: i for i in range(nf)}
    return _call(
        body, name=name, grid=grid, in_specs=in_specs,
        out_specs=[o[1] for o in outs], out_shape=[o[0] for o in outs],
        scratch_shapes=[pltpu.VMEM(acc_shape, F32)] if nk > 1 else [], sem=sem, args=args, aliases=aliases)


def _ident(acc):
    return (acc,)


def _rmsnorm_fwd(name, x, gain, width, col_block=0):
    s = x.shape[0]
    bm = _row_block(s)

    def body(x_ref, g_ref, o_ref):
        xf = x_ref[...]
        r = lax.rsqrt(jnp.mean(xf * xf, axis=-1, keepdims=True) + RMS_EPS)
        o_ref[...] = ((xf * r) * g_ref[...]).astype(o_ref.dtype)

    return pl.pallas_call(
        body, name=name, grid=(s // bm,),
        in_specs=[pl.BlockSpec((bm, width), lambda i: (i, col_block)), pl.BlockSpec((1, width), lambda i: (0, 0))],
        out_specs=pl.BlockSpec((bm, width), lambda i: (i, 0)),
        out_shape=jax.ShapeDtypeStruct((s, width), BF16),
        compiler_params=_params("parallel"),
    )(x, gain)


def _rms_bwd_math(dy, xf, g, width):
    r = lax.rsqrt(jnp.mean(xf * xf, axis=-1, keepdims=True) + RMS_EPS)
    dyg = dy * g
    dot = jnp.sum(dyg * xf, axis=-1, keepdims=True)
    dx = r * dyg - xf * ((r * r * r) * (dot * (1.0 / width)))
    dgain = jnp.sum(dy * (xf * r), axis=0, keepdims=True)
    return dx, dgain


def _rmsnorm_bwd(name, dy, x, gain, width, col_block=0, dres=None, out_dtype=F32):
    s = x.shape[0]
    bm = _row_block(s)
    has_res = dres is not None

    def body(*refs):
        if has_res:
            dy_ref, x_ref, g_ref, r_ref, dx_ref, dg_ref, dxb_ref = refs
        else:
            dy_ref, x_ref, g_ref, dx_ref, dg_ref = refs
        dx, dgain = _rms_bwd_math(dy_ref[...].astype(F32), x_ref[...], g_ref[...], width)
        if has_res:
            dx = dx + r_ref[...]
            dxb_ref[...] = dx.astype(BF16)
        dx_ref[...] = dx.astype(dx_ref.dtype)

        @pl.when(pl.program_id(0) == 0)
        def _():
            dg_ref[...] = dgain

        @pl.when(pl.program_id(0) > 0)
        def _():
            dg_ref[...] += dgain

    row = pl.BlockSpec((bm, width), lambda i: (i, 0))
    in_specs = [row, pl.BlockSpec((bm, width), lambda i: (i, col_block)), pl.BlockSpec((1, width), lambda i: (0, 0))]
    args = [dy, x, gain]
    out_specs = [row, pl.BlockSpec((1, width), lambda i: (0, 0))]
    out_shape = [jax.ShapeDtypeStruct((s, width), out_dtype), jax.ShapeDtypeStruct((1, width), F32)]
    if has_res:
        in_specs.append(row)
        args.append(dres)
        out_specs.append(row)
        out_shape.append(jax.ShapeDtypeStruct((s, width), BF16))
    return _call(body, name=name, grid=(s // bm,), in_specs=in_specs, out_specs=out_specs, out_shape=out_shape,
                 sem=("arbitrary",), args=args)


def _loss_and_final_norm(h, gain, target):
    s, d = h.shape
    bm = _row_block(s, 512)

    def body(h_ref, g_ref, t_ref, dh_ref, dhb_ref, loss_ref, dg_ref):
        xf = h_ref[...]
        g = g_ref[...]
        r = lax.rsqrt(jnp.mean(xf * xf, axis=-1, keepdims=True) + RMS_EPS)
        err = (xf * r) * g - t_ref[...]
        part = 0.5 * jnp.sum(jnp.mean(err * err, axis=-1, keepdims=True), axis=0, keepdims=True)
        dx, dgain = _rms_bwd_math(err * (1.0 / d), xf, g, d)
        dh_ref[...] = dx
        dhb_ref[...] = dx.astype(BF16)

        @pl.when(pl.program_id(0) == 0)
        def _():
            dg_ref[...] = dgain
            loss_ref[...] = jnp.broadcast_to(part, loss_ref.shape)

        @pl.when(pl.program_id(0) > 0)
        def _():
            dg_ref[...] += dgain
            loss_ref[...] += jnp.broadcast_to(part, loss_ref.shape)

    row = pl.BlockSpec((bm, d), lambda i: (i, 0))
    vec = pl.BlockSpec((1, d), lambda i: (0, 0))
    return pl.pallas_call(
        body, name="loss_final_norm", grid=(s // bm,), in_specs=[row, vec, row],
        out_specs=[row, row, pl.BlockSpec((1, 128), lambda i: (0, 0)), vec],
        out_shape=[jax.ShapeDtypeStruct((s, d), F32), jax.ShapeDtypeStruct((s, d), BF16),
                   jax.ShapeDtypeStruct((1, 128), F32),
                   jax.ShapeDtypeStruct((1, d), F32)],
        compiler_params=_params("arbitrary"),
    )(h, gain, target)


def _ffn_up(name, n, wg, wu):
    s = n.shape[0]
    bm = _row_block(s)

    def body(n_ref, wg_ref, wu_ref, a_ref, dadu_ref, dadg_ref):
        x = n_ref[...]
        g = _dot(x, wg_ref[...], NT)
        u = _dot(x, wu_ref[...], NT)
        sg = jax.nn.sigmoid(g)
        silu = g * sg
        a_ref[...] = (silu * u).astype(BF16)
        dadu_ref[...] = silu.astype(BF16)
        dadg_ref[...] = (u * (sg * (1.0 + g * (1.0 - sg)))).astype(BF16)

    w_spec = pl.BlockSpec((None, FF_SHARD, D_MODEL), lambda j, i: (j, 0, 0))
    o_spec = pl.BlockSpec((None, bm, FF_SHARD), lambda j, i: (j, i, 0))
    shp = jax.ShapeDtypeStruct((N_CHIPS, s, FF_SHARD), BF16)
    return _call(
        body, name=name, grid=(N_CHIPS, s // bm),
        in_specs=[pl.BlockSpec((bm, D_MODEL), lambda j, i: (i, 0)), w_spec, w_spec],
        out_specs=[o_spec, o_spec, o_spec], out_shape=[shp, shp, shp],
        sem=("parallel", "parallel"), args=[n, wg, wu])


def _ffn1_up_gather(n, g_sh, u_sh, d_sh):
    s = n.shape[0]
    bm = _row_block(s)
    nrb = s // bm
    rows, cols = g_sh.shape
    rels = (1, 2, 3)

    def body(n_ref, gs, us, ds, a_ref, dadu_ref, dadg_ref, wg, wu, wd, gbuf, ubuf, send, recv, fsend, frecv, loc, ld):
        r, i = pl.program_id(0), pl.program_id(1)
        x, y, c = lax.axis_index("x"), lax.axis_index("y"), lax.axis_index("c")
        sib = (x, y, 1 - c)
        mine, _ = _half_rows(c, rows)
        shards, fulls, bufs = (gs, us, ds), (wg, wu, wd), (gbuf, ubuf)

        def ici(k, rel, dev=sib):
            return pltpu.make_async_remote_copy(
                src_ref=shards[k].at[mine], dst_ref=fulls[k].at[rel, mine], send_sem=send.at[k, rel - 1],
                recv_sem=recv.at[k, rel - 1], device_id=dev, device_id_type=_MESH)

        def peer(rel):
            return ((1 - x) if rel & 2 else x, (1 - y) if rel & 1 else y, c)

        def fwd(k, rel):
            return pltpu.make_async_remote_copy(
                src_ref=fulls[k].at[rel, mine], dst_ref=fulls[k].at[rel, mine], send_sem=fsend.at[k, rel - 1],
                recv_sem=frecv.at[k, rel - 1], device_id=sib, device_id_type=_MESH)

        def own(k):
            return pltpu.make_async_copy(shards[k], fulls[k].at[0], loc.at[k])

        def load(k, src):
            return pltpu.make_async_copy(src, bufs[k], ld.at[k])

        @pl.when(jnp.logical_and(r == 0, i == 0))
        def _():
            for k in range(3):
                own(k).start()
            for rel in (1, 2):
                for k in (0, 1):
                    ici(k, rel, peer(rel)).start()
            for k in (0, 1):
                load(k, shards[k]).start()
            for k in (0, 1):
                load(k, shards[k]).wait()

        @pl.when(jnp.logical_and(r > 0, i == 0))
        def _():
            for k in (0, 1):
                ici(k, r).wait_recv()
                fwd(k, r).start()
            for k in (0, 1):
                fwd(k, r).wait_recv()
                load(k, fulls[k].at[r]).start()
            for k in (0, 1):
                load(k, fulls[k].at[r]).wait()

        @pl.when(jnp.logical_and(r == 1, i == 0))
        def _():
            for k in (0, 1):
                ici(k, 3, peer(3)).start()

        @pl.when(jnp.logical_and(r == 2, i == 0))
        def _():
            for rel in (1, 2):
                ici(2, rel, peer(rel)).start()

        @pl.when(jnp.logical_and(r == 3, i == 0))
        def _():
            ici(2, 3, peer(3)).start()

        xv = n_ref[...]
        g = _dot(xv, gbuf[...], NT)
        u = _dot(xv, ubuf[...], NT)
        sg = jax.nn.sigmoid(g)
        silu = g * sg
        a_ref[...] = (silu * u).astype(BF16)
        dadu_ref[...] = silu.astype(BF16)
        dadg_ref[...] = (u * (sg * (1.0 + g * (1.0 - sg)))).astype(BF16)

        @pl.when(jnp.logical_and(r == 3, i == nrb - 1))
        def _():
            for rel in rels:
                ici(2, rel).wait_recv()
                fwd(2, rel).start()
            for rel in rels:
                fwd(2, rel).wait_recv()
            for k in range(3):
                for rel in rels:
                    ici(k, rel).wait_send()
                    fwd(k, rel).wait_send()
                own(k).wait()

    o_spec = pl.BlockSpec((None, bm, FF_SHARD), lambda r, i: (r, i, 0))
    act = jax.ShapeDtypeStruct((N_CHIPS, s, FF_SHARD), BF16)
    full = jax.ShapeDtypeStruct((N_CHIPS, rows, cols), BF16)
    dma = pltpu.SemaphoreType.DMA
    if _PLAN is not None:
        _PLAN.last_slab_step = 3 * nrb
    return _call(
        body, name="ffn1_up", grid=(N_CHIPS, nrb),
        in_specs=[pl.BlockSpec((bm, D_MODEL), lambda r, i: (i, 0)), _ANY, _ANY, _ANY],
        out_specs=[o_spec, o_spec, o_spec, _ANY, _ANY, _ANY], out_shape=[act, act, act, full, full, full],
        scratch_shapes=[pltpu.VMEM((rows, cols), BF16), pltpu.VMEM((rows, cols), BF16), dma((3, 3)), dma((3, 3)),
                        dma((3, 3)), dma((3, 3)), dma((3,)), dma((2,))],
        sem=("arbitrary", "arbitrary"), args=[n, g_sh, u_sh, d_sh])


def _residual_epilogue(alpha, with_norm):
    if not with_norm:
        return lambda acc, r: (r + alpha * acc,)

    def epilogue(acc, r, g):
        h = r + alpha * acc
        rs = lax.rsqrt(jnp.mean(h * h, axis=-1, keepdims=True) + RMS_EPS)
        return h, (h * rs) * g

    return epilogue


def _residual_outs(s, bm, gain):
    row = pl.BlockSpec((bm, D_MODEL), lambda i, k: (i, 0))
    outs = [(jax.ShapeDtypeStruct((s, D_MODEL), F32), row)]
    if gain is None:
        return [], outs
    return [(gain, pl.BlockSpec((1, D_MODEL), lambda i, k: (0, 0)))], outs + [(jax.ShapeDtypeStruct((s, D_MODEL), BF16), row)]


def _ffn_down(name, a, wd, res, gain=None):
    s = a.shape[1]
    bm = _row_block(s, 512)
    row = pl.BlockSpec((bm, D_MODEL), lambda i, k: (i, 0))
    terms = [(a, pl.BlockSpec((None, bm, FF_SHARD), lambda i, k, j=j: (j, i, 0)),
              wd, pl.BlockSpec((None, FF_SHARD, D_MODEL), lambda i, k, j=j: (j, 0, 0)), NN) for j in range(N_CHIPS)]
    extras, outs = _residual_outs(s, bm, gain)
    res_out = _matmul(name, (s // bm, 1), terms, [(res, row)] + extras, outs,
                      _residual_epilogue(0.5, gain is not None), None)
    return res_out if gain is not None else res_out[0]


def _norm_bwd_epilogue(width):
    def epilogue(acc, h, g, dres):
        dx, dgain = _rms_bwd_math(acc, h, g, width)
        dx = dx + dres
        return dx, dx, dgain

    return epilogue


def _norm_bwd_operands(s, bm, h, gain, dres):
    row = pl.BlockSpec((bm, D_MODEL), lambda i, k: (i, 0))
    vec = pl.BlockSpec((1, D_MODEL), lambda i, k: (0, 0))
    extras = [(h, row), (gain, vec), (dres, row)]
    outs = [(jax.ShapeDtypeStruct((s, D_MODEL), F32), row), (jax.ShapeDtypeStruct((s, D_MODEL), BF16), row),
            (jax.ShapeDtypeStruct((1, D_MODEL), F32), vec)]
    return extras, outs, (2,)


def _ffn_bwd(tag, dh, n, dadg, dadu, a, wg, wu, wd, grads, norm_bwd=None):
    s = dh.shape[0]
    bm = _row_block(s)
    bk = _reduce_block(s)
    nk = s // bk

    def act_bwd(acc, dg_da, du_da):
        da = 0.5 * acc
        return da * dg_da.astype(F32), da * du_da.astype(F32)

    slab = pl.BlockSpec((None, bm, FF_SHARD), lambda j, i, k: (j, i, 0))
    shp = jax.ShapeDtypeStruct((N_CHIPS, s, FF_SHARD), BF16)
    dg, du = _matmul(
        tag + "_dact", (N_CHIPS, s // bm, 1),
        [(dh, pl.BlockSpec((bm, D_MODEL), lambda j, i, k: (i, 0)),
          wd, pl.BlockSpec((None, FF_SHARD, D_MODEL), lambda j, i, k: (j, 0, 0)), NT)],
        [(dadg, slab), (dadu, slab)], [(shp, slab), (shp, slab)], act_bwd, None)

    grads[tag + "_w_down"] = _matmul(
        tag + "_dwd", (N_CHIPS, nk),
        [(a, pl.BlockSpec((None, bk, FF_SHARD), lambda j, k: (j, k, 0)),
          dh, pl.BlockSpec((bk, D_MODEL), lambda j, k: (k, 0)), TN)],
        [], [(jax.ShapeDtypeStruct((N_CHIPS, FF_SHARD, D_MODEL), BF16),
              pl.BlockSpec((None, FF_SHARD, D_MODEL), lambda j, k: (j, 0, 0)))],
        lambda acc: (0.5 * acc,), (FF_SHARD, D_MODEL))[0]

    def dw_up(nm, dact):
        return _matmul(
            nm, (N_CHIPS, nk),
            [(dact, pl.BlockSpec((None, bk, FF_SHARD), lambda j, k: (j, k, 0)),
              n, pl.BlockSpec((bk, D_MODEL), lambda j, k: (k, 0)), TN)],
            [], [(jax.ShapeDtypeStruct((N_CHIPS, FF_SHARD, D_MODEL), BF16),
                  pl.BlockSpec((None, FF_SHARD, D_MODEL), lambda j, k: (j, 0, 0)))],
            _ident, (FF_SHARD, D_MODEL))[0]

    grads[tag + "_w_gate"] = dw_up(tag + "_dwg", dg)
    grads[tag + "_w_up"] = dw_up(tag + "_dwu", du)

    bn = _row_block(s, 512)
    steps = s // bn // 2
    prev, dgain = (), None
    for part, off in (("_dn_a", 0), ("_dn_b", steps)):
        row = pl.BlockSpec((bn, D_MODEL), lambda i, k, off=off: (i + off, 0))
        terms = []
        for j in range(N_CHIPS):
            a_slab = pl.BlockSpec((None, bn, FF_SHARD), lambda i, k, j=j, off=off: (j, i + off, 0))
            w_slab = pl.BlockSpec((None, FF_SHARD, D_MODEL), lambda i, k, j=j: (j, 0, 0))
            terms += [(dg, a_slab, wg, w_slab, NN), (du, a_slab, wu, w_slab, NN)]
        if norm_bwd is None:
            prev = _matmul(tag + part, (steps, 1), terms, [], [(jax.ShapeDtypeStruct((s, D_MODEL), F32), row)],
                           _ident, None, fill=prev)
            continue
        h, gain, dres = norm_bwd
        vec = pl.BlockSpec((1, D_MODEL), lambda i, k: (0, 0))
        res = _matmul(
            tag + part, (steps, 1), terms, [(h, row), (gain, vec), (dres, row)],
            [(jax.ShapeDtypeStruct((s, D_MODEL), F32), row), (jax.ShapeDtypeStruct((s, D_MODEL), BF16), row),
             (jax.ShapeDtypeStruct((1, D_MODEL), F32), vec)],
            _norm_bwd_epilogue(D_MODEL), None, fill=prev, summed=(2,))
        prev = res[:2]
        dgain = res[2] if dgain is None else dgain + res[2]
    return prev[0] if norm_bwd is None else (prev[0], prev[1], dgain)


def _mm_nn(name, a, b, out_dtype, res=None, gain=None):
    s, k = a.shape
    nn = b.shape[1]
    bm = _row_block(s)
    row = pl.BlockSpec((bm, nn), lambda i, kk: (i, 0))
    term = [(a, pl.BlockSpec((bm, k), lambda i, kk: (i, 0)), b, pl.BlockSpec((k, nn), lambda i, kk: (0, 0)), NN)]
    if res is None:
        return _matmul(name, (s // bm, 1), term, [], [(jax.ShapeDtypeStruct((s, nn), out_dtype), row)], _ident, None)[0]
    extras, outs = _residual_outs(s, bm, gain)
    res_out = _matmul(name, (s // bm, 1), term, [(res, row)] + extras, outs,
                      _residual_epilogue(1.0, gain is not None), None)
    return res_out if gain is not None else res_out[0]


def _mm_nt(name, a, b, out_dtype, attn_out=None, nh=0, dv=0):
    s, nn = a.shape
    k = b.shape[0]
    bm = _row_block(s)
    term = [(a, pl.BlockSpec((bm, nn), lambda i, kk: (i, 0)), b, pl.BlockSpec((k, nn), lambda i, kk: (0, 0)), NT)]
    out = (jax.ShapeDtypeStruct((s, k), out_dtype), pl.BlockSpec((bm, k), lambda i, kk: (i, 0)))
    if attn_out is None:
        return _matmul(name, (s // bm, 1), term, [], [out], _ident, None)[0]

    def with_delta(acc, o):
        do = acc.astype(out_dtype).astype(F32)
        cols = [jnp.sum(do[:, h * dv:(h + 1) * dv] * o[:, h * dv:(h + 1) * dv].astype(F32), axis=-1, keepdims=True)
                for h in range(nh)]
        return acc, jnp.stack(cols, axis=0)

    return _matmul(
        name, (s // bm, 1), term, [(attn_out, pl.BlockSpec((bm, nh * dv), lambda i, kk: (i, 0)))],
        [out, (jax.ShapeDtypeStruct((nh, s, 1), F32), pl.BlockSpec((nh, bm, 1), lambda i, kk: (0, i, 0)))],
        with_delta, None)


def _mm_nt_norm_bwd(name, a, b, h, gain, dres):
    s, nn = a.shape
    bm = _row_block(s, 512)
    extras, outs, summed = _norm_bwd_operands(s, bm, h, gain, dres)
    return _matmul(
        name, (s // bm, 1),
        [(a, pl.BlockSpec((bm, nn), lambda i, kk: (i, 0)), b, pl.BlockSpec(b.shape, lambda i, kk: (0, 0)), NT)],
        extras, outs, _norm_bwd_epilogue(D_MODEL), None, summed=summed)


def _w_in_dx_norm_bwd(dz, w_t, h, gain, dres):
    s = dz.shape[0]
    bm = _row_block(s, 512)
    epilogue = _norm_bwd_epilogue(D_MODEL)

    def body(dz_ref, w_ref, h_ref, g_ref, r_ref, dx_ref, dxb_ref, dg_ref):
        dzv = dz_ref[...]
        dn = jnp.concatenate([_dot(dzv, w_ref[j], NN) for j in range(N_CHIPS)], axis=1)
        dx, _, dgain = epilogue(dn, h_ref[...], g_ref[...], r_ref[...])
        dx_ref[...] = dx
        dxb_ref[...] = dx.astype(BF16)

        @pl.when(pl.program_id(0) == 0)
        def _():
            dg_ref[...] = dgain

        @pl.when(pl.program_id(0) > 0)
        def _():
            dg_ref[...] += dgain

    row = pl.BlockSpec((bm, D_MODEL), lambda i: (i, 0))
    vec = pl.BlockSpec((1, D_MODEL), lambda i: (0, 0))
    return _call(
        body, name="w_in_dx", grid=(s // bm,),
        in_specs=[row, pl.BlockSpec(w_t.shape, lambda i: (0, 0, 0)), row, vec, row],
        out_specs=[row, row, vec],
        out_shape=[jax.ShapeDtypeStruct((s, D_MODEL), F32), jax.ShapeDtypeStruct((s, D_MODEL), BF16),
                   jax.ShapeDtypeStruct((1, D_MODEL), F32)],
        sem=("arbitrary",), args=[dz, w_t, h, gain, dres])


def _mm_tn(name, a, b, out_dtype=BF16):
    s, k = a.shape
    nn = b.shape[1]
    bk = _reduce_block(s)
    return _matmul(
        name, (s // bk,),
        [(a, pl.BlockSpec((bk, k), lambda kk: (kk, 0)), b, pl.BlockSpec((bk, nn), lambda kk: (kk, 0)), TN)],
        [], [(jax.ShapeDtypeStruct((k, nn), out_dtype), pl.BlockSpec((k, nn), lambda kk: (0, 0)))],
        _ident, (k, nn))[0]


def _mm_heads_fwd(name, a, w, out_dtype, w_transposed=False):
    s, k = a.shape
    nh = w.shape[0]
    nn = w.shape[1] if w_transposed else w.shape[2]
    bm = _row_block(s)
    return _matmul(
        name, (nh, s // bm, 1),
        [(a, pl.BlockSpec((bm, k), lambda h, i, kk: (i, 0)),
          w, pl.BlockSpec((None,) + w.shape[1:], lambda h, i, kk: (h, 0, 0)), NT if w_transposed else NN)],
        [], [(jax.ShapeDtypeStruct((s, nh * nn), out_dtype), pl.BlockSpec((bm, nn), lambda h, i, kk: (i, h)))],
        _ident, None)[0]


def _mm_heads_bwd(name, dy, a, w, w_transposed=False):
    s, k = a.shape
    nh = w.shape[0]
    nn = w.shape[1] if w_transposed else w.shape[2]
    bm = _row_block(s)
    bk = _reduce_block(s)
    w_spec = pl.BlockSpec((None,) + w.shape[1:], lambda i, h: (h, 0, 0))
    da = _matmul(
        name + "_dx", (s // bm, nh),
        [(dy, pl.BlockSpec((bm, nn), lambda i, h: (i, h)), w, w_spec, NN if w_transposed else NT)],
        [], [(jax.ShapeDtypeStruct((s, k), F32), pl.BlockSpec((bm, k), lambda i, h: (i, 0)))], _ident, (bm, k))[0]
    a_term = (a, pl.BlockSpec((bk, k), lambda h, kk: (kk, 0)))
    dy_term = (dy, pl.BlockSpec((bk, nn), lambda h, kk: (kk, h)))
    lhs, rhs = (dy_term, a_term) if w_transposed else (a_term, dy_term)
    dw = _matmul(
        name + "_dw", (nh, s // bk), [lhs + rhs + (TN,)],
        [], [(jax.ShapeDtypeStruct(w.shape, BF16), pl.BlockSpec((None,) + w.shape[1:], lambda h, kk: (h, 0, 0)))],
        _ident, w.shape[1:])[0]
    return da, dw


def _w_in_fwd(n, w_t):
    s = n.shape[0]
    bm = _row_block(s)
    nh, nout, kin = w_t.shape
    terms = [(n, pl.BlockSpec((bm, kin), lambda i, k, j=j: (i, j)),
              w_t, pl.BlockSpec((None, nout, kin), lambda i, k, j=j: (j, 0, 0)), NT) for j in range(nh)]
    row = pl.BlockSpec((bm, nout), lambda i, k: (i, 0))
    return _matmul("w_in", (s // bm, 1), terms, [], [(jax.ShapeDtypeStruct((s, nout), F32), row)], _ident, None)[0]


def _w_in_dw(dz, n):
    s, nout = dz.shape
    kin = n.shape[1] // N_CHIPS
    bk = _reduce_block(s)
    return _matmul(
        "w_in_dw", (N_CHIPS, s // bk),
        [(dz, pl.BlockSpec((bk, nout), lambda j, k: (k, 0)), n, pl.BlockSpec((bk, kin), lambda j, k: (k, j)), TN)],
        [], [(jax.ShapeDtypeStruct((N_CHIPS, nout, kin), BF16), pl.BlockSpec((None, nout, kin), lambda j, k: (j, 0, 0)))],
        _ident, (nout, kin))[0]


def _rope_tables(positions):
    half = ROPE_DIM // 2
    freqs = 1.0 / (ROPE_BASE ** (jnp.arange(0, ROPE_DIM, 2, dtype=F32) / ROPE_DIM))
    ang = positions.astype(F32)[:, None] * freqs
    cos, sin = jnp.cos(ang), jnp.sin(ang)
    z = jnp.zeros_like(cos)
    tc = jnp.concatenate([cos, cos, z, z], axis=-1)
    ta = jnp.concatenate([-sin, z, z, z], axis=-1)
    tb = jnp.concatenate([z, sin, z, z], axis=-1)
    assert tc.shape[-1] == 4 * half
    return tc, ta, tb


def _rope(x, tc, ta, tb):
    return x * tc + pltpu.roll(x, 96, 1) * ta + pltpu.roll(x, 32, 1) * tb


def _rope_t(dy, tc, ta, tb):
    return dy * tc + pltpu.roll(dy * ta, 32, 1) + pltpu.roll(dy * tb, 96, 1)


def _q_rope(q, tc, ta, tb, transpose):
    s = q.shape[0]
    bm = _row_block(s, 512)
    rot = _rope_t if transpose else _rope

    def body(q_ref, tc_ref, ta_ref, tb_ref, o_ref):
        c, a, b = tc_ref[...], ta_ref[...], tb_ref[...]
        for h in range(MLA_HEADS):
            lo = h * HEAD_QK
            o_ref[:, lo:lo + 128] = q_ref[:, lo:lo + 128].astype(BF16)
            o_ref[:, lo + 128:lo + 256] = rot(q_ref[:, lo + 128:lo + 256], c, a, b).astype(BF16)

    row = pl.BlockSpec((bm, MLA_HEADS * HEAD_QK), lambda i: (i, 0))
    tab = pl.BlockSpec((bm, 128), lambda i: (i, 0))
    return pl.pallas_call(
        body, name="q_rope_t" if transpose else "q_rope", grid=(s // bm,), in_specs=[row, tab, tab, tab],
        out_specs=row, out_shape=jax.ShapeDtypeStruct((s, MLA_HEADS * HEAD_QK), BF16),
        compiler_params=_params("parallel"),
    )(q, tc, ta, tb)


def _kv_assemble(kv, z, tc, ta, tb):
    s = kv.shape[0]
    bm = _row_block(s, 512)

    def body(kv_ref, kr_ref, tc_ref, ta_ref, tb_ref, k_ref, v_ref):
        kpe = _rope(kr_ref[...], tc_ref[...], ta_ref[...], tb_ref[...]).astype(BF16)
        for h in range(MLA_HEADS):
            lo = h * 256
            k_ref[:, lo:lo + 128] = kv_ref[:, lo:lo + 128].astype(BF16)
            k_ref[:, lo + 128:lo + 256] = kpe
            v_ref[:, h * 128:(h + 1) * 128] = kv_ref[:, lo + 128:lo + 256].astype(BF16)

    row = pl.BlockSpec((bm, 1024), lambda i: (i, 0))
    tab = pl.BlockSpec((bm, 128), lambda i: (i, 0))
    return pl.pallas_call(
        body, name="kv_assemble", grid=(s // bm,),
        in_specs=[row, pl.BlockSpec((bm, 128), lambda i: (i, 3)), tab, tab, tab],
        out_specs=[row, pl.BlockSpec((bm, 512), lambda i: (i, 0))],
        out_shape=[jax.ShapeDtypeStruct((s, 1024), BF16), jax.ShapeDtypeStruct((s, 512), BF16)],
        compiler_params=_params("parallel"),
    )(kv, z, tc, ta, tb)


def _kv_assemble_bwd(dk, dv, tc, ta, tb):
    s = dk.shape[0]
    bm = _row_block(s, 512)

    def body(dk_ref, dv_ref, tc_ref, ta_ref, tb_ref, dkv_ref, dkr_ref):
        dpe = None
        for h in range(MLA_HEADS):
            lo = h * 256
            dkv_ref[:, lo:lo + 128] = dk_ref[:, lo:lo + 128].astype(BF16)
            dkv_ref[:, lo + 128:lo + 256] = dv_ref[:, h * 128:(h + 1) * 128].astype(BF16)
            t = dk_ref[:, lo + 128:lo + 256]
            dpe = t if dpe is None else dpe + t
        dkr_ref[...] = _rope_t(dpe, tc_ref[...], ta_ref[...], tb_ref[...])

    row = pl.BlockSpec((bm, 1024), lambda i: (i, 0))
    tab = pl.BlockSpec((bm, 128), lambda i: (i, 0))
    return pl.pallas_call(
        body, name="kv_assemble_bwd", grid=(s // bm,),
        in_specs=[row, pl.BlockSpec((bm, 512), lambda i: (i, 0)), tab, tab, tab],
        out_specs=[row, tab],
        out_shape=[jax.ShapeDtypeStruct((s, 1024), BF16), jax.ShapeDtypeStruct((s, 128), F32)],
        compiler_params=_params("parallel"),
    )(dk, dv, tc, ta, tb)


def _norm_bf16(x, g):
    r = lax.rsqrt(jnp.mean(x * x, axis=-1, keepdims=True) + RMS_EPS)
    return ((x * r) * g).astype(BF16)


def _qkv_prep(z, q_gain, kv_gain, wq_t, wkv, tc, ta, tb):
    s = z.shape[0]
    bm = _row_block(s, 512)

    def body(zq_ref, zkv_ref, zkr_ref, qg_ref, kvg_ref, wq_ref, wkv_ref, tc_ref, ta_ref, tb_ref,
             qn_ref, kvn_ref, q_ref, k_ref, v_ref):
        c, a, b = tc_ref[...], ta_ref[...], tb_ref[...]
        qn = _norm_bf16(zq_ref[...], qg_ref[...])
        kvn = _norm_bf16(zkv_ref[...], kvg_ref[...])
        qn_ref[...] = qn
        kvn_ref[...] = kvn
        kpe = _rope(zkr_ref[...], c, a, b).astype(BF16)
        for h in range(MLA_HEADS):
            lo = h * HEAD_QK
            qp = _dot(qn, wq_ref[h], NT)
            q_ref[:, lo:lo + 128] = qp[:, :128].astype(BF16)
            q_ref[:, lo + 128:lo + 256] = _rope(qp[:, 128:], c, a, b).astype(BF16)
            kv = _dot(kvn, wkv_ref[h], NN)
            k_ref[:, lo:lo + 128] = kv[:, :128].astype(BF16)
            k_ref[:, lo + 128:lo + 256] = kpe
            v_ref[:, h * HEAD_V:(h + 1) * HEAD_V] = kv[:, 128:].astype(BF16)

    def cols(width, blk):
        return pl.BlockSpec((bm, width), lambda i: (i, blk))

    def whole(a):
        return pl.BlockSpec(a.shape, lambda i: (0,) * a.ndim)

    tab = cols(128, 0)
    return _call(
        body, name="qkv_prep", grid=(s // bm,),
        in_specs=[cols(Q_LORA, 0), cols(KV_LORA, 2), cols(128, 3), whole(q_gain), whole(kv_gain), whole(wq_t),
                  whole(wkv), tab, tab, tab],
        out_specs=[cols(Q_LORA, 0), cols(KV_LORA, 0), cols(1024, 0), cols(1024, 0), cols(512, 0)],
        out_shape=[jax.ShapeDtypeStruct((s, Q_LORA), BF16), jax.ShapeDtypeStruct((s, KV_LORA), BF16),
                   jax.ShapeDtypeStruct((s, 1024), BF16), jax.ShapeDtypeStruct((s, 1024), BF16),
                   jax.ShapeDtypeStruct((s, 512), BF16)],
        sem=("parallel",), args=[z, z, z, q_gain, kv_gain, wq_t, wkv, tc, ta, tb])


def _qkv_prep_bwd(dq, dk, dv, z, qn, kvn, q_gain, kv_gain, wq_t, wkv, tc, ta, tb):
    s = z.shape[0]
    bm = _row_block(s, 512)
    nsteps = s // bm

    def body(dq_ref, dk_ref, dv_ref, zq_ref, zkv_ref, qn_ref, kvn_ref, qg_ref, kvg_ref, wq_ref, wkv_ref,
             tc_ref, ta_ref, tb_ref, dz_ref, dqg_ref, dkvg_ref, dwq_ref, dwkv_ref, wq_acc, wkv_acc):
        i = pl.program_id(0)
        c, a, b = tc_ref[...], ta_ref[...], tb_ref[...]

        @pl.when(i == 0)
        def _():
            wq_acc[...] = jnp.zeros_like(wq_acc)
            wkv_acc[...] = jnp.zeros_like(wkv_acc)

        qn, kvn = qn_ref[...], kvn_ref[...]
        dqn = jnp.zeros((bm, Q_LORA), F32)
        dkvn = jnp.zeros((bm, KV_LORA), F32)
        dpe = jnp.zeros((bm, 128), F32)
        for h in range(MLA_HEADS):
            lo = h * HEAD_QK
            dqp = jnp.concatenate([dq_ref[:, lo:lo + 128].astype(BF16),
                                   _rope_t(dq_ref[:, lo + 128:lo + 256], c, a, b).astype(BF16)], axis=1)
            dqn = dqn + _dot(dqp, wq_ref[h], NN)
            wq_acc[h] += _dot(dqp, qn, TN)
            dkv = jnp.concatenate([dk_ref[:, lo:lo + 128].astype(BF16),
                                   dv_ref[:, h * HEAD_V:(h + 1) * HEAD_V].astype(BF16)], axis=1)
            dkvn = dkvn + _dot(dkv, wkv_ref[h], NT)
            wkv_acc[h] += _dot(kvn, dkv, TN)
            dpe = dpe + dk_ref[:, lo + 128:lo + 256]
        dcq, dqg = _rms_bwd_math(dqn, zq_ref[...], qg_ref[...], Q_LORA)
        dckv, dkvg = _rms_bwd_math(dkvn, zkv_ref[...], kvg_ref[...], KV_LORA)
        dz_ref[:, 0:Q_LORA] = dcq.astype(BF16)
        dz_ref[:, Q_LORA:Q_LORA + KV_LORA] = dckv.astype(BF16)
        dz_ref[:, Q_LORA + KV_LORA:512] = _rope_t(dpe, c, a, b).astype(BF16)

        @pl.when(i == 0)
        def _():
            dqg_ref[...] = dqg
            dkvg_ref[...] = dkvg

        @pl.when(i > 0)
        def _():
            dqg_ref[...] += dqg
            dkvg_ref[...] += dkvg

        @pl.when(i == nsteps - 1)
        def _():
            dwq_ref[...] = wq_acc[...].astype(BF16)
            dwkv_ref[...] = wkv_acc[...].astype(BF16)

    def cols(width, blk):
        return pl.BlockSpec((bm, width), lambda i: (i, blk))

    def whole(shape):
        return pl.BlockSpec(shape, lambda i: (0,) * len(shape))

    tab = cols(128, 0)
    return _call(
        body, name="qkv_prep_bwd", grid=(nsteps,),
        in_specs=[cols(1024, 0), cols(1024, 0), cols(512, 0), cols(Q_LORA, 0), cols(KV_LORA, 2), cols(Q_LORA, 0),
                  cols(KV_LORA, 0), whole(q_gain.shape), whole(kv_gain.shape), whole(wq_t.shape), whole(wkv.shape),
                  tab, tab, tab],
        out_specs=[cols(512, 0), whole(q_gain.shape), whole(kv_gain.shape), whole(wq_t.shape), whole(wkv.shape)],
        out_shape=[jax.ShapeDtypeStruct((s, 512), BF16), jax.ShapeDtypeStruct(q_gain.shape, F32),
                   jax.ShapeDtypeStruct(kv_gain.shape, F32), jax.ShapeDtypeStruct(wq_t.shape, BF16),
                   jax.ShapeDtypeStruct(wkv.shape, BF16)],
        scratch_shapes=[pltpu.VMEM(wq_t.shape, F32), pltpu.VMEM(wkv.shape, F32)],
        sem=("arbitrary",), args=[dq, dk, dv, z, z, qn, kvn, q_gain, kv_gain, wq_t, wkv, tc, ta, tb])


def _causal_mask(s, row0, col0):
    rows = row0 + lax.broadcasted_iota(jnp.int32, s.shape, 0)
    cols = col0 + lax.broadcasted_iota(jnp.int32, s.shape, 1)
    return jnp.where(cols <= rows, s, -jnp.inf)


def _attn_fwd(name, q, k, k_off, v, v_off, nh, dq, dv, scale, causal, blk):
    sq, sk = q.shape[0], k.shape[0]
    bq = min(blk, sq)
    bk = min(blk, sk)
    nkv = sk // bk
    assert not causal or (sq == sk and bq == bk)

    hq = bq
    log2e = 1.4426950408889634
    c2 = scale * log2e

    def body(q_ref, k_ref, v_ref, o_ref, lse_ref):
        qi = pl.program_id(1)
        qs = (q_ref[...],)

        def step(j, carry, masked):
            rows = pl.ds(pl.multiple_of(j * bk, bk), bk)
            kb, vb = k_ref[rows, :], v_ref[rows, :]
            out = []
            for t, (m, l, acc) in enumerate(carry):
                s = _dot(qs[t], kb, NT) * c2
                if masked:
                    s = _causal_mask(s, qi * bq + t * hq, j * bk)
                m_new = jnp.maximum(m, jnp.max(s, axis=-1, keepdims=True))
                alpha = jnp.exp2(m - m_new)
                p = jnp.exp2(s - m_new)
                l = alpha * l + jnp.sum(p, axis=-1, keepdims=True)
                acc = alpha * acc + _dot(p, vb, NN)
                out.append((m_new, l, acc))
            return tuple(out)

        one = (jnp.full((hq, 1), -jnp.inf, F32), jnp.zeros((hq, 1), F32), jnp.zeros((hq, dv), F32))
        init = (one,)
        if causal:
            carry = lax.fori_loop(0, qi, lambda j, c: step(j, c, False), init)
            fin = step(qi, carry, True)
        else:
            fin = lax.fori_loop(0, nkv, lambda j, c: step(j, c, False), init)
        for t, (m, l, acc) in enumerate(fin):
            o_ref[t * hq:(t + 1) * hq, :] = (acc / l).astype(o_ref.dtype)
            lse_ref[t * hq:(t + 1) * hq, :] = m * (1.0 / log2e) + jnp.log(l)

    return _call(
        body, name=name, grid=(nh, sq // bq),
        in_specs=[pl.BlockSpec((bq, dq), lambda h, i: (i, h)),
                  pl.BlockSpec((sk, dq), lambda h, i: (0, k_off + h)),
                  pl.BlockSpec((sk, dv), lambda h, i: (0, v_off + h))],
        out_specs=[pl.BlockSpec((bq, dv), lambda h, i: (i, h)), pl.BlockSpec((None, bq, 1), lambda h, i: (h, i, 0))],
        out_shape=[jax.ShapeDtypeStruct((sq, nh * dv), BF16), jax.ShapeDtypeStruct((nh, sq, 1), F32)],
        sem=("parallel", "parallel"), args=[q, k, v])


def _attn_delta(name, do, do_off, o, nh, dv):
    s = o.shape[0]
    bm = _row_block(s, 512)

    def body(do_ref, o_ref, d_ref):
        d_ref[...] = jnp.sum(do_ref[...].astype(F32) * o_ref[...].astype(F32), axis=-1, keepdims=True)

    return pl.pallas_call(
        body, name=name, grid=(nh, s // bm),
        in_specs=[pl.BlockSpec((bm, dv), lambda h, i: (i, do_off + h)), pl.BlockSpec((bm, dv), lambda h, i: (i, h))],
        out_specs=pl.BlockSpec((None, bm, 1), lambda h, i: (h, i, 0)),
        out_shape=jax.ShapeDtypeStruct((nh, s, 1), F32),
        compiler_params=_params("parallel", "parallel"),
    )(do, o)


def _attn_bwd(name, q, k, k_off, v, v_off, do, do_off, lse, delta, nh, dq, dv, scale, causal, blk):
    sq, sk = q.shape[0], k.shape[0]
    bq = min(blk, sq)
    bk = min(blk, sk)
    nq = sq // bq
    assert not causal or (sq == sk and bq == bk)

    def body(q_ref, k_ref, v_ref, do_ref, lse_ref, dl_ref, dq_ref, dk_ref, dv_ref, dk_acc, dv_acc):
        j = pl.program_id(1)

        @pl.when(j == 0)
        def _():
            dq_ref[...] = jnp.zeros_like(dq_ref)

        dk_acc[...] = jnp.zeros_like(dk_acc)
        dv_acc[...] = jnp.zeros_like(dv_acc)
        kv = k_ref[...]
        vv = v_ref[...]

        def step(i, masked):
            rows = pl.ds(pl.multiple_of(i * bq, bq), bq)
            qv = q_ref[rows, :]
            dov = do_ref[rows, :].astype(BF16)
            s = _dot(qv, kv, NT) * scale
            if masked:
                s = _causal_mask(s, i * bq, j * bk)
            p = jnp.exp(s - lse_ref[rows, :])
            dp = _dot(dov, vv, NT)
            ds = (p * (dp - dl_ref[rows, :]) * scale).astype(BF16)
            dv_acc[...] += _dot(p, dov, TN)
            dk_acc[...] += _dot(ds, qv, TN)
            dq_ref[rows, :] += _dot(ds, kv, NN)

        if causal:
            step(j, True)

            def loop(i, c):
                step(i, False)
                return c

            lax.fori_loop(j + 1, nq, loop, 0)
        else:
            def loop(i, c):
                step(i, False)
                return c

            lax.fori_loop(0, nq, loop, 0)
        dk_ref[...] = dk_acc[...]
        dv_ref[...] = dv_acc[...]

    stat = pl.BlockSpec((None, sq, 1), lambda h, j: (h, 0, 0))
    return _call(
        body, name=name, grid=(nh, sk // bk),
        in_specs=[pl.BlockSpec((sq, dq), lambda h, j: (0, h)),
                  pl.BlockSpec((bk, dq), lambda h, j: (j, k_off + h)),
                  pl.BlockSpec((bk, dv), lambda h, j: (j, v_off + h)),
                  pl.BlockSpec((sq, dv), lambda h, j: (0, do_off + h)), stat, stat],
        out_specs=[pl.BlockSpec((sq, dq), lambda h, j: (0, h)),
                   pl.BlockSpec((bk, dq), lambda h, j: (j, h)),
                   pl.BlockSpec((bk, dv), lambda h, j: (j, h))],
        out_shape=[jax.ShapeDtypeStruct((sq, nh * dq), F32), jax.ShapeDtypeStruct((sk, nh * dq), F32),
                   jax.ShapeDtypeStruct((sk, nh * dv), F32)],
        scratch_shapes=[pltpu.VMEM((bk, dq), F32), pltpu.VMEM((bk, dv), F32)],
        sem=("parallel", "arbitrary"), args=[q, k, v, do, lse, delta])


def _pool_diff(z, g):
    s = z.shape[0]
    t = lax.broadcasted_iota(jnp.int32, z.shape, 0)
    acc = z
    sums = []
    for k in (1, 2, 4, 8):
        acc = acc + jnp.where(t >= k, pltpu.roll(acc, k, 0), 0.0)
        sums.append(acc)
    win = jnp.where(g == 0, sums[0], jnp.where(g == 1, sums[1], jnp.where(g == 2, sums[2], sums[3])))
    w = lax.shift_left(jnp.int32(2), g)
    count = jnp.minimum(t + 1, w).astype(F32)
    del s
    return win / count - z, count


def _pool_fwd(z, pool_w, pool_scale):
    s = z.shape[0]

    def body(z_ref, w_ref, sc_ref, o_ref):
        diff, _ = _pool_diff(z_ref[...], pl.program_id(0))
        o_ref[...] = (_dot(diff, w_ref[...], NN) * sc_ref[...]).astype(o_ref.dtype)

    return _call(
        body, name="pool_fwd", grid=(POOL_GROUPS,),
        in_specs=[pl.BlockSpec((s, POOL_CH), lambda g: (0, 4 + g)),
                  pl.BlockSpec((None, POOL_CH, POOL_CH), lambda g: (g, 0, 0)),
                  pl.BlockSpec((1, POOL_CH), lambda g: (0, g))],
        out_specs=[pl.BlockSpec((s, POOL_CH), lambda g: (0, g))],
        out_shape=[jax.ShapeDtypeStruct((s, POOL_GROUPS * POOL_CH), BF16)],
        sem=("parallel",), args=[z, pool_w, pool_scale])[0]


def _pool_bwd(dcat, z, pool_w, pool_scale):
    s = z.shape[0]

    def body(dp_ref, z_ref, w_ref, sc_ref, dz_ref, dw_ref, dsc_ref):
        g = pl.program_id(0)
        diff, count = _pool_diff(z_ref[...], g)
        dpf = dp_ref[...].astype(F32)
        u = _dot(diff, w_ref[...], NN)
        dsc_ref[...] = jnp.sum(dpf * u, axis=0, keepdims=True)
        du = (dpf * sc_ref[...]).astype(BF16)
        dw_ref[...] = _dot(diff, du, TN)
        ddiff = _dot(du, w_ref[...], NT)
        t = lax.broadcasted_iota(jnp.int32, ddiff.shape, 0)
        acc = ddiff / count
        sums = []
        for k in (1, 2, 4, 8):
            acc = acc + jnp.where(t < s - k, pltpu.roll(acc, s - k, 0), 0.0)
            sums.append(acc)
        win = jnp.where(g == 0, sums[0], jnp.where(g == 1, sums[1], jnp.where(g == 2, sums[2], sums[3])))
        dz_ref[...] = win - ddiff

    return pl.pallas_call(
        body, name="pool_bwd", grid=(POOL_GROUPS,),
        in_specs=[pl.BlockSpec((s, POOL_CH), lambda g: (0, 4 + g)),
                  pl.BlockSpec((s, POOL_CH), lambda g: (0, 4 + g)),
                  pl.BlockSpec((None, POOL_CH, POOL_CH), lambda g: (g, 0, 0)),
                  pl.BlockSpec((1, POOL_CH), lambda g: (0, g))],
        out_specs=[pl.BlockSpec((s, POOL_CH), lambda g: (0, g)),
                   pl.BlockSpec((None, POOL_CH, POOL_CH), lambda g: (g, 0, 0)),
                   pl.BlockSpec((1, POOL_CH), lambda g: (0, g))],
        out_shape=[jax.ShapeDtypeStruct((s, POOL_GROUPS * POOL_CH), F32),
                   jax.ShapeDtypeStruct((POOL_GROUPS, POOL_CH, POOL_CH), F32),
                   jax.ShapeDtypeStruct((1, POOL_GROUPS * POOL_CH), F32)],
        compiler_params=_params("parallel"),
    )(dcat, z, pool_w, pool_scale)


def _local_step(x, mem, positions, target, w, grads):
    tc, ta, tb = _rope_tables(positions)
    blk = _ATT_BLOCK

    n1 = _rmsnorm_fwd("ffn1_norm", x, w["ffn1_norm"], D_MODEL)
    if "ffn1_shards" in w:
        a1, dadu1, dadg1, w["ffn1_w_gate"], w["ffn1_w_up"], w["ffn1_w_down"] = _ffn1_up_gather(n1, *w["ffn1_shards"])
    else:
        a1, dadu1, dadg1 = _ffn_up("ffn1_up", n1, w["ffn1_w_gate"], w["ffn1_w_up"])
    h1, n2 = _ffn_down("ffn1_down", a1, w["ffn1_w_down"], x, w["mix_norm"])
    z = _w_in_fwd(n2, w["w_in"])
    qn, kvn, qf, kf, vf = _qkv_prep(z, w["q_norm"], w["kv_norm"], w["w_q_up"], w["w_kv_up"], tc, ta, tb)
    att, lse = _attn_fwd("mla_fwd", qf, kf, 0, vf, 0, MLA_HEADS, HEAD_QK, HEAD_V, MLA_SCALE, True, blk)
    pool = _pool_fwd(z, w["pool_w"], w["pool_scale"])
    s = x.shape[0]
    bm = _row_block(s)
    row = pl.BlockSpec((bm, D_MODEL), lambda i, k: (i, 0))
    half = pl.BlockSpec((bm, 512), lambda i, k: (i, 0))
    h2, n3 = _matmul(
        "w_out", (s // bm, 1),
        [(att, half, w["w_out"], pl.BlockSpec((512, D_MODEL), lambda i, k: (0, 0)), NN),
         (pool, half, w["w_out"], pl.BlockSpec((512, D_MODEL), lambda i, k: (1, 0)), NN)],
        [(h1, row)] + _residual_outs(s, bm, w["xattn_norm"])[0], _residual_outs(s, bm, w["xattn_norm"])[1],
        _residual_epilogue(1.0, True), None)
    memn = _rmsnorm_fwd("mem_norm", mem, w["mem_norm"], D_MODEL)
    qm = _mm_nn("w_mq", n3, w["w_mq"], BF16)
    kvm = _mm_heads_fwd("w_mkv", memn, w["w_mkv"], BF16)
    om, lse_m = _attn_fwd("xattn_fwd", qm, kvm, 0, kvm, MEM_HEADS, MEM_HEADS, MEM_HEAD_DIM, MEM_HEAD_DIM,
                          MEM_SCALE, False, blk)
    h3, n4 = _mm_nn("w_mo", om, w["w_mo"], F32, res=h2, gain=w["ffn2_norm"])
    a2, dadu2, dadg2 = _ffn_up("ffn2_up", n4, w["ffn2_w_gate"], w["ffn2_w_up"])
    h4 = _ffn_down("ffn2_down", a2, w["ffn2_w_down"], h3)

    dh4, dh4b, loss_vec, d_final = _loss_and_final_norm(h4, w["final_norm"], target)
    grads["final_norm"] = d_final

    dh3, dh3b, grads["ffn2_norm"] = _ffn_bwd("ffn2", dh4b, n4, dadg2, dadu2, a2, w["ffn2_w_gate"], w["ffn2_w_up"],
                                             w["ffn2_w_down"], grads, norm_bwd=(h3, w["ffn2_norm"], dh4))

    dom, delta_m = _mm_nt("w_mo_dx", dh3b, w["w_mo"], BF16, attn_out=om, nh=MEM_HEADS, dv=MEM_HEAD_DIM)
    grads["w_mo"] = _mm_tn("w_mo_dw", om, dh3b)
    dqm, dkm, dvm = _attn_bwd("xattn_bwd", qm, kvm, 0, kvm, MEM_HEADS, dom, 0, lse_m, delta_m, MEM_HEADS,
                              MEM_HEAD_DIM, MEM_HEAD_DIM, MEM_SCALE, False, blk)
    dkvm = jnp.concatenate([dkm, dvm], axis=1).astype(BF16)
    dh2, dh2b, grads["xattn_norm"] = _mm_nt_norm_bwd("w_mq_dx", dqm, w["w_mq"], h2, w["xattn_norm"], dh3)
    grads["w_mq"] = _mm_tn("w_mq_dw", n3, dqm)
    dmemn, grads["w_mkv"] = _mm_heads_bwd("w_mkv", dkvm, memn, w["w_mkv"])
    _, grads["mem_norm"] = _rmsnorm_bwd("mem_norm_bwd", dmemn, mem, w["mem_norm"], D_MODEL, out_dtype=BF16)

    dcat, delta = _mm_nt("w_out_dx", dh2b, w["w_out"], BF16, attn_out=att, nh=MLA_HEADS, dv=HEAD_V)
    grads["w_out"] = jnp.concatenate([_mm_tn("w_out_dw_a", att, dh2b), _mm_tn("w_out_dw_p", pool, dh2b)], axis=0)
    dzp, grads["pool_w"], grads["pool_scale"] = _pool_bwd(dcat, z, w["pool_w"], w["pool_scale"])
    dqf, dkf, dvf = _attn_bwd("mla_bwd", qf, kf, 0, vf, 0, dcat, 0, lse, delta, MLA_HEADS, HEAD_QK, HEAD_V,
                              MLA_SCALE, True, blk)
    dz_lat, grads["q_norm"], grads["kv_norm"], grads["w_q_up"], grads["w_kv_up"] = _qkv_prep_bwd(
        dqf, dkf, dvf, z, qn, kvn, w["q_norm"], w["kv_norm"], w["w_q_up"], w["w_kv_up"], tc, ta, tb)
    dz = jnp.concatenate([dz_lat, dzp.astype(BF16)], axis=1)
    grads["w_in"] = _w_in_dw(dz, n2)
    dh1, dh1b, grads["mix_norm"] = _w_in_dx_norm_bwd(dz, w["w_in"], h1, w["mix_norm"], dh2)

    dn1 = _ffn_bwd("ffn1", dh1b, n1, dadg1, dadu1, a1, w["ffn1_w_gate"], w["ffn1_w_up"], w["ffn1_w_down"], grads)
    dx, grads["ffn1_norm"], _ = _rmsnorm_bwd("ffn1_norm_bwd", dn1, x, w["ffn1_norm"], D_MODEL, dres=dh1)
    return loss_vec[0, 0], dx


def _mesh_pos():
    x, y, c = lax.axis_index("x"), lax.axis_index("y"), lax.axis_index("c")
    chips = [(1 - x, y), (x, 1 - y), (1 - x, 1 - y)]
    chip_ids = [2 * cx + cy for cx, cy in chips]
    return x, y, c, 2 * x + y, chips, chip_ids


def _half_rows(c, rows):
    hr = rows // 2
    return pl.ds(pl.multiple_of(c * hr, 16), hr), pl.ds(pl.multiple_of((1 - c) * hr, 16), hr)


def _ag_ici_stage(shards, relative=False):
    n = len(shards)

    def copies(ins, outs):
        x, y, c, me, chips, _ = _mesh_pos()
        out = []
        for k in range(n):
            mine, _ = _half_rows(c, ins[k].shape[0])
            out.append((ins[k], outs[k].at[0 if relative else me], None))
            for j, (cx, cy) in enumerate(chips):
                slab = _REL_OF_PEER[j] if relative else me
                out.append((ins[k].at[mine], outs[k].at[slab, mine], (cx, cy, c)))
        return out

    return _Stage(shards, [jax.ShapeDtypeStruct((N_CHIPS,) + s.shape, s.dtype) for s in shards], 3 * n, n, copies)


def _ag_d2d_stage(fulls, relative=False):
    n = len(fulls)

    def copies(ins, outs):
        x, y, c, me, _, chip_ids = _mesh_pos()
        out = []
        for k in range(n):
            mine, _ = _half_rows(c, ins[k].shape[1])
            for j in range(3):
                slab = _REL_OF_PEER[j] if relative else chip_ids[j]
                out.append((ins[k].at[slab, mine], outs[k].at[slab, mine], (x, y, 1 - c)))
        return out

    return _Stage(fulls, [jax.ShapeDtypeStruct(f.shape, f.dtype) for f in fulls], 3 * n, 0, copies,
                  aliases={k: k for k in range(n)})


def _rs_swap_stage(grads):
    n = len(grads)

    def copies(ins, outs):
        x, y, c, _, _, _ = _mesh_pos()
        out = []
        for k in range(n):
            _, other = _half_rows(c, ins[k].shape[1])
            out.append((ins[k].at[:, other, :], outs[k], (x, y, 1 - c)))
        return out

    return _Stage(grads, [jax.ShapeDtypeStruct((N_CHIPS, g.shape[1] // 2, g.shape[2]), g.dtype) for g in grads],
                  n, 0, copies)


_REL_OF_PEER = (2, 1, 3)


def _rs_scatter_stage(sums, relative=False):
    n = len(sums)

    def copies(ins, outs):
        x, y, c, me, chips, chip_ids = _mesh_pos()
        out = []
        for k in range(n):
            mine, _ = _half_rows(c, 2 * ins[k].shape[1])
            out.append((ins[k].at[0 if relative else me], outs[k].at[0, mine, :], None))
            for j, (cx, cy) in enumerate(chips):
                slab = _REL_OF_PEER[j] if relative else chip_ids[j]
                out.append((ins[k].at[slab], outs[k].at[1 + j, mine, :], (cx, cy, c)))
        return out

    return _Stage(sums, [jax.ShapeDtypeStruct((N_CHIPS, 2 * s.shape[1], s.shape[2]), s.dtype) for s in sums],
                  3 * n, n, copies)


def _rs_mirror_stage(parts):
    n = len(parts)

    def copies(ins, outs):
        x, y, c, _, _, _ = _mesh_pos()
        out = []
        for k in range(n):
            mine, _ = _half_rows(c, ins[k].shape[1])
            out.append((ins[k].at[:, mine, :], outs[k].at[:, mine, :], (x, y, 1 - c)))
        return out

    return _Stage(parts, [jax.ShapeDtypeStruct(p.shape, p.dtype) for p in parts], n, 0, copies,
                  aliases={k: k for k in range(n)})


def _pair_add(name, g, r1, core):
    _, rows, cols = g.shape
    hr = rows // 2

    def body(c_ref, g_ref, r_ref, o_ref):
        o_ref[...] = (g_ref[...].astype(F32) + r_ref[...].astype(F32)).astype(BF16)

    half = pl.BlockSpec((None, hr, cols), lambda j, c: (j, 0, 0))
    return pl.pallas_call(
        body, name=name,
        grid_spec=pltpu.PrefetchScalarGridSpec(
            num_scalar_prefetch=1, grid=(N_CHIPS,),
            in_specs=[pl.BlockSpec((None, hr, cols), lambda j, c: (j, c[0], 0)), half], out_specs=half),
        out_shape=jax.ShapeDtypeStruct((N_CHIPS, hr, cols), BF16),
        compiler_params=_params("parallel"),
    )(core, g, r1)


def _all_gather_weights(shards):
    n = len(shards)

    def body(*refs):
        ins, outs = refs[:n], refs[n:2 * n]
        send, recv, loc = refs[2 * n:]
        x, y, c, me, chips, chip_ids = _mesh_pos()
        sib = (x, y, 1 - c)

        def halves(k):
            hr = ins[k].shape[0] // 2
            return pl.ds(pl.multiple_of(c * hr, 16), hr), pl.ds(pl.multiple_of((1 - c) * hr, 16), hr)

        def remote(src, dst, k, j, dev):
            return pltpu.make_async_remote_copy(src_ref=src, dst_ref=dst, send_sem=send.at[k, j],
                                                recv_sem=recv.at[k, j], device_id=dev, device_id_type=_MESH)

        started = []
        local = []
        for k in range(n):
            mine, _ = halves(k)
            cp = pltpu.make_async_copy(ins[k], outs[k].at[me], loc.at[k])
            cp.start()
            local.append(cp)
            for j, (cx, cy) in enumerate(chips):
                cp = remote(ins[k].at[mine], outs[k].at[me, mine], k, j, (cx, cy, c))
                cp.start()
                started.append(cp)
        for k in range(n):
            mine, _ = halves(k)
            for j in range(3):
                land = outs[k].at[chip_ids[j], mine]
                remote(land, land, k, j, sib).wait_recv()
                cp = remote(land, land, k, 3 + j, sib)
                cp.start()
                started.append(cp)
        for k in range(n):
            _, other = halves(k)
            for j in range(3):
                land = outs[k].at[chip_ids[j], other]
                remote(land, land, k, 3 + j, sib).wait_recv()
        for cp in started:
            cp.wait_send()
        for cp in local:
            cp.wait()

    return pl.pallas_call(
        body, name="all_gather_weights", in_specs=[_ANY] * n, out_specs=[_ANY] * n,
        out_shape=[jax.ShapeDtypeStruct((N_CHIPS,) + s.shape, s.dtype) for s in shards],
        scratch_shapes=[pltpu.SemaphoreType.DMA((n, 6)), pltpu.SemaphoreType.DMA((n, 6)),
                        pltpu.SemaphoreType.DMA((n,))],
        compiler_params=pltpu.CompilerParams(vmem_limit_bytes=V7X_VMEM_LIMIT_BYTES),
    )(*shards)


_RS_CHUNK = 32


def _reduce_scatter(name, grads):
    n = len(grads)

    def body(*refs):
        gs, outs = refs[:n], refs[n:2 * n]
        own, r1, r2, fin = (refs[(2 + i) * n:(3 + i) * n] for i in range(4))
        a_send, a_recv, b_send, b_recv, c_send, c_recv, l_in, l_out = refs[6 * n:]
        x, y, c, me, chips, chip_ids = _mesh_pos()
        sib = (x, y, 1 - c)

        def halves(k):
            hr = gs[k].shape[1] // 2
            return hr, pl.ds(pl.multiple_of(c * hr, 16), hr), pl.ds(pl.multiple_of((1 - c) * hr, 16), hr)

        def remote(src, dst, ssem, rsem, dev):
            return pltpu.make_async_remote_copy(src_ref=src, dst_ref=dst, send_sem=ssem, recv_sem=rsem,
                                                device_id=dev, device_id_type=_MESH)

        sends, locals_in = [], []
        for k in range(n):
            hr, mine, other = halves(k)
            cp = remote(gs[k].at[:, other, :], r1[k], a_send.at[k], a_recv.at[k], sib)
            cp.start()
            sends.append(cp)
            cp = pltpu.make_async_copy(gs[k].at[:, mine, :], own[k], l_in.at[k])
            cp.start()
            locals_in.append(cp)

        for k in range(n):
            hr, mine, other = halves(k)
            locals_in[k].wait()
            remote(r1[k], r1[k], a_send.at[k], a_recv.at[k], sib).wait_recv()
            for j in range(N_CHIPS):
                def add(i, carry, k=k, j=j):
                    rows = pl.ds(pl.multiple_of(i * _RS_CHUNK, _RS_CHUNK), _RS_CHUNK)
                    own[k][j, rows, :] = (own[k][j, rows, :].astype(F32) + r1[k][j, rows, :].astype(F32)).astype(BF16)
                    return carry

                lax.fori_loop(0, hr // _RS_CHUNK, add, 0)
            for j, (cx, cy) in enumerate(chips):
                cp = remote(own[k].at[chip_ids[j]], r2[k].at[j], b_send.at[k, j], b_recv.at[k, j], (cx, cy, c))
                cp.start()
                sends.append(cp)

        locals_out = []
        for k in range(n):
            hr, mine, other = halves(k)
            for j in range(3):
                remote(r2[k].at[j], r2[k].at[j], b_send.at[k, j], b_recv.at[k, j], sib).wait_recv()

            def total(i, carry, k=k):
                rows = pl.ds(pl.multiple_of(i * _RS_CHUNK, _RS_CHUNK), _RS_CHUNK)
                acc = own[k][me, rows, :].astype(F32)
                for j in range(3):
                    acc = acc + r2[k][j, rows, :].astype(F32)
                fin[k][rows, :] = acc
                return carry

            lax.fori_loop(0, hr // _RS_CHUNK, total, 0)
            cp = remote(fin[k], outs[k].at[mine, :], c_send.at[k], c_recv.at[k], sib)
            cp.start()
            sends.append(cp)
            cp = pltpu.make_async_copy(fin[k], outs[k].at[mine, :], l_out.at[k])
            cp.start()
            locals_out.append(cp)

        for k in range(n):
            hr, mine, other = halves(k)
            land = outs[k].at[other, :]
            remote(land, land, c_send.at[k], c_recv.at[k], sib).wait_recv()
        for cp in sends:
            cp.wait_send()
        for cp in locals_out:
            cp.wait()

    scratch = []
    for g in grads:
        scratch.append(pltpu.VMEM((N_CHIPS, g.shape[1] // 2, g.shape[2]), BF16))
    for g in grads:
        scratch.append(pltpu.VMEM((N_CHIPS, g.shape[1] // 2, g.shape[2]), BF16))
    for g in grads:
        scratch.append(pltpu.VMEM((3, g.shape[1] // 2, g.shape[2]), BF16))
    for g in grads:
        scratch.append(pltpu.VMEM((g.shape[1] // 2, g.shape[2]), F32))
    dma = pltpu.SemaphoreType.DMA
    scratch += [dma((n,)), dma((n,)), dma((n, 3)), dma((n, 3)), dma((n,)), dma((n,)), dma((n,)), dma((n,))]
    return pl.pallas_call(
        body, name=name, in_specs=[_ANY] * n, out_specs=[_ANY] * n,
        out_shape=[jax.ShapeDtypeStruct(g.shape[1:], F32) for g in grads],
        scratch_shapes=scratch,
        compiler_params=pltpu.CompilerParams(vmem_limit_bytes=V7X_VMEM_LIMIT_BYTES),
    )(*grads)


def _adamw_math(w, g, m, v):
    m = ADAM_B1 * m + (1.0 - ADAM_B1) * g
    v = ADAM_B2 * v + (1.0 - ADAM_B2) * (g * g)
    m_hat = m / (1.0 - ADAM_B1 ** ADAM_STEP)
    v_hat = v / (1.0 - ADAM_B2 ** ADAM_STEP)
    delta = -ADAM_LR * (m_hat / (jnp.sqrt(v_hat) + ADAM_EPS) + ADAM_WD * w)
    return delta, m, v


def _adamw_sum(name, w, parts, m, v):
    r, c = w.shape
    br = r
    while br * c * 4 > (1 << 20) and br % 32 == 0:
        br //= 2

    def body(w_ref, p_ref, m_ref, v_ref, g_ref, d_ref, nm_ref, nv_ref):
        g = p_ref[0].astype(F32)
        for j in range(1, N_CHIPS):
            g = g + p_ref[j].astype(F32)
        d, nm, nv = _adamw_math(w_ref[...], g, m_ref[...], v_ref[...])
        g_ref[...] = g
        d_ref[...] = d
        nm_ref[...] = nm
        nv_ref[...] = nv

    spec = pl.BlockSpec((br, c), lambda i: (i, 0))
    shp = jax.ShapeDtypeStruct((r, c), F32)
    return _call(
        body, name=name, grid=(r // br,),
        in_specs=[spec, pl.BlockSpec((N_CHIPS, br, c), lambda i: (0, i, 0)), spec, spec],
        out_specs=[spec] * 4, out_shape=[shp] * 4, sem=("parallel",), args=[w, parts, m, v])


_SMALL_ROWS = 80


def _small_allreduce_adamw(gpack, wpack, mpack, vpack):
    half = _SMALL_ROWS // 2

    def body(g_ref, w_ref, m_ref, v_ref, go_ref, d_ref, nm_ref, nv_ref, sib_buf, chip_sum, buf, send, recv):
        x, y, c, me, chips, chip_ids = _mesh_pos()
        sib = (x, y, 1 - c)
        mine = pl.ds(pl.multiple_of(c * half, 8), half)

        def remote(src, dst, k, dev):
            return pltpu.make_async_remote_copy(src_ref=src, dst_ref=dst, send_sem=send.at[k], recv_sem=recv.at[k],
                                                device_id=dev, device_id_type=_MESH)

        swap = remote(g_ref, sib_buf, 0, sib)
        swap.start()
        swap.wait()
        chip_sum[...] = g_ref[...] + sib_buf[...]
        buf[me] = chip_sum[...]
        sends = [remote(chip_sum.at[mine], buf.at[me, mine], 1 + j, (cx, cy, c)) for j, (cx, cy) in enumerate(chips)]
        for cp in sends:
            cp.start()
        for cp in sends:
            cp.wait()
        mirrors = [remote(buf.at[chip_ids[j], mine], buf.at[chip_ids[j], mine], 4 + j, sib) for j in range(3)]
        for cp in mirrors:
            cp.start()
        for cp in mirrors:
            cp.wait()
        total = buf[0]
        for i in range(1, N_CHIPS):
            total = total + buf[i]
        go_ref[...] = total
        d, nm, nv = _adamw_math(w_ref[...], total, m_ref[...], v_ref[...])
        d_ref[...] = d
        nm_ref[...] = nm
        nv_ref[...] = nv

    vm = pl.BlockSpec(memory_space=pltpu.VMEM)
    shp = jax.ShapeDtypeStruct((_SMALL_ROWS, D_MODEL), F32)
    return pl.pallas_call(
        body, name="small_allreduce_adamw", in_specs=[vm] * 4, out_specs=[vm] * 4, out_shape=[shp] * 4,
        scratch_shapes=[pltpu.VMEM((_SMALL_ROWS, D_MODEL), F32), pltpu.VMEM((_SMALL_ROWS, D_MODEL), F32),
                        pltpu.VMEM((N_CHIPS, _SMALL_ROWS, D_MODEL), F32), pltpu.SemaphoreType.DMA((7,)),
                        pltpu.SemaphoreType.DMA((7,))],
        compiler_params=pltpu.CompilerParams(vmem_limit_bytes=V7X_VMEM_LIMIT_BYTES),
    )(gpack, wpack, mpack, vpack)


_SMALL_VECTORS = ("ffn1_norm", "mix_norm", "xattn_norm", "mem_norm", "ffn2_norm", "final_norm", "q_norm",
                  "kv_norm", "pool_scale")


_LOSS_ROW = 9


def _pack_small(d, scalar=None):
    rows = []
    for n in _SMALL_VECTORS:
        v = d[n].reshape(1, -1).astype(F32)
        rows.append(jnp.pad(v, ((0, 0), (0, D_MODEL - v.shape[1]))))
    assert len(rows) == _LOSS_ROW
    extra = jnp.zeros((1, D_MODEL), F32) if scalar is None else jnp.pad(scalar.reshape(1, 1), ((0, 0), (0, D_MODEL - 1)))
    rows.append(extra)
    rows.append(jnp.zeros((16 - len(rows), D_MODEL), F32))
    rows.append(d["pool_w"].reshape(64, D_MODEL).astype(F32))
    return jnp.concatenate(rows, axis=0)


def _unpack_small(pack, like):
    out = {}
    for i, n in enumerate(_SMALL_VECTORS):
        out[n] = pack[i, :like[n].size].reshape(like[n].shape)
    out["pool_w"] = pack[16:].reshape(like["pool_w"].shape)
    return out


_WEIGHTS = ("ffn1_norm", "ffn1_w_gate", "ffn1_w_up", "ffn1_w_down", "mix_norm", "w_in", "q_norm", "w_q_up",
            "kv_norm", "w_kv_up", "pool_w", "pool_scale", "w_out", "xattn_norm", "mem_norm", "w_mq", "w_mkv",
            "w_mo", "ffn2_norm", "ffn2_w_gate", "ffn2_w_up", "ffn2_w_down", "final_norm")
_SHARDED = ("ffn1_w_gate", "ffn1_w_up", "ffn1_w_down", "w_in", "w_q_up", "w_kv_up", "w_out", "w_mq", "w_mkv",
            "w_mo", "ffn2_w_gate", "ffn2_w_up", "ffn2_w_down")
_RS_GROUPS = (("ffn2_w_gate", "ffn2_w_up", "ffn2_w_down"),
              ("w_mo", "w_mq", "w_mkv", "w_out", "w_q_up", "w_kv_up", "w_in"),
              ("ffn1_w_gate", "ffn1_w_up", "ffn1_w_down"))
W_IN_SPLIT = Q_LORA + KV_LORA + ROPE_DIM


_FFN1 = ("ffn1_w_gate", "ffn1_w_up", "ffn1_w_down")
_TRANSPOSED = ("ffn1_w_gate", "ffn1_w_up", "ffn2_w_gate", "ffn2_w_up", "w_in", "w_q_up")


def _local_view(name, a):
    return jnp.swapaxes(a, 1, 2)[0] if name in _TRANSPOSED else a[0]


def _global_view(name, a):
    return jnp.swapaxes(a[None], 1, 2) if name in _TRANSPOSED else a[None]


def _pad_shard(name, a):
    if name == "w_in":
        return jnp.concatenate([a[:W_IN_SPLIT], jnp.zeros((64, a.shape[1]), a.dtype), a[W_IN_SPLIT:]], axis=0)
    if name == "w_q_up":
        return jnp.pad(a, ((0, 64), (0, 0)))
    return a


def _unpad_shard(name, a):
    if name == "w_in":
        return jnp.concatenate([a[:, :W_IN_SPLIT], a[:, W_IN_SPLIT + 64:]], axis=1)
    if name == "w_q_up":
        return a[:, :192]
    return a


def _stacked(g):
    return g if g.ndim == 3 else g.reshape(N_CHIPS, g.shape[0] // N_CHIPS, g.shape[1])


class _Plan:
    AG_UNITS = (
        (("w_in", "w_q_up", "w_kv_up"), "ffn1_up", -1),
        (("w_out",), "w_in", 0),
        (("w_mq",), "qkv_prep", 0),
        (("w_mkv", "w_mo", "ffn2_w_gate"), "mla_fwd", 0),
        (("ffn2_w_up",), "xattn_fwd", 0),
        (("ffn2_w_down",), "ffn2_up", 0),
    )
    RS_UNITS = (
        (("ffn2_w_gate", "ffn2_w_up", "ffn2_w_down"), "ffn2_dn_a", "mla_bwd", "qkv_prep_bwd"),
        (("w_mo", "w_mq", "w_mkv"), "w_out_dx", "mla_bwd", "qkv_prep_bwd"),
        (("w_out", "w_q_up", "w_kv_up", "w_in"), "w_in_dx", "ffn1_dact", "ffn1_dwd"),
        (("ffn1_w_down",), "ffn1_dwg", "ffn1_dwu", "ffn1_dn_a"),
        (("ffn1_w_gate",), "ffn1_dwu", "ffn1_dn_a", "ffn1_dn_b"),
        (("ffn1_w_up",), "ffn1_dn_a", "ffn1_dn_b", "adamw_w_kv_up"),
    )
    ADAMW_ORDER = ("w_kv_up", "ffn2_w_gate", "ffn2_w_up", "ffn2_w_down", "w_mo", "w_mq", "w_mkv", "w_out", "w_q_up",
                   "w_in", "ffn1_w_down", "ffn1_w_gate", "ffn1_w_up")

    def __init__(self, shards, w, grads, core):
        self.shards, self.w, self.grads, self.core = shards, w, grads, core
        self.last_slab_step = 0
        self.parts = {}
        self.ag = [None for _ in self.AG_UNITS]
        self.rs = [[None, None, None, None] for _ in self.RS_UNITS]

    def pre(self, name):
        for i, (names, host, start) in enumerate(self.AG_UNITS):
            if name == host:
                relative = names[0] in _FFN1
                st = _ag_ici_stage([self.shards[n] for n in names], relative)
                st.then = _ag_d2d_stage(st.outs, relative)
                st.start_step = self.last_slab_step if start < 0 else start
                self.ag[i] = _host(name, st)
        for i, (names, h1, h2, h3) in enumerate(self.RS_UNITS):
            if name == h1:
                self.rs[i][0] = _host(name, _rs_swap_stage([_stacked(self.grads[n]) for n in names]))
            if name == h2:
                self.rs[i][2] = _host(name, _rs_scatter_stage(self.rs[i][1], relative=names[0] in _FFN1))
            if name == h3:
                self.rs[i][3] = _host(name, _rs_mirror_stage(self.rs[i][2].results))

    def post(self, name):
        for i, (names, host, start) in enumerate(self.AG_UNITS):
            if name == host:
                for n, f in zip(names, self.ag[i].results):
                    self.w[n] = _full_weight(n, f)
        for i, (names, h1, h2, h3) in enumerate(self.RS_UNITS):
            if name == h1:
                self.rs[i][1] = [_pair_add("pair_add_" + n, _stacked(self.grads[n]), r1, self.core)
                                 for n, r1 in zip(names, self.rs[i][0].results)]
            if name == h3:
                for n, p in zip(names, self.rs[i][3].results):
                    self.parts[n] = p


def _full_weight(name, stacked):
    if name in ("w_out", "w_mq", "w_mo"):
        return stacked.reshape(D_MODEL, D_MODEL)
    return stacked


def kernel(x, mem, positions, ffn1_norm, ffn1_w_gate, ffn1_w_up, ffn1_w_down, mix_norm, w_in, q_norm, w_q_up, kv_norm, w_kv_up, pool_w, pool_scale, w_out, xattn_norm, mem_norm, w_mq, w_mkv, w_mo, ffn2_norm, ffn2_w_gate, ffn2_w_up, ffn2_w_down, final_norm, loss_target, m_ffn1_norm, m_ffn1_w_gate, m_ffn1_w_up, m_ffn1_w_down, m_mix_norm, m_w_in, m_q_norm, m_w_q_up, m_kv_norm, m_w_kv_up, m_pool_w, m_pool_scale, m_w_out, m_xattn_norm, m_mem_norm, m_w_mq, m_w_mkv, m_w_mo, m_ffn2_norm, m_ffn2_w_gate, m_ffn2_w_up, m_ffn2_w_down, m_final_norm, v_ffn1_norm, v_ffn1_w_gate, v_ffn1_w_up, v_ffn1_w_down, v_mix_norm, v_w_in, v_q_norm, v_w_q_up, v_kv_norm, v_w_kv_up, v_pool_w, v_pool_scale, v_w_out, v_xattn_norm, v_mem_norm, v_w_mq, v_w_mkv, v_w_mo, v_ffn2_norm, v_ffn2_w_gate, v_ffn2_w_up, v_ffn2_w_down, v_final_norm):
    wts = dict(zip(_WEIGHTS, (ffn1_norm, ffn1_w_gate, ffn1_w_up, ffn1_w_down, mix_norm, w_in, q_norm, w_q_up, kv_norm, w_kv_up, pool_w, pool_scale, w_out, xattn_norm, mem_norm, w_mq, w_mkv, w_mo, ffn2_norm, ffn2_w_gate, ffn2_w_up, ffn2_w_down, final_norm)))
    mom = dict(zip(_WEIGHTS, (m_ffn1_norm, m_ffn1_w_gate, m_ffn1_w_up, m_ffn1_w_down, m_mix_norm, m_w_in, m_q_norm, m_w_q_up, m_kv_norm, m_w_kv_up, m_pool_w, m_pool_scale, m_w_out, m_xattn_norm, m_mem_norm, m_w_mq, m_w_mkv, m_w_mo, m_ffn2_norm, m_ffn2_w_gate, m_ffn2_w_up, m_ffn2_w_down, m_final_norm)))
    var = dict(zip(_WEIGHTS, (v_ffn1_norm, v_ffn1_w_gate, v_ffn1_w_up, v_ffn1_w_down, v_mix_norm, v_w_in, v_q_norm, v_w_q_up, v_kv_norm, v_w_kv_up, v_pool_w, v_pool_scale, v_w_out, v_xattn_norm, v_mem_norm, v_w_mq, v_w_mkv, v_w_mo, v_ffn2_norm, v_ffn2_w_gate, v_ffn2_w_up, v_ffn2_w_down, v_final_norm)))
    small = [n for n in _WEIGHTS if n not in _SHARDED]

    global _PLAN
    shards = {n: _pad_shard(n, _local_view(n, wts[n])).astype(BF16) for n in _SHARDED}
    w = {n: wts[n].reshape(1, -1) for n in _SMALL_VECTORS}
    w["pool_w"] = pool_w[0].astype(BF16)
    grads = {}
    core = lax.axis_index("c").astype(jnp.int32).reshape(1)
    plan = _Plan(shards, w, grads, core)
    _PLAN = plan
    try:
        w["ffn1_shards"] = tuple(shards[n] for n in _FFN1)

        loss_local, dx = _local_step(x[0], mem[0], positions[0], loss_target[0], w, grads)

        gpack, dpack, mpack, vpack = _small_allreduce_adamw(
            _pack_small({n: grads[n] for n in small}, loss_local), _pack_small({n: wts[n] for n in small}),
            _pack_small({n: mom[n] for n in small}), _pack_small({n: var[n] for n in small}))
        like = {n: wts[n] for n in small}
        g_out, d_out, m_out, v_out = (_unpack_small(p, like) for p in (gpack, dpack, mpack, vpack))
        loss = gpack[_LOSS_ROW, 0]

        for n in _Plan.ADAMW_ORDER:
            res = _adamw_sum("adamw_" + n, _local_view(n, wts[n]), _unpad_shard(n, plan.parts[n]),
                             _local_view(n, mom[n]), _local_view(n, var[n]))
            g_out[n], d_out[n], m_out[n], v_out[n] = (_global_view(n, r) for r in res)
    finally:
        _PLAN = None
        _PENDING.clear()

    return (loss, dx[None], *[g_out[n] for n in _WEIGHTS], *[d_out[n] for n in _WEIGHTS],
            *[m_out[n] for n in _WEIGHTS], *[v_out[n] for n in _WEIGHTS])
```

```python
import functools

import jax
import jax.numpy as jnp
from jax import lax
from jax.experimental import pallas as pl
from jax.experimental.pallas import tpu as pltpu

F32 = jnp.float32
BF16 = jnp.bfloat16

D_MODEL = 1024
D_FF = 2816
N_CHIPS = 4
FF_SHARD = D_FF // N_CHIPS
MLA_HEADS = 4
Q_LORA = 256
KV_LORA = 128
ROPE_DIM = 64
HEAD_QK = 256
HEAD_V = 128
POOL_GROUPS = 4
POOL_CH = 128
MEM_HEADS = 4
MEM_HEAD_DIM = 256
RMS_EPS = 1e-6
ROPE_BASE = 10000.0
MLA_SCALE = (128 + 64) ** -0.5
MEM_SCALE = MEM_HEAD_DIM ** -0.5

ADAM_LR = 0.001
ADAM_B1 = 0.9
ADAM_B2 = 0.999
ADAM_EPS = 1e-08
ADAM_WD = 0.01
ADAM_STEP = 10

V7X_VMEM_LIMIT_BYTES = 56 * 1024 * 1024

NN = ((1,), (0,))
NT = ((1,), (1,))
TN = ((0,), (0,))


def _params(*sem):
    return pltpu.CompilerParams(dimension_semantics=sem, vmem_limit_bytes=V7X_VMEM_LIMIT_BYTES)


_MESH = pl.DeviceIdType.MESH
_ANY = pl.BlockSpec(memory_space=pl.ANY)


class _Stage:
    def __init__(self, ins, outs, n_remote, n_local, copies, aliases=None):
        self.ins, self.outs, self.n_remote, self.n_local = list(ins), list(outs), n_remote, n_local
        self.copies, self.aliases = copies, dict(aliases or {})
        self.results = None
        self.start_step = 0
        self.then = None

    def descriptors(self, in_refs, out_refs, send, recv, loc):
        ds, ri, li = [], 0, 0
        for src, dst, dev in self.copies(in_refs, out_refs):
            if dev is None:
                ds.append(pltpu.make_async_copy(src, dst, loc.at[li]))
                li += 1
            else:
                ds.append(pltpu.make_async_remote_copy(src_ref=src, dst_ref=dst, send_sem=send.at[ri],
                                                       recv_sem=recv.at[ri], device_id=dev, device_id_type=_MESH))
                ri += 1
        assert ri == self.n_remote and li == self.n_local
        return ds


_PENDING = {}


def _host(name, stage):
    _PENDING.setdefault(name, []).append(stage)
    return stage


_PLAN = None


def _call(body, **kw):
    if _PLAN is not None:
        _PLAN.pre(kw["name"])
    res = _call_hosting(body, **kw)
    if _PLAN is not None:
        _PLAN.post(kw["name"])
    return res


def _call_hosting(body, *, name, grid, in_specs, out_specs, out_shape, sem, args, scratch_shapes=(), aliases=None):
    stages = _PENDING.pop(name, [])
    scratch_shapes = list(scratch_shapes)
    if not stages:
        return pl.pallas_call(body, name=name, grid=grid, in_specs=in_specs, out_specs=out_specs,
                              out_shape=out_shape, scratch_shapes=scratch_shapes,
                              input_output_aliases=dict(aliases or {}), compiler_params=_params(*sem))(*args)
    ni, no, ns = len(in_specs), len(out_shape), len(scratch_shapes)
    c_ins = [a for st in stages for a in st.ins]
    c_outs = [o for st in stages for o in st.outs]
    nci, nco = len(c_ins), len(c_outs)
    aliases, io, oo = dict(aliases or {}), 0, 0
    for st in stages:
        for i, j in st.aliases.items():
            aliases[ni + io + i] = no + oo + j
        io += len(st.ins)
        oo += len(st.outs)
    dma = pltpu.SemaphoreType.DMA
    sems = []
    for st in stages:
        sems += [dma((max(st.n_remote, 1),)), dma((max(st.n_remote, 1),)), dma((max(st.n_local, 1),))]
    followers = [st.then for st in stages if st.then is not None]
    for st in followers:
        sems += [dma((max(st.n_remote, 1),)), dma((max(st.n_remote, 1),)), dma((max(st.n_local, 1),))]

    def wrapped(*refs):
        ins, cin = refs[:ni], refs[ni:ni + nci]
        outs, cout = refs[ni + nci:ni + nci + no], refs[ni + nci + no:ni + nci + no + nco]
        scr = refs[ni + nci + no + nco:ni + nci + no + nco + ns]
        sem_refs = refs[ni + nci + no + nco + ns:]
        step = pl.program_id(0)
        last = pl.program_id(0) == grid[0] - 1
        for ax in range(1, len(grid)):
            step = step * grid[ax] + pl.program_id(ax)
            last = jnp.logical_and(last, pl.program_id(ax) == grid[ax] - 1)

        def descriptors(si):
            io = sum(len(st.ins) for st in stages[:si])
            oo = sum(len(st.outs) for st in stages[:si])
            st = stages[si]
            return st.descriptors(cin[io:io + len(st.ins)], cout[oo:oo + len(st.outs)], *sem_refs[3 * si:3 * si + 3])

        def follower_descriptors(fi):
            si = [k for k, st in enumerate(stages) if st.then is not None][fi]
            oo = sum(len(st.outs) for st in stages[:si])
            bufs = cout[oo:oo + len(stages[si].outs)]
            k0 = 3 * (len(stages) + fi)
            return followers[fi].descriptors(bufs, bufs, *sem_refs[k0:k0 + 3])

        def start(si):
            @pl.when(step == stages[si].start_step)
            def _():
                for d in descriptors(si):
                    d.start()

        for si, st in enumerate(stages):
            if st.start_step == 0:
                start(si)
        body(*ins, *outs, *scr)
        for si, st in enumerate(stages):
            if st.start_step != 0:
                start(si)

        @pl.when(last)
        def _():
            for si in range(len(stages)):
                for d in descriptors(si):
                    d.wait()
            for fi in range(len(followers)):
                for d in follower_descriptors(fi):
                    d.start()
            for fi in range(len(followers)):
                for d in follower_descriptors(fi):
                    d.wait()

    res = pl.pallas_call(
        wrapped, name=name, grid=grid, in_specs=list(in_specs) + [_ANY] * nci,
        out_specs=list(out_specs) + [_ANY] * nco, out_shape=list(out_shape) + c_outs,
        scratch_shapes=scratch_shapes + sems, input_output_aliases=aliases,
        compiler_params=_params(*(("arbitrary",) * len(grid))))(*args, *c_ins)
    oo = no
    for st in stages:
        st.results = list(res[oo:oo + len(st.outs)])
        oo += len(st.outs)
    return list(res[:no])


def _dot(a, b, dims):
    return lax.dot_general(a.astype(BF16), b.astype(BF16), (dims, ((), ())), preferred_element_type=F32)


_MAX_ROW_BLOCK = 1024
_ATT_BLOCK = 512


_MAX_REDUCE_BLOCK = 2048


def _row_block(s, want=1024):
    return min(want, s, _MAX_ROW_BLOCK)


def _reduce_block(s):
    return min(s, _MAX_REDUCE_BLOCK)


def _matmul(name, grid, terms, extras, outs, epilogue, acc_shape, fill=(), summed=()):
    nt, ne, no, nf = len(terms), len(extras), len(outs), len(fill)
    nk = grid[-1]
    dims = [t[4] for t in terms]

    def body(*refs):
        a_refs, b_refs = refs[:nt], refs[nt:2 * nt]
        e_refs = refs[2 * nt:2 * nt + ne]
        o_refs = refs[2 * nt + ne + nf:2 * nt + ne + nf + no]

        def finish(acc):
            vals = epilogue(acc, *[e[...] for e in e_refs])
            for idx, (o, val) in enumerate(zip(o_refs, vals)):
                if idx in summed:
                    @pl.when(pl.program_id(0) == 0)
                    def _(o=o, val=val):
                        o[...] = val.astype(o.dtype)

                    @pl.when(pl.program_id(0) > 0)
                    def _(o=o, val=val):
                        o[...] += val.astype(o.dtype)
                else:
                    o[...] = val.astype(o.dtype)

        if nk == 1:
            part = None
            for a, b, d in zip(a_refs, b_refs, dims):
                t = _dot(a[...], b[...], d)
                part = t if part is None else part + t
            finish(part)
        else:
            acc_ref = refs[-1]
            k = pl.program_id(len(grid) - 1)

            @pl.when(k == 0)
            def _():
                acc_ref[...] = jnp.zeros_like(acc_ref)

            for a, b, d in zip(a_refs, b_refs, dims):
                acc_ref[...] += _dot(a[...], b[...], d)

            @pl.when(k == nk - 1)
            def _():
                finish(acc_ref[...])

    in_specs = [t[1] for t in terms] + [t[3] for t in terms] + [e[1] for e in extras] + [_ANY] * nf
    args = [t[0] for t in terms] + [t[2] for t in terms] + [e[0] for e in extras] + list(fill)
    sem = ("arbitrary" if summed else "parallel",) * (len(grid) - 1) + ("arbitrary",)
    aliases = {2 * nt + ne + i: i for i in range(nf)}
    return _call(
        body, name=name, grid=grid, in_specs=in_specs,
        out_specs=[o[1] for o in outs], out_shape=[o[0] for o in outs],
        scratch_shapes=[pltpu.VMEM(acc_shape, F32)] if nk > 1 else [], sem=sem, args=args, aliases=aliases)


def _ident(acc):
    return (acc,)


def _rmsnorm_fwd(name, x, gain, width, col_block=0):
    s = x.shape[0]
    bm = _row_block(s)

    def body(x_ref, g_ref, o_ref):
        xf = x_ref[...]
        r = lax.rsqrt(jnp.mean(xf * xf, axis=-1, keepdims=True) + RMS_EPS)
        o_ref[...] = ((xf * r) * g_ref[...]).astype(o_ref.dtype)

    return pl.pallas_call(
        body, name=name, grid=(s // bm,),
        in_specs=[pl.BlockSpec((bm, width), lambda i: (i, col_block)), pl.BlockSpec((1, width), lambda i: (0, 0))],
        out_specs=pl.BlockSpec((bm, width), lambda i: (i, 0)),
        out_shape=jax.ShapeDtypeStruct((s, width), BF16),
        compiler_params=_params("parallel"),
    )(x, gain)


def _rms_bwd_math(dy, xf, g, width):
    r = lax.rsqrt(jnp.mean(xf * xf, axis=-1, keepdims=True) + RMS_EPS)
    dyg = dy * g
    dot = jnp.sum(dyg * xf, axis=-1, keepdims=True)
    dx = r * dyg - xf * ((r * r * r) * (dot * (1.0 / width)))
    dgain = jnp.sum(dy * (xf * r), axis=0, keepdims=True)
    return dx, dgain


def _rmsnorm_bwd(name, dy, x, gain, width, col_block=0, dres=None, out_dtype=F32):
    s = x.shape[0]
    bm = _row_block(s)
    has_res = dres is not None

    def body(*refs):
        if has_res:
            dy_ref, x_ref, g_ref, r_ref, dx_ref, dg_ref, dxb_ref = refs
        else:
            dy_ref, x_ref, g_ref, dx_ref, dg_ref = refs
        dx, dgain = _rms_bwd_math(dy_ref[...].astype(F32), x_ref[...], g_ref[...], width)
        if has_res:
            dx = dx + r_ref[...]
            dxb_ref[...] = dx.astype(BF16)
        dx_ref[...] = dx.astype(dx_ref.dtype)

        @pl.when(pl.program_id(0) == 0)
        def _():
            dg_ref[...] = dgain

        @pl.when(pl.program_id(0) > 0)
        def _():
            dg_ref[...] += dgain

    row = pl.BlockSpec((bm, width), lambda i: (i, 0))
    in_specs = [row, pl.BlockSpec((bm, width), lambda i: (i, col_block)), pl.BlockSpec((1, width), lambda i: (0, 0))]
    args = [dy, x, gain]
    out_specs = [row, pl.BlockSpec((1, width), lambda i: (0, 0))]
    out_shape = [jax.ShapeDtypeStruct((s, width), out_dtype), jax.ShapeDtypeStruct((1, width), F32)]
    if has_res:
        in_specs.append(row)
        args.append(dres)
        out_specs.append(row)
        out_shape.append(jax.ShapeDtypeStruct((s, width), BF16))
    return _call(body, name=name, grid=(s // bm,), in_specs=in_specs, out_specs=out_specs, out_shape=out_shape,
                 sem=("arbitrary",), args=args)


def _loss_and_final_norm(h, gain, target):
    s, d = h.shape
    bm = _row_block(s, 512)

    def body(h_ref, g_ref, t_ref, dh_ref, dhb_ref, loss_ref, dg_ref):
        xf = h_ref[...]
        g = g_ref[...]
        r = lax.rsqrt(jnp.mean(xf * xf, axis=-1, keepdims=True) + RMS_EPS)
        err = (xf * r) * g - t_ref[...]
        part = 0.5 * jnp.sum(jnp.mean(err * err, axis=-1, keepdims=True), axis=0, keepdims=True)
        dx, dgain = _rms_bwd_math(err * (1.0 / d), xf, g, d)
        dh_ref[...] = dx
        dhb_ref[...] = dx.astype(BF16)

        @pl.when(pl.program_id(0) == 0)
        def _():
            dg_ref[...] = dgain
            loss_ref[...] = jnp.broadcast_to(part, loss_ref.shape)

        @pl.when(pl.program_id(0) > 0)
        def _():
            dg_ref[...] += dgain
            loss_ref[...] += jnp.broadcast_to(part, loss_ref.shape)

    row = pl.BlockSpec((bm, d), lambda i: (i, 0))
    vec = pl.BlockSpec((1, d), lambda i: (0, 0))
    return pl.pallas_call(
        body, name="loss_final_norm", grid=(s // bm,), in_specs=[row, vec, row],
        out_specs=[row, row, pl.BlockSpec((1, 128), lambda i: (0, 0)), vec],
        out_shape=[jax.ShapeDtypeStruct((s, d), F32), jax.ShapeDtypeStruct((s, d), BF16),
                   jax.ShapeDtypeStruct((1, 128), F32),
                   jax.ShapeDtypeStruct((1, d), F32)],
        compiler_params=_params("arbitrary"),
    )(h, gain, target)


def _ffn_up(name, n, wg, wu):
    s = n.shape[0]
    bm = _row_block(s)

    def body(n_ref, wg_ref, wu_ref, a_ref, dadu_ref, dadg_ref):
        x = n_ref[...]
        g = _dot(x, wg_ref[...], NT)
        u = _dot(x, wu_ref[...], NT)
        sg = jax.nn.sigmoid(g)
        silu = g * sg
        a_ref[...] = (silu * u).astype(BF16)
        dadu_ref[...] = silu.astype(BF16)
        dadg_ref[...] = (u * (sg * (1.0 + g * (1.0 - sg)))).astype(BF16)

    w_spec = pl.BlockSpec((None, FF_SHARD, D_MODEL), lambda j, i: (j, 0, 0))
    o_spec = pl.BlockSpec((None, bm, FF_SHARD), lambda j, i: (j, i, 0))
    shp = jax.ShapeDtypeStruct((N_CHIPS, s, FF_SHARD), BF16)
    return _call(
        body, name=name, grid=(N_CHIPS, s // bm),
        in_specs=[pl.BlockSpec((bm, D_MODEL), lambda j, i: (i, 0)), w_spec, w_spec],
        out_specs=[o_spec, o_spec, o_spec], out_shape=[shp, shp, shp],
        sem=("parallel", "parallel"), args=[n, wg, wu])


def _ffn1_up_gather_direct(n, g_sh, u_sh, d_sh):
    s = n.shape[0]
    bm = _row_block(s)
    nrb = s // bm
    rows, cols = g_sh.shape
    rels = (1, 2, 3)

    def body(n_ref, gs, us, ds, a_ref, dadu_ref, dadg_ref, wg, wu, wd, gbuf, ubuf, send, recv, fsend, frecv, loc, ld):
        r, i = pl.program_id(0), pl.program_id(1)
        x, y, c = lax.axis_index("x"), lax.axis_index("y"), lax.axis_index("c")
        sib = (x, y, 1 - c)
        mine, _ = _half_rows(c, rows)
        shards, fulls, bufs = (gs, us, ds), (wg, wu, wd), (gbuf, ubuf)

        def ici(k, rel, dev=sib):
            return pltpu.make_async_remote_copy(
                src_ref=shards[k].at[mine], dst_ref=fulls[k].at[rel, mine], send_sem=send.at[k, rel - 1],
                recv_sem=recv.at[k, rel - 1], device_id=dev, device_id_type=_MESH)

        def peer(rel):
            return ((1 - x) if rel & 2 else x, (1 - y) if rel & 1 else y, c)

        def fwd(k, rel):
            return pltpu.make_async_remote_copy(
                src_ref=fulls[k].at[rel, mine], dst_ref=fulls[k].at[rel, mine], send_sem=fsend.at[k, rel - 1],
                recv_sem=frecv.at[k, rel - 1], device_id=sib, device_id_type=_MESH)

        def own(k):
            return pltpu.make_async_copy(shards[k], fulls[k].at[0], loc.at[k])

        def load(k, src):
            return pltpu.make_async_copy(src, bufs[k], ld.at[k])

        @pl.when(jnp.logical_and(r == 0, i == 0))
        def _():
            for k in range(3):
                own(k).start()
            for rel in (1, 2):
                for k in (0, 1):
                    ici(k, rel, peer(rel)).start()
            for k in (0, 1):
                load(k, shards[k]).start()
            for k in (0, 1):
                load(k, shards[k]).wait()

        @pl.when(jnp.logical_and(r > 0, i == 0))
        def _():
            for k in (0, 1):
                ici(k, r).wait_recv()
                fwd(k, r).start()
            for k in (0, 1):
                fwd(k, r).wait_recv()
                load(k, fulls[k].at[r]).start()
            for k in (0, 1):
                load(k, fulls[k].at[r]).wait()

        @pl.when(jnp.logical_and(r == 1, i == 0))
        def _():
            for k in (0, 1):
                ici(k, 3, peer(3)).start()

        @pl.when(jnp.logical_and(r == 2, i == 0))
        def _():
            for rel in (1, 2):
                ici(2, rel, peer(rel)).start()

        @pl.when(jnp.logical_and(r == 3, i == 0))
        def _():
            ici(2, 3, peer(3)).start()

        xv = n_ref[...]
        g = _dot(xv, gbuf[...], NT)
        u = _dot(xv, ubuf[...], NT)
        sg = jax.nn.sigmoid(g)
        silu = g * sg
        a_ref[...] = (silu * u).astype(BF16)
        dadu_ref[...] = silu.astype(BF16)
        dadg_ref[...] = (u * (sg * (1.0 + g * (1.0 - sg)))).astype(BF16)

        @pl.when(jnp.logical_and(r == 3, i == nrb - 1))
        def _():
            for rel in rels:
                ici(2, rel).wait_recv()
                fwd(2, rel).start()
            for rel in rels:
                fwd(2, rel).wait_recv()
            for k in range(3):
                for rel in rels:
                    ici(k, rel).wait_send()
                    fwd(k, rel).wait_send()
                own(k).wait()

    o_spec = pl.BlockSpec((None, bm, FF_SHARD), lambda r, i: (r, i, 0))
    act = jax.ShapeDtypeStruct((N_CHIPS, s, FF_SHARD), BF16)
    full = jax.ShapeDtypeStruct((N_CHIPS, rows, cols), BF16)
    dma = pltpu.SemaphoreType.DMA
    if _PLAN is not None:
        _PLAN.last_slab_step = 3 * nrb
    return _call(
        body, name="ffn1_up", grid=(N_CHIPS, nrb),
        in_specs=[pl.BlockSpec((bm, D_MODEL), lambda r, i: (i, 0)), _ANY, _ANY, _ANY],
        out_specs=[o_spec, o_spec, o_spec, _ANY, _ANY, _ANY], out_shape=[act, act, act, full, full, full],
        scratch_shapes=[pltpu.VMEM((rows, cols), BF16), pltpu.VMEM((rows, cols), BF16), dma((3, 3)), dma((3, 3)),
                        dma((3, 3)), dma((3, 3)), dma((3,)), dma((2,))],
        sem=("arbitrary", "arbitrary"), args=[n, g_sh, u_sh, d_sh])


def _ffn1_up_gather(n, g_sh, u_sh, d_sh):
    s = n.shape[0]
    bm = _row_block(s)
    nrb = s // bm
    rows, cols = g_sh.shape

    def body(n_ref, gs, us, ds, a_ref, dadu_ref, dadg_ref, wg, wu, wd, gbuf, ubuf,
             send, recv, qsend, qrecv, fsend, frecv, loc, ld):
        r, i = pl.program_id(0), pl.program_id(1)
        x, y, c = lax.axis_index("x"), lax.axis_index("y"), lax.axis_index("c")
        sib = (x, y, 1 - c)
        mine, _ = _half_rows(c, rows)
        quarters = _quarter_rows(c, rows)
        shards, fulls, bufs = (gs, us, ds), (wg, wu, wd), (gbuf, ubuf)

        def remote(src, dst, ssem, rsem, dev):
            return pltpu.make_async_remote_copy(src_ref=src, dst_ref=dst, send_sem=ssem, recv_sem=rsem,
                                                device_id=dev, device_id_type=_MESH)

        def peer(rel):
            return ((1 - x) if rel & 2 else x, (1 - y) if rel & 1 else y, c)

        def ici(k, rel, dev=sib):
            return remote(shards[k].at[mine], fulls[k].at[rel, mine], send.at[k, rel - 1], recv.at[k, rel - 1], dev)

        def quarter(k, which, dev=sib):
            slab, q = ((2, quarters[0]), (1, quarters[1]))[which]
            return remote(fulls[k].at[slab, q], fulls[k].at[3, q], qsend.at[k, which], qrecv.at[k, which], dev)

        def fwd(k, rel):
            return remote(fulls[k].at[rel, mine], fulls[k].at[rel, mine], fsend.at[k, rel - 1], frecv.at[k, rel - 1], sib)

        def own(k):
            return pltpu.make_async_copy(shards[k], fulls[k].at[0], loc.at[k])

        def load(slab):
            for k in (0, 1):
                pltpu.make_async_copy(shards[k] if slab == 0 else fulls[k].at[slab], bufs[k], ld.at[k]).start()
            for k in (0, 1):
                pltpu.make_async_copy(shards[k] if slab == 0 else fulls[k].at[slab], bufs[k], ld.at[k]).wait()

        def from_neighbour(ks, rel):
            for k in ks:
                ici(k, rel).wait_recv()
                fwd(k, rel).start()
                quarter(k, 0 if rel == 2 else 1, peer(1 if rel == 2 else 2)).start()
            for k in ks:
                fwd(k, rel).wait_recv()

        def from_diagonal(ks):
            for k in ks:
                quarter(k, 0).wait_recv()
                quarter(k, 1).wait_recv()
                fwd(k, 3).start()
            for k in ks:
                fwd(k, 3).wait_recv()

        @pl.when(jnp.logical_and(r == 0, i == 0))
        def _():
            for k in range(3):
                own(k).start()
            for rel in (1, 2):
                for k in (0, 1):
                    ici(k, rel, peer(rel)).start()
            load(0)

        @pl.when(jnp.logical_and(r == 1, i == 0))
        def _():
            from_neighbour((0, 1), 1)
            load(1)
            for rel in (1, 2):
                ici(2, rel, peer(rel)).start()

        @pl.when(jnp.logical_and(r == 2, i == 0))
        def _():
            from_neighbour((0, 1), 2)
            load(2)

        @pl.when(jnp.logical_and(r == 3, i == 0))
        def _():
            from_diagonal((0, 1))
            load(3)

        xv = n_ref[...]
        g = _dot(xv, gbuf[...], NT)
        u = _dot(xv, ubuf[...], NT)
        sg = jax.nn.sigmoid(g)
        silu = g * sg
        a_ref[...] = (silu * u).astype(BF16)
        dadu_ref[...] = silu.astype(BF16)
        dadg_ref[...] = (u * (sg * (1.0 + g * (1.0 - sg)))).astype(BF16)

        @pl.when(jnp.logical_and(r == 3, i == nrb - 1))
        def _():
            from_neighbour((2,), 1)
            from_neighbour((2,), 2)
            from_diagonal((2,))
            for k in range(3):
                for rel in (1, 2):
                    ici(k, rel).wait_send()
                for which in (0, 1):
                    quarter(k, which).wait_send()
                for rel in (1, 2, 3):
                    fwd(k, rel).wait_send()
                own(k).wait()

    o_spec = pl.BlockSpec((None, bm, FF_SHARD), lambda r, i: (r, i, 0))
    act = jax.ShapeDtypeStruct((N_CHIPS, s, FF_SHARD), BF16)
    full = jax.ShapeDtypeStruct((N_CHIPS, rows, cols), BF16)
    dma = pltpu.SemaphoreType.DMA
    if _PLAN is not None:
        _PLAN.last_slab_step = 3 * nrb
    return _call(
        body, name="ffn1_up", grid=(N_CHIPS, nrb),
        in_specs=[pl.BlockSpec((bm, D_MODEL), lambda r, i: (i, 0)), _ANY, _ANY, _ANY],
        out_specs=[o_spec, o_spec, o_spec, _ANY, _ANY, _ANY], out_shape=[act, act, act, full, full, full],
        scratch_shapes=[pltpu.VMEM((rows, cols), BF16), pltpu.VMEM((rows, cols), BF16), dma((3, 2)), dma((3, 2)),
                        dma((3, 2)), dma((3, 2)), dma((3, 3)), dma((3, 3)), dma((3,)), dma((2,))],
        sem=("arbitrary", "arbitrary"), args=[n, g_sh, u_sh, d_sh])


def _residual_epilogue(alpha, with_norm):
    if not with_norm:
        return lambda acc, r: (r + alpha * acc,)

    def epilogue(acc, r, g):
        h = r + alpha * acc
        rs = lax.rsqrt(jnp.mean(h * h, axis=-1, keepdims=True) + RMS_EPS)
        return h, (h * rs) * g

    return epilogue


def _residual_outs(s, bm, gain):
    row = pl.BlockSpec((bm, D_MODEL), lambda i, k: (i, 0))
    outs = [(jax.ShapeDtypeStruct((s, D_MODEL), F32), row)]
    if gain is None:
        return [], outs
    return [(gain, pl.BlockSpec((1, D_MODEL), lambda i, k: (0, 0)))], outs + [(jax.ShapeDtypeStruct((s, D_MODEL), BF16), row)]


def _ffn_down(name, a, wd, res, gain=None):
    s = a.shape[1]
    bm = _row_block(s, 512)
    row = pl.BlockSpec((bm, D_MODEL), lambda i, k: (i, 0))
    terms = [(a, pl.BlockSpec((None, bm, FF_SHARD), lambda i, k, j=j: (j, i, 0)),
              wd, pl.BlockSpec((None, FF_SHARD, D_MODEL), lambda i, k, j=j: (j, 0, 0)), NN) for j in range(N_CHIPS)]
    extras, outs = _residual_outs(s, bm, gain)
    res_out = _matmul(name, (s // bm, 1), terms, [(res, row)] + extras, outs,
                      _residual_epilogue(0.5, gain is not None), None)
    return res_out if gain is not None else res_out[0]


def _norm_bwd_epilogue(width):
    def epilogue(acc, h, g, dres):
        dx, dgain = _rms_bwd_math(acc, h, g, width)
        dx = dx + dres
        return dx, dx, dgain

    return epilogue


def _norm_bwd_operands(s, bm, h, gain, dres):
    row = pl.BlockSpec((bm, D_MODEL), lambda i, k: (i, 0))
    vec = pl.BlockSpec((1, D_MODEL), lambda i, k: (0, 0))
    extras = [(h, row), (gain, vec), (dres, row)]
    outs = [(jax.ShapeDtypeStruct((s, D_MODEL), F32), row), (jax.ShapeDtypeStruct((s, D_MODEL), BF16), row),
            (jax.ShapeDtypeStruct((1, D_MODEL), F32), vec)]
    return extras, outs, (2,)


def _ffn_bwd(tag, dh, n, dadg, dadu, a, wg, wu, wd, grads, norm_bwd=None):
    s = dh.shape[0]
    bm = _row_block(s)
    bk = _reduce_block(s)
    nk = s // bk

    def act_bwd(acc, dg_da, du_da):
        da = 0.5 * acc
        return da * dg_da.astype(F32), da * du_da.astype(F32)

    slab = pl.BlockSpec((None, bm, FF_SHARD), lambda j, i, k: (j, i, 0))
    shp = jax.ShapeDtypeStruct((N_CHIPS, s, FF_SHARD), BF16)
    dg, du = _matmul(
        tag + "_dact", (N_CHIPS, s // bm, 1),
        [(dh, pl.BlockSpec((bm, D_MODEL), lambda j, i, k: (i, 0)),
          wd, pl.BlockSpec((None, FF_SHARD, D_MODEL), lambda j, i, k: (j, 0, 0)), NT)],
        [(dadg, slab), (dadu, slab)], [(shp, slab), (shp, slab)], act_bwd, None)

    grads[tag + "_w_down"] = _matmul(
        tag + "_dwd", (N_CHIPS, nk),
        [(a, pl.BlockSpec((None, bk, FF_SHARD), lambda j, k: (j, k, 0)),
          dh, pl.BlockSpec((bk, D_MODEL), lambda j, k: (k, 0)), TN)],
        [], [(jax.ShapeDtypeStruct((N_CHIPS, FF_SHARD, D_MODEL), BF16),
              pl.BlockSpec((None, FF_SHARD, D_MODEL), lambda j, k: (j, 0, 0)))],
        lambda acc: (0.5 * acc,), (FF_SHARD, D_MODEL))[0]

    def dw_up(nm, dact):
        return _matmul(
            nm, (N_CHIPS, nk),
            [(dact, pl.BlockSpec((None, bk, FF_SHARD), lambda j, k: (j, k, 0)),
              n, pl.BlockSpec((bk, D_MODEL), lambda j, k: (k, 0)), TN)],
            [], [(jax.ShapeDtypeStruct((N_CHIPS, FF_SHARD, D_MODEL), BF16),
                  pl.BlockSpec((None, FF_SHARD, D_MODEL), lambda j, k: (j, 0, 0)))],
            _ident, (FF_SHARD, D_MODEL))[0]

    grads[tag + "_w_gate"] = dw_up(tag + "_dwg", dg)
    grads[tag + "_w_up"] = dw_up(tag + "_dwu", du)

    bn = _row_block(s, 512)
    steps = s // bn // 2
    prev, dgain = (), None
    for part, off in (("_dn_a", 0), ("_dn_b", steps)):
        row = pl.BlockSpec((bn, D_MODEL), lambda i, k, off=off: (i + off, 0))
        terms = []
        for j in range(N_CHIPS):
            a_slab = pl.BlockSpec((None, bn, FF_SHARD), lambda i, k, j=j, off=off: (j, i + off, 0))
            w_slab = pl.BlockSpec((None, FF_SHARD, D_MODEL), lambda i, k, j=j: (j, 0, 0))
            terms += [(dg, a_slab, wg, w_slab, NN), (du, a_slab, wu, w_slab, NN)]
        if norm_bwd is None:
            prev = _matmul(tag + part, (steps, 1), terms, [], [(jax.ShapeDtypeStruct((s, D_MODEL), F32), row)],
                           _ident, None, fill=prev)
            continue
        h, gain, dres = norm_bwd
        vec = pl.BlockSpec((1, D_MODEL), lambda i, k: (0, 0))
        res = _matmul(
            tag + part, (steps, 1), terms, [(h, row), (gain, vec), (dres, row)],
            [(jax.ShapeDtypeStruct((s, D_MODEL), F32), row), (jax.ShapeDtypeStruct((s, D_MODEL), BF16), row),
             (jax.ShapeDtypeStruct((1, D_MODEL), F32), vec)],
            _norm_bwd_epilogue(D_MODEL), None, fill=prev, summed=(2,))
        prev = res[:2]
        dgain = res[2] if dgain is None else dgain + res[2]
    return prev[0] if norm_bwd is None else (prev[0], prev[1], dgain)


def _mm_nn(name, a, b, out_dtype, res=None, gain=None):
    s, k = a.shape
    nn = b.shape[1]
    bm = _row_block(s)
    row = pl.BlockSpec((bm, nn), lambda i, kk: (i, 0))
    term = [(a, pl.BlockSpec((bm, k), lambda i, kk: (i, 0)), b, pl.BlockSpec((k, nn), lambda i, kk: (0, 0)), NN)]
    if res is None:
        return _matmul(name, (s // bm, 1), term, [], [(jax.ShapeDtypeStruct((s, nn), out_dtype), row)], _ident, None)[0]
    extras, outs = _residual_outs(s, bm, gain)
    res_out = _matmul(name, (s // bm, 1), term, [(res, row)] + extras, outs,
                      _residual_epilogue(1.0, gain is not None), None)
    return res_out if gain is not None else res_out[0]


def _mm_nt(name, a, b, out_dtype, attn_out=None, nh=0, dv=0):
    s, nn = a.shape
    k = b.shape[0]
    bm = _row_block(s)
    term = [(a, pl.BlockSpec((bm, nn), lambda i, kk: (i, 0)), b, pl.BlockSpec((k, nn), lambda i, kk: (0, 0)), NT)]
    out = (jax.ShapeDtypeStruct((s, k), out_dtype), pl.BlockSpec((bm, k), lambda i, kk: (i, 0)))
    if attn_out is None:
        return _matmul(name, (s // bm, 1), term, [], [out], _ident, None)[0]

    def with_delta(acc, o):
        do = acc.astype(out_dtype).astype(F32)
        cols = [jnp.sum(do[:, h * dv:(h + 1) * dv] * o[:, h * dv:(h + 1) * dv].astype(F32), axis=-1, keepdims=True)
                for h in range(nh)]
        return acc, jnp.stack(cols, axis=0)

    return _matmul(
        name, (s // bm, 1), term, [(attn_out, pl.BlockSpec((bm, nh * dv), lambda i, kk: (i, 0)))],
        [out, (jax.ShapeDtypeStruct((nh, s, 1), F32), pl.BlockSpec((nh, bm, 1), lambda i, kk: (0, i, 0)))],
        with_delta, None)


def _mm_nt_norm_bwd(name, a, b, h, gain, dres):
    s, nn = a.shape
    bm = _row_block(s, 512)
    extras, outs, summed = _norm_bwd_operands(s, bm, h, gain, dres)
    return _matmul(
        name, (s // bm, 1),
        [(a, pl.BlockSpec((bm, nn), lambda i, kk: (i, 0)), b, pl.BlockSpec(b.shape, lambda i, kk: (0, 0)), NT)],
        extras, outs, _norm_bwd_epilogue(D_MODEL), None, summed=summed)


def _w_in_dx_norm_bwd(dz, w_t, h, gain, dres):
    s = dz.shape[0]
    bm = _row_block(s, 512)
    epilogue = _norm_bwd_epilogue(D_MODEL)

    def body(dz_ref, w_ref, h_ref, g_ref, r_ref, dx_ref, dxb_ref, dg_ref):
        dzv = dz_ref[...]
        dn = jnp.concatenate([_dot(dzv, w_ref[j], NN) for j in range(N_CHIPS)], axis=1)
        dx, _, dgain = epilogue(dn, h_ref[...], g_ref[...], r_ref[...])
        dx_ref[...] = dx
        dxb_ref[...] = dx.astype(BF16)

        @pl.when(pl.program_id(0) == 0)
        def _():
            dg_ref[...] = dgain

        @pl.when(pl.program_id(0) > 0)
        def _():
            dg_ref[...] += dgain

    row = pl.BlockSpec((bm, D_MODEL), lambda i: (i, 0))
    vec = pl.BlockSpec((1, D_MODEL), lambda i: (0, 0))
    return _call(
        body, name="w_in_dx", grid=(s // bm,),
        in_specs=[row, pl.BlockSpec(w_t.shape, lambda i: (0, 0, 0)), row, vec, row],
        out_specs=[row, row, vec],
        out_shape=[jax.ShapeDtypeStruct((s, D_MODEL), F32), jax.ShapeDtypeStruct((s, D_MODEL), BF16),
                   jax.ShapeDtypeStruct((1, D_MODEL), F32)],
        sem=("arbitrary",), args=[dz, w_t, h, gain, dres])


def _mm_tn(name, a, b, out_dtype=BF16):
    s, k = a.shape
    nn = b.shape[1]
    bk = _reduce_block(s)
    return _matmul(
        name, (s // bk,),
        [(a, pl.BlockSpec((bk, k), lambda kk: (kk, 0)), b, pl.BlockSpec((bk, nn), lambda kk: (kk, 0)), TN)],
        [], [(jax.ShapeDtypeStruct((k, nn), out_dtype), pl.BlockSpec((k, nn), lambda kk: (0, 0)))],
        _ident, (k, nn))[0]


def _mm_heads_fwd(name, a, w, out_dtype, w_transposed=False):
    s, k = a.shape
    nh = w.shape[0]
    nn = w.shape[1] if w_transposed else w.shape[2]
    bm = _row_block(s)
    return _matmul(
        name, (nh, s // bm, 1),
        [(a, pl.BlockSpec((bm, k), lambda h, i, kk: (i, 0)),
          w, pl.BlockSpec((None,) + w.shape[1:], lambda h, i, kk: (h, 0, 0)), NT if w_transposed else NN)],
        [], [(jax.ShapeDtypeStruct((s, nh * nn), out_dtype), pl.BlockSpec((bm, nn), lambda h, i, kk: (i, h)))],
        _ident, None)[0]


def _mm_heads_bwd(name, dy, a, w, w_transposed=False):
    s, k = a.shape
    nh = w.shape[0]
    nn = w.shape[1] if w_transposed else w.shape[2]
    bm = _row_block(s)
    bk = _reduce_block(s)
    w_spec = pl.BlockSpec((None,) + w.shape[1:], lambda i, h: (h, 0, 0))
    da = _matmul(
        name + "_dx", (s // bm, nh),
        [(dy, pl.BlockSpec((bm, nn), lambda i, h: (i, h)), w, w_spec, NN if w_transposed else NT)],
        [], [(jax.ShapeDtypeStruct((s, k), F32), pl.BlockSpec((bm, k), lambda i, h: (i, 0)))], _ident, (bm, k))[0]
    a_term = (a, pl.BlockSpec((bk, k), lambda h, kk: (kk, 0)))
    dy_term = (dy, pl.BlockSpec((bk, nn), lambda h, kk: (kk, h)))
    lhs, rhs = (dy_term, a_term) if w_transposed else (a_term, dy_term)
    dw = _matmul(
        name + "_dw", (nh, s // bk), [lhs + rhs + (TN,)],
        [], [(jax.ShapeDtypeStruct(w.shape, BF16), pl.BlockSpec((None,) + w.shape[1:], lambda h, kk: (h, 0, 0)))],
        _ident, w.shape[1:])[0]
    return da, dw


def _w_in_fwd(n, w_t):
    s = n.shape[0]
    bm = _row_block(s)
    nh, nout, kin = w_t.shape
    terms = [(n, pl.BlockSpec((bm, kin), lambda i, k, j=j: (i, j)),
              w_t, pl.BlockSpec((None, nout, kin), lambda i, k, j=j: (j, 0, 0)), NT) for j in range(nh)]
    row = pl.BlockSpec((bm, nout), lambda i, k: (i, 0))
    return _matmul("w_in", (s // bm, 1), terms, [], [(jax.ShapeDtypeStruct((s, nout), F32), row)], _ident, None)[0]


def _w_in_dw(dz, n):
    s, nout = dz.shape
    kin = n.shape[1] // N_CHIPS
    bk = _reduce_block(s)
    return _matmul(
        "w_in_dw", (N_CHIPS, s // bk),
        [(dz, pl.BlockSpec((bk, nout), lambda j, k: (k, 0)), n, pl.BlockSpec((bk, kin), lambda j, k: (k, j)), TN)],
        [], [(jax.ShapeDtypeStruct((N_CHIPS, nout, kin), BF16), pl.BlockSpec((None, nout, kin), lambda j, k: (j, 0, 0)))],
        _ident, (nout, kin))[0]


def _rope_tables(positions):
    half = ROPE_DIM // 2
    freqs = 1.0 / (ROPE_BASE ** (jnp.arange(0, ROPE_DIM, 2, dtype=F32) / ROPE_DIM))
    ang = positions.astype(F32)[:, None] * freqs
    cos, sin = jnp.cos(ang), jnp.sin(ang)
    z = jnp.zeros_like(cos)
    tc = jnp.concatenate([cos, cos, z, z], axis=-1)
    ta = jnp.concatenate([-sin, z, z, z], axis=-1)
    tb = jnp.concatenate([z, sin, z, z], axis=-1)
    assert tc.shape[-1] == 4 * half
    return tc, ta, tb


def _rope(x, tc, ta, tb):
    return x * tc + pltpu.roll(x, 96, 1) * ta + pltpu.roll(x, 32, 1) * tb


def _rope_t(dy, tc, ta, tb):
    return dy * tc + pltpu.roll(dy * ta, 32, 1) + pltpu.roll(dy * tb, 96, 1)


def _q_rope(q, tc, ta, tb, transpose):
    s = q.shape[0]
    bm = _row_block(s, 512)
    rot = _rope_t if transpose else _rope

    def body(q_ref, tc_ref, ta_ref, tb_ref, o_ref):
        c, a, b = tc_ref[...], ta_ref[...], tb_ref[...]
        for h in range(MLA_HEADS):
            lo = h * HEAD_QK
            o_ref[:, lo:lo + 128] = q_ref[:, lo:lo + 128].astype(BF16)
            o_ref[:, lo + 128:lo + 256] = rot(q_ref[:, lo + 128:lo + 256], c, a, b).astype(BF16)

    row = pl.BlockSpec((bm, MLA_HEADS * HEAD_QK), lambda i: (i, 0))
    tab = pl.BlockSpec((bm, 128), lambda i: (i, 0))
    return pl.pallas_call(
        body, name="q_rope_t" if transpose else "q_rope", grid=(s // bm,), in_specs=[row, tab, tab, tab],
        out_specs=row, out_shape=jax.ShapeDtypeStruct((s, MLA_HEADS * HEAD_QK), BF16),
        compiler_params=_params("parallel"),
    )(q, tc, ta, tb)


def _kv_assemble(kv, z, tc, ta, tb):
    s = kv.shape[0]
    bm = _row_block(s, 512)

    def body(kv_ref, kr_ref, tc_ref, ta_ref, tb_ref, k_ref, v_ref):
        kpe = _rope(kr_ref[...], tc_ref[...], ta_ref[...], tb_ref[...]).astype(BF16)
        for h in range(MLA_HEADS):
            lo = h * 256
            k_ref[:, lo:lo + 128] = kv_ref[:, lo:lo + 128].astype(BF16)
            k_ref[:, lo + 128:lo + 256] = kpe
            v_ref[:, h * 128:(h + 1) * 128] = kv_ref[:, lo + 128:lo + 256].astype(BF16)

    row = pl.BlockSpec((bm, 1024), lambda i: (i, 0))
    tab = pl.BlockSpec((bm, 128), lambda i: (i, 0))
    return pl.pallas_call(
        body, name="kv_assemble", grid=(s // bm,),
        in_specs=[row, pl.BlockSpec((bm, 128), lambda i: (i, 3)), tab, tab, tab],
        out_specs=[row, pl.BlockSpec((bm, 512), lambda i: (i, 0))],
        out_shape=[jax.ShapeDtypeStruct((s, 1024), BF16), jax.ShapeDtypeStruct((s, 512), BF16)],
        compiler_params=_params("parallel"),
    )(kv, z, tc, ta, tb)


def _kv_assemble_bwd(dk, dv, tc, ta, tb):
    s = dk.shape[0]
    bm = _row_block(s, 512)

    def body(dk_ref, dv_ref, tc_ref, ta_ref, tb_ref, dkv_ref, dkr_ref):
        dpe = None
        for h in range(MLA_HEADS):
            lo = h * 256
            dkv_ref[:, lo:lo + 128] = dk_ref[:, lo:lo + 128].astype(BF16)
            dkv_ref[:, lo + 128:lo + 256] = dv_ref[:, h * 128:(h + 1) * 128].astype(BF16)
            t = dk_ref[:, lo + 128:lo + 256]
            dpe = t if dpe is None else dpe + t
        dkr_ref[...] = _rope_t(dpe, tc_ref[...], ta_ref[...], tb_ref[...])

    row = pl.BlockSpec((bm, 1024), lambda i: (i, 0))
    tab = pl.BlockSpec((bm, 128), lambda i: (i, 0))
    return pl.pallas_call(
        body, name="kv_assemble_bwd", grid=(s // bm,),
        in_specs=[row, pl.BlockSpec((bm, 512), lambda i: (i, 0)), tab, tab, tab],
        out_specs=[row, tab],
        out_shape=[jax.ShapeDtypeStruct((s, 1024), BF16), jax.ShapeDtypeStruct((s, 128), F32)],
        compiler_params=_params("parallel"),
    )(dk, dv, tc, ta, tb)


def _norm_bf16(x, g):
    r = lax.rsqrt(jnp.mean(x * x, axis=-1, keepdims=True) + RMS_EPS)
    return ((x * r) * g).astype(BF16)


def _qkv_prep(z, q_gain, kv_gain, wq_t, wkv, tc, ta, tb):
    s = z.shape[0]
    bm = _row_block(s, 512)

    def body(zq_ref, zkv_ref, zkr_ref, qg_ref, kvg_ref, wq_ref, wkv_ref, tc_ref, ta_ref, tb_ref,
             qn_ref, kvn_ref, q_ref, k_ref, v_ref):
        c, a, b = tc_ref[...], ta_ref[...], tb_ref[...]
        qn = _norm_bf16(zq_ref[...], qg_ref[...])
        kvn = _norm_bf16(zkv_ref[...], kvg_ref[...])
        qn_ref[...] = qn
        kvn_ref[...] = kvn
        kpe = _rope(zkr_ref[...], c, a, b).astype(BF16)
        for h in range(MLA_HEADS):
            lo = h * HEAD_QK
            qp = _dot(qn, wq_ref[h], NT)
            q_ref[:, lo:lo + 128] = qp[:, :128].astype(BF16)
            q_ref[:, lo + 128:lo + 256] = _rope(qp[:, 128:], c, a, b).astype(BF16)
            kv = _dot(kvn, wkv_ref[h], NN)
            k_ref[:, lo:lo + 128] = kv[:, :128].astype(BF16)
            k_ref[:, lo + 128:lo + 256] = kpe
            v_ref[:, h * HEAD_V:(h + 1) * HEAD_V] = kv[:, 128:].astype(BF16)

    def cols(width, blk):
        return pl.BlockSpec((bm, width), lambda i: (i, blk))

    def whole(a):
        return pl.BlockSpec(a.shape, lambda i: (0,) * a.ndim)

    tab = cols(128, 0)
    return _call(
        body, name="qkv_prep", grid=(s // bm,),
        in_specs=[cols(Q_LORA, 0), cols(KV_LORA, 2), cols(128, 3), whole(q_gain), whole(kv_gain), whole(wq_t),
                  whole(wkv), tab, tab, tab],
        out_specs=[cols(Q_LORA, 0), cols(KV_LORA, 0), cols(1024, 0), cols(1024, 0), cols(512, 0)],
        out_shape=[jax.ShapeDtypeStruct((s, Q_LORA), BF16), jax.ShapeDtypeStruct((s, KV_LORA), BF16),
                   jax.ShapeDtypeStruct((s, 1024), BF16), jax.ShapeDtypeStruct((s, 1024), BF16),
                   jax.ShapeDtypeStruct((s, 512), BF16)],
        sem=("parallel",), args=[z, z, z, q_gain, kv_gain, wq_t, wkv, tc, ta, tb])


def _qkv_prep_bwd(dq, dk, dv, z, qn, kvn, q_gain, kv_gain, wq_t, wkv, tc, ta, tb):
    s = z.shape[0]
    bm = _row_block(s, 512)
    nsteps = s // bm

    def body(dq_ref, dk_ref, dv_ref, zq_ref, zkv_ref, qn_ref, kvn_ref, qg_ref, kvg_ref, wq_ref, wkv_ref,
             tc_ref, ta_ref, tb_ref, dz_ref, dqg_ref, dkvg_ref, dwq_ref, dwkv_ref, wq_acc, wkv_acc):
        i = pl.program_id(0)
        c, a, b = tc_ref[...], ta_ref[...], tb_ref[...]

        @pl.when(i == 0)
        def _():
            wq_acc[...] = jnp.zeros_like(wq_acc)
            wkv_acc[...] = jnp.zeros_like(wkv_acc)

        qn, kvn = qn_ref[...], kvn_ref[...]
        dqn = jnp.zeros((bm, Q_LORA), F32)
        dkvn = jnp.zeros((bm, KV_LORA), F32)
        dpe = jnp.zeros((bm, 128), F32)
        for h in range(MLA_HEADS):
            lo = h * HEAD_QK
            dqp = jnp.concatenate([dq_ref[:, lo:lo + 128].astype(BF16),
                                   _rope_t(dq_ref[:, lo + 128:lo + 256], c, a, b).astype(BF16)], axis=1)
            dqn = dqn + _dot(dqp, wq_ref[h], NN)
            wq_acc[h] += _dot(dqp, qn, TN)
            dkv = jnp.concatenate([dk_ref[:, lo:lo + 128].astype(BF16),
                                   dv_ref[:, h * HEAD_V:(h + 1) * HEAD_V].astype(BF16)], axis=1)
            dkvn = dkvn + _dot(dkv, wkv_ref[h], NT)
            wkv_acc[h] += _dot(kvn, dkv, TN)
            dpe = dpe + dk_ref[:, lo + 128:lo + 256]
        dcq, dqg = _rms_bwd_math(dqn, zq_ref[...], qg_ref[...], Q_LORA)
        dckv, dkvg = _rms_bwd_math(dkvn, zkv_ref[...], kvg_ref[...], KV_LORA)
        dz_ref[:, 0:Q_LORA] = dcq.astype(BF16)
        dz_ref[:, Q_LORA:Q_LORA + KV_LORA] = dckv.astype(BF16)
        dz_ref[:, Q_LORA + KV_LORA:512] = _rope_t(dpe, c, a, b).astype(BF16)

        @pl.when(i == 0)
        def _():
            dqg_ref[...] = dqg
            dkvg_ref[...] = dkvg

        @pl.when(i > 0)
        def _():
            dqg_ref[...] += dqg
            dkvg_ref[...] += dkvg

        @pl.when(i == nsteps - 1)
        def _():
            dwq_ref[...] = wq_acc[...].astype(BF16)
            dwkv_ref[...] = wkv_acc[...].astype(BF16)

    def cols(width, blk):
        return pl.BlockSpec((bm, width), lambda i: (i, blk))

    def whole(shape):
        return pl.BlockSpec(shape, lambda i: (0,) * len(shape))

    tab = cols(128, 0)
    return _call(
        body, name="qkv_prep_bwd", grid=(nsteps,),
        in_specs=[cols(1024, 0), cols(1024, 0), cols(512, 0), cols(Q_LORA, 0), cols(KV_LORA, 2), cols(Q_LORA, 0),
                  cols(KV_LORA, 0), whole(q_gain.shape), whole(kv_gain.shape), whole(wq_t.shape), whole(wkv.shape),
                  tab, tab, tab],
        out_specs=[cols(512, 0), whole(q_gain.shape), whole(kv_gain.shape), whole(wq_t.shape), whole(wkv.shape)],
        out_shape=[jax.ShapeDtypeStruct((s, 512), BF16), jax.ShapeDtypeStruct(q_gain.shape, F32),
                   jax.ShapeDtypeStruct(kv_gain.shape, F32), jax.ShapeDtypeStruct(wq_t.shape, BF16),
                   jax.ShapeDtypeStruct(wkv.shape, BF16)],
        scratch_shapes=[pltpu.VMEM(wq_t.shape, F32), pltpu.VMEM(wkv.shape, F32)],
        sem=("arbitrary",), args=[dq, dk, dv, z, z, qn, kvn, q_gain, kv_gain, wq_t, wkv, tc, ta, tb])


def _causal_mask(s, row0, col0):
    rows = row0 + lax.broadcasted_iota(jnp.int32, s.shape, 0)
    cols = col0 + lax.broadcasted_iota(jnp.int32, s.shape, 1)
    return jnp.where(cols <= rows, s, -jnp.inf)


def _attn_fwd(name, q, k, k_off, v, v_off, nh, dq, dv, scale, causal, blk):
    sq, sk = q.shape[0], k.shape[0]
    bq = min(blk, sq)
    bk = min(blk, sk)
    nkv = sk // bk
    assert not causal or (sq == sk and bq == bk)

    hq = bq
    log2e = 1.4426950408889634
    c2 = scale * log2e

    def body(q_ref, k_ref, v_ref, o_ref, lse_ref):
        qi = pl.program_id(1)
        qs = (q_ref[...],)

        def step(j, carry, masked):
            rows = pl.ds(pl.multiple_of(j * bk, bk), bk)
            kb, vb = k_ref[rows, :], v_ref[rows, :]
            out = []
            for t, (m, l, acc) in enumerate(carry):
                s = _dot(qs[t], kb, NT) * c2
                if masked:
                    s = _causal_mask(s, qi * bq + t * hq, j * bk)
                m_new = jnp.maximum(m, jnp.max(s, axis=-1, keepdims=True))
                alpha = jnp.exp2(m - m_new)
                p = jnp.exp2(s - m_new)
                l = alpha * l + jnp.sum(p, axis=-1, keepdims=True)
                acc = alpha * acc + _dot(p, vb, NN)
                out.append((m_new, l, acc))
            return tuple(out)

        one = (jnp.full((hq, 1), -jnp.inf, F32), jnp.zeros((hq, 1), F32), jnp.zeros((hq, dv), F32))
        init = (one,)
        if causal:
            carry = lax.fori_loop(0, qi, lambda j, c: step(j, c, False), init)
            fin = step(qi, carry, True)
        else:
            fin = lax.fori_loop(0, nkv, lambda j, c: step(j, c, False), init)
        for t, (m, l, acc) in enumerate(fin):
            o_ref[t * hq:(t + 1) * hq, :] = (acc / l).astype(o_ref.dtype)
            lse_ref[t * hq:(t + 1) * hq, :] = m * (1.0 / log2e) + jnp.log(l)

    return _call(
        body, name=name, grid=(nh, sq // bq),
        in_specs=[pl.BlockSpec((bq, dq), lambda h, i: (i, h)),
                  pl.BlockSpec((sk, dq), lambda h, i: (0, k_off + h)),
                  pl.BlockSpec((sk, dv), lambda h, i: (0, v_off + h))],
        out_specs=[pl.BlockSpec((bq, dv), lambda h, i: (i, h)), pl.BlockSpec((None, bq, 1), lambda h, i: (h, i, 0))],
        out_shape=[jax.ShapeDtypeStruct((sq, nh * dv), BF16), jax.ShapeDtypeStruct((nh, sq, 1), F32)],
        sem=("parallel", "parallel"), args=[q, k, v])


def _attn_delta(name, do, do_off, o, nh, dv):
    s = o.shape[0]
    bm = _row_block(s, 512)

    def body(do_ref, o_ref, d_ref):
        d_ref[...] = jnp.sum(do_ref[...].astype(F32) * o_ref[...].astype(F32), axis=-1, keepdims=True)

    return pl.pallas_call(
        body, name=name, grid=(nh, s // bm),
        in_specs=[pl.BlockSpec((bm, dv), lambda h, i: (i, do_off + h)), pl.BlockSpec((bm, dv), lambda h, i: (i, h))],
        out_specs=pl.BlockSpec((None, bm, 1), lambda h, i: (h, i, 0)),
        out_shape=jax.ShapeDtypeStruct((nh, s, 1), F32),
        compiler_params=_params("parallel", "parallel"),
    )(do, o)


def _attn_bwd(name, q, k, k_off, v, v_off, do, do_off, lse, delta, nh, dq, dv, scale, causal, blk):
    sq, sk = q.shape[0], k.shape[0]
    bq = min(blk, sq)
    bk = min(blk, sk)
    nq = sq // bq
    assert not causal or (sq == sk and bq == bk)

    def body(q_ref, k_ref, v_ref, do_ref, lse_ref, dl_ref, dq_ref, dk_ref, dv_ref, dk_acc, dv_acc):
        j = pl.program_id(1)

        @pl.when(j == 0)
        def _():
            dq_ref[...] = jnp.zeros_like(dq_ref)

        dk_acc[...] = jnp.zeros_like(dk_acc)
        dv_acc[...] = jnp.zeros_like(dv_acc)
        kv = k_ref[...]
        vv = v_ref[...]

        def step(i, masked):
            rows = pl.ds(pl.multiple_of(i * bq, bq), bq)
            qv = q_ref[rows, :]
            dov = do_ref[rows, :].astype(BF16)
            s = _dot(qv, kv, NT) * scale
            if masked:
                s = _causal_mask(s, i * bq, j * bk)
            p = jnp.exp(s - lse_ref[rows, :])
            dp = _dot(dov, vv, NT)
            ds = (p * (dp - dl_ref[rows, :]) * scale).astype(BF16)
            dv_acc[...] += _dot(p, dov, TN)
            dk_acc[...] += _dot(ds, qv, TN)
            dq_ref[rows, :] += _dot(ds, kv, NN)

        if causal:
            step(j, True)

            def loop(i, c):
                step(i, False)
                return c

            lax.fori_loop(j + 1, nq, loop, 0)
        else:
            def loop(i, c):
                step(i, False)
                return c

            lax.fori_loop(0, nq, loop, 0)
        dk_ref[...] = dk_acc[...]
        dv_ref[...] = dv_acc[...]

    stat = pl.BlockSpec((None, sq, 1), lambda h, j: (h, 0, 0))
    return _call(
        body, name=name, grid=(nh, sk // bk),
        in_specs=[pl.BlockSpec((sq, dq), lambda h, j: (0, h)),
                  pl.BlockSpec((bk, dq), lambda h, j: (j, k_off + h)),
                  pl.BlockSpec((bk, dv), lambda h, j: (j, v_off + h)),
                  pl.BlockSpec((sq, dv), lambda h, j: (0, do_off + h)), stat, stat],
        out_specs=[pl.BlockSpec((sq, dq), lambda h, j: (0, h)),
                   pl.BlockSpec((bk, dq), lambda h, j: (j, h)),
                   pl.BlockSpec((bk, dv), lambda h, j: (j, h))],
        out_shape=[jax.ShapeDtypeStruct((sq, nh * dq), F32), jax.ShapeDtypeStruct((sk, nh * dq), F32),
                   jax.ShapeDtypeStruct((sk, nh * dv), F32)],
        scratch_shapes=[pltpu.VMEM((bk, dq), F32), pltpu.VMEM((bk, dv), F32)],
        sem=("parallel", "arbitrary"), args=[q, k, v, do, lse, delta])


def _pool_diff(z, g):
    s = z.shape[0]
    t = lax.broadcasted_iota(jnp.int32, z.shape, 0)
    acc = z
    sums = []
    for k in (1, 2, 4, 8):
        acc = acc + jnp.where(t >= k, pltpu.roll(acc, k, 0), 0.0)
        sums.append(acc)
    win = jnp.where(g == 0, sums[0], jnp.where(g == 1, sums[1], jnp.where(g == 2, sums[2], sums[3])))
    w = lax.shift_left(jnp.int32(2), g)
    count = jnp.minimum(t + 1, w).astype(F32)
    del s
    return win / count - z, count


def _pool_fwd(z, pool_w, pool_scale):
    s = z.shape[0]

    def body(z_ref, w_ref, sc_ref, o_ref):
        diff, _ = _pool_diff(z_ref[...], pl.program_id(0))
        o_ref[...] = (_dot(diff, w_ref[...], NN) * sc_ref[...]).astype(o_ref.dtype)

    return _call(
        body, name="pool_fwd", grid=(POOL_GROUPS,),
        in_specs=[pl.BlockSpec((s, POOL_CH), lambda g: (0, 4 + g)),
                  pl.BlockSpec((None, POOL_CH, POOL_CH), lambda g: (g, 0, 0)),
                  pl.BlockSpec((1, POOL_CH), lambda g: (0, g))],
        out_specs=[pl.BlockSpec((s, POOL_CH), lambda g: (0, g))],
        out_shape=[jax.ShapeDtypeStruct((s, POOL_GROUPS * POOL_CH), BF16)],
        sem=("parallel",), args=[z, pool_w, pool_scale])[0]


def _pool_bwd(dcat, z, pool_w, pool_scale):
    s = z.shape[0]

    def body(dp_ref, z_ref, w_ref, sc_ref, dz_ref, dw_ref, dsc_ref):
        g = pl.program_id(0)
        diff, count = _pool_diff(z_ref[...], g)
        dpf = dp_ref[...].astype(F32)
        u = _dot(diff, w_ref[...], NN)
        dsc_ref[...] = jnp.sum(dpf * u, axis=0, keepdims=True)
        du = (dpf * sc_ref[...]).astype(BF16)
        dw_ref[...] = _dot(diff, du, TN)
        ddiff = _dot(du, w_ref[...], NT)
        t = lax.broadcasted_iota(jnp.int32, ddiff.shape, 0)
        acc = ddiff / count
        sums = []
        for k in (1, 2, 4, 8):
            acc = acc + jnp.where(t < s - k, pltpu.roll(acc, s - k, 0), 0.0)
            sums.append(acc)
        win = jnp.where(g == 0, sums[0], jnp.where(g == 1, sums[1], jnp.where(g == 2, sums[2], sums[3])))
        dz_ref[...] = win - ddiff

    return pl.pallas_call(
        body, name="pool_bwd", grid=(POOL_GROUPS,),
        in_specs=[pl.BlockSpec((s, POOL_CH), lambda g: (0, 4 + g)),
                  pl.BlockSpec((s, POOL_CH), lambda g: (0, 4 + g)),
                  pl.BlockSpec((None, POOL_CH, POOL_CH), lambda g: (g, 0, 0)),
                  pl.BlockSpec((1, POOL_CH), lambda g: (0, g))],
        out_specs=[pl.BlockSpec((s, POOL_CH), lambda g: (0, g)),
                   pl.BlockSpec((None, POOL_CH, POOL_CH), lambda g: (g, 0, 0)),
                   pl.BlockSpec((1, POOL_CH), lambda g: (0, g))],
        out_shape=[jax.ShapeDtypeStruct((s, POOL_GROUPS * POOL_CH), F32),
                   jax.ShapeDtypeStruct((POOL_GROUPS, POOL_CH, POOL_CH), F32),
                   jax.ShapeDtypeStruct((1, POOL_GROUPS * POOL_CH), F32)],
        compiler_params=_params("parallel"),
    )(dcat, z, pool_w, pool_scale)


def _local_step(x, mem, positions, target, w, grads):
    tc, ta, tb = _rope_tables(positions)
    blk = _ATT_BLOCK

    n1 = _rmsnorm_fwd("ffn1_norm", x, w["ffn1_norm"], D_MODEL)
    if "ffn1_shards" in w:
        a1, dadu1, dadg1, w["ffn1_w_gate"], w["ffn1_w_up"], w["ffn1_w_down"] = _ffn1_up_gather(n1, *w["ffn1_shards"])
    else:
        a1, dadu1, dadg1 = _ffn_up("ffn1_up", n1, w["ffn1_w_gate"], w["ffn1_w_up"])
    h1, n2 = _ffn_down("ffn1_down", a1, w["ffn1_w_down"], x, w["mix_norm"])
    z = _w_in_fwd(n2, w["w_in"])
    qn, kvn, qf, kf, vf = _qkv_prep(z, w["q_norm"], w["kv_norm"], w["w_q_up"], w["w_kv_up"], tc, ta, tb)
    att, lse = _attn_fwd("mla_fwd", qf, kf, 0, vf, 0, MLA_HEADS, HEAD_QK, HEAD_V, MLA_SCALE, True, blk)
    pool = _pool_fwd(z, w["pool_w"], w["pool_scale"])
    s = x.shape[0]
    bm = _row_block(s)
    row = pl.BlockSpec((bm, D_MODEL), lambda i, k: (i, 0))
    half = pl.BlockSpec((bm, 512), lambda i, k: (i, 0))
    h2, n3 = _matmul(
        "w_out", (s // bm, 1),
        [(att, half, w["w_out"], pl.BlockSpec((512, D_MODEL), lambda i, k: (0, 0)), NN),
         (pool, half, w["w_out"], pl.BlockSpec((512, D_MODEL), lambda i, k: (1, 0)), NN)],
        [(h1, row)] + _residual_outs(s, bm, w["xattn_norm"])[0], _residual_outs(s, bm, w["xattn_norm"])[1],
        _residual_epilogue(1.0, True), None)
    memn = _rmsnorm_fwd("mem_norm", mem, w["mem_norm"], D_MODEL)
    qm = _mm_nn("w_mq", n3, w["w_mq"], BF16)
    kvm = _mm_heads_fwd("w_mkv", memn, w["w_mkv"], BF16)
    om, lse_m = _attn_fwd("xattn_fwd", qm, kvm, 0, kvm, MEM_HEADS, MEM_HEADS, MEM_HEAD_DIM, MEM_HEAD_DIM,
                          MEM_SCALE, False, blk)
    h3, n4 = _mm_nn("w_mo", om, w["w_mo"], F32, res=h2, gain=w["ffn2_norm"])
    a2, dadu2, dadg2 = _ffn_up("ffn2_up", n4, w["ffn2_w_gate"], w["ffn2_w_up"])
    h4 = _ffn_down("ffn2_down", a2, w["ffn2_w_down"], h3)

    dh4, dh4b, loss_vec, d_final = _loss_and_final_norm(h4, w["final_norm"], target)
    grads["final_norm"] = d_final

    dh3, dh3b, grads["ffn2_norm"] = _ffn_bwd("ffn2", dh4b, n4, dadg2, dadu2, a2, w["ffn2_w_gate"], w["ffn2_w_up"],
                                             w["ffn2_w_down"], grads, norm_bwd=(h3, w["ffn2_norm"], dh4))

    dom, delta_m = _mm_nt("w_mo_dx", dh3b, w["w_mo"], BF16, attn_out=om, nh=MEM_HEADS, dv=MEM_HEAD_DIM)
    grads["w_mo"] = _mm_tn("w_mo_dw", om, dh3b)
    dqm, dkm, dvm = _attn_bwd("xattn_bwd", qm, kvm, 0, kvm, MEM_HEADS, dom, 0, lse_m, delta_m, MEM_HEADS,
                              MEM_HEAD_DIM, MEM_HEAD_DIM, MEM_SCALE, False, blk)
    dkvm = jnp.concatenate([dkm, dvm], axis=1).astype(BF16)
    dh2, dh2b, grads["xattn_norm"] = _mm_nt_norm_bwd("w_mq_dx", dqm, w["w_mq"], h2, w["xattn_norm"], dh3)
    grads["w_mq"] = _mm_tn("w_mq_dw", n3, dqm)
    dmemn, grads["w_mkv"] = _mm_heads_bwd("w_mkv", dkvm, memn, w["w_mkv"])
    _, grads["mem_norm"] = _rmsnorm_bwd("mem_norm_bwd", dmemn, mem, w["mem_norm"], D_MODEL, out_dtype=BF16)

    dcat, delta = _mm_nt("w_out_dx", dh2b, w["w_out"], BF16, attn_out=att, nh=MLA_HEADS, dv=HEAD_V)
    grads["w_out"] = jnp.concatenate([_mm_tn("w_out_dw_a", att, dh2b), _mm_tn("w_out_dw_p", pool, dh2b)], axis=0)
    dzp, grads["pool_w"], grads["pool_scale"] = _pool_bwd(dcat, z, w["pool_w"], w["pool_scale"])
    dqf, dkf, dvf = _attn_bwd("mla_bwd", qf, kf, 0, vf, 0, dcat, 0, lse, delta, MLA_HEADS, HEAD_QK, HEAD_V,
                              MLA_SCALE, True, blk)
    dz_lat, grads["q_norm"], grads["kv_norm"], grads["w_q_up"], grads["w_kv_up"] = _qkv_prep_bwd(
        dqf, dkf, dvf, z, qn, kvn, w["q_norm"], w["kv_norm"], w["w_q_up"], w["w_kv_up"], tc, ta, tb)
    dz = jnp.concatenate([dz_lat, dzp.astype(BF16)], axis=1)
    grads["w_in"] = _w_in_dw(dz, n2)
    dh1, dh1b, grads["mix_norm"] = _w_in_dx_norm_bwd(dz, w["w_in"], h1, w["mix_norm"], dh2)

    dn1 = _ffn_bwd("ffn1", dh1b, n1, dadg1, dadu1, a1, w["ffn1_w_gate"], w["ffn1_w_up"], w["ffn1_w_down"], grads)
    dx, grads["ffn1_norm"], _ = _rmsnorm_bwd("ffn1_norm_bwd", dn1, x, w["ffn1_norm"], D_MODEL, dres=dh1)
    return loss_vec[0, 0], dx


def _mesh_pos():
    x, y, c = lax.axis_index("x"), lax.axis_index("y"), lax.axis_index("c")
    chips = [(1 - x, y), (x, 1 - y), (1 - x, 1 - y)]
    chip_ids = [2 * cx + cy for cx, cy in chips]
    return x, y, c, 2 * x + y, chips, chip_ids


def _half_rows(c, rows):
    hr = rows // 2
    return pl.ds(pl.multiple_of(c * hr, 16), hr), pl.ds(pl.multiple_of((1 - c) * hr, 16), hr)


def _ag_ici_stage(shards, relative=False):
    n = len(shards)

    def copies(ins, outs):
        x, y, c, me, chips, _ = _mesh_pos()
        out = []
        for k in range(n):
            mine, _ = _half_rows(c, ins[k].shape[0])
            out.append((ins[k], outs[k].at[0 if relative else me], None))
            for j, (cx, cy) in enumerate(chips[:2]):
                slab = _REL_OF_PEER[j] if relative else me
                out.append((ins[k].at[mine], outs[k].at[slab, mine], (cx, cy, c)))
        return out

    return _Stage(shards, [jax.ShapeDtypeStruct((N_CHIPS,) + s.shape, s.dtype) for s in shards], 2 * n, n, copies)


def _quarter_rows(c, rows):
    qr = rows // 4
    return pl.ds(pl.multiple_of(c * 2 * qr, 16), qr), pl.ds(pl.multiple_of(c * 2 * qr + qr, 16), qr)


def _ag_fwd_stage(fulls, relative=False):
    n = len(fulls)

    def copies(ins, outs):
        x, y, c, me, chips, chip_ids = _mesh_pos()
        out = []
        for k in range(n):
            q0, q1 = _quarter_rows(c, ins[k].shape[1])
            from_x, from_y, diag = (2, 1, 3) if relative else chip_ids
            out.append((ins[k].at[from_x, q0], outs[k].at[diag if relative else from_x, q0], (*chips[1], c)))
            out.append((ins[k].at[from_y, q1], outs[k].at[diag if relative else from_y, q1], (*chips[0], c)))
        return out

    return _Stage(fulls, [jax.ShapeDtypeStruct(f.shape, f.dtype) for f in fulls], 2 * n, 0, copies,
                  aliases={k: k for k in range(n)})


def _ag_d2d_stage(fulls, relative=False):
    n = len(fulls)

    def copies(ins, outs):
        x, y, c, me, _, chip_ids = _mesh_pos()
        out = []
        for k in range(n):
            mine, _ = _half_rows(c, ins[k].shape[1])
            for j in range(3):
                slab = _REL_OF_PEER[j] if relative else chip_ids[j]
                out.append((ins[k].at[slab, mine], outs[k].at[slab, mine], (x, y, 1 - c)))
        return out

    return _Stage(fulls, [jax.ShapeDtypeStruct(f.shape, f.dtype) for f in fulls], 3 * n, 0, copies,
                  aliases={k: k for k in range(n)})


def _rs_swap_stage(grads):
    n = len(grads)

    def copies(ins, outs):
        x, y, c, _, _, _ = _mesh_pos()
        out = []
        for k in range(n):
            _, other = _half_rows(c, ins[k].shape[1])
            out.append((ins[k].at[:, other, :], outs[k], (x, y, 1 - c)))
        return out

    return _Stage(grads, [jax.ShapeDtypeStruct((N_CHIPS, g.shape[1] // 2, g.shape[2]), g.dtype) for g in grads],
                  n, 0, copies)


_REL_OF_PEER = (2, 1, 3)


def _rs_scatter_stage(sums, relative=False):
    n = len(sums)

    def copies(ins, outs):
        x, y, c, me, chips, chip_ids = _mesh_pos()
        out = []
        for k in range(n):
            mine, _ = _half_rows(c, 2 * ins[k].shape[1])
            out.append((ins[k].at[0 if relative else me], outs[k].at[0, mine, :], None))
            for j, (cx, cy) in enumerate(chips):
                slab = _REL_OF_PEER[j] if relative else chip_ids[j]
                out.append((ins[k].at[slab], outs[k].at[1 + j, mine, :], (cx, cy, c)))
        return out

    return _Stage(sums, [jax.ShapeDtypeStruct((N_CHIPS, 2 * s.shape[1], s.shape[2]), s.dtype) for s in sums],
                  3 * n, n, copies)


def _rs_mirror_stage(parts):
    n = len(parts)

    def copies(ins, outs):
        x, y, c, _, _, _ = _mesh_pos()
        out = []
        for k in range(n):
            mine, _ = _half_rows(c, ins[k].shape[1])
            out.append((ins[k].at[:, mine, :], outs[k].at[:, mine, :], (x, y, 1 - c)))
        return out

    return _Stage(parts, [jax.ShapeDtypeStruct(p.shape, p.dtype) for p in parts], n, 0, copies,
                  aliases={k: k for k in range(n)})


def _pair_add(name, g, r1, core):
    _, rows, cols = g.shape
    hr = rows // 2

    def body(c_ref, g_ref, r_ref, o_ref):
        o_ref[...] = (g_ref[...].astype(F32) + r_ref[...].astype(F32)).astype(BF16)

    half = pl.BlockSpec((None, hr, cols), lambda j, c: (j, 0, 0))
    return pl.pallas_call(
        body, name=name,
        grid_spec=pltpu.PrefetchScalarGridSpec(
            num_scalar_prefetch=1, grid=(N_CHIPS,),
            in_specs=[pl.BlockSpec((None, hr, cols), lambda j, c: (j, c[0], 0)), half], out_specs=half),
        out_shape=jax.ShapeDtypeStruct((N_CHIPS, hr, cols), BF16),
        compiler_params=_params("parallel"),
    )(core, g, r1)


def _all_gather_weights(shards):
    n = len(shards)

    def body(*refs):
        ins, outs = refs[:n], refs[n:2 * n]
        send, recv, loc = refs[2 * n:]
        x, y, c, me, chips, chip_ids = _mesh_pos()
        sib = (x, y, 1 - c)

        def halves(k):
            hr = ins[k].shape[0] // 2
            return pl.ds(pl.multiple_of(c * hr, 16), hr), pl.ds(pl.multiple_of((1 - c) * hr, 16), hr)

        def remote(src, dst, k, j, dev):
            return pltpu.make_async_remote_copy(src_ref=src, dst_ref=dst, send_sem=send.at[k, j],
                                                recv_sem=recv.at[k, j], device_id=dev, device_id_type=_MESH)

        started = []
        local = []
        for k in range(n):
            mine, _ = halves(k)
            cp = pltpu.make_async_copy(ins[k], outs[k].at[me], loc.at[k])
            cp.start()
            local.append(cp)
            for j, (cx, cy) in enumerate(chips):
                cp = remote(ins[k].at[mine], outs[k].at[me, mine], k, j, (cx, cy, c))
                cp.start()
                started.append(cp)
        for k in range(n):
            mine, _ = halves(k)
            for j in range(3):
                land = outs[k].at[chip_ids[j], mine]
                remote(land, land, k, j, sib).wait_recv()
                cp = remote(land, land, k, 3 + j, sib)
                cp.start()
                started.append(cp)
        for k in range(n):
            _, other = halves(k)
            for j in range(3):
                land = outs[k].at[chip_ids[j], other]
                remote(land, land, k, 3 + j, sib).wait_recv()
        for cp in started:
            cp.wait_send()
        for cp in local:
            cp.wait()

    return pl.pallas_call(
        body, name="all_gather_weights", in_specs=[_ANY] * n, out_specs=[_ANY] * n,
        out_shape=[jax.ShapeDtypeStruct((N_CHIPS,) + s.shape, s.dtype) for s in shards],
        scratch_shapes=[pltpu.SemaphoreType.DMA((n, 6)), pltpu.SemaphoreType.DMA((n, 6)),
                        pltpu.SemaphoreType.DMA((n,))],
        compiler_params=pltpu.CompilerParams(vmem_limit_bytes=V7X_VMEM_LIMIT_BYTES),
    )(*shards)


_RS_CHUNK = 32


def _reduce_scatter(name, grads):
    n = len(grads)

    def body(*refs):
        gs, outs = refs[:n], refs[n:2 * n]
        own, r1, r2, fin = (refs[(2 + i) * n:(3 + i) * n] for i in range(4))
        a_send, a_recv, b_send, b_recv, c_send, c_recv, l_in, l_out = refs[6 * n:]
        x, y, c, me, chips, chip_ids = _mesh_pos()
        sib = (x, y, 1 - c)

        def halves(k):
            hr = gs[k].shape[1] // 2
            return hr, pl.ds(pl.multiple_of(c * hr, 16), hr), pl.ds(pl.multiple_of((1 - c) * hr, 16), hr)

        def remote(src, dst, ssem, rsem, dev):
            return pltpu.make_async_remote_copy(src_ref=src, dst_ref=dst, send_sem=ssem, recv_sem=rsem,
                                                device_id=dev, device_id_type=_MESH)

        sends, locals_in = [], []
        for k in range(n):
            hr, mine, other = halves(k)
            cp = remote(gs[k].at[:, other, :], r1[k], a_send.at[k], a_recv.at[k], sib)
            cp.start()
            sends.append(cp)
            cp = pltpu.make_async_copy(gs[k].at[:, mine, :], own[k], l_in.at[k])
            cp.start()
            locals_in.append(cp)

        for k in range(n):
            hr, mine, other = halves(k)
            locals_in[k].wait()
            remote(r1[k], r1[k], a_send.at[k], a_recv.at[k], sib).wait_recv()
            for j in range(N_CHIPS):
                def add(i, carry, k=k, j=j):
                    rows = pl.ds(pl.multiple_of(i * _RS_CHUNK, _RS_CHUNK), _RS_CHUNK)
                    own[k][j, rows, :] = (own[k][j, rows, :].astype(F32) + r1[k][j, rows, :].astype(F32)).astype(BF16)
                    return carry

                lax.fori_loop(0, hr // _RS_CHUNK, add, 0)
            for j, (cx, cy) in enumerate(chips):
                cp = remote(own[k].at[chip_ids[j]], r2[k].at[j], b_send.at[k, j], b_recv.at[k, j], (cx, cy, c))
                cp.start()
                sends.append(cp)

        locals_out = []
        for k in range(n):
            hr, mine, other = halves(k)
            for j in range(3):
                remote(r2[k].at[j], r2[k].at[j], b_send.at[k, j], b_recv.at[k, j], sib).wait_recv()

            def total(i, carry, k=k):
                rows = pl.ds(pl.multiple_of(i * _RS_CHUNK, _RS_CHUNK), _RS_CHUNK)
                acc = own[k][me, rows, :].astype(F32)
                for j in range(3):
                    acc = acc + r2[k][j, rows, :].astype(F32)
                fin[k][rows, :] = acc
                return carry

            lax.fori_loop(0, hr // _RS_CHUNK, total, 0)
            cp = remote(fin[k], outs[k].at[mine, :], c_send.at[k], c_recv.at[k], sib)
            cp.start()
            sends.append(cp)
            cp = pltpu.make_async_copy(fin[k], outs[k].at[mine, :], l_out.at[k])
            cp.start()
            locals_out.append(cp)

        for k in range(n):
            hr, mine, other = halves(k)
            land = outs[k].at[other, :]
            remote(land, land, c_send.at[k], c_recv.at[k], sib).wait_recv()
        for cp in sends:
            cp.wait_send()
        for cp in locals_out:
            cp.wait()

    scratch = []
    for g in grads:
        scratch.append(pltpu.VMEM((N_CHIPS, g.shape[1] // 2, g.shape[2]), BF16))
    for g in grads:
        scratch.append(pltpu.VMEM((N_CHIPS, g.shape[1] // 2, g.shape[2]), BF16))
    for g in grads:
        scratch.append(pltpu.VMEM((3, g.shape[1] // 2, g.shape[2]), BF16))
    for g in grads:
        scratch.append(pltpu.VMEM((g.shape[1] // 2, g.shape[2]), F32))
    dma = pltpu.SemaphoreType.DMA
    scratch += [dma((n,)), dma((n,)), dma((n, 3)), dma((n, 3)), dma((n,)), dma((n,)), dma((n,)), dma((n,))]
    return pl.pallas_call(
        body, name=name, in_specs=[_ANY] * n, out_specs=[_ANY] * n,
        out_shape=[jax.ShapeDtypeStruct(g.shape[1:], F32) for g in grads],
        scratch_shapes=scratch,
        compiler_params=pltpu.CompilerParams(vmem_limit_bytes=V7X_VMEM_LIMIT_BYTES),
    )(*grads)


def _adamw_math(w, g, m, v):
    m = ADAM_B1 * m + (1.0 - ADAM_B1) * g
    v = ADAM_B2 * v + (1.0 - ADAM_B2) * (g * g)
    m_hat = m / (1.0 - ADAM_B1 ** ADAM_STEP)
    v_hat = v / (1.0 - ADAM_B2 ** ADAM_STEP)
    delta = -ADAM_LR * (m_hat / (jnp.sqrt(v_hat) + ADAM_EPS) + ADAM_WD * w)
    return delta, m, v


def _adamw_sum(name, w, parts, m, v):
    r, c = w.shape
    br = r
    while br * c * 4 > (1 << 20) and br % 32 == 0:
        br //= 2

    def body(w_ref, p_ref, m_ref, v_ref, g_ref, d_ref, nm_ref, nv_ref):
        g = p_ref[0].astype(F32)
        for j in range(1, N_CHIPS):
            g = g + p_ref[j].astype(F32)
        d, nm, nv = _adamw_math(w_ref[...], g, m_ref[...], v_ref[...])
        g_ref[...] = g
        d_ref[...] = d
        nm_ref[...] = nm
        nv_ref[...] = nv

    spec = pl.BlockSpec((br, c), lambda i: (i, 0))
    shp = jax.ShapeDtypeStruct((r, c), F32)
    return _call(
        body, name=name, grid=(r // br,),
        in_specs=[spec, pl.BlockSpec((N_CHIPS, br, c), lambda i: (0, i, 0)), spec, spec],
        out_specs=[spec] * 4, out_shape=[shp] * 4, sem=("parallel",), args=[w, parts, m, v])


_SMALL_ROWS = 80


def _small_allreduce_adamw(gpack, wpack, mpack, vpack):
    half = _SMALL_ROWS // 2

    def body(g_ref, w_ref, m_ref, v_ref, go_ref, d_ref, nm_ref, nv_ref, sib_buf, chip_sum, buf, send, recv):
        x, y, c, me, chips, chip_ids = _mesh_pos()
        sib = (x, y, 1 - c)
        mine = pl.ds(pl.multiple_of(c * half, 8), half)

        def remote(src, dst, k, dev):
            return pltpu.make_async_remote_copy(src_ref=src, dst_ref=dst, send_sem=send.at[k], recv_sem=recv.at[k],
                                                device_id=dev, device_id_type=_MESH)

        swap = remote(g_ref, sib_buf, 0, sib)
        swap.start()
        swap.wait()
        chip_sum[...] = g_ref[...] + sib_buf[...]
        buf[me] = chip_sum[...]
        sends = [remote(chip_sum.at[mine], buf.at[me, mine], 1 + j, (cx, cy, c)) for j, (cx, cy) in enumerate(chips)]
        for cp in sends:
            cp.start()
        for cp in sends:
            cp.wait()
        mirrors = [remote(buf.at[chip_ids[j], mine], buf.at[chip_ids[j], mine], 4 + j, sib) for j in range(3)]
        for cp in mirrors:
            cp.start()
        for cp in mirrors:
            cp.wait()
        total = buf[0]
        for i in range(1, N_CHIPS):
            total = total + buf[i]
        go_ref[...] = total
        d, nm, nv = _adamw_math(w_ref[...], total, m_ref[...], v_ref[...])
        d_ref[...] = d
        nm_ref[...] = nm
        nv_ref[...] = nv

    vm = pl.BlockSpec(memory_space=pltpu.VMEM)
    shp = jax.ShapeDtypeStruct((_SMALL_ROWS, D_MODEL), F32)
    return pl.pallas_call(
        body, name="small_allreduce_adamw", in_specs=[vm] * 4, out_specs=[vm] * 4, out_shape=[shp] * 4,
        scratch_shapes=[pltpu.VMEM((_SMALL_ROWS, D_MODEL), F32), pltpu.VMEM((_SMALL_ROWS, D_MODEL), F32),
                        pltpu.VMEM((N_CHIPS, _SMALL_ROWS, D_MODEL), F32), pltpu.SemaphoreType.DMA((7,)),
                        pltpu.SemaphoreType.DMA((7,))],
        compiler_params=pltpu.CompilerParams(vmem_limit_bytes=V7X_VMEM_LIMIT_BYTES),
    )(gpack, wpack, mpack, vpack)


_SMALL_VECTORS = ("ffn1_norm", "mix_norm", "xattn_norm", "mem_norm", "ffn2_norm", "final_norm", "q_norm",
                  "kv_norm", "pool_scale")


_LOSS_ROW = 9


def _pack_small(d, scalar=None):
    rows = []
    for n in _SMALL_VECTORS:
        v = d[n].reshape(1, -1).astype(F32)
        rows.append(jnp.pad(v, ((0, 0), (0, D_MODEL - v.shape[1]))))
    assert len(rows) == _LOSS_ROW
    extra = jnp.zeros((1, D_MODEL), F32) if scalar is None else jnp.pad(scalar.reshape(1, 1), ((0, 0), (0, D_MODEL - 1)))
    rows.append(extra)
    rows.append(jnp.zeros((16 - len(rows), D_MODEL), F32))
    rows.append(d["pool_w"].reshape(64, D_MODEL).astype(F32))
    return jnp.concatenate(rows, axis=0)


def _unpack_small(pack, like):
    out = {}
    for i, n in enumerate(_SMALL_VECTORS):
        out[n] = pack[i, :like[n].size].reshape(like[n].shape)
    out["pool_w"] = pack[16:].reshape(like["pool_w"].shape)
    return out


_WEIGHTS = ("ffn1_norm", "ffn1_w_gate", "ffn1_w_up", "ffn1_w_down", "mix_norm", "w_in", "q_norm", "w_q_up",
            "kv_norm", "w_kv_up", "pool_w", "pool_scale", "w_out", "xattn_norm", "mem_norm", "w_mq", "w_mkv",
            "w_mo", "ffn2_norm", "ffn2_w_gate", "ffn2_w_up", "ffn2_w_down", "final_norm")
_SHARDED = ("ffn1_w_gate", "ffn1_w_up", "ffn1_w_down", "w_in", "w_q_up", "w_kv_up", "w_out", "w_mq", "w_mkv",
            "w_mo", "ffn2_w_gate", "ffn2_w_up", "ffn2_w_down")
_RS_GROUPS = (("ffn2_w_gate", "ffn2_w_up", "ffn2_w_down"),
              ("w_mo", "w_mq", "w_mkv", "w_out", "w_q_up", "w_kv_up", "w_in"),
              ("ffn1_w_gate", "ffn1_w_up", "ffn1_w_down"))
W_IN_SPLIT = Q_LORA + KV_LORA + ROPE_DIM


_FFN1 = ("ffn1_w_gate", "ffn1_w_up", "ffn1_w_down")
_TRANSPOSED = ("ffn1_w_gate", "ffn1_w_up", "ffn2_w_gate", "ffn2_w_up", "w_in", "w_q_up")


def _local_view(name, a):
    return jnp.swapaxes(a, 1, 2)[0] if name in _TRANSPOSED else a[0]


def _global_view(name, a):
    return jnp.swapaxes(a[None], 1, 2) if name in _TRANSPOSED else a[None]


def _pad_shard(name, a):
    if name == "w_in":
        return jnp.concatenate([a[:W_IN_SPLIT], jnp.zeros((64, a.shape[1]), a.dtype), a[W_IN_SPLIT:]], axis=0)
    if name == "w_q_up":
        return jnp.pad(a, ((0, 64), (0, 0)))
    return a


def _unpad_shard(name, a):
    if name == "w_in":
        return jnp.concatenate([a[:, :W_IN_SPLIT], a[:, W_IN_SPLIT + 64:]], axis=1)
    if name == "w_q_up":
        return a[:, :192]
    return a


def _stacked(g):
    return g if g.ndim == 3 else g.reshape(N_CHIPS, g.shape[0] // N_CHIPS, g.shape[1])


class _Plan:
    AG_UNITS = (
        (("w_in", "w_q_up", "w_kv_up"), "ffn1_up", "ffn1_down"),
        (("w_out",), "w_in", "qkv_prep"),
        (("w_mq",), "qkv_prep", "mla_fwd"),
        (("w_mkv", "w_mo"), "mla_fwd", "pool_fwd"),
        (("ffn2_w_gate",), "mla_fwd", "w_out"),
        (("ffn2_w_up",), "xattn_fwd", "w_mo"),
        (("ffn2_w_down",), "w_mo", "ffn2_up"),
    )
    RS_UNITS = (
        (("ffn2_w_gate", "ffn2_w_up", "ffn2_w_down"), "ffn2_dn_a", "mla_bwd", "qkv_prep_bwd"),
        (("w_mo", "w_mq", "w_mkv"), "w_out_dx", "mla_bwd", "qkv_prep_bwd"),
        (("w_out", "w_q_up", "w_kv_up", "w_in"), "w_in_dx", "ffn1_dact", "ffn1_dwd"),
        (("ffn1_w_down",), "ffn1_dwg", "ffn1_dwu", "ffn1_dn_a"),
        (("ffn1_w_gate",), "ffn1_dwu", "ffn1_dn_a", "ffn1_dn_b"),
        (("ffn1_w_up",), "ffn1_dn_a", "ffn1_dn_b", "adamw_w_kv_up"),
    )
    ADAMW_ORDER = ("w_kv_up", "ffn2_w_gate", "ffn2_w_up", "ffn2_w_down", "w_mo", "w_mq", "w_mkv", "w_out", "w_q_up",
                   "w_in", "ffn1_w_down", "ffn1_w_gate", "ffn1_w_up")

    def __init__(self, shards, w, grads, core):
        self.shards, self.w, self.grads, self.core = shards, w, grads, core
        self.last_slab_step = 0
        self.parts = {}
        self.ag = [[None, None] for _ in self.AG_UNITS]
        self.rs = [[None, None, None, None] for _ in self.RS_UNITS]

    def pre(self, name):
        for i, (names, h1, h2) in enumerate(self.AG_UNITS):
            if name == h1:
                st = _ag_ici_stage([self.shards[n] for n in names])
                st.start_step = self.last_slab_step if name == "ffn1_up" else 0
                self.ag[i][0] = _host(name, st)
            if name == h2:
                st = _ag_fwd_stage(self.ag[i][0].results)
                st.then = _ag_d2d_stage(st.outs)
                self.ag[i][1] = _host(name, st)
        for i, (names, h1, h2, h3) in enumerate(self.RS_UNITS):
            if name == h1:
                self.rs[i][0] = _host(name, _rs_swap_stage([_stacked(self.grads[n]) for n in names]))
            if name == h2:
                self.rs[i][2] = _host(name, _rs_scatter_stage(self.rs[i][1], relative=names[0] in _FFN1))
            if name == h3:
                self.rs[i][3] = _host(name, _rs_mirror_stage(self.rs[i][2].results))

    def post(self, name):
        for i, (names, h1, h2) in enumerate(self.AG_UNITS):
            if name == h2:
                for n, f in zip(names, self.ag[i][1].results):
                    self.w[n] = _full_weight(n, f)
        for i, (names, h1, h2, h3) in enumerate(self.RS_UNITS):
            if name == h1:
                self.rs[i][1] = [_pair_add("pair_add_" + n, _stacked(self.grads[n]), r1, self.core)
                                 for n, r1 in zip(names, self.rs[i][0].results)]
            if name == h3:
                for n, p in zip(names, self.rs[i][3].results):
                    self.parts[n] = p


def _full_weight(name, stacked):
    if name in ("w_out", "w_mq", "w_mo"):
        return stacked.reshape(D_MODEL, D_MODEL)
    return stacked


def kernel(x, mem, positions, ffn1_norm, ffn1_w_gate, ffn1_w_up, ffn1_w_down, mix_norm, w_in, q_norm, w_q_up, kv_norm, w_kv_up, pool_w, pool_scale, w_out, xattn_norm, mem_norm, w_mq, w_mkv, w_mo, ffn2_norm, ffn2_w_gate, ffn2_w_up, ffn2_w_down, final_norm, loss_target, m_ffn1_norm, m_ffn1_w_gate, m_ffn1_w_up, m_ffn1_w_down, m_mix_norm, m_w_in, m_q_norm, m_w_q_up, m_kv_norm, m_w_kv_up, m_pool_w, m_pool_scale, m_w_out, m_xattn_norm, m_mem_norm, m_w_mq, m_w_mkv, m_w_mo, m_ffn2_norm, m_ffn2_w_gate, m_ffn2_w_up, m_ffn2_w_down, m_final_norm, v_ffn1_norm, v_ffn1_w_gate, v_ffn1_w_up, v_ffn1_w_down, v_mix_norm, v_w_in, v_q_norm, v_w_q_up, v_kv_norm, v_w_kv_up, v_pool_w, v_pool_scale, v_w_out, v_xattn_norm, v_mem_norm, v_w_mq, v_w_mkv, v_w_mo, v_ffn2_norm, v_ffn2_w_gate, v_ffn2_w_up, v_ffn2_w_down, v_final_norm):
    wts = dict(zip(_WEIGHTS, (ffn1_norm, ffn1_w_gate, ffn1_w_up, ffn1_w_down, mix_norm, w_in, q_norm, w_q_up, kv_norm, w_kv_up, pool_w, pool_scale, w_out, xattn_norm, mem_norm, w_mq, w_mkv, w_mo, ffn2_norm, ffn2_w_gate, ffn2_w_up, ffn2_w_down, final_norm)))
    mom = dict(zip(_WEIGHTS, (m_ffn1_norm, m_ffn1_w_gate, m_ffn1_w_up, m_ffn1_w_down, m_mix_norm, m_w_in, m_q_norm, m_w_q_up, m_kv_norm, m_w_kv_up, m_pool_w, m_pool_scale, m_w_out, m_xattn_norm, m_mem_norm, m_w_mq, m_w_mkv, m_w_mo, m_ffn2_norm, m_ffn2_w_gate, m_ffn2_w_up, m_ffn2_w_down, m_final_norm)))
    var = dict(zip(_WEIGHTS, (v_ffn1_norm, v_ffn1_w_gate, v_ffn1_w_up, v_ffn1_w_down, v_mix_norm, v_w_in, v_q_norm, v_w_q_up, v_kv_norm, v_w_kv_up, v_pool_w, v_pool_scale, v_w_out, v_xattn_norm, v_mem_norm, v_w_mq, v_w_mkv, v_w_mo, v_ffn2_norm, v_ffn2_w_gate, v_ffn2_w_up, v_ffn2_w_down, v_final_norm)))
    small = [n for n in _WEIGHTS if n not in _SHARDED]

    global _PLAN
    shards = {n: _pad_shard(n, _local_view(n, wts[n])).astype(BF16) for n in _SHARDED}
    w = {n: wts[n].reshape(1, -1) for n in _SMALL_VECTORS}
    w["pool_w"] = pool_w[0].astype(BF16)
    grads = {}
    core = lax.axis_index("c").astype(jnp.int32).reshape(1)
    plan = _Plan(shards, w, grads, core)
    _PLAN = plan
    try:
        w["ffn1_shards"] = tuple(shards[n] for n in _FFN1)

        loss_local, dx = _local_step(x[0], mem[0], positions[0], loss_target[0], w, grads)

        gpack, dpack, mpack, vpack = _small_allreduce_adamw(
            _pack_small({n: grads[n] for n in small}, loss_local), _pack_small({n: wts[n] for n in small}),
            _pack_small({n: mom[n] for n in small}), _pack_small({n: var[n] for n in small}))
        like = {n: wts[n] for n in small}
        g_out, d_out, m_out, v_out = (_unpack_small(p, like) for p in (gpack, dpack, mpack, vpack))
        loss = gpack[_LOSS_ROW, 0]

        for n in _Plan.ADAMW_ORDER:
            res = _adamw_sum("adamw_" + n, _local_view(n, wts[n]), _unpad_shard(n, plan.parts[n]),
                             _local_view(n, mom[n]), _local_view(n, var[n]))
            g_out[n], d_out[n], m_out[n], v_out[n] = (_global_view(n, r) for r in res)
    finally:
        _PLAN = None
        _PENDING.clear()

    return (loss, dx[None], *[g_out[n] for n in _WEIGHTS], *[d_out[n] for n in _WEIGHTS],
            *[m_out[n] for n in _WEIGHTS], *[v_out[n] for n in _WEIGHTS])
```

```python
import functools

import jax
import jax.numpy as jnp
from jax import lax
from jax.experimental import pallas as pl
from jax.experimental.pallas import tpu as pltpu

F32 = jnp.float32
BF16 = jnp.bfloat16

D_MODEL = 1024
D_FF = 2816
N_CHIPS = 4
FF_SHARD = D_FF // N_CHIPS
MLA_HEADS = 4
Q_LORA = 256
KV_LORA = 128
ROPE_DIM = 64
HEAD_QK = 256
HEAD_V = 128
POOL_GROUPS = 4
POOL_CH = 128
MEM_HEADS = 4
MEM_HEAD_DIM = 256
RMS_EPS = 1e-6
ROPE_BASE = 10000.0
MLA_SCALE = (128 + 64) ** -0.5
MEM_SCALE = MEM_HEAD_DIM ** -0.5

ADAM_LR = 0.001
ADAM_B1 = 0.9
ADAM_B2 = 0.999
ADAM_EPS = 1e-08
ADAM_WD = 0.01
ADAM_STEP = 10

V7X_VMEM_LIMIT_BYTES = 56 * 1024 * 1024

NN = ((1,), (0,))
NT = ((1,), (1,))
TN = ((0,), (0,))


def _params(*sem):
    return pltpu.CompilerParams(dimension_semantics=sem, vmem_limit_bytes=V7X_VMEM_LIMIT_BYTES)


_MESH = pl.DeviceIdType.MESH
_ANY = pl.BlockSpec(memory_space=pl.ANY)


class _Stage:
    def __init__(self, ins, outs, n_remote, n_local, copies, aliases=None):
        self.ins, self.outs, self.n_remote, self.n_local = list(ins), list(outs), n_remote, n_local
        self.copies, self.aliases = copies, dict(aliases or {})
        self.results = None
        self.start_step = 0
        self.then = None

    def descriptors(self, in_refs, out_refs, send, recv, loc):
        ds, ri, li = [], 0, 0
        for src, dst, dev in self.copies(in_refs, out_refs):
            if dev is None:
                ds.append(pltpu.make_async_copy(src, dst, loc.at[li]))
                li += 1
            else:
                ds.append(pltpu.make_async_remote_copy(src_ref=src, dst_ref=dst, send_sem=send.at[ri],
                                                       recv_sem=recv.at[ri], device_id=dev, device_id_type=_MESH))
                ri += 1
        assert ri == self.n_remote and li == self.n_local
        return ds


_PENDING = {}


def _host(name, stage):
    _PENDING.setdefault(name, []).append(stage)
    return stage


_PLAN = None


def _call(body, **kw):
    if _PLAN is not None:
        _PLAN.pre(kw["name"])
    res = _call_hosting(body, **kw)
    if _PLAN is not None:
        _PLAN.post(kw["name"])
    return res


def _call_hosting(body, *, name, grid, in_specs, out_specs, out_shape, sem, args, scratch_shapes=(), aliases=None):
    stages = _PENDING.pop(name, [])
    scratch_shapes = list(scratch_shapes)
    if not stages:
        return pl.pallas_call(body, name=name, grid=grid, in_specs=in_specs, out_specs=out_specs,
                              out_shape=out_shape, scratch_shapes=scratch_shapes,
                              input_output_aliases=dict(aliases or {}), compiler_params=_params(*sem))(*args)
    ni, no, ns = len(in_specs), len(out_shape), len(scratch_shapes)
    c_ins = [a for st in stages for a in st.ins]
    c_outs = [o for st in stages for o in st.outs]
    nci, nco = len(c_ins), len(c_outs)
    aliases, io, oo = dict(aliases or {}), 0, 0
    for st in stages:
        for i, j in st.aliases.items():
            aliases[ni + io + i] = no + oo + j
        io += len(st.ins)
        oo += len(st.outs)
    dma = pltpu.SemaphoreType.DMA
    sems = []
    for st in stages:
        sems += [dma((max(st.n_remote, 1),)), dma((max(st.n_remote, 1),)), dma((max(st.n_local, 1),))]
    followers = [st.then for st in stages if st.then is not None]
    for st in followers:
        sems += [dma((max(st.n_remote, 1),)), dma((max(st.n_remote, 1),)), dma((max(st.n_local, 1),))]

    def wrapped(*refs):
        ins, cin = refs[:ni], refs[ni:ni + nci]
        outs, cout = refs[ni + nci:ni + nci + no], refs[ni + nci + no:ni + nci + no + nco]
        scr = refs[ni + nci + no + nco:ni + nci + no + nco + ns]
        sem_refs = refs[ni + nci + no + nco + ns:]
        step = pl.program_id(0)
        last = pl.program_id(0) == grid[0] - 1
        for ax in range(1, len(grid)):
            step = step * grid[ax] + pl.program_id(ax)
            last = jnp.logical_and(last, pl.program_id(ax) == grid[ax] - 1)

        def descriptors(si):
            io = sum(len(st.ins) for st in stages[:si])
            oo = sum(len(st.outs) for st in stages[:si])
            st = stages[si]
            return st.descriptors(cin[io:io + len(st.ins)], cout[oo:oo + len(st.outs)], *sem_refs[3 * si:3 * si + 3])

        def follower_descriptors(fi):
            si = [k for k, st in enumerate(stages) if st.then is not None][fi]
            oo = sum(len(st.outs) for st in stages[:si])
            bufs = cout[oo:oo + len(stages[si].outs)]
            k0 = 3 * (len(stages) + fi)
            return followers[fi].descriptors(bufs, bufs, *sem_refs[k0:k0 + 3])

        def start(si):
            @pl.when(step == stages[si].start_step)
            def _():
                for d in descriptors(si):
                    d.start()

        for si, st in enumerate(stages):
            if st.start_step == 0:
                start(si)
        body(*ins, *outs, *scr)
        for si, st in enumerate(stages):
            if st.start_step != 0:
                start(si)

        @pl.when(last)
        def _():
            for si in range(len(stages)):
                for d in descriptors(si):
                    d.wait()
            for fi in range(len(followers)):
                for d in follower_descriptors(fi):
                    d.start()
            for fi in range(len(followers)):
                for d in follower_descriptors(fi):
                    d.wait()

    res = pl.pallas_call(
        wrapped, name=name, grid=grid, in_specs=list(in_specs) + [_ANY] * nci,
        out_specs=list(out_specs) + [_ANY] * nco, out_shape=list(out_shape) + c_outs,
        scratch_shapes=scratch_shapes + sems, input_output_aliases=aliases,
        compiler_params=_params(*(("arbitrary",) * len(grid))))(*args, *c_ins)
    oo = no
    for st in stages:
        st.results = list(res[oo:oo + len(st.outs)])
        oo += len(st.outs)
    return list(res[:no])


def _dot(a, b, dims):
    return lax.dot_general(a.astype(BF16), b.astype(BF16), (dims, ((), ())), preferred_element_type=F32)


_MAX_ROW_BLOCK = 1024
_ATT_BLOCK = 512


_MAX_REDUCE_BLOCK = 2048


def _row_block(s, want=1024):
    return min(want, s, _MAX_ROW_BLOCK)


def _reduce_block(s):
    return min(s, _MAX_REDUCE_BLOCK)


def _matmul(name, grid, terms, extras, outs, epilogue, acc_shape, fill=(), summed=()):
    nt, ne, no, nf = len(terms), len(extras), len(outs), len(fill)
    nk = grid[-1]
    dims = [t[4] for t in terms]

    def body(*refs):
        a_refs, b_refs = refs[:nt], refs[nt:2 * nt]
        e_refs = refs[2 * nt:2 * nt + ne]
        o_refs = refs[2 * nt + ne + nf:2 * nt + ne + nf + no]

        def finish(acc):
            vals = epilogue(acc, *[e[...] for e in e_refs])
            for idx, (o, val) in enumerate(zip(o_refs, vals)):
                if idx in summed:
                    @pl.when(pl.program_id(0) == 0)
                    def _(o=o, val=val):
                        o[...] = val.astype(o.dtype)

                    @pl.when(pl.program_id(0) > 0)
                    def _(o=o, val=val):
                        o[...] += val.astype(o.dtype)
                else:
                    o[...] = val.astype(o.dtype)

        if nk == 1:
            part = None
            for a, b, d in zip(a_refs, b_refs, dims):
                t = _dot(a[...], b[...], d)
                part = t if part is None else part + t
            finish(part)
        else:
            acc_ref = refs[-1]
            k = pl.program_id(len(grid) - 1)

            @pl.when(k == 0)
            def _():
                acc_ref[...] = jnp.zeros_like(acc_ref)

            for a, b, d in zip(a_refs, b_refs, dims):
                acc_ref[...] += _dot(a[...], b[...], d)

            @pl.when(k == nk - 1)
            def _():
                finish(acc_ref[...])

    in_specs = [t[1] for t in terms] + [t[3] for t in terms] + [e[1] for e in extras] + [_ANY] * nf
    args = [t[0] for t in terms] + [t[2] for t in terms] + [e[0] for e in extras] + list(fill)
    sem = ("arbitrary" if summed else "parallel",) * (len(grid) - 1) + ("arbitrary",)
    aliases = {2 * nt + ne + i: i for i in range(nf)}
    return _call(
        body, name=name, grid=grid, in_specs=in_specs,
        out_specs=[o[1] for o in outs], out_shape=[o[0] for o in outs],
        scratch_shapes=[pltpu.VMEM(acc_shape, F32)] if nk > 1 else [], sem=sem, args=args, aliases=aliases)


def _ident(acc):
    return (acc,)


def _rmsnorm_fwd(name, x, gain, width, col_block=0):
    s = x.shape[0]
    bm = _row_block(s)

    def body(x_ref, g_ref, o_ref):
        xf = x_ref[...]
        r = lax.rsqrt(jnp.mean(xf * xf, axis=-1, keepdims=True) + RMS_EPS)
        o_ref[...] = ((xf * r) * g_ref[...]).astype(o_ref.dtype)

    return pl.pallas_call(
        body, name=name, grid=(s // bm,),
        in_specs=[pl.BlockSpec((bm, width), lambda i: (i, col_block)), pl.BlockSpec((1, width), lambda i: (0, 0))],
        out_specs=pl.BlockSpec((bm, width), lambda i: (i, 0)),
        out_shape=jax.ShapeDtypeStruct((s, width), BF16),
        compiler_params=_params("parallel"),
    )(x, gain)


def _rms_bwd_math(dy, xf, g, width):
    r = lax.rsqrt(jnp.mean(xf * xf, axis=-1, keepdims=True) + RMS_EPS)
    dyg = dy * g
    dot = jnp.sum(dyg * xf, axis=-1, keepdims=True)
    dx = r * dyg - xf * ((r * r * r) * (dot * (1.0 / width)))
    dgain = jnp.sum(dy * (xf * r), axis=0, keepdims=True)
    return dx, dgain


def _rmsnorm_bwd(name, dy, x, gain, width, col_block=0, dres=None, out_dtype=F32):
    s = x.shape[0]
    bm = _row_block(s)
    has_res = dres is not None

    def body(*refs):
        if has_res:
            dy_ref, x_ref, g_ref, r_ref, dx_ref, dg_ref, dxb_ref = refs
        else:
            dy_ref, x_ref, g_ref, dx_ref, dg_ref = refs
        dx, dgain = _rms_bwd_math(dy_ref[...].astype(F32), x_ref[...], g_ref[...], width)
        if has_res:
            dx = dx + r_ref[...]
            dxb_ref[...] = dx.astype(BF16)
        dx_ref[...] = dx.astype(dx_ref.dtype)

        @pl.when(pl.program_id(0) == 0)
        def _():
            dg_ref[...] = dgain

        @pl.when(pl.program_id(0) > 0)
        def _():
            dg_ref[...] += dgain

    row = pl.BlockSpec((bm, width), lambda i: (i, 0))
    in_specs = [row, pl.BlockSpec((bm, width), lambda i: (i, col_block)), pl.BlockSpec((1, width), lambda i: (0, 0))]
    args = [dy, x, gain]
    out_specs = [row, pl.BlockSpec((1, width), lambda i: (0, 0))]
    out_shape = [jax.ShapeDtypeStruct((s, width), out_dtype), jax.ShapeDtypeStruct((1, width), F32)]
    if has_res:
        in_specs.append(row)
        args.append(dres)
        out_specs.append(row)
        out_shape.append(jax.ShapeDtypeStruct((s, width), BF16))
    return _call(body, name=name, grid=(s // bm,), in_specs=in_specs, out_specs=out_specs, out_shape=out_shape,
                 sem=("arbitrary",), args=args)


def _loss_and_final_norm(h, gain, target):
    s, d = h.shape
    bm = _row_block(s, 512)

    def body(h_ref, g_ref, t_ref, dh_ref, dhb_ref, loss_ref, dg_ref):
        xf = h_ref[...]
        g = g_ref[...]
        r = lax.rsqrt(jnp.mean(xf * xf, axis=-1, keepdims=True) + RMS_EPS)
        err = (xf * r) * g - t_ref[...]
        part = 0.5 * jnp.sum(jnp.mean(err * err, axis=-1, keepdims=True), axis=0, keepdims=True)
        dx, dgain = _rms_bwd_math(err * (1.0 / d), xf, g, d)
        dh_ref[...] = dx
        dhb_ref[...] = dx.astype(BF16)

        @pl.when(pl.program_id(0) == 0)
        def _():
            dg_ref[...] = dgain
            loss_ref[...] = jnp.broadcast_to(part, loss_ref.shape)

        @pl.when(pl.program_id(0) > 0)
        def _():
            dg_ref[...] += dgain
            loss_ref[...] += jnp.broadcast_to(part, loss_ref.shape)

    row = pl.BlockSpec((bm, d), lambda i: (i, 0))
    vec = pl.BlockSpec((1, d), lambda i: (0, 0))
    return pl.pallas_call(
        body, name="loss_final_norm", grid=(s // bm,), in_specs=[row, vec, row],
        out_specs=[row, row, pl.BlockSpec((1, 128), lambda i: (0, 0)), vec],
        out_shape=[jax.ShapeDtypeStruct((s, d), F32), jax.ShapeDtypeStruct((s, d), BF16),
                   jax.ShapeDtypeStruct((1, 128), F32),
                   jax.ShapeDtypeStruct((1, d), F32)],
        compiler_params=_params("arbitrary"),
    )(h, gain, target)


def _ffn_up(name, n, wg, wu):
    s = n.shape[0]
    bm = _row_block(s)

    def body(n_ref, wg_ref, wu_ref, a_ref, dadu_ref, dadg_ref):
        x = n_ref[...]
        g = _dot(x, wg_ref[...], NT)
        u = _dot(x, wu_ref[...], NT)
        sg = jax.nn.sigmoid(g)
        silu = g * sg
        a_ref[...] = (silu * u).astype(BF16)
        dadu_ref[...] = silu.astype(BF16)
        dadg_ref[...] = (u * (sg * (1.0 + g * (1.0 - sg)))).astype(BF16)

    w_spec = pl.BlockSpec((None, FF_SHARD, D_MODEL), lambda j, i: (j, 0, 0))
    o_spec = pl.BlockSpec((None, bm, FF_SHARD), lambda j, i: (j, i, 0))
    shp = jax.ShapeDtypeStruct((N_CHIPS, s, FF_SHARD), BF16)
    return _call(
        body, name=name, grid=(N_CHIPS, s // bm),
        in_specs=[pl.BlockSpec((bm, D_MODEL), lambda j, i: (i, 0)), w_spec, w_spec],
        out_specs=[o_spec, o_spec, o_spec], out_shape=[shp, shp, shp],
        sem=("parallel", "parallel"), args=[n, wg, wu])


def _ffn1_up_gather_direct(n, g_sh, u_sh, d_sh):
    s = n.shape[0]
    bm = _row_block(s)
    nrb = s // bm
    rows, cols = g_sh.shape
    rels = (1, 2, 3)

    def body(n_ref, gs, us, ds, a_ref, dadu_ref, dadg_ref, wg, wu, wd, gbuf, ubuf, send, recv, fsend, frecv, loc, ld):
        r, i = pl.program_id(0), pl.program_id(1)
        x, y, c = lax.axis_index("x"), lax.axis_index("y"), lax.axis_index("c")
        sib = (x, y, 1 - c)
        mine, _ = _half_rows(c, rows)
        shards, fulls, bufs = (gs, us, ds), (wg, wu, wd), (gbuf, ubuf)

        def ici(k, rel, dev=sib):
            return pltpu.make_async_remote_copy(
                src_ref=shards[k].at[mine], dst_ref=fulls[k].at[rel, mine], send_sem=send.at[k, rel - 1],
                recv_sem=recv.at[k, rel - 1], device_id=dev, device_id_type=_MESH)

        def peer(rel):
            return ((1 - x) if rel & 2 else x, (1 - y) if rel & 1 else y, c)

        def fwd(k, rel):
            return pltpu.make_async_remote_copy(
                src_ref=fulls[k].at[rel, mine], dst_ref=fulls[k].at[rel, mine], send_sem=fsend.at[k, rel - 1],
                recv_sem=frecv.at[k, rel - 1], device_id=sib, device_id_type=_MESH)

        def own(k):
            return pltpu.make_async_copy(shards[k], fulls[k].at[0], loc.at[k])

        def load(k, src):
            return pltpu.make_async_copy(src, bufs[k], ld.at[k])

        @pl.when(jnp.logical_and(r == 0, i == 0))
        def _():
            for k in range(3):
                own(k).start()
            for rel in (1, 2):
                for k in (0, 1):
                    ici(k, rel, peer(rel)).start()
            for k in (0, 1):
                load(k, shards[k]).start()
            for k in (0, 1):
                load(k, shards[k]).wait()

        @pl.when(jnp.logical_and(r > 0, i == 0))
        def _():
            for k in (0, 1):
                ici(k, r).wait_recv()
                fwd(k, r).start()
            for k in (0, 1):
                fwd(k, r).wait_recv()
                load(k, fulls[k].at[r]).start()
            for k in (0, 1):
                load(k, fulls[k].at[r]).wait()

        @pl.when(jnp.logical_and(r == 1, i == 0))
        def _():
            for k in (0, 1):
                ici(k, 3, peer(3)).start()

        @pl.when(jnp.logical_and(r == 2, i == 0))
        def _():
            for rel in (1, 2):
                ici(2, rel, peer(rel)).start()

        @pl.when(jnp.logical_and(r == 3, i == 0))
        def _():
            ici(2, 3, peer(3)).start()

        xv = n_ref[...]
        g = _dot(xv, gbuf[...], NT)
        u = _dot(xv, ubuf[...], NT)
        sg = jax.nn.sigmoid(g)
        silu = g * sg
        a_ref[...] = (silu * u).astype(BF16)
        dadu_ref[...] = silu.astype(BF16)
        dadg_ref[...] = (u * (sg * (1.0 + g * (1.0 - sg)))).astype(BF16)

        @pl.when(jnp.logical_and(r == 3, i == nrb - 1))
        def _():
            for rel in rels:
                ici(2, rel).wait_recv()
                fwd(2, rel).start()
            for rel in rels:
                fwd(2, rel).wait_recv()
            for k in range(3):
                for rel in rels:
                    ici(k, rel).wait_send()
                    fwd(k, rel).wait_send()
                own(k).wait()

    o_spec = pl.BlockSpec((None, bm, FF_SHARD), lambda r, i: (r, i, 0))
    act = jax.ShapeDtypeStruct((N_CHIPS, s, FF_SHARD), BF16)
    full = jax.ShapeDtypeStruct((N_CHIPS, rows, cols), BF16)
    dma = pltpu.SemaphoreType.DMA
    if _PLAN is not None:
        _PLAN.last_slab_step = 3 * nrb
    return _call(
        body, name="ffn1_up", grid=(N_CHIPS, nrb),
        in_specs=[pl.BlockSpec((bm, D_MODEL), lambda r, i: (i, 0)), _ANY, _ANY, _ANY],
        out_specs=[o_spec, o_spec, o_spec, _ANY, _ANY, _ANY], out_shape=[act, act, act, full, full, full],
        scratch_shapes=[pltpu.VMEM((rows, cols), BF16), pltpu.VMEM((rows, cols), BF16), dma((3, 3)), dma((3, 3)),
                        dma((3, 3)), dma((3, 3)), dma((3,)), dma((2,))],
        sem=("arbitrary", "arbitrary"), args=[n, g_sh, u_sh, d_sh])


def _ffn1_up_gather(n, g_sh, u_sh, d_sh):
    s = n.shape[0]
    bm = _row_block(s)
    nrb = s // bm
    rows, cols = g_sh.shape

    def body(n_ref, gs, us, ds, a_ref, dadu_ref, dadg_ref, wg, wu, wd, gbuf, ubuf,
             send, recv, qsend, qrecv, fsend, frecv, loc, ld):
        r, i = pl.program_id(0), pl.program_id(1)
        x, y, c = lax.axis_index("x"), lax.axis_index("y"), lax.axis_index("c")
        sib = (x, y, 1 - c)
        mine, _ = _half_rows(c, rows)
        quarters = _quarter_rows(c, rows)
        shards, fulls, bufs = (gs, us, ds), (wg, wu, wd), (gbuf, ubuf)

        def remote(src, dst, ssem, rsem, dev):
            return pltpu.make_async_remote_copy(src_ref=src, dst_ref=dst, send_sem=ssem, recv_sem=rsem,
                                                device_id=dev, device_id_type=_MESH)

        def peer(rel):
            return ((1 - x) if rel & 2 else x, (1 - y) if rel & 1 else y, c)

        def ici(k, rel, dev=sib):
            return remote(shards[k].at[mine], fulls[k].at[rel, mine], send.at[k, rel - 1], recv.at[k, rel - 1], dev)

        def quarter(k, which, dev=sib):
            slab, q = ((2, quarters[0]), (1, quarters[1]))[which]
            return remote(fulls[k].at[slab, q], fulls[k].at[3, q], qsend.at[k, which], qrecv.at[k, which], dev)

        def fwd(k, rel):
            return remote(fulls[k].at[rel, mine], fulls[k].at[rel, mine], fsend.at[k, rel - 1], frecv.at[k, rel - 1], sib)

        def own(k):
            return pltpu.make_async_copy(shards[k], fulls[k].at[0], loc.at[k])

        def load(slab):
            for k in (0, 1):
                pltpu.make_async_copy(shards[k] if slab == 0 else fulls[k].at[slab], bufs[k], ld.at[k]).start()
            for k in (0, 1):
                pltpu.make_async_copy(shards[k] if slab == 0 else fulls[k].at[slab], bufs[k], ld.at[k]).wait()

        def from_neighbour(ks, rel):
            for k in ks:
                ici(k, rel).wait_recv()
                fwd(k, rel).start()
                quarter(k, 0 if rel == 2 else 1, peer(1 if rel == 2 else 2)).start()
            for k in ks:
                fwd(k, rel).wait_recv()

        def from_diagonal(ks):
            for k in ks:
                quarter(k, 0).wait_recv()
                quarter(k, 1).wait_recv()
                fwd(k, 3).start()
            for k in ks:
                fwd(k, 3).wait_recv()

        @pl.when(jnp.logical_and(r == 0, i == 0))
        def _():
            for k in range(3):
                own(k).start()
            for rel in (1, 2):
                for k in (0, 1):
                    ici(k, rel, peer(rel)).start()
            load(0)

        @pl.when(jnp.logical_and(r == 1, i == 0))
        def _():
            from_neighbour((0, 1), 1)
            load(1)
            for rel in (1, 2):
                ici(2, rel, peer(rel)).start()

        @pl.when(jnp.logical_and(r == 2, i == 0))
        def _():
            from_neighbour((0, 1), 2)
            load(2)

        @pl.when(jnp.logical_and(r == 3, i == 0))
        def _():
            from_diagonal((0, 1))
            load(3)

        xv = n_ref[...]
        g = _dot(xv, gbuf[...], NT)
        u = _dot(xv, ubuf[...], NT)
        sg = jax.nn.sigmoid(g)
        silu = g * sg
        a_ref[...] = (silu * u).astype(BF16)
        dadu_ref[...] = silu.astype(BF16)
        dadg_ref[...] = (u * (sg * (1.0 + g * (1.0 - sg)))).astype(BF16)

        @pl.when(jnp.logical_and(r == 3, i == nrb - 1))
        def _():
            from_neighbour((2,), 1)
            from_neighbour((2,), 2)
            from_diagonal((2,))
            for k in range(3):
                for rel in (1, 2):
                    ici(k, rel).wait_send()
                for which in (0, 1):
                    quarter(k, which).wait_send()
                for rel in (1, 2, 3):
                    fwd(k, rel).wait_send()
                own(k).wait()

    o_spec = pl.BlockSpec((None, bm, FF_SHARD), lambda r, i: (r, i, 0))
    act = jax.ShapeDtypeStruct((N_CHIPS, s, FF_SHARD), BF16)
    full = jax.ShapeDtypeStruct((N_CHIPS, rows, cols), BF16)
    dma = pltpu.SemaphoreType.DMA
    if _PLAN is not None:
        _PLAN.last_slab_step = 3 * nrb
    return _call(
        body, name="ffn1_up", grid=(N_CHIPS, nrb),
        in_specs=[pl.BlockSpec((bm, D_MODEL), lambda r, i: (i, 0)), _ANY, _ANY, _ANY],
        out_specs=[o_spec, o_spec, o_spec, _ANY, _ANY, _ANY], out_shape=[act, act, act, full, full, full],
        scratch_shapes=[pltpu.VMEM((rows, cols), BF16), pltpu.VMEM((rows, cols), BF16), dma((3, 2)), dma((3, 2)),
                        dma((3, 2)), dma((3, 2)), dma((3, 3)), dma((3, 3)), dma((3,)), dma((2,))],
        sem=("arbitrary", "arbitrary"), args=[n, g_sh, u_sh, d_sh])


def _residual_epilogue(alpha, with_norm):
    if not with_norm:
        return lambda acc, r: (r + alpha * acc,)

    def epilogue(acc, r, g):
        h = r + alpha * acc
        rs = lax.rsqrt(jnp.mean(h * h, axis=-1, keepdims=True) + RMS_EPS)
        return h, (h * rs) * g

    return epilogue


def _residual_outs(s, bm, gain):
    row = pl.BlockSpec((bm, D_MODEL), lambda i, k: (i, 0))
    outs = [(jax.ShapeDtypeStruct((s, D_MODEL), F32), row)]
    if gain is None:
        return [], outs
    return [(gain, pl.BlockSpec((1, D_MODEL), lambda i, k: (0, 0)))], outs + [(jax.ShapeDtypeStruct((s, D_MODEL), BF16), row)]


def _ffn_down(name, a, wd, res, gain=None):
    s = a.shape[1]
    bm = _row_block(s, 512)
    row = pl.BlockSpec((bm, D_MODEL), lambda i, k: (i, 0))
    terms = [(a, pl.BlockSpec((None, bm, FF_SHARD), lambda i, k, j=j: (j, i, 0)),
              wd, pl.BlockSpec((None, FF_SHARD, D_MODEL), lambda i, k, j=j: (j, 0, 0)), NN) for j in range(N_CHIPS)]
    extras, outs = _residual_outs(s, bm, gain)
    res_out = _matmul(name, (s // bm, 1), terms, [(res, row)] + extras, outs,
                      _residual_epilogue(0.5, gain is not None), None)
    return res_out if gain is not None else res_out[0]


def _norm_bwd_epilogue(width):
    def epilogue(acc, h, g, dres):
        dx, dgain = _rms_bwd_math(acc, h, g, width)
        dx = dx + dres
        return dx, dx, dgain

    return epilogue


def _norm_bwd_operands(s, bm, h, gain, dres):
    row = pl.BlockSpec((bm, D_MODEL), lambda i, k: (i, 0))
    vec = pl.BlockSpec((1, D_MODEL), lambda i, k: (0, 0))
    extras = [(h, row), (gain, vec), (dres, row)]
    outs = [(jax.ShapeDtypeStruct((s, D_MODEL), F32), row), (jax.ShapeDtypeStruct((s, D_MODEL), BF16), row),
            (jax.ShapeDtypeStruct((1, D_MODEL), F32), vec)]
    return extras, outs, (2,)


def _ffn_bwd(tag, dh, n, dadg, dadu, a, wg, wu, wd, grads, norm_bwd=None):
    s = dh.shape[0]
    bm = _row_block(s)
    bk = _reduce_block(s)
    nk = s // bk

    def act_bwd(acc, dg_da, du_da):
        da = 0.5 * acc
        return da * dg_da.astype(F32), da * du_da.astype(F32)

    slab = pl.BlockSpec((None, bm, FF_SHARD), lambda j, i, k: (j, i, 0))
    shp = jax.ShapeDtypeStruct((N_CHIPS, s, FF_SHARD), BF16)
    dg, du = _matmul(
        tag + "_dact", (N_CHIPS, s // bm, 1),
        [(dh, pl.BlockSpec((bm, D_MODEL), lambda j, i, k: (i, 0)),
          wd, pl.BlockSpec((None, FF_SHARD, D_MODEL), lambda j, i, k: (j, 0, 0)), NT)],
        [(dadg, slab), (dadu, slab)], [(shp, slab), (shp, slab)], act_bwd, None)

    grads[tag + "_w_down"] = _matmul(
        tag + "_dwd", (N_CHIPS, nk),
        [(a, pl.BlockSpec((None, bk, FF_SHARD), lambda j, k: (j, k, 0)),
          dh, pl.BlockSpec((bk, D_MODEL), lambda j, k: (k, 0)), TN)],
        [], [(jax.ShapeDtypeStruct((N_CHIPS, FF_SHARD, D_MODEL), BF16),
              pl.BlockSpec((None, FF_SHARD, D_MODEL), lambda j, k: (j, 0, 0)))],
        lambda acc: (0.5 * acc,), (FF_SHARD, D_MODEL))[0]

    def dw_up(nm, dact):
        return _matmul(
            nm, (N_CHIPS, nk),
            [(dact, pl.BlockSpec((None, bk, FF_SHARD), lambda j, k: (j, k, 0)),
              n, pl.BlockSpec((bk, D_MODEL), lambda j, k: (k, 0)), TN)],
            [], [(jax.ShapeDtypeStruct((N_CHIPS, FF_SHARD, D_MODEL), BF16),
                  pl.BlockSpec((None, FF_SHARD, D_MODEL), lambda j, k: (j, 0, 0)))],
            _ident, (FF_SHARD, D_MODEL))[0]

    grads[tag + "_w_gate"] = dw_up(tag + "_dwg", dg)
    grads[tag + "_w_up"] = dw_up(tag + "_dwu", du)

    bn = _row_block(s, 512)
    steps = s // bn // 2
    prev, dgain = (), None
    for part, off in (("_dn_a", 0), ("_dn_b", steps)):
        row = pl.BlockSpec((bn, D_MODEL), lambda i, k, off=off: (i + off, 0))
        terms = []
        for j in range(N_CHIPS):
            a_slab = pl.BlockSpec((None, bn, FF_SHARD), lambda i, k, j=j, off=off: (j, i + off, 0))
            w_slab = pl.BlockSpec((None, FF_SHARD, D_MODEL), lambda i, k, j=j: (j, 0, 0))
            terms += [(dg, a_slab, wg, w_slab, NN), (du, a_slab, wu, w_slab, NN)]
        if norm_bwd is None:
            prev = _matmul(tag + part, (steps, 1), terms, [], [(jax.ShapeDtypeStruct((s, D_MODEL), F32), row)],
                           _ident, None, fill=prev)
            continue
        h, gain, dres = norm_bwd
        vec = pl.BlockSpec((1, D_MODEL), lambda i, k: (0, 0))
        res = _matmul(
            tag + part, (steps, 1), terms, [(h, row), (gain, vec), (dres, row)],
            [(jax.ShapeDtypeStruct((s, D_MODEL), F32), row), (jax.ShapeDtypeStruct((s, D_MODEL), BF16), row),
             (jax.ShapeDtypeStruct((1, D_MODEL), F32), vec)],
            _norm_bwd_epilogue(D_MODEL), None, fill=prev, summed=(2,))
        prev = res[:2]
        dgain = res[2] if dgain is None else dgain + res[2]
    return prev[0] if norm_bwd is None else (prev[0], prev[1], dgain)


def _mm_nn(name, a, b, out_dtype, res=None, gain=None):
    s, k = a.shape
    nn = b.shape[1]
    bm = _row_block(s)
    row = pl.BlockSpec((bm, nn), lambda i, kk: (i, 0))
    term = [(a, pl.BlockSpec((bm, k), lambda i, kk: (i, 0)), b, pl.BlockSpec((k, nn), lambda i, kk: (0, 0)), NN)]
    if res is None:
        return _matmul(name, (s // bm, 1), term, [], [(jax.ShapeDtypeStruct((s, nn), out_dtype), row)], _ident, None)[0]
    extras, outs = _residual_outs(s, bm, gain)
    res_out = _matmul(name, (s // bm, 1), term, [(res, row)] + extras, outs,
                      _residual_epilogue(1.0, gain is not None), None)
    return res_out if gain is not None else res_out[0]


def _mm_nt(name, a, b, out_dtype, attn_out=None, nh=0, dv=0):
    s, nn = a.shape
    k = b.shape[0]
    bm = _row_block(s)
    term = [(a, pl.BlockSpec((bm, nn), lambda i, kk: (i, 0)), b, pl.BlockSpec((k, nn), lambda i, kk: (0, 0)), NT)]
    out = (jax.ShapeDtypeStruct((s, k), out_dtype), pl.BlockSpec((bm, k), lambda i, kk: (i, 0)))
    if attn_out is None:
        return _matmul(name, (s // bm, 1), term, [], [out], _ident, None)[0]

    def with_delta(acc, o):
        do = acc.astype(out_dtype).astype(F32)
        cols = [jnp.sum(do[:, h * dv:(h + 1) * dv] * o[:, h * dv:(h + 1) * dv].astype(F32), axis=-1, keepdims=True)
                for h in range(nh)]
        return acc, jnp.stack(cols, axis=0)

    return _matmul(
        name, (s // bm, 1), term, [(attn_out, pl.BlockSpec((bm, nh * dv), lambda i, kk: (i, 0)))],
        [out, (jax.ShapeDtypeStruct((nh, s, 1), F32), pl.BlockSpec((nh, bm, 1), lambda i, kk: (0, i, 0)))],
        with_delta, None)


def _mm_nt_norm_bwd(name, a, b, h, gain, dres):
    s, nn = a.shape
    bm = _row_block(s, 512)
    extras, outs, summed = _norm_bwd_operands(s, bm, h, gain, dres)
    return _matmul(
        name, (s // bm, 1),
        [(a, pl.BlockSpec((bm, nn), lambda i, kk: (i, 0)), b, pl.BlockSpec(b.shape, lambda i, kk: (0, 0)), NT)],
        extras, outs, _norm_bwd_epilogue(D_MODEL), None, summed=summed)


def _w_in_dx_norm_bwd(dz, w_t, h, gain, dres):
    s = dz.shape[0]
    bm = _row_block(s, 512)
    epilogue = _norm_bwd_epilogue(D_MODEL)

    def body(dz_ref, w_ref, h_ref, g_ref, r_ref, dx_ref, dxb_ref, dg_ref):
        dzv = dz_ref[...]
        dn = jnp.concatenate([_dot(dzv, w_ref[j], NN) for j in range(N_CHIPS)], axis=1)
        dx, _, dgain = epilogue(dn, h_ref[...], g_ref[...], r_ref[...])
        dx_ref[...] = dx
        dxb_ref[...] = dx.astype(BF16)

        @pl.when(pl.program_id(0) == 0)
        def _():
            dg_ref[...] = dgain

        @pl.when(pl.program_id(0) > 0)
        def _():
            dg_ref[...] += dgain

    row = pl.BlockSpec((bm, D_MODEL), lambda i: (i, 0))
    vec = pl.BlockSpec((1, D_MODEL), lambda i: (0, 0))
    return _call(
        body, name="w_in_dx", grid=(s // bm,),
        in_specs=[row, pl.BlockSpec(w_t.shape, lambda i: (0, 0, 0)), row, vec, row],
        out_specs=[row, row, vec],
        out_shape=[jax.ShapeDtypeStruct((s, D_MODEL), F32), jax.ShapeDtypeStruct((s, D_MODEL), BF16),
                   jax.ShapeDtypeStruct((1, D_MODEL), F32)],
        sem=("arbitrary",), args=[dz, w_t, h, gain, dres])


def _mm_tn(name, a, b, out_dtype=BF16):
    s, k = a.shape
    nn = b.shape[1]
    bk = _reduce_block(s)
    return _matmul(
        name, (s // bk,),
        [(a, pl.BlockSpec((bk, k), lambda kk: (kk, 0)), b, pl.BlockSpec((bk, nn), lambda kk: (kk, 0)), TN)],
        [], [(jax.ShapeDtypeStruct((k, nn), out_dtype), pl.BlockSpec((k, nn), lambda kk: (0, 0)))],
        _ident, (k, nn))[0]


def _mm_heads_fwd(name, a, w, out_dtype, w_transposed=False):
    s, k = a.shape
    nh = w.shape[0]
    nn = w.shape[1] if w_transposed else w.shape[2]
    bm = _row_block(s)
    return _matmul(
        name, (nh, s // bm, 1),
        [(a, pl.BlockSpec((bm, k), lambda h, i, kk: (i, 0)),
          w, pl.BlockSpec((None,) + w.shape[1:], lambda h, i, kk: (h, 0, 0)), NT if w_transposed else NN)],
        [], [(jax.ShapeDtypeStruct((s, nh * nn), out_dtype), pl.BlockSpec((bm, nn), lambda h, i, kk: (i, h)))],
        _ident, None)[0]


def _mm_heads_bwd(name, dy, a, w, w_transposed=False):
    s, k = a.shape
    nh = w.shape[0]
    nn = w.shape[1] if w_transposed else w.shape[2]
    bm = _row_block(s)
    bk = _reduce_block(s)
    w_spec = pl.BlockSpec((None,) + w.shape[1:], lambda i, h: (h, 0, 0))
    da = _matmul(
        name + "_dx", (s // bm, nh),
        [(dy, pl.BlockSpec((bm, nn), lambda i, h: (i, h)), w, w_spec, NN if w_transposed else NT)],
        [], [(jax.ShapeDtypeStruct((s, k), F32), pl.BlockSpec((bm, k), lambda i, h: (i, 0)))], _ident, (bm, k))[0]
    a_term = (a, pl.BlockSpec((bk, k), lambda h, kk: (kk, 0)))
    dy_term = (dy, pl.BlockSpec((bk, nn), lambda h, kk: (kk, h)))
    lhs, rhs = (dy_term, a_term) if w_transposed else (a_term, dy_term)
    dw = _matmul(
        name + "_dw", (nh, s // bk), [lhs + rhs + (TN,)],
        [], [(jax.ShapeDtypeStruct(w.shape, BF16), pl.BlockSpec((None,) + w.shape[1:], lambda h, kk: (h, 0, 0)))],
        _ident, w.shape[1:])[0]
    return da, dw


def _w_in_fwd(n, w_t):
    s = n.shape[0]
    bm = _row_block(s)
    nh, nout, kin = w_t.shape
    terms = [(n, pl.BlockSpec((bm, kin), lambda i, k, j=j: (i, j)),
              w_t, pl.BlockSpec((None, nout, kin), lambda i, k, j=j: (j, 0, 0)), NT) for j in range(nh)]
    row = pl.BlockSpec((bm, nout), lambda i, k: (i, 0))
    return _matmul("w_in", (s // bm, 1), terms, [], [(jax.ShapeDtypeStruct((s, nout), F32), row)], _ident, None)[0]


def _w_in_dw(dz, n):
    s, nout = dz.shape
    kin = n.shape[1] // N_CHIPS
    bk = _reduce_block(s)
    return _matmul(
        "w_in_dw", (N_CHIPS, s // bk),
        [(dz, pl.BlockSpec((bk, nout), lambda j, k: (k, 0)), n, pl.BlockSpec((bk, kin), lambda j, k: (k, j)), TN)],
        [], [(jax.ShapeDtypeStruct((N_CHIPS, nout, kin), BF16), pl.BlockSpec((None, nout, kin), lambda j, k: (j, 0, 0)))],
        _ident, (nout, kin))[0]


def _rope_tables(positions):
    half = ROPE_DIM // 2
    freqs = 1.0 / (ROPE_BASE ** (jnp.arange(0, ROPE_DIM, 2, dtype=F32) / ROPE_DIM))
    ang = positions.astype(F32)[:, None] * freqs
    cos, sin = jnp.cos(ang), jnp.sin(ang)
    z = jnp.zeros_like(cos)
    tc = jnp.concatenate([cos, cos, z, z], axis=-1)
    ta = jnp.concatenate([-sin, z, z, z], axis=-1)
    tb = jnp.concatenate([z, sin, z, z], axis=-1)
    assert tc.shape[-1] == 4 * half
    return tc, ta, tb


def _rope(x, tc, ta, tb):
    return x * tc + pltpu.roll(x, 96, 1) * ta + pltpu.roll(x, 32, 1) * tb


def _rope_t(dy, tc, ta, tb):
    return dy * tc + pltpu.roll(dy * ta, 32, 1) + pltpu.roll(dy * tb, 96, 1)


def _q_rope(q, tc, ta, tb, transpose):
    s = q.shape[0]
    bm = _row_block(s, 512)
    rot = _rope_t if transpose else _rope

    def body(q_ref, tc_ref, ta_ref, tb_ref, o_ref):
        c, a, b = tc_ref[...], ta_ref[...], tb_ref[...]
        for h in range(MLA_HEADS):
            lo = h * HEAD_QK
            o_ref[:, lo:lo + 128] = q_ref[:, lo:lo + 128].astype(BF16)
            o_ref[:, lo + 128:lo + 256] = rot(q_ref[:, lo + 128:lo + 256], c, a, b).astype(BF16)

    row = pl.BlockSpec((bm, MLA_HEADS * HEAD_QK), lambda i: (i, 0))
    tab = pl.BlockSpec((bm, 128), lambda i: (i, 0))
    return pl.pallas_call(
        body, name="q_rope_t" if transpose else "q_rope", grid=(s // bm,), in_specs=[row, tab, tab, tab],
        out_specs=row, out_shape=jax.ShapeDtypeStruct((s, MLA_HEADS * HEAD_QK), BF16),
        compiler_params=_params("parallel"),
    )(q, tc, ta, tb)


def _kv_assemble(kv, z, tc, ta, tb):
    s = kv.shape[0]
    bm = _row_block(s, 512)

    def body(kv_ref, kr_ref, tc_ref, ta_ref, tb_ref, k_ref, v_ref):
        kpe = _rope(kr_ref[...], tc_ref[...], ta_ref[...], tb_ref[...]).astype(BF16)
        for h in range(MLA_HEADS):
            lo = h * 256
            k_ref[:, lo:lo + 128] = kv_ref[:, lo:lo + 128].astype(BF16)
            k_ref[:, lo + 128:lo + 256] = kpe
            v_ref[:, h * 128:(h + 1) * 128] = kv_ref[:, lo + 128:lo + 256].astype(BF16)

    row = pl.BlockSpec((bm, 1024), lambda i: (i, 0))
    tab = pl.BlockSpec((bm, 128), lambda i: (i, 0))
    return pl.pallas_call(
        body, name="kv_assemble", grid=(s // bm,),
        in_specs=[row, pl.BlockSpec((bm, 128), lambda i: (i, 3)), tab, tab, tab],
        out_specs=[row, pl.BlockSpec((bm, 512), lambda i: (i, 0))],
        out_shape=[jax.ShapeDtypeStruct((s, 1024), BF16), jax.ShapeDtypeStruct((s, 512), BF16)],
        compiler_params=_params("parallel"),
    )(kv, z, tc, ta, tb)


def _kv_assemble_bwd(dk, dv, tc, ta, tb):
    s = dk.shape[0]
    bm = _row_block(s, 512)

    def body(dk_ref, dv_ref, tc_ref, ta_ref, tb_ref, dkv_ref, dkr_ref):
        dpe = None
        for h in range(MLA_HEADS):
            lo = h * 256
            dkv_ref[:, lo:lo + 128] = dk_ref[:, lo:lo + 128].astype(BF16)
            dkv_ref[:, lo + 128:lo + 256] = dv_ref[:, h * 128:(h + 1) * 128].astype(BF16)
            t = dk_ref[:, lo + 128:lo + 256]
            dpe = t if dpe is None else dpe + t
        dkr_ref[...] = _rope_t(dpe, tc_ref[...], ta_ref[...], tb_ref[...])

    row = pl.BlockSpec((bm, 1024), lambda i: (i, 0))
    tab = pl.BlockSpec((bm, 128), lambda i: (i, 0))
    return pl.pallas_call(
        body, name="kv_assemble_bwd", grid=(s // bm,),
        in_specs=[row, pl.BlockSpec((bm, 512), lambda i: (i, 0)), tab, tab, tab],
        out_specs=[row, tab],
        out_shape=[jax.ShapeDtypeStruct((s, 1024), BF16), jax.ShapeDtypeStruct((s, 128), F32)],
        compiler_params=_params("parallel"),
    )(dk, dv, tc, ta, tb)


def _norm_bf16(x, g):
    r = lax.rsqrt(jnp.mean(x * x, axis=-1, keepdims=True) + RMS_EPS)
    return ((x * r) * g).astype(BF16)


def _qkv_prep(z, q_gain, kv_gain, wq_t, wkv, tc, ta, tb):
    s = z.shape[0]
    bm = _row_block(s, 512)

    def body(zq_ref, zkv_ref, zkr_ref, qg_ref, kvg_ref, wq_ref, wkv_ref, tc_ref, ta_ref, tb_ref,
             qn_ref, kvn_ref, q_ref, k_ref, v_ref):
        c, a, b = tc_ref[...], ta_ref[...], tb_ref[...]
        qn = _norm_bf16(zq_ref[...], qg_ref[...])
        kvn = _norm_bf16(zkv_ref[...], kvg_ref[...])
        qn_ref[...] = qn
        kvn_ref[...] = kvn
        kpe = _rope(zkr_ref[...], c, a, b).astype(BF16)
        for h in range(MLA_HEADS):
            lo = h * HEAD_QK
            qp = _dot(qn, wq_ref[h], NT)
            q_ref[:, lo:lo + 128] = qp[:, :128].astype(BF16)
            q_ref[:, lo + 128:lo + 256] = _rope(qp[:, 128:], c, a, b).astype(BF16)
            kv = _dot(kvn, wkv_ref[h], NN)
            k_ref[:, lo:lo + 128] = kv[:, :128].astype(BF16)
            k_ref[:, lo + 128:lo + 256] = kpe
            v_ref[:, h * HEAD_V:(h + 1) * HEAD_V] = kv[:, 128:].astype(BF16)

    def cols(width, blk):
        return pl.BlockSpec((bm, width), lambda i: (i, blk))

    def whole(a):
        return pl.BlockSpec(a.shape, lambda i: (0,) * a.ndim)

    tab = cols(128, 0)
    return _call(
        body, name="qkv_prep", grid=(s // bm,),
        in_specs=[cols(Q_LORA, 0), cols(KV_LORA, 2), cols(128, 3), whole(q_gain), whole(kv_gain), whole(wq_t),
                  whole(wkv), tab, tab, tab],
        out_specs=[cols(Q_LORA, 0), cols(KV_LORA, 0), cols(1024, 0), cols(1024, 0), cols(512, 0)],
        out_shape=[jax.ShapeDtypeStruct((s, Q_LORA), BF16), jax.ShapeDtypeStruct((s, KV_LORA), BF16),
                   jax.ShapeDtypeStruct((s, 1024), BF16), jax.ShapeDtypeStruct((s, 1024), BF16),
                   jax.ShapeDtypeStruct((s, 512), BF16)],
        sem=("parallel",), args=[z, z, z, q_gain, kv_gain, wq_t, wkv, tc, ta, tb])


def _qkv_prep_bwd(dq, dk, dv, z, qn, kvn, q_gain, kv_gain, wq_t, wkv, tc, ta, tb):
    s = z.shape[0]
    bm = _row_block(s, 512)
    nsteps = s // bm

    def body(dq_ref, dk_ref, dv_ref, zq_ref, zkv_ref, qn_ref, kvn_ref, qg_ref, kvg_ref, wq_ref, wkv_ref,
             tc_ref, ta_ref, tb_ref, dz_ref, dqg_ref, dkvg_ref, dwq_ref, dwkv_ref, wq_acc, wkv_acc):
        i = pl.program_id(0)
        c, a, b = tc_ref[...], ta_ref[...], tb_ref[...]

        @pl.when(i == 0)
        def _():
            wq_acc[...] = jnp.zeros_like(wq_acc)
            wkv_acc[...] = jnp.zeros_like(wkv_acc)

        qn, kvn = qn_ref[...], kvn_ref[...]
        dqn = jnp.zeros((bm, Q_LORA), F32)
        dkvn = jnp.zeros((bm, KV_LORA), F32)
        dpe = jnp.zeros((bm, 128), F32)
        for h in range(MLA_HEADS):
            lo = h * HEAD_QK
            dqp = jnp.concatenate([dq_ref[:, lo:lo + 128].astype(BF16),
                                   _rope_t(dq_ref[:, lo + 128:lo + 256], c, a, b).astype(BF16)], axis=1)
            dqn = dqn + _dot(dqp, wq_ref[h], NN)
            wq_acc[h] += _dot(dqp, qn, TN)
            dkv = jnp.concatenate([dk_ref[:, lo:lo + 128].astype(BF16),
                                   dv_ref[:, h * HEAD_V:(h + 1) * HEAD_V].astype(BF16)], axis=1)
            dkvn = dkvn + _dot(dkv, wkv_ref[h], NT)
            wkv_acc[h] += _dot(kvn, dkv, TN)
            dpe = dpe + dk_ref[:, lo + 128:lo + 256]
        dcq, dqg = _rms_bwd_math(dqn, zq_ref[...], qg_ref[...], Q_LORA)
        dckv, dkvg = _rms_bwd_math(dkvn, zkv_ref[...], kvg_ref[...], KV_LORA)
        dz_ref[:, 0:Q_LORA] = dcq.astype(BF16)
        dz_ref[:, Q_LORA:Q_LORA + KV_LORA] = dckv.astype(BF16)
        dz_ref[:, Q_LORA + KV_LORA:512] = _rope_t(dpe, c, a, b).astype(BF16)

        @pl.when(i == 0)
        def _():
            dqg_ref[...] = dqg
            dkvg_ref[...] = dkvg

        @pl.when(i > 0)
        def _():
            dqg_ref[...] += dqg
            dkvg_ref[...] += dkvg

        @pl.when(i == nsteps - 1)
        def _():
            dwq_ref[...] = wq_acc[...].astype(BF16)
            dwkv_ref[...] = wkv_acc[...].astype(BF16)

    def cols(width, blk):
        return pl.BlockSpec((bm, width), lambda i: (i, blk))

    def whole(shape):
        return pl.BlockSpec(shape, lambda i: (0,) * len(shape))

    tab = cols(128, 0)
    return _call(
        body, name="qkv_prep_bwd", grid=(nsteps,),
        in_specs=[cols(1024, 0), cols(1024, 0), cols(512, 0), cols(Q_LORA, 0), cols(KV_LORA, 2), cols(Q_LORA, 0),
                  cols(KV_LORA, 0), whole(q_gain.shape), whole(kv_gain.shape), whole(wq_t.shape), whole(wkv.shape),
                  tab, tab, tab],
        out_specs=[cols(512, 0), whole(q_gain.shape), whole(kv_gain.shape), whole(wq_t.shape), whole(wkv.shape)],
        out_shape=[jax.ShapeDtypeStruct((s, 512), BF16), jax.ShapeDtypeStruct(q_gain.shape, F32),
                   jax.ShapeDtypeStruct(kv_gain.shape, F32), jax.ShapeDtypeStruct(wq_t.shape, BF16),
                   jax.ShapeDtypeStruct(wkv.shape, BF16)],
        scratch_shapes=[pltpu.VMEM(wq_t.shape, F32), pltpu.VMEM(wkv.shape, F32)],
        sem=("arbitrary",), args=[dq, dk, dv, z, z, qn, kvn, q_gain, kv_gain, wq_t, wkv, tc, ta, tb])


def _causal_mask(s, row0, col0):
    rows = row0 + lax.broadcasted_iota(jnp.int32, s.shape, 0)
    cols = col0 + lax.broadcasted_iota(jnp.int32, s.shape, 1)
    return jnp.where(cols <= rows, s, -jnp.inf)


def _attn_fwd(name, q, k, k_off, v, v_off, nh, dq, dv, scale, causal, blk):
    sq, sk = q.shape[0], k.shape[0]
    bq = min(blk, sq)
    bk = min(blk, sk)
    nkv = sk // bk
    assert not causal or (sq == sk and bq == bk)

    hq = bq
    log2e = 1.4426950408889634
    c2 = scale * log2e

    def body(q_ref, k_ref, v_ref, o_ref, lse_ref):
        qi = pl.program_id(1)
        qs = (q_ref[...],)

        def step(j, carry, masked):
            rows = pl.ds(pl.multiple_of(j * bk, bk), bk)
            kb, vb = k_ref[rows, :], v_ref[rows, :]
            out = []
            for t, (m, l, acc) in enumerate(carry):
                s = _dot(qs[t], kb, NT) * c2
                if masked:
                    s = _causal_mask(s, qi * bq + t * hq, j * bk)
                m_new = jnp.maximum(m, jnp.max(s, axis=-1, keepdims=True))
                alpha = jnp.exp2(m - m_new)
                p = jnp.exp2(s - m_new)
                l = alpha * l + jnp.sum(p, axis=-1, keepdims=True)
                acc = alpha * acc + _dot(p, vb, NN)
                out.append((m_new, l, acc))
            return tuple(out)

        one = (jnp.full((hq, 1), -jnp.inf, F32), jnp.zeros((hq, 1), F32), jnp.zeros((hq, dv), F32))
        init = (one,)
        if causal:
            carry = lax.fori_loop(0, qi, lambda j, c: step(j, c, False), init)
            fin = step(qi, carry, True)
        else:
            fin = lax.fori_loop(0, nkv, lambda j, c: step(j, c, False), init)
        for t, (m, l, acc) in enumerate(fin):
            o_ref[t * hq:(t + 1) * hq, :] = (acc / l).astype(o_ref.dtype)
            lse_ref[t * hq:(t + 1) * hq, :] = m * (1.0 / log2e) + jnp.log(l)

    return _call(
        body, name=name, grid=(nh, sq // bq),
        in_specs=[pl.BlockSpec((bq, dq), lambda h, i: (i, h)),
                  pl.BlockSpec((sk, dq), lambda h, i: (0, k_off + h)),
                  pl.BlockSpec((sk, dv), lambda h, i: (0, v_off + h))],
        out_specs=[pl.BlockSpec((bq, dv), lambda h, i: (i, h)), pl.BlockSpec((None, bq, 1), lambda h, i: (h, i, 0))],
        out_shape=[jax.ShapeDtypeStruct((sq, nh * dv), BF16), jax.ShapeDtypeStruct((nh, sq, 1), F32)],
        sem=("parallel", "parallel"), args=[q, k, v])


def _attn_delta(name, do, do_off, o, nh, dv):
    s = o.shape[0]
    bm = _row_block(s, 512)

    def body(do_ref, o_ref, d_ref):
        d_ref[...] = jnp.sum(do_ref[...].astype(F32) * o_ref[...].astype(F32), axis=-1, keepdims=True)

    return pl.pallas_call(
        body, name=name, grid=(nh, s // bm),
        in_specs=[pl.BlockSpec((bm, dv), lambda h, i: (i, do_off + h)), pl.BlockSpec((bm, dv), lambda h, i: (i, h))],
        out_specs=pl.BlockSpec((None, bm, 1), lambda h, i: (h, i, 0)),
        out_shape=jax.ShapeDtypeStruct((nh, s, 1), F32),
        compiler_params=_params("parallel", "parallel"),
    )(do, o)


def _attn_bwd(name, q, k, k_off, v, v_off, do, do_off, lse, delta, nh, dq, dv, scale, causal, blk):
    sq, sk = q.shape[0], k.shape[0]
    bq = min(blk, sq)
    bk = min(blk, sk)
    nq = sq // bq
    assert not causal or (sq == sk and bq == bk)

    def body(q_ref, k_ref, v_ref, do_ref, lse_ref, dl_ref, dq_ref, dk_ref, dv_ref, dk_acc, dv_acc):
        j = pl.program_id(1)

        @pl.when(j == 0)
        def _():
            dq_ref[...] = jnp.zeros_like(dq_ref)

        dk_acc[...] = jnp.zeros_like(dk_acc)
        dv_acc[...] = jnp.zeros_like(dv_acc)
        kv = k_ref[...]
        vv = v_ref[...]

        def step(i, masked):
            rows = pl.ds(pl.multiple_of(i * bq, bq), bq)
            qv = q_ref[rows, :]
            dov = do_ref[rows, :].astype(BF16)
            s = _dot(qv, kv, NT) * scale
            if masked:
                s = _causal_mask(s, i * bq, j * bk)
            p = jnp.exp(s - lse_ref[rows, :])
            dp = _dot(dov, vv, NT)
            ds = (p * (dp - dl_ref[rows, :]) * scale).astype(BF16)
            dv_acc[...] += _dot(p, dov, TN)
            dk_acc[...] += _dot(ds, qv, TN)
            dq_ref[rows, :] += _dot(ds, kv, NN)

        if causal:
            step(j, True)

            def loop(i, c):
                step(i, False)
                return c

            lax.fori_loop(j + 1, nq, loop, 0)
        else:
            def loop(i, c):
                step(i, False)
                return c

            lax.fori_loop(0, nq, loop, 0)
        dk_ref[...] = dk_acc[...]
        dv_ref[...] = dv_acc[...]

    stat = pl.BlockSpec((None, sq, 1), lambda h, j: (h, 0, 0))
    return _call(
        body, name=name, grid=(nh, sk // bk),
        in_specs=[pl.BlockSpec((sq, dq), lambda h, j: (0, h)),
                  pl.BlockSpec((bk, dq), lambda h, j: (j, k_off + h)),
                  pl.BlockSpec((bk, dv), lambda h, j: (j, v_off + h)),
                  pl.BlockSpec((sq, dv), lambda h, j: (0, do_off + h)), stat, stat],
        out_specs=[pl.BlockSpec((sq, dq), lambda h, j: (0, h)),
                   pl.BlockSpec((bk, dq), lambda h, j: (j, h)),
                   pl.BlockSpec((bk, dv), lambda h, j: (j, h))],
        out_shape=[jax.ShapeDtypeStruct((sq, nh * dq), F32), jax.ShapeDtypeStruct((sk, nh * dq), F32),
                   jax.ShapeDtypeStruct((sk, nh * dv), F32)],
        scratch_shapes=[pltpu.VMEM((bk, dq), F32), pltpu.VMEM((bk, dv), F32)],
        sem=("parallel", "arbitrary"), args=[q, k, v, do, lse, delta])


def _pool_diff(z, g):
    s = z.shape[0]
    t = lax.broadcasted_iota(jnp.int32, z.shape, 0)
    acc = z
    sums = []
    for k in (1, 2, 4, 8):
        acc = acc + jnp.where(t >= k, pltpu.roll(acc, k, 0), 0.0)
        sums.append(acc)
    win = jnp.where(g == 0, sums[0], jnp.where(g == 1, sums[1], jnp.where(g == 2, sums[2], sums[3])))
    w = lax.shift_left(jnp.int32(2), g)
    count = jnp.minimum(t + 1, w).astype(F32)
    del s
    return win / count - z, count


def _pool_fwd(z, pool_w, pool_scale):
    s = z.shape[0]

    def body(z_ref, w_ref, sc_ref, o_ref):
        diff, _ = _pool_diff(z_ref[...], pl.program_id(0))
        o_ref[...] = (_dot(diff, w_ref[...], NN) * sc_ref[...]).astype(o_ref.dtype)

    return _call(
        body, name="pool_fwd", grid=(POOL_GROUPS,),
        in_specs=[pl.BlockSpec((s, POOL_CH), lambda g: (0, 4 + g)),
                  pl.BlockSpec((None, POOL_CH, POOL_CH), lambda g: (g, 0, 0)),
                  pl.BlockSpec((1, POOL_CH), lambda g: (0, g))],
        out_specs=[pl.BlockSpec((s, POOL_CH), lambda g: (0, g))],
        out_shape=[jax.ShapeDtypeStruct((s, POOL_GROUPS * POOL_CH), BF16)],
        sem=("parallel",), args=[z, pool_w, pool_scale])[0]


def _pool_bwd(dcat, z, pool_w, pool_scale):
    s = z.shape[0]

    def body(dp_ref, z_ref, w_ref, sc_ref, dz_ref, dw_ref, dsc_ref):
        g = pl.program_id(0)
        diff, count = _pool_diff(z_ref[...], g)
        dpf = dp_ref[...].astype(F32)
        u = _dot(diff, w_ref[...], NN)
        dsc_ref[...] = jnp.sum(dpf * u, axis=0, keepdims=True)
        du = (dpf * sc_ref[...]).astype(BF16)
        dw_ref[...] = _dot(diff, du, TN)
        ddiff = _dot(du, w_ref[...], NT)
        t = lax.broadcasted_iota(jnp.int32, ddiff.shape, 0)
        acc = ddiff / count
        sums = []
        for k in (1, 2, 4, 8):
            acc = acc + jnp.where(t < s - k, pltpu.roll(acc, s - k, 0), 0.0)
            sums.append(acc)
        win = jnp.where(g == 0, sums[0], jnp.where(g == 1, sums[1], jnp.where(g == 2, sums[2], sums[3])))
        dz_ref[...] = win - ddiff

    return pl.pallas_call(
        body, name="pool_bwd", grid=(POOL_GROUPS,),
        in_specs=[pl.BlockSpec((s, POOL_CH), lambda g: (0, 4 + g)),
                  pl.BlockSpec((s, POOL_CH), lambda g: (0, 4 + g)),
                  pl.BlockSpec((None, POOL_CH, POOL_CH), lambda g: (g, 0, 0)),
                  pl.BlockSpec((1, POOL_CH), lambda g: (0, g))],
        out_specs=[pl.BlockSpec((s, POOL_CH), lambda g: (0, g)),
                   pl.BlockSpec((None, POOL_CH, POOL_CH), lambda g: (g, 0, 0)),
                   pl.BlockSpec((1, POOL_CH), lambda g: (0, g))],
        out_shape=[jax.ShapeDtypeStruct((s, POOL_GROUPS * POOL_CH), F32),
                   jax.ShapeDtypeStruct((POOL_GROUPS, POOL_CH, POOL_CH), F32),
                   jax.ShapeDtypeStruct((1, POOL_GROUPS * POOL_CH), F32)],
        compiler_params=_params("parallel"),
    )(dcat, z, pool_w, pool_scale)


def _local_step(x, mem, positions, target, w, grads):
    tc, ta, tb = _rope_tables(positions)
    blk = _ATT_BLOCK

    n1 = _rmsnorm_fwd("ffn1_norm", x, w["ffn1_norm"], D_MODEL)
    if "ffn1_shards" in w:
        a1, dadu1, dadg1, w["ffn1_w_gate"], w["ffn1_w_up"], w["ffn1_w_down"] = _ffn1_up_gather(n1, *w["ffn1_shards"])
    else:
        a1, dadu1, dadg1 = _ffn_up("ffn1_up", n1, w["ffn1_w_gate"], w["ffn1_w_up"])
    h1, n2 = _ffn_down("ffn1_down", a1, w["ffn1_w_down"], x, w["mix_norm"])
    z = _w_in_fwd(n2, w["w_in"])
    qn, kvn, qf, kf, vf = _qkv_prep(z, w["q_norm"], w["kv_norm"], w["w_q_up"], w["w_kv_up"], tc, ta, tb)
    att, lse = _attn_fwd("mla_fwd", qf, kf, 0, vf, 0, MLA_HEADS, HEAD_QK, HEAD_V, MLA_SCALE, True, blk)
    pool = _pool_fwd(z, w["pool_w"], w["pool_scale"])
    s = x.shape[0]
    bm = _row_block(s)
    row = pl.BlockSpec((bm, D_MODEL), lambda i, k: (i, 0))
    half = pl.BlockSpec((bm, 512), lambda i, k: (i, 0))
    h2, n3 = _matmul(
        "w_out", (s // bm, 1),
        [(att, half, w["w_out"], pl.BlockSpec((512, D_MODEL), lambda i, k: (0, 0)), NN),
         (pool, half, w["w_out"], pl.BlockSpec((512, D_MODEL), lambda i, k: (1, 0)), NN)],
        [(h1, row)] + _residual_outs(s, bm, w["xattn_norm"])[0], _residual_outs(s, bm, w["xattn_norm"])[1],
        _residual_epilogue(1.0, True), None)
    memn = _rmsnorm_fwd("mem_norm", mem, w["mem_norm"], D_MODEL)
    qm = _mm_nn("w_mq", n3, w["w_mq"], BF16)
    kvm = _mm_heads_fwd("w_mkv", memn, w["w_mkv"], BF16)
    om, lse_m = _attn_fwd("xattn_fwd", qm, kvm, 0, kvm, MEM_HEADS, MEM_HEADS, MEM_HEAD_DIM, MEM_HEAD_DIM,
                          MEM_SCALE, False, blk)
    h3, n4 = _mm_nn("w_mo", om, w["w_mo"], F32, res=h2, gain=w["ffn2_norm"])
    a2, dadu2, dadg2 = _ffn_up("ffn2_up", n4, w["ffn2_w_gate"], w["ffn2_w_up"])
    h4 = _ffn_down("ffn2_down", a2, w["ffn2_w_down"], h3)

    dh4, dh4b, loss_vec, d_final = _loss_and_final_norm(h4, w["final_norm"], target)
    grads["final_norm"] = d_final

    dh3, dh3b, grads["ffn2_norm"] = _ffn_bwd("ffn2", dh4b, n4, dadg2, dadu2, a2, w["ffn2_w_gate"], w["ffn2_w_up"],
                                             w["ffn2_w_down"], grads, norm_bwd=(h3, w["ffn2_norm"], dh4))

    dom, delta_m = _mm_nt("w_mo_dx", dh3b, w["w_mo"], BF16, attn_out=om, nh=MEM_HEADS, dv=MEM_HEAD_DIM)
    grads["w_mo"] = _mm_tn("w_mo_dw", om, dh3b)
    dqm, dkm, dvm = _attn_bwd("xattn_bwd", qm, kvm, 0, kvm, MEM_HEADS, dom, 0, lse_m, delta_m, MEM_HEADS,
                              MEM_HEAD_DIM, MEM_HEAD_DIM, MEM_SCALE, False, blk)
    dkvm = jnp.concatenate([dkm, dvm], axis=1).astype(BF16)
    dh2, dh2b, grads["xattn_norm"] = _mm_nt_norm_bwd("w_mq_dx", dqm, w["w_mq"], h2, w["xattn_norm"], dh3)
    grads["w_mq"] = _mm_tn("w_mq_dw", n3, dqm)
    dmemn, grads["w_mkv"] = _mm_heads_bwd("w_mkv", dkvm, memn, w["w_mkv"])
    _, grads["mem_norm"] = _rmsnorm_bwd("mem_norm_bwd", dmemn, mem, w["mem_norm"], D_MODEL, out_dtype=BF16)

    dcat, delta = _mm_nt("w_out_dx", dh2b, w["w_out"], BF16, attn_out=att, nh=MLA_HEADS, dv=HEAD_V)
    grads["w_out"] = jnp.concatenate([_mm_tn("w_out_dw_a", att, dh2b), _mm_tn("w_out_dw_p", pool, dh2b)], axis=0)
    dzp, grads["pool_w"], grads["pool_scale"] = _pool_bwd(dcat, z, w["pool_w"], w["pool_scale"])
    dqf, dkf, dvf = _attn_bwd("mla_bwd", qf, kf, 0, vf, 0, dcat, 0, lse, delta, MLA_HEADS, HEAD_QK, HEAD_V,
                              MLA_SCALE, True, blk)
    dz_lat, grads["q_norm"], grads["kv_norm"], grads["w_q_up"], grads["w_kv_up"] = _qkv_prep_bwd(
        dqf, dkf, dvf, z, qn, kvn, w["q_norm"], w["kv_norm"], w["w_q_up"], w["w_kv_up"], tc, ta, tb)
    dz = jnp.concatenate([dz_lat, dzp.astype(BF16)], axis=1)
    grads["w_in"] = _w_in_dw(dz, n2)
    dh1, dh1b, grads["mix_norm"] = _w_in_dx_norm_bwd(dz, w["w_in"], h1, w["mix_norm"], dh2)

    dn1 = _ffn_bwd("ffn1", dh1b, n1, dadg1, dadu1, a1, w["ffn1_w_gate"], w["ffn1_w_up"], w["ffn1_w_down"], grads)
    dx, grads["ffn1_norm"], _ = _rmsnorm_bwd("ffn1_norm_bwd", dn1, x, w["ffn1_norm"], D_MODEL, dres=dh1)
    return loss_vec[0, 0], dx


def _mesh_pos():
    x, y, c = lax.axis_index("x"), lax.axis_index("y"), lax.axis_index("c")
    chips = [(1 - x, y), (x, 1 - y), (1 - x, 1 - y)]
    chip_ids = [2 * cx + cy for cx, cy in chips]
    return x, y, c, 2 * x + y, chips, chip_ids


def _half_rows(c, rows):
    hr = rows // 2
    return pl.ds(pl.multiple_of(c * hr, 16), hr), pl.ds(pl.multiple_of((1 - c) * hr, 16), hr)


def _ag_ici_stage(shards, relative=False):
    n = len(shards)

    def copies(ins, outs):
        x, y, c, me, chips, _ = _mesh_pos()
        out = []
        for k in range(n):
            mine, _ = _half_rows(c, ins[k].shape[0])
            out.append((ins[k], outs[k].at[0 if relative else me], None))
            for j, (cx, cy) in enumerate(chips):
                slab = _REL_OF_PEER[j] if relative else me
                out.append((ins[k].at[mine], outs[k].at[slab, mine], (cx, cy, c)))
        return out

    return _Stage(shards, [jax.ShapeDtypeStruct((N_CHIPS,) + s.shape, s.dtype) for s in shards], 3 * n, n, copies)


def _quarter_rows(c, rows):
    qr = rows // 4
    return pl.ds(pl.multiple_of(c * 2 * qr, 16), qr), pl.ds(pl.multiple_of(c * 2 * qr + qr, 16), qr)


def _ag_fwd_stage(fulls, relative=False):
    n = len(fulls)

    def copies(ins, outs):
        x, y, c, me, chips, chip_ids = _mesh_pos()
        out = []
        for k in range(n):
            q0, q1 = _quarter_rows(c, ins[k].shape[1])
            from_x, from_y, diag = (2, 1, 3) if relative else chip_ids
            out.append((ins[k].at[from_x, q0], outs[k].at[diag if relative else from_x, q0], (*chips[1], c)))
            out.append((ins[k].at[from_y, q1], outs[k].at[diag if relative else from_y, q1], (*chips[0], c)))
        return out

    return _Stage(fulls, [jax.ShapeDtypeStruct(f.shape, f.dtype) for f in fulls], 2 * n, 0, copies,
                  aliases={k: k for k in range(n)})


def _ag_d2d_stage(fulls, relative=False):
    n = len(fulls)

    def copies(ins, outs):
        x, y, c, me, _, chip_ids = _mesh_pos()
        out = []
        for k in range(n):
            mine, _ = _half_rows(c, ins[k].shape[1])
            for j in range(3):
                slab = _REL_OF_PEER[j] if relative else chip_ids[j]
                out.append((ins[k].at[slab, mine], outs[k].at[slab, mine], (x, y, 1 - c)))
        return out

    return _Stage(fulls, [jax.ShapeDtypeStruct(f.shape, f.dtype) for f in fulls], 3 * n, 0, copies,
                  aliases={k: k for k in range(n)})


def _rs_swap_stage(grads):
    n = len(grads)

    def copies(ins, outs):
        x, y, c, _, _, _ = _mesh_pos()
        out = []
        for k in range(n):
            _, other = _half_rows(c, ins[k].shape[1])
            out.append((ins[k].at[:, other, :], outs[k], (x, y, 1 - c)))
        return out

    return _Stage(grads, [jax.ShapeDtypeStruct((N_CHIPS, g.shape[1] // 2, g.shape[2]), g.dtype) for g in grads],
                  n, 0, copies)


_REL_OF_PEER = (2, 1, 3)


def _rs_scatter_stage(sums, relative=False):
    n = len(sums)

    def copies(ins, outs):
        x, y, c, me, chips, chip_ids = _mesh_pos()
        out = []
        for k in range(n):
            mine, _ = _half_rows(c, 2 * ins[k].shape[1])
            out.append((ins[k].at[0 if relative else me], outs[k].at[0, mine, :], None))
            for j, (cx, cy) in enumerate(chips):
                slab = _REL_OF_PEER[j] if relative else chip_ids[j]
                out.append((ins[k].at[slab], outs[k].at[1 + j, mine, :], (cx, cy, c)))
        return out

    return _Stage(sums, [jax.ShapeDtypeStruct((N_CHIPS, 2 * s.shape[1], s.shape[2]), s.dtype) for s in sums],
                  3 * n, n, copies)


def _rs_mirror_stage(parts):
    n = len(parts)

    def copies(ins, outs):
        x, y, c, _, _, _ = _mesh_pos()
        out = []
        for k in range(n):
            mine, _ = _half_rows(c, ins[k].shape[1])
            out.append((ins[k].at[:, mine, :], outs[k].at[:, mine, :], (x, y, 1 - c)))
        return out

    return _Stage(parts, [jax.ShapeDtypeStruct(p.shape, p.dtype) for p in parts], n, 0, copies,
                  aliases={k: k for k in range(n)})


def _pair_add(name, g, r1, core):
    _, rows, cols = g.shape
    hr = rows // 2

    def body(c_ref, g_ref, r_ref, o_ref):
        o_ref[...] = (g_ref[...].astype(F32) + r_ref[...].astype(F32)).astype(BF16)

    half = pl.BlockSpec((None, hr, cols), lambda j, c: (j, 0, 0))
    return pl.pallas_call(
        body, name=name,
        grid_spec=pltpu.PrefetchScalarGridSpec(
            num_scalar_prefetch=1, grid=(N_CHIPS,),
            in_specs=[pl.BlockSpec((None, hr, cols), lambda j, c: (j, c[0], 0)), half], out_specs=half),
        out_shape=jax.ShapeDtypeStruct((N_CHIPS, hr, cols), BF16),
        compiler_params=_params("parallel"),
    )(core, g, r1)


def _all_gather_weights(shards):
    n = len(shards)

    def body(*refs):
        ins, outs = refs[:n], refs[n:2 * n]
        send, recv, loc = refs[2 * n:]
        x, y, c, me, chips, chip_ids = _mesh_pos()
        sib = (x, y, 1 - c)

        def halves(k):
            hr = ins[k].shape[0] // 2
            return pl.ds(pl.multiple_of(c * hr, 16), hr), pl.ds(pl.multiple_of((1 - c) * hr, 16), hr)

        def remote(src, dst, k, j, dev):
            return pltpu.make_async_remote_copy(src_ref=src, dst_ref=dst, send_sem=send.at[k, j],
                                                recv_sem=recv.at[k, j], device_id=dev, device_id_type=_MESH)

        started = []
        local = []
        for k in range(n):
            mine, _ = halves(k)
            cp = pltpu.make_async_copy(ins[k], outs[k].at[me], loc.at[k])
            cp.start()
            local.append(cp)
            for j, (cx, cy) in enumerate(chips):
                cp = remote(ins[k].at[mine], outs[k].at[me, mine], k, j, (cx, cy, c))
                cp.start()
                started.append(cp)
        for k in range(n):
            mine, _ = halves(k)
            for j in range(3):
                land = outs[k].at[chip_ids[j], mine]
                remote(land, land, k, j, sib).wait_recv()
                cp = remote(land, land, k, 3 + j, sib)
                cp.start()
                started.append(cp)
        for k in range(n):
            _, other = halves(k)
            for j in range(3):
                land = outs[k].at[chip_ids[j], other]
                remote(land, land, k, 3 + j, sib).wait_recv()
        for cp in started:
            cp.wait_send()
        for cp in local:
            cp.wait()

    return pl.pallas_call(
        body, name="all_gather_weights", in_specs=[_ANY] * n, out_specs=[_ANY] * n,
        out_shape=[jax.ShapeDtypeStruct((N_CHIPS,) + s.shape, s.dtype) for s in shards],
        scratch_shapes=[pltpu.SemaphoreType.DMA((n, 6)), pltpu.SemaphoreType.DMA((n, 6)),
                        pltpu.SemaphoreType.DMA((n,))],
        compiler_params=pltpu.CompilerParams(vmem_limit_bytes=V7X_VMEM_LIMIT_BYTES),
    )(*shards)


_RS_CHUNK = 32


def _reduce_scatter(name, grads):
    n = len(grads)

    def body(*refs):
        gs, outs = refs[:n], refs[n:2 * n]
        own, r1, r2, fin = (refs[(2 + i) * n:(3 + i) * n] for i in range(4))
        a_send, a_recv, b_send, b_recv, c_send, c_recv, l_in, l_out = refs[6 * n:]
        x, y, c, me, chips, chip_ids = _mesh_pos()
        sib = (x, y, 1 - c)

        def halves(k):
            hr = gs[k].shape[1] // 2
            return hr, pl.ds(pl.multiple_of(c * hr, 16), hr), pl.ds(pl.multiple_of((1 - c) * hr, 16), hr)

        def remote(src, dst, ssem, rsem, dev):
            return pltpu.make_async_remote_copy(src_ref=src, dst_ref=dst, send_sem=ssem, recv_sem=rsem,
                                                device_id=dev, device_id_type=_MESH)

        sends, locals_in = [], []
        for k in range(n):
            hr, mine, other = halves(k)
            cp = remote(gs[k].at[:, other, :], r1[k], a_send.at[k], a_recv.at[k], sib)
            cp.start()
            sends.append(cp)
            cp = pltpu.make_async_copy(gs[k].at[:, mine, :], own[k], l_in.at[k])
            cp.start()
            locals_in.append(cp)

        for k in range(n):
            hr, mine, other = halves(k)
            locals_in[k].wait()
            remote(r1[k], r1[k], a_send.at[k], a_recv.at[k], sib).wait_recv()
            for j in range(N_CHIPS):
                def add(i, carry, k=k, j=j):
                    rows = pl.ds(pl.multiple_of(i * _RS_CHUNK, _RS_CHUNK), _RS_CHUNK)
                    own[k][j, rows, :] = (own[k][j, rows, :].astype(F32) + r1[k][j, rows, :].astype(F32)).astype(BF16)
                    return carry

                lax.fori_loop(0, hr // _RS_CHUNK, add, 0)
            for j, (cx, cy) in enumerate(chips):
                cp = remote(own[k].at[chip_ids[j]], r2[k].at[j], b_send.at[k, j], b_recv.at[k, j], (cx, cy, c))
                cp.start()
                sends.append(cp)

        locals_out = []
        for k in range(n):
            hr, mine, other = halves(k)
            for j in range(3):
                remote(r2[k].at[j], r2[k].at[j], b_send.at[k, j], b_recv.at[k, j], sib).wait_recv()

            def total(i, carry, k=k):
                rows = pl.ds(pl.multiple_of(i * _RS_CHUNK, _RS_CHUNK), _RS_CHUNK)
                acc = own[k][me, rows, :].astype(F32)
                for j in range(3):
                    acc = acc + r2[k][j, rows, :].astype(F32)
                fin[k][rows, :] = acc
                return carry

            lax.fori_loop(0, hr // _RS_CHUNK, total, 0)
            cp = remote(fin[k], outs[k].at[mine, :], c_send.at[k], c_recv.at[k], sib)
            cp.start()
            sends.append(cp)
            cp = pltpu.make_async_copy(fin[k], outs[k].at[mine, :], l_out.at[k])
            cp.start()
            locals_out.append(cp)

        for k in range(n):
            hr, mine, other = halves(k)
            land = outs[k].at[other, :]
            remote(land, land, c_send.at[k], c_recv.at[k], sib).wait_recv()
        for cp in sends:
            cp.wait_send()
        for cp in locals_out:
            cp.wait()

    scratch = []
    for g in grads:
        scratch.append(pltpu.VMEM((N_CHIPS, g.shape[1] // 2, g.shape[2]), BF16))
    for g in grads:
        scratch.append(pltpu.VMEM((N_CHIPS, g.shape[1] // 2, g.shape[2]), BF16))
    for g in grads:
        scratch.append(pltpu.VMEM((3, g.shape[1] // 2, g.shape[2]), BF16))
    for g in grads:
        scratch.append(pltpu.VMEM((g.shape[1] // 2, g.shape[2]), F32))
    dma = pltpu.SemaphoreType.DMA
    scratch += [dma((n,)), dma((n,)), dma((n, 3)), dma((n, 3)), dma((n,)), dma((n,)), dma((n,)), dma((n,))]
    return pl.pallas_call(
        body, name=name, in_specs=[_ANY] * n, out_specs=[_ANY] * n,
        out_shape=[jax.ShapeDtypeStruct(g.shape[1:], F32) for g in grads],
        scratch_shapes=scratch,
        compiler_params=pltpu.CompilerParams(vmem_limit_bytes=V7X_VMEM_LIMIT_BYTES),
    )(*grads)


def _adamw_math(w, g, m, v):
    m = ADAM_B1 * m + (1.0 - ADAM_B1) * g
    v = ADAM_B2 * v + (1.0 - ADAM_B2) * (g * g)
    m_hat = m / (1.0 - ADAM_B1 ** ADAM_STEP)
    v_hat = v / (1.0 - ADAM_B2 ** ADAM_STEP)
    delta = -ADAM_LR * (m_hat / (jnp.sqrt(v_hat) + ADAM_EPS) + ADAM_WD * w)
    return delta, m, v


def _adamw_sum(name, w, parts, m, v):
    r, c = w.shape
    br = r
    while br * c * 4 > (1 << 20) and br % 32 == 0:
        br //= 2

    def body(w_ref, p_ref, m_ref, v_ref, g_ref, d_ref, nm_ref, nv_ref):
        g = p_ref[0].astype(F32)
        for j in range(1, N_CHIPS):
            g = g + p_ref[j].astype(F32)
        d, nm, nv = _adamw_math(w_ref[...], g, m_ref[...], v_ref[...])
        g_ref[...] = g
        d_ref[...] = d
        nm_ref[...] = nm
        nv_ref[...] = nv

    spec = pl.BlockSpec((br, c), lambda i: (i, 0))
    shp = jax.ShapeDtypeStruct((r, c), F32)
    return _call(
        body, name=name, grid=(r // br,),
        in_specs=[spec, pl.BlockSpec((N_CHIPS, br, c), lambda i: (0, i, 0)), spec, spec],
        out_specs=[spec] * 4, out_shape=[shp] * 4, sem=("parallel",), args=[w, parts, m, v])


_SMALL_ROWS = 80


def _small_allreduce_adamw(gpack, wpack, mpack, vpack):
    half = _SMALL_ROWS // 2

    def body(g_ref, w_ref, m_ref, v_ref, go_ref, d_ref, nm_ref, nv_ref, sib_buf, chip_sum, buf, send, recv):
        x, y, c, me, chips, chip_ids = _mesh_pos()
        sib = (x, y, 1 - c)
        mine = pl.ds(pl.multiple_of(c * half, 8), half)

        def remote(src, dst, k, dev):
            return pltpu.make_async_remote_copy(src_ref=src, dst_ref=dst, send_sem=send.at[k], recv_sem=recv.at[k],
                                                device_id=dev, device_id_type=_MESH)

        swap = remote(g_ref, sib_buf, 0, sib)
        swap.start()
        swap.wait()
        chip_sum[...] = g_ref[...] + sib_buf[...]
        buf[me] = chip_sum[...]
        sends = [remote(chip_sum.at[mine], buf.at[me, mine], 1 + j, (cx, cy, c)) for j, (cx, cy) in enumerate(chips)]
        for cp in sends:
            cp.start()
        for cp in sends:
            cp.wait()
        mirrors = [remote(buf.at[chip_ids[j], mine], buf.at[chip_ids[j], mine], 4 + j, sib) for j in range(3)]
        for cp in mirrors:
            cp.start()
        for cp in mirrors:
            cp.wait()
        total = buf[0]
        for i in range(1, N_CHIPS):
            total = total + buf[i]
        go_ref[...] = total
        d, nm, nv = _adamw_math(w_ref[...], total, m_ref[...], v_ref[...])
        d_ref[...] = d
        nm_ref[...] = nm
        nv_ref[...] = nv

    vm = pl.BlockSpec(memory_space=pltpu.VMEM)
    shp = jax.ShapeDtypeStruct((_SMALL_ROWS, D_MODEL), F32)
    return pl.pallas_call(
        body, name="small_allreduce_adamw", in_specs=[vm] * 4, out_specs=[vm] * 4, out_shape=[shp] * 4,
        scratch_shapes=[pltpu.VMEM((_SMALL_ROWS, D_MODEL), F32), pltpu.VMEM((_SMALL_ROWS, D_MODEL), F32),
                        pltpu.VMEM((N_CHIPS, _SMALL_ROWS, D_MODEL), F32), pltpu.SemaphoreType.DMA((7,)),
                        pltpu.SemaphoreType.DMA((7,))],
        compiler_params=pltpu.CompilerParams(vmem_limit_bytes=V7X_VMEM_LIMIT_BYTES),
    )(gpack, wpack, mpack, vpack)


_SMALL_VECTORS = ("ffn1_norm", "mix_norm", "xattn_norm", "mem_norm", "ffn2_norm", "final_norm", "q_norm",
                  "kv_norm", "pool_scale")


_LOSS_ROW = 9


def _pack_small(d, scalar=None):
    rows = []
    for n in _SMALL_VECTORS:
        v = d[n].reshape(1, -1).astype(F32)
        rows.append(jnp.pad(v, ((0, 0), (0, D_MODEL - v.shape[1]))))
    assert len(rows) == _LOSS_ROW
    extra = jnp.zeros((1, D_MODEL), F32) if scalar is None else jnp.pad(scalar.reshape(1, 1), ((0, 0), (0, D_MODEL - 1)))
    rows.append(extra)
    rows.append(jnp.zeros((16 - len(rows), D_MODEL), F32))
    rows.append(d["pool_w"].reshape(64, D_MODEL).astype(F32))
    return jnp.concatenate(rows, axis=0)


def _unpack_small(pack, like):
    out = {}
    for i, n in enumerate(_SMALL_VECTORS):
        out[n] = pack[i, :like[n].size].reshape(like[n].shape)
    out["pool_w"] = pack[16:].reshape(like["pool_w"].shape)
    return out


_WEIGHTS = ("ffn1_norm", "ffn1_w_gate", "ffn1_w_up", "ffn1_w_down", "mix_norm", "w_in", "q_norm", "w_q_up",
            "kv_norm", "w_kv_up", "pool_w", "pool_scale", "w_out", "xattn_norm", "mem_norm", "w_mq", "w_mkv",
            "w_mo", "ffn2_norm", "ffn2_w_gate", "ffn2_w_up", "ffn2_w_down", "final_norm")
_SHARDED = ("ffn1_w_gate", "ffn1_w_up", "ffn1_w_down", "w_in", "w_q_up", "w_kv_up", "w_out", "w_mq", "w_mkv",
            "w_mo", "ffn2_w_gate", "ffn2_w_up", "ffn2_w_down")
_RS_GROUPS = (("ffn2_w_gate", "ffn2_w_up", "ffn2_w_down"),
              ("w_mo", "w_mq", "w_mkv", "w_out", "w_q_up", "w_kv_up", "w_in"),
              ("ffn1_w_gate", "ffn1_w_up", "ffn1_w_down"))
W_IN_SPLIT = Q_LORA + KV_LORA + ROPE_DIM


_FFN1 = ("ffn1_w_gate", "ffn1_w_up", "ffn1_w_down")
_TRANSPOSED = ("ffn1_w_gate", "ffn1_w_up", "ffn2_w_gate", "ffn2_w_up", "w_in", "w_q_up")


def _local_view(name, a):
    return jnp.swapaxes(a, 1, 2)[0] if name in _TRANSPOSED else a[0]


def _global_view(name, a):
    return jnp.swapaxes(a[None], 1, 2) if name in _TRANSPOSED else a[None]


def _pad_shard(name, a):
    if name == "w_in":
        return jnp.concatenate([a[:W_IN_SPLIT], jnp.zeros((64, a.shape[1]), a.dtype), a[W_IN_SPLIT:]], axis=0)
    if name == "w_q_up":
        return jnp.pad(a, ((0, 64), (0, 0)))
    return a


def _unpad_shard(name, a):
    if name == "w_in":
        return jnp.concatenate([a[:, :W_IN_SPLIT], a[:, W_IN_SPLIT + 64:]], axis=1)
    if name == "w_q_up":
        return a[:, :192]
    return a


def _stacked(g):
    return g if g.ndim == 3 else g.reshape(N_CHIPS, g.shape[0] // N_CHIPS, g.shape[1])


class _Plan:
    AG_UNITS = (
        (("w_in", "w_q_up", "w_kv_up"), "ffn1_up"),
        (("w_out",), "w_in"),
        (("w_mq",), "qkv_prep"),
        (("w_mkv", "w_mo", "ffn2_w_gate"), "mla_fwd"),
        (("ffn2_w_up",), "xattn_fwd"),
        (("ffn2_w_down",), "ffn2_up"),
    )
    RS_UNITS = (
        (("ffn2_w_gate", "ffn2_w_up", "ffn2_w_down"), "ffn2_dn_a", "mla_bwd", "qkv_prep_bwd"),
        (("w_mo", "w_mq", "w_mkv"), "w_out_dx", "mla_bwd", "qkv_prep_bwd"),
        (("w_out", "w_q_up", "w_kv_up", "w_in"), "w_in_dx", "ffn1_dact", "ffn1_dwd"),
        (("ffn1_w_down",), "ffn1_dwg", "ffn1_dwu", "ffn1_dn_a"),
        (("ffn1_w_gate",), "ffn1_dwu", "ffn1_dn_a", "ffn1_dn_b"),
        (("ffn1_w_up",), "ffn1_dn_a", "ffn1_dn_b", "adamw_w_kv_up"),
    )
    ADAMW_ORDER = ("w_kv_up", "ffn2_w_gate", "ffn2_w_up", "ffn2_w_down", "w_mo", "w_mq", "w_mkv", "w_out", "w_q_up",
                   "w_in", "ffn1_w_down", "ffn1_w_gate", "ffn1_w_up")

    def __init__(self, shards, w, grads, core):
        self.shards, self.w, self.grads, self.core = shards, w, grads, core
        self.last_slab_step = 0
        self.parts = {}
        self.ag = [None for _ in self.AG_UNITS]
        self.rs = [[None, None, None, None] for _ in self.RS_UNITS]

    def pre(self, name):
        for i, (names, host) in enumerate(self.AG_UNITS):
            if name == host:
                st = _ag_ici_stage([self.shards[n] for n in names])
                st.then = _ag_d2d_stage(st.outs)
                st.start_step = self.last_slab_step if name == "ffn1_up" else 0
                self.ag[i] = _host(name, st)
        for i, (names, h1, h2, h3) in enumerate(self.RS_UNITS):
            if name == h1:
                self.rs[i][0] = _host(name, _rs_swap_stage([_stacked(self.grads[n]) for n in names]))
            if name == h2:
                self.rs[i][2] = _host(name, _rs_scatter_stage(self.rs[i][1], relative=names[0] in _FFN1))
            if name == h3:
                self.rs[i][3] = _host(name, _rs_mirror_stage(self.rs[i][2].results))

    def post(self, name):
        for i, (names, host) in enumerate(self.AG_UNITS):
            if name == host:
                for n, f in zip(names, self.ag[i].results):
                    self.w[n] = _full_weight(n, f)
        for i, (names, h1, h2, h3) in enumerate(self.RS_UNITS):
            if name == h1:
                self.rs[i][1] = [_pair_add("pair_add_" + n, _stacked(self.grads[n]), r1, self.core)
                                 for n, r1 in zip(names, self.rs[i][0].results)]
            if name == h3:
                for n, p in zip(names, self.rs[i][3].results):
                    self.parts[n] = p


def _full_weight(name, stacked):
    if name in ("w_out", "w_mq", "w_mo"):
        return stacked.reshape(D_MODEL, D_MODEL)
    return stacked


def kernel(x, mem, positions, ffn1_norm, ffn1_w_gate, ffn1_w_up, ffn1_w_down, mix_norm, w_in, q_norm, w_q_up, kv_norm, w_kv_up, pool_w, pool_scale, w_out, xattn_norm, mem_norm, w_mq, w_mkv, w_mo, ffn2_norm, ffn2_w_gate, ffn2_w_up, ffn2_w_down, final_norm, loss_target, m_ffn1_norm, m_ffn1_w_gate, m_ffn1_w_up, m_ffn1_w_down, m_mix_norm, m_w_in, m_q_norm, m_w_q_up, m_kv_norm, m_w_kv_up, m_pool_w, m_pool_scale, m_w_out, m_xattn_norm, m_mem_norm, m_w_mq, m_w_mkv, m_w_mo, m_ffn2_norm, m_ffn2_w_gate, m_ffn2_w_up, m_ffn2_w_down, m_final_norm, v_ffn1_norm, v_ffn1_w_gate, v_ffn1_w_up, v_ffn1_w_down, v_mix_norm, v_w_in, v_q_norm, v_w_q_up, v_kv_norm, v_w_kv_up, v_pool_w, v_pool_scale, v_w_out, v_xattn_norm, v_mem_norm, v_w_mq, v_w_mkv, v_w_mo, v_ffn2_norm, v_ffn2_w_gate, v_ffn2_w_up, v_ffn2_w_down, v_final_norm):
    wts = dict(zip(_WEIGHTS, (ffn1_norm, ffn1_w_gate, ffn1_w_up, ffn1_w_down, mix_norm, w_in, q_norm, w_q_up, kv_norm, w_kv_up, pool_w, pool_scale, w_out, xattn_norm, mem_norm, w_mq, w_mkv, w_mo, ffn2_norm, ffn2_w_gate, ffn2_w_up, ffn2_w_down, final_norm)))
    mom = dict(zip(_WEIGHTS, (m_ffn1_norm, m_ffn1_w_gate, m_ffn1_w_up, m_ffn1_w_down, m_mix_norm, m_w_in, m_q_norm, m_w_q_up, m_kv_norm, m_w_kv_up, m_pool_w, m_pool_scale, m_w_out, m_xattn_norm, m_mem_norm, m_w_mq, m_w_mkv, m_w_mo, m_ffn2_norm, m_ffn2_w_gate, m_ffn2_w_up, m_ffn2_w_down, m_final_norm)))
    var = dict(zip(_WEIGHTS, (v_ffn1_norm, v_ffn1_w_gate, v_ffn1_w_up, v_ffn1_w_down, v_mix_norm, v_w_in, v_q_norm, v_w_q_up, v_kv_norm, v_w_kv_up, v_pool_w, v_pool_scale, v_w_out, v_xattn_norm, v_mem_norm, v_w_mq, v_w_mkv, v_w_mo, v_ffn2_norm, v_ffn2_w_gate, v_ffn2_w_up, v_ffn2_w_down, v_final_norm)))
    small = [n for n in _WEIGHTS if n not in _SHARDED]

    global _PLAN
    shards = {n: _pad_shard(n, _local_view(n, wts[n])).astype(BF16) for n in _SHARDED}
    w = {n: wts[n].reshape(1, -1) for n in _SMALL_VECTORS}
    w["pool_w"] = pool_w[0].astype(BF16)
    grads = {}
    core = lax.axis_index("c").astype(jnp.int32).reshape(1)
    plan = _Plan(shards, w, grads, core)
    _PLAN = plan
    try:
        w["ffn1_shards"] = tuple(shards[n] for n in _FFN1)

        loss_local, dx = _local_step(x[0], mem[0], positions[0], loss_target[0], w, grads)

        gpack, dpack, mpack, vpack = _small_allreduce_adamw(
            _pack_small({n: grads[n] for n in small}, loss_local), _pack_small({n: wts[n] for n in small}),
            _pack_small({n: mom[n] for n in small}), _pack_small({n: var[n] for n in small}))
        like = {n: wts[n] for n in small}
        g_out, d_out, m_out, v_out = (_unpack_small(p, like) for p in (gpack, dpack, mpack, vpack))
        loss = gpack[_LOSS_ROW, 0]

        for n in _Plan.ADAMW_ORDER:
            res = _adamw_sum("adamw_" + n, _local_view(n, wts[n]), _unpad_shard(n, plan.parts[n]),
                             _local_view(n, mom[n]), _local_view(n, var[n]))
            g_out[n], d_out[n], m_out[n], v_out[n] = (_global_view(n, r) for r in res)
    finally:
        _PLAN = None
        _PENDING.clear()

    return (loss, dx[None], *[g_out[n] for n in _WEIGHTS], *[d_out[n] for n in _WEIGHTS],
            *[m_out[n] for n in _WEIGHTS], *[v_out[n] for n in _WEIGHTS])
```

```python
import functools

import jax
import jax.numpy as jnp
from jax import lax
from jax.experimental import pallas as pl
from jax.experimental.pallas import tpu as pltpu

F32 = jnp.float32
BF16 = jnp.bfloat16

D_MODEL = 1024
D_FF = 2816
N_CHIPS = 4
FF_SHARD = D_FF // N_CHIPS
MLA_HEADS = 4
Q_LORA = 256
KV_LORA = 128
ROPE_DIM = 64
HEAD_QK = 256
HEAD_V = 128
POOL_GROUPS = 4
POOL_CH = 128
MEM_HEADS = 4
MEM_HEAD_DIM = 256
RMS_EPS = 1e-6
ROPE_BASE = 10000.0
MLA_SCALE = (128 + 64) ** -0.5
MEM_SCALE = MEM_HEAD_DIM ** -0.5

ADAM_LR = 0.001
ADAM_B1 = 0.9
ADAM_B2 = 0.999
ADAM_EPS = 1e-08
ADAM_WD = 0.01
ADAM_STEP = 10

V7X_VMEM_LIMIT_BYTES = 56 * 1024 * 1024

NN = ((1,), (0,))
NT = ((1,), (1,))
TN = ((0,), (0,))


def _params(*sem):
    return pltpu.CompilerParams(dimension_semantics=sem, vmem_limit_bytes=V7X_VMEM_LIMIT_BYTES)


_MESH = pl.DeviceIdType.MESH
_ANY = pl.BlockSpec(memory_space=pl.ANY)


class _Stage:
    def __init__(self, ins, outs, n_remote, n_local, copies, aliases=None):
        self.ins, self.outs, self.n_remote, self.n_local = list(ins), list(outs), n_remote, n_local
        self.copies, self.aliases = copies, dict(aliases or {})
        self.results = None
        self.start_step = 0
        self.then = None

    def descriptors(self, in_refs, out_refs, send, recv, loc):
        ds, ri, li = [], 0, 0
        for src, dst, dev in self.copies(in_refs, out_refs):
            if dev is None:
                ds.append(pltpu.make_async_copy(src, dst, loc.at[li]))
                li += 1
            else:
                ds.append(pltpu.make_async_remote_copy(src_ref=src, dst_ref=dst, send_sem=send.at[ri],
                                                       recv_sem=recv.at[ri], device_id=dev, device_id_type=_MESH))
                ri += 1
        assert ri == self.n_remote and li == self.n_local
        return ds


_PENDING = {}


def _host(name, stage):
    _PENDING.setdefault(name, []).append(stage)
    return stage


_PLAN = None


def _call(body, **kw):
    if _PLAN is not None:
        _PLAN.pre(kw["name"])
    res = _call_hosting(body, **kw)
    if _PLAN is not None:
        _PLAN.post(kw["name"])
    return res


def _call_hosting(body, *, name, grid, in_specs, out_specs, out_shape, sem, args, scratch_shapes=(), aliases=None):
    stages = _PENDING.pop(name, [])
    scratch_shapes = list(scratch_shapes)
    if not stages:
        return pl.pallas_call(body, name=name, grid=grid, in_specs=in_specs, out_specs=out_specs,
                              out_shape=out_shape, scratch_shapes=scratch_shapes,
                              input_output_aliases=dict(aliases or {}), compiler_params=_params(*sem))(*args)
    ni, no, ns = len(in_specs), len(out_shape), len(scratch_shapes)
    c_ins = [a for st in stages for a in st.ins]
    c_outs = [o for st in stages for o in st.outs]
    nci, nco = len(c_ins), len(c_outs)
    aliases, io, oo = dict(aliases or {}), 0, 0
    for st in stages:
        for i, j in st.aliases.items():
            aliases[ni + io + i] = no + oo + j
        io += len(st.ins)
        oo += len(st.outs)
    dma = pltpu.SemaphoreType.DMA
    sems = []
    for st in stages:
        sems += [dma((max(st.n_remote, 1),)), dma((max(st.n_remote, 1),)), dma((max(st.n_local, 1),))]
    followers = [st.then for st in stages if st.then is not None]
    for st in followers:
        sems += [dma((max(st.n_remote, 1),)), dma((max(st.n_remote, 1),)), dma((max(st.n_local, 1),))]

    def wrapped(*refs):
        ins, cin = refs[:ni], refs[ni:ni + nci]
        outs, cout = refs[ni + nci:ni + nci + no], refs[ni + nci + no:ni + nci + no + nco]
        scr = refs[ni + nci + no + nco:ni + nci + no + nco + ns]
        sem_refs = refs[ni + nci + no + nco + ns:]
        step = pl.program_id(0)
        last = pl.program_id(0) == grid[0] - 1
        for ax in range(1, len(grid)):
            step = step * grid[ax] + pl.program_id(ax)
            last = jnp.logical_and(last, pl.program_id(ax) == grid[ax] - 1)

        def descriptors(si):
            io = sum(len(st.ins) for st in stages[:si])
            oo = sum(len(st.outs) for st in stages[:si])
            st = stages[si]
            return st.descriptors(cin[io:io + len(st.ins)], cout[oo:oo + len(st.outs)], *sem_refs[3 * si:3 * si + 3])

        def follower_descriptors(fi):
            si = [k for k, st in enumerate(stages) if st.then is not None][fi]
            oo = sum(len(st.outs) for st in stages[:si])
            bufs = cout[oo:oo + len(stages[si].outs)]
            k0 = 3 * (len(stages) + fi)
            return followers[fi].descriptors(bufs, bufs, *sem_refs[k0:k0 + 3])

        def start(si):
            @pl.when(step == stages[si].start_step)
            def _():
                for d in descriptors(si):
                    d.start()

        for si, st in enumerate(stages):
            if st.start_step == 0:
                start(si)
        body(*ins, *outs, *scr)
        for si, st in enumerate(stages):
            if st.start_step != 0:
                start(si)

        @pl.when(last)
        def _():
            for si in range(len(stages)):
                for d in descriptors(si):
                    d.wait()
            for fi in range(len(followers)):
                for d in follower_descriptors(fi):
                    d.start()
            for fi in range(len(followers)):
                for d in follower_descriptors(fi):
                    d.wait()

    res = pl.pallas_call(
        wrapped, name=name, grid=grid, in_specs=list(in_specs) + [_ANY] * nci,
        out_specs=list(out_specs) + [_ANY] * nco, out_shape=list(out_shape) + c_outs,
        scratch_shapes=scratch_shapes + sems, input_output_aliases=aliases,
        compiler_params=_params(*(("arbitrary",) * len(grid))))(*args, *c_ins)
    oo = no
    for st in stages:
        st.results = list(res[oo:oo + len(st.outs)])
        oo += len(st.outs)
    return list(res[:no])


def _dot(a, b, dims):
    return lax.dot_general(a.astype(BF16), b.astype(BF16), (dims, ((), ())), preferred_element_type=F32)


_MAX_ROW_BLOCK = 1024
_ATT_BLOCK = 512


_MAX_REDUCE_BLOCK = 2048


def _row_block(s, want=1024):
    return min(want, s, _MAX_ROW_BLOCK)


def _reduce_block(s):
    return min(s, _MAX_REDUCE_BLOCK)


def _matmul(name, grid, terms, extras, outs, epilogue, acc_shape, fill=(), summed=()):
    nt, ne, no, nf = len(terms), len(extras), len(outs), len(fill)
    nk = grid[-1]
    dims = [t[4] for t in terms]

    def body(*refs):
        a_refs, b_refs = refs[:nt], refs[nt:2 * nt]
        e_refs = refs[2 * nt:2 * nt + ne]
        o_refs = refs[2 * nt + ne + nf:2 * nt + ne + nf + no]

        def finish(acc):
            vals = epilogue(acc, *[e[...] for e in e_refs])
            for idx, (o, val) in enumerate(zip(o_refs, vals)):
                if idx in summed:
                    @pl.when(pl.program_id(0) == 0)
                    def _(o=o, val=val):
                        o[...] = val.astype(o.dtype)

                    @pl.when(pl.program_id(0) > 0)
                    def _(o=o, val=val):
                        o[...] += val.astype(o.dtype)
                else:
                    o[...] = val.astype(o.dtype)

        if nk == 1:
            part = None
            for a, b, d in zip(a_refs, b_refs, dims):
                t = _dot(a[...], b[...], d)
                part = t if part is None else part + t
            finish(part)
        else:
            acc_ref = refs[-1]
            k = pl.program_id(len(grid) - 1)

            @pl.when(k == 0)
            def _():
                acc_ref[...] = jnp.zeros_like(acc_ref)

            for a, b, d in zip(a_refs, b_refs, dims):
                acc_ref[...] += _dot(a[...], b[...], d)

            @pl.when(k == nk - 1)
            def _():
                finish(acc_ref[...])

    in_specs = [t[1] for t in terms] + [t[3] for t in terms] + [e[1] for e in extras] + [_ANY] * nf
    args = [t[0] for t in terms] + [t[2] for t in terms] + [e[0] for e in extras] + list(fill)
    sem = ("arbitrary" if summed else "parallel",) * (len(grid) - 1) + ("arbitrary",)
    aliases = {2 * nt + ne + i: i for i in range(nf)}
    return _call(
        body, name=name, grid=grid, in_specs=in_specs,
        out_specs=[o[1] for o in outs], out_shape=[o[0] for o in outs],
        scratch_shapes=[pltpu.VMEM(acc_shape, F32)] if nk > 1 else [], sem=sem, args=args, aliases=aliases)


def _ident(acc):
    return (acc,)


def _rmsnorm_fwd(name, x, gain, width, col_block=0):
    s = x.shape[0]
    bm = _row_block(s)

    def body(x_ref, g_ref, o_ref):
        xf = x_ref[...]
        r = lax.rsqrt(jnp.mean(xf * xf, axis=-1, keepdims=True) + RMS_EPS)
        o_ref[...] = ((xf * r) * g_ref[...]).astype(o_ref.dtype)

    return pl.pallas_call(
        body, name=name, grid=(s // bm,),
        in_specs=[pl.BlockSpec((bm, width), lambda i: (i, col_block)), pl.BlockSpec((1, width), lambda i: (0, 0))],
        out_specs=pl.BlockSpec((bm, width), lambda i: (i, 0)),
        out_shape=jax.ShapeDtypeStruct((s, width), BF16),
        compiler_params=_params("parallel"),
    )(x, gain)


def _rms_bwd_math(dy, xf, g, width):
    r = lax.rsqrt(jnp.mean(xf * xf, axis=-1, keepdims=True) + RMS_EPS)
    dyg = dy * g
    dot = jnp.sum(dyg * xf, axis=-1, keepdims=True)
    dx = r * dyg - xf * ((r * r * r) * (dot * (1.0 / width)))
    dgain = jnp.sum(dy * (xf * r), axis=0, keepdims=True)
    return dx, dgain


def _rmsnorm_bwd(name, dy, x, gain, width, col_block=0, dres=None, out_dtype=F32):
    s = x.shape[0]
    bm = _row_block(s)
    has_res = dres is not None

    def body(*refs):
        if has_res:
            dy_ref, x_ref, g_ref, r_ref, dx_ref, dg_ref, dxb_ref = refs
        else:
            dy_ref, x_ref, g_ref, dx_ref, dg_ref = refs
        dx, dgain = _rms_bwd_math(dy_ref[...].astype(F32), x_ref[...], g_ref[...], width)
        if has_res:
            dx = dx + r_ref[...]
            dxb_ref[...] = dx.astype(BF16)
        dx_ref[...] = dx.astype(dx_ref.dtype)

        @pl.when(pl.program_id(0) == 0)
        def _():
            dg_ref[...] = dgain

        @pl.when(pl.program_id(0) > 0)
        def _():
            dg_ref[...] += dgain

    row = pl.BlockSpec((bm, width), lambda i: (i, 0))
    in_specs = [row, pl.BlockSpec((bm, width), lambda i: (i, col_block)), pl.BlockSpec((1, width), lambda i: (0, 0))]
    args = [dy, x, gain]
    out_specs = [row, pl.BlockSpec((1, width), lambda i: (0, 0))]
    out_shape = [jax.ShapeDtypeStruct((s, width), out_dtype), jax.ShapeDtypeStruct((1, width), F32)]
    if has_res:
        in_specs.append(row)
        args.append(dres)
        out_specs.append(row)
        out_shape.append(jax.ShapeDtypeStruct((s, width), BF16))
    return _call(body, name=name, grid=(s // bm,), in_specs=in_specs, out_specs=out_specs, out_shape=out_shape,
                 sem=("arbitrary",), args=args)


def _loss_and_final_norm(h, gain, target):
    s, d = h.shape
    bm = _row_block(s, 512)

    def body(h_ref, g_ref, t_ref, dh_ref, dhb_ref, loss_ref, dg_ref):
        xf = h_ref[...]
        g = g_ref[...]
        r = lax.rsqrt(jnp.mean(xf * xf, axis=-1, keepdims=True) + RMS_EPS)
        err = (xf * r) * g - t_ref[...]
        part = 0.5 * jnp.sum(jnp.mean(err * err, axis=-1, keepdims=True), axis=0, keepdims=True)
        dx, dgain = _rms_bwd_math(err * (1.0 / d), xf, g, d)
        dh_ref[...] = dx
        dhb_ref[...] = dx.astype(BF16)

        @pl.when(pl.program_id(0) == 0)
        def _():
            dg_ref[...] = dgain
            loss_ref[...] = jnp.broadcast_to(part, loss_ref.shape)

        @pl.when(pl.program_id(0) > 0)
        def _():
            dg_ref[...] += dgain
            loss_ref[...] += jnp.broadcast_to(part, loss_ref.shape)

    row = pl.BlockSpec((bm, d), lambda i: (i, 0))
    vec = pl.BlockSpec((1, d), lambda i: (0, 0))
    return pl.pallas_call(
        body, name="loss_final_norm", grid=(s // bm,), in_specs=[row, vec, row],
        out_specs=[row, row, pl.BlockSpec((1, 128), lambda i: (0, 0)), vec],
        out_shape=[jax.ShapeDtypeStruct((s, d), F32), jax.ShapeDtypeStruct((s, d), BF16),
                   jax.ShapeDtypeStruct((1, 128), F32),
                   jax.ShapeDtypeStruct((1, d), F32)],
        compiler_params=_params("arbitrary"),
    )(h, gain, target)


def _ffn_up(name, n, wg, wu):
    s = n.shape[0]
    bm = _row_block(s)

    def body(n_ref, wg_ref, wu_ref, a_ref, dadu_ref, dadg_ref):
        x = n_ref[...]
        g = _dot(x, wg_ref[...], NT)
        u = _dot(x, wu_ref[...], NT)
        sg = jax.nn.sigmoid(g)
        silu = g * sg
        a_ref[...] = (silu * u).astype(BF16)
        dadu_ref[...] = silu.astype(BF16)
        dadg_ref[...] = (u * (sg * (1.0 + g * (1.0 - sg)))).astype(BF16)

    w_spec = pl.BlockSpec((None, FF_SHARD, D_MODEL), lambda j, i: (j, 0, 0))
    o_spec = pl.BlockSpec((None, bm, FF_SHARD), lambda j, i: (j, i, 0))
    shp = jax.ShapeDtypeStruct((N_CHIPS, s, FF_SHARD), BF16)
    return _call(
        body, name=name, grid=(N_CHIPS, s // bm),
        in_specs=[pl.BlockSpec((bm, D_MODEL), lambda j, i: (i, 0)), w_spec, w_spec],
        out_specs=[o_spec, o_spec, o_spec], out_shape=[shp, shp, shp],
        sem=("parallel", "parallel"), args=[n, wg, wu])


def _ffn1_up_gather_direct(n, g_sh, u_sh, d_sh):
    s = n.shape[0]
    bm = _row_block(s)
    nrb = s // bm
    rows, cols = g_sh.shape
    rels = (1, 2, 3)

    def body(n_ref, gs, us, ds, a_ref, dadu_ref, dadg_ref, wg, wu, wd, gbuf, ubuf, send, recv, fsend, frecv, loc, ld):
        r, i = pl.program_id(0), pl.program_id(1)
        x, y, c = lax.axis_index("x"), lax.axis_index("y"), lax.axis_index("c")
        sib = (x, y, 1 - c)
        mine, _ = _half_rows(c, rows)
        shards, fulls, bufs = (gs, us, ds), (wg, wu, wd), (gbuf, ubuf)

        def ici(k, rel, dev=sib):
            return pltpu.make_async_remote_copy(
                src_ref=shards[k].at[mine], dst_ref=fulls[k].at[rel, mine], send_sem=send.at[k, rel - 1],
                recv_sem=recv.at[k, rel - 1], device_id=dev, device_id_type=_MESH)

        def peer(rel):
            return ((1 - x) if rel & 2 else x, (1 - y) if rel & 1 else y, c)

        def fwd(k, rel):
            return pltpu.make_async_remote_copy(
                src_ref=fulls[k].at[rel, mine], dst_ref=fulls[k].at[rel, mine], send_sem=fsend.at[k, rel - 1],
                recv_sem=frecv.at[k, rel - 1], device_id=sib, device_id_type=_MESH)

        def own(k):
            return pltpu.make_async_copy(shards[k], fulls[k].at[0], loc.at[k])

        def load(k, src):
            return pltpu.make_async_copy(src, bufs[k], ld.at[k])

        @pl.when(jnp.logical_and(r == 0, i == 0))
        def _():
            for k in range(3):
                own(k).start()
            for rel in (1, 2):
                for k in (0, 1):
                    ici(k, rel, peer(rel)).start()
            for k in (0, 1):
                load(k, shards[k]).start()
            for k in (0, 1):
                load(k, shards[k]).wait()

        @pl.when(jnp.logical_and(r > 0, i == 0))
        def _():
            for k in (0, 1):
                ici(k, r).wait_recv()
                fwd(k, r).start()
            for k in (0, 1):
                fwd(k, r).wait_recv()
                load(k, fulls[k].at[r]).start()
            for k in (0, 1):
                load(k, fulls[k].at[r]).wait()

        @pl.when(jnp.logical_and(r == 1, i == 0))
        def _():
            for k in (0, 1):
                ici(k, 3, peer(3)).start()

        @pl.when(jnp.logical_and(r == 2, i == 0))
        def _():
            for rel in (1, 2):
                ici(2, rel, peer(rel)).start()

        @pl.when(jnp.logical_and(r == 3, i == 0))
        def _():
            ici(2, 3, peer(3)).start()

        xv = n_ref[...]
        g = _dot(xv, gbuf[...], NT)
        u = _dot(xv, ubuf[...], NT)
        sg = jax.nn.sigmoid(g)
        silu = g * sg
        a_ref[...] = (silu * u).astype(BF16)
        dadu_ref[...] = silu.astype(BF16)
        dadg_ref[...] = (u * (sg * (1.0 + g * (1.0 - sg)))).astype(BF16)

        @pl.when(jnp.logical_and(r == 3, i == nrb - 1))
        def _():
            for rel in rels:
                ici(2, rel).wait_recv()
                fwd(2, rel).start()
            for rel in rels:
                fwd(2, rel).wait_recv()
            for k in range(3):
                for rel in rels:
                    ici(k, rel).wait_send()
                    fwd(k, rel).wait_send()
                own(k).wait()

    o_spec = pl.BlockSpec((None, bm, FF_SHARD), lambda r, i: (r, i, 0))
    act = jax.ShapeDtypeStruct((N_CHIPS, s, FF_SHARD), BF16)
    full = jax.ShapeDtypeStruct((N_CHIPS, rows, cols), BF16)
    dma = pltpu.SemaphoreType.DMA
    if _PLAN is not None:
        _PLAN.last_slab_step = 3 * nrb
    return _call(
        body, name="ffn1_up", grid=(N_CHIPS, nrb),
        in_specs=[pl.BlockSpec((bm, D_MODEL), lambda r, i: (i, 0)), _ANY, _ANY, _ANY],
        out_specs=[o_spec, o_spec, o_spec, _ANY, _ANY, _ANY], out_shape=[act, act, act, full, full, full],
        scratch_shapes=[pltpu.VMEM((rows, cols), BF16), pltpu.VMEM((rows, cols), BF16), dma((3, 3)), dma((3, 3)),
                        dma((3, 3)), dma((3, 3)), dma((3,)), dma((2,))],
        sem=("arbitrary", "arbitrary"), args=[n, g_sh, u_sh, d_sh])


def _ffn1_up_gather(xin, gain, g_sh, u_sh, d_sh):
    s = xin.shape[0]
    bm = _row_block(s)
    nrb = s // bm
    rows, cols = g_sh.shape

    def body(x_ref, gain_ref, gs, us, ds, n_ref, a_ref, dadu_ref, dadg_ref, wg, wu, wd, gbuf, ubuf,
             send, recv, qsend, qrecv, fsend, frecv, loc, ld):
        r, i = pl.program_id(0), pl.program_id(1)
        x, y, c = lax.axis_index("x"), lax.axis_index("y"), lax.axis_index("c")
        sib = (x, y, 1 - c)
        mine, _ = _half_rows(c, rows)
        quarters = _quarter_rows(c, rows)
        shards, fulls, bufs = (gs, us, ds), (wg, wu, wd), (gbuf, ubuf)

        def remote(src, dst, ssem, rsem, dev):
            return pltpu.make_async_remote_copy(src_ref=src, dst_ref=dst, send_sem=ssem, recv_sem=rsem,
                                                device_id=dev, device_id_type=_MESH)

        def peer(rel):
            return ((1 - x) if rel & 2 else x, (1 - y) if rel & 1 else y, c)

        def ici(k, rel, dev=sib):
            return remote(shards[k].at[mine], fulls[k].at[rel, mine], send.at[k, rel - 1], recv.at[k, rel - 1], dev)

        def quarter(k, which, dev=sib):
            slab, q = ((2, quarters[0]), (1, quarters[1]))[which]
            return remote(fulls[k].at[slab, q], fulls[k].at[3, q], qsend.at[k, which], qrecv.at[k, which], dev)

        def fwd(k, rel):
            return remote(fulls[k].at[rel, mine], fulls[k].at[rel, mine], fsend.at[k, rel - 1], frecv.at[k, rel - 1], sib)

        def own(k):
            return pltpu.make_async_copy(shards[k], fulls[k].at[0], loc.at[k])

        def load(slab):
            for k in (0, 1):
                pltpu.make_async_copy(shards[k] if slab == 0 else fulls[k].at[slab], bufs[k], ld.at[k]).start()
            for k in (0, 1):
                pltpu.make_async_copy(shards[k] if slab == 0 else fulls[k].at[slab], bufs[k], ld.at[k]).wait()

        def from_neighbour(ks, rel):
            for k in ks:
                ici(k, rel).wait_recv()
                fwd(k, rel).start()
                quarter(k, 0 if rel == 2 else 1, peer(1 if rel == 2 else 2)).start()
            for k in ks:
                fwd(k, rel).wait_recv()

        def from_diagonal(ks):
            for k in ks:
                quarter(k, 0).wait_recv()
                quarter(k, 1).wait_recv()
                fwd(k, 3).start()
            for k in ks:
                fwd(k, 3).wait_recv()

        @pl.when(jnp.logical_and(r == 0, i == 0))
        def _():
            for k in range(3):
                own(k).start()
            for rel in (1, 2):
                for k in (0, 1):
                    ici(k, rel, peer(rel)).start()
            load(0)

        @pl.when(jnp.logical_and(r == 1, i == 0))
        def _():
            from_neighbour((0, 1), 1)
            load(1)
            for rel in (1, 2):
                ici(2, rel, peer(rel)).start()

        @pl.when(jnp.logical_and(r == 2, i == 0))
        def _():
            from_neighbour((0, 1), 2)
            load(2)

        @pl.when(jnp.logical_and(r == 3, i == 0))
        def _():
            from_diagonal((0, 1))
            load(3)

        xv = _norm_bf16(x_ref[...], gain_ref[...])

        @pl.when(r == 0)
        def _():
            n_ref[...] = xv

        g = _dot(xv, gbuf[...], NT)
        u = _dot(xv, ubuf[...], NT)
        sg = jax.nn.sigmoid(g)
        silu = g * sg
        a_ref[...] = (silu * u).astype(BF16)
        dadu_ref[...] = silu.astype(BF16)
        dadg_ref[...] = (u * (sg * (1.0 + g * (1.0 - sg)))).astype(BF16)

        @pl.when(jnp.logical_and(r == 3, i == nrb - 1))
        def _():
            from_neighbour((2,), 1)
            from_neighbour((2,), 2)
            from_diagonal((2,))
            for k in range(3):
                for rel in (1, 2):
                    ici(k, rel).wait_send()
                for which in (0, 1):
                    quarter(k, which).wait_send()
                for rel in (1, 2, 3):
                    fwd(k, rel).wait_send()
                own(k).wait()

    o_spec = pl.BlockSpec((None, bm, FF_SHARD), lambda r, i: (r, i, 0))
    act = jax.ShapeDtypeStruct((N_CHIPS, s, FF_SHARD), BF16)
    full = jax.ShapeDtypeStruct((N_CHIPS, rows, cols), BF16)
    dma = pltpu.SemaphoreType.DMA
    if _PLAN is not None:
        _PLAN.last_slab_step = 3 * nrb
    n_spec = pl.BlockSpec((bm, D_MODEL), lambda r, i: (jnp.where(r == 0, i, nrb - 1), 0))
    return _call(
        body, name="ffn1_up", grid=(N_CHIPS, nrb),
        in_specs=[pl.BlockSpec((bm, D_MODEL), lambda r, i: (i, 0)), pl.BlockSpec((1, D_MODEL), lambda r, i: (0, 0)),
                  _ANY, _ANY, _ANY],
        out_specs=[n_spec, o_spec, o_spec, o_spec, _ANY, _ANY, _ANY],
        out_shape=[jax.ShapeDtypeStruct((s, D_MODEL), BF16), act, act, act, full, full, full],
        scratch_shapes=[pltpu.VMEM((rows, cols), BF16), pltpu.VMEM((rows, cols), BF16), dma((3, 2)), dma((3, 2)),
                        dma((3, 2)), dma((3, 2)), dma((3, 3)), dma((3, 3)), dma((3,)), dma((2,))],
        sem=("arbitrary", "arbitrary"), args=[xin, gain, g_sh, u_sh, d_sh])


def _residual_epilogue(alpha, with_norm):
    if not with_norm:
        return lambda acc, r: (r + alpha * acc,)

    def epilogue(acc, r, g):
        h = r + alpha * acc
        rs = lax.rsqrt(jnp.mean(h * h, axis=-1, keepdims=True) + RMS_EPS)
        return h, (h * rs) * g

    return epilogue


def _residual_outs(s, bm, gain):
    row = pl.BlockSpec((bm, D_MODEL), lambda i, k: (i, 0))
    outs = [(jax.ShapeDtypeStruct((s, D_MODEL), F32), row)]
    if gain is None:
        return [], outs
    return [(gain, pl.BlockSpec((1, D_MODEL), lambda i, k: (0, 0)))], outs + [(jax.ShapeDtypeStruct((s, D_MODEL), BF16), row)]


def _loss_epilogue(acc, res, g, target):
    d = acc.shape[-1]
    h = res + 0.5 * acc
    r = lax.rsqrt(jnp.mean(h * h, axis=-1, keepdims=True) + RMS_EPS)
    err = (h * r) * g - target
    part = 0.5 * jnp.sum(jnp.mean(err * err, axis=-1, keepdims=True), axis=0, keepdims=True)
    dx, dgain = _rms_bwd_math(err * (1.0 / d), h, g, d)
    return dx, dx, jnp.broadcast_to(part, (1, 128)), dgain


def _ffn_down(name, a, wd, res, gain=None, loss=None):
    s = a.shape[1]
    bm = _row_block(s, 512)
    row = pl.BlockSpec((bm, D_MODEL), lambda i, k: (i, 0))
    terms = [(a, pl.BlockSpec((None, bm, FF_SHARD), lambda i, k, j=j: (j, i, 0)),
              wd, pl.BlockSpec((None, FF_SHARD, D_MODEL), lambda i, k, j=j: (j, 0, 0)), NN) for j in range(N_CHIPS)]
    if loss is not None:
        vec = pl.BlockSpec((1, D_MODEL), lambda i, k: (0, 0))
        outs = [(jax.ShapeDtypeStruct((s, D_MODEL), F32), row), (jax.ShapeDtypeStruct((s, D_MODEL), BF16), row),
                (jax.ShapeDtypeStruct((1, 128), F32), pl.BlockSpec((1, 128), lambda i, k: (0, 0))),
                (jax.ShapeDtypeStruct((1, D_MODEL), F32), vec)]
        return _matmul(name, (s // bm, 1), terms, [(res, row), (loss[0], vec), (loss[1], row)], outs,
                       _loss_epilogue, None, summed=(2, 3))
    extras, outs = _residual_outs(s, bm, gain)
    res_out = _matmul(name, (s // bm, 1), terms, [(res, row)] + extras, outs,
                      _residual_epilogue(0.5, gain is not None), None)
    return res_out if gain is not None else res_out[0]


def _norm_bwd_epilogue(width):
    def epilogue(acc, h, g, dres):
        dx, dgain = _rms_bwd_math(acc, h, g, width)
        dx = dx + dres
        return dx, dx, dgain

    return epilogue


def _norm_bwd_operands(s, bm, h, gain, dres):
    row = pl.BlockSpec((bm, D_MODEL), lambda i, k: (i, 0))
    vec = pl.BlockSpec((1, D_MODEL), lambda i, k: (0, 0))
    extras = [(h, row), (gain, vec), (dres, row)]
    outs = [(jax.ShapeDtypeStruct((s, D_MODEL), F32), row), (jax.ShapeDtypeStruct((s, D_MODEL), BF16), row),
            (jax.ShapeDtypeStruct((1, D_MODEL), F32), vec)]
    return extras, outs, (2,)


def _ffn_bwd(tag, dh, n, dadg, dadu, a, wg, wu, wd, grads, norm_bwd=None):
    s = dh.shape[0]
    bm = _row_block(s)
    bk = _reduce_block(s)
    nk = s // bk

    def act_bwd(acc, dg_da, du_da):
        da = 0.5 * acc
        return da * dg_da.astype(F32), da * du_da.astype(F32)

    slab = pl.BlockSpec((None, bm, FF_SHARD), lambda j, i, k: (j, i, 0))
    shp = jax.ShapeDtypeStruct((N_CHIPS, s, FF_SHARD), BF16)
    dg, du = _matmul(
        tag + "_dact", (N_CHIPS, s // bm, 1),
        [(dh, pl.BlockSpec((bm, D_MODEL), lambda j, i, k: (i, 0)),
          wd, pl.BlockSpec((None, FF_SHARD, D_MODEL), lambda j, i, k: (j, 0, 0)), NT)],
        [(dadg, slab), (dadu, slab)], [(shp, slab), (shp, slab)], act_bwd, None)

    grads[tag + "_w_down"] = _matmul(
        tag + "_dwd", (N_CHIPS, nk),
        [(a, pl.BlockSpec((None, bk, FF_SHARD), lambda j, k: (j, k, 0)),
          dh, pl.BlockSpec((bk, D_MODEL), lambda j, k: (k, 0)), TN)],
        [], [(jax.ShapeDtypeStruct((N_CHIPS, FF_SHARD, D_MODEL), BF16),
              pl.BlockSpec((None, FF_SHARD, D_MODEL), lambda j, k: (j, 0, 0)))],
        lambda acc: (0.5 * acc,), (FF_SHARD, D_MODEL))[0]

    def dw_up(nm, dact):
        return _matmul(
            nm, (N_CHIPS, nk),
            [(dact, pl.BlockSpec((None, bk, FF_SHARD), lambda j, k: (j, k, 0)),
              n, pl.BlockSpec((bk, D_MODEL), lambda j, k: (k, 0)), TN)],
            [], [(jax.ShapeDtypeStruct((N_CHIPS, FF_SHARD, D_MODEL), BF16),
                  pl.BlockSpec((None, FF_SHARD, D_MODEL), lambda j, k: (j, 0, 0)))],
            _ident, (FF_SHARD, D_MODEL))[0]

    grads[tag + "_w_gate"] = dw_up(tag + "_dwg", dg)
    grads[tag + "_w_up"] = dw_up(tag + "_dwu", du)

    bn = _row_block(s, 512)
    steps = s // bn // 2
    prev, dgain = (), None
    for part, off in (("_dn_a", 0), ("_dn_b", steps)):
        row = pl.BlockSpec((bn, D_MODEL), lambda i, k, off=off: (i + off, 0))
        terms = []
        for j in range(N_CHIPS):
            a_slab = pl.BlockSpec((None, bn, FF_SHARD), lambda i, k, j=j, off=off: (j, i + off, 0))
            w_slab = pl.BlockSpec((None, FF_SHARD, D_MODEL), lambda i, k, j=j: (j, 0, 0))
            terms += [(dg, a_slab, wg, w_slab, NN), (du, a_slab, wu, w_slab, NN)]
        if norm_bwd is None:
            prev = _matmul(tag + part, (steps, 1), terms, [], [(jax.ShapeDtypeStruct((s, D_MODEL), F32), row)],
                           _ident, None, fill=prev)
            continue
        h, gain, dres = norm_bwd
        vec = pl.BlockSpec((1, D_MODEL), lambda i, k: (0, 0))
        res = _matmul(
            tag + part, (steps, 1), terms, [(h, row), (gain, vec), (dres, row)],
            [(jax.ShapeDtypeStruct((s, D_MODEL), F32), row), (jax.ShapeDtypeStruct((s, D_MODEL), BF16), row),
             (jax.ShapeDtypeStruct((1, D_MODEL), F32), vec)],
            _norm_bwd_epilogue(D_MODEL), None, fill=prev, summed=(2,))
        prev = res[:2]
        dgain = res[2] if dgain is None else dgain + res[2]
    return prev[0] if norm_bwd is None else (prev[0], prev[1], dgain)


def _mm_nn(name, a, b, out_dtype, res=None, gain=None):
    s, k = a.shape
    nn = b.shape[1]
    bm = _row_block(s)
    row = pl.BlockSpec((bm, nn), lambda i, kk: (i, 0))
    term = [(a, pl.BlockSpec((bm, k), lambda i, kk: (i, 0)), b, pl.BlockSpec((k, nn), lambda i, kk: (0, 0)), NN)]
    if res is None:
        return _matmul(name, (s // bm, 1), term, [], [(jax.ShapeDtypeStruct((s, nn), out_dtype), row)], _ident, None)[0]
    extras, outs = _residual_outs(s, bm, gain)
    res_out = _matmul(name, (s // bm, 1), term, [(res, row)] + extras, outs,
                      _residual_epilogue(1.0, gain is not None), None)
    return res_out if gain is not None else res_out[0]


def _mm_nt(name, a, b, out_dtype, attn_out=None, nh=0, dv=0):
    s, nn = a.shape
    k = b.shape[0]
    bm = _row_block(s)
    term = [(a, pl.BlockSpec((bm, nn), lambda i, kk: (i, 0)), b, pl.BlockSpec((k, nn), lambda i, kk: (0, 0)), NT)]
    out = (jax.ShapeDtypeStruct((s, k), out_dtype), pl.BlockSpec((bm, k), lambda i, kk: (i, 0)))
    if attn_out is None:
        return _matmul(name, (s // bm, 1), term, [], [out], _ident, None)[0]

    def with_delta(acc, o):
        do = acc.astype(out_dtype).astype(F32)
        cols = [jnp.sum(do[:, h * dv:(h + 1) * dv] * o[:, h * dv:(h + 1) * dv].astype(F32), axis=-1, keepdims=True)
                for h in range(nh)]
        return acc, jnp.stack(cols, axis=0)

    return _matmul(
        name, (s // bm, 1), term, [(attn_out, pl.BlockSpec((bm, nh * dv), lambda i, kk: (i, 0)))],
        [out, (jax.ShapeDtypeStruct((nh, s, 1), F32), pl.BlockSpec((nh, bm, 1), lambda i, kk: (0, i, 0)))],
        with_delta, None)


def _mm_nt_norm_bwd(name, a, b, h, gain, dres):
    s, nn = a.shape
    bm = _row_block(s, 512)
    extras, outs, summed = _norm_bwd_operands(s, bm, h, gain, dres)
    return _matmul(
        name, (s // bm, 1),
        [(a, pl.BlockSpec((bm, nn), lambda i, kk: (i, 0)), b, pl.BlockSpec(b.shape, lambda i, kk: (0, 0)), NT)],
        extras, outs, _norm_bwd_epilogue(D_MODEL), None, summed=summed)


def _w_in_dx_norm_bwd(dz, w_t, h, gain, dres):
    s = dz.shape[0]
    bm = _row_block(s, 512)
    epilogue = _norm_bwd_epilogue(D_MODEL)

    def body(dz_ref, w_ref, h_ref, g_ref, r_ref, dx_ref, dxb_ref, dg_ref):
        dzv = dz_ref[...]
        dn = jnp.concatenate([_dot(dzv, w_ref[j], NN) for j in range(N_CHIPS)], axis=1)
        dx, _, dgain = epilogue(dn, h_ref[...], g_ref[...], r_ref[...])
        dx_ref[...] = dx
        dxb_ref[...] = dx.astype(BF16)

        @pl.when(pl.program_id(0) == 0)
        def _():
            dg_ref[...] = dgain

        @pl.when(pl.program_id(0) > 0)
        def _():
            dg_ref[...] += dgain

    row = pl.BlockSpec((bm, D_MODEL), lambda i: (i, 0))
    vec = pl.BlockSpec((1, D_MODEL), lambda i: (0, 0))
    return _call(
        body, name="w_in_dx", grid=(s // bm,),
        in_specs=[row, pl.BlockSpec(w_t.shape, lambda i: (0, 0, 0)), row, vec, row],
        out_specs=[row, row, vec],
        out_shape=[jax.ShapeDtypeStruct((s, D_MODEL), F32), jax.ShapeDtypeStruct((s, D_MODEL), BF16),
                   jax.ShapeDtypeStruct((1, D_MODEL), F32)],
        sem=("arbitrary",), args=[dz, w_t, h, gain, dres])


def _mm_tn(name, a, b, out_dtype=BF16):
    s, k = a.shape
    nn = b.shape[1]
    bk = _reduce_block(s)
    return _matmul(
        name, (s // bk,),
        [(a, pl.BlockSpec((bk, k), lambda kk: (kk, 0)), b, pl.BlockSpec((bk, nn), lambda kk: (kk, 0)), TN)],
        [], [(jax.ShapeDtypeStruct((k, nn), out_dtype), pl.BlockSpec((k, nn), lambda kk: (0, 0)))],
        _ident, (k, nn))[0]


def _mm_heads_fwd(name, a, w, out_dtype, w_transposed=False):
    s, k = a.shape
    nh = w.shape[0]
    nn = w.shape[1] if w_transposed else w.shape[2]
    bm = _row_block(s)
    return _matmul(
        name, (nh, s // bm, 1),
        [(a, pl.BlockSpec((bm, k), lambda h, i, kk: (i, 0)),
          w, pl.BlockSpec((None,) + w.shape[1:], lambda h, i, kk: (h, 0, 0)), NT if w_transposed else NN)],
        [], [(jax.ShapeDtypeStruct((s, nh * nn), out_dtype), pl.BlockSpec((bm, nn), lambda h, i, kk: (i, h)))],
        _ident, None)[0]


def _mm_heads_bwd(name, dy, a, w, w_transposed=False):
    s, k = a.shape
    nh = w.shape[0]
    nn = w.shape[1] if w_transposed else w.shape[2]
    bm = _row_block(s)
    bk = _reduce_block(s)
    w_spec = pl.BlockSpec((None,) + w.shape[1:], lambda i, h: (h, 0, 0))
    da = _matmul(
        name + "_dx", (s // bm, nh),
        [(dy, pl.BlockSpec((bm, nn), lambda i, h: (i, h)), w, w_spec, NN if w_transposed else NT)],
        [], [(jax.ShapeDtypeStruct((s, k), F32), pl.BlockSpec((bm, k), lambda i, h: (i, 0)))], _ident, (bm, k))[0]
    a_term = (a, pl.BlockSpec((bk, k), lambda h, kk: (kk, 0)))
    dy_term = (dy, pl.BlockSpec((bk, nn), lambda h, kk: (kk, h)))
    lhs, rhs = (dy_term, a_term) if w_transposed else (a_term, dy_term)
    dw = _matmul(
        name + "_dw", (nh, s // bk), [lhs + rhs + (TN,)],
        [], [(jax.ShapeDtypeStruct(w.shape, BF16), pl.BlockSpec((None,) + w.shape[1:], lambda h, kk: (h, 0, 0)))],
        _ident, w.shape[1:])[0]
    return da, dw


def _w_in_fwd(n, w_t):
    s = n.shape[0]
    bm = _row_block(s)
    nh, nout, kin = w_t.shape
    terms = [(n, pl.BlockSpec((bm, kin), lambda i, k, j=j: (i, j)),
              w_t, pl.BlockSpec((None, nout, kin), lambda i, k, j=j: (j, 0, 0)), NT) for j in range(nh)]
    row = pl.BlockSpec((bm, nout), lambda i, k: (i, 0))
    return _matmul("w_in", (s // bm, 1), terms, [], [(jax.ShapeDtypeStruct((s, nout), F32), row)], _ident, None)[0]


def _w_in_dw(dz, n):
    s, nout = dz.shape
    kin = n.shape[1] // N_CHIPS
    bk = _reduce_block(s)
    return _matmul(
        "w_in_dw", (N_CHIPS, s // bk),
        [(dz, pl.BlockSpec((bk, nout), lambda j, k: (k, 0)), n, pl.BlockSpec((bk, kin), lambda j, k: (k, j)), TN)],
        [], [(jax.ShapeDtypeStruct((N_CHIPS, nout, kin), BF16), pl.BlockSpec((None, nout, kin), lambda j, k: (j, 0, 0)))],
        _ident, (nout, kin))[0]


def _rope_tables(positions):
    half = ROPE_DIM // 2
    freqs = 1.0 / (ROPE_BASE ** (jnp.arange(0, ROPE_DIM, 2, dtype=F32) / ROPE_DIM))
    ang = positions.astype(F32)[:, None] * freqs
    cos, sin = jnp.cos(ang), jnp.sin(ang)
    z = jnp.zeros_like(cos)
    tc = jnp.concatenate([cos, cos, z, z], axis=-1)
    ta = jnp.concatenate([-sin, z, z, z], axis=-1)
    tb = jnp.concatenate([z, sin, z, z], axis=-1)
    assert tc.shape[-1] == 4 * half
    return tc, ta, tb


def _rope(x, tc, ta, tb):
    return x * tc + pltpu.roll(x, 96, 1) * ta + pltpu.roll(x, 32, 1) * tb


def _rope_t(dy, tc, ta, tb):
    return dy * tc + pltpu.roll(dy * ta, 32, 1) + pltpu.roll(dy * tb, 96, 1)


def _q_rope(q, tc, ta, tb, transpose):
    s = q.shape[0]
    bm = _row_block(s, 512)
    rot = _rope_t if transpose else _rope

    def body(q_ref, tc_ref, ta_ref, tb_ref, o_ref):
        c, a, b = tc_ref[...], ta_ref[...], tb_ref[...]
        for h in range(MLA_HEADS):
            lo = h * HEAD_QK
            o_ref[:, lo:lo + 128] = q_ref[:, lo:lo + 128].astype(BF16)
            o_ref[:, lo + 128:lo + 256] = rot(q_ref[:, lo + 128:lo + 256], c, a, b).astype(BF16)

    row = pl.BlockSpec((bm, MLA_HEADS * HEAD_QK), lambda i: (i, 0))
    tab = pl.BlockSpec((bm, 128), lambda i: (i, 0))
    return pl.pallas_call(
        body, name="q_rope_t" if transpose else "q_rope", grid=(s // bm,), in_specs=[row, tab, tab, tab],
        out_specs=row, out_shape=jax.ShapeDtypeStruct((s, MLA_HEADS * HEAD_QK), BF16),
        compiler_params=_params("parallel"),
    )(q, tc, ta, tb)


def _kv_assemble(kv, z, tc, ta, tb):
    s = kv.shape[0]
    bm = _row_block(s, 512)

    def body(kv_ref, kr_ref, tc_ref, ta_ref, tb_ref, k_ref, v_ref):
        kpe = _rope(kr_ref[...], tc_ref[...], ta_ref[...], tb_ref[...]).astype(BF16)
        for h in range(MLA_HEADS):
            lo = h * 256
            k_ref[:, lo:lo + 128] = kv_ref[:, lo:lo + 128].astype(BF16)
            k_ref[:, lo + 128:lo + 256] = kpe
            v_ref[:, h * 128:(h + 1) * 128] = kv_ref[:, lo + 128:lo + 256].astype(BF16)

    row = pl.BlockSpec((bm, 1024), lambda i: (i, 0))
    tab = pl.BlockSpec((bm, 128), lambda i: (i, 0))
    return pl.pallas_call(
        body, name="kv_assemble", grid=(s // bm,),
        in_specs=[row, pl.BlockSpec((bm, 128), lambda i: (i, 3)), tab, tab, tab],
        out_specs=[row, pl.BlockSpec((bm, 512), lambda i: (i, 0))],
        out_shape=[jax.ShapeDtypeStruct((s, 1024), BF16), jax.ShapeDtypeStruct((s, 512), BF16)],
        compiler_params=_params("parallel"),
    )(kv, z, tc, ta, tb)


def _kv_assemble_bwd(dk, dv, tc, ta, tb):
    s = dk.shape[0]
    bm = _row_block(s, 512)

    def body(dk_ref, dv_ref, tc_ref, ta_ref, tb_ref, dkv_ref, dkr_ref):
        dpe = None
        for h in range(MLA_HEADS):
            lo = h * 256
            dkv_ref[:, lo:lo + 128] = dk_ref[:, lo:lo + 128].astype(BF16)
            dkv_ref[:, lo + 128:lo + 256] = dv_ref[:, h * 128:(h + 1) * 128].astype(BF16)
            t = dk_ref[:, lo + 128:lo + 256]
            dpe = t if dpe is None else dpe + t
        dkr_ref[...] = _rope_t(dpe, tc_ref[...], ta_ref[...], tb_ref[...])

    row = pl.BlockSpec((bm, 1024), lambda i: (i, 0))
    tab = pl.BlockSpec((bm, 128), lambda i: (i, 0))
    return pl.pallas_call(
        body, name="kv_assemble_bwd", grid=(s // bm,),
        in_specs=[row, pl.BlockSpec((bm, 512), lambda i: (i, 0)), tab, tab, tab],
        out_specs=[row, tab],
        out_shape=[jax.ShapeDtypeStruct((s, 1024), BF16), jax.ShapeDtypeStruct((s, 128), F32)],
        compiler_params=_params("parallel"),
    )(dk, dv, tc, ta, tb)


def _norm_bf16(x, g):
    r = lax.rsqrt(jnp.mean(x * x, axis=-1, keepdims=True) + RMS_EPS)
    return ((x * r) * g).astype(BF16)


def _qkv_prep(z, q_gain, kv_gain, wq_t, wkv, tc, ta, tb):
    s = z.shape[0]
    bm = _row_block(s, 512)

    def body(zq_ref, zkv_ref, zkr_ref, qg_ref, kvg_ref, wq_ref, wkv_ref, tc_ref, ta_ref, tb_ref,
             qn_ref, kvn_ref, q_ref, k_ref, v_ref):
        c, a, b = tc_ref[...], ta_ref[...], tb_ref[...]
        qn = _norm_bf16(zq_ref[...], qg_ref[...])
        kvn = _norm_bf16(zkv_ref[...], kvg_ref[...])
        qn_ref[...] = qn
        kvn_ref[...] = kvn
        kpe = _rope(zkr_ref[...], c, a, b).astype(BF16)
        for h in range(MLA_HEADS):
            lo = h * HEAD_QK
            qp = _dot(qn, wq_ref[h], NT)
            q_ref[:, lo:lo + 128] = qp[:, :128].astype(BF16)
            q_ref[:, lo + 128:lo + 256] = _rope(qp[:, 128:], c, a, b).astype(BF16)
            kv = _dot(kvn, wkv_ref[h], NN)
            k_ref[:, lo:lo + 128] = kv[:, :128].astype(BF16)
            k_ref[:, lo + 128:lo + 256] = kpe
            v_ref[:, h * HEAD_V:(h + 1) * HEAD_V] = kv[:, 128:].astype(BF16)

    def cols(width, blk):
        return pl.BlockSpec((bm, width), lambda i: (i, blk))

    def whole(a):
        return pl.BlockSpec(a.shape, lambda i: (0,) * a.ndim)

    tab = cols(128, 0)
    return _call(
        body, name="qkv_prep", grid=(s // bm,),
        in_specs=[cols(Q_LORA, 0), cols(KV_LORA, 2), cols(128, 3), whole(q_gain), whole(kv_gain), whole(wq_t),
                  whole(wkv), tab, tab, tab],
        out_specs=[cols(Q_LORA, 0), cols(KV_LORA, 0), cols(1024, 0), cols(1024, 0), cols(512, 0)],
        out_shape=[jax.ShapeDtypeStruct((s, Q_LORA), BF16), jax.ShapeDtypeStruct((s, KV_LORA), BF16),
                   jax.ShapeDtypeStruct((s, 1024), BF16), jax.ShapeDtypeStruct((s, 1024), BF16),
                   jax.ShapeDtypeStruct((s, 512), BF16)],
        sem=("parallel",), args=[z, z, z, q_gain, kv_gain, wq_t, wkv, tc, ta, tb])


def _qkv_prep_bwd(dq, dk, dv, z, qn, kvn, q_gain, kv_gain, wq_t, wkv, tc, ta, tb):
    s = z.shape[0]
    bm = _row_block(s, 512)
    nsteps = s // bm

    def body(dq_ref, dk_ref, dv_ref, zq_ref, zkv_ref, qn_ref, kvn_ref, qg_ref, kvg_ref, wq_ref, wkv_ref,
             tc_ref, ta_ref, tb_ref, dz_ref, dqg_ref, dkvg_ref, dwq_ref, dwkv_ref, wq_acc, wkv_acc):
        i = pl.program_id(0)
        c, a, b = tc_ref[...], ta_ref[...], tb_ref[...]

        @pl.when(i == 0)
        def _():
            wq_acc[...] = jnp.zeros_like(wq_acc)
            wkv_acc[...] = jnp.zeros_like(wkv_acc)

        qn, kvn = qn_ref[...], kvn_ref[...]
        dqn = jnp.zeros((bm, Q_LORA), F32)
        dkvn = jnp.zeros((bm, KV_LORA), F32)
        dpe = jnp.zeros((bm, 128), F32)
        for h in range(MLA_HEADS):
            lo = h * HEAD_QK
            dqp = jnp.concatenate([dq_ref[:, lo:lo + 128].astype(BF16),
                                   _rope_t(dq_ref[:, lo + 128:lo + 256], c, a, b).astype(BF16)], axis=1)
            dqn = dqn + _dot(dqp, wq_ref[h], NN)
            wq_acc[h] += _dot(dqp, qn, TN)
            dkv = jnp.concatenate([dk_ref[:, lo:lo + 128].astype(BF16),
                                   dv_ref[:, h * HEAD_V:(h + 1) * HEAD_V].astype(BF16)], axis=1)
            dkvn = dkvn + _dot(dkv, wkv_ref[h], NT)
            wkv_acc[h] += _dot(kvn, dkv, TN)
            dpe = dpe + dk_ref[:, lo + 128:lo + 256]
        dcq, dqg = _rms_bwd_math(dqn, zq_ref[...], qg_ref[...], Q_LORA)
        dckv, dkvg = _rms_bwd_math(dkvn, zkv_ref[...], kvg_ref[...], KV_LORA)
        dz_ref[:, 0:Q_LORA] = dcq.astype(BF16)
        dz_ref[:, Q_LORA:Q_LORA + KV_LORA] = dckv.astype(BF16)
        dz_ref[:, Q_LORA + KV_LORA:512] = _rope_t(dpe, c, a, b).astype(BF16)

        @pl.when(i == 0)
        def _():
            dqg_ref[...] = dqg
            dkvg_ref[...] = dkvg

        @pl.when(i > 0)
        def _():
            dqg_ref[...] += dqg
            dkvg_ref[...] += dkvg

        @pl.when(i == nsteps - 1)
        def _():
            dwq_ref[...] = wq_acc[...].astype(BF16)
            dwkv_ref[...] = wkv_acc[...].astype(BF16)

    def cols(width, blk):
        return pl.BlockSpec((bm, width), lambda i: (i, blk))

    def whole(shape):
        return pl.BlockSpec(shape, lambda i: (0,) * len(shape))

    tab = cols(128, 0)
    return _call(
        body, name="qkv_prep_bwd", grid=(nsteps,),
        in_specs=[cols(1024, 0), cols(1024, 0), cols(512, 0), cols(Q_LORA, 0), cols(KV_LORA, 2), cols(Q_LORA, 0),
                  cols(KV_LORA, 0), whole(q_gain.shape), whole(kv_gain.shape), whole(wq_t.shape), whole(wkv.shape),
                  tab, tab, tab],
        out_specs=[cols(512, 0), whole(q_gain.shape), whole(kv_gain.shape), whole(wq_t.shape), whole(wkv.shape)],
        out_shape=[jax.ShapeDtypeStruct((s, 512), BF16), jax.ShapeDtypeStruct(q_gain.shape, F32),
                   jax.ShapeDtypeStruct(kv_gain.shape, F32), jax.ShapeDtypeStruct(wq_t.shape, BF16),
                   jax.ShapeDtypeStruct(wkv.shape, BF16)],
        scratch_shapes=[pltpu.VMEM(wq_t.shape, F32), pltpu.VMEM(wkv.shape, F32)],
        sem=("arbitrary",), args=[dq, dk, dv, z, z, qn, kvn, q_gain, kv_gain, wq_t, wkv, tc, ta, tb])


def _causal_mask(s, row0, col0):
    rows = row0 + lax.broadcasted_iota(jnp.int32, s.shape, 0)
    cols = col0 + lax.broadcasted_iota(jnp.int32, s.shape, 1)
    return jnp.where(cols <= rows, s, -jnp.inf)


def _attn_fwd(name, q, k, k_off, v, v_off, nh, dq, dv, scale, causal, blk):
    sq, sk = q.shape[0], k.shape[0]
    bq = min(blk, sq)
    bk = min(blk, sk)
    nkv = sk // bk
    assert not causal or (sq == sk and bq == bk)

    hq = bq
    log2e = 1.4426950408889634
    c2 = scale * log2e

    def body(q_ref, k_ref, v_ref, o_ref, lse_ref):
        qi = pl.program_id(1)
        qs = (q_ref[...],)

        def step(j, carry, masked):
            rows = pl.ds(pl.multiple_of(j * bk, bk), bk)
            kb, vb = k_ref[rows, :], v_ref[rows, :]
            out = []
            for t, (m, l, acc) in enumerate(carry):
                s = _dot(qs[t], kb, NT) * c2
                if masked:
                    s = _causal_mask(s, qi * bq + t * hq, j * bk)
                m_new = jnp.maximum(m, jnp.max(s, axis=-1, keepdims=True))
                alpha = jnp.exp2(m - m_new)
                p = jnp.exp2(s - m_new)
                l = alpha * l + jnp.sum(p, axis=-1, keepdims=True)
                acc = alpha * acc + _dot(p, vb, NN)
                out.append((m_new, l, acc))
            return tuple(out)

        one = (jnp.full((hq, 1), -jnp.inf, F32), jnp.zeros((hq, 1), F32), jnp.zeros((hq, dv), F32))
        init = (one,)
        if causal:
            carry = lax.fori_loop(0, qi, lambda j, c: step(j, c, False), init)
            fin = step(qi, carry, True)
        else:
            fin = lax.fori_loop(0, nkv, lambda j, c: step(j, c, False), init)
        for t, (m, l, acc) in enumerate(fin):
            o_ref[t * hq:(t + 1) * hq, :] = (acc / l).astype(o_ref.dtype)
            lse_ref[t * hq:(t + 1) * hq, :] = m * (1.0 / log2e) + jnp.log(l)

    return _call(
        body, name=name, grid=(nh, sq // bq),
        in_specs=[pl.BlockSpec((bq, dq), lambda h, i: (i, h)),
                  pl.BlockSpec((sk, dq), lambda h, i: (0, k_off + h)),
                  pl.BlockSpec((sk, dv), lambda h, i: (0, v_off + h))],
        out_specs=[pl.BlockSpec((bq, dv), lambda h, i: (i, h)), pl.BlockSpec((None, bq, 1), lambda h, i: (h, i, 0))],
        out_shape=[jax.ShapeDtypeStruct((sq, nh * dv), BF16), jax.ShapeDtypeStruct((nh, sq, 1), F32)],
        sem=("parallel", "parallel"), args=[q, k, v])


def _attn_delta(name, do, do_off, o, nh, dv):
    s = o.shape[0]
    bm = _row_block(s, 512)

    def body(do_ref, o_ref, d_ref):
        d_ref[...] = jnp.sum(do_ref[...].astype(F32) * o_ref[...].astype(F32), axis=-1, keepdims=True)

    return pl.pallas_call(
        body, name=name, grid=(nh, s // bm),
        in_specs=[pl.BlockSpec((bm, dv), lambda h, i: (i, do_off + h)), pl.BlockSpec((bm, dv), lambda h, i: (i, h))],
        out_specs=pl.BlockSpec((None, bm, 1), lambda h, i: (h, i, 0)),
        out_shape=jax.ShapeDtypeStruct((nh, s, 1), F32),
        compiler_params=_params("parallel", "parallel"),
    )(do, o)


def _attn_bwd(name, q, k, k_off, v, v_off, do, do_off, lse, delta, nh, dq, dv, scale, causal, blk):
    sq, sk = q.shape[0], k.shape[0]
    bq = min(blk, sq)
    bk = min(blk, sk)
    nq = sq // bq
    assert not causal or (sq == sk and bq == bk)

    def body(q_ref, k_ref, v_ref, do_ref, lse_ref, dl_ref, dq_ref, dk_ref, dv_ref, dk_acc, dv_acc):
        j = pl.program_id(1)

        @pl.when(j == 0)
        def _():
            dq_ref[...] = jnp.zeros_like(dq_ref)

        dk_acc[...] = jnp.zeros_like(dk_acc)
        dv_acc[...] = jnp.zeros_like(dv_acc)
        kv = k_ref[...]
        vv = v_ref[...]

        def step(i, masked):
            rows = pl.ds(pl.multiple_of(i * bq, bq), bq)
            qv = q_ref[rows, :]
            dov = do_ref[rows, :].astype(BF16)
            s = _dot(qv, kv, NT) * scale
            if masked:
                s = _causal_mask(s, i * bq, j * bk)
            p = jnp.exp(s - lse_ref[rows, :])
            dp = _dot(dov, vv, NT)
            ds = (p * (dp - dl_ref[rows, :]) * scale).astype(BF16)
            dv_acc[...] += _dot(p, dov, TN)
            dk_acc[...] += _dot(ds, qv, TN)
            dq_ref[rows, :] += _dot(ds, kv, NN)

        if causal:
            step(j, True)

            def loop(i, c):
                step(i, False)
                return c

            lax.fori_loop(j + 1, nq, loop, 0)
        else:
            def loop(i, c):
                step(i, False)
                return c

            lax.fori_loop(0, nq, loop, 0)
        dk_ref[...] = dk_acc[...]
        dv_ref[...] = dv_acc[...]

    stat = pl.BlockSpec((None, sq, 1), lambda h, j: (h, 0, 0))
    return _call(
        body, name=name, grid=(nh, sk // bk),
        in_specs=[pl.BlockSpec((sq, dq), lambda h, j: (0, h)),
                  pl.BlockSpec((bk, dq), lambda h, j: (j, k_off + h)),
                  pl.BlockSpec((bk, dv), lambda h, j: (j, v_off + h)),
                  pl.BlockSpec((sq, dv), lambda h, j: (0, do_off + h)), stat, stat],
        out_specs=[pl.BlockSpec((sq, dq), lambda h, j: (0, h)),
                   pl.BlockSpec((bk, dq), lambda h, j: (j, h)),
                   pl.BlockSpec((bk, dv), lambda h, j: (j, h))],
        out_shape=[jax.ShapeDtypeStruct((sq, nh * dq), F32), jax.ShapeDtypeStruct((sk, nh * dq), F32),
                   jax.ShapeDtypeStruct((sk, nh * dv), F32)],
        scratch_shapes=[pltpu.VMEM((bk, dq), F32), pltpu.VMEM((bk, dv), F32)],
        sem=("parallel", "arbitrary"), args=[q, k, v, do, lse, delta])


def _pool_diff(z, g):
    s = z.shape[0]
    t = lax.broadcasted_iota(jnp.int32, z.shape, 0)
    acc = z
    sums = []
    for k in (1, 2, 4, 8):
        acc = acc + jnp.where(t >= k, pltpu.roll(acc, k, 0), 0.0)
        sums.append(acc)
    win = jnp.where(g == 0, sums[0], jnp.where(g == 1, sums[1], jnp.where(g == 2, sums[2], sums[3])))
    w = lax.shift_left(jnp.int32(2), g)
    count = jnp.minimum(t + 1, w).astype(F32)
    del s
    return win / count - z, count


def _pool_fwd(z, pool_w, pool_scale):
    s = z.shape[0]

    def body(z_ref, w_ref, sc_ref, o_ref):
        diff, _ = _pool_diff(z_ref[...], pl.program_id(0))
        o_ref[...] = (_dot(diff, w_ref[...], NN) * sc_ref[...]).astype(o_ref.dtype)

    return _call(
        body, name="pool_fwd", grid=(POOL_GROUPS,),
        in_specs=[pl.BlockSpec((s, POOL_CH), lambda g: (0, 4 + g)),
                  pl.BlockSpec((None, POOL_CH, POOL_CH), lambda g: (g, 0, 0)),
                  pl.BlockSpec((1, POOL_CH), lambda g: (0, g))],
        out_specs=[pl.BlockSpec((s, POOL_CH), lambda g: (0, g))],
        out_shape=[jax.ShapeDtypeStruct((s, POOL_GROUPS * POOL_CH), BF16)],
        sem=("parallel",), args=[z, pool_w, pool_scale])[0]


def _pool_bwd(dcat, z, pool_w, pool_scale):
    s = z.shape[0]

    def body(dp_ref, z_ref, w_ref, sc_ref, dz_ref, dw_ref, dsc_ref):
        g = pl.program_id(0)
        diff, count = _pool_diff(z_ref[...], g)
        dpf = dp_ref[...].astype(F32)
        u = _dot(diff, w_ref[...], NN)
        dsc_ref[...] = jnp.sum(dpf * u, axis=0, keepdims=True)
        du = (dpf * sc_ref[...]).astype(BF16)
        dw_ref[...] = _dot(diff, du, TN)
        ddiff = _dot(du, w_ref[...], NT)
        t = lax.broadcasted_iota(jnp.int32, ddiff.shape, 0)
        acc = ddiff / count
        sums = []
        for k in (1, 2, 4, 8):
            acc = acc + jnp.where(t < s - k, pltpu.roll(acc, s - k, 0), 0.0)
            sums.append(acc)
        win = jnp.where(g == 0, sums[0], jnp.where(g == 1, sums[1], jnp.where(g == 2, sums[2], sums[3])))
        dz_ref[...] = win - ddiff

    return pl.pallas_call(
        body, name="pool_bwd", grid=(POOL_GROUPS,),
        in_specs=[pl.BlockSpec((s, POOL_CH), lambda g: (0, 4 + g)),
                  pl.BlockSpec((s, POOL_CH), lambda g: (0, 4 + g)),
                  pl.BlockSpec((None, POOL_CH, POOL_CH), lambda g: (g, 0, 0)),
                  pl.BlockSpec((1, POOL_CH), lambda g: (0, g))],
        out_specs=[pl.BlockSpec((s, POOL_CH), lambda g: (0, g)),
                   pl.BlockSpec((None, POOL_CH, POOL_CH), lambda g: (g, 0, 0)),
                   pl.BlockSpec((1, POOL_CH), lambda g: (0, g))],
        out_shape=[jax.ShapeDtypeStruct((s, POOL_GROUPS * POOL_CH), F32),
                   jax.ShapeDtypeStruct((POOL_GROUPS, POOL_CH, POOL_CH), F32),
                   jax.ShapeDtypeStruct((1, POOL_GROUPS * POOL_CH), F32)],
        compiler_params=_params("parallel"),
    )(dcat, z, pool_w, pool_scale)


def _local_step(x, mem, positions, target, w, grads):
    tc, ta, tb = _rope_tables(positions)
    blk = _ATT_BLOCK

    if "ffn1_shards" in w:
        n1, a1, dadu1, dadg1, w["ffn1_w_gate"], w["ffn1_w_up"], w["ffn1_w_down"] = _ffn1_up_gather(
            x, w["ffn1_norm"], *w["ffn1_shards"])
    else:
        n1 = _rmsnorm_fwd("ffn1_norm", x, w["ffn1_norm"], D_MODEL)
        a1, dadu1, dadg1 = _ffn_up("ffn1_up", n1, w["ffn1_w_gate"], w["ffn1_w_up"])
    h1, n2 = _ffn_down("ffn1_down", a1, w["ffn1_w_down"], x, w["mix_norm"])
    z = _w_in_fwd(n2, w["w_in"])
    qn, kvn, qf, kf, vf = _qkv_prep(z, w["q_norm"], w["kv_norm"], w["w_q_up"], w["w_kv_up"], tc, ta, tb)
    att, lse = _attn_fwd("mla_fwd", qf, kf, 0, vf, 0, MLA_HEADS, HEAD_QK, HEAD_V, MLA_SCALE, True, blk)
    pool = _pool_fwd(z, w["pool_w"], w["pool_scale"])
    s = x.shape[0]
    bm = _row_block(s)
    row = pl.BlockSpec((bm, D_MODEL), lambda i, k: (i, 0))
    half = pl.BlockSpec((bm, 512), lambda i, k: (i, 0))
    h2, n3 = _matmul(
        "w_out", (s // bm, 1),
        [(att, half, w["w_out"], pl.BlockSpec((512, D_MODEL), lambda i, k: (0, 0)), NN),
         (pool, half, w["w_out"], pl.BlockSpec((512, D_MODEL), lambda i, k: (1, 0)), NN)],
        [(h1, row)] + _residual_outs(s, bm, w["xattn_norm"])[0], _residual_outs(s, bm, w["xattn_norm"])[1],
        _residual_epilogue(1.0, True), None)
    memn = _rmsnorm_fwd("mem_norm", mem, w["mem_norm"], D_MODEL)
    qm = _mm_nn("w_mq", n3, w["w_mq"], BF16)
    kvm = _mm_heads_fwd("w_mkv", memn, w["w_mkv"], BF16)
    om, lse_m = _attn_fwd("xattn_fwd", qm, kvm, 0, kvm, MEM_HEADS, MEM_HEADS, MEM_HEAD_DIM, MEM_HEAD_DIM,
                          MEM_SCALE, False, blk)
    h3, n4 = _mm_nn("w_mo", om, w["w_mo"], F32, res=h2, gain=w["ffn2_norm"])
    a2, dadu2, dadg2 = _ffn_up("ffn2_up", n4, w["ffn2_w_gate"], w["ffn2_w_up"])
    dh4, dh4b, loss_vec, d_final = _ffn_down("ffn2_down", a2, w["ffn2_w_down"], h3, loss=(w["final_norm"], target))
    grads["final_norm"] = d_final

    dh3, dh3b, grads["ffn2_norm"] = _ffn_bwd("ffn2", dh4b, n4, dadg2, dadu2, a2, w["ffn2_w_gate"], w["ffn2_w_up"],
                                             w["ffn2_w_down"], grads, norm_bwd=(h3, w["ffn2_norm"], dh4))

    dom, delta_m = _mm_nt("w_mo_dx", dh3b, w["w_mo"], BF16, attn_out=om, nh=MEM_HEADS, dv=MEM_HEAD_DIM)
    grads["w_mo"] = _mm_tn("w_mo_dw", om, dh3b)
    dqm, dkm, dvm = _attn_bwd("xattn_bwd", qm, kvm, 0, kvm, MEM_HEADS, dom, 0, lse_m, delta_m, MEM_HEADS,
                              MEM_HEAD_DIM, MEM_HEAD_DIM, MEM_SCALE, False, blk)
    dkvm = jnp.concatenate([dkm, dvm], axis=1).astype(BF16)
    dh2, dh2b, grads["xattn_norm"] = _mm_nt_norm_bwd("w_mq_dx", dqm, w["w_mq"], h2, w["xattn_norm"], dh3)
    grads["w_mq"] = _mm_tn("w_mq_dw", n3, dqm)
    dmemn, grads["w_mkv"] = _mm_heads_bwd("w_mkv", dkvm, memn, w["w_mkv"])
    _, grads["mem_norm"] = _rmsnorm_bwd("mem_norm_bwd", dmemn, mem, w["mem_norm"], D_MODEL, out_dtype=BF16)

    dcat, delta = _mm_nt("w_out_dx", dh2b, w["w_out"], BF16, attn_out=att, nh=MLA_HEADS, dv=HEAD_V)
    grads["w_out"] = jnp.concatenate([_mm_tn("w_out_dw_a", att, dh2b), _mm_tn("w_out_dw_p", pool, dh2b)], axis=0)
    dzp, grads["pool_w"], grads["pool_scale"] = _pool_bwd(dcat, z, w["pool_w"], w["pool_scale"])
    dqf, dkf, dvf = _attn_bwd("mla_bwd", qf, kf, 0, vf, 0, dcat, 0, lse, delta, MLA_HEADS, HEAD_QK, HEAD_V,
                              MLA_SCALE, True, blk)
    dz_lat, grads["q_norm"], grads["kv_norm"], grads["w_q_up"], grads["w_kv_up"] = _qkv_prep_bwd(
        dqf, dkf, dvf, z, qn, kvn, w["q_norm"], w["kv_norm"], w["w_q_up"], w["w_kv_up"], tc, ta, tb)
    dz = jnp.concatenate([dz_lat, dzp.astype(BF16)], axis=1)
    grads["w_in"] = _w_in_dw(dz, n2)
    dh1, dh1b, grads["mix_norm"] = _w_in_dx_norm_bwd(dz, w["w_in"], h1, w["mix_norm"], dh2)

    dn1 = _ffn_bwd("ffn1", dh1b, n1, dadg1, dadu1, a1, w["ffn1_w_gate"], w["ffn1_w_up"], w["ffn1_w_down"], grads)
    dx, grads["ffn1_norm"], _ = _rmsnorm_bwd("ffn1_norm_bwd", dn1, x, w["ffn1_norm"], D_MODEL, dres=dh1)
    return loss_vec[0, 0], dx


def _mesh_pos():
    x, y, c = lax.axis_index("x"), lax.axis_index("y"), lax.axis_index("c")
    chips = [(1 - x, y), (x, 1 - y), (1 - x, 1 - y)]
    chip_ids = [2 * cx + cy for cx, cy in chips]
    return x, y, c, 2 * x + y, chips, chip_ids


def _half_rows(c, rows):
    hr = rows // 2
    return pl.ds(pl.multiple_of(c * hr, 16), hr), pl.ds(pl.multiple_of((1 - c) * hr, 16), hr)


def _ag_ici_stage(shards, relative=False):
    n = len(shards)

    def copies(ins, outs):
        x, y, c, me, chips, _ = _mesh_pos()
        out = []
        for k in range(n):
            mine, _ = _half_rows(c, ins[k].shape[0])
            out.append((ins[k], outs[k].at[0 if relative else me], None))
            for j, (cx, cy) in enumerate(chips):
                slab = _REL_OF_PEER[j] if relative else me
                out.append((ins[k].at[mine], outs[k].at[slab, mine], (cx, cy, c)))
        return out

    return _Stage(shards, [jax.ShapeDtypeStruct((N_CHIPS,) + s.shape, s.dtype) for s in shards], 3 * n, n, copies)


def _quarter_rows(c, rows):
    qr = rows // 4
    return pl.ds(pl.multiple_of(c * 2 * qr, 16), qr), pl.ds(pl.multiple_of(c * 2 * qr + qr, 16), qr)


def _ag_fwd_stage(fulls, relative=False):
    n = len(fulls)

    def copies(ins, outs):
        x, y, c, me, chips, chip_ids = _mesh_pos()
        out = []
        for k in range(n):
            q0, q1 = _quarter_rows(c, ins[k].shape[1])
            from_x, from_y, diag = (2, 1, 3) if relative else chip_ids
            out.append((ins[k].at[from_x, q0], outs[k].at[diag if relative else from_x, q0], (*chips[1], c)))
            out.append((ins[k].at[from_y, q1], outs[k].at[diag if relative else from_y, q1], (*chips[0], c)))
        return out

    return _Stage(fulls, [jax.ShapeDtypeStruct(f.shape, f.dtype) for f in fulls], 2 * n, 0, copies,
                  aliases={k: k for k in range(n)})


def _ag_d2d_stage(fulls, relative=False):
    n = len(fulls)

    def copies(ins, outs):
        x, y, c, me, _, chip_ids = _mesh_pos()
        out = []
        for k in range(n):
            mine, _ = _half_rows(c, ins[k].shape[1])
            for j in range(3):
                slab = _REL_OF_PEER[j] if relative else chip_ids[j]
                out.append((ins[k].at[slab, mine], outs[k].at[slab, mine], (x, y, 1 - c)))
        return out

    return _Stage(fulls, [jax.ShapeDtypeStruct(f.shape, f.dtype) for f in fulls], 3 * n, 0, copies,
                  aliases={k: k for k in range(n)})


def _rs_swap_stage(grads):
    n = len(grads)

    def copies(ins, outs):
        x, y, c, _, _, _ = _mesh_pos()
        out = []
        for k in range(n):
            _, other = _half_rows(c, ins[k].shape[1])
            out.append((ins[k].at[:, other, :], outs[k], (x, y, 1 - c)))
        return out

    return _Stage(grads, [jax.ShapeDtypeStruct((N_CHIPS, g.shape[1] // 2, g.shape[2]), g.dtype) for g in grads],
                  n, 0, copies)


_REL_OF_PEER = (2, 1, 3)


def _rs_scatter_stage(sums, relative=False):
    n = len(sums)

    def copies(ins, outs):
        x, y, c, me, chips, chip_ids = _mesh_pos()
        out = []
        for k in range(n):
            mine, _ = _half_rows(c, 2 * ins[k].shape[1])
            out.append((ins[k].at[0 if relative else me], outs[k].at[0, mine, :], None))
            for j, (cx, cy) in enumerate(chips):
                slab = _REL_OF_PEER[j] if relative else chip_ids[j]
                out.append((ins[k].at[slab], outs[k].at[1 + j, mine, :], (cx, cy, c)))
        return out

    return _Stage(sums, [jax.ShapeDtypeStruct((N_CHIPS, 2 * s.shape[1], s.shape[2]), s.dtype) for s in sums],
                  3 * n, n, copies)


def _rs_mirror_stage(parts):
    n = len(parts)

    def copies(ins, outs):
        x, y, c, _, _, _ = _mesh_pos()
        out = []
        for k in range(n):
            mine, _ = _half_rows(c, ins[k].shape[1])
            out.append((ins[k].at[:, mine, :], outs[k].at[:, mine, :], (x, y, 1 - c)))
        return out

    return _Stage(parts, [jax.ShapeDtypeStruct(p.shape, p.dtype) for p in parts], n, 0, copies,
                  aliases={k: k for k in range(n)})


def _pair_add(name, g, r1, core):
    _, rows, cols = g.shape
    hr = rows // 2

    def body(c_ref, g_ref, r_ref, o_ref):
        o_ref[...] = (g_ref[...].astype(F32) + r_ref[...].astype(F32)).astype(BF16)

    half = pl.BlockSpec((None, hr, cols), lambda j, c: (j, 0, 0))
    return pl.pallas_call(
        body, name=name,
        grid_spec=pltpu.PrefetchScalarGridSpec(
            num_scalar_prefetch=1, grid=(N_CHIPS,),
            in_specs=[pl.BlockSpec((None, hr, cols), lambda j, c: (j, c[0], 0)), half], out_specs=half),
        out_shape=jax.ShapeDtypeStruct((N_CHIPS, hr, cols), BF16),
        compiler_params=_params("parallel"),
    )(core, g, r1)


def _all_gather_weights(shards):
    n = len(shards)

    def body(*refs):
        ins, outs = refs[:n], refs[n:2 * n]
        send, recv, loc = refs[2 * n:]
        x, y, c, me, chips, chip_ids = _mesh_pos()
        sib = (x, y, 1 - c)

        def halves(k):
            hr = ins[k].shape[0] // 2
            return pl.ds(pl.multiple_of(c * hr, 16), hr), pl.ds(pl.multiple_of((1 - c) * hr, 16), hr)

        def remote(src, dst, k, j, dev):
            return pltpu.make_async_remote_copy(src_ref=src, dst_ref=dst, send_sem=send.at[k, j],
                                                recv_sem=recv.at[k, j], device_id=dev, device_id_type=_MESH)

        started = []
        local = []
        for k in range(n):
            mine, _ = halves(k)
            cp = pltpu.make_async_copy(ins[k], outs[k].at[me], loc.at[k])
            cp.start()
            local.append(cp)
            for j, (cx, cy) in enumerate(chips):
                cp = remote(ins[k].at[mine], outs[k].at[me, mine], k, j, (cx, cy, c))
                cp.start()
                started.append(cp)
        for k in range(n):
            mine, _ = halves(k)
            for j in range(3):
                land = outs[k].at[chip_ids[j], mine]
                remote(land, land, k, j, sib).wait_recv()
                cp = remote(land, land, k, 3 + j, sib)
                cp.start()
                started.append(cp)
        for k in range(n):
            _, other = halves(k)
            for j in range(3):
                land = outs[k].at[chip_ids[j], other]
                remote(land, land, k, 3 + j, sib).wait_recv()
        for cp in started:
            cp.wait_send()
        for cp in local:
            cp.wait()

    return pl.pallas_call(
        body, name="all_gather_weights", in_specs=[_ANY] * n, out_specs=[_ANY] * n,
        out_shape=[jax.ShapeDtypeStruct((N_CHIPS,) + s.shape, s.dtype) for s in shards],
        scratch_shapes=[pltpu.SemaphoreType.DMA((n, 6)), pltpu.SemaphoreType.DMA((n, 6)),
                        pltpu.SemaphoreType.DMA((n,))],
        compiler_params=pltpu.CompilerParams(vmem_limit_bytes=V7X_VMEM_LIMIT_BYTES),
    )(*shards)


_RS_CHUNK = 32


def _reduce_scatter(name, grads):
    n = len(grads)

    def body(*refs):
        gs, outs = refs[:n], refs[n:2 * n]
        own, r1, r2, fin = (refs[(2 + i) * n:(3 + i) * n] for i in range(4))
        a_send, a_recv, b_send, b_recv, c_send, c_recv, l_in, l_out = refs[6 * n:]
        x, y, c, me, chips, chip_ids = _mesh_pos()
        sib = (x, y, 1 - c)

        def halves(k):
            hr = gs[k].shape[1] // 2
            return hr, pl.ds(pl.multiple_of(c * hr, 16), hr), pl.ds(pl.multiple_of((1 - c) * hr, 16), hr)

        def remote(src, dst, ssem, rsem, dev):
            return pltpu.make_async_remote_copy(src_ref=src, dst_ref=dst, send_sem=ssem, recv_sem=rsem,
                                                device_id=dev, device_id_type=_MESH)

        sends, locals_in = [], []
        for k in range(n):
            hr, mine, other = halves(k)
            cp = remote(gs[k].at[:, other, :], r1[k], a_send.at[k], a_recv.at[k], sib)
            cp.start()
            sends.append(cp)
            cp = pltpu.make_async_copy(gs[k].at[:, mine, :], own[k], l_in.at[k])
            cp.start()
            locals_in.append(cp)

        for k in range(n):
            hr, mine, other = halves(k)
            locals_in[k].wait()
            remote(r1[k], r1[k], a_send.at[k], a_recv.at[k], sib).wait_recv()
            for j in range(N_CHIPS):
                def add(i, carry, k=k, j=j):
                    rows = pl.ds(pl.multiple_of(i * _RS_CHUNK, _RS_CHUNK), _RS_CHUNK)
                    own[k][j, rows, :] = (own[k][j, rows, :].astype(F32) + r1[k][j, rows, :].astype(F32)).astype(BF16)
                    return carry

                lax.fori_loop(0, hr // _RS_CHUNK, add, 0)
            for j, (cx, cy) in enumerate(chips):
                cp = remote(own[k].at[chip_ids[j]], r2[k].at[j], b_send.at[k, j], b_recv.at[k, j], (cx, cy, c))
                cp.start()
                sends.append(cp)

        locals_out = []
        for k in range(n):
            hr, mine, other = halves(k)
            for j in range(3):
                remote(r2[k].at[j], r2[k].at[j], b_send.at[k, j], b_recv.at[k, j], sib).wait_recv()

            def total(i, carry, k=k):
                rows = pl.ds(pl.multiple_of(i * _RS_CHUNK, _RS_CHUNK), _RS_CHUNK)
                acc = own[k][me, rows, :].astype(F32)
                for j in range(3):
                    acc = acc + r2[k][j, rows, :].astype(F32)
                fin[k][rows, :] = acc
                return carry

            lax.fori_loop(0, hr // _RS_CHUNK, total, 0)
            cp = remote(fin[k], outs[k].at[mine, :], c_send.at[k], c_recv.at[k], sib)
            cp.start()
            sends.append(cp)
            cp = pltpu.make_async_copy(fin[k], outs[k].at[mine, :], l_out.at[k])
            cp.start()
            locals_out.append(cp)

        for k in range(n):
            hr, mine, other = halves(k)
            land = outs[k].at[other, :]
            remote(land, land, c_send.at[k], c_recv.at[k], sib).wait_recv()
        for cp in sends:
            cp.wait_send()
        for cp in locals_out:
            cp.wait()

    scratch = []
    for g in grads:
        scratch.append(pltpu.VMEM((N_CHIPS, g.shape[1] // 2, g.shape[2]), BF16))
    for g in grads:
        scratch.append(pltpu.VMEM((N_CHIPS, g.shape[1] // 2, g.shape[2]), BF16))
    for g in grads:
        scratch.append(pltpu.VMEM((3, g.shape[1] // 2, g.shape[2]), BF16))
    for g in grads:
        scratch.append(pltpu.VMEM((g.shape[1] // 2, g.shape[2]), F32))
    dma = pltpu.SemaphoreType.DMA
    scratch += [dma((n,)), dma((n,)), dma((n, 3)), dma((n, 3)), dma((n,)), dma((n,)), dma((n,)), dma((n,))]
    return pl.pallas_call(
        body, name=name, in_specs=[_ANY] * n, out_specs=[_ANY] * n,
        out_shape=[jax.ShapeDtypeStruct(g.shape[1:], F32) for g in grads],
        scratch_shapes=scratch,
        compiler_params=pltpu.CompilerParams(vmem_limit_bytes=V7X_VMEM_LIMIT_BYTES),
    )(*grads)


def _adamw_math(w, g, m, v):
    m = ADAM_B1 * m + (1.0 - ADAM_B1) * g
    v = ADAM_B2 * v + (1.0 - ADAM_B2) * (g * g)
    m_hat = m / (1.0 - ADAM_B1 ** ADAM_STEP)
    v_hat = v / (1.0 - ADAM_B2 ** ADAM_STEP)
    delta = -ADAM_LR * (m_hat / (jnp.sqrt(v_hat) + ADAM_EPS) + ADAM_WD * w)
    return delta, m, v


def _adamw_sum(name, w, parts, m, v):
    r, c = w.shape
    br = r
    while br * c * 4 > (1 << 20) and br % 32 == 0:
        br //= 2

    def body(w_ref, p_ref, m_ref, v_ref, g_ref, d_ref, nm_ref, nv_ref):
        g = p_ref[0].astype(F32)
        for j in range(1, N_CHIPS):
            g = g + p_ref[j].astype(F32)
        d, nm, nv = _adamw_math(w_ref[...], g, m_ref[...], v_ref[...])
        g_ref[...] = g
        d_ref[...] = d
        nm_ref[...] = nm
        nv_ref[...] = nv

    spec = pl.BlockSpec((br, c), lambda i: (i, 0))
    shp = jax.ShapeDtypeStruct((r, c), F32)
    return _call(
        body, name=name, grid=(r // br,),
        in_specs=[spec, pl.BlockSpec((N_CHIPS, br, c), lambda i: (0, i, 0)), spec, spec],
        out_specs=[spec] * 4, out_shape=[shp] * 4, sem=("parallel",), args=[w, parts, m, v])


_SMALL_ROWS = 80


def _small_allreduce_adamw(gpack, wpack, mpack, vpack):
    half = _SMALL_ROWS // 2

    def body(g_ref, w_ref, m_ref, v_ref, go_ref, d_ref, nm_ref, nv_ref, sib_buf, chip_sum, buf, send, recv):
        x, y, c, me, chips, chip_ids = _mesh_pos()
        sib = (x, y, 1 - c)
        mine = pl.ds(pl.multiple_of(c * half, 8), half)

        def remote(src, dst, k, dev):
            return pltpu.make_async_remote_copy(src_ref=src, dst_ref=dst, send_sem=send.at[k], recv_sem=recv.at[k],
                                                device_id=dev, device_id_type=_MESH)

        swap = remote(g_ref, sib_buf, 0, sib)
        swap.start()
        swap.wait()
        chip_sum[...] = g_ref[...] + sib_buf[...]
        buf[me] = chip_sum[...]
        sends = [remote(chip_sum.at[mine], buf.at[me, mine], 1 + j, (cx, cy, c)) for j, (cx, cy) in enumerate(chips)]
        for cp in sends:
            cp.start()
        for cp in sends:
            cp.wait()
        mirrors = [remote(buf.at[chip_ids[j], mine], buf.at[chip_ids[j], mine], 4 + j, sib) for j in range(3)]
        for cp in mirrors:
            cp.start()
        for cp in mirrors:
            cp.wait()
        total = buf[0]
        for i in range(1, N_CHIPS):
            total = total + buf[i]
        go_ref[...] = total
        d, nm, nv = _adamw_math(w_ref[...], total, m_ref[...], v_ref[...])
        d_ref[...] = d
        nm_ref[...] = nm
        nv_ref[...] = nv

    vm = pl.BlockSpec(memory_space=pltpu.VMEM)
    shp = jax.ShapeDtypeStruct((_SMALL_ROWS, D_MODEL), F32)
    return pl.pallas_call(
        body, name="small_allreduce_adamw", in_specs=[vm] * 4, out_specs=[vm] * 4, out_shape=[shp] * 4,
        scratch_shapes=[pltpu.VMEM((_SMALL_ROWS, D_MODEL), F32), pltpu.VMEM((_SMALL_ROWS, D_MODEL), F32),
                        pltpu.VMEM((N_CHIPS, _SMALL_ROWS, D_MODEL), F32), pltpu.SemaphoreType.DMA((7,)),
                        pltpu.SemaphoreType.DMA((7,))],
        compiler_params=pltpu.CompilerParams(vmem_limit_bytes=V7X_VMEM_LIMIT_BYTES),
    )(gpack, wpack, mpack, vpack)


_SMALL_VECTORS = ("ffn1_norm", "mix_norm", "xattn_norm", "mem_norm", "ffn2_norm", "final_norm", "q_norm",
                  "kv_norm", "pool_scale")


_LOSS_ROW = 9


def _pack_small(d, scalar=None):
    rows = []
    for n in _SMALL_VECTORS:
        v = d[n].reshape(1, -1).astype(F32)
        rows.append(jnp.pad(v, ((0, 0), (0, D_MODEL - v.shape[1]))))
    assert len(rows) == _LOSS_ROW
    extra = jnp.zeros((1, D_MODEL), F32) if scalar is None else jnp.pad(scalar.reshape(1, 1), ((0, 0), (0, D_MODEL - 1)))
    rows.append(extra)
    rows.append(jnp.zeros((16 - len(rows), D_MODEL), F32))
    rows.append(d["pool_w"].reshape(64, D_MODEL).astype(F32))
    return jnp.concatenate(rows, axis=0)


def _unpack_small(pack, like):
    out = {}
    for i, n in enumerate(_SMALL_VECTORS):
        out[n] = pack[i, :like[n].size].reshape(like[n].shape)
    out["pool_w"] = pack[16:].reshape(like["pool_w"].shape)
    return out


_WEIGHTS = ("ffn1_norm", "ffn1_w_gate", "ffn1_w_up", "ffn1_w_down", "mix_norm", "w_in", "q_norm", "w_q_up",
            "kv_norm", "w_kv_up", "pool_w", "pool_scale", "w_out", "xattn_norm", "mem_norm", "w_mq", "w_mkv",
            "w_mo", "ffn2_norm", "ffn2_w_gate", "ffn2_w_up", "ffn2_w_down", "final_norm")
_SHARDED = ("ffn1_w_gate", "ffn1_w_up", "ffn1_w_down", "w_in", "w_q_up", "w_kv_up", "w_out", "w_mq", "w_mkv",
            "w_mo", "ffn2_w_gate", "ffn2_w_up", "ffn2_w_down")
_RS_GROUPS = (("ffn2_w_gate", "ffn2_w_up", "ffn2_w_down"),
              ("w_mo", "w_mq", "w_mkv", "w_out", "w_q_up", "w_kv_up", "w_in"),
              ("ffn1_w_gate", "ffn1_w_up", "ffn1_w_down"))
W_IN_SPLIT = Q_LORA + KV_LORA + ROPE_DIM


_FFN1 = ("ffn1_w_gate", "ffn1_w_up", "ffn1_w_down")
_TRANSPOSED = ("ffn1_w_gate", "ffn1_w_up", "ffn2_w_gate", "ffn2_w_up", "w_in", "w_q_up")


def _local_view(name, a):
    return jnp.swapaxes(a, 1, 2)[0] if name in _TRANSPOSED else a[0]


def _global_view(name, a):
    return jnp.swapaxes(a[None], 1, 2) if name in _TRANSPOSED else a[None]


def _pad_shard(name, a):
    if name == "w_in":
        return jnp.concatenate([a[:W_IN_SPLIT], jnp.zeros((64, a.shape[1]), a.dtype), a[W_IN_SPLIT:]], axis=0)
    if name == "w_q_up":
        return jnp.pad(a, ((0, 64), (0, 0)))
    return a


def _unpad_shard(name, a):
    if name == "w_in":
        return jnp.concatenate([a[:, :W_IN_SPLIT], a[:, W_IN_SPLIT + 64:]], axis=1)
    if name == "w_q_up":
        return a[:, :192]
    return a


def _stacked(g):
    return g if g.ndim == 3 else g.reshape(N_CHIPS, g.shape[0] // N_CHIPS, g.shape[1])


class _Plan:
    AG_UNITS = (
        (("w_in", "w_q_up", "w_kv_up"), "ffn1_up"),
        (("w_out",), "w_in"),
        (("w_mq",), "qkv_prep"),
        (("w_mkv", "w_mo", "ffn2_w_gate"), "mla_fwd"),
        (("ffn2_w_up",), "xattn_fwd"),
        (("ffn2_w_down",), "ffn2_up"),
    )
    RS_UNITS = (
        (("ffn2_w_gate", "ffn2_w_up", "ffn2_w_down"), "ffn2_dn_a", "mla_bwd", "qkv_prep_bwd"),
        (("w_mo", "w_mq", "w_mkv"), "w_out_dx", "mla_bwd", "qkv_prep_bwd"),
        (("w_out", "w_q_up", "w_kv_up", "w_in"), "w_in_dx", "ffn1_dact", "ffn1_dwd"),
        (("ffn1_w_down",), "ffn1_dwg", "ffn1_dwu", "ffn1_dn_a"),
        (("ffn1_w_gate",), "ffn1_dwu", "ffn1_dn_a", "ffn1_dn_b"),
        (("ffn1_w_up",), "ffn1_dn_a", "ffn1_dn_b", "adamw_w_kv_up"),
    )
    ADAMW_ORDER = ("w_kv_up", "ffn2_w_gate", "ffn2_w_up", "ffn2_w_down", "w_mo", "w_mq", "w_mkv", "w_out", "w_q_up",
                   "w_in", "ffn1_w_down", "ffn1_w_gate", "ffn1_w_up")

    def __init__(self, shards, w, grads, core):
        self.shards, self.w, self.grads, self.core = shards, w, grads, core
        self.last_slab_step = 0
        self.parts = {}
        self.ag = [None for _ in self.AG_UNITS]
        self.rs = [[None, None, None, None] for _ in self.RS_UNITS]

    def pre(self, name):
        for i, (names, host) in enumerate(self.AG_UNITS):
            if name == host:
                st = _ag_ici_stage([self.shards[n] for n in names])
                st.then = _ag_d2d_stage(st.outs)
                st.start_step = self.last_slab_step if name == "ffn1_up" else 0
                self.ag[i] = _host(name, st)
        for i, (names, h1, h2, h3) in enumerate(self.RS_UNITS):
            if name == h1:
                self.rs[i][0] = _host(name, _rs_swap_stage([_stacked(self.grads[n]) for n in names]))
            if name == h2:
                self.rs[i][2] = _host(name, _rs_scatter_stage(self.rs[i][1], relative=names[0] in _FFN1))
            if name == h3:
                self.rs[i][3] = _host(name, _rs_mirror_stage(self.rs[i][2].results))

    def post(self, name):
        for i, (names, host) in enumerate(self.AG_UNITS):
            if name == host:
                for n, f in zip(names, self.ag[i].results):
                    self.w[n] = _full_weight(n, f)
        for i, (names, h1, h2, h3) in enumerate(self.RS_UNITS):
            if name == h1:
                self.rs[i][1] = [_pair_add("pair_add_" + n, _stacked(self.grads[n]), r1, self.core)
                                 for n, r1 in zip(names, self.rs[i][0].results)]
            if name == h3:
                for n, p in zip(names, self.rs[i][3].results):
                    self.parts[n] = p


def _full_weight(name, stacked):
    if name in ("w_out", "w_mq", "w_mo"):
        return stacked.reshape(D_MODEL, D_MODEL)
    return stacked


def kernel(x, mem, positions, ffn1_norm, ffn1_w_gate, ffn1_w_up, ffn1_w_down, mix_norm, w_in, q_norm, w_q_up, kv_norm, w_kv_up, pool_w, pool_scale, w_out, xattn_norm, mem_norm, w_mq, w_mkv, w_mo, ffn2_norm, ffn2_w_gate, ffn2_w_up, ffn2_w_down, final_norm, loss_target, m_ffn1_norm, m_ffn1_w_gate, m_ffn1_w_up, m_ffn1_w_down, m_mix_norm, m_w_in, m_q_norm, m_w_q_up, m_kv_norm, m_w_kv_up, m_pool_w, m_pool_scale, m_w_out, m_xattn_norm, m_mem_norm, m_w_mq, m_w_mkv, m_w_mo, m_ffn2_norm, m_ffn2_w_gate, m_ffn2_w_up, m_ffn2_w_down, m_final_norm, v_ffn1_norm, v_ffn1_w_gate, v_ffn1_w_up, v_ffn1_w_down, v_mix_norm, v_w_in, v_q_norm, v_w_q_up, v_kv_norm, v_w_kv_up, v_pool_w, v_pool_scale, v_w_out, v_xattn_norm, v_mem_norm, v_w_mq, v_w_mkv, v_w_mo, v_ffn2_norm, v_ffn2_w_gate, v_ffn2_w_up, v_ffn2_w_down, v_final_norm):
    wts = dict(zip(_WEIGHTS, (ffn1_norm, ffn1_w_gate, ffn1_w_up, ffn1_w_down, mix_norm, w_in, q_norm, w_q_up, kv_norm, w_kv_up, pool_w, pool_scale, w_out, xattn_norm, mem_norm, w_mq, w_mkv, w_mo, ffn2_norm, ffn2_w_gate, ffn2_w_up, ffn2_w_down, final_norm)))
    mom = dict(zip(_WEIGHTS, (m_ffn1_norm, m_ffn1_w_gate, m_ffn1_w_up, m_ffn1_w_down, m_mix_norm, m_w_in, m_q_norm, m_w_q_up, m_kv_norm, m_w_kv_up, m_pool_w, m_pool_scale, m_w_out, m_xattn_norm, m_mem_norm, m_w_mq, m_w_mkv, m_w_mo, m_ffn2_norm, m_ffn2_w_gate, m_ffn2_w_up, m_ffn2_w_down, m_final_norm)))
    var = dict(zip(_WEIGHTS, (v_ffn1_norm, v_ffn1_w_gate, v_ffn1_w_up, v_ffn1_w_down, v_mix_norm, v_w_in, v_q_norm, v_w_q_up, v_kv_norm, v_w_kv_up, v_pool_w, v_pool_scale, v_w_out, v_xattn_norm, v_mem_norm, v_w_mq, v_w_mkv, v_w_mo, v_ffn2_norm, v_ffn2_w_gate, v_ffn2_w_up, v_ffn2_w_down, v_final_norm)))
    small = [n for n in _WEIGHTS if n not in _SHARDED]

    global _PLAN
    shards = {n: _pad_shard(n, _local_view(n, wts[n])).astype(BF16) for n in _SHARDED}
    w = {n: wts[n].reshape(1, -1) for n in _SMALL_VECTORS}
    w["pool_w"] = pool_w[0].astype(BF16)
    grads = {}
    core = lax.axis_index("c").astype(jnp.int32).reshape(1)
    plan = _Plan(shards, w, grads, core)
    _PLAN = plan
    try:
        w["ffn1_shards"] = tuple(shards[n] for n in _FFN1)

        loss_local, dx = _local_step(x[0], mem[0], positions[0], loss_target[0], w, grads)

        gpack, dpack, mpack, vpack = _small_allreduce_adamw(
            _pack_small({n: grads[n] for n in small}, loss_local), _pack_small({n: wts[n] for n in small}),
            _pack_small({n: mom[n] for n in small}), _pack_small({n: var[n] for n in small}))
        like = {n: wts[n] for n in small}
        g_out, d_out, m_out, v_out = (_unpack_small(p, like) for p in (gpack, dpack, mpack, vpack))
        loss = gpack[_LOSS_ROW, 0]

        for n in _Plan.ADAMW_ORDER:
            res = _adamw_sum("adamw_" + n, _local_view(n, wts[n]), _unpad_shard(n, plan.parts[n]),
                             _local_view(n, mom[n]), _local_view(n, var[n]))
            g_out[n], d_out[n], m_out[n], v_out[n] = (_global_view(n, r) for r in res)
    finally:
        _PLAN = None
        _PENDING.clear()

    return (loss, dx[None], *[g_out[n] for n in _WEIGHTS], *[d_out[n] for n in _WEIGHTS],
            *[m_out[n] for n in _WEIGHTS], *[v_out[n] for n in _WEIGHTS])
```

```python
import functools

import jax
import jax.numpy as jnp
from jax import lax
from jax.experimental import pallas as pl
from jax.experimental.pallas import tpu as pltpu

F32 = jnp.float32
BF16 = jnp.bfloat16

D_MODEL = 1024
D_FF = 2816
N_CHIPS = 4
FF_SHARD = D_FF // N_CHIPS
MLA_HEADS = 4
Q_LORA = 256
KV_LORA = 128
ROPE_DIM = 64
HEAD_QK = 256
HEAD_V = 128
POOL_GROUPS = 4
POOL_CH = 128
MEM_HEADS = 4
MEM_HEAD_DIM = 256
RMS_EPS = 1e-6
ROPE_BASE = 10000.0
MLA_SCALE = (128 + 64) ** -0.5
MEM_SCALE = MEM_HEAD_DIM ** -0.5

ADAM_LR = 0.001
ADAM_B1 = 0.9
ADAM_B2 = 0.999
ADAM_EPS = 1e-08
ADAM_WD = 0.01
ADAM_STEP = 10

V7X_VMEM_LIMIT_BYTES = 56 * 1024 * 1024

NN = ((1,), (0,))
NT = ((1,), (1,))
TN = ((0,), (0,))


def _params(*sem):
    return pltpu.CompilerParams(dimension_semantics=sem, vmem_limit_bytes=V7X_VMEM_LIMIT_BYTES)


_MESH = pl.DeviceIdType.MESH
_ANY = pl.BlockSpec(memory_space=pl.ANY)


class _Stage:
    def __init__(self, ins, outs, n_remote, n_local, copies, aliases=None):
        self.ins, self.outs, self.n_remote, self.n_local = list(ins), list(outs), n_remote, n_local
        self.copies, self.aliases = copies, dict(aliases or {})
        self.results = None
        self.start_step = 0
        self.then = None

    def descriptors(self, in_refs, out_refs, send, recv, loc):
        ds, ri, li = [], 0, 0
        for src, dst, dev in self.copies(in_refs, out_refs):
            if dev is None:
                ds.append(pltpu.make_async_copy(src, dst, loc.at[li]))
                li += 1
            else:
                ds.append(pltpu.make_async_remote_copy(src_ref=src, dst_ref=dst, send_sem=send.at[ri],
                                                       recv_sem=recv.at[ri], device_id=dev, device_id_type=_MESH))
                ri += 1
        assert ri == self.n_remote and li == self.n_local
        return ds


_PENDING = {}


def _host(name, stage):
    _PENDING.setdefault(name, []).append(stage)
    return stage


_PLAN = None


def _call(body, **kw):
    if _PLAN is not None:
        _PLAN.pre(kw["name"])
    res = _call_hosting(body, **kw)
    if _PLAN is not None:
        _PLAN.post(kw["name"])
    return res


def _call_hosting(body, *, name, grid, in_specs, out_specs, out_shape, sem, args, scratch_shapes=(), aliases=None):
    stages = _PENDING.pop(name, [])
    scratch_shapes = list(scratch_shapes)
    if not stages:
        return pl.pallas_call(body, name=name, grid=grid, in_specs=in_specs, out_specs=out_specs,
                              out_shape=out_shape, scratch_shapes=scratch_shapes,
                              input_output_aliases=dict(aliases or {}), compiler_params=_params(*sem))(*args)
    ni, no, ns = len(in_specs), len(out_shape), len(scratch_shapes)
    c_ins = [a for st in stages for a in st.ins]
    c_outs = [o for st in stages for o in st.outs]
    nci, nco = len(c_ins), len(c_outs)
    aliases, io, oo = dict(aliases or {}), 0, 0
    for st in stages:
        for i, j in st.aliases.items():
            aliases[ni + io + i] = no + oo + j
        io += len(st.ins)
        oo += len(st.outs)
    dma = pltpu.SemaphoreType.DMA
    sems = []
    for st in stages:
        sems += [dma((max(st.n_remote, 1),)), dma((max(st.n_remote, 1),)), dma((max(st.n_local, 1),))]
    followers = [st.then for st in stages if st.then is not None]
    for st in followers:
        sems += [dma((max(st.n_remote, 1),)), dma((max(st.n_remote, 1),)), dma((max(st.n_local, 1),))]

    def wrapped(*refs):
        ins, cin = refs[:ni], refs[ni:ni + nci]
        outs, cout = refs[ni + nci:ni + nci + no], refs[ni + nci + no:ni + nci + no + nco]
        scr = refs[ni + nci + no + nco:ni + nci + no + nco + ns]
        sem_refs = refs[ni + nci + no + nco + ns:]
        step = pl.program_id(0)
        last = pl.program_id(0) == grid[0] - 1
        for ax in range(1, len(grid)):
            step = step * grid[ax] + pl.program_id(ax)
            last = jnp.logical_and(last, pl.program_id(ax) == grid[ax] - 1)

        def descriptors(si):
            io = sum(len(st.ins) for st in stages[:si])
            oo = sum(len(st.outs) for st in stages[:si])
            st = stages[si]
            return st.descriptors(cin[io:io + len(st.ins)], cout[oo:oo + len(st.outs)], *sem_refs[3 * si:3 * si + 3])

        def follower_descriptors(fi):
            si = [k for k, st in enumerate(stages) if st.then is not None][fi]
            oo = sum(len(st.outs) for st in stages[:si])
            bufs = cout[oo:oo + len(stages[si].outs)]
            k0 = 3 * (len(stages) + fi)
            return followers[fi].descriptors(bufs, bufs, *sem_refs[k0:k0 + 3])

        def start(si):
            @pl.when(step == stages[si].start_step)
            def _():
                for d in descriptors(si):
                    d.start()

        for si, st in enumerate(stages):
            if st.start_step == 0:
                start(si)
        body(*ins, *outs, *scr)
        for si, st in enumerate(stages):
            if st.start_step != 0:
                start(si)

        @pl.when(last)
        def _():
            for si in range(len(stages)):
                for d in descriptors(si):
                    d.wait()
            for fi in range(len(followers)):
                for d in follower_descriptors(fi):
                    d.start()
            for fi in range(len(followers)):
                for d in follower_descriptors(fi):
                    d.wait()

    res = pl.pallas_call(
        wrapped, name=name, grid=grid, in_specs=list(in_specs) + [_ANY] * nci,
        out_specs=list(out_specs) + [_ANY] * nco, out_shape=list(out_shape) + c_outs,
        scratch_shapes=scratch_shapes + sems, input_output_aliases=aliases,
        compiler_params=_params(*(("arbitrary",) * len(grid))))(*args, *c_ins)
    oo = no
    for st in stages:
        st.results = list(res[oo:oo + len(st.outs)])
        oo += len(st.outs)
    return list(res[:no])


def _dot(a, b, dims):
    return lax.dot_general(a.astype(BF16), b.astype(BF16), (dims, ((), ())), preferred_element_type=F32)


_MAX_ROW_BLOCK = 1024
_ATT_BLOCK = 512


_MAX_REDUCE_BLOCK = 2048


def _row_block(s, want=1024):
    return min(want, s, _MAX_ROW_BLOCK)


def _reduce_block(s):
    return min(s, _MAX_REDUCE_BLOCK)


def _matmul(name, grid, terms, extras, outs, epilogue, acc_shape, fill=(), summed=()):
    nt, ne, no, nf = len(terms), len(extras), len(outs), len(fill)
    nk = grid[-1]
    dims = [t[4] for t in terms]

    def body(*refs):
        a_refs, b_refs = refs[:nt], refs[nt:2 * nt]
        e_refs = refs[2 * nt:2 * nt + ne]
        o_refs = refs[2 * nt + ne + nf:2 * nt + ne + nf + no]

        def finish(acc):
            vals = epilogue(acc, *[e[...] for e in e_refs])
            for idx, (o, val) in enumerate(zip(o_refs, vals)):
                if idx in summed:
                    @pl.when(pl.program_id(0) == 0)
                    def _(o=o, val=val):
                        o[...] = val.astype(o.dtype)

                    @pl.when(pl.program_id(0) > 0)
                    def _(o=o, val=val):
                        o[...] += val.astype(o.dtype)
                else:
                    o[...] = val.astype(o.dtype)

        if nk == 1:
            part = None
            for a, b, d in zip(a_refs, b_refs, dims):
                t = _dot(a[...], b[...], d)
                part = t if part is None else part + t
            finish(part)
        else:
            acc_ref = refs[-1]
            k = pl.program_id(len(grid) - 1)

            @pl.when(k == 0)
            def _():
                acc_ref[...] = jnp.zeros_like(acc_ref)

            for a, b, d in zip(a_refs, b_refs, dims):
                acc_ref[...] += _dot(a[...], b[...], d)

            @pl.when(k == nk - 1)
            def _():
                finish(acc_ref[...])

    in_specs = [t[1] for t in terms] + [t[3] for t in terms] + [e[1] for e in extras] + [_ANY] * nf
    args = [t[0] for t in terms] + [t[2] for t in terms] + [e[0] for e in extras] + list(fill)
    sem = ("arbitrary" if summed else "parallel",) * (len(grid) - 1) + ("arbitrary",)
    aliases = {2 * nt + ne + i: i for i in range(nf)}
    return _call(
        body, name=name, grid=grid, in_specs=in_specs,
        out_specs=[o[1] for o in outs], out_shape=[o[0] for o in outs],
        scratch_shapes=[pltpu.VMEM(acc_shape, F32)] if nk > 1 else [], sem=sem, args=args, aliases=aliases)


def _ident(acc):
    return (acc,)


def _rmsnorm_fwd(name, x, gain, width, col_block=0):
    s = x.shape[0]
    bm = _row_block(s)

    def body(x_ref, g_ref, o_ref):
        xf = x_ref[...]
        r = lax.rsqrt(jnp.mean(xf * xf, axis=-1, keepdims=True) + RMS_EPS)
        o_ref[...] = ((xf * r) * g_ref[...]).astype(o_ref.dtype)

    return pl.pallas_call(
        body, name=name, grid=(s // bm,),
        in_specs=[pl.BlockSpec((bm, width), lambda i: (i, col_block)), pl.BlockSpec((1, width), lambda i: (0, 0))],
        out_specs=pl.BlockSpec((bm, width), lambda i: (i, 0)),
        out_shape=jax.ShapeDtypeStruct((s, width), BF16),
        compiler_params=_params("parallel"),
    )(x, gain)


def _rms_bwd_math(dy, xf, g, width):
    r = lax.rsqrt(jnp.mean(xf * xf, axis=-1, keepdims=True) + RMS_EPS)
    dyg = dy * g
    dot = jnp.sum(dyg * xf, axis=-1, keepdims=True)
    dx = r * dyg - xf * ((r * r * r) * (dot * (1.0 / width)))
    dgain = jnp.sum(dy * (xf * r), axis=0, keepdims=True)
    return dx, dgain


def _rmsnorm_bwd(name, dy, x, gain, width, col_block=0, dres=None, out_dtype=F32):
    s = x.shape[0]
    bm = _row_block(s)
    has_res = dres is not None

    def body(*refs):
        if has_res:
            dy_ref, x_ref, g_ref, r_ref, dx_ref, dg_ref, dxb_ref = refs
        else:
            dy_ref, x_ref, g_ref, dx_ref, dg_ref = refs
        dx, dgain = _rms_bwd_math(dy_ref[...].astype(F32), x_ref[...], g_ref[...], width)
        if has_res:
            dx = dx + r_ref[...]
            dxb_ref[...] = dx.astype(BF16)
        dx_ref[...] = dx.astype(dx_ref.dtype)

        @pl.when(pl.program_id(0) == 0)
        def _():
            dg_ref[...] = dgain

        @pl.when(pl.program_id(0) > 0)
        def _():
            dg_ref[...] += dgain

    row = pl.BlockSpec((bm, width), lambda i: (i, 0))
    in_specs = [row, pl.BlockSpec((bm, width), lambda i: (i, col_block)), pl.BlockSpec((1, width), lambda i: (0, 0))]
    args = [dy, x, gain]
    out_specs = [row, pl.BlockSpec((1, width), lambda i: (0, 0))]
    out_shape = [jax.ShapeDtypeStruct((s, width), out_dtype), jax.ShapeDtypeStruct((1, width), F32)]
    if has_res:
        in_specs.append(row)
        args.append(dres)
        out_specs.append(row)
        out_shape.append(jax.ShapeDtypeStruct((s, width), BF16))
    return _call(body, name=name, grid=(s // bm,), in_specs=in_specs, out_specs=out_specs, out_shape=out_shape,
                 sem=("arbitrary",), args=args)


def _loss_and_final_norm(h, gain, target):
    s, d = h.shape
    bm = _row_block(s, 512)

    def body(h_ref, g_ref, t_ref, dh_ref, dhb_ref, loss_ref, dg_ref):
        xf = h_ref[...]
        g = g_ref[...]
        r = lax.rsqrt(jnp.mean(xf * xf, axis=-1, keepdims=True) + RMS_EPS)
        err = (xf * r) * g - t_ref[...]
        part = 0.5 * jnp.sum(jnp.mean(err * err, axis=-1, keepdims=True), axis=0, keepdims=True)
        dx, dgain = _rms_bwd_math(err * (1.0 / d), xf, g, d)
        dh_ref[...] = dx
        dhb_ref[...] = dx.astype(BF16)

        @pl.when(pl.program_id(0) == 0)
        def _():
            dg_ref[...] = dgain
            loss_ref[...] = jnp.broadcast_to(part, loss_ref.shape)

        @pl.when(pl.program_id(0) > 0)
        def _():
            dg_ref[...] += dgain
            loss_ref[...] += jnp.broadcast_to(part, loss_ref.shape)

    row = pl.BlockSpec((bm, d), lambda i: (i, 0))
    vec = pl.BlockSpec((1, d), lambda i: (0, 0))
    return pl.pallas_call(
        body, name="loss_final_norm", grid=(s // bm,), in_specs=[row, vec, row],
        out_specs=[row, row, pl.BlockSpec((1, 128), lambda i: (0, 0)), vec],
        out_shape=[jax.ShapeDtypeStruct((s, d), F32), jax.ShapeDtypeStruct((s, d), BF16),
                   jax.ShapeDtypeStruct((1, 128), F32),
                   jax.ShapeDtypeStruct((1, d), F32)],
        compiler_params=_params("arbitrary"),
    )(h, gain, target)


def _ffn_up(name, n, wg, wu):
    s = n.shape[0]
    bm = _row_block(s)

    def body(n_ref, wg_ref, wu_ref, a_ref, dadu_ref, dadg_ref):
        x = n_ref[...]
        g = _dot(x, wg_ref[...], NT)
        u = _dot(x, wu_ref[...], NT)
        sg = jax.nn.sigmoid(g)
        silu = g * sg
        a_ref[...] = (silu * u).astype(BF16)
        dadu_ref[...] = silu.astype(BF16)
        dadg_ref[...] = (u * (sg * (1.0 + g * (1.0 - sg)))).astype(BF16)

    w_spec = pl.BlockSpec((None, FF_SHARD, D_MODEL), lambda j, i: (j, 0, 0))
    o_spec = pl.BlockSpec((None, bm, FF_SHARD), lambda j, i: (j, i, 0))
    shp = jax.ShapeDtypeStruct((N_CHIPS, s, FF_SHARD), BF16)
    return _call(
        body, name=name, grid=(N_CHIPS, s // bm),
        in_specs=[pl.BlockSpec((bm, D_MODEL), lambda j, i: (i, 0)), w_spec, w_spec],
        out_specs=[o_spec, o_spec, o_spec], out_shape=[shp, shp, shp],
        sem=("parallel", "parallel"), args=[n, wg, wu])


def _ffn1_up_gather_direct(n, g_sh, u_sh, d_sh):
    s = n.shape[0]
    bm = _row_block(s)
    nrb = s // bm
    rows, cols = g_sh.shape
    rels = (1, 2, 3)

    def body(n_ref, gs, us, ds, a_ref, dadu_ref, dadg_ref, wg, wu, wd, gbuf, ubuf, send, recv, fsend, frecv, loc, ld):
        r, i = pl.program_id(0), pl.program_id(1)
        x, y, c = lax.axis_index("x"), lax.axis_index("y"), lax.axis_index("c")
        sib = (x, y, 1 - c)
        mine, _ = _half_rows(c, rows)
        shards, fulls, bufs = (gs, us, ds), (wg, wu, wd), (gbuf, ubuf)

        def ici(k, rel, dev=sib):
            return pltpu.make_async_remote_copy(
                src_ref=shards[k].at[mine], dst_ref=fulls[k].at[rel, mine], send_sem=send.at[k, rel - 1],
                recv_sem=recv.at[k, rel - 1], device_id=dev, device_id_type=_MESH)

        def peer(rel):
            return ((1 - x) if rel & 2 else x, (1 - y) if rel & 1 else y, c)

        def fwd(k, rel):
            return pltpu.make_async_remote_copy(
                src_ref=fulls[k].at[rel, mine], dst_ref=fulls[k].at[rel, mine], send_sem=fsend.at[k, rel - 1],
                recv_sem=frecv.at[k, rel - 1], device_id=sib, device_id_type=_MESH)

        def own(k):
            return pltpu.make_async_copy(shards[k], fulls[k].at[0], loc.at[k])

        def load(k, src):
            return pltpu.make_async_copy(src, bufs[k], ld.at[k])

        @pl.when(jnp.logical_and(r == 0, i == 0))
        def _():
            for k in range(3):
                own(k).start()
            for rel in (1, 2):
                for k in (0, 1):
                    ici(k, rel, peer(rel)).start()
            for k in (0, 1):
                load(k, shards[k]).start()
            for k in (0, 1):
                load(k, shards[k]).wait()

        @pl.when(jnp.logical_and(r > 0, i == 0))
        def _():
            for k in (0, 1):
                ici(k, r).wait_recv()
                fwd(k, r).start()
            for k in (0, 1):
                fwd(k, r).wait_recv()
                load(k, fulls[k].at[r]).start()
            for k in (0, 1):
                load(k, fulls[k].at[r]).wait()

        @pl.when(jnp.logical_and(r == 1, i == 0))
        def _():
            for k in (0, 1):
                ici(k, 3, peer(3)).start()

        @pl.when(jnp.logical_and(r == 2, i == 0))
        def _():
            for rel in (1, 2):
                ici(2, rel, peer(rel)).start()

        @pl.when(jnp.logical_and(r == 3, i == 0))
        def _():
            ici(2, 3, peer(3)).start()

        xv = n_ref[...]
        g = _dot(xv, gbuf[...], NT)
        u = _dot(xv, ubuf[...], NT)
        sg = jax.nn.sigmoid(g)
        silu = g * sg
        a_ref[...] = (silu * u).astype(BF16)
        dadu_ref[...] = silu.astype(BF16)
        dadg_ref[...] = (u * (sg * (1.0 + g * (1.0 - sg)))).astype(BF16)

        @pl.when(jnp.logical_and(r == 3, i == nrb - 1))
        def _():
            for rel in rels:
                ici(2, rel).wait_recv()
                fwd(2, rel).start()
            for rel in rels:
                fwd(2, rel).wait_recv()
            for k in range(3):
                for rel in rels:
                    ici(k, rel).wait_send()
                    fwd(k, rel).wait_send()
                own(k).wait()

    o_spec = pl.BlockSpec((None, bm, FF_SHARD), lambda r, i: (r, i, 0))
    act = jax.ShapeDtypeStruct((N_CHIPS, s, FF_SHARD), BF16)
    full = jax.ShapeDtypeStruct((N_CHIPS, rows, cols), BF16)
    dma = pltpu.SemaphoreType.DMA
    if _PLAN is not None:
        _PLAN.last_slab_step = 3 * nrb
    return _call(
        body, name="ffn1_up", grid=(N_CHIPS, nrb),
        in_specs=[pl.BlockSpec((bm, D_MODEL), lambda r, i: (i, 0)), _ANY, _ANY, _ANY],
        out_specs=[o_spec, o_spec, o_spec, _ANY, _ANY, _ANY], out_shape=[act, act, act, full, full, full],
        scratch_shapes=[pltpu.VMEM((rows, cols), BF16), pltpu.VMEM((rows, cols), BF16), dma((3, 3)), dma((3, 3)),
                        dma((3, 3)), dma((3, 3)), dma((3,)), dma((2,))],
        sem=("arbitrary", "arbitrary"), args=[n, g_sh, u_sh, d_sh])


def _ffn1_up_gather(xin, gain, g_sh, u_sh, d_sh):
    s = xin.shape[0]
    bm = _row_block(s)
    nrb = s // bm
    rows, cols = g_sh.shape

    def body(x_ref, gain_ref, gs, us, ds, n_ref, a_ref, dadu_ref, dadg_ref, wg, wu, wd, gbuf, ubuf,
             send, recv, qsend, qrecv, fsend, frecv, loc, ld):
        r, i = pl.program_id(0), pl.program_id(1)
        x, y, c = lax.axis_index("x"), lax.axis_index("y"), lax.axis_index("c")
        sib = (x, y, 1 - c)
        mine, _ = _half_rows(c, rows)
        quarters = _quarter_rows(c, rows)
        shards, fulls, bufs = (gs, us, ds), (wg, wu, wd), (gbuf, ubuf)

        def remote(src, dst, ssem, rsem, dev):
            return pltpu.make_async_remote_copy(src_ref=src, dst_ref=dst, send_sem=ssem, recv_sem=rsem,
                                                device_id=dev, device_id_type=_MESH)

        def peer(rel):
            return ((1 - x) if rel & 2 else x, (1 - y) if rel & 1 else y, c)

        def ici(k, rel, dev=sib):
            return remote(shards[k].at[mine], fulls[k].at[rel, mine], send.at[k, rel - 1], recv.at[k, rel - 1], dev)

        def quarter(k, which, dev=sib):
            slab, q = ((2, quarters[0]), (1, quarters[1]))[which]
            return remote(fulls[k].at[slab, q], fulls[k].at[3, q], qsend.at[k, which], qrecv.at[k, which], dev)

        def fwd(k, rel):
            return remote(fulls[k].at[rel, mine], fulls[k].at[rel, mine], fsend.at[k, rel - 1], frecv.at[k, rel - 1], sib)

        def own(k):
            return pltpu.make_async_copy(shards[k], fulls[k].at[0], loc.at[k])

        def load(slab):
            for k in (0, 1):
                pltpu.make_async_copy(shards[k] if slab == 0 else fulls[k].at[slab], bufs[k], ld.at[k]).start()
            for k in (0, 1):
                pltpu.make_async_copy(shards[k] if slab == 0 else fulls[k].at[slab], bufs[k], ld.at[k]).wait()

        def from_neighbour(ks, rel):
            for k in ks:
                ici(k, rel).wait_recv()
                fwd(k, rel).start()
                quarter(k, 0 if rel == 2 else 1, peer(1 if rel == 2 else 2)).start()
            for k in ks:
                fwd(k, rel).wait_recv()

        def from_diagonal(ks):
            for k in ks:
                quarter(k, 0).wait_recv()
                quarter(k, 1).wait_recv()
                fwd(k, 3).start()
            for k in ks:
                fwd(k, 3).wait_recv()

        @pl.when(jnp.logical_and(r == 0, i == 0))
        def _():
            for k in range(3):
                own(k).start()
            for rel in (1, 2):
                for k in (0, 1):
                    ici(k, rel, peer(rel)).start()
            load(0)

        @pl.when(jnp.logical_and(r == 1, i == 0))
        def _():
            from_neighbour((0, 1), 1)
            load(1)
            for rel in (1, 2):
                ici(2, rel, peer(rel)).start()

        @pl.when(jnp.logical_and(r == 2, i == 0))
        def _():
            from_neighbour((0, 1), 2)
            load(2)

        @pl.when(jnp.logical_and(r == 3, i == 0))
        def _():
            from_diagonal((0, 1))
            load(3)

        xv = _norm_bf16(x_ref[...], gain_ref[...])

        @pl.when(r == 0)
        def _():
            n_ref[...] = xv

        g = _dot(xv, gbuf[...], NT)
        u = _dot(xv, ubuf[...], NT)
        sg = jax.nn.sigmoid(g)
        silu = g * sg
        a_ref[...] = (silu * u).astype(BF16)
        dadu_ref[...] = silu.astype(BF16)
        dadg_ref[...] = (u * (sg * (1.0 + g * (1.0 - sg)))).astype(BF16)

        @pl.when(jnp.logical_and(r == 3, i == nrb - 1))
        def _():
            from_neighbour((2,), 1)
            from_neighbour((2,), 2)
            from_diagonal((2,))
            for k in range(3):
                for rel in (1, 2):
                    ici(k, rel).wait_send()
                for which in (0, 1):
                    quarter(k, which).wait_send()
                for rel in (1, 2, 3):
                    fwd(k, rel).wait_send()
                own(k).wait()

    o_spec = pl.BlockSpec((None, bm, FF_SHARD), lambda r, i: (r, i, 0))
    act = jax.ShapeDtypeStruct((N_CHIPS, s, FF_SHARD), BF16)
    full = jax.ShapeDtypeStruct((N_CHIPS, rows, cols), BF16)
    dma = pltpu.SemaphoreType.DMA
    if _PLAN is not None:
        _PLAN.last_slab_step = 3 * nrb
    n_spec = pl.BlockSpec((bm, D_MODEL), lambda r, i: (jnp.where(r == 0, i, nrb - 1), 0))
    return _call(
        body, name="ffn1_up", grid=(N_CHIPS, nrb),
        in_specs=[pl.BlockSpec((bm, D_MODEL), lambda r, i: (i, 0)), pl.BlockSpec((1, D_MODEL), lambda r, i: (0, 0)),
                  _ANY, _ANY, _ANY],
        out_specs=[n_spec, o_spec, o_spec, o_spec, _ANY, _ANY, _ANY],
        out_shape=[jax.ShapeDtypeStruct((s, D_MODEL), BF16), act, act, act, full, full, full],
        scratch_shapes=[pltpu.VMEM((rows, cols), BF16), pltpu.VMEM((rows, cols), BF16), dma((3, 2)), dma((3, 2)),
                        dma((3, 2)), dma((3, 2)), dma((3, 3)), dma((3, 3)), dma((3,)), dma((2,))],
        sem=("arbitrary", "arbitrary"), args=[xin, gain, g_sh, u_sh, d_sh])


def _residual_epilogue(alpha, with_norm):
    if not with_norm:
        return lambda acc, r: (r + alpha * acc,)

    def epilogue(acc, r, g):
        h = r + alpha * acc
        rs = lax.rsqrt(jnp.mean(h * h, axis=-1, keepdims=True) + RMS_EPS)
        return h, (h * rs) * g

    return epilogue


def _residual_outs(s, bm, gain):
    row = pl.BlockSpec((bm, D_MODEL), lambda i, k: (i, 0))
    outs = [(jax.ShapeDtypeStruct((s, D_MODEL), F32), row)]
    if gain is None:
        return [], outs
    return [(gain, pl.BlockSpec((1, D_MODEL), lambda i, k: (0, 0)))], outs + [(jax.ShapeDtypeStruct((s, D_MODEL), BF16), row)]


def _loss_epilogue(acc, res, g, target):
    d = acc.shape[-1]
    h = res + 0.5 * acc
    r = lax.rsqrt(jnp.mean(h * h, axis=-1, keepdims=True) + RMS_EPS)
    err = (h * r) * g - target
    part = 0.5 * jnp.sum(jnp.mean(err * err, axis=-1, keepdims=True), axis=0, keepdims=True)
    dx, dgain = _rms_bwd_math(err * (1.0 / d), h, g, d)
    return dx, dx, jnp.broadcast_to(part, (1, 128)), dgain


def _ffn_down(name, a, wd, res, gain=None, loss=None):
    s = a.shape[1]
    bm = _row_block(s, 512)
    row = pl.BlockSpec((bm, D_MODEL), lambda i, k: (i, 0))
    terms = [(a, pl.BlockSpec((None, bm, FF_SHARD), lambda i, k, j=j: (j, i, 0)),
              wd, pl.BlockSpec((None, FF_SHARD, D_MODEL), lambda i, k, j=j: (j, 0, 0)), NN) for j in range(N_CHIPS)]
    if loss is not None:
        vec = pl.BlockSpec((1, D_MODEL), lambda i, k: (0, 0))
        outs = [(jax.ShapeDtypeStruct((s, D_MODEL), F32), row), (jax.ShapeDtypeStruct((s, D_MODEL), BF16), row),
                (jax.ShapeDtypeStruct((1, 128), F32), pl.BlockSpec((1, 128), lambda i, k: (0, 0))),
                (jax.ShapeDtypeStruct((1, D_MODEL), F32), vec)]
        return _matmul(name, (s // bm, 1), terms, [(res, row), (loss[0], vec), (loss[1], row)], outs,
                       _loss_epilogue, None, summed=(2, 3))
    extras, outs = _residual_outs(s, bm, gain)
    res_out = _matmul(name, (s // bm, 1), terms, [(res, row)] + extras, outs,
                      _residual_epilogue(0.5, gain is not None), None)
    return res_out if gain is not None else res_out[0]


def _norm_bwd_epilogue(width):
    def epilogue(acc, h, g, dres):
        dx, dgain = _rms_bwd_math(acc, h, g, width)
        dx = dx + dres
        return dx, dx, dgain

    return epilogue


def _norm_bwd_operands(s, bm, h, gain, dres):
    row = pl.BlockSpec((bm, D_MODEL), lambda i, k: (i, 0))
    vec = pl.BlockSpec((1, D_MODEL), lambda i, k: (0, 0))
    extras = [(h, row), (gain, vec), (dres, row)]
    outs = [(jax.ShapeDtypeStruct((s, D_MODEL), F32), row), (jax.ShapeDtypeStruct((s, D_MODEL), BF16), row),
            (jax.ShapeDtypeStruct((1, D_MODEL), F32), vec)]
    return extras, outs, (2,)


def _ffn_bwd(tag, dh, n, dadg, dadu, a, wg, wu, wd, grads, norm_bwd=None):
    s = dh.shape[0]
    bm = _row_block(s)
    bk = _reduce_block(s)
    nk = s // bk

    def act_bwd(acc, dg_da, du_da):
        da = 0.5 * acc
        return da * dg_da.astype(F32), da * du_da.astype(F32)

    slab = pl.BlockSpec((None, bm, FF_SHARD), lambda j, i, k: (j, i, 0))
    shp = jax.ShapeDtypeStruct((N_CHIPS, s, FF_SHARD), BF16)
    dg, du = _matmul(
        tag + "_dact", (N_CHIPS, s // bm, 1),
        [(dh, pl.BlockSpec((bm, D_MODEL), lambda j, i, k: (i, 0)),
          wd, pl.BlockSpec((None, FF_SHARD, D_MODEL), lambda j, i, k: (j, 0, 0)), NT)],
        [(dadg, slab), (dadu, slab)], [(shp, slab), (shp, slab)], act_bwd, None)

    grads[tag + "_w_down"] = _matmul(
        tag + "_dwd", (N_CHIPS, nk),
        [(a, pl.BlockSpec((None, bk, FF_SHARD), lambda j, k: (j, k, 0)),
          dh, pl.BlockSpec((bk, D_MODEL), lambda j, k: (k, 0)), TN)],
        [], [(jax.ShapeDtypeStruct((N_CHIPS, FF_SHARD, D_MODEL), BF16),
              pl.BlockSpec((None, FF_SHARD, D_MODEL), lambda j, k: (j, 0, 0)))],
        lambda acc: (0.5 * acc,), (FF_SHARD, D_MODEL))[0]

    def dw_up(nm, dact):
        return _matmul(
            nm, (N_CHIPS, nk),
            [(dact, pl.BlockSpec((None, bk, FF_SHARD), lambda j, k: (j, k, 0)),
              n, pl.BlockSpec((bk, D_MODEL), lambda j, k: (k, 0)), TN)],
            [], [(jax.ShapeDtypeStruct((N_CHIPS, FF_SHARD, D_MODEL), BF16),
                  pl.BlockSpec((None, FF_SHARD, D_MODEL), lambda j, k: (j, 0, 0)))],
            _ident, (FF_SHARD, D_MODEL))[0]

    grads[tag + "_w_gate"] = dw_up(tag + "_dwg", dg)
    grads[tag + "_w_up"] = dw_up(tag + "_dwu", du)

    bn = _row_block(s, 512)
    steps = s // bn // 2
    prev, dgain = (), None
    for part, off in (("_dn_a", 0), ("_dn_b", steps)):
        row = pl.BlockSpec((bn, D_MODEL), lambda i, k, off=off: (i + off, 0))
        terms = []
        for j in range(N_CHIPS):
            a_slab = pl.BlockSpec((None, bn, FF_SHARD), lambda i, k, j=j, off=off: (j, i + off, 0))
            w_slab = pl.BlockSpec((None, FF_SHARD, D_MODEL), lambda i, k, j=j: (j, 0, 0))
            terms += [(dg, a_slab, wg, w_slab, NN), (du, a_slab, wu, w_slab, NN)]
        if norm_bwd is None:
            prev = _matmul(tag + part, (steps, 1), terms, [], [(jax.ShapeDtypeStruct((s, D_MODEL), F32), row)],
                           _ident, None, fill=prev)
            continue
        h, gain, dres = norm_bwd
        vec = pl.BlockSpec((1, D_MODEL), lambda i, k: (0, 0))
        res = _matmul(
            tag + part, (steps, 1), terms, [(h, row), (gain, vec), (dres, row)],
            [(jax.ShapeDtypeStruct((s, D_MODEL), F32), row), (jax.ShapeDtypeStruct((s, D_MODEL), BF16), row),
             (jax.ShapeDtypeStruct((1, D_MODEL), F32), vec)],
            _norm_bwd_epilogue(D_MODEL), None, fill=prev, summed=(2,))
        prev = res[:2]
        dgain = res[2] if dgain is None else dgain + res[2]
    return prev[0] if norm_bwd is None else (prev[0], prev[1], dgain)


def _mm_nn(name, a, b, out_dtype, res=None, gain=None):
    s, k = a.shape
    nn = b.shape[1]
    bm = _row_block(s)
    row = pl.BlockSpec((bm, nn), lambda i, kk: (i, 0))
    term = [(a, pl.BlockSpec((bm, k), lambda i, kk: (i, 0)), b, pl.BlockSpec((k, nn), lambda i, kk: (0, 0)), NN)]
    if res is None:
        return _matmul(name, (s // bm, 1), term, [], [(jax.ShapeDtypeStruct((s, nn), out_dtype), row)], _ident, None)[0]
    extras, outs = _residual_outs(s, bm, gain)
    res_out = _matmul(name, (s // bm, 1), term, [(res, row)] + extras, outs,
                      _residual_epilogue(1.0, gain is not None), None)
    return res_out if gain is not None else res_out[0]


def _mm_nt(name, a, b, out_dtype, attn_out=None, nh=0, dv=0):
    s, nn = a.shape
    k = b.shape[0]
    bm = _row_block(s)
    term = [(a, pl.BlockSpec((bm, nn), lambda i, kk: (i, 0)), b, pl.BlockSpec((k, nn), lambda i, kk: (0, 0)), NT)]
    out = (jax.ShapeDtypeStruct((s, k), out_dtype), pl.BlockSpec((bm, k), lambda i, kk: (i, 0)))
    if attn_out is None:
        return _matmul(name, (s // bm, 1), term, [], [out], _ident, None)[0]

    def with_delta(acc, o):
        do = acc.astype(out_dtype).astype(F32)
        cols = [jnp.sum(do[:, h * dv:(h + 1) * dv] * o[:, h * dv:(h + 1) * dv].astype(F32), axis=-1, keepdims=True)
                for h in range(nh)]
        return acc, jnp.stack(cols, axis=0)

    return _matmul(
        name, (s // bm, 1), term, [(attn_out, pl.BlockSpec((bm, nh * dv), lambda i, kk: (i, 0)))],
        [out, (jax.ShapeDtypeStruct((nh, s, 1), F32), pl.BlockSpec((nh, bm, 1), lambda i, kk: (0, i, 0)))],
        with_delta, None)


def _mm_nt_norm_bwd(name, a, b, h, gain, dres):
    s, nn = a.shape
    bm = _row_block(s, 512)
    extras, outs, summed = _norm_bwd_operands(s, bm, h, gain, dres)
    return _matmul(
        name, (s // bm, 1),
        [(a, pl.BlockSpec((bm, nn), lambda i, kk: (i, 0)), b, pl.BlockSpec(b.shape, lambda i, kk: (0, 0)), NT)],
        extras, outs, _norm_bwd_epilogue(D_MODEL), None, summed=summed)


def _w_in_dx_norm_bwd(dz, w_t, h, gain, dres):
    s = dz.shape[0]
    bm = _row_block(s, 512)
    epilogue = _norm_bwd_epilogue(D_MODEL)

    def body(dz_ref, w_ref, h_ref, g_ref, r_ref, dx_ref, dxb_ref, dg_ref):
        dzv = dz_ref[...]
        dn = jnp.concatenate([_dot(dzv, w_ref[j], NN) for j in range(N_CHIPS)], axis=1)
        dx, _, dgain = epilogue(dn, h_ref[...], g_ref[...], r_ref[...])
        dx_ref[...] = dx
        dxb_ref[...] = dx.astype(BF16)

        @pl.when(pl.program_id(0) == 0)
        def _():
            dg_ref[...] = dgain

        @pl.when(pl.program_id(0) > 0)
        def _():
            dg_ref[...] += dgain

    row = pl.BlockSpec((bm, D_MODEL), lambda i: (i, 0))
    vec = pl.BlockSpec((1, D_MODEL), lambda i: (0, 0))
    return _call(
        body, name="w_in_dx", grid=(s // bm,),
        in_specs=[row, pl.BlockSpec(w_t.shape, lambda i: (0, 0, 0)), row, vec, row],
        out_specs=[row, row, vec],
        out_shape=[jax.ShapeDtypeStruct((s, D_MODEL), F32), jax.ShapeDtypeStruct((s, D_MODEL), BF16),
                   jax.ShapeDtypeStruct((1, D_MODEL), F32)],
        sem=("arbitrary",), args=[dz, w_t, h, gain, dres])


def _mm_tn(name, a, b, out_dtype=BF16):
    s, k = a.shape
    nn = b.shape[1]
    bk = _reduce_block(s)
    return _matmul(
        name, (s // bk,),
        [(a, pl.BlockSpec((bk, k), lambda kk: (kk, 0)), b, pl.BlockSpec((bk, nn), lambda kk: (kk, 0)), TN)],
        [], [(jax.ShapeDtypeStruct((k, nn), out_dtype), pl.BlockSpec((k, nn), lambda kk: (0, 0)))],
        _ident, (k, nn))[0]


def _mm_heads_fwd(name, a, w, out_dtype, w_transposed=False):
    s, k = a.shape
    nh = w.shape[0]
    nn = w.shape[1] if w_transposed else w.shape[2]
    bm = _row_block(s)
    return _matmul(
        name, (nh, s // bm, 1),
        [(a, pl.BlockSpec((bm, k), lambda h, i, kk: (i, 0)),
          w, pl.BlockSpec((None,) + w.shape[1:], lambda h, i, kk: (h, 0, 0)), NT if w_transposed else NN)],
        [], [(jax.ShapeDtypeStruct((s, nh * nn), out_dtype), pl.BlockSpec((bm, nn), lambda h, i, kk: (i, h)))],
        _ident, None)[0]


def _mm_heads_bwd(name, dy, a, w, w_transposed=False):
    s, k = a.shape
    nh = w.shape[0]
    nn = w.shape[1] if w_transposed else w.shape[2]
    bm = _row_block(s)
    bk = _reduce_block(s)
    w_spec = pl.BlockSpec((None,) + w.shape[1:], lambda i, h: (h, 0, 0))
    da = _matmul(
        name + "_dx", (s // bm, nh),
        [(dy, pl.BlockSpec((bm, nn), lambda i, h: (i, h)), w, w_spec, NN if w_transposed else NT)],
        [], [(jax.ShapeDtypeStruct((s, k), F32), pl.BlockSpec((bm, k), lambda i, h: (i, 0)))], _ident, (bm, k))[0]
    a_term = (a, pl.BlockSpec((bk, k), lambda h, kk: (kk, 0)))
    dy_term = (dy, pl.BlockSpec((bk, nn), lambda h, kk: (kk, h)))
    lhs, rhs = (dy_term, a_term) if w_transposed else (a_term, dy_term)
    dw = _matmul(
        name + "_dw", (nh, s // bk), [lhs + rhs + (TN,)],
        [], [(jax.ShapeDtypeStruct(w.shape, BF16), pl.BlockSpec((None,) + w.shape[1:], lambda h, kk: (h, 0, 0)))],
        _ident, w.shape[1:])[0]
    return da, dw


def _w_in_fwd(n, w_t):
    s = n.shape[0]
    bm = _row_block(s)
    nh, nout, kin = w_t.shape
    terms = [(n, pl.BlockSpec((bm, kin), lambda i, k, j=j: (i, j)),
              w_t, pl.BlockSpec((None, nout, kin), lambda i, k, j=j: (j, 0, 0)), NT) for j in range(nh)]
    row = pl.BlockSpec((bm, nout), lambda i, k: (i, 0))
    return _matmul("w_in", (s // bm, 1), terms, [], [(jax.ShapeDtypeStruct((s, nout), F32), row)], _ident, None)[0]


def _w_in_dw(dz, n):
    s, nout = dz.shape
    kin = n.shape[1] // N_CHIPS
    bk = _reduce_block(s)
    return _matmul(
        "w_in_dw", (N_CHIPS, s // bk),
        [(dz, pl.BlockSpec((bk, nout), lambda j, k: (k, 0)), n, pl.BlockSpec((bk, kin), lambda j, k: (k, j)), TN)],
        [], [(jax.ShapeDtypeStruct((N_CHIPS, nout, kin), BF16), pl.BlockSpec((None, nout, kin), lambda j, k: (j, 0, 0)))],
        _ident, (nout, kin))[0]


def _rope_tables(positions):
    half = ROPE_DIM // 2
    freqs = 1.0 / (ROPE_BASE ** (jnp.arange(0, ROPE_DIM, 2, dtype=F32) / ROPE_DIM))
    ang = positions.astype(F32)[:, None] * freqs
    cos, sin = jnp.cos(ang), jnp.sin(ang)
    z = jnp.zeros_like(cos)
    tc = jnp.concatenate([cos, cos, z, z], axis=-1)
    ta = jnp.concatenate([-sin, z, z, z], axis=-1)
    tb = jnp.concatenate([z, sin, z, z], axis=-1)
    assert tc.shape[-1] == 4 * half
    return tc, ta, tb


def _rope(x, tc, ta, tb):
    return x * tc + pltpu.roll(x, 96, 1) * ta + pltpu.roll(x, 32, 1) * tb


def _rope_t(dy, tc, ta, tb):
    return dy * tc + pltpu.roll(dy * ta, 32, 1) + pltpu.roll(dy * tb, 96, 1)


def _q_rope(q, tc, ta, tb, transpose):
    s = q.shape[0]
    bm = _row_block(s, 512)
    rot = _rope_t if transpose else _rope

    def body(q_ref, tc_ref, ta_ref, tb_ref, o_ref):
        c, a, b = tc_ref[...], ta_ref[...], tb_ref[...]
        for h in range(MLA_HEADS):
            lo = h * HEAD_QK
            o_ref[:, lo:lo + 128] = q_ref[:, lo:lo + 128].astype(BF16)
            o_ref[:, lo + 128:lo + 256] = rot(q_ref[:, lo + 128:lo + 256], c, a, b).astype(BF16)

    row = pl.BlockSpec((bm, MLA_HEADS * HEAD_QK), lambda i: (i, 0))
    tab = pl.BlockSpec((bm, 128), lambda i: (i, 0))
    return pl.pallas_call(
        body, name="q_rope_t" if transpose else "q_rope", grid=(s // bm,), in_specs=[row, tab, tab, tab],
        out_specs=row, out_shape=jax.ShapeDtypeStruct((s, MLA_HEADS * HEAD_QK), BF16),
        compiler_params=_params("parallel"),
    )(q, tc, ta, tb)


def _kv_assemble(kv, z, tc, ta, tb):
    s = kv.shape[0]
    bm = _row_block(s, 512)

    def body(kv_ref, kr_ref, tc_ref, ta_ref, tb_ref, k_ref, v_ref):
        kpe = _rope(kr_ref[...], tc_ref[...], ta_ref[...], tb_ref[...]).astype(BF16)
        for h in range(MLA_HEADS):
            lo = h * 256
            k_ref[:, lo:lo + 128] = kv_ref[:, lo:lo + 128].astype(BF16)
            k_ref[:, lo + 128:lo + 256] = kpe
            v_ref[:, h * 128:(h + 1) * 128] = kv_ref[:, lo + 128:lo + 256].astype(BF16)

    row = pl.BlockSpec((bm, 1024), lambda i: (i, 0))
    tab = pl.BlockSpec((bm, 128), lambda i: (i, 0))
    return pl.pallas_call(
        body, name="kv_assemble", grid=(s // bm,),
        in_specs=[row, pl.BlockSpec((bm, 128), lambda i: (i, 3)), tab, tab, tab],
        out_specs=[row, pl.BlockSpec((bm, 512), lambda i: (i, 0))],
        out_shape=[jax.ShapeDtypeStruct((s, 1024), BF16), jax.ShapeDtypeStruct((s, 512), BF16)],
        compiler_params=_params("parallel"),
    )(kv, z, tc, ta, tb)


def _kv_assemble_bwd(dk, dv, tc, ta, tb):
    s = dk.shape[0]
    bm = _row_block(s, 512)

    def body(dk_ref, dv_ref, tc_ref, ta_ref, tb_ref, dkv_ref, dkr_ref):
        dpe = None
        for h in range(MLA_HEADS):
            lo = h * 256
            dkv_ref[:, lo:lo + 128] = dk_ref[:, lo:lo + 128].astype(BF16)
            dkv_ref[:, lo + 128:lo + 256] = dv_ref[:, h * 128:(h + 1) * 128].astype(BF16)
            t = dk_ref[:, lo + 128:lo + 256]
            dpe = t if dpe is None else dpe + t
        dkr_ref[...] = _rope_t(dpe, tc_ref[...], ta_ref[...], tb_ref[...])

    row = pl.BlockSpec((bm, 1024), lambda i: (i, 0))
    tab = pl.BlockSpec((bm, 128), lambda i: (i, 0))
    return pl.pallas_call(
        body, name="kv_assemble_bwd", grid=(s // bm,),
        in_specs=[row, pl.BlockSpec((bm, 512), lambda i: (i, 0)), tab, tab, tab],
        out_specs=[row, tab],
        out_shape=[jax.ShapeDtypeStruct((s, 1024), BF16), jax.ShapeDtypeStruct((s, 128), F32)],
        compiler_params=_params("parallel"),
    )(dk, dv, tc, ta, tb)


def _norm_bf16(x, g):
    r = lax.rsqrt(jnp.mean(x * x, axis=-1, keepdims=True) + RMS_EPS)
    return ((x * r) * g).astype(BF16)


def _qkv_prep(z, q_gain, kv_gain, wq_t, wkv, tc, ta, tb):
    s = z.shape[0]
    bm = _row_block(s, 512)

    def body(zq_ref, zkv_ref, zkr_ref, qg_ref, kvg_ref, wq_ref, wkv_ref, tc_ref, ta_ref, tb_ref,
             qn_ref, kvn_ref, q_ref, k_ref, v_ref):
        c, a, b = tc_ref[...], ta_ref[...], tb_ref[...]
        qn = _norm_bf16(zq_ref[...], qg_ref[...])
        kvn = _norm_bf16(zkv_ref[...], kvg_ref[...])
        qn_ref[...] = qn
        kvn_ref[...] = kvn
        kpe = _rope(zkr_ref[...], c, a, b).astype(BF16)
        for h in range(MLA_HEADS):
            lo = h * HEAD_QK
            qp = _dot(qn, wq_ref[h], NT)
            q_ref[:, lo:lo + 128] = qp[:, :128].astype(BF16)
            q_ref[:, lo + 128:lo + 256] = _rope(qp[:, 128:], c, a, b).astype(BF16)
            kv = _dot(kvn, wkv_ref[h], NN)
            k_ref[:, lo:lo + 128] = kv[:, :128].astype(BF16)
            k_ref[:, lo + 128:lo + 256] = kpe
            v_ref[:, h * HEAD_V:(h + 1) * HEAD_V] = kv[:, 128:].astype(BF16)

    def cols(width, blk):
        return pl.BlockSpec((bm, width), lambda i: (i, blk))

    def whole(a):
        return pl.BlockSpec(a.shape, lambda i: (0,) * a.ndim)

    tab = cols(128, 0)
    return _call(
        body, name="qkv_prep", grid=(s // bm,),
        in_specs=[cols(Q_LORA, 0), cols(KV_LORA, 2), cols(128, 3), whole(q_gain), whole(kv_gain), whole(wq_t),
                  whole(wkv), tab, tab, tab],
        out_specs=[cols(Q_LORA, 0), cols(KV_LORA, 0), cols(1024, 0), cols(1024, 0), cols(512, 0)],
        out_shape=[jax.ShapeDtypeStruct((s, Q_LORA), BF16), jax.ShapeDtypeStruct((s, KV_LORA), BF16),
                   jax.ShapeDtypeStruct((s, 1024), BF16), jax.ShapeDtypeStruct((s, 1024), BF16),
                   jax.ShapeDtypeStruct((s, 512), BF16)],
        sem=("parallel",), args=[z, z, z, q_gain, kv_gain, wq_t, wkv, tc, ta, tb])


def _qkv_prep_bwd(dq, dk, dv, z, qn, kvn, q_gain, kv_gain, wq_t, wkv, tc, ta, tb):
    s = z.shape[0]
    bm = _row_block(s, 512)
    nsteps = s // bm

    def body(dq_ref, dk_ref, dv_ref, zq_ref, zkv_ref, qn_ref, kvn_ref, qg_ref, kvg_ref, wq_ref, wkv_ref,
             tc_ref, ta_ref, tb_ref, dz_ref, dqg_ref, dkvg_ref, dwq_ref, dwkv_ref, wq_acc, wkv_acc):
        i = pl.program_id(0)
        c, a, b = tc_ref[...], ta_ref[...], tb_ref[...]

        @pl.when(i == 0)
        def _():
            wq_acc[...] = jnp.zeros_like(wq_acc)
            wkv_acc[...] = jnp.zeros_like(wkv_acc)

        qn, kvn = qn_ref[...], kvn_ref[...]
        dqn = jnp.zeros((bm, Q_LORA), F32)
        dkvn = jnp.zeros((bm, KV_LORA), F32)
        dpe = jnp.zeros((bm, 128), F32)
        for h in range(MLA_HEADS):
            lo = h * HEAD_QK
            dqp = jnp.concatenate([dq_ref[:, lo:lo + 128].astype(BF16),
                                   _rope_t(dq_ref[:, lo + 128:lo + 256], c, a, b).astype(BF16)], axis=1)
            dqn = dqn + _dot(dqp, wq_ref[h], NN)
            wq_acc[h] += _dot(dqp, qn, TN)
            dkv = jnp.concatenate([dk_ref[:, lo:lo + 128].astype(BF16),
                                   dv_ref[:, h * HEAD_V:(h + 1) * HEAD_V].astype(BF16)], axis=1)
            dkvn = dkvn + _dot(dkv, wkv_ref[h], NT)
            wkv_acc[h] += _dot(kvn, dkv, TN)
            dpe = dpe + dk_ref[:, lo + 128:lo + 256]
        dcq, dqg = _rms_bwd_math(dqn, zq_ref[...], qg_ref[...], Q_LORA)
        dckv, dkvg = _rms_bwd_math(dkvn, zkv_ref[...], kvg_ref[...], KV_LORA)
        dz_ref[:, 0:Q_LORA] = dcq.astype(BF16)
        dz_ref[:, Q_LORA:Q_LORA + KV_LORA] = dckv.astype(BF16)
        dz_ref[:, Q_LORA + KV_LORA:512] = _rope_t(dpe, c, a, b).astype(BF16)

        @pl.when(i == 0)
        def _():
            dqg_ref[...] = dqg
            dkvg_ref[...] = dkvg

        @pl.when(i > 0)
        def _():
            dqg_ref[...] += dqg
            dkvg_ref[...] += dkvg

        @pl.when(i == nsteps - 1)
        def _():
            dwq_ref[...] = wq_acc[...].astype(BF16)
            dwkv_ref[...] = wkv_acc[...].astype(BF16)

    def cols(width, blk):
        return pl.BlockSpec((bm, width), lambda i: (i, blk))

    def whole(shape):
        return pl.BlockSpec(shape, lambda i: (0,) * len(shape))

    tab = cols(128, 0)
    return _call(
        body, name="qkv_prep_bwd", grid=(nsteps,),
        in_specs=[cols(1024, 0), cols(1024, 0), cols(512, 0), cols(Q_LORA, 0), cols(KV_LORA, 2), cols(Q_LORA, 0),
                  cols(KV_LORA, 0), whole(q_gain.shape), whole(kv_gain.shape), whole(wq_t.shape), whole(wkv.shape),
                  tab, tab, tab],
        out_specs=[cols(512, 0), whole(q_gain.shape), whole(kv_gain.shape), whole(wq_t.shape), whole(wkv.shape)],
        out_shape=[jax.ShapeDtypeStruct((s, 512), BF16), jax.ShapeDtypeStruct(q_gain.shape, F32),
                   jax.ShapeDtypeStruct(kv_gain.shape, F32), jax.ShapeDtypeStruct(wq_t.shape, BF16),
                   jax.ShapeDtypeStruct(wkv.shape, BF16)],
        scratch_shapes=[pltpu.VMEM(wq_t.shape, F32), pltpu.VMEM(wkv.shape, F32)],
        sem=("arbitrary",), args=[dq, dk, dv, z, z, qn, kvn, q_gain, kv_gain, wq_t, wkv, tc, ta, tb])


def _causal_mask(s, row0, col0):
    rows = row0 + lax.broadcasted_iota(jnp.int32, s.shape, 0)
    cols = col0 + lax.broadcasted_iota(jnp.int32, s.shape, 1)
    return jnp.where(cols <= rows, s, -jnp.inf)


def _attn_fwd(name, q, k, k_off, v, v_off, nh, dq, dv, scale, causal, blk):
    sq, sk = q.shape[0], k.shape[0]
    bq = min(blk, sq)
    bk = min(blk, sk)
    nkv = sk // bk
    assert not causal or (sq == sk and bq == bk)

    hq = bq
    log2e = 1.4426950408889634
    c2 = scale * log2e

    def body(q_ref, k_ref, v_ref, o_ref, lse_ref):
        qi = pl.program_id(1)
        qs = (q_ref[...],)

        def step(j, carry, masked):
            rows = pl.ds(pl.multiple_of(j * bk, bk), bk)
            kb, vb = k_ref[rows, :], v_ref[rows, :]
            out = []
            for t, (m, l, acc) in enumerate(carry):
                s = _dot(qs[t], kb, NT) * c2
                if masked:
                    s = _causal_mask(s, qi * bq + t * hq, j * bk)
                m_new = jnp.maximum(m, jnp.max(s, axis=-1, keepdims=True))
                alpha = jnp.exp2(m - m_new)
                p = jnp.exp2(s - m_new)
                l = alpha * l + jnp.sum(p, axis=-1, keepdims=True)
                acc = alpha * acc + _dot(p, vb, NN)
                out.append((m_new, l, acc))
            return tuple(out)

        one = (jnp.full((hq, 1), -jnp.inf, F32), jnp.zeros((hq, 1), F32), jnp.zeros((hq, dv), F32))
        init = (one,)
        if causal:
            carry = lax.fori_loop(0, qi, lambda j, c: step(j, c, False), init)
            fin = step(qi, carry, True)
        else:
            fin = lax.fori_loop(0, nkv, lambda j, c: step(j, c, False), init)
        for t, (m, l, acc) in enumerate(fin):
            o_ref[t * hq:(t + 1) * hq, :] = (acc / l).astype(o_ref.dtype)
            lse_ref[t * hq:(t + 1) * hq, :] = m * (1.0 / log2e) + jnp.log(l)

    return _call(
        body, name=name, grid=(nh, sq // bq),
        in_specs=[pl.BlockSpec((bq, dq), lambda h, i: (i, h)),
                  pl.BlockSpec((sk, dq), lambda h, i: (0, k_off + h)),
                  pl.BlockSpec((sk, dv), lambda h, i: (0, v_off + h))],
        out_specs=[pl.BlockSpec((bq, dv), lambda h, i: (i, h)), pl.BlockSpec((None, bq, 1), lambda h, i: (h, i, 0))],
        out_shape=[jax.ShapeDtypeStruct((sq, nh * dv), BF16), jax.ShapeDtypeStruct((nh, sq, 1), F32)],
        sem=("parallel", "parallel"), args=[q, k, v])


def _attn_delta(name, do, do_off, o, nh, dv):
    s = o.shape[0]
    bm = _row_block(s, 512)

    def body(do_ref, o_ref, d_ref):
        d_ref[...] = jnp.sum(do_ref[...].astype(F32) * o_ref[...].astype(F32), axis=-1, keepdims=True)

    return pl.pallas_call(
        body, name=name, grid=(nh, s // bm),
        in_specs=[pl.BlockSpec((bm, dv), lambda h, i: (i, do_off + h)), pl.BlockSpec((bm, dv), lambda h, i: (i, h))],
        out_specs=pl.BlockSpec((None, bm, 1), lambda h, i: (h, i, 0)),
        out_shape=jax.ShapeDtypeStruct((nh, s, 1), F32),
        compiler_params=_params("parallel", "parallel"),
    )(do, o)


def _attn_bwd(name, q, k, k_off, v, v_off, do, do_off, lse, delta, nh, dq, dv, scale, causal, blk):
    sq, sk = q.shape[0], k.shape[0]
    bq = min(blk, sq)
    bk = min(blk, sk)
    nq = sq // bq
    assert not causal or (sq == sk and bq == bk)

    def body(q_ref, k_ref, v_ref, do_ref, lse_ref, dl_ref, dq_ref, dk_ref, dv_ref, dk_acc, dv_acc):
        j = pl.program_id(1)

        @pl.when(j == 0)
        def _():
            dq_ref[...] = jnp.zeros_like(dq_ref)

        dk_acc[...] = jnp.zeros_like(dk_acc)
        dv_acc[...] = jnp.zeros_like(dv_acc)
        kv = k_ref[...]
        vv = v_ref[...]

        def step(i, masked):
            rows = pl.ds(pl.multiple_of(i * bq, bq), bq)
            qv = q_ref[rows, :]
            dov = do_ref[rows, :].astype(BF16)
            s = _dot(qv, kv, NT) * scale
            if masked:
                s = _causal_mask(s, i * bq, j * bk)
            p = jnp.exp(s - lse_ref[rows, :])
            dp = _dot(dov, vv, NT)
            ds = (p * (dp - dl_ref[rows, :]) * scale).astype(BF16)
            dv_acc[...] += _dot(p, dov, TN)
            dk_acc[...] += _dot(ds, qv, TN)
            dq_ref[rows, :] += _dot(ds, kv, NN)

        if causal:
            step(j, True)

            def loop(i, c):
                step(i, False)
                return c

            lax.fori_loop(j + 1, nq, loop, 0)
        else:
            def loop(i, c):
                step(i, False)
                return c

            lax.fori_loop(0, nq, loop, 0)
        dk_ref[...] = dk_acc[...]
        dv_ref[...] = dv_acc[...]

    stat = pl.BlockSpec((None, sq, 1), lambda h, j: (h, 0, 0))
    return _call(
        body, name=name, grid=(nh, sk // bk),
        in_specs=[pl.BlockSpec((sq, dq), lambda h, j: (0, h)),
                  pl.BlockSpec((bk, dq), lambda h, j: (j, k_off + h)),
                  pl.BlockSpec((bk, dv), lambda h, j: (j, v_off + h)),
                  pl.BlockSpec((sq, dv), lambda h, j: (0, do_off + h)), stat, stat],
        out_specs=[pl.BlockSpec((sq, dq), lambda h, j: (0, h)),
                   pl.BlockSpec((bk, dq), lambda h, j: (j, h)),
                   pl.BlockSpec((bk, dv), lambda h, j: (j, h))],
        out_shape=[jax.ShapeDtypeStruct((sq, nh * dq), F32), jax.ShapeDtypeStruct((sk, nh * dq), F32),
                   jax.ShapeDtypeStruct((sk, nh * dv), F32)],
        scratch_shapes=[pltpu.VMEM((bk, dq), F32), pltpu.VMEM((bk, dv), F32)],
        sem=("parallel", "arbitrary"), args=[q, k, v, do, lse, delta])


def _pool_diff(z, g):
    s = z.shape[0]
    t = lax.broadcasted_iota(jnp.int32, z.shape, 0)
    acc = z
    sums = []
    for k in (1, 2, 4, 8):
        acc = acc + jnp.where(t >= k, pltpu.roll(acc, k, 0), 0.0)
        sums.append(acc)
    win = jnp.where(g == 0, sums[0], jnp.where(g == 1, sums[1], jnp.where(g == 2, sums[2], sums[3])))
    w = lax.shift_left(jnp.int32(2), g)
    count = jnp.minimum(t + 1, w).astype(F32)
    del s
    return win / count - z, count


def _pool_fwd(z, pool_w, pool_scale):
    s = z.shape[0]

    def body(z_ref, w_ref, sc_ref, o_ref):
        diff, _ = _pool_diff(z_ref[...], pl.program_id(0))
        o_ref[...] = (_dot(diff, w_ref[...], NN) * sc_ref[...]).astype(o_ref.dtype)

    return _call(
        body, name="pool_fwd", grid=(POOL_GROUPS,),
        in_specs=[pl.BlockSpec((s, POOL_CH), lambda g: (0, 4 + g)),
                  pl.BlockSpec((None, POOL_CH, POOL_CH), lambda g: (g, 0, 0)),
                  pl.BlockSpec((1, POOL_CH), lambda g: (0, g))],
        out_specs=[pl.BlockSpec((s, POOL_CH), lambda g: (0, g))],
        out_shape=[jax.ShapeDtypeStruct((s, POOL_GROUPS * POOL_CH), BF16)],
        sem=("parallel",), args=[z, pool_w, pool_scale])[0]


def _pool_bwd(dcat, z, pool_w, pool_scale):
    s = z.shape[0]

    def body(dp_ref, z_ref, w_ref, sc_ref, dz_ref, dw_ref, dsc_ref):
        g = pl.program_id(0)
        diff, count = _pool_diff(z_ref[...], g)
        dpf = dp_ref[...].astype(F32)
        u = _dot(diff, w_ref[...], NN)
        dsc_ref[...] = jnp.sum(dpf * u, axis=0, keepdims=True)
        du = (dpf * sc_ref[...]).astype(BF16)
        dw_ref[...] = _dot(diff, du, TN)
        ddiff = _dot(du, w_ref[...], NT)
        t = lax.broadcasted_iota(jnp.int32, ddiff.shape, 0)
        acc = ddiff / count
        sums = []
        for k in (1, 2, 4, 8):
            acc = acc + jnp.where(t < s - k, pltpu.roll(acc, s - k, 0), 0.0)
            sums.append(acc)
        win = jnp.where(g == 0, sums[0], jnp.where(g == 1, sums[1], jnp.where(g == 2, sums[2], sums[3])))
        dz_ref[...] = win - ddiff

    return pl.pallas_call(
        body, name="pool_bwd", grid=(POOL_GROUPS,),
        in_specs=[pl.BlockSpec((s, POOL_CH), lambda g: (0, 4 + g)),
                  pl.BlockSpec((s, POOL_CH), lambda g: (0, 4 + g)),
                  pl.BlockSpec((None, POOL_CH, POOL_CH), lambda g: (g, 0, 0)),
                  pl.BlockSpec((1, POOL_CH), lambda g: (0, g))],
        out_specs=[pl.BlockSpec((s, POOL_CH), lambda g: (0, g)),
                   pl.BlockSpec((None, POOL_CH, POOL_CH), lambda g: (g, 0, 0)),
                   pl.BlockSpec((1, POOL_CH), lambda g: (0, g))],
        out_shape=[jax.ShapeDtypeStruct((s, POOL_GROUPS * POOL_CH), F32),
                   jax.ShapeDtypeStruct((POOL_GROUPS, POOL_CH, POOL_CH), F32),
                   jax.ShapeDtypeStruct((1, POOL_GROUPS * POOL_CH), F32)],
        compiler_params=_params("parallel"),
    )(dcat, z, pool_w, pool_scale)


def _local_step(x, mem, positions, target, w, grads):
    tc, ta, tb = _rope_tables(positions)
    blk = _ATT_BLOCK

    if "ffn1_shards" in w:
        n1, a1, dadu1, dadg1, w["ffn1_w_gate"], w["ffn1_w_up"], w["ffn1_w_down"] = _ffn1_up_gather(
            x, w["ffn1_norm"], *w["ffn1_shards"])
    else:
        n1 = _rmsnorm_fwd("ffn1_norm", x, w["ffn1_norm"], D_MODEL)
        a1, dadu1, dadg1 = _ffn_up("ffn1_up", n1, w["ffn1_w_gate"], w["ffn1_w_up"])
    h1, n2 = _ffn_down("ffn1_down", a1, w["ffn1_w_down"], x, w["mix_norm"])
    z = _w_in_fwd(n2, w["w_in"])
    qn, kvn, qf, kf, vf = _qkv_prep(z, w["q_norm"], w["kv_norm"], w["w_q_up"], w["w_kv_up"], tc, ta, tb)
    att, lse = _attn_fwd("mla_fwd", qf, kf, 0, vf, 0, MLA_HEADS, HEAD_QK, HEAD_V, MLA_SCALE, True, blk)
    pool = _pool_fwd(z, w["pool_w"], w["pool_scale"])
    s = x.shape[0]
    bm = _row_block(s)
    row = pl.BlockSpec((bm, D_MODEL), lambda i, k: (i, 0))
    half = pl.BlockSpec((bm, 512), lambda i, k: (i, 0))
    h2, n3 = _matmul(
        "w_out", (s // bm, 1),
        [(att, half, w["w_out"], pl.BlockSpec((512, D_MODEL), lambda i, k: (0, 0)), NN),
         (pool, half, w["w_out"], pl.BlockSpec((512, D_MODEL), lambda i, k: (1, 0)), NN)],
        [(h1, row)] + _residual_outs(s, bm, w["xattn_norm"])[0], _residual_outs(s, bm, w["xattn_norm"])[1],
        _residual_epilogue(1.0, True), None)
    memn = _rmsnorm_fwd("mem_norm", mem, w["mem_norm"], D_MODEL)
    qm = _mm_nn("w_mq", n3, w["w_mq"], BF16)
    kvm = _mm_heads_fwd("w_mkv", memn, w["w_mkv"], BF16)
    om, lse_m = _attn_fwd("xattn_fwd", qm, kvm, 0, kvm, MEM_HEADS, MEM_HEADS, MEM_HEAD_DIM, MEM_HEAD_DIM,
                          MEM_SCALE, False, blk)
    h3, n4 = _mm_nn("w_mo", om, w["w_mo"], F32, res=h2, gain=w["ffn2_norm"])
    a2, dadu2, dadg2 = _ffn_up("ffn2_up", n4, w["ffn2_w_gate"], w["ffn2_w_up"])
    dh4, dh4b, loss_vec, d_final = _ffn_down("ffn2_down", a2, w["ffn2_w_down"], h3, loss=(w["final_norm"], target))
    grads["final_norm"] = d_final

    dh3, dh3b, grads["ffn2_norm"] = _ffn_bwd("ffn2", dh4b, n4, dadg2, dadu2, a2, w["ffn2_w_gate"], w["ffn2_w_up"],
                                             w["ffn2_w_down"], grads, norm_bwd=(h3, w["ffn2_norm"], dh4))

    dom, delta_m = _mm_nt("w_mo_dx", dh3b, w["w_mo"], BF16, attn_out=om, nh=MEM_HEADS, dv=MEM_HEAD_DIM)
    grads["w_mo"] = _mm_tn("w_mo_dw", om, dh3b)
    dqm, dkm, dvm = _attn_bwd("xattn_bwd", qm, kvm, 0, kvm, MEM_HEADS, dom, 0, lse_m, delta_m, MEM_HEADS,
                              MEM_HEAD_DIM, MEM_HEAD_DIM, MEM_SCALE, False, blk)
    dkvm = jnp.concatenate([dkm, dvm], axis=1).astype(BF16)
    dh2, dh2b, grads["xattn_norm"] = _mm_nt_norm_bwd("w_mq_dx", dqm, w["w_mq"], h2, w["xattn_norm"], dh3)
    grads["w_mq"] = _mm_tn("w_mq_dw", n3, dqm)
    dmemn, grads["w_mkv"] = _mm_heads_bwd("w_mkv", dkvm, memn, w["w_mkv"])
    _, grads["mem_norm"] = _rmsnorm_bwd("mem_norm_bwd", dmemn, mem, w["mem_norm"], D_MODEL, out_dtype=BF16)

    dcat, delta = _mm_nt("w_out_dx", dh2b, w["w_out"], BF16, attn_out=att, nh=MLA_HEADS, dv=HEAD_V)
    grads["w_out"] = jnp.concatenate([_mm_tn("w_out_dw_a", att, dh2b), _mm_tn("w_out_dw_p", pool, dh2b)], axis=0)
    dzp, grads["pool_w"], grads["pool_scale"] = _pool_bwd(dcat, z, w["pool_w"], w["pool_scale"])
    dqf, dkf, dvf = _attn_bwd("mla_bwd", qf, kf, 0, vf, 0, dcat, 0, lse, delta, MLA_HEADS, HEAD_QK, HEAD_V,
                              MLA_SCALE, True, blk)
    dz_lat, grads["q_norm"], grads["kv_norm"], grads["w_q_up"], grads["w_kv_up"] = _qkv_prep_bwd(
        dqf, dkf, dvf, z, qn, kvn, w["q_norm"], w["kv_norm"], w["w_q_up"], w["w_kv_up"], tc, ta, tb)
    dz = jnp.concatenate([dz_lat, dzp.astype(BF16)], axis=1)
    grads["w_in"] = _w_in_dw(dz, n2)
    dh1, dh1b, grads["mix_norm"] = _w_in_dx_norm_bwd(dz, w["w_in"], h1, w["mix_norm"], dh2)

    dn1 = _ffn_bwd("ffn1", dh1b, n1, dadg1, dadu1, a1, w["ffn1_w_gate"], w["ffn1_w_up"], w["ffn1_w_down"], grads)
    dx, grads["ffn1_norm"], _ = _rmsnorm_bwd("ffn1_norm_bwd", dn1, x, w["ffn1_norm"], D_MODEL, dres=dh1)
    return loss_vec[0, 0], dx


def _mesh_pos():
    x, y, c = lax.axis_index("x"), lax.axis_index("y"), lax.axis_index("c")
    chips = [(1 - x, y), (x, 1 - y), (1 - x, 1 - y)]
    chip_ids = [2 * cx + cy for cx, cy in chips]
    return x, y, c, 2 * x + y, chips, chip_ids


def _half_rows(c, rows):
    hr = rows // 2
    return pl.ds(pl.multiple_of(c * hr, 16), hr), pl.ds(pl.multiple_of((1 - c) * hr, 16), hr)


def _ag_ici_stage(shards, relative=False):
    n = len(shards)

    def copies(ins, outs):
        x, y, c, me, chips, _ = _mesh_pos()
        out = []
        for k in range(n):
            mine, _ = _half_rows(c, ins[k].shape[0])
            out.append((ins[k], outs[k].at[0 if relative else me], None))
            for j, (cx, cy) in enumerate(chips):
                slab = _REL_OF_PEER[j] if relative else me
                out.append((ins[k].at[mine], outs[k].at[slab, mine], (cx, cy, c)))
        return out

    return _Stage(shards, [jax.ShapeDtypeStruct((N_CHIPS,) + s.shape, s.dtype) for s in shards], 3 * n, n, copies)


def _quarter_rows(c, rows):
    qr = rows // 4
    return pl.ds(pl.multiple_of(c * 2 * qr, 16), qr), pl.ds(pl.multiple_of(c * 2 * qr + qr, 16), qr)


def _ag_fwd_stage(fulls, relative=False):
    n = len(fulls)

    def copies(ins, outs):
        x, y, c, me, chips, chip_ids = _mesh_pos()
        out = []
        for k in range(n):
            q0, q1 = _quarter_rows(c, ins[k].shape[1])
            from_x, from_y, diag = (2, 1, 3) if relative else chip_ids
            out.append((ins[k].at[from_x, q0], outs[k].at[diag if relative else from_x, q0], (*chips[1], c)))
            out.append((ins[k].at[from_y, q1], outs[k].at[diag if relative else from_y, q1], (*chips[0], c)))
        return out

    return _Stage(fulls, [jax.ShapeDtypeStruct(f.shape, f.dtype) for f in fulls], 2 * n, 0, copies,
                  aliases={k: k for k in range(n)})


def _ag_d2d_stage(fulls, relative=False):
    n = len(fulls)

    def copies(ins, outs):
        x, y, c, me, _, chip_ids = _mesh_pos()
        out = []
        for k in range(n):
            mine, _ = _half_rows(c, ins[k].shape[1])
            for j in range(3):
                slab = _REL_OF_PEER[j] if relative else chip_ids[j]
                out.append((ins[k].at[slab, mine], outs[k].at[slab, mine], (x, y, 1 - c)))
        return out

    return _Stage(fulls, [jax.ShapeDtypeStruct(f.shape, f.dtype) for f in fulls], 3 * n, 0, copies,
                  aliases={k: k for k in range(n)})


def _rs_swap_stage(grads):
    n = len(grads)

    def copies(ins, outs):
        x, y, c, _, _, _ = _mesh_pos()
        out = []
        for k in range(n):
            _, other = _half_rows(c, ins[k].shape[1])
            out.append((ins[k].at[:, other, :], outs[k], (x, y, 1 - c)))
        return out

    return _Stage(grads, [jax.ShapeDtypeStruct((N_CHIPS, g.shape[1] // 2, g.shape[2]), g.dtype) for g in grads],
                  n, 0, copies)


_REL_OF_PEER = (2, 1, 3)


def _rs_scatter_stage(sums, relative=False):
    n = len(sums)

    def copies(ins, outs):
        x, y, c, me, chips, chip_ids = _mesh_pos()
        out = []
        for k in range(n):
            mine, _ = _half_rows(c, 2 * ins[k].shape[1])
            out.append((ins[k].at[0 if relative else me], outs[k].at[0, mine, :], None))
            for j, (cx, cy) in enumerate(chips):
                slab = _REL_OF_PEER[j] if relative else chip_ids[j]
                out.append((ins[k].at[slab], outs[k].at[1 + j, mine, :], (cx, cy, c)))
        return out

    return _Stage(sums, [jax.ShapeDtypeStruct((N_CHIPS, 2 * s.shape[1], s.shape[2]), s.dtype) for s in sums],
                  3 * n, n, copies)


def _rs_mirror_stage(parts):
    n = len(parts)

    def copies(ins, outs):
        x, y, c, _, _, _ = _mesh_pos()
        out = []
        for k in range(n):
            mine, _ = _half_rows(c, ins[k].shape[1])
            out.append((ins[k].at[:, mine, :], outs[k].at[:, mine, :], (x, y, 1 - c)))
        return out

    return _Stage(parts, [jax.ShapeDtypeStruct(p.shape, p.dtype) for p in parts], n, 0, copies,
                  aliases={k: k for k in range(n)})


def _pair_add(name, g, r1, core):
    _, rows, cols = g.shape
    hr = rows // 2

    def body(c_ref, g_ref, r_ref, o_ref):
        o_ref[...] = (g_ref[...].astype(F32) + r_ref[...].astype(F32)).astype(BF16)

    half = pl.BlockSpec((None, hr, cols), lambda j, c: (j, 0, 0))
    return pl.pallas_call(
        body, name=name,
        grid_spec=pltpu.PrefetchScalarGridSpec(
            num_scalar_prefetch=1, grid=(N_CHIPS,),
            in_specs=[pl.BlockSpec((None, hr, cols), lambda j, c: (j, c[0], 0)), half], out_specs=half),
        out_shape=jax.ShapeDtypeStruct((N_CHIPS, hr, cols), BF16),
        compiler_params=_params("parallel"),
    )(core, g, r1)


def _all_gather_weights(shards):
    n = len(shards)

    def body(*refs):
        ins, outs = refs[:n], refs[n:2 * n]
        send, recv, loc = refs[2 * n:]
        x, y, c, me, chips, chip_ids = _mesh_pos()
        sib = (x, y, 1 - c)

        def halves(k):
            hr = ins[k].shape[0] // 2
            return pl.ds(pl.multiple_of(c * hr, 16), hr), pl.ds(pl.multiple_of((1 - c) * hr, 16), hr)

        def remote(src, dst, k, j, dev):
            return pltpu.make_async_remote_copy(src_ref=src, dst_ref=dst, send_sem=send.at[k, j],
                                                recv_sem=recv.at[k, j], device_id=dev, device_id_type=_MESH)

        started = []
        local = []
        for k in range(n):
            mine, _ = halves(k)
            cp = pltpu.make_async_copy(ins[k], outs[k].at[me], loc.at[k])
            cp.start()
            local.append(cp)
            for j, (cx, cy) in enumerate(chips):
                cp = remote(ins[k].at[mine], outs[k].at[me, mine], k, j, (cx, cy, c))
                cp.start()
                started.append(cp)
        for k in range(n):
            mine, _ = halves(k)
            for j in range(3):
                land = outs[k].at[chip_ids[j], mine]
                remote(land, land, k, j, sib).wait_recv()
                cp = remote(land, land, k, 3 + j, sib)
                cp.start()
                started.append(cp)
        for k in range(n):
            _, other = halves(k)
            for j in range(3):
                land = outs[k].at[chip_ids[j], other]
                remote(land, land, k, 3 + j, sib).wait_recv()
        for cp in started:
            cp.wait_send()
        for cp in local:
            cp.wait()

    return pl.pallas_call(
        body, name="all_gather_weights", in_specs=[_ANY] * n, out_specs=[_ANY] * n,
        out_shape=[jax.ShapeDtypeStruct((N_CHIPS,) + s.shape, s.dtype) for s in shards],
        scratch_shapes=[pltpu.SemaphoreType.DMA((n, 6)), pltpu.SemaphoreType.DMA((n, 6)),
                        pltpu.SemaphoreType.DMA((n,))],
        compiler_params=pltpu.CompilerParams(vmem_limit_bytes=V7X_VMEM_LIMIT_BYTES),
    )(*shards)


_RS_CHUNK = 32


def _reduce_scatter(name, grads):
    n = len(grads)

    def body(*refs):
        gs, outs = refs[:n], refs[n:2 * n]
        own, r1, r2, fin = (refs[(2 + i) * n:(3 + i) * n] for i in range(4))
        a_send, a_recv, b_send, b_recv, c_send, c_recv, l_in, l_out = refs[6 * n:]
        x, y, c, me, chips, chip_ids = _mesh_pos()
        sib = (x, y, 1 - c)

        def halves(k):
            hr = gs[k].shape[1] // 2
            return hr, pl.ds(pl.multiple_of(c * hr, 16), hr), pl.ds(pl.multiple_of((1 - c) * hr, 16), hr)

        def remote(src, dst, ssem, rsem, dev):
            return pltpu.make_async_remote_copy(src_ref=src, dst_ref=dst, send_sem=ssem, recv_sem=rsem,
                                                device_id=dev, device_id_type=_MESH)

        sends, locals_in = [], []
        for k in range(n):
            hr, mine, other = halves(k)
            cp = remote(gs[k].at[:, other, :], r1[k], a_send.at[k], a_recv.at[k], sib)
            cp.start()
            sends.append(cp)
            cp = pltpu.make_async_copy(gs[k].at[:, mine, :], own[k], l_in.at[k])
            cp.start()
            locals_in.append(cp)

        for k in range(n):
            hr, mine, other = halves(k)
            locals_in[k].wait()
            remote(r1[k], r1[k], a_send.at[k], a_recv.at[k], sib).wait_recv()
            for j in range(N_CHIPS):
                def add(i, carry, k=k, j=j):
                    rows = pl.ds(pl.multiple_of(i * _RS_CHUNK, _RS_CHUNK), _RS_CHUNK)
                    own[k][j, rows, :] = (own[k][j, rows, :].astype(F32) + r1[k][j, rows, :].astype(F32)).astype(BF16)
                    return carry

                lax.fori_loop(0, hr // _RS_CHUNK, add, 0)
            for j, (cx, cy) in enumerate(chips):
                cp = remote(own[k].at[chip_ids[j]], r2[k].at[j], b_send.at[k, j], b_recv.at[k, j], (cx, cy, c))
                cp.start()
                sends.append(cp)

        locals_out = []
        for k in range(n):
            hr, mine, other = halves(k)
            for j in range(3):
                remote(r2[k].at[j], r2[k].at[j], b_send.at[k, j], b_recv.at[k, j], sib).wait_recv()

            def total(i, carry, k=k):
                rows = pl.ds(pl.multiple_of(i * _RS_CHUNK, _RS_CHUNK), _RS_CHUNK)
                acc = own[k][me, rows, :].astype(F32)
                for j in range(3):
                    acc = acc + r2[k][j, rows, :].astype(F32)
                fin[k][rows, :] = acc
                return carry

            lax.fori_loop(0, hr // _RS_CHUNK, total, 0)
            cp = remote(fin[k], outs[k].at[mine, :], c_send.at[k], c_recv.at[k], sib)
            cp.start()
            sends.append(cp)
            cp = pltpu.make_async_copy(fin[k], outs[k].at[mine, :], l_out.at[k])
            cp.start()
            locals_out.append(cp)

        for k in range(n):
            hr, mine, other = halves(k)
            land = outs[k].at[other, :]
            remote(land, land, c_send.at[k], c_recv.at[k], sib).wait_recv()
        for cp in sends:
            cp.wait_send()
        for cp in locals_out:
            cp.wait()

    scratch = []
    for g in grads:
        scratch.append(pltpu.VMEM((N_CHIPS, g.shape[1] // 2, g.shape[2]), BF16))
    for g in grads:
        scratch.append(pltpu.VMEM((N_CHIPS, g.shape[1] // 2, g.shape[2]), BF16))
    for g in grads:
        scratch.append(pltpu.VMEM((3, g.shape[1] // 2, g.shape[2]), BF16))
    for g in grads:
        scratch.append(pltpu.VMEM((g.shape[1] // 2, g.shape[2]), F32))
    dma = pltpu.SemaphoreType.DMA
    scratch += [dma((n,)), dma((n,)), dma((n, 3)), dma((n, 3)), dma((n,)), dma((n,)), dma((n,)), dma((n,))]
    return pl.pallas_call(
        body, name=name, in_specs=[_ANY] * n, out_specs=[_ANY] * n,
        out_shape=[jax.ShapeDtypeStruct(g.shape[1:], F32) for g in grads],
        scratch_shapes=scratch,
        compiler_params=pltpu.CompilerParams(vmem_limit_bytes=V7X_VMEM_LIMIT_BYTES),
    )(*grads)


def _adamw_math(w, g, m, v):
    m = ADAM_B1 * m + (1.0 - ADAM_B1) * g
    v = ADAM_B2 * v + (1.0 - ADAM_B2) * (g * g)
    m_hat = m / (1.0 - ADAM_B1 ** ADAM_STEP)
    v_hat = v / (1.0 - ADAM_B2 ** ADAM_STEP)
    delta = -ADAM_LR * (m_hat / (jnp.sqrt(v_hat) + ADAM_EPS) + ADAM_WD * w)
    return delta, m, v


def _adamw_sum(name, w, parts, m, v):
    r, c = w.shape
    br = r
    while br * c * 4 > (1 << 20) and br % 32 == 0:
        br //= 2

    def body(w_ref, p_ref, m_ref, v_ref, g_ref, d_ref, nm_ref, nv_ref):
        g = p_ref[0].astype(F32)
        for j in range(1, N_CHIPS):
            g = g + p_ref[j].astype(F32)
        d, nm, nv = _adamw_math(w_ref[...], g, m_ref[...], v_ref[...])
        g_ref[...] = g
        d_ref[...] = d
        nm_ref[...] = nm
        nv_ref[...] = nv

    spec = pl.BlockSpec((br, c), lambda i: (i, 0))
    shp = jax.ShapeDtypeStruct((r, c), F32)
    return _call(
        body, name=name, grid=(r // br,),
        in_specs=[spec, pl.BlockSpec((N_CHIPS, br, c), lambda i: (0, i, 0)), spec, spec],
        out_specs=[spec] * 4, out_shape=[shp] * 4, sem=("parallel",), args=[w, parts, m, v])


_SMALL_ROWS = 80


def _small_allreduce_adamw(gpack, wpack, mpack, vpack):
    half = _SMALL_ROWS // 2

    def body(g_ref, w_ref, m_ref, v_ref, go_ref, d_ref, nm_ref, nv_ref, sib_buf, chip_sum, buf, send, recv):
        x, y, c, me, chips, chip_ids = _mesh_pos()
        sib = (x, y, 1 - c)
        mine = pl.ds(pl.multiple_of(c * half, 8), half)

        def remote(src, dst, k, dev):
            return pltpu.make_async_remote_copy(src_ref=src, dst_ref=dst, send_sem=send.at[k], recv_sem=recv.at[k],
                                                device_id=dev, device_id_type=_MESH)

        swap = remote(g_ref, sib_buf, 0, sib)
        swap.start()
        swap.wait()
        chip_sum[...] = g_ref[...] + sib_buf[...]
        buf[me] = chip_sum[...]
        sends = [remote(chip_sum.at[mine], buf.at[me, mine], 1 + j, (cx, cy, c)) for j, (cx, cy) in enumerate(chips)]
        for cp in sends:
            cp.start()
        for cp in sends:
            cp.wait()
        mirrors = [remote(buf.at[chip_ids[j], mine], buf.at[chip_ids[j], mine], 4 + j, sib) for j in range(3)]
        for cp in mirrors:
            cp.start()
        for cp in mirrors:
            cp.wait()
        total = buf[0]
        for i in range(1, N_CHIPS):
            total = total + buf[i]
        go_ref[...] = total
        d, nm, nv = _adamw_math(w_ref[...], total, m_ref[...], v_ref[...])
        d_ref[...] = d
        nm_ref[...] = nm
        nv_ref[...] = nv

    vm = pl.BlockSpec(memory_space=pltpu.VMEM)
    shp = jax.ShapeDtypeStruct((_SMALL_ROWS, D_MODEL), F32)
    return pl.pallas_call(
        body, name="small_allreduce_adamw", in_specs=[vm] * 4, out_specs=[vm] * 4, out_shape=[shp] * 4,
        scratch_shapes=[pltpu.VMEM((_SMALL_ROWS, D_MODEL), F32), pltpu.VMEM((_SMALL_ROWS, D_MODEL), F32),
                        pltpu.VMEM((N_CHIPS, _SMALL_ROWS, D_MODEL), F32), pltpu.SemaphoreType.DMA((7,)),
                        pltpu.SemaphoreType.DMA((7,))],
        compiler_params=pltpu.CompilerParams(vmem_limit_bytes=V7X_VMEM_LIMIT_BYTES),
    )(gpack, wpack, mpack, vpack)


_SMALL_VECTORS = ("ffn1_norm", "mix_norm", "xattn_norm", "mem_norm", "ffn2_norm", "final_norm", "q_norm",
                  "kv_norm", "pool_scale")


_LOSS_ROW = 9
_VEC_ROWS = 16
_POOL_ROWS = POOL_GROUPS * POOL_CH


def _small_params_step(g, w, m, v, loss_local):
    names = list(_SMALL_VECTORS) + ["pool_w"]
    nv = len(_SMALL_VECTORS)
    widths = [g[n].shape[1] for n in _SMALL_VECTORS]
    shapes = {"vec": (_VEC_ROWS, D_MODEL), "pool": (_POOL_ROWS, POOL_CH)}

    def body(*refs):
        ins = refs[:4 * (nv + 1) + 1]
        outs = refs[len(ins):len(ins) + 4 * (nv + 1) + 1]
        vec_own, vec_sib, vec_all, pool_sib, pool_sum, pool_all, send, recv = refs[len(ins) + len(outs):]
        g_in, w_in, m_in, v_in = (ins[k * (nv + 1):(k + 1) * (nv + 1)] for k in range(4))
        loss_in = ins[-1]
        g_out, d_out, m_out, v_out = (outs[k * (nv + 1):(k + 1) * (nv + 1)] for k in range(4))
        loss_out = outs[-1]
        x, y, c, me, chips, chip_ids = _mesh_pos()
        sib = (x, y, 1 - c)

        def remote(src, dst, k, dev):
            return pltpu.make_async_remote_copy(src_ref=src, dst_ref=dst, send_sem=send.at[k], recv_sem=recv.at[k],
                                                device_id=dev, device_id_type=_MESH)

        vec_own[...] = jnp.zeros_like(vec_own)
        for i in range(nv):
            vec_own[i:i + 1, 0:widths[i]] = g_in[i][...]
        vec_own[_LOSS_ROW:_LOSS_ROW + 1, 0:128] = loss_in[...]
        swaps = [remote(vec_own, vec_sib, 0, sib), remote(g_in[nv], pool_sib, 1, sib)]
        for cp in swaps:
            cp.start()
        for cp in swaps:
            cp.wait()
        vec_all[me] = vec_own[...] + vec_sib[...]
        pool_sum[...] = g_in[nv][...] + pool_sib[...]
        pool_all[me] = pool_sum[...]
        hv, hp = _VEC_ROWS // 2, _POOL_ROWS // 2
        mine_v = pl.ds(pl.multiple_of(c * hv, 8), hv)
        mine_p = pl.ds(pl.multiple_of(c * hp, 8), hp)
        sends = []
        for j, (cx, cy) in enumerate(chips):
            sends.append(remote(vec_all.at[me, mine_v], vec_all.at[me, mine_v], 2 + j, (cx, cy, c)))
            sends.append(remote(pool_sum.at[mine_p], pool_all.at[me, mine_p], 5 + j, (cx, cy, c)))
        for cp in sends:
            cp.start()
        for cp in sends:
            cp.wait()
        mirrors = []
        for j in range(3):
            mirrors.append(remote(vec_all.at[chip_ids[j], mine_v], vec_all.at[chip_ids[j], mine_v], 8 + j, sib))
            mirrors.append(remote(pool_all.at[chip_ids[j], mine_p], pool_all.at[chip_ids[j], mine_p], 11 + j, sib))
        for cp in mirrors:
            cp.start()
        for cp in mirrors:
            cp.wait()
        vec_tot = vec_all[0]
        pool_tot = pool_all[0]
        for i in range(1, N_CHIPS):
            vec_tot = vec_tot + vec_all[i]
            pool_tot = pool_tot + pool_all[i]
        vec_sib[...] = vec_tot
        loss_out[...] = vec_sib[_LOSS_ROW:_LOSS_ROW + 1, 0:128]
        for i in range(nv + 1):
            gi = pool_tot if i == nv else vec_sib[i:i + 1, 0:widths[i]]
            d, nm, nvv = _adamw_math(w_in[i][...], gi, m_in[i][...], v_in[i][...])
            g_out[i][...] = gi
            d_out[i][...] = d
            m_out[i][...] = nm
            v_out[i][...] = nvv

    vm = pl.BlockSpec(memory_space=pltpu.VMEM)
    args = [d[n] for d in (g, w, m, v) for n in names] + [jnp.broadcast_to(loss_local.reshape(1, 1), (1, 128))]
    out_shape = [jax.ShapeDtypeStruct(g[n].shape, F32) for _ in range(4) for n in names]
    out_shape.append(jax.ShapeDtypeStruct((1, 128), F32))
    res = pl.pallas_call(
        body, name="small_params_step", in_specs=[vm] * len(args), out_specs=[vm] * len(out_shape),
        out_shape=out_shape,
        scratch_shapes=[pltpu.VMEM(shapes["vec"], F32), pltpu.VMEM(shapes["vec"], F32),
                        pltpu.VMEM((N_CHIPS,) + shapes["vec"], F32), pltpu.VMEM(shapes["pool"], F32),
                        pltpu.VMEM(shapes["pool"], F32), pltpu.VMEM((N_CHIPS,) + shapes["pool"], F32),
                        pltpu.SemaphoreType.DMA((14,)), pltpu.SemaphoreType.DMA((14,))],
        compiler_params=pltpu.CompilerParams(vmem_limit_bytes=V7X_VMEM_LIMIT_BYTES),
    )(*args)
    k = len(names)
    dicts = [dict(zip(names, res[i * k:(i + 1) * k])) for i in range(4)]
    return dicts[0], dicts[1], dicts[2], dicts[3], res[-1]


def _pack_small(d, scalar=None):
    rows = []
    for n in _SMALL_VECTORS:
        v = d[n].reshape(1, -1).astype(F32)
        rows.append(jnp.pad(v, ((0, 0), (0, D_MODEL - v.shape[1]))))
    assert len(rows) == _LOSS_ROW
    extra = jnp.zeros((1, D_MODEL), F32) if scalar is None else jnp.pad(scalar.reshape(1, 1), ((0, 0), (0, D_MODEL - 1)))
    rows.append(extra)
    rows.append(jnp.zeros((16 - len(rows), D_MODEL), F32))
    rows.append(d["pool_w"].reshape(64, D_MODEL).astype(F32))
    return jnp.concatenate(rows, axis=0)


def _unpack_small(pack, like):
    out = {}
    for i, n in enumerate(_SMALL_VECTORS):
        out[n] = pack[i, :like[n].size].reshape(like[n].shape)
    out["pool_w"] = pack[16:].reshape(like["pool_w"].shape)
    return out


_WEIGHTS = ("ffn1_norm", "ffn1_w_gate", "ffn1_w_up", "ffn1_w_down", "mix_norm", "w_in", "q_norm", "w_q_up",
            "kv_norm", "w_kv_up", "pool_w", "pool_scale", "w_out", "xattn_norm", "mem_norm", "w_mq", "w_mkv",
            "w_mo", "ffn2_norm", "ffn2_w_gate", "ffn2_w_up", "ffn2_w_down", "final_norm")
_SHARDED = ("ffn1_w_gate", "ffn1_w_up", "ffn1_w_down", "w_in", "w_q_up", "w_kv_up", "w_out", "w_mq", "w_mkv",
            "w_mo", "ffn2_w_gate", "ffn2_w_up", "ffn2_w_down")
_RS_GROUPS = (("ffn2_w_gate", "ffn2_w_up", "ffn2_w_down"),
              ("w_mo", "w_mq", "w_mkv", "w_out", "w_q_up", "w_kv_up", "w_in"),
              ("ffn1_w_gate", "ffn1_w_up", "ffn1_w_down"))
W_IN_SPLIT = Q_LORA + KV_LORA + ROPE_DIM


_FFN1 = ("ffn1_w_gate", "ffn1_w_up", "ffn1_w_down")
_TRANSPOSED = ("ffn1_w_gate", "ffn1_w_up", "ffn2_w_gate", "ffn2_w_up", "w_in", "w_q_up")


def _local_view(name, a):
    return jnp.swapaxes(a, 1, 2)[0] if name in _TRANSPOSED else a[0]


def _global_view(name, a):
    return jnp.swapaxes(a[None], 1, 2) if name in _TRANSPOSED else a[None]


def _pad_shard(name, a):
    if name == "w_in":
        return jnp.concatenate([a[:W_IN_SPLIT], jnp.zeros((64, a.shape[1]), a.dtype), a[W_IN_SPLIT:]], axis=0)
    if name == "w_q_up":
        return jnp.pad(a, ((0, 64), (0, 0)))
    return a


def _unpad_shard(name, a):
    if name == "w_in":
        return jnp.concatenate([a[:, :W_IN_SPLIT], a[:, W_IN_SPLIT + 64:]], axis=1)
    if name == "w_q_up":
        return a[:, :192]
    return a


def _stacked(g):
    return g if g.ndim == 3 else g.reshape(N_CHIPS, g.shape[0] // N_CHIPS, g.shape[1])


class _Plan:
    AG_UNITS = (
        (("w_in", "w_q_up", "w_kv_up"), "ffn1_up"),
        (("w_out",), "w_in"),
        (("w_mq",), "qkv_prep"),
        (("w_mkv", "w_mo", "ffn2_w_gate"), "mla_fwd"),
        (("ffn2_w_up",), "xattn_fwd"),
        (("ffn2_w_down",), "ffn2_up"),
    )
    RS_UNITS = (
        (("ffn2_w_gate", "ffn2_w_up", "ffn2_w_down"), "ffn2_dn_a", "mla_bwd", "qkv_prep_bwd"),
        (("w_mo", "w_mq", "w_mkv"), "w_out_dx", "mla_bwd", "qkv_prep_bwd"),
        (("w_out", "w_q_up", "w_kv_up", "w_in"), "w_in_dx", "ffn1_dact", "ffn1_dwd"),
        (("ffn1_w_down",), "ffn1_dwg", "ffn1_dwu", "ffn1_dn_a"),
        (("ffn1_w_gate",), "ffn1_dwu", "ffn1_dn_a", "ffn1_dn_b"),
        (("ffn1_w_up",), "ffn1_dn_a", "ffn1_dn_b", "adamw_w_kv_up"),
    )
    ADAMW_ORDER = ("w_kv_up", "ffn2_w_gate", "ffn2_w_up", "ffn2_w_down", "w_mo", "w_mq", "w_mkv", "w_out", "w_q_up",
                   "w_in", "ffn1_w_down", "ffn1_w_gate", "ffn1_w_up")

    def __init__(self, shards, w, grads, core):
        self.shards, self.w, self.grads, self.core = shards, w, grads, core
        self.last_slab_step = 0
        self.parts = {}
        self.ag = [None for _ in self.AG_UNITS]
        self.rs = [[None, None, None, None] for _ in self.RS_UNITS]

    def pre(self, name):
        for i, (names, host) in enumerate(self.AG_UNITS):
            if name == host:
                st = _ag_ici_stage([self.shards[n] for n in names])
                st.then = _ag_d2d_stage(st.outs)
                st.start_step = self.last_slab_step if name == "ffn1_up" else 0
                self.ag[i] = _host(name, st)
        for i, (names, h1, h2, h3) in enumerate(self.RS_UNITS):
            if name == h1:
                self.rs[i][0] = _host(name, _rs_swap_stage([_stacked(self.grads[n]) for n in names]))
            if name == h2:
                self.rs[i][2] = _host(name, _rs_scatter_stage(self.rs[i][1], relative=names[0] in _FFN1))
            if name == h3:
                self.rs[i][3] = _host(name, _rs_mirror_stage(self.rs[i][2].results))

    def post(self, name):
        for i, (names, host) in enumerate(self.AG_UNITS):
            if name == host:
                for n, f in zip(names, self.ag[i].results):
                    self.w[n] = _full_weight(n, f)
        for i, (names, h1, h2, h3) in enumerate(self.RS_UNITS):
            if name == h1:
                self.rs[i][1] = [_pair_add("pair_add_" + n, _stacked(self.grads[n]), r1, self.core)
                                 for n, r1 in zip(names, self.rs[i][0].results)]
            if name == h3:
                for n, p in zip(names, self.rs[i][3].results):
                    self.parts[n] = p


def _full_weight(name, stacked):
    if name in ("w_out", "w_mq", "w_mo"):
        return stacked.reshape(D_MODEL, D_MODEL)
    return stacked


def kernel(x, mem, positions, ffn1_norm, ffn1_w_gate, ffn1_w_up, ffn1_w_down, mix_norm, w_in, q_norm, w_q_up, kv_norm, w_kv_up, pool_w, pool_scale, w_out, xattn_norm, mem_norm, w_mq, w_mkv, w_mo, ffn2_norm, ffn2_w_gate, ffn2_w_up, ffn2_w_down, final_norm, loss_target, m_ffn1_norm, m_ffn1_w_gate, m_ffn1_w_up, m_ffn1_w_down, m_mix_norm, m_w_in, m_q_norm, m_w_q_up, m_kv_norm, m_w_kv_up, m_pool_w, m_pool_scale, m_w_out, m_xattn_norm, m_mem_norm, m_w_mq, m_w_mkv, m_w_mo, m_ffn2_norm, m_ffn2_w_gate, m_ffn2_w_up, m_ffn2_w_down, m_final_norm, v_ffn1_norm, v_ffn1_w_gate, v_ffn1_w_up, v_ffn1_w_down, v_mix_norm, v_w_in, v_q_norm, v_w_q_up, v_kv_norm, v_w_kv_up, v_pool_w, v_pool_scale, v_w_out, v_xattn_norm, v_mem_norm, v_w_mq, v_w_mkv, v_w_mo, v_ffn2_norm, v_ffn2_w_gate, v_ffn2_w_up, v_ffn2_w_down, v_final_norm):
    wts = dict(zip(_WEIGHTS, (ffn1_norm, ffn1_w_gate, ffn1_w_up, ffn1_w_down, mix_norm, w_in, q_norm, w_q_up, kv_norm, w_kv_up, pool_w, pool_scale, w_out, xattn_norm, mem_norm, w_mq, w_mkv, w_mo, ffn2_norm, ffn2_w_gate, ffn2_w_up, ffn2_w_down, final_norm)))
    mom = dict(zip(_WEIGHTS, (m_ffn1_norm, m_ffn1_w_gate, m_ffn1_w_up, m_ffn1_w_down, m_mix_norm, m_w_in, m_q_norm, m_w_q_up, m_kv_norm, m_w_kv_up, m_pool_w, m_pool_scale, m_w_out, m_xattn_norm, m_mem_norm, m_w_mq, m_w_mkv, m_w_mo, m_ffn2_norm, m_ffn2_w_gate, m_ffn2_w_up, m_ffn2_w_down, m_final_norm)))
    var = dict(zip(_WEIGHTS, (v_ffn1_norm, v_ffn1_w_gate, v_ffn1_w_up, v_ffn1_w_down, v_mix_norm, v_w_in, v_q_norm, v_w_q_up, v_kv_norm, v_w_kv_up, v_pool_w, v_pool_scale, v_w_out, v_xattn_norm, v_mem_norm, v_w_mq, v_w_mkv, v_w_mo, v_ffn2_norm, v_ffn2_w_gate, v_ffn2_w_up, v_ffn2_w_down, v_final_norm)))
    small = [n for n in _WEIGHTS if n not in _SHARDED]

    global _PLAN
    shards = {n: _pad_shard(n, _local_view(n, wts[n])).astype(BF16) for n in _SHARDED}
    w = {n: wts[n].reshape(1, -1) for n in _SMALL_VECTORS}
    w["pool_w"] = pool_w[0].astype(BF16)
    grads = {}
    core = lax.axis_index("c").astype(jnp.int32).reshape(1)
    plan = _Plan(shards, w, grads, core)
    _PLAN = plan
    try:
        w["ffn1_shards"] = tuple(shards[n] for n in _FFN1)

        loss_local, dx = _local_step(x[0], mem[0], positions[0], loss_target[0], w, grads)

        def small_view(d):
            out = {n: d[n].reshape(1, -1) for n in _SMALL_VECTORS}
            out["pool_w"] = d["pool_w"].reshape(_POOL_ROWS, POOL_CH)
            return out

        *small_res, loss_vec = _small_params_step(small_view(grads), small_view(wts), small_view(mom),
                                                  small_view(var), loss_local)
        g_out, d_out, m_out, v_out = ({n: r[n].reshape(wts[n].shape) for n in small} for r in small_res)
        loss = loss_vec[0, 0]

        for n in _Plan.ADAMW_ORDER:
            res = _adamw_sum("adamw_" + n, _local_view(n, wts[n]), _unpad_shard(n, plan.parts[n]),
                             _local_view(n, mom[n]), _local_view(n, var[n]))
            g_out[n], d_out[n], m_out[n], v_out[n] = (_global_view(n, r) for r in res)
    finally:
        _PLAN = None
        _PENDING.clear()

    return (loss, dx[None], *[g_out[n] for n in _WEIGHTS], *[d_out[n] for n in _WEIGHTS],
            *[m_out[n] for n in _WEIGHTS], *[v_out[n] for n in _WEIGHTS])
```

```python
import functools

import jax
import jax.numpy as jnp
from jax import lax
from jax.experimental import pallas as pl
from jax.experimental.pallas import tpu as pltpu

F32 = jnp.float32
BF16 = jnp.bfloat16

D_MODEL = 1024
D_FF = 2816
N_CHIPS = 4
FF_SHARD = D_FF // N_CHIPS
MLA_HEADS = 4
Q_LORA = 256
KV_LORA = 128
ROPE_DIM = 64
HEAD_QK = 256
HEAD_V = 128
POOL_GROUPS = 4
POOL_CH = 128
MEM_HEADS = 4
MEM_HEAD_DIM = 256
RMS_EPS = 1e-6
ROPE_BASE = 10000.0
MLA_SCALE = (128 + 64) ** -0.5
MEM_SCALE = MEM_HEAD_DIM ** -0.5

ADAM_LR = 0.001
ADAM_B1 = 0.9
ADAM_B2 = 0.999
ADAM_EPS = 1e-08
ADAM_WD = 0.01
ADAM_STEP = 10

V7X_VMEM_LIMIT_BYTES = 56 * 1024 * 1024

NN = ((1,), (0,))
NT = ((1,), (1,))
TN = ((0,), (0,))


def _params(*sem):
    return pltpu.CompilerParams(dimension_semantics=sem, vmem_limit_bytes=V7X_VMEM_LIMIT_BYTES)


_MESH = pl.DeviceIdType.MESH
_ANY = pl.BlockSpec(memory_space=pl.ANY)


class _Stage:
    def __init__(self, ins, outs, n_remote, n_local, copies, aliases=None):
        self.ins, self.outs, self.n_remote, self.n_local = list(ins), list(outs), n_remote, n_local
        self.copies, self.aliases = copies, dict(aliases or {})
        self.results = None
        self.start_step = 0
        self.then = None

    def descriptors(self, in_refs, out_refs, send, recv, loc):
        ds, ri, li = [], 0, 0
        for src, dst, dev in self.copies(in_refs, out_refs):
            if dev is None:
                ds.append(pltpu.make_async_copy(src, dst, loc.at[li]))
                li += 1
            else:
                ds.append(pltpu.make_async_remote_copy(src_ref=src, dst_ref=dst, send_sem=send.at[ri],
                                                       recv_sem=recv.at[ri], device_id=dev, device_id_type=_MESH))
                ri += 1
        assert ri == self.n_remote and li == self.n_local
        return ds


_PENDING = {}


def _host(name, stage):
    _PENDING.setdefault(name, []).append(stage)
    return stage


_PLAN = None


def _call(body, **kw):
    if _PLAN is not None:
        _PLAN.pre(kw["name"])
    res = _call_hosting(body, **kw)
    if _PLAN is not None:
        _PLAN.post(kw["name"])
    return res


def _call_hosting(body, *, name, grid, in_specs, out_specs, out_shape, sem, args, scratch_shapes=(), aliases=None):
    stages = _PENDING.pop(name, [])
    scratch_shapes = list(scratch_shapes)
    if not stages:
        return pl.pallas_call(body, name=name, grid=grid, in_specs=in_specs, out_specs=out_specs,
                              out_shape=out_shape, scratch_shapes=scratch_shapes,
                              input_output_aliases=dict(aliases or {}), compiler_params=_params(*sem))(*args)
    ni, no, ns = len(in_specs), len(out_shape), len(scratch_shapes)
    c_ins = [a for st in stages for a in st.ins]
    c_outs = [o for st in stages for o in st.outs]
    nci, nco = len(c_ins), len(c_outs)
    aliases, io, oo = dict(aliases or {}), 0, 0
    for st in stages:
        for i, j in st.aliases.items():
            aliases[ni + io + i] = no + oo + j
        io += len(st.ins)
        oo += len(st.outs)
    dma = pltpu.SemaphoreType.DMA
    sems = []
    for st in stages:
        sems += [dma((max(st.n_remote, 1),)), dma((max(st.n_remote, 1),)), dma((max(st.n_local, 1),))]
    followers = [st.then for st in stages if st.then is not None]
    for st in followers:
        sems += [dma((max(st.n_remote, 1),)), dma((max(st.n_remote, 1),)), dma((max(st.n_local, 1),))]

    def wrapped(*refs):
        ins, cin = refs[:ni], refs[ni:ni + nci]
        outs, cout = refs[ni + nci:ni + nci + no], refs[ni + nci + no:ni + nci + no + nco]
        scr = refs[ni + nci + no + nco:ni + nci + no + nco + ns]
        sem_refs = refs[ni + nci + no + nco + ns:]
        step = pl.program_id(0)
        last = pl.program_id(0) == grid[0] - 1
        for ax in range(1, len(grid)):
            step = step * grid[ax] + pl.program_id(ax)
            last = jnp.logical_and(last, pl.program_id(ax) == grid[ax] - 1)

        def descriptors(si):
            io = sum(len(st.ins) for st in stages[:si])
            oo = sum(len(st.outs) for st in stages[:si])
            st = stages[si]
            return st.descriptors(cin[io:io + len(st.ins)], cout[oo:oo + len(st.outs)], *sem_refs[3 * si:3 * si + 3])

        def follower_descriptors(fi):
            si = [k for k, st in enumerate(stages) if st.then is not None][fi]
            oo = sum(len(st.outs) for st in stages[:si])
            bufs = cout[oo:oo + len(stages[si].outs)]
            k0 = 3 * (len(stages) + fi)
            return followers[fi].descriptors(bufs, bufs, *sem_refs[k0:k0 + 3])

        def start(si):
            @pl.when(step == stages[si].start_step)
            def _():
                for d in descriptors(si):
                    d.start()

        for si, st in enumerate(stages):
            if st.start_step == 0:
                start(si)
        body(*ins, *outs, *scr)
        for si, st in enumerate(stages):
            if st.start_step != 0:
                start(si)

        @pl.when(last)
        def _():
            for si in range(len(stages)):
                for d in descriptors(si):
                    d.wait()
            for fi in range(len(followers)):
                for d in follower_descriptors(fi):
                    d.start()
            for fi in range(len(followers)):
                for d in follower_descriptors(fi):
                    d.wait()

    res = pl.pallas_call(
        wrapped, name=name, grid=grid, in_specs=list(in_specs) + [_ANY] * nci,
        out_specs=list(out_specs) + [_ANY] * nco, out_shape=list(out_shape) + c_outs,
        scratch_shapes=scratch_shapes + sems, input_output_aliases=aliases,
        compiler_params=_params(*(("arbitrary",) * len(grid))))(*args, *c_ins)
    oo = no
    for st in stages:
        st.results = list(res[oo:oo + len(st.outs)])
        oo += len(st.outs)
    return list(res[:no])


def _dot(a, b, dims):
    return lax.dot_general(a.astype(BF16), b.astype(BF16), (dims, ((), ())), preferred_element_type=F32)


_MAX_ROW_BLOCK = 1024
_ATT_BLOCK = 512


_MAX_REDUCE_BLOCK = 2048


def _row_block(s, want=1024):
    return min(want, s, _MAX_ROW_BLOCK)


def _reduce_block(s):
    return min(s, _MAX_REDUCE_BLOCK)


def _matmul(name, grid, terms, extras, outs, epilogue, acc_shape, fill=(), summed=()):
    nt, ne, no, nf = len(terms), len(extras), len(outs), len(fill)
    nk = grid[-1]
    dims = [t[4] for t in terms]

    def body(*refs):
        a_refs, b_refs = refs[:nt], refs[nt:2 * nt]
        e_refs = refs[2 * nt:2 * nt + ne]
        o_refs = refs[2 * nt + ne + nf:2 * nt + ne + nf + no]

        def finish(acc):
            vals = epilogue(acc, *[e[...] for e in e_refs])
            for idx, (o, val) in enumerate(zip(o_refs, vals)):
                if idx in summed:
                    @pl.when(pl.program_id(0) == 0)
                    def _(o=o, val=val):
                        o[...] = val.astype(o.dtype)

                    @pl.when(pl.program_id(0) > 0)
                    def _(o=o, val=val):
                        o[...] += val.astype(o.dtype)
                else:
                    o[...] = val.astype(o.dtype)

        if nk == 1:
            part = None
            for a, b, d in zip(a_refs, b_refs, dims):
                t = _dot(a[...], b[...], d)
                part = t if part is None else part + t
            finish(part)
        else:
            acc_ref = refs[-1]
            k = pl.program_id(len(grid) - 1)

            @pl.when(k == 0)
            def _():
                acc_ref[...] = jnp.zeros_like(acc_ref)

            for a, b, d in zip(a_refs, b_refs, dims):
                acc_ref[...] += _dot(a[...], b[...], d)

            @pl.when(k == nk - 1)
            def _():
                finish(acc_ref[...])

    in_specs = [t[1] for t in terms] + [t[3] for t in terms] + [e[1] for e in extras] + [_ANY] * nf
    args = [t[0] for t in terms] + [t[2] for t in terms] + [e[0] for e in extras] + list(fill)
    sem = ("arbitrary" if summed else "parallel",) * (len(grid) - 1) + ("arbitrary",)
    aliases = {2 * nt + ne + i: i for i in range(nf)}
    return _call(
        body, name=name, grid=grid, in_specs=in_specs,
        out_specs=[o[1] for o in outs], out_shape=[o[0] for o in outs],
        scratch_shapes=[pltpu.VMEM(acc_shape, F32)] if nk > 1 else [], sem=sem, args=args, aliases=aliases)


def _ident(acc):
    return (acc,)


def _rmsnorm_fwd(name, x, gain, width, col_block=0):
    s = x.shape[0]
    bm = _row_block(s)

    def body(x_ref, g_ref, o_ref):
        xf = x_ref[...]
        r = lax.rsqrt(jnp.mean(xf * xf, axis=-1, keepdims=True) + RMS_EPS)
        o_ref[...] = ((xf * r) * g_ref[...]).astype(o_ref.dtype)

    return pl.pallas_call(
        body, name=name, grid=(s // bm,),
        in_specs=[pl.BlockSpec((bm, width), lambda i: (i, col_block)), pl.BlockSpec((1, width), lambda i: (0, 0))],
        out_specs=pl.BlockSpec((bm, width), lambda i: (i, 0)),
        out_shape=jax.ShapeDtypeStruct((s, width), BF16),
        compiler_params=_params("parallel"),
    )(x, gain)


def _rms_bwd_math(dy, xf, g, width):
    r = lax.rsqrt(jnp.mean(xf * xf, axis=-1, keepdims=True) + RMS_EPS)
    dyg = dy * g
    dot = jnp.sum(dyg * xf, axis=-1, keepdims=True)
    dx = r * dyg - xf * ((r * r * r) * (dot * (1.0 / width)))
    dgain = jnp.sum(dy * (xf * r), axis=0, keepdims=True)
    return dx, dgain


def _rmsnorm_bwd(name, dy, x, gain, width, col_block=0, dres=None, out_dtype=F32):
    s = x.shape[0]
    bm = _row_block(s)
    has_res = dres is not None

    def body(*refs):
        if has_res:
            dy_ref, x_ref, g_ref, r_ref, dx_ref, dg_ref, dxb_ref = refs
        else:
            dy_ref, x_ref, g_ref, dx_ref, dg_ref = refs
        dx, dgain = _rms_bwd_math(dy_ref[...].astype(F32), x_ref[...], g_ref[...], width)
        if has_res:
            dx = dx + r_ref[...]
            dxb_ref[...] = dx.astype(BF16)
        dx_ref[...] = dx.astype(dx_ref.dtype)

        @pl.when(pl.program_id(0) == 0)
        def _():
            dg_ref[...] = dgain

        @pl.when(pl.program_id(0) > 0)
        def _():
            dg_ref[...] += dgain

    row = pl.BlockSpec((bm, width), lambda i: (i, 0))
    in_specs = [row, pl.BlockSpec((bm, width), lambda i: (i, col_block)), pl.BlockSpec((1, width), lambda i: (0, 0))]
    args = [dy, x, gain]
    out_specs = [row, pl.BlockSpec((1, width), lambda i: (0, 0))]
    out_shape = [jax.ShapeDtypeStruct((s, width), out_dtype), jax.ShapeDtypeStruct((1, width), F32)]
    if has_res:
        in_specs.append(row)
        args.append(dres)
        out_specs.append(row)
        out_shape.append(jax.ShapeDtypeStruct((s, width), BF16))
    return _call(body, name=name, grid=(s // bm,), in_specs=in_specs, out_specs=out_specs, out_shape=out_shape,
                 sem=("arbitrary",), args=args)


def _loss_and_final_norm(h, gain, target):
    s, d = h.shape
    bm = _row_block(s, 512)

    def body(h_ref, g_ref, t_ref, dh_ref, dhb_ref, loss_ref, dg_ref):
        xf = h_ref[...]
        g = g_ref[...]
        r = lax.rsqrt(jnp.mean(xf * xf, axis=-1, keepdims=True) + RMS_EPS)
        err = (xf * r) * g - t_ref[...]
        part = 0.5 * jnp.sum(jnp.mean(err * err, axis=-1, keepdims=True), axis=0, keepdims=True)
        dx, dgain = _rms_bwd_math(err * (1.0 / d), xf, g, d)
        dh_ref[...] = dx
        dhb_ref[...] = dx.astype(BF16)

        @pl.when(pl.program_id(0) == 0)
        def _():
            dg_ref[...] = dgain
            loss_ref[...] = jnp.broadcast_to(part, loss_ref.shape)

        @pl.when(pl.program_id(0) > 0)
        def _():
            dg_ref[...] += dgain
            loss_ref[...] += jnp.broadcast_to(part, loss_ref.shape)

    row = pl.BlockSpec((bm, d), lambda i: (i, 0))
    vec = pl.BlockSpec((1, d), lambda i: (0, 0))
    return pl.pallas_call(
        body, name="loss_final_norm", grid=(s // bm,), in_specs=[row, vec, row],
        out_specs=[row, row, pl.BlockSpec((1, 128), lambda i: (0, 0)), vec],
        out_shape=[jax.ShapeDtypeStruct((s, d), F32), jax.ShapeDtypeStruct((s, d), BF16),
                   jax.ShapeDtypeStruct((1, 128), F32),
                   jax.ShapeDtypeStruct((1, d), F32)],
        compiler_params=_params("arbitrary"),
    )(h, gain, target)


def _ffn_up(name, n, wg, wu):
    s = n.shape[0]
    bm = _row_block(s)

    def body(n_ref, wg_ref, wu_ref, a_ref, dadu_ref, dadg_ref):
        x = n_ref[...]
        g = _dot(x, wg_ref[...], NT)
        u = _dot(x, wu_ref[...], NT)
        sg = jax.nn.sigmoid(g)
        silu = g * sg
        a_ref[...] = (silu * u).astype(BF16)
        dadu_ref[...] = silu.astype(BF16)
        dadg_ref[...] = (u * (sg * (1.0 + g * (1.0 - sg)))).astype(BF16)

    w_spec = pl.BlockSpec((None, FF_SHARD, D_MODEL), lambda j, i: (j, 0, 0))
    o_spec = pl.BlockSpec((None, bm, FF_SHARD), lambda j, i: (j, i, 0))
    shp = jax.ShapeDtypeStruct((N_CHIPS, s, FF_SHARD), BF16)
    return _call(
        body, name=name, grid=(N_CHIPS, s // bm),
        in_specs=[pl.BlockSpec((bm, D_MODEL), lambda j, i: (i, 0)), w_spec, w_spec],
        out_specs=[o_spec, o_spec, o_spec], out_shape=[shp, shp, shp],
        sem=("parallel", "parallel"), args=[n, wg, wu])


def _ffn1_up_gather_direct(n, g_sh, u_sh, d_sh):
    s = n.shape[0]
    bm = _row_block(s)
    nrb = s // bm
    rows, cols = g_sh.shape
    rels = (1, 2, 3)

    def body(n_ref, gs, us, ds, a_ref, dadu_ref, dadg_ref, wg, wu, wd, gbuf, ubuf, send, recv, fsend, frecv, loc, ld):
        r, i = pl.program_id(0), pl.program_id(1)
        x, y, c = lax.axis_index("x"), lax.axis_index("y"), lax.axis_index("c")
        sib = (x, y, 1 - c)
        mine, _ = _half_rows(c, rows)
        shards, fulls, bufs = (gs, us, ds), (wg, wu, wd), (gbuf, ubuf)

        def ici(k, rel, dev=sib):
            return pltpu.make_async_remote_copy(
                src_ref=shards[k].at[mine], dst_ref=fulls[k].at[rel, mine], send_sem=send.at[k, rel - 1],
                recv_sem=recv.at[k, rel - 1], device_id=dev, device_id_type=_MESH)

        def peer(rel):
            return ((1 - x) if rel & 2 else x, (1 - y) if rel & 1 else y, c)

        def fwd(k, rel):
            return pltpu.make_async_remote_copy(
                src_ref=fulls[k].at[rel, mine], dst_ref=fulls[k].at[rel, mine], send_sem=fsend.at[k, rel - 1],
                recv_sem=frecv.at[k, rel - 1], device_id=sib, device_id_type=_MESH)

        def own(k):
            return pltpu.make_async_copy(shards[k], fulls[k].at[0], loc.at[k])

        def load(k, src):
            return pltpu.make_async_copy(src, bufs[k], ld.at[k])

        @pl.when(jnp.logical_and(r == 0, i == 0))
        def _():
            for k in range(3):
                own(k).start()
            for rel in (1, 2):
                for k in (0, 1):
                    ici(k, rel, peer(rel)).start()
            for k in (0, 1):
                load(k, shards[k]).start()
            for k in (0, 1):
                load(k, shards[k]).wait()

        @pl.when(jnp.logical_and(r > 0, i == 0))
        def _():
            for k in (0, 1):
                ici(k, r).wait_recv()
                fwd(k, r).start()
            for k in (0, 1):
                fwd(k, r).wait_recv()
                load(k, fulls[k].at[r]).start()
            for k in (0, 1):
                load(k, fulls[k].at[r]).wait()

        @pl.when(jnp.logical_and(r == 1, i == 0))
        def _():
            for k in (0, 1):
                ici(k, 3, peer(3)).start()

        @pl.when(jnp.logical_and(r == 2, i == 0))
        def _():
            for rel in (1, 2):
                ici(2, rel, peer(rel)).start()

        @pl.when(jnp.logical_and(r == 3, i == 0))
        def _():
            ici(2, 3, peer(3)).start()

        xv = n_ref[...]
        g = _dot(xv, gbuf[...], NT)
        u = _dot(xv, ubuf[...], NT)
        sg = jax.nn.sigmoid(g)
        silu = g * sg
        a_ref[...] = (silu * u).astype(BF16)
        dadu_ref[...] = silu.astype(BF16)
        dadg_ref[...] = (u * (sg * (1.0 + g * (1.0 - sg)))).astype(BF16)

        @pl.when(jnp.logical_and(r == 3, i == nrb - 1))
        def _():
            for rel in rels:
                ici(2, rel).wait_recv()
                fwd(2, rel).start()
            for rel in rels:
                fwd(2, rel).wait_recv()
            for k in range(3):
                for rel in rels:
                    ici(k, rel).wait_send()
                    fwd(k, rel).wait_send()
                own(k).wait()

    o_spec = pl.BlockSpec((None, bm, FF_SHARD), lambda r, i: (r, i, 0))
    act = jax.ShapeDtypeStruct((N_CHIPS, s, FF_SHARD), BF16)
    full = jax.ShapeDtypeStruct((N_CHIPS, rows, cols), BF16)
    dma = pltpu.SemaphoreType.DMA
    if _PLAN is not None:
        _PLAN.last_slab_step = 3 * nrb
    return _call(
        body, name="ffn1_up", grid=(N_CHIPS, nrb),
        in_specs=[pl.BlockSpec((bm, D_MODEL), lambda r, i: (i, 0)), _ANY, _ANY, _ANY],
        out_specs=[o_spec, o_spec, o_spec, _ANY, _ANY, _ANY], out_shape=[act, act, act, full, full, full],
        scratch_shapes=[pltpu.VMEM((rows, cols), BF16), pltpu.VMEM((rows, cols), BF16), dma((3, 3)), dma((3, 3)),
                        dma((3, 3)), dma((3, 3)), dma((3,)), dma((2,))],
        sem=("arbitrary", "arbitrary"), args=[n, g_sh, u_sh, d_sh])


def _ffn1_up_gather(xin, gain, g_sh, u_sh, d_sh):
    s = xin.shape[0]
    bm = _row_block(s)
    nrb = s // bm
    rows, cols = g_sh.shape

    def body(x_ref, gain_ref, gs, us, ds, n_ref, a_ref, dadu_ref, dadg_ref, wg, wu, wd, gbuf, ubuf,
             send, recv, qsend, qrecv, fsend, frecv, loc, ld):
        r, i = pl.program_id(0), pl.program_id(1)
        x, y, c = lax.axis_index("x"), lax.axis_index("y"), lax.axis_index("c")
        sib = (x, y, 1 - c)
        mine, _ = _half_rows(c, rows)
        quarters = _quarter_rows(c, rows)
        shards, fulls, bufs = (gs, us, ds), (wg, wu, wd), (gbuf, ubuf)

        def remote(src, dst, ssem, rsem, dev):
            return pltpu.make_async_remote_copy(src_ref=src, dst_ref=dst, send_sem=ssem, recv_sem=rsem,
                                                device_id=dev, device_id_type=_MESH)

        def peer(rel):
            return ((1 - x) if rel & 2 else x, (1 - y) if rel & 1 else y, c)

        def ici(k, rel, dev=sib):
            return remote(shards[k].at[mine], fulls[k].at[rel, mine], send.at[k, rel - 1], recv.at[k, rel - 1], dev)

        def quarter(k, which, dev=sib):
            slab, q = ((2, quarters[0]), (1, quarters[1]))[which]
            return remote(fulls[k].at[slab, q], fulls[k].at[3, q], qsend.at[k, which], qrecv.at[k, which], dev)

        def fwd(k, rel):
            return remote(fulls[k].at[rel, mine], fulls[k].at[rel, mine], fsend.at[k, rel - 1], frecv.at[k, rel - 1], sib)

        def own(k):
            return pltpu.make_async_copy(shards[k], fulls[k].at[0], loc.at[k])

        def load(slab):
            for k in (0, 1):
                pltpu.make_async_copy(shards[k] if slab == 0 else fulls[k].at[slab], bufs[k], ld.at[k]).start()
            for k in (0, 1):
                pltpu.make_async_copy(shards[k] if slab == 0 else fulls[k].at[slab], bufs[k], ld.at[k]).wait()

        def from_neighbour(ks, rel):
            for k in ks:
                ici(k, rel).wait_recv()
                fwd(k, rel).start()
                quarter(k, 0 if rel == 2 else 1, peer(1 if rel == 2 else 2)).start()
            for k in ks:
                fwd(k, rel).wait_recv()

        def from_diagonal(ks):
            for k in ks:
                quarter(k, 0).wait_recv()
                quarter(k, 1).wait_recv()
                fwd(k, 3).start()
            for k in ks:
                fwd(k, 3).wait_recv()

        @pl.when(jnp.logical_and(r == 0, i == 0))
        def _():
            for k in range(3):
                own(k).start()
            for rel in (1, 2):
                for k in (0, 1):
                    ici(k, rel, peer(rel)).start()
            load(0)

        @pl.when(jnp.logical_and(r == 1, i == 0))
        def _():
            from_neighbour((0, 1), 1)
            load(1)
            for rel in (1, 2):
                ici(2, rel, peer(rel)).start()

        @pl.when(jnp.logical_and(r == 2, i == 0))
        def _():
            from_neighbour((0, 1), 2)
            load(2)

        @pl.when(jnp.logical_and(r == 3, i == 0))
        def _():
            from_diagonal((0, 1))
            load(3)

        xv = _norm_bf16(x_ref[...], gain_ref[...])

        @pl.when(r == 0)
        def _():
            n_ref[...] = xv

        g = _dot(xv, gbuf[...], NT)
        u = _dot(xv, ubuf[...], NT)
        sg = jax.nn.sigmoid(g)
        silu = g * sg
        a_ref[...] = (silu * u).astype(BF16)
        dadu_ref[...] = silu.astype(BF16)
        dadg_ref[...] = (u * (sg * (1.0 + g * (1.0 - sg)))).astype(BF16)

        @pl.when(jnp.logical_and(r == 3, i == nrb - 1))
        def _():
            from_neighbour((2,), 1)
            from_neighbour((2,), 2)
            from_diagonal((2,))
            for k in range(3):
                for rel in (1, 2):
                    ici(k, rel).wait_send()
                for which in (0, 1):
                    quarter(k, which).wait_send()
                for rel in (1, 2, 3):
                    fwd(k, rel).wait_send()
                own(k).wait()

    o_spec = pl.BlockSpec((None, bm, FF_SHARD), lambda r, i: (r, i, 0))
    act = jax.ShapeDtypeStruct((N_CHIPS, s, FF_SHARD), BF16)
    full = jax.ShapeDtypeStruct((N_CHIPS, rows, cols), BF16)
    dma = pltpu.SemaphoreType.DMA
    if _PLAN is not None:
        _PLAN.last_slab_step = 3 * nrb
    n_spec = pl.BlockSpec((bm, D_MODEL), lambda r, i: (jnp.where(r == 0, i, nrb - 1), 0))
    return _call(
        body, name="ffn1_up", grid=(N_CHIPS, nrb),
        in_specs=[pl.BlockSpec((bm, D_MODEL), lambda r, i: (i, 0)), pl.BlockSpec((1, D_MODEL), lambda r, i: (0, 0)),
                  _ANY, _ANY, _ANY],
        out_specs=[n_spec, o_spec, o_spec, o_spec, _ANY, _ANY, _ANY],
        out_shape=[jax.ShapeDtypeStruct((s, D_MODEL), BF16), act, act, act, full, full, full],
        scratch_shapes=[pltpu.VMEM((rows, cols), BF16), pltpu.VMEM((rows, cols), BF16), dma((3, 2)), dma((3, 2)),
                        dma((3, 2)), dma((3, 2)), dma((3, 3)), dma((3, 3)), dma((3,)), dma((2,))],
        sem=("arbitrary", "arbitrary"), args=[xin, gain, g_sh, u_sh, d_sh])


def _residual_epilogue(alpha, with_norm):
    if not with_norm:
        return lambda acc, r: (r + alpha * acc,)

    def epilogue(acc, r, g):
        h = r + alpha * acc
        rs = lax.rsqrt(jnp.mean(h * h, axis=-1, keepdims=True) + RMS_EPS)
        return h, (h * rs) * g

    return epilogue


def _residual_outs(s, bm, gain):
    row = pl.BlockSpec((bm, D_MODEL), lambda i, k: (i, 0))
    outs = [(jax.ShapeDtypeStruct((s, D_MODEL), F32), row)]
    if gain is None:
        return [], outs
    return [(gain, pl.BlockSpec((1, D_MODEL), lambda i, k: (0, 0)))], outs + [(jax.ShapeDtypeStruct((s, D_MODEL), BF16), row)]


def _loss_epilogue(acc, res, g, target):
    d = acc.shape[-1]
    h = res + 0.5 * acc
    r = lax.rsqrt(jnp.mean(h * h, axis=-1, keepdims=True) + RMS_EPS)
    err = (h * r) * g - target
    part = 0.5 * jnp.sum(jnp.mean(err * err, axis=-1, keepdims=True), axis=0, keepdims=True)
    dx, dgain = _rms_bwd_math(err * (1.0 / d), h, g, d)
    return dx, dx, jnp.broadcast_to(part, (1, 128)), dgain


def _ffn_down(name, a, wd, res, gain=None, loss=None):
    s = a.shape[1]
    bm = _row_block(s, 512)
    row = pl.BlockSpec((bm, D_MODEL), lambda i, k: (i, 0))
    terms = [(a, pl.BlockSpec((None, bm, FF_SHARD), lambda i, k, j=j: (j, i, 0)),
              wd, pl.BlockSpec((None, FF_SHARD, D_MODEL), lambda i, k, j=j: (j, 0, 0)), NN) for j in range(N_CHIPS)]
    if loss is not None:
        vec = pl.BlockSpec((1, D_MODEL), lambda i, k: (0, 0))
        outs = [(jax.ShapeDtypeStruct((s, D_MODEL), F32), row), (jax.ShapeDtypeStruct((s, D_MODEL), BF16), row),
                (jax.ShapeDtypeStruct((1, 128), F32), pl.BlockSpec((1, 128), lambda i, k: (0, 0))),
                (jax.ShapeDtypeStruct((1, D_MODEL), F32), vec)]
        return _matmul(name, (s // bm, 1), terms, [(res, row), (loss[0], vec), (loss[1], row)], outs,
                       _loss_epilogue, None, summed=(2, 3))
    extras, outs = _residual_outs(s, bm, gain)
    res_out = _matmul(name, (s // bm, 1), terms, [(res, row)] + extras, outs,
                      _residual_epilogue(0.5, gain is not None), None)
    return res_out if gain is not None else res_out[0]


def _norm_bwd_epilogue(width):
    def epilogue(acc, h, g, dres):
        dx, dgain = _rms_bwd_math(acc, h, g, width)
        dx = dx + dres
        return dx, dx, dgain

    return epilogue


def _norm_bwd_operands(s, bm, h, gain, dres):
    row = pl.BlockSpec((bm, D_MODEL), lambda i, k: (i, 0))
    vec = pl.BlockSpec((1, D_MODEL), lambda i, k: (0, 0))
    extras = [(h, row), (gain, vec), (dres, row)]
    outs = [(jax.ShapeDtypeStruct((s, D_MODEL), F32), row), (jax.ShapeDtypeStruct((s, D_MODEL), BF16), row),
            (jax.ShapeDtypeStruct((1, D_MODEL), F32), vec)]
    return extras, outs, (2,)


def _ffn_bwd(tag, dh, n, dadg, dadu, a, wg, wu, wd, grads, norm_bwd=None):
    s = dh.shape[0]
    bm = _row_block(s)
    bk = _reduce_block(s)
    nk = s // bk

    def act_bwd(acc, dg_da, du_da):
        da = 0.5 * acc
        return da * dg_da.astype(F32), da * du_da.astype(F32)

    slab = pl.BlockSpec((None, bm, FF_SHARD), lambda j, i, k: (j, i, 0))
    shp = jax.ShapeDtypeStruct((N_CHIPS, s, FF_SHARD), BF16)
    dg, du = _matmul(
        tag + "_dact", (N_CHIPS, s // bm, 1),
        [(dh, pl.BlockSpec((bm, D_MODEL), lambda j, i, k: (i, 0)),
          wd, pl.BlockSpec((None, FF_SHARD, D_MODEL), lambda j, i, k: (j, 0, 0)), NT)],
        [(dadg, slab), (dadu, slab)], [(shp, slab), (shp, slab)], act_bwd, None)

    grads[tag + "_w_down"] = _matmul(
        tag + "_dwd", (N_CHIPS, nk),
        [(a, pl.BlockSpec((None, bk, FF_SHARD), lambda j, k: (j, k, 0)),
          dh, pl.BlockSpec((bk, D_MODEL), lambda j, k: (k, 0)), TN)],
        [], [(jax.ShapeDtypeStruct((N_CHIPS, FF_SHARD, D_MODEL), BF16),
              pl.BlockSpec((None, FF_SHARD, D_MODEL), lambda j, k: (j, 0, 0)))],
        lambda acc: (0.5 * acc,), (FF_SHARD, D_MODEL))[0]

    def dw_up(nm, dact):
        return _matmul(
            nm, (N_CHIPS, nk),
            [(dact, pl.BlockSpec((None, bk, FF_SHARD), lambda j, k: (j, k, 0)),
              n, pl.BlockSpec((bk, D_MODEL), lambda j, k: (k, 0)), TN)],
            [], [(jax.ShapeDtypeStruct((N_CHIPS, FF_SHARD, D_MODEL), BF16),
                  pl.BlockSpec((None, FF_SHARD, D_MODEL), lambda j, k: (j, 0, 0)))],
            _ident, (FF_SHARD, D_MODEL))[0]

    grads[tag + "_w_gate"] = dw_up(tag + "_dwg", dg)
    grads[tag + "_w_up"] = dw_up(tag + "_dwu", du)

    bn = _row_block(s, 512)
    steps = s // bn // 2
    prev, dgain = (), None
    for part, off in (("_dn_a", 0), ("_dn_b", steps)):
        row = pl.BlockSpec((bn, D_MODEL), lambda i, k, off=off: (i + off, 0))
        terms = []
        for j in range(N_CHIPS):
            a_slab = pl.BlockSpec((None, bn, FF_SHARD), lambda i, k, j=j, off=off: (j, i + off, 0))
            w_slab = pl.BlockSpec((None, FF_SHARD, D_MODEL), lambda i, k, j=j: (j, 0, 0))
            terms += [(dg, a_slab, wg, w_slab, NN), (du, a_slab, wu, w_slab, NN)]
        if norm_bwd is None:
            prev = _matmul(tag + part, (steps, 1), terms, [], [(jax.ShapeDtypeStruct((s, D_MODEL), F32), row)],
                           _ident, None, fill=prev)
            continue
        h, gain, dres = norm_bwd
        vec = pl.BlockSpec((1, D_MODEL), lambda i, k: (0, 0))
        res = _matmul(
            tag + part, (steps, 1), terms, [(h, row), (gain, vec), (dres, row)],
            [(jax.ShapeDtypeStruct((s, D_MODEL), F32), row), (jax.ShapeDtypeStruct((s, D_MODEL), BF16), row),
             (jax.ShapeDtypeStruct((1, D_MODEL), F32), vec)],
            _norm_bwd_epilogue(D_MODEL), None, fill=prev, summed=(2,))
        prev = res[:2]
        dgain = res[2] if dgain is None else dgain + res[2]
    return prev[0] if norm_bwd is None else (prev[0], prev[1], dgain)


def _mm_nn(name, a, b, out_dtype, res=None, gain=None):
    s, k = a.shape
    nn = b.shape[1]
    bm = _row_block(s)
    row = pl.BlockSpec((bm, nn), lambda i, kk: (i, 0))
    term = [(a, pl.BlockSpec((bm, k), lambda i, kk: (i, 0)), b, pl.BlockSpec((k, nn), lambda i, kk: (0, 0)), NN)]
    if res is None:
        return _matmul(name, (s // bm, 1), term, [], [(jax.ShapeDtypeStruct((s, nn), out_dtype), row)], _ident, None)[0]
    extras, outs = _residual_outs(s, bm, gain)
    res_out = _matmul(name, (s // bm, 1), term, [(res, row)] + extras, outs,
                      _residual_epilogue(1.0, gain is not None), None)
    return res_out if gain is not None else res_out[0]


def _mm_nt(name, a, b, out_dtype, attn_out=None, nh=0, dv=0):
    s, nn = a.shape
    k = b.shape[0]
    bm = _row_block(s)
    term = [(a, pl.BlockSpec((bm, nn), lambda i, kk: (i, 0)), b, pl.BlockSpec((k, nn), lambda i, kk: (0, 0)), NT)]
    out = (jax.ShapeDtypeStruct((s, k), out_dtype), pl.BlockSpec((bm, k), lambda i, kk: (i, 0)))
    if attn_out is None:
        return _matmul(name, (s // bm, 1), term, [], [out], _ident, None)[0]

    def with_delta(acc, o):
        do = acc.astype(out_dtype).astype(F32)
        cols = [jnp.sum(do[:, h * dv:(h + 1) * dv] * o[:, h * dv:(h + 1) * dv].astype(F32), axis=-1, keepdims=True)
                for h in range(nh)]
        return acc, jnp.stack(cols, axis=0)

    return _matmul(
        name, (s // bm, 1), term, [(attn_out, pl.BlockSpec((bm, nh * dv), lambda i, kk: (i, 0)))],
        [out, (jax.ShapeDtypeStruct((nh, s, 1), F32), pl.BlockSpec((nh, bm, 1), lambda i, kk: (0, i, 0)))],
        with_delta, None)


def _mm_nt_norm_bwd(name, a, b, h, gain, dres):
    s, nn = a.shape
    bm = _row_block(s, 512)
    extras, outs, summed = _norm_bwd_operands(s, bm, h, gain, dres)
    return _matmul(
        name, (s // bm, 1),
        [(a, pl.BlockSpec((bm, nn), lambda i, kk: (i, 0)), b, pl.BlockSpec(b.shape, lambda i, kk: (0, 0)), NT)],
        extras, outs, _norm_bwd_epilogue(D_MODEL), None, summed=summed)


def _w_in_dx_norm_bwd(dz, w_t, h, gain, dres):
    s = dz.shape[0]
    bm = _row_block(s, 512)
    epilogue = _norm_bwd_epilogue(D_MODEL)

    def body(dz_ref, w_ref, h_ref, g_ref, r_ref, dx_ref, dxb_ref, dg_ref):
        dzv = dz_ref[...]
        dn = jnp.concatenate([_dot(dzv, w_ref[j], NN) for j in range(N_CHIPS)], axis=1)
        dx, _, dgain = epilogue(dn, h_ref[...], g_ref[...], r_ref[...])
        dx_ref[...] = dx
        dxb_ref[...] = dx.astype(BF16)

        @pl.when(pl.program_id(0) == 0)
        def _():
            dg_ref[...] = dgain

        @pl.when(pl.program_id(0) > 0)
        def _():
            dg_ref[...] += dgain

    row = pl.BlockSpec((bm, D_MODEL), lambda i: (i, 0))
    vec = pl.BlockSpec((1, D_MODEL), lambda i: (0, 0))
    return _call(
        body, name="w_in_dx", grid=(s // bm,),
        in_specs=[row, pl.BlockSpec(w_t.shape, lambda i: (0, 0, 0)), row, vec, row],
        out_specs=[row, row, vec],
        out_shape=[jax.ShapeDtypeStruct((s, D_MODEL), F32), jax.ShapeDtypeStruct((s, D_MODEL), BF16),
                   jax.ShapeDtypeStruct((1, D_MODEL), F32)],
        sem=("arbitrary",), args=[dz, w_t, h, gain, dres])


def _mm_tn(name, a, b, out_dtype=BF16):
    s, k = a.shape
    nn = b.shape[1]
    bk = _reduce_block(s)
    return _matmul(
        name, (s // bk,),
        [(a, pl.BlockSpec((bk, k), lambda kk: (kk, 0)), b, pl.BlockSpec((bk, nn), lambda kk: (kk, 0)), TN)],
        [], [(jax.ShapeDtypeStruct((k, nn), out_dtype), pl.BlockSpec((k, nn), lambda kk: (0, 0)))],
        _ident, (k, nn))[0]


def _mm_heads_fwd(name, a, w, out_dtype, w_transposed=False):
    s, k = a.shape
    nh = w.shape[0]
    nn = w.shape[1] if w_transposed else w.shape[2]
    bm = _row_block(s)
    return _matmul(
        name, (nh, s // bm, 1),
        [(a, pl.BlockSpec((bm, k), lambda h, i, kk: (i, 0)),
          w, pl.BlockSpec((None,) + w.shape[1:], lambda h, i, kk: (h, 0, 0)), NT if w_transposed else NN)],
        [], [(jax.ShapeDtypeStruct((s, nh * nn), out_dtype), pl.BlockSpec((bm, nn), lambda h, i, kk: (i, h)))],
        _ident, None)[0]


def _mm_heads_bwd(name, dy, a, w, w_transposed=False):
    s, k = a.shape
    nh = w.shape[0]
    nn = w.shape[1] if w_transposed else w.shape[2]
    bm = _row_block(s)
    bk = _reduce_block(s)
    w_spec = pl.BlockSpec((None,) + w.shape[1:], lambda i, h: (h, 0, 0))
    da = _matmul(
        name + "_dx", (s // bm, nh),
        [(dy, pl.BlockSpec((bm, nn), lambda i, h: (i, h)), w, w_spec, NN if w_transposed else NT)],
        [], [(jax.ShapeDtypeStruct((s, k), F32), pl.BlockSpec((bm, k), lambda i, h: (i, 0)))], _ident, (bm, k))[0]
    a_term = (a, pl.BlockSpec((bk, k), lambda h, kk: (kk, 0)))
    dy_term = (dy, pl.BlockSpec((bk, nn), lambda h, kk: (kk, h)))
    lhs, rhs = (dy_term, a_term) if w_transposed else (a_term, dy_term)
    dw = _matmul(
        name + "_dw", (nh, s // bk), [lhs + rhs + (TN,)],
        [], [(jax.ShapeDtypeStruct(w.shape, BF16), pl.BlockSpec((None,) + w.shape[1:], lambda h, kk: (h, 0, 0)))],
        _ident, w.shape[1:])[0]
    return da, dw


def _w_in_fwd(n, w_t):
    s = n.shape[0]
    bm = _row_block(s)
    nh, nout, kin = w_t.shape
    terms = [(n, pl.BlockSpec((bm, kin), lambda i, k, j=j: (i, j)),
              w_t, pl.BlockSpec((None, nout, kin), lambda i, k, j=j: (j, 0, 0)), NT) for j in range(nh)]
    row = pl.BlockSpec((bm, nout), lambda i, k: (i, 0))
    return _matmul("w_in", (s // bm, 1), terms, [], [(jax.ShapeDtypeStruct((s, nout), F32), row)], _ident, None)[0]


def _w_in_dw(dz, n):
    s, nout = dz.shape
    kin = n.shape[1] // N_CHIPS
    bk = _reduce_block(s)
    return _matmul(
        "w_in_dw", (N_CHIPS, s // bk),
        [(dz, pl.BlockSpec((bk, nout), lambda j, k: (k, 0)), n, pl.BlockSpec((bk, kin), lambda j, k: (k, j)), TN)],
        [], [(jax.ShapeDtypeStruct((N_CHIPS, nout, kin), BF16), pl.BlockSpec((None, nout, kin), lambda j, k: (j, 0, 0)))],
        _ident, (nout, kin))[0]


def _rope_tables(positions):
    half = ROPE_DIM // 2
    freqs = 1.0 / (ROPE_BASE ** (jnp.arange(0, ROPE_DIM, 2, dtype=F32) / ROPE_DIM))
    ang = positions.astype(F32)[:, None] * freqs
    cos, sin = jnp.cos(ang), jnp.sin(ang)
    z = jnp.zeros_like(cos)
    tc = jnp.concatenate([cos, cos, z, z], axis=-1)
    ta = jnp.concatenate([-sin, z, z, z], axis=-1)
    tb = jnp.concatenate([z, sin, z, z], axis=-1)
    assert tc.shape[-1] == 4 * half
    return tc, ta, tb


def _rope(x, tc, ta, tb):
    return x * tc + pltpu.roll(x, 96, 1) * ta + pltpu.roll(x, 32, 1) * tb


def _rope_t(dy, tc, ta, tb):
    return dy * tc + pltpu.roll(dy * ta, 32, 1) + pltpu.roll(dy * tb, 96, 1)


def _q_rope(q, tc, ta, tb, transpose):
    s = q.shape[0]
    bm = _row_block(s, 512)
    rot = _rope_t if transpose else _rope

    def body(q_ref, tc_ref, ta_ref, tb_ref, o_ref):
        c, a, b = tc_ref[...], ta_ref[...], tb_ref[...]
        for h in range(MLA_HEADS):
            lo = h * HEAD_QK
            o_ref[:, lo:lo + 128] = q_ref[:, lo:lo + 128].astype(BF16)
            o_ref[:, lo + 128:lo + 256] = rot(q_ref[:, lo + 128:lo + 256], c, a, b).astype(BF16)

    row = pl.BlockSpec((bm, MLA_HEADS * HEAD_QK), lambda i: (i, 0))
    tab = pl.BlockSpec((bm, 128), lambda i: (i, 0))
    return pl.pallas_call(
        body, name="q_rope_t" if transpose else "q_rope", grid=(s // bm,), in_specs=[row, tab, tab, tab],
        out_specs=row, out_shape=jax.ShapeDtypeStruct((s, MLA_HEADS * HEAD_QK), BF16),
        compiler_params=_params("parallel"),
    )(q, tc, ta, tb)


def _kv_assemble(kv, z, tc, ta, tb):
    s = kv.shape[0]
    bm = _row_block(s, 512)

    def body(kv_ref, kr_ref, tc_ref, ta_ref, tb_ref, k_ref, v_ref):
        kpe = _rope(kr_ref[...], tc_ref[...], ta_ref[...], tb_ref[...]).astype(BF16)
        for h in range(MLA_HEADS):
            lo = h * 256
            k_ref[:, lo:lo + 128] = kv_ref[:, lo:lo + 128].astype(BF16)
            k_ref[:, lo + 128:lo + 256] = kpe
            v_ref[:, h * 128:(h + 1) * 128] = kv_ref[:, lo + 128:lo + 256].astype(BF16)

    row = pl.BlockSpec((bm, 1024), lambda i: (i, 0))
    tab = pl.BlockSpec((bm, 128), lambda i: (i, 0))
    return pl.pallas_call(
        body, name="kv_assemble", grid=(s // bm,),
        in_specs=[row, pl.BlockSpec((bm, 128), lambda i: (i, 3)), tab, tab, tab],
        out_specs=[row, pl.BlockSpec((bm, 512), lambda i: (i, 0))],
        out_shape=[jax.ShapeDtypeStruct((s, 1024), BF16), jax.ShapeDtypeStruct((s, 512), BF16)],
        compiler_params=_params("parallel"),
    )(kv, z, tc, ta, tb)


def _kv_assemble_bwd(dk, dv, tc, ta, tb):
    s = dk.shape[0]
    bm = _row_block(s, 512)

    def body(dk_ref, dv_ref, tc_ref, ta_ref, tb_ref, dkv_ref, dkr_ref):
        dpe = None
        for h in range(MLA_HEADS):
            lo = h * 256
            dkv_ref[:, lo:lo + 128] = dk_ref[:, lo:lo + 128].astype(BF16)
            dkv_ref[:, lo + 128:lo + 256] = dv_ref[:, h * 128:(h + 1) * 128].astype(BF16)
            t = dk_ref[:, lo + 128:lo + 256]
            dpe = t if dpe is None else dpe + t
        dkr_ref[...] = _rope_t(dpe, tc_ref[...], ta_ref[...], tb_ref[...])

    row = pl.BlockSpec((bm, 1024), lambda i: (i, 0))
    tab = pl.BlockSpec((bm, 128), lambda i: (i, 0))
    return pl.pallas_call(
        body, name="kv_assemble_bwd", grid=(s // bm,),
        in_specs=[row, pl.BlockSpec((bm, 512), lambda i: (i, 0)), tab, tab, tab],
        out_specs=[row, tab],
        out_shape=[jax.ShapeDtypeStruct((s, 1024), BF16), jax.ShapeDtypeStruct((s, 128), F32)],
        compiler_params=_params("parallel"),
    )(dk, dv, tc, ta, tb)


def _norm_bf16(x, g):
    r = lax.rsqrt(jnp.mean(x * x, axis=-1, keepdims=True) + RMS_EPS)
    return ((x * r) * g).astype(BF16)


def _qkv_prep(z, q_gain, kv_gain, wq_t, wkv, tc, ta, tb):
    s = z.shape[0]
    bm = _row_block(s, 512)

    def body(zq_ref, zkv_ref, zkr_ref, qg_ref, kvg_ref, wq_ref, wkv_ref, tc_ref, ta_ref, tb_ref,
             qn_ref, kvn_ref, q_ref, k_ref, v_ref):
        c, a, b = tc_ref[...], ta_ref[...], tb_ref[...]
        qn = _norm_bf16(zq_ref[...], qg_ref[...])
        kvn = _norm_bf16(zkv_ref[...], kvg_ref[...])
        qn_ref[...] = qn
        kvn_ref[...] = kvn
        kpe = _rope(zkr_ref[...], c, a, b).astype(BF16)
        for h in range(MLA_HEADS):
            lo = h * HEAD_QK
            qp = _dot(qn, wq_ref[h], NT)
            q_ref[:, lo:lo + 128] = qp[:, :128].astype(BF16)
            q_ref[:, lo + 128:lo + 256] = _rope(qp[:, 128:], c, a, b).astype(BF16)
            kv = _dot(kvn, wkv_ref[h], NN)
            k_ref[:, lo:lo + 128] = kv[:, :128].astype(BF16)
            k_ref[:, lo + 128:lo + 256] = kpe
            v_ref[:, h * HEAD_V:(h + 1) * HEAD_V] = kv[:, 128:].astype(BF16)

    def cols(width, blk):
        return pl.BlockSpec((bm, width), lambda i: (i, blk))

    def whole(a):
        return pl.BlockSpec(a.shape, lambda i: (0,) * a.ndim)

    tab = cols(128, 0)
    return _call(
        body, name="qkv_prep", grid=(s // bm,),
        in_specs=[cols(Q_LORA, 0), cols(KV_LORA, 2), cols(128, 3), whole(q_gain), whole(kv_gain), whole(wq_t),
                  whole(wkv), tab, tab, tab],
        out_specs=[cols(Q_LORA, 0), cols(KV_LORA, 0), cols(1024, 0), cols(1024, 0), cols(512, 0)],
        out_shape=[jax.ShapeDtypeStruct((s, Q_LORA), BF16), jax.ShapeDtypeStruct((s, KV_LORA), BF16),
                   jax.ShapeDtypeStruct((s, 1024), BF16), jax.ShapeDtypeStruct((s, 1024), BF16),
                   jax.ShapeDtypeStruct((s, 512), BF16)],
        sem=("parallel",), args=[z, z, z, q_gain, kv_gain, wq_t, wkv, tc, ta, tb])


def _qkv_prep_bwd(dq, dk, dv, z, qn, kvn, q_gain, kv_gain, wq_t, wkv, tc, ta, tb):
    s = z.shape[0]
    bm = _row_block(s, 512)
    nsteps = s // bm

    def body(dq_ref, dk_ref, dv_ref, zq_ref, zkv_ref, qn_ref, kvn_ref, qg_ref, kvg_ref, wq_ref, wkv_ref,
             tc_ref, ta_ref, tb_ref, dz_ref, dqg_ref, dkvg_ref, dwq_ref, dwkv_ref, wq_acc, wkv_acc):
        i = pl.program_id(0)
        c, a, b = tc_ref[...], ta_ref[...], tb_ref[...]

        @pl.when(i == 0)
        def _():
            wq_acc[...] = jnp.zeros_like(wq_acc)
            wkv_acc[...] = jnp.zeros_like(wkv_acc)

        qn, kvn = qn_ref[...], kvn_ref[...]
        dqn = jnp.zeros((bm, Q_LORA), F32)
        dkvn = jnp.zeros((bm, KV_LORA), F32)
        dpe = jnp.zeros((bm, 128), F32)
        for h in range(MLA_HEADS):
            lo = h * HEAD_QK
            dqp = jnp.concatenate([dq_ref[:, lo:lo + 128].astype(BF16),
                                   _rope_t(dq_ref[:, lo + 128:lo + 256], c, a, b).astype(BF16)], axis=1)
            dqn = dqn + _dot(dqp, wq_ref[h], NN)
            wq_acc[h] += _dot(dqp, qn, TN)
            dkv = jnp.concatenate([dk_ref[:, lo:lo + 128].astype(BF16),
                                   dv_ref[:, h * HEAD_V:(h + 1) * HEAD_V].astype(BF16)], axis=1)
            dkvn = dkvn + _dot(dkv, wkv_ref[h], NT)
            wkv_acc[h] += _dot(kvn, dkv, TN)
            dpe = dpe + dk_ref[:, lo + 128:lo + 256]
        dcq, dqg = _rms_bwd_math(dqn, zq_ref[...], qg_ref[...], Q_LORA)
        dckv, dkvg = _rms_bwd_math(dkvn, zkv_ref[...], kvg_ref[...], KV_LORA)
        dz_ref[:, 0:Q_LORA] = dcq.astype(BF16)
        dz_ref[:, Q_LORA:Q_LORA + KV_LORA] = dckv.astype(BF16)
        dz_ref[:, Q_LORA + KV_LORA:512] = _rope_t(dpe, c, a, b).astype(BF16)

        @pl.when(i == 0)
        def _():
            dqg_ref[...] = dqg
            dkvg_ref[...] = dkvg

        @pl.when(i > 0)
        def _():
            dqg_ref[...] += dqg
            dkvg_ref[...] += dkvg

        @pl.when(i == nsteps - 1)
        def _():
            dwq_ref[...] = wq_acc[...].astype(BF16)
            dwkv_ref[...] = wkv_acc[...].astype(BF16)

    def cols(width, blk):
        return pl.BlockSpec((bm, width), lambda i: (i, blk))

    def whole(shape):
        return pl.BlockSpec(shape, lambda i: (0,) * len(shape))

    tab = cols(128, 0)
    return _call(
        body, name="qkv_prep_bwd", grid=(nsteps,),
        in_specs=[cols(1024, 0), cols(1024, 0), cols(512, 0), cols(Q_LORA, 0), cols(KV_LORA, 2), cols(Q_LORA, 0),
                  cols(KV_LORA, 0), whole(q_gain.shape), whole(kv_gain.shape), whole(wq_t.shape), whole(wkv.shape),
                  tab, tab, tab],
        out_specs=[cols(512, 0), whole(q_gain.shape), whole(kv_gain.shape), whole(wq_t.shape), whole(wkv.shape)],
        out_shape=[jax.ShapeDtypeStruct((s, 512), BF16), jax.ShapeDtypeStruct(q_gain.shape, F32),
                   jax.ShapeDtypeStruct(kv_gain.shape, F32), jax.ShapeDtypeStruct(wq_t.shape, BF16),
                   jax.ShapeDtypeStruct(wkv.shape, BF16)],
        scratch_shapes=[pltpu.VMEM(wq_t.shape, F32), pltpu.VMEM(wkv.shape, F32)],
        sem=("arbitrary",), args=[dq, dk, dv, z, z, qn, kvn, q_gain, kv_gain, wq_t, wkv, tc, ta, tb])


def _causal_mask(s, row0, col0):
    rows = row0 + lax.broadcasted_iota(jnp.int32, s.shape, 0)
    cols = col0 + lax.broadcasted_iota(jnp.int32, s.shape, 1)
    return jnp.where(cols <= rows, s, -jnp.inf)


def _attn_fwd(name, q, k, k_off, v, v_off, nh, dq, dv, scale, causal, blk):
    sq, sk = q.shape[0], k.shape[0]
    bq = min(blk, sq)
    bk = min(blk, sk)
    nkv = sk // bk
    assert not causal or (sq == sk and bq == bk)

    hq = bq
    log2e = 1.4426950408889634
    c2 = scale * log2e

    def body(q_ref, k_ref, v_ref, o_ref, lse_ref):
        qi = pl.program_id(1)
        qs = (q_ref[...],)

        def step(j, carry, masked):
            rows = pl.ds(pl.multiple_of(j * bk, bk), bk)
            kb, vb = k_ref[rows, :], v_ref[rows, :]
            out = []
            for t, (m, l, acc) in enumerate(carry):
                s = _dot(qs[t], kb, NT) * c2
                if masked:
                    s = _causal_mask(s, qi * bq + t * hq, j * bk)
                m_new = jnp.maximum(m, jnp.max(s, axis=-1, keepdims=True))
                alpha = jnp.exp2(m - m_new)
                p = jnp.exp2(s - m_new)
                l = alpha * l + jnp.sum(p, axis=-1, keepdims=True)
                acc = alpha * acc + _dot(p, vb, NN)
                out.append((m_new, l, acc))
            return tuple(out)

        one = (jnp.full((hq, 1), -jnp.inf, F32), jnp.zeros((hq, 1), F32), jnp.zeros((hq, dv), F32))
        init = (one,)
        if causal:
            carry = lax.fori_loop(0, qi, lambda j, c: step(j, c, False), init)
            fin = step(qi, carry, True)
        else:
            fin = lax.fori_loop(0, nkv, lambda j, c: step(j, c, False), init)
        for t, (m, l, acc) in enumerate(fin):
            o_ref[t * hq:(t + 1) * hq, :] = (acc / l).astype(o_ref.dtype)
            lse_ref[t * hq:(t + 1) * hq, :] = m * (1.0 / log2e) + jnp.log(l)

    return _call(
        body, name=name, grid=(nh, sq // bq),
        in_specs=[pl.BlockSpec((bq, dq), lambda h, i: (i, h)),
                  pl.BlockSpec((sk, dq), lambda h, i: (0, k_off + h)),
                  pl.BlockSpec((sk, dv), lambda h, i: (0, v_off + h))],
        out_specs=[pl.BlockSpec((bq, dv), lambda h, i: (i, h)), pl.BlockSpec((None, bq, 1), lambda h, i: (h, i, 0))],
        out_shape=[jax.ShapeDtypeStruct((sq, nh * dv), BF16), jax.ShapeDtypeStruct((nh, sq, 1), F32)],
        sem=("parallel", "parallel"), args=[q, k, v])


def _attn_delta(name, do, do_off, o, nh, dv):
    s = o.shape[0]
    bm = _row_block(s, 512)

    def body(do_ref, o_ref, d_ref):
        d_ref[...] = jnp.sum(do_ref[...].astype(F32) * o_ref[...].astype(F32), axis=-1, keepdims=True)

    return pl.pallas_call(
        body, name=name, grid=(nh, s // bm),
        in_specs=[pl.BlockSpec((bm, dv), lambda h, i: (i, do_off + h)), pl.BlockSpec((bm, dv), lambda h, i: (i, h))],
        out_specs=pl.BlockSpec((None, bm, 1), lambda h, i: (h, i, 0)),
        out_shape=jax.ShapeDtypeStruct((nh, s, 1), F32),
        compiler_params=_params("parallel", "parallel"),
    )(do, o)


def _attn_bwd(name, q, k, k_off, v, v_off, do, do_off, lse, delta, nh, dq, dv, scale, causal, blk):
    sq, sk = q.shape[0], k.shape[0]
    bq = min(blk, sq)
    bk = min(blk, sk)
    nq = sq // bq
    assert not causal or (sq == sk and bq == bk)

    def body(q_ref, k_ref, v_ref, do_ref, lse_ref, dl_ref, dq_ref, dk_ref, dv_ref, dk_acc, dv_acc):
        j = pl.program_id(1)

        @pl.when(j == 0)
        def _():
            dq_ref[...] = jnp.zeros_like(dq_ref)

        dk_acc[...] = jnp.zeros_like(dk_acc)
        dv_acc[...] = jnp.zeros_like(dv_acc)
        kv = k_ref[...]
        vv = v_ref[...]

        def step(i, masked):
            rows = pl.ds(pl.multiple_of(i * bq, bq), bq)
            qv = q_ref[rows, :]
            dov = do_ref[rows, :].astype(BF16)
            s = _dot(qv, kv, NT) * scale
            if masked:
                s = _causal_mask(s, i * bq, j * bk)
            p = jnp.exp(s - lse_ref[rows, :])
            dp = _dot(dov, vv, NT)
            ds = (p * (dp - dl_ref[rows, :]) * scale).astype(BF16)
            dv_acc[...] += _dot(p, dov, TN)
            dk_acc[...] += _dot(ds, qv, TN)
            dq_ref[rows, :] += _dot(ds, kv, NN)

        if causal:
            step(j, True)

            def loop(i, c):
                step(i, False)
                return c

            lax.fori_loop(j + 1, nq, loop, 0)
        else:
            def loop(i, c):
                step(i, False)
                return c

            lax.fori_loop(0, nq, loop, 0)
        dk_ref[...] = dk_acc[...]
        dv_ref[...] = dv_acc[...]

    stat = pl.BlockSpec((None, sq, 1), lambda h, j: (h, 0, 0))
    return _call(
        body, name=name, grid=(nh, sk // bk),
        in_specs=[pl.BlockSpec((sq, dq), lambda h, j: (0, h)),
                  pl.BlockSpec((bk, dq), lambda h, j: (j, k_off + h)),
                  pl.BlockSpec((bk, dv), lambda h, j: (j, v_off + h)),
                  pl.BlockSpec((sq, dv), lambda h, j: (0, do_off + h)), stat, stat],
        out_specs=[pl.BlockSpec((sq, dq), lambda h, j: (0, h)),
                   pl.BlockSpec((bk, dq), lambda h, j: (j, h)),
                   pl.BlockSpec((bk, dv), lambda h, j: (j, h))],
        out_shape=[jax.ShapeDtypeStruct((sq, nh * dq), F32), jax.ShapeDtypeStruct((sk, nh * dq), F32),
                   jax.ShapeDtypeStruct((sk, nh * dv), F32)],
        scratch_shapes=[pltpu.VMEM((bk, dq), F32), pltpu.VMEM((bk, dv), F32)],
        sem=("parallel", "arbitrary"), args=[q, k, v, do, lse, delta])


def _pool_diff(z, g):
    s = z.shape[0]
    t = lax.broadcasted_iota(jnp.int32, z.shape, 0)
    acc = z
    sums = []
    for k in (1, 2, 4, 8):
        acc = acc + jnp.where(t >= k, pltpu.roll(acc, k, 0), 0.0)
        sums.append(acc)
    win = jnp.where(g == 0, sums[0], jnp.where(g == 1, sums[1], jnp.where(g == 2, sums[2], sums[3])))
    w = lax.shift_left(jnp.int32(2), g)
    count = jnp.minimum(t + 1, w).astype(F32)
    del s
    return win / count - z, count


def _pool_fwd(z, pool_w, pool_scale):
    s = z.shape[0]

    def body(z_ref, w_ref, sc_ref, o_ref):
        diff, _ = _pool_diff(z_ref[...], pl.program_id(0))
        o_ref[...] = (_dot(diff, w_ref[...], NN) * sc_ref[...]).astype(o_ref.dtype)

    return _call(
        body, name="pool_fwd", grid=(POOL_GROUPS,),
        in_specs=[pl.BlockSpec((s, POOL_CH), lambda g: (0, 4 + g)),
                  pl.BlockSpec((None, POOL_CH, POOL_CH), lambda g: (g, 0, 0)),
                  pl.BlockSpec((1, POOL_CH), lambda g: (0, g))],
        out_specs=[pl.BlockSpec((s, POOL_CH), lambda g: (0, g))],
        out_shape=[jax.ShapeDtypeStruct((s, POOL_GROUPS * POOL_CH), BF16)],
        sem=("parallel",), args=[z, pool_w, pool_scale])[0]


def _pool_bwd(dcat, z, pool_w, pool_scale):
    s = z.shape[0]

    def body(dp_ref, z_ref, w_ref, sc_ref, dz_ref, dw_ref, dsc_ref):
        g = pl.program_id(0)
        diff, count = _pool_diff(z_ref[...], g)
        dpf = dp_ref[...].astype(F32)
        u = _dot(diff, w_ref[...], NN)
        dsc_ref[...] = jnp.sum(dpf * u, axis=0, keepdims=True)
        du = (dpf * sc_ref[...]).astype(BF16)
        dw_ref[...] = _dot(diff, du, TN)
        ddiff = _dot(du, w_ref[...], NT)
        t = lax.broadcasted_iota(jnp.int32, ddiff.shape, 0)
        acc = ddiff / count
        sums = []
        for k in (1, 2, 4, 8):
            acc = acc + jnp.where(t < s - k, pltpu.roll(acc, s - k, 0), 0.0)
            sums.append(acc)
        win = jnp.where(g == 0, sums[0], jnp.where(g == 1, sums[1], jnp.where(g == 2, sums[2], sums[3])))
        dz_ref[...] = win - ddiff

    return pl.pallas_call(
        body, name="pool_bwd", grid=(POOL_GROUPS,),
        in_specs=[pl.BlockSpec((s, POOL_CH), lambda g: (0, 4 + g)),
                  pl.BlockSpec((s, POOL_CH), lambda g: (0, 4 + g)),
                  pl.BlockSpec((None, POOL_CH, POOL_CH), lambda g: (g, 0, 0)),
                  pl.BlockSpec((1, POOL_CH), lambda g: (0, g))],
        out_specs=[pl.BlockSpec((s, POOL_CH), lambda g: (0, g)),
                   pl.BlockSpec((None, POOL_CH, POOL_CH), lambda g: (g, 0, 0)),
                   pl.BlockSpec((1, POOL_CH), lambda g: (0, g))],
        out_shape=[jax.ShapeDtypeStruct((s, POOL_GROUPS * POOL_CH), F32),
                   jax.ShapeDtypeStruct((POOL_GROUPS, POOL_CH, POOL_CH), F32),
                   jax.ShapeDtypeStruct((1, POOL_GROUPS * POOL_CH), F32)],
        compiler_params=_params("parallel"),
    )(dcat, z, pool_w, pool_scale)


def _local_step(x, mem, positions, target, w, grads):
    tc, ta, tb = _rope_tables(positions)
    blk = _ATT_BLOCK

    if "ffn1_shards" in w:
        n1, a1, dadu1, dadg1, w["ffn1_w_gate"], w["ffn1_w_up"], w["ffn1_w_down"] = _ffn1_up_gather(
            x, w["ffn1_norm"], *w["ffn1_shards"])
    else:
        n1 = _rmsnorm_fwd("ffn1_norm", x, w["ffn1_norm"], D_MODEL)
        a1, dadu1, dadg1 = _ffn_up("ffn1_up", n1, w["ffn1_w_gate"], w["ffn1_w_up"])
    h1, n2 = _ffn_down("ffn1_down", a1, w["ffn1_w_down"], x, w["mix_norm"])
    z = _w_in_fwd(n2, w["w_in"])
    qn, kvn, qf, kf, vf = _qkv_prep(z, w["q_norm"], w["kv_norm"], w["w_q_up"], w["w_kv_up"], tc, ta, tb)
    att, lse = _attn_fwd("mla_fwd", qf, kf, 0, vf, 0, MLA_HEADS, HEAD_QK, HEAD_V, MLA_SCALE, True, blk)
    pool = _pool_fwd(z, w["pool_w"], w["pool_scale"])
    s = x.shape[0]
    bm = _row_block(s)
    row = pl.BlockSpec((bm, D_MODEL), lambda i, k: (i, 0))
    half = pl.BlockSpec((bm, 512), lambda i, k: (i, 0))
    h2, n3 = _matmul(
        "w_out", (s // bm, 1),
        [(att, half, w["w_out"], pl.BlockSpec((512, D_MODEL), lambda i, k: (0, 0)), NN),
         (pool, half, w["w_out"], pl.BlockSpec((512, D_MODEL), lambda i, k: (1, 0)), NN)],
        [(h1, row)] + _residual_outs(s, bm, w["xattn_norm"])[0], _residual_outs(s, bm, w["xattn_norm"])[1],
        _residual_epilogue(1.0, True), None)
    memn = _rmsnorm_fwd("mem_norm", mem, w["mem_norm"], D_MODEL)
    qm = _mm_nn("w_mq", n3, w["w_mq"], BF16)
    kvm = _mm_heads_fwd("w_mkv", memn, w["w_mkv"], BF16)
    om, lse_m = _attn_fwd("xattn_fwd", qm, kvm, 0, kvm, MEM_HEADS, MEM_HEADS, MEM_HEAD_DIM, MEM_HEAD_DIM,
                          MEM_SCALE, False, blk)
    h3, n4 = _mm_nn("w_mo", om, w["w_mo"], F32, res=h2, gain=w["ffn2_norm"])
    a2, dadu2, dadg2 = _ffn_up("ffn2_up", n4, w["ffn2_w_gate"], w["ffn2_w_up"])
    dh4, dh4b, loss_vec, d_final = _ffn_down("ffn2_down", a2, w["ffn2_w_down"], h3, loss=(w["final_norm"], target))
    grads["final_norm"] = d_final

    dh3, dh3b, grads["ffn2_norm"] = _ffn_bwd("ffn2", dh4b, n4, dadg2, dadu2, a2, w["ffn2_w_gate"], w["ffn2_w_up"],
                                             w["ffn2_w_down"], grads, norm_bwd=(h3, w["ffn2_norm"], dh4))

    dom, delta_m = _mm_nt("w_mo_dx", dh3b, w["w_mo"], BF16, attn_out=om, nh=MEM_HEADS, dv=MEM_HEAD_DIM)
    grads["w_mo"] = _mm_tn("w_mo_dw", om, dh3b)
    dqm, dkm, dvm = _attn_bwd("xattn_bwd", qm, kvm, 0, kvm, MEM_HEADS, dom, 0, lse_m, delta_m, MEM_HEADS,
                              MEM_HEAD_DIM, MEM_HEAD_DIM, MEM_SCALE, False, blk)
    dkvm = jnp.concatenate([dkm, dvm], axis=1).astype(BF16)
    dh2, dh2b, grads["xattn_norm"] = _mm_nt_norm_bwd("w_mq_dx", dqm, w["w_mq"], h2, w["xattn_norm"], dh3)
    grads["w_mq"] = _mm_tn("w_mq_dw", n3, dqm)
    dmemn, grads["w_mkv"] = _mm_heads_bwd("w_mkv", dkvm, memn, w["w_mkv"])
    _, grads["mem_norm"] = _rmsnorm_bwd("mem_norm_bwd", dmemn, mem, w["mem_norm"], D_MODEL, out_dtype=BF16)

    dcat, delta = _mm_nt("w_out_dx", dh2b, w["w_out"], BF16, attn_out=att, nh=MLA_HEADS, dv=HEAD_V)
    grads["w_out"] = jnp.concatenate([_mm_tn("w_out_dw_a", att, dh2b), _mm_tn("w_out_dw_p", pool, dh2b)], axis=0)
    dzp, grads["pool_w"], grads["pool_scale"] = _pool_bwd(dcat, z, w["pool_w"], w["pool_scale"])
    dqf, dkf, dvf = _attn_bwd("mla_bwd", qf, kf, 0, vf, 0, dcat, 0, lse, delta, MLA_HEADS, HEAD_QK, HEAD_V,
                              MLA_SCALE, True, blk)
    dz_lat, grads["q_norm"], grads["kv_norm"], grads["w_q_up"], grads["w_kv_up"] = _qkv_prep_bwd(
        dqf, dkf, dvf, z, qn, kvn, w["q_norm"], w["kv_norm"], w["w_q_up"], w["w_kv_up"], tc, ta, tb)
    dz = jnp.concatenate([dz_lat, dzp.astype(BF16)], axis=1)
    grads["w_in"] = _w_in_dw(dz, n2)
    dh1, dh1b, grads["mix_norm"] = _w_in_dx_norm_bwd(dz, w["w_in"], h1, w["mix_norm"], dh2)

    dn1 = _ffn_bwd("ffn1", dh1b, n1, dadg1, dadu1, a1, w["ffn1_w_gate"], w["ffn1_w_up"], w["ffn1_w_down"], grads)
    dx, grads["ffn1_norm"], _ = _rmsnorm_bwd("ffn1_norm_bwd", dn1, x, w["ffn1_norm"], D_MODEL, dres=dh1)
    return loss_vec[0, 0], dx


def _mesh_pos():
    x, y, c = lax.axis_index("x"), lax.axis_index("y"), lax.axis_index("c")
    chips = [(1 - x, y), (x, 1 - y), (1 - x, 1 - y)]
    chip_ids = [2 * cx + cy for cx, cy in chips]
    return x, y, c, 2 * x + y, chips, chip_ids


def _half_rows(c, rows):
    hr = rows // 2
    return pl.ds(pl.multiple_of(c * hr, 16), hr), pl.ds(pl.multiple_of((1 - c) * hr, 16), hr)


def _ag_ici_stage(shards, relative=False):
    n = len(shards)

    def copies(ins, outs):
        x, y, c, me, chips, _ = _mesh_pos()
        out = []
        for k in range(n):
            mine, _ = _half_rows(c, ins[k].shape[0])
            out.append((ins[k], outs[k].at[0 if relative else me], None))
            for j, (cx, cy) in enumerate(chips):
                slab = _REL_OF_PEER[j] if relative else me
                out.append((ins[k].at[mine], outs[k].at[slab, mine], (cx, cy, c)))
        return out

    return _Stage(shards, [jax.ShapeDtypeStruct((N_CHIPS,) + s.shape, s.dtype) for s in shards], 3 * n, n, copies)


def _quarter_rows(c, rows):
    qr = rows // 4
    return pl.ds(pl.multiple_of(c * 2 * qr, 16), qr), pl.ds(pl.multiple_of(c * 2 * qr + qr, 16), qr)


def _ag_fwd_stage(fulls, relative=False):
    n = len(fulls)

    def copies(ins, outs):
        x, y, c, me, chips, chip_ids = _mesh_pos()
        out = []
        for k in range(n):
            q0, q1 = _quarter_rows(c, ins[k].shape[1])
            from_x, from_y, diag = (2, 1, 3) if relative else chip_ids
            out.append((ins[k].at[from_x, q0], outs[k].at[diag if relative else from_x, q0], (*chips[1], c)))
            out.append((ins[k].at[from_y, q1], outs[k].at[diag if relative else from_y, q1], (*chips[0], c)))
        return out

    return _Stage(fulls, [jax.ShapeDtypeStruct(f.shape, f.dtype) for f in fulls], 2 * n, 0, copies,
                  aliases={k: k for k in range(n)})


def _ag_d2d_stage(fulls, relative=False):
    n = len(fulls)

    def copies(ins, outs):
        x, y, c, me, _, chip_ids = _mesh_pos()
        out = []
        for k in range(n):
            mine, _ = _half_rows(c, ins[k].shape[1])
            for j in range(3):
                slab = _REL_OF_PEER[j] if relative else chip_ids[j]
                out.append((ins[k].at[slab, mine], outs[k].at[slab, mine], (x, y, 1 - c)))
        return out

    return _Stage(fulls, [jax.ShapeDtypeStruct(f.shape, f.dtype) for f in fulls], 3 * n, 0, copies,
                  aliases={k: k for k in range(n)})


def _rs_swap_stage(grads):
    n = len(grads)

    def copies(ins, outs):
        x, y, c, _, _, _ = _mesh_pos()
        out = []
        for k in range(n):
            _, other = _half_rows(c, ins[k].shape[1])
            out.append((ins[k].at[:, other, :], outs[k], (x, y, 1 - c)))
        return out

    return _Stage(grads, [jax.ShapeDtypeStruct((N_CHIPS, g.shape[1] // 2, g.shape[2]), g.dtype) for g in grads],
                  n, 0, copies)


_REL_OF_PEER = (2, 1, 3)


def _rs_scatter_stage(sums, relative=False):
    n = len(sums)

    def copies(ins, outs):
        x, y, c, me, chips, chip_ids = _mesh_pos()
        out = []
        for k in range(n):
            mine, _ = _half_rows(c, 2 * ins[k].shape[1])
            out.append((ins[k].at[0 if relative else me], outs[k].at[0, mine, :], None))
            for j, (cx, cy) in enumerate(chips):
                slab = _REL_OF_PEER[j] if relative else chip_ids[j]
                out.append((ins[k].at[slab], outs[k].at[1 + j, mine, :], (cx, cy, c)))
        return out

    return _Stage(sums, [jax.ShapeDtypeStruct((N_CHIPS, 2 * s.shape[1], s.shape[2]), s.dtype) for s in sums],
                  3 * n, n, copies)


def _rs_mirror_stage(parts):
    n = len(parts)

    def copies(ins, outs):
        x, y, c, _, _, _ = _mesh_pos()
        out = []
        for k in range(n):
            mine, _ = _half_rows(c, ins[k].shape[1])
            out.append((ins[k].at[:, mine, :], outs[k].at[:, mine, :], (x, y, 1 - c)))
        return out

    return _Stage(parts, [jax.ShapeDtypeStruct(p.shape, p.dtype) for p in parts], n, 0, copies,
                  aliases={k: k for k in range(n)})


def _pair_add(name, g, r1, core):
    _, rows, cols = g.shape
    hr = rows // 2

    def body(c_ref, g_ref, r_ref, o_ref):
        o_ref[...] = (g_ref[...].astype(F32) + r_ref[...].astype(F32)).astype(BF16)

    half = pl.BlockSpec((None, hr, cols), lambda j, c: (j, 0, 0))
    return pl.pallas_call(
        body, name=name,
        grid_spec=pltpu.PrefetchScalarGridSpec(
            num_scalar_prefetch=1, grid=(N_CHIPS,),
            in_specs=[pl.BlockSpec((None, hr, cols), lambda j, c: (j, c[0], 0)), half], out_specs=half),
        out_shape=jax.ShapeDtypeStruct((N_CHIPS, hr, cols), BF16),
        compiler_params=_params("parallel"),
    )(core, g, r1)


def _all_gather_weights(shards):
    n = len(shards)

    def body(*refs):
        ins, outs = refs[:n], refs[n:2 * n]
        send, recv, loc = refs[2 * n:]
        x, y, c, me, chips, chip_ids = _mesh_pos()
        sib = (x, y, 1 - c)

        def halves(k):
            hr = ins[k].shape[0] // 2
            return pl.ds(pl.multiple_of(c * hr, 16), hr), pl.ds(pl.multiple_of((1 - c) * hr, 16), hr)

        def remote(src, dst, k, j, dev):
            return pltpu.make_async_remote_copy(src_ref=src, dst_ref=dst, send_sem=send.at[k, j],
                                                recv_sem=recv.at[k, j], device_id=dev, device_id_type=_MESH)

        started = []
        local = []
        for k in range(n):
            mine, _ = halves(k)
            cp = pltpu.make_async_copy(ins[k], outs[k].at[me], loc.at[k])
            cp.start()
            local.append(cp)
            for j, (cx, cy) in enumerate(chips):
                cp = remote(ins[k].at[mine], outs[k].at[me, mine], k, j, (cx, cy, c))
                cp.start()
                started.append(cp)
        for k in range(n):
            mine, _ = halves(k)
            for j in range(3):
                land = outs[k].at[chip_ids[j], mine]
                remote(land, land, k, j, sib).wait_recv()
                cp = remote(land, land, k, 3 + j, sib)
                cp.start()
                started.append(cp)
        for k in range(n):
            _, other = halves(k)
            for j in range(3):
                land = outs[k].at[chip_ids[j], other]
                remote(land, land, k, 3 + j, sib).wait_recv()
        for cp in started:
            cp.wait_send()
        for cp in local:
            cp.wait()

    return pl.pallas_call(
        body, name="all_gather_weights", in_specs=[_ANY] * n, out_specs=[_ANY] * n,
        out_shape=[jax.ShapeDtypeStruct((N_CHIPS,) + s.shape, s.dtype) for s in shards],
        scratch_shapes=[pltpu.SemaphoreType.DMA((n, 6)), pltpu.SemaphoreType.DMA((n, 6)),
                        pltpu.SemaphoreType.DMA((n,))],
        compiler_params=pltpu.CompilerParams(vmem_limit_bytes=V7X_VMEM_LIMIT_BYTES),
    )(*shards)


_RS_CHUNK = 32


def _reduce_scatter(name, grads):
    n = len(grads)

    def body(*refs):
        gs, outs = refs[:n], refs[n:2 * n]
        own, r1, r2, fin = (refs[(2 + i) * n:(3 + i) * n] for i in range(4))
        a_send, a_recv, b_send, b_recv, c_send, c_recv, l_in, l_out = refs[6 * n:]
        x, y, c, me, chips, chip_ids = _mesh_pos()
        sib = (x, y, 1 - c)

        def halves(k):
            hr = gs[k].shape[1] // 2
            return hr, pl.ds(pl.multiple_of(c * hr, 16), hr), pl.ds(pl.multiple_of((1 - c) * hr, 16), hr)

        def remote(src, dst, ssem, rsem, dev):
            return pltpu.make_async_remote_copy(src_ref=src, dst_ref=dst, send_sem=ssem, recv_sem=rsem,
                                                device_id=dev, device_id_type=_MESH)

        sends, locals_in = [], []
        for k in range(n):
            hr, mine, other = halves(k)
            cp = remote(gs[k].at[:, other, :], r1[k], a_send.at[k], a_recv.at[k], sib)
            cp.start()
            sends.append(cp)
            cp = pltpu.make_async_copy(gs[k].at[:, mine, :], own[k], l_in.at[k])
            cp.start()
            locals_in.append(cp)

        for k in range(n):
            hr, mine, other = halves(k)
            locals_in[k].wait()
            remote(r1[k], r1[k], a_send.at[k], a_recv.at[k], sib).wait_recv()
            for j in range(N_CHIPS):
                def add(i, carry, k=k, j=j):
                    rows = pl.ds(pl.multiple_of(i * _RS_CHUNK, _RS_CHUNK), _RS_CHUNK)
                    own[k][j, rows, :] = (own[k][j, rows, :].astype(F32) + r1[k][j, rows, :].astype(F32)).astype(BF16)
                    return carry

                lax.fori_loop(0, hr // _RS_CHUNK, add, 0)
            for j, (cx, cy) in enumerate(chips):
                cp = remote(own[k].at[chip_ids[j]], r2[k].at[j], b_send.at[k, j], b_recv.at[k, j], (cx, cy, c))
                cp.start()
                sends.append(cp)

        locals_out = []
        for k in range(n):
            hr, mine, other = halves(k)
            for j in range(3):
                remote(r2[k].at[j], r2[k].at[j], b_send.at[k, j], b_recv.at[k, j], sib).wait_recv()

            def total(i, carry, k=k):
                rows = pl.ds(pl.multiple_of(i * _RS_CHUNK, _RS_CHUNK), _RS_CHUNK)
                acc = own[k][me, rows, :].astype(F32)
                for j in range(3):
                    acc = acc + r2[k][j, rows, :].astype(F32)
                fin[k][rows, :] = acc
                return carry

            lax.fori_loop(0, hr // _RS_CHUNK, total, 0)
            cp = remote(fin[k], outs[k].at[mine, :], c_send.at[k], c_recv.at[k], sib)
            cp.start()
            sends.append(cp)
            cp = pltpu.make_async_copy(fin[k], outs[k].at[mine, :], l_out.at[k])
            cp.start()
            locals_out.append(cp)

        for k in range(n):
            hr, mine, other = halves(k)
            land = outs[k].at[other, :]
            remote(land, land, c_send.at[k], c_recv.at[k], sib).wait_recv()
        for cp in sends:
            cp.wait_send()
        for cp in locals_out:
            cp.wait()

    scratch = []
    for g in grads:
        scratch.append(pltpu.VMEM((N_CHIPS, g.shape[1] // 2, g.shape[2]), BF16))
    for g in grads:
        scratch.append(pltpu.VMEM((N_CHIPS, g.shape[1] // 2, g.shape[2]), BF16))
    for g in grads:
        scratch.append(pltpu.VMEM((3, g.shape[1] // 2, g.shape[2]), BF16))
    for g in grads:
        scratch.append(pltpu.VMEM((g.shape[1] // 2, g.shape[2]), F32))
    dma = pltpu.SemaphoreType.DMA
    scratch += [dma((n,)), dma((n,)), dma((n, 3)), dma((n, 3)), dma((n,)), dma((n,)), dma((n,)), dma((n,))]
    return pl.pallas_call(
        body, name=name, in_specs=[_ANY] * n, out_specs=[_ANY] * n,
        out_shape=[jax.ShapeDtypeStruct(g.shape[1:], F32) for g in grads],
        scratch_shapes=scratch,
        compiler_params=pltpu.CompilerParams(vmem_limit_bytes=V7X_VMEM_LIMIT_BYTES),
    )(*grads)


def _adamw_math(w, g, m, v):
    m = ADAM_B1 * m + (1.0 - ADAM_B1) * g
    v = ADAM_B2 * v + (1.0 - ADAM_B2) * (g * g)
    m_hat = m / (1.0 - ADAM_B1 ** ADAM_STEP)
    v_hat = v / (1.0 - ADAM_B2 ** ADAM_STEP)
    delta = -ADAM_LR * (m_hat / (jnp.sqrt(v_hat) + ADAM_EPS) + ADAM_WD * w)
    return delta, m, v


def _adamw_sum(name, w, parts, m, v):
    r, c = w.shape
    br = r
    while br * c * 4 > (1 << 20) and br % 32 == 0:
        br //= 2

    def body(w_ref, p_ref, m_ref, v_ref, g_ref, d_ref, nm_ref, nv_ref):
        g = p_ref[0].astype(F32)
        for j in range(1, N_CHIPS):
            g = g + p_ref[j].astype(F32)
        d, nm, nv = _adamw_math(w_ref[...], g, m_ref[...], v_ref[...])
        g_ref[...] = g
        d_ref[...] = d
        nm_ref[...] = nm
        nv_ref[...] = nv

    spec = pl.BlockSpec((br, c), lambda i: (i, 0))
    shp = jax.ShapeDtypeStruct((r, c), F32)
    return _call(
        body, name=name, grid=(r // br,),
        in_specs=[spec, pl.BlockSpec((N_CHIPS, br, c), lambda i: (0, i, 0)), spec, spec],
        out_specs=[spec] * 4, out_shape=[shp] * 4, sem=("parallel",), args=[w, parts, m, v])


_SMALL_ROWS = 80


def _small_allreduce_adamw(gpack, wpack, mpack, vpack):
    half = _SMALL_ROWS // 2

    def body(g_ref, w_ref, m_ref, v_ref, go_ref, d_ref, nm_ref, nv_ref, sib_buf, chip_sum, buf, send, recv):
        x, y, c, me, chips, chip_ids = _mesh_pos()
        sib = (x, y, 1 - c)
        mine = pl.ds(pl.multiple_of(c * half, 8), half)

        def remote(src, dst, k, dev):
            return pltpu.make_async_remote_copy(src_ref=src, dst_ref=dst, send_sem=send.at[k], recv_sem=recv.at[k],
                                                device_id=dev, device_id_type=_MESH)

        swap = remote(g_ref, sib_buf, 0, sib)
        swap.start()
        swap.wait()
        chip_sum[...] = g_ref[...] + sib_buf[...]
        buf[me] = chip_sum[...]
        sends = [remote(chip_sum.at[mine], buf.at[me, mine], 1 + j, (cx, cy, c)) for j, (cx, cy) in enumerate(chips)]
        for cp in sends:
            cp.start()
        for cp in sends:
            cp.wait()
        mirrors = [remote(buf.at[chip_ids[j], mine], buf.at[chip_ids[j], mine], 4 + j, sib) for j in range(3)]
        for cp in mirrors:
            cp.start()
        for cp in mirrors:
            cp.wait()
        total = buf[0]
        for i in range(1, N_CHIPS):
            total = total + buf[i]
        go_ref[...] = total
        d, nm, nv = _adamw_math(w_ref[...], total, m_ref[...], v_ref[...])
        d_ref[...] = d
        nm_ref[...] = nm
        nv_ref[...] = nv

    vm = pl.BlockSpec(memory_space=pltpu.VMEM)
    shp = jax.ShapeDtypeStruct((_SMALL_ROWS, D_MODEL), F32)
    return pl.pallas_call(
        body, name="small_allreduce_adamw", in_specs=[vm] * 4, out_specs=[vm] * 4, out_shape=[shp] * 4,
        scratch_shapes=[pltpu.VMEM((_SMALL_ROWS, D_MODEL), F32), pltpu.VMEM((_SMALL_ROWS, D_MODEL), F32),
                        pltpu.VMEM((N_CHIPS, _SMALL_ROWS, D_MODEL), F32), pltpu.SemaphoreType.DMA((7,)),
                        pltpu.SemaphoreType.DMA((7,))],
        compiler_params=pltpu.CompilerParams(vmem_limit_bytes=V7X_VMEM_LIMIT_BYTES),
    )(gpack, wpack, mpack, vpack)


_SMALL_VECTORS = ("ffn1_norm", "mix_norm", "xattn_norm", "mem_norm", "ffn2_norm", "final_norm", "q_norm",
                  "kv_norm", "pool_scale")


_LOSS_ROW = 9
_VEC_ROWS = 16
_POOL_ROWS = POOL_GROUPS * POOL_CH


def _small_params_step(g, w, m, v, loss_local):
    names = list(_SMALL_VECTORS) + ["pool_w"]
    nv = len(_SMALL_VECTORS)
    widths = [g[n].shape[1] for n in _SMALL_VECTORS]
    shapes = {"vec": (_VEC_ROWS, D_MODEL), "pool": (_POOL_ROWS, POOL_CH)}

    def body(*refs):
        ins = refs[:4 * (nv + 1) + 1]
        outs = refs[len(ins):len(ins) + 4 * (nv + 1) + 1]
        vec_own, vec_sib, vec_all, pool_sib, pool_sum, pool_all, send, recv = refs[len(ins) + len(outs):]
        g_in, w_in, m_in, v_in = (ins[k * (nv + 1):(k + 1) * (nv + 1)] for k in range(4))
        loss_in = ins[-1]
        g_out, d_out, m_out, v_out = (outs[k * (nv + 1):(k + 1) * (nv + 1)] for k in range(4))
        loss_out = outs[-1]
        x, y, c, me, chips, chip_ids = _mesh_pos()
        sib = (x, y, 1 - c)

        def remote(src, dst, k, dev):
            return pltpu.make_async_remote_copy(src_ref=src, dst_ref=dst, send_sem=send.at[k], recv_sem=recv.at[k],
                                                device_id=dev, device_id_type=_MESH)

        vec_own[...] = jnp.zeros_like(vec_own)
        for i in range(nv):
            vec_own[i:i + 1, 0:widths[i]] = g_in[i][...]
        vec_own[_LOSS_ROW:_LOSS_ROW + 1, 0:128] = loss_in[...]
        swaps = [remote(vec_own, vec_sib, 0, sib), remote(g_in[nv], pool_sib, 1, sib)]
        for cp in swaps:
            cp.start()
        for cp in swaps:
            cp.wait()
        vec_all[me] = vec_own[...] + vec_sib[...]
        pool_sum[...] = g_in[nv][...] + pool_sib[...]
        pool_all[me] = pool_sum[...]
        hv, hp = _VEC_ROWS // 2, _POOL_ROWS // 2
        mine_v = pl.ds(pl.multiple_of(c * hv, 8), hv)
        mine_p = pl.ds(pl.multiple_of(c * hp, 8), hp)
        sends = []
        for j, (cx, cy) in enumerate(chips):
            sends.append(remote(vec_all.at[me, mine_v], vec_all.at[me, mine_v], 2 + j, (cx, cy, c)))
            sends.append(remote(pool_sum.at[mine_p], pool_all.at[me, mine_p], 5 + j, (cx, cy, c)))
        for cp in sends:
            cp.start()
        for cp in sends:
            cp.wait()
        mirrors = []
        for j in range(3):
            mirrors.append(remote(vec_all.at[chip_ids[j], mine_v], vec_all.at[chip_ids[j], mine_v], 8 + j, sib))
            mirrors.append(remote(pool_all.at[chip_ids[j], mine_p], pool_all.at[chip_ids[j], mine_p], 11 + j, sib))
        for cp in mirrors:
            cp.start()
        for cp in mirrors:
            cp.wait()
        vec_tot = vec_all[0]
        pool_tot = pool_all[0]
        for i in range(1, N_CHIPS):
            vec_tot = vec_tot + vec_all[i]
            pool_tot = pool_tot + pool_all[i]
        vec_sib[...] = vec_tot
        loss_out[...] = vec_sib[_LOSS_ROW:_LOSS_ROW + 1, 0:128]
        for i in range(nv + 1):
            gi = pool_tot if i == nv else vec_sib[i:i + 1, 0:widths[i]]
            d, nm, nvv = _adamw_math(w_in[i][...], gi, m_in[i][...], v_in[i][...])
            g_out[i][...] = gi
            d_out[i][...] = d
            m_out[i][...] = nm
            v_out[i][...] = nvv

    vm = pl.BlockSpec(memory_space=pltpu.VMEM)
    args = [d[n] for d in (g, w, m, v) for n in names] + [jnp.broadcast_to(loss_local.reshape(1, 1), (1, 128))]
    out_shape = [jax.ShapeDtypeStruct(g[n].shape, F32) for _ in range(4) for n in names]
    out_shape.append(jax.ShapeDtypeStruct((1, 128), F32))
    res = pl.pallas_call(
        body, name="small_params_step", in_specs=[vm] * len(args), out_specs=[vm] * len(out_shape),
        out_shape=out_shape,
        scratch_shapes=[pltpu.VMEM(shapes["vec"], F32), pltpu.VMEM(shapes["vec"], F32),
                        pltpu.VMEM((N_CHIPS,) + shapes["vec"], F32), pltpu.VMEM(shapes["pool"], F32),
                        pltpu.VMEM(shapes["pool"], F32), pltpu.VMEM((N_CHIPS,) + shapes["pool"], F32),
                        pltpu.SemaphoreType.DMA((14,)), pltpu.SemaphoreType.DMA((14,))],
        compiler_params=pltpu.CompilerParams(vmem_limit_bytes=V7X_VMEM_LIMIT_BYTES),
    )(*args)
    k = len(names)
    dicts = [dict(zip(names, res[i * k:(i + 1) * k])) for i in range(4)]
    return dicts[0], dicts[1], dicts[2], dicts[3], res[-1]


def _pack_small(d, scalar=None):
    rows = []
    for n in _SMALL_VECTORS:
        v = d[n].reshape(1, -1).astype(F32)
        rows.append(jnp.pad(v, ((0, 0), (0, D_MODEL - v.shape[1]))))
    assert len(rows) == _LOSS_ROW
    extra = jnp.zeros((1, D_MODEL), F32) if scalar is None else jnp.pad(scalar.reshape(1, 1), ((0, 0), (0, D_MODEL - 1)))
    rows.append(extra)
    rows.append(jnp.zeros((16 - len(rows), D_MODEL), F32))
    rows.append(d["pool_w"].reshape(64, D_MODEL).astype(F32))
    return jnp.concatenate(rows, axis=0)


def _unpack_small(pack, like):
    out = {}
    for i, n in enumerate(_SMALL_VECTORS):
        out[n] = pack[i, :like[n].size].reshape(like[n].shape)
    out["pool_w"] = pack[16:].reshape(like["pool_w"].shape)
    return out


_WEIGHTS = ("ffn1_norm", "ffn1_w_gate", "ffn1_w_up", "ffn1_w_down", "mix_norm", "w_in", "q_norm", "w_q_up",
            "kv_norm", "w_kv_up", "pool_w", "pool_scale", "w_out", "xattn_norm", "mem_norm", "w_mq", "w_mkv",
            "w_mo", "ffn2_norm", "ffn2_w_gate", "ffn2_w_up", "ffn2_w_down", "final_norm")
_SHARDED = ("ffn1_w_gate", "ffn1_w_up", "ffn1_w_down", "w_in", "w_q_up", "w_kv_up", "w_out", "w_mq", "w_mkv",
            "w_mo", "ffn2_w_gate", "ffn2_w_up", "ffn2_w_down")
_RS_GROUPS = (("ffn2_w_gate", "ffn2_w_up", "ffn2_w_down"),
              ("w_mo", "w_mq", "w_mkv", "w_out", "w_q_up", "w_kv_up", "w_in"),
              ("ffn1_w_gate", "ffn1_w_up", "ffn1_w_down"))
W_IN_SPLIT = Q_LORA + KV_LORA + ROPE_DIM


_FFN1 = ("ffn1_w_gate", "ffn1_w_up", "ffn1_w_down")
_TRANSPOSED = ("ffn1_w_gate", "ffn1_w_up", "ffn2_w_gate", "ffn2_w_up", "w_in", "w_q_up")


def _local_view(name, a):
    return jnp.swapaxes(a, 1, 2)[0] if name in _TRANSPOSED else a[0]


def _global_view(name, a):
    return jnp.swapaxes(a[None], 1, 2) if name in _TRANSPOSED else a[None]


def _pad_shard(name, a):
    if name == "w_in":
        return jnp.concatenate([a[:W_IN_SPLIT], jnp.zeros((64, a.shape[1]), a.dtype), a[W_IN_SPLIT:]], axis=0)
    if name == "w_q_up":
        return jnp.pad(a, ((0, 64), (0, 0)))
    return a


def _unpad_shard(name, a):
    if name == "w_in":
        return jnp.concatenate([a[:, :W_IN_SPLIT], a[:, W_IN_SPLIT + 64:]], axis=1)
    if name == "w_q_up":
        return a[:, :192]
    return a


def _stacked(g):
    return g if g.ndim == 3 else g.reshape(N_CHIPS, g.shape[0] // N_CHIPS, g.shape[1])


class _Plan:
    AG_UNITS = (
        (("w_in", "w_q_up", "w_kv_up"), "ffn1_up"),
        (("w_out", "w_mq"), "ffn1_down"),
        (("w_mkv",), "w_in"),
        (("w_mo",), "qkv_prep"),
        (("ffn2_w_gate", "ffn2_w_up"), "mla_fwd"),
        (("ffn2_w_down",), "xattn_fwd"),
    )
    RS_UNITS = (
        (("ffn2_w_gate", "ffn2_w_up", "ffn2_w_down"), "ffn2_dn_a", "mla_bwd", "qkv_prep_bwd"),
        (("w_mo", "w_mq", "w_mkv"), "w_out_dx", "mla_bwd", "qkv_prep_bwd"),
        (("w_out", "w_q_up", "w_kv_up", "w_in"), "w_in_dx", "ffn1_dact", "ffn1_dwd"),
        (("ffn1_w_down",), "ffn1_dwg", "ffn1_dwu", "ffn1_dn_a"),
        (("ffn1_w_gate",), "ffn1_dwu", "ffn1_dn_a", "ffn1_dn_b"),
        (("ffn1_w_up",), "ffn1_dn_a", "ffn1_dn_b", "adamw_w_kv_up"),
    )
    ADAMW_ORDER = ("w_kv_up", "ffn2_w_gate", "ffn2_w_up", "ffn2_w_down", "w_mo", "w_mq", "w_mkv", "w_out", "w_q_up",
                   "w_in", "ffn1_w_down", "ffn1_w_gate", "ffn1_w_up")

    def __init__(self, shards, w, grads, core):
        self.shards, self.w, self.grads, self.core = shards, w, grads, core
        self.last_slab_step = 0
        self.parts = {}
        self.ag = [None for _ in self.AG_UNITS]
        self.rs = [[None, None, None, None] for _ in self.RS_UNITS]

    def pre(self, name):
        for i, (names, host) in enumerate(self.AG_UNITS):
            if name == host:
                st = _ag_ici_stage([self.shards[n] for n in names])
                st.then = _ag_d2d_stage(st.outs)
                st.start_step = self.last_slab_step if name == "ffn1_up" else 0
                self.ag[i] = _host(name, st)
        for i, (names, h1, h2, h3) in enumerate(self.RS_UNITS):
            if name == h1:
                self.rs[i][0] = _host(name, _rs_swap_stage([_stacked(self.grads[n]) for n in names]))
            if name == h2:
                self.rs[i][2] = _host(name, _rs_scatter_stage(self.rs[i][1], relative=names[0] in _FFN1))
            if name == h3:
                self.rs[i][3] = _host(name, _rs_mirror_stage(self.rs[i][2].results))

    def post(self, name):
        for i, (names, host) in enumerate(self.AG_UNITS):
            if name == host:
                for n, f in zip(names, self.ag[i].results):
                    self.w[n] = _full_weight(n, f)
        for i, (names, h1, h2, h3) in enumerate(self.RS_UNITS):
            if name == h1:
                self.rs[i][1] = [_pair_add("pair_add_" + n, _stacked(self.grads[n]), r1, self.core)
                                 for n, r1 in zip(names, self.rs[i][0].results)]
            if name == h3:
                for n, p in zip(names, self.rs[i][3].results):
                    self.parts[n] = p


def _full_weight(name, stacked):
    if name in ("w_out", "w_mq", "w_mo"):
        return stacked.reshape(D_MODEL, D_MODEL)
    return stacked


def kernel(x, mem, positions, ffn1_norm, ffn1_w_gate, ffn1_w_up, ffn1_w_down, mix_norm, w_in, q_norm, w_q_up, kv_norm, w_kv_up, pool_w, pool_scale, w_out, xattn_norm, mem_norm, w_mq, w_mkv, w_mo, ffn2_norm, ffn2_w_gate, ffn2_w_up, ffn2_w_down, final_norm, loss_target, m_ffn1_norm, m_ffn1_w_gate, m_ffn1_w_up, m_ffn1_w_down, m_mix_norm, m_w_in, m_q_norm, m_w_q_up, m_kv_norm, m_w_kv_up, m_pool_w, m_pool_scale, m_w_out, m_xattn_norm, m_mem_norm, m_w_mq, m_w_mkv, m_w_mo, m_ffn2_norm, m_ffn2_w_gate, m_ffn2_w_up, m_ffn2_w_down, m_final_norm, v_ffn1_norm, v_ffn1_w_gate, v_ffn1_w_up, v_ffn1_w_down, v_mix_norm, v_w_in, v_q_norm, v_w_q_up, v_kv_norm, v_w_kv_up, v_pool_w, v_pool_scale, v_w_out, v_xattn_norm, v_mem_norm, v_w_mq, v_w_mkv, v_w_mo, v_ffn2_norm, v_ffn2_w_gate, v_ffn2_w_up, v_ffn2_w_down, v_final_norm):
    wts = dict(zip(_WEIGHTS, (ffn1_norm, ffn1_w_gate, ffn1_w_up, ffn1_w_down, mix_norm, w_in, q_norm, w_q_up, kv_norm, w_kv_up, pool_w, pool_scale, w_out, xattn_norm, mem_norm, w_mq, w_mkv, w_mo, ffn2_norm, ffn2_w_gate, ffn2_w_up, ffn2_w_down, final_norm)))
    mom = dict(zip(_WEIGHTS, (m_ffn1_norm, m_ffn1_w_gate, m_ffn1_w_up, m_ffn1_w_down, m_mix_norm, m_w_in, m_q_norm, m_w_q_up, m_kv_norm, m_w_kv_up, m_pool_w, m_pool_scale, m_w_out, m_xattn_norm, m_mem_norm, m_w_mq, m_w_mkv, m_w_mo, m_ffn2_norm, m_ffn2_w_gate, m_ffn2_w_up, m_ffn2_w_down, m_final_norm)))
    var = dict(zip(_WEIGHTS, (v_ffn1_norm, v_ffn1_w_gate, v_ffn1_w_up, v_ffn1_w_down, v_mix_norm, v_w_in, v_q_norm, v_w_q_up, v_kv_norm, v_w_kv_up, v_pool_w, v_pool_scale, v_w_out, v_xattn_norm, v_mem_norm, v_w_mq, v_w_mkv, v_w_mo, v_ffn2_norm, v_ffn2_w_gate, v_ffn2_w_up, v_ffn2_w_down, v_final_norm)))
    small = [n for n in _WEIGHTS if n not in _SHARDED]

    global _PLAN
    shards = {n: _pad_shard(n, _local_view(n, wts[n])).astype(BF16) for n in _SHARDED}
    w = {n: wts[n].reshape(1, -1) for n in _SMALL_VECTORS}
    w["pool_w"] = pool_w[0].astype(BF16)
    grads = {}
    core = lax.axis_index("c").astype(jnp.int32).reshape(1)
    plan = _Plan(shards, w, grads, core)
    _PLAN = plan
    try:
        w["ffn1_shards"] = tuple(shards[n] for n in _FFN1)

        loss_local, dx = _local_step(x[0], mem[0], positions[0], loss_target[0], w, grads)

        def small_view(d):
            out = {n: d[n].reshape(1, -1) for n in _SMALL_VECTORS}
            out["pool_w"] = d["pool_w"].reshape(_POOL_ROWS, POOL_CH)
            return out

        *small_res, loss_vec = _small_params_step(small_view(grads), small_view(wts), small_view(mom),
                                                  small_view(var), loss_local)
        g_out, d_out, m_out, v_out = ({n: r[n].reshape(wts[n].shape) for n in small} for r in small_res)
        loss = loss_vec[0, 0]

        for n in _Plan.ADAMW_ORDER:
            res = _adamw_sum("adamw_" + n, _local_view(n, wts[n]), _unpad_shard(n, plan.parts[n]),
                             _local_view(n, mom[n]), _local_view(n, var[n]))
            g_out[n], d_out[n], m_out[n], v_out[n] = (_global_view(n, r) for r in res)
    finally:
        _PLAN = None
        _PENDING.clear()

    return (loss, dx[None], *[g_out[n] for n in _WEIGHTS], *[d_out[n] for n in _WEIGHTS],
            *[m_out[n] for n in _WEIGHTS], *[v_out[n] for n in _WEIGHTS])
```

```python
import functools

import jax
import jax.numpy as jnp
from jax import lax
from jax.experimental import pallas as pl
from jax.experimental.pallas import tpu as pltpu

F32 = jnp.float32
BF16 = jnp.bfloat16

D_MODEL = 1024
D_FF = 2816
N_CHIPS = 4
FF_SHARD = D_FF // N_CHIPS
MLA_HEADS = 4
Q_LORA = 256
KV_LORA = 128
ROPE_DIM = 64
HEAD_QK = 256
HEAD_V = 128
POOL_GROUPS = 4
POOL_CH = 128
MEM_HEADS = 4
MEM_HEAD_DIM = 256
RMS_EPS = 1e-6
ROPE_BASE = 10000.0
MLA_SCALE = (128 + 64) ** -0.5
MEM_SCALE = MEM_HEAD_DIM ** -0.5

ADAM_LR = 0.001
ADAM_B1 = 0.9
ADAM_B2 = 0.999
ADAM_EPS = 1e-08
ADAM_WD = 0.01
ADAM_STEP = 10

V7X_VMEM_LIMIT_BYTES = 56 * 1024 * 1024

NN = ((1,), (0,))
NT = ((1,), (1,))
TN = ((0,), (0,))


def _params(*sem):
    return pltpu.CompilerParams(dimension_semantics=sem, vmem_limit_bytes=V7X_VMEM_LIMIT_BYTES)


_MESH = pl.DeviceIdType.MESH
_ANY = pl.BlockSpec(memory_space=pl.ANY)


class _Stage:
    def __init__(self, ins, outs, n_remote, n_local, copies, aliases=None):
        self.ins, self.outs, self.n_remote, self.n_local = list(ins), list(outs), n_remote, n_local
        self.copies, self.aliases = copies, dict(aliases or {})
        self.results = None
        self.start_step = 0
        self.then = None

    def descriptors(self, in_refs, out_refs, send, recv, loc):
        ds, ri, li = [], 0, 0
        for src, dst, dev in self.copies(in_refs, out_refs):
            if dev is None:
                ds.append(pltpu.make_async_copy(src, dst, loc.at[li]))
                li += 1
            else:
                ds.append(pltpu.make_async_remote_copy(src_ref=src, dst_ref=dst, send_sem=send.at[ri],
                                                       recv_sem=recv.at[ri], device_id=dev, device_id_type=_MESH))
                ri += 1
        assert ri == self.n_remote and li == self.n_local
        return ds


_PENDING = {}


def _host(name, stage):
    _PENDING.setdefault(name, []).append(stage)
    return stage


_PLAN = None


def _call(body, **kw):
    if _PLAN is not None:
        _PLAN.pre(kw["name"])
    res = _call_hosting(body, **kw)
    if _PLAN is not None:
        _PLAN.post(kw["name"])
    return res


def _call_hosting(body, *, name, grid, in_specs, out_specs, out_shape, sem, args, scratch_shapes=(), aliases=None):
    stages = _PENDING.pop(name, [])
    scratch_shapes = list(scratch_shapes)
    if not stages:
        return pl.pallas_call(body, name=name, grid=grid, in_specs=in_specs, out_specs=out_specs,
                              out_shape=out_shape, scratch_shapes=scratch_shapes,
                              input_output_aliases=dict(aliases or {}), compiler_params=_params(*sem))(*args)
    ni, no, ns = len(in_specs), len(out_shape), len(scratch_shapes)
    c_ins = [a for st in stages for a in st.ins]
    c_outs = [o for st in stages for o in st.outs]
    nci, nco = len(c_ins), len(c_outs)
    aliases, io, oo = dict(aliases or {}), 0, 0
    for st in stages:
        for i, j in st.aliases.items():
            aliases[ni + io + i] = no + oo + j
        io += len(st.ins)
        oo += len(st.outs)
    dma = pltpu.SemaphoreType.DMA
    sems = []
    for st in stages:
        sems += [dma((max(st.n_remote, 1),)), dma((max(st.n_remote, 1),)), dma((max(st.n_local, 1),))]
    followers = [st.then for st in stages if st.then is not None]
    for st in followers:
        sems += [dma((max(st.n_remote, 1),)), dma((max(st.n_remote, 1),)), dma((max(st.n_local, 1),))]

    def wrapped(*refs):
        ins, cin = refs[:ni], refs[ni:ni + nci]
        outs, cout = refs[ni + nci:ni + nci + no], refs[ni + nci + no:ni + nci + no + nco]
        scr = refs[ni + nci + no + nco:ni + nci + no + nco + ns]
        sem_refs = refs[ni + nci + no + nco + ns:]
        step = pl.program_id(0)
        last = pl.program_id(0) == grid[0] - 1
        for ax in range(1, len(grid)):
            step = step * grid[ax] + pl.program_id(ax)
            last = jnp.logical_and(last, pl.program_id(ax) == grid[ax] - 1)

        def descriptors(si):
            io = sum(len(st.ins) for st in stages[:si])
            oo = sum(len(st.outs) for st in stages[:si])
            st = stages[si]
            return st.descriptors(cin[io:io + len(st.ins)], cout[oo:oo + len(st.outs)], *sem_refs[3 * si:3 * si + 3])

        def follower_descriptors(fi):
            si = [k for k, st in enumerate(stages) if st.then is not None][fi]
            oo = sum(len(st.outs) for st in stages[:si])
            bufs = cout[oo:oo + len(stages[si].outs)]
            k0 = 3 * (len(stages) + fi)
            return followers[fi].descriptors(bufs, bufs, *sem_refs[k0:k0 + 3])

        def start(si):
            @pl.when(step == stages[si].start_step)
            def _():
                for d in descriptors(si):
                    d.start()

        for si, st in enumerate(stages):
            if st.start_step == 0:
                start(si)
        body(*ins, *outs, *scr)
        for si, st in enumerate(stages):
            if st.start_step != 0:
                start(si)

        @pl.when(last)
        def _():
            for si in range(len(stages)):
                for d in descriptors(si):
                    d.wait()
            for fi in range(len(followers)):
                for d in follower_descriptors(fi):
                    d.start()
            for fi in range(len(followers)):
                for d in follower_descriptors(fi):
                    d.wait()

    res = pl.pallas_call(
        wrapped, name=name, grid=grid, in_specs=list(in_specs) + [_ANY] * nci,
        out_specs=list(out_specs) + [_ANY] * nco, out_shape=list(out_shape) + c_outs,
        scratch_shapes=scratch_shapes + sems, input_output_aliases=aliases,
        compiler_params=_params(*(("arbitrary",) * len(grid))))(*args, *c_ins)
    oo = no
    for st in stages:
        st.results = list(res[oo:oo + len(st.outs)])
        oo += len(st.outs)
    return list(res[:no])


def _dot(a, b, dims):
    return lax.dot_general(a.astype(BF16), b.astype(BF16), (dims, ((), ())), preferred_element_type=F32)


_MAX_ROW_BLOCK = 1024
_ATT_BLOCK = 512


_MAX_REDUCE_BLOCK = 2048


def _row_block(s, want=1024):
    return min(want, s, _MAX_ROW_BLOCK)


def _reduce_block(s):
    return min(s, _MAX_REDUCE_BLOCK)


def _matmul(name, grid, terms, extras, outs, epilogue, acc_shape, fill=(), summed=()):
    nt, ne, no, nf = len(terms), len(extras), len(outs), len(fill)
    nk = grid[-1]
    dims = [t[4] for t in terms]

    def body(*refs):
        a_refs, b_refs = refs[:nt], refs[nt:2 * nt]
        e_refs = refs[2 * nt:2 * nt + ne]
        o_refs = refs[2 * nt + ne + nf:2 * nt + ne + nf + no]

        def finish(acc):
            vals = epilogue(acc, *[e[...] for e in e_refs])
            for idx, (o, val) in enumerate(zip(o_refs, vals)):
                if idx in summed:
                    @pl.when(pl.program_id(0) == 0)
                    def _(o=o, val=val):
                        o[...] = val.astype(o.dtype)

                    @pl.when(pl.program_id(0) > 0)
                    def _(o=o, val=val):
                        o[...] += val.astype(o.dtype)
                else:
                    o[...] = val.astype(o.dtype)

        if nk == 1:
            part = None
            for a, b, d in zip(a_refs, b_refs, dims):
                t = _dot(a[...], b[...], d)
                part = t if part is None else part + t
            finish(part)
        else:
            acc_ref = refs[-1]
            k = pl.program_id(len(grid) - 1)

            @pl.when(k == 0)
            def _():
                acc_ref[...] = jnp.zeros_like(acc_ref)

            for a, b, d in zip(a_refs, b_refs, dims):
                acc_ref[...] += _dot(a[...], b[...], d)

            @pl.when(k == nk - 1)
            def _():
                finish(acc_ref[...])

    in_specs = [t[1] for t in terms] + [t[3] for t in terms] + [e[1] for e in extras] + [_ANY] * nf
    args = [t[0] for t in terms] + [t[2] for t in terms] + [e[0] for e in extras] + list(fill)
    sem = ("arbitrary" if summed else "parallel",) * (len(grid) - 1) + ("arbitrary",)
    aliases = {2 * nt + ne + i: i for i in range(nf)}
    return _call(
        body, name=name, grid=grid, in_specs=in_specs,
        out_specs=[o[1] for o in outs], out_shape=[o[0] for o in outs],
        scratch_shapes=[pltpu.VMEM(acc_shape, F32)] if nk > 1 else [], sem=sem, args=args, aliases=aliases)


def _ident(acc):
    return (acc,)


def _rmsnorm_fwd(name, x, gain, width, col_block=0):
    s = x.shape[0]
    bm = _row_block(s)

    def body(x_ref, g_ref, o_ref):
        xf = x_ref[...]
        r = lax.rsqrt(jnp.mean(xf * xf, axis=-1, keepdims=True) + RMS_EPS)
        o_ref[...] = ((xf * r) * g_ref[...]).astype(o_ref.dtype)

    return pl.pallas_call(
        body, name=name, grid=(s // bm,),
        in_specs=[pl.BlockSpec((bm, width), lambda i: (i, col_block)), pl.BlockSpec((1, width), lambda i: (0, 0))],
        out_specs=pl.BlockSpec((bm, width), lambda i: (i, 0)),
        out_shape=jax.ShapeDtypeStruct((s, width), BF16),
        compiler_params=_params("parallel"),
    )(x, gain)


def _rms_bwd_math(dy, xf, g, width):
    r = lax.rsqrt(jnp.mean(xf * xf, axis=-1, keepdims=True) + RMS_EPS)
    dyg = dy * g
    dot = jnp.sum(dyg * xf, axis=-1, keepdims=True)
    dx = r * dyg - xf * ((r * r * r) * (dot * (1.0 / width)))
    dgain = jnp.sum(dy * (xf * r), axis=0, keepdims=True)
    return dx, dgain


def _rmsnorm_bwd(name, dy, x, gain, width, col_block=0, dres=None, out_dtype=F32):
    s = x.shape[0]
    bm = _row_block(s)
    has_res = dres is not None

    def body(*refs):
        if has_res:
            dy_ref, x_ref, g_ref, r_ref, dx_ref, dg_ref, dxb_ref = refs
        else:
            dy_ref, x_ref, g_ref, dx_ref, dg_ref = refs
        dx, dgain = _rms_bwd_math(dy_ref[...].astype(F32), x_ref[...], g_ref[...], width)
        if has_res:
            dx = dx + r_ref[...]
            dxb_ref[...] = dx.astype(BF16)
        dx_ref[...] = dx.astype(dx_ref.dtype)

        @pl.when(pl.program_id(0) == 0)
        def _():
            dg_ref[...] = dgain

        @pl.when(pl.program_id(0) > 0)
        def _():
            dg_ref[...] += dgain

    row = pl.BlockSpec((bm, width), lambda i: (i, 0))
    in_specs = [row, pl.BlockSpec((bm, width), lambda i: (i, col_block)), pl.BlockSpec((1, width), lambda i: (0, 0))]
    args = [dy, x, gain]
    out_specs = [row, pl.BlockSpec((1, width), lambda i: (0, 0))]
    out_shape = [jax.ShapeDtypeStruct((s, width), out_dtype), jax.ShapeDtypeStruct((1, width), F32)]
    if has_res:
        in_specs.append(row)
        args.append(dres)
        out_specs.append(row)
        out_shape.append(jax.ShapeDtypeStruct((s, width), BF16))
    return _call(body, name=name, grid=(s // bm,), in_specs=in_specs, out_specs=out_specs, out_shape=out_shape,
                 sem=("arbitrary",), args=args)


def _loss_and_final_norm(h, gain, target):
    s, d = h.shape
    bm = _row_block(s, 512)

    def body(h_ref, g_ref, t_ref, dh_ref, dhb_ref, loss_ref, dg_ref):
        xf = h_ref[...]
        g = g_ref[...]
        r = lax.rsqrt(jnp.mean(xf * xf, axis=-1, keepdims=True) + RMS_EPS)
        err = (xf * r) * g - t_ref[...]
        part = 0.5 * jnp.sum(jnp.mean(err * err, axis=-1, keepdims=True), axis=0, keepdims=True)
        dx, dgain = _rms_bwd_math(err * (1.0 / d), xf, g, d)
        dh_ref[...] = dx
        dhb_ref[...] = dx.astype(BF16)

        @pl.when(pl.program_id(0) == 0)
        def _():
            dg_ref[...] = dgain
            loss_ref[...] = jnp.broadcast_to(part, loss_ref.shape)

        @pl.when(pl.program_id(0) > 0)
        def _():
            dg_ref[...] += dgain
            loss_ref[...] += jnp.broadcast_to(part, loss_ref.shape)

    row = pl.BlockSpec((bm, d), lambda i: (i, 0))
    vec = pl.BlockSpec((1, d), lambda i: (0, 0))
    return pl.pallas_call(
        body, name="loss_final_norm", grid=(s // bm,), in_specs=[row, vec, row],
        out_specs=[row, row, pl.BlockSpec((1, 128), lambda i: (0, 0)), vec],
        out_shape=[jax.ShapeDtypeStruct((s, d), F32), jax.ShapeDtypeStruct((s, d), BF16),
                   jax.ShapeDtypeStruct((1, 128), F32),
                   jax.ShapeDtypeStruct((1, d), F32)],
        compiler_params=_params("arbitrary"),
    )(h, gain, target)


def _ffn_up(name, n, wg, wu):
    s = n.shape[0]
    bm = _row_block(s)

    def body(n_ref, wg_ref, wu_ref, a_ref, dadu_ref, dadg_ref):
        x = n_ref[...]
        g = _dot(x, wg_ref[...], NT)
        u = _dot(x, wu_ref[...], NT)
        sg = jax.nn.sigmoid(g)
        silu = g * sg
        a_ref[...] = (silu * u).astype(BF16)
        dadu_ref[...] = silu.astype(BF16)
        dadg_ref[...] = (u * (sg * (1.0 + g * (1.0 - sg)))).astype(BF16)

    w_spec = pl.BlockSpec((None, FF_SHARD, D_MODEL), lambda j, i: (j, 0, 0))
    o_spec = pl.BlockSpec((None, bm, FF_SHARD), lambda j, i: (j, i, 0))
    shp = jax.ShapeDtypeStruct((N_CHIPS, s, FF_SHARD), BF16)
    return _call(
        body, name=name, grid=(N_CHIPS, s // bm),
        in_specs=[pl.BlockSpec((bm, D_MODEL), lambda j, i: (i, 0)), w_spec, w_spec],
        out_specs=[o_spec, o_spec, o_spec], out_shape=[shp, shp, shp],
        sem=("parallel", "parallel"), args=[n, wg, wu])


def _ffn1_up_gather_direct(n, g_sh, u_sh, d_sh):
    s = n.shape[0]
    bm = _row_block(s)
    nrb = s // bm
    rows, cols = g_sh.shape
    rels = (1, 2, 3)

    def body(n_ref, gs, us, ds, a_ref, dadu_ref, dadg_ref, wg, wu, wd, gbuf, ubuf, send, recv, fsend, frecv, loc, ld):
        r, i = pl.program_id(0), pl.program_id(1)
        x, y, c = lax.axis_index("x"), lax.axis_index("y"), lax.axis_index("c")
        sib = (x, y, 1 - c)
        mine, _ = _half_rows(c, rows)
        shards, fulls, bufs = (gs, us, ds), (wg, wu, wd), (gbuf, ubuf)

        def ici(k, rel, dev=sib):
            return pltpu.make_async_remote_copy(
                src_ref=shards[k].at[mine], dst_ref=fulls[k].at[rel, mine], send_sem=send.at[k, rel - 1],
                recv_sem=recv.at[k, rel - 1], device_id=dev, device_id_type=_MESH)

        def peer(rel):
            return ((1 - x) if rel & 2 else x, (1 - y) if rel & 1 else y, c)

        def fwd(k, rel):
            return pltpu.make_async_remote_copy(
                src_ref=fulls[k].at[rel, mine], dst_ref=fulls[k].at[rel, mine], send_sem=fsend.at[k, rel - 1],
                recv_sem=frecv.at[k, rel - 1], device_id=sib, device_id_type=_MESH)

        def own(k):
            return pltpu.make_async_copy(shards[k], fulls[k].at[0], loc.at[k])

        def load(k, src):
            return pltpu.make_async_copy(src, bufs[k], ld.at[k])

        @pl.when(jnp.logical_and(r == 0, i == 0))
        def _():
            for k in range(3):
                own(k).start()
            for rel in (1, 2):
                for k in (0, 1):
                    ici(k, rel, peer(rel)).start()
            for k in (0, 1):
                load(k, shards[k]).start()
            for k in (0, 1):
                load(k, shards[k]).wait()

        @pl.when(jnp.logical_and(r > 0, i == 0))
        def _():
            for k in (0, 1):
                ici(k, r).wait_recv()
                fwd(k, r).start()
            for k in (0, 1):
                fwd(k, r).wait_recv()
                load(k, fulls[k].at[r]).start()
            for k in (0, 1):
                load(k, fulls[k].at[r]).wait()

        @pl.when(jnp.logical_and(r == 1, i == 0))
        def _():
            for k in (0, 1):
                ici(k, 3, peer(3)).start()

        @pl.when(jnp.logical_and(r == 2, i == 0))
        def _():
            for rel in (1, 2):
                ici(2, rel, peer(rel)).start()

        @pl.when(jnp.logical_and(r == 3, i == 0))
        def _():
            ici(2, 3, peer(3)).start()

        xv = n_ref[...]
        g = _dot(xv, gbuf[...], NT)
        u = _dot(xv, ubuf[...], NT)
        sg = jax.nn.sigmoid(g)
        silu = g * sg
        a_ref[...] = (silu * u).astype(BF16)
        dadu_ref[...] = silu.astype(BF16)
        dadg_ref[...] = (u * (sg * (1.0 + g * (1.0 - sg)))).astype(BF16)

        @pl.when(jnp.logical_and(r == 3, i == nrb - 1))
        def _():
            for rel in rels:
                ici(2, rel).wait_recv()
                fwd(2, rel).start()
            for rel in rels:
                fwd(2, rel).wait_recv()
            for k in range(3):
                for rel in rels:
                    ici(k, rel).wait_send()
                    fwd(k, rel).wait_send()
                own(k).wait()

    o_spec = pl.BlockSpec((None, bm, FF_SHARD), lambda r, i: (r, i, 0))
    act = jax.ShapeDtypeStruct((N_CHIPS, s, FF_SHARD), BF16)
    full = jax.ShapeDtypeStruct((N_CHIPS, rows, cols), BF16)
    dma = pltpu.SemaphoreType.DMA
    if _PLAN is not None:
        _PLAN.last_slab_step = 3 * nrb
    return _call(
        body, name="ffn1_up", grid=(N_CHIPS, nrb),
        in_specs=[pl.BlockSpec((bm, D_MODEL), lambda r, i: (i, 0)), _ANY, _ANY, _ANY],
        out_specs=[o_spec, o_spec, o_spec, _ANY, _ANY, _ANY], out_shape=[act, act, act, full, full, full],
        scratch_shapes=[pltpu.VMEM((rows, cols), BF16), pltpu.VMEM((rows, cols), BF16), dma((3, 3)), dma((3, 3)),
                        dma((3, 3)), dma((3, 3)), dma((3,)), dma((2,))],
        sem=("arbitrary", "arbitrary"), args=[n, g_sh, u_sh, d_sh])


def _ffn1_up_gather(xin, gain, g_sh, u_sh, d_sh):
    s = xin.shape[0]
    bm = _row_block(s)
    nrb = s // bm
    rows, cols = g_sh.shape

    def body(x_ref, gain_ref, gs, us, ds, n_ref, a_ref, dadu_ref, dadg_ref, wg, wu, wd, gbuf, ubuf,
             send, recv, qsend, qrecv, fsend, frecv, loc, ld):
        r, i = pl.program_id(0), pl.program_id(1)
        x, y, c = lax.axis_index("x"), lax.axis_index("y"), lax.axis_index("c")
        sib = (x, y, 1 - c)
        mine, _ = _half_rows(c, rows)
        quarters = _quarter_rows(c, rows)
        shards, fulls, bufs = (gs, us, ds), (wg, wu, wd), (gbuf, ubuf)

        def remote(src, dst, ssem, rsem, dev):
            return pltpu.make_async_remote_copy(src_ref=src, dst_ref=dst, send_sem=ssem, recv_sem=rsem,
                                                device_id=dev, device_id_type=_MESH)

        def peer(rel):
            return ((1 - x) if rel & 2 else x, (1 - y) if rel & 1 else y, c)

        def ici(k, rel, dev=sib):
            return remote(shards[k].at[mine], fulls[k].at[rel, mine], send.at[k, rel - 1], recv.at[k, rel - 1], dev)

        def quarter(k, which, dev=sib):
            slab, q = ((2, quarters[0]), (1, quarters[1]))[which]
            return remote(fulls[k].at[slab, q], fulls[k].at[3, q], qsend.at[k, which], qrecv.at[k, which], dev)

        def fwd(k, rel):
            return remote(fulls[k].at[rel, mine], fulls[k].at[rel, mine], fsend.at[k, rel - 1], frecv.at[k, rel - 1], sib)

        def own(k):
            return pltpu.make_async_copy(shards[k], fulls[k].at[0], loc.at[k])

        def load(slab):
            for k in (0, 1):
                pltpu.make_async_copy(shards[k] if slab == 0 else fulls[k].at[slab], bufs[k], ld.at[k]).start()
            for k in (0, 1):
                pltpu.make_async_copy(shards[k] if slab == 0 else fulls[k].at[slab], bufs[k], ld.at[k]).wait()

        def from_neighbour(ks, rel):
            for k in ks:
                ici(k, rel).wait_recv()
                fwd(k, rel).start()
                quarter(k, 0 if rel == 2 else 1, peer(1 if rel == 2 else 2)).start()
            for k in ks:
                fwd(k, rel).wait_recv()

        def from_diagonal(ks):
            for k in ks:
                quarter(k, 0).wait_recv()
                quarter(k, 1).wait_recv()
                fwd(k, 3).start()
            for k in ks:
                fwd(k, 3).wait_recv()

        @pl.when(jnp.logical_and(r == 0, i == 0))
        def _():
            for k in range(3):
                own(k).start()
            for rel in (1, 2):
                for k in (0, 1):
                    ici(k, rel, peer(rel)).start()
            load(0)

        @pl.when(jnp.logical_and(r == 1, i == 0))
        def _():
            from_neighbour((0, 1), 1)
            load(1)
            for rel in (1, 2):
                ici(2, rel, peer(rel)).start()

        @pl.when(jnp.logical_and(r == 2, i == 0))
        def _():
            from_neighbour((0, 1), 2)
            load(2)

        @pl.when(jnp.logical_and(r == 3, i == 0))
        def _():
            from_diagonal((0, 1))
            load(3)

        xv = _norm_bf16(x_ref[...], gain_ref[...])

        @pl.when(r == 0)
        def _():
            n_ref[...] = xv

        g = _dot(xv, gbuf[...], NT)
        u = _dot(xv, ubuf[...], NT)
        sg = jax.nn.sigmoid(g)
        silu = g * sg
        a_ref[...] = (silu * u).astype(BF16)
        dadu_ref[...] = silu.astype(BF16)
        dadg_ref[...] = (u * (sg * (1.0 + g * (1.0 - sg)))).astype(BF16)

        @pl.when(jnp.logical_and(r == 3, i == nrb - 1))
        def _():
            from_neighbour((2,), 1)
            from_neighbour((2,), 2)
            from_diagonal((2,))
            for k in range(3):
                for rel in (1, 2):
                    ici(k, rel).wait_send()
                for which in (0, 1):
                    quarter(k, which).wait_send()
                for rel in (1, 2, 3):
                    fwd(k, rel).wait_send()
                own(k).wait()

    o_spec = pl.BlockSpec((None, bm, FF_SHARD), lambda r, i: (r, i, 0))
    act = jax.ShapeDtypeStruct((N_CHIPS, s, FF_SHARD), BF16)
    full = jax.ShapeDtypeStruct((N_CHIPS, rows, cols), BF16)
    dma = pltpu.SemaphoreType.DMA
    if _PLAN is not None:
        _PLAN.last_slab_step = 3 * nrb
    n_spec = pl.BlockSpec((bm, D_MODEL), lambda r, i: (jnp.where(r == 0, i, nrb - 1), 0))
    return _call(
        body, name="ffn1_up", grid=(N_CHIPS, nrb),
        in_specs=[pl.BlockSpec((bm, D_MODEL), lambda r, i: (i, 0)), pl.BlockSpec((1, D_MODEL), lambda r, i: (0, 0)),
                  _ANY, _ANY, _ANY],
        out_specs=[n_spec, o_spec, o_spec, o_spec, _ANY, _ANY, _ANY],
        out_shape=[jax.ShapeDtypeStruct((s, D_MODEL), BF16), act, act, act, full, full, full],
        scratch_shapes=[pltpu.VMEM((rows, cols), BF16), pltpu.VMEM((rows, cols), BF16), dma((3, 2)), dma((3, 2)),
                        dma((3, 2)), dma((3, 2)), dma((3, 3)), dma((3, 3)), dma((3,)), dma((2,))],
        sem=("arbitrary", "arbitrary"), args=[xin, gain, g_sh, u_sh, d_sh])


def _residual_epilogue(alpha, with_norm):
    if not with_norm:
        return lambda acc, r: (r + alpha * acc,)

    def epilogue(acc, r, g):
        h = r + alpha * acc
        rs = lax.rsqrt(jnp.mean(h * h, axis=-1, keepdims=True) + RMS_EPS)
        return h, (h * rs) * g

    return epilogue


def _residual_outs(s, bm, gain):
    row = pl.BlockSpec((bm, D_MODEL), lambda i, k: (i, 0))
    outs = [(jax.ShapeDtypeStruct((s, D_MODEL), F32), row)]
    if gain is None:
        return [], outs
    return [(gain, pl.BlockSpec((1, D_MODEL), lambda i, k: (0, 0)))], outs + [(jax.ShapeDtypeStruct((s, D_MODEL), BF16), row)]


def _loss_epilogue(acc, res, g, target):
    d = acc.shape[-1]
    h = res + 0.5 * acc
    r = lax.rsqrt(jnp.mean(h * h, axis=-1, keepdims=True) + RMS_EPS)
    err = (h * r) * g - target
    part = 0.5 * jnp.sum(jnp.mean(err * err, axis=-1, keepdims=True), axis=0, keepdims=True)
    dx, dgain = _rms_bwd_math(err * (1.0 / d), h, g, d)
    return dx, dx, jnp.broadcast_to(part, (1, 128)), dgain


def _ffn_down(name, a, wd, res, gain=None, loss=None):
    s = a.shape[1]
    bm = _row_block(s, 512)
    row = pl.BlockSpec((bm, D_MODEL), lambda i, k: (i, 0))
    terms = [(a, pl.BlockSpec((None, bm, FF_SHARD), lambda i, k, j=j: (j, i, 0)),
              wd, pl.BlockSpec((None, FF_SHARD, D_MODEL), lambda i, k, j=j: (j, 0, 0)), NN) for j in range(N_CHIPS)]
    if loss is not None:
        vec = pl.BlockSpec((1, D_MODEL), lambda i, k: (0, 0))
        outs = [(jax.ShapeDtypeStruct((s, D_MODEL), F32), row), (jax.ShapeDtypeStruct((s, D_MODEL), BF16), row),
                (jax.ShapeDtypeStruct((1, 128), F32), pl.BlockSpec((1, 128), lambda i, k: (0, 0))),
                (jax.ShapeDtypeStruct((1, D_MODEL), F32), vec)]
        return _matmul(name, (s // bm, 1), terms, [(res, row), (loss[0], vec), (loss[1], row)], outs,
                       _loss_epilogue, None, summed=(2, 3))
    extras, outs = _residual_outs(s, bm, gain)
    res_out = _matmul(name, (s // bm, 1), terms, [(res, row)] + extras, outs,
                      _residual_epilogue(0.5, gain is not None), None)
    return res_out if gain is not None else res_out[0]


def _norm_bwd_epilogue(width):
    def epilogue(acc, h, g, dres):
        dx, dgain = _rms_bwd_math(acc, h, g, width)
        dx = dx + dres
        return dx, dx, dgain

    return epilogue


def _norm_bwd_operands(s, bm, h, gain, dres):
    row = pl.BlockSpec((bm, D_MODEL), lambda i, k: (i, 0))
    vec = pl.BlockSpec((1, D_MODEL), lambda i, k: (0, 0))
    extras = [(h, row), (gain, vec), (dres, row)]
    outs = [(jax.ShapeDtypeStruct((s, D_MODEL), F32), row), (jax.ShapeDtypeStruct((s, D_MODEL), BF16), row),
            (jax.ShapeDtypeStruct((1, D_MODEL), F32), vec)]
    return extras, outs, (2,)


def _ffn_bwd(tag, dh, n, dadg, dadu, a, wg, wu, wd, grads, norm_bwd=None):
    s = dh.shape[0]
    bm = _row_block(s)
    bk = _reduce_block(s)
    nk = s // bk

    def act_bwd(acc, dg_da, du_da):
        da = 0.5 * acc
        return da * dg_da.astype(F32), da * du_da.astype(F32)

    slab = pl.BlockSpec((None, bm, FF_SHARD), lambda j, i, k: (j, i, 0))
    shp = jax.ShapeDtypeStruct((N_CHIPS, s, FF_SHARD), BF16)
    dg, du = _matmul(
        tag + "_dact", (N_CHIPS, s // bm, 1),
        [(dh, pl.BlockSpec((bm, D_MODEL), lambda j, i, k: (i, 0)),
          wd, pl.BlockSpec((None, FF_SHARD, D_MODEL), lambda j, i, k: (j, 0, 0)), NT)],
        [(dadg, slab), (dadu, slab)], [(shp, slab), (shp, slab)], act_bwd, None)

    grads[tag + "_w_down"] = _matmul(
        tag + "_dwd", (N_CHIPS, nk),
        [(a, pl.BlockSpec((None, bk, FF_SHARD), lambda j, k: (j, k, 0)),
          dh, pl.BlockSpec((bk, D_MODEL), lambda j, k: (k, 0)), TN)],
        [], [(jax.ShapeDtypeStruct((N_CHIPS, FF_SHARD, D_MODEL), BF16),
              pl.BlockSpec((None, FF_SHARD, D_MODEL), lambda j, k: (j, 0, 0)))],
        lambda acc: (0.5 * acc,), (FF_SHARD, D_MODEL))[0]

    def dw_up(nm, dact):
        return _matmul(
            nm, (N_CHIPS, nk),
            [(dact, pl.BlockSpec((None, bk, FF_SHARD), lambda j, k: (j, k, 0)),
              n, pl.BlockSpec((bk, D_MODEL), lambda j, k: (k, 0)), TN)],
            [], [(jax.ShapeDtypeStruct((N_CHIPS, FF_SHARD, D_MODEL), BF16),
                  pl.BlockSpec((None, FF_SHARD, D_MODEL), lambda j, k: (j, 0, 0)))],
            _ident, (FF_SHARD, D_MODEL))[0]

    grads[tag + "_w_gate"] = dw_up(tag + "_dwg", dg)
    grads[tag + "_w_up"] = dw_up(tag + "_dwu", du)

    bn = _row_block(s, 512)
    steps = s // bn // 2
    prev, dgain = (), None
    for part, off in (("_dn_a", 0), ("_dn_b", steps)):
        row = pl.BlockSpec((bn, D_MODEL), lambda i, k, off=off: (i + off, 0))
        terms = []
        for j in range(N_CHIPS):
            a_slab = pl.BlockSpec((None, bn, FF_SHARD), lambda i, k, j=j, off=off: (j, i + off, 0))
            w_slab = pl.BlockSpec((None, FF_SHARD, D_MODEL), lambda i, k, j=j: (j, 0, 0))
            terms += [(dg, a_slab, wg, w_slab, NN), (du, a_slab, wu, w_slab, NN)]
        if norm_bwd is None:
            prev = _matmul(tag + part, (steps, 1), terms, [], [(jax.ShapeDtypeStruct((s, D_MODEL), F32), row)],
                           _ident, None, fill=prev)
            continue
        h, gain, dres = norm_bwd
        vec = pl.BlockSpec((1, D_MODEL), lambda i, k: (0, 0))
        res = _matmul(
            tag + part, (steps, 1), terms, [(h, row), (gain, vec), (dres, row)],
            [(jax.ShapeDtypeStruct((s, D_MODEL), F32), row), (jax.ShapeDtypeStruct((s, D_MODEL), BF16), row),
             (jax.ShapeDtypeStruct((1, D_MODEL), F32), vec)],
            _norm_bwd_epilogue(D_MODEL), None, fill=prev, summed=(2,))
        prev = res[:2]
        dgain = res[2] if dgain is None else dgain + res[2]
    return prev[0] if norm_bwd is None else (prev[0], prev[1], dgain)


def _mm_nn(name, a, b, out_dtype, res=None, gain=None):
    s, k = a.shape
    nn = b.shape[1]
    bm = _row_block(s)
    row = pl.BlockSpec((bm, nn), lambda i, kk: (i, 0))
    term = [(a, pl.BlockSpec((bm, k), lambda i, kk: (i, 0)), b, pl.BlockSpec((k, nn), lambda i, kk: (0, 0)), NN)]
    if res is None:
        return _matmul(name, (s // bm, 1), term, [], [(jax.ShapeDtypeStruct((s, nn), out_dtype), row)], _ident, None)[0]
    extras, outs = _residual_outs(s, bm, gain)
    res_out = _matmul(name, (s // bm, 1), term, [(res, row)] + extras, outs,
                      _residual_epilogue(1.0, gain is not None), None)
    return res_out if gain is not None else res_out[0]


def _mm_nt(name, a, b, out_dtype, attn_out=None, nh=0, dv=0):
    s, nn = a.shape
    k = b.shape[0]
    bm = _row_block(s)
    term = [(a, pl.BlockSpec((bm, nn), lambda i, kk: (i, 0)), b, pl.BlockSpec((k, nn), lambda i, kk: (0, 0)), NT)]
    out = (jax.ShapeDtypeStruct((s, k), out_dtype), pl.BlockSpec((bm, k), lambda i, kk: (i, 0)))
    if attn_out is None:
        return _matmul(name, (s // bm, 1), term, [], [out], _ident, None)[0]

    def with_delta(acc, o):
        do = acc.astype(out_dtype).astype(F32)
        cols = [jnp.sum(do[:, h * dv:(h + 1) * dv] * o[:, h * dv:(h + 1) * dv].astype(F32), axis=-1, keepdims=True)
                for h in range(nh)]
        return acc, jnp.stack(cols, axis=0)

    return _matmul(
        name, (s // bm, 1), term, [(attn_out, pl.BlockSpec((bm, nh * dv), lambda i, kk: (i, 0)))],
        [out, (jax.ShapeDtypeStruct((nh, s, 1), F32), pl.BlockSpec((nh, bm, 1), lambda i, kk: (0, i, 0)))],
        with_delta, None)


def _mm_nt_norm_bwd(name, a, b, h, gain, dres):
    s, nn = a.shape
    bm = _row_block(s, 512)
    extras, outs, summed = _norm_bwd_operands(s, bm, h, gain, dres)
    return _matmul(
        name, (s // bm, 1),
        [(a, pl.BlockSpec((bm, nn), lambda i, kk: (i, 0)), b, pl.BlockSpec(b.shape, lambda i, kk: (0, 0)), NT)],
        extras, outs, _norm_bwd_epilogue(D_MODEL), None, summed=summed)


def _w_in_dx_norm_bwd(dz, w_t, h, gain, dres):
    s = dz.shape[0]
    bm = _row_block(s, 512)
    epilogue = _norm_bwd_epilogue(D_MODEL)

    def body(dz_ref, w_ref, h_ref, g_ref, r_ref, dx_ref, dxb_ref, dg_ref):
        dzv = dz_ref[...]
        dn = jnp.concatenate([_dot(dzv, w_ref[j], NN) for j in range(N_CHIPS)], axis=1)
        dx, _, dgain = epilogue(dn, h_ref[...], g_ref[...], r_ref[...])
        dx_ref[...] = dx
        dxb_ref[...] = dx.astype(BF16)

        @pl.when(pl.program_id(0) == 0)
        def _():
            dg_ref[...] = dgain

        @pl.when(pl.program_id(0) > 0)
        def _():
            dg_ref[...] += dgain

    row = pl.BlockSpec((bm, D_MODEL), lambda i: (i, 0))
    vec = pl.BlockSpec((1, D_MODEL), lambda i: (0, 0))
    return _call(
        body, name="w_in_dx", grid=(s // bm,),
        in_specs=[row, pl.BlockSpec(w_t.shape, lambda i: (0, 0, 0)), row, vec, row],
        out_specs=[row, row, vec],
        out_shape=[jax.ShapeDtypeStruct((s, D_MODEL), F32), jax.ShapeDtypeStruct((s, D_MODEL), BF16),
                   jax.ShapeDtypeStruct((1, D_MODEL), F32)],
        sem=("arbitrary",), args=[dz, w_t, h, gain, dres])


def _mm_tn(name, a, b, out_dtype=BF16):
    s, k = a.shape
    nn = b.shape[1]
    bk = _reduce_block(s)
    return _matmul(
        name, (s // bk,),
        [(a, pl.BlockSpec((bk, k), lambda kk: (kk, 0)), b, pl.BlockSpec((bk, nn), lambda kk: (kk, 0)), TN)],
        [], [(jax.ShapeDtypeStruct((k, nn), out_dtype), pl.BlockSpec((k, nn), lambda kk: (0, 0)))],
        _ident, (k, nn))[0]


def _mm_heads_fwd(name, a, w, out_dtype, w_transposed=False):
    s, k = a.shape
    nh = w.shape[0]
    nn = w.shape[1] if w_transposed else w.shape[2]
    bm = _row_block(s)
    return _matmul(
        name, (nh, s // bm, 1),
        [(a, pl.BlockSpec((bm, k), lambda h, i, kk: (i, 0)),
          w, pl.BlockSpec((None,) + w.shape[1:], lambda h, i, kk: (h, 0, 0)), NT if w_transposed else NN)],
        [], [(jax.ShapeDtypeStruct((s, nh * nn), out_dtype), pl.BlockSpec((bm, nn), lambda h, i, kk: (i, h)))],
        _ident, None)[0]


def _mm_heads_bwd(name, dy, a, w, w_transposed=False):
    s, k = a.shape
    nh = w.shape[0]
    nn = w.shape[1] if w_transposed else w.shape[2]
    bm = _row_block(s)
    bk = _reduce_block(s)
    w_spec = pl.BlockSpec((None,) + w.shape[1:], lambda i, h: (h, 0, 0))
    da = _matmul(
        name + "_dx", (s // bm, nh),
        [(dy, pl.BlockSpec((bm, nn), lambda i, h: (i, h)), w, w_spec, NN if w_transposed else NT)],
        [], [(jax.ShapeDtypeStruct((s, k), F32), pl.BlockSpec((bm, k), lambda i, h: (i, 0)))], _ident, (bm, k))[0]
    a_term = (a, pl.BlockSpec((bk, k), lambda h, kk: (kk, 0)))
    dy_term = (dy, pl.BlockSpec((bk, nn), lambda h, kk: (kk, h)))
    lhs, rhs = (dy_term, a_term) if w_transposed else (a_term, dy_term)
    dw = _matmul(
        name + "_dw", (nh, s // bk), [lhs + rhs + (TN,)],
        [], [(jax.ShapeDtypeStruct(w.shape, BF16), pl.BlockSpec((None,) + w.shape[1:], lambda h, kk: (h, 0, 0)))],
        _ident, w.shape[1:])[0]
    return da, dw


def _w_in_fwd(n, w_t):
    s = n.shape[0]
    bm = _row_block(s)
    nh, nout, kin = w_t.shape
    terms = [(n, pl.BlockSpec((bm, kin), lambda i, k, j=j: (i, j)),
              w_t, pl.BlockSpec((None, nout, kin), lambda i, k, j=j: (j, 0, 0)), NT) for j in range(nh)]
    row = pl.BlockSpec((bm, nout), lambda i, k: (i, 0))
    return _matmul("w_in", (s // bm, 1), terms, [], [(jax.ShapeDtypeStruct((s, nout), F32), row)], _ident, None)[0]


def _w_in_dw(dz, n):
    s, nout = dz.shape
    kin = n.shape[1] // N_CHIPS
    bk = _reduce_block(s)
    return _matmul(
        "w_in_dw", (N_CHIPS, s // bk),
        [(dz, pl.BlockSpec((bk, nout), lambda j, k: (k, 0)), n, pl.BlockSpec((bk, kin), lambda j, k: (k, j)), TN)],
        [], [(jax.ShapeDtypeStruct((N_CHIPS, nout, kin), BF16), pl.BlockSpec((None, nout, kin), lambda j, k: (j, 0, 0)))],
        _ident, (nout, kin))[0]


def _rope_tables(positions):
    half = ROPE_DIM // 2
    freqs = 1.0 / (ROPE_BASE ** (jnp.arange(0, ROPE_DIM, 2, dtype=F32) / ROPE_DIM))
    ang = positions.astype(F32)[:, None] * freqs
    cos, sin = jnp.cos(ang), jnp.sin(ang)
    z = jnp.zeros_like(cos)
    tc = jnp.concatenate([cos, cos, z, z], axis=-1)
    ta = jnp.concatenate([-sin, z, z, z], axis=-1)
    tb = jnp.concatenate([z, sin, z, z], axis=-1)
    assert tc.shape[-1] == 4 * half
    return tc, ta, tb


def _rope(x, tc, ta, tb):
    return x * tc + pltpu.roll(x, 96, 1) * ta + pltpu.roll(x, 32, 1) * tb


def _rope_t(dy, tc, ta, tb):
    return dy * tc + pltpu.roll(dy * ta, 32, 1) + pltpu.roll(dy * tb, 96, 1)


def _q_rope(q, tc, ta, tb, transpose):
    s = q.shape[0]
    bm = _row_block(s, 512)
    rot = _rope_t if transpose else _rope

    def body(q_ref, tc_ref, ta_ref, tb_ref, o_ref):
        c, a, b = tc_ref[...], ta_ref[...], tb_ref[...]
        for h in range(MLA_HEADS):
            lo = h * HEAD_QK
            o_ref[:, lo:lo + 128] = q_ref[:, lo:lo + 128].astype(BF16)
            o_ref[:, lo + 128:lo + 256] = rot(q_ref[:, lo + 128:lo + 256], c, a, b).astype(BF16)

    row = pl.BlockSpec((bm, MLA_HEADS * HEAD_QK), lambda i: (i, 0))
    tab = pl.BlockSpec((bm, 128), lambda i: (i, 0))
    return pl.pallas_call(
        body, name="q_rope_t" if transpose else "q_rope", grid=(s // bm,), in_specs=[row, tab, tab, tab],
        out_specs=row, out_shape=jax.ShapeDtypeStruct((s, MLA_HEADS * HEAD_QK), BF16),
        compiler_params=_params("parallel"),
    )(q, tc, ta, tb)


def _kv_assemble(kv, z, tc, ta, tb):
    s = kv.shape[0]
    bm = _row_block(s, 512)

    def body(kv_ref, kr_ref, tc_ref, ta_ref, tb_ref, k_ref, v_ref):
        kpe = _rope(kr_ref[...], tc_ref[...], ta_ref[...], tb_ref[...]).astype(BF16)
        for h in range(MLA_HEADS):
            lo = h * 256
            k_ref[:, lo:lo + 128] = kv_ref[:, lo:lo + 128].astype(BF16)
            k_ref[:, lo + 128:lo + 256] = kpe
            v_ref[:, h * 128:(h + 1) * 128] = kv_ref[:, lo + 128:lo + 256].astype(BF16)

    row = pl.BlockSpec((bm, 1024), lambda i: (i, 0))
    tab = pl.BlockSpec((bm, 128), lambda i: (i, 0))
    return pl.pallas_call(
        body, name="kv_assemble", grid=(s // bm,),
        in_specs=[row, pl.BlockSpec((bm, 128), lambda i: (i, 3)), tab, tab, tab],
        out_specs=[row, pl.BlockSpec((bm, 512), lambda i: (i, 0))],
        out_shape=[jax.ShapeDtypeStruct((s, 1024), BF16), jax.ShapeDtypeStruct((s, 512), BF16)],
        compiler_params=_params("parallel"),
    )(kv, z, tc, ta, tb)


def _kv_assemble_bwd(dk, dv, tc, ta, tb):
    s = dk.shape[0]
    bm = _row_block(s, 512)

    def body(dk_ref, dv_ref, tc_ref, ta_ref, tb_ref, dkv_ref, dkr_ref):
        dpe = None
        for h in range(MLA_HEADS):
            lo = h * 256
            dkv_ref[:, lo:lo + 128] = dk_ref[:, lo:lo + 128].astype(BF16)
            dkv_ref[:, lo + 128:lo + 256] = dv_ref[:, h * 128:(h + 1) * 128].astype(BF16)
            t = dk_ref[:, lo + 128:lo + 256]
            dpe = t if dpe is None else dpe + t
        dkr_ref[...] = _rope_t(dpe, tc_ref[...], ta_ref[...], tb_ref[...])

    row = pl.BlockSpec((bm, 1024), lambda i: (i, 0))
    tab = pl.BlockSpec((bm, 128), lambda i: (i, 0))
    return pl.pallas_call(
        body, name="kv_assemble_bwd", grid=(s // bm,),
        in_specs=[row, pl.BlockSpec((bm, 512), lambda i: (i, 0)), tab, tab, tab],
        out_specs=[row, tab],
        out_shape=[jax.ShapeDtypeStruct((s, 1024), BF16), jax.ShapeDtypeStruct((s, 128), F32)],
        compiler_params=_params("parallel"),
    )(dk, dv, tc, ta, tb)


def _norm_bf16(x, g):
    r = lax.rsqrt(jnp.mean(x * x, axis=-1, keepdims=True) + RMS_EPS)
    return ((x * r) * g).astype(BF16)


def _qkv_prep(z, q_gain, kv_gain, wq_t, wkv, tc, ta, tb):
    s = z.shape[0]
    bm = _row_block(s, 512)

    def body(zq_ref, zkv_ref, zkr_ref, qg_ref, kvg_ref, wq_ref, wkv_ref, tc_ref, ta_ref, tb_ref,
             qn_ref, kvn_ref, q_ref, k_ref, v_ref):
        c, a, b = tc_ref[...], ta_ref[...], tb_ref[...]
        qn = _norm_bf16(zq_ref[...], qg_ref[...])
        kvn = _norm_bf16(zkv_ref[...], kvg_ref[...])
        qn_ref[...] = qn
        kvn_ref[...] = kvn
        kpe = _rope(zkr_ref[...], c, a, b).astype(BF16)
        for h in range(MLA_HEADS):
            lo = h * HEAD_QK
            qp = _dot(qn, wq_ref[h], NT)
            q_ref[:, lo:lo + 128] = qp[:, :128].astype(BF16)
            q_ref[:, lo + 128:lo + 256] = _rope(qp[:, 128:], c, a, b).astype(BF16)
            kv = _dot(kvn, wkv_ref[h], NN)
            k_ref[:, lo:lo + 128] = kv[:, :128].astype(BF16)
            k_ref[:, lo + 128:lo + 256] = kpe
            v_ref[:, h * HEAD_V:(h + 1) * HEAD_V] = kv[:, 128:].astype(BF16)

    def cols(width, blk):
        return pl.BlockSpec((bm, width), lambda i: (i, blk))

    def whole(a):
        return pl.BlockSpec(a.shape, lambda i: (0,) * a.ndim)

    tab = cols(128, 0)
    return _call(
        body, name="qkv_prep", grid=(s // bm,),
        in_specs=[cols(Q_LORA, 0), cols(KV_LORA, 2), cols(128, 3), whole(q_gain), whole(kv_gain), whole(wq_t),
                  whole(wkv), tab, tab, tab],
        out_specs=[cols(Q_LORA, 0), cols(KV_LORA, 0), cols(1024, 0), cols(1024, 0), cols(512, 0)],
        out_shape=[jax.ShapeDtypeStruct((s, Q_LORA), BF16), jax.ShapeDtypeStruct((s, KV_LORA), BF16),
                   jax.ShapeDtypeStruct((s, 1024), BF16), jax.ShapeDtypeStruct((s, 1024), BF16),
                   jax.ShapeDtypeStruct((s, 512), BF16)],
        sem=("parallel",), args=[z, z, z, q_gain, kv_gain, wq_t, wkv, tc, ta, tb])


def _qkv_prep_bwd(dq, dk, dv, z, qn, kvn, q_gain, kv_gain, wq_t, wkv, tc, ta, tb):
    s = z.shape[0]
    bm = _row_block(s, 512)
    nsteps = s // bm

    def body(dq_ref, dk_ref, dv_ref, zq_ref, zkv_ref, qn_ref, kvn_ref, qg_ref, kvg_ref, wq_ref, wkv_ref,
             tc_ref, ta_ref, tb_ref, dz_ref, dqg_ref, dkvg_ref, dwq_ref, dwkv_ref, wq_acc, wkv_acc):
        i = pl.program_id(0)
        c, a, b = tc_ref[...], ta_ref[...], tb_ref[...]

        @pl.when(i == 0)
        def _():
            wq_acc[...] = jnp.zeros_like(wq_acc)
            wkv_acc[...] = jnp.zeros_like(wkv_acc)

        qn, kvn = qn_ref[...], kvn_ref[...]
        dqn = jnp.zeros((bm, Q_LORA), F32)
        dkvn = jnp.zeros((bm, KV_LORA), F32)
        dpe = jnp.zeros((bm, 128), F32)
        for h in range(MLA_HEADS):
            lo = h * HEAD_QK
            dqp = jnp.concatenate([dq_ref[:, lo:lo + 128].astype(BF16),
                                   _rope_t(dq_ref[:, lo + 128:lo + 256], c, a, b).astype(BF16)], axis=1)
            dqn = dqn + _dot(dqp, wq_ref[h], NN)
            wq_acc[h] += _dot(dqp, qn, TN)
            dkv = jnp.concatenate([dk_ref[:, lo:lo + 128].astype(BF16),
                                   dv_ref[:, h * HEAD_V:(h + 1) * HEAD_V].astype(BF16)], axis=1)
            dkvn = dkvn + _dot(dkv, wkv_ref[h], NT)
            wkv_acc[h] += _dot(kvn, dkv, TN)
            dpe = dpe + dk_ref[:, lo + 128:lo + 256]
        dcq, dqg = _rms_bwd_math(dqn, zq_ref[...], qg_ref[...], Q_LORA)
        dckv, dkvg = _rms_bwd_math(dkvn, zkv_ref[...], kvg_ref[...], KV_LORA)
        dz_ref[:, 0:Q_LORA] = dcq.astype(BF16)
        dz_ref[:, Q_LORA:Q_LORA + KV_LORA] = dckv.astype(BF16)
        dz_ref[:, Q_LORA + KV_LORA:512] = _rope_t(dpe, c, a, b).astype(BF16)

        @pl.when(i == 0)
        def _():
            dqg_ref[...] = dqg
            dkvg_ref[...] = dkvg

        @pl.when(i > 0)
        def _():
            dqg_ref[...] += dqg
            dkvg_ref[...] += dkvg

        @pl.when(i == nsteps - 1)
        def _():
            dwq_ref[...] = wq_acc[...].astype(BF16)
            dwkv_ref[...] = wkv_acc[...].astype(BF16)

    def cols(width, blk):
        return pl.BlockSpec((bm, width), lambda i: (i, blk))

    def whole(shape):
        return pl.BlockSpec(shape, lambda i: (0,) * len(shape))

    tab = cols(128, 0)
    return _call(
        body, name="qkv_prep_bwd", grid=(nsteps,),
        in_specs=[cols(1024, 0), cols(1024, 0), cols(512, 0), cols(Q_LORA, 0), cols(KV_LORA, 2), cols(Q_LORA, 0),
                  cols(KV_LORA, 0), whole(q_gain.shape), whole(kv_gain.shape), whole(wq_t.shape), whole(wkv.shape),
                  tab, tab, tab],
        out_specs=[cols(512, 0), whole(q_gain.shape), whole(kv_gain.shape), whole(wq_t.shape), whole(wkv.shape)],
        out_shape=[jax.ShapeDtypeStruct((s, 512), BF16), jax.ShapeDtypeStruct(q_gain.shape, F32),
                   jax.ShapeDtypeStruct(kv_gain.shape, F32), jax.ShapeDtypeStruct(wq_t.shape, BF16),
                   jax.ShapeDtypeStruct(wkv.shape, BF16)],
        scratch_shapes=[pltpu.VMEM(wq_t.shape, F32), pltpu.VMEM(wkv.shape, F32)],
        sem=("arbitrary",), args=[dq, dk, dv, z, z, qn, kvn, q_gain, kv_gain, wq_t, wkv, tc, ta, tb])


def _causal_mask(s, row0, col0):
    rows = row0 + lax.broadcasted_iota(jnp.int32, s.shape, 0)
    cols = col0 + lax.broadcasted_iota(jnp.int32, s.shape, 1)
    return jnp.where(cols <= rows, s, -jnp.inf)


def _attn_fwd(name, q, k, k_off, v, v_off, nh, dq, dv, scale, causal, blk):
    sq, sk = q.shape[0], k.shape[0]
    bq = min(blk, sq)
    bk = min(blk, sk)
    nkv = sk // bk
    assert not causal or (sq == sk and bq == bk)

    hq = bq
    log2e = 1.4426950408889634
    c2 = scale * log2e

    def body(q_ref, k_ref, v_ref, o_ref, lse_ref):
        qi = pl.program_id(1)
        qs = (q_ref[...],)

        def step(j, carry, masked):
            rows = pl.ds(pl.multiple_of(j * bk, bk), bk)
            kb, vb = k_ref[rows, :], v_ref[rows, :]
            out = []
            for t, (m, l, acc) in enumerate(carry):
                s = _dot(qs[t], kb, NT) * c2
                if masked:
                    s = _causal_mask(s, qi * bq + t * hq, j * bk)
                m_new = jnp.maximum(m, jnp.max(s, axis=-1, keepdims=True))
                alpha = jnp.exp2(m - m_new)
                p = jnp.exp2(s - m_new)
                l = alpha * l + jnp.sum(p, axis=-1, keepdims=True)
                acc = alpha * acc + _dot(p, vb, NN)
                out.append((m_new, l, acc))
            return tuple(out)

        one = (jnp.full((hq, 1), -jnp.inf, F32), jnp.zeros((hq, 1), F32), jnp.zeros((hq, dv), F32))
        init = (one,)
        if causal:
            carry = lax.fori_loop(0, qi, lambda j, c: step(j, c, False), init)
            fin = step(qi, carry, True)
        else:
            fin = lax.fori_loop(0, nkv, lambda j, c: step(j, c, False), init)
        for t, (m, l, acc) in enumerate(fin):
            o_ref[t * hq:(t + 1) * hq, :] = (acc / l).astype(o_ref.dtype)
            lse_ref[t * hq:(t + 1) * hq, :] = m * (1.0 / log2e) + jnp.log(l)

    return _call(
        body, name=name, grid=(nh, sq // bq),
        in_specs=[pl.BlockSpec((bq, dq), lambda h, i: (i, h)),
                  pl.BlockSpec((sk, dq), lambda h, i: (0, k_off + h)),
                  pl.BlockSpec((sk, dv), lambda h, i: (0, v_off + h))],
        out_specs=[pl.BlockSpec((bq, dv), lambda h, i: (i, h)), pl.BlockSpec((None, bq, 1), lambda h, i: (h, i, 0))],
        out_shape=[jax.ShapeDtypeStruct((sq, nh * dv), BF16), jax.ShapeDtypeStruct((nh, sq, 1), F32)],
        sem=("parallel", "parallel"), args=[q, k, v])


def _attn_delta(name, do, do_off, o, nh, dv):
    s = o.shape[0]
    bm = _row_block(s, 512)

    def body(do_ref, o_ref, d_ref):
        d_ref[...] = jnp.sum(do_ref[...].astype(F32) * o_ref[...].astype(F32), axis=-1, keepdims=True)

    return pl.pallas_call(
        body, name=name, grid=(nh, s // bm),
        in_specs=[pl.BlockSpec((bm, dv), lambda h, i: (i, do_off + h)), pl.BlockSpec((bm, dv), lambda h, i: (i, h))],
        out_specs=pl.BlockSpec((None, bm, 1), lambda h, i: (h, i, 0)),
        out_shape=jax.ShapeDtypeStruct((nh, s, 1), F32),
        compiler_params=_params("parallel", "parallel"),
    )(do, o)


def _attn_bwd(name, q, k, k_off, v, v_off, do, do_off, lse, delta, nh, dq, dv, scale, causal, blk):
    sq, sk = q.shape[0], k.shape[0]
    bq = min(blk, sq)
    bk = min(blk, sk)
    nq = sq // bq
    assert not causal or (sq == sk and bq == bk)

    def body(q_ref, k_ref, v_ref, do_ref, lse_ref, dl_ref, dq_ref, dk_ref, dv_ref, dk_acc, dv_acc):
        j = pl.program_id(1)

        @pl.when(j == 0)
        def _():
            dq_ref[...] = jnp.zeros_like(dq_ref)

        dk_acc[...] = jnp.zeros_like(dk_acc)
        dv_acc[...] = jnp.zeros_like(dv_acc)
        kv = k_ref[...]
        vv = v_ref[...]

        def step(i, masked):
            rows = pl.ds(pl.multiple_of(i * bq, bq), bq)
            qv = q_ref[rows, :]
            dov = do_ref[rows, :].astype(BF16)
            s = _dot(qv, kv, NT) * scale
            if masked:
                s = _causal_mask(s, i * bq, j * bk)
            p = jnp.exp(s - lse_ref[rows, :])
            dp = _dot(dov, vv, NT)
            ds = (p * (dp - dl_ref[rows, :]) * scale).astype(BF16)
            dv_acc[...] += _dot(p, dov, TN)
            dk_acc[...] += _dot(ds, qv, TN)
            dq_ref[rows, :] += _dot(ds, kv, NN)

        if causal:
            step(j, True)

            def loop(i, c):
                step(i, False)
                return c

            lax.fori_loop(j + 1, nq, loop, 0)
        else:
            def loop(i, c):
                step(i, False)
                return c

            lax.fori_loop(0, nq, loop, 0)
        dk_ref[...] = dk_acc[...]
        dv_ref[...] = dv_acc[...]

    stat = pl.BlockSpec((None, sq, 1), lambda h, j: (h, 0, 0))
    return _call(
        body, name=name, grid=(nh, sk // bk),
        in_specs=[pl.BlockSpec((sq, dq), lambda h, j: (0, h)),
                  pl.BlockSpec((bk, dq), lambda h, j: (j, k_off + h)),
                  pl.BlockSpec((bk, dv), lambda h, j: (j, v_off + h)),
                  pl.BlockSpec((sq, dv), lambda h, j: (0, do_off + h)), stat, stat],
        out_specs=[pl.BlockSpec((sq, dq), lambda h, j: (0, h)),
                   pl.BlockSpec((bk, dq), lambda h, j: (j, h)),
                   pl.BlockSpec((bk, dv), lambda h, j: (j, h))],
        out_shape=[jax.ShapeDtypeStruct((sq, nh * dq), F32), jax.ShapeDtypeStruct((sk, nh * dq), F32),
                   jax.ShapeDtypeStruct((sk, nh * dv), F32)],
        scratch_shapes=[pltpu.VMEM((bk, dq), F32), pltpu.VMEM((bk, dv), F32)],
        sem=("parallel", "arbitrary"), args=[q, k, v, do, lse, delta])


def _pool_diff(z, g):
    s = z.shape[0]
    t = lax.broadcasted_iota(jnp.int32, z.shape, 0)
    acc = z
    sums = []
    for k in (1, 2, 4, 8):
        acc = acc + jnp.where(t >= k, pltpu.roll(acc, k, 0), 0.0)
        sums.append(acc)
    win = jnp.where(g == 0, sums[0], jnp.where(g == 1, sums[1], jnp.where(g == 2, sums[2], sums[3])))
    w = lax.shift_left(jnp.int32(2), g)
    count = jnp.minimum(t + 1, w).astype(F32)
    del s
    return win / count - z, count


def _pool_fwd(z, pool_w, pool_scale):
    s = z.shape[0]

    def body(z_ref, w_ref, sc_ref, o_ref):
        diff, _ = _pool_diff(z_ref[...], pl.program_id(0))
        o_ref[...] = (_dot(diff, w_ref[...], NN) * sc_ref[...]).astype(o_ref.dtype)

    return _call(
        body, name="pool_fwd", grid=(POOL_GROUPS,),
        in_specs=[pl.BlockSpec((s, POOL_CH), lambda g: (0, 4 + g)),
                  pl.BlockSpec((None, POOL_CH, POOL_CH), lambda g: (g, 0, 0)),
                  pl.BlockSpec((1, POOL_CH), lambda g: (0, g))],
        out_specs=[pl.BlockSpec((s, POOL_CH), lambda g: (0, g))],
        out_shape=[jax.ShapeDtypeStruct((s, POOL_GROUPS * POOL_CH), BF16)],
        sem=("parallel",), args=[z, pool_w, pool_scale])[0]


def _pool_bwd(dcat, z, pool_w, pool_scale):
    s = z.shape[0]

    def body(dp_ref, z_ref, w_ref, sc_ref, dz_ref, dw_ref, dsc_ref):
        g = pl.program_id(0)
        diff, count = _pool_diff(z_ref[...], g)
        dpf = dp_ref[...].astype(F32)
        u = _dot(diff, w_ref[...], NN)
        dsc_ref[...] = jnp.sum(dpf * u, axis=0, keepdims=True)
        du = (dpf * sc_ref[...]).astype(BF16)
        dw_ref[...] = _dot(diff, du, TN)
        ddiff = _dot(du, w_ref[...], NT)
        t = lax.broadcasted_iota(jnp.int32, ddiff.shape, 0)
        acc = ddiff / count
        sums = []
        for k in (1, 2, 4, 8):
            acc = acc + jnp.where(t < s - k, pltpu.roll(acc, s - k, 0), 0.0)
            sums.append(acc)
        win = jnp.where(g == 0, sums[0], jnp.where(g == 1, sums[1], jnp.where(g == 2, sums[2], sums[3])))
        dz_ref[...] = win - ddiff

    return pl.pallas_call(
        body, name="pool_bwd", grid=(POOL_GROUPS,),
        in_specs=[pl.BlockSpec((s, POOL_CH), lambda g: (0, 4 + g)),
                  pl.BlockSpec((s, POOL_CH), lambda g: (0, 4 + g)),
                  pl.BlockSpec((None, POOL_CH, POOL_CH), lambda g: (g, 0, 0)),
                  pl.BlockSpec((1, POOL_CH), lambda g: (0, g))],
        out_specs=[pl.BlockSpec((s, POOL_CH), lambda g: (0, g)),
                   pl.BlockSpec((None, POOL_CH, POOL_CH), lambda g: (g, 0, 0)),
                   pl.BlockSpec((1, POOL_CH), lambda g: (0, g))],
        out_shape=[jax.ShapeDtypeStruct((s, POOL_GROUPS * POOL_CH), F32),
                   jax.ShapeDtypeStruct((POOL_GROUPS, POOL_CH, POOL_CH), F32),
                   jax.ShapeDtypeStruct((1, POOL_GROUPS * POOL_CH), F32)],
        compiler_params=_params("parallel"),
    )(dcat, z, pool_w, pool_scale)


def _local_step(x, mem, positions, target, w, grads):
    tc, ta, tb = _rope_tables(positions)
    blk = _ATT_BLOCK

    if "ffn1_shards" in w:
        n1, a1, dadu1, dadg1, w["ffn1_w_gate"], w["ffn1_w_up"], w["ffn1_w_down"] = _ffn1_up_gather(
            x, w["ffn1_norm"], *w["ffn1_shards"])
    else:
        n1 = _rmsnorm_fwd("ffn1_norm", x, w["ffn1_norm"], D_MODEL)
        a1, dadu1, dadg1 = _ffn_up("ffn1_up", n1, w["ffn1_w_gate"], w["ffn1_w_up"])
    h1, n2 = _ffn_down("ffn1_down", a1, w["ffn1_w_down"], x, w["mix_norm"])
    z = _w_in_fwd(n2, w["w_in"])
    qn, kvn, qf, kf, vf = _qkv_prep(z, w["q_norm"], w["kv_norm"], w["w_q_up"], w["w_kv_up"], tc, ta, tb)
    att, lse = _attn_fwd("mla_fwd", qf, kf, 0, vf, 0, MLA_HEADS, HEAD_QK, HEAD_V, MLA_SCALE, True, blk)
    pool = _pool_fwd(z, w["pool_w"], w["pool_scale"])
    s = x.shape[0]
    bm = _row_block(s)
    row = pl.BlockSpec((bm, D_MODEL), lambda i, k: (i, 0))
    half = pl.BlockSpec((bm, 512), lambda i, k: (i, 0))
    h2, n3 = _matmul(
        "w_out", (s // bm, 1),
        [(att, half, w["w_out"], pl.BlockSpec((512, D_MODEL), lambda i, k: (0, 0)), NN),
         (pool, half, w["w_out"], pl.BlockSpec((512, D_MODEL), lambda i, k: (1, 0)), NN)],
        [(h1, row)] + _residual_outs(s, bm, w["xattn_norm"])[0], _residual_outs(s, bm, w["xattn_norm"])[1],
        _residual_epilogue(1.0, True), None)
    memn = _rmsnorm_fwd("mem_norm", mem, w["mem_norm"], D_MODEL)
    qm = _mm_nn("w_mq", n3, w["w_mq"], BF16)
    kvm = _mm_heads_fwd("w_mkv", memn, w["w_mkv"], BF16)
    om, lse_m = _attn_fwd("xattn_fwd", qm, kvm, 0, kvm, MEM_HEADS, MEM_HEADS, MEM_HEAD_DIM, MEM_HEAD_DIM,
                          MEM_SCALE, False, blk)
    h3, n4 = _mm_nn("w_mo", om, w["w_mo"], F32, res=h2, gain=w["ffn2_norm"])
    a2, dadu2, dadg2 = _ffn_up("ffn2_up", n4, w["ffn2_w_gate"], w["ffn2_w_up"])
    dh4, dh4b, loss_vec, d_final = _ffn_down("ffn2_down", a2, w["ffn2_w_down"], h3, loss=(w["final_norm"], target))
    grads["final_norm"] = d_final

    dh3, dh3b, grads["ffn2_norm"] = _ffn_bwd("ffn2", dh4b, n4, dadg2, dadu2, a2, w["ffn2_w_gate"], w["ffn2_w_up"],
                                             w["ffn2_w_down"], grads, norm_bwd=(h3, w["ffn2_norm"], dh4))

    dom, delta_m = _mm_nt("w_mo_dx", dh3b, w["w_mo"], BF16, attn_out=om, nh=MEM_HEADS, dv=MEM_HEAD_DIM)
    grads["w_mo"] = _mm_tn("w_mo_dw", om, dh3b)
    dqm, dkm, dvm = _attn_bwd("xattn_bwd", qm, kvm, 0, kvm, MEM_HEADS, dom, 0, lse_m, delta_m, MEM_HEADS,
                              MEM_HEAD_DIM, MEM_HEAD_DIM, MEM_SCALE, False, blk)
    dkvm = jnp.concatenate([dkm, dvm], axis=1).astype(BF16)
    dh2, dh2b, grads["xattn_norm"] = _mm_nt_norm_bwd("w_mq_dx", dqm, w["w_mq"], h2, w["xattn_norm"], dh3)
    grads["w_mq"] = _mm_tn("w_mq_dw", n3, dqm)
    dmemn, grads["w_mkv"] = _mm_heads_bwd("w_mkv", dkvm, memn, w["w_mkv"])
    _, grads["mem_norm"] = _rmsnorm_bwd("mem_norm_bwd", dmemn, mem, w["mem_norm"], D_MODEL, out_dtype=BF16)

    dcat, delta = _mm_nt("w_out_dx", dh2b, w["w_out"], BF16, attn_out=att, nh=MLA_HEADS, dv=HEAD_V)
    grads["w_out"] = jnp.concatenate([_mm_tn("w_out_dw_a", att, dh2b), _mm_tn("w_out_dw_p", pool, dh2b)], axis=0)
    dzp, grads["pool_w"], grads["pool_scale"] = _pool_bwd(dcat, z, w["pool_w"], w["pool_scale"])
    dqf, dkf, dvf = _attn_bwd("mla_bwd", qf, kf, 0, vf, 0, dcat, 0, lse, delta, MLA_HEADS, HEAD_QK, HEAD_V,
                              MLA_SCALE, True, blk)
    dz_lat, grads["q_norm"], grads["kv_norm"], grads["w_q_up"], grads["w_kv_up"] = _qkv_prep_bwd(
        dqf, dkf, dvf, z, qn, kvn, w["q_norm"], w["kv_norm"], w["w_q_up"], w["w_kv_up"], tc, ta, tb)
    dz = jnp.concatenate([dz_lat, dzp.astype(BF16)], axis=1)
    grads["w_in"] = _w_in_dw(dz, n2)
    dh1, dh1b, grads["mix_norm"] = _w_in_dx_norm_bwd(dz, w["w_in"], h1, w["mix_norm"], dh2)

    dn1 = _ffn_bwd("ffn1", dh1b, n1, dadg1, dadu1, a1, w["ffn1_w_gate"], w["ffn1_w_up"], w["ffn1_w_down"], grads)
    dx, grads["ffn1_norm"], _ = _rmsnorm_bwd("ffn1_norm_bwd", dn1, x, w["ffn1_norm"], D_MODEL, dres=dh1)
    return loss_vec[0, 0], dx


def _mesh_pos():
    x, y, c = lax.axis_index("x"), lax.axis_index("y"), lax.axis_index("c")
    chips = [(1 - x, y), (x, 1 - y), (1 - x, 1 - y)]
    chip_ids = [2 * cx + cy for cx, cy in chips]
    return x, y, c, 2 * x + y, chips, chip_ids


def _half_rows(c, rows):
    hr = rows // 2
    return pl.ds(pl.multiple_of(c * hr, 16), hr), pl.ds(pl.multiple_of((1 - c) * hr, 16), hr)


def _ag_ici_stage(shards, relative=False, peers=(0, 1, 2), into=None):
    n = len(shards)
    first = into is None

    def copies(ins, outs):
        x, y, c, me, chips, _ = _mesh_pos()
        out = []
        for k in range(n):
            mine, _ = _half_rows(c, ins[k].shape[0])
            if first:
                out.append((ins[k], outs[k].at[0 if relative else me], None))
            for j in peers:
                slab = _REL_OF_PEER[j] if relative else me
                out.append((ins[k].at[mine], outs[k].at[slab, mine], (*chips[j], c)))
        return out

    outs = [jax.ShapeDtypeStruct((N_CHIPS,) + s.shape, s.dtype) for s in shards]
    if first:
        return _Stage(shards, outs, len(peers) * n, n, copies)
    return _Stage(list(shards) + list(into), outs, len(peers) * n, 0, copies, aliases={n + k: k for k in range(n)})


def _quarter_rows(c, rows):
    qr = rows // 4
    return pl.ds(pl.multiple_of(c * 2 * qr, 16), qr), pl.ds(pl.multiple_of(c * 2 * qr + qr, 16), qr)


def _ag_fwd_stage(fulls, relative=False):
    n = len(fulls)

    def copies(ins, outs):
        x, y, c, me, chips, chip_ids = _mesh_pos()
        out = []
        for k in range(n):
            q0, q1 = _quarter_rows(c, ins[k].shape[1])
            from_x, from_y, diag = (2, 1, 3) if relative else chip_ids
            out.append((ins[k].at[from_x, q0], outs[k].at[diag if relative else from_x, q0], (*chips[1], c)))
            out.append((ins[k].at[from_y, q1], outs[k].at[diag if relative else from_y, q1], (*chips[0], c)))
        return out

    return _Stage(fulls, [jax.ShapeDtypeStruct(f.shape, f.dtype) for f in fulls], 2 * n, 0, copies,
                  aliases={k: k for k in range(n)})


def _ag_d2d_stage(fulls, relative=False):
    n = len(fulls)

    def copies(ins, outs):
        x, y, c, me, _, chip_ids = _mesh_pos()
        out = []
        for k in range(n):
            mine, _ = _half_rows(c, ins[k].shape[1])
            for j in range(3):
                slab = _REL_OF_PEER[j] if relative else chip_ids[j]
                out.append((ins[k].at[slab, mine], outs[k].at[slab, mine], (x, y, 1 - c)))
        return out

    return _Stage(fulls, [jax.ShapeDtypeStruct(f.shape, f.dtype) for f in fulls], 3 * n, 0, copies,
                  aliases={k: k for k in range(n)})


def _rs_swap_stage(grads):
    n = len(grads)

    def copies(ins, outs):
        x, y, c, _, _, _ = _mesh_pos()
        out = []
        for k in range(n):
            _, other = _half_rows(c, ins[k].shape[1])
            out.append((ins[k].at[:, other, :], outs[k], (x, y, 1 - c)))
        return out

    return _Stage(grads, [jax.ShapeDtypeStruct((N_CHIPS, g.shape[1] // 2, g.shape[2]), g.dtype) for g in grads],
                  n, 0, copies)


_REL_OF_PEER = (2, 1, 3)


def _rs_scatter_stage(sums, relative=False):
    n = len(sums)

    def copies(ins, outs):
        x, y, c, me, chips, chip_ids = _mesh_pos()
        out = []
        for k in range(n):
            mine, _ = _half_rows(c, 2 * ins[k].shape[1])
            out.append((ins[k].at[0 if relative else me], outs[k].at[0, mine, :], None))
            for j, (cx, cy) in enumerate(chips):
                slab = _REL_OF_PEER[j] if relative else chip_ids[j]
                out.append((ins[k].at[slab], outs[k].at[1 + j, mine, :], (cx, cy, c)))
        return out

    return _Stage(sums, [jax.ShapeDtypeStruct((N_CHIPS, 2 * s.shape[1], s.shape[2]), s.dtype) for s in sums],
                  3 * n, n, copies)


def _rs_mirror_stage(parts):
    n = len(parts)

    def copies(ins, outs):
        x, y, c, _, _, _ = _mesh_pos()
        out = []
        for k in range(n):
            mine, _ = _half_rows(c, ins[k].shape[1])
            out.append((ins[k].at[:, mine, :], outs[k].at[:, mine, :], (x, y, 1 - c)))
        return out

    return _Stage(parts, [jax.ShapeDtypeStruct(p.shape, p.dtype) for p in parts], n, 0, copies,
                  aliases={k: k for k in range(n)})


def _pair_add(name, g, r1, core):
    _, rows, cols = g.shape
    hr = rows // 2

    def body(c_ref, g_ref, r_ref, o_ref):
        o_ref[...] = (g_ref[...].astype(F32) + r_ref[...].astype(F32)).astype(BF16)

    half = pl.BlockSpec((None, hr, cols), lambda j, c: (j, 0, 0))
    return pl.pallas_call(
        body, name=name,
        grid_spec=pltpu.PrefetchScalarGridSpec(
            num_scalar_prefetch=1, grid=(N_CHIPS,),
            in_specs=[pl.BlockSpec((None, hr, cols), lambda j, c: (j, c[0], 0)), half], out_specs=half),
        out_shape=jax.ShapeDtypeStruct((N_CHIPS, hr, cols), BF16),
        compiler_params=_params("parallel"),
    )(core, g, r1)


def _all_gather_weights(shards):
    n = len(shards)

    def body(*refs):
        ins, outs = refs[:n], refs[n:2 * n]
        send, recv, loc = refs[2 * n:]
        x, y, c, me, chips, chip_ids = _mesh_pos()
        sib = (x, y, 1 - c)

        def halves(k):
            hr = ins[k].shape[0] // 2
            return pl.ds(pl.multiple_of(c * hr, 16), hr), pl.ds(pl.multiple_of((1 - c) * hr, 16), hr)

        def remote(src, dst, k, j, dev):
            return pltpu.make_async_remote_copy(src_ref=src, dst_ref=dst, send_sem=send.at[k, j],
                                                recv_sem=recv.at[k, j], device_id=dev, device_id_type=_MESH)

        started = []
        local = []
        for k in range(n):
            mine, _ = halves(k)
            cp = pltpu.make_async_copy(ins[k], outs[k].at[me], loc.at[k])
            cp.start()
            local.append(cp)
            for j, (cx, cy) in enumerate(chips):
                cp = remote(ins[k].at[mine], outs[k].at[me, mine], k, j, (cx, cy, c))
                cp.start()
                started.append(cp)
        for k in range(n):
            mine, _ = halves(k)
            for j in range(3):
                land = outs[k].at[chip_ids[j], mine]
                remote(land, land, k, j, sib).wait_recv()
                cp = remote(land, land, k, 3 + j, sib)
                cp.start()
                started.append(cp)
        for k in range(n):
            _, other = halves(k)
            for j in range(3):
                land = outs[k].at[chip_ids[j], other]
                remote(land, land, k, 3 + j, sib).wait_recv()
        for cp in started:
            cp.wait_send()
        for cp in local:
            cp.wait()

    return pl.pallas_call(
        body, name="all_gather_weights", in_specs=[_ANY] * n, out_specs=[_ANY] * n,
        out_shape=[jax.ShapeDtypeStruct((N_CHIPS,) + s.shape, s.dtype) for s in shards],
        scratch_shapes=[pltpu.SemaphoreType.DMA((n, 6)), pltpu.SemaphoreType.DMA((n, 6)),
                        pltpu.SemaphoreType.DMA((n,))],
        compiler_params=pltpu.CompilerParams(vmem_limit_bytes=V7X_VMEM_LIMIT_BYTES),
    )(*shards)


_RS_CHUNK = 32


def _reduce_scatter(name, grads):
    n = len(grads)

    def body(*refs):
        gs, outs = refs[:n], refs[n:2 * n]
        own, r1, r2, fin = (refs[(2 + i) * n:(3 + i) * n] for i in range(4))
        a_send, a_recv, b_send, b_recv, c_send, c_recv, l_in, l_out = refs[6 * n:]
        x, y, c, me, chips, chip_ids = _mesh_pos()
        sib = (x, y, 1 - c)

        def halves(k):
            hr = gs[k].shape[1] // 2
            return hr, pl.ds(pl.multiple_of(c * hr, 16), hr), pl.ds(pl.multiple_of((1 - c) * hr, 16), hr)

        def remote(src, dst, ssem, rsem, dev):
            return pltpu.make_async_remote_copy(src_ref=src, dst_ref=dst, send_sem=ssem, recv_sem=rsem,
                                                device_id=dev, device_id_type=_MESH)

        sends, locals_in = [], []
        for k in range(n):
            hr, mine, other = halves(k)
            cp = remote(gs[k].at[:, other, :], r1[k], a_send.at[k], a_recv.at[k], sib)
            cp.start()
            sends.append(cp)
            cp = pltpu.make_async_copy(gs[k].at[:, mine, :], own[k], l_in.at[k])
            cp.start()
            locals_in.append(cp)

        for k in range(n):
            hr, mine, other = halves(k)
            locals_in[k].wait()
            remote(r1[k], r1[k], a_send.at[k], a_recv.at[k], sib).wait_recv()
            for j in range(N_CHIPS):
                def add(i, carry, k=k, j=j):
                    rows = pl.ds(pl.multiple_of(i * _RS_CHUNK, _RS_CHUNK), _RS_CHUNK)
                    own[k][j, rows, :] = (own[k][j, rows, :].astype(F32) + r1[k][j, rows, :].astype(F32)).astype(BF16)
                    return carry

                lax.fori_loop(0, hr // _RS_CHUNK, add, 0)
            for j, (cx, cy) in enumerate(chips):
                cp = remote(own[k].at[chip_ids[j]], r2[k].at[j], b_send.at[k, j], b_recv.at[k, j], (cx, cy, c))
                cp.start()
                sends.append(cp)

        locals_out = []
        for k in range(n):
            hr, mine, other = halves(k)
            for j in range(3):
                remote(r2[k].at[j], r2[k].at[j], b_send.at[k, j], b_recv.at[k, j], sib).wait_recv()

            def total(i, carry, k=k):
                rows = pl.ds(pl.multiple_of(i * _RS_CHUNK, _RS_CHUNK), _RS_CHUNK)
                acc = own[k][me, rows, :].astype(F32)
                for j in range(3):
                    acc = acc + r2[k][j, rows, :].astype(F32)
                fin[k][rows, :] = acc
                return carry

            lax.fori_loop(0, hr // _RS_CHUNK, total, 0)
            cp = remote(fin[k], outs[k].at[mine, :], c_send.at[k], c_recv.at[k], sib)
            cp.start()
            sends.append(cp)
            cp = pltpu.make_async_copy(fin[k], outs[k].at[mine, :], l_out.at[k])
            cp.start()
            locals_out.append(cp)

        for k in range(n):
            hr, mine, other = halves(k)
            land = outs[k].at[other, :]
            remote(land, land, c_send.at[k], c_recv.at[k], sib).wait_recv()
        for cp in sends:
            cp.wait_send()
        for cp in locals_out:
            cp.wait()

    scratch = []
    for g in grads:
        scratch.append(pltpu.VMEM((N_CHIPS, g.shape[1] // 2, g.shape[2]), BF16))
    for g in grads:
        scratch.append(pltpu.VMEM((N_CHIPS, g.shape[1] // 2, g.shape[2]), BF16))
    for g in grads:
        scratch.append(pltpu.VMEM((3, g.shape[1] // 2, g.shape[2]), BF16))
    for g in grads:
        scratch.append(pltpu.VMEM((g.shape[1] // 2, g.shape[2]), F32))
    dma = pltpu.SemaphoreType.DMA
    scratch += [dma((n,)), dma((n,)), dma((n, 3)), dma((n, 3)), dma((n,)), dma((n,)), dma((n,)), dma((n,))]
    return pl.pallas_call(
        body, name=name, in_specs=[_ANY] * n, out_specs=[_ANY] * n,
        out_shape=[jax.ShapeDtypeStruct(g.shape[1:], F32) for g in grads],
        scratch_shapes=scratch,
        compiler_params=pltpu.CompilerParams(vmem_limit_bytes=V7X_VMEM_LIMIT_BYTES),
    )(*grads)


def _adamw_math(w, g, m, v):
    m = ADAM_B1 * m + (1.0 - ADAM_B1) * g
    v = ADAM_B2 * v + (1.0 - ADAM_B2) * (g * g)
    m_hat = m / (1.0 - ADAM_B1 ** ADAM_STEP)
    v_hat = v / (1.0 - ADAM_B2 ** ADAM_STEP)
    delta = -ADAM_LR * (m_hat / (jnp.sqrt(v_hat) + ADAM_EPS) + ADAM_WD * w)
    return delta, m, v


def _adamw_sum(name, w, parts, m, v):
    r, c = w.shape
    br = r
    while br * c * 4 > (1 << 20) and br % 32 == 0:
        br //= 2

    def body(w_ref, p_ref, m_ref, v_ref, g_ref, d_ref, nm_ref, nv_ref):
        g = p_ref[0].astype(F32)
        for j in range(1, N_CHIPS):
            g = g + p_ref[j].astype(F32)
        d, nm, nv = _adamw_math(w_ref[...], g, m_ref[...], v_ref[...])
        g_ref[...] = g
        d_ref[...] = d
        nm_ref[...] = nm
        nv_ref[...] = nv

    spec = pl.BlockSpec((br, c), lambda i: (i, 0))
    shp = jax.ShapeDtypeStruct((r, c), F32)
    return _call(
        body, name=name, grid=(r // br,),
        in_specs=[spec, pl.BlockSpec((N_CHIPS, br, c), lambda i: (0, i, 0)), spec, spec],
        out_specs=[spec] * 4, out_shape=[shp] * 4, sem=("parallel",), args=[w, parts, m, v])


_SMALL_ROWS = 80


def _small_allreduce_adamw(gpack, wpack, mpack, vpack):
    half = _SMALL_ROWS // 2

    def body(g_ref, w_ref, m_ref, v_ref, go_ref, d_ref, nm_ref, nv_ref, sib_buf, chip_sum, buf, send, recv):
        x, y, c, me, chips, chip_ids = _mesh_pos()
        sib = (x, y, 1 - c)
        mine = pl.ds(pl.multiple_of(c * half, 8), half)

        def remote(src, dst, k, dev):
            return pltpu.make_async_remote_copy(src_ref=src, dst_ref=dst, send_sem=send.at[k], recv_sem=recv.at[k],
                                                device_id=dev, device_id_type=_MESH)

        swap = remote(g_ref, sib_buf, 0, sib)
        swap.start()
        swap.wait()
        chip_sum[...] = g_ref[...] + sib_buf[...]
        buf[me] = chip_sum[...]
        sends = [remote(chip_sum.at[mine], buf.at[me, mine], 1 + j, (cx, cy, c)) for j, (cx, cy) in enumerate(chips)]
        for cp in sends:
            cp.start()
        for cp in sends:
            cp.wait()
        mirrors = [remote(buf.at[chip_ids[j], mine], buf.at[chip_ids[j], mine], 4 + j, sib) for j in range(3)]
        for cp in mirrors:
            cp.start()
        for cp in mirrors:
            cp.wait()
        total = buf[0]
        for i in range(1, N_CHIPS):
            total = total + buf[i]
        go_ref[...] = total
        d, nm, nv = _adamw_math(w_ref[...], total, m_ref[...], v_ref[...])
        d_ref[...] = d
        nm_ref[...] = nm
        nv_ref[...] = nv

    vm = pl.BlockSpec(memory_space=pltpu.VMEM)
    shp = jax.ShapeDtypeStruct((_SMALL_ROWS, D_MODEL), F32)
    return pl.pallas_call(
        body, name="small_allreduce_adamw", in_specs=[vm] * 4, out_specs=[vm] * 4, out_shape=[shp] * 4,
        scratch_shapes=[pltpu.VMEM((_SMALL_ROWS, D_MODEL), F32), pltpu.VMEM((_SMALL_ROWS, D_MODEL), F32),
                        pltpu.VMEM((N_CHIPS, _SMALL_ROWS, D_MODEL), F32), pltpu.SemaphoreType.DMA((7,)),
                        pltpu.SemaphoreType.DMA((7,))],
        compiler_params=pltpu.CompilerParams(vmem_limit_bytes=V7X_VMEM_LIMIT_BYTES),
    )(gpack, wpack, mpack, vpack)


_SMALL_VECTORS = ("ffn1_norm", "mix_norm", "xattn_norm", "mem_norm", "ffn2_norm", "final_norm", "q_norm",
                  "kv_norm", "pool_scale")


_LOSS_ROW = 9
_VEC_ROWS = 16
_POOL_ROWS = POOL_GROUPS * POOL_CH


def _small_params_step(g, w, m, v, loss_local):
    names = list(_SMALL_VECTORS) + ["pool_w"]
    nv = len(_SMALL_VECTORS)
    widths = [g[n].shape[1] for n in _SMALL_VECTORS]
    shapes = {"vec": (_VEC_ROWS, D_MODEL), "pool": (_POOL_ROWS, POOL_CH)}

    def body(*refs):
        ins = refs[:4 * (nv + 1) + 1]
        outs = refs[len(ins):len(ins) + 4 * (nv + 1) + 1]
        vec_own, vec_sib, vec_all, pool_sib, pool_sum, pool_all, send, recv = refs[len(ins) + len(outs):]
        g_in, w_in, m_in, v_in = (ins[k * (nv + 1):(k + 1) * (nv + 1)] for k in range(4))
        loss_in = ins[-1]
        g_out, d_out, m_out, v_out = (outs[k * (nv + 1):(k + 1) * (nv + 1)] for k in range(4))
        loss_out = outs[-1]
        x, y, c, me, chips, chip_ids = _mesh_pos()
        sib = (x, y, 1 - c)

        def remote(src, dst, k, dev):
            return pltpu.make_async_remote_copy(src_ref=src, dst_ref=dst, send_sem=send.at[k], recv_sem=recv.at[k],
                                                device_id=dev, device_id_type=_MESH)

        vec_own[...] = jnp.zeros_like(vec_own)
        for i in range(nv):
            vec_own[i:i + 1, 0:widths[i]] = g_in[i][...]
        vec_own[_LOSS_ROW:_LOSS_ROW + 1, 0:128] = loss_in[...]
        swaps = [remote(vec_own, vec_sib, 0, sib), remote(g_in[nv], pool_sib, 1, sib)]
        for cp in swaps:
            cp.start()
        for cp in swaps:
            cp.wait()
        vec_all[me] = vec_own[...] + vec_sib[...]
        pool_sum[...] = g_in[nv][...] + pool_sib[...]
        pool_all[me] = pool_sum[...]
        hv, hp = _VEC_ROWS // 2, _POOL_ROWS // 2
        mine_v = pl.ds(pl.multiple_of(c * hv, 8), hv)
        mine_p = pl.ds(pl.multiple_of(c * hp, 8), hp)
        sends = []
        for j, (cx, cy) in enumerate(chips):
            sends.append(remote(vec_all.at[me, mine_v], vec_all.at[me, mine_v], 2 + j, (cx, cy, c)))
            sends.append(remote(pool_sum.at[mine_p], pool_all.at[me, mine_p], 5 + j, (cx, cy, c)))
        for cp in sends:
            cp.start()
        for cp in sends:
            cp.wait()
        mirrors = []
        for j in range(3):
            mirrors.append(remote(vec_all.at[chip_ids[j], mine_v], vec_all.at[chip_ids[j], mine_v], 8 + j, sib))
            mirrors.append(remote(pool_all.at[chip_ids[j], mine_p], pool_all.at[chip_ids[j], mine_p], 11 + j, sib))
        for cp in mirrors:
            cp.start()
        for cp in mirrors:
            cp.wait()
        vec_tot = vec_all[0]
        pool_tot = pool_all[0]
        for i in range(1, N_CHIPS):
            vec_tot = vec_tot + vec_all[i]
            pool_tot = pool_tot + pool_all[i]
        vec_sib[...] = vec_tot
        loss_out[...] = vec_sib[_LOSS_ROW:_LOSS_ROW + 1, 0:128]
        for i in range(nv + 1):
            gi = pool_tot if i == nv else vec_sib[i:i + 1, 0:widths[i]]
            d, nm, nvv = _adamw_math(w_in[i][...], gi, m_in[i][...], v_in[i][...])
            g_out[i][...] = gi
            d_out[i][...] = d
            m_out[i][...] = nm
            v_out[i][...] = nvv

    vm = pl.BlockSpec(memory_space=pltpu.VMEM)
    args = [d[n] for d in (g, w, m, v) for n in names] + [jnp.broadcast_to(loss_local.reshape(1, 1), (1, 128))]
    out_shape = [jax.ShapeDtypeStruct(g[n].shape, F32) for _ in range(4) for n in names]
    out_shape.append(jax.ShapeDtypeStruct((1, 128), F32))
    res = pl.pallas_call(
        body, name="small_params_step", in_specs=[vm] * len(args), out_specs=[vm] * len(out_shape),
        out_shape=out_shape,
        scratch_shapes=[pltpu.VMEM(shapes["vec"], F32), pltpu.VMEM(shapes["vec"], F32),
                        pltpu.VMEM((N_CHIPS,) + shapes["vec"], F32), pltpu.VMEM(shapes["pool"], F32),
                        pltpu.VMEM(shapes["pool"], F32), pltpu.VMEM((N_CHIPS,) + shapes["pool"], F32),
                        pltpu.SemaphoreType.DMA((14,)), pltpu.SemaphoreType.DMA((14,))],
        compiler_params=pltpu.CompilerParams(vmem_limit_bytes=V7X_VMEM_LIMIT_BYTES),
    )(*args)
    k = len(names)
    dicts = [dict(zip(names, res[i * k:(i + 1) * k])) for i in range(4)]
    return dicts[0], dicts[1], dicts[2], dicts[3], res[-1]


def _pack_small(d, scalar=None):
    rows = []
    for n in _SMALL_VECTORS:
        v = d[n].reshape(1, -1).astype(F32)
        rows.append(jnp.pad(v, ((0, 0), (0, D_MODEL - v.shape[1]))))
    assert len(rows) == _LOSS_ROW
    extra = jnp.zeros((1, D_MODEL), F32) if scalar is None else jnp.pad(scalar.reshape(1, 1), ((0, 0), (0, D_MODEL - 1)))
    rows.append(extra)
    rows.append(jnp.zeros((16 - len(rows), D_MODEL), F32))
    rows.append(d["pool_w"].reshape(64, D_MODEL).astype(F32))
    return jnp.concatenate(rows, axis=0)


def _unpack_small(pack, like):
    out = {}
    for i, n in enumerate(_SMALL_VECTORS):
        out[n] = pack[i, :like[n].size].reshape(like[n].shape)
    out["pool_w"] = pack[16:].reshape(like["pool_w"].shape)
    return out


_WEIGHTS = ("ffn1_norm", "ffn1_w_gate", "ffn1_w_up", "ffn1_w_down", "mix_norm", "w_in", "q_norm", "w_q_up",
            "kv_norm", "w_kv_up", "pool_w", "pool_scale", "w_out", "xattn_norm", "mem_norm", "w_mq", "w_mkv",
            "w_mo", "ffn2_norm", "ffn2_w_gate", "ffn2_w_up", "ffn2_w_down", "final_norm")
_SHARDED = ("ffn1_w_gate", "ffn1_w_up", "ffn1_w_down", "w_in", "w_q_up", "w_kv_up", "w_out", "w_mq", "w_mkv",
            "w_mo", "ffn2_w_gate", "ffn2_w_up", "ffn2_w_down")
_RS_GROUPS = (("ffn2_w_gate", "ffn2_w_up", "ffn2_w_down"),
              ("w_mo", "w_mq", "w_mkv", "w_out", "w_q_up", "w_kv_up", "w_in"),
              ("ffn1_w_gate", "ffn1_w_up", "ffn1_w_down"))
W_IN_SPLIT = Q_LORA + KV_LORA + ROPE_DIM


_FFN1 = ("ffn1_w_gate", "ffn1_w_up", "ffn1_w_down")
_TRANSPOSED = ("ffn1_w_gate", "ffn1_w_up", "ffn2_w_gate", "ffn2_w_up", "w_in", "w_q_up")


def _local_view(name, a):
    return jnp.swapaxes(a, 1, 2)[0] if name in _TRANSPOSED else a[0]


def _global_view(name, a):
    return jnp.swapaxes(a[None], 1, 2) if name in _TRANSPOSED else a[None]


def _pad_shard(name, a):
    if name == "w_in":
        return jnp.concatenate([a[:W_IN_SPLIT], jnp.zeros((64, a.shape[1]), a.dtype), a[W_IN_SPLIT:]], axis=0)
    if name == "w_q_up":
        return jnp.pad(a, ((0, 64), (0, 0)))
    return a


def _unpad_shard(name, a):
    if name == "w_in":
        return jnp.concatenate([a[:, :W_IN_SPLIT], a[:, W_IN_SPLIT + 64:]], axis=1)
    if name == "w_q_up":
        return a[:, :192]
    return a


def _stacked(g):
    return g if g.ndim == 3 else g.reshape(N_CHIPS, g.shape[0] // N_CHIPS, g.shape[1])


class _Plan:
    ALL = (0, 1, 2)
    AG_UNITS = (
        (("w_in", "w_q_up", "w_kv_up"), (("ffn1_up", ALL),), "ffn1_up"),
        (("w_out",), (("w_in", ALL),), "w_in"),
        (("w_mq",), (("qkv_prep", ALL),), "qkv_prep"),
        (("w_mkv", "w_mo", "ffn2_w_gate"), (("mla_fwd", ALL),), "mla_fwd"),
        (("ffn2_w_up",), (("pool_fwd", (0,)), ("w_out", (1,)), ("w_mq", (2,))), "xattn_fwd"),
        (("ffn2_w_down",), (("xattn_fwd", (0, 1)), ("w_mo", (2,))), "ffn2_up"),
    )
    RS_UNITS = (
        (("ffn2_w_gate", "ffn2_w_up", "ffn2_w_down"), "ffn2_dn_a", "mla_bwd", "qkv_prep_bwd"),
        (("w_mo", "w_mq", "w_mkv"), "w_out_dx", "mla_bwd", "qkv_prep_bwd"),
        (("w_out", "w_q_up", "w_kv_up", "w_in"), "w_in_dx", "ffn1_dact", "ffn1_dwd"),
        (("ffn1_w_down",), "ffn1_dwg", "ffn1_dwu", "ffn1_dn_a"),
        (("ffn1_w_gate",), "ffn1_dwu", "ffn1_dn_a", "ffn1_dn_b"),
        (("ffn1_w_up",), "ffn1_dn_a", "ffn1_dn_b", "adamw_w_kv_up"),
    )
    ADAMW_ORDER = ("w_kv_up", "ffn2_w_gate", "ffn2_w_up", "ffn2_w_down", "w_mo", "w_mq", "w_mkv", "w_out", "w_q_up",
                   "w_in", "ffn1_w_down", "ffn1_w_gate", "ffn1_w_up")

    def __init__(self, shards, w, grads, core):
        self.shards, self.w, self.grads, self.core = shards, w, grads, core
        self.last_slab_step = 0
        self.parts = {}
        self.ag = [None for _ in self.AG_UNITS]
        self.rs = [[None, None, None, None] for _ in self.RS_UNITS]

    def pre(self, name):
        for i, (names, sends, d_host) in enumerate(self.AG_UNITS):
            for host, peers in sends:
                if name == host:
                    st = _ag_ici_stage([self.shards[n] for n in names], peers=peers,
                                       into=self.ag[i].results if self.ag[i] is not None else None)
                    st.start_step = self.last_slab_step if name == "ffn1_up" else 0
                    if d_host == host:
                        st.then = _ag_d2d_stage(st.outs)
                    self.ag[i] = _host(name, st)
            if name == d_host and d_host != sends[-1][0]:
                self.ag[i] = _host(name, _ag_d2d_stage(self.ag[i].results))
        for i, (names, h1, h2, h3) in enumerate(self.RS_UNITS):
            if name == h1:
                self.rs[i][0] = _host(name, _rs_swap_stage([_stacked(self.grads[n]) for n in names]))
            if name == h2:
                self.rs[i][2] = _host(name, _rs_scatter_stage(self.rs[i][1], relative=names[0] in _FFN1))
            if name == h3:
                self.rs[i][3] = _host(name, _rs_mirror_stage(self.rs[i][2].results))

    def post(self, name):
        for i, (names, sends, d_host) in enumerate(self.AG_UNITS):
            if name == d_host:
                for n, f in zip(names, self.ag[i].results):
                    self.w[n] = _full_weight(n, f)
        for i, (names, h1, h2, h3) in enumerate(self.RS_UNITS):
            if name == h1:
                self.rs[i][1] = [_pair_add("pair_add_" + n, _stacked(self.grads[n]), r1, self.core)
                                 for n, r1 in zip(names, self.rs[i][0].results)]
            if name == h3:
                for n, p in zip(names, self.rs[i][3].results):
                    self.parts[n] = p


def _full_weight(name, stacked):
    if name in ("w_out", "w_mq", "w_mo"):
        return stacked.reshape(D_MODEL, D_MODEL)
    return stacked


def kernel(x, mem, positions, ffn1_norm, ffn1_w_gate, ffn1_w_up, ffn1_w_down, mix_norm, w_in, q_norm, w_q_up, kv_norm, w_kv_up, pool_w, pool_scale, w_out, xattn_norm, mem_norm, w_mq, w_mkv, w_mo, ffn2_norm, ffn2_w_gate, ffn2_w_up, ffn2_w_down, final_norm, loss_target, m_ffn1_norm, m_ffn1_w_gate, m_ffn1_w_up, m_ffn1_w_down, m_mix_norm, m_w_in, m_q_norm, m_w_q_up, m_kv_norm, m_w_kv_up, m_pool_w, m_pool_scale, m_w_out, m_xattn_norm, m_mem_norm, m_w_mq, m_w_mkv, m_w_mo, m_ffn2_norm, m_ffn2_w_gate, m_ffn2_w_up, m_ffn2_w_down, m_final_norm, v_ffn1_norm, v_ffn1_w_gate, v_ffn1_w_up, v_ffn1_w_down, v_mix_norm, v_w_in, v_q_norm, v_w_q_up, v_kv_norm, v_w_kv_up, v_pool_w, v_pool_scale, v_w_out, v_xattn_norm, v_mem_norm, v_w_mq, v_w_mkv, v_w_mo, v_ffn2_norm, v_ffn2_w_gate, v_ffn2_w_up, v_ffn2_w_down, v_final_norm):
    wts = dict(zip(_WEIGHTS, (ffn1_norm, ffn1_w_gate, ffn1_w_up, ffn1_w_down, mix_norm, w_in, q_norm, w_q_up, kv_norm, w_kv_up, pool_w, pool_scale, w_out, xattn_norm, mem_norm, w_mq, w_mkv, w_mo, ffn2_norm, ffn2_w_gate, ffn2_w_up, ffn2_w_down, final_norm)))
    mom = dict(zip(_WEIGHTS, (m_ffn1_norm, m_ffn1_w_gate, m_ffn1_w_up, m_ffn1_w_down, m_mix_norm, m_w_in, m_q_norm, m_w_q_up, m_kv_norm, m_w_kv_up, m_pool_w, m_pool_scale, m_w_out, m_xattn_norm, m_mem_norm, m_w_mq, m_w_mkv, m_w_mo, m_ffn2_norm, m_ffn2_w_gate, m_ffn2_w_up, m_ffn2_w_down, m_final_norm)))
    var = dict(zip(_WEIGHTS, (v_ffn1_norm, v_ffn1_w_gate, v_ffn1_w_up, v_ffn1_w_down, v_mix_norm, v_w_in, v_q_norm, v_w_q_up, v_kv_norm, v_w_kv_up, v_pool_w, v_pool_scale, v_w_out, v_xattn_norm, v_mem_norm, v_w_mq, v_w_mkv, v_w_mo, v_ffn2_norm, v_ffn2_w_gate, v_ffn2_w_up, v_ffn2_w_down, v_final_norm)))
    small = [n for n in _WEIGHTS if n not in _SHARDED]

    global _PLAN
    shards = {n: _pad_shard(n, _local_view(n, wts[n])).astype(BF16) for n in _SHARDED}
    w = {n: wts[n].reshape(1, -1) for n in _SMALL_VECTORS}
    w["pool_w"] = pool_w[0].astype(BF16)
    grads = {}
    core = lax.axis_index("c").astype(jnp.int32).reshape(1)
    plan = _Plan(shards, w, grads, core)
    _PLAN = plan
    try:
        w["ffn1_shards"] = tuple(shards[n] for n in _FFN1)

        loss_local, dx = _local_step(x[0], mem[0], positions[0], loss_target[0], w, grads)

        def small_view(d):
            out = {n: d[n].reshape(1, -1) for n in _SMALL_VECTORS}
            out["pool_w"] = d["pool_w"].reshape(_POOL_ROWS, POOL_CH)
            return out

        *small_res, loss_vec = _small_params_step(small_view(grads), small_view(wts), small_view(mom),
                                                  small_view(var), loss_local)
        g_out, d_out, m_out, v_out = ({n: r[n].reshape(wts[n].shape) for n in small} for r in small_res)
        loss = loss_vec[0, 0]

        for n in _Plan.ADAMW_ORDER:
            res = _adamw_sum("adamw_" + n, _local_view(n, wts[n]), _unpad_shard(n, plan.parts[n]),
                             _local_view(n, mom[n]), _local_view(n, var[n]))
            g_out[n], d_out[n], m_out[n], v_out[n] = (_global_view(n, r) for r in res)
    finally:
        _PLAN = None
        _PENDING.clear()

    return (loss, dx[None], *[g_out[n] for n in _WEIGHTS], *[d_out[n] for n in _WEIGHTS],
            *[m_out[n] for n in _WEIGHTS], *[v_out[n] for n in _WEIGHTS])
```

```python
import functools

import jax
import jax.numpy as jnp
from jax import lax
from jax.experimental import pallas as pl
from jax.experimental.pallas import tpu as pltpu

F32 = jnp.float32
BF16 = jnp.bfloat16

D_MODEL = 1024
D_FF = 2816
N_CHIPS = 4
FF_SHARD = D_FF // N_CHIPS
MLA_HEADS = 4
Q_LORA = 256
KV_LORA = 128
ROPE_DIM = 64
HEAD_QK = 256
HEAD_V = 128
POOL_GROUPS = 4
POOL_CH = 128
MEM_HEADS = 4
MEM_HEAD_DIM = 256
RMS_EPS = 1e-6
ROPE_BASE = 10000.0
MLA_SCALE = (128 + 64) ** -0.5
MEM_SCALE = MEM_HEAD_DIM ** -0.5

ADAM_LR = 0.001
ADAM_B1 = 0.9
ADAM_B2 = 0.999
ADAM_EPS = 1e-08
ADAM_WD = 0.01
ADAM_STEP = 10

V7X_VMEM_LIMIT_BYTES = 56 * 1024 * 1024

NN = ((1,), (0,))
NT = ((1,), (1,))
TN = ((0,), (0,))


def _params(*sem):
    return pltpu.CompilerParams(dimension_semantics=sem, vmem_limit_bytes=V7X_VMEM_LIMIT_BYTES)


_MESH = pl.DeviceIdType.MESH
_ANY = pl.BlockSpec(memory_space=pl.ANY)


class _Stage:
    def __init__(self, ins, outs, n_remote, n_local, copies, aliases=None):
        self.ins, self.outs, self.n_remote, self.n_local = list(ins), list(outs), n_remote, n_local
        self.copies, self.aliases = copies, dict(aliases or {})
        self.results = None
        self.start_step = 0
        self.then = None

    def descriptors(self, in_refs, out_refs, send, recv, loc):
        ds, ri, li = [], 0, 0
        for src, dst, dev in self.copies(in_refs, out_refs):
            if dev is None:
                ds.append(pltpu.make_async_copy(src, dst, loc.at[li]))
                li += 1
            else:
                ds.append(pltpu.make_async_remote_copy(src_ref=src, dst_ref=dst, send_sem=send.at[ri],
                                                       recv_sem=recv.at[ri], device_id=dev, device_id_type=_MESH))
                ri += 1
        assert ri == self.n_remote and li == self.n_local
        return ds


_PENDING = {}


def _host(name, stage):
    _PENDING.setdefault(name, []).append(stage)
    return stage


_PLAN = None


def _call(body, **kw):
    if _PLAN is not None:
        _PLAN.pre(kw["name"])
    res = _call_hosting(body, **kw)
    if _PLAN is not None:
        _PLAN.post(kw["name"])
    return res


def _call_hosting(body, *, name, grid, in_specs, out_specs, out_shape, sem, args, scratch_shapes=(), aliases=None):
    stages = _PENDING.pop(name, [])
    scratch_shapes = list(scratch_shapes)
    if not stages:
        return pl.pallas_call(body, name=name, grid=grid, in_specs=in_specs, out_specs=out_specs,
                              out_shape=out_shape, scratch_shapes=scratch_shapes,
                              input_output_aliases=dict(aliases or {}), compiler_params=_params(*sem))(*args)
    ni, no, ns = len(in_specs), len(out_shape), len(scratch_shapes)
    c_ins = [a for st in stages for a in st.ins]
    c_outs = [o for st in stages for o in st.outs]
    nci, nco = len(c_ins), len(c_outs)
    aliases, io, oo = dict(aliases or {}), 0, 0
    for st in stages:
        for i, j in st.aliases.items():
            aliases[ni + io + i] = no + oo + j
        io += len(st.ins)
        oo += len(st.outs)
    dma = pltpu.SemaphoreType.DMA
    sems = []
    for st in stages:
        sems += [dma((max(st.n_remote, 1),)), dma((max(st.n_remote, 1),)), dma((max(st.n_local, 1),))]
    followers = [st.then for st in stages if st.then is not None]
    for st in followers:
        sems += [dma((max(st.n_remote, 1),)), dma((max(st.n_remote, 1),)), dma((max(st.n_local, 1),))]

    def wrapped(*refs):
        ins, cin = refs[:ni], refs[ni:ni + nci]
        outs, cout = refs[ni + nci:ni + nci + no], refs[ni + nci + no:ni + nci + no + nco]
        scr = refs[ni + nci + no + nco:ni + nci + no + nco + ns]
        sem_refs = refs[ni + nci + no + nco + ns:]
        step = pl.program_id(0)
        last = pl.program_id(0) == grid[0] - 1
        for ax in range(1, len(grid)):
            step = step * grid[ax] + pl.program_id(ax)
            last = jnp.logical_and(last, pl.program_id(ax) == grid[ax] - 1)

        def descriptors(si):
            io = sum(len(st.ins) for st in stages[:si])
            oo = sum(len(st.outs) for st in stages[:si])
            st = stages[si]
            return st.descriptors(cin[io:io + len(st.ins)], cout[oo:oo + len(st.outs)], *sem_refs[3 * si:3 * si + 3])

        def follower_descriptors(fi):
            si = [k for k, st in enumerate(stages) if st.then is not None][fi]
            oo = sum(len(st.outs) for st in stages[:si])
            bufs = cout[oo:oo + len(stages[si].outs)]
            k0 = 3 * (len(stages) + fi)
            return followers[fi].descriptors(bufs, bufs, *sem_refs[k0:k0 + 3])

        def start(si):
            @pl.when(step == stages[si].start_step)
            def _():
                for d in descriptors(si):
                    d.start()

        for si, st in enumerate(stages):
            if st.start_step == 0:
                start(si)
        body(*ins, *outs, *scr)
        for si, st in enumerate(stages):
            if st.start_step != 0:
                start(si)

        @pl.when(last)
        def _():
            for si in range(len(stages)):
                for d in descriptors(si):
                    d.wait()
            for fi in range(len(followers)):
                for d in follower_descriptors(fi):
                    d.start()
            for fi in range(len(followers)):
                for d in follower_descriptors(fi):
                    d.wait()

    res = pl.pallas_call(
        wrapped, name=name, grid=grid, in_specs=list(in_specs) + [_ANY] * nci,
        out_specs=list(out_specs) + [_ANY] * nco, out_shape=list(out_shape) + c_outs,
        scratch_shapes=scratch_shapes + sems, input_output_aliases=aliases,
        compiler_params=_params(*(("arbitrary",) * len(grid))))(*args, *c_ins)
    oo = no
    for st in stages:
        st.results = list(res[oo:oo + len(st.outs)])
        oo += len(st.outs)
    return list(res[:no])


def _dot(a, b, dims):
    return lax.dot_general(a.astype(BF16), b.astype(BF16), (dims, ((), ())), preferred_element_type=F32)


_MAX_ROW_BLOCK = 1024
_ATT_BLOCK = 512


_MAX_REDUCE_BLOCK = 2048


def _row_block(s, want=1024):
    return min(want, s, _MAX_ROW_BLOCK)


def _reduce_block(s):
    return min(s, _MAX_REDUCE_BLOCK)


def _matmul(name, grid, terms, extras, outs, epilogue, acc_shape, fill=(), summed=()):
    nt, ne, no, nf = len(terms), len(extras), len(outs), len(fill)
    nk = grid[-1]
    dims = [t[4] for t in terms]

    def body(*refs):
        a_refs, b_refs = refs[:nt], refs[nt:2 * nt]
        e_refs = refs[2 * nt:2 * nt + ne]
        o_refs = refs[2 * nt + ne + nf:2 * nt + ne + nf + no]

        def finish(acc):
            vals = epilogue(acc, *[e[...] for e in e_refs])
            for idx, (o, val) in enumerate(zip(o_refs, vals)):
                if idx in summed:
                    @pl.when(pl.program_id(0) == 0)
                    def _(o=o, val=val):
                        o[...] = val.astype(o.dtype)

                    @pl.when(pl.program_id(0) > 0)
                    def _(o=o, val=val):
                        o[...] += val.astype(o.dtype)
                else:
                    o[...] = val.astype(o.dtype)

        if nk == 1:
            part = None
            for a, b, d in zip(a_refs, b_refs, dims):
                t = _dot(a[...], b[...], d)
                part = t if part is None else part + t
            finish(part)
        else:
            acc_ref = refs[-1]
            k = pl.program_id(len(grid) - 1)

            @pl.when(k == 0)
            def _():
                acc_ref[...] = jnp.zeros_like(acc_ref)

            for a, b, d in zip(a_refs, b_refs, dims):
                acc_ref[...] += _dot(a[...], b[...], d)

            @pl.when(k == nk - 1)
            def _():
                finish(acc_ref[...])

    in_specs = [t[1] for t in terms] + [t[3] for t in terms] + [e[1] for e in extras] + [_ANY] * nf
    args = [t[0] for t in terms] + [t[2] for t in terms] + [e[0] for e in extras] + list(fill)
    sem = ("arbitrary" if summed else "parallel",) * (len(grid) - 1) + ("arbitrary",)
    aliases = {2 * nt + ne + i: i for i in range(nf)}
    return _call(
        body, name=name, grid=grid, in_specs=in_specs,
        out_specs=[o[1] for o in outs], out_shape=[o[0] for o in outs],
        scratch_shapes=[pltpu.VMEM(acc_shape, F32)] if nk > 1 else [], sem=sem, args=args, aliases=aliases)


def _ident(acc):
    return (acc,)


def _rmsnorm_fwd(name, x, gain, width, col_block=0):
    s = x.shape[0]
    bm = _row_block(s)

    def body(x_ref, g_ref, o_ref):
        xf = x_ref[...]
        r = lax.rsqrt(jnp.mean(xf * xf, axis=-1, keepdims=True) + RMS_EPS)
        o_ref[...] = ((xf * r) * g_ref[...]).astype(o_ref.dtype)

    return pl.pallas_call(
        body, name=name, grid=(s // bm,),
        in_specs=[pl.BlockSpec((bm, width), lambda i: (i, col_block)), pl.BlockSpec((1, width), lambda i: (0, 0))],
        out_specs=pl.BlockSpec((bm, width), lambda i: (i, 0)),
        out_shape=jax.ShapeDtypeStruct((s, width), BF16),
        compiler_params=_params("parallel"),
    )(x, gain)


def _rms_bwd_math(dy, xf, g, width):
    r = lax.rsqrt(jnp.mean(xf * xf, axis=-1, keepdims=True) + RMS_EPS)
    dyg = dy * g
    dot = jnp.sum(dyg * xf, axis=-1, keepdims=True)
    dx = r * dyg - xf * ((r * r * r) * (dot * (1.0 / width)))
    dgain = jnp.sum(dy * (xf * r), axis=0, keepdims=True)
    return dx, dgain


def _rmsnorm_bwd(name, dy, x, gain, width, col_block=0, dres=None, out_dtype=F32):
    s = x.shape[0]
    bm = _row_block(s)
    has_res = dres is not None

    def body(*refs):
        if has_res:
            dy_ref, x_ref, g_ref, r_ref, dx_ref, dg_ref, dxb_ref = refs
        else:
            dy_ref, x_ref, g_ref, dx_ref, dg_ref = refs
        dx, dgain = _rms_bwd_math(dy_ref[...].astype(F32), x_ref[...], g_ref[...], width)
        if has_res:
            dx = dx + r_ref[...]
            dxb_ref[...] = dx.astype(BF16)
        dx_ref[...] = dx.astype(dx_ref.dtype)

        @pl.when(pl.program_id(0) == 0)
        def _():
            dg_ref[...] = dgain

        @pl.when(pl.program_id(0) > 0)
        def _():
            dg_ref[...] += dgain

    row = pl.BlockSpec((bm, width), lambda i: (i, 0))
    in_specs = [row, pl.BlockSpec((bm, width), lambda i: (i, col_block)), pl.BlockSpec((1, width), lambda i: (0, 0))]
    args = [dy, x, gain]
    out_specs = [row, pl.BlockSpec((1, width), lambda i: (0, 0))]
    out_shape = [jax.ShapeDtypeStruct((s, width), out_dtype), jax.ShapeDtypeStruct((1, width), F32)]
    if has_res:
        in_specs.append(row)
        args.append(dres)
        out_specs.append(row)
        out_shape.append(jax.ShapeDtypeStruct((s, width), BF16))
    return _call(body, name=name, grid=(s // bm,), in_specs=in_specs, out_specs=out_specs, out_shape=out_shape,
                 sem=("arbitrary",), args=args)


def _loss_and_final_norm(h, gain, target):
    s, d = h.shape
    bm = _row_block(s, 512)

    def body(h_ref, g_ref, t_ref, dh_ref, dhb_ref, loss_ref, dg_ref):
        xf = h_ref[...]
        g = g_ref[...]
        r = lax.rsqrt(jnp.mean(xf * xf, axis=-1, keepdims=True) + RMS_EPS)
        err = (xf * r) * g - t_ref[...]
        part = 0.5 * jnp.sum(jnp.mean(err * err, axis=-1, keepdims=True), axis=0, keepdims=True)
        dx, dgain = _rms_bwd_math(err * (1.0 / d), xf, g, d)
        dh_ref[...] = dx
        dhb_ref[...] = dx.astype(BF16)

        @pl.when(pl.program_id(0) == 0)
        def _():
            dg_ref[...] = dgain
            loss_ref[...] = jnp.broadcast_to(part, loss_ref.shape)

        @pl.when(pl.program_id(0) > 0)
        def _():
            dg_ref[...] += dgain
            loss_ref[...] += jnp.broadcast_to(part, loss_ref.shape)

    row = pl.BlockSpec((bm, d), lambda i: (i, 0))
    vec = pl.BlockSpec((1, d), lambda i: (0, 0))
    return pl.pallas_call(
        body, name="loss_final_norm", grid=(s // bm,), in_specs=[row, vec, row],
        out_specs=[row, row, pl.BlockSpec((1, 128), lambda i: (0, 0)), vec],
        out_shape=[jax.ShapeDtypeStruct((s, d), F32), jax.ShapeDtypeStruct((s, d), BF16),
                   jax.ShapeDtypeStruct((1, 128), F32),
                   jax.ShapeDtypeStruct((1, d), F32)],
        compiler_params=_params("arbitrary"),
    )(h, gain, target)


def _ffn_up(name, n, wg, wu):
    s = n.shape[0]
    bm = _row_block(s)

    def body(n_ref, wg_ref, wu_ref, a_ref, dadu_ref, dadg_ref):
        x = n_ref[...]
        g = _dot(x, wg_ref[...], NT)
        u = _dot(x, wu_ref[...], NT)
        sg = jax.nn.sigmoid(g)
        silu = g * sg
        a_ref[...] = (silu * u).astype(BF16)
        dadu_ref[...] = silu.astype(BF16)
        dadg_ref[...] = (u * (sg * (1.0 + g * (1.0 - sg)))).astype(BF16)

    w_spec = pl.BlockSpec((None, FF_SHARD, D_MODEL), lambda j, i: (j, 0, 0))
    o_spec = pl.BlockSpec((None, bm, FF_SHARD), lambda j, i: (j, i, 0))
    shp = jax.ShapeDtypeStruct((N_CHIPS, s, FF_SHARD), BF16)
    return _call(
        body, name=name, grid=(N_CHIPS, s // bm),
        in_specs=[pl.BlockSpec((bm, D_MODEL), lambda j, i: (i, 0)), w_spec, w_spec],
        out_specs=[o_spec, o_spec, o_spec], out_shape=[shp, shp, shp],
        sem=("parallel", "parallel"), args=[n, wg, wu])


def _ffn1_up_gather_direct(n, g_sh, u_sh, d_sh):
    s = n.shape[0]
    bm = _row_block(s)
    nrb = s // bm
    rows, cols = g_sh.shape
    rels = (1, 2, 3)

    def body(n_ref, gs, us, ds, a_ref, dadu_ref, dadg_ref, wg, wu, wd, gbuf, ubuf, send, recv, fsend, frecv, loc, ld):
        r, i = pl.program_id(0), pl.program_id(1)
        x, y, c = lax.axis_index("x"), lax.axis_index("y"), lax.axis_index("c")
        sib = (x, y, 1 - c)
        mine, _ = _half_rows(c, rows)
        shards, fulls, bufs = (gs, us, ds), (wg, wu, wd), (gbuf, ubuf)

        def ici(k, rel, dev=sib):
            return pltpu.make_async_remote_copy(
                src_ref=shards[k].at[mine], dst_ref=fulls[k].at[rel, mine], send_sem=send.at[k, rel - 1],
                recv_sem=recv.at[k, rel - 1], device_id=dev, device_id_type=_MESH)

        def peer(rel):
            return ((1 - x) if rel & 2 else x, (1 - y) if rel & 1 else y, c)

        def fwd(k, rel):
            return pltpu.make_async_remote_copy(
                src_ref=fulls[k].at[rel, mine], dst_ref=fulls[k].at[rel, mine], send_sem=fsend.at[k, rel - 1],
                recv_sem=frecv.at[k, rel - 1], device_id=sib, device_id_type=_MESH)

        def own(k):
            return pltpu.make_async_copy(shards[k], fulls[k].at[0], loc.at[k])

        def load(k, src):
            return pltpu.make_async_copy(src, bufs[k], ld.at[k])

        @pl.when(jnp.logical_and(r == 0, i == 0))
        def _():
            for k in range(3):
                own(k).start()
            for rel in (1, 2):
                for k in (0, 1):
                    ici(k, rel, peer(rel)).start()
            for k in (0, 1):
                load(k, shards[k]).start()
            for k in (0, 1):
                load(k, shards[k]).wait()

        @pl.when(jnp.logical_and(r > 0, i == 0))
        def _():
            for k in (0, 1):
                ici(k, r).wait_recv()
                fwd(k, r).start()
            for k in (0, 1):
                fwd(k, r).wait_recv()
                load(k, fulls[k].at[r]).start()
            for k in (0, 1):
                load(k, fulls[k].at[r]).wait()

        @pl.when(jnp.logical_and(r == 1, i == 0))
        def _():
            for k in (0, 1):
                ici(k, 3, peer(3)).start()

        @pl.when(jnp.logical_and(r == 2, i == 0))
        def _():
            for rel in (1, 2):
                ici(2, rel, peer(rel)).start()

        @pl.when(jnp.logical_and(r == 3, i == 0))
        def _():
            ici(2, 3, peer(3)).start()

        xv = n_ref[...]
        g = _dot(xv, gbuf[...], NT)
        u = _dot(xv, ubuf[...], NT)
        sg = jax.nn.sigmoid(g)
        silu = g * sg
        a_ref[...] = (silu * u).astype(BF16)
        dadu_ref[...] = silu.astype(BF16)
        dadg_ref[...] = (u * (sg * (1.0 + g * (1.0 - sg)))).astype(BF16)

        @pl.when(jnp.logical_and(r == 3, i == nrb - 1))
        def _():
            for rel in rels:
                ici(2, rel).wait_recv()
                fwd(2, rel).start()
            for rel in rels:
                fwd(2, rel).wait_recv()
            for k in range(3):
                for rel in rels:
                    ici(k, rel).wait_send()
                    fwd(k, rel).wait_send()
                own(k).wait()

    o_spec = pl.BlockSpec((None, bm, FF_SHARD), lambda r, i: (r, i, 0))
    act = jax.ShapeDtypeStruct((N_CHIPS, s, FF_SHARD), BF16)
    full = jax.ShapeDtypeStruct((N_CHIPS, rows, cols), BF16)
    dma = pltpu.SemaphoreType.DMA
    if _PLAN is not None:
        _PLAN.last_slab_step = 3 * nrb
    return _call(
        body, name="ffn1_up", grid=(N_CHIPS, nrb),
        in_specs=[pl.BlockSpec((bm, D_MODEL), lambda r, i: (i, 0)), _ANY, _ANY, _ANY],
        out_specs=[o_spec, o_spec, o_spec, _ANY, _ANY, _ANY], out_shape=[act, act, act, full, full, full],
        scratch_shapes=[pltpu.VMEM((rows, cols), BF16), pltpu.VMEM((rows, cols), BF16), dma((3, 3)), dma((3, 3)),
                        dma((3, 3)), dma((3, 3)), dma((3,)), dma((2,))],
        sem=("arbitrary", "arbitrary"), args=[n, g_sh, u_sh, d_sh])


def _ffn1_up_gather(xin, gain, g_sh, u_sh, d_sh):
    s = xin.shape[0]
    bm = _row_block(s)
    nrb = s // bm
    rows, cols = g_sh.shape

    def body(x_ref, gain_ref, gs, us, ds, n_ref, a_ref, dadu_ref, dadg_ref, wg, wu, wd, gbuf, ubuf,
             send, recv, qsend, qrecv, fsend, frecv, loc, ld):
        r, i = pl.program_id(0), pl.program_id(1)
        x, y, c = lax.axis_index("x"), lax.axis_index("y"), lax.axis_index("c")
        sib = (x, y, 1 - c)
        mine, _ = _half_rows(c, rows)
        quarters = _quarter_rows(c, rows)
        shards, fulls, bufs = (gs, us, ds), (wg, wu, wd), (gbuf, ubuf)

        def remote(src, dst, ssem, rsem, dev):
            return pltpu.make_async_remote_copy(src_ref=src, dst_ref=dst, send_sem=ssem, recv_sem=rsem,
                                                device_id=dev, device_id_type=_MESH)

        def peer(rel):
            return ((1 - x) if rel & 2 else x, (1 - y) if rel & 1 else y, c)

        def ici(k, rel, dev=sib):
            return remote(shards[k].at[mine], fulls[k].at[rel, mine], send.at[k, rel - 1], recv.at[k, rel - 1], dev)

        def quarter(k, which, dev=sib):
            slab, q = ((2, quarters[0]), (1, quarters[1]))[which]
            return remote(fulls[k].at[slab, q], fulls[k].at[3, q], qsend.at[k, which], qrecv.at[k, which], dev)

        def fwd(k, rel):
            return remote(fulls[k].at[rel, mine], fulls[k].at[rel, mine], fsend.at[k, rel - 1], frecv.at[k, rel - 1], sib)

        def own(k):
            return pltpu.make_async_copy(shards[k], fulls[k].at[0], loc.at[k])

        def load(slab):
            for k in (0, 1):
                pltpu.make_async_copy(shards[k] if slab == 0 else fulls[k].at[slab], bufs[k], ld.at[k]).start()
            for k in (0, 1):
                pltpu.make_async_copy(shards[k] if slab == 0 else fulls[k].at[slab], bufs[k], ld.at[k]).wait()

        def from_neighbour(ks, rel):
            for k in ks:
                ici(k, rel).wait_recv()
                fwd(k, rel).start()
                quarter(k, 0 if rel == 2 else 1, peer(1 if rel == 2 else 2)).start()
            for k in ks:
                fwd(k, rel).wait_recv()

        def from_diagonal(ks):
            for k in ks:
                quarter(k, 0).wait_recv()
                quarter(k, 1).wait_recv()
                fwd(k, 3).start()
            for k in ks:
                fwd(k, 3).wait_recv()

        @pl.when(jnp.logical_and(r == 0, i == 0))
        def _():
            for k in range(3):
                own(k).start()
            for rel in (1, 2):
                for k in (0, 1):
                    ici(k, rel, peer(rel)).start()
            load(0)

        @pl.when(jnp.logical_and(r == 1, i == 0))
        def _():
            from_neighbour((0, 1), 1)
            load(1)
            for rel in (1, 2):
                ici(2, rel, peer(rel)).start()

        @pl.when(jnp.logical_and(r == 2, i == 0))
        def _():
            from_neighbour((0, 1), 2)
            load(2)

        @pl.when(jnp.logical_and(r == 3, i == 0))
        def _():
            from_diagonal((0, 1))
            load(3)

        xv = _norm_bf16(x_ref[...], gain_ref[...])

        @pl.when(r == 0)
        def _():
            n_ref[...] = xv

        g = _dot(xv, gbuf[...], NT)
        u = _dot(xv, ubuf[...], NT)
        sg = jax.nn.sigmoid(g)
        silu = g * sg
        a_ref[...] = (silu * u).astype(BF16)
        dadu_ref[...] = silu.astype(BF16)
        dadg_ref[...] = (u * (sg * (1.0 + g * (1.0 - sg)))).astype(BF16)

        @pl.when(jnp.logical_and(r == 3, i == nrb - 1))
        def _():
            from_neighbour((2,), 1)
            from_neighbour((2,), 2)
            from_diagonal((2,))
            for k in range(3):
                for rel in (1, 2):
                    ici(k, rel).wait_send()
                for which in (0, 1):
                    quarter(k, which).wait_send()
                for rel in (1, 2, 3):
                    fwd(k, rel).wait_send()
                own(k).wait()

    o_spec = pl.BlockSpec((None, bm, FF_SHARD), lambda r, i: (r, i, 0))
    act = jax.ShapeDtypeStruct((N_CHIPS, s, FF_SHARD), BF16)
    full = jax.ShapeDtypeStruct((N_CHIPS, rows, cols), BF16)
    dma = pltpu.SemaphoreType.DMA
    if _PLAN is not None:
        _PLAN.last_slab_step = 3 * nrb
    n_spec = pl.BlockSpec((bm, D_MODEL), lambda r, i: (jnp.where(r == 0, i, nrb - 1), 0))
    return _call(
        body, name="ffn1_up", grid=(N_CHIPS, nrb),
        in_specs=[pl.BlockSpec((bm, D_MODEL), lambda r, i: (i, 0)), pl.BlockSpec((1, D_MODEL), lambda r, i: (0, 0)),
                  _ANY, _ANY, _ANY],
        out_specs=[n_spec, o_spec, o_spec, o_spec, _ANY, _ANY, _ANY],
        out_shape=[jax.ShapeDtypeStruct((s, D_MODEL), BF16), act, act, act, full, full, full],
        scratch_shapes=[pltpu.VMEM((rows, cols), BF16), pltpu.VMEM((rows, cols), BF16), dma((3, 2)), dma((3, 2)),
                        dma((3, 2)), dma((3, 2)), dma((3, 3)), dma((3, 3)), dma((3,)), dma((2,))],
        sem=("arbitrary", "arbitrary"), args=[xin, gain, g_sh, u_sh, d_sh])


def _residual_epilogue(alpha, with_norm):
    if not with_norm:
        return lambda acc, r: (r + alpha * acc,)

    def epilogue(acc, r, g):
        h = r + alpha * acc
        rs = lax.rsqrt(jnp.mean(h * h, axis=-1, keepdims=True) + RMS_EPS)
        return h, (h * rs) * g

    return epilogue


def _residual_outs(s, bm, gain):
    row = pl.BlockSpec((bm, D_MODEL), lambda i, k: (i, 0))
    outs = [(jax.ShapeDtypeStruct((s, D_MODEL), F32), row)]
    if gain is None:
        return [], outs
    return [(gain, pl.BlockSpec((1, D_MODEL), lambda i, k: (0, 0)))], outs + [(jax.ShapeDtypeStruct((s, D_MODEL), BF16), row)]


def _loss_epilogue(acc, res, g, target):
    d = acc.shape[-1]
    h = res + 0.5 * acc
    r = lax.rsqrt(jnp.mean(h * h, axis=-1, keepdims=True) + RMS_EPS)
    err = (h * r) * g - target
    part = 0.5 * jnp.sum(jnp.mean(err * err, axis=-1, keepdims=True), axis=0, keepdims=True)
    dx, dgain = _rms_bwd_math(err * (1.0 / d), h, g, d)
    return dx, dx, jnp.broadcast_to(part, (1, 128)), dgain


def _ffn_down(name, a, wd, res, gain=None, loss=None):
    s = a.shape[1]
    bm = _row_block(s, 512)
    row = pl.BlockSpec((bm, D_MODEL), lambda i, k: (i, 0))
    terms = [(a, pl.BlockSpec((None, bm, FF_SHARD), lambda i, k, j=j: (j, i, 0)),
              wd, pl.BlockSpec((None, FF_SHARD, D_MODEL), lambda i, k, j=j: (j, 0, 0)), NN) for j in range(N_CHIPS)]
    if loss is not None:
        vec = pl.BlockSpec((1, D_MODEL), lambda i, k: (0, 0))
        outs = [(jax.ShapeDtypeStruct((s, D_MODEL), F32), row), (jax.ShapeDtypeStruct((s, D_MODEL), BF16), row),
                (jax.ShapeDtypeStruct((1, 128), F32), pl.BlockSpec((1, 128), lambda i, k: (0, 0))),
                (jax.ShapeDtypeStruct((1, D_MODEL), F32), vec)]
        return _matmul(name, (s // bm, 1), terms, [(res, row), (loss[0], vec), (loss[1], row)], outs,
                       _loss_epilogue, None, summed=(2, 3))
    extras, outs = _residual_outs(s, bm, gain)
    res_out = _matmul(name, (s // bm, 1), terms, [(res, row)] + extras, outs,
                      _residual_epilogue(0.5, gain is not None), None)
    return res_out if gain is not None else res_out[0]


def _norm_bwd_epilogue(width):
    def epilogue(acc, h, g, dres):
        dx, dgain = _rms_bwd_math(acc, h, g, width)
        dx = dx + dres
        return dx, dx, dgain

    return epilogue


def _norm_bwd_operands(s, bm, h, gain, dres):
    row = pl.BlockSpec((bm, D_MODEL), lambda i, k: (i, 0))
    vec = pl.BlockSpec((1, D_MODEL), lambda i, k: (0, 0))
    extras = [(h, row), (gain, vec), (dres, row)]
    outs = [(jax.ShapeDtypeStruct((s, D_MODEL), F32), row), (jax.ShapeDtypeStruct((s, D_MODEL), BF16), row),
            (jax.ShapeDtypeStruct((1, D_MODEL), F32), vec)]
    return extras, outs, (2,)


def _ffn_bwd(tag, dh, n, dadg, dadu, a, wg, wu, wd, grads, norm_bwd=None):
    s = dh.shape[0]
    bm = _row_block(s)
    bk = _reduce_block(s)
    nk = s // bk

    def act_bwd(acc, dg_da, du_da):
        da = 0.5 * acc
        return da * dg_da.astype(F32), da * du_da.astype(F32)

    slab = pl.BlockSpec((None, bm, FF_SHARD), lambda j, i, k: (j, i, 0))
    shp = jax.ShapeDtypeStruct((N_CHIPS, s, FF_SHARD), BF16)
    dg, du = _matmul(
        tag + "_dact", (N_CHIPS, s // bm, 1),
        [(dh, pl.BlockSpec((bm, D_MODEL), lambda j, i, k: (i, 0)),
          wd, pl.BlockSpec((None, FF_SHARD, D_MODEL), lambda j, i, k: (j, 0, 0)), NT)],
        [(dadg, slab), (dadu, slab)], [(shp, slab), (shp, slab)], act_bwd, None)

    grads[tag + "_w_down"] = _matmul(
        tag + "_dwd", (N_CHIPS, nk),
        [(a, pl.BlockSpec((None, bk, FF_SHARD), lambda j, k: (j, k, 0)),
          dh, pl.BlockSpec((bk, D_MODEL), lambda j, k: (k, 0)), TN)],
        [], [(jax.ShapeDtypeStruct((N_CHIPS, FF_SHARD, D_MODEL), BF16),
              pl.BlockSpec((None, FF_SHARD, D_MODEL), lambda j, k: (j, 0, 0)))],
        lambda acc: (0.5 * acc,), (FF_SHARD, D_MODEL))[0]

    def dw_up(nm, dact):
        return _matmul(
            nm, (N_CHIPS, nk),
            [(dact, pl.BlockSpec((None, bk, FF_SHARD), lambda j, k: (j, k, 0)),
              n, pl.BlockSpec((bk, D_MODEL), lambda j, k: (k, 0)), TN)],
            [], [(jax.ShapeDtypeStruct((N_CHIPS, FF_SHARD, D_MODEL), BF16),
                  pl.BlockSpec((None, FF_SHARD, D_MODEL), lambda j, k: (j, 0, 0)))],
            _ident, (FF_SHARD, D_MODEL))[0]

    grads[tag + "_w_gate"] = dw_up(tag + "_dwg", dg)
    grads[tag + "_w_up"] = dw_up(tag + "_dwu", du)

    bn = _row_block(s, 512)
    steps = s // bn // 2
    prev, dgain = (), None
    for part, off in (("_dn_a", 0), ("_dn_b", steps)):
        row = pl.BlockSpec((bn, D_MODEL), lambda i, k, off=off: (i + off, 0))
        terms = []
        for j in range(N_CHIPS):
            a_slab = pl.BlockSpec((None, bn, FF_SHARD), lambda i, k, j=j, off=off: (j, i + off, 0))
            w_slab = pl.BlockSpec((None, FF_SHARD, D_MODEL), lambda i, k, j=j: (j, 0, 0))
            terms += [(dg, a_slab, wg, w_slab, NN), (du, a_slab, wu, w_slab, NN)]
        if norm_bwd is None:
            prev = _matmul(tag + part, (steps, 1), terms, [], [(jax.ShapeDtypeStruct((s, D_MODEL), F32), row)],
                           _ident, None, fill=prev)
            continue
        h, gain, dres = norm_bwd
        vec = pl.BlockSpec((1, D_MODEL), lambda i, k: (0, 0))
        res = _matmul(
            tag + part, (steps, 1), terms, [(h, row), (gain, vec), (dres, row)],
            [(jax.ShapeDtypeStruct((s, D_MODEL), F32), row), (jax.ShapeDtypeStruct((s, D_MODEL), BF16), row),
             (jax.ShapeDtypeStruct((1, D_MODEL), F32), vec)],
            _norm_bwd_epilogue(D_MODEL), None, fill=prev, summed=(2,))
        prev = res[:2]
        dgain = res[2] if dgain is None else dgain + res[2]
    return prev[0] if norm_bwd is None else (prev[0], prev[1], dgain)


def _mm_nn(name, a, b, out_dtype, res=None, gain=None):
    s, k = a.shape
    nn = b.shape[1]
    bm = _row_block(s)
    row = pl.BlockSpec((bm, nn), lambda i, kk: (i, 0))
    term = [(a, pl.BlockSpec((bm, k), lambda i, kk: (i, 0)), b, pl.BlockSpec((k, nn), lambda i, kk: (0, 0)), NN)]
    if res is None:
        return _matmul(name, (s // bm, 1), term, [], [(jax.ShapeDtypeStruct((s, nn), out_dtype), row)], _ident, None)[0]
    extras, outs = _residual_outs(s, bm, gain)
    res_out = _matmul(name, (s // bm, 1), term, [(res, row)] + extras, outs,
                      _residual_epilogue(1.0, gain is not None), None)
    return res_out if gain is not None else res_out[0]


def _mm_nt(name, a, b, out_dtype, attn_out=None, nh=0, dv=0):
    s, nn = a.shape
    k = b.shape[0]
    bm = _row_block(s)
    term = [(a, pl.BlockSpec((bm, nn), lambda i, kk: (i, 0)), b, pl.BlockSpec((k, nn), lambda i, kk: (0, 0)), NT)]
    out = (jax.ShapeDtypeStruct((s, k), out_dtype), pl.BlockSpec((bm, k), lambda i, kk: (i, 0)))
    if attn_out is None:
        return _matmul(name, (s // bm, 1), term, [], [out], _ident, None)[0]

    def with_delta(acc, o):
        do = acc.astype(out_dtype).astype(F32)
        cols = [jnp.sum(do[:, h * dv:(h + 1) * dv] * o[:, h * dv:(h + 1) * dv].astype(F32), axis=-1, keepdims=True)
                for h in range(nh)]
        return acc, jnp.stack(cols, axis=0)

    return _matmul(
        name, (s // bm, 1), term, [(attn_out, pl.BlockSpec((bm, nh * dv), lambda i, kk: (i, 0)))],
        [out, (jax.ShapeDtypeStruct((nh, s, 1), F32), pl.BlockSpec((nh, bm, 1), lambda i, kk: (0, i, 0)))],
        with_delta, None)


def _mm_nt_norm_bwd(name, a, b, h, gain, dres):
    s, nn = a.shape
    bm = _row_block(s, 512)
    extras, outs, summed = _norm_bwd_operands(s, bm, h, gain, dres)
    return _matmul(
        name, (s // bm, 1),
        [(a, pl.BlockSpec((bm, nn), lambda i, kk: (i, 0)), b, pl.BlockSpec(b.shape, lambda i, kk: (0, 0)), NT)],
        extras, outs, _norm_bwd_epilogue(D_MODEL), None, summed=summed)


def _w_in_dx_norm_bwd(dz, w_t, h, gain, dres):
    s = dz.shape[0]
    bm = _row_block(s, 512)
    epilogue = _norm_bwd_epilogue(D_MODEL)

    def body(dz_ref, w_ref, h_ref, g_ref, r_ref, dx_ref, dxb_ref, dg_ref):
        dzv = dz_ref[...]
        dn = jnp.concatenate([_dot(dzv, w_ref[j], NN) for j in range(N_CHIPS)], axis=1)
        dx, _, dgain = epilogue(dn, h_ref[...], g_ref[...], r_ref[...])
        dx_ref[...] = dx
        dxb_ref[...] = dx.astype(BF16)

        @pl.when(pl.program_id(0) == 0)
        def _():
            dg_ref[...] = dgain

        @pl.when(pl.program_id(0) > 0)
        def _():
            dg_ref[...] += dgain

    row = pl.BlockSpec((bm, D_MODEL), lambda i: (i, 0))
    vec = pl.BlockSpec((1, D_MODEL), lambda i: (0, 0))
    return _call(
        body, name="w_in_dx", grid=(s // bm,),
        in_specs=[row, pl.BlockSpec(w_t.shape, lambda i: (0, 0, 0)), row, vec, row],
        out_specs=[row, row, vec],
        out_shape=[jax.ShapeDtypeStruct((s, D_MODEL), F32), jax.ShapeDtypeStruct((s, D_MODEL), BF16),
                   jax.ShapeDtypeStruct((1, D_MODEL), F32)],
        sem=("arbitrary",), args=[dz, w_t, h, gain, dres])


def _mm_tn(name, a, b, out_dtype=BF16):
    s, k = a.shape
    nn = b.shape[1]
    bk = _reduce_block(s)
    return _matmul(
        name, (s // bk,),
        [(a, pl.BlockSpec((bk, k), lambda kk: (kk, 0)), b, pl.BlockSpec((bk, nn), lambda kk: (kk, 0)), TN)],
        [], [(jax.ShapeDtypeStruct((k, nn), out_dtype), pl.BlockSpec((k, nn), lambda kk: (0, 0)))],
        _ident, (k, nn))[0]


def _mm_heads_fwd(name, a, w, out_dtype, w_transposed=False):
    s, k = a.shape
    nh = w.shape[0]
    nn = w.shape[1] if w_transposed else w.shape[2]
    bm = _row_block(s)
    return _matmul(
        name, (nh, s // bm, 1),
        [(a, pl.BlockSpec((bm, k), lambda h, i, kk: (i, 0)),
          w, pl.BlockSpec((None,) + w.shape[1:], lambda h, i, kk: (h, 0, 0)), NT if w_transposed else NN)],
        [], [(jax.ShapeDtypeStruct((s, nh * nn), out_dtype), pl.BlockSpec((bm, nn), lambda h, i, kk: (i, h)))],
        _ident, None)[0]


def _mm_heads_bwd(name, dy, a, w, w_transposed=False):
    s, k = a.shape
    nh = w.shape[0]
    nn = w.shape[1] if w_transposed else w.shape[2]
    bm = _row_block(s)
    bk = _reduce_block(s)
    w_spec = pl.BlockSpec((None,) + w.shape[1:], lambda i, h: (h, 0, 0))
    da = _matmul(
        name + "_dx", (s // bm, nh),
        [(dy, pl.BlockSpec((bm, nn), lambda i, h: (i, h)), w, w_spec, NN if w_transposed else NT)],
        [], [(jax.ShapeDtypeStruct((s, k), F32), pl.BlockSpec((bm, k), lambda i, h: (i, 0)))], _ident, (bm, k))[0]
    a_term = (a, pl.BlockSpec((bk, k), lambda h, kk: (kk, 0)))
    dy_term = (dy, pl.BlockSpec((bk, nn), lambda h, kk: (kk, h)))
    lhs, rhs = (dy_term, a_term) if w_transposed else (a_term, dy_term)
    dw = _matmul(
        name + "_dw", (nh, s // bk), [lhs + rhs + (TN,)],
        [], [(jax.ShapeDtypeStruct(w.shape, BF16), pl.BlockSpec((None,) + w.shape[1:], lambda h, kk: (h, 0, 0)))],
        _ident, w.shape[1:])[0]
    return da, dw


def _w_in_fwd(n, w_t):
    s = n.shape[0]
    bm = _row_block(s)
    nh, nout, kin = w_t.shape
    terms = [(n, pl.BlockSpec((bm, kin), lambda i, k, j=j: (i, j)),
              w_t, pl.BlockSpec((None, nout, kin), lambda i, k, j=j: (j, 0, 0)), NT) for j in range(nh)]
    row = pl.BlockSpec((bm, nout), lambda i, k: (i, 0))
    return _matmul("w_in", (s // bm, 1), terms, [], [(jax.ShapeDtypeStruct((s, nout), F32), row)], _ident, None)[0]


def _w_in_dw(dz, n):
    s, nout = dz.shape
    kin = n.shape[1] // N_CHIPS
    bk = _reduce_block(s)
    return _matmul(
        "w_in_dw", (N_CHIPS, s // bk),
        [(dz, pl.BlockSpec((bk, nout), lambda j, k: (k, 0)), n, pl.BlockSpec((bk, kin), lambda j, k: (k, j)), TN)],
        [], [(jax.ShapeDtypeStruct((N_CHIPS, nout, kin), BF16), pl.BlockSpec((None, nout, kin), lambda j, k: (j, 0, 0)))],
        _ident, (nout, kin))[0]


def _rope_tables(positions):
    half = ROPE_DIM // 2
    freqs = 1.0 / (ROPE_BASE ** (jnp.arange(0, ROPE_DIM, 2, dtype=F32) / ROPE_DIM))
    ang = positions.astype(F32)[:, None] * freqs
    cos, sin = jnp.cos(ang), jnp.sin(ang)
    z = jnp.zeros_like(cos)
    tc = jnp.concatenate([cos, cos, z, z], axis=-1)
    ta = jnp.concatenate([-sin, z, z, z], axis=-1)
    tb = jnp.concatenate([z, sin, z, z], axis=-1)
    assert tc.shape[-1] == 4 * half
    return tc, ta, tb


def _rope(x, tc, ta, tb):
    return x * tc + pltpu.roll(x, 96, 1) * ta + pltpu.roll(x, 32, 1) * tb


def _rope_t(dy, tc, ta, tb):
    return dy * tc + pltpu.roll(dy * ta, 32, 1) + pltpu.roll(dy * tb, 96, 1)


def _q_rope(q, tc, ta, tb, transpose):
    s = q.shape[0]
    bm = _row_block(s, 512)
    rot = _rope_t if transpose else _rope

    def body(q_ref, tc_ref, ta_ref, tb_ref, o_ref):
        c, a, b = tc_ref[...], ta_ref[...], tb_ref[...]
        for h in range(MLA_HEADS):
            lo = h * HEAD_QK
            o_ref[:, lo:lo + 128] = q_ref[:, lo:lo + 128].astype(BF16)
            o_ref[:, lo + 128:lo + 256] = rot(q_ref[:, lo + 128:lo + 256], c, a, b).astype(BF16)

    row = pl.BlockSpec((bm, MLA_HEADS * HEAD_QK), lambda i: (i, 0))
    tab = pl.BlockSpec((bm, 128), lambda i: (i, 0))
    return pl.pallas_call(
        body, name="q_rope_t" if transpose else "q_rope", grid=(s // bm,), in_specs=[row, tab, tab, tab],
        out_specs=row, out_shape=jax.ShapeDtypeStruct((s, MLA_HEADS * HEAD_QK), BF16),
        compiler_params=_params("parallel"),
    )(q, tc, ta, tb)


def _kv_assemble(kv, z, tc, ta, tb):
    s = kv.shape[0]
    bm = _row_block(s, 512)

    def body(kv_ref, kr_ref, tc_ref, ta_ref, tb_ref, k_ref, v_ref):
        kpe = _rope(kr_ref[...], tc_ref[...], ta_ref[...], tb_ref[...]).astype(BF16)
        for h in range(MLA_HEADS):
            lo = h * 256
            k_ref[:, lo:lo + 128] = kv_ref[:, lo:lo + 128].astype(BF16)
            k_ref[:, lo + 128:lo + 256] = kpe
            v_ref[:, h * 128:(h + 1) * 128] = kv_ref[:, lo + 128:lo + 256].astype(BF16)

    row = pl.BlockSpec((bm, 1024), lambda i: (i, 0))
    tab = pl.BlockSpec((bm, 128), lambda i: (i, 0))
    return pl.pallas_call(
        body, name="kv_assemble", grid=(s // bm,),
        in_specs=[row, pl.BlockSpec((bm, 128), lambda i: (i, 3)), tab, tab, tab],
        out_specs=[row, pl.BlockSpec((bm, 512), lambda i: (i, 0))],
        out_shape=[jax.ShapeDtypeStruct((s, 1024), BF16), jax.ShapeDtypeStruct((s, 512), BF16)],
        compiler_params=_params("parallel"),
    )(kv, z, tc, ta, tb)


def _kv_assemble_bwd(dk, dv, tc, ta, tb):
    s = dk.shape[0]
    bm = _row_block(s, 512)

    def body(dk_ref, dv_ref, tc_ref, ta_ref, tb_ref, dkv_ref, dkr_ref):
        dpe = None
        for h in range(MLA_HEADS):
            lo = h * 256
            dkv_ref[:, lo:lo + 128] = dk_ref[:, lo:lo + 128].astype(BF16)
            dkv_ref[:, lo + 128:lo + 256] = dv_ref[:, h * 128:(h + 1) * 128].astype(BF16)
            t = dk_ref[:, lo + 128:lo + 256]
            dpe = t if dpe is None else dpe + t
        dkr_ref[...] = _rope_t(dpe, tc_ref[...], ta_ref[...], tb_ref[...])

    row = pl.BlockSpec((bm, 1024), lambda i: (i, 0))
    tab = pl.BlockSpec((bm, 128), lambda i: (i, 0))
    return pl.pallas_call(
        body, name="kv_assemble_bwd", grid=(s // bm,),
        in_specs=[row, pl.BlockSpec((bm, 512), lambda i: (i, 0)), tab, tab, tab],
        out_specs=[row, tab],
        out_shape=[jax.ShapeDtypeStruct((s, 1024), BF16), jax.ShapeDtypeStruct((s, 128), F32)],
        compiler_params=_params("parallel"),
    )(dk, dv, tc, ta, tb)


def _norm_bf16(x, g):
    r = lax.rsqrt(jnp.mean(x * x, axis=-1, keepdims=True) + RMS_EPS)
    return ((x * r) * g).astype(BF16)


def _qkv_prep(z, q_gain, kv_gain, wq_t, wkv, tc, ta, tb):
    s = z.shape[0]
    bm = _row_block(s, 512)

    def body(zq_ref, zkv_ref, zkr_ref, qg_ref, kvg_ref, wq_ref, wkv_ref, tc_ref, ta_ref, tb_ref,
             qn_ref, kvn_ref, q_ref, k_ref, v_ref):
        c, a, b = tc_ref[...], ta_ref[...], tb_ref[...]
        qn = _norm_bf16(zq_ref[...], qg_ref[...])
        kvn = _norm_bf16(zkv_ref[...], kvg_ref[...])
        qn_ref[...] = qn
        kvn_ref[...] = kvn
        kpe = _rope(zkr_ref[...], c, a, b).astype(BF16)
        for h in range(MLA_HEADS):
            lo = h * HEAD_QK
            qp = _dot(qn, wq_ref[h], NT)
            q_ref[:, lo:lo + 128] = qp[:, :128].astype(BF16)
            q_ref[:, lo + 128:lo + 256] = _rope(qp[:, 128:], c, a, b).astype(BF16)
            kv = _dot(kvn, wkv_ref[h], NN)
            k_ref[:, lo:lo + 128] = kv[:, :128].astype(BF16)
            k_ref[:, lo + 128:lo + 256] = kpe
            v_ref[:, h * HEAD_V:(h + 1) * HEAD_V] = kv[:, 128:].astype(BF16)

    def cols(width, blk):
        return pl.BlockSpec((bm, width), lambda i: (i, blk))

    def whole(a):
        return pl.BlockSpec(a.shape, lambda i: (0,) * a.ndim)

    tab = cols(128, 0)
    return _call(
        body, name="qkv_prep", grid=(s // bm,),
        in_specs=[cols(Q_LORA, 0), cols(KV_LORA, 2), cols(128, 3), whole(q_gain), whole(kv_gain), whole(wq_t),
                  whole(wkv), tab, tab, tab],
        out_specs=[cols(Q_LORA, 0), cols(KV_LORA, 0), cols(1024, 0), cols(1024, 0), cols(512, 0)],
        out_shape=[jax.ShapeDtypeStruct((s, Q_LORA), BF16), jax.ShapeDtypeStruct((s, KV_LORA), BF16),
                   jax.ShapeDtypeStruct((s, 1024), BF16), jax.ShapeDtypeStruct((s, 1024), BF16),
                   jax.ShapeDtypeStruct((s, 512), BF16)],
        sem=("parallel",), args=[z, z, z, q_gain, kv_gain, wq_t, wkv, tc, ta, tb])


def _qkv_prep_bwd(dq, dk, dv, z, qn, kvn, q_gain, kv_gain, wq_t, wkv, tc, ta, tb):
    s = z.shape[0]
    bm = _row_block(s, 512)
    nsteps = s // bm

    def body(dq_ref, dk_ref, dv_ref, zq_ref, zkv_ref, qn_ref, kvn_ref, qg_ref, kvg_ref, wq_ref, wkv_ref,
             tc_ref, ta_ref, tb_ref, dz_ref, dqg_ref, dkvg_ref, dwq_ref, dwkv_ref, wq_acc, wkv_acc):
        i = pl.program_id(0)
        c, a, b = tc_ref[...], ta_ref[...], tb_ref[...]

        @pl.when(i == 0)
        def _():
            wq_acc[...] = jnp.zeros_like(wq_acc)
            wkv_acc[...] = jnp.zeros_like(wkv_acc)

        qn, kvn = qn_ref[...], kvn_ref[...]
        dqn = jnp.zeros((bm, Q_LORA), F32)
        dkvn = jnp.zeros((bm, KV_LORA), F32)
        dpe = jnp.zeros((bm, 128), F32)
        for h in range(MLA_HEADS):
            lo = h * HEAD_QK
            dqp = jnp.concatenate([dq_ref[:, lo:lo + 128].astype(BF16),
                                   _rope_t(dq_ref[:, lo + 128:lo + 256], c, a, b).astype(BF16)], axis=1)
            dqn = dqn + _dot(dqp, wq_ref[h], NN)
            wq_acc[h] += _dot(dqp, qn, TN)
            dkv = jnp.concatenate([dk_ref[:, lo:lo + 128].astype(BF16),
                                   dv_ref[:, h * HEAD_V:(h + 1) * HEAD_V].astype(BF16)], axis=1)
            dkvn = dkvn + _dot(dkv, wkv_ref[h], NT)
            wkv_acc[h] += _dot(kvn, dkv, TN)
            dpe = dpe + dk_ref[:, lo + 128:lo + 256]
        dcq, dqg = _rms_bwd_math(dqn, zq_ref[...], qg_ref[...], Q_LORA)
        dckv, dkvg = _rms_bwd_math(dkvn, zkv_ref[...], kvg_ref[...], KV_LORA)
        dz_ref[:, 0:Q_LORA] = dcq.astype(BF16)
        dz_ref[:, Q_LORA:Q_LORA + KV_LORA] = dckv.astype(BF16)
        dz_ref[:, Q_LORA + KV_LORA:512] = _rope_t(dpe, c, a, b).astype(BF16)

        @pl.when(i == 0)
        def _():
            dqg_ref[...] = dqg
            dkvg_ref[...] = dkvg

        @pl.when(i > 0)
        def _():
            dqg_ref[...] += dqg
            dkvg_ref[...] += dkvg

        @pl.when(i == nsteps - 1)
        def _():
            dwq_ref[...] = wq_acc[...].astype(BF16)
            dwkv_ref[...] = wkv_acc[...].astype(BF16)

    def cols(width, blk):
        return pl.BlockSpec((bm, width), lambda i: (i, blk))

    def whole(shape):
        return pl.BlockSpec(shape, lambda i: (0,) * len(shape))

    tab = cols(128, 0)
    return _call(
        body, name="qkv_prep_bwd", grid=(nsteps,),
        in_specs=[cols(1024, 0), cols(1024, 0), cols(512, 0), cols(Q_LORA, 0), cols(KV_LORA, 2), cols(Q_LORA, 0),
                  cols(KV_LORA, 0), whole(q_gain.shape), whole(kv_gain.shape), whole(wq_t.shape), whole(wkv.shape),
                  tab, tab, tab],
        out_specs=[cols(512, 0), whole(q_gain.shape), whole(kv_gain.shape), whole(wq_t.shape), whole(wkv.shape)],
        out_shape=[jax.ShapeDtypeStruct((s, 512), BF16), jax.ShapeDtypeStruct(q_gain.shape, F32),
                   jax.ShapeDtypeStruct(kv_gain.shape, F32), jax.ShapeDtypeStruct(wq_t.shape, BF16),
                   jax.ShapeDtypeStruct(wkv.shape, BF16)],
        scratch_shapes=[pltpu.VMEM(wq_t.shape, F32), pltpu.VMEM(wkv.shape, F32)],
        sem=("arbitrary",), args=[dq, dk, dv, z, z, qn, kvn, q_gain, kv_gain, wq_t, wkv, tc, ta, tb])


def _causal_mask(s, row0, col0):
    rows = row0 + lax.broadcasted_iota(jnp.int32, s.shape, 0)
    cols = col0 + lax.broadcasted_iota(jnp.int32, s.shape, 1)
    return jnp.where(cols <= rows, s, -jnp.inf)


def _attn_fwd(name, q, k, k_off, v, v_off, nh, dq, dv, scale, causal, blk):
    sq, sk = q.shape[0], k.shape[0]
    bq = min(blk, sq)
    bk = min(blk, sk)
    nkv = sk // bk
    assert not causal or (sq == sk and bq == bk)

    hq = bq
    log2e = 1.4426950408889634
    c2 = scale * log2e

    def body(q_ref, k_ref, v_ref, o_ref, lse_ref):
        qi = pl.program_id(1)
        qs = (q_ref[...],)

        def step(j, carry, masked):
            rows = pl.ds(pl.multiple_of(j * bk, bk), bk)
            kb, vb = k_ref[rows, :], v_ref[rows, :]
            out = []
            for t, (m, l, acc) in enumerate(carry):
                s = _dot(qs[t], kb, NT) * c2
                if masked:
                    s = _causal_mask(s, qi * bq + t * hq, j * bk)
                m_new = jnp.maximum(m, jnp.max(s, axis=-1, keepdims=True))
                alpha = jnp.exp2(m - m_new)
                p = jnp.exp2(s - m_new)
                l = alpha * l + jnp.sum(p, axis=-1, keepdims=True)
                acc = alpha * acc + _dot(p, vb, NN)
                out.append((m_new, l, acc))
            return tuple(out)

        one = (jnp.full((hq, 1), -jnp.inf, F32), jnp.zeros((hq, 1), F32), jnp.zeros((hq, dv), F32))
        init = (one,)
        if causal:
            carry = lax.fori_loop(0, qi, lambda j, c: step(j, c, False), init)
            fin = step(qi, carry, True)
        else:
            fin = lax.fori_loop(0, nkv, lambda j, c: step(j, c, False), init)
        for t, (m, l, acc) in enumerate(fin):
            o_ref[t * hq:(t + 1) * hq, :] = (acc / l).astype(o_ref.dtype)
            lse_ref[t * hq:(t + 1) * hq, :] = m * (1.0 / log2e) + jnp.log(l)

    return _call(
        body, name=name, grid=(nh, sq // bq),
        in_specs=[pl.BlockSpec((bq, dq), lambda h, i: (i, h)),
                  pl.BlockSpec((sk, dq), lambda h, i: (0, k_off + h)),
                  pl.BlockSpec((sk, dv), lambda h, i: (0, v_off + h))],
        out_specs=[pl.BlockSpec((bq, dv), lambda h, i: (i, h)), pl.BlockSpec((None, bq, 1), lambda h, i: (h, i, 0))],
        out_shape=[jax.ShapeDtypeStruct((sq, nh * dv), BF16), jax.ShapeDtypeStruct((nh, sq, 1), F32)],
        sem=("parallel", "parallel"), args=[q, k, v])


def _attn_delta(name, do, do_off, o, nh, dv):
    s = o.shape[0]
    bm = _row_block(s, 512)

    def body(do_ref, o_ref, d_ref):
        d_ref[...] = jnp.sum(do_ref[...].astype(F32) * o_ref[...].astype(F32), axis=-1, keepdims=True)

    return pl.pallas_call(
        body, name=name, grid=(nh, s // bm),
        in_specs=[pl.BlockSpec((bm, dv), lambda h, i: (i, do_off + h)), pl.BlockSpec((bm, dv), lambda h, i: (i, h))],
        out_specs=pl.BlockSpec((None, bm, 1), lambda h, i: (h, i, 0)),
        out_shape=jax.ShapeDtypeStruct((nh, s, 1), F32),
        compiler_params=_params("parallel", "parallel"),
    )(do, o)


def _attn_bwd(name, q, k, k_off, v, v_off, do, do_off, lse, delta, nh, dq, dv, scale, causal, blk):
    sq, sk = q.shape[0], k.shape[0]
    bq = min(blk, sq)
    bk = min(blk, sk)
    nq = sq // bq
    assert not causal or (sq == sk and bq == bk)

    def body(q_ref, k_ref, v_ref, do_ref, lse_ref, dl_ref, dq_ref, dk_ref, dv_ref, dk_acc, dv_acc):
        j = pl.program_id(1)

        @pl.when(j == 0)
        def _():
            dq_ref[...] = jnp.zeros_like(dq_ref)

        dk_acc[...] = jnp.zeros_like(dk_acc)
        dv_acc[...] = jnp.zeros_like(dv_acc)
        kv = k_ref[...]
        vv = v_ref[...]

        def step(i, masked):
            rows = pl.ds(pl.multiple_of(i * bq, bq), bq)
            qv = q_ref[rows, :]
            dov = do_ref[rows, :].astype(BF16)
            s = _dot(qv, kv, NT) * scale
            if masked:
                s = _causal_mask(s, i * bq, j * bk)
            p = jnp.exp(s - lse_ref[rows, :])
            dp = _dot(dov, vv, NT)
            ds = (p * (dp - dl_ref[rows, :]) * scale).astype(BF16)
            dv_acc[...] += _dot(p, dov, TN)
            dk_acc[...] += _dot(ds, qv, TN)
            dq_ref[rows, :] += _dot(ds, kv, NN)

        if causal:
            step(j, True)

            def loop(i, c):
                step(i, False)
                return c

            lax.fori_loop(j + 1, nq, loop, 0)
        else:
            def loop(i, c):
                step(i, False)
                return c

            lax.fori_loop(0, nq, loop, 0)
        dk_ref[...] = dk_acc[...]
        dv_ref[...] = dv_acc[...]

    stat = pl.BlockSpec((None, sq, 1), lambda h, j: (h, 0, 0))
    return _call(
        body, name=name, grid=(nh, sk // bk),
        in_specs=[pl.BlockSpec((sq, dq), lambda h, j: (0, h)),
                  pl.BlockSpec((bk, dq), lambda h, j: (j, k_off + h)),
                  pl.BlockSpec((bk, dv), lambda h, j: (j, v_off + h)),
                  pl.BlockSpec((sq, dv), lambda h, j: (0, do_off + h)), stat, stat],
        out_specs=[pl.BlockSpec((sq, dq), lambda h, j: (0, h)),
                   pl.BlockSpec((bk, dq), lambda h, j: (j, h)),
                   pl.BlockSpec((bk, dv), lambda h, j: (j, h))],
        out_shape=[jax.ShapeDtypeStruct((sq, nh * dq), F32), jax.ShapeDtypeStruct((sk, nh * dq), F32),
                   jax.ShapeDtypeStruct((sk, nh * dv), F32)],
        scratch_shapes=[pltpu.VMEM((bk, dq), F32), pltpu.VMEM((bk, dv), F32)],
        sem=("parallel", "arbitrary"), args=[q, k, v, do, lse, delta])


def _pool_diff(z, g):
    s = z.shape[0]
    t = lax.broadcasted_iota(jnp.int32, z.shape, 0)
    acc = z
    sums = []
    for k in (1, 2, 4, 8):
        acc = acc + jnp.where(t >= k, pltpu.roll(acc, k, 0), 0.0)
        sums.append(acc)
    win = jnp.where(g == 0, sums[0], jnp.where(g == 1, sums[1], jnp.where(g == 2, sums[2], sums[3])))
    w = lax.shift_left(jnp.int32(2), g)
    count = jnp.minimum(t + 1, w).astype(F32)
    del s
    return win / count - z, count


def _pool_fwd(z, pool_w, pool_scale):
    s = z.shape[0]

    def body(z_ref, w_ref, sc_ref, o_ref):
        diff, _ = _pool_diff(z_ref[...], pl.program_id(0))
        o_ref[...] = (_dot(diff, w_ref[...], NN) * sc_ref[...]).astype(o_ref.dtype)

    return _call(
        body, name="pool_fwd", grid=(POOL_GROUPS,),
        in_specs=[pl.BlockSpec((s, POOL_CH), lambda g: (0, 4 + g)),
                  pl.BlockSpec((None, POOL_CH, POOL_CH), lambda g: (g, 0, 0)),
                  pl.BlockSpec((1, POOL_CH), lambda g: (0, g))],
        out_specs=[pl.BlockSpec((s, POOL_CH), lambda g: (0, g))],
        out_shape=[jax.ShapeDtypeStruct((s, POOL_GROUPS * POOL_CH), BF16)],
        sem=("parallel",), args=[z, pool_w, pool_scale])[0]


def _pool_bwd(dcat, z, pool_w, pool_scale):
    s = z.shape[0]

    def body(dp_ref, z_ref, w_ref, sc_ref, dz_ref, dw_ref, dsc_ref):
        g = pl.program_id(0)
        diff, count = _pool_diff(z_ref[...], g)
        dpf = dp_ref[...].astype(F32)
        u = _dot(diff, w_ref[...], NN)
        dsc_ref[...] = jnp.sum(dpf * u, axis=0, keepdims=True)
        du = (dpf * sc_ref[...]).astype(BF16)
        dw_ref[...] = _dot(diff, du, TN)
        ddiff = _dot(du, w_ref[...], NT)
        t = lax.broadcasted_iota(jnp.int32, ddiff.shape, 0)
        acc = ddiff / count
        sums = []
        for k in (1, 2, 4, 8):
            acc = acc + jnp.where(t < s - k, pltpu.roll(acc, s - k, 0), 0.0)
            sums.append(acc)
        win = jnp.where(g == 0, sums[0], jnp.where(g == 1, sums[1], jnp.where(g == 2, sums[2], sums[3])))
        dz_ref[...] = win - ddiff

    return pl.pallas_call(
        body, name="pool_bwd", grid=(POOL_GROUPS,),
        in_specs=[pl.BlockSpec((s, POOL_CH), lambda g: (0, 4 + g)),
                  pl.BlockSpec((s, POOL_CH), lambda g: (0, 4 + g)),
                  pl.BlockSpec((None, POOL_CH, POOL_CH), lambda g: (g, 0, 0)),
                  pl.BlockSpec((1, POOL_CH), lambda g: (0, g))],
        out_specs=[pl.BlockSpec((s, POOL_CH), lambda g: (0, g)),
                   pl.BlockSpec((None, POOL_CH, POOL_CH), lambda g: (g, 0, 0)),
                   pl.BlockSpec((1, POOL_CH), lambda g: (0, g))],
        out_shape=[jax.ShapeDtypeStruct((s, POOL_GROUPS * POOL_CH), F32),
                   jax.ShapeDtypeStruct((POOL_GROUPS, POOL_CH, POOL_CH), F32),
                   jax.ShapeDtypeStruct((1, POOL_GROUPS * POOL_CH), F32)],
        compiler_params=_params("parallel"),
    )(dcat, z, pool_w, pool_scale)


def _local_step(x, mem, positions, target, w, grads):
    tc, ta, tb = _rope_tables(positions)
    blk = _ATT_BLOCK

    if "ffn1_shards" in w:
        n1, a1, dadu1, dadg1, w["ffn1_w_gate"], w["ffn1_w_up"], w["ffn1_w_down"] = _ffn1_up_gather(
            x, w["ffn1_norm"], *w["ffn1_shards"])
    else:
        n1 = _rmsnorm_fwd("ffn1_norm", x, w["ffn1_norm"], D_MODEL)
        a1, dadu1, dadg1 = _ffn_up("ffn1_up", n1, w["ffn1_w_gate"], w["ffn1_w_up"])
    h1, n2 = _ffn_down("ffn1_down", a1, w["ffn1_w_down"], x, w["mix_norm"])
    z = _w_in_fwd(n2, w["w_in"])
    qn, kvn, qf, kf, vf = _qkv_prep(z, w["q_norm"], w["kv_norm"], w["w_q_up"], w["w_kv_up"], tc, ta, tb)
    att, lse = _attn_fwd("mla_fwd", qf, kf, 0, vf, 0, MLA_HEADS, HEAD_QK, HEAD_V, MLA_SCALE, True, blk)
    pool = _pool_fwd(z, w["pool_w"], w["pool_scale"])
    s = x.shape[0]
    bm = _row_block(s)
    row = pl.BlockSpec((bm, D_MODEL), lambda i, k: (i, 0))
    half = pl.BlockSpec((bm, 512), lambda i, k: (i, 0))
    h2, n3 = _matmul(
        "w_out", (s // bm, 1),
        [(att, half, w["w_out"], pl.BlockSpec((512, D_MODEL), lambda i, k: (0, 0)), NN),
         (pool, half, w["w_out"], pl.BlockSpec((512, D_MODEL), lambda i, k: (1, 0)), NN)],
        [(h1, row)] + _residual_outs(s, bm, w["xattn_norm"])[0], _residual_outs(s, bm, w["xattn_norm"])[1],
        _residual_epilogue(1.0, True), None)
    memn = _rmsnorm_fwd("mem_norm", mem, w["mem_norm"], D_MODEL)
    qm = _mm_nn("w_mq", n3, w["w_mq"], BF16)
    kvm = _mm_heads_fwd("w_mkv", memn, w["w_mkv"], BF16)
    om, lse_m = _attn_fwd("xattn_fwd", qm, kvm, 0, kvm, MEM_HEADS, MEM_HEADS, MEM_HEAD_DIM, MEM_HEAD_DIM,
                          MEM_SCALE, False, blk)
    h3, n4 = _mm_nn("w_mo", om, w["w_mo"], F32, res=h2, gain=w["ffn2_norm"])
    a2, dadu2, dadg2 = _ffn_up("ffn2_up", n4, w["ffn2_w_gate"], w["ffn2_w_up"])
    dh4, dh4b, loss_vec, d_final = _ffn_down("ffn2_down", a2, w["ffn2_w_down"], h3, loss=(w["final_norm"], target))
    grads["final_norm"] = d_final

    dh3, dh3b, grads["ffn2_norm"] = _ffn_bwd("ffn2", dh4b, n4, dadg2, dadu2, a2, w["ffn2_w_gate"], w["ffn2_w_up"],
                                             w["ffn2_w_down"], grads, norm_bwd=(h3, w["ffn2_norm"], dh4))

    dom, delta_m = _mm_nt("w_mo_dx", dh3b, w["w_mo"], BF16, attn_out=om, nh=MEM_HEADS, dv=MEM_HEAD_DIM)
    grads["w_mo"] = _mm_tn("w_mo_dw", om, dh3b)
    dqm, dkm, dvm = _attn_bwd("xattn_bwd", qm, kvm, 0, kvm, MEM_HEADS, dom, 0, lse_m, delta_m, MEM_HEADS,
                              MEM_HEAD_DIM, MEM_HEAD_DIM, MEM_SCALE, False, blk)
    dkvm = jnp.concatenate([dkm, dvm], axis=1).astype(BF16)
    dh2, dh2b, grads["xattn_norm"] = _mm_nt_norm_bwd("w_mq_dx", dqm, w["w_mq"], h2, w["xattn_norm"], dh3)
    grads["w_mq"] = _mm_tn("w_mq_dw", n3, dqm)
    dmemn, grads["w_mkv"] = _mm_heads_bwd("w_mkv", dkvm, memn, w["w_mkv"])
    _, grads["mem_norm"] = _rmsnorm_bwd("mem_norm_bwd", dmemn, mem, w["mem_norm"], D_MODEL, out_dtype=BF16)

    dcat, delta = _mm_nt("w_out_dx", dh2b, w["w_out"], BF16, attn_out=att, nh=MLA_HEADS, dv=HEAD_V)
    grads["w_out"] = jnp.concatenate([_mm_tn("w_out_dw_a", att, dh2b), _mm_tn("w_out_dw_p", pool, dh2b)], axis=0)
    dzp, grads["pool_w"], grads["pool_scale"] = _pool_bwd(dcat, z, w["pool_w"], w["pool_scale"])
    dqf, dkf, dvf = _attn_bwd("mla_bwd", qf, kf, 0, vf, 0, dcat, 0, lse, delta, MLA_HEADS, HEAD_QK, HEAD_V,
                              MLA_SCALE, True, blk)
    dz_lat, grads["q_norm"], grads["kv_norm"], grads["w_q_up"], grads["w_kv_up"] = _qkv_prep_bwd(
        dqf, dkf, dvf, z, qn, kvn, w["q_norm"], w["kv_norm"], w["w_q_up"], w["w_kv_up"], tc, ta, tb)
    dz = jnp.concatenate([dz_lat, dzp.astype(BF16)], axis=1)
    grads["w_in"] = _w_in_dw(dz, n2)
    dh1, dh1b, grads["mix_norm"] = _w_in_dx_norm_bwd(dz, w["w_in"], h1, w["mix_norm"], dh2)

    dn1 = _ffn_bwd("ffn1", dh1b, n1, dadg1, dadu1, a1, w["ffn1_w_gate"], w["ffn1_w_up"], w["ffn1_w_down"], grads)
    dx, grads["ffn1_norm"], _ = _rmsnorm_bwd("ffn1_norm_bwd", dn1, x, w["ffn1_norm"], D_MODEL, dres=dh1)
    return loss_vec[0, 0], dx


def _mesh_pos():
    x, y, c = lax.axis_index("x"), lax.axis_index("y"), lax.axis_index("c")
    chips = [(1 - x, y), (x, 1 - y), (1 - x, 1 - y)]
    chip_ids = [2 * cx + cy for cx, cy in chips]
    return x, y, c, 2 * x + y, chips, chip_ids


def _half_rows(c, rows):
    hr = rows // 2
    return pl.ds(pl.multiple_of(c * hr, 16), hr), pl.ds(pl.multiple_of((1 - c) * hr, 16), hr)


_COPY_PIECES = 2


def _row_pieces(rows):
    if rows % (16 * _COPY_PIECES) or rows < 128:
        return [(0, rows)]
    size = rows // _COPY_PIECES
    return [(i * size, size) for i in range(_COPY_PIECES)]


def _ag_ici_stage(shards, relative=False, peers=(0, 1, 2), into=None):
    n = len(shards)
    first = into is None
    pieces = [_row_pieces(s.shape[0] // 2) for s in shards]

    def copies(ins, outs):
        x, y, c, me, chips, _ = _mesh_pos()
        out = []
        for k in range(n):
            hr = ins[k].shape[0] // 2
            if first:
                out.append((ins[k], outs[k].at[0 if relative else me], None))
            for j in peers:
                slab = _REL_OF_PEER[j] if relative else me
                for lo, size in pieces[k]:
                    rows = pl.ds(pl.multiple_of(c * hr + lo, 16), size)
                    out.append((ins[k].at[rows], outs[k].at[slab, rows], (*chips[j], c)))
        return out

    outs = [jax.ShapeDtypeStruct((N_CHIPS,) + s.shape, s.dtype) for s in shards]
    n_remote = len(peers) * sum(len(p) for p in pieces)
    if first:
        return _Stage(shards, outs, n_remote, n, copies)
    return _Stage(list(shards) + list(into), outs, n_remote, 0, copies, aliases={n + k: k for k in range(n)})


def _quarter_rows(c, rows):
    qr = rows // 4
    return pl.ds(pl.multiple_of(c * 2 * qr, 16), qr), pl.ds(pl.multiple_of(c * 2 * qr + qr, 16), qr)


def _ag_fwd_stage(fulls, relative=False):
    n = len(fulls)

    def copies(ins, outs):
        x, y, c, me, chips, chip_ids = _mesh_pos()
        out = []
        for k in range(n):
            q0, q1 = _quarter_rows(c, ins[k].shape[1])
            from_x, from_y, diag = (2, 1, 3) if relative else chip_ids
            out.append((ins[k].at[from_x, q0], outs[k].at[diag if relative else from_x, q0], (*chips[1], c)))
            out.append((ins[k].at[from_y, q1], outs[k].at[diag if relative else from_y, q1], (*chips[0], c)))
        return out

    return _Stage(fulls, [jax.ShapeDtypeStruct(f.shape, f.dtype) for f in fulls], 2 * n, 0, copies,
                  aliases={k: k for k in range(n)})


def _ag_d2d_stage(fulls, relative=False):
    n = len(fulls)

    def copies(ins, outs):
        x, y, c, me, _, chip_ids = _mesh_pos()
        out = []
        for k in range(n):
            mine, _ = _half_rows(c, ins[k].shape[1])
            for j in range(3):
                slab = _REL_OF_PEER[j] if relative else chip_ids[j]
                out.append((ins[k].at[slab, mine], outs[k].at[slab, mine], (x, y, 1 - c)))
        return out

    return _Stage(fulls, [jax.ShapeDtypeStruct(f.shape, f.dtype) for f in fulls], 3 * n, 0, copies,
                  aliases={k: k for k in range(n)})


def _rs_swap_stage(grads):
    n = len(grads)

    def copies(ins, outs):
        x, y, c, _, _, _ = _mesh_pos()
        out = []
        for k in range(n):
            _, other = _half_rows(c, ins[k].shape[1])
            out.append((ins[k].at[:, other, :], outs[k], (x, y, 1 - c)))
        return out

    return _Stage(grads, [jax.ShapeDtypeStruct((N_CHIPS, g.shape[1] // 2, g.shape[2]), g.dtype) for g in grads],
                  n, 0, copies)


_REL_OF_PEER = (2, 1, 3)


def _rs_scatter_stage(sums, relative=False):
    n = len(sums)
    pieces = [_row_pieces(s.shape[1]) for s in sums]

    def copies(ins, outs):
        x, y, c, me, chips, chip_ids = _mesh_pos()
        out = []
        for k in range(n):
            hr = ins[k].shape[1]
            mine, _ = _half_rows(c, 2 * hr)
            out.append((ins[k].at[0 if relative else me], outs[k].at[0, mine, :], None))
            for j, (cx, cy) in enumerate(chips):
                slab = _REL_OF_PEER[j] if relative else chip_ids[j]
                for lo, size in pieces[k]:
                    dst_rows = pl.ds(pl.multiple_of(c * hr + lo, 16), size)
                    out.append((ins[k].at[slab, pl.ds(lo, size), :], outs[k].at[1 + j, dst_rows, :], (cx, cy, c)))
        return out

    return _Stage(sums, [jax.ShapeDtypeStruct((N_CHIPS, 2 * s.shape[1], s.shape[2]), s.dtype) for s in sums],
                  3 * sum(len(p) for p in pieces), n, copies)


def _rs_mirror_stage(parts):
    n = len(parts)

    def copies(ins, outs):
        x, y, c, _, _, _ = _mesh_pos()
        out = []
        for k in range(n):
            mine, _ = _half_rows(c, ins[k].shape[1])
            out.append((ins[k].at[:, mine, :], outs[k].at[:, mine, :], (x, y, 1 - c)))
        return out

    return _Stage(parts, [jax.ShapeDtypeStruct(p.shape, p.dtype) for p in parts], n, 0, copies,
                  aliases={k: k for k in range(n)})


def _pair_add(name, g, r1, core):
    _, rows, cols = g.shape
    hr = rows // 2

    def body(c_ref, g_ref, r_ref, o_ref):
        o_ref[...] = (g_ref[...].astype(F32) + r_ref[...].astype(F32)).astype(BF16)

    half = pl.BlockSpec((None, hr, cols), lambda j, c: (j, 0, 0))
    return pl.pallas_call(
        body, name=name,
        grid_spec=pltpu.PrefetchScalarGridSpec(
            num_scalar_prefetch=1, grid=(N_CHIPS,),
            in_specs=[pl.BlockSpec((None, hr, cols), lambda j, c: (j, c[0], 0)), half], out_specs=half),
        out_shape=jax.ShapeDtypeStruct((N_CHIPS, hr, cols), BF16),
        compiler_params=_params("parallel"),
    )(core, g, r1)


def _all_gather_weights(shards):
    n = len(shards)

    def body(*refs):
        ins, outs = refs[:n], refs[n:2 * n]
        send, recv, loc = refs[2 * n:]
        x, y, c, me, chips, chip_ids = _mesh_pos()
        sib = (x, y, 1 - c)

        def halves(k):
            hr = ins[k].shape[0] // 2
            return pl.ds(pl.multiple_of(c * hr, 16), hr), pl.ds(pl.multiple_of((1 - c) * hr, 16), hr)

        def remote(src, dst, k, j, dev):
            return pltpu.make_async_remote_copy(src_ref=src, dst_ref=dst, send_sem=send.at[k, j],
                                                recv_sem=recv.at[k, j], device_id=dev, device_id_type=_MESH)

        started = []
        local = []
        for k in range(n):
            mine, _ = halves(k)
            cp = pltpu.make_async_copy(ins[k], outs[k].at[me], loc.at[k])
            cp.start()
            local.append(cp)
            for j, (cx, cy) in enumerate(chips):
                cp = remote(ins[k].at[mine], outs[k].at[me, mine], k, j, (cx, cy, c))
                cp.start()
                started.append(cp)
        for k in range(n):
            mine, _ = halves(k)
            for j in range(3):
                land = outs[k].at[chip_ids[j], mine]
                remote(land, land, k, j, sib).wait_recv()
                cp = remote(land, land, k, 3 + j, sib)
                cp.start()
                started.append(cp)
        for k in range(n):
            _, other = halves(k)
            for j in range(3):
                land = outs[k].at[chip_ids[j], other]
                remote(land, land, k, 3 + j, sib).wait_recv()
        for cp in started:
            cp.wait_send()
        for cp in local:
            cp.wait()

    return pl.pallas_call(
        body, name="all_gather_weights", in_specs=[_ANY] * n, out_specs=[_ANY] * n,
        out_shape=[jax.ShapeDtypeStruct((N_CHIPS,) + s.shape, s.dtype) for s in shards],
        scratch_shapes=[pltpu.SemaphoreType.DMA((n, 6)), pltpu.SemaphoreType.DMA((n, 6)),
                        pltpu.SemaphoreType.DMA((n,))],
        compiler_params=pltpu.CompilerParams(vmem_limit_bytes=V7X_VMEM_LIMIT_BYTES),
    )(*shards)


_RS_CHUNK = 32


def _reduce_scatter(name, grads):
    n = len(grads)

    def body(*refs):
        gs, outs = refs[:n], refs[n:2 * n]
        own, r1, r2, fin = (refs[(2 + i) * n:(3 + i) * n] for i in range(4))
        a_send, a_recv, b_send, b_recv, c_send, c_recv, l_in, l_out = refs[6 * n:]
        x, y, c, me, chips, chip_ids = _mesh_pos()
        sib = (x, y, 1 - c)

        def halves(k):
            hr = gs[k].shape[1] // 2
            return hr, pl.ds(pl.multiple_of(c * hr, 16), hr), pl.ds(pl.multiple_of((1 - c) * hr, 16), hr)

        def remote(src, dst, ssem, rsem, dev):
            return pltpu.make_async_remote_copy(src_ref=src, dst_ref=dst, send_sem=ssem, recv_sem=rsem,
                                                device_id=dev, device_id_type=_MESH)

        sends, locals_in = [], []
        for k in range(n):
            hr, mine, other = halves(k)
            cp = remote(gs[k].at[:, other, :], r1[k], a_send.at[k], a_recv.at[k], sib)
            cp.start()
            sends.append(cp)
            cp = pltpu.make_async_copy(gs[k].at[:, mine, :], own[k], l_in.at[k])
            cp.start()
            locals_in.append(cp)

        for k in range(n):
            hr, mine, other = halves(k)
            locals_in[k].wait()
            remote(r1[k], r1[k], a_send.at[k], a_recv.at[k], sib).wait_recv()
            for j in range(N_CHIPS):
                def add(i, carry, k=k, j=j):
                    rows = pl.ds(pl.multiple_of(i * _RS_CHUNK, _RS_CHUNK), _RS_CHUNK)
                    own[k][j, rows, :] = (own[k][j, rows, :].astype(F32) + r1[k][j, rows, :].astype(F32)).astype(BF16)
                    return carry

                lax.fori_loop(0, hr // _RS_CHUNK, add, 0)
            for j, (cx, cy) in enumerate(chips):
                cp = remote(own[k].at[chip_ids[j]], r2[k].at[j], b_send.at[k, j], b_recv.at[k, j], (cx, cy, c))
                cp.start()
                sends.append(cp)

        locals_out = []
        for k in range(n):
            hr, mine, other = halves(k)
            for j in range(3):
                remote(r2[k].at[j], r2[k].at[j], b_send.at[k, j], b_recv.at[k, j], sib).wait_recv()

            def total(i, carry, k=k):
                rows = pl.ds(pl.multiple_of(i * _RS_CHUNK, _RS_CHUNK), _RS_CHUNK)
                acc = own[k][me, rows, :].astype(F32)
                for j in range(3):
                    acc = acc + r2[k][j, rows, :].astype(F32)
                fin[k][rows, :] = acc
                return carry

            lax.fori_loop(0, hr // _RS_CHUNK, total, 0)
            cp = remote(fin[k], outs[k].at[mine, :], c_send.at[k], c_recv.at[k], sib)
            cp.start()
            sends.append(cp)
            cp = pltpu.make_async_copy(fin[k], outs[k].at[mine, :], l_out.at[k])
            cp.start()
            locals_out.append(cp)

        for k in range(n):
            hr, mine, other = halves(k)
            land = outs[k].at[other, :]
            remote(land, land, c_send.at[k], c_recv.at[k], sib).wait_recv()
        for cp in sends:
            cp.wait_send()
        for cp in locals_out:
            cp.wait()

    scratch = []
    for g in grads:
        scratch.append(pltpu.VMEM((N_CHIPS, g.shape[1] // 2, g.shape[2]), BF16))
    for g in grads:
        scratch.append(pltpu.VMEM((N_CHIPS, g.shape[1] // 2, g.shape[2]), BF16))
    for g in grads:
        scratch.append(pltpu.VMEM((3, g.shape[1] // 2, g.shape[2]), BF16))
    for g in grads:
        scratch.append(pltpu.VMEM((g.shape[1] // 2, g.shape[2]), F32))
    dma = pltpu.SemaphoreType.DMA
    scratch += [dma((n,)), dma((n,)), dma((n, 3)), dma((n, 3)), dma((n,)), dma((n,)), dma((n,)), dma((n,))]
    return pl.pallas_call(
        body, name=name, in_specs=[_ANY] * n, out_specs=[_ANY] * n,
        out_shape=[jax.ShapeDtypeStruct(g.shape[1:], F32) for g in grads],
        scratch_shapes=scratch,
        compiler_params=pltpu.CompilerParams(vmem_limit_bytes=V7X_VMEM_LIMIT_BYTES),
    )(*grads)


def _adamw_math(w, g, m, v):
    m = ADAM_B1 * m + (1.0 - ADAM_B1) * g
    v = ADAM_B2 * v + (1.0 - ADAM_B2) * (g * g)
    m_hat = m / (1.0 - ADAM_B1 ** ADAM_STEP)
    v_hat = v / (1.0 - ADAM_B2 ** ADAM_STEP)
    delta = -ADAM_LR * (m_hat / (jnp.sqrt(v_hat) + ADAM_EPS) + ADAM_WD * w)
    return delta, m, v


def _adamw_sum(name, w, parts, m, v):
    r, c = w.shape
    br = r
    while br * c * 4 > (1 << 20) and br % 32 == 0:
        br //= 2

    def body(w_ref, p_ref, m_ref, v_ref, g_ref, d_ref, nm_ref, nv_ref):
        g = p_ref[0].astype(F32)
        for j in range(1, N_CHIPS):
            g = g + p_ref[j].astype(F32)
        d, nm, nv = _adamw_math(w_ref[...], g, m_ref[...], v_ref[...])
        g_ref[...] = g
        d_ref[...] = d
        nm_ref[...] = nm
        nv_ref[...] = nv

    spec = pl.BlockSpec((br, c), lambda i: (i, 0))
    shp = jax.ShapeDtypeStruct((r, c), F32)
    return _call(
        body, name=name, grid=(r // br,),
        in_specs=[spec, pl.BlockSpec((N_CHIPS, br, c), lambda i: (0, i, 0)), spec, spec],
        out_specs=[spec] * 4, out_shape=[shp] * 4, sem=("parallel",), args=[w, parts, m, v])


_SMALL_ROWS = 80


def _small_allreduce_adamw(gpack, wpack, mpack, vpack):
    half = _SMALL_ROWS // 2

    def body(g_ref, w_ref, m_ref, v_ref, go_ref, d_ref, nm_ref, nv_ref, sib_buf, chip_sum, buf, send, recv):
        x, y, c, me, chips, chip_ids = _mesh_pos()
        sib = (x, y, 1 - c)
        mine = pl.ds(pl.multiple_of(c * half, 8), half)

        def remote(src, dst, k, dev):
            return pltpu.make_async_remote_copy(src_ref=src, dst_ref=dst, send_sem=send.at[k], recv_sem=recv.at[k],
                                                device_id=dev, device_id_type=_MESH)

        swap = remote(g_ref, sib_buf, 0, sib)
        swap.start()
        swap.wait()
        chip_sum[...] = g_ref[...] + sib_buf[...]
        buf[me] = chip_sum[...]
        sends = [remote(chip_sum.at[mine], buf.at[me, mine], 1 + j, (cx, cy, c)) for j, (cx, cy) in enumerate(chips)]
        for cp in sends:
            cp.start()
        for cp in sends:
            cp.wait()
        mirrors = [remote(buf.at[chip_ids[j], mine], buf.at[chip_ids[j], mine], 4 + j, sib) for j in range(3)]
        for cp in mirrors:
            cp.start()
        for cp in mirrors:
            cp.wait()
        total = buf[0]
        for i in range(1, N_CHIPS):
            total = total + buf[i]
        go_ref[...] = total
        d, nm, nv = _adamw_math(w_ref[...], total, m_ref[...], v_ref[...])
        d_ref[...] = d
        nm_ref[...] = nm
        nv_ref[...] = nv

    vm = pl.BlockSpec(memory_space=pltpu.VMEM)
    shp = jax.ShapeDtypeStruct((_SMALL_ROWS, D_MODEL), F32)
    return pl.pallas_call(
        body, name="small_allreduce_adamw", in_specs=[vm] * 4, out_specs=[vm] * 4, out_shape=[shp] * 4,
        scratch_shapes=[pltpu.VMEM((_SMALL_ROWS, D_MODEL), F32), pltpu.VMEM((_SMALL_ROWS, D_MODEL), F32),
                        pltpu.VMEM((N_CHIPS, _SMALL_ROWS, D_MODEL), F32), pltpu.SemaphoreType.DMA((7,)),
                        pltpu.SemaphoreType.DMA((7,))],
        compiler_params=pltpu.CompilerParams(vmem_limit_bytes=V7X_VMEM_LIMIT_BYTES),
    )(gpack, wpack, mpack, vpack)


_SMALL_VECTORS = ("ffn1_norm", "mix_norm", "xattn_norm", "mem_norm", "ffn2_norm", "final_norm", "q_norm",
                  "kv_norm", "pool_scale")


_LOSS_ROW = 9
_VEC_ROWS = 16
_POOL_ROWS = POOL_GROUPS * POOL_CH


def _small_params_step(g, w, m, v, loss_local):
    names = list(_SMALL_VECTORS) + ["pool_w"]
    nv = len(_SMALL_VECTORS)
    widths = [g[n].shape[1] for n in _SMALL_VECTORS]
    shapes = {"vec": (_VEC_ROWS, D_MODEL), "pool": (_POOL_ROWS, POOL_CH)}

    def body(*refs):
        ins = refs[:4 * (nv + 1) + 1]
        outs = refs[len(ins):len(ins) + 4 * (nv + 1) + 1]
        vec_own, vec_sib, vec_all, pool_sib, pool_sum, pool_all, send, recv = refs[len(ins) + len(outs):]
        g_in, w_in, m_in, v_in = (ins[k * (nv + 1):(k + 1) * (nv + 1)] for k in range(4))
        loss_in = ins[-1]
        g_out, d_out, m_out, v_out = (outs[k * (nv + 1):(k + 1) * (nv + 1)] for k in range(4))
        loss_out = outs[-1]
        x, y, c, me, chips, chip_ids = _mesh_pos()
        sib = (x, y, 1 - c)

        def remote(src, dst, k, dev):
            return pltpu.make_async_remote_copy(src_ref=src, dst_ref=dst, send_sem=send.at[k], recv_sem=recv.at[k],
                                                device_id=dev, device_id_type=_MESH)

        vec_own[...] = jnp.zeros_like(vec_own)
        for i in range(nv):
            vec_own[i:i + 1, 0:widths[i]] = g_in[i][...]
        vec_own[_LOSS_ROW:_LOSS_ROW + 1, 0:128] = loss_in[...]
        swaps = [remote(vec_own, vec_sib, 0, sib), remote(g_in[nv], pool_sib, 1, sib)]
        for cp in swaps:
            cp.start()
        for cp in swaps:
            cp.wait()
        vec_all[me] = vec_own[...] + vec_sib[...]
        pool_sum[...] = g_in[nv][...] + pool_sib[...]
        pool_all[me] = pool_sum[...]
        hv, hp = _VEC_ROWS // 2, _POOL_ROWS // 2
        mine_v = pl.ds(pl.multiple_of(c * hv, 8), hv)
        mine_p = pl.ds(pl.multiple_of(c * hp, 8), hp)
        sends = []
        for j, (cx, cy) in enumerate(chips):
            sends.append(remote(vec_all.at[me, mine_v], vec_all.at[me, mine_v], 2 + j, (cx, cy, c)))
            sends.append(remote(pool_sum.at[mine_p], pool_all.at[me, mine_p], 5 + j, (cx, cy, c)))
        for cp in sends:
            cp.start()
        for cp in sends:
            cp.wait()
        mirrors = []
        for j in range(3):
            mirrors.append(remote(vec_all.at[chip_ids[j], mine_v], vec_all.at[chip_ids[j], mine_v], 8 + j, sib))
            mirrors.append(remote(pool_all.at[chip_ids[j], mine_p], pool_all.at[chip_ids[j], mine_p], 11 + j, sib))
        for cp in mirrors:
            cp.start()
        for cp in mirrors:
            cp.wait()
        vec_tot = vec_all[0]
        pool_tot = pool_all[0]
        for i in range(1, N_CHIPS):
            vec_tot = vec_tot + vec_all[i]
            pool_tot = pool_tot + pool_all[i]
        vec_sib[...] = vec_tot
        loss_out[...] = vec_sib[_LOSS_ROW:_LOSS_ROW + 1, 0:128]
        for i in range(nv + 1):
            gi = pool_tot if i == nv else vec_sib[i:i + 1, 0:widths[i]]
            d, nm, nvv = _adamw_math(w_in[i][...], gi, m_in[i][...], v_in[i][...])
            g_out[i][...] = gi
            d_out[i][...] = d
            m_out[i][...] = nm
            v_out[i][...] = nvv

    vm = pl.BlockSpec(memory_space=pltpu.VMEM)
    args = [d[n] for d in (g, w, m, v) for n in names] + [jnp.broadcast_to(loss_local.reshape(1, 1), (1, 128))]
    out_shape = [jax.ShapeDtypeStruct(g[n].shape, F32) for _ in range(4) for n in names]
    out_shape.append(jax.ShapeDtypeStruct((1, 128), F32))
    res = pl.pallas_call(
        body, name="small_params_step", in_specs=[vm] * len(args), out_specs=[vm] * len(out_shape),
        out_shape=out_shape,
        scratch_shapes=[pltpu.VMEM(shapes["vec"], F32), pltpu.VMEM(shapes["vec"], F32),
                        pltpu.VMEM((N_CHIPS,) + shapes["vec"], F32), pltpu.VMEM(shapes["pool"], F32),
                        pltpu.VMEM(shapes["pool"], F32), pltpu.VMEM((N_CHIPS,) + shapes["pool"], F32),
                        pltpu.SemaphoreType.DMA((14,)), pltpu.SemaphoreType.DMA((14,))],
        compiler_params=pltpu.CompilerParams(vmem_limit_bytes=V7X_VMEM_LIMIT_BYTES),
    )(*args)
    k = len(names)
    dicts = [dict(zip(names, res[i * k:(i + 1) * k])) for i in range(4)]
    return dicts[0], dicts[1], dicts[2], dicts[3], res[-1]


def _pack_small(d, scalar=None):
    rows = []
    for n in _SMALL_VECTORS:
        v = d[n].reshape(1, -1).astype(F32)
        rows.append(jnp.pad(v, ((0, 0), (0, D_MODEL - v.shape[1]))))
    assert len(rows) == _LOSS_ROW
    extra = jnp.zeros((1, D_MODEL), F32) if scalar is None else jnp.pad(scalar.reshape(1, 1), ((0, 0), (0, D_MODEL - 1)))
    rows.append(extra)
    rows.append(jnp.zeros((16 - len(rows), D_MODEL), F32))
    rows.append(d["pool_w"].reshape(64, D_MODEL).astype(F32))
    return jnp.concatenate(rows, axis=0)


def _unpack_small(pack, like):
    out = {}
    for i, n in enumerate(_SMALL_VECTORS):
        out[n] = pack[i, :like[n].size].reshape(like[n].shape)
    out["pool_w"] = pack[16:].reshape(like["pool_w"].shape)
    return out


_WEIGHTS = ("ffn1_norm", "ffn1_w_gate", "ffn1_w_up", "ffn1_w_down", "mix_norm", "w_in", "q_norm", "w_q_up",
            "kv_norm", "w_kv_up", "pool_w", "pool_scale", "w_out", "xattn_norm", "mem_norm", "w_mq", "w_mkv",
            "w_mo", "ffn2_norm", "ffn2_w_gate", "ffn2_w_up", "ffn2_w_down", "final_norm")
_SHARDED = ("ffn1_w_gate", "ffn1_w_up", "ffn1_w_down", "w_in", "w_q_up", "w_kv_up", "w_out", "w_mq", "w_mkv",
            "w_mo", "ffn2_w_gate", "ffn2_w_up", "ffn2_w_down")
_RS_GROUPS = (("ffn2_w_gate", "ffn2_w_up", "ffn2_w_down"),
              ("w_mo", "w_mq", "w_mkv", "w_out", "w_q_up", "w_kv_up", "w_in"),
              ("ffn1_w_gate", "ffn1_w_up", "ffn1_w_down"))
W_IN_SPLIT = Q_LORA + KV_LORA + ROPE_DIM


_FFN1 = ("ffn1_w_gate", "ffn1_w_up", "ffn1_w_down")
_TRANSPOSED = ("ffn1_w_gate", "ffn1_w_up", "ffn2_w_gate", "ffn2_w_up", "w_in", "w_q_up")


def _local_view(name, a):
    return jnp.swapaxes(a, 1, 2)[0] if name in _TRANSPOSED else a[0]


def _global_view(name, a):
    return jnp.swapaxes(a[None], 1, 2) if name in _TRANSPOSED else a[None]


def _pad_shard(name, a):
    if name == "w_in":
        return jnp.concatenate([a[:W_IN_SPLIT], jnp.zeros((64, a.shape[1]), a.dtype), a[W_IN_SPLIT:]], axis=0)
    if name == "w_q_up":
        return jnp.pad(a, ((0, 64), (0, 0)))
    return a


def _unpad_shard(name, a):
    if name == "w_in":
        return jnp.concatenate([a[:, :W_IN_SPLIT], a[:, W_IN_SPLIT + 64:]], axis=1)
    if name == "w_q_up":
        return a[:, :192]
    return a


def _stacked(g):
    return g if g.ndim == 3 else g.reshape(N_CHIPS, g.shape[0] // N_CHIPS, g.shape[1])


class _Plan:
    ALL = (0, 1, 2)
    AG_UNITS = (
        (("w_in", "w_q_up", "w_kv_up"), (("ffn1_up", ALL),), "ffn1_up"),
        (("w_out",), (("w_in", ALL),), "w_in"),
        (("w_mq",), (("qkv_prep", ALL),), "qkv_prep"),
        (("w_mkv", "w_mo", "ffn2_w_gate"), (("mla_fwd", ALL),), "mla_fwd"),
        (("ffn2_w_up",), (("xattn_fwd", ALL),), "xattn_fwd"),
        (("ffn2_w_down",), (("ffn2_up", ALL),), "ffn2_up"),
    )
    RS_UNITS = (
        (("ffn2_w_gate", "ffn2_w_up", "ffn2_w_down"), "ffn2_dn_a", "mla_bwd", "qkv_prep_bwd"),
        (("w_mo", "w_mq", "w_mkv"), "w_out_dx", "mla_bwd", "qkv_prep_bwd"),
        (("w_out", "w_q_up", "w_kv_up", "w_in"), "w_in_dx", "ffn1_dact", "ffn1_dwd"),
        (("ffn1_w_down",), "ffn1_dwg", "ffn1_dwu", "ffn1_dn_a"),
        (("ffn1_w_gate",), "ffn1_dwu", "ffn1_dn_a", "ffn1_dn_b"),
        (("ffn1_w_up",), "ffn1_dn_a", "ffn1_dn_b", "adamw_w_kv_up"),
    )
    ADAMW_ORDER = ("w_kv_up", "ffn2_w_gate", "ffn2_w_up", "ffn2_w_down", "w_mo", "w_mq", "w_mkv", "w_out", "w_q_up",
                   "w_in", "ffn1_w_down", "ffn1_w_gate", "ffn1_w_up")

    def __init__(self, shards, w, grads, core):
        self.shards, self.w, self.grads, self.core = shards, w, grads, core
        self.last_slab_step = 0
        self.parts = {}
        self.ag = [None for _ in self.AG_UNITS]
        self.rs = [[None, None, None, None] for _ in self.RS_UNITS]

    def pre(self, name):
        for i, (names, sends, d_host) in enumerate(self.AG_UNITS):
            for host, peers in sends:
                if name == host:
                    st = _ag_ici_stage([self.shards[n] for n in names], peers=peers,
                                       into=self.ag[i].results if self.ag[i] is not None else None)
                    st.start_step = self.last_slab_step if name == "ffn1_up" else 0
                    if d_host == host:
                        st.then = _ag_d2d_stage(st.outs)
                    self.ag[i] = _host(name, st)
            if name == d_host and d_host != sends[-1][0]:
                self.ag[i] = _host(name, _ag_d2d_stage(self.ag[i].results))
        for i, (names, h1, h2, h3) in enumerate(self.RS_UNITS):
            if name == h1:
                self.rs[i][0] = _host(name, _rs_swap_stage([_stacked(self.grads[n]) for n in names]))
            if name == h2:
                self.rs[i][2] = _host(name, _rs_scatter_stage(self.rs[i][1], relative=names[0] in _FFN1))
            if name == h3:
                self.rs[i][3] = _host(name, _rs_mirror_stage(self.rs[i][2].results))

    def post(self, name):
        for i, (names, sends, d_host) in enumerate(self.AG_UNITS):
            if name == d_host:
                for n, f in zip(names, self.ag[i].results):
                    self.w[n] = _full_weight(n, f)
        for i, (names, h1, h2, h3) in enumerate(self.RS_UNITS):
            if name == h1:
                self.rs[i][1] = [_pair_add("pair_add_" + n, _stacked(self.grads[n]), r1, self.core)
                                 for n, r1 in zip(names, self.rs[i][0].results)]
            if name == h3:
                for n, p in zip(names, self.rs[i][3].results):
                    self.parts[n] = p


def _full_weight(name, stacked):
    if name in ("w_out", "w_mq", "w_mo"):
        return stacked.reshape(D_MODEL, D_MODEL)
    return stacked


def kernel(x, mem, positions, ffn1_norm, ffn1_w_gate, ffn1_w_up, ffn1_w_down, mix_norm, w_in, q_norm, w_q_up, kv_norm, w_kv_up, pool_w, pool_scale, w_out, xattn_norm, mem_norm, w_mq, w_mkv, w_mo, ffn2_norm, ffn2_w_gate, ffn2_w_up, ffn2_w_down, final_norm, loss_target, m_ffn1_norm, m_ffn1_w_gate, m_ffn1_w_up, m_ffn1_w_down, m_mix_norm, m_w_in, m_q_norm, m_w_q_up, m_kv_norm, m_w_kv_up, m_pool_w, m_pool_scale, m_w_out, m_xattn_norm, m_mem_norm, m_w_mq, m_w_mkv, m_w_mo, m_ffn2_norm, m_ffn2_w_gate, m_ffn2_w_up, m_ffn2_w_down, m_final_norm, v_ffn1_norm, v_ffn1_w_gate, v_ffn1_w_up, v_ffn1_w_down, v_mix_norm, v_w_in, v_q_norm, v_w_q_up, v_kv_norm, v_w_kv_up, v_pool_w, v_pool_scale, v_w_out, v_xattn_norm, v_mem_norm, v_w_mq, v_w_mkv, v_w_mo, v_ffn2_norm, v_ffn2_w_gate, v_ffn2_w_up, v_ffn2_w_down, v_final_norm):
    wts = dict(zip(_WEIGHTS, (ffn1_norm, ffn1_w_gate, ffn1_w_up, ffn1_w_down, mix_norm, w_in, q_norm, w_q_up, kv_norm, w_kv_up, pool_w, pool_scale, w_out, xattn_norm, mem_norm, w_mq, w_mkv, w_mo, ffn2_norm, ffn2_w_gate, ffn2_w_up, ffn2_w_down, final_norm)))
    mom = dict(zip(_WEIGHTS, (m_ffn1_norm, m_ffn1_w_gate, m_ffn1_w_up, m_ffn1_w_down, m_mix_norm, m_w_in, m_q_norm, m_w_q_up, m_kv_norm, m_w_kv_up, m_pool_w, m_pool_scale, m_w_out, m_xattn_norm, m_mem_norm, m_w_mq, m_w_mkv, m_w_mo, m_ffn2_norm, m_ffn2_w_gate, m_ffn2_w_up, m_ffn2_w_down, m_final_norm)))
    var = dict(zip(_WEIGHTS, (v_ffn1_norm, v_ffn1_w_gate, v_ffn1_w_up, v_ffn1_w_down, v_mix_norm, v_w_in, v_q_norm, v_w_q_up, v_kv_norm, v_w_kv_up, v_pool_w, v_pool_scale, v_w_out, v_xattn_norm, v_mem_norm, v_w_mq, v_w_mkv, v_w_mo, v_ffn2_norm, v_ffn2_w_gate, v_ffn2_w_up, v_ffn2_w_down, v_final_norm)))
    small = [n for n in _WEIGHTS if n not in _SHARDED]

    global _PLAN
    shards = {n: _pad_shard(n, _local_view(n, wts[n])).astype(BF16) for n in _SHARDED}
    w = {n: wts[n].reshape(1, -1) for n in _SMALL_VECTORS}
    w["pool_w"] = pool_w[0].astype(BF16)
    grads = {}
    core = lax.axis_index("c").astype(jnp.int32).reshape(1)
    plan = _Plan(shards, w, grads, core)
    _PLAN = plan
    try:
        w["ffn1_shards"] = tuple(shards[n] for n in _FFN1)

        loss_local, dx = _local_step(x[0], mem[0], positions[0], loss_target[0], w, grads)

        def small_view(d):
            out = {n: d[n].reshape(1, -1) for n in _SMALL_VECTORS}
            out["pool_w"] = d["pool_w"].reshape(_POOL_ROWS, POOL_CH)
            return out

        *small_res, loss_vec = _small_params_step(small_view(grads), small_view(wts), small_view(mom),
                                                  small_view(var), loss_local)
        g_out, d_out, m_out, v_out = ({n: r[n].reshape(wts[n].shape) for n in small} for r in small_res)
        loss = loss_vec[0, 0]

        for n in _Plan.ADAMW_ORDER:
            res = _adamw_sum("adamw_" + n, _local_view(n, wts[n]), _unpad_shard(n, plan.parts[n]),
                             _local_view(n, mom[n]), _local_view(n, var[n]))
            g_out[n], d_out[n], m_out[n], v_out[n] = (_global_view(n, r) for r in res)
    finally:
        _PLAN = None
        _PENDING.clear()

    return (loss, dx[None], *[g_out[n] for n in _WEIGHTS], *[d_out[n] for n in _WEIGHTS],
            *[m_out[n] for n in _WEIGHTS], *[v_out[n] for n in _WEIGHTS])
```

```python
import jax
import jax.numpy as jnp
from jax import lax
from jax.experimental import pallas as pl
from jax.experimental.pallas import tpu as pltpu

F32 = jnp.float32
BF16 = jnp.bfloat16

D_MODEL = 1024
D_FF = 2816
N_CHIPS = 4
FF_SHARD = D_FF // N_CHIPS
MLA_HEADS = 4
Q_LORA = 256
KV_LORA = 128
ROPE_DIM = 64
HEAD_QK = 256
HEAD_V = 128
POOL_GROUPS = 4
POOL_CH = 128
MEM_HEADS = 4
MEM_HEAD_DIM = 256
RMS_EPS = 1e-6
ROPE_BASE = 10000.0
MLA_SCALE = (128 + 64) ** -0.5
MEM_SCALE = MEM_HEAD_DIM ** -0.5

ADAM_LR = 0.001
ADAM_B1 = 0.9
ADAM_B2 = 0.999
ADAM_EPS = 1e-08
ADAM_WD = 0.01
ADAM_STEP = 10

V7X_VMEM_LIMIT_BYTES = 56 * 1024 * 1024

NN = ((1,), (0,))
NT = ((1,), (1,))
TN = ((0,), (0,))


def _params(*sem):
    return pltpu.CompilerParams(dimension_semantics=sem, vmem_limit_bytes=V7X_VMEM_LIMIT_BYTES)


_MESH = pl.DeviceIdType.MESH
_ANY = pl.BlockSpec(memory_space=pl.ANY)


class _Stage:
    def __init__(self, ins, outs, n_remote, n_local, copies, aliases=None):
        self.ins, self.outs, self.n_remote, self.n_local = list(ins), list(outs), n_remote, n_local
        self.copies, self.aliases = copies, dict(aliases or {})
        self.results = None
        self.start_step = 0
        self.then = None

    def descriptors(self, in_refs, out_refs, send, recv, loc):
        ds, ri, li = [], 0, 0
        for src, dst, dev in self.copies(in_refs, out_refs):
            if dev is None:
                ds.append(pltpu.make_async_copy(src, dst, loc.at[li]))
                li += 1
            else:
                ds.append(pltpu.make_async_remote_copy(src_ref=src, dst_ref=dst, send_sem=send.at[ri],
                                                       recv_sem=recv.at[ri], device_id=dev, device_id_type=_MESH))
                ri += 1
        assert ri == self.n_remote and li == self.n_local
        return ds


_PENDING = {}


def _host(name, stage):
    _PENDING.setdefault(name, []).append(stage)
    return stage


_PLAN = None


def _call(body, **kw):
    if _PLAN is not None:
        _PLAN.pre(kw["name"])
    res = _call_hosting(body, **kw)
    if _PLAN is not None:
        _PLAN.post(kw["name"])
    return res


def _call_hosting(body, *, name, grid, in_specs, out_specs, out_shape, sem, args, scratch_shapes=(), aliases=None):
    stages = _PENDING.pop(name, [])
    scratch_shapes = list(scratch_shapes)
    if not stages:
        return pl.pallas_call(body, name=name, grid=grid, in_specs=in_specs, out_specs=out_specs,
                              out_shape=out_shape, scratch_shapes=scratch_shapes,
                              input_output_aliases=dict(aliases or {}), compiler_params=_params(*sem))(*args)
    ni, no, ns = len(in_specs), len(out_shape), len(scratch_shapes)
    c_ins = [a for st in stages for a in st.ins]
    c_outs = [o for st in stages for o in st.outs]
    nci, nco = len(c_ins), len(c_outs)
    aliases, io, oo = dict(aliases or {}), 0, 0
    for st in stages:
        for i, j in st.aliases.items():
            aliases[ni + io + i] = no + oo + j
        io += len(st.ins)
        oo += len(st.outs)
    dma = pltpu.SemaphoreType.DMA
    sems = []
    for st in stages:
        sems += [dma((max(st.n_remote, 1),)), dma((max(st.n_remote, 1),)), dma((max(st.n_local, 1),))]
    followers = [st.then for st in stages if st.then is not None]
    for st in followers:
        sems += [dma((max(st.n_remote, 1),)), dma((max(st.n_remote, 1),)), dma((max(st.n_local, 1),))]

    def wrapped(*refs):
        ins, cin = refs[:ni], refs[ni:ni + nci]
        outs, cout = refs[ni + nci:ni + nci + no], refs[ni + nci + no:ni + nci + no + nco]
        scr = refs[ni + nci + no + nco:ni + nci + no + nco + ns]
        sem_refs = refs[ni + nci + no + nco + ns:]
        step = pl.program_id(0)
        last = pl.program_id(0) == grid[0] - 1
        for ax in range(1, len(grid)):
            step = step * grid[ax] + pl.program_id(ax)
            last = jnp.logical_and(last, pl.program_id(ax) == grid[ax] - 1)

        def descriptors(si):
            io = sum(len(st.ins) for st in stages[:si])
            oo = sum(len(st.outs) for st in stages[:si])
            st = stages[si]
            return st.descriptors(cin[io:io + len(st.ins)], cout[oo:oo + len(st.outs)], *sem_refs[3 * si:3 * si + 3])

        def follower_descriptors(fi):
            si = [k for k, st in enumerate(stages) if st.then is not None][fi]
            oo = sum(len(st.outs) for st in stages[:si])
            bufs = cout[oo:oo + len(stages[si].outs)]
            k0 = 3 * (len(stages) + fi)
            return followers[fi].descriptors(bufs, bufs, *sem_refs[k0:k0 + 3])

        def start(si):
            @pl.when(step == stages[si].start_step)
            def _():
                for d in descriptors(si):
                    d.start()

        for si, st in enumerate(stages):
            if st.start_step == 0:
                start(si)
        body(*ins, *outs, *scr)
        for si, st in enumerate(stages):
            if st.start_step != 0:
                start(si)

        @pl.when(last)
        def _():
            for si in range(len(stages)):
                for d in descriptors(si):
                    d.wait()
            for fi in range(len(followers)):
                for d in follower_descriptors(fi):
                    d.start()
            for fi in range(len(followers)):
                for d in follower_descriptors(fi):
                    d.wait()

    res = pl.pallas_call(
        wrapped, name=name, grid=grid, in_specs=list(in_specs) + [_ANY] * nci,
        out_specs=list(out_specs) + [_ANY] * nco, out_shape=list(out_shape) + c_outs,
        scratch_shapes=scratch_shapes + sems, input_output_aliases=aliases,
        compiler_params=_params(*(("arbitrary",) * len(grid))))(*args, *c_ins)
    oo = no
    for st in stages:
        st.results = list(res[oo:oo + len(st.outs)])
        oo += len(st.outs)
    return list(res[:no])


def _dot(a, b, dims):
    return lax.dot_general(a.astype(BF16), b.astype(BF16), (dims, ((), ())), preferred_element_type=F32)


_MAX_ROW_BLOCK = 1024
_ATT_BLOCK = 512


_MAX_REDUCE_BLOCK = 2048


def _row_block(s, want=1024):
    return min(want, s, _MAX_ROW_BLOCK)


def _reduce_block(s):
    return min(s, _MAX_REDUCE_BLOCK)


def _matmul(name, grid, terms, extras, outs, epilogue, acc_shape, fill=(), summed=()):
    nt, ne, no, nf = len(terms), len(extras), len(outs), len(fill)
    nk = grid[-1]
    dims = [t[4] for t in terms]

    def body(*refs):
        a_refs, b_refs = refs[:nt], refs[nt:2 * nt]
        e_refs = refs[2 * nt:2 * nt + ne]
        o_refs = refs[2 * nt + ne + nf:2 * nt + ne + nf + no]

        def finish(acc):
            vals = epilogue(acc, *[e[...] for e in e_refs])
            for idx, (o, val) in enumerate(zip(o_refs, vals)):
                if idx in summed:
                    @pl.when(pl.program_id(0) == 0)
                    def _(o=o, val=val):
                        o[...] = val.astype(o.dtype)

                    @pl.when(pl.program_id(0) > 0)
                    def _(o=o, val=val):
                        o[...] += val.astype(o.dtype)
                else:
                    o[...] = val.astype(o.dtype)

        if nk == 1:
            part = None
            for a, b, d in zip(a_refs, b_refs, dims):
                t = _dot(a[...], b[...], d)
                part = t if part is None else part + t
            finish(part)
        else:
            acc_ref = refs[-1]
            k = pl.program_id(len(grid) - 1)

            @pl.when(k == 0)
            def _():
                acc_ref[...] = jnp.zeros_like(acc_ref)

            for a, b, d in zip(a_refs, b_refs, dims):
                acc_ref[...] += _dot(a[...], b[...], d)

            @pl.when(k == nk - 1)
            def _():
                finish(acc_ref[...])

    in_specs = [t[1] for t in terms] + [t[3] for t in terms] + [e[1] for e in extras] + [_ANY] * nf
    args = [t[0] for t in terms] + [t[2] for t in terms] + [e[0] for e in extras] + list(fill)
    sem = ("arbitrary" if summed else "parallel",) * (len(grid) - 1) + ("arbitrary",)
    aliases = {2 * nt + ne + i: i for i in range(nf)}
    return _call(
        body, name=name, grid=grid, in_specs=in_specs,
        out_specs=[o[1] for o in outs], out_shape=[o[0] for o in outs],
        scratch_shapes=[pltpu.VMEM(acc_shape, F32)] if nk > 1 else [], sem=sem, args=args, aliases=aliases)


def _ident(acc):
    return (acc,)


def _rmsnorm_fwd(name, x, gain, width, col_block=0):
    s = x.shape[0]
    bm = _row_block(s)

    def body(x_ref, g_ref, o_ref):
        xf = x_ref[...]
        r = lax.rsqrt(jnp.mean(xf * xf, axis=-1, keepdims=True) + RMS_EPS)
        o_ref[...] = ((xf * r) * g_ref[...]).astype(o_ref.dtype)

    return pl.pallas_call(
        body, name=name, grid=(s // bm,),
        in_specs=[pl.BlockSpec((bm, width), lambda i: (i, col_block)), pl.BlockSpec((1, width), lambda i: (0, 0))],
        out_specs=pl.BlockSpec((bm, width), lambda i: (i, 0)),
        out_shape=jax.ShapeDtypeStruct((s, width), BF16),
        compiler_params=_params("parallel"),
    )(x, gain)


def _rms_bwd_math(dy, xf, g, width):
    r = lax.rsqrt(jnp.mean(xf * xf, axis=-1, keepdims=True) + RMS_EPS)
    dyg = dy * g
    dot = jnp.sum(dyg * xf, axis=-1, keepdims=True)
    dx = r * dyg - xf * ((r * r * r) * (dot * (1.0 / width)))
    dgain = jnp.sum(dy * (xf * r), axis=0, keepdims=True)
    return dx, dgain


def _rmsnorm_bwd(name, dy, x, gain, width, col_block=0, dres=None, out_dtype=F32):
    s = x.shape[0]
    bm = _row_block(s)
    has_res = dres is not None

    def body(*refs):
        if has_res:
            dy_ref, x_ref, g_ref, r_ref, dx_ref, dg_ref, dxb_ref = refs
        else:
            dy_ref, x_ref, g_ref, dx_ref, dg_ref = refs
        dx, dgain = _rms_bwd_math(dy_ref[...].astype(F32), x_ref[...], g_ref[...], width)
        if has_res:
            dx = dx + r_ref[...]
            dxb_ref[...] = dx.astype(BF16)
        dx_ref[...] = dx.astype(dx_ref.dtype)

        @pl.when(pl.program_id(0) == 0)
        def _():
            dg_ref[...] = dgain

        @pl.when(pl.program_id(0) > 0)
        def _():
            dg_ref[...] += dgain

    row = pl.BlockSpec((bm, width), lambda i: (i, 0))
    in_specs = [row, pl.BlockSpec((bm, width), lambda i: (i, col_block)), pl.BlockSpec((1, width), lambda i: (0, 0))]
    args = [dy, x, gain]
    out_specs = [row, pl.BlockSpec((1, width), lambda i: (0, 0))]
    out_shape = [jax.ShapeDtypeStruct((s, width), out_dtype), jax.ShapeDtypeStruct((1, width), F32)]
    if has_res:
        in_specs.append(row)
        args.append(dres)
        out_specs.append(row)
        out_shape.append(jax.ShapeDtypeStruct((s, width), BF16))
    return _call(body, name=name, grid=(s // bm,), in_specs=in_specs, out_specs=out_specs, out_shape=out_shape,
                 sem=("arbitrary",), args=args)


def _ffn_up(name, n, wg, wu):
    s = n.shape[0]
    bm = _row_block(s)

    def body(n_ref, wg_ref, wu_ref, a_ref, dadu_ref, dadg_ref):
        x = n_ref[...]
        g = _dot(x, wg_ref[...], NT)
        u = _dot(x, wu_ref[...], NT)
        sg = jax.nn.sigmoid(g)
        silu = g * sg
        a_ref[...] = (silu * u).astype(BF16)
        dadu_ref[...] = silu.astype(BF16)
        dadg_ref[...] = (u * (sg * (1.0 + g * (1.0 - sg)))).astype(BF16)

    w_spec = pl.BlockSpec((None, FF_SHARD, D_MODEL), lambda j, i: (j, 0, 0))
    o_spec = pl.BlockSpec((None, bm, FF_SHARD), lambda j, i: (j, i, 0))
    shp = jax.ShapeDtypeStruct((N_CHIPS, s, FF_SHARD), BF16)
    return _call(
        body, name=name, grid=(N_CHIPS, s // bm),
        in_specs=[pl.BlockSpec((bm, D_MODEL), lambda j, i: (i, 0)), w_spec, w_spec],
        out_specs=[o_spec, o_spec, o_spec], out_shape=[shp, shp, shp],
        sem=("parallel", "parallel"), args=[n, wg, wu])


def _ffn1_up_gather(xin, gain, g_sh, u_sh, d_sh):
    s = xin.shape[0]
    bm = _row_block(s)
    nrb = s // bm
    rows, cols = g_sh.shape

    def body(x_ref, gain_ref, gs, us, ds, n_ref, a_ref, dadu_ref, dadg_ref, wg, wu, wd, gbuf, ubuf,
             send, recv, qsend, qrecv, fsend, frecv, loc, ld):
        r, i = pl.program_id(0), pl.program_id(1)
        x, y, c = lax.axis_index("x"), lax.axis_index("y"), lax.axis_index("c")
        sib = (x, y, 1 - c)
        mine, _ = _half_rows(c, rows)
        quarters = _quarter_rows(c, rows)
        shards, fulls, bufs = (gs, us, ds), (wg, wu, wd), (gbuf, ubuf)

        def remote(src, dst, ssem, rsem, dev):
            return pltpu.make_async_remote_copy(src_ref=src, dst_ref=dst, send_sem=ssem, recv_sem=rsem,
                                                device_id=dev, device_id_type=_MESH)

        def peer(rel):
            return ((1 - x) if rel & 2 else x, (1 - y) if rel & 1 else y, c)

        def ici(k, rel, dev=sib):
            return remote(shards[k].at[mine], fulls[k].at[rel, mine], send.at[k, rel - 1], recv.at[k, rel - 1], dev)

        def quarter(k, which, dev=sib):
            slab, q = ((2, quarters[0]), (1, quarters[1]))[which]
            return remote(fulls[k].at[slab, q], fulls[k].at[3, q], qsend.at[k, which], qrecv.at[k, which], dev)

        def fwd(k, rel):
            return remote(fulls[k].at[rel, mine], fulls[k].at[rel, mine], fsend.at[k, rel - 1], frecv.at[k, rel - 1], sib)

        def own(k):
            return pltpu.make_async_copy(shards[k], fulls[k].at[0], loc.at[k])

        def load(slab):
            for k in (0, 1):
                pltpu.make_async_copy(shards[k] if slab == 0 else fulls[k].at[slab], bufs[k], ld.at[k]).start()
            for k in (0, 1):
                pltpu.make_async_copy(shards[k] if slab == 0 else fulls[k].at[slab], bufs[k], ld.at[k]).wait()

        def from_neighbour(ks, rel):
            for k in ks:
                ici(k, rel).wait_recv()
                fwd(k, rel).start()
                quarter(k, 0 if rel == 2 else 1, peer(1 if rel == 2 else 2)).start()
            for k in ks:
                fwd(k, rel).wait_recv()

        def from_diagonal(ks):
            for k in ks:
                quarter(k, 0).wait_recv()
                quarter(k, 1).wait_recv()
                fwd(k, 3).start()
            for k in ks:
                fwd(k, 3).wait_recv()

        @pl.when(jnp.logical_and(r == 0, i == 0))
        def _():
            for k in range(3):
                own(k).start()
            for rel in (1, 2):
                for k in (0, 1):
                    ici(k, rel, peer(rel)).start()
            load(0)

        @pl.when(jnp.logical_and(r == 1, i == 0))
        def _():
            from_neighbour((0, 1), 1)
            load(1)
            for rel in (1, 2):
                ici(2, rel, peer(rel)).start()

        @pl.when(jnp.logical_and(r == 2, i == 0))
        def _():
            from_neighbour((0, 1), 2)
            load(2)

        @pl.when(jnp.logical_and(r == 3, i == 0))
        def _():
            from_diagonal((0, 1))
            load(3)

        xv = _norm_bf16(x_ref[...], gain_ref[...])

        @pl.when(r == 0)
        def _():
            n_ref[...] = xv

        g = _dot(xv, gbuf[...], NT)
        u = _dot(xv, ubuf[...], NT)
        sg = jax.nn.sigmoid(g)
        silu = g * sg
        a_ref[...] = (silu * u).astype(BF16)
        dadu_ref[...] = silu.astype(BF16)
        dadg_ref[...] = (u * (sg * (1.0 + g * (1.0 - sg)))).astype(BF16)

        @pl.when(jnp.logical_and(r == 3, i == nrb - 1))
        def _():
            from_neighbour((2,), 1)
            from_neighbour((2,), 2)
            from_diagonal((2,))
            for k in range(3):
                for rel in (1, 2):
                    ici(k, rel).wait_send()
                for which in (0, 1):
                    quarter(k, which).wait_send()
                for rel in (1, 2, 3):
                    fwd(k, rel).wait_send()
                own(k).wait()

    o_spec = pl.BlockSpec((None, bm, FF_SHARD), lambda r, i: (r, i, 0))
    act = jax.ShapeDtypeStruct((N_CHIPS, s, FF_SHARD), BF16)
    full = jax.ShapeDtypeStruct((N_CHIPS, rows, cols), BF16)
    dma = pltpu.SemaphoreType.DMA
    if _PLAN is not None:
        _PLAN.last_slab_step = 3 * nrb
    n_spec = pl.BlockSpec((bm, D_MODEL), lambda r, i: (jnp.where(r == 0, i, nrb - 1), 0))
    return _call(
        body, name="ffn1_up", grid=(N_CHIPS, nrb),
        in_specs=[pl.BlockSpec((bm, D_MODEL), lambda r, i: (i, 0)), pl.BlockSpec((1, D_MODEL), lambda r, i: (0, 0)),
                  _ANY, _ANY, _ANY],
        out_specs=[n_spec, o_spec, o_spec, o_spec, _ANY, _ANY, _ANY],
        out_shape=[jax.ShapeDtypeStruct((s, D_MODEL), BF16), act, act, act, full, full, full],
        scratch_shapes=[pltpu.VMEM((rows, cols), BF16), pltpu.VMEM((rows, cols), BF16), dma((3, 2)), dma((3, 2)),
                        dma((3, 2)), dma((3, 2)), dma((3, 3)), dma((3, 3)), dma((3,)), dma((2,))],
        sem=("arbitrary", "arbitrary"), args=[xin, gain, g_sh, u_sh, d_sh])


def _residual_epilogue(alpha, with_norm):
    if not with_norm:
        return lambda acc, r: (r + alpha * acc,)

    def epilogue(acc, r, g):
        h = r + alpha * acc
        rs = lax.rsqrt(jnp.mean(h * h, axis=-1, keepdims=True) + RMS_EPS)
        return h, (h * rs) * g

    return epilogue


def _residual_outs(s, bm, gain):
    row = pl.BlockSpec((bm, D_MODEL), lambda i, k: (i, 0))
    outs = [(jax.ShapeDtypeStruct((s, D_MODEL), F32), row)]
    if gain is None:
        return [], outs
    return [(gain, pl.BlockSpec((1, D_MODEL), lambda i, k: (0, 0)))], outs + [(jax.ShapeDtypeStruct((s, D_MODEL), BF16), row)]


def _loss_epilogue(acc, res, g, target):
    d = acc.shape[-1]
    h = res + 0.5 * acc
    r = lax.rsqrt(jnp.mean(h * h, axis=-1, keepdims=True) + RMS_EPS)
    err = (h * r) * g - target
    part = 0.5 * jnp.sum(jnp.mean(err * err, axis=-1, keepdims=True), axis=0, keepdims=True)
    dx, dgain = _rms_bwd_math(err * (1.0 / d), h, g, d)
    return dx, dx, jnp.broadcast_to(part, (1, 128)), dgain


def _ffn_down(name, a, wd, res, gain=None, loss=None):
    s = a.shape[1]
    bm = _row_block(s, 512)
    row = pl.BlockSpec((bm, D_MODEL), lambda i, k: (i, 0))
    terms = [(a, pl.BlockSpec((None, bm, FF_SHARD), lambda i, k, j=j: (j, i, 0)),
              wd, pl.BlockSpec((None, FF_SHARD, D_MODEL), lambda i, k, j=j: (j, 0, 0)), NN) for j in range(N_CHIPS)]
    if loss is not None:
        vec = pl.BlockSpec((1, D_MODEL), lambda i, k: (0, 0))
        outs = [(jax.ShapeDtypeStruct((s, D_MODEL), F32), row), (jax.ShapeDtypeStruct((s, D_MODEL), BF16), row),
                (jax.ShapeDtypeStruct((1, 128), F32), pl.BlockSpec((1, 128), lambda i, k: (0, 0))),
                (jax.ShapeDtypeStruct((1, D_MODEL), F32), vec)]
        return _matmul(name, (s // bm, 1), terms, [(res, row), (loss[0], vec), (loss[1], row)], outs,
                       _loss_epilogue, None, summed=(2, 3))
    extras, outs = _residual_outs(s, bm, gain)
    res_out = _matmul(name, (s // bm, 1), terms, [(res, row)] + extras, outs,
                      _residual_epilogue(0.5, gain is not None), None)
    return res_out if gain is not None else res_out[0]


def _norm_bwd_epilogue(width):
    def epilogue(acc, h, g, dres):
        dx, dgain = _rms_bwd_math(acc, h, g, width)
        dx = dx + dres
        return dx, dx, dgain

    return epilogue


def _norm_bwd_operands(s, bm, h, gain, dres):
    row = pl.BlockSpec((bm, D_MODEL), lambda i, k: (i, 0))
    vec = pl.BlockSpec((1, D_MODEL), lambda i, k: (0, 0))
    extras = [(h, row), (gain, vec), (dres, row)]
    outs = [(jax.ShapeDtypeStruct((s, D_MODEL), F32), row), (jax.ShapeDtypeStruct((s, D_MODEL), BF16), row),
            (jax.ShapeDtypeStruct((1, D_MODEL), F32), vec)]
    return extras, outs, (2,)


def _ffn_bwd(tag, dh, n, dadg, dadu, a, wg, wu, wd, grads, norm_bwd=None):
    s = dh.shape[0]
    bm = _row_block(s)
    bk = _reduce_block(s)
    nk = s // bk

    def act_bwd(acc, dg_da, du_da):
        da = 0.5 * acc
        return da * dg_da.astype(F32), da * du_da.astype(F32)

    slab = pl.BlockSpec((None, bm, FF_SHARD), lambda j, i, k: (j, i, 0))
    shp = jax.ShapeDtypeStruct((N_CHIPS, s, FF_SHARD), BF16)
    dg, du = _matmul(
        tag + "_dact", (N_CHIPS, s // bm, 1),
        [(dh, pl.BlockSpec((bm, D_MODEL), lambda j, i, k: (i, 0)),
          wd, pl.BlockSpec((None, FF_SHARD, D_MODEL), lambda j, i, k: (j, 0, 0)), NT)],
        [(dadg, slab), (dadu, slab)], [(shp, slab), (shp, slab)], act_bwd, None)

    grads[tag + "_w_down"] = _matmul(
        tag + "_dwd", (N_CHIPS, nk),
        [(a, pl.BlockSpec((None, bk, FF_SHARD), lambda j, k: (j, k, 0)),
          dh, pl.BlockSpec((bk, D_MODEL), lambda j, k: (k, 0)), TN)],
        [], [(jax.ShapeDtypeStruct((N_CHIPS, FF_SHARD, D_MODEL), BF16),
              pl.BlockSpec((None, FF_SHARD, D_MODEL), lambda j, k: (j, 0, 0)))],
        lambda acc: (0.5 * acc,), (FF_SHARD, D_MODEL))[0]

    def dw_up(nm, dact):
        return _matmul(
            nm, (N_CHIPS, nk),
            [(dact, pl.BlockSpec((None, bk, FF_SHARD), lambda j, k: (j, k, 0)),
              n, pl.BlockSpec((bk, D_MODEL), lambda j, k: (k, 0)), TN)],
            [], [(jax.ShapeDtypeStruct((N_CHIPS, FF_SHARD, D_MODEL), BF16),
                  pl.BlockSpec((None, FF_SHARD, D_MODEL), lambda j, k: (j, 0, 0)))],
            _ident, (FF_SHARD, D_MODEL))[0]

    grads[tag + "_w_gate"] = dw_up(tag + "_dwg", dg)
    grads[tag + "_w_up"] = dw_up(tag + "_dwu", du)

    bn = _row_block(s, 512)
    steps = s // bn // 2
    prev, dgain = (), None
    for part, off in (("_dn_a", 0), ("_dn_b", steps)):
        row = pl.BlockSpec((bn, D_MODEL), lambda i, k, off=off: (i + off, 0))
        terms = []
        for j in range(N_CHIPS):
            a_slab = pl.BlockSpec((None, bn, FF_SHARD), lambda i, k, j=j, off=off: (j, i + off, 0))
            w_slab = pl.BlockSpec((None, FF_SHARD, D_MODEL), lambda i, k, j=j: (j, 0, 0))
            terms += [(dg, a_slab, wg, w_slab, NN), (du, a_slab, wu, w_slab, NN)]
        if norm_bwd is None:
            prev = _matmul(tag + part, (steps, 1), terms, [], [(jax.ShapeDtypeStruct((s, D_MODEL), F32), row)],
                           _ident, None, fill=prev)
            continue
        h, gain, dres = norm_bwd
        vec = pl.BlockSpec((1, D_MODEL), lambda i, k: (0, 0))
        res = _matmul(
            tag + part, (steps, 1), terms, [(h, row), (gain, vec), (dres, row)],
            [(jax.ShapeDtypeStruct((s, D_MODEL), F32), row), (jax.ShapeDtypeStruct((s, D_MODEL), BF16), row),
             (jax.ShapeDtypeStruct((1, D_MODEL), F32), vec)],
            _norm_bwd_epilogue(D_MODEL), None, fill=prev, summed=(2,))
        prev = res[:2]
        dgain = res[2] if dgain is None else dgain + res[2]
    return prev[0] if norm_bwd is None else (prev[0], prev[1], dgain)


def _mm_nn(name, a, b, out_dtype, res=None, gain=None):
    s, k = a.shape
    nn = b.shape[1]
    bm = _row_block(s)
    row = pl.BlockSpec((bm, nn), lambda i, kk: (i, 0))
    term = [(a, pl.BlockSpec((bm, k), lambda i, kk: (i, 0)), b, pl.BlockSpec((k, nn), lambda i, kk: (0, 0)), NN)]
    if res is None:
        return _matmul(name, (s // bm, 1), term, [], [(jax.ShapeDtypeStruct((s, nn), out_dtype), row)], _ident, None)[0]
    extras, outs = _residual_outs(s, bm, gain)
    res_out = _matmul(name, (s // bm, 1), term, [(res, row)] + extras, outs,
                      _residual_epilogue(1.0, gain is not None), None)
    return res_out if gain is not None else res_out[0]


def _mm_nt(name, a, b, out_dtype, attn_out=None, nh=0, dv=0):
    s, nn = a.shape
    k = b.shape[0]
    bm = _row_block(s)
    term = [(a, pl.BlockSpec((bm, nn), lambda i, kk: (i, 0)), b, pl.BlockSpec((k, nn), lambda i, kk: (0, 0)), NT)]
    out = (jax.ShapeDtypeStruct((s, k), out_dtype), pl.BlockSpec((bm, k), lambda i, kk: (i, 0)))
    if attn_out is None:
        return _matmul(name, (s // bm, 1), term, [], [out], _ident, None)[0]

    def with_delta(acc, o):
        do = acc.astype(out_dtype).astype(F32)
        cols = [jnp.sum(do[:, h * dv:(h + 1) * dv] * o[:, h * dv:(h + 1) * dv].astype(F32), axis=-1, keepdims=True)
                for h in range(nh)]
        return acc, jnp.stack(cols, axis=0)

    return _matmul(
        name, (s // bm, 1), term, [(attn_out, pl.BlockSpec((bm, nh * dv), lambda i, kk: (i, 0)))],
        [out, (jax.ShapeDtypeStruct((nh, s, 1), F32), pl.BlockSpec((nh, bm, 1), lambda i, kk: (0, i, 0)))],
        with_delta, None)


def _mm_nt_norm_bwd(name, a, b, h, gain, dres):
    s, nn = a.shape
    bm = _row_block(s, 512)
    extras, outs, summed = _norm_bwd_operands(s, bm, h, gain, dres)
    return _matmul(
        name, (s // bm, 1),
        [(a, pl.BlockSpec((bm, nn), lambda i, kk: (i, 0)), b, pl.BlockSpec(b.shape, lambda i, kk: (0, 0)), NT)],
        extras, outs, _norm_bwd_epilogue(D_MODEL), None, summed=summed)


def _w_in_dx_norm_bwd(dz, w_t, h, gain, dres):
    s = dz.shape[0]
    bm = _row_block(s, 512)
    epilogue = _norm_bwd_epilogue(D_MODEL)

    def body(dz_ref, w_ref, h_ref, g_ref, r_ref, dx_ref, dxb_ref, dg_ref):
        dzv = dz_ref[...]
        dn = jnp.concatenate([_dot(dzv, w_ref[j], NN) for j in range(N_CHIPS)], axis=1)
        dx, _, dgain = epilogue(dn, h_ref[...], g_ref[...], r_ref[...])
        dx_ref[...] = dx
        dxb_ref[...] = dx.astype(BF16)

        @pl.when(pl.program_id(0) == 0)
        def _():
            dg_ref[...] = dgain

        @pl.when(pl.program_id(0) > 0)
        def _():
            dg_ref[...] += dgain

    row = pl.BlockSpec((bm, D_MODEL), lambda i: (i, 0))
    vec = pl.BlockSpec((1, D_MODEL), lambda i: (0, 0))
    return _call(
        body, name="w_in_dx", grid=(s // bm,),
        in_specs=[row, pl.BlockSpec(w_t.shape, lambda i: (0, 0, 0)), row, vec, row],
        out_specs=[row, row, vec],
        out_shape=[jax.ShapeDtypeStruct((s, D_MODEL), F32), jax.ShapeDtypeStruct((s, D_MODEL), BF16),
                   jax.ShapeDtypeStruct((1, D_MODEL), F32)],
        sem=("arbitrary",), args=[dz, w_t, h, gain, dres])


def _mm_tn(name, a, b, out_dtype=BF16):
    s, k = a.shape
    nn = b.shape[1]
    bk = _reduce_block(s)
    return _matmul(
        name, (s // bk,),
        [(a, pl.BlockSpec((bk, k), lambda kk: (kk, 0)), b, pl.BlockSpec((bk, nn), lambda kk: (kk, 0)), TN)],
        [], [(jax.ShapeDtypeStruct((k, nn), out_dtype), pl.BlockSpec((k, nn), lambda kk: (0, 0)))],
        _ident, (k, nn))[0]


def _mm_heads_fwd(name, a, w, out_dtype, w_transposed=False):
    s, k = a.shape
    nh = w.shape[0]
    nn = w.shape[1] if w_transposed else w.shape[2]
    bm = _row_block(s)
    return _matmul(
        name, (nh, s // bm, 1),
        [(a, pl.BlockSpec((bm, k), lambda h, i, kk: (i, 0)),
          w, pl.BlockSpec((None,) + w.shape[1:], lambda h, i, kk: (h, 0, 0)), NT if w_transposed else NN)],
        [], [(jax.ShapeDtypeStruct((s, nh * nn), out_dtype), pl.BlockSpec((bm, nn), lambda h, i, kk: (i, h)))],
        _ident, None)[0]


def _mm_heads_bwd(name, dy, a, w, w_transposed=False):
    s, k = a.shape
    nh = w.shape[0]
    nn = w.shape[1] if w_transposed else w.shape[2]
    bm = _row_block(s)
    bk = _reduce_block(s)
    w_spec = pl.BlockSpec((None,) + w.shape[1:], lambda i, h: (h, 0, 0))
    da = _matmul(
        name + "_dx", (s // bm, nh),
        [(dy, pl.BlockSpec((bm, nn), lambda i, h: (i, h)), w, w_spec, NN if w_transposed else NT)],
        [], [(jax.ShapeDtypeStruct((s, k), F32), pl.BlockSpec((bm, k), lambda i, h: (i, 0)))], _ident, (bm, k))[0]
    a_term = (a, pl.BlockSpec((bk, k), lambda h, kk: (kk, 0)))
    dy_term = (dy, pl.BlockSpec((bk, nn), lambda h, kk: (kk, h)))
    lhs, rhs = (dy_term, a_term) if w_transposed else (a_term, dy_term)
    dw = _matmul(
        name + "_dw", (nh, s // bk), [lhs + rhs + (TN,)],
        [], [(jax.ShapeDtypeStruct(w.shape, BF16), pl.BlockSpec((None,) + w.shape[1:], lambda h, kk: (h, 0, 0)))],
        _ident, w.shape[1:])[0]
    return da, dw


def _w_in_fwd(n, w_t):
    s = n.shape[0]
    bm = _row_block(s)
    nh, nout, kin = w_t.shape
    terms = [(n, pl.BlockSpec((bm, kin), lambda i, k, j=j: (i, j)),
              w_t, pl.BlockSpec((None, nout, kin), lambda i, k, j=j: (j, 0, 0)), NT) for j in range(nh)]
    row = pl.BlockSpec((bm, nout), lambda i, k: (i, 0))
    return _matmul("w_in", (s // bm, 1), terms, [], [(jax.ShapeDtypeStruct((s, nout), F32), row)], _ident, None)[0]


def _w_in_dw(dz, n):
    s, nout = dz.shape
    kin = n.shape[1] // N_CHIPS
    bk = _reduce_block(s)
    return _matmul(
        "w_in_dw", (N_CHIPS, s // bk),
        [(dz, pl.BlockSpec((bk, nout), lambda j, k: (k, 0)), n, pl.BlockSpec((bk, kin), lambda j, k: (k, j)), TN)],
        [], [(jax.ShapeDtypeStruct((N_CHIPS, nout, kin), BF16), pl.BlockSpec((None, nout, kin), lambda j, k: (j, 0, 0)))],
        _ident, (nout, kin))[0]


def _rope_tables(positions):
    half = ROPE_DIM // 2
    freqs = 1.0 / (ROPE_BASE ** (jnp.arange(0, ROPE_DIM, 2, dtype=F32) / ROPE_DIM))
    ang = positions.astype(F32)[:, None] * freqs
    cos, sin = jnp.cos(ang), jnp.sin(ang)
    z = jnp.zeros_like(cos)
    tc = jnp.concatenate([cos, cos, z, z], axis=-1)
    ta = jnp.concatenate([-sin, z, z, z], axis=-1)
    tb = jnp.concatenate([z, sin, z, z], axis=-1)
    assert tc.shape[-1] == 4 * half
    return tc, ta, tb


def _rope(x, tc, ta, tb):
    return x * tc + pltpu.roll(x, 96, 1) * ta + pltpu.roll(x, 32, 1) * tb


def _rope_t(dy, tc, ta, tb):
    return dy * tc + pltpu.roll(dy * ta, 32, 1) + pltpu.roll(dy * tb, 96, 1)


def _norm_bf16(x, g):
    r = lax.rsqrt(jnp.mean(x * x, axis=-1, keepdims=True) + RMS_EPS)
    return ((x * r) * g).astype(BF16)


def _qkv_prep(z, q_gain, kv_gain, wq_t, wkv, tc, ta, tb):
    s = z.shape[0]
    bm = _row_block(s, 512)

    def body(zq_ref, zkv_ref, zkr_ref, qg_ref, kvg_ref, wq_ref, wkv_ref, tc_ref, ta_ref, tb_ref,
             qn_ref, kvn_ref, q_ref, k_ref, v_ref):
        c, a, b = tc_ref[...], ta_ref[...], tb_ref[...]
        qn = _norm_bf16(zq_ref[...], qg_ref[...])
        kvn = _norm_bf16(zkv_ref[...], kvg_ref[...])
        qn_ref[...] = qn
        kvn_ref[...] = kvn
        kpe = _rope(zkr_ref[...], c, a, b).astype(BF16)
        for h in range(MLA_HEADS):
            lo = h * HEAD_QK
            qp = _dot(qn, wq_ref[h], NT)
            q_ref[:, lo:lo + 128] = qp[:, :128].astype(BF16)
            q_ref[:, lo + 128:lo + 256] = _rope(qp[:, 128:], c, a, b).astype(BF16)
            kv = _dot(kvn, wkv_ref[h], NN)
            k_ref[:, lo:lo + 128] = kv[:, :128].astype(BF16)
            k_ref[:, lo + 128:lo + 256] = kpe
            v_ref[:, h * HEAD_V:(h + 1) * HEAD_V] = kv[:, 128:].astype(BF16)

    def cols(width, blk):
        return pl.BlockSpec((bm, width), lambda i: (i, blk))

    def whole(a):
        return pl.BlockSpec(a.shape, lambda i: (0,) * a.ndim)

    tab = cols(128, 0)
    return _call(
        body, name="qkv_prep", grid=(s // bm,),
        in_specs=[cols(Q_LORA, 0), cols(KV_LORA, 2), cols(128, 3), whole(q_gain), whole(kv_gain), whole(wq_t),
                  whole(wkv), tab, tab, tab],
        out_specs=[cols(Q_LORA, 0), cols(KV_LORA, 0), cols(1024, 0), cols(1024, 0), cols(512, 0)],
        out_shape=[jax.ShapeDtypeStruct((s, Q_LORA), BF16), jax.ShapeDtypeStruct((s, KV_LORA), BF16),
                   jax.ShapeDtypeStruct((s, 1024), BF16), jax.ShapeDtypeStruct((s, 1024), BF16),
                   jax.ShapeDtypeStruct((s, 512), BF16)],
        sem=("parallel",), args=[z, z, z, q_gain, kv_gain, wq_t, wkv, tc, ta, tb])


def _qkv_prep_bwd(dq, dk, dv, z, qn, kvn, q_gain, kv_gain, wq_t, wkv, tc, ta, tb):
    s = z.shape[0]
    bm = _row_block(s, 512)
    nsteps = s // bm

    def body(dq_ref, dk_ref, dv_ref, zq_ref, zkv_ref, qn_ref, kvn_ref, qg_ref, kvg_ref, wq_ref, wkv_ref,
             tc_ref, ta_ref, tb_ref, dz_ref, dqg_ref, dkvg_ref, dwq_ref, dwkv_ref, wq_acc, wkv_acc):
        i = pl.program_id(0)
        c, a, b = tc_ref[...], ta_ref[...], tb_ref[...]

        @pl.when(i == 0)
        def _():
            wq_acc[...] = jnp.zeros_like(wq_acc)
            wkv_acc[...] = jnp.zeros_like(wkv_acc)

        qn, kvn = qn_ref[...], kvn_ref[...]
        dqn = jnp.zeros((bm, Q_LORA), F32)
        dkvn = jnp.zeros((bm, KV_LORA), F32)
        dpe = jnp.zeros((bm, 128), F32)
        for h in range(MLA_HEADS):
            lo = h * HEAD_QK
            dqp = jnp.concatenate([dq_ref[:, lo:lo + 128].astype(BF16),
                                   _rope_t(dq_ref[:, lo + 128:lo + 256], c, a, b).astype(BF16)], axis=1)
            dqn = dqn + _dot(dqp, wq_ref[h], NN)
            wq_acc[h] += _dot(dqp, qn, TN)
            dkv = jnp.concatenate([dk_ref[:, lo:lo + 128].astype(BF16),
                                   dv_ref[:, h * HEAD_V:(h + 1) * HEAD_V].astype(BF16)], axis=1)
            dkvn = dkvn + _dot(dkv, wkv_ref[h], NT)
            wkv_acc[h] += _dot(kvn, dkv, TN)
            dpe = dpe + dk_ref[:, lo + 128:lo + 256]
        dcq, dqg = _rms_bwd_math(dqn, zq_ref[...], qg_ref[...], Q_LORA)
        dckv, dkvg = _rms_bwd_math(dkvn, zkv_ref[...], kvg_ref[...], KV_LORA)
        dz_ref[:, 0:Q_LORA] = dcq.astype(BF16)
        dz_ref[:, Q_LORA:Q_LORA + KV_LORA] = dckv.astype(BF16)
        dz_ref[:, Q_LORA + KV_LORA:512] = _rope_t(dpe, c, a, b).astype(BF16)

        @pl.when(i == 0)
        def _():
            dqg_ref[...] = dqg
            dkvg_ref[...] = dkvg

        @pl.when(i > 0)
        def _():
            dqg_ref[...] += dqg
            dkvg_ref[...] += dkvg

        @pl.when(i == nsteps - 1)
        def _():
            dwq_ref[...] = wq_acc[...].astype(BF16)
            dwkv_ref[...] = wkv_acc[...].astype(BF16)

    def cols(width, blk):
        return pl.BlockSpec((bm, width), lambda i: (i, blk))

    def whole(shape):
        return pl.BlockSpec(shape, lambda i: (0,) * len(shape))

    tab = cols(128, 0)
    return _call(
        body, name="qkv_prep_bwd", grid=(nsteps,),
        in_specs=[cols(1024, 0), cols(1024, 0), cols(512, 0), cols(Q_LORA, 0), cols(KV_LORA, 2), cols(Q_LORA, 0),
                  cols(KV_LORA, 0), whole(q_gain.shape), whole(kv_gain.shape), whole(wq_t.shape), whole(wkv.shape),
                  tab, tab, tab],
        out_specs=[cols(512, 0), whole(q_gain.shape), whole(kv_gain.shape), whole(wq_t.shape), whole(wkv.shape)],
        out_shape=[jax.ShapeDtypeStruct((s, 512), BF16), jax.ShapeDtypeStruct(q_gain.shape, F32),
                   jax.ShapeDtypeStruct(kv_gain.shape, F32), jax.ShapeDtypeStruct(wq_t.shape, BF16),
                   jax.ShapeDtypeStruct(wkv.shape, BF16)],
        scratch_shapes=[pltpu.VMEM(wq_t.shape, F32), pltpu.VMEM(wkv.shape, F32)],
        sem=("arbitrary",), args=[dq, dk, dv, z, z, qn, kvn, q_gain, kv_gain, wq_t, wkv, tc, ta, tb])


def _causal_mask(s, row0, col0):
    rows = row0 + lax.broadcasted_iota(jnp.int32, s.shape, 0)
    cols = col0 + lax.broadcasted_iota(jnp.int32, s.shape, 1)
    return jnp.where(cols <= rows, s, -jnp.inf)


def _attn_fwd(name, q, k, k_off, v, v_off, nh, dq, dv, scale, causal, blk):
    sq, sk = q.shape[0], k.shape[0]
    bq = min(blk, sq)
    bk = min(blk, sk)
    nkv = sk // bk
    assert not causal or (sq == sk and bq == bk)

    hq = bq
    log2e = 1.4426950408889634
    c2 = scale * log2e

    def body(q_ref, k_ref, v_ref, o_ref, lse_ref):
        qi = pl.program_id(1)
        qs = (q_ref[...],)

        def step(j, carry, masked):
            rows = pl.ds(pl.multiple_of(j * bk, bk), bk)
            kb, vb = k_ref[rows, :], v_ref[rows, :]
            out = []
            for t, (m, l, acc) in enumerate(carry):
                s = _dot(qs[t], kb, NT) * c2
                if masked:
                    s = _causal_mask(s, qi * bq + t * hq, j * bk)
                m_new = jnp.maximum(m, jnp.max(s, axis=-1, keepdims=True))
                alpha = jnp.exp2(m - m_new)
                p = jnp.exp2(s - m_new)
                l = alpha * l + jnp.sum(p, axis=-1, keepdims=True)
                acc = alpha * acc + _dot(p, vb, NN)
                out.append((m_new, l, acc))
            return tuple(out)

        one = (jnp.full((hq, 1), -jnp.inf, F32), jnp.zeros((hq, 1), F32), jnp.zeros((hq, dv), F32))
        init = (one,)
        if causal:
            carry = lax.fori_loop(0, qi, lambda j, c: step(j, c, False), init)
            fin = step(qi, carry, True)
        else:
            fin = lax.fori_loop(0, nkv, lambda j, c: step(j, c, False), init)
        for t, (m, l, acc) in enumerate(fin):
            o_ref[t * hq:(t + 1) * hq, :] = (acc / l).astype(o_ref.dtype)
            lse_ref[t * hq:(t + 1) * hq, :] = m * (1.0 / log2e) + jnp.log(l)

    return _call(
        body, name=name, grid=(nh, sq // bq),
        in_specs=[pl.BlockSpec((bq, dq), lambda h, i: (i, h)),
                  pl.BlockSpec((sk, dq), lambda h, i: (0, k_off + h)),
                  pl.BlockSpec((sk, dv), lambda h, i: (0, v_off + h))],
        out_specs=[pl.BlockSpec((bq, dv), lambda h, i: (i, h)), pl.BlockSpec((None, bq, 1), lambda h, i: (h, i, 0))],
        out_shape=[jax.ShapeDtypeStruct((sq, nh * dv), BF16), jax.ShapeDtypeStruct((nh, sq, 1), F32)],
        sem=("parallel", "parallel"), args=[q, k, v])


def _attn_bwd(name, q, k, k_off, v, v_off, do, do_off, lse, delta, nh, dq, dv, scale, causal, blk):
    sq, sk = q.shape[0], k.shape[0]
    bq = min(blk, sq)
    bk = min(blk, sk)
    nq = sq // bq
    assert not causal or (sq == sk and bq == bk)

    def body(q_ref, k_ref, v_ref, do_ref, lse_ref, dl_ref, dq_ref, dk_ref, dv_ref, dk_acc, dv_acc):
        j = pl.program_id(1)

        @pl.when(j == 0)
        def _():
            dq_ref[...] = jnp.zeros_like(dq_ref)

        dk_acc[...] = jnp.zeros_like(dk_acc)
        dv_acc[...] = jnp.zeros_like(dv_acc)
        kv = k_ref[...]
        vv = v_ref[...]

        def step(i, masked):
            rows = pl.ds(pl.multiple_of(i * bq, bq), bq)
            qv = q_ref[rows, :]
            dov = do_ref[rows, :].astype(BF16)
            s = _dot(qv, kv, NT) * scale
            if masked:
                s = _causal_mask(s, i * bq, j * bk)
            p = jnp.exp(s - lse_ref[rows, :])
            dp = _dot(dov, vv, NT)
            ds = (p * (dp - dl_ref[rows, :]) * scale).astype(BF16)
            dv_acc[...] += _dot(p, dov, TN)
            dk_acc[...] += _dot(ds, qv, TN)
            dq_ref[rows, :] += _dot(ds, kv, NN)

        if causal:
            step(j, True)

            def loop(i, c):
                step(i, False)
                return c

            lax.fori_loop(j + 1, nq, loop, 0)
        else:
            def loop(i, c):
                step(i, False)
                return c

            lax.fori_loop(0, nq, loop, 0)
        dk_ref[...] = dk_acc[...]
        dv_ref[...] = dv_acc[...]

    stat = pl.BlockSpec((None, sq, 1), lambda h, j: (h, 0, 0))
    return _call(
        body, name=name, grid=(nh, sk // bk),
        in_specs=[pl.BlockSpec((sq, dq), lambda h, j: (0, h)),
                  pl.BlockSpec((bk, dq), lambda h, j: (j, k_off + h)),
                  pl.BlockSpec((bk, dv), lambda h, j: (j, v_off + h)),
                  pl.BlockSpec((sq, dv), lambda h, j: (0, do_off + h)), stat, stat],
        out_specs=[pl.BlockSpec((sq, dq), lambda h, j: (0, h)),
                   pl.BlockSpec((bk, dq), lambda h, j: (j, h)),
                   pl.BlockSpec((bk, dv), lambda h, j: (j, h))],
        out_shape=[jax.ShapeDtypeStruct((sq, nh * dq), F32), jax.ShapeDtypeStruct((sk, nh * dq), F32),
                   jax.ShapeDtypeStruct((sk, nh * dv), F32)],
        scratch_shapes=[pltpu.VMEM((bk, dq), F32), pltpu.VMEM((bk, dv), F32)],
        sem=("parallel", "arbitrary"), args=[q, k, v, do, lse, delta])


def _pool_diff(z, g):
    s = z.shape[0]
    t = lax.broadcasted_iota(jnp.int32, z.shape, 0)
    acc = z
    sums = []
    for k in (1, 2, 4, 8):
        acc = acc + jnp.where(t >= k, pltpu.roll(acc, k, 0), 0.0)
        sums.append(acc)
    win = jnp.where(g == 0, sums[0], jnp.where(g == 1, sums[1], jnp.where(g == 2, sums[2], sums[3])))
    w = lax.shift_left(jnp.int32(2), g)
    count = jnp.minimum(t + 1, w).astype(F32)
    del s
    return win / count - z, count


def _pool_fwd(z, pool_w, pool_scale):
    s = z.shape[0]

    def body(z_ref, w_ref, sc_ref, o_ref):
        diff, _ = _pool_diff(z_ref[...], pl.program_id(0))
        o_ref[...] = (_dot(diff, w_ref[...], NN) * sc_ref[...]).astype(o_ref.dtype)

    return _call(
        body, name="pool_fwd", grid=(POOL_GROUPS,),
        in_specs=[pl.BlockSpec((s, POOL_CH), lambda g: (0, 4 + g)),
                  pl.BlockSpec((None, POOL_CH, POOL_CH), lambda g: (g, 0, 0)),
                  pl.BlockSpec((1, POOL_CH), lambda g: (0, g))],
        out_specs=[pl.BlockSpec((s, POOL_CH), lambda g: (0, g))],
        out_shape=[jax.ShapeDtypeStruct((s, POOL_GROUPS * POOL_CH), BF16)],
        sem=("parallel",), args=[z, pool_w, pool_scale])[0]


def _pool_bwd(dcat, z, pool_w, pool_scale):
    s = z.shape[0]

    def body(dp_ref, z_ref, w_ref, sc_ref, dz_ref, dw_ref, dsc_ref):
        g = pl.program_id(0)
        diff, count = _pool_diff(z_ref[...], g)
        dpf = dp_ref[...].astype(F32)
        u = _dot(diff, w_ref[...], NN)
        dsc_ref[...] = jnp.sum(dpf * u, axis=0, keepdims=True)
        du = (dpf * sc_ref[...]).astype(BF16)
        dw_ref[...] = _dot(diff, du, TN)
        ddiff = _dot(du, w_ref[...], NT)
        t = lax.broadcasted_iota(jnp.int32, ddiff.shape, 0)
        acc = ddiff / count
        sums = []
        for k in (1, 2, 4, 8):
            acc = acc + jnp.where(t < s - k, pltpu.roll(acc, s - k, 0), 0.0)
            sums.append(acc)
        win = jnp.where(g == 0, sums[0], jnp.where(g == 1, sums[1], jnp.where(g == 2, sums[2], sums[3])))
        dz_ref[...] = win - ddiff

    return pl.pallas_call(
        body, name="pool_bwd", grid=(POOL_GROUPS,),
        in_specs=[pl.BlockSpec((s, POOL_CH), lambda g: (0, 4 + g)),
                  pl.BlockSpec((s, POOL_CH), lambda g: (0, 4 + g)),
                  pl.BlockSpec((None, POOL_CH, POOL_CH), lambda g: (g, 0, 0)),
                  pl.BlockSpec((1, POOL_CH), lambda g: (0, g))],
        out_specs=[pl.BlockSpec((s, POOL_CH), lambda g: (0, g)),
                   pl.BlockSpec((None, POOL_CH, POOL_CH), lambda g: (g, 0, 0)),
                   pl.BlockSpec((1, POOL_CH), lambda g: (0, g))],
        out_shape=[jax.ShapeDtypeStruct((s, POOL_GROUPS * POOL_CH), F32),
                   jax.ShapeDtypeStruct((POOL_GROUPS, POOL_CH, POOL_CH), F32),
                   jax.ShapeDtypeStruct((1, POOL_GROUPS * POOL_CH), F32)],
        compiler_params=_params("parallel"),
    )(dcat, z, pool_w, pool_scale)


def _local_step(x, mem, positions, target, w, grads):
    tc, ta, tb = _rope_tables(positions)
    blk = _ATT_BLOCK

    if "ffn1_shards" in w:
        n1, a1, dadu1, dadg1, w["ffn1_w_gate"], w["ffn1_w_up"], w["ffn1_w_down"] = _ffn1_up_gather(
            x, w["ffn1_norm"], *w["ffn1_shards"])
    else:
        n1 = _rmsnorm_fwd("ffn1_norm", x, w["ffn1_norm"], D_MODEL)
        a1, dadu1, dadg1 = _ffn_up("ffn1_up", n1, w["ffn1_w_gate"], w["ffn1_w_up"])
    h1, n2 = _ffn_down("ffn1_down", a1, w["ffn1_w_down"], x, w["mix_norm"])
    z = _w_in_fwd(n2, w["w_in"])
    qn, kvn, qf, kf, vf = _qkv_prep(z, w["q_norm"], w["kv_norm"], w["w_q_up"], w["w_kv_up"], tc, ta, tb)
    att, lse = _attn_fwd("mla_fwd", qf, kf, 0, vf, 0, MLA_HEADS, HEAD_QK, HEAD_V, MLA_SCALE, True, blk)
    pool = _pool_fwd(z, w["pool_w"], w["pool_scale"])
    s = x.shape[0]
    bm = _row_block(s)
    row = pl.BlockSpec((bm, D_MODEL), lambda i, k: (i, 0))
    half = pl.BlockSpec((bm, 512), lambda i, k: (i, 0))
    h2, n3 = _matmul(
        "w_out", (s // bm, 1),
        [(att, half, w["w_out"], pl.BlockSpec((512, D_MODEL), lambda i, k: (0, 0)), NN),
         (pool, half, w["w_out"], pl.BlockSpec((512, D_MODEL), lambda i, k: (1, 0)), NN)],
        [(h1, row)] + _residual_outs(s, bm, w["xattn_norm"])[0], _residual_outs(s, bm, w["xattn_norm"])[1],
        _residual_epilogue(1.0, True), None)
    memn = _rmsnorm_fwd("mem_norm", mem, w["mem_norm"], D_MODEL)
    qm = _mm_nn("w_mq", n3, w["w_mq"], BF16)
    kvm = _mm_heads_fwd("w_mkv", memn, w["w_mkv"], BF16)
    om, lse_m = _attn_fwd("xattn_fwd", qm, kvm, 0, kvm, MEM_HEADS, MEM_HEADS, MEM_HEAD_DIM, MEM_HEAD_DIM,
                          MEM_SCALE, False, blk)
    h3, n4 = _mm_nn("w_mo", om, w["w_mo"], F32, res=h2, gain=w["ffn2_norm"])
    a2, dadu2, dadg2 = _ffn_up("ffn2_up", n4, w["ffn2_w_gate"], w["ffn2_w_up"])
    dh4, dh4b, loss_vec, d_final = _ffn_down("ffn2_down", a2, w["ffn2_w_down"], h3, loss=(w["final_norm"], target))
    grads["final_norm"] = d_final

    dh3, dh3b, grads["ffn2_norm"] = _ffn_bwd("ffn2", dh4b, n4, dadg2, dadu2, a2, w["ffn2_w_gate"], w["ffn2_w_up"],
                                             w["ffn2_w_down"], grads, norm_bwd=(h3, w["ffn2_norm"], dh4))

    dom, delta_m = _mm_nt("w_mo_dx", dh3b, w["w_mo"], BF16, attn_out=om, nh=MEM_HEADS, dv=MEM_HEAD_DIM)
    grads["w_mo"] = _mm_tn("w_mo_dw", om, dh3b)
    dqm, dkm, dvm = _attn_bwd("xattn_bwd", qm, kvm, 0, kvm, MEM_HEADS, dom, 0, lse_m, delta_m, MEM_HEADS,
                              MEM_HEAD_DIM, MEM_HEAD_DIM, MEM_SCALE, False, blk)
    dkvm = jnp.concatenate([dkm, dvm], axis=1).astype(BF16)
    dh2, dh2b, grads["xattn_norm"] = _mm_nt_norm_bwd("w_mq_dx", dqm, w["w_mq"], h2, w["xattn_norm"], dh3)
    grads["w_mq"] = _mm_tn("w_mq_dw", n3, dqm)
    dmemn, grads["w_mkv"] = _mm_heads_bwd("w_mkv", dkvm, memn, w["w_mkv"])
    _, grads["mem_norm"] = _rmsnorm_bwd("mem_norm_bwd", dmemn, mem, w["mem_norm"], D_MODEL, out_dtype=BF16)

    dcat, delta = _mm_nt("w_out_dx", dh2b, w["w_out"], BF16, attn_out=att, nh=MLA_HEADS, dv=HEAD_V)
    grads["w_out"] = jnp.concatenate([_mm_tn("w_out_dw_a", att, dh2b), _mm_tn("w_out_dw_p", pool, dh2b)], axis=0)
    dzp, grads["pool_w"], grads["pool_scale"] = _pool_bwd(dcat, z, w["pool_w"], w["pool_scale"])
    dqf, dkf, dvf = _attn_bwd("mla_bwd", qf, kf, 0, vf, 0, dcat, 0, lse, delta, MLA_HEADS, HEAD_QK, HEAD_V,
                              MLA_SCALE, True, blk)
    dz_lat, grads["q_norm"], grads["kv_norm"], grads["w_q_up"], grads["w_kv_up"] = _qkv_prep_bwd(
        dqf, dkf, dvf, z, qn, kvn, w["q_norm"], w["kv_norm"], w["w_q_up"], w["w_kv_up"], tc, ta, tb)
    dz = jnp.concatenate([dz_lat, dzp.astype(BF16)], axis=1)
    grads["w_in"] = _w_in_dw(dz, n2)
    dh1, dh1b, grads["mix_norm"] = _w_in_dx_norm_bwd(dz, w["w_in"], h1, w["mix_norm"], dh2)

    dn1 = _ffn_bwd("ffn1", dh1b, n1, dadg1, dadu1, a1, w["ffn1_w_gate"], w["ffn1_w_up"], w["ffn1_w_down"], grads)
    dx, grads["ffn1_norm"], _ = _rmsnorm_bwd("ffn1_norm_bwd", dn1, x, w["ffn1_norm"], D_MODEL, dres=dh1)
    return loss_vec[0, 0], dx


def _mesh_pos():
    x, y, c = lax.axis_index("x"), lax.axis_index("y"), lax.axis_index("c")
    chips = [(1 - x, y), (x, 1 - y), (1 - x, 1 - y)]
    chip_ids = [2 * cx + cy for cx, cy in chips]
    return x, y, c, 2 * x + y, chips, chip_ids


def _half_rows(c, rows):
    hr = rows // 2
    return pl.ds(pl.multiple_of(c * hr, 16), hr), pl.ds(pl.multiple_of((1 - c) * hr, 16), hr)


def _ag_ici_stage(shards):
    n = len(shards)

    def copies(ins, outs):
        x, y, c, me, chips, _ = _mesh_pos()
        out = []
        for k in range(n):
            mine, _ = _half_rows(c, ins[k].shape[0])
            out.append((ins[k], outs[k].at[me], None))
            for cx, cy in chips:
                out.append((ins[k].at[mine], outs[k].at[me, mine], (cx, cy, c)))
        return out

    return _Stage(shards, [jax.ShapeDtypeStruct((N_CHIPS,) + s.shape, s.dtype) for s in shards], 3 * n, n, copies)


def _quarter_rows(c, rows):
    qr = rows // 4
    return pl.ds(pl.multiple_of(c * 2 * qr, 16), qr), pl.ds(pl.multiple_of(c * 2 * qr + qr, 16), qr)


def _ag_d2d_stage(fulls):
    n = len(fulls)

    def copies(ins, outs):
        x, y, c, me, _, chip_ids = _mesh_pos()
        out = []
        for k in range(n):
            mine, _ = _half_rows(c, ins[k].shape[1])
            for j in range(3):
                out.append((ins[k].at[chip_ids[j], mine], outs[k].at[chip_ids[j], mine], (x, y, 1 - c)))
        return out

    return _Stage(fulls, [jax.ShapeDtypeStruct(f.shape, f.dtype) for f in fulls], 3 * n, 0, copies,
                  aliases={k: k for k in range(n)})


def _rs_swap_stage(grads):
    n = len(grads)

    def copies(ins, outs):
        x, y, c, _, _, _ = _mesh_pos()
        out = []
        for k in range(n):
            _, other = _half_rows(c, ins[k].shape[1])
            out.append((ins[k].at[:, other, :], outs[k], (x, y, 1 - c)))
        return out

    return _Stage(grads, [jax.ShapeDtypeStruct((N_CHIPS, g.shape[1] // 2, g.shape[2]), g.dtype) for g in grads],
                  n, 0, copies)


_REL_OF_PEER = (2, 1, 3)


def _rs_scatter_stage(sums, relative=False):
    n = len(sums)

    def copies(ins, outs):
        x, y, c, me, chips, chip_ids = _mesh_pos()
        out = []
        for k in range(n):
            mine, _ = _half_rows(c, 2 * ins[k].shape[1])
            out.append((ins[k].at[0 if relative else me], outs[k].at[0, mine, :], None))
            for j, (cx, cy) in enumerate(chips):
                slab = _REL_OF_PEER[j] if relative else chip_ids[j]
                out.append((ins[k].at[slab], outs[k].at[1 + j, mine, :], (cx, cy, c)))
        return out

    return _Stage(sums, [jax.ShapeDtypeStruct((N_CHIPS, 2 * s.shape[1], s.shape[2]), s.dtype) for s in sums],
                  3 * n, n, copies)


def _rs_mirror_stage(parts):
    n = len(parts)

    def copies(ins, outs):
        x, y, c, _, _, _ = _mesh_pos()
        out = []
        for k in range(n):
            mine, _ = _half_rows(c, ins[k].shape[1])
            out.append((ins[k].at[:, mine, :], outs[k].at[:, mine, :], (x, y, 1 - c)))
        return out

    return _Stage(parts, [jax.ShapeDtypeStruct(p.shape, p.dtype) for p in parts], n, 0, copies,
                  aliases={k: k for k in range(n)})


def _pair_add(name, gs, r1s, core):
    n = len(gs)

    def body(c_ref, *refs):
        for k in range(n):
            g_ref, r_ref, o_ref = refs[k], refs[n + k], refs[2 * n + k]
            o_ref[...] = (g_ref[...].astype(F32) + r_ref[...].astype(F32)).astype(BF16)

    def half(g):
        return pl.BlockSpec((None, g.shape[1] // 2, g.shape[2]), lambda j, c: (j, 0, 0))

    def mine(g):
        return pl.BlockSpec((None, g.shape[1] // 2, g.shape[2]), lambda j, c: (j, c[0], 0))

    return pl.pallas_call(
        body, name=name,
        grid_spec=pltpu.PrefetchScalarGridSpec(
            num_scalar_prefetch=1, grid=(N_CHIPS,),
            in_specs=[mine(g) for g in gs] + [half(g) for g in gs], out_specs=[half(g) for g in gs]),
        out_shape=[jax.ShapeDtypeStruct((N_CHIPS, g.shape[1] // 2, g.shape[2]), BF16) for g in gs],
        compiler_params=_params("parallel"),
    )(core, *gs, *r1s)


def _adamw_math(w, g, m, v):
    m = ADAM_B1 * m + (1.0 - ADAM_B1) * g
    v = ADAM_B2 * v + (1.0 - ADAM_B2) * (g * g)
    m_hat = m / (1.0 - ADAM_B1 ** ADAM_STEP)
    v_hat = v / (1.0 - ADAM_B2 ** ADAM_STEP)
    delta = -ADAM_LR * (m_hat / (jnp.sqrt(v_hat) + ADAM_EPS) + ADAM_WD * w)
    return delta, m, v


def _adamw_sum(name, w, parts, m, v):
    r, c = w.shape
    br = r
    while br * c * 4 > (1 << 20) and br % 32 == 0:
        br //= 2

    def body(w_ref, p_ref, m_ref, v_ref, g_ref, d_ref, nm_ref, nv_ref):
        g = p_ref[0].astype(F32)
        for j in range(1, N_CHIPS):
            g = g + p_ref[j].astype(F32)
        d, nm, nv = _adamw_math(w_ref[...], g, m_ref[...], v_ref[...])
        g_ref[...] = g
        d_ref[...] = d
        nm_ref[...] = nm
        nv_ref[...] = nv

    spec = pl.BlockSpec((br, c), lambda i: (i, 0))
    shp = jax.ShapeDtypeStruct((r, c), F32)
    return _call(
        body, name=name, grid=(r // br,),
        in_specs=[spec, pl.BlockSpec((N_CHIPS, br, c), lambda i: (0, i, 0)), spec, spec],
        out_specs=[spec] * 4, out_shape=[shp] * 4, sem=("parallel",), args=[w, parts, m, v])


_SMALL_VECTORS = ("ffn1_norm", "mix_norm", "xattn_norm", "mem_norm", "ffn2_norm", "final_norm", "q_norm",
                  "kv_norm", "pool_scale")


_LOSS_ROW = 9
_VEC_ROWS = 16
_POOL_ROWS = POOL_GROUPS * POOL_CH


def _small_params_step(g, w, m, v, loss_local):
    names = list(_SMALL_VECTORS) + ["pool_w"]
    nv = len(_SMALL_VECTORS)
    widths = [g[n].shape[1] for n in _SMALL_VECTORS]
    shapes = {"vec": (_VEC_ROWS, D_MODEL), "pool": (_POOL_ROWS, POOL_CH)}

    def body(*refs):
        ins = refs[:4 * (nv + 1) + 1]
        outs = refs[len(ins):len(ins) + 4 * (nv + 1) + 1]
        vec_own, vec_sib, vec_all, pool_sib, pool_sum, pool_all, send, recv = refs[len(ins) + len(outs):]
        g_in, w_in, m_in, v_in = (ins[k * (nv + 1):(k + 1) * (nv + 1)] for k in range(4))
        loss_in = ins[-1]
        g_out, d_out, m_out, v_out = (outs[k * (nv + 1):(k + 1) * (nv + 1)] for k in range(4))
        loss_out = outs[-1]
        x, y, c, me, chips, chip_ids = _mesh_pos()
        sib = (x, y, 1 - c)

        def remote(src, dst, k, dev):
            return pltpu.make_async_remote_copy(src_ref=src, dst_ref=dst, send_sem=send.at[k], recv_sem=recv.at[k],
                                                device_id=dev, device_id_type=_MESH)

        vec_own[...] = jnp.zeros_like(vec_own)
        for i in range(nv):
            vec_own[i:i + 1, 0:widths[i]] = g_in[i][...]
        vec_own[_LOSS_ROW:_LOSS_ROW + 1, 0:128] = loss_in[...]
        swaps = [remote(vec_own, vec_sib, 0, sib), remote(g_in[nv], pool_sib, 1, sib)]
        for cp in swaps:
            cp.start()
        for cp in swaps:
            cp.wait()
        vec_all[me] = vec_own[...] + vec_sib[...]
        pool_sum[...] = g_in[nv][...] + pool_sib[...]
        pool_all[me] = pool_sum[...]
        hv, hp = _VEC_ROWS // 2, _POOL_ROWS // 2
        mine_v = pl.ds(pl.multiple_of(c * hv, 8), hv)
        mine_p = pl.ds(pl.multiple_of(c * hp, 8), hp)
        sends = []
        for j, (cx, cy) in enumerate(chips):
            sends.append(remote(vec_all.at[me, mine_v], vec_all.at[me, mine_v], 2 + j, (cx, cy, c)))
            sends.append(remote(pool_sum.at[mine_p], pool_all.at[me, mine_p], 5 + j, (cx, cy, c)))
        for cp in sends:
            cp.start()
        for cp in sends:
            cp.wait()
        mirrors = []
        for j in range(3):
            mirrors.append(remote(vec_all.at[chip_ids[j], mine_v], vec_all.at[chip_ids[j], mine_v], 8 + j, sib))
            mirrors.append(remote(pool_all.at[chip_ids[j], mine_p], pool_all.at[chip_ids[j], mine_p], 11 + j, sib))
        for cp in mirrors:
            cp.start()
        for cp in mirrors:
            cp.wait()
        vec_tot = vec_all[0]
        pool_tot = pool_all[0]
        for i in range(1, N_CHIPS):
            vec_tot = vec_tot + vec_all[i]
            pool_tot = pool_tot + pool_all[i]
        vec_sib[...] = vec_tot
        loss_out[...] = vec_sib[_LOSS_ROW:_LOSS_ROW + 1, 0:128]
        for i in range(nv + 1):
            gi = pool_tot if i == nv else vec_sib[i:i + 1, 0:widths[i]]
            d, nm, nvv = _adamw_math(w_in[i][...], gi, m_in[i][...], v_in[i][...])
            g_out[i][...] = gi
            d_out[i][...] = d
            m_out[i][...] = nm
            v_out[i][...] = nvv

    vm = pl.BlockSpec(memory_space=pltpu.VMEM)
    args = [d[n] for d in (g, w, m, v) for n in names] + [jnp.broadcast_to(loss_local.reshape(1, 1), (1, 128))]
    out_shape = [jax.ShapeDtypeStruct(g[n].shape, F32) for _ in range(4) for n in names]
    out_shape.append(jax.ShapeDtypeStruct((1, 128), F32))
    res = pl.pallas_call(
        body, name="small_params_step", in_specs=[vm] * len(args), out_specs=[vm] * len(out_shape),
        out_shape=out_shape,
        scratch_shapes=[pltpu.VMEM(shapes["vec"], F32), pltpu.VMEM(shapes["vec"], F32),
                        pltpu.VMEM((N_CHIPS,) + shapes["vec"], F32), pltpu.VMEM(shapes["pool"], F32),
                        pltpu.VMEM(shapes["pool"], F32), pltpu.VMEM((N_CHIPS,) + shapes["pool"], F32),
                        pltpu.SemaphoreType.DMA((14,)), pltpu.SemaphoreType.DMA((14,))],
        compiler_params=pltpu.CompilerParams(vmem_limit_bytes=V7X_VMEM_LIMIT_BYTES),
    )(*args)
    k = len(names)
    dicts = [dict(zip(names, res[i * k:(i + 1) * k])) for i in range(4)]
    return dicts[0], dicts[1], dicts[2], dicts[3], res[-1]


_WEIGHTS = ("ffn1_norm", "ffn1_w_gate", "ffn1_w_up", "ffn1_w_down", "mix_norm", "w_in", "q_norm", "w_q_up",
            "kv_norm", "w_kv_up", "pool_w", "pool_scale", "w_out", "xattn_norm", "mem_norm", "w_mq", "w_mkv",
            "w_mo", "ffn2_norm", "ffn2_w_gate", "ffn2_w_up", "ffn2_w_down", "final_norm")
_SHARDED = ("ffn1_w_gate", "ffn1_w_up", "ffn1_w_down", "w_in", "w_q_up", "w_kv_up", "w_out", "w_mq", "w_mkv",
            "w_mo", "ffn2_w_gate", "ffn2_w_up", "ffn2_w_down")
W_IN_SPLIT = Q_LORA + KV_LORA + ROPE_DIM


_FFN1 = ("ffn1_w_gate", "ffn1_w_up", "ffn1_w_down")
_TRANSPOSED = ("ffn1_w_gate", "ffn1_w_up", "ffn2_w_gate", "ffn2_w_up", "w_in", "w_q_up")


def _local_view(name, a):
    return jnp.swapaxes(a, 1, 2)[0] if name in _TRANSPOSED else a[0]


def _global_view(name, a):
    return jnp.swapaxes(a[None], 1, 2) if name in _TRANSPOSED else a[None]


def _pad_shard(name, a):
    if name == "w_in":
        return jnp.concatenate([a[:W_IN_SPLIT], jnp.zeros((64, a.shape[1]), a.dtype), a[W_IN_SPLIT:]], axis=0)
    if name == "w_q_up":
        return jnp.pad(a, ((0, 64), (0, 0)))
    return a


def _unpad_shard(name, a):
    if name == "w_in":
        return jnp.concatenate([a[:, :W_IN_SPLIT], a[:, W_IN_SPLIT + 64:]], axis=1)
    if name == "w_q_up":
        return a[:, :192]
    return a


def _stacked(g):
    return g if g.ndim == 3 else g.reshape(N_CHIPS, g.shape[0] // N_CHIPS, g.shape[1])


class _Plan:
    AG_UNITS = (
        (("w_in", "w_q_up", "w_kv_up"), "ffn1_up"),
        (("w_out",), "w_in"),
        (("w_mq",), "qkv_prep"),
        (("w_mkv", "w_mo", "ffn2_w_gate"), "mla_fwd"),
        (("ffn2_w_up",), "xattn_fwd"),
        (("ffn2_w_down",), "ffn2_up"),
    )
    RS_UNITS = (
        (("ffn2_w_gate", "ffn2_w_up", "ffn2_w_down"), "ffn2_dn_a", "mla_bwd", "qkv_prep_bwd"),
        (("w_mo", "w_mq", "w_mkv"), "w_out_dx", "mla_bwd", "qkv_prep_bwd"),
        (("w_out", "w_q_up", "w_kv_up", "w_in"), "w_in_dx", "ffn1_dact", "ffn1_dwd"),
        (("ffn1_w_down",), "ffn1_dwg", "ffn1_dwu", "ffn1_dn_a"),
        (("ffn1_w_gate",), "ffn1_dwu", "ffn1_dn_a", "ffn1_dn_b"),
        (("ffn1_w_up",), "ffn1_dn_a", "ffn1_dn_b", "adamw_w_kv_up"),
    )
    ADAMW_ORDER = ("w_kv_up", "ffn2_w_gate", "ffn2_w_up", "ffn2_w_down", "w_mo", "w_mq", "w_mkv", "w_out", "w_q_up",
                   "w_in", "ffn1_w_down", "ffn1_w_gate", "ffn1_w_up")

    def __init__(self, shards, w, grads, core):
        self.shards, self.w, self.grads, self.core = shards, w, grads, core
        self.last_slab_step = 0
        self.parts = {}
        self.ag = [None for _ in self.AG_UNITS]
        self.rs = [[None, None, None, None] for _ in self.RS_UNITS]

    def pre(self, name):
        for i, (names, host) in enumerate(self.AG_UNITS):
            if name == host:
                st = _ag_ici_stage([self.shards[n] for n in names])
                st.then = _ag_d2d_stage(st.outs)
                st.start_step = self.last_slab_step if name == "ffn1_up" else 0
                self.ag[i] = _host(name, st)
        for i, (names, h1, h2, h3) in enumerate(self.RS_UNITS):
            if name == h1:
                self.rs[i][0] = _host(name, _rs_swap_stage([_stacked(self.grads[n]) for n in names]))
            if name == h2:
                self.rs[i][2] = _host(name, _rs_scatter_stage(self.rs[i][1], relative=names[0] in _FFN1))
            if name == h3:
                self.rs[i][3] = _host(name, _rs_mirror_stage(self.rs[i][2].results))

    def post(self, name):
        for i, (names, host) in enumerate(self.AG_UNITS):
            if name == host:
                for n, f in zip(names, self.ag[i].results):
                    self.w[n] = _full_weight(n, f)
        for i, (names, h1, h2, h3) in enumerate(self.RS_UNITS):
            if name == h1:
                self.rs[i][1] = list(_pair_add("pair_add_" + names[0], [_stacked(self.grads[n]) for n in names],
                                               self.rs[i][0].results, self.core))
            if name == h3:
                for n, p in zip(names, self.rs[i][3].results):
                    self.parts[n] = p


def _full_weight(name, stacked):
    if name in ("w_out", "w_mq", "w_mo"):
        return stacked.reshape(D_MODEL, D_MODEL)
    return stacked


def kernel(x, mem, positions, ffn1_norm, ffn1_w_gate, ffn1_w_up, ffn1_w_down, mix_norm, w_in, q_norm, w_q_up, kv_norm, w_kv_up, pool_w, pool_scale, w_out, xattn_norm, mem_norm, w_mq, w_mkv, w_mo, ffn2_norm, ffn2_w_gate, ffn2_w_up, ffn2_w_down, final_norm, loss_target, m_ffn1_norm, m_ffn1_w_gate, m_ffn1_w_up, m_ffn1_w_down, m_mix_norm, m_w_in, m_q_norm, m_w_q_up, m_kv_norm, m_w_kv_up, m_pool_w, m_pool_scale, m_w_out, m_xattn_norm, m_mem_norm, m_w_mq, m_w_mkv, m_w_mo, m_ffn2_norm, m_ffn2_w_gate, m_ffn2_w_up, m_ffn2_w_down, m_final_norm, v_ffn1_norm, v_ffn1_w_gate, v_ffn1_w_up, v_ffn1_w_down, v_mix_norm, v_w_in, v_q_norm, v_w_q_up, v_kv_norm, v_w_kv_up, v_pool_w, v_pool_scale, v_w_out, v_xattn_norm, v_mem_norm, v_w_mq, v_w_mkv, v_w_mo, v_ffn2_norm, v_ffn2_w_gate, v_ffn2_w_up, v_ffn2_w_down, v_final_norm):
    wts = dict(zip(_WEIGHTS, (ffn1_norm, ffn1_w_gate, ffn1_w_up, ffn1_w_down, mix_norm, w_in, q_norm, w_q_up, kv_norm, w_kv_up, pool_w, pool_scale, w_out, xattn_norm, mem_norm, w_mq, w_mkv, w_mo, ffn2_norm, ffn2_w_gate, ffn2_w_up, ffn2_w_down, final_norm)))
    mom = dict(zip(_WEIGHTS, (m_ffn1_norm, m_ffn1_w_gate, m_ffn1_w_up, m_ffn1_w_down, m_mix_norm, m_w_in, m_q_norm, m_w_q_up, m_kv_norm, m_w_kv_up, m_pool_w, m_pool_scale, m_w_out, m_xattn_norm, m_mem_norm, m_w_mq, m_w_mkv, m_w_mo, m_ffn2_norm, m_ffn2_w_gate, m_ffn2_w_up, m_ffn2_w_down, m_final_norm)))
    var = dict(zip(_WEIGHTS, (v_ffn1_norm, v_ffn1_w_gate, v_ffn1_w_up, v_ffn1_w_down, v_mix_norm, v_w_in, v_q_norm, v_w_q_up, v_kv_norm, v_w_kv_up, v_pool_w, v_pool_scale, v_w_out, v_xattn_norm, v_mem_norm, v_w_mq, v_w_mkv, v_w_mo, v_ffn2_norm, v_ffn2_w_gate, v_ffn2_w_up, v_ffn2_w_down, v_final_norm)))
    small = [n for n in _WEIGHTS if n not in _SHARDED]

    global _PLAN
    shards = {n: _pad_shard(n, _local_view(n, wts[n])).astype(BF16) for n in _SHARDED}
    w = {n: wts[n].reshape(1, -1) for n in _SMALL_VECTORS}
    w["pool_w"] = pool_w[0].astype(BF16)
    grads = {}
    core = lax.axis_index("c").astype(jnp.int32).reshape(1)
    plan = _Plan(shards, w, grads, core)
    _PLAN = plan
    try:
        w["ffn1_shards"] = tuple(shards[n] for n in _FFN1)

        loss_local, dx = _local_step(x[0], mem[0], positions[0], loss_target[0], w, grads)

        def small_view(d):
            out = {n: d[n].reshape(1, -1) for n in _SMALL_VECTORS}
            out["pool_w"] = d["pool_w"].reshape(_POOL_ROWS, POOL_CH)
            return out

        *small_res, loss_vec = _small_params_step(small_view(grads), small_view(wts), small_view(mom),
                                                  small_view(var), loss_local)
        g_out, d_out, m_out, v_out = ({n: r[n].reshape(wts[n].shape) for n in small} for r in small_res)
        loss = loss_vec[0, 0]

        for n in _Plan.ADAMW_ORDER:
            res = _adamw_sum("adamw_" + n, _local_view(n, wts[n]), _unpad_shard(n, plan.parts[n]),
                             _local_view(n, mom[n]), _local_view(n, var[n]))
            g_out[n], d_out[n], m_out[n], v_out[n] = (_global_view(n, r) for r in res)
    finally:
        _PLAN = None
        _PENDING.clear()

    return (loss, dx[None], *[g_out[n] for n in _WEIGHTS], *[d_out[n] for n in _WEIGHTS],
            *[m_out[n] for n in _WEIGHTS], *[v_out[n] for n in _WEIGHTS])
```

```python
import jax
import jax.numpy as jnp
from jax import lax
from jax.experimental import pallas as pl
from jax.experimental.pallas import tpu as pltpu

F32 = jnp.float32
BF16 = jnp.bfloat16

D_MODEL = 1024
D_FF = 2816
N_CHIPS = 4
FF_SHARD = D_FF // N_CHIPS
MLA_HEADS = 4
Q_LORA = 256
KV_LORA = 128
ROPE_DIM = 64
HEAD_QK = 256
HEAD_V = 128
POOL_GROUPS = 4
POOL_CH = 128
MEM_HEADS = 4
MEM_HEAD_DIM = 256
RMS_EPS = 1e-6
ROPE_BASE = 10000.0
MLA_SCALE = (128 + 64) ** -0.5
MEM_SCALE = MEM_HEAD_DIM ** -0.5

ADAM_LR = 0.001
ADAM_B1 = 0.9
ADAM_B2 = 0.999
ADAM_EPS = 1e-08
ADAM_WD = 0.01
ADAM_STEP = 10

V7X_VMEM_LIMIT_BYTES = 56 * 1024 * 1024

NN = ((1,), (0,))
NT = ((1,), (1,))
TN = ((0,), (0,))


def _params(*sem):
    return pltpu.CompilerParams(dimension_semantics=sem, vmem_limit_bytes=V7X_VMEM_LIMIT_BYTES)


_MESH = pl.DeviceIdType.MESH
_ANY = pl.BlockSpec(memory_space=pl.ANY)


class _Stage:
    def __init__(self, ins, outs, n_remote, n_local, copies, aliases=None):
        self.ins, self.outs, self.n_remote, self.n_local = list(ins), list(outs), n_remote, n_local
        self.copies, self.aliases = copies, dict(aliases or {})
        self.results = None
        self.start_step = 0
        self.then = None

    def descriptors(self, in_refs, out_refs, send, recv, loc):
        ds, ri, li = [], 0, 0
        for src, dst, dev in self.copies(in_refs, out_refs):
            if dev is None:
                ds.append(pltpu.make_async_copy(src, dst, loc.at[li]))
                li += 1
            else:
                ds.append(pltpu.make_async_remote_copy(src_ref=src, dst_ref=dst, send_sem=send.at[ri],
                                                       recv_sem=recv.at[ri], device_id=dev, device_id_type=_MESH))
                ri += 1
        assert ri == self.n_remote and li == self.n_local
        return ds


_PENDING = {}


def _host(name, stage):
    _PENDING.setdefault(name, []).append(stage)
    return stage


_PLAN = None


def _call(body, **kw):
    if _PLAN is not None:
        _PLAN.pre(kw["name"])
    res = _call_hosting(body, **kw)
    if _PLAN is not None:
        _PLAN.post(kw["name"])
    return res


def _call_hosting(body, *, name, grid, in_specs, out_specs, out_shape, sem, args, scratch_shapes=(), aliases=None):
    stages = _PENDING.pop(name, [])
    scratch_shapes = list(scratch_shapes)
    if not stages:
        return pl.pallas_call(body, name=name, grid=grid, in_specs=in_specs, out_specs=out_specs,
                              out_shape=out_shape, scratch_shapes=scratch_shapes,
                              input_output_aliases=dict(aliases or {}), compiler_params=_params(*sem))(*args)
    ni, no, ns = len(in_specs), len(out_shape), len(scratch_shapes)
    c_ins = [a for st in stages for a in st.ins]
    c_outs = [o for st in stages for o in st.outs]
    nci, nco = len(c_ins), len(c_outs)
    aliases, io, oo = dict(aliases or {}), 0, 0
    for st in stages:
        for i, j in st.aliases.items():
            aliases[ni + io + i] = no + oo + j
        io += len(st.ins)
        oo += len(st.outs)
    dma = pltpu.SemaphoreType.DMA
    sems = []
    for st in stages:
        sems += [dma((max(st.n_remote, 1),)), dma((max(st.n_remote, 1),)), dma((max(st.n_local, 1),))]
    followers = [st.then for st in stages if st.then is not None]
    for st in followers:
        sems += [dma((max(st.n_remote, 1),)), dma((max(st.n_remote, 1),)), dma((max(st.n_local, 1),))]

    def wrapped(*refs):
        ins, cin = refs[:ni], refs[ni:ni + nci]
        outs, cout = refs[ni + nci:ni + nci + no], refs[ni + nci + no:ni + nci + no + nco]
        scr = refs[ni + nci + no + nco:ni + nci + no + nco + ns]
        sem_refs = refs[ni + nci + no + nco + ns:]
        step = pl.program_id(0)
        last = pl.program_id(0) == grid[0] - 1
        for ax in range(1, len(grid)):
            step = step * grid[ax] + pl.program_id(ax)
            last = jnp.logical_and(last, pl.program_id(ax) == grid[ax] - 1)

        def descriptors(si):
            io = sum(len(st.ins) for st in stages[:si])
            oo = sum(len(st.outs) for st in stages[:si])
            st = stages[si]
            return st.descriptors(cin[io:io + len(st.ins)], cout[oo:oo + len(st.outs)], *sem_refs[3 * si:3 * si + 3])

        def follower_descriptors(fi):
            si = [k for k, st in enumerate(stages) if st.then is not None][fi]
            oo = sum(len(st.outs) for st in stages[:si])
            bufs = cout[oo:oo + len(stages[si].outs)]
            k0 = 3 * (len(stages) + fi)
            return followers[fi].descriptors(bufs, bufs, *sem_refs[k0:k0 + 3])

        def start(si):
            @pl.when(step == stages[si].start_step)
            def _():
                for d in descriptors(si):
                    d.start()

        for si, st in enumerate(stages):
            if st.start_step == 0:
                start(si)
        body(*ins, *outs, *scr)
        for si, st in enumerate(stages):
            if st.start_step != 0:
                start(si)

        @pl.when(last)
        def _():
            for si in range(len(stages)):
                for d in descriptors(si):
                    d.wait()
            for fi in range(len(followers)):
                for d in follower_descriptors(fi):
                    d.start()
            for fi in range(len(followers)):
                for d in follower_descriptors(fi):
                    d.wait()

    res = pl.pallas_call(
        wrapped, name=name, grid=grid, in_specs=list(in_specs) + [_ANY] * nci,
        out_specs=list(out_specs) + [_ANY] * nco, out_shape=list(out_shape) + c_outs,
        scratch_shapes=scratch_shapes + sems, input_output_aliases=aliases,
        compiler_params=_params(*(("arbitrary",) * len(grid))))(*args, *c_ins)
    oo = no
    for st in stages:
        st.results = list(res[oo:oo + len(st.outs)])
        oo += len(st.outs)
    return list(res[:no])


def _dot(a, b, dims):
    return lax.dot_general(a.astype(BF16), b.astype(BF16), (dims, ((), ())), preferred_element_type=F32)


_MAX_ROW_BLOCK = 1024
_ATT_BLOCK = 512


_MAX_REDUCE_BLOCK = 2048


def _row_block(s, want=1024):
    return min(want, s, _MAX_ROW_BLOCK)


def _reduce_block(s):
    return min(s, _MAX_REDUCE_BLOCK)


def _matmul(name, grid, terms, extras, outs, epilogue, acc_shape, fill=(), summed=()):
    nt, ne, no, nf = len(terms), len(extras), len(outs), len(fill)
    nk = grid[-1]
    dims = [t[4] for t in terms]

    def body(*refs):
        a_refs, b_refs = refs[:nt], refs[nt:2 * nt]
        e_refs = refs[2 * nt:2 * nt + ne]
        o_refs = refs[2 * nt + ne + nf:2 * nt + ne + nf + no]

        def finish(acc):
            vals = epilogue(acc, *[e[...] for e in e_refs])
            for idx, (o, val) in enumerate(zip(o_refs, vals)):
                if idx in summed:
                    @pl.when(pl.program_id(0) == 0)
                    def _(o=o, val=val):
                        o[...] = val.astype(o.dtype)

                    @pl.when(pl.program_id(0) > 0)
                    def _(o=o, val=val):
                        o[...] += val.astype(o.dtype)
                else:
                    o[...] = val.astype(o.dtype)

        if nk == 1:
            part = None
            for a, b, d in zip(a_refs, b_refs, dims):
                t = _dot(a[...], b[...], d)
                part = t if part is None else part + t
            finish(part)
        else:
            acc_ref = refs[-1]
            k = pl.program_id(len(grid) - 1)

            @pl.when(k == 0)
            def _():
                acc_ref[...] = jnp.zeros_like(acc_ref)

            for a, b, d in zip(a_refs, b_refs, dims):
                acc_ref[...] += _dot(a[...], b[...], d)

            @pl.when(k == nk - 1)
            def _():
                finish(acc_ref[...])

    in_specs = [t[1] for t in terms] + [t[3] for t in terms] + [e[1] for e in extras] + [_ANY] * nf
    args = [t[0] for t in terms] + [t[2] for t in terms] + [e[0] for e in extras] + list(fill)
    sem = ("arbitrary" if summed else "parallel",) * (len(grid) - 1) + ("arbitrary",)
    aliases = {2 * nt + ne + i: i for i in range(nf)}
    return _call(
        body, name=name, grid=grid, in_specs=in_specs,
        out_specs=[o[1] for o in outs], out_shape=[o[0] for o in outs],
        scratch_shapes=[pltpu.VMEM(acc_shape, F32)] if nk > 1 else [], sem=sem, args=args, aliases=aliases)


def _ident(acc):
    return (acc,)


def _rmsnorm_fwd(name, x, gain, width, col_block=0):
    s = x.shape[0]
    bm = _row_block(s)

    def body(x_ref, g_ref, o_ref):
        xf = x_ref[...]
        r = lax.rsqrt(jnp.mean(xf * xf, axis=-1, keepdims=True) + RMS_EPS)
        o_ref[...] = ((xf * r) * g_ref[...]).astype(o_ref.dtype)

    return pl.pallas_call(
        body, name=name, grid=(s // bm,),
        in_specs=[pl.BlockSpec((bm, width), lambda i: (i, col_block)), pl.BlockSpec((1, width), lambda i: (0, 0))],
        out_specs=pl.BlockSpec((bm, width), lambda i: (i, 0)),
        out_shape=jax.ShapeDtypeStruct((s, width), BF16),
        compiler_params=_params("parallel"),
    )(x, gain)


def _rms_bwd_math(dy, xf, g, width):
    r = lax.rsqrt(jnp.mean(xf * xf, axis=-1, keepdims=True) + RMS_EPS)
    dyg = dy * g
    dot = jnp.sum(dyg * xf, axis=-1, keepdims=True)
    dx = r * dyg - xf * ((r * r * r) * (dot * (1.0 / width)))
    dgain = jnp.sum(dy * (xf * r), axis=0, keepdims=True)
    return dx, dgain


def _rmsnorm_bwd(name, dy, x, gain, width, col_block=0, dres=None, out_dtype=F32):
    s = x.shape[0]
    bm = _row_block(s)
    has_res = dres is not None

    def body(*refs):
        if has_res:
            dy_ref, x_ref, g_ref, r_ref, dx_ref, dg_ref, dxb_ref = refs
        else:
            dy_ref, x_ref, g_ref, dx_ref, dg_ref = refs
        dx, dgain = _rms_bwd_math(dy_ref[...].astype(F32), x_ref[...], g_ref[...], width)
        if has_res:
            dx = dx + r_ref[...]
            dxb_ref[...] = dx.astype(BF16)
        dx_ref[...] = dx.astype(dx_ref.dtype)

        @pl.when(pl.program_id(0) == 0)
        def _():
            dg_ref[...] = dgain

        @pl.when(pl.program_id(0) > 0)
        def _():
            dg_ref[...] += dgain

    row = pl.BlockSpec((bm, width), lambda i: (i, 0))
    in_specs = [row, pl.BlockSpec((bm, width), lambda i: (i, col_block)), pl.BlockSpec((1, width), lambda i: (0, 0))]
    args = [dy, x, gain]
    out_specs = [row, pl.BlockSpec((1, width), lambda i: (0, 0))]
    out_shape = [jax.ShapeDtypeStruct((s, width), out_dtype), jax.ShapeDtypeStruct((1, width), F32)]
    if has_res:
        in_specs.append(row)
        args.append(dres)
        out_specs.append(row)
        out_shape.append(jax.ShapeDtypeStruct((s, width), BF16))
    return _call(body, name=name, grid=(s // bm,), in_specs=in_specs, out_specs=out_specs, out_shape=out_shape,
                 sem=("arbitrary",), args=args)


def _ffn_up(name, n, wg, wu):
    s = n.shape[0]
    bm = _row_block(s)

    def body(n_ref, wg_ref, wu_ref, a_ref, dadu_ref, dadg_ref):
        x = n_ref[...]
        g = _dot(x, wg_ref[...], NT)
        u = _dot(x, wu_ref[...], NT)
        sg = jax.nn.sigmoid(g)
        silu = g * sg
        a_ref[...] = (silu * u).astype(BF16)
        dadu_ref[...] = silu.astype(BF16)
        dadg_ref[...] = (u * (sg * (1.0 + g * (1.0 - sg)))).astype(BF16)

    w_spec = pl.BlockSpec((None, FF_SHARD, D_MODEL), lambda j, i: (j, 0, 0))
    o_spec = pl.BlockSpec((None, bm, FF_SHARD), lambda j, i: (j, i, 0))
    shp = jax.ShapeDtypeStruct((N_CHIPS, s, FF_SHARD), BF16)
    return _call(
        body, name=name, grid=(N_CHIPS, s // bm),
        in_specs=[pl.BlockSpec((bm, D_MODEL), lambda j, i: (i, 0)), w_spec, w_spec],
        out_specs=[o_spec, o_spec, o_spec], out_shape=[shp, shp, shp],
        sem=("parallel", "parallel"), args=[n, wg, wu])


def _ffn1_up_gather(xin, gain, g_sh, u_sh, d_sh):
    s = xin.shape[0]
    bm = _row_block(s)
    nrb = s // bm
    rows, cols = g_sh.shape

    def body(x_ref, gain_ref, gs, us, ds, n_ref, a_ref, dadu_ref, dadg_ref, wg, wu, wd, gbuf, ubuf,
             send, recv, qsend, qrecv, fsend, frecv, loc, ld):
        r, i = pl.program_id(0), pl.program_id(1)
        x, y, c = lax.axis_index("x"), lax.axis_index("y"), lax.axis_index("c")
        sib = (x, y, 1 - c)
        mine, _ = _half_rows(c, rows)
        quarters = _quarter_rows(c, rows)
        shards, fulls, bufs = (gs, us, ds), (wg, wu, wd), (gbuf, ubuf)

        def remote(src, dst, ssem, rsem, dev):
            return pltpu.make_async_remote_copy(src_ref=src, dst_ref=dst, send_sem=ssem, recv_sem=rsem,
                                                device_id=dev, device_id_type=_MESH)

        def peer(rel):
            return ((1 - x) if rel & 2 else x, (1 - y) if rel & 1 else y, c)

        def ici(k, rel, dev=sib):
            return remote(shards[k].at[mine], fulls[k].at[rel, mine], send.at[k, rel - 1], recv.at[k, rel - 1], dev)

        def quarter(k, which, dev=sib):
            slab, q = ((2, quarters[0]), (1, quarters[1]))[which]
            return remote(fulls[k].at[slab, q], fulls[k].at[3, q], qsend.at[k, which], qrecv.at[k, which], dev)

        def fwd(k, rel):
            return remote(fulls[k].at[rel, mine], fulls[k].at[rel, mine], fsend.at[k, rel - 1], frecv.at[k, rel - 1], sib)

        def own(k):
            return pltpu.make_async_copy(shards[k], fulls[k].at[0], loc.at[k])

        def load(slab):
            for k in (0, 1):
                pltpu.make_async_copy(shards[k] if slab == 0 else fulls[k].at[slab], bufs[k], ld.at[k]).start()
            for k in (0, 1):
                pltpu.make_async_copy(shards[k] if slab == 0 else fulls[k].at[slab], bufs[k], ld.at[k]).wait()

        def from_neighbour(ks, rel):
            for k in ks:
                ici(k, rel).wait_recv()
                fwd(k, rel).start()
                quarter(k, 0 if rel == 2 else 1, peer(1 if rel == 2 else 2)).start()
            for k in ks:
                fwd(k, rel).wait_recv()

        def from_diagonal(ks):
            for k in ks:
                quarter(k, 0).wait_recv()
                quarter(k, 1).wait_recv()
                fwd(k, 3).start()
            for k in ks:
                fwd(k, 3).wait_recv()

        @pl.when(jnp.logical_and(r == 0, i == 0))
        def _():
            for k in range(3):
                own(k).start()
            for rel in (1, 2):
                for k in (0, 1):
                    ici(k, rel, peer(rel)).start()
            load(0)

        @pl.when(jnp.logical_and(r == 1, i == 0))
        def _():
            from_neighbour((0, 1), 1)
            load(1)
            for rel in (1, 2):
                ici(2, rel, peer(rel)).start()

        @pl.when(jnp.logical_and(r == 2, i == 0))
        def _():
            from_neighbour((0, 1), 2)
            load(2)

        @pl.when(jnp.logical_and(r == 3, i == 0))
        def _():
            from_diagonal((0, 1))
            load(3)

        xv = _norm_bf16(x_ref[...], gain_ref[...])

        @pl.when(r == 0)
        def _():
            n_ref[...] = xv

        g = _dot(xv, gbuf[...], NT)
        u = _dot(xv, ubuf[...], NT)
        sg = jax.nn.sigmoid(g)
        silu = g * sg
        a_ref[...] = (silu * u).astype(BF16)
        dadu_ref[...] = silu.astype(BF16)
        dadg_ref[...] = (u * (sg * (1.0 + g * (1.0 - sg)))).astype(BF16)

        @pl.when(jnp.logical_and(r == 3, i == nrb - 1))
        def _():
            from_neighbour((2,), 1)
            from_neighbour((2,), 2)
            from_diagonal((2,))
            for k in range(3):
                for rel in (1, 2):
                    ici(k, rel).wait_send()
                for which in (0, 1):
                    quarter(k, which).wait_send()
                for rel in (1, 2, 3):
                    fwd(k, rel).wait_send()
                own(k).wait()

    o_spec = pl.BlockSpec((None, bm, FF_SHARD), lambda r, i: (r, i, 0))
    act = jax.ShapeDtypeStruct((N_CHIPS, s, FF_SHARD), BF16)
    full = jax.ShapeDtypeStruct((N_CHIPS, rows, cols), BF16)
    dma = pltpu.SemaphoreType.DMA
    if _PLAN is not None:
        _PLAN.last_slab_step = 3 * nrb
    n_spec = pl.BlockSpec((bm, D_MODEL), lambda r, i: (jnp.where(r == 0, i, nrb - 1), 0))
    return _call(
        body, name="ffn1_up", grid=(N_CHIPS, nrb),
        in_specs=[pl.BlockSpec((bm, D_MODEL), lambda r, i: (i, 0)), pl.BlockSpec((1, D_MODEL), lambda r, i: (0, 0)),
                  _ANY, _ANY, _ANY],
        out_specs=[n_spec, o_spec, o_spec, o_spec, _ANY, _ANY, _ANY],
        out_shape=[jax.ShapeDtypeStruct((s, D_MODEL), BF16), act, act, act, full, full, full],
        scratch_shapes=[pltpu.VMEM((rows, cols), BF16), pltpu.VMEM((rows, cols), BF16), dma((3, 2)), dma((3, 2)),
                        dma((3, 2)), dma((3, 2)), dma((3, 3)), dma((3, 3)), dma((3,)), dma((2,))],
        sem=("arbitrary", "arbitrary"), args=[xin, gain, g_sh, u_sh, d_sh])


def _residual_epilogue(alpha, with_norm):
    if not with_norm:
        return lambda acc, r: (r + alpha * acc,)

    def epilogue(acc, r, g):
        h = r + alpha * acc
        rs = lax.rsqrt(jnp.mean(h * h, axis=-1, keepdims=True) + RMS_EPS)
        return h, (h * rs) * g

    return epilogue


def _residual_outs(s, bm, gain):
    row = pl.BlockSpec((bm, D_MODEL), lambda i, k: (i, 0))
    outs = [(jax.ShapeDtypeStruct((s, D_MODEL), F32), row)]
    if gain is None:
        return [], outs
    return [(gain, pl.BlockSpec((1, D_MODEL), lambda i, k: (0, 0)))], outs + [(jax.ShapeDtypeStruct((s, D_MODEL), BF16), row)]


def _loss_epilogue(acc, res, g, target):
    d = acc.shape[-1]
    h = res + 0.5 * acc
    r = lax.rsqrt(jnp.mean(h * h, axis=-1, keepdims=True) + RMS_EPS)
    err = (h * r) * g - target
    part = 0.5 * jnp.sum(jnp.mean(err * err, axis=-1, keepdims=True), axis=0, keepdims=True)
    dx, dgain = _rms_bwd_math(err * (1.0 / d), h, g, d)
    return dx, dx, jnp.broadcast_to(part, (1, 128)), dgain


def _ffn_down(name, a, wd, res, gain=None, loss=None):
    s = a.shape[1]
    bm = _row_block(s, 512)
    row = pl.BlockSpec((bm, D_MODEL), lambda i, k: (i, 0))
    terms = [(a, pl.BlockSpec((None, bm, FF_SHARD), lambda i, k, j=j: (j, i, 0)),
              wd, pl.BlockSpec((None, FF_SHARD, D_MODEL), lambda i, k, j=j: (j, 0, 0)), NN) for j in range(N_CHIPS)]
    if loss is not None:
        vec = pl.BlockSpec((1, D_MODEL), lambda i, k: (0, 0))
        outs = [(jax.ShapeDtypeStruct((s, D_MODEL), F32), row), (jax.ShapeDtypeStruct((s, D_MODEL), BF16), row),
                (jax.ShapeDtypeStruct((1, 128), F32), pl.BlockSpec((1, 128), lambda i, k: (0, 0))),
                (jax.ShapeDtypeStruct((1, D_MODEL), F32), vec)]
        return _matmul(name, (s // bm, 1), terms, [(res, row), (loss[0], vec), (loss[1], row)], outs,
                       _loss_epilogue, None, summed=(2, 3))
    extras, outs = _residual_outs(s, bm, gain)
    res_out = _matmul(name, (s // bm, 1), terms, [(res, row)] + extras, outs,
                      _residual_epilogue(0.5, gain is not None), None)
    return res_out if gain is not None else res_out[0]


def _norm_bwd_epilogue(width):
    def epilogue(acc, h, g, dres):
        dx, dgain = _rms_bwd_math(acc, h, g, width)
        dx = dx + dres
        return dx, dx, dgain

    return epilogue


def _norm_bwd_operands(s, bm, h, gain, dres):
    row = pl.BlockSpec((bm, D_MODEL), lambda i, k: (i, 0))
    vec = pl.BlockSpec((1, D_MODEL), lambda i, k: (0, 0))
    extras = [(h, row), (gain, vec), (dres, row)]
    outs = [(jax.ShapeDtypeStruct((s, D_MODEL), F32), row), (jax.ShapeDtypeStruct((s, D_MODEL), BF16), row),
            (jax.ShapeDtypeStruct((1, D_MODEL), F32), vec)]
    return extras, outs, (2,)


def _ffn_bwd(tag, dh, n, dadg, dadu, a, wg, wu, wd, grads, norm_bwd=None):
    s = dh.shape[0]
    bm = _row_block(s)
    bk = _reduce_block(s)
    nk = s // bk

    def act_bwd(acc, dg_da, du_da):
        da = 0.5 * acc
        return da * dg_da.astype(F32), da * du_da.astype(F32)

    slab = pl.BlockSpec((None, bm, FF_SHARD), lambda j, i, k: (j, i, 0))
    shp = jax.ShapeDtypeStruct((N_CHIPS, s, FF_SHARD), BF16)
    dg, du = _matmul(
        tag + "_dact", (N_CHIPS, s // bm, 1),
        [(dh, pl.BlockSpec((bm, D_MODEL), lambda j, i, k: (i, 0)),
          wd, pl.BlockSpec((None, FF_SHARD, D_MODEL), lambda j, i, k: (j, 0, 0)), NT)],
        [(dadg, slab), (dadu, slab)], [(shp, slab), (shp, slab)], act_bwd, None)

    grads[tag + "_w_down"] = _matmul(
        tag + "_dwd", (N_CHIPS, nk),
        [(a, pl.BlockSpec((None, bk, FF_SHARD), lambda j, k: (j, k, 0)),
          dh, pl.BlockSpec((bk, D_MODEL), lambda j, k: (k, 0)), TN)],
        [], [(jax.ShapeDtypeStruct((N_CHIPS, FF_SHARD, D_MODEL), BF16),
              pl.BlockSpec((None, FF_SHARD, D_MODEL), lambda j, k: (j, 0, 0)))],
        lambda acc: (0.5 * acc,), (FF_SHARD, D_MODEL))[0]

    def dw_up(nm, dact):
        return _matmul(
            nm, (N_CHIPS, nk),
            [(dact, pl.BlockSpec((None, bk, FF_SHARD), lambda j, k: (j, k, 0)),
              n, pl.BlockSpec((bk, D_MODEL), lambda j, k: (k, 0)), TN)],
            [], [(jax.ShapeDtypeStruct((N_CHIPS, FF_SHARD, D_MODEL), BF16),
                  pl.BlockSpec((None, FF_SHARD, D_MODEL), lambda j, k: (j, 0, 0)))],
            _ident, (FF_SHARD, D_MODEL))[0]

    grads[tag + "_w_gate"] = dw_up(tag + "_dwg", dg)
    grads[tag + "_w_up"] = dw_up(tag + "_dwu", du)

    bn = _row_block(s, 512)
    steps = s // bn // 2
    prev, dgain = (), None
    for part, off in (("_dn_a", 0), ("_dn_b", steps)):
        row = pl.BlockSpec((bn, D_MODEL), lambda i, k, off=off: (i + off, 0))
        terms = []
        for j in range(N_CHIPS):
            a_slab = pl.BlockSpec((None, bn, FF_SHARD), lambda i, k, j=j, off=off: (j, i + off, 0))
            w_slab = pl.BlockSpec((None, FF_SHARD, D_MODEL), lambda i, k, j=j: (j, 0, 0))
            terms += [(dg, a_slab, wg, w_slab, NN), (du, a_slab, wu, w_slab, NN)]
        if norm_bwd is None:
            prev = _matmul(tag + part, (steps, 1), terms, [], [(jax.ShapeDtypeStruct((s, D_MODEL), F32), row)],
                           _ident, None, fill=prev)
            continue
        h, gain, dres = norm_bwd
        vec = pl.BlockSpec((1, D_MODEL), lambda i, k: (0, 0))
        res = _matmul(
            tag + part, (steps, 1), terms, [(h, row), (gain, vec), (dres, row)],
            [(jax.ShapeDtypeStruct((s, D_MODEL), F32), row), (jax.ShapeDtypeStruct((s, D_MODEL), BF16), row),
             (jax.ShapeDtypeStruct((1, D_MODEL), F32), vec)],
            _norm_bwd_epilogue(D_MODEL), None, fill=prev, summed=(2,))
        prev = res[:2]
        dgain = res[2] if dgain is None else dgain + res[2]
    return prev[0] if norm_bwd is None else (prev[0], prev[1], dgain)


def _mm_nn(name, a, b, out_dtype, res=None, gain=None):
    s, k = a.shape
    nn = b.shape[1]
    bm = _row_block(s)
    row = pl.BlockSpec((bm, nn), lambda i, kk: (i, 0))
    term = [(a, pl.BlockSpec((bm, k), lambda i, kk: (i, 0)), b, pl.BlockSpec((k, nn), lambda i, kk: (0, 0)), NN)]
    if res is None:
        return _matmul(name, (s // bm, 1), term, [], [(jax.ShapeDtypeStruct((s, nn), out_dtype), row)], _ident, None)[0]
    extras, outs = _residual_outs(s, bm, gain)
    res_out = _matmul(name, (s // bm, 1), term, [(res, row)] + extras, outs,
                      _residual_epilogue(1.0, gain is not None), None)
    return res_out if gain is not None else res_out[0]


def _mm_nt(name, a, b, out_dtype, attn_out=None, nh=0, dv=0):
    s, nn = a.shape
    k = b.shape[0]
    bm = _row_block(s)
    term = [(a, pl.BlockSpec((bm, nn), lambda i, kk: (i, 0)), b, pl.BlockSpec((k, nn), lambda i, kk: (0, 0)), NT)]
    out = (jax.ShapeDtypeStruct((s, k), out_dtype), pl.BlockSpec((bm, k), lambda i, kk: (i, 0)))
    if attn_out is None:
        return _matmul(name, (s // bm, 1), term, [], [out], _ident, None)[0]

    def with_delta(acc, o):
        do = acc.astype(out_dtype).astype(F32)
        cols = [jnp.sum(do[:, h * dv:(h + 1) * dv] * o[:, h * dv:(h + 1) * dv].astype(F32), axis=-1, keepdims=True)
                for h in range(nh)]
        return acc, jnp.stack(cols, axis=0)

    return _matmul(
        name, (s // bm, 1), term, [(attn_out, pl.BlockSpec((bm, nh * dv), lambda i, kk: (i, 0)))],
        [out, (jax.ShapeDtypeStruct((nh, s, 1), F32), pl.BlockSpec((nh, bm, 1), lambda i, kk: (0, i, 0)))],
        with_delta, None)


def _mm_nt_norm_bwd(name, a, b, h, gain, dres):
    s, nn = a.shape
    bm = _row_block(s, 512)
    extras, outs, summed = _norm_bwd_operands(s, bm, h, gain, dres)
    return _matmul(
        name, (s // bm, 1),
        [(a, pl.BlockSpec((bm, nn), lambda i, kk: (i, 0)), b, pl.BlockSpec(b.shape, lambda i, kk: (0, 0)), NT)],
        extras, outs, _norm_bwd_epilogue(D_MODEL), None, summed=summed)


def _w_in_dx_norm_bwd(dz, w_t, h, gain, dres):
    s = dz.shape[0]
    bm = _row_block(s, 512)
    epilogue = _norm_bwd_epilogue(D_MODEL)

    def body(dz_ref, w_ref, h_ref, g_ref, r_ref, dx_ref, dxb_ref, dg_ref):
        dzv = dz_ref[...]
        dn = jnp.concatenate([_dot(dzv, w_ref[j], NN) for j in range(N_CHIPS)], axis=1)
        dx, _, dgain = epilogue(dn, h_ref[...], g_ref[...], r_ref[...])
        dx_ref[...] = dx
        dxb_ref[...] = dx.astype(BF16)

        @pl.when(pl.program_id(0) == 0)
        def _():
            dg_ref[...] = dgain

        @pl.when(pl.program_id(0) > 0)
        def _():
            dg_ref[...] += dgain

    row = pl.BlockSpec((bm, D_MODEL), lambda i: (i, 0))
    vec = pl.BlockSpec((1, D_MODEL), lambda i: (0, 0))
    return _call(
        body, name="w_in_dx", grid=(s // bm,),
        in_specs=[row, pl.BlockSpec(w_t.shape, lambda i: (0, 0, 0)), row, vec, row],
        out_specs=[row, row, vec],
        out_shape=[jax.ShapeDtypeStruct((s, D_MODEL), F32), jax.ShapeDtypeStruct((s, D_MODEL), BF16),
                   jax.ShapeDtypeStruct((1, D_MODEL), F32)],
        sem=("arbitrary",), args=[dz, w_t, h, gain, dres])


def _mm_tn(name, a, b, out_dtype=BF16):
    s, k = a.shape
    nn = b.shape[1]
    bk = _reduce_block(s)
    return _matmul(
        name, (s // bk,),
        [(a, pl.BlockSpec((bk, k), lambda kk: (kk, 0)), b, pl.BlockSpec((bk, nn), lambda kk: (kk, 0)), TN)],
        [], [(jax.ShapeDtypeStruct((k, nn), out_dtype), pl.BlockSpec((k, nn), lambda kk: (0, 0)))],
        _ident, (k, nn))[0]


def _mm_heads_fwd(name, a, w, out_dtype, w_transposed=False):
    s, k = a.shape
    nh = w.shape[0]
    nn = w.shape[1] if w_transposed else w.shape[2]
    bm = _row_block(s)
    return _matmul(
        name, (nh, s // bm, 1),
        [(a, pl.BlockSpec((bm, k), lambda h, i, kk: (i, 0)),
          w, pl.BlockSpec((None,) + w.shape[1:], lambda h, i, kk: (h, 0, 0)), NT if w_transposed else NN)],
        [], [(jax.ShapeDtypeStruct((s, nh * nn), out_dtype), pl.BlockSpec((bm, nn), lambda h, i, kk: (i, h)))],
        _ident, None)[0]


def _mm_heads_bwd(name, dy, a, w, w_transposed=False):
    s, k = a.shape
    nh = w.shape[0]
    nn = w.shape[1] if w_transposed else w.shape[2]
    bm = _row_block(s)
    bk = _reduce_block(s)
    w_spec = pl.BlockSpec((None,) + w.shape[1:], lambda i, h: (h, 0, 0))
    da = _matmul(
        name + "_dx", (s // bm, nh),
        [(dy, pl.BlockSpec((bm, nn), lambda i, h: (i, h)), w, w_spec, NN if w_transposed else NT)],
        [], [(jax.ShapeDtypeStruct((s, k), F32), pl.BlockSpec((bm, k), lambda i, h: (i, 0)))], _ident, (bm, k))[0]
    a_term = (a, pl.BlockSpec((bk, k), lambda h, kk: (kk, 0)))
    dy_term = (dy, pl.BlockSpec((bk, nn), lambda h, kk: (kk, h)))
    lhs, rhs = (dy_term, a_term) if w_transposed else (a_term, dy_term)
    dw = _matmul(
        name + "_dw", (nh, s // bk), [lhs + rhs + (TN,)],
        [], [(jax.ShapeDtypeStruct(w.shape, BF16), pl.BlockSpec((None,) + w.shape[1:], lambda h, kk: (h, 0, 0)))],
        _ident, w.shape[1:])[0]
    return da, dw


def _w_in_fwd(n, w_t):
    s = n.shape[0]
    bm = _row_block(s)
    nh, nout, kin = w_t.shape
    terms = [(n, pl.BlockSpec((bm, kin), lambda i, k, j=j: (i, j)),
              w_t, pl.BlockSpec((None, nout, kin), lambda i, k, j=j: (j, 0, 0)), NT) for j in range(nh)]
    row = pl.BlockSpec((bm, nout), lambda i, k: (i, 0))
    return _matmul("w_in", (s // bm, 1), terms, [], [(jax.ShapeDtypeStruct((s, nout), F32), row)], _ident, None)[0]


def _w_in_dw(dz, n):
    s, nout = dz.shape
    kin = n.shape[1] // N_CHIPS
    bk = _reduce_block(s)
    return _matmul(
        "w_in_dw", (N_CHIPS, s // bk),
        [(dz, pl.BlockSpec((bk, nout), lambda j, k: (k, 0)), n, pl.BlockSpec((bk, kin), lambda j, k: (k, j)), TN)],
        [], [(jax.ShapeDtypeStruct((N_CHIPS, nout, kin), BF16), pl.BlockSpec((None, nout, kin), lambda j, k: (j, 0, 0)))],
        _ident, (nout, kin))[0]


def _rope_tables(positions):
    half = ROPE_DIM // 2
    freqs = 1.0 / (ROPE_BASE ** (jnp.arange(0, ROPE_DIM, 2, dtype=F32) / ROPE_DIM))
    ang = positions.astype(F32)[:, None] * freqs
    cos, sin = jnp.cos(ang), jnp.sin(ang)
    z = jnp.zeros_like(cos)
    tc = jnp.concatenate([cos, cos, z, z], axis=-1)
    ta = jnp.concatenate([-sin, z, z, z], axis=-1)
    tb = jnp.concatenate([z, sin, z, z], axis=-1)
    assert tc.shape[-1] == 4 * half
    return tc, ta, tb


def _rope(x, tc, ta, tb):
    return x * tc + pltpu.roll(x, 96, 1) * ta + pltpu.roll(x, 32, 1) * tb


def _rope_t(dy, tc, ta, tb):
    return dy * tc + pltpu.roll(dy * ta, 32, 1) + pltpu.roll(dy * tb, 96, 1)


def _norm_bf16(x, g):
    r = lax.rsqrt(jnp.mean(x * x, axis=-1, keepdims=True) + RMS_EPS)
    return ((x * r) * g).astype(BF16)


def _qkv_prep(z, q_gain, kv_gain, wq_t, wkv, tc, ta, tb):
    s = z.shape[0]
    bm = _row_block(s, 512)

    def body(zq_ref, zkv_ref, zkr_ref, qg_ref, kvg_ref, wq_ref, wkv_ref, tc_ref, ta_ref, tb_ref,
             qn_ref, kvn_ref, q_ref, k_ref, v_ref):
        c, a, b = tc_ref[...], ta_ref[...], tb_ref[...]
        qn = _norm_bf16(zq_ref[...], qg_ref[...])
        kvn = _norm_bf16(zkv_ref[...], kvg_ref[...])
        qn_ref[...] = qn
        kvn_ref[...] = kvn
        kpe = _rope(zkr_ref[...], c, a, b).astype(BF16)
        for h in range(MLA_HEADS):
            lo = h * HEAD_QK
            qp = _dot(qn, wq_ref[h], NT)
            q_ref[:, lo:lo + 128] = qp[:, :128].astype(BF16)
            q_ref[:, lo + 128:lo + 256] = _rope(qp[:, 128:], c, a, b).astype(BF16)
            kv = _dot(kvn, wkv_ref[h], NN)
            k_ref[:, lo:lo + 128] = kv[:, :128].astype(BF16)
            k_ref[:, lo + 128:lo + 256] = kpe
            v_ref[:, h * HEAD_V:(h + 1) * HEAD_V] = kv[:, 128:].astype(BF16)

    def cols(width, blk):
        return pl.BlockSpec((bm, width), lambda i: (i, blk))

    def whole(a):
        return pl.BlockSpec(a.shape, lambda i: (0,) * a.ndim)

    tab = cols(128, 0)
    return _call(
        body, name="qkv_prep", grid=(s // bm,),
        in_specs=[cols(Q_LORA, 0), cols(KV_LORA, 2), cols(128, 3), whole(q_gain), whole(kv_gain), whole(wq_t),
                  whole(wkv), tab, tab, tab],
        out_specs=[cols(Q_LORA, 0), cols(KV_LORA, 0), cols(1024, 0), cols(1024, 0), cols(512, 0)],
        out_shape=[jax.ShapeDtypeStruct((s, Q_LORA), BF16), jax.ShapeDtypeStruct((s, KV_LORA), BF16),
                   jax.ShapeDtypeStruct((s, 1024), BF16), jax.ShapeDtypeStruct((s, 1024), BF16),
                   jax.ShapeDtypeStruct((s, 512), BF16)],
        sem=("parallel",), args=[z, z, z, q_gain, kv_gain, wq_t, wkv, tc, ta, tb])


def _qkv_prep_bwd(dq, dk, dv, z, qn, kvn, q_gain, kv_gain, wq_t, wkv, tc, ta, tb):
    s = z.shape[0]
    bm = _row_block(s, 512)
    nsteps = s // bm

    def body(dq_ref, dk_ref, dv_ref, zq_ref, zkv_ref, qn_ref, kvn_ref, qg_ref, kvg_ref, wq_ref, wkv_ref,
             tc_ref, ta_ref, tb_ref, dz_ref, dqg_ref, dkvg_ref, dwq_ref, dwkv_ref, wq_acc, wkv_acc):
        i = pl.program_id(0)
        c, a, b = tc_ref[...], ta_ref[...], tb_ref[...]

        @pl.when(i == 0)
        def _():
            wq_acc[...] = jnp.zeros_like(wq_acc)
            wkv_acc[...] = jnp.zeros_like(wkv_acc)

        qn, kvn = qn_ref[...], kvn_ref[...]
        dqn = jnp.zeros((bm, Q_LORA), F32)
        dkvn = jnp.zeros((bm, KV_LORA), F32)
        dpe = jnp.zeros((bm, 128), F32)
        for h in range(MLA_HEADS):
            lo = h * HEAD_QK
            dqp = jnp.concatenate([dq_ref[:, lo:lo + 128].astype(BF16),
                                   _rope_t(dq_ref[:, lo + 128:lo + 256], c, a, b).astype(BF16)], axis=1)
            dqn = dqn + _dot(dqp, wq_ref[h], NN)
            wq_acc[h] += _dot(dqp, qn, TN)
            dkv = jnp.concatenate([dk_ref[:, lo:lo + 128].astype(BF16),
                                   dv_ref[:, h * HEAD_V:(h + 1) * HEAD_V].astype(BF16)], axis=1)
            dkvn = dkvn + _dot(dkv, wkv_ref[h], NT)
            wkv_acc[h] += _dot(kvn, dkv, TN)
            dpe = dpe + dk_ref[:, lo + 128:lo + 256]
        dcq, dqg = _rms_bwd_math(dqn, zq_ref[...], qg_ref[...], Q_LORA)
        dckv, dkvg = _rms_bwd_math(dkvn, zkv_ref[...], kvg_ref[...], KV_LORA)
        dz_ref[:, 0:Q_LORA] = dcq.astype(BF16)
        dz_ref[:, Q_LORA:Q_LORA + KV_LORA] = dckv.astype(BF16)
        dz_ref[:, Q_LORA + KV_LORA:512] = _rope_t(dpe, c, a, b).astype(BF16)

        @pl.when(i == 0)
        def _():
            dqg_ref[...] = dqg
            dkvg_ref[...] = dkvg

        @pl.when(i > 0)
        def _():
            dqg_ref[...] += dqg
            dkvg_ref[...] += dkvg

        @pl.when(i == nsteps - 1)
        def _():
            dwq_ref[...] = wq_acc[...].astype(BF16)
            dwkv_ref[...] = wkv_acc[...].astype(BF16)

    def cols(width, blk):
        return pl.BlockSpec((bm, width), lambda i: (i, blk))

    def whole(shape):
        return pl.BlockSpec(shape, lambda i: (0,) * len(shape))

    tab = cols(128, 0)
    return _call(
        body, name="qkv_prep_bwd", grid=(nsteps,),
        in_specs=[cols(1024, 0), cols(1024, 0), cols(512, 0), cols(Q_LORA, 0), cols(KV_LORA, 2), cols(Q_LORA, 0),
                  cols(KV_LORA, 0), whole(q_gain.shape), whole(kv_gain.shape), whole(wq_t.shape), whole(wkv.shape),
                  tab, tab, tab],
        out_specs=[cols(512, 0), whole(q_gain.shape), whole(kv_gain.shape), whole(wq_t.shape), whole(wkv.shape)],
        out_shape=[jax.ShapeDtypeStruct((s, 512), BF16), jax.ShapeDtypeStruct(q_gain.shape, F32),
                   jax.ShapeDtypeStruct(kv_gain.shape, F32), jax.ShapeDtypeStruct(wq_t.shape, BF16),
                   jax.ShapeDtypeStruct(wkv.shape, BF16)],
        scratch_shapes=[pltpu.VMEM(wq_t.shape, F32), pltpu.VMEM(wkv.shape, F32)],
        sem=("arbitrary",), args=[dq, dk, dv, z, z, qn, kvn, q_gain, kv_gain, wq_t, wkv, tc, ta, tb])


def _causal_mask(s, row0, col0):
    rows = row0 + lax.broadcasted_iota(jnp.int32, s.shape, 0)
    cols = col0 + lax.broadcasted_iota(jnp.int32, s.shape, 1)
    return jnp.where(cols <= rows, s, -jnp.inf)


def _attn_fwd(name, q, k, k_off, v, v_off, nh, dq, dv, scale, causal, blk):
    sq, sk = q.shape[0], k.shape[0]
    bq = min(blk, sq)
    bk = min(blk, sk)
    nkv = sk // bk
    assert not causal or (sq == sk and bq == bk)

    hq = bq
    log2e = 1.4426950408889634
    c2 = scale * log2e

    def body(q_ref, k_ref, v_ref, o_ref, lse_ref):
        qi = pl.program_id(1)
        qs = (q_ref[...],)

        def step(j, carry, masked):
            rows = pl.ds(pl.multiple_of(j * bk, bk), bk)
            kb, vb = k_ref[rows, :], v_ref[rows, :]
            out = []
            for t, (m, l, acc) in enumerate(carry):
                s = _dot(qs[t], kb, NT) * c2
                if masked:
                    s = _causal_mask(s, qi * bq + t * hq, j * bk)
                m_new = jnp.maximum(m, jnp.max(s, axis=-1, keepdims=True))
                alpha = jnp.exp2(m - m_new)
                p = jnp.exp2(s - m_new)
                l = alpha * l + jnp.sum(p, axis=-1, keepdims=True)
                acc = alpha * acc + _dot(p, vb, NN)
                out.append((m_new, l, acc))
            return tuple(out)

        one = (jnp.full((hq, 1), -jnp.inf, F32), jnp.zeros((hq, 1), F32), jnp.zeros((hq, dv), F32))
        init = (one,)
        if causal:
            carry = lax.fori_loop(0, qi, lambda j, c: step(j, c, False), init)
            fin = step(qi, carry, True)
        else:
            fin = lax.fori_loop(0, nkv, lambda j, c: step(j, c, False), init)
        for t, (m, l, acc) in enumerate(fin):
            o_ref[t * hq:(t + 1) * hq, :] = (acc / l).astype(o_ref.dtype)
            lse_ref[t * hq:(t + 1) * hq, :] = m * (1.0 / log2e) + jnp.log(l)

    return _call(
        body, name=name, grid=(nh, sq // bq),
        in_specs=[pl.BlockSpec((bq, dq), lambda h, i: (i, h)),
                  pl.BlockSpec((sk, dq), lambda h, i: (0, k_off + h)),
                  pl.BlockSpec((sk, dv), lambda h, i: (0, v_off + h))],
        out_specs=[pl.BlockSpec((bq, dv), lambda h, i: (i, h)), pl.BlockSpec((None, bq, 1), lambda h, i: (h, i, 0))],
        out_shape=[jax.ShapeDtypeStruct((sq, nh * dv), BF16), jax.ShapeDtypeStruct((nh, sq, 1), F32)],
        sem=("parallel", "parallel"), args=[q, k, v])


def _attn_bwd(name, q, k, k_off, v, v_off, do, do_off, lse, delta, nh, dq, dv, scale, causal, blk):
    sq, sk = q.shape[0], k.shape[0]
    bq = min(blk, sq)
    bk = min(blk, sk)
    nq = sq // bq
    assert not causal or (sq == sk and bq == bk)

    def body(q_ref, k_ref, v_ref, do_ref, lse_ref, dl_ref, dq_ref, dk_ref, dv_ref, dk_acc, dv_acc):
        j = pl.program_id(1)

        @pl.when(j == 0)
        def _():
            dq_ref[...] = jnp.zeros_like(dq_ref)

        dk_acc[...] = jnp.zeros_like(dk_acc)
        dv_acc[...] = jnp.zeros_like(dv_acc)
        kv = k_ref[...]
        vv = v_ref[...]

        def step(i, masked):
            rows = pl.ds(pl.multiple_of(i * bq, bq), bq)
            qv = q_ref[rows, :]
            dov = do_ref[rows, :].astype(BF16)
            s = _dot(qv, kv, NT) * scale
            if masked:
                s = _causal_mask(s, i * bq, j * bk)
            p = jnp.exp(s - lse_ref[rows, :])
            dp = _dot(dov, vv, NT)
            ds = (p * (dp - dl_ref[rows, :]) * scale).astype(BF16)
            dv_acc[...] += _dot(p, dov, TN)
            dk_acc[...] += _dot(ds, qv, TN)
            dq_ref[rows, :] += _dot(ds, kv, NN)

        if causal:
            step(j, True)

            def loop(i, c):
                step(i, False)
                return c

            lax.fori_loop(j + 1, nq, loop, 0)
        else:
            def loop(i, c):
                step(i, False)
                return c

            lax.fori_loop(0, nq, loop, 0)
        dk_ref[...] = dk_acc[...]
        dv_ref[...] = dv_acc[...]

    stat = pl.BlockSpec((None, sq, 1), lambda h, j: (h, 0, 0))
    return _call(
        body, name=name, grid=(nh, sk // bk),
        in_specs=[pl.BlockSpec((sq, dq), lambda h, j: (0, h)),
                  pl.BlockSpec((bk, dq), lambda h, j: (j, k_off + h)),
                  pl.BlockSpec((bk, dv), lambda h, j: (j, v_off + h)),
                  pl.BlockSpec((sq, dv), lambda h, j: (0, do_off + h)), stat, stat],
        out_specs=[pl.BlockSpec((sq, dq), lambda h, j: (0, h)),
                   pl.BlockSpec((bk, dq), lambda h, j: (j, h)),
                   pl.BlockSpec((bk, dv), lambda h, j: (j, h))],
        out_shape=[jax.ShapeDtypeStruct((sq, nh * dq), F32), jax.ShapeDtypeStruct((sk, nh * dq), F32),
                   jax.ShapeDtypeStruct((sk, nh * dv), F32)],
        scratch_shapes=[pltpu.VMEM((bk, dq), F32), pltpu.VMEM((bk, dv), F32)],
        sem=("parallel", "arbitrary"), args=[q, k, v, do, lse, delta])


def _pool_diff(z, g):
    s = z.shape[0]
    t = lax.broadcasted_iota(jnp.int32, z.shape, 0)
    acc = z
    sums = []
    for k in (1, 2, 4, 8):
        acc = acc + jnp.where(t >= k, pltpu.roll(acc, k, 0), 0.0)
        sums.append(acc)
    win = jnp.where(g == 0, sums[0], jnp.where(g == 1, sums[1], jnp.where(g == 2, sums[2], sums[3])))
    w = lax.shift_left(jnp.int32(2), g)
    count = jnp.minimum(t + 1, w).astype(F32)
    del s
    return win / count - z, count


def _pool_fwd(z, pool_w, pool_scale):
    s = z.shape[0]

    def body(z_ref, w_ref, sc_ref, o_ref):
        diff, _ = _pool_diff(z_ref[...], pl.program_id(0))
        o_ref[...] = (_dot(diff, w_ref[...], NN) * sc_ref[...]).astype(o_ref.dtype)

    return _call(
        body, name="pool_fwd", grid=(POOL_GROUPS,),
        in_specs=[pl.BlockSpec((s, POOL_CH), lambda g: (0, 4 + g)),
                  pl.BlockSpec((None, POOL_CH, POOL_CH), lambda g: (g, 0, 0)),
                  pl.BlockSpec((1, POOL_CH), lambda g: (0, g))],
        out_specs=[pl.BlockSpec((s, POOL_CH), lambda g: (0, g))],
        out_shape=[jax.ShapeDtypeStruct((s, POOL_GROUPS * POOL_CH), BF16)],
        sem=("parallel",), args=[z, pool_w, pool_scale])[0]


def _pool_bwd(dcat, z, pool_w, pool_scale):
    s = z.shape[0]

    def body(dp_ref, z_ref, w_ref, sc_ref, dz_ref, dw_ref, dsc_ref):
        g = pl.program_id(0)
        diff, count = _pool_diff(z_ref[...], g)
        dpf = dp_ref[...].astype(F32)
        u = _dot(diff, w_ref[...], NN)
        dsc_ref[...] = jnp.sum(dpf * u, axis=0, keepdims=True)
        du = (dpf * sc_ref[...]).astype(BF16)
        dw_ref[...] = _dot(diff, du, TN)
        ddiff = _dot(du, w_ref[...], NT)
        t = lax.broadcasted_iota(jnp.int32, ddiff.shape, 0)
        acc = ddiff / count
        sums = []
        for k in (1, 2, 4, 8):
            acc = acc + jnp.where(t < s - k, pltpu.roll(acc, s - k, 0), 0.0)
            sums.append(acc)
        win = jnp.where(g == 0, sums[0], jnp.where(g == 1, sums[1], jnp.where(g == 2, sums[2], sums[3])))
        dz_ref[...] = win - ddiff

    return pl.pallas_call(
        body, name="pool_bwd", grid=(POOL_GROUPS,),
        in_specs=[pl.BlockSpec((s, POOL_CH), lambda g: (0, 4 + g)),
                  pl.BlockSpec((s, POOL_CH), lambda g: (0, 4 + g)),
                  pl.BlockSpec((None, POOL_CH, POOL_CH), lambda g: (g, 0, 0)),
                  pl.BlockSpec((1, POOL_CH), lambda g: (0, g))],
        out_specs=[pl.BlockSpec((s, POOL_CH), lambda g: (0, g)),
                   pl.BlockSpec((None, POOL_CH, POOL_CH), lambda g: (g, 0, 0)),
                   pl.BlockSpec((1, POOL_CH), lambda g: (0, g))],
        out_shape=[jax.ShapeDtypeStruct((s, POOL_GROUPS * POOL_CH), F32),
                   jax.ShapeDtypeStruct((POOL_GROUPS, POOL_CH, POOL_CH), F32),
                   jax.ShapeDtypeStruct((1, POOL_GROUPS * POOL_CH), F32)],
        compiler_params=_params("parallel"),
    )(dcat, z, pool_w, pool_scale)


def _local_step(x, mem, positions, target, w, grads):
    tc, ta, tb = _rope_tables(positions)
    blk = _ATT_BLOCK

    if "ffn1_shards" in w:
        n1, a1, dadu1, dadg1, w["ffn1_w_gate"], w["ffn1_w_up"], w["ffn1_w_down"] = _ffn1_up_gather(
            x, w["ffn1_norm"], *w["ffn1_shards"])
    else:
        n1 = _rmsnorm_fwd("ffn1_norm", x, w["ffn1_norm"], D_MODEL)
        a1, dadu1, dadg1 = _ffn_up("ffn1_up", n1, w["ffn1_w_gate"], w["ffn1_w_up"])
    h1, n2 = _ffn_down("ffn1_down", a1, w["ffn1_w_down"], x, w["mix_norm"])
    z = _w_in_fwd(n2, w["w_in"])
    qn, kvn, qf, kf, vf = _qkv_prep(z, w["q_norm"], w["kv_norm"], w["w_q_up"], w["w_kv_up"], tc, ta, tb)
    att, lse = _attn_fwd("mla_fwd", qf, kf, 0, vf, 0, MLA_HEADS, HEAD_QK, HEAD_V, MLA_SCALE, True, blk)
    pool = _pool_fwd(z, w["pool_w"], w["pool_scale"])
    s = x.shape[0]
    bm = _row_block(s)
    row = pl.BlockSpec((bm, D_MODEL), lambda i, k: (i, 0))
    half = pl.BlockSpec((bm, 512), lambda i, k: (i, 0))
    h2, n3 = _matmul(
        "w_out", (s // bm, 1),
        [(att, half, w["w_out"], pl.BlockSpec((512, D_MODEL), lambda i, k: (0, 0)), NN),
         (pool, half, w["w_out"], pl.BlockSpec((512, D_MODEL), lambda i, k: (1, 0)), NN)],
        [(h1, row)] + _residual_outs(s, bm, w["xattn_norm"])[0], _residual_outs(s, bm, w["xattn_norm"])[1],
        _residual_epilogue(1.0, True), None)
    memn = _rmsnorm_fwd("mem_norm", mem, w["mem_norm"], D_MODEL)
    qm = _mm_nn("w_mq", n3, w["w_mq"], BF16)
    kvm = _mm_heads_fwd("w_mkv", memn, w["w_mkv"], BF16)
    om, lse_m = _attn_fwd("xattn_fwd", qm, kvm, 0, kvm, MEM_HEADS, MEM_HEADS, MEM_HEAD_DIM, MEM_HEAD_DIM,
                          MEM_SCALE, False, blk)
    h3, n4 = _mm_nn("w_mo", om, w["w_mo"], F32, res=h2, gain=w["ffn2_norm"])
    a2, dadu2, dadg2 = _ffn_up("ffn2_up", n4, w["ffn2_w_gate"], w["ffn2_w_up"])
    dh4, dh4b, loss_vec, d_final = _ffn_down("ffn2_down", a2, w["ffn2_w_down"], h3, loss=(w["final_norm"], target))
    grads["final_norm"] = d_final

    dh3, dh3b, grads["ffn2_norm"] = _ffn_bwd("ffn2", dh4b, n4, dadg2, dadu2, a2, w["ffn2_w_gate"], w["ffn2_w_up"],
                                             w["ffn2_w_down"], grads, norm_bwd=(h3, w["ffn2_norm"], dh4))

    dom, delta_m = _mm_nt("w_mo_dx", dh3b, w["w_mo"], BF16, attn_out=om, nh=MEM_HEADS, dv=MEM_HEAD_DIM)
    grads["w_mo"] = _mm_tn("w_mo_dw", om, dh3b)
    dqm, dkm, dvm = _attn_bwd("xattn_bwd", qm, kvm, 0, kvm, MEM_HEADS, dom, 0, lse_m, delta_m, MEM_HEADS,
                              MEM_HEAD_DIM, MEM_HEAD_DIM, MEM_SCALE, False, blk)
    dkvm = jnp.concatenate([dkm, dvm], axis=1).astype(BF16)
    dh2, dh2b, grads["xattn_norm"] = _mm_nt_norm_bwd("w_mq_dx", dqm, w["w_mq"], h2, w["xattn_norm"], dh3)
    grads["w_mq"] = _mm_tn("w_mq_dw", n3, dqm)
    dmemn, grads["w_mkv"] = _mm_heads_bwd("w_mkv", dkvm, memn, w["w_mkv"])
    _, grads["mem_norm"] = _rmsnorm_bwd("mem_norm_bwd", dmemn, mem, w["mem_norm"], D_MODEL, out_dtype=BF16)

    dcat, delta = _mm_nt("w_out_dx", dh2b, w["w_out"], BF16, attn_out=att, nh=MLA_HEADS, dv=HEAD_V)
    grads["w_out"] = jnp.concatenate([_mm_tn("w_out_dw_a", att, dh2b), _mm_tn("w_out_dw_p", pool, dh2b)], axis=0)
    dzp, grads["pool_w"], grads["pool_scale"] = _pool_bwd(dcat, z, w["pool_w"], w["pool_scale"])
    dqf, dkf, dvf = _attn_bwd("mla_bwd", qf, kf, 0, vf, 0, dcat, 0, lse, delta, MLA_HEADS, HEAD_QK, HEAD_V,
                              MLA_SCALE, True, blk)
    dz_lat, grads["q_norm"], grads["kv_norm"], grads["w_q_up"], grads["w_kv_up"] = _qkv_prep_bwd(
        dqf, dkf, dvf, z, qn, kvn, w["q_norm"], w["kv_norm"], w["w_q_up"], w["w_kv_up"], tc, ta, tb)
    dz = jnp.concatenate([dz_lat, dzp.astype(BF16)], axis=1)
    grads["w_in"] = _w_in_dw(dz, n2)
    dh1, dh1b, grads["mix_norm"] = _w_in_dx_norm_bwd(dz, w["w_in"], h1, w["mix_norm"], dh2)

    dn1 = _ffn_bwd("ffn1", dh1b, n1, dadg1, dadu1, a1, w["ffn1_w_gate"], w["ffn1_w_up"], w["ffn1_w_down"], grads)
    dx, grads["ffn1_norm"], _ = _rmsnorm_bwd("ffn1_norm_bwd", dn1, x, w["ffn1_norm"], D_MODEL, dres=dh1)
    return loss_vec[0, 0], dx


def _mesh_pos():
    x, y, c = lax.axis_index("x"), lax.axis_index("y"), lax.axis_index("c")
    chips = [(1 - x, y), (x, 1 - y), (1 - x, 1 - y)]
    chip_ids = [2 * cx + cy for cx, cy in chips]
    return x, y, c, 2 * x + y, chips, chip_ids


def _half_rows(c, rows):
    hr = rows // 2
    return pl.ds(pl.multiple_of(c * hr, 16), hr), pl.ds(pl.multiple_of((1 - c) * hr, 16), hr)


def _ag_ici_stage(shards):
    n = len(shards)

    def copies(ins, outs):
        x, y, c, me, chips, _ = _mesh_pos()
        out = []
        for k in range(n):
            mine, _ = _half_rows(c, ins[k].shape[0])
            out.append((ins[k], outs[k].at[me], None))
            for cx, cy in chips:
                out.append((ins[k].at[mine], outs[k].at[me, mine], (cx, cy, c)))
        return out

    return _Stage(shards, [jax.ShapeDtypeStruct((N_CHIPS,) + s.shape, s.dtype) for s in shards], 3 * n, n, copies)


def _quarter_rows(c, rows):
    qr = rows // 4
    return pl.ds(pl.multiple_of(c * 2 * qr, 16), qr), pl.ds(pl.multiple_of(c * 2 * qr + qr, 16), qr)


def _ag_d2d_stage(fulls):
    n = len(fulls)

    def copies(ins, outs):
        x, y, c, me, _, chip_ids = _mesh_pos()
        out = []
        for k in range(n):
            mine, _ = _half_rows(c, ins[k].shape[1])
            for j in range(3):
                out.append((ins[k].at[chip_ids[j], mine], outs[k].at[chip_ids[j], mine], (x, y, 1 - c)))
        return out

    return _Stage(fulls, [jax.ShapeDtypeStruct(f.shape, f.dtype) for f in fulls], 3 * n, 0, copies,
                  aliases={k: k for k in range(n)})


def _rs_swap_stage(grads):
    n = len(grads)

    def copies(ins, outs):
        x, y, c, _, _, _ = _mesh_pos()
        out = []
        for k in range(n):
            _, other = _half_rows(c, ins[k].shape[1])
            out.append((ins[k].at[:, other, :], outs[k], (x, y, 1 - c)))
        return out

    return _Stage(grads, [jax.ShapeDtypeStruct((N_CHIPS, g.shape[1] // 2, g.shape[2]), g.dtype) for g in grads],
                  n, 0, copies)


_REL_OF_PEER = (2, 1, 3)


def _rs_scatter_stage(sums, relative=False):
    n = len(sums)

    def copies(ins, outs):
        x, y, c, me, chips, chip_ids = _mesh_pos()
        out = []
        for k in range(n):
            mine, _ = _half_rows(c, 2 * ins[k].shape[1])
            out.append((ins[k].at[0 if relative else me], outs[k].at[0, mine, :], None))
            for j, (cx, cy) in enumerate(chips):
                slab = _REL_OF_PEER[j] if relative else chip_ids[j]
                out.append((ins[k].at[slab], outs[k].at[1 + j, mine, :], (cx, cy, c)))
        return out

    return _Stage(sums, [jax.ShapeDtypeStruct((N_CHIPS, 2 * s.shape[1], s.shape[2]), s.dtype) for s in sums],
                  3 * n, n, copies)


def _rs_mirror_stage(parts):
    n = len(parts)

    def copies(ins, outs):
        x, y, c, _, _, _ = _mesh_pos()
        out = []
        for k in range(n):
            mine, _ = _half_rows(c, ins[k].shape[1])
            out.append((ins[k].at[:, mine, :], outs[k].at[:, mine, :], (x, y, 1 - c)))
        return out

    return _Stage(parts, [jax.ShapeDtypeStruct(p.shape, p.dtype) for p in parts], n, 0, copies,
                  aliases={k: k for k in range(n)})


def _pair_add(name, gs, r1s, core):
    n = len(gs)

    def body(c_ref, *refs):
        for k in range(n):
            g_ref, r_ref, o_ref = refs[k], refs[n + k], refs[2 * n + k]
            o_ref[...] = (g_ref[...].astype(F32) + r_ref[...].astype(F32)).astype(BF16)

    def half(g):
        return pl.BlockSpec((None, g.shape[1] // 2, g.shape[2]), lambda j, c: (j, 0, 0))

    def mine(g):
        return pl.BlockSpec((None, g.shape[1] // 2, g.shape[2]), lambda j, c: (j, c[0], 0))

    return pl.pallas_call(
        body, name=name,
        grid_spec=pltpu.PrefetchScalarGridSpec(
            num_scalar_prefetch=1, grid=(N_CHIPS,),
            in_specs=[mine(g) for g in gs] + [half(g) for g in gs], out_specs=[half(g) for g in gs]),
        out_shape=[jax.ShapeDtypeStruct((N_CHIPS, g.shape[1] // 2, g.shape[2]), BF16) for g in gs],
        compiler_params=_params("parallel"),
    )(core, *gs, *r1s)


def _adamw_math(w, g, m, v):
    m = ADAM_B1 * m + (1.0 - ADAM_B1) * g
    v = ADAM_B2 * v + (1.0 - ADAM_B2) * (g * g)
    m_hat = m / (1.0 - ADAM_B1 ** ADAM_STEP)
    v_hat = v / (1.0 - ADAM_B2 ** ADAM_STEP)
    delta = -ADAM_LR * (m_hat / (jnp.sqrt(v_hat) + ADAM_EPS) + ADAM_WD * w)
    return delta, m, v


def _adamw_sum(name, ws, parts, ms, vs):
    n = len(ws)
    r, c = ws[0].shape
    assert all(w.shape == (r, c) for w in ws)
    br = r
    while br * c * 4 > (1 << 20) and br % 32 == 0:
        br //= 2

    def body(*refs):
        for k in range(n):
            w_ref, p_ref, m_ref, v_ref = refs[4 * k:4 * k + 4]
            g_ref, d_ref, nm_ref, nv_ref = refs[4 * n + 4 * k:4 * n + 4 * k + 4]
            g = p_ref[0].astype(F32)
            for j in range(1, N_CHIPS):
                g = g + p_ref[j].astype(F32)
            d, nm, nv = _adamw_math(w_ref[...], g, m_ref[...], v_ref[...])
            g_ref[...] = g
            d_ref[...] = d
            nm_ref[...] = nm
            nv_ref[...] = nv

    spec = pl.BlockSpec((br, c), lambda i: (i, 0))
    shp = jax.ShapeDtypeStruct((r, c), F32)
    args = [a for k in range(n) for a in (ws[k], parts[k], ms[k], vs[k])]
    res = _call(
        body, name=name, grid=(r // br,),
        in_specs=[spec, pl.BlockSpec((N_CHIPS, br, c), lambda i: (0, i, 0)), spec, spec] * n,
        out_specs=[spec] * (4 * n), out_shape=[shp] * (4 * n), sem=("parallel",), args=args)
    return [res[4 * k:4 * k + 4] for k in range(n)]


_SMALL_VECTORS = ("ffn1_norm", "mix_norm", "xattn_norm", "mem_norm", "ffn2_norm", "final_norm", "q_norm",
                  "kv_norm", "pool_scale")


_LOSS_ROW = 9
_VEC_ROWS = 16
_POOL_ROWS = POOL_GROUPS * POOL_CH


def _small_params_step(g, w, m, v, loss_local):
    names = list(_SMALL_VECTORS) + ["pool_w"]
    nv = len(_SMALL_VECTORS)
    widths = [g[n].shape[1] for n in _SMALL_VECTORS]
    shapes = {"vec": (_VEC_ROWS, D_MODEL), "pool": (_POOL_ROWS, POOL_CH)}

    def body(*refs):
        ins = refs[:4 * (nv + 1) + 1]
        outs = refs[len(ins):len(ins) + 4 * (nv + 1) + 1]
        vec_own, vec_sib, vec_all, pool_sib, pool_sum, pool_all, send, recv = refs[len(ins) + len(outs):]
        g_in, w_in, m_in, v_in = (ins[k * (nv + 1):(k + 1) * (nv + 1)] for k in range(4))
        loss_in = ins[-1]
        g_out, d_out, m_out, v_out = (outs[k * (nv + 1):(k + 1) * (nv + 1)] for k in range(4))
        loss_out = outs[-1]
        x, y, c, me, chips, chip_ids = _mesh_pos()
        sib = (x, y, 1 - c)

        def remote(src, dst, k, dev):
            return pltpu.make_async_remote_copy(src_ref=src, dst_ref=dst, send_sem=send.at[k], recv_sem=recv.at[k],
                                                device_id=dev, device_id_type=_MESH)

        vec_own[...] = jnp.zeros_like(vec_own)
        for i in range(nv):
            vec_own[i:i + 1, 0:widths[i]] = g_in[i][...]
        vec_own[_LOSS_ROW:_LOSS_ROW + 1, 0:128] = loss_in[...]
        swaps = [remote(vec_own, vec_sib, 0, sib), remote(g_in[nv], pool_sib, 1, sib)]
        for cp in swaps:
            cp.start()
        for cp in swaps:
            cp.wait()
        vec_all[me] = vec_own[...] + vec_sib[...]
        pool_sum[...] = g_in[nv][...] + pool_sib[...]
        pool_all[me] = pool_sum[...]
        hv, hp = _VEC_ROWS // 2, _POOL_ROWS // 2
        mine_v = pl.ds(pl.multiple_of(c * hv, 8), hv)
        mine_p = pl.ds(pl.multiple_of(c * hp, 8), hp)
        sends = []
        for j, (cx, cy) in enumerate(chips):
            sends.append(remote(vec_all.at[me, mine_v], vec_all.at[me, mine_v], 2 + j, (cx, cy, c)))
            sends.append(remote(pool_sum.at[mine_p], pool_all.at[me, mine_p], 5 + j, (cx, cy, c)))
        for cp in sends:
            cp.start()
        for cp in sends:
            cp.wait()
        mirrors = []
        for j in range(3):
            mirrors.append(remote(vec_all.at[chip_ids[j], mine_v], vec_all.at[chip_ids[j], mine_v], 8 + j, sib))
            mirrors.append(remote(pool_all.at[chip_ids[j], mine_p], pool_all.at[chip_ids[j], mine_p], 11 + j, sib))
        for cp in mirrors:
            cp.start()
        for cp in mirrors:
            cp.wait()
        vec_tot = vec_all[0]
        pool_tot = pool_all[0]
        for i in range(1, N_CHIPS):
            vec_tot = vec_tot + vec_all[i]
            pool_tot = pool_tot + pool_all[i]
        vec_sib[...] = vec_tot
        loss_out[...] = vec_sib[_LOSS_ROW:_LOSS_ROW + 1, 0:128]
        for i in range(nv + 1):
            gi = pool_tot if i == nv else vec_sib[i:i + 1, 0:widths[i]]
            d, nm, nvv = _adamw_math(w_in[i][...], gi, m_in[i][...], v_in[i][...])
            g_out[i][...] = gi
            d_out[i][...] = d
            m_out[i][...] = nm
            v_out[i][...] = nvv

    vm = pl.BlockSpec(memory_space=pltpu.VMEM)
    args = [d[n] for d in (g, w, m, v) for n in names] + [jnp.broadcast_to(loss_local.reshape(1, 1), (1, 128))]
    out_shape = [jax.ShapeDtypeStruct(g[n].shape, F32) for _ in range(4) for n in names]
    out_shape.append(jax.ShapeDtypeStruct((1, 128), F32))
    res = pl.pallas_call(
        body, name="small_params_step", in_specs=[vm] * len(args), out_specs=[vm] * len(out_shape),
        out_shape=out_shape,
        scratch_shapes=[pltpu.VMEM(shapes["vec"], F32), pltpu.VMEM(shapes["vec"], F32),
                        pltpu.VMEM((N_CHIPS,) + shapes["vec"], F32), pltpu.VMEM(shapes["pool"], F32),
                        pltpu.VMEM(shapes["pool"], F32), pltpu.VMEM((N_CHIPS,) + shapes["pool"], F32),
                        pltpu.SemaphoreType.DMA((14,)), pltpu.SemaphoreType.DMA((14,))],
        compiler_params=pltpu.CompilerParams(vmem_limit_bytes=V7X_VMEM_LIMIT_BYTES),
    )(*args)
    k = len(names)
    dicts = [dict(zip(names, res[i * k:(i + 1) * k])) for i in range(4)]
    return dicts[0], dicts[1], dicts[2], dicts[3], res[-1]


_WEIGHTS = ("ffn1_norm", "ffn1_w_gate", "ffn1_w_up", "ffn1_w_down", "mix_norm", "w_in", "q_norm", "w_q_up",
            "kv_norm", "w_kv_up", "pool_w", "pool_scale", "w_out", "xattn_norm", "mem_norm", "w_mq", "w_mkv",
            "w_mo", "ffn2_norm", "ffn2_w_gate", "ffn2_w_up", "ffn2_w_down", "final_norm")
_SHARDED = ("ffn1_w_gate", "ffn1_w_up", "ffn1_w_down", "w_in", "w_q_up", "w_kv_up", "w_out", "w_mq", "w_mkv",
            "w_mo", "ffn2_w_gate", "ffn2_w_up", "ffn2_w_down")
W_IN_SPLIT = Q_LORA + KV_LORA + ROPE_DIM


_FFN1 = ("ffn1_w_gate", "ffn1_w_up", "ffn1_w_down")
_TRANSPOSED = ("ffn1_w_gate", "ffn1_w_up", "ffn2_w_gate", "ffn2_w_up", "w_in", "w_q_up")


def _local_view(name, a):
    return jnp.swapaxes(a, 1, 2)[0] if name in _TRANSPOSED else a[0]


def _global_view(name, a):
    return jnp.swapaxes(a[None], 1, 2) if name in _TRANSPOSED else a[None]


def _pad_shard(name, a):
    if name == "w_in":
        return jnp.concatenate([a[:W_IN_SPLIT], jnp.zeros((64, a.shape[1]), a.dtype), a[W_IN_SPLIT:]], axis=0)
    if name == "w_q_up":
        return jnp.pad(a, ((0, 64), (0, 0)))
    return a


def _unpad_shard(name, a):
    if name == "w_in":
        return jnp.concatenate([a[:, :W_IN_SPLIT], a[:, W_IN_SPLIT + 64:]], axis=1)
    if name == "w_q_up":
        return a[:, :192]
    return a


def _stacked(g):
    return g if g.ndim == 3 else g.reshape(N_CHIPS, g.shape[0] // N_CHIPS, g.shape[1])


class _Plan:
    AG_UNITS = (
        (("w_in", "w_q_up", "w_kv_up"), "ffn1_up"),
        (("w_out",), "w_in"),
        (("w_mq",), "qkv_prep"),
        (("w_mkv", "w_mo", "ffn2_w_gate"), "mla_fwd"),
        (("ffn2_w_up",), "xattn_fwd"),
        (("ffn2_w_down",), "ffn2_up"),
    )
    RS_UNITS = (
        (("ffn2_w_gate", "ffn2_w_up", "ffn2_w_down"), "ffn2_dn_a", "mla_bwd", "qkv_prep_bwd"),
        (("w_mo", "w_mq", "w_mkv"), "w_out_dx", "mla_bwd", "qkv_prep_bwd"),
        (("w_out", "w_q_up", "w_kv_up", "w_in"), "w_in_dx", "ffn1_dact", "ffn1_dwd"),
        (("ffn1_w_down",), "ffn1_dwg", "ffn1_dwu", "ffn1_dn_a"),
        (("ffn1_w_gate",), "ffn1_dwu", "ffn1_dn_a", "ffn1_dn_b"),
        (("ffn1_w_up",), "ffn1_dn_a", "ffn1_dn_b", "adamw_w_kv_up"),
    )
    ADAMW_ORDER = (("w_kv_up",), ("ffn2_w_gate", "ffn2_w_up"), ("ffn2_w_down", "ffn1_w_down"), ("w_mo", "w_mq", "w_out"),
                   ("w_mkv",), ("w_q_up",), ("w_in",), ("ffn1_w_gate", "ffn1_w_up"))

    def __init__(self, shards, w, grads, core):
        self.shards, self.w, self.grads, self.core = shards, w, grads, core
        self.last_slab_step = 0
        self.parts = {}
        self.ag = [None for _ in self.AG_UNITS]
        self.rs = [[None, None, None, None] for _ in self.RS_UNITS]

    def pre(self, name):
        for i, (names, host) in enumerate(self.AG_UNITS):
            if name == host:
                st = _ag_ici_stage([self.shards[n] for n in names])
                st.then = _ag_d2d_stage(st.outs)
                st.start_step = self.last_slab_step if name == "ffn1_up" else 0
                self.ag[i] = _host(name, st)
        for i, (names, h1, h2, h3) in enumerate(self.RS_UNITS):
            if name == h1:
                self.rs[i][0] = _host(name, _rs_swap_stage([_stacked(self.grads[n]) for n in names]))
            if name == h2:
                self.rs[i][2] = _host(name, _rs_scatter_stage(self.rs[i][1], relative=names[0] in _FFN1))
            if name == h3:
                self.rs[i][3] = _host(name, _rs_mirror_stage(self.rs[i][2].results))

    def post(self, name):
        for i, (names, host) in enumerate(self.AG_UNITS):
            if name == host:
                for n, f in zip(names, self.ag[i].results):
                    self.w[n] = _full_weight(n, f)
        for i, (names, h1, h2, h3) in enumerate(self.RS_UNITS):
            if name == h1:
                self.rs[i][1] = list(_pair_add("pair_add_" + names[0], [_stacked(self.grads[n]) for n in names],
                                               self.rs[i][0].results, self.core))
            if name == h3:
                for n, p in zip(names, self.rs[i][3].results):
                    self.parts[n] = p


def _full_weight(name, stacked):
    if name in ("w_out", "w_mq", "w_mo"):
        return stacked.reshape(D_MODEL, D_MODEL)
    return stacked


def kernel(x, mem, positions, ffn1_norm, ffn1_w_gate, ffn1_w_up, ffn1_w_down, mix_norm, w_in, q_norm, w_q_up, kv_norm, w_kv_up, pool_w, pool_scale, w_out, xattn_norm, mem_norm, w_mq, w_mkv, w_mo, ffn2_norm, ffn2_w_gate, ffn2_w_up, ffn2_w_down, final_norm, loss_target, m_ffn1_norm, m_ffn1_w_gate, m_ffn1_w_up, m_ffn1_w_down, m_mix_norm, m_w_in, m_q_norm, m_w_q_up, m_kv_norm, m_w_kv_up, m_pool_w, m_pool_scale, m_w_out, m_xattn_norm, m_mem_norm, m_w_mq, m_w_mkv, m_w_mo, m_ffn2_norm, m_ffn2_w_gate, m_ffn2_w_up, m_ffn2_w_down, m_final_norm, v_ffn1_norm, v_ffn1_w_gate, v_ffn1_w_up, v_ffn1_w_down, v_mix_norm, v_w_in, v_q_norm, v_w_q_up, v_kv_norm, v_w_kv_up, v_pool_w, v_pool_scale, v_w_out, v_xattn_norm, v_mem_norm, v_w_mq, v_w_mkv, v_w_mo, v_ffn2_norm, v_ffn2_w_gate, v_ffn2_w_up, v_ffn2_w_down, v_final_norm):
    wts = dict(zip(_WEIGHTS, (ffn1_norm, ffn1_w_gate, ffn1_w_up, ffn1_w_down, mix_norm, w_in, q_norm, w_q_up, kv_norm, w_kv_up, pool_w, pool_scale, w_out, xattn_norm, mem_norm, w_mq, w_mkv, w_mo, ffn2_norm, ffn2_w_gate, ffn2_w_up, ffn2_w_down, final_norm)))
    mom = dict(zip(_WEIGHTS, (m_ffn1_norm, m_ffn1_w_gate, m_ffn1_w_up, m_ffn1_w_down, m_mix_norm, m_w_in, m_q_norm, m_w_q_up, m_kv_norm, m_w_kv_up, m_pool_w, m_pool_scale, m_w_out, m_xattn_norm, m_mem_norm, m_w_mq, m_w_mkv, m_w_mo, m_ffn2_norm, m_ffn2_w_gate, m_ffn2_w_up, m_ffn2_w_down, m_final_norm)))
    var = dict(zip(_WEIGHTS, (v_ffn1_norm, v_ffn1_w_gate, v_ffn1_w_up, v_ffn1_w_down, v_mix_norm, v_w_in, v_q_norm, v_w_q_up, v_kv_norm, v_w_kv_up, v_pool_w, v_pool_scale, v_w_out, v_xattn_norm, v_mem_norm, v_w_mq, v_w_mkv, v_w_mo, v_ffn2_norm, v_ffn2_w_gate, v_ffn2_w_up, v_ffn2_w_down, v_final_norm)))
    small = [n for n in _WEIGHTS if n not in _SHARDED]

    global _PLAN
    shards = {n: _pad_shard(n, _local_view(n, wts[n])).astype(BF16) for n in _SHARDED}
    w = {n: wts[n].reshape(1, -1) for n in _SMALL_VECTORS}
    w["pool_w"] = pool_w[0].astype(BF16)
    grads = {}
    core = lax.axis_index("c").astype(jnp.int32).reshape(1)
    plan = _Plan(shards, w, grads, core)
    _PLAN = plan
    try:
        w["ffn1_shards"] = tuple(shards[n] for n in _FFN1)

        loss_local, dx = _local_step(x[0], mem[0], positions[0], loss_target[0], w, grads)

        def small_view(d):
            out = {n: d[n].reshape(1, -1) for n in _SMALL_VECTORS}
            out["pool_w"] = d["pool_w"].reshape(_POOL_ROWS, POOL_CH)
            return out

        *small_res, loss_vec = _small_params_step(small_view(grads), small_view(wts), small_view(mom),
                                                  small_view(var), loss_local)
        g_out, d_out, m_out, v_out = ({n: r[n].reshape(wts[n].shape) for n in small} for r in small_res)
        loss = loss_vec[0, 0]

        for names in _Plan.ADAMW_ORDER:
            res = _adamw_sum("adamw_" + names[0], [_local_view(n, wts[n]) for n in names],
                             [_unpad_shard(n, plan.parts[n]) for n in names],
                             [_local_view(n, mom[n]) for n in names], [_local_view(n, var[n]) for n in names])
            for n, r4 in zip(names, res):
                g_out[n], d_out[n], m_out[n], v_out[n] = (_global_view(n, r) for r in r4)
    finally:
        _PLAN = None
        _PENDING.clear()

    return (loss, dx[None], *[g_out[n] for n in _WEIGHTS], *[d_out[n] for n in _WEIGHTS],
            *[m_out[n] for n in _WEIGHTS], *[v_out[n] for n in _WEIGHTS])
```

```python
import jax
import jax.numpy as jnp
from jax import lax
from jax.experimental import pallas as pl
from jax.experimental.pallas import tpu as pltpu

F32 = jnp.float32
BF16 = jnp.bfloat16

D_MODEL = 1024
D_FF = 2816
N_CHIPS = 4
FF_SHARD = D_FF // N_CHIPS
MLA_HEADS = 4
Q_LORA = 256
KV_LORA = 128
ROPE_DIM = 64
HEAD_QK = 256
HEAD_V = 128
POOL_GROUPS = 4
POOL_CH = 128
MEM_HEADS = 4
MEM_HEAD_DIM = 256
RMS_EPS = 1e-6
ROPE_BASE = 10000.0
MLA_SCALE = (128 + 64) ** -0.5
MEM_SCALE = MEM_HEAD_DIM ** -0.5

ADAM_LR = 0.001
ADAM_B1 = 0.9
ADAM_B2 = 0.999
ADAM_EPS = 1e-08
ADAM_WD = 0.01
ADAM_STEP = 10

V7X_VMEM_LIMIT_BYTES = 56 * 1024 * 1024

NN = ((1,), (0,))
NT = ((1,), (1,))
TN = ((0,), (0,))


def _params(*sem):
    return pltpu.CompilerParams(dimension_semantics=sem, vmem_limit_bytes=V7X_VMEM_LIMIT_BYTES)


_MESH = pl.DeviceIdType.MESH
_ANY = pl.BlockSpec(memory_space=pl.ANY)


class _Stage:
    def __init__(self, ins, outs, n_remote, n_local, copies, aliases=None):
        self.ins, self.outs, self.n_remote, self.n_local = list(ins), list(outs), n_remote, n_local
        self.copies, self.aliases = copies, dict(aliases or {})
        self.results = None
        self.start_step = 0
        self.then = None

    def descriptors(self, in_refs, out_refs, send, recv, loc):
        ds, ri, li = [], 0, 0
        for src, dst, dev in self.copies(in_refs, out_refs):
            if dev is None:
                ds.append(pltpu.make_async_copy(src, dst, loc.at[li]))
                li += 1
            else:
                ds.append(pltpu.make_async_remote_copy(src_ref=src, dst_ref=dst, send_sem=send.at[ri],
                                                       recv_sem=recv.at[ri], device_id=dev, device_id_type=_MESH))
                ri += 1
        assert ri == self.n_remote and li == self.n_local
        return ds


_PENDING = {}


def _host(name, stage):
    _PENDING.setdefault(name, []).append(stage)
    return stage


_PLAN = None


def _call(body, **kw):
    if _PLAN is not None:
        _PLAN.pre(kw["name"])
    res = _call_hosting(body, **kw)
    if _PLAN is not None:
        _PLAN.post(kw["name"])
    return res


def _call_hosting(body, *, name, grid, in_specs, out_specs, out_shape, sem, args, scratch_shapes=(), aliases=None):
    stages = _PENDING.pop(name, [])
    scratch_shapes = list(scratch_shapes)
    if not stages:
        return pl.pallas_call(body, name=name, grid=grid, in_specs=in_specs, out_specs=out_specs,
                              out_shape=out_shape, scratch_shapes=scratch_shapes,
                              input_output_aliases=dict(aliases or {}), compiler_params=_params(*sem))(*args)
    ni, no, ns = len(in_specs), len(out_shape), len(scratch_shapes)
    c_ins = [a for st in stages for a in st.ins]
    c_outs = [o for st in stages for o in st.outs]
    nci, nco = len(c_ins), len(c_outs)
    aliases, io, oo = dict(aliases or {}), 0, 0
    for st in stages:
        for i, j in st.aliases.items():
            aliases[ni + io + i] = no + oo + j
        io += len(st.ins)
        oo += len(st.outs)
    dma = pltpu.SemaphoreType.DMA
    sems = []
    for st in stages:
        sems += [dma((max(st.n_remote, 1),)), dma((max(st.n_remote, 1),)), dma((max(st.n_local, 1),))]
    followers = [st.then for st in stages if st.then is not None]
    for st in followers:
        sems += [dma((max(st.n_remote, 1),)), dma((max(st.n_remote, 1),)), dma((max(st.n_local, 1),))]

    def wrapped(*refs):
        ins, cin = refs[:ni], refs[ni:ni + nci]
        outs, cout = refs[ni + nci:ni + nci + no], refs[ni + nci + no:ni + nci + no + nco]
        scr = refs[ni + nci + no + nco:ni + nci + no + nco + ns]
        sem_refs = refs[ni + nci + no + nco + ns:]
        step = pl.program_id(0)
        last = pl.program_id(0) == grid[0] - 1
        for ax in range(1, len(grid)):
            step = step * grid[ax] + pl.program_id(ax)
            last = jnp.logical_and(last, pl.program_id(ax) == grid[ax] - 1)

        def descriptors(si):
            io = sum(len(st.ins) for st in stages[:si])
            oo = sum(len(st.outs) for st in stages[:si])
            st = stages[si]
            return st.descriptors(cin[io:io + len(st.ins)], cout[oo:oo + len(st.outs)], *sem_refs[3 * si:3 * si + 3])

        def follower_descriptors(fi):
            si = [k for k, st in enumerate(stages) if st.then is not None][fi]
            oo = sum(len(st.outs) for st in stages[:si])
            bufs = cout[oo:oo + len(stages[si].outs)]
            k0 = 3 * (len(stages) + fi)
            return followers[fi].descriptors(bufs, bufs, *sem_refs[k0:k0 + 3])

        def start(si):
            @pl.when(step == stages[si].start_step)
            def _():
                for d in descriptors(si):
                    d.start()

        for si, st in enumerate(stages):
            if st.start_step == 0:
                start(si)
        body(*ins, *outs, *scr)
        for si, st in enumerate(stages):
            if st.start_step != 0:
                start(si)

        @pl.when(last)
        def _():
            for si in range(len(stages)):
                for d in descriptors(si):
                    d.wait()
            for fi in range(len(followers)):
                for d in follower_descriptors(fi):
                    d.start()
            for fi in range(len(followers)):
                for d in follower_descriptors(fi):
                    d.wait()

    res = pl.pallas_call(
        wrapped, name=name, grid=grid, in_specs=list(in_specs) + [_ANY] * nci,
        out_specs=list(out_specs) + [_ANY] * nco, out_shape=list(out_shape) + c_outs,
        scratch_shapes=scratch_shapes + sems, input_output_aliases=aliases,
        compiler_params=_params(*(("arbitrary",) * len(grid))))(*args, *c_ins)
    oo = no
    for st in stages:
        st.results = list(res[oo:oo + len(st.outs)])
        oo += len(st.outs)
    return list(res[:no])


def _dot(a, b, dims):
    return lax.dot_general(a.astype(BF16), b.astype(BF16), (dims, ((), ())), preferred_element_type=F32)


_MAX_ROW_BLOCK = 1024
_ATT_BLOCK = 512


_MAX_REDUCE_BLOCK = 2048


def _row_block(s, want=1024):
    return min(want, s, _MAX_ROW_BLOCK)


def _reduce_block(s):
    return min(s, _MAX_REDUCE_BLOCK)


def _matmul(name, grid, terms, extras, outs, epilogue, acc_shape, fill=(), summed=()):
    nt, ne, no, nf = len(terms), len(extras), len(outs), len(fill)
    nk = grid[-1]
    dims = [t[4] for t in terms]

    def body(*refs):
        a_refs, b_refs = refs[:nt], refs[nt:2 * nt]
        e_refs = refs[2 * nt:2 * nt + ne]
        o_refs = refs[2 * nt + ne + nf:2 * nt + ne + nf + no]

        def finish(acc):
            vals = epilogue(acc, *[e[...] for e in e_refs])
            for idx, (o, val) in enumerate(zip(o_refs, vals)):
                if idx in summed:
                    @pl.when(pl.program_id(0) == 0)
                    def _(o=o, val=val):
                        o[...] = val.astype(o.dtype)

                    @pl.when(pl.program_id(0) > 0)
                    def _(o=o, val=val):
                        o[...] += val.astype(o.dtype)
                else:
                    o[...] = val.astype(o.dtype)

        if nk == 1:
            part = None
            for a, b, d in zip(a_refs, b_refs, dims):
                t = _dot(a[...], b[...], d)
                part = t if part is None else part + t
            finish(part)
        else:
            acc_ref = refs[-1]
            k = pl.program_id(len(grid) - 1)

            @pl.when(k == 0)
            def _():
                acc_ref[...] = jnp.zeros_like(acc_ref)

            for a, b, d in zip(a_refs, b_refs, dims):
                acc_ref[...] += _dot(a[...], b[...], d)

            @pl.when(k == nk - 1)
            def _():
                finish(acc_ref[...])

    in_specs = [t[1] for t in terms] + [t[3] for t in terms] + [e[1] for e in extras] + [_ANY] * nf
    args = [t[0] for t in terms] + [t[2] for t in terms] + [e[0] for e in extras] + list(fill)
    sem = ("arbitrary" if summed else "parallel",) * (len(grid) - 1) + ("arbitrary",)
    aliases = {2 * nt + ne + i: i for i in range(nf)}
    return _call(
        body, name=name, grid=grid, in_specs=in_specs,
        out_specs=[o[1] for o in outs], out_shape=[o[0] for o in outs],
        scratch_shapes=[pltpu.VMEM(acc_shape, F32)] if nk > 1 else [], sem=sem, args=args, aliases=aliases)


def _ident(acc):
    return (acc,)


def _rmsnorm_fwd(name, x, gain, width, col_block=0):
    s = x.shape[0]
    bm = _row_block(s)

    def body(x_ref, g_ref, o_ref):
        xf = x_ref[...]
        r = lax.rsqrt(jnp.mean(xf * xf, axis=-1, keepdims=True) + RMS_EPS)
        o_ref[...] = ((xf * r) * g_ref[...]).astype(o_ref.dtype)

    return pl.pallas_call(
        body, name=name, grid=(s // bm,),
        in_specs=[pl.BlockSpec((bm, width), lambda i: (i, col_block)), pl.BlockSpec((1, width), lambda i: (0, 0))],
        out_specs=pl.BlockSpec((bm, width), lambda i: (i, 0)),
        out_shape=jax.ShapeDtypeStruct((s, width), BF16),
        compiler_params=_params("parallel"),
    )(x, gain)


def _rms_bwd_math(dy, xf, g, width):
    r = lax.rsqrt(jnp.mean(xf * xf, axis=-1, keepdims=True) + RMS_EPS)
    dyg = dy * g
    dot = jnp.sum(dyg * xf, axis=-1, keepdims=True)
    dx = r * dyg - xf * ((r * r * r) * (dot * (1.0 / width)))
    dgain = jnp.sum(dy * (xf * r), axis=0, keepdims=True)
    return dx, dgain


def _rmsnorm_bwd(name, dy, x, gain, width, col_block=0, dres=None, out_dtype=F32):
    s = x.shape[0]
    bm = _row_block(s)
    has_res = dres is not None

    def body(*refs):
        if has_res:
            dy_ref, x_ref, g_ref, r_ref, dx_ref, dg_ref, dxb_ref = refs
        else:
            dy_ref, x_ref, g_ref, dx_ref, dg_ref = refs
        dx, dgain = _rms_bwd_math(dy_ref[...].astype(F32), x_ref[...], g_ref[...], width)
        if has_res:
            dx = dx + r_ref[...]
            dxb_ref[...] = dx.astype(BF16)
        dx_ref[...] = dx.astype(dx_ref.dtype)

        @pl.when(pl.program_id(0) == 0)
        def _():
            dg_ref[...] = dgain

        @pl.when(pl.program_id(0) > 0)
        def _():
            dg_ref[...] += dgain

    row = pl.BlockSpec((bm, width), lambda i: (i, 0))
    in_specs = [row, pl.BlockSpec((bm, width), lambda i: (i, col_block)), pl.BlockSpec((1, width), lambda i: (0, 0))]
    args = [dy, x, gain]
    out_specs = [row, pl.BlockSpec((1, width), lambda i: (0, 0))]
    out_shape = [jax.ShapeDtypeStruct((s, width), out_dtype), jax.ShapeDtypeStruct((1, width), F32)]
    if has_res:
        in_specs.append(row)
        args.append(dres)
        out_specs.append(row)
        out_shape.append(jax.ShapeDtypeStruct((s, width), BF16))
    return _call(body, name=name, grid=(s // bm,), in_specs=in_specs, out_specs=out_specs, out_shape=out_shape,
                 sem=("arbitrary",), args=args)


def _ffn_up(name, n, wg, wu):
    s = n.shape[0]
    bm = _row_block(s)

    def body(n_ref, wg_ref, wu_ref, a_ref, dadu_ref, dadg_ref):
        x = n_ref[...]
        g = _dot(x, wg_ref[...], NT)
        u = _dot(x, wu_ref[...], NT)
        sg = jax.nn.sigmoid(g)
        silu = g * sg
        a_ref[...] = (silu * u).astype(BF16)
        dadu_ref[...] = silu.astype(BF16)
        dadg_ref[...] = (u * (sg * (1.0 + g * (1.0 - sg)))).astype(BF16)

    w_spec = pl.BlockSpec((None, FF_SHARD, D_MODEL), lambda j, i: (j, 0, 0))
    o_spec = pl.BlockSpec((None, bm, FF_SHARD), lambda j, i: (j, i, 0))
    shp = jax.ShapeDtypeStruct((N_CHIPS, s, FF_SHARD), BF16)
    return _call(
        body, name=name, grid=(N_CHIPS, s // bm),
        in_specs=[pl.BlockSpec((bm, D_MODEL), lambda j, i: (i, 0)), w_spec, w_spec],
        out_specs=[o_spec, o_spec, o_spec], out_shape=[shp, shp, shp],
        sem=("parallel", "parallel"), args=[n, wg, wu])


def _ffn1_up_gather(xin, gain, g_sh, u_sh, d_sh):
    s = xin.shape[0]
    bm = _row_block(s)
    nrb = s // bm
    rows, cols = g_sh.shape

    def body(x_ref, gain_ref, gs, us, ds, n_ref, a_ref, dadu_ref, dadg_ref, wg, wu, wd, gbuf, ubuf,
             send, recv, qsend, qrecv, fsend, frecv, loc, ld):
        r, i = pl.program_id(0), pl.program_id(1)
        x, y, c = lax.axis_index("x"), lax.axis_index("y"), lax.axis_index("c")
        sib = (x, y, 1 - c)
        mine, _ = _half_rows(c, rows)
        quarters = _quarter_rows(c, rows)
        shards, fulls, bufs = (gs, us, ds), (wg, wu, wd), (gbuf, ubuf)

        def remote(src, dst, ssem, rsem, dev):
            return pltpu.make_async_remote_copy(src_ref=src, dst_ref=dst, send_sem=ssem, recv_sem=rsem,
                                                device_id=dev, device_id_type=_MESH)

        def peer(rel):
            return ((1 - x) if rel & 2 else x, (1 - y) if rel & 1 else y, c)

        def ici(k, rel, dev=sib):
            return remote(shards[k].at[mine], fulls[k].at[rel, mine], send.at[k, rel - 1], recv.at[k, rel - 1], dev)

        def quarter(k, which, dev=sib):
            slab, q = ((2, quarters[0]), (1, quarters[1]))[which]
            return remote(fulls[k].at[slab, q], fulls[k].at[3, q], qsend.at[k, which], qrecv.at[k, which], dev)

        def fwd(k, rel):
            return remote(fulls[k].at[rel, mine], fulls[k].at[rel, mine], fsend.at[k, rel - 1], frecv.at[k, rel - 1], sib)

        def own(k):
            return pltpu.make_async_copy(shards[k], fulls[k].at[0], loc.at[k])

        def load(slab):
            for k in (0, 1):
                pltpu.make_async_copy(shards[k] if slab == 0 else fulls[k].at[slab], bufs[k], ld.at[k]).start()
            for k in (0, 1):
                pltpu.make_async_copy(shards[k] if slab == 0 else fulls[k].at[slab], bufs[k], ld.at[k]).wait()

        def from_neighbour(ks, rel):
            for k in ks:
                ici(k, rel).wait_recv()
                fwd(k, rel).start()
                quarter(k, 0 if rel == 2 else 1, peer(1 if rel == 2 else 2)).start()
            for k in ks:
                fwd(k, rel).wait_recv()

        def from_diagonal(ks):
            for k in ks:
                quarter(k, 0).wait_recv()
                quarter(k, 1).wait_recv()
                fwd(k, 3).start()
            for k in ks:
                fwd(k, 3).wait_recv()

        @pl.when(jnp.logical_and(r == 0, i == 0))
        def _():
            for k in range(3):
                own(k).start()
            for rel in (1, 2):
                for k in (0, 1):
                    ici(k, rel, peer(rel)).start()
            load(0)

        @pl.when(jnp.logical_and(r == 1, i == 0))
        def _():
            from_neighbour((0, 1), 1)
            load(1)
            for rel in (1, 2):
                ici(2, rel, peer(rel)).start()

        @pl.when(jnp.logical_and(r == 2, i == 0))
        def _():
            from_neighbour((0, 1), 2)
            load(2)

        @pl.when(jnp.logical_and(r == 3, i == 0))
        def _():
            from_diagonal((0, 1))
            load(3)

        xv = _norm_bf16(x_ref[...], gain_ref[...])

        @pl.when(r == 0)
        def _():
            n_ref[...] = xv

        g = _dot(xv, gbuf[...], NT)
        u = _dot(xv, ubuf[...], NT)
        sg = jax.nn.sigmoid(g)
        silu = g * sg
        a_ref[...] = (silu * u).astype(BF16)
        dadu_ref[...] = silu.astype(BF16)
        dadg_ref[...] = (u * (sg * (1.0 + g * (1.0 - sg)))).astype(BF16)

        @pl.when(jnp.logical_and(r == 3, i == nrb - 1))
        def _():
            from_neighbour((2,), 1)
            from_neighbour((2,), 2)
            from_diagonal((2,))
            for k in range(3):
                for rel in (1, 2):
                    ici(k, rel).wait_send()
                for which in (0, 1):
                    quarter(k, which).wait_send()
                for rel in (1, 2, 3):
                    fwd(k, rel).wait_send()
                own(k).wait()

    o_spec = pl.BlockSpec((None, bm, FF_SHARD), lambda r, i: (r, i, 0))
    act = jax.ShapeDtypeStruct((N_CHIPS, s, FF_SHARD), BF16)
    full = jax.ShapeDtypeStruct((N_CHIPS, rows, cols), BF16)
    dma = pltpu.SemaphoreType.DMA
    if _PLAN is not None:
        _PLAN.last_slab_step = 3 * nrb
    n_spec = pl.BlockSpec((bm, D_MODEL), lambda r, i: (jnp.where(r == 0, i, nrb - 1), 0))
    return _call(
        body, name="ffn1_up", grid=(N_CHIPS, nrb),
        in_specs=[pl.BlockSpec((bm, D_MODEL), lambda r, i: (i, 0)), pl.BlockSpec((1, D_MODEL), lambda r, i: (0, 0)),
                  _ANY, _ANY, _ANY],
        out_specs=[n_spec, o_spec, o_spec, o_spec, _ANY, _ANY, _ANY],
        out_shape=[jax.ShapeDtypeStruct((s, D_MODEL), BF16), act, act, act, full, full, full],
        scratch_shapes=[pltpu.VMEM((rows, cols), BF16), pltpu.VMEM((rows, cols), BF16), dma((3, 2)), dma((3, 2)),
                        dma((3, 2)), dma((3, 2)), dma((3, 3)), dma((3, 3)), dma((3,)), dma((2,))],
        sem=("arbitrary", "arbitrary"), args=[xin, gain, g_sh, u_sh, d_sh])


def _residual_epilogue(alpha, with_norm):
    if not with_norm:
        return lambda acc, r: (r + alpha * acc,)

    def epilogue(acc, r, g):
        h = r + alpha * acc
        rs = lax.rsqrt(jnp.mean(h * h, axis=-1, keepdims=True) + RMS_EPS)
        return h, (h * rs) * g

    return epilogue


def _residual_outs(s, bm, gain):
    row = pl.BlockSpec((bm, D_MODEL), lambda i, k: (i, 0))
    outs = [(jax.ShapeDtypeStruct((s, D_MODEL), F32), row)]
    if gain is None:
        return [], outs
    return [(gain, pl.BlockSpec((1, D_MODEL), lambda i, k: (0, 0)))], outs + [(jax.ShapeDtypeStruct((s, D_MODEL), BF16), row)]


def _loss_epilogue(acc, res, g, target):
    d = acc.shape[-1]
    h = res + 0.5 * acc
    r = lax.rsqrt(jnp.mean(h * h, axis=-1, keepdims=True) + RMS_EPS)
    err = (h * r) * g - target
    part = 0.5 * jnp.sum(jnp.mean(err * err, axis=-1, keepdims=True), axis=0, keepdims=True)
    dx, dgain = _rms_bwd_math(err * (1.0 / d), h, g, d)
    return dx, dx, jnp.broadcast_to(part, (1, 128)), dgain


def _ffn_down(name, a, wd, res, gain=None, loss=None):
    s = a.shape[1]
    bm = _row_block(s, 512)
    row = pl.BlockSpec((bm, D_MODEL), lambda i, k: (i, 0))
    terms = [(a, pl.BlockSpec((None, bm, FF_SHARD), lambda i, k, j=j: (j, i, 0)),
              wd, pl.BlockSpec((None, FF_SHARD, D_MODEL), lambda i, k, j=j: (j, 0, 0)), NN) for j in range(N_CHIPS)]
    if loss is not None:
        vec = pl.BlockSpec((1, D_MODEL), lambda i, k: (0, 0))
        outs = [(jax.ShapeDtypeStruct((s, D_MODEL), F32), row), (jax.ShapeDtypeStruct((s, D_MODEL), BF16), row),
                (jax.ShapeDtypeStruct((1, 128), F32), pl.BlockSpec((1, 128), lambda i, k: (0, 0))),
                (jax.ShapeDtypeStruct((1, D_MODEL), F32), vec)]
        return _matmul(name, (s // bm, 1), terms, [(res, row), (loss[0], vec), (loss[1], row)], outs,
                       _loss_epilogue, None, summed=(2, 3))
    extras, outs = _residual_outs(s, bm, gain)
    res_out = _matmul(name, (s // bm, 1), terms, [(res, row)] + extras, outs,
                      _residual_epilogue(0.5, gain is not None), None)
    return res_out if gain is not None else res_out[0]


def _norm_bwd_epilogue(width):
    def epilogue(acc, h, g, dres):
        dx, dgain = _rms_bwd_math(acc, h, g, width)
        dx = dx + dres
        return dx, dx, dgain

    return epilogue


def _norm_bwd_operands(s, bm, h, gain, dres):
    row = pl.BlockSpec((bm, D_MODEL), lambda i, k: (i, 0))
    vec = pl.BlockSpec((1, D_MODEL), lambda i, k: (0, 0))
    extras = [(h, row), (gain, vec), (dres, row)]
    outs = [(jax.ShapeDtypeStruct((s, D_MODEL), F32), row), (jax.ShapeDtypeStruct((s, D_MODEL), BF16), row),
            (jax.ShapeDtypeStruct((1, D_MODEL), F32), vec)]
    return extras, outs, (2,)


def _ffn_bwd(tag, dh, n, dadg, dadu, a, wg, wu, wd, grads, norm_bwd=None):
    s = dh.shape[0]
    bm = _row_block(s)
    bk = _reduce_block(s)
    nk = s // bk

    def act_bwd(acc, dg_da, du_da):
        da = 0.5 * acc
        return da * dg_da.astype(F32), da * du_da.astype(F32)

    slab = pl.BlockSpec((None, bm, FF_SHARD), lambda j, i, k: (j, i, 0))
    shp = jax.ShapeDtypeStruct((N_CHIPS, s, FF_SHARD), BF16)
    dg, du = _matmul(
        tag + "_dact", (N_CHIPS, s // bm, 1),
        [(dh, pl.BlockSpec((bm, D_MODEL), lambda j, i, k: (i, 0)),
          wd, pl.BlockSpec((None, FF_SHARD, D_MODEL), lambda j, i, k: (j, 0, 0)), NT)],
        [(dadg, slab), (dadu, slab)], [(shp, slab), (shp, slab)], act_bwd, None)

    grads[tag + "_w_down"] = _matmul(
        tag + "_dwd", (N_CHIPS, nk),
        [(a, pl.BlockSpec((None, bk, FF_SHARD), lambda j, k: (j, k, 0)),
          dh, pl.BlockSpec((bk, D_MODEL), lambda j, k: (k, 0)), TN)],
        [], [(jax.ShapeDtypeStruct((N_CHIPS, FF_SHARD, D_MODEL), BF16),
              pl.BlockSpec((None, FF_SHARD, D_MODEL), lambda j, k: (j, 0, 0)))],
        lambda acc: (0.5 * acc,), (FF_SHARD, D_MODEL))[0]

    def dw_up(nm, dact):
        return _matmul(
            nm, (N_CHIPS, nk),
            [(dact, pl.BlockSpec((None, bk, FF_SHARD), lambda j, k: (j, k, 0)),
              n, pl.BlockSpec((bk, D_MODEL), lambda j, k: (k, 0)), TN)],
            [], [(jax.ShapeDtypeStruct((N_CHIPS, FF_SHARD, D_MODEL), BF16),
                  pl.BlockSpec((None, FF_SHARD, D_MODEL), lambda j, k: (j, 0, 0)))],
            _ident, (FF_SHARD, D_MODEL))[0]

    grads[tag + "_w_gate"] = dw_up(tag + "_dwg", dg)
    grads[tag + "_w_up"] = dw_up(tag + "_dwu", du)

    bn = _row_block(s, 512)
    steps = s // bn // 2
    prev, dgain = (), None
    for part, off in (("_dn_a", 0), ("_dn_b", steps)):
        row = pl.BlockSpec((bn, D_MODEL), lambda i, k, off=off: (i + off, 0))
        terms = []
        for j in range(N_CHIPS):
            a_slab = pl.BlockSpec((None, bn, FF_SHARD), lambda i, k, j=j, off=off: (j, i + off, 0))
            w_slab = pl.BlockSpec((None, FF_SHARD, D_MODEL), lambda i, k, j=j: (j, 0, 0))
            terms += [(dg, a_slab, wg, w_slab, NN), (du, a_slab, wu, w_slab, NN)]
        if norm_bwd is None:
            prev = _matmul(tag + part, (steps, 1), terms, [], [(jax.ShapeDtypeStruct((s, D_MODEL), F32), row)],
                           _ident, None, fill=prev)
            continue
        h, gain, dres = norm_bwd
        vec = pl.BlockSpec((1, D_MODEL), lambda i, k: (0, 0))
        res = _matmul(
            tag + part, (steps, 1), terms, [(h, row), (gain, vec), (dres, row)],
            [(jax.ShapeDtypeStruct((s, D_MODEL), F32), row), (jax.ShapeDtypeStruct((s, D_MODEL), BF16), row),
             (jax.ShapeDtypeStruct((1, D_MODEL), F32), vec)],
            _norm_bwd_epilogue(D_MODEL), None, fill=prev, summed=(2,))
        prev = res[:2]
        dgain = res[2] if dgain is None else dgain + res[2]
    return prev[0] if norm_bwd is None else (prev[0], prev[1], dgain)


def _mm_nn(name, a, b, out_dtype, res=None, gain=None):
    s, k = a.shape
    nn = b.shape[1]
    bm = _row_block(s)
    row = pl.BlockSpec((bm, nn), lambda i, kk: (i, 0))
    term = [(a, pl.BlockSpec((bm, k), lambda i, kk: (i, 0)), b, pl.BlockSpec((k, nn), lambda i, kk: (0, 0)), NN)]
    if res is None:
        return _matmul(name, (s // bm, 1), term, [], [(jax.ShapeDtypeStruct((s, nn), out_dtype), row)], _ident, None)[0]
    extras, outs = _residual_outs(s, bm, gain)
    res_out = _matmul(name, (s // bm, 1), term, [(res, row)] + extras, outs,
                      _residual_epilogue(1.0, gain is not None), None)
    return res_out if gain is not None else res_out[0]


def _mm_nt(name, a, b, out_dtype, attn_out=None, nh=0, dv=0):
    s, nn = a.shape
    k = b.shape[0]
    bm = _row_block(s)
    term = [(a, pl.BlockSpec((bm, nn), lambda i, kk: (i, 0)), b, pl.BlockSpec((k, nn), lambda i, kk: (0, 0)), NT)]
    out = (jax.ShapeDtypeStruct((s, k), out_dtype), pl.BlockSpec((bm, k), lambda i, kk: (i, 0)))
    if attn_out is None:
        return _matmul(name, (s // bm, 1), term, [], [out], _ident, None)[0]

    def with_delta(acc, o):
        do = acc.astype(out_dtype).astype(F32)
        cols = [jnp.sum(do[:, h * dv:(h + 1) * dv] * o[:, h * dv:(h + 1) * dv].astype(F32), axis=-1, keepdims=True)
                for h in range(nh)]
        return acc, jnp.stack(cols, axis=0)

    return _matmul(
        name, (s // bm, 1), term, [(attn_out, pl.BlockSpec((bm, nh * dv), lambda i, kk: (i, 0)))],
        [out, (jax.ShapeDtypeStruct((nh, s, 1), F32), pl.BlockSpec((nh, bm, 1), lambda i, kk: (0, i, 0)))],
        with_delta, None)


def _mm_nt_norm_bwd(name, a, b, h, gain, dres):
    s, nn = a.shape
    bm = _row_block(s, 512)
    extras, outs, summed = _norm_bwd_operands(s, bm, h, gain, dres)
    return _matmul(
        name, (s // bm, 1),
        [(a, pl.BlockSpec((bm, nn), lambda i, kk: (i, 0)), b, pl.BlockSpec(b.shape, lambda i, kk: (0, 0)), NT)],
        extras, outs, _norm_bwd_epilogue(D_MODEL), None, summed=summed)


def _w_in_dx_norm_bwd(dz, w_t, h, gain, dres):
    s = dz.shape[0]
    bm = _row_block(s, 512)
    epilogue = _norm_bwd_epilogue(D_MODEL)

    def body(dz_ref, w_ref, h_ref, g_ref, r_ref, dx_ref, dxb_ref, dg_ref):
        dzv = dz_ref[...]
        dn = jnp.concatenate([_dot(dzv, w_ref[j], NN) for j in range(N_CHIPS)], axis=1)
        dx, _, dgain = epilogue(dn, h_ref[...], g_ref[...], r_ref[...])
        dx_ref[...] = dx
        dxb_ref[...] = dx.astype(BF16)

        @pl.when(pl.program_id(0) == 0)
        def _():
            dg_ref[...] = dgain

        @pl.when(pl.program_id(0) > 0)
        def _():
            dg_ref[...] += dgain

    row = pl.BlockSpec((bm, D_MODEL), lambda i: (i, 0))
    vec = pl.BlockSpec((1, D_MODEL), lambda i: (0, 0))
    return _call(
        body, name="w_in_dx", grid=(s // bm,),
        in_specs=[row, pl.BlockSpec(w_t.shape, lambda i: (0, 0, 0)), row, vec, row],
        out_specs=[row, row, vec],
        out_shape=[jax.ShapeDtypeStruct((s, D_MODEL), F32), jax.ShapeDtypeStruct((s, D_MODEL), BF16),
                   jax.ShapeDtypeStruct((1, D_MODEL), F32)],
        sem=("arbitrary",), args=[dz, w_t, h, gain, dres])


def _mm_tn(name, a, b, out_dtype=BF16):
    s, k = a.shape
    nn = b.shape[1]
    bk = _reduce_block(s)
    return _matmul(
        name, (s // bk,),
        [(a, pl.BlockSpec((bk, k), lambda kk: (kk, 0)), b, pl.BlockSpec((bk, nn), lambda kk: (kk, 0)), TN)],
        [], [(jax.ShapeDtypeStruct((k, nn), out_dtype), pl.BlockSpec((k, nn), lambda kk: (0, 0)))],
        _ident, (k, nn))[0]


def _mm_heads_fwd(name, a, w, out_dtype, w_transposed=False):
    s, k = a.shape
    nh = w.shape[0]
    nn = w.shape[1] if w_transposed else w.shape[2]
    bm = _row_block(s)
    return _matmul(
        name, (nh, s // bm, 1),
        [(a, pl.BlockSpec((bm, k), lambda h, i, kk: (i, 0)),
          w, pl.BlockSpec((None,) + w.shape[1:], lambda h, i, kk: (h, 0, 0)), NT if w_transposed else NN)],
        [], [(jax.ShapeDtypeStruct((s, nh * nn), out_dtype), pl.BlockSpec((bm, nn), lambda h, i, kk: (i, h)))],
        _ident, None)[0]


def _mm_heads_bwd(name, dy, a, w, w_transposed=False):
    s, k = a.shape
    nh = w.shape[0]
    nn = w.shape[1] if w_transposed else w.shape[2]
    bm = _row_block(s)
    bk = _reduce_block(s)
    w_spec = pl.BlockSpec((None,) + w.shape[1:], lambda i, h: (h, 0, 0))
    da = _matmul(
        name + "_dx", (s // bm, nh),
        [(dy, pl.BlockSpec((bm, nn), lambda i, h: (i, h)), w, w_spec, NN if w_transposed else NT)],
        [], [(jax.ShapeDtypeStruct((s, k), F32), pl.BlockSpec((bm, k), lambda i, h: (i, 0)))], _ident, (bm, k))[0]
    a_term = (a, pl.BlockSpec((bk, k), lambda h, kk: (kk, 0)))
    dy_term = (dy, pl.BlockSpec((bk, nn), lambda h, kk: (kk, h)))
    lhs, rhs = (dy_term, a_term) if w_transposed else (a_term, dy_term)
    dw = _matmul(
        name + "_dw", (nh, s // bk), [lhs + rhs + (TN,)],
        [], [(jax.ShapeDtypeStruct(w.shape, BF16), pl.BlockSpec((None,) + w.shape[1:], lambda h, kk: (h, 0, 0)))],
        _ident, w.shape[1:])[0]
    return da, dw


def _w_in_fwd(n, w_t):
    s = n.shape[0]
    bm = _row_block(s)
    nh, nout, kin = w_t.shape
    terms = [(n, pl.BlockSpec((bm, kin), lambda i, k, j=j: (i, j)),
              w_t, pl.BlockSpec((None, nout, kin), lambda i, k, j=j: (j, 0, 0)), NT) for j in range(nh)]
    row = pl.BlockSpec((bm, nout), lambda i, k: (i, 0))
    return _matmul("w_in", (s // bm, 1), terms, [], [(jax.ShapeDtypeStruct((s, nout), F32), row)], _ident, None)[0]


def _w_in_dw(dz, n):
    s, nout = dz.shape
    kin = n.shape[1] // N_CHIPS
    bk = _reduce_block(s)
    return _matmul(
        "w_in_dw", (N_CHIPS, s // bk),
        [(dz, pl.BlockSpec((bk, nout), lambda j, k: (k, 0)), n, pl.BlockSpec((bk, kin), lambda j, k: (k, j)), TN)],
        [], [(jax.ShapeDtypeStruct((N_CHIPS, nout, kin), BF16), pl.BlockSpec((None, nout, kin), lambda j, k: (j, 0, 0)))],
        _ident, (nout, kin))[0]


def _rope_tables(positions):
    half = ROPE_DIM // 2
    freqs = 1.0 / (ROPE_BASE ** (jnp.arange(0, ROPE_DIM, 2, dtype=F32) / ROPE_DIM))
    ang = positions.astype(F32)[:, None] * freqs
    cos, sin = jnp.cos(ang), jnp.sin(ang)
    z = jnp.zeros_like(cos)
    tc = jnp.concatenate([cos, cos, z, z], axis=-1)
    ta = jnp.concatenate([-sin, z, z, z], axis=-1)
    tb = jnp.concatenate([z, sin, z, z], axis=-1)
    assert tc.shape[-1] == 4 * half
    return tc, ta, tb


def _rope(x, tc, ta, tb):
    return x * tc + pltpu.roll(x, 96, 1) * ta + pltpu.roll(x, 32, 1) * tb


def _rope_t(dy, tc, ta, tb):
    return dy * tc + pltpu.roll(dy * ta, 32, 1) + pltpu.roll(dy * tb, 96, 1)


def _norm_bf16(x, g):
    r = lax.rsqrt(jnp.mean(x * x, axis=-1, keepdims=True) + RMS_EPS)
    return ((x * r) * g).astype(BF16)


def _qkv_prep(z, q_gain, kv_gain, wq_t, wkv, tc, ta, tb):
    s = z.shape[0]
    bm = _row_block(s, 512)

    def body(zq_ref, zkv_ref, zkr_ref, qg_ref, kvg_ref, wq_ref, wkv_ref, tc_ref, ta_ref, tb_ref,
             qn_ref, kvn_ref, q_ref, k_ref, v_ref):
        c, a, b = tc_ref[...], ta_ref[...], tb_ref[...]
        qn = _norm_bf16(zq_ref[...], qg_ref[...])
        kvn = _norm_bf16(zkv_ref[...], kvg_ref[...])
        qn_ref[...] = qn
        kvn_ref[...] = kvn
        kpe = _rope(zkr_ref[...], c, a, b).astype(BF16)
        for h in range(MLA_HEADS):
            lo = h * HEAD_QK
            qp = _dot(qn, wq_ref[h], NT)
            q_ref[:, lo:lo + 128] = qp[:, :128].astype(BF16)
            q_ref[:, lo + 128:lo + 256] = _rope(qp[:, 128:], c, a, b).astype(BF16)
            kv = _dot(kvn, wkv_ref[h], NN)
            k_ref[:, lo:lo + 128] = kv[:, :128].astype(BF16)
            k_ref[:, lo + 128:lo + 256] = kpe
            v_ref[:, h * HEAD_V:(h + 1) * HEAD_V] = kv[:, 128:].astype(BF16)

    def cols(width, blk):
        return pl.BlockSpec((bm, width), lambda i: (i, blk))

    def whole(a):
        return pl.BlockSpec(a.shape, lambda i: (0,) * a.ndim)

    tab = cols(128, 0)
    return _call(
        body, name="qkv_prep", grid=(s // bm,),
        in_specs=[cols(Q_LORA, 0), cols(KV_LORA, 2), cols(128, 3), whole(q_gain), whole(kv_gain), whole(wq_t),
                  whole(wkv), tab, tab, tab],
        out_specs=[cols(Q_LORA, 0), cols(KV_LORA, 0), cols(1024, 0), cols(1024, 0), cols(512, 0)],
        out_shape=[jax.ShapeDtypeStruct((s, Q_LORA), BF16), jax.ShapeDtypeStruct((s, KV_LORA), BF16),
                   jax.ShapeDtypeStruct((s, 1024), BF16), jax.ShapeDtypeStruct((s, 1024), BF16),
                   jax.ShapeDtypeStruct((s, 512), BF16)],
        sem=("parallel",), args=[z, z, z, q_gain, kv_gain, wq_t, wkv, tc, ta, tb])


def _qkv_prep_bwd(dq, dk, dv, z, qn, kvn, q_gain, kv_gain, wq_t, wkv, tc, ta, tb):
    s = z.shape[0]
    bm = _row_block(s, 512)
    nsteps = s // bm

    def body(dq_ref, dk_ref, dv_ref, zq_ref, zkv_ref, qn_ref, kvn_ref, qg_ref, kvg_ref, wq_ref, wkv_ref,
             tc_ref, ta_ref, tb_ref, dz_ref, dqg_ref, dkvg_ref, dwq_ref, dwkv_ref, wq_acc, wkv_acc):
        i = pl.program_id(0)
        c, a, b = tc_ref[...], ta_ref[...], tb_ref[...]

        @pl.when(i == 0)
        def _():
            wq_acc[...] = jnp.zeros_like(wq_acc)
            wkv_acc[...] = jnp.zeros_like(wkv_acc)

        qn, kvn = qn_ref[...], kvn_ref[...]
        dqn = jnp.zeros((bm, Q_LORA), F32)
        dkvn = jnp.zeros((bm, KV_LORA), F32)
        dpe = jnp.zeros((bm, 128), F32)
        for h in range(MLA_HEADS):
            lo = h * HEAD_QK
            dqp = jnp.concatenate([dq_ref[:, lo:lo + 128],
                                   _rope_t(dq_ref[:, lo + 128:lo + 256].astype(F32), c, a, b).astype(BF16)], axis=1)
            dqn = dqn + _dot(dqp, wq_ref[h], NN)
            wq_acc[h] += _dot(dqp, qn, TN)
            dkv = jnp.concatenate([dk_ref[:, lo:lo + 128], dv_ref[:, h * HEAD_V:(h + 1) * HEAD_V]], axis=1)
            dkvn = dkvn + _dot(dkv, wkv_ref[h], NT)
            wkv_acc[h] += _dot(kvn, dkv, TN)
            dpe = dpe + dk_ref[:, lo + 128:lo + 256].astype(F32)
        dcq, dqg = _rms_bwd_math(dqn, zq_ref[...], qg_ref[...], Q_LORA)
        dckv, dkvg = _rms_bwd_math(dkvn, zkv_ref[...], kvg_ref[...], KV_LORA)
        dz_ref[:, 0:Q_LORA] = dcq.astype(BF16)
        dz_ref[:, Q_LORA:Q_LORA + KV_LORA] = dckv.astype(BF16)
        dz_ref[:, Q_LORA + KV_LORA:512] = _rope_t(dpe, c, a, b).astype(BF16)

        @pl.when(i == 0)
        def _():
            dqg_ref[...] = dqg
            dkvg_ref[...] = dkvg

        @pl.when(i > 0)
        def _():
            dqg_ref[...] += dqg
            dkvg_ref[...] += dkvg

        @pl.when(i == nsteps - 1)
        def _():
            dwq_ref[...] = wq_acc[...].astype(BF16)
            dwkv_ref[...] = wkv_acc[...].astype(BF16)

    def cols(width, blk):
        return pl.BlockSpec((bm, width), lambda i: (i, blk))

    def whole(shape):
        return pl.BlockSpec(shape, lambda i: (0,) * len(shape))

    tab = cols(128, 0)
    return _call(
        body, name="qkv_prep_bwd", grid=(nsteps,),
        in_specs=[cols(1024, 0), cols(1024, 0), cols(512, 0), cols(Q_LORA, 0), cols(KV_LORA, 2), cols(Q_LORA, 0),
                  cols(KV_LORA, 0), whole(q_gain.shape), whole(kv_gain.shape), whole(wq_t.shape), whole(wkv.shape),
                  tab, tab, tab],
        out_specs=[cols(512, 0), whole(q_gain.shape), whole(kv_gain.shape), whole(wq_t.shape), whole(wkv.shape)],
        out_shape=[jax.ShapeDtypeStruct((s, 512), BF16), jax.ShapeDtypeStruct(q_gain.shape, F32),
                   jax.ShapeDtypeStruct(kv_gain.shape, F32), jax.ShapeDtypeStruct(wq_t.shape, BF16),
                   jax.ShapeDtypeStruct(wkv.shape, BF16)],
        scratch_shapes=[pltpu.VMEM(wq_t.shape, F32), pltpu.VMEM(wkv.shape, F32)],
        sem=("arbitrary",), args=[dq, dk, dv, z, z, qn, kvn, q_gain, kv_gain, wq_t, wkv, tc, ta, tb])


def _causal_mask(s, row0, col0):
    rows = row0 + lax.broadcasted_iota(jnp.int32, s.shape, 0)
    cols = col0 + lax.broadcasted_iota(jnp.int32, s.shape, 1)
    return jnp.where(cols <= rows, s, -jnp.inf)


def _attn_fwd(name, q, k, k_off, v, v_off, nh, dq, dv, scale, causal, blk):
    sq, sk = q.shape[0], k.shape[0]
    bq = min(blk, sq)
    bk = min(blk, sk)
    nkv = sk // bk
    assert not causal or (sq == sk and bq == bk)

    hq = bq
    log2e = 1.4426950408889634
    c2 = scale * log2e

    def body(q_ref, k_ref, v_ref, o_ref, lse_ref):
        qi = pl.program_id(1)
        qs = (q_ref[...],)

        def step(j, carry, masked):
            rows = pl.ds(pl.multiple_of(j * bk, bk), bk)
            kb, vb = k_ref[rows, :], v_ref[rows, :]
            out = []
            for t, (m, l, acc) in enumerate(carry):
                s = _dot(qs[t], kb, NT) * c2
                if masked:
                    s = _causal_mask(s, qi * bq + t * hq, j * bk)
                m_new = jnp.maximum(m, jnp.max(s, axis=-1, keepdims=True))
                alpha = jnp.exp2(m - m_new)
                p = jnp.exp2(s - m_new)
                l = alpha * l + jnp.sum(p, axis=-1, keepdims=True)
                acc = alpha * acc + _dot(p, vb, NN)
                out.append((m_new, l, acc))
            return tuple(out)

        one = (jnp.full((hq, 1), -jnp.inf, F32), jnp.zeros((hq, 1), F32), jnp.zeros((hq, dv), F32))
        init = (one,)
        if causal:
            carry = lax.fori_loop(0, qi, lambda j, c: step(j, c, False), init)
            fin = step(qi, carry, True)
        else:
            fin = lax.fori_loop(0, nkv, lambda j, c: step(j, c, False), init)
        for t, (m, l, acc) in enumerate(fin):
            o_ref[t * hq:(t + 1) * hq, :] = (acc / l).astype(o_ref.dtype)
            lse_ref[t * hq:(t + 1) * hq, :] = m * (1.0 / log2e) + jnp.log(l)

    return _call(
        body, name=name, grid=(nh, sq // bq),
        in_specs=[pl.BlockSpec((bq, dq), lambda h, i: (i, h)),
                  pl.BlockSpec((sk, dq), lambda h, i: (0, k_off + h)),
                  pl.BlockSpec((sk, dv), lambda h, i: (0, v_off + h))],
        out_specs=[pl.BlockSpec((bq, dv), lambda h, i: (i, h)), pl.BlockSpec((None, bq, 1), lambda h, i: (h, i, 0))],
        out_shape=[jax.ShapeDtypeStruct((sq, nh * dv), BF16), jax.ShapeDtypeStruct((nh, sq, 1), F32)],
        sem=("parallel", "parallel"), args=[q, k, v])


def _attn_bwd(name, q, k, k_off, v, v_off, do, do_off, lse, delta, nh, dq, dv, scale, causal, blk):
    sq, sk = q.shape[0], k.shape[0]
    bq = min(blk, sq)
    bk = min(blk, sk)
    nq = sq // bq
    assert not causal or (sq == sk and bq == bk)

    nkv = sk // bk

    def body(q_ref, k_ref, v_ref, do_ref, lse_ref, dl_ref, dq_ref, dk_ref, dv_ref, dq_acc, dk_acc, dv_acc):
        j = pl.program_id(1)

        @pl.when(j == 0)
        def _():
            dq_acc[...] = jnp.zeros_like(dq_acc)

        dk_acc[...] = jnp.zeros_like(dk_acc)
        dv_acc[...] = jnp.zeros_like(dv_acc)
        kv = k_ref[...]
        vv = v_ref[...]

        def step(i, masked):
            rows = pl.ds(pl.multiple_of(i * bq, bq), bq)
            qv = q_ref[rows, :]
            dov = do_ref[rows, :].astype(BF16)
            s = _dot(qv, kv, NT) * scale
            if masked:
                s = _causal_mask(s, i * bq, j * bk)
            p = jnp.exp(s - lse_ref[rows, :])
            dp = _dot(dov, vv, NT)
            ds = (p * (dp - dl_ref[rows, :]) * scale).astype(BF16)
            dv_acc[...] += _dot(p, dov, TN)
            dk_acc[...] += _dot(ds, qv, TN)
            dq_acc[rows, :] += _dot(ds, kv, NN)

        if causal:
            step(j, True)

            def loop(i, c):
                step(i, False)
                return c

            lax.fori_loop(j + 1, nq, loop, 0)
        else:
            def loop(i, c):
                step(i, False)
                return c

            lax.fori_loop(0, nq, loop, 0)
        dk_ref[...] = dk_acc[...].astype(dk_ref.dtype)
        dv_ref[...] = dv_acc[...].astype(dv_ref.dtype)

        @pl.when(j == nkv - 1)
        def _():
            dq_ref[...] = dq_acc[...].astype(dq_ref.dtype)

    stat = pl.BlockSpec((None, sq, 1), lambda h, j: (h, 0, 0))
    return _call(
        body, name=name, grid=(nh, sk // bk),
        in_specs=[pl.BlockSpec((sq, dq), lambda h, j: (0, h)),
                  pl.BlockSpec((bk, dq), lambda h, j: (j, k_off + h)),
                  pl.BlockSpec((bk, dv), lambda h, j: (j, v_off + h)),
                  pl.BlockSpec((sq, dv), lambda h, j: (0, do_off + h)), stat, stat],
        out_specs=[pl.BlockSpec((sq, dq), lambda h, j: (0, h)),
                   pl.BlockSpec((bk, dq), lambda h, j: (j, h)),
                   pl.BlockSpec((bk, dv), lambda h, j: (j, h))],
        out_shape=[jax.ShapeDtypeStruct((sq, nh * dq), BF16), jax.ShapeDtypeStruct((sk, nh * dq), BF16),
                   jax.ShapeDtypeStruct((sk, nh * dv), BF16)],
        scratch_shapes=[pltpu.VMEM((sq, dq), F32), pltpu.VMEM((bk, dq), F32), pltpu.VMEM((bk, dv), F32)],
        sem=("parallel", "arbitrary"), args=[q, k, v, do, lse, delta])


def _pool_diff(z, g):
    s = z.shape[0]
    t = lax.broadcasted_iota(jnp.int32, z.shape, 0)
    acc = z
    sums = []
    for k in (1, 2, 4, 8):
        acc = acc + jnp.where(t >= k, pltpu.roll(acc, k, 0), 0.0)
        sums.append(acc)
    win = jnp.where(g == 0, sums[0], jnp.where(g == 1, sums[1], jnp.where(g == 2, sums[2], sums[3])))
    w = lax.shift_left(jnp.int32(2), g)
    count = jnp.minimum(t + 1, w).astype(F32)
    del s
    return win / count - z, count


def _pool_fwd(z, pool_w, pool_scale):
    s = z.shape[0]

    def body(z_ref, w_ref, sc_ref, o_ref):
        diff, _ = _pool_diff(z_ref[...], pl.program_id(0))
        o_ref[...] = (_dot(diff, w_ref[...], NN) * sc_ref[...]).astype(o_ref.dtype)

    return _call(
        body, name="pool_fwd", grid=(POOL_GROUPS,),
        in_specs=[pl.BlockSpec((s, POOL_CH), lambda g: (0, 4 + g)),
                  pl.BlockSpec((None, POOL_CH, POOL_CH), lambda g: (g, 0, 0)),
                  pl.BlockSpec((1, POOL_CH), lambda g: (0, g))],
        out_specs=[pl.BlockSpec((s, POOL_CH), lambda g: (0, g))],
        out_shape=[jax.ShapeDtypeStruct((s, POOL_GROUPS * POOL_CH), BF16)],
        sem=("parallel",), args=[z, pool_w, pool_scale])[0]


def _pool_bwd(dcat, z, pool_w, pool_scale):
    s = z.shape[0]

    def body(dp_ref, z_ref, w_ref, sc_ref, dz_ref, dw_ref, dsc_ref):
        g = pl.program_id(0)
        diff, count = _pool_diff(z_ref[...], g)
        dpf = dp_ref[...].astype(F32)
        u = _dot(diff, w_ref[...], NN)
        dsc_ref[...] = jnp.sum(dpf * u, axis=0, keepdims=True)
        du = (dpf * sc_ref[...]).astype(BF16)
        dw_ref[...] = _dot(diff, du, TN)
        ddiff = _dot(du, w_ref[...], NT)
        t = lax.broadcasted_iota(jnp.int32, ddiff.shape, 0)
        acc = ddiff / count
        sums = []
        for k in (1, 2, 4, 8):
            acc = acc + jnp.where(t < s - k, pltpu.roll(acc, s - k, 0), 0.0)
            sums.append(acc)
        win = jnp.where(g == 0, sums[0], jnp.where(g == 1, sums[1], jnp.where(g == 2, sums[2], sums[3])))
        dz_ref[...] = win - ddiff

    return pl.pallas_call(
        body, name="pool_bwd", grid=(POOL_GROUPS,),
        in_specs=[pl.BlockSpec((s, POOL_CH), lambda g: (0, 4 + g)),
                  pl.BlockSpec((s, POOL_CH), lambda g: (0, 4 + g)),
                  pl.BlockSpec((None, POOL_CH, POOL_CH), lambda g: (g, 0, 0)),
                  pl.BlockSpec((1, POOL_CH), lambda g: (0, g))],
        out_specs=[pl.BlockSpec((s, POOL_CH), lambda g: (0, g)),
                   pl.BlockSpec((None, POOL_CH, POOL_CH), lambda g: (g, 0, 0)),
                   pl.BlockSpec((1, POOL_CH), lambda g: (0, g))],
        out_shape=[jax.ShapeDtypeStruct((s, POOL_GROUPS * POOL_CH), F32),
                   jax.ShapeDtypeStruct((POOL_GROUPS, POOL_CH, POOL_CH), F32),
                   jax.ShapeDtypeStruct((1, POOL_GROUPS * POOL_CH), F32)],
        compiler_params=_params("parallel"),
    )(dcat, z, pool_w, pool_scale)


def _local_step(x, mem, positions, target, w, grads):
    tc, ta, tb = _rope_tables(positions)
    blk = _ATT_BLOCK

    if "ffn1_shards" in w:
        n1, a1, dadu1, dadg1, w["ffn1_w_gate"], w["ffn1_w_up"], w["ffn1_w_down"] = _ffn1_up_gather(
            x, w["ffn1_norm"], *w["ffn1_shards"])
    else:
        n1 = _rmsnorm_fwd("ffn1_norm", x, w["ffn1_norm"], D_MODEL)
        a1, dadu1, dadg1 = _ffn_up("ffn1_up", n1, w["ffn1_w_gate"], w["ffn1_w_up"])
    h1, n2 = _ffn_down("ffn1_down", a1, w["ffn1_w_down"], x, w["mix_norm"])
    z = _w_in_fwd(n2, w["w_in"])
    qn, kvn, qf, kf, vf = _qkv_prep(z, w["q_norm"], w["kv_norm"], w["w_q_up"], w["w_kv_up"], tc, ta, tb)
    att, lse = _attn_fwd("mla_fwd", qf, kf, 0, vf, 0, MLA_HEADS, HEAD_QK, HEAD_V, MLA_SCALE, True, blk)
    pool = _pool_fwd(z, w["pool_w"], w["pool_scale"])
    s = x.shape[0]
    bm = _row_block(s)
    row = pl.BlockSpec((bm, D_MODEL), lambda i, k: (i, 0))
    half = pl.BlockSpec((bm, 512), lambda i, k: (i, 0))
    h2, n3 = _matmul(
        "w_out", (s // bm, 1),
        [(att, half, w["w_out"], pl.BlockSpec((512, D_MODEL), lambda i, k: (0, 0)), NN),
         (pool, half, w["w_out"], pl.BlockSpec((512, D_MODEL), lambda i, k: (1, 0)), NN)],
        [(h1, row)] + _residual_outs(s, bm, w["xattn_norm"])[0], _residual_outs(s, bm, w["xattn_norm"])[1],
        _residual_epilogue(1.0, True), None)
    memn = _rmsnorm_fwd("mem_norm", mem, w["mem_norm"], D_MODEL)
    qm = _mm_nn("w_mq", n3, w["w_mq"], BF16)
    kvm = _mm_heads_fwd("w_mkv", memn, w["w_mkv"], BF16)
    om, lse_m = _attn_fwd("xattn_fwd", qm, kvm, 0, kvm, MEM_HEADS, MEM_HEADS, MEM_HEAD_DIM, MEM_HEAD_DIM,
                          MEM_SCALE, False, blk)
    h3, n4 = _mm_nn("w_mo", om, w["w_mo"], F32, res=h2, gain=w["ffn2_norm"])
    a2, dadu2, dadg2 = _ffn_up("ffn2_up", n4, w["ffn2_w_gate"], w["ffn2_w_up"])
    dh4, dh4b, loss_vec, d_final = _ffn_down("ffn2_down", a2, w["ffn2_w_down"], h3, loss=(w["final_norm"], target))
    grads["final_norm"] = d_final

    dh3, dh3b, grads["ffn2_norm"] = _ffn_bwd("ffn2", dh4b, n4, dadg2, dadu2, a2, w["ffn2_w_gate"], w["ffn2_w_up"],
                                             w["ffn2_w_down"], grads, norm_bwd=(h3, w["ffn2_norm"], dh4))

    dom, delta_m = _mm_nt("w_mo_dx", dh3b, w["w_mo"], BF16, attn_out=om, nh=MEM_HEADS, dv=MEM_HEAD_DIM)
    grads["w_mo"] = _mm_tn("w_mo_dw", om, dh3b)
    dqm, dkm, dvm = _attn_bwd("xattn_bwd", qm, kvm, 0, kvm, MEM_HEADS, dom, 0, lse_m, delta_m, MEM_HEADS,
                              MEM_HEAD_DIM, MEM_HEAD_DIM, MEM_SCALE, False, blk)
    dkvm = jnp.concatenate([dkm, dvm], axis=1)
    dh2, dh2b, grads["xattn_norm"] = _mm_nt_norm_bwd("w_mq_dx", dqm, w["w_mq"], h2, w["xattn_norm"], dh3)
    grads["w_mq"] = _mm_tn("w_mq_dw", n3, dqm)
    dmemn, grads["w_mkv"] = _mm_heads_bwd("w_mkv", dkvm, memn, w["w_mkv"])
    _, grads["mem_norm"] = _rmsnorm_bwd("mem_norm_bwd", dmemn, mem, w["mem_norm"], D_MODEL, out_dtype=BF16)

    dcat, delta = _mm_nt("w_out_dx", dh2b, w["w_out"], BF16, attn_out=att, nh=MLA_HEADS, dv=HEAD_V)
    grads["w_out"] = jnp.concatenate([_mm_tn("w_out_dw_a", att, dh2b), _mm_tn("w_out_dw_p", pool, dh2b)], axis=0)
    dzp, grads["pool_w"], grads["pool_scale"] = _pool_bwd(dcat, z, w["pool_w"], w["pool_scale"])
    dqf, dkf, dvf = _attn_bwd("mla_bwd", qf, kf, 0, vf, 0, dcat, 0, lse, delta, MLA_HEADS, HEAD_QK, HEAD_V,
                              MLA_SCALE, True, blk)
    dz_lat, grads["q_norm"], grads["kv_norm"], grads["w_q_up"], grads["w_kv_up"] = _qkv_prep_bwd(
        dqf, dkf, dvf, z, qn, kvn, w["q_norm"], w["kv_norm"], w["w_q_up"], w["w_kv_up"], tc, ta, tb)
    dz = jnp.concatenate([dz_lat, dzp.astype(BF16)], axis=1)
    grads["w_in"] = _w_in_dw(dz, n2)
    dh1, dh1b, grads["mix_norm"] = _w_in_dx_norm_bwd(dz, w["w_in"], h1, w["mix_norm"], dh2)

    dn1 = _ffn_bwd("ffn1", dh1b, n1, dadg1, dadu1, a1, w["ffn1_w_gate"], w["ffn1_w_up"], w["ffn1_w_down"], grads)
    dx, grads["ffn1_norm"], _ = _rmsnorm_bwd("ffn1_norm_bwd", dn1, x, w["ffn1_norm"], D_MODEL, dres=dh1)
    return loss_vec[0, 0], dx


def _mesh_pos():
    x, y, c = lax.axis_index("x"), lax.axis_index("y"), lax.axis_index("c")
    chips = [(1 - x, y), (x, 1 - y), (1 - x, 1 - y)]
    chip_ids = [2 * cx + cy for cx, cy in chips]
    return x, y, c, 2 * x + y, chips, chip_ids


def _half_rows(c, rows):
    hr = rows // 2
    return pl.ds(pl.multiple_of(c * hr, 16), hr), pl.ds(pl.multiple_of((1 - c) * hr, 16), hr)


def _ag_ici_stage(shards):
    n = len(shards)

    def copies(ins, outs):
        x, y, c, me, chips, _ = _mesh_pos()
        out = []
        for k in range(n):
            mine, _ = _half_rows(c, ins[k].shape[0])
            out.append((ins[k], outs[k].at[me], None))
            for cx, cy in chips:
                out.append((ins[k].at[mine], outs[k].at[me, mine], (cx, cy, c)))
        return out

    return _Stage(shards, [jax.ShapeDtypeStruct((N_CHIPS,) + s.shape, s.dtype) for s in shards], 3 * n, n, copies)


def _quarter_rows(c, rows):
    qr = rows // 4
    return pl.ds(pl.multiple_of(c * 2 * qr, 16), qr), pl.ds(pl.multiple_of(c * 2 * qr + qr, 16), qr)


def _ag_d2d_stage(fulls):
    n = len(fulls)

    def copies(ins, outs):
        x, y, c, me, _, chip_ids = _mesh_pos()
        out = []
        for k in range(n):
            mine, _ = _half_rows(c, ins[k].shape[1])
            for j in range(3):
                out.append((ins[k].at[chip_ids[j], mine], outs[k].at[chip_ids[j], mine], (x, y, 1 - c)))
        return out

    return _Stage(fulls, [jax.ShapeDtypeStruct(f.shape, f.dtype) for f in fulls], 3 * n, 0, copies,
                  aliases={k: k for k in range(n)})


def _rs_swap_stage(grads):
    n = len(grads)

    def copies(ins, outs):
        x, y, c, _, _, _ = _mesh_pos()
        out = []
        for k in range(n):
            _, other = _half_rows(c, ins[k].shape[1])
            out.append((ins[k].at[:, other, :], outs[k], (x, y, 1 - c)))
        return out

    return _Stage(grads, [jax.ShapeDtypeStruct((N_CHIPS, g.shape[1] // 2, g.shape[2]), g.dtype) for g in grads],
                  n, 0, copies)


_REL_OF_PEER = (2, 1, 3)


def _rs_scatter_stage(sums, relative=False):
    n = len(sums)

    def copies(ins, outs):
        x, y, c, me, chips, chip_ids = _mesh_pos()
        out = []
        for k in range(n):
            mine, _ = _half_rows(c, 2 * ins[k].shape[1])
            out.append((ins[k].at[0 if relative else me], outs[k].at[0, mine, :], None))
            for j, (cx, cy) in enumerate(chips):
                slab = _REL_OF_PEER[j] if relative else chip_ids[j]
                out.append((ins[k].at[slab], outs[k].at[1 + j, mine, :], (cx, cy, c)))
        return out

    return _Stage(sums, [jax.ShapeDtypeStruct((N_CHIPS, 2 * s.shape[1], s.shape[2]), s.dtype) for s in sums],
                  3 * n, n, copies)


def _rs_mirror_stage(parts):
    n = len(parts)

    def copies(ins, outs):
        x, y, c, _, _, _ = _mesh_pos()
        out = []
        for k in range(n):
            mine, _ = _half_rows(c, ins[k].shape[1])
            out.append((ins[k].at[:, mine, :], outs[k].at[:, mine, :], (x, y, 1 - c)))
        return out

    return _Stage(parts, [jax.ShapeDtypeStruct(p.shape, p.dtype) for p in parts], n, 0, copies,
                  aliases={k: k for k in range(n)})


def _pair_add(name, gs, r1s, core):
    n = len(gs)

    def body(c_ref, *refs):
        for k in range(n):
            g_ref, r_ref, o_ref = refs[k], refs[n + k], refs[2 * n + k]
            o_ref[...] = (g_ref[...].astype(F32) + r_ref[...].astype(F32)).astype(BF16)

    def half(g):
        return pl.BlockSpec((None, g.shape[1] // 2, g.shape[2]), lambda j, c: (j, 0, 0))

    def mine(g):
        return pl.BlockSpec((None, g.shape[1] // 2, g.shape[2]), lambda j, c: (j, c[0], 0))

    return pl.pallas_call(
        body, name=name,
        grid_spec=pltpu.PrefetchScalarGridSpec(
            num_scalar_prefetch=1, grid=(N_CHIPS,),
            in_specs=[mine(g) for g in gs] + [half(g) for g in gs], out_specs=[half(g) for g in gs]),
        out_shape=[jax.ShapeDtypeStruct((N_CHIPS, g.shape[1] // 2, g.shape[2]), BF16) for g in gs],
        compiler_params=_params("parallel"),
    )(core, *gs, *r1s)


def _adamw_math(w, g, m, v):
    m = ADAM_B1 * m + (1.0 - ADAM_B1) * g
    v = ADAM_B2 * v + (1.0 - ADAM_B2) * (g * g)
    m_hat = m / (1.0 - ADAM_B1 ** ADAM_STEP)
    v_hat = v / (1.0 - ADAM_B2 ** ADAM_STEP)
    delta = -ADAM_LR * (m_hat / (jnp.sqrt(v_hat) + ADAM_EPS) + ADAM_WD * w)
    return delta, m, v


def _adamw_sum(name, ws, parts, ms, vs):
    n = len(ws)
    r, c = ws[0].shape
    assert all(w.shape == (r, c) for w in ws)
    br = r
    while br * c * 4 > (1 << 20) and br % 32 == 0:
        br //= 2

    def body(*refs):
        for k in range(n):
            w_ref, p_ref, m_ref, v_ref = refs[4 * k:4 * k + 4]
            g_ref, d_ref, nm_ref, nv_ref = refs[4 * n + 4 * k:4 * n + 4 * k + 4]
            g = p_ref[0].astype(F32)
            for j in range(1, N_CHIPS):
                g = g + p_ref[j].astype(F32)
            d, nm, nv = _adamw_math(w_ref[...], g, m_ref[...], v_ref[...])
            g_ref[...] = g
            d_ref[...] = d
            nm_ref[...] = nm
            nv_ref[...] = nv

    spec = pl.BlockSpec((br, c), lambda i: (i, 0))
    shp = jax.ShapeDtypeStruct((r, c), F32)
    args = [a for k in range(n) for a in (ws[k], parts[k], ms[k], vs[k])]
    res = _call(
        body, name=name, grid=(r // br,),
        in_specs=[spec, pl.BlockSpec((N_CHIPS, br, c), lambda i: (0, i, 0)), spec, spec] * n,
        out_specs=[spec] * (4 * n), out_shape=[shp] * (4 * n), sem=("parallel",), args=args)
    return [res[4 * k:4 * k + 4] for k in range(n)]


_SMALL_VECTORS = ("ffn1_norm", "mix_norm", "xattn_norm", "mem_norm", "ffn2_norm", "final_norm", "q_norm",
                  "kv_norm", "pool_scale")


_LOSS_ROW = 9
_VEC_ROWS = 16
_POOL_ROWS = POOL_GROUPS * POOL_CH


def _small_params_step(g, w, m, v, loss_local):
    names = list(_SMALL_VECTORS) + ["pool_w"]
    nv = len(_SMALL_VECTORS)
    widths = [g[n].shape[1] for n in _SMALL_VECTORS]
    shapes = {"vec": (_VEC_ROWS, D_MODEL), "pool": (_POOL_ROWS, POOL_CH)}

    def body(*refs):
        ins = refs[:4 * (nv + 1) + 1]
        outs = refs[len(ins):len(ins) + 4 * (nv + 1) + 1]
        vec_own, vec_sib, vec_all, pool_sib, pool_sum, pool_all, send, recv = refs[len(ins) + len(outs):]
        g_in, w_in, m_in, v_in = (ins[k * (nv + 1):(k + 1) * (nv + 1)] for k in range(4))
        loss_in = ins[-1]
        g_out, d_out, m_out, v_out = (outs[k * (nv + 1):(k + 1) * (nv + 1)] for k in range(4))
        loss_out = outs[-1]
        x, y, c, me, chips, chip_ids = _mesh_pos()
        sib = (x, y, 1 - c)

        def remote(src, dst, k, dev):
            return pltpu.make_async_remote_copy(src_ref=src, dst_ref=dst, send_sem=send.at[k], recv_sem=recv.at[k],
                                                device_id=dev, device_id_type=_MESH)

        vec_own[...] = jnp.zeros_like(vec_own)
        for i in range(nv):
            vec_own[i:i + 1, 0:widths[i]] = g_in[i][...]
        vec_own[_LOSS_ROW:_LOSS_ROW + 1, 0:128] = loss_in[...]
        swaps = [remote(vec_own, vec_sib, 0, sib), remote(g_in[nv], pool_sib, 1, sib)]
        for cp in swaps:
            cp.start()
        for cp in swaps:
            cp.wait()
        vec_all[me] = vec_own[...] + vec_sib[...]
        pool_sum[...] = g_in[nv][...] + pool_sib[...]
        pool_all[me] = pool_sum[...]
        hv, hp = _VEC_ROWS // 2, _POOL_ROWS // 2
        mine_v = pl.ds(pl.multiple_of(c * hv, 8), hv)
        mine_p = pl.ds(pl.multiple_of(c * hp, 8), hp)
        sends = []
        for j, (cx, cy) in enumerate(chips):
            sends.append(remote(vec_all.at[me, mine_v], vec_all.at[me, mine_v], 2 + j, (cx, cy, c)))
            sends.append(remote(pool_sum.at[mine_p], pool_all.at[me, mine_p], 5 + j, (cx, cy, c)))
        for cp in sends:
            cp.start()
        for cp in sends:
            cp.wait()
        mirrors = []
        for j in range(3):
            mirrors.append(remote(vec_all.at[chip_ids[j], mine_v], vec_all.at[chip_ids[j], mine_v], 8 + j, sib))
            mirrors.append(remote(pool_all.at[chip_ids[j], mine_p], pool_all.at[chip_ids[j], mine_p], 11 + j, sib))
        for cp in mirrors:
            cp.start()
        for cp in mirrors:
            cp.wait()
        vec_tot = vec_all[0]
        pool_tot = pool_all[0]
        for i in range(1, N_CHIPS):
            vec_tot = vec_tot + vec_all[i]
            pool_tot = pool_tot + pool_all[i]
        vec_sib[...] = vec_tot
        loss_out[...] = vec_sib[_LOSS_ROW:_LOSS_ROW + 1, 0:128]
        for i in range(nv + 1):
            gi = pool_tot if i == nv else vec_sib[i:i + 1, 0:widths[i]]
            d, nm, nvv = _adamw_math(w_in[i][...], gi, m_in[i][...], v_in[i][...])
            g_out[i][...] = gi
            d_out[i][...] = d
            m_out[i][...] = nm
            v_out[i][...] = nvv

    vm = pl.BlockSpec(memory_space=pltpu.VMEM)
    args = [d[n] for d in (g, w, m, v) for n in names] + [jnp.broadcast_to(loss_local.reshape(1, 1), (1, 128))]
    out_shape = [jax.ShapeDtypeStruct(g[n].shape, F32) for _ in range(4) for n in names]
    out_shape.append(jax.ShapeDtypeStruct((1, 128), F32))
    res = pl.pallas_call(
        body, name="small_params_step", in_specs=[vm] * len(args), out_specs=[vm] * len(out_shape),
        out_shape=out_shape,
        scratch_shapes=[pltpu.VMEM(shapes["vec"], F32), pltpu.VMEM(shapes["vec"], F32),
                        pltpu.VMEM((N_CHIPS,) + shapes["vec"], F32), pltpu.VMEM(shapes["pool"], F32),
                        pltpu.VMEM(shapes["pool"], F32), pltpu.VMEM((N_CHIPS,) + shapes["pool"], F32),
                        pltpu.SemaphoreType.DMA((14,)), pltpu.SemaphoreType.DMA((14,))],
        compiler_params=pltpu.CompilerParams(vmem_limit_bytes=V7X_VMEM_LIMIT_BYTES),
    )(*args)
    k = len(names)
    dicts = [dict(zip(names, res[i * k:(i + 1) * k])) for i in range(4)]
    return dicts[0], dicts[1], dicts[2], dicts[3], res[-1]


_WEIGHTS = ("ffn1_norm", "ffn1_w_gate", "ffn1_w_up", "ffn1_w_down", "mix_norm", "w_in", "q_norm", "w_q_up",
            "kv_norm", "w_kv_up", "pool_w", "pool_scale", "w_out", "xattn_norm", "mem_norm", "w_mq", "w_mkv",
            "w_mo", "ffn2_norm", "ffn2_w_gate", "ffn2_w_up", "ffn2_w_down", "final_norm")
_SHARDED = ("ffn1_w_gate", "ffn1_w_up", "ffn1_w_down", "w_in", "w_q_up", "w_kv_up", "w_out", "w_mq", "w_mkv",
            "w_mo", "ffn2_w_gate", "ffn2_w_up", "ffn2_w_down")
W_IN_SPLIT = Q_LORA + KV_LORA + ROPE_DIM


_FFN1 = ("ffn1_w_gate", "ffn1_w_up", "ffn1_w_down")
_TRANSPOSED = ("ffn1_w_gate", "ffn1_w_up", "ffn2_w_gate", "ffn2_w_up", "w_in", "w_q_up")


def _local_view(name, a):
    return jnp.swapaxes(a, 1, 2)[0] if name in _TRANSPOSED else a[0]


def _global_view(name, a):
    return jnp.swapaxes(a[None], 1, 2) if name in _TRANSPOSED else a[None]


def _pad_shard(name, a):
    if name == "w_in":
        return jnp.concatenate([a[:W_IN_SPLIT], jnp.zeros((64, a.shape[1]), a.dtype), a[W_IN_SPLIT:]], axis=0)
    if name == "w_q_up":
        return jnp.pad(a, ((0, 64), (0, 0)))
    return a


def _unpad_shard(name, a):
    if name == "w_in":
        return jnp.concatenate([a[:, :W_IN_SPLIT], a[:, W_IN_SPLIT + 64:]], axis=1)
    if name == "w_q_up":
        return a[:, :192]
    return a


def _stacked(g):
    return g if g.ndim == 3 else g.reshape(N_CHIPS, g.shape[0] // N_CHIPS, g.shape[1])


class _Plan:
    AG_UNITS = (
        (("w_in", "w_q_up", "w_kv_up"), "ffn1_up"),
        (("w_out",), "w_in"),
        (("w_mq",), "qkv_prep"),
        (("w_mkv", "w_mo", "ffn2_w_gate"), "mla_fwd"),
        (("ffn2_w_up",), "xattn_fwd"),
        (("ffn2_w_down",), "ffn2_up"),
    )
    RS_UNITS = (
        (("ffn2_w_gate", "ffn2_w_up", "ffn2_w_down"), "ffn2_dn_a", "mla_bwd", "qkv_prep_bwd"),
        (("w_mo", "w_mq", "w_mkv"), "w_out_dx", "mla_bwd", "qkv_prep_bwd"),
        (("w_out", "w_q_up", "w_kv_up", "w_in"), "w_in_dx", "ffn1_dact", "ffn1_dwd"),
        (("ffn1_w_down",), "ffn1_dwg", "ffn1_dwu", "ffn1_dn_a"),
        (("ffn1_w_gate",), "ffn1_dwu", "ffn1_dn_a", "ffn1_dn_b"),
        (("ffn1_w_up",), "ffn1_dn_a", "ffn1_dn_b", "adamw_w_kv_up"),
    )
    ADAMW_ORDER = (("w_kv_up",), ("ffn2_w_gate", "ffn2_w_up"), ("ffn2_w_down", "ffn1_w_down"), ("w_mo", "w_mq", "w_out"),
                   ("w_mkv",), ("w_q_up",), ("w_in",), ("ffn1_w_gate", "ffn1_w_up"))

    def __init__(self, shards, w, grads, core):
        self.shards, self.w, self.grads, self.core = shards, w, grads, core
        self.last_slab_step = 0
        self.parts = {}
        self.ag = [None for _ in self.AG_UNITS]
        self.rs = [[None, None, None, None] for _ in self.RS_UNITS]

    def pre(self, name):
        for i, (names, host) in enumerate(self.AG_UNITS):
            if name == host:
                st = _ag_ici_stage([self.shards[n] for n in names])
                st.then = _ag_d2d_stage(st.outs)
                st.start_step = self.last_slab_step if name == "ffn1_up" else 0
                self.ag[i] = _host(name, st)
        for i, (names, h1, h2, h3) in enumerate(self.RS_UNITS):
            if name == h1:
                self.rs[i][0] = _host(name, _rs_swap_stage([_stacked(self.grads[n]) for n in names]))
            if name == h2:
                self.rs[i][2] = _host(name, _rs_scatter_stage(self.rs[i][1], relative=names[0] in _FFN1))
            if name == h3:
                self.rs[i][3] = _host(name, _rs_mirror_stage(self.rs[i][2].results))

    def post(self, name):
        for i, (names, host) in enumerate(self.AG_UNITS):
            if name == host:
                for n, f in zip(names, self.ag[i].results):
                    self.w[n] = _full_weight(n, f)
        for i, (names, h1, h2, h3) in enumerate(self.RS_UNITS):
            if name == h1:
                self.rs[i][1] = list(_pair_add("pair_add_" + names[0], [_stacked(self.grads[n]) for n in names],
                                               self.rs[i][0].results, self.core))
            if name == h3:
                for n, p in zip(names, self.rs[i][3].results):
                    self.parts[n] = p


def _full_weight(name, stacked):
    if name in ("w_out", "w_mq", "w_mo"):
        return stacked.reshape(D_MODEL, D_MODEL)
    return stacked


def kernel(x, mem, positions, ffn1_norm, ffn1_w_gate, ffn1_w_up, ffn1_w_down, mix_norm, w_in, q_norm, w_q_up, kv_norm, w_kv_up, pool_w, pool_scale, w_out, xattn_norm, mem_norm, w_mq, w_mkv, w_mo, ffn2_norm, ffn2_w_gate, ffn2_w_up, ffn2_w_down, final_norm, loss_target, m_ffn1_norm, m_ffn1_w_gate, m_ffn1_w_up, m_ffn1_w_down, m_mix_norm, m_w_in, m_q_norm, m_w_q_up, m_kv_norm, m_w_kv_up, m_pool_w, m_pool_scale, m_w_out, m_xattn_norm, m_mem_norm, m_w_mq, m_w_mkv, m_w_mo, m_ffn2_norm, m_ffn2_w_gate, m_ffn2_w_up, m_ffn2_w_down, m_final_norm, v_ffn1_norm, v_ffn1_w_gate, v_ffn1_w_up, v_ffn1_w_down, v_mix_norm, v_w_in, v_q_norm, v_w_q_up, v_kv_norm, v_w_kv_up, v_pool_w, v_pool_scale, v_w_out, v_xattn_norm, v_mem_norm, v_w_mq, v_w_mkv, v_w_mo, v_ffn2_norm, v_ffn2_w_gate, v_ffn2_w_up, v_ffn2_w_down, v_final_norm):
    wts = dict(zip(_WEIGHTS, (ffn1_norm, ffn1_w_gate, ffn1_w_up, ffn1_w_down, mix_norm, w_in, q_norm, w_q_up, kv_norm, w_kv_up, pool_w, pool_scale, w_out, xattn_norm, mem_norm, w_mq, w_mkv, w_mo, ffn2_norm, ffn2_w_gate, ffn2_w_up, ffn2_w_down, final_norm)))
    mom = dict(zip(_WEIGHTS, (m_ffn1_norm, m_ffn1_w_gate, m_ffn1_w_up, m_ffn1_w_down, m_mix_norm, m_w_in, m_q_norm, m_w_q_up, m_kv_norm, m_w_kv_up, m_pool_w, m_pool_scale, m_w_out, m_xattn_norm, m_mem_norm, m_w_mq, m_w_mkv, m_w_mo, m_ffn2_norm, m_ffn2_w_gate, m_ffn2_w_up, m_ffn2_w_down, m_final_norm)))
    var = dict(zip(_WEIGHTS, (v_ffn1_norm, v_ffn1_w_gate, v_ffn1_w_up, v_ffn1_w_down, v_mix_norm, v_w_in, v_q_norm, v_w_q_up, v_kv_norm, v_w_kv_up, v_pool_w, v_pool_scale, v_w_out, v_xattn_norm, v_mem_norm, v_w_mq, v_w_mkv, v_w_mo, v_ffn2_norm, v_ffn2_w_gate, v_ffn2_w_up, v_ffn2_w_down, v_final_norm)))
    small = [n for n in _WEIGHTS if n not in _SHARDED]

    global _PLAN
    shards = {n: _pad_shard(n, _local_view(n, wts[n])).astype(BF16) for n in _SHARDED}
    w = {n: wts[n].reshape(1, -1) for n in _SMALL_VECTORS}
    w["pool_w"] = pool_w[0].astype(BF16)
    grads = {}
    core = lax.axis_index("c").astype(jnp.int32).reshape(1)
    plan = _Plan(shards, w, grads, core)
    _PLAN = plan
    try:
        w["ffn1_shards"] = tuple(shards[n] for n in _FFN1)

        loss_local, dx = _local_step(x[0], mem[0], positions[0], loss_target[0], w, grads)

        def small_view(d):
            out = {n: d[n].reshape(1, -1) for n in _SMALL_VECTORS}
            out["pool_w"] = d["pool_w"].reshape(_POOL_ROWS, POOL_CH)
            return out

        *small_res, loss_vec = _small_params_step(small_view(grads), small_view(wts), small_view(mom),
                                                  small_view(var), loss_local)
        g_out, d_out, m_out, v_out = ({n: r[n].reshape(wts[n].shape) for n in small} for r in small_res)
        loss = loss_vec[0, 0]

        for names in _Plan.ADAMW_ORDER:
            res = _adamw_sum("adamw_" + names[0], [_local_view(n, wts[n]) for n in names],
                             [_unpad_shard(n, plan.parts[n]) for n in names],
                             [_local_view(n, mom[n]) for n in names], [_local_view(n, var[n]) for n in names])
            for n, r4 in zip(names, res):
                g_out[n], d_out[n], m_out[n], v_out[n] = (_global_view(n, r) for r in r4)
    finally:
        _PLAN = None
        _PENDING.clear()

    return (loss, dx[None], *[g_out[n] for n in _WEIGHTS], *[d_out[n] for n in _WEIGHTS],
            *[m_out[n] for n in _WEIGHTS], *[v_out[n] for n in _WEIGHTS])
```

```python
import jax
import jax.numpy as jnp
from jax import lax
from jax.experimental import pallas as pl
from jax.experimental.pallas import tpu as pltpu

F32 = jnp.float32
BF16 = jnp.bfloat16

D_MODEL = 1024
D_FF = 2816
N_CHIPS = 4
FF_SHARD = D_FF // N_CHIPS
MLA_HEADS = 4
Q_LORA = 256
KV_LORA = 128
ROPE_DIM = 64
HEAD_QK = 256
HEAD_V = 128
POOL_GROUPS = 4
POOL_CH = 128
MEM_HEADS = 4
MEM_HEAD_DIM = 256
RMS_EPS = 1e-6
ROPE_BASE = 10000.0
MLA_SCALE = (128 + 64) ** -0.5
MEM_SCALE = MEM_HEAD_DIM ** -0.5

ADAM_LR = 0.001
ADAM_B1 = 0.9
ADAM_B2 = 0.999
ADAM_EPS = 1e-08
ADAM_WD = 0.01
ADAM_STEP = 10

V7X_VMEM_LIMIT_BYTES = 56 * 1024 * 1024

NN = ((1,), (0,))
NT = ((1,), (1,))
TN = ((0,), (0,))


def _params(*sem):
    return pltpu.CompilerParams(dimension_semantics=sem, vmem_limit_bytes=V7X_VMEM_LIMIT_BYTES)


_MESH = pl.DeviceIdType.MESH
_ANY = pl.BlockSpec(memory_space=pl.ANY)


class _Stage:
    def __init__(self, ins, outs, n_remote, n_local, copies, aliases=None):
        self.ins, self.outs, self.n_remote, self.n_local = list(ins), list(outs), n_remote, n_local
        self.copies, self.aliases = copies, dict(aliases or {})
        self.results = None
        self.start_step = 0
        self.then = None

    def descriptors(self, in_refs, out_refs, send, recv, loc):
        ds, ri, li = [], 0, 0
        for src, dst, dev in self.copies(in_refs, out_refs):
            if dev is None:
                ds.append(pltpu.make_async_copy(src, dst, loc.at[li]))
                li += 1
            else:
                ds.append(pltpu.make_async_remote_copy(src_ref=src, dst_ref=dst, send_sem=send.at[ri],
                                                       recv_sem=recv.at[ri], device_id=dev, device_id_type=_MESH))
                ri += 1
        assert ri == self.n_remote and li == self.n_local
        return ds


_PENDING = {}


def _host(name, stage):
    _PENDING.setdefault(name, []).append(stage)
    return stage


_PLAN = None


def _call(body, **kw):
    if _PLAN is not None:
        _PLAN.pre(kw["name"])
    res = _call_hosting(body, **kw)
    if _PLAN is not None:
        _PLAN.post(kw["name"])
    return res


def _call_hosting(body, *, name, grid, in_specs, out_specs, out_shape, sem, args, scratch_shapes=(), aliases=None):
    stages = _PENDING.pop(name, [])
    scratch_shapes = list(scratch_shapes)
    if not stages:
        return pl.pallas_call(body, name=name, grid=grid, in_specs=in_specs, out_specs=out_specs,
                              out_shape=out_shape, scratch_shapes=scratch_shapes,
                              input_output_aliases=dict(aliases or {}), compiler_params=_params(*sem))(*args)
    ni, no, ns = len(in_specs), len(out_shape), len(scratch_shapes)
    c_ins = [a for st in stages for a in st.ins]
    c_outs = [o for st in stages for o in st.outs]
    nci, nco = len(c_ins), len(c_outs)
    aliases, io, oo = dict(aliases or {}), 0, 0
    for st in stages:
        for i, j in st.aliases.items():
            aliases[ni + io + i] = no + oo + j
        io += len(st.ins)
        oo += len(st.outs)
    dma = pltpu.SemaphoreType.DMA
    sems = []
    for st in stages:
        sems += [dma((max(st.n_remote, 1),)), dma((max(st.n_remote, 1),)), dma((max(st.n_local, 1),))]
    followers = [st.then for st in stages if st.then is not None]
    for st in followers:
        sems += [dma((max(st.n_remote, 1),)), dma((max(st.n_remote, 1),)), dma((max(st.n_local, 1),))]

    def wrapped(*refs):
        ins, cin = refs[:ni], refs[ni:ni + nci]
        outs, cout = refs[ni + nci:ni + nci + no], refs[ni + nci + no:ni + nci + no + nco]
        scr = refs[ni + nci + no + nco:ni + nci + no + nco + ns]
        sem_refs = refs[ni + nci + no + nco + ns:]
        step = pl.program_id(0)
        last = pl.program_id(0) == grid[0] - 1
        for ax in range(1, len(grid)):
            step = step * grid[ax] + pl.program_id(ax)
            last = jnp.logical_and(last, pl.program_id(ax) == grid[ax] - 1)

        def descriptors(si):
            io = sum(len(st.ins) for st in stages[:si])
            oo = sum(len(st.outs) for st in stages[:si])
            st = stages[si]
            return st.descriptors(cin[io:io + len(st.ins)], cout[oo:oo + len(st.outs)], *sem_refs[3 * si:3 * si + 3])

        def follower_descriptors(fi):
            si = [k for k, st in enumerate(stages) if st.then is not None][fi]
            oo = sum(len(st.outs) for st in stages[:si])
            bufs = cout[oo:oo + len(stages[si].outs)]
            k0 = 3 * (len(stages) + fi)
            return followers[fi].descriptors(bufs, bufs, *sem_refs[k0:k0 + 3])

        def start(si):
            @pl.when(step == stages[si].start_step)
            def _():
                for d in descriptors(si):
                    d.start()

        for si, st in enumerate(stages):
            if st.start_step == 0:
                start(si)
        body(*ins, *outs, *scr)
        for si, st in enumerate(stages):
            if st.start_step != 0:
                start(si)

        @pl.when(last)
        def _():
            for si in range(len(stages)):
                for d in descriptors(si):
                    d.wait()
            for fi in range(len(followers)):
                for d in follower_descriptors(fi):
                    d.start()
            for fi in range(len(followers)):
                for d in follower_descriptors(fi):
                    d.wait()

    res = pl.pallas_call(
        wrapped, name=name, grid=grid, in_specs=list(in_specs) + [_ANY] * nci,
        out_specs=list(out_specs) + [_ANY] * nco, out_shape=list(out_shape) + c_outs,
        scratch_shapes=scratch_shapes + sems, input_output_aliases=aliases,
        compiler_params=_params(*(("arbitrary",) * len(grid))))(*args, *c_ins)
    oo = no
    for st in stages:
        st.results = list(res[oo:oo + len(st.outs)])
        oo += len(st.outs)
    return list(res[:no])


def _dot(a, b, dims):
    return lax.dot_general(a.astype(BF16), b.astype(BF16), (dims, ((), ())), preferred_element_type=F32)


_MAX_ROW_BLOCK = 1024
_ATT_BLOCK = 512


_MAX_REDUCE_BLOCK = 2048


def _row_block(s, want=1024):
    return min(want, s, _MAX_ROW_BLOCK)


def _reduce_block(s):
    return min(s, _MAX_REDUCE_BLOCK)


def _matmul(name, grid, terms, extras, outs, epilogue, acc_shape, fill=(), summed=()):
    nt, ne, no, nf = len(terms), len(extras), len(outs), len(fill)
    nk = grid[-1]
    dims = [t[4] for t in terms]

    def body(*refs):
        a_refs, b_refs = refs[:nt], refs[nt:2 * nt]
        e_refs = refs[2 * nt:2 * nt + ne]
        o_refs = refs[2 * nt + ne + nf:2 * nt + ne + nf + no]

        def finish(acc):
            vals = epilogue(acc, *[e[...] for e in e_refs])
            for idx, (o, val) in enumerate(zip(o_refs, vals)):
                if idx in summed:
                    @pl.when(pl.program_id(0) == 0)
                    def _(o=o, val=val):
                        o[...] = val.astype(o.dtype)

                    @pl.when(pl.program_id(0) > 0)
                    def _(o=o, val=val):
                        o[...] += val.astype(o.dtype)
                else:
                    o[...] = val.astype(o.dtype)

        if nk == 1:
            part = None
            for a, b, d in zip(a_refs, b_refs, dims):
                t = _dot(a[...], b[...], d)
                part = t if part is None else part + t
            finish(part)
        else:
            acc_ref = refs[-1]
            k = pl.program_id(len(grid) - 1)

            @pl.when(k == 0)
            def _():
                acc_ref[...] = jnp.zeros_like(acc_ref)

            for a, b, d in zip(a_refs, b_refs, dims):
                acc_ref[...] += _dot(a[...], b[...], d)

            @pl.when(k == nk - 1)
            def _():
                finish(acc_ref[...])

    in_specs = [t[1] for t in terms] + [t[3] for t in terms] + [e[1] for e in extras] + [_ANY] * nf
    args = [t[0] for t in terms] + [t[2] for t in terms] + [e[0] for e in extras] + list(fill)
    sem = ("arbitrary" if summed else "parallel",) * (len(grid) - 1) + ("arbitrary",)
    aliases = {2 * nt + ne + i: i for i in range(nf)}
    return _call(
        body, name=name, grid=grid, in_specs=in_specs,
        out_specs=[o[1] for o in outs], out_shape=[o[0] for o in outs],
        scratch_shapes=[pltpu.VMEM(acc_shape, F32)] if nk > 1 else [], sem=sem, args=args, aliases=aliases)


def _ident(acc):
    return (acc,)


def _rmsnorm_fwd(name, x, gain, width, col_block=0):
    s = x.shape[0]
    bm = _row_block(s)

    def body(x_ref, g_ref, o_ref):
        xf = x_ref[...]
        r = lax.rsqrt(jnp.mean(xf * xf, axis=-1, keepdims=True) + RMS_EPS)
        o_ref[...] = ((xf * r) * g_ref[...]).astype(o_ref.dtype)

    return pl.pallas_call(
        body, name=name, grid=(s // bm,),
        in_specs=[pl.BlockSpec((bm, width), lambda i: (i, col_block)), pl.BlockSpec((1, width), lambda i: (0, 0))],
        out_specs=pl.BlockSpec((bm, width), lambda i: (i, 0)),
        out_shape=jax.ShapeDtypeStruct((s, width), BF16),
        compiler_params=_params("parallel"),
    )(x, gain)


def _rms_bwd_math(dy, xf, g, width):
    r = lax.rsqrt(jnp.mean(xf * xf, axis=-1, keepdims=True) + RMS_EPS)
    dyg = dy * g
    dot = jnp.sum(dyg * xf, axis=-1, keepdims=True)
    dx = r * dyg - xf * ((r * r * r) * (dot * (1.0 / width)))
    dgain = jnp.sum(dy * (xf * r), axis=0, keepdims=True)
    return dx, dgain


def _rmsnorm_bwd(name, dy, x, gain, width, col_block=0, dres=None, out_dtype=F32):
    s = x.shape[0]
    bm = _row_block(s)
    has_res = dres is not None

    def body(*refs):
        if has_res:
            dy_ref, x_ref, g_ref, r_ref, dx_ref, dg_ref = refs
        else:
            dy_ref, x_ref, g_ref, dx_ref, dg_ref = refs
        dx, dgain = _rms_bwd_math(dy_ref[...].astype(F32), x_ref[...], g_ref[...], width)
        if has_res:
            dx = dx + r_ref[...]
        dx_ref[...] = dx.astype(dx_ref.dtype)

        @pl.when(pl.program_id(0) == 0)
        def _():
            dg_ref[...] = dgain

        @pl.when(pl.program_id(0) > 0)
        def _():
            dg_ref[...] += dgain

    row = pl.BlockSpec((bm, width), lambda i: (i, 0))
    in_specs = [row, pl.BlockSpec((bm, width), lambda i: (i, col_block)), pl.BlockSpec((1, width), lambda i: (0, 0))]
    args = [dy, x, gain]
    out_specs = [row, pl.BlockSpec((1, width), lambda i: (0, 0))]
    out_shape = [jax.ShapeDtypeStruct((s, width), out_dtype), jax.ShapeDtypeStruct((1, width), F32)]
    if has_res:
        in_specs.append(row)
        args.append(dres)
    return _call(body, name=name, grid=(s // bm,), in_specs=in_specs, out_specs=out_specs, out_shape=out_shape,
                 sem=("arbitrary",), args=args)


def _ffn_up(name, n, wg, wu):
    s = n.shape[0]
    bm = _row_block(s)

    def body(n_ref, wg_ref, wu_ref, a_ref, dadu_ref, dadg_ref):
        x = n_ref[...]
        g = _dot(x, wg_ref[...], NT)
        u = _dot(x, wu_ref[...], NT)
        sg = jax.nn.sigmoid(g)
        silu = g * sg
        a_ref[...] = (silu * u).astype(BF16)
        dadu_ref[...] = silu.astype(BF16)
        dadg_ref[...] = (u * (sg * (1.0 + g * (1.0 - sg)))).astype(BF16)

    w_spec = pl.BlockSpec((None, FF_SHARD, D_MODEL), lambda j, i: (j, 0, 0))
    o_spec = pl.BlockSpec((None, bm, FF_SHARD), lambda j, i: (j, i, 0))
    shp = jax.ShapeDtypeStruct((N_CHIPS, s, FF_SHARD), BF16)
    return _call(
        body, name=name, grid=(N_CHIPS, s // bm),
        in_specs=[pl.BlockSpec((bm, D_MODEL), lambda j, i: (i, 0)), w_spec, w_spec],
        out_specs=[o_spec, o_spec, o_spec], out_shape=[shp, shp, shp],
        sem=("parallel", "parallel"), args=[n, wg, wu])


def _ffn1_up_gather(xin, gain, g_sh, u_sh, d_sh):
    s = xin.shape[0]
    bm = _row_block(s)
    nrb = s // bm
    rows, cols = g_sh.shape

    def body(x_ref, gain_ref, gs, us, ds, n_ref, a_ref, dadu_ref, dadg_ref, wg, wu, wd, gbuf, ubuf,
             send, recv, qsend, qrecv, fsend, frecv, loc, ld):
        r, i = pl.program_id(0), pl.program_id(1)
        x, y, c = lax.axis_index("x"), lax.axis_index("y"), lax.axis_index("c")
        sib = (x, y, 1 - c)
        mine, _ = _half_rows(c, rows)
        quarters = _quarter_rows(c, rows)
        shards, fulls, bufs = (gs, us, ds), (wg, wu, wd), (gbuf, ubuf)

        def remote(src, dst, ssem, rsem, dev):
            return pltpu.make_async_remote_copy(src_ref=src, dst_ref=dst, send_sem=ssem, recv_sem=rsem,
                                                device_id=dev, device_id_type=_MESH)

        def peer(rel):
            return ((1 - x) if rel & 2 else x, (1 - y) if rel & 1 else y, c)

        def ici(k, rel, dev=sib):
            return remote(shards[k].at[mine], fulls[k].at[rel, mine], send.at[k, rel - 1], recv.at[k, rel - 1], dev)

        def quarter(k, which, dev=sib):
            slab, q = ((2, quarters[0]), (1, quarters[1]))[which]
            return remote(fulls[k].at[slab, q], fulls[k].at[3, q], qsend.at[k, which], qrecv.at[k, which], dev)

        def fwd(k, rel):
            return remote(fulls[k].at[rel, mine], fulls[k].at[rel, mine], fsend.at[k, rel - 1], frecv.at[k, rel - 1], sib)

        def own(k):
            return pltpu.make_async_copy(shards[k], fulls[k].at[0], loc.at[k])

        def load(slab):
            for k in (0, 1):
                pltpu.make_async_copy(shards[k] if slab == 0 else fulls[k].at[slab], bufs[k], ld.at[k]).start()
            for k in (0, 1):
                pltpu.make_async_copy(shards[k] if slab == 0 else fulls[k].at[slab], bufs[k], ld.at[k]).wait()

        def from_neighbour(ks, rel):
            for k in ks:
                ici(k, rel).wait_recv()
                fwd(k, rel).start()
                quarter(k, 0 if rel == 2 else 1, peer(1 if rel == 2 else 2)).start()
            for k in ks:
                fwd(k, rel).wait_recv()

        def from_diagonal(ks):
            for k in ks:
                quarter(k, 0).wait_recv()
                quarter(k, 1).wait_recv()
                fwd(k, 3).start()
            for k in ks:
                fwd(k, 3).wait_recv()

        @pl.when(jnp.logical_and(r == 0, i == 0))
        def _():
            for k in range(3):
                own(k).start()
            for rel in (1, 2):
                for k in (0, 1):
                    ici(k, rel, peer(rel)).start()
            load(0)

        @pl.when(jnp.logical_and(r == 1, i == 0))
        def _():
            from_neighbour((0, 1), 1)
            load(1)
            for rel in (1, 2):
                ici(2, rel, peer(rel)).start()

        @pl.when(jnp.logical_and(r == 2, i == 0))
        def _():
            from_neighbour((0, 1), 2)
            load(2)

        @pl.when(jnp.logical_and(r == 3, i == 0))
        def _():
            from_diagonal((0, 1))
            load(3)

        xv = _norm_bf16(x_ref[...], gain_ref[...])

        @pl.when(r == 0)
        def _():
            n_ref[...] = xv

        g = _dot(xv, gbuf[...], NT)
        u = _dot(xv, ubuf[...], NT)
        sg = jax.nn.sigmoid(g)
        silu = g * sg
        a_ref[...] = (silu * u).astype(BF16)
        dadu_ref[...] = silu.astype(BF16)
        dadg_ref[...] = (u * (sg * (1.0 + g * (1.0 - sg)))).astype(BF16)

        @pl.when(jnp.logical_and(r == 3, i == nrb - 1))
        def _():
            from_neighbour((2,), 1)
            from_neighbour((2,), 2)
            from_diagonal((2,))
            for k in range(3):
                for rel in (1, 2):
                    ici(k, rel).wait_send()
                for which in (0, 1):
                    quarter(k, which).wait_send()
                for rel in (1, 2, 3):
                    fwd(k, rel).wait_send()
                own(k).wait()

    o_spec = pl.BlockSpec((None, bm, FF_SHARD), lambda r, i: (r, i, 0))
    act = jax.ShapeDtypeStruct((N_CHIPS, s, FF_SHARD), BF16)
    full = jax.ShapeDtypeStruct((N_CHIPS, rows, cols), BF16)
    dma = pltpu.SemaphoreType.DMA
    if _PLAN is not None:
        _PLAN.last_slab_step = 3 * nrb
    n_spec = pl.BlockSpec((bm, D_MODEL), lambda r, i: (jnp.where(r == 0, i, nrb - 1), 0))
    return _call(
        body, name="ffn1_up", grid=(N_CHIPS, nrb),
        in_specs=[pl.BlockSpec((bm, D_MODEL), lambda r, i: (i, 0)), pl.BlockSpec((1, D_MODEL), lambda r, i: (0, 0)),
                  _ANY, _ANY, _ANY],
        out_specs=[n_spec, o_spec, o_spec, o_spec, _ANY, _ANY, _ANY],
        out_shape=[jax.ShapeDtypeStruct((s, D_MODEL), BF16), act, act, act, full, full, full],
        scratch_shapes=[pltpu.VMEM((rows, cols), BF16), pltpu.VMEM((rows, cols), BF16), dma((3, 2)), dma((3, 2)),
                        dma((3, 2)), dma((3, 2)), dma((3, 3)), dma((3, 3)), dma((3,)), dma((2,))],
        sem=("arbitrary", "arbitrary"), args=[xin, gain, g_sh, u_sh, d_sh])


def _residual_epilogue(alpha, with_norm):
    if not with_norm:
        return lambda acc, r: (r + alpha * acc,)

    def epilogue(acc, r, g):
        h = r + alpha * acc
        rs = lax.rsqrt(jnp.mean(h * h, axis=-1, keepdims=True) + RMS_EPS)
        return h, (h * rs) * g

    return epilogue


def _residual_outs(s, bm, gain):
    row = pl.BlockSpec((bm, D_MODEL), lambda i, k: (i, 0))
    outs = [(jax.ShapeDtypeStruct((s, D_MODEL), F32), row)]
    if gain is None:
        return [], outs
    return [(gain, pl.BlockSpec((1, D_MODEL), lambda i, k: (0, 0)))], outs + [(jax.ShapeDtypeStruct((s, D_MODEL), BF16), row)]


def _loss_epilogue(acc, res, g, target):
    d = acc.shape[-1]
    h = res + 0.5 * acc
    r = lax.rsqrt(jnp.mean(h * h, axis=-1, keepdims=True) + RMS_EPS)
    err = (h * r) * g - target
    part = 0.5 * jnp.sum(jnp.mean(err * err, axis=-1, keepdims=True), axis=0, keepdims=True)
    dx, dgain = _rms_bwd_math(err * (1.0 / d), h, g, d)
    return dx, dx, jnp.broadcast_to(part, (1, 128)), dgain


def _ffn_down(name, a, wd, res, gain=None, loss=None):
    s = a.shape[1]
    bm = _row_block(s, 512)
    row = pl.BlockSpec((bm, D_MODEL), lambda i, k: (i, 0))
    terms = [(a, pl.BlockSpec((None, bm, FF_SHARD), lambda i, k, j=j: (j, i, 0)),
              wd, pl.BlockSpec((None, FF_SHARD, D_MODEL), lambda i, k, j=j: (j, 0, 0)), NN) for j in range(N_CHIPS)]
    if loss is not None:
        vec = pl.BlockSpec((1, D_MODEL), lambda i, k: (0, 0))
        outs = [(jax.ShapeDtypeStruct((s, D_MODEL), F32), row), (jax.ShapeDtypeStruct((s, D_MODEL), BF16), row),
                (jax.ShapeDtypeStruct((1, 128), F32), pl.BlockSpec((1, 128), lambda i, k: (0, 0))),
                (jax.ShapeDtypeStruct((1, D_MODEL), F32), vec)]
        return _matmul(name, (s // bm, 1), terms, [(res, row), (loss[0], vec), (loss[1], row)], outs,
                       _loss_epilogue, None, summed=(2, 3))
    extras, outs = _residual_outs(s, bm, gain)
    res_out = _matmul(name, (s // bm, 1), terms, [(res, row)] + extras, outs,
                      _residual_epilogue(0.5, gain is not None), None)
    return res_out if gain is not None else res_out[0]


def _norm_bwd_epilogue(width):
    def epilogue(acc, h, g, dres):
        dx, dgain = _rms_bwd_math(acc, h, g, width)
        dx = dx + dres
        return dx, dx, dgain

    return epilogue


def _norm_bwd_operands(s, bm, h, gain, dres):
    row = pl.BlockSpec((bm, D_MODEL), lambda i, k: (i, 0))
    vec = pl.BlockSpec((1, D_MODEL), lambda i, k: (0, 0))
    extras = [(h, row), (gain, vec), (dres, row)]
    outs = [(jax.ShapeDtypeStruct((s, D_MODEL), F32), row), (jax.ShapeDtypeStruct((s, D_MODEL), BF16), row),
            (jax.ShapeDtypeStruct((1, D_MODEL), F32), vec)]
    return extras, outs, (2,)


def _ffn_bwd(tag, dh, n, dadg, dadu, a, wg, wu, wd, grads, norm_bwd=None):
    s = dh.shape[0]
    bm = _row_block(s)
    bk = _reduce_block(s)
    nk = s // bk

    def act_bwd(acc, dg_da, du_da):
        da = 0.5 * acc
        return da * dg_da.astype(F32), da * du_da.astype(F32)

    slab = pl.BlockSpec((None, bm, FF_SHARD), lambda j, i, k: (j, i, 0))
    shp = jax.ShapeDtypeStruct((N_CHIPS, s, FF_SHARD), BF16)
    dg, du = _matmul(
        tag + "_dact", (N_CHIPS, s // bm, 1),
        [(dh, pl.BlockSpec((bm, D_MODEL), lambda j, i, k: (i, 0)),
          wd, pl.BlockSpec((None, FF_SHARD, D_MODEL), lambda j, i, k: (j, 0, 0)), NT)],
        [(dadg, slab), (dadu, slab)], [(shp, slab), (shp, slab)], act_bwd, None)

    grads[tag + "_w_down"] = _matmul(
        tag + "_dwd", (N_CHIPS, nk),
        [(a, pl.BlockSpec((None, bk, FF_SHARD), lambda j, k: (j, k, 0)),
          dh, pl.BlockSpec((bk, D_MODEL), lambda j, k: (k, 0)), TN)],
        [], [(jax.ShapeDtypeStruct((N_CHIPS, FF_SHARD, D_MODEL), BF16),
              pl.BlockSpec((None, FF_SHARD, D_MODEL), lambda j, k: (j, 0, 0)))],
        lambda acc: (0.5 * acc,), (FF_SHARD, D_MODEL))[0]

    def dw_up(nm, dact):
        return _matmul(
            nm, (N_CHIPS, nk),
            [(dact, pl.BlockSpec((None, bk, FF_SHARD), lambda j, k: (j, k, 0)),
              n, pl.BlockSpec((bk, D_MODEL), lambda j, k: (k, 0)), TN)],
            [], [(jax.ShapeDtypeStruct((N_CHIPS, FF_SHARD, D_MODEL), BF16),
                  pl.BlockSpec((None, FF_SHARD, D_MODEL), lambda j, k: (j, 0, 0)))],
            _ident, (FF_SHARD, D_MODEL))[0]

    grads[tag + "_w_gate"] = dw_up(tag + "_dwg", dg)
    grads[tag + "_w_up"] = dw_up(tag + "_dwu", du)

    bn = _row_block(s, 512)
    steps = s // bn // 2
    prev, dgain = (), None
    for part, off in (("_dn_a", 0), ("_dn_b", steps)):
        row = pl.BlockSpec((bn, D_MODEL), lambda i, k, off=off: (i + off, 0))
        terms = []
        for j in range(N_CHIPS):
            a_slab = pl.BlockSpec((None, bn, FF_SHARD), lambda i, k, j=j, off=off: (j, i + off, 0))
            w_slab = pl.BlockSpec((None, FF_SHARD, D_MODEL), lambda i, k, j=j: (j, 0, 0))
            terms += [(dg, a_slab, wg, w_slab, NN), (du, a_slab, wu, w_slab, NN)]
        if norm_bwd is None:
            prev = _matmul(tag + part, (steps, 1), terms, [], [(jax.ShapeDtypeStruct((s, D_MODEL), F32), row)],
                           _ident, None, fill=prev)
            continue
        h, gain, dres = norm_bwd
        vec = pl.BlockSpec((1, D_MODEL), lambda i, k: (0, 0))
        res = _matmul(
            tag + part, (steps, 1), terms, [(h, row), (gain, vec), (dres, row)],
            [(jax.ShapeDtypeStruct((s, D_MODEL), F32), row), (jax.ShapeDtypeStruct((s, D_MODEL), BF16), row),
             (jax.ShapeDtypeStruct((1, D_MODEL), F32), vec)],
            _norm_bwd_epilogue(D_MODEL), None, fill=prev, summed=(2,))
        prev = res[:2]
        dgain = res[2] if dgain is None else dgain + res[2]
    return prev[0] if norm_bwd is None else (prev[0], prev[1], dgain)


def _mm_nn(name, a, b, out_dtype, res=None, gain=None):
    s, k = a.shape
    nn = b.shape[1]
    bm = _row_block(s)
    row = pl.BlockSpec((bm, nn), lambda i, kk: (i, 0))
    term = [(a, pl.BlockSpec((bm, k), lambda i, kk: (i, 0)), b, pl.BlockSpec((k, nn), lambda i, kk: (0, 0)), NN)]
    if res is None:
        return _matmul(name, (s // bm, 1), term, [], [(jax.ShapeDtypeStruct((s, nn), out_dtype), row)], _ident, None)[0]
    extras, outs = _residual_outs(s, bm, gain)
    res_out = _matmul(name, (s // bm, 1), term, [(res, row)] + extras, outs,
                      _residual_epilogue(1.0, gain is not None), None)
    return res_out if gain is not None else res_out[0]


def _mm_nt(name, a, b, out_dtype, attn_out=None, nh=0, dv=0):
    s, nn = a.shape
    k = b.shape[0]
    bm = _row_block(s)
    term = [(a, pl.BlockSpec((bm, nn), lambda i, kk: (i, 0)), b, pl.BlockSpec((k, nn), lambda i, kk: (0, 0)), NT)]
    out = (jax.ShapeDtypeStruct((s, k), out_dtype), pl.BlockSpec((bm, k), lambda i, kk: (i, 0)))
    if attn_out is None:
        return _matmul(name, (s // bm, 1), term, [], [out], _ident, None)[0]

    def with_delta(acc, o):
        do = acc.astype(out_dtype).astype(F32)
        cols = [jnp.sum(do[:, h * dv:(h + 1) * dv] * o[:, h * dv:(h + 1) * dv].astype(F32), axis=-1, keepdims=True)
                for h in range(nh)]
        return acc, jnp.stack(cols, axis=0)

    return _matmul(
        name, (s // bm, 1), term, [(attn_out, pl.BlockSpec((bm, nh * dv), lambda i, kk: (i, 0)))],
        [out, (jax.ShapeDtypeStruct((nh, s, 1), F32), pl.BlockSpec((nh, bm, 1), lambda i, kk: (0, i, 0)))],
        with_delta, None)


def _mm_nt_norm_bwd(name, a, b, h, gain, dres):
    s, nn = a.shape
    bm = _row_block(s, 512)
    extras, outs, summed = _norm_bwd_operands(s, bm, h, gain, dres)
    return _matmul(
        name, (s // bm, 1),
        [(a, pl.BlockSpec((bm, nn), lambda i, kk: (i, 0)), b, pl.BlockSpec(b.shape, lambda i, kk: (0, 0)), NT)],
        extras, outs, _norm_bwd_epilogue(D_MODEL), None, summed=summed)


def _w_in_dx_norm_bwd(dz, w_t, h, gain, dres):
    s = dz.shape[0]
    bm = _row_block(s, 512)
    epilogue = _norm_bwd_epilogue(D_MODEL)

    def body(dz_ref, w_ref, h_ref, g_ref, r_ref, dx_ref, dxb_ref, dg_ref):
        dzv = dz_ref[...]
        dn = jnp.concatenate([_dot(dzv, w_ref[j], NN) for j in range(N_CHIPS)], axis=1)
        dx, _, dgain = epilogue(dn, h_ref[...], g_ref[...], r_ref[...])
        dx_ref[...] = dx
        dxb_ref[...] = dx.astype(BF16)

        @pl.when(pl.program_id(0) == 0)
        def _():
            dg_ref[...] = dgain

        @pl.when(pl.program_id(0) > 0)
        def _():
            dg_ref[...] += dgain

    row = pl.BlockSpec((bm, D_MODEL), lambda i: (i, 0))
    vec = pl.BlockSpec((1, D_MODEL), lambda i: (0, 0))
    return _call(
        body, name="w_in_dx", grid=(s // bm,),
        in_specs=[row, pl.BlockSpec(w_t.shape, lambda i: (0, 0, 0)), row, vec, row],
        out_specs=[row, row, vec],
        out_shape=[jax.ShapeDtypeStruct((s, D_MODEL), F32), jax.ShapeDtypeStruct((s, D_MODEL), BF16),
                   jax.ShapeDtypeStruct((1, D_MODEL), F32)],
        sem=("arbitrary",), args=[dz, w_t, h, gain, dres])


def _mm_tn(name, a, b, out_dtype=BF16):
    s, k = a.shape
    nn = b.shape[1]
    bk = _reduce_block(s)
    return _matmul(
        name, (s // bk,),
        [(a, pl.BlockSpec((bk, k), lambda kk: (kk, 0)), b, pl.BlockSpec((bk, nn), lambda kk: (kk, 0)), TN)],
        [], [(jax.ShapeDtypeStruct((k, nn), out_dtype), pl.BlockSpec((k, nn), lambda kk: (0, 0)))],
        _ident, (k, nn))[0]


def _mm_heads_fwd(name, a, w, out_dtype, w_transposed=False):
    s, k = a.shape
    nh = w.shape[0]
    nn = w.shape[1] if w_transposed else w.shape[2]
    bm = _row_block(s)
    return _matmul(
        name, (nh, s // bm, 1),
        [(a, pl.BlockSpec((bm, k), lambda h, i, kk: (i, 0)),
          w, pl.BlockSpec((None,) + w.shape[1:], lambda h, i, kk: (h, 0, 0)), NT if w_transposed else NN)],
        [], [(jax.ShapeDtypeStruct((s, nh * nn), out_dtype), pl.BlockSpec((bm, nn), lambda h, i, kk: (i, h)))],
        _ident, None)[0]


def _mm_heads_bwd(name, dy, a, w, w_transposed=False):
    s, k = a.shape
    nh = w.shape[0]
    nn = w.shape[1] if w_transposed else w.shape[2]
    bm = _row_block(s)
    bk = _reduce_block(s)
    w_spec = pl.BlockSpec((None,) + w.shape[1:], lambda i, h: (h, 0, 0))
    da = _matmul(
        name + "_dx", (s // bm, nh),
        [(dy, pl.BlockSpec((bm, nn), lambda i, h: (i, h)), w, w_spec, NN if w_transposed else NT)],
        [], [(jax.ShapeDtypeStruct((s, k), F32), pl.BlockSpec((bm, k), lambda i, h: (i, 0)))], _ident, (bm, k))[0]
    a_term = (a, pl.BlockSpec((bk, k), lambda h, kk: (kk, 0)))
    dy_term = (dy, pl.BlockSpec((bk, nn), lambda h, kk: (kk, h)))
    lhs, rhs = (dy_term, a_term) if w_transposed else (a_term, dy_term)
    dw = _matmul(
        name + "_dw", (nh, s // bk), [lhs + rhs + (TN,)],
        [], [(jax.ShapeDtypeStruct(w.shape, BF16), pl.BlockSpec((None,) + w.shape[1:], lambda h, kk: (h, 0, 0)))],
        _ident, w.shape[1:])[0]
    return da, dw


def _w_in_fwd(n, w_t):
    s = n.shape[0]
    bm = _row_block(s)
    nh, nout, kin = w_t.shape
    terms = [(n, pl.BlockSpec((bm, kin), lambda i, k, j=j: (i, j)),
              w_t, pl.BlockSpec((None, nout, kin), lambda i, k, j=j: (j, 0, 0)), NT) for j in range(nh)]
    row = pl.BlockSpec((bm, nout), lambda i, k: (i, 0))
    return _matmul("w_in", (s // bm, 1), terms, [], [(jax.ShapeDtypeStruct((s, nout), F32), row)], _ident, None)[0]


def _w_in_dw(dz, n):
    s, nout = dz.shape
    kin = n.shape[1] // N_CHIPS
    bk = _reduce_block(s)
    return _matmul(
        "w_in_dw", (N_CHIPS, s // bk),
        [(dz, pl.BlockSpec((bk, nout), lambda j, k: (k, 0)), n, pl.BlockSpec((bk, kin), lambda j, k: (k, j)), TN)],
        [], [(jax.ShapeDtypeStruct((N_CHIPS, nout, kin), BF16), pl.BlockSpec((None, nout, kin), lambda j, k: (j, 0, 0)))],
        _ident, (nout, kin))[0]


def _rope_tables(positions):
    half = ROPE_DIM // 2
    freqs = 1.0 / (ROPE_BASE ** (jnp.arange(0, ROPE_DIM, 2, dtype=F32) / ROPE_DIM))
    ang = positions.astype(F32)[:, None] * freqs
    cos, sin = jnp.cos(ang), jnp.sin(ang)
    z = jnp.zeros_like(cos)
    tc = jnp.concatenate([cos, cos, z, z], axis=-1)
    ta = jnp.concatenate([-sin, z, z, z], axis=-1)
    tb = jnp.concatenate([z, sin, z, z], axis=-1)
    assert tc.shape[-1] == 4 * half
    return tc, ta, tb


def _rope(x, tc, ta, tb):
    return x * tc + pltpu.roll(x, 96, 1) * ta + pltpu.roll(x, 32, 1) * tb


def _rope_t(dy, tc, ta, tb):
    return dy * tc + pltpu.roll(dy * ta, 32, 1) + pltpu.roll(dy * tb, 96, 1)


def _norm_bf16(x, g):
    r = lax.rsqrt(jnp.mean(x * x, axis=-1, keepdims=True) + RMS_EPS)
    return ((x * r) * g).astype(BF16)


def _qkv_prep(z, q_gain, kv_gain, wq_t, wkv, tc, ta, tb):
    s = z.shape[0]
    bm = _row_block(s, 512)

    def body(zq_ref, zkv_ref, zkr_ref, qg_ref, kvg_ref, wq_ref, wkv_ref, tc_ref, ta_ref, tb_ref,
             qn_ref, kvn_ref, q_ref, k_ref, v_ref):
        c, a, b = tc_ref[...], ta_ref[...], tb_ref[...]
        qn = _norm_bf16(zq_ref[...], qg_ref[...])
        kvn = _norm_bf16(zkv_ref[...], kvg_ref[...])
        qn_ref[...] = qn
        kvn_ref[...] = kvn
        kpe = _rope(zkr_ref[...], c, a, b).astype(BF16)
        for h in range(MLA_HEADS):
            lo = h * HEAD_QK
            qp = _dot(qn, wq_ref[h], NT)
            q_ref[:, lo:lo + 128] = qp[:, :128].astype(BF16)
            q_ref[:, lo + 128:lo + 256] = _rope(qp[:, 128:], c, a, b).astype(BF16)
            kv = _dot(kvn, wkv_ref[h], NN)
            k_ref[:, lo:lo + 128] = kv[:, :128].astype(BF16)
            k_ref[:, lo + 128:lo + 256] = kpe
            v_ref[:, h * HEAD_V:(h + 1) * HEAD_V] = kv[:, 128:].astype(BF16)

    def cols(width, blk):
        return pl.BlockSpec((bm, width), lambda i: (i, blk))

    def whole(a):
        return pl.BlockSpec(a.shape, lambda i: (0,) * a.ndim)

    tab = cols(128, 0)
    return _call(
        body, name="qkv_prep", grid=(s // bm,),
        in_specs=[cols(Q_LORA, 0), cols(KV_LORA, 2), cols(128, 3), whole(q_gain), whole(kv_gain), whole(wq_t),
                  whole(wkv), tab, tab, tab],
        out_specs=[cols(Q_LORA, 0), cols(KV_LORA, 0), cols(1024, 0), cols(1024, 0), cols(512, 0)],
        out_shape=[jax.ShapeDtypeStruct((s, Q_LORA), BF16), jax.ShapeDtypeStruct((s, KV_LORA), BF16),
                   jax.ShapeDtypeStruct((s, 1024), BF16), jax.ShapeDtypeStruct((s, 1024), BF16),
                   jax.ShapeDtypeStruct((s, 512), BF16)],
        sem=("parallel",), args=[z, z, z, q_gain, kv_gain, wq_t, wkv, tc, ta, tb])


def _qkv_prep_bwd(dq, dk, dv, z, qn, kvn, q_gain, kv_gain, wq_t, wkv, tc, ta, tb):
    s = z.shape[0]
    bm = _row_block(s, 512)
    nsteps = s // bm

    def body(dq_ref, dk_ref, dv_ref, zq_ref, zkv_ref, qn_ref, kvn_ref, qg_ref, kvg_ref, wq_ref, wkv_ref,
             tc_ref, ta_ref, tb_ref, dz_ref, dqg_ref, dkvg_ref, dwq_ref, dwkv_ref, wq_acc, wkv_acc):
        i = pl.program_id(0)
        c, a, b = tc_ref[...], ta_ref[...], tb_ref[...]

        @pl.when(i == 0)
        def _():
            wq_acc[...] = jnp.zeros_like(wq_acc)
            wkv_acc[...] = jnp.zeros_like(wkv_acc)

        qn, kvn = qn_ref[...], kvn_ref[...]
        dqn = jnp.zeros((bm, Q_LORA), F32)
        dkvn = jnp.zeros((bm, KV_LORA), F32)
        dpe = jnp.zeros((bm, 128), F32)
        for h in range(MLA_HEADS):
            lo = h * HEAD_QK
            dqp = jnp.concatenate([dq_ref[:, lo:lo + 128],
                                   _rope_t(dq_ref[:, lo + 128:lo + 256].astype(F32), c, a, b).astype(BF16)], axis=1)
            dqn = dqn + _dot(dqp, wq_ref[h], NN)
            wq_acc[h] += _dot(dqp, qn, TN)
            dkv = jnp.concatenate([dk_ref[:, lo:lo + 128], dv_ref[:, h * HEAD_V:(h + 1) * HEAD_V]], axis=1)
            dkvn = dkvn + _dot(dkv, wkv_ref[h], NT)
            wkv_acc[h] += _dot(kvn, dkv, TN)
            dpe = dpe + dk_ref[:, lo + 128:lo + 256].astype(F32)
        dcq, dqg = _rms_bwd_math(dqn, zq_ref[...], qg_ref[...], Q_LORA)
        dckv, dkvg = _rms_bwd_math(dkvn, zkv_ref[...], kvg_ref[...], KV_LORA)
        dz_ref[:, 0:Q_LORA] = dcq.astype(BF16)
        dz_ref[:, Q_LORA:Q_LORA + KV_LORA] = dckv.astype(BF16)
        dz_ref[:, Q_LORA + KV_LORA:512] = _rope_t(dpe, c, a, b).astype(BF16)

        @pl.when(i == 0)
        def _():
            dqg_ref[...] = dqg
            dkvg_ref[...] = dkvg

        @pl.when(i > 0)
        def _():
            dqg_ref[...] += dqg
            dkvg_ref[...] += dkvg

        @pl.when(i == nsteps - 1)
        def _():
            dwq_ref[...] = wq_acc[...].astype(BF16)
            dwkv_ref[...] = wkv_acc[...].astype(BF16)

    def cols(width, blk):
        return pl.BlockSpec((bm, width), lambda i: (i, blk))

    def whole(shape):
        return pl.BlockSpec(shape, lambda i: (0,) * len(shape))

    tab = cols(128, 0)
    return _call(
        body, name="qkv_prep_bwd", grid=(nsteps,),
        in_specs=[cols(1024, 0), cols(1024, 0), cols(512, 0), cols(Q_LORA, 0), cols(KV_LORA, 2), cols(Q_LORA, 0),
                  cols(KV_LORA, 0), whole(q_gain.shape), whole(kv_gain.shape), whole(wq_t.shape), whole(wkv.shape),
                  tab, tab, tab],
        out_specs=[cols(512, 0), whole(q_gain.shape), whole(kv_gain.shape), whole(wq_t.shape), whole(wkv.shape)],
        out_shape=[jax.ShapeDtypeStruct((s, 512), BF16), jax.ShapeDtypeStruct(q_gain.shape, F32),
                   jax.ShapeDtypeStruct(kv_gain.shape, F32), jax.ShapeDtypeStruct(wq_t.shape, BF16),
                   jax.ShapeDtypeStruct(wkv.shape, BF16)],
        scratch_shapes=[pltpu.VMEM(wq_t.shape, F32), pltpu.VMEM(wkv.shape, F32)],
        sem=("arbitrary",), args=[dq, dk, dv, z, z, qn, kvn, q_gain, kv_gain, wq_t, wkv, tc, ta, tb])


def _causal_mask(s, row0, col0):
    rows = row0 + lax.broadcasted_iota(jnp.int32, s.shape, 0)
    cols = col0 + lax.broadcasted_iota(jnp.int32, s.shape, 1)
    return jnp.where(cols <= rows, s, -jnp.inf)


def _attn_fwd(name, q, k, k_off, v, v_off, nh, dq, dv, scale, causal, blk):
    sq, sk = q.shape[0], k.shape[0]
    bq = min(blk, sq)
    bk = min(blk, sk)
    nkv = sk // bk
    assert not causal or (sq == sk and bq == bk)

    hq = bq
    log2e = 1.4426950408889634
    c2 = scale * log2e

    def body(q_ref, k_ref, v_ref, o_ref, lse_ref):
        qi = pl.program_id(1)
        qs = (q_ref[...],)

        def step(j, carry, masked):
            rows = pl.ds(pl.multiple_of(j * bk, bk), bk)
            kb, vb = k_ref[rows, :], v_ref[rows, :]
            out = []
            for t, (m, l, acc) in enumerate(carry):
                s = _dot(qs[t], kb, NT) * c2
                if masked:
                    s = _causal_mask(s, qi * bq + t * hq, j * bk)
                m_new = jnp.maximum(m, jnp.max(s, axis=-1, keepdims=True))
                alpha = jnp.exp2(m - m_new)
                p = jnp.exp2(s - m_new)
                l = alpha * l + jnp.sum(p, axis=-1, keepdims=True)
                acc = alpha * acc + _dot(p, vb, NN)
                out.append((m_new, l, acc))
            return tuple(out)

        one = (jnp.full((hq, 1), -jnp.inf, F32), jnp.zeros((hq, 1), F32), jnp.zeros((hq, dv), F32))
        init = (one,)
        if causal:
            carry = lax.fori_loop(0, qi, lambda j, c: step(j, c, False), init)
            fin = step(qi, carry, True)
        else:
            fin = lax.fori_loop(0, nkv, lambda j, c: step(j, c, False), init)
        for t, (m, l, acc) in enumerate(fin):
            o_ref[t * hq:(t + 1) * hq, :] = (acc / l).astype(o_ref.dtype)
            lse_ref[t * hq:(t + 1) * hq, :] = m * (1.0 / log2e) + jnp.log(l)

    return _call(
        body, name=name, grid=(nh, sq // bq),
        in_specs=[pl.BlockSpec((bq, dq), lambda h, i: (i, h)),
                  pl.BlockSpec((sk, dq), lambda h, i: (0, k_off + h)),
                  pl.BlockSpec((sk, dv), lambda h, i: (0, v_off + h))],
        out_specs=[pl.BlockSpec((bq, dv), lambda h, i: (i, h)), pl.BlockSpec((None, bq, 1), lambda h, i: (h, i, 0))],
        out_shape=[jax.ShapeDtypeStruct((sq, nh * dv), BF16), jax.ShapeDtypeStruct((nh, sq, 1), F32)],
        sem=("parallel", "parallel"), args=[q, k, v])


def _attn_bwd(name, q, k, k_off, v, v_off, do, do_off, lse, delta, nh, dq, dv, scale, causal, blk):
    sq, sk = q.shape[0], k.shape[0]
    bq = min(blk, sq)
    bk = min(blk, sk)
    nq = sq // bq
    assert not causal or (sq == sk and bq == bk)

    nkv = sk // bk

    def body(q_ref, k_ref, v_ref, do_ref, lse_ref, dl_ref, dq_ref, dk_ref, dv_ref, dq_acc, dk_acc, dv_acc):
        j = pl.program_id(1)

        @pl.when(j == 0)
        def _():
            dq_acc[...] = jnp.zeros_like(dq_acc)

        dk_acc[...] = jnp.zeros_like(dk_acc)
        dv_acc[...] = jnp.zeros_like(dv_acc)
        kv = k_ref[...]
        vv = v_ref[...]

        def step(i, masked):
            rows = pl.ds(pl.multiple_of(i * bq, bq), bq)
            qv = q_ref[rows, :]
            dov = do_ref[rows, :].astype(BF16)
            s = _dot(qv, kv, NT) * scale
            if masked:
                s = _causal_mask(s, i * bq, j * bk)
            p = jnp.exp(s - lse_ref[rows, :])
            dp = _dot(dov, vv, NT)
            ds = (p * (dp - dl_ref[rows, :]) * scale).astype(BF16)
            dv_acc[...] += _dot(p, dov, TN)
            dk_acc[...] += _dot(ds, qv, TN)
            dq_acc[rows, :] += _dot(ds, kv, NN)

        if causal:
            step(j, True)

            def loop(i, c):
                step(i, False)
                return c

            lax.fori_loop(j + 1, nq, loop, 0)
        else:
            def loop(i, c):
                step(i, False)
                return c

            lax.fori_loop(0, nq, loop, 0)
        dk_ref[...] = dk_acc[...].astype(dk_ref.dtype)
        dv_ref[...] = dv_acc[...].astype(dv_ref.dtype)

        @pl.when(j == nkv - 1)
        def _():
            dq_ref[...] = dq_acc[...].astype(dq_ref.dtype)

    stat = pl.BlockSpec((None, sq, 1), lambda h, j: (h, 0, 0))
    return _call(
        body, name=name, grid=(nh, sk // bk),
        in_specs=[pl.BlockSpec((sq, dq), lambda h, j: (0, h)),
                  pl.BlockSpec((bk, dq), lambda h, j: (j, k_off + h)),
                  pl.BlockSpec((bk, dv), lambda h, j: (j, v_off + h)),
                  pl.BlockSpec((sq, dv), lambda h, j: (0, do_off + h)), stat, stat],
        out_specs=[pl.BlockSpec((sq, dq), lambda h, j: (0, h)),
                   pl.BlockSpec((bk, dq), lambda h, j: (j, h)),
                   pl.BlockSpec((bk, dv), lambda h, j: (j, h))],
        out_shape=[jax.ShapeDtypeStruct((sq, nh * dq), BF16), jax.ShapeDtypeStruct((sk, nh * dq), BF16),
                   jax.ShapeDtypeStruct((sk, nh * dv), BF16)],
        scratch_shapes=[pltpu.VMEM((sq, dq), F32), pltpu.VMEM((bk, dq), F32), pltpu.VMEM((bk, dv), F32)],
        sem=("parallel", "arbitrary"), args=[q, k, v, do, lse, delta])


def _pool_diff(z, g):
    s = z.shape[0]
    t = lax.broadcasted_iota(jnp.int32, z.shape, 0)
    acc = z
    sums = []
    for k in (1, 2, 4, 8):
        acc = acc + jnp.where(t >= k, pltpu.roll(acc, k, 0), 0.0)
        sums.append(acc)
    win = jnp.where(g == 0, sums[0], jnp.where(g == 1, sums[1], jnp.where(g == 2, sums[2], sums[3])))
    w = lax.shift_left(jnp.int32(2), g)
    count = jnp.minimum(t + 1, w).astype(F32)
    del s
    return win / count - z, count


def _pool_fwd(z, pool_w, pool_scale):
    s = z.shape[0]

    def body(z_ref, w_ref, sc_ref, o_ref):
        diff, _ = _pool_diff(z_ref[...], pl.program_id(0))
        o_ref[...] = (_dot(diff, w_ref[...], NN) * sc_ref[...]).astype(o_ref.dtype)

    return _call(
        body, name="pool_fwd", grid=(POOL_GROUPS,),
        in_specs=[pl.BlockSpec((s, POOL_CH), lambda g: (0, 4 + g)),
                  pl.BlockSpec((None, POOL_CH, POOL_CH), lambda g: (g, 0, 0)),
                  pl.BlockSpec((1, POOL_CH), lambda g: (0, g))],
        out_specs=[pl.BlockSpec((s, POOL_CH), lambda g: (0, g))],
        out_shape=[jax.ShapeDtypeStruct((s, POOL_GROUPS * POOL_CH), BF16)],
        sem=("parallel",), args=[z, pool_w, pool_scale])[0]


def _pool_bwd(dcat, z, pool_w, pool_scale):
    s = z.shape[0]

    def body(dp_ref, z_ref, w_ref, sc_ref, dz_ref, dw_ref, dsc_ref):
        g = pl.program_id(0)
        diff, count = _pool_diff(z_ref[...], g)
        dpf = dp_ref[...].astype(F32)
        u = _dot(diff, w_ref[...], NN)
        dsc_ref[...] = jnp.sum(dpf * u, axis=0, keepdims=True)
        du = (dpf * sc_ref[...]).astype(BF16)
        dw_ref[...] = _dot(diff, du, TN)
        ddiff = _dot(du, w_ref[...], NT)
        t = lax.broadcasted_iota(jnp.int32, ddiff.shape, 0)
        acc = ddiff / count
        sums = []
        for k in (1, 2, 4, 8):
            acc = acc + jnp.where(t < s - k, pltpu.roll(acc, s - k, 0), 0.0)
            sums.append(acc)
        win = jnp.where(g == 0, sums[0], jnp.where(g == 1, sums[1], jnp.where(g == 2, sums[2], sums[3])))
        dz_ref[...] = (win - ddiff).astype(dz_ref.dtype)

    return pl.pallas_call(
        body, name="pool_bwd", grid=(POOL_GROUPS,),
        in_specs=[pl.BlockSpec((s, POOL_CH), lambda g: (0, 4 + g)),
                  pl.BlockSpec((s, POOL_CH), lambda g: (0, 4 + g)),
                  pl.BlockSpec((None, POOL_CH, POOL_CH), lambda g: (g, 0, 0)),
                  pl.BlockSpec((1, POOL_CH), lambda g: (0, g))],
        out_specs=[pl.BlockSpec((s, POOL_CH), lambda g: (0, g)),
                   pl.BlockSpec((None, POOL_CH, POOL_CH), lambda g: (g, 0, 0)),
                   pl.BlockSpec((1, POOL_CH), lambda g: (0, g))],
        out_shape=[jax.ShapeDtypeStruct((s, POOL_GROUPS * POOL_CH), BF16),
                   jax.ShapeDtypeStruct((POOL_GROUPS, POOL_CH, POOL_CH), F32),
                   jax.ShapeDtypeStruct((1, POOL_GROUPS * POOL_CH), F32)],
        compiler_params=_params("parallel"),
    )(dcat, z, pool_w, pool_scale)


def _local_step(x, mem, positions, target, w, grads):
    tc, ta, tb = _rope_tables(positions)
    blk = _ATT_BLOCK

    if "ffn1_shards" in w:
        n1, a1, dadu1, dadg1, w["ffn1_w_gate"], w["ffn1_w_up"], w["ffn1_w_down"] = _ffn1_up_gather(
            x, w["ffn1_norm"], *w["ffn1_shards"])
    else:
        n1 = _rmsnorm_fwd("ffn1_norm", x, w["ffn1_norm"], D_MODEL)
        a1, dadu1, dadg1 = _ffn_up("ffn1_up", n1, w["ffn1_w_gate"], w["ffn1_w_up"])
    h1, n2 = _ffn_down("ffn1_down", a1, w["ffn1_w_down"], x, w["mix_norm"])
    z = _w_in_fwd(n2, w["w_in"])
    qn, kvn, qf, kf, vf = _qkv_prep(z, w["q_norm"], w["kv_norm"], w["w_q_up"], w["w_kv_up"], tc, ta, tb)
    att, lse = _attn_fwd("mla_fwd", qf, kf, 0, vf, 0, MLA_HEADS, HEAD_QK, HEAD_V, MLA_SCALE, True, blk)
    pool = _pool_fwd(z, w["pool_w"], w["pool_scale"])
    s = x.shape[0]
    bm = _row_block(s)
    row = pl.BlockSpec((bm, D_MODEL), lambda i, k: (i, 0))
    half = pl.BlockSpec((bm, 512), lambda i, k: (i, 0))
    h2, n3 = _matmul(
        "w_out", (s // bm, 1),
        [(att, half, w["w_out"], pl.BlockSpec((512, D_MODEL), lambda i, k: (0, 0)), NN),
         (pool, half, w["w_out"], pl.BlockSpec((512, D_MODEL), lambda i, k: (1, 0)), NN)],
        [(h1, row)] + _residual_outs(s, bm, w["xattn_norm"])[0], _residual_outs(s, bm, w["xattn_norm"])[1],
        _residual_epilogue(1.0, True), None)
    memn = _rmsnorm_fwd("mem_norm", mem, w["mem_norm"], D_MODEL)
    qm = _mm_nn("w_mq", n3, w["w_mq"], BF16)
    kvm = _mm_heads_fwd("w_mkv", memn, w["w_mkv"], BF16)
    om, lse_m = _attn_fwd("xattn_fwd", qm, kvm, 0, kvm, MEM_HEADS, MEM_HEADS, MEM_HEAD_DIM, MEM_HEAD_DIM,
                          MEM_SCALE, False, blk)
    h3, n4 = _mm_nn("w_mo", om, w["w_mo"], F32, res=h2, gain=w["ffn2_norm"])
    a2, dadu2, dadg2 = _ffn_up("ffn2_up", n4, w["ffn2_w_gate"], w["ffn2_w_up"])
    dh4, dh4b, loss_vec, d_final = _ffn_down("ffn2_down", a2, w["ffn2_w_down"], h3, loss=(w["final_norm"], target))
    grads["final_norm"] = d_final

    dh3, dh3b, grads["ffn2_norm"] = _ffn_bwd("ffn2", dh4b, n4, dadg2, dadu2, a2, w["ffn2_w_gate"], w["ffn2_w_up"],
                                             w["ffn2_w_down"], grads, norm_bwd=(h3, w["ffn2_norm"], dh4))

    dom, delta_m = _mm_nt("w_mo_dx", dh3b, w["w_mo"], BF16, attn_out=om, nh=MEM_HEADS, dv=MEM_HEAD_DIM)
    grads["w_mo"] = _mm_tn("w_mo_dw", om, dh3b)
    dqm, dkm, dvm = _attn_bwd("xattn_bwd", qm, kvm, 0, kvm, MEM_HEADS, dom, 0, lse_m, delta_m, MEM_HEADS,
                              MEM_HEAD_DIM, MEM_HEAD_DIM, MEM_SCALE, False, blk)
    dkvm = jnp.concatenate([dkm, dvm], axis=1)
    dh2, dh2b, grads["xattn_norm"] = _mm_nt_norm_bwd("w_mq_dx", dqm, w["w_mq"], h2, w["xattn_norm"], dh3)
    grads["w_mq"] = _mm_tn("w_mq_dw", n3, dqm)
    dmemn, grads["w_mkv"] = _mm_heads_bwd("w_mkv", dkvm, memn, w["w_mkv"])
    _, grads["mem_norm"] = _rmsnorm_bwd("mem_norm_bwd", dmemn, mem, w["mem_norm"], D_MODEL, out_dtype=BF16)

    dcat, delta = _mm_nt("w_out_dx", dh2b, w["w_out"], BF16, attn_out=att, nh=MLA_HEADS, dv=HEAD_V)
    grads["w_out"] = jnp.concatenate([_mm_tn("w_out_dw_a", att, dh2b), _mm_tn("w_out_dw_p", pool, dh2b)], axis=0)
    dzp, grads["pool_w"], grads["pool_scale"] = _pool_bwd(dcat, z, w["pool_w"], w["pool_scale"])
    dqf, dkf, dvf = _attn_bwd("mla_bwd", qf, kf, 0, vf, 0, dcat, 0, lse, delta, MLA_HEADS, HEAD_QK, HEAD_V,
                              MLA_SCALE, True, blk)
    dz_lat, grads["q_norm"], grads["kv_norm"], grads["w_q_up"], grads["w_kv_up"] = _qkv_prep_bwd(
        dqf, dkf, dvf, z, qn, kvn, w["q_norm"], w["kv_norm"], w["w_q_up"], w["w_kv_up"], tc, ta, tb)
    dz = jnp.concatenate([dz_lat, dzp], axis=1)
    grads["w_in"] = _w_in_dw(dz, n2)
    dh1, dh1b, grads["mix_norm"] = _w_in_dx_norm_bwd(dz, w["w_in"], h1, w["mix_norm"], dh2)

    dn1 = _ffn_bwd("ffn1", dh1b, n1, dadg1, dadu1, a1, w["ffn1_w_gate"], w["ffn1_w_up"], w["ffn1_w_down"], grads)
    dx, grads["ffn1_norm"] = _rmsnorm_bwd("ffn1_norm_bwd", dn1, x, w["ffn1_norm"], D_MODEL, dres=dh1)
    return loss_vec[0, 0], dx


def _mesh_pos():
    x, y, c = lax.axis_index("x"), lax.axis_index("y"), lax.axis_index("c")
    chips = [(1 - x, y), (x, 1 - y), (1 - x, 1 - y)]
    chip_ids = [2 * cx + cy for cx, cy in chips]
    return x, y, c, 2 * x + y, chips, chip_ids


def _half_rows(c, rows):
    hr = rows // 2
    return pl.ds(pl.multiple_of(c * hr, 16), hr), pl.ds(pl.multiple_of((1 - c) * hr, 16), hr)


def _ag_ici_stage(shards):
    n = len(shards)

    def copies(ins, outs):
        x, y, c, me, chips, _ = _mesh_pos()
        out = []
        for k in range(n):
            mine, _ = _half_rows(c, ins[k].shape[0])
            out.append((ins[k], outs[k].at[me], None))
            for cx, cy in chips:
                out.append((ins[k].at[mine], outs[k].at[me, mine], (cx, cy, c)))
        return out

    return _Stage(shards, [jax.ShapeDtypeStruct((N_CHIPS,) + s.shape, s.dtype) for s in shards], 3 * n, n, copies)


def _quarter_rows(c, rows):
    qr = rows // 4
    return pl.ds(pl.multiple_of(c * 2 * qr, 16), qr), pl.ds(pl.multiple_of(c * 2 * qr + qr, 16), qr)


def _ag_d2d_stage(fulls):
    n = len(fulls)

    def copies(ins, outs):
        x, y, c, me, _, chip_ids = _mesh_pos()
        out = []
        for k in range(n):
            mine, _ = _half_rows(c, ins[k].shape[1])
            for j in range(3):
                out.append((ins[k].at[chip_ids[j], mine], outs[k].at[chip_ids[j], mine], (x, y, 1 - c)))
        return out

    return _Stage(fulls, [jax.ShapeDtypeStruct(f.shape, f.dtype) for f in fulls], 3 * n, 0, copies,
                  aliases={k: k for k in range(n)})


def _rs_swap_stage(grads):
    n = len(grads)

    def copies(ins, outs):
        x, y, c, _, _, _ = _mesh_pos()
        out = []
        for k in range(n):
            _, other = _half_rows(c, ins[k].shape[1])
            out.append((ins[k].at[:, other, :], outs[k], (x, y, 1 - c)))
        return out

    return _Stage(grads, [jax.ShapeDtypeStruct((N_CHIPS, g.shape[1] // 2, g.shape[2]), g.dtype) for g in grads],
                  n, 0, copies)


_REL_OF_PEER = (2, 1, 3)


def _rs_scatter_stage(sums, relative=False):
    n = len(sums)

    def copies(ins, outs):
        x, y, c, me, chips, chip_ids = _mesh_pos()
        out = []
        for k in range(n):
            mine, _ = _half_rows(c, 2 * ins[k].shape[1])
            out.append((ins[k].at[0 if relative else me], outs[k].at[0, mine, :], None))
            for j, (cx, cy) in enumerate(chips):
                slab = _REL_OF_PEER[j] if relative else chip_ids[j]
                out.append((ins[k].at[slab], outs[k].at[1 + j, mine, :], (cx, cy, c)))
        return out

    return _Stage(sums, [jax.ShapeDtypeStruct((N_CHIPS, 2 * s.shape[1], s.shape[2]), s.dtype) for s in sums],
                  3 * n, n, copies)


def _rs_mirror_stage(parts):
    n = len(parts)

    def copies(ins, outs):
        x, y, c, _, _, _ = _mesh_pos()
        out = []
        for k in range(n):
            mine, _ = _half_rows(c, ins[k].shape[1])
            out.append((ins[k].at[:, mine, :], outs[k].at[:, mine, :], (x, y, 1 - c)))
        return out

    return _Stage(parts, [jax.ShapeDtypeStruct(p.shape, p.dtype) for p in parts], n, 0, copies,
                  aliases={k: k for k in range(n)})


def _pair_add(name, gs, r1s, core):
    n = len(gs)

    def body(c_ref, *refs):
        for k in range(n):
            g_ref, r_ref, o_ref = refs[k], refs[n + k], refs[2 * n + k]
            o_ref[...] = (g_ref[...].astype(F32) + r_ref[...].astype(F32)).astype(BF16)

    def half(g):
        return pl.BlockSpec((None, g.shape[1] // 2, g.shape[2]), lambda j, c: (j, 0, 0))

    def mine(g):
        return pl.BlockSpec((None, g.shape[1] // 2, g.shape[2]), lambda j, c: (j, c[0], 0))

    return pl.pallas_call(
        body, name=name,
        grid_spec=pltpu.PrefetchScalarGridSpec(
            num_scalar_prefetch=1, grid=(N_CHIPS,),
            in_specs=[mine(g) for g in gs] + [half(g) for g in gs], out_specs=[half(g) for g in gs]),
        out_shape=[jax.ShapeDtypeStruct((N_CHIPS, g.shape[1] // 2, g.shape[2]), BF16) for g in gs],
        compiler_params=_params("parallel"),
    )(core, *gs, *r1s)


def _adamw_math(w, g, m, v):
    m = ADAM_B1 * m + (1.0 - ADAM_B1) * g
    v = ADAM_B2 * v + (1.0 - ADAM_B2) * (g * g)
    m_hat = m / (1.0 - ADAM_B1 ** ADAM_STEP)
    v_hat = v / (1.0 - ADAM_B2 ** ADAM_STEP)
    delta = -ADAM_LR * (m_hat / (jnp.sqrt(v_hat) + ADAM_EPS) + ADAM_WD * w)
    return delta, m, v


def _adamw_sum(name, ws, parts, ms, vs):
    n = len(ws)
    r, c = ws[0].shape
    assert all(w.shape == (r, c) for w in ws)
    br = r
    while br * c * 4 > (1 << 20) and br % 32 == 0:
        br //= 2

    def body(*refs):
        for k in range(n):
            w_ref, p_ref, m_ref, v_ref = refs[4 * k:4 * k + 4]
            g_ref, d_ref, nm_ref, nv_ref = refs[4 * n + 4 * k:4 * n + 4 * k + 4]
            g = p_ref[0].astype(F32)
            for j in range(1, N_CHIPS):
                g = g + p_ref[j].astype(F32)
            d, nm, nv = _adamw_math(w_ref[...], g, m_ref[...], v_ref[...])
            g_ref[...] = g
            d_ref[...] = d
            nm_ref[...] = nm
            nv_ref[...] = nv

    spec = pl.BlockSpec((br, c), lambda i: (i, 0))
    shp = jax.ShapeDtypeStruct((r, c), F32)
    args = [a for k in range(n) for a in (ws[k], parts[k], ms[k], vs[k])]
    res = _call(
        body, name=name, grid=(r // br,),
        in_specs=[spec, pl.BlockSpec((N_CHIPS, br, c), lambda i: (0, i, 0)), spec, spec] * n,
        out_specs=[spec] * (4 * n), out_shape=[shp] * (4 * n), sem=("parallel",), args=args)
    return [res[4 * k:4 * k + 4] for k in range(n)]


_SMALL_VECTORS = ("ffn1_norm", "mix_norm", "xattn_norm", "mem_norm", "ffn2_norm", "final_norm", "q_norm",
                  "kv_norm", "pool_scale")


_LOSS_ROW = 9
_VEC_ROWS = 16
_POOL_ROWS = POOL_GROUPS * POOL_CH


def _small_params_step(g, w, m, v, loss_local):
    names = list(_SMALL_VECTORS) + ["pool_w"]
    nv = len(_SMALL_VECTORS)
    widths = [g[n].shape[1] for n in _SMALL_VECTORS]
    shapes = {"vec": (_VEC_ROWS, D_MODEL), "pool": (_POOL_ROWS, POOL_CH)}

    def body(*refs):
        ins = refs[:4 * (nv + 1) + 1]
        outs = refs[len(ins):len(ins) + 4 * (nv + 1) + 1]
        vec_own, vec_sib, vec_all, pool_sib, pool_sum, pool_all, send, recv = refs[len(ins) + len(outs):]
        g_in, w_in, m_in, v_in = (ins[k * (nv + 1):(k + 1) * (nv + 1)] for k in range(4))
        loss_in = ins[-1]
        g_out, d_out, m_out, v_out = (outs[k * (nv + 1):(k + 1) * (nv + 1)] for k in range(4))
        loss_out = outs[-1]
        x, y, c, me, chips, chip_ids = _mesh_pos()
        sib = (x, y, 1 - c)

        def remote(src, dst, k, dev):
            return pltpu.make_async_remote_copy(src_ref=src, dst_ref=dst, send_sem=send.at[k], recv_sem=recv.at[k],
                                                device_id=dev, device_id_type=_MESH)

        vec_own[...] = jnp.zeros_like(vec_own)
        for i in range(nv):
            vec_own[i:i + 1, 0:widths[i]] = g_in[i][...]
        vec_own[_LOSS_ROW:_LOSS_ROW + 1, 0:128] = loss_in[...]
        swaps = [remote(vec_own, vec_sib, 0, sib), remote(g_in[nv], pool_sib, 1, sib)]
        for cp in swaps:
            cp.start()
        for cp in swaps:
            cp.wait()
        vec_all[me] = vec_own[...] + vec_sib[...]
        pool_sum[...] = g_in[nv][...] + pool_sib[...]
        pool_all[me] = pool_sum[...]
        hv, hp = _VEC_ROWS // 2, _POOL_ROWS // 2
        mine_v = pl.ds(pl.multiple_of(c * hv, 8), hv)
        mine_p = pl.ds(pl.multiple_of(c * hp, 8), hp)
        sends = []
        for j, (cx, cy) in enumerate(chips):
            sends.append(remote(vec_all.at[me, mine_v], vec_all.at[me, mine_v], 2 + j, (cx, cy, c)))
            sends.append(remote(pool_sum.at[mine_p], pool_all.at[me, mine_p], 5 + j, (cx, cy, c)))
        for cp in sends:
            cp.start()
        for cp in sends:
            cp.wait()
        mirrors = []
        for j in range(3):
            mirrors.append(remote(vec_all.at[chip_ids[j], mine_v], vec_all.at[chip_ids[j], mine_v], 8 + j, sib))
            mirrors.append(remote(pool_all.at[chip_ids[j], mine_p], pool_all.at[chip_ids[j], mine_p], 11 + j, sib))
        for cp in mirrors:
            cp.start()
        for cp in mirrors:
            cp.wait()
        vec_tot = vec_all[0]
        pool_tot = pool_all[0]
        for i in range(1, N_CHIPS):
            vec_tot = vec_tot + vec_all[i]
            pool_tot = pool_tot + pool_all[i]
        vec_sib[...] = vec_tot
        loss_out[...] = vec_sib[_LOSS_ROW:_LOSS_ROW + 1, 0:128]
        for i in range(nv + 1):
            gi = pool_tot if i == nv else vec_sib[i:i + 1, 0:widths[i]]
            d, nm, nvv = _adamw_math(w_in[i][...], gi, m_in[i][...], v_in[i][...])
            g_out[i][...] = gi
            d_out[i][...] = d
            m_out[i][...] = nm
            v_out[i][...] = nvv

    vm = pl.BlockSpec(memory_space=pltpu.VMEM)
    args = [d[n] for d in (g, w, m, v) for n in names] + [jnp.broadcast_to(loss_local.reshape(1, 1), (1, 128))]
    out_shape = [jax.ShapeDtypeStruct(g[n].shape, F32) for _ in range(4) for n in names]
    out_shape.append(jax.ShapeDtypeStruct((1, 128), F32))
    res = pl.pallas_call(
        body, name="small_params_step", in_specs=[vm] * len(args), out_specs=[vm] * len(out_shape),
        out_shape=out_shape,
        scratch_shapes=[pltpu.VMEM(shapes["vec"], F32), pltpu.VMEM(shapes["vec"], F32),
                        pltpu.VMEM((N_CHIPS,) + shapes["vec"], F32), pltpu.VMEM(shapes["pool"], F32),
                        pltpu.VMEM(shapes["pool"], F32), pltpu.VMEM((N_CHIPS,) + shapes["pool"], F32),
                        pltpu.SemaphoreType.DMA((14,)), pltpu.SemaphoreType.DMA((14,))],
        compiler_params=pltpu.CompilerParams(vmem_limit_bytes=V7X_VMEM_LIMIT_BYTES),
    )(*args)
    k = len(names)
    dicts = [dict(zip(names, res[i * k:(i + 1) * k])) for i in range(4)]
    return dicts[0], dicts[1], dicts[2], dicts[3], res[-1]


_WEIGHTS = ("ffn1_norm", "ffn1_w_gate", "ffn1_w_up", "ffn1_w_down", "mix_norm", "w_in", "q_norm", "w_q_up",
            "kv_norm", "w_kv_up", "pool_w", "pool_scale", "w_out", "xattn_norm", "mem_norm", "w_mq", "w_mkv",
            "w_mo", "ffn2_norm", "ffn2_w_gate", "ffn2_w_up", "ffn2_w_down", "final_norm")
_SHARDED = ("ffn1_w_gate", "ffn1_w_up", "ffn1_w_down", "w_in", "w_q_up", "w_kv_up", "w_out", "w_mq", "w_mkv",
            "w_mo", "ffn2_w_gate", "ffn2_w_up", "ffn2_w_down")
W_IN_SPLIT = Q_LORA + KV_LORA + ROPE_DIM


_FFN1 = ("ffn1_w_gate", "ffn1_w_up", "ffn1_w_down")
_TRANSPOSED = ("ffn1_w_gate", "ffn1_w_up", "ffn2_w_gate", "ffn2_w_up", "w_in", "w_q_up")


def _local_view(name, a):
    return jnp.swapaxes(a, 1, 2)[0] if name in _TRANSPOSED else a[0]


def _global_view(name, a):
    return jnp.swapaxes(a[None], 1, 2) if name in _TRANSPOSED else a[None]


def _pad_shard(name, a):
    if name == "w_in":
        return jnp.concatenate([a[:W_IN_SPLIT], jnp.zeros((64, a.shape[1]), a.dtype), a[W_IN_SPLIT:]], axis=0)
    if name == "w_q_up":
        return jnp.pad(a, ((0, 64), (0, 0)))
    return a


def _unpad_shard(name, a):
    if name == "w_in":
        return jnp.concatenate([a[:, :W_IN_SPLIT], a[:, W_IN_SPLIT + 64:]], axis=1)
    if name == "w_q_up":
        return a[:, :192]
    return a


def _stacked(g):
    return g if g.ndim == 3 else g.reshape(N_CHIPS, g.shape[0] // N_CHIPS, g.shape[1])


class _Plan:
    AG_UNITS = (
        (("w_in", "w_q_up", "w_kv_up"), "ffn1_up"),
        (("w_out",), "w_in"),
        (("w_mq",), "qkv_prep"),
        (("w_mkv", "w_mo", "ffn2_w_gate"), "mla_fwd"),
        (("ffn2_w_up",), "xattn_fwd"),
        (("ffn2_w_down",), "ffn2_up"),
    )
    RS_UNITS = (
        (("ffn2_w_gate", "ffn2_w_up", "ffn2_w_down"), "ffn2_dn_a", "mla_bwd", "qkv_prep_bwd"),
        (("w_mo", "w_mq", "w_mkv"), "w_out_dx", "mla_bwd", "qkv_prep_bwd"),
        (("w_out", "w_q_up", "w_kv_up", "w_in"), "w_in_dx", "ffn1_dact", "ffn1_dwd"),
        (("ffn1_w_down",), "ffn1_dwg", "ffn1_dwu", "ffn1_dn_a"),
        (("ffn1_w_gate",), "ffn1_dwu", "ffn1_dn_a", "ffn1_dn_b"),
        (("ffn1_w_up",), "ffn1_dn_a", "ffn1_dn_b", "adamw_w_kv_up"),
    )
    ADAMW_ORDER = (("w_kv_up",), ("ffn2_w_gate", "ffn2_w_up"), ("ffn2_w_down", "ffn1_w_down"), ("w_mo", "w_mq", "w_out"),
                   ("w_mkv",), ("w_q_up",), ("w_in",), ("ffn1_w_gate", "ffn1_w_up"))

    def __init__(self, shards, w, grads, core):
        self.shards, self.w, self.grads, self.core = shards, w, grads, core
        self.last_slab_step = 0
        self.parts = {}
        self.ag = [None for _ in self.AG_UNITS]
        self.rs = [[None, None, None, None] for _ in self.RS_UNITS]

    def pre(self, name):
        for i, (names, host) in enumerate(self.AG_UNITS):
            if name == host:
                st = _ag_ici_stage([self.shards[n] for n in names])
                st.then = _ag_d2d_stage(st.outs)
                st.start_step = self.last_slab_step if name == "ffn1_up" else 0
                self.ag[i] = _host(name, st)
        for i, (names, h1, h2, h3) in enumerate(self.RS_UNITS):
            if name == h1:
                self.rs[i][0] = _host(name, _rs_swap_stage([_stacked(self.grads[n]) for n in names]))
            if name == h2:
                self.rs[i][2] = _host(name, _rs_scatter_stage(self.rs[i][1], relative=names[0] in _FFN1))
            if name == h3:
                self.rs[i][3] = _host(name, _rs_mirror_stage(self.rs[i][2].results))

    def post(self, name):
        for i, (names, host) in enumerate(self.AG_UNITS):
            if name == host:
                for n, f in zip(names, self.ag[i].results):
                    self.w[n] = _full_weight(n, f)
        for i, (names, h1, h2, h3) in enumerate(self.RS_UNITS):
            if name == h1:
                self.rs[i][1] = list(_pair_add("pair_add_" + names[0], [_stacked(self.grads[n]) for n in names],
                                               self.rs[i][0].results, self.core))
            if name == h3:
                for n, p in zip(names, self.rs[i][3].results):
                    self.parts[n] = p


def _full_weight(name, stacked):
    if name in ("w_out", "w_mq", "w_mo"):
        return stacked.reshape(D_MODEL, D_MODEL)
    return stacked


def kernel(x, mem, positions, ffn1_norm, ffn1_w_gate, ffn1_w_up, ffn1_w_down, mix_norm, w_in, q_norm, w_q_up, kv_norm, w_kv_up, pool_w, pool_scale, w_out, xattn_norm, mem_norm, w_mq, w_mkv, w_mo, ffn2_norm, ffn2_w_gate, ffn2_w_up, ffn2_w_down, final_norm, loss_target, m_ffn1_norm, m_ffn1_w_gate, m_ffn1_w_up, m_ffn1_w_down, m_mix_norm, m_w_in, m_q_norm, m_w_q_up, m_kv_norm, m_w_kv_up, m_pool_w, m_pool_scale, m_w_out, m_xattn_norm, m_mem_norm, m_w_mq, m_w_mkv, m_w_mo, m_ffn2_norm, m_ffn2_w_gate, m_ffn2_w_up, m_ffn2_w_down, m_final_norm, v_ffn1_norm, v_ffn1_w_gate, v_ffn1_w_up, v_ffn1_w_down, v_mix_norm, v_w_in, v_q_norm, v_w_q_up, v_kv_norm, v_w_kv_up, v_pool_w, v_pool_scale, v_w_out, v_xattn_norm, v_mem_norm, v_w_mq, v_w_mkv, v_w_mo, v_ffn2_norm, v_ffn2_w_gate, v_ffn2_w_up, v_ffn2_w_down, v_final_norm):
    wts = dict(zip(_WEIGHTS, (ffn1_norm, ffn1_w_gate, ffn1_w_up, ffn1_w_down, mix_norm, w_in, q_norm, w_q_up, kv_norm, w_kv_up, pool_w, pool_scale, w_out, xattn_norm, mem_norm, w_mq, w_mkv, w_mo, ffn2_norm, ffn2_w_gate, ffn2_w_up, ffn2_w_down, final_norm)))
    mom = dict(zip(_WEIGHTS, (m_ffn1_norm, m_ffn1_w_gate, m_ffn1_w_up, m_ffn1_w_down, m_mix_norm, m_w_in, m_q_norm, m_w_q_up, m_kv_norm, m_w_kv_up, m_pool_w, m_pool_scale, m_w_out, m_xattn_norm, m_mem_norm, m_w_mq, m_w_mkv, m_w_mo, m_ffn2_norm, m_ffn2_w_gate, m_ffn2_w_up, m_ffn2_w_down, m_final_norm)))
    var = dict(zip(_WEIGHTS, (v_ffn1_norm, v_ffn1_w_gate, v_ffn1_w_up, v_ffn1_w_down, v_mix_norm, v_w_in, v_q_norm, v_w_q_up, v_kv_norm, v_w_kv_up, v_pool_w, v_pool_scale, v_w_out, v_xattn_norm, v_mem_norm, v_w_mq, v_w_mkv, v_w_mo, v_ffn2_norm, v_ffn2_w_gate, v_ffn2_w_up, v_ffn2_w_down, v_final_norm)))
    small = [n for n in _WEIGHTS if n not in _SHARDED]

    global _PLAN
    shards = {n: _pad_shard(n, _local_view(n, wts[n])).astype(BF16) for n in _SHARDED}
    w = {n: wts[n].reshape(1, -1) for n in _SMALL_VECTORS}
    w["pool_w"] = pool_w[0].astype(BF16)
    grads = {}
    core = lax.axis_index("c").astype(jnp.int32).reshape(1)
    plan = _Plan(shards, w, grads, core)
    _PLAN = plan
    try:
        w["ffn1_shards"] = tuple(shards[n] for n in _FFN1)

        loss_local, dx = _local_step(x[0], mem[0], positions[0], loss_target[0], w, grads)

        def small_view(d):
            out = {n: d[n].reshape(1, -1) for n in _SMALL_VECTORS}
            out["pool_w"] = d["pool_w"].reshape(_POOL_ROWS, POOL_CH)
            return out

        *small_res, loss_vec = _small_params_step(small_view(grads), small_view(wts), small_view(mom),
                                                  small_view(var), loss_local)
        g_out, d_out, m_out, v_out = ({n: r[n].reshape(wts[n].shape) for n in small} for r in small_res)
        loss = loss_vec[0, 0]

        for names in _Plan.ADAMW_ORDER:
            res = _adamw_sum("adamw_" + names[0], [_local_view(n, wts[n]) for n in names],
                             [_unpad_shard(n, plan.parts[n]) for n in names],
                             [_local_view(n, mom[n]) for n in names], [_local_view(n, var[n]) for n in names])
            for n, r4 in zip(names, res):
                g_out[n], d_out[n], m_out[n], v_out[n] = (_global_view(n, r) for r in r4)
    finally:
        _PLAN = None
        _PENDING.clear()

    return (loss, dx[None], *[g_out[n] for n in _WEIGHTS], *[d_out[n] for n in _WEIGHTS],
            *[m_out[n] for n in _WEIGHTS], *[v_out[n] for n in _WEIGHTS])
```

```python
import jax
import jax.numpy as jnp
from jax import lax
from jax.experimental import pallas as pl
from jax.experimental.pallas import tpu as pltpu

F32 = jnp.float32
BF16 = jnp.bfloat16

D_MODEL = 1024
D_FF = 2816
N_CHIPS = 4
FF_SHARD = D_FF // N_CHIPS
MLA_HEADS = 4
Q_LORA = 256
KV_LORA = 128
ROPE_DIM = 64
HEAD_QK = 256
HEAD_V = 128
POOL_GROUPS = 4
POOL_CH = 128
MEM_HEADS = 4
MEM_HEAD_DIM = 256
RMS_EPS = 1e-6
ROPE_BASE = 10000.0
MLA_SCALE = (128 + 64) ** -0.5
MEM_SCALE = MEM_HEAD_DIM ** -0.5

ADAM_LR = 0.001
ADAM_B1 = 0.9
ADAM_B2 = 0.999
ADAM_EPS = 1e-08
ADAM_WD = 0.01
ADAM_STEP = 10

V7X_VMEM_LIMIT_BYTES = 56 * 1024 * 1024

NN = ((1,), (0,))
NT = ((1,), (1,))
TN = ((0,), (0,))


def _params(*sem):
    return pltpu.CompilerParams(dimension_semantics=sem, vmem_limit_bytes=V7X_VMEM_LIMIT_BYTES)


_MESH = pl.DeviceIdType.MESH
_ANY = pl.BlockSpec(memory_space=pl.ANY)


class _Stage:
    def __init__(self, ins, outs, n_remote, n_local, copies, aliases=None):
        self.ins, self.outs, self.n_remote, self.n_local = list(ins), list(outs), n_remote, n_local
        self.copies, self.aliases = copies, dict(aliases or {})
        self.results = None
        self.start_step = 0
        self.then = None

    def descriptors(self, in_refs, out_refs, send, recv, loc):
        ds, ri, li = [], 0, 0
        for src, dst, dev in self.copies(in_refs, out_refs):
            if dev is None:
                ds.append(pltpu.make_async_copy(src, dst, loc.at[li]))
                li += 1
            else:
                ds.append(pltpu.make_async_remote_copy(src_ref=src, dst_ref=dst, send_sem=send.at[ri],
                                                       recv_sem=recv.at[ri], device_id=dev, device_id_type=_MESH))
                ri += 1
        assert ri == self.n_remote and li == self.n_local
        return ds


_PENDING = {}


def _host(name, stage):
    _PENDING.setdefault(name, []).append(stage)
    return stage


_PLAN = None


def _call(body, **kw):
    if _PLAN is not None:
        _PLAN.pre(kw["name"])
    res = _call_hosting(body, **kw)
    if _PLAN is not None:
        _PLAN.post(kw["name"])
    return res


def _call_hosting(body, *, name, grid, in_specs, out_specs, out_shape, sem, args, scratch_shapes=(), aliases=None):
    stages = _PENDING.pop(name, [])
    scratch_shapes = list(scratch_shapes)
    if not stages:
        return pl.pallas_call(body, name=name, grid=grid, in_specs=in_specs, out_specs=out_specs,
                              out_shape=out_shape, scratch_shapes=scratch_shapes,
                              input_output_aliases=dict(aliases or {}), compiler_params=_params(*sem))(*args)
    ni, no, ns = len(in_specs), len(out_shape), len(scratch_shapes)
    c_ins = [a for st in stages for a in st.ins]
    c_outs = [o for st in stages for o in st.outs]
    nci, nco = len(c_ins), len(c_outs)
    aliases, io, oo = dict(aliases or {}), 0, 0
    for st in stages:
        for i, j in st.aliases.items():
            aliases[ni + io + i] = no + oo + j
        io += len(st.ins)
        oo += len(st.outs)
    dma = pltpu.SemaphoreType.DMA
    sems = []
    for st in stages:
        sems += [dma((max(st.n_remote, 1),)), dma((max(st.n_remote, 1),)), dma((max(st.n_local, 1),))]
    followers = [st.then for st in stages if st.then is not None]
    for st in followers:
        sems += [dma((max(st.n_remote, 1),)), dma((max(st.n_remote, 1),)), dma((max(st.n_local, 1),))]

    def wrapped(*refs):
        ins, cin = refs[:ni], refs[ni:ni + nci]
        outs, cout = refs[ni + nci:ni + nci + no], refs[ni + nci + no:ni + nci + no + nco]
        scr = refs[ni + nci + no + nco:ni + nci + no + nco + ns]
        sem_refs = refs[ni + nci + no + nco + ns:]
        step = pl.program_id(0)
        last = pl.program_id(0) == grid[0] - 1
        for ax in range(1, len(grid)):
            step = step * grid[ax] + pl.program_id(ax)
            last = jnp.logical_and(last, pl.program_id(ax) == grid[ax] - 1)

        def descriptors(si):
            io = sum(len(st.ins) for st in stages[:si])
            oo = sum(len(st.outs) for st in stages[:si])
            st = stages[si]
            return st.descriptors(cin[io:io + len(st.ins)], cout[oo:oo + len(st.outs)], *sem_refs[3 * si:3 * si + 3])

        def follower_descriptors(fi):
            si = [k for k, st in enumerate(stages) if st.then is not None][fi]
            oo = sum(len(st.outs) for st in stages[:si])
            bufs = cout[oo:oo + len(stages[si].outs)]
            k0 = 3 * (len(stages) + fi)
            return followers[fi].descriptors(bufs, bufs, *sem_refs[k0:k0 + 3])

        def start(si):
            @pl.when(step == stages[si].start_step)
            def _():
                for d in descriptors(si):
                    d.start()

        for si, st in enumerate(stages):
            if st.start_step == 0:
                start(si)
        body(*ins, *outs, *scr)
        for si, st in enumerate(stages):
            if st.start_step != 0:
                start(si)

        @pl.when(last)
        def _():
            for si in range(len(stages)):
                for d in descriptors(si):
                    d.wait()
            for fi in range(len(followers)):
                for d in follower_descriptors(fi):
                    d.start()
            for fi in range(len(followers)):
                for d in follower_descriptors(fi):
                    d.wait()

    res = pl.pallas_call(
        wrapped, name=name, grid=grid, in_specs=list(in_specs) + [_ANY] * nci,
        out_specs=list(out_specs) + [_ANY] * nco, out_shape=list(out_shape) + c_outs,
        scratch_shapes=scratch_shapes + sems, input_output_aliases=aliases,
        compiler_params=_params(*(("arbitrary",) * len(grid))))(*args, *c_ins)
    oo = no
    for st in stages:
        st.results = list(res[oo:oo + len(st.outs)])
        oo += len(st.outs)
    return list(res[:no])


def _dot(a, b, dims):
    return lax.dot_general(a.astype(BF16), b.astype(BF16), (dims, ((), ())), preferred_element_type=F32)


_MAX_ROW_BLOCK = 1024
_ATT_BLOCK = 512


_MAX_REDUCE_BLOCK = 2048


def _row_block(s, want=1024):
    return min(want, s, _MAX_ROW_BLOCK)


def _reduce_block(s):
    return min(s, _MAX_REDUCE_BLOCK)


def _matmul(name, grid, terms, extras, outs, epilogue, acc_shape, fill=(), summed=()):
    nt, ne, no, nf = len(terms), len(extras), len(outs), len(fill)
    nk = grid[-1]
    dims = [t[4] for t in terms]

    def body(*refs):
        a_refs, b_refs = refs[:nt], refs[nt:2 * nt]
        e_refs = refs[2 * nt:2 * nt + ne]
        o_refs = refs[2 * nt + ne + nf:2 * nt + ne + nf + no]

        def finish(acc):
            vals = epilogue(acc, *[e[...] for e in e_refs])
            for idx, (o, val) in enumerate(zip(o_refs, vals)):
                if idx in summed:
                    @pl.when(pl.program_id(0) == 0)
                    def _(o=o, val=val):
                        o[...] = val.astype(o.dtype)

                    @pl.when(pl.program_id(0) > 0)
                    def _(o=o, val=val):
                        o[...] += val.astype(o.dtype)
                else:
                    o[...] = val.astype(o.dtype)

        if nk == 1:
            part = None
            for a, b, d in zip(a_refs, b_refs, dims):
                t = _dot(a[...], b[...], d)
                part = t if part is None else part + t
            finish(part)
        else:
            acc_ref = refs[-1]
            k = pl.program_id(len(grid) - 1)

            @pl.when(k == 0)
            def _():
                acc_ref[...] = jnp.zeros_like(acc_ref)

            for a, b, d in zip(a_refs, b_refs, dims):
                acc_ref[...] += _dot(a[...], b[...], d)

            @pl.when(k == nk - 1)
            def _():
                finish(acc_ref[...])

    in_specs = [t[1] for t in terms] + [t[3] for t in terms] + [e[1] for e in extras] + [_ANY] * nf
    args = [t[0] for t in terms] + [t[2] for t in terms] + [e[0] for e in extras] + list(fill)
    sem = ("arbitrary" if summed else "parallel",) * (len(grid) - 1) + ("arbitrary",)
    aliases = {2 * nt + ne + i: i for i in range(nf)}
    return _call(
        body, name=name, grid=grid, in_specs=in_specs,
        out_specs=[o[1] for o in outs], out_shape=[o[0] for o in outs],
        scratch_shapes=[pltpu.VMEM(acc_shape, F32)] if nk > 1 else [], sem=sem, args=args, aliases=aliases)


def _ident(acc):
    return (acc,)


def _rmsnorm_fwd(name, x, gain, width, col_block=0):
    s = x.shape[0]
    bm = _row_block(s)

    def body(x_ref, g_ref, o_ref):
        xf = x_ref[...]
        r = lax.rsqrt(jnp.mean(xf * xf, axis=-1, keepdims=True) + RMS_EPS)
        o_ref[...] = ((xf * r) * g_ref[...]).astype(o_ref.dtype)

    return pl.pallas_call(
        body, name=name, grid=(s // bm,),
        in_specs=[pl.BlockSpec((bm, width), lambda i: (i, col_block)), pl.BlockSpec((1, width), lambda i: (0, 0))],
        out_specs=pl.BlockSpec((bm, width), lambda i: (i, 0)),
        out_shape=jax.ShapeDtypeStruct((s, width), BF16),
        compiler_params=_params("parallel"),
    )(x, gain)


def _rms_bwd_math(dy, xf, g, width):
    r = lax.rsqrt(jnp.mean(xf * xf, axis=-1, keepdims=True) + RMS_EPS)
    dyg = dy * g
    dot = jnp.sum(dyg * xf, axis=-1, keepdims=True)
    dx = r * dyg - xf * ((r * r * r) * (dot * (1.0 / width)))
    dgain = jnp.sum(dy * (xf * r), axis=0, keepdims=True)
    return dx, dgain


def _rmsnorm_bwd(name, dy, x, gain, width, col_block=0, dres=None, out_dtype=F32):
    s = x.shape[0]
    bm = _row_block(s)
    has_res = dres is not None

    def body(*refs):
        if has_res:
            dy_ref, x_ref, g_ref, r_ref, dx_ref, dg_ref = refs
        else:
            dy_ref, x_ref, g_ref, dx_ref, dg_ref = refs
        dx, dgain = _rms_bwd_math(dy_ref[...].astype(F32), x_ref[...], g_ref[...], width)
        if has_res:
            dx = dx + r_ref[...]
        dx_ref[...] = dx.astype(dx_ref.dtype)

        @pl.when(pl.program_id(0) == 0)
        def _():
            dg_ref[...] = dgain

        @pl.when(pl.program_id(0) > 0)
        def _():
            dg_ref[...] += dgain

    row = pl.BlockSpec((bm, width), lambda i: (i, 0))
    in_specs = [row, pl.BlockSpec((bm, width), lambda i: (i, col_block)), pl.BlockSpec((1, width), lambda i: (0, 0))]
    args = [dy, x, gain]
    out_specs = [row, pl.BlockSpec((1, width), lambda i: (0, 0))]
    out_shape = [jax.ShapeDtypeStruct((s, width), out_dtype), jax.ShapeDtypeStruct((1, width), F32)]
    if has_res:
        in_specs.append(row)
        args.append(dres)
    return _call(body, name=name, grid=(s // bm,), in_specs=in_specs, out_specs=out_specs, out_shape=out_shape,
                 sem=("arbitrary",), args=args)


def _ffn_up(name, n, wg, wu):
    s = n.shape[0]
    bm = _row_block(s)

    def body(n_ref, wg_ref, wu_ref, a_ref, dadu_ref, dadg_ref):
        x = n_ref[...]
        g = _dot(x, wg_ref[...], NT)
        u = _dot(x, wu_ref[...], NT)
        sg = jax.nn.sigmoid(g)
        silu = g * sg
        a_ref[...] = (silu * u).astype(BF16)
        dadu_ref[...] = silu.astype(BF16)
        dadg_ref[...] = (u * (sg * (1.0 + g * (1.0 - sg)))).astype(BF16)

    w_spec = pl.BlockSpec((None, FF_SHARD, D_MODEL), lambda j, i: (j, 0, 0))
    o_spec = pl.BlockSpec((None, bm, FF_SHARD), lambda j, i: (j, i, 0))
    shp = jax.ShapeDtypeStruct((N_CHIPS, s, FF_SHARD), BF16)
    return _call(
        body, name=name, grid=(N_CHIPS, s // bm),
        in_specs=[pl.BlockSpec((bm, D_MODEL), lambda j, i: (i, 0)), w_spec, w_spec],
        out_specs=[o_spec, o_spec, o_spec], out_shape=[shp, shp, shp],
        sem=("parallel", "parallel"), args=[n, wg, wu])


def _ffn1_up_gather(xin, gain, g_sh, u_sh, d_sh):
    s = xin.shape[0]
    bm = _row_block(s)
    nrb = s // bm
    rows, cols = g_sh.shape

    def body(x_ref, gain_ref, gs, us, ds, n_ref, a_ref, dadu_ref, dadg_ref, wg, wu, wd, gbuf, ubuf,
             send, recv, qsend, qrecv, fsend, frecv, loc, ld):
        r, i = pl.program_id(0), pl.program_id(1)
        x, y, c = lax.axis_index("x"), lax.axis_index("y"), lax.axis_index("c")
        sib = (x, y, 1 - c)
        mine, _ = _half_rows(c, rows)
        quarters = _quarter_rows(c, rows)
        shards, fulls, bufs = (gs, us, ds), (wg, wu, wd), (gbuf, ubuf)

        def remote(src, dst, ssem, rsem, dev):
            return pltpu.make_async_remote_copy(src_ref=src, dst_ref=dst, send_sem=ssem, recv_sem=rsem,
                                                device_id=dev, device_id_type=_MESH)

        def peer(rel):
            return ((1 - x) if rel & 2 else x, (1 - y) if rel & 1 else y, c)

        def ici(k, rel, dev=sib):
            return remote(shards[k].at[mine], fulls[k].at[rel, mine], send.at[k, rel - 1], recv.at[k, rel - 1], dev)

        def quarter(k, which, dev=sib):
            slab, q = ((2, quarters[0]), (1, quarters[1]))[which]
            return remote(fulls[k].at[slab, q], fulls[k].at[3, q], qsend.at[k, which], qrecv.at[k, which], dev)

        def fwd(k, rel):
            return remote(fulls[k].at[rel, mine], fulls[k].at[rel, mine], fsend.at[k, rel - 1], frecv.at[k, rel - 1], sib)

        def own(k):
            return pltpu.make_async_copy(shards[k], fulls[k].at[0], loc.at[k])

        def load(slab):
            for k in (0, 1):
                pltpu.make_async_copy(shards[k] if slab == 0 else fulls[k].at[slab], bufs[k], ld.at[k]).start()
            for k in (0, 1):
                pltpu.make_async_copy(shards[k] if slab == 0 else fulls[k].at[slab], bufs[k], ld.at[k]).wait()

        def from_neighbour(ks, rel):
            for k in ks:
                ici(k, rel).wait_recv()
                fwd(k, rel).start()
                quarter(k, 0 if rel == 2 else 1, peer(1 if rel == 2 else 2)).start()
            for k in ks:
                fwd(k, rel).wait_recv()

        def from_diagonal(ks):
            for k in ks:
                quarter(k, 0).wait_recv()
                quarter(k, 1).wait_recv()
                fwd(k, 3).start()
            for k in ks:
                fwd(k, 3).wait_recv()

        @pl.when(jnp.logical_and(r == 0, i == 0))
        def _():
            for k in range(3):
                own(k).start()
            for rel in (1, 2):
                for k in (0, 1):
                    ici(k, rel, peer(rel)).start()
            load(0)

        @pl.when(jnp.logical_and(r == 1, i == 0))
        def _():
            from_neighbour((0, 1), 1)
            load(1)
            for rel in (1, 2):
                ici(2, rel, peer(rel)).start()

        @pl.when(jnp.logical_and(r == 2, i == 0))
        def _():
            from_neighbour((0, 1), 2)
            load(2)

        @pl.when(jnp.logical_and(r == 3, i == 0))
        def _():
            from_diagonal((0, 1))
            load(3)

        xv = _norm_bf16(x_ref[...], gain_ref[...])

        @pl.when(r == 0)
        def _():
            n_ref[...] = xv

        g = _dot(xv, gbuf[...], NT)
        u = _dot(xv, ubuf[...], NT)
        sg = jax.nn.sigmoid(g)
        silu = g * sg
        a_ref[...] = (silu * u).astype(BF16)
        dadu_ref[...] = silu.astype(BF16)
        dadg_ref[...] = (u * (sg * (1.0 + g * (1.0 - sg)))).astype(BF16)

        @pl.when(jnp.logical_and(r == 3, i == nrb - 1))
        def _():
            from_neighbour((2,), 1)
            from_neighbour((2,), 2)
            from_diagonal((2,))
            for k in range(3):
                for rel in (1, 2):
                    ici(k, rel).wait_send()
                for which in (0, 1):
                    quarter(k, which).wait_send()
                for rel in (1, 2, 3):
                    fwd(k, rel).wait_send()
                own(k).wait()

    o_spec = pl.BlockSpec((None, bm, FF_SHARD), lambda r, i: (r, i, 0))
    act = jax.ShapeDtypeStruct((N_CHIPS, s, FF_SHARD), BF16)
    full = jax.ShapeDtypeStruct((N_CHIPS, rows, cols), BF16)
    dma = pltpu.SemaphoreType.DMA
    if _PLAN is not None:
        _PLAN.last_slab_step = 3 * nrb
    n_spec = pl.BlockSpec((bm, D_MODEL), lambda r, i: (jnp.where(r == 0, i, nrb - 1), 0))
    return _call(
        body, name="ffn1_up", grid=(N_CHIPS, nrb),
        in_specs=[pl.BlockSpec((bm, D_MODEL), lambda r, i: (i, 0)), pl.BlockSpec((1, D_MODEL), lambda r, i: (0, 0)),
                  _ANY, _ANY, _ANY],
        out_specs=[n_spec, o_spec, o_spec, o_spec, _ANY, _ANY, _ANY],
        out_shape=[jax.ShapeDtypeStruct((s, D_MODEL), BF16), act, act, act, full, full, full],
        scratch_shapes=[pltpu.VMEM((rows, cols), BF16), pltpu.VMEM((rows, cols), BF16), dma((3, 2)), dma((3, 2)),
                        dma((3, 2)), dma((3, 2)), dma((3, 3)), dma((3, 3)), dma((3,)), dma((2,))],
        sem=("arbitrary", "arbitrary"), args=[xin, gain, g_sh, u_sh, d_sh])


def _residual_epilogue(alpha, with_norm):
    if not with_norm:
        return lambda acc, r: (r + alpha * acc,)

    def epilogue(acc, r, g):
        h = r + alpha * acc
        rs = lax.rsqrt(jnp.mean(h * h, axis=-1, keepdims=True) + RMS_EPS)
        return h, (h * rs) * g

    return epilogue


def _residual_outs(s, bm, gain):
    row = pl.BlockSpec((bm, D_MODEL), lambda i, k: (i, 0))
    outs = [(jax.ShapeDtypeStruct((s, D_MODEL), F32), row)]
    if gain is None:
        return [], outs
    return [(gain, pl.BlockSpec((1, D_MODEL), lambda i, k: (0, 0)))], outs + [(jax.ShapeDtypeStruct((s, D_MODEL), BF16), row)]


def _loss_epilogue(acc, res, g, target):
    d = acc.shape[-1]
    h = res + 0.5 * acc
    r = lax.rsqrt(jnp.mean(h * h, axis=-1, keepdims=True) + RMS_EPS)
    err = (h * r) * g - target
    part = 0.5 * jnp.sum(jnp.mean(err * err, axis=-1, keepdims=True), axis=0, keepdims=True)
    dx, dgain = _rms_bwd_math(err * (1.0 / d), h, g, d)
    return dx, dx, jnp.broadcast_to(part, (1, 128)), dgain


def _ffn_down(name, a, wd, res, gain=None, loss=None):
    s = a.shape[1]
    bm = _row_block(s, 512)
    row = pl.BlockSpec((bm, D_MODEL), lambda i, k: (i, 0))
    terms = [(a, pl.BlockSpec((None, bm, FF_SHARD), lambda i, k, j=j: (j, i, 0)),
              wd, pl.BlockSpec((None, FF_SHARD, D_MODEL), lambda i, k, j=j: (j, 0, 0)), NN) for j in range(N_CHIPS)]
    if loss is not None:
        vec = pl.BlockSpec((1, D_MODEL), lambda i, k: (0, 0))
        outs = [(jax.ShapeDtypeStruct((s, D_MODEL), F32), row), (jax.ShapeDtypeStruct((s, D_MODEL), BF16), row),
                (jax.ShapeDtypeStruct((1, 128), F32), pl.BlockSpec((1, 128), lambda i, k: (0, 0))),
                (jax.ShapeDtypeStruct((1, D_MODEL), F32), vec)]
        return _matmul(name, (s // bm, 1), terms, [(res, row), (loss[0], vec), (loss[1], row)], outs,
                       _loss_epilogue, None, summed=(2, 3))
    extras, outs = _residual_outs(s, bm, gain)
    res_out = _matmul(name, (s // bm, 1), terms, [(res, row)] + extras, outs,
                      _residual_epilogue(0.5, gain is not None), None)
    return res_out if gain is not None else res_out[0]


def _norm_bwd_epilogue(width):
    def epilogue(acc, h, g, dres):
        dx, dgain = _rms_bwd_math(acc, h, g, width)
        dx = dx + dres
        return dx, dx, dgain

    return epilogue


def _norm_bwd_operands(s, bm, h, gain, dres):
    row = pl.BlockSpec((bm, D_MODEL), lambda i, k: (i, 0))
    vec = pl.BlockSpec((1, D_MODEL), lambda i, k: (0, 0))
    extras = [(h, row), (gain, vec), (dres, row)]
    outs = [(jax.ShapeDtypeStruct((s, D_MODEL), F32), row), (jax.ShapeDtypeStruct((s, D_MODEL), BF16), row),
            (jax.ShapeDtypeStruct((1, D_MODEL), F32), vec)]
    return extras, outs, (2,)


def _ffn_bwd(tag, dh, n, dadg, dadu, a, wg, wu, wd, grads, norm_bwd=None):
    s = dh.shape[0]
    bm = _row_block(s)
    bk = _reduce_block(s)
    nk = s // bk

    def act_bwd(acc, dg_da, du_da):
        da = 0.5 * acc
        return da * dg_da.astype(F32), da * du_da.astype(F32)

    slab = pl.BlockSpec((None, bm, FF_SHARD), lambda j, i, k: (j, i, 0))
    shp = jax.ShapeDtypeStruct((N_CHIPS, s, FF_SHARD), BF16)
    dg, du = _matmul(
        tag + "_dact", (N_CHIPS, s // bm, 1),
        [(dh, pl.BlockSpec((bm, D_MODEL), lambda j, i, k: (i, 0)),
          wd, pl.BlockSpec((None, FF_SHARD, D_MODEL), lambda j, i, k: (j, 0, 0)), NT)],
        [(dadg, slab), (dadu, slab)], [(shp, slab), (shp, slab)], act_bwd, None)

    grads[tag + "_w_down"] = _matmul(
        tag + "_dwd", (N_CHIPS, nk),
        [(a, pl.BlockSpec((None, bk, FF_SHARD), lambda j, k: (j, k, 0)),
          dh, pl.BlockSpec((bk, D_MODEL), lambda j, k: (k, 0)), TN)],
        [], [(jax.ShapeDtypeStruct((N_CHIPS, FF_SHARD, D_MODEL), BF16),
              pl.BlockSpec((None, FF_SHARD, D_MODEL), lambda j, k: (j, 0, 0)))],
        lambda acc: (0.5 * acc,), (FF_SHARD, D_MODEL))[0]

    def dw_up(nm, dact):
        return _matmul(
            nm, (N_CHIPS, nk),
            [(dact, pl.BlockSpec((None, bk, FF_SHARD), lambda j, k: (j, k, 0)),
              n, pl.BlockSpec((bk, D_MODEL), lambda j, k: (k, 0)), TN)],
            [], [(jax.ShapeDtypeStruct((N_CHIPS, FF_SHARD, D_MODEL), BF16),
                  pl.BlockSpec((None, FF_SHARD, D_MODEL), lambda j, k: (j, 0, 0)))],
            _ident, (FF_SHARD, D_MODEL))[0]

    grads[tag + "_w_gate"] = dw_up(tag + "_dwg", dg)
    grads[tag + "_w_up"] = dw_up(tag + "_dwu", du)

    bn = _row_block(s, 512)
    steps = s // bn // 2
    prev, dgain = (), None
    for part, off in (("_dn_a", 0), ("_dn_b", steps)):
        row = pl.BlockSpec((bn, D_MODEL), lambda i, k, off=off: (i + off, 0))
        terms = []
        for j in range(N_CHIPS):
            a_slab = pl.BlockSpec((None, bn, FF_SHARD), lambda i, k, j=j, off=off: (j, i + off, 0))
            w_slab = pl.BlockSpec((None, FF_SHARD, D_MODEL), lambda i, k, j=j: (j, 0, 0))
            terms += [(dg, a_slab, wg, w_slab, NN), (du, a_slab, wu, w_slab, NN)]
        if norm_bwd is None:
            prev = _matmul(tag + part, (steps, 1), terms, [], [(jax.ShapeDtypeStruct((s, D_MODEL), F32), row)],
                           _ident, None, fill=prev)
            continue
        h, gain, dres = norm_bwd
        vec = pl.BlockSpec((1, D_MODEL), lambda i, k: (0, 0))
        res = _matmul(
            tag + part, (steps, 1), terms, [(h, row), (gain, vec), (dres, row)],
            [(jax.ShapeDtypeStruct((s, D_MODEL), F32), row), (jax.ShapeDtypeStruct((s, D_MODEL), BF16), row),
             (jax.ShapeDtypeStruct((1, D_MODEL), F32), vec)],
            _norm_bwd_epilogue(D_MODEL), None, fill=prev, summed=(2,))
        prev = res[:2]
        dgain = res[2] if dgain is None else dgain + res[2]
    return prev[0] if norm_bwd is None else (prev[0], prev[1], dgain)


def _mm_nn(name, a, b, out_dtype, res=None, gain=None):
    s, k = a.shape
    nn = b.shape[1]
    bm = _row_block(s)
    row = pl.BlockSpec((bm, nn), lambda i, kk: (i, 0))
    term = [(a, pl.BlockSpec((bm, k), lambda i, kk: (i, 0)), b, pl.BlockSpec((k, nn), lambda i, kk: (0, 0)), NN)]
    if res is None:
        return _matmul(name, (s // bm, 1), term, [], [(jax.ShapeDtypeStruct((s, nn), out_dtype), row)], _ident, None)[0]
    extras, outs = _residual_outs(s, bm, gain)
    res_out = _matmul(name, (s // bm, 1), term, [(res, row)] + extras, outs,
                      _residual_epilogue(1.0, gain is not None), None)
    return res_out if gain is not None else res_out[0]


def _mm_nt(name, a, b, out_dtype, attn_out=None, nh=0, dv=0):
    s, nn = a.shape
    k = b.shape[0]
    bm = _row_block(s)
    term = [(a, pl.BlockSpec((bm, nn), lambda i, kk: (i, 0)), b, pl.BlockSpec((k, nn), lambda i, kk: (0, 0)), NT)]
    out = (jax.ShapeDtypeStruct((s, k), out_dtype), pl.BlockSpec((bm, k), lambda i, kk: (i, 0)))
    if attn_out is None:
        return _matmul(name, (s // bm, 1), term, [], [out], _ident, None)[0]

    def with_delta(acc, o):
        do = acc.astype(out_dtype).astype(F32)
        cols = [jnp.sum(do[:, h * dv:(h + 1) * dv] * o[:, h * dv:(h + 1) * dv].astype(F32), axis=-1, keepdims=True)
                for h in range(nh)]
        return acc, jnp.stack(cols, axis=0)

    return _matmul(
        name, (s // bm, 1), term, [(attn_out, pl.BlockSpec((bm, nh * dv), lambda i, kk: (i, 0)))],
        [out, (jax.ShapeDtypeStruct((nh, s, 1), F32), pl.BlockSpec((nh, bm, 1), lambda i, kk: (0, i, 0)))],
        with_delta, None)


def _mm_nt_norm_bwd(name, a, b, h, gain, dres):
    s, nn = a.shape
    bm = _row_block(s, 512)
    extras, outs, summed = _norm_bwd_operands(s, bm, h, gain, dres)
    return _matmul(
        name, (s // bm, 1),
        [(a, pl.BlockSpec((bm, nn), lambda i, kk: (i, 0)), b, pl.BlockSpec(b.shape, lambda i, kk: (0, 0)), NT)],
        extras, outs, _norm_bwd_epilogue(D_MODEL), None, summed=summed)


def _w_in_dx_norm_bwd(dz, w_t, h, gain, dres):
    s = dz.shape[0]
    bm = _row_block(s, 512)
    epilogue = _norm_bwd_epilogue(D_MODEL)

    def body(dz_ref, w_ref, h_ref, g_ref, r_ref, dx_ref, dxb_ref, dg_ref):
        dzv = dz_ref[...]
        dn = jnp.concatenate([_dot(dzv, w_ref[j], NN) for j in range(N_CHIPS)], axis=1)
        dx, _, dgain = epilogue(dn, h_ref[...], g_ref[...], r_ref[...])
        dx_ref[...] = dx
        dxb_ref[...] = dx.astype(BF16)

        @pl.when(pl.program_id(0) == 0)
        def _():
            dg_ref[...] = dgain

        @pl.when(pl.program_id(0) > 0)
        def _():
            dg_ref[...] += dgain

    row = pl.BlockSpec((bm, D_MODEL), lambda i: (i, 0))
    vec = pl.BlockSpec((1, D_MODEL), lambda i: (0, 0))
    return _call(
        body, name="w_in_dx", grid=(s // bm,),
        in_specs=[row, pl.BlockSpec(w_t.shape, lambda i: (0, 0, 0)), row, vec, row],
        out_specs=[row, row, vec],
        out_shape=[jax.ShapeDtypeStruct((s, D_MODEL), F32), jax.ShapeDtypeStruct((s, D_MODEL), BF16),
                   jax.ShapeDtypeStruct((1, D_MODEL), F32)],
        sem=("arbitrary",), args=[dz, w_t, h, gain, dres])


def _mm_tn_stacked(name, a_list, b):
    s, nn = b.shape
    widths = [a.shape[1] for a in a_list]
    total = sum(widths)
    bk = _reduce_block(s)
    nk = s // bk
    na = len(a_list)

    def body(*refs):
        a_refs, b_ref, o_ref, acc_ref = refs[:na], refs[na], refs[na + 1], refs[na + 2]
        k = pl.program_id(0)

        @pl.when(k == 0)
        def _():
            acc_ref[...] = jnp.zeros_like(acc_ref)

        bv = b_ref[...]
        lo = 0
        for a_ref, w in zip(a_refs, widths):
            acc_ref[lo:lo + w, :] += _dot(a_ref[...], bv, TN)
            lo += w

        @pl.when(k == nk - 1)
        def _():
            o_ref[...] = acc_ref[...].astype(o_ref.dtype)

    return _call(
        body, name=name, grid=(nk,),
        in_specs=[pl.BlockSpec((bk, w), lambda k: (k, 0)) for w in widths] + [pl.BlockSpec((bk, nn), lambda k: (k, 0))],
        out_specs=[pl.BlockSpec((total, nn), lambda k: (0, 0))],
        out_shape=[jax.ShapeDtypeStruct((total, nn), BF16)],
        scratch_shapes=[pltpu.VMEM((total, nn), F32)], sem=("arbitrary",), args=list(a_list) + [b])[0]


def _mm_tn(name, a, b, out_dtype=BF16):
    s, k = a.shape
    nn = b.shape[1]
    bk = _reduce_block(s)
    return _matmul(
        name, (s // bk,),
        [(a, pl.BlockSpec((bk, k), lambda kk: (kk, 0)), b, pl.BlockSpec((bk, nn), lambda kk: (kk, 0)), TN)],
        [], [(jax.ShapeDtypeStruct((k, nn), out_dtype), pl.BlockSpec((k, nn), lambda kk: (0, 0)))],
        _ident, (k, nn))[0]


def _mm_heads_fwd(name, a, w, out_dtype, w_transposed=False):
    s, k = a.shape
    nh = w.shape[0]
    nn = w.shape[1] if w_transposed else w.shape[2]
    bm = _row_block(s)
    return _matmul(
        name, (nh, s // bm, 1),
        [(a, pl.BlockSpec((bm, k), lambda h, i, kk: (i, 0)),
          w, pl.BlockSpec((None,) + w.shape[1:], lambda h, i, kk: (h, 0, 0)), NT if w_transposed else NN)],
        [], [(jax.ShapeDtypeStruct((s, nh * nn), out_dtype), pl.BlockSpec((bm, nn), lambda h, i, kk: (i, h)))],
        _ident, None)[0]


def _mm_heads_bwd(name, dy, a, w, w_transposed=False):
    s, k = a.shape
    nh = w.shape[0]
    nn = w.shape[1] if w_transposed else w.shape[2]
    bm = _row_block(s)
    bk = _reduce_block(s)
    w_spec = pl.BlockSpec((None,) + w.shape[1:], lambda i, h: (h, 0, 0))
    da = _matmul(
        name + "_dx", (s // bm, nh),
        [(dy, pl.BlockSpec((bm, nn), lambda i, h: (i, h)), w, w_spec, NN if w_transposed else NT)],
        [], [(jax.ShapeDtypeStruct((s, k), F32), pl.BlockSpec((bm, k), lambda i, h: (i, 0)))], _ident, (bm, k))[0]
    a_term = (a, pl.BlockSpec((bk, k), lambda h, kk: (kk, 0)))
    dy_term = (dy, pl.BlockSpec((bk, nn), lambda h, kk: (kk, h)))
    lhs, rhs = (dy_term, a_term) if w_transposed else (a_term, dy_term)
    dw = _matmul(
        name + "_dw", (nh, s // bk), [lhs + rhs + (TN,)],
        [], [(jax.ShapeDtypeStruct(w.shape, BF16), pl.BlockSpec((None,) + w.shape[1:], lambda h, kk: (h, 0, 0)))],
        _ident, w.shape[1:])[0]
    return da, dw


def _w_in_fwd(n, w_t):
    s = n.shape[0]
    bm = _row_block(s)
    nh, nout, kin = w_t.shape
    terms = [(n, pl.BlockSpec((bm, kin), lambda i, k, j=j: (i, j)),
              w_t, pl.BlockSpec((None, nout, kin), lambda i, k, j=j: (j, 0, 0)), NT) for j in range(nh)]
    row = pl.BlockSpec((bm, nout), lambda i, k: (i, 0))
    return _matmul("w_in", (s // bm, 1), terms, [], [(jax.ShapeDtypeStruct((s, nout), F32), row)], _ident, None)[0]


def _w_in_dw(dz, n):
    s, nout = dz.shape
    kin = n.shape[1] // N_CHIPS
    bk = _reduce_block(s)
    return _matmul(
        "w_in_dw", (N_CHIPS, s // bk),
        [(dz, pl.BlockSpec((bk, nout), lambda j, k: (k, 0)), n, pl.BlockSpec((bk, kin), lambda j, k: (k, j)), TN)],
        [], [(jax.ShapeDtypeStruct((N_CHIPS, nout, kin), BF16), pl.BlockSpec((None, nout, kin), lambda j, k: (j, 0, 0)))],
        _ident, (nout, kin))[0]


def _rope_tables(positions):
    half = ROPE_DIM // 2
    freqs = 1.0 / (ROPE_BASE ** (jnp.arange(0, ROPE_DIM, 2, dtype=F32) / ROPE_DIM))
    ang = positions.astype(F32)[:, None] * freqs
    cos, sin = jnp.cos(ang), jnp.sin(ang)
    z = jnp.zeros_like(cos)
    tc = jnp.concatenate([cos, cos, z, z], axis=-1)
    ta = jnp.concatenate([-sin, z, z, z], axis=-1)
    tb = jnp.concatenate([z, sin, z, z], axis=-1)
    assert tc.shape[-1] == 4 * half
    return tc, ta, tb


def _rope(x, tc, ta, tb):
    return x * tc + pltpu.roll(x, 96, 1) * ta + pltpu.roll(x, 32, 1) * tb


def _rope_t(dy, tc, ta, tb):
    return dy * tc + pltpu.roll(dy * ta, 32, 1) + pltpu.roll(dy * tb, 96, 1)


def _norm_bf16(x, g):
    r = lax.rsqrt(jnp.mean(x * x, axis=-1, keepdims=True) + RMS_EPS)
    return ((x * r) * g).astype(BF16)


def _qkv_prep(z, q_gain, kv_gain, wq_t, wkv, tc, ta, tb):
    s = z.shape[0]
    bm = _row_block(s, 512)

    def body(zq_ref, zkv_ref, zkr_ref, qg_ref, kvg_ref, wq_ref, wkv_ref, tc_ref, ta_ref, tb_ref,
             qn_ref, kvn_ref, q_ref, k_ref, v_ref):
        c, a, b = tc_ref[...], ta_ref[...], tb_ref[...]
        qn = _norm_bf16(zq_ref[...], qg_ref[...])
        kvn = _norm_bf16(zkv_ref[...], kvg_ref[...])
        qn_ref[...] = qn
        kvn_ref[...] = kvn
        kpe = _rope(zkr_ref[...], c, a, b).astype(BF16)
        for h in range(MLA_HEADS):
            lo = h * HEAD_QK
            qp = _dot(qn, wq_ref[h], NT)
            q_ref[:, lo:lo + 128] = qp[:, :128].astype(BF16)
            q_ref[:, lo + 128:lo + 256] = _rope(qp[:, 128:], c, a, b).astype(BF16)
            kv = _dot(kvn, wkv_ref[h], NN)
            k_ref[:, lo:lo + 128] = kv[:, :128].astype(BF16)
            k_ref[:, lo + 128:lo + 256] = kpe
            v_ref[:, h * HEAD_V:(h + 1) * HEAD_V] = kv[:, 128:].astype(BF16)

    def cols(width, blk):
        return pl.BlockSpec((bm, width), lambda i: (i, blk))

    def whole(a):
        return pl.BlockSpec(a.shape, lambda i: (0,) * a.ndim)

    tab = cols(128, 0)
    return _call(
        body, name="qkv_prep", grid=(s // bm,),
        in_specs=[cols(Q_LORA, 0), cols(KV_LORA, 2), cols(128, 3), whole(q_gain), whole(kv_gain), whole(wq_t),
                  whole(wkv), tab, tab, tab],
        out_specs=[cols(Q_LORA, 0), cols(KV_LORA, 0), cols(1024, 0), cols(1024, 0), cols(512, 0)],
        out_shape=[jax.ShapeDtypeStruct((s, Q_LORA), BF16), jax.ShapeDtypeStruct((s, KV_LORA), BF16),
                   jax.ShapeDtypeStruct((s, 1024), BF16), jax.ShapeDtypeStruct((s, 1024), BF16),
                   jax.ShapeDtypeStruct((s, 512), BF16)],
        sem=("parallel",), args=[z, z, z, q_gain, kv_gain, wq_t, wkv, tc, ta, tb])


def _qkv_prep_bwd(dq, dk, dv, z, qn, kvn, q_gain, kv_gain, wq_t, wkv, tc, ta, tb):
    s = z.shape[0]
    bm = _row_block(s, 512)
    nsteps = s // bm

    def body(dq_ref, dk_ref, dv_ref, zq_ref, zkv_ref, qn_ref, kvn_ref, qg_ref, kvg_ref, wq_ref, wkv_ref,
             tc_ref, ta_ref, tb_ref, dz_ref, dqg_ref, dkvg_ref, dwq_ref, dwkv_ref, wq_acc, wkv_acc):
        i = pl.program_id(0)
        c, a, b = tc_ref[...], ta_ref[...], tb_ref[...]

        @pl.when(i == 0)
        def _():
            wq_acc[...] = jnp.zeros_like(wq_acc)
            wkv_acc[...] = jnp.zeros_like(wkv_acc)

        qn, kvn = qn_ref[...], kvn_ref[...]
        dqn = jnp.zeros((bm, Q_LORA), F32)
        dkvn = jnp.zeros((bm, KV_LORA), F32)
        dpe = jnp.zeros((bm, 128), F32)
        for h in range(MLA_HEADS):
            lo = h * HEAD_QK
            dqp = jnp.concatenate([dq_ref[:, lo:lo + 128],
                                   _rope_t(dq_ref[:, lo + 128:lo + 256].astype(F32), c, a, b).astype(BF16)], axis=1)
            dqn = dqn + _dot(dqp, wq_ref[h], NN)
            wq_acc[h] += _dot(dqp, qn, TN)
            dkv = jnp.concatenate([dk_ref[:, lo:lo + 128], dv_ref[:, h * HEAD_V:(h + 1) * HEAD_V]], axis=1)
            dkvn = dkvn + _dot(dkv, wkv_ref[h], NT)
            wkv_acc[h] += _dot(kvn, dkv, TN)
            dpe = dpe + dk_ref[:, lo + 128:lo + 256].astype(F32)
        dcq, dqg = _rms_bwd_math(dqn, zq_ref[...], qg_ref[...], Q_LORA)
        dckv, dkvg = _rms_bwd_math(dkvn, zkv_ref[...], kvg_ref[...], KV_LORA)
        dz_ref[:, 0:Q_LORA] = dcq.astype(BF16)
        dz_ref[:, Q_LORA:Q_LORA + KV_LORA] = dckv.astype(BF16)
        dz_ref[:, Q_LORA + KV_LORA:512] = _rope_t(dpe, c, a, b).astype(BF16)

        @pl.when(i == 0)
        def _():
            dqg_ref[...] = dqg
            dkvg_ref[...] = dkvg

        @pl.when(i > 0)
        def _():
            dqg_ref[...] += dqg
            dkvg_ref[...] += dkvg

        @pl.when(i == nsteps - 1)
        def _():
            dwq_ref[...] = wq_acc[...].astype(BF16)
            dwkv_ref[...] = wkv_acc[...].astype(BF16)

    def cols(width, blk):
        return pl.BlockSpec((bm, width), lambda i: (i, blk))

    def whole(shape):
        return pl.BlockSpec(shape, lambda i: (0,) * len(shape))

    tab = cols(128, 0)
    return _call(
        body, name="qkv_prep_bwd", grid=(nsteps,),
        in_specs=[cols(1024, 0), cols(1024, 0), cols(512, 0), cols(Q_LORA, 0), cols(KV_LORA, 2), cols(Q_LORA, 0),
                  cols(KV_LORA, 0), whole(q_gain.shape), whole(kv_gain.shape), whole(wq_t.shape), whole(wkv.shape),
                  tab, tab, tab],
        out_specs=[cols(512, 0), whole(q_gain.shape), whole(kv_gain.shape), whole(wq_t.shape), whole(wkv.shape)],
        out_shape=[jax.ShapeDtypeStruct((s, 512), BF16), jax.ShapeDtypeStruct(q_gain.shape, F32),
                   jax.ShapeDtypeStruct(kv_gain.shape, F32), jax.ShapeDtypeStruct(wq_t.shape, BF16),
                   jax.ShapeDtypeStruct(wkv.shape, BF16)],
        scratch_shapes=[pltpu.VMEM(wq_t.shape, F32), pltpu.VMEM(wkv.shape, F32)],
        sem=("arbitrary",), args=[dq, dk, dv, z, z, qn, kvn, q_gain, kv_gain, wq_t, wkv, tc, ta, tb])


def _causal_mask(s, row0, col0):
    rows = row0 + lax.broadcasted_iota(jnp.int32, s.shape, 0)
    cols = col0 + lax.broadcasted_iota(jnp.int32, s.shape, 1)
    return jnp.where(cols <= rows, s, -jnp.inf)


def _attn_fwd(name, q, k, k_off, v, v_off, nh, dq, dv, scale, causal, blk):
    sq, sk = q.shape[0], k.shape[0]
    bq = min(blk, sq)
    bk = min(blk, sk)
    nkv = sk // bk
    assert not causal or (sq == sk and bq == bk)

    hq = bq
    log2e = 1.4426950408889634
    c2 = scale * log2e

    def body(q_ref, k_ref, v_ref, o_ref, lse_ref):
        qi = pl.program_id(1)
        qs = (q_ref[...],)

        def step(j, carry, masked):
            rows = pl.ds(pl.multiple_of(j * bk, bk), bk)
            kb, vb = k_ref[rows, :], v_ref[rows, :]
            out = []
            for t, (m, l, acc) in enumerate(carry):
                s = _dot(qs[t], kb, NT) * c2
                if masked:
                    s = _causal_mask(s, qi * bq + t * hq, j * bk)
                m_new = jnp.maximum(m, jnp.max(s, axis=-1, keepdims=True))
                alpha = jnp.exp2(m - m_new)
                p = jnp.exp2(s - m_new)
                l = alpha * l + jnp.sum(p, axis=-1, keepdims=True)
                acc = alpha * acc + _dot(p, vb, NN)
                out.append((m_new, l, acc))
            return tuple(out)

        one = (jnp.full((hq, 1), -jnp.inf, F32), jnp.zeros((hq, 1), F32), jnp.zeros((hq, dv), F32))
        init = (one,)
        if causal:
            carry = lax.fori_loop(0, qi, lambda j, c: step(j, c, False), init)
            fin = step(qi, carry, True)
        else:
            fin = lax.fori_loop(0, nkv, lambda j, c: step(j, c, False), init)
        for t, (m, l, acc) in enumerate(fin):
            o_ref[t * hq:(t + 1) * hq, :] = (acc / l).astype(o_ref.dtype)
            lse_ref[t * hq:(t + 1) * hq, :] = m * (1.0 / log2e) + jnp.log(l)

    return _call(
        body, name=name, grid=(nh, sq // bq),
        in_specs=[pl.BlockSpec((bq, dq), lambda h, i: (i, h)),
                  pl.BlockSpec((sk, dq), lambda h, i: (0, k_off + h)),
                  pl.BlockSpec((sk, dv), lambda h, i: (0, v_off + h))],
        out_specs=[pl.BlockSpec((bq, dv), lambda h, i: (i, h)), pl.BlockSpec((None, bq, 1), lambda h, i: (h, i, 0))],
        out_shape=[jax.ShapeDtypeStruct((sq, nh * dv), BF16), jax.ShapeDtypeStruct((nh, sq, 1), F32)],
        sem=("parallel", "parallel"), args=[q, k, v])


def _attn_bwd(name, q, k, k_off, v, v_off, do, do_off, lse, delta, nh, dq, dv, scale, causal, blk):
    sq, sk = q.shape[0], k.shape[0]
    bq = min(blk, sq)
    bk = min(blk, sk)
    nq = sq // bq
    assert not causal or (sq == sk and bq == bk)

    nkv = sk // bk

    def body(q_ref, k_ref, v_ref, do_ref, lse_ref, dl_ref, dq_ref, dk_ref, dv_ref, dq_acc, dk_acc, dv_acc):
        j = pl.program_id(1)

        @pl.when(j == 0)
        def _():
            dq_acc[...] = jnp.zeros_like(dq_acc)

        dk_acc[...] = jnp.zeros_like(dk_acc)
        dv_acc[...] = jnp.zeros_like(dv_acc)
        kv = k_ref[...]
        vv = v_ref[...]

        def step(i, masked):
            rows = pl.ds(pl.multiple_of(i * bq, bq), bq)
            qv = q_ref[rows, :]
            dov = do_ref[rows, :].astype(BF16)
            s = _dot(qv, kv, NT) * scale
            if masked:
                s = _causal_mask(s, i * bq, j * bk)
            p = jnp.exp(s - lse_ref[rows, :])
            dp = _dot(dov, vv, NT)
            ds = (p * (dp - dl_ref[rows, :]) * scale).astype(BF16)
            dv_acc[...] += _dot(p, dov, TN)
            dk_acc[...] += _dot(ds, qv, TN)
            dq_acc[rows, :] += _dot(ds, kv, NN)

        if causal:
            step(j, True)

            def loop(i, c):
                step(i, False)
                return c

            lax.fori_loop(j + 1, nq, loop, 0)
        else:
            def loop(i, c):
                step(i, False)
                return c

            lax.fori_loop(0, nq, loop, 0)
        dk_ref[...] = dk_acc[...].astype(dk_ref.dtype)
        dv_ref[...] = dv_acc[...].astype(dv_ref.dtype)

        @pl.when(j == nkv - 1)
        def _():
            dq_ref[...] = dq_acc[...].astype(dq_ref.dtype)

    stat = pl.BlockSpec((None, sq, 1), lambda h, j: (h, 0, 0))
    return _call(
        body, name=name, grid=(nh, sk // bk),
        in_specs=[pl.BlockSpec((sq, dq), lambda h, j: (0, h)),
                  pl.BlockSpec((bk, dq), lambda h, j: (j, k_off + h)),
                  pl.BlockSpec((bk, dv), lambda h, j: (j, v_off + h)),
                  pl.BlockSpec((sq, dv), lambda h, j: (0, do_off + h)), stat, stat],
        out_specs=[pl.BlockSpec((sq, dq), lambda h, j: (0, h)),
                   pl.BlockSpec((bk, dq), lambda h, j: (j, h)),
                   pl.BlockSpec((bk, dv), lambda h, j: (j, h))],
        out_shape=[jax.ShapeDtypeStruct((sq, nh * dq), BF16), jax.ShapeDtypeStruct((sk, nh * dq), BF16),
                   jax.ShapeDtypeStruct((sk, nh * dv), BF16)],
        scratch_shapes=[pltpu.VMEM((sq, dq), F32), pltpu.VMEM((bk, dq), F32), pltpu.VMEM((bk, dv), F32)],
        sem=("parallel", "arbitrary"), args=[q, k, v, do, lse, delta])


def _pool_diff(z, g):
    s = z.shape[0]
    t = lax.broadcasted_iota(jnp.int32, z.shape, 0)
    acc = z
    sums = []
    for k in (1, 2, 4, 8):
        acc = acc + jnp.where(t >= k, pltpu.roll(acc, k, 0), 0.0)
        sums.append(acc)
    win = jnp.where(g == 0, sums[0], jnp.where(g == 1, sums[1], jnp.where(g == 2, sums[2], sums[3])))
    w = lax.shift_left(jnp.int32(2), g)
    count = jnp.minimum(t + 1, w).astype(F32)
    del s
    return win / count - z, count


def _pool_fwd(z, pool_w, pool_scale):
    s = z.shape[0]

    def body(z_ref, w_ref, sc_ref, o_ref):
        diff, _ = _pool_diff(z_ref[...], pl.program_id(0))
        o_ref[...] = (_dot(diff, w_ref[...], NN) * sc_ref[...]).astype(o_ref.dtype)

    return _call(
        body, name="pool_fwd", grid=(POOL_GROUPS,),
        in_specs=[pl.BlockSpec((s, POOL_CH), lambda g: (0, 4 + g)),
                  pl.BlockSpec((None, POOL_CH, POOL_CH), lambda g: (g, 0, 0)),
                  pl.BlockSpec((1, POOL_CH), lambda g: (0, g))],
        out_specs=[pl.BlockSpec((s, POOL_CH), lambda g: (0, g))],
        out_shape=[jax.ShapeDtypeStruct((s, POOL_GROUPS * POOL_CH), BF16)],
        sem=("parallel",), args=[z, pool_w, pool_scale])[0]


def _pool_bwd(dcat, z, pool_w, pool_scale):
    s = z.shape[0]

    def body(dp_ref, z_ref, w_ref, sc_ref, dz_ref, dw_ref, dsc_ref):
        g = pl.program_id(0)
        diff, count = _pool_diff(z_ref[...], g)
        dpf = dp_ref[...].astype(F32)
        u = _dot(diff, w_ref[...], NN)
        dsc_ref[...] = jnp.sum(dpf * u, axis=0, keepdims=True)
        du = (dpf * sc_ref[...]).astype(BF16)
        dw_ref[...] = _dot(diff, du, TN)
        ddiff = _dot(du, w_ref[...], NT)
        t = lax.broadcasted_iota(jnp.int32, ddiff.shape, 0)
        acc = ddiff / count
        sums = []
        for k in (1, 2, 4, 8):
            acc = acc + jnp.where(t < s - k, pltpu.roll(acc, s - k, 0), 0.0)
            sums.append(acc)
        win = jnp.where(g == 0, sums[0], jnp.where(g == 1, sums[1], jnp.where(g == 2, sums[2], sums[3])))
        dz_ref[...] = (win - ddiff).astype(dz_ref.dtype)

    return pl.pallas_call(
        body, name="pool_bwd", grid=(POOL_GROUPS,),
        in_specs=[pl.BlockSpec((s, POOL_CH), lambda g: (0, 4 + g)),
                  pl.BlockSpec((s, POOL_CH), lambda g: (0, 4 + g)),
                  pl.BlockSpec((None, POOL_CH, POOL_CH), lambda g: (g, 0, 0)),
                  pl.BlockSpec((1, POOL_CH), lambda g: (0, g))],
        out_specs=[pl.BlockSpec((s, POOL_CH), lambda g: (0, g)),
                   pl.BlockSpec((None, POOL_CH, POOL_CH), lambda g: (g, 0, 0)),
                   pl.BlockSpec((1, POOL_CH), lambda g: (0, g))],
        out_shape=[jax.ShapeDtypeStruct((s, POOL_GROUPS * POOL_CH), BF16),
                   jax.ShapeDtypeStruct((POOL_GROUPS, POOL_CH, POOL_CH), F32),
                   jax.ShapeDtypeStruct((1, POOL_GROUPS * POOL_CH), F32)],
        compiler_params=_params("parallel"),
    )(dcat, z, pool_w, pool_scale)


def _local_step(x, mem, positions, target, w, grads):
    tc, ta, tb = _rope_tables(positions)
    blk = _ATT_BLOCK

    if "ffn1_shards" in w:
        n1, a1, dadu1, dadg1, w["ffn1_w_gate"], w["ffn1_w_up"], w["ffn1_w_down"] = _ffn1_up_gather(
            x, w["ffn1_norm"], *w["ffn1_shards"])
    else:
        n1 = _rmsnorm_fwd("ffn1_norm", x, w["ffn1_norm"], D_MODEL)
        a1, dadu1, dadg1 = _ffn_up("ffn1_up", n1, w["ffn1_w_gate"], w["ffn1_w_up"])
    h1, n2 = _ffn_down("ffn1_down", a1, w["ffn1_w_down"], x, w["mix_norm"])
    z = _w_in_fwd(n2, w["w_in"])
    qn, kvn, qf, kf, vf = _qkv_prep(z, w["q_norm"], w["kv_norm"], w["w_q_up"], w["w_kv_up"], tc, ta, tb)
    att, lse = _attn_fwd("mla_fwd", qf, kf, 0, vf, 0, MLA_HEADS, HEAD_QK, HEAD_V, MLA_SCALE, True, blk)
    pool = _pool_fwd(z, w["pool_w"], w["pool_scale"])
    s = x.shape[0]
    bm = _row_block(s)
    row = pl.BlockSpec((bm, D_MODEL), lambda i, k: (i, 0))
    half = pl.BlockSpec((bm, 512), lambda i, k: (i, 0))
    h2, n3 = _matmul(
        "w_out", (s // bm, 1),
        [(att, half, w["w_out"], pl.BlockSpec((512, D_MODEL), lambda i, k: (0, 0)), NN),
         (pool, half, w["w_out"], pl.BlockSpec((512, D_MODEL), lambda i, k: (1, 0)), NN)],
        [(h1, row)] + _residual_outs(s, bm, w["xattn_norm"])[0], _residual_outs(s, bm, w["xattn_norm"])[1],
        _residual_epilogue(1.0, True), None)
    memn = _rmsnorm_fwd("mem_norm", mem, w["mem_norm"], D_MODEL)
    qm = _mm_nn("w_mq", n3, w["w_mq"], BF16)
    kvm = _mm_heads_fwd("w_mkv", memn, w["w_mkv"], BF16)
    om, lse_m = _attn_fwd("xattn_fwd", qm, kvm, 0, kvm, MEM_HEADS, MEM_HEADS, MEM_HEAD_DIM, MEM_HEAD_DIM,
                          MEM_SCALE, False, blk)
    h3, n4 = _mm_nn("w_mo", om, w["w_mo"], F32, res=h2, gain=w["ffn2_norm"])
    a2, dadu2, dadg2 = _ffn_up("ffn2_up", n4, w["ffn2_w_gate"], w["ffn2_w_up"])
    dh4, dh4b, loss_vec, d_final = _ffn_down("ffn2_down", a2, w["ffn2_w_down"], h3, loss=(w["final_norm"], target))
    grads["final_norm"] = d_final

    dh3, dh3b, grads["ffn2_norm"] = _ffn_bwd("ffn2", dh4b, n4, dadg2, dadu2, a2, w["ffn2_w_gate"], w["ffn2_w_up"],
                                             w["ffn2_w_down"], grads, norm_bwd=(h3, w["ffn2_norm"], dh4))

    dom, delta_m = _mm_nt("w_mo_dx", dh3b, w["w_mo"], BF16, attn_out=om, nh=MEM_HEADS, dv=MEM_HEAD_DIM)
    grads["w_mo"] = _mm_tn("w_mo_dw", om, dh3b)
    dqm, dkm, dvm = _attn_bwd("xattn_bwd", qm, kvm, 0, kvm, MEM_HEADS, dom, 0, lse_m, delta_m, MEM_HEADS,
                              MEM_HEAD_DIM, MEM_HEAD_DIM, MEM_SCALE, False, blk)
    dkvm = jnp.concatenate([dkm, dvm], axis=1)
    dh2, dh2b, grads["xattn_norm"] = _mm_nt_norm_bwd("w_mq_dx", dqm, w["w_mq"], h2, w["xattn_norm"], dh3)
    grads["w_mq"] = _mm_tn("w_mq_dw", n3, dqm)
    dmemn, grads["w_mkv"] = _mm_heads_bwd("w_mkv", dkvm, memn, w["w_mkv"])
    _, grads["mem_norm"] = _rmsnorm_bwd("mem_norm_bwd", dmemn, mem, w["mem_norm"], D_MODEL, out_dtype=BF16)

    dcat, delta = _mm_nt("w_out_dx", dh2b, w["w_out"], BF16, attn_out=att, nh=MLA_HEADS, dv=HEAD_V)
    grads["w_out"] = _mm_tn_stacked("w_out_dw", [att, pool], dh2b)
    dzp, grads["pool_w"], grads["pool_scale"] = _pool_bwd(dcat, z, w["pool_w"], w["pool_scale"])
    dqf, dkf, dvf = _attn_bwd("mla_bwd", qf, kf, 0, vf, 0, dcat, 0, lse, delta, MLA_HEADS, HEAD_QK, HEAD_V,
                              MLA_SCALE, True, blk)
    dz_lat, grads["q_norm"], grads["kv_norm"], grads["w_q_up"], grads["w_kv_up"] = _qkv_prep_bwd(
        dqf, dkf, dvf, z, qn, kvn, w["q_norm"], w["kv_norm"], w["w_q_up"], w["w_kv_up"], tc, ta, tb)
    dz = jnp.concatenate([dz_lat, dzp], axis=1)
    grads["w_in"] = _w_in_dw(dz, n2)
    dh1, dh1b, grads["mix_norm"] = _w_in_dx_norm_bwd(dz, w["w_in"], h1, w["mix_norm"], dh2)

    dn1 = _ffn_bwd("ffn1", dh1b, n1, dadg1, dadu1, a1, w["ffn1_w_gate"], w["ffn1_w_up"], w["ffn1_w_down"], grads)
    dx, grads["ffn1_norm"] = _rmsnorm_bwd("ffn1_norm_bwd", dn1, x, w["ffn1_norm"], D_MODEL, dres=dh1)
    return loss_vec[0, 0], dx


def _mesh_pos():
    x, y, c = lax.axis_index("x"), lax.axis_index("y"), lax.axis_index("c")
    chips = [(1 - x, y), (x, 1 - y), (1 - x, 1 - y)]
    chip_ids = [2 * cx + cy for cx, cy in chips]
    return x, y, c, 2 * x + y, chips, chip_ids


def _half_rows(c, rows):
    hr = rows // 2
    return pl.ds(pl.multiple_of(c * hr, 16), hr), pl.ds(pl.multiple_of((1 - c) * hr, 16), hr)


def _ag_ici_stage(shards):
    n = len(shards)

    def copies(ins, outs):
        x, y, c, me, chips, _ = _mesh_pos()
        out = []
        for k in range(n):
            mine, _ = _half_rows(c, ins[k].shape[0])
            out.append((ins[k], outs[k].at[me], None))
            for cx, cy in chips:
                out.append((ins[k].at[mine], outs[k].at[me, mine], (cx, cy, c)))
        return out

    return _Stage(shards, [jax.ShapeDtypeStruct((N_CHIPS,) + s.shape, s.dtype) for s in shards], 3 * n, n, copies)


def _quarter_rows(c, rows):
    qr = rows // 4
    return pl.ds(pl.multiple_of(c * 2 * qr, 16), qr), pl.ds(pl.multiple_of(c * 2 * qr + qr, 16), qr)


def _ag_d2d_stage(fulls):
    n = len(fulls)

    def copies(ins, outs):
        x, y, c, me, _, chip_ids = _mesh_pos()
        out = []
        for k in range(n):
            mine, _ = _half_rows(c, ins[k].shape[1])
            for j in range(3):
                out.append((ins[k].at[chip_ids[j], mine], outs[k].at[chip_ids[j], mine], (x, y, 1 - c)))
        return out

    return _Stage(fulls, [jax.ShapeDtypeStruct(f.shape, f.dtype) for f in fulls], 3 * n, 0, copies,
                  aliases={k: k for k in range(n)})


def _rs_swap_stage(grads):
    n = len(grads)

    def copies(ins, outs):
        x, y, c, _, _, _ = _mesh_pos()
        out = []
        for k in range(n):
            _, other = _half_rows(c, ins[k].shape[1])
            out.append((ins[k].at[:, other, :], outs[k], (x, y, 1 - c)))
        return out

    return _Stage(grads, [jax.ShapeDtypeStruct((N_CHIPS, g.shape[1] // 2, g.shape[2]), g.dtype) for g in grads],
                  n, 0, copies)


_REL_OF_PEER = (2, 1, 3)


def _rs_scatter_stage(sums, relative=False):
    n = len(sums)

    def copies(ins, outs):
        x, y, c, me, chips, chip_ids = _mesh_pos()
        out = []
        for k in range(n):
            mine, _ = _half_rows(c, 2 * ins[k].shape[1])
            out.append((ins[k].at[0 if relative else me], outs[k].at[0, mine, :], None))
            for j, (cx, cy) in enumerate(chips):
                slab = _REL_OF_PEER[j] if relative else chip_ids[j]
                out.append((ins[k].at[slab], outs[k].at[1 + j, mine, :], (cx, cy, c)))
        return out

    return _Stage(sums, [jax.ShapeDtypeStruct((N_CHIPS, 2 * s.shape[1], s.shape[2]), s.dtype) for s in sums],
                  3 * n, n, copies)


def _rs_mirror_stage(parts):
    n = len(parts)

    def copies(ins, outs):
        x, y, c, _, _, _ = _mesh_pos()
        out = []
        for k in range(n):
            mine, _ = _half_rows(c, ins[k].shape[1])
            out.append((ins[k].at[:, mine, :], outs[k].at[:, mine, :], (x, y, 1 - c)))
        return out

    return _Stage(parts, [jax.ShapeDtypeStruct(p.shape, p.dtype) for p in parts], n, 0, copies,
                  aliases={k: k for k in range(n)})


def _pair_add(name, gs, r1s, core):
    n = len(gs)

    def body(c_ref, *refs):
        for k in range(n):
            g_ref, r_ref, o_ref = refs[k], refs[n + k], refs[2 * n + k]
            o_ref[...] = (g_ref[...].astype(F32) + r_ref[...].astype(F32)).astype(BF16)

    def half(g):
        return pl.BlockSpec((None, g.shape[1] // 2, g.shape[2]), lambda j, c: (j, 0, 0))

    def mine(g):
        return pl.BlockSpec((None, g.shape[1] // 2, g.shape[2]), lambda j, c: (j, c[0], 0))

    return pl.pallas_call(
        body, name=name,
        grid_spec=pltpu.PrefetchScalarGridSpec(
            num_scalar_prefetch=1, grid=(N_CHIPS,),
            in_specs=[mine(g) for g in gs] + [half(g) for g in gs], out_specs=[half(g) for g in gs]),
        out_shape=[jax.ShapeDtypeStruct((N_CHIPS, g.shape[1] // 2, g.shape[2]), BF16) for g in gs],
        compiler_params=_params("parallel"),
    )(core, *gs, *r1s)


def _adamw_math(w, g, m, v):
    m = ADAM_B1 * m + (1.0 - ADAM_B1) * g
    v = ADAM_B2 * v + (1.0 - ADAM_B2) * (g * g)
    m_hat = m / (1.0 - ADAM_B1 ** ADAM_STEP)
    v_hat = v / (1.0 - ADAM_B2 ** ADAM_STEP)
    delta = -ADAM_LR * (m_hat / (jnp.sqrt(v_hat) + ADAM_EPS) + ADAM_WD * w)
    return delta, m, v


def _adamw_sum(name, ws, parts, ms, vs):
    n = len(ws)
    r, c = ws[0].shape
    assert all(w.shape == (r, c) for w in ws)
    br = r
    while br * c * 4 > (1 << 20) and br % 32 == 0:
        br //= 2

    def body(*refs):
        for k in range(n):
            w_ref, p_ref, m_ref, v_ref = refs[4 * k:4 * k + 4]
            g_ref, d_ref, nm_ref, nv_ref = refs[4 * n + 4 * k:4 * n + 4 * k + 4]
            g = p_ref[0].astype(F32)
            for j in range(1, N_CHIPS):
                g = g + p_ref[j].astype(F32)
            d, nm, nv = _adamw_math(w_ref[...], g, m_ref[...], v_ref[...])
            g_ref[...] = g
            d_ref[...] = d
            nm_ref[...] = nm
            nv_ref[...] = nv

    spec = pl.BlockSpec((br, c), lambda i: (i, 0))
    shp = jax.ShapeDtypeStruct((r, c), F32)
    args = [a for k in range(n) for a in (ws[k], parts[k], ms[k], vs[k])]
    res = _call(
        body, name=name, grid=(r // br,),
        in_specs=[spec, pl.BlockSpec((N_CHIPS, br, c), lambda i: (0, i, 0)), spec, spec] * n,
        out_specs=[spec] * (4 * n), out_shape=[shp] * (4 * n), sem=("parallel",), args=args)
    return [res[4 * k:4 * k + 4] for k in range(n)]


_SMALL_VECTORS = ("ffn1_norm", "mix_norm", "xattn_norm", "mem_norm", "ffn2_norm", "final_norm", "q_norm",
                  "kv_norm", "pool_scale")


_LOSS_ROW = 9
_VEC_ROWS = 16
_POOL_ROWS = POOL_GROUPS * POOL_CH


def _small_params_step(g, w, m, v, loss_local):
    names = list(_SMALL_VECTORS) + ["pool_w"]
    nv = len(_SMALL_VECTORS)
    widths = [g[n].shape[1] for n in _SMALL_VECTORS]
    shapes = {"vec": (_VEC_ROWS, D_MODEL), "pool": (_POOL_ROWS, POOL_CH)}

    def body(*refs):
        ins = refs[:4 * (nv + 1) + 1]
        outs = refs[len(ins):len(ins) + 4 * (nv + 1) + 1]
        vec_own, vec_sib, vec_all, pool_sib, pool_sum, pool_all, send, recv = refs[len(ins) + len(outs):]
        g_in, w_in, m_in, v_in = (ins[k * (nv + 1):(k + 1) * (nv + 1)] for k in range(4))
        loss_in = ins[-1]
        g_out, d_out, m_out, v_out = (outs[k * (nv + 1):(k + 1) * (nv + 1)] for k in range(4))
        loss_out = outs[-1]
        x, y, c, me, chips, chip_ids = _mesh_pos()
        sib = (x, y, 1 - c)

        def remote(src, dst, k, dev):
            return pltpu.make_async_remote_copy(src_ref=src, dst_ref=dst, send_sem=send.at[k], recv_sem=recv.at[k],
                                                device_id=dev, device_id_type=_MESH)

        vec_own[...] = jnp.zeros_like(vec_own)
        for i in range(nv):
            vec_own[i:i + 1, 0:widths[i]] = g_in[i][...]
        vec_own[_LOSS_ROW:_LOSS_ROW + 1, 0:128] = loss_in[...]
        swaps = [remote(vec_own, vec_sib, 0, sib), remote(g_in[nv], pool_sib, 1, sib)]
        for cp in swaps:
            cp.start()
        for cp in swaps:
            cp.wait()
        vec_all[me] = vec_own[...] + vec_sib[...]
        pool_sum[...] = g_in[nv][...] + pool_sib[...]
        pool_all[me] = pool_sum[...]
        hv, hp = _VEC_ROWS // 2, _POOL_ROWS // 2
        mine_v = pl.ds(pl.multiple_of(c * hv, 8), hv)
        mine_p = pl.ds(pl.multiple_of(c * hp, 8), hp)
        sends = []
        for j, (cx, cy) in enumerate(chips):
            sends.append(remote(vec_all.at[me, mine_v], vec_all.at[me, mine_v], 2 + j, (cx, cy, c)))
            sends.append(remote(pool_sum.at[mine_p], pool_all.at[me, mine_p], 5 + j, (cx, cy, c)))
        for cp in sends:
            cp.start()
        for cp in sends:
            cp.wait()
        mirrors = []
        for j in range(3):
            mirrors.append(remote(vec_all.at[chip_ids[j], mine_v], vec_all.at[chip_ids[j], mine_v], 8 + j, sib))
            mirrors.append(remote(pool_all.at[chip_ids[j], mine_p], pool_all.at[chip_ids[j], mine_p], 11 + j, sib))
        for cp in mirrors:
            cp.start()
        for cp in mirrors:
            cp.wait()
        vec_tot = vec_all[0]
        pool_tot = pool_all[0]
        for i in range(1, N_CHIPS):
            vec_tot = vec_tot + vec_all[i]
            pool_tot = pool_tot + pool_all[i]
        vec_sib[...] = vec_tot
        loss_out[...] = vec_sib[_LOSS_ROW:_LOSS_ROW + 1, 0:128]
        for i in range(nv + 1):
            gi = pool_tot if i == nv else vec_sib[i:i + 1, 0:widths[i]]
            d, nm, nvv = _adamw_math(w_in[i][...], gi, m_in[i][...], v_in[i][...])
            g_out[i][...] = gi
            d_out[i][...] = d
            m_out[i][...] = nm
            v_out[i][...] = nvv

    vm = pl.BlockSpec(memory_space=pltpu.VMEM)
    args = [d[n] for d in (g, w, m, v) for n in names] + [jnp.broadcast_to(loss_local.reshape(1, 1), (1, 128))]
    out_shape = [jax.ShapeDtypeStruct(g[n].shape, F32) for _ in range(4) for n in names]
    out_shape.append(jax.ShapeDtypeStruct((1, 128), F32))
    res = pl.pallas_call(
        body, name="small_params_step", in_specs=[vm] * len(args), out_specs=[vm] * len(out_shape),
        out_shape=out_shape,
        scratch_shapes=[pltpu.VMEM(shapes["vec"], F32), pltpu.VMEM(shapes["vec"], F32),
                        pltpu.VMEM((N_CHIPS,) + shapes["vec"], F32), pltpu.VMEM(shapes["pool"], F32),
                        pltpu.VMEM(shapes["pool"], F32), pltpu.VMEM((N_CHIPS,) + shapes["pool"], F32),
                        pltpu.SemaphoreType.DMA((14,)), pltpu.SemaphoreType.DMA((14,))],
        compiler_params=pltpu.CompilerParams(vmem_limit_bytes=V7X_VMEM_LIMIT_BYTES),
    )(*args)
    k = len(names)
    dicts = [dict(zip(names, res[i * k:(i + 1) * k])) for i in range(4)]
    return dicts[0], dicts[1], dicts[2], dicts[3], res[-1]


_WEIGHTS = ("ffn1_norm", "ffn1_w_gate", "ffn1_w_up", "ffn1_w_down", "mix_norm", "w_in", "q_norm", "w_q_up",
            "kv_norm", "w_kv_up", "pool_w", "pool_scale", "w_out", "xattn_norm", "mem_norm", "w_mq", "w_mkv",
            "w_mo", "ffn2_norm", "ffn2_w_gate", "ffn2_w_up", "ffn2_w_down", "final_norm")
_SHARDED = ("ffn1_w_gate", "ffn1_w_up", "ffn1_w_down", "w_in", "w_q_up", "w_kv_up", "w_out", "w_mq", "w_mkv",
            "w_mo", "ffn2_w_gate", "ffn2_w_up", "ffn2_w_down")
W_IN_SPLIT = Q_LORA + KV_LORA + ROPE_DIM


_FFN1 = ("ffn1_w_gate", "ffn1_w_up", "ffn1_w_down")
_TRANSPOSED = ("ffn1_w_gate", "ffn1_w_up", "ffn2_w_gate", "ffn2_w_up", "w_in", "w_q_up")


def _local_view(name, a):
    return jnp.swapaxes(a, 1, 2)[0] if name in _TRANSPOSED else a[0]


def _global_view(name, a):
    return jnp.swapaxes(a[None], 1, 2) if name in _TRANSPOSED else a[None]


def _pad_shard(name, a):
    if name == "w_in":
        return jnp.concatenate([a[:W_IN_SPLIT], jnp.zeros((64, a.shape[1]), a.dtype), a[W_IN_SPLIT:]], axis=0)
    if name == "w_q_up":
        return jnp.pad(a, ((0, 64), (0, 0)))
    return a


def _unpad_shard(name, a):
    if name == "w_in":
        return jnp.concatenate([a[:, :W_IN_SPLIT], a[:, W_IN_SPLIT + 64:]], axis=1)
    if name == "w_q_up":
        return a[:, :192]
    return a


def _stacked(g):
    return g if g.ndim == 3 else g.reshape(N_CHIPS, g.shape[0] // N_CHIPS, g.shape[1])


class _Plan:
    AG_UNITS = (
        (("w_in", "w_q_up", "w_kv_up"), "ffn1_up"),
        (("w_out",), "w_in"),
        (("w_mq",), "qkv_prep"),
        (("w_mkv", "w_mo", "ffn2_w_gate"), "mla_fwd"),
        (("ffn2_w_up",), "xattn_fwd"),
        (("ffn2_w_down",), "ffn2_up"),
    )
    RS_UNITS = (
        (("ffn2_w_gate", "ffn2_w_up", "ffn2_w_down"), "ffn2_dn_a", "mla_bwd", "qkv_prep_bwd"),
        (("w_mo", "w_mq", "w_mkv"), "w_out_dx", "mla_bwd", "qkv_prep_bwd"),
        (("w_out", "w_q_up", "w_kv_up", "w_in"), "w_in_dx", "ffn1_dact", "ffn1_dwd"),
        (("ffn1_w_down",), "ffn1_dwg", "ffn1_dwu", "ffn1_dn_a"),
        (("ffn1_w_gate",), "ffn1_dwu", "ffn1_dn_a", "ffn1_dn_b"),
        (("ffn1_w_up",), "ffn1_dn_a", "ffn1_dn_b", "adamw_w_kv_up"),
    )
    ADAMW_ORDER = (("w_kv_up",), ("ffn2_w_gate", "ffn2_w_up"), ("ffn2_w_down", "ffn1_w_down"), ("w_mo", "w_mq", "w_out"),
                   ("w_mkv",), ("w_q_up",), ("w_in",), ("ffn1_w_gate", "ffn1_w_up"))

    def __init__(self, shards, w, grads, core):
        self.shards, self.w, self.grads, self.core = shards, w, grads, core
        self.last_slab_step = 0
        self.parts = {}
        self.ag = [None for _ in self.AG_UNITS]
        self.rs = [[None, None, None, None] for _ in self.RS_UNITS]

    def pre(self, name):
        for i, (names, host) in enumerate(self.AG_UNITS):
            if name == host:
                st = _ag_ici_stage([self.shards[n] for n in names])
                st.then = _ag_d2d_stage(st.outs)
                st.start_step = self.last_slab_step if name == "ffn1_up" else 0
                self.ag[i] = _host(name, st)
        for i, (names, h1, h2, h3) in enumerate(self.RS_UNITS):
            if name == h1:
                self.rs[i][0] = _host(name, _rs_swap_stage([_stacked(self.grads[n]) for n in names]))
            if name == h2:
                self.rs[i][2] = _host(name, _rs_scatter_stage(self.rs[i][1], relative=names[0] in _FFN1))
            if name == h3:
                self.rs[i][3] = _host(name, _rs_mirror_stage(self.rs[i][2].results))

    def post(self, name):
        for i, (names, host) in enumerate(self.AG_UNITS):
            if name == host:
                for n, f in zip(names, self.ag[i].results):
                    self.w[n] = _full_weight(n, f)
        for i, (names, h1, h2, h3) in enumerate(self.RS_UNITS):
            if name == h1:
                self.rs[i][1] = list(_pair_add("pair_add_" + names[0], [_stacked(self.grads[n]) for n in names],
                                               self.rs[i][0].results, self.core))
            if name == h3:
                for n, p in zip(names, self.rs[i][3].results):
                    self.parts[n] = p


def _full_weight(name, stacked):
    if name in ("w_out", "w_mq", "w_mo"):
        return stacked.reshape(D_MODEL, D_MODEL)
    return stacked


def kernel(x, mem, positions, ffn1_norm, ffn1_w_gate, ffn1_w_up, ffn1_w_down, mix_norm, w_in, q_norm, w_q_up, kv_norm, w_kv_up, pool_w, pool_scale, w_out, xattn_norm, mem_norm, w_mq, w_mkv, w_mo, ffn2_norm, ffn2_w_gate, ffn2_w_up, ffn2_w_down, final_norm, loss_target, m_ffn1_norm, m_ffn1_w_gate, m_ffn1_w_up, m_ffn1_w_down, m_mix_norm, m_w_in, m_q_norm, m_w_q_up, m_kv_norm, m_w_kv_up, m_pool_w, m_pool_scale, m_w_out, m_xattn_norm, m_mem_norm, m_w_mq, m_w_mkv, m_w_mo, m_ffn2_norm, m_ffn2_w_gate, m_ffn2_w_up, m_ffn2_w_down, m_final_norm, v_ffn1_norm, v_ffn1_w_gate, v_ffn1_w_up, v_ffn1_w_down, v_mix_norm, v_w_in, v_q_norm, v_w_q_up, v_kv_norm, v_w_kv_up, v_pool_w, v_pool_scale, v_w_out, v_xattn_norm, v_mem_norm, v_w_mq, v_w_mkv, v_w_mo, v_ffn2_norm, v_ffn2_w_gate, v_ffn2_w_up, v_ffn2_w_down, v_final_norm):
    wts = dict(zip(_WEIGHTS, (ffn1_norm, ffn1_w_gate, ffn1_w_up, ffn1_w_down, mix_norm, w_in, q_norm, w_q_up, kv_norm, w_kv_up, pool_w, pool_scale, w_out, xattn_norm, mem_norm, w_mq, w_mkv, w_mo, ffn2_norm, ffn2_w_gate, ffn2_w_up, ffn2_w_down, final_norm)))
    mom = dict(zip(_WEIGHTS, (m_ffn1_norm, m_ffn1_w_gate, m_ffn1_w_up, m_ffn1_w_down, m_mix_norm, m_w_in, m_q_norm, m_w_q_up, m_kv_norm, m_w_kv_up, m_pool_w, m_pool_scale, m_w_out, m_xattn_norm, m_mem_norm, m_w_mq, m_w_mkv, m_w_mo, m_ffn2_norm, m_ffn2_w_gate, m_ffn2_w_up, m_ffn2_w_down, m_final_norm)))
    var = dict(zip(_WEIGHTS, (v_ffn1_norm, v_ffn1_w_gate, v_ffn1_w_up, v_ffn1_w_down, v_mix_norm, v_w_in, v_q_norm, v_w_q_up, v_kv_norm, v_w_kv_up, v_pool_w, v_pool_scale, v_w_out, v_xattn_norm, v_mem_norm, v_w_mq, v_w_mkv, v_w_mo, v_ffn2_norm, v_ffn2_w_gate, v_ffn2_w_up, v_ffn2_w_down, v_final_norm)))
    small = [n for n in _WEIGHTS if n not in _SHARDED]

    global _PLAN
    shards = {n: _pad_shard(n, _local_view(n, wts[n])).astype(BF16) for n in _SHARDED}
    w = {n: wts[n].reshape(1, -1) for n in _SMALL_VECTORS}
    w["pool_w"] = pool_w[0].astype(BF16)
    grads = {}
    core = lax.axis_index("c").astype(jnp.int32).reshape(1)
    plan = _Plan(shards, w, grads, core)
    _PLAN = plan
    try:
        w["ffn1_shards"] = tuple(shards[n] for n in _FFN1)

        loss_local, dx = _local_step(x[0], mem[0], positions[0], loss_target[0], w, grads)

        def small_view(d):
            out = {n: d[n].reshape(1, -1) for n in _SMALL_VECTORS}
            out["pool_w"] = d["pool_w"].reshape(_POOL_ROWS, POOL_CH)
            return out

        *small_res, loss_vec = _small_params_step(small_view(grads), small_view(wts), small_view(mom),
                                                  small_view(var), loss_local)
        g_out, d_out, m_out, v_out = ({n: r[n].reshape(wts[n].shape) for n in small} for r in small_res)
        loss = loss_vec[0, 0]

        for names in _Plan.ADAMW_ORDER:
            res = _adamw_sum("adamw_" + names[0], [_local_view(n, wts[n]) for n in names],
                             [_unpad_shard(n, plan.parts[n]) for n in names],
                             [_local_view(n, mom[n]) for n in names], [_local_view(n, var[n]) for n in names])
            for n, r4 in zip(names, res):
                g_out[n], d_out[n], m_out[n], v_out[n] = (_global_view(n, r) for r in r4)
    finally:
        _PLAN = None
        _PENDING.clear()

    return (loss, dx[None], *[g_out[n] for n in _WEIGHTS], *[d_out[n] for n in _WEIGHTS],
            *[m_out[n] for n in _WEIGHTS], *[v_out[n] for n in _WEIGHTS])
```

```python
import jax
import jax.numpy as jnp
from jax import lax
from jax.experimental import pallas as pl
from jax.experimental.pallas import tpu as pltpu

F32 = jnp.float32
BF16 = jnp.bfloat16

D_MODEL = 1024
D_FF = 2816
N_CHIPS = 4
FF_SHARD = D_FF // N_CHIPS
MLA_HEADS = 4
Q_LORA = 256
KV_LORA = 128
ROPE_DIM = 64
HEAD_QK = 256
HEAD_V = 128
POOL_GROUPS = 4
POOL_CH = 128
MEM_HEADS = 4
MEM_HEAD_DIM = 256
RMS_EPS = 1e-6
ROPE_BASE = 10000.0
MLA_SCALE = (128 + 64) ** -0.5
MEM_SCALE = MEM_HEAD_DIM ** -0.5

ADAM_LR = 0.001
ADAM_B1 = 0.9
ADAM_B2 = 0.999
ADAM_EPS = 1e-08
ADAM_WD = 0.01
ADAM_STEP = 10

V7X_VMEM_LIMIT_BYTES = 56 * 1024 * 1024

NN = ((1,), (0,))
NT = ((1,), (1,))
TN = ((0,), (0,))


def _params(*sem):
    return pltpu.CompilerParams(dimension_semantics=sem, vmem_limit_bytes=V7X_VMEM_LIMIT_BYTES)


_MESH = pl.DeviceIdType.MESH
_ANY = pl.BlockSpec(memory_space=pl.ANY)


class _Stage:
    def __init__(self, ins, outs, n_remote, n_local, copies, aliases=None):
        self.ins, self.outs, self.n_remote, self.n_local = list(ins), list(outs), n_remote, n_local
        self.copies, self.aliases = copies, dict(aliases or {})
        self.results = None
        self.start_step = 0
        self.then = None

    def descriptors(self, in_refs, out_refs, send, recv, loc):
        ds, ri, li = [], 0, 0
        for src, dst, dev in self.copies(in_refs, out_refs):
            if dev is None:
                ds.append(pltpu.make_async_copy(src, dst, loc.at[li]))
                li += 1
            else:
                ds.append(pltpu.make_async_remote_copy(src_ref=src, dst_ref=dst, send_sem=send.at[ri],
                                                       recv_sem=recv.at[ri], device_id=dev, device_id_type=_MESH))
                ri += 1
        assert ri == self.n_remote and li == self.n_local
        return ds


_PENDING = {}


def _host(name, stage):
    _PENDING.setdefault(name, []).append(stage)
    return stage


_PLAN = None


def _call(body, **kw):
    if _PLAN is not None:
        _PLAN.pre(kw["name"])
    res = _call_hosting(body, **kw)
    if _PLAN is not None:
        _PLAN.post(kw["name"])
    return res


def _call_hosting(body, *, name, grid, in_specs, out_specs, out_shape, sem, args, scratch_shapes=(), aliases=None):
    stages = _PENDING.pop(name, [])
    scratch_shapes = list(scratch_shapes)
    if not stages:
        return pl.pallas_call(body, name=name, grid=grid, in_specs=in_specs, out_specs=out_specs,
                              out_shape=out_shape, scratch_shapes=scratch_shapes,
                              input_output_aliases=dict(aliases or {}), compiler_params=_params(*sem))(*args)
    ni, no, ns = len(in_specs), len(out_shape), len(scratch_shapes)
    c_ins = [a for st in stages for a in st.ins]
    c_outs = [o for st in stages for o in st.outs]
    nci, nco = len(c_ins), len(c_outs)
    aliases, io, oo = dict(aliases or {}), 0, 0
    for st in stages:
        for i, j in st.aliases.items():
            aliases[ni + io + i] = no + oo + j
        io += len(st.ins)
        oo += len(st.outs)
    dma = pltpu.SemaphoreType.DMA
    sems = []
    for st in stages:
        sems += [dma((max(st.n_remote, 1),)), dma((max(st.n_remote, 1),)), dma((max(st.n_local, 1),))]
    followers = [st.then for st in stages if st.then is not None]
    for st in followers:
        sems += [dma((max(st.n_remote, 1),)), dma((max(st.n_remote, 1),)), dma((max(st.n_local, 1),))]

    def wrapped(*refs):
        ins, cin = refs[:ni], refs[ni:ni + nci]
        outs, cout = refs[ni + nci:ni + nci + no], refs[ni + nci + no:ni + nci + no + nco]
        scr = refs[ni + nci + no + nco:ni + nci + no + nco + ns]
        sem_refs = refs[ni + nci + no + nco + ns:]
        step = pl.program_id(0)
        last = pl.program_id(0) == grid[0] - 1
        for ax in range(1, len(grid)):
            step = step * grid[ax] + pl.program_id(ax)
            last = jnp.logical_and(last, pl.program_id(ax) == grid[ax] - 1)

        def descriptors(si):
            io = sum(len(st.ins) for st in stages[:si])
            oo = sum(len(st.outs) for st in stages[:si])
            st = stages[si]
            return st.descriptors(cin[io:io + len(st.ins)], cout[oo:oo + len(st.outs)], *sem_refs[3 * si:3 * si + 3])

        def follower_descriptors(fi):
            si = [k for k, st in enumerate(stages) if st.then is not None][fi]
            oo = sum(len(st.outs) for st in stages[:si])
            bufs = cout[oo:oo + len(stages[si].outs)]
            k0 = 3 * (len(stages) + fi)
            return followers[fi].descriptors(bufs, bufs, *sem_refs[k0:k0 + 3])

        def start(si):
            @pl.when(step == stages[si].start_step)
            def _():
                for d in descriptors(si):
                    d.start()

        for si, st in enumerate(stages):
            if st.start_step == 0:
                start(si)
        body(*ins, *outs, *scr)
        for si, st in enumerate(stages):
            if st.start_step != 0:
                start(si)

        @pl.when(last)
        def _():
            for si in range(len(stages)):
                for d in descriptors(si):
                    d.wait()
            for fi in range(len(followers)):
                for d in follower_descriptors(fi):
                    d.start()
            for fi in range(len(followers)):
                for d in follower_descriptors(fi):
                    d.wait()

    res = pl.pallas_call(
        wrapped, name=name, grid=grid, in_specs=list(in_specs) + [_ANY] * nci,
        out_specs=list(out_specs) + [_ANY] * nco, out_shape=list(out_shape) + c_outs,
        scratch_shapes=scratch_shapes + sems, input_output_aliases=aliases,
        compiler_params=_params(*(("arbitrary",) * len(grid))))(*args, *c_ins)
    oo = no
    for st in stages:
        st.results = list(res[oo:oo + len(st.outs)])
        oo += len(st.outs)
    return list(res[:no])


def _dot(a, b, dims):
    return lax.dot_general(a.astype(BF16), b.astype(BF16), (dims, ((), ())), preferred_element_type=F32)


_MAX_ROW_BLOCK = 1024
_ATT_BLOCK = 512


_MAX_REDUCE_BLOCK = 2048


def _row_block(s, want=1024):
    return min(want, s, _MAX_ROW_BLOCK)


def _reduce_block(s):
    return min(s, _MAX_REDUCE_BLOCK)


def _matmul(name, grid, terms, extras, outs, epilogue, acc_shape, fill=(), summed=()):
    nt, ne, no, nf = len(terms), len(extras), len(outs), len(fill)
    nk = grid[-1]
    dims = [t[4] for t in terms]

    def body(*refs):
        a_refs, b_refs = refs[:nt], refs[nt:2 * nt]
        e_refs = refs[2 * nt:2 * nt + ne]
        o_refs = refs[2 * nt + ne + nf:2 * nt + ne + nf + no]

        def finish(acc):
            vals = epilogue(acc, *[e[...] for e in e_refs])
            for idx, (o, val) in enumerate(zip(o_refs, vals)):
                if idx in summed:
                    @pl.when(pl.program_id(0) == 0)
                    def _(o=o, val=val):
                        o[...] = val.astype(o.dtype)

                    @pl.when(pl.program_id(0) > 0)
                    def _(o=o, val=val):
                        o[...] += val.astype(o.dtype)
                else:
                    o[...] = val.astype(o.dtype)

        if nk == 1:
            part = None
            for a, b, d in zip(a_refs, b_refs, dims):
                t = _dot(a[...], b[...], d)
                part = t if part is None else part + t
            finish(part)
        else:
            acc_ref = refs[-1]
            k = pl.program_id(len(grid) - 1)

            @pl.when(k == 0)
            def _():
                acc_ref[...] = jnp.zeros_like(acc_ref)

            for a, b, d in zip(a_refs, b_refs, dims):
                acc_ref[...] += _dot(a[...], b[...], d)

            @pl.when(k == nk - 1)
            def _():
                finish(acc_ref[...])

    in_specs = [t[1] for t in terms] + [t[3] for t in terms] + [e[1] for e in extras] + [_ANY] * nf
    args = [t[0] for t in terms] + [t[2] for t in terms] + [e[0] for e in extras] + list(fill)
    sem = ("arbitrary" if summed else "parallel",) * (len(grid) - 1) + ("arbitrary",)
    aliases = {2 * nt + ne + i: i for i in range(nf)}
    return _call(
        body, name=name, grid=grid, in_specs=in_specs,
        out_specs=[o[1] for o in outs], out_shape=[o[0] for o in outs],
        scratch_shapes=[pltpu.VMEM(acc_shape, F32)] if nk > 1 else [], sem=sem, args=args, aliases=aliases)


def _ident(acc):
    return (acc,)


def _rmsnorm_fwd(name, x, gain, width, col_block=0):
    s = x.shape[0]
    bm = _row_block(s)

    def body(x_ref, g_ref, o_ref):
        xf = x_ref[...]
        r = lax.rsqrt(jnp.mean(xf * xf, axis=-1, keepdims=True) + RMS_EPS)
        o_ref[...] = ((xf * r) * g_ref[...]).astype(o_ref.dtype)

    return pl.pallas_call(
        body, name=name, grid=(s // bm,),
        in_specs=[pl.BlockSpec((bm, width), lambda i: (i, col_block)), pl.BlockSpec((1, width), lambda i: (0, 0))],
        out_specs=pl.BlockSpec((bm, width), lambda i: (i, 0)),
        out_shape=jax.ShapeDtypeStruct((s, width), BF16),
        compiler_params=_params("parallel"),
    )(x, gain)


def _rms_bwd_math(dy, xf, g, width):
    r = lax.rsqrt(jnp.mean(xf * xf, axis=-1, keepdims=True) + RMS_EPS)
    dyg = dy * g
    dot = jnp.sum(dyg * xf, axis=-1, keepdims=True)
    dx = r * dyg - xf * ((r * r * r) * (dot * (1.0 / width)))
    dgain = jnp.sum(dy * (xf * r), axis=0, keepdims=True)
    return dx, dgain


def _rmsnorm_bwd(name, dy, x, gain, width, col_block=0, dres=None, out_dtype=F32):
    s = x.shape[0]
    bm = _row_block(s)
    has_res = dres is not None

    def body(*refs):
        if has_res:
            dy_ref, x_ref, g_ref, r_ref, dx_ref, dg_ref = refs
        else:
            dy_ref, x_ref, g_ref, dx_ref, dg_ref = refs
        dx, dgain = _rms_bwd_math(dy_ref[...].astype(F32), x_ref[...], g_ref[...], width)
        if has_res:
            dx = dx + r_ref[...]
        dx_ref[...] = dx.astype(dx_ref.dtype)

        @pl.when(pl.program_id(0) == 0)
        def _():
            dg_ref[...] = dgain

        @pl.when(pl.program_id(0) > 0)
        def _():
            dg_ref[...] += dgain

    row = pl.BlockSpec((bm, width), lambda i: (i, 0))
    in_specs = [row, pl.BlockSpec((bm, width), lambda i: (i, col_block)), pl.BlockSpec((1, width), lambda i: (0, 0))]
    args = [dy, x, gain]
    out_specs = [row, pl.BlockSpec((1, width), lambda i: (0, 0))]
    out_shape = [jax.ShapeDtypeStruct((s, width), out_dtype), jax.ShapeDtypeStruct((1, width), F32)]
    if has_res:
        in_specs.append(row)
        args.append(dres)
    return _call(body, name=name, grid=(s // bm,), in_specs=in_specs, out_specs=out_specs, out_shape=out_shape,
                 sem=("arbitrary",), args=args)


def _ffn_up(name, n, wg, wu):
    s = n.shape[0]
    bm = _row_block(s)

    def body(n_ref, wg_ref, wu_ref, a_ref, dadu_ref, dadg_ref):
        x = n_ref[...]
        g = _dot(x, wg_ref[...], NT)
        u = _dot(x, wu_ref[...], NT)
        sg = jax.nn.sigmoid(g)
        silu = g * sg
        a_ref[...] = (silu * u).astype(BF16)
        dadu_ref[...] = silu.astype(BF16)
        dadg_ref[...] = (u * (sg * (1.0 + g * (1.0 - sg)))).astype(BF16)

    w_spec = pl.BlockSpec((None, FF_SHARD, D_MODEL), lambda j, i: (j, 0, 0))
    o_spec = pl.BlockSpec((None, bm, FF_SHARD), lambda j, i: (j, i, 0))
    shp = jax.ShapeDtypeStruct((N_CHIPS, s, FF_SHARD), BF16)
    return _call(
        body, name=name, grid=(N_CHIPS, s // bm),
        in_specs=[pl.BlockSpec((bm, D_MODEL), lambda j, i: (i, 0)), w_spec, w_spec],
        out_specs=[o_spec, o_spec, o_spec], out_shape=[shp, shp, shp],
        sem=("parallel", "parallel"), args=[n, wg, wu])


def _ffn1_up_gather(xin, gain, g_sh, u_sh, d_sh):
    s = xin.shape[0]
    bm = _row_block(s)
    nrb = s // bm
    rows, cols = g_sh.shape

    def body(x_ref, gain_ref, gs, us, ds, n_ref, a_ref, dadu_ref, dadg_ref, wg, wu, wd, gbuf, ubuf,
             send, recv, qsend, qrecv, fsend, frecv, loc, ld):
        r, i = pl.program_id(0), pl.program_id(1)
        x, y, c = lax.axis_index("x"), lax.axis_index("y"), lax.axis_index("c")
        sib = (x, y, 1 - c)
        mine, _ = _half_rows(c, rows)
        quarters = _quarter_rows(c, rows)
        shards, fulls, bufs = (gs, us, ds), (wg, wu, wd), (gbuf, ubuf)

        def remote(src, dst, ssem, rsem, dev):
            return pltpu.make_async_remote_copy(src_ref=src, dst_ref=dst, send_sem=ssem, recv_sem=rsem,
                                                device_id=dev, device_id_type=_MESH)

        def peer(rel):
            return ((1 - x) if rel & 2 else x, (1 - y) if rel & 1 else y, c)

        def ici(k, rel, dev=sib):
            return remote(shards[k].at[mine], fulls[k].at[rel, mine], send.at[k, rel - 1], recv.at[k, rel - 1], dev)

        def quarter(k, which, dev=sib):
            slab, q = ((2, quarters[0]), (1, quarters[1]))[which]
            return remote(fulls[k].at[slab, q], fulls[k].at[3, q], qsend.at[k, which], qrecv.at[k, which], dev)

        def fwd(k, rel):
            return remote(fulls[k].at[rel, mine], fulls[k].at[rel, mine], fsend.at[k, rel - 1], frecv.at[k, rel - 1], sib)

        def own(k):
            return pltpu.make_async_copy(shards[k], fulls[k].at[0], loc.at[k])

        def load(slab):
            for k in (0, 1):
                pltpu.make_async_copy(shards[k] if slab == 0 else fulls[k].at[slab], bufs[k], ld.at[k]).start()
            for k in (0, 1):
                pltpu.make_async_copy(shards[k] if slab == 0 else fulls[k].at[slab], bufs[k], ld.at[k]).wait()

        def from_neighbour(ks, rel):
            for k in ks:
                ici(k, rel).wait_recv()
                fwd(k, rel).start()
                quarter(k, 0 if rel == 2 else 1, peer(1 if rel == 2 else 2)).start()
            for k in ks:
                fwd(k, rel).wait_recv()

        def from_diagonal(ks):
            for k in ks:
                quarter(k, 0).wait_recv()
                quarter(k, 1).wait_recv()
                fwd(k, 3).start()
            for k in ks:
                fwd(k, 3).wait_recv()

        @pl.when(jnp.logical_and(r == 0, i == 0))
        def _():
            for k in range(3):
                own(k).start()
            for rel in (1, 2):
                for k in (0, 1):
                    ici(k, rel, peer(rel)).start()
            load(0)

        @pl.when(jnp.logical_and(r == 1, i == 0))
        def _():
            from_neighbour((0, 1), 1)
            load(1)
            for rel in (1, 2):
                ici(2, rel, peer(rel)).start()

        @pl.when(jnp.logical_and(r == 2, i == 0))
        def _():
            from_neighbour((0, 1), 2)
            load(2)

        @pl.when(jnp.logical_and(r == 3, i == 0))
        def _():
            from_diagonal((0, 1))
            load(3)

        xv = _norm_bf16(x_ref[...], gain_ref[...])

        @pl.when(r == 0)
        def _():
            n_ref[...] = xv

        g = _dot(xv, gbuf[...], NT)
        u = _dot(xv, ubuf[...], NT)
        sg = jax.nn.sigmoid(g)
        silu = g * sg
        a_ref[...] = (silu * u).astype(BF16)
        dadu_ref[...] = silu.astype(BF16)
        dadg_ref[...] = (u * (sg * (1.0 + g * (1.0 - sg)))).astype(BF16)

        @pl.when(jnp.logical_and(r == 3, i == nrb - 1))
        def _():
            from_neighbour((2,), 1)
            from_neighbour((2,), 2)
            from_diagonal((2,))
            for k in range(3):
                for rel in (1, 2):
                    ici(k, rel).wait_send()
                for which in (0, 1):
                    quarter(k, which).wait_send()
                for rel in (1, 2, 3):
                    fwd(k, rel).wait_send()
                own(k).wait()

    o_spec = pl.BlockSpec((None, bm, FF_SHARD), lambda r, i: (r, i, 0))
    act = jax.ShapeDtypeStruct((N_CHIPS, s, FF_SHARD), BF16)
    full = jax.ShapeDtypeStruct((N_CHIPS, rows, cols), BF16)
    dma = pltpu.SemaphoreType.DMA
    if _PLAN is not None:
        _PLAN.last_slab_step = 3 * nrb
    n_spec = pl.BlockSpec((bm, D_MODEL), lambda r, i: (jnp.where(r == 0, i, nrb - 1), 0))
    return _call(
        body, name="ffn1_up", grid=(N_CHIPS, nrb),
        in_specs=[pl.BlockSpec((bm, D_MODEL), lambda r, i: (i, 0)), pl.BlockSpec((1, D_MODEL), lambda r, i: (0, 0)),
                  _ANY, _ANY, _ANY],
        out_specs=[n_spec, o_spec, o_spec, o_spec, _ANY, _ANY, _ANY],
        out_shape=[jax.ShapeDtypeStruct((s, D_MODEL), BF16), act, act, act, full, full, full],
        scratch_shapes=[pltpu.VMEM((rows, cols), BF16), pltpu.VMEM((rows, cols), BF16), dma((3, 2)), dma((3, 2)),
                        dma((3, 2)), dma((3, 2)), dma((3, 3)), dma((3, 3)), dma((3,)), dma((2,))],
        sem=("arbitrary", "arbitrary"), args=[xin, gain, g_sh, u_sh, d_sh])


def _residual_epilogue(alpha, with_norm):
    if not with_norm:
        return lambda acc, r: (r + alpha * acc,)

    def epilogue(acc, r, g):
        h = r + alpha * acc
        rs = lax.rsqrt(jnp.mean(h * h, axis=-1, keepdims=True) + RMS_EPS)
        return h, (h * rs) * g

    return epilogue


def _residual_outs(s, bm, gain):
    row = pl.BlockSpec((bm, D_MODEL), lambda i, k: (i, 0))
    outs = [(jax.ShapeDtypeStruct((s, D_MODEL), F32), row)]
    if gain is None:
        return [], outs
    return [(gain, pl.BlockSpec((1, D_MODEL), lambda i, k: (0, 0)))], outs + [(jax.ShapeDtypeStruct((s, D_MODEL), BF16), row)]


def _loss_epilogue(acc, res, g, target):
    d = acc.shape[-1]
    h = res + 0.5 * acc
    r = lax.rsqrt(jnp.mean(h * h, axis=-1, keepdims=True) + RMS_EPS)
    err = (h * r) * g - target
    part = 0.5 * jnp.sum(jnp.mean(err * err, axis=-1, keepdims=True), axis=0, keepdims=True)
    dx, dgain = _rms_bwd_math(err * (1.0 / d), h, g, d)
    return dx, dx, jnp.broadcast_to(part, (1, 128)), dgain


def _ffn_down(name, a, wd, res, gain=None, loss=None):
    s = a.shape[1]
    bm = _row_block(s, 512)
    row = pl.BlockSpec((bm, D_MODEL), lambda i, k: (i, 0))
    terms = [(a, pl.BlockSpec((None, bm, FF_SHARD), lambda i, k, j=j: (j, i, 0)),
              wd, pl.BlockSpec((None, FF_SHARD, D_MODEL), lambda i, k, j=j: (j, 0, 0)), NN) for j in range(N_CHIPS)]
    if loss is not None:
        vec = pl.BlockSpec((1, D_MODEL), lambda i, k: (0, 0))
        outs = [(jax.ShapeDtypeStruct((s, D_MODEL), F32), row), (jax.ShapeDtypeStruct((s, D_MODEL), BF16), row),
                (jax.ShapeDtypeStruct((1, 128), F32), pl.BlockSpec((1, 128), lambda i, k: (0, 0))),
                (jax.ShapeDtypeStruct((1, D_MODEL), F32), vec)]
        return _matmul(name, (s // bm, 1), terms, [(res, row), (loss[0], vec), (loss[1], row)], outs,
                       _loss_epilogue, None, summed=(2, 3))
    extras, outs = _residual_outs(s, bm, gain)
    res_out = _matmul(name, (s // bm, 1), terms, [(res, row)] + extras, outs,
                      _residual_epilogue(0.5, gain is not None), None)
    return res_out if gain is not None else res_out[0]


def _norm_bwd_epilogue(width):
    def epilogue(acc, h, g, dres):
        dx, dgain = _rms_bwd_math(acc, h, g, width)
        dx = dx + dres
        return dx, dx, dgain

    return epilogue


def _norm_bwd_operands(s, bm, h, gain, dres):
    row = pl.BlockSpec((bm, D_MODEL), lambda i, k: (i, 0))
    vec = pl.BlockSpec((1, D_MODEL), lambda i, k: (0, 0))
    extras = [(h, row), (gain, vec), (dres, row)]
    outs = [(jax.ShapeDtypeStruct((s, D_MODEL), F32), row), (jax.ShapeDtypeStruct((s, D_MODEL), BF16), row),
            (jax.ShapeDtypeStruct((1, D_MODEL), F32), vec)]
    return extras, outs, (2,)


def _ffn_bwd(tag, dh, n, dadg, dadu, a, wg, wu, wd, grads, norm_bwd=None, many_calls=True):
    s = dh.shape[0]
    bm = _row_block(s)
    bk = _reduce_block(s)
    nk = s // bk

    def act_bwd(acc, dg_da, du_da):
        da = 0.5 * acc
        return da * dg_da.astype(F32), da * du_da.astype(F32)

    slab = pl.BlockSpec((None, bm, FF_SHARD), lambda j, i, k: (j, i, 0))
    shp = jax.ShapeDtypeStruct((N_CHIPS, s, FF_SHARD), BF16)
    dg, du = _matmul(
        tag + "_dact", (N_CHIPS, s // bm, 1),
        [(dh, pl.BlockSpec((bm, D_MODEL), lambda j, i, k: (i, 0)),
          wd, pl.BlockSpec((None, FF_SHARD, D_MODEL), lambda j, i, k: (j, 0, 0)), NT)],
        [(dadg, slab), (dadu, slab)], [(shp, slab), (shp, slab)], act_bwd, None)

    grads[tag + "_w_down"] = _matmul(
        tag + "_dwd", (N_CHIPS, nk),
        [(a, pl.BlockSpec((None, bk, FF_SHARD), lambda j, k: (j, k, 0)),
          dh, pl.BlockSpec((bk, D_MODEL), lambda j, k: (k, 0)), TN)],
        [], [(jax.ShapeDtypeStruct((N_CHIPS, FF_SHARD, D_MODEL), BF16),
              pl.BlockSpec((None, FF_SHARD, D_MODEL), lambda j, k: (j, 0, 0)))],
        lambda acc: (0.5 * acc,), (FF_SHARD, D_MODEL))[0]

    def dw_up(nm, dact):
        return _matmul(
            nm, (N_CHIPS, nk),
            [(dact, pl.BlockSpec((None, bk, FF_SHARD), lambda j, k: (j, k, 0)),
              n, pl.BlockSpec((bk, D_MODEL), lambda j, k: (k, 0)), TN)],
            [], [(jax.ShapeDtypeStruct((N_CHIPS, FF_SHARD, D_MODEL), BF16),
                  pl.BlockSpec((None, FF_SHARD, D_MODEL), lambda j, k: (j, 0, 0)))],
            _ident, (FF_SHARD, D_MODEL))[0]

    def dw_up_pair(nm):
        def body(dg_ref, du_ref, n_ref, og_ref, ou_ref, acc_g, acc_u):
            k = pl.program_id(1)

            @pl.when(k == 0)
            def _():
                acc_g[...] = jnp.zeros_like(acc_g)
                acc_u[...] = jnp.zeros_like(acc_u)

            nv = n_ref[...]
            acc_g[...] += _dot(dg_ref[...], nv, TN)
            acc_u[...] += _dot(du_ref[...], nv, TN)

            @pl.when(k == nk - 1)
            def _():
                og_ref[...] = acc_g[...].astype(BF16)
                ou_ref[...] = acc_u[...].astype(BF16)

        act = pl.BlockSpec((None, bk, FF_SHARD), lambda j, k: (j, k, 0))
        out = pl.BlockSpec((None, FF_SHARD, D_MODEL), lambda j, k: (j, 0, 0))
        shape = jax.ShapeDtypeStruct((N_CHIPS, FF_SHARD, D_MODEL), BF16)
        return _call(body, name=nm, grid=(N_CHIPS, nk),
                     in_specs=[act, act, pl.BlockSpec((bk, D_MODEL), lambda j, k: (k, 0))], out_specs=[out, out],
                     out_shape=[shape, shape],
                     scratch_shapes=[pltpu.VMEM((FF_SHARD, D_MODEL), F32), pltpu.VMEM((FF_SHARD, D_MODEL), F32)],
                     sem=("parallel", "arbitrary"), args=[dg, du, n])

    if many_calls:
        grads[tag + "_w_gate"] = dw_up(tag + "_dwg", dg)
        grads[tag + "_w_up"] = dw_up(tag + "_dwu", du)
    else:
        grads[tag + "_w_gate"], grads[tag + "_w_up"] = dw_up_pair(tag + "_dwgu")

    bn = _row_block(s, 512)
    steps = s // bn // 2 if many_calls else s // bn
    prev, dgain = (), None
    for part, off in ((("_dn_a", 0), ("_dn_b", steps)) if many_calls else (("_dn", 0),)):
        row = pl.BlockSpec((bn, D_MODEL), lambda i, k, off=off: (i + off, 0))
        terms = []
        for j in range(N_CHIPS):
            a_slab = pl.BlockSpec((None, bn, FF_SHARD), lambda i, k, j=j, off=off: (j, i + off, 0))
            w_slab = pl.BlockSpec((None, FF_SHARD, D_MODEL), lambda i, k, j=j: (j, 0, 0))
            terms += [(dg, a_slab, wg, w_slab, NN), (du, a_slab, wu, w_slab, NN)]
        if norm_bwd is None:
            prev = _matmul(tag + part, (steps, 1), terms, [], [(jax.ShapeDtypeStruct((s, D_MODEL), F32), row)],
                           _ident, None, fill=prev)
            continue
        h, gain, dres = norm_bwd
        vec = pl.BlockSpec((1, D_MODEL), lambda i, k: (0, 0))
        res = _matmul(
            tag + part, (steps, 1), terms, [(h, row), (gain, vec), (dres, row)],
            [(jax.ShapeDtypeStruct((s, D_MODEL), F32), row), (jax.ShapeDtypeStruct((s, D_MODEL), BF16), row),
             (jax.ShapeDtypeStruct((1, D_MODEL), F32), vec)],
            _norm_bwd_epilogue(D_MODEL), None, fill=prev, summed=(2,))
        prev = res[:2]
        dgain = res[2] if dgain is None else dgain + res[2]
    return prev[0] if norm_bwd is None else (prev[0], prev[1], dgain)


def _mm_nn(name, a, b, out_dtype, res=None, gain=None):
    s, k = a.shape
    nn = b.shape[1]
    bm = _row_block(s)
    row = pl.BlockSpec((bm, nn), lambda i, kk: (i, 0))
    term = [(a, pl.BlockSpec((bm, k), lambda i, kk: (i, 0)), b, pl.BlockSpec((k, nn), lambda i, kk: (0, 0)), NN)]
    if res is None:
        return _matmul(name, (s // bm, 1), term, [], [(jax.ShapeDtypeStruct((s, nn), out_dtype), row)], _ident, None)[0]
    extras, outs = _residual_outs(s, bm, gain)
    res_out = _matmul(name, (s // bm, 1), term, [(res, row)] + extras, outs,
                      _residual_epilogue(1.0, gain is not None), None)
    return res_out if gain is not None else res_out[0]


def _mm_nt(name, a, b, out_dtype, attn_out=None, nh=0, dv=0):
    s, nn = a.shape
    k = b.shape[0]
    bm = _row_block(s)
    term = [(a, pl.BlockSpec((bm, nn), lambda i, kk: (i, 0)), b, pl.BlockSpec((k, nn), lambda i, kk: (0, 0)), NT)]
    out = (jax.ShapeDtypeStruct((s, k), out_dtype), pl.BlockSpec((bm, k), lambda i, kk: (i, 0)))
    if attn_out is None:
        return _matmul(name, (s // bm, 1), term, [], [out], _ident, None)[0]

    def with_delta(acc, o):
        do = acc.astype(out_dtype).astype(F32)
        cols = [jnp.sum(do[:, h * dv:(h + 1) * dv] * o[:, h * dv:(h + 1) * dv].astype(F32), axis=-1, keepdims=True)
                for h in range(nh)]
        return acc, jnp.stack(cols, axis=0)

    return _matmul(
        name, (s // bm, 1), term, [(attn_out, pl.BlockSpec((bm, nh * dv), lambda i, kk: (i, 0)))],
        [out, (jax.ShapeDtypeStruct((nh, s, 1), F32), pl.BlockSpec((nh, bm, 1), lambda i, kk: (0, i, 0)))],
        with_delta, None)


def _mm_nt_norm_bwd(name, a, b, h, gain, dres):
    s, nn = a.shape
    bm = _row_block(s, 512)
    extras, outs, summed = _norm_bwd_operands(s, bm, h, gain, dres)
    return _matmul(
        name, (s // bm, 1),
        [(a, pl.BlockSpec((bm, nn), lambda i, kk: (i, 0)), b, pl.BlockSpec(b.shape, lambda i, kk: (0, 0)), NT)],
        extras, outs, _norm_bwd_epilogue(D_MODEL), None, summed=summed)


def _w_in_dx_norm_bwd(dz, w_t, h, gain, dres):
    s = dz.shape[0]
    bm = _row_block(s, 512)
    epilogue = _norm_bwd_epilogue(D_MODEL)

    def body(dz_ref, w_ref, h_ref, g_ref, r_ref, dx_ref, dxb_ref, dg_ref):
        dzv = dz_ref[...]
        dn = jnp.concatenate([_dot(dzv, w_ref[j], NN) for j in range(N_CHIPS)], axis=1)
        dx, _, dgain = epilogue(dn, h_ref[...], g_ref[...], r_ref[...])
        dx_ref[...] = dx
        dxb_ref[...] = dx.astype(BF16)

        @pl.when(pl.program_id(0) == 0)
        def _():
            dg_ref[...] = dgain

        @pl.when(pl.program_id(0) > 0)
        def _():
            dg_ref[...] += dgain

    row = pl.BlockSpec((bm, D_MODEL), lambda i: (i, 0))
    vec = pl.BlockSpec((1, D_MODEL), lambda i: (0, 0))
    return _call(
        body, name="w_in_dx", grid=(s // bm,),
        in_specs=[row, pl.BlockSpec(w_t.shape, lambda i: (0, 0, 0)), row, vec, row],
        out_specs=[row, row, vec],
        out_shape=[jax.ShapeDtypeStruct((s, D_MODEL), F32), jax.ShapeDtypeStruct((s, D_MODEL), BF16),
                   jax.ShapeDtypeStruct((1, D_MODEL), F32)],
        sem=("arbitrary",), args=[dz, w_t, h, gain, dres])


def _mm_tn_stacked(name, a_list, b):
    s, nn = b.shape
    widths = [a.shape[1] for a in a_list]
    total = sum(widths)
    bk = _reduce_block(s)
    nk = s // bk
    na = len(a_list)

    def body(*refs):
        a_refs, b_ref, o_ref, acc_ref = refs[:na], refs[na], refs[na + 1], refs[na + 2]
        k = pl.program_id(0)

        @pl.when(k == 0)
        def _():
            acc_ref[...] = jnp.zeros_like(acc_ref)

        bv = b_ref[...]
        lo = 0
        for a_ref, w in zip(a_refs, widths):
            acc_ref[lo:lo + w, :] += _dot(a_ref[...], bv, TN)
            lo += w

        @pl.when(k == nk - 1)
        def _():
            o_ref[...] = acc_ref[...].astype(o_ref.dtype)

    return _call(
        body, name=name, grid=(nk,),
        in_specs=[pl.BlockSpec((bk, w), lambda k: (k, 0)) for w in widths] + [pl.BlockSpec((bk, nn), lambda k: (k, 0))],
        out_specs=[pl.BlockSpec((total, nn), lambda k: (0, 0))],
        out_shape=[jax.ShapeDtypeStruct((total, nn), BF16)],
        scratch_shapes=[pltpu.VMEM((total, nn), F32)], sem=("arbitrary",), args=list(a_list) + [b])[0]


def _mm_tn(name, a, b, out_dtype=BF16):
    s, k = a.shape
    nn = b.shape[1]
    bk = _reduce_block(s)
    return _matmul(
        name, (s // bk,),
        [(a, pl.BlockSpec((bk, k), lambda kk: (kk, 0)), b, pl.BlockSpec((bk, nn), lambda kk: (kk, 0)), TN)],
        [], [(jax.ShapeDtypeStruct((k, nn), out_dtype), pl.BlockSpec((k, nn), lambda kk: (0, 0)))],
        _ident, (k, nn))[0]


def _mm_heads_fwd(name, a, w, out_dtype, w_transposed=False):
    s, k = a.shape
    nh = w.shape[0]
    nn = w.shape[1] if w_transposed else w.shape[2]
    bm = _row_block(s)
    return _matmul(
        name, (nh, s // bm, 1),
        [(a, pl.BlockSpec((bm, k), lambda h, i, kk: (i, 0)),
          w, pl.BlockSpec((None,) + w.shape[1:], lambda h, i, kk: (h, 0, 0)), NT if w_transposed else NN)],
        [], [(jax.ShapeDtypeStruct((s, nh * nn), out_dtype), pl.BlockSpec((bm, nn), lambda h, i, kk: (i, h)))],
        _ident, None)[0]


def _mm_heads_bwd(name, dy, a, w, w_transposed=False):
    s, k = a.shape
    nh = w.shape[0]
    nn = w.shape[1] if w_transposed else w.shape[2]
    bm = _row_block(s)
    bk = _reduce_block(s)
    w_spec = pl.BlockSpec((None,) + w.shape[1:], lambda i, h: (h, 0, 0))
    da = _matmul(
        name + "_dx", (s // bm, nh),
        [(dy, pl.BlockSpec((bm, nn), lambda i, h: (i, h)), w, w_spec, NN if w_transposed else NT)],
        [], [(jax.ShapeDtypeStruct((s, k), F32), pl.BlockSpec((bm, k), lambda i, h: (i, 0)))], _ident, (bm, k))[0]
    a_term = (a, pl.BlockSpec((bk, k), lambda h, kk: (kk, 0)))
    dy_term = (dy, pl.BlockSpec((bk, nn), lambda h, kk: (kk, h)))
    lhs, rhs = (dy_term, a_term) if w_transposed else (a_term, dy_term)
    dw = _matmul(
        name + "_dw", (nh, s // bk), [lhs + rhs + (TN,)],
        [], [(jax.ShapeDtypeStruct(w.shape, BF16), pl.BlockSpec((None,) + w.shape[1:], lambda h, kk: (h, 0, 0)))],
        _ident, w.shape[1:])[0]
    return da, dw


def _w_in_fwd(n, w_t):
    s = n.shape[0]
    bm = _row_block(s)
    nh, nout, kin = w_t.shape
    terms = [(n, pl.BlockSpec((bm, kin), lambda i, k, j=j: (i, j)),
              w_t, pl.BlockSpec((None, nout, kin), lambda i, k, j=j: (j, 0, 0)), NT) for j in range(nh)]
    row = pl.BlockSpec((bm, nout), lambda i, k: (i, 0))
    return _matmul("w_in", (s // bm, 1), terms, [], [(jax.ShapeDtypeStruct((s, nout), F32), row)], _ident, None)[0]


def _w_in_dw(dz, n):
    s, nout = dz.shape
    kin = n.shape[1] // N_CHIPS
    bk = _reduce_block(s)
    return _matmul(
        "w_in_dw", (N_CHIPS, s // bk),
        [(dz, pl.BlockSpec((bk, nout), lambda j, k: (k, 0)), n, pl.BlockSpec((bk, kin), lambda j, k: (k, j)), TN)],
        [], [(jax.ShapeDtypeStruct((N_CHIPS, nout, kin), BF16), pl.BlockSpec((None, nout, kin), lambda j, k: (j, 0, 0)))],
        _ident, (nout, kin))[0]


def _rope_tables(positions):
    half = ROPE_DIM // 2
    freqs = 1.0 / (ROPE_BASE ** (jnp.arange(0, ROPE_DIM, 2, dtype=F32) / ROPE_DIM))
    ang = positions.astype(F32)[:, None] * freqs
    cos, sin = jnp.cos(ang), jnp.sin(ang)
    z = jnp.zeros_like(cos)
    tc = jnp.concatenate([cos, cos, z, z], axis=-1)
    ta = jnp.concatenate([-sin, z, z, z], axis=-1)
    tb = jnp.concatenate([z, sin, z, z], axis=-1)
    assert tc.shape[-1] == 4 * half
    return tc, ta, tb


def _rope(x, tc, ta, tb):
    return x * tc + pltpu.roll(x, 96, 1) * ta + pltpu.roll(x, 32, 1) * tb


def _rope_t(dy, tc, ta, tb):
    return dy * tc + pltpu.roll(dy * ta, 32, 1) + pltpu.roll(dy * tb, 96, 1)


def _norm_bf16(x, g):
    r = lax.rsqrt(jnp.mean(x * x, axis=-1, keepdims=True) + RMS_EPS)
    return ((x * r) * g).astype(BF16)


def _qkv_prep(z, q_gain, kv_gain, wq_t, wkv, tc, ta, tb):
    s = z.shape[0]
    bm = _row_block(s, 512)

    def body(zq_ref, zkv_ref, zkr_ref, qg_ref, kvg_ref, wq_ref, wkv_ref, tc_ref, ta_ref, tb_ref,
             qn_ref, kvn_ref, q_ref, k_ref, v_ref):
        c, a, b = tc_ref[...], ta_ref[...], tb_ref[...]
        qn = _norm_bf16(zq_ref[...], qg_ref[...])
        kvn = _norm_bf16(zkv_ref[...], kvg_ref[...])
        qn_ref[...] = qn
        kvn_ref[...] = kvn
        kpe = _rope(zkr_ref[...], c, a, b).astype(BF16)
        for h in range(MLA_HEADS):
            lo = h * HEAD_QK
            qp = _dot(qn, wq_ref[h], NT)
            q_ref[:, lo:lo + 128] = qp[:, :128].astype(BF16)
            q_ref[:, lo + 128:lo + 256] = _rope(qp[:, 128:], c, a, b).astype(BF16)
            kv = _dot(kvn, wkv_ref[h], NN)
            k_ref[:, lo:lo + 128] = kv[:, :128].astype(BF16)
            k_ref[:, lo + 128:lo + 256] = kpe
            v_ref[:, h * HEAD_V:(h + 1) * HEAD_V] = kv[:, 128:].astype(BF16)

    def cols(width, blk):
        return pl.BlockSpec((bm, width), lambda i: (i, blk))

    def whole(a):
        return pl.BlockSpec(a.shape, lambda i: (0,) * a.ndim)

    tab = cols(128, 0)
    return _call(
        body, name="qkv_prep", grid=(s // bm,),
        in_specs=[cols(Q_LORA, 0), cols(KV_LORA, 2), cols(128, 3), whole(q_gain), whole(kv_gain), whole(wq_t),
                  whole(wkv), tab, tab, tab],
        out_specs=[cols(Q_LORA, 0), cols(KV_LORA, 0), cols(1024, 0), cols(1024, 0), cols(512, 0)],
        out_shape=[jax.ShapeDtypeStruct((s, Q_LORA), BF16), jax.ShapeDtypeStruct((s, KV_LORA), BF16),
                   jax.ShapeDtypeStruct((s, 1024), BF16), jax.ShapeDtypeStruct((s, 1024), BF16),
                   jax.ShapeDtypeStruct((s, 512), BF16)],
        sem=("parallel",), args=[z, z, z, q_gain, kv_gain, wq_t, wkv, tc, ta, tb])


def _qkv_prep_bwd(dq, dk, dv, z, qn, kvn, q_gain, kv_gain, wq_t, wkv, tc, ta, tb):
    s = z.shape[0]
    bm = _row_block(s, 512)
    nsteps = s // bm

    def body(dq_ref, dk_ref, dv_ref, zq_ref, zkv_ref, qn_ref, kvn_ref, qg_ref, kvg_ref, wq_ref, wkv_ref,
             tc_ref, ta_ref, tb_ref, dz_ref, dqg_ref, dkvg_ref, dwq_ref, dwkv_ref, wq_acc, wkv_acc):
        i = pl.program_id(0)
        c, a, b = tc_ref[...], ta_ref[...], tb_ref[...]

        @pl.when(i == 0)
        def _():
            wq_acc[...] = jnp.zeros_like(wq_acc)
            wkv_acc[...] = jnp.zeros_like(wkv_acc)

        qn, kvn = qn_ref[...], kvn_ref[...]
        dqn = jnp.zeros((bm, Q_LORA), F32)
        dkvn = jnp.zeros((bm, KV_LORA), F32)
        dpe = jnp.zeros((bm, 128), F32)
        for h in range(MLA_HEADS):
            lo = h * HEAD_QK
            dqp = jnp.concatenate([dq_ref[:, lo:lo + 128],
                                   _rope_t(dq_ref[:, lo + 128:lo + 256].astype(F32), c, a, b).astype(BF16)], axis=1)
            dqn = dqn + _dot(dqp, wq_ref[h], NN)
            wq_acc[h] += _dot(dqp, qn, TN)
            dkv = jnp.concatenate([dk_ref[:, lo:lo + 128], dv_ref[:, h * HEAD_V:(h + 1) * HEAD_V]], axis=1)
            dkvn = dkvn + _dot(dkv, wkv_ref[h], NT)
            wkv_acc[h] += _dot(kvn, dkv, TN)
            dpe = dpe + dk_ref[:, lo + 128:lo + 256].astype(F32)
        dcq, dqg = _rms_bwd_math(dqn, zq_ref[...], qg_ref[...], Q_LORA)
        dckv, dkvg = _rms_bwd_math(dkvn, zkv_ref[...], kvg_ref[...], KV_LORA)
        dz_ref[:, 0:Q_LORA] = dcq.astype(BF16)
        dz_ref[:, Q_LORA:Q_LORA + KV_LORA] = dckv.astype(BF16)
        dz_ref[:, Q_LORA + KV_LORA:512] = _rope_t(dpe, c, a, b).astype(BF16)

        @pl.when(i == 0)
        def _():
            dqg_ref[...] = dqg
            dkvg_ref[...] = dkvg

        @pl.when(i > 0)
        def _():
            dqg_ref[...] += dqg
            dkvg_ref[...] += dkvg

        @pl.when(i == nsteps - 1)
        def _():
            dwq_ref[...] = wq_acc[...].astype(BF16)
            dwkv_ref[...] = wkv_acc[...].astype(BF16)

    def cols(width, blk):
        return pl.BlockSpec((bm, width), lambda i: (i, blk))

    def whole(shape):
        return pl.BlockSpec(shape, lambda i: (0,) * len(shape))

    tab = cols(128, 0)
    return _call(
        body, name="qkv_prep_bwd", grid=(nsteps,),
        in_specs=[cols(1024, 0), cols(1024, 0), cols(512, 0), cols(Q_LORA, 0), cols(KV_LORA, 2), cols(Q_LORA, 0),
                  cols(KV_LORA, 0), whole(q_gain.shape), whole(kv_gain.shape), whole(wq_t.shape), whole(wkv.shape),
                  tab, tab, tab],
        out_specs=[cols(512, 0), whole(q_gain.shape), whole(kv_gain.shape), whole(wq_t.shape), whole(wkv.shape)],
        out_shape=[jax.ShapeDtypeStruct((s, 512), BF16), jax.ShapeDtypeStruct(q_gain.shape, F32),
                   jax.ShapeDtypeStruct(kv_gain.shape, F32), jax.ShapeDtypeStruct(wq_t.shape, BF16),
                   jax.ShapeDtypeStruct(wkv.shape, BF16)],
        scratch_shapes=[pltpu.VMEM(wq_t.shape, F32), pltpu.VMEM(wkv.shape, F32)],
        sem=("arbitrary",), args=[dq, dk, dv, z, z, qn, kvn, q_gain, kv_gain, wq_t, wkv, tc, ta, tb])


def _causal_mask(s, row0, col0):
    rows = row0 + lax.broadcasted_iota(jnp.int32, s.shape, 0)
    cols = col0 + lax.broadcasted_iota(jnp.int32, s.shape, 1)
    return jnp.where(cols <= rows, s, -jnp.inf)


def _attn_fwd(name, q, k, k_off, v, v_off, nh, dq, dv, scale, causal, blk):
    sq, sk = q.shape[0], k.shape[0]
    bq = min(blk, sq)
    bk = min(blk, sk)
    nkv = sk // bk
    assert not causal or (sq == sk and bq == bk)

    hq = bq
    log2e = 1.4426950408889634
    c2 = scale * log2e

    def body(q_ref, k_ref, v_ref, o_ref, lse_ref):
        qi = pl.program_id(1)
        qs = (q_ref[...],)

        def step(j, carry, masked):
            rows = pl.ds(pl.multiple_of(j * bk, bk), bk)
            kb, vb = k_ref[rows, :], v_ref[rows, :]
            out = []
            for t, (m, l, acc) in enumerate(carry):
                s = _dot(qs[t], kb, NT) * c2
                if masked:
                    s = _causal_mask(s, qi * bq + t * hq, j * bk)
                m_new = jnp.maximum(m, jnp.max(s, axis=-1, keepdims=True))
                alpha = jnp.exp2(m - m_new)
                p = jnp.exp2(s - m_new)
                l = alpha * l + jnp.sum(p, axis=-1, keepdims=True)
                acc = alpha * acc + _dot(p, vb, NN)
                out.append((m_new, l, acc))
            return tuple(out)

        one = (jnp.full((hq, 1), -jnp.inf, F32), jnp.zeros((hq, 1), F32), jnp.zeros((hq, dv), F32))
        init = (one,)
        if causal:
            carry = lax.fori_loop(0, qi, lambda j, c: step(j, c, False), init)
            fin = step(qi, carry, True)
        else:
            fin = lax.fori_loop(0, nkv, lambda j, c: step(j, c, False), init)
        for t, (m, l, acc) in enumerate(fin):
            o_ref[t * hq:(t + 1) * hq, :] = (acc / l).astype(o_ref.dtype)
            lse_ref[t * hq:(t + 1) * hq, :] = m * (1.0 / log2e) + jnp.log(l)

    return _call(
        body, name=name, grid=(nh, sq // bq),
        in_specs=[pl.BlockSpec((bq, dq), lambda h, i: (i, h)),
                  pl.BlockSpec((sk, dq), lambda h, i: (0, k_off + h)),
                  pl.BlockSpec((sk, dv), lambda h, i: (0, v_off + h))],
        out_specs=[pl.BlockSpec((bq, dv), lambda h, i: (i, h)), pl.BlockSpec((None, bq, 1), lambda h, i: (h, i, 0))],
        out_shape=[jax.ShapeDtypeStruct((sq, nh * dv), BF16), jax.ShapeDtypeStruct((nh, sq, 1), F32)],
        sem=("parallel", "parallel"), args=[q, k, v])


def _attn_bwd(name, q, k, k_off, v, v_off, do, do_off, lse, delta, nh, dq, dv, scale, causal, blk):
    sq, sk = q.shape[0], k.shape[0]
    bq = min(blk, sq)
    bk = min(blk, sk)
    nq = sq // bq
    assert not causal or (sq == sk and bq == bk)

    nkv = sk // bk

    def body(q_ref, k_ref, v_ref, do_ref, lse_ref, dl_ref, dq_ref, dk_ref, dv_ref, dq_acc, dk_acc, dv_acc):
        j = pl.program_id(1)

        @pl.when(j == 0)
        def _():
            dq_acc[...] = jnp.zeros_like(dq_acc)

        dk_acc[...] = jnp.zeros_like(dk_acc)
        dv_acc[...] = jnp.zeros_like(dv_acc)
        kv = k_ref[...]
        vv = v_ref[...]

        def step(i, masked):
            rows = pl.ds(pl.multiple_of(i * bq, bq), bq)
            qv = q_ref[rows, :]
            dov = do_ref[rows, :].astype(BF16)
            s = _dot(qv, kv, NT) * scale
            if masked:
                s = _causal_mask(s, i * bq, j * bk)
            p = jnp.exp(s - lse_ref[rows, :])
            dp = _dot(dov, vv, NT)
            ds = (p * (dp - dl_ref[rows, :]) * scale).astype(BF16)
            dv_acc[...] += _dot(p, dov, TN)
            dk_acc[...] += _dot(ds, qv, TN)
            dq_acc[rows, :] += _dot(ds, kv, NN)

        if causal:
            step(j, True)

            def loop(i, c):
                step(i, False)
                return c

            lax.fori_loop(j + 1, nq, loop, 0)
        else:
            def loop(i, c):
                step(i, False)
                return c

            lax.fori_loop(0, nq, loop, 0)
        dk_ref[...] = dk_acc[...].astype(dk_ref.dtype)
        dv_ref[...] = dv_acc[...].astype(dv_ref.dtype)

        @pl.when(j == nkv - 1)
        def _():
            dq_ref[...] = dq_acc[...].astype(dq_ref.dtype)

    stat = pl.BlockSpec((None, sq, 1), lambda h, j: (h, 0, 0))
    return _call(
        body, name=name, grid=(nh, sk // bk),
        in_specs=[pl.BlockSpec((sq, dq), lambda h, j: (0, h)),
                  pl.BlockSpec((bk, dq), lambda h, j: (j, k_off + h)),
                  pl.BlockSpec((bk, dv), lambda h, j: (j, v_off + h)),
                  pl.BlockSpec((sq, dv), lambda h, j: (0, do_off + h)), stat, stat],
        out_specs=[pl.BlockSpec((sq, dq), lambda h, j: (0, h)),
                   pl.BlockSpec((bk, dq), lambda h, j: (j, h)),
                   pl.BlockSpec((bk, dv), lambda h, j: (j, h))],
        out_shape=[jax.ShapeDtypeStruct((sq, nh * dq), BF16), jax.ShapeDtypeStruct((sk, nh * dq), BF16),
                   jax.ShapeDtypeStruct((sk, nh * dv), BF16)],
        scratch_shapes=[pltpu.VMEM((sq, dq), F32), pltpu.VMEM((bk, dq), F32), pltpu.VMEM((bk, dv), F32)],
        sem=("parallel", "arbitrary"), args=[q, k, v, do, lse, delta])


def _pool_diff(z, g):
    s = z.shape[0]
    t = lax.broadcasted_iota(jnp.int32, z.shape, 0)
    acc = z
    sums = []
    for k in (1, 2, 4, 8):
        acc = acc + jnp.where(t >= k, pltpu.roll(acc, k, 0), 0.0)
        sums.append(acc)
    win = jnp.where(g == 0, sums[0], jnp.where(g == 1, sums[1], jnp.where(g == 2, sums[2], sums[3])))
    w = lax.shift_left(jnp.int32(2), g)
    count = jnp.minimum(t + 1, w).astype(F32)
    del s
    return win / count - z, count


def _pool_fwd(z, pool_w, pool_scale):
    s = z.shape[0]

    def body(z_ref, w_ref, sc_ref, o_ref):
        diff, _ = _pool_diff(z_ref[...], pl.program_id(0))
        o_ref[...] = (_dot(diff, w_ref[...], NN) * sc_ref[...]).astype(o_ref.dtype)

    return _call(
        body, name="pool_fwd", grid=(POOL_GROUPS,),
        in_specs=[pl.BlockSpec((s, POOL_CH), lambda g: (0, 4 + g)),
                  pl.BlockSpec((None, POOL_CH, POOL_CH), lambda g: (g, 0, 0)),
                  pl.BlockSpec((1, POOL_CH), lambda g: (0, g))],
        out_specs=[pl.BlockSpec((s, POOL_CH), lambda g: (0, g))],
        out_shape=[jax.ShapeDtypeStruct((s, POOL_GROUPS * POOL_CH), BF16)],
        sem=("parallel",), args=[z, pool_w, pool_scale])[0]


def _pool_bwd(dcat, z, pool_w, pool_scale):
    s = z.shape[0]

    def body(dp_ref, z_ref, w_ref, sc_ref, dz_ref, dw_ref, dsc_ref):
        g = pl.program_id(0)
        diff, count = _pool_diff(z_ref[...], g)
        dpf = dp_ref[...].astype(F32)
        u = _dot(diff, w_ref[...], NN)
        dsc_ref[...] = jnp.sum(dpf * u, axis=0, keepdims=True)
        du = (dpf * sc_ref[...]).astype(BF16)
        dw_ref[...] = _dot(diff, du, TN)
        ddiff = _dot(du, w_ref[...], NT)
        t = lax.broadcasted_iota(jnp.int32, ddiff.shape, 0)
        acc = ddiff / count
        sums = []
        for k in (1, 2, 4, 8):
            acc = acc + jnp.where(t < s - k, pltpu.roll(acc, s - k, 0), 0.0)
            sums.append(acc)
        win = jnp.where(g == 0, sums[0], jnp.where(g == 1, sums[1], jnp.where(g == 2, sums[2], sums[3])))
        dz_ref[...] = (win - ddiff).astype(dz_ref.dtype)

    return pl.pallas_call(
        body, name="pool_bwd", grid=(POOL_GROUPS,),
        in_specs=[pl.BlockSpec((s, POOL_CH), lambda g: (0, 4 + g)),
                  pl.BlockSpec((s, POOL_CH), lambda g: (0, 4 + g)),
                  pl.BlockSpec((None, POOL_CH, POOL_CH), lambda g: (g, 0, 0)),
                  pl.BlockSpec((1, POOL_CH), lambda g: (0, g))],
        out_specs=[pl.BlockSpec((s, POOL_CH), lambda g: (0, g)),
                   pl.BlockSpec((None, POOL_CH, POOL_CH), lambda g: (g, 0, 0)),
                   pl.BlockSpec((1, POOL_CH), lambda g: (0, g))],
        out_shape=[jax.ShapeDtypeStruct((s, POOL_GROUPS * POOL_CH), BF16),
                   jax.ShapeDtypeStruct((POOL_GROUPS, POOL_CH, POOL_CH), F32),
                   jax.ShapeDtypeStruct((1, POOL_GROUPS * POOL_CH), F32)],
        compiler_params=_params("parallel"),
    )(dcat, z, pool_w, pool_scale)


def _local_step(x, mem, positions, target, w, grads):
    tc, ta, tb = _rope_tables(positions)
    blk = _ATT_BLOCK

    if "ffn1_shards" in w:
        n1, a1, dadu1, dadg1, w["ffn1_w_gate"], w["ffn1_w_up"], w["ffn1_w_down"] = _ffn1_up_gather(
            x, w["ffn1_norm"], *w["ffn1_shards"])
    else:
        n1 = _rmsnorm_fwd("ffn1_norm", x, w["ffn1_norm"], D_MODEL)
        a1, dadu1, dadg1 = _ffn_up("ffn1_up", n1, w["ffn1_w_gate"], w["ffn1_w_up"])
    h1, n2 = _ffn_down("ffn1_down", a1, w["ffn1_w_down"], x, w["mix_norm"])
    z = _w_in_fwd(n2, w["w_in"])
    qn, kvn, qf, kf, vf = _qkv_prep(z, w["q_norm"], w["kv_norm"], w["w_q_up"], w["w_kv_up"], tc, ta, tb)
    att, lse = _attn_fwd("mla_fwd", qf, kf, 0, vf, 0, MLA_HEADS, HEAD_QK, HEAD_V, MLA_SCALE, True, blk)
    pool = _pool_fwd(z, w["pool_w"], w["pool_scale"])
    s = x.shape[0]
    bm = _row_block(s)
    row = pl.BlockSpec((bm, D_MODEL), lambda i, k: (i, 0))
    half = pl.BlockSpec((bm, 512), lambda i, k: (i, 0))
    h2, n3 = _matmul(
        "w_out", (s // bm, 1),
        [(att, half, w["w_out"], pl.BlockSpec((512, D_MODEL), lambda i, k: (0, 0)), NN),
         (pool, half, w["w_out"], pl.BlockSpec((512, D_MODEL), lambda i, k: (1, 0)), NN)],
        [(h1, row)] + _residual_outs(s, bm, w["xattn_norm"])[0], _residual_outs(s, bm, w["xattn_norm"])[1],
        _residual_epilogue(1.0, True), None)
    memn = _rmsnorm_fwd("mem_norm", mem, w["mem_norm"], D_MODEL)
    qm = _mm_nn("w_mq", n3, w["w_mq"], BF16)
    kvm = _mm_heads_fwd("w_mkv", memn, w["w_mkv"], BF16)
    om, lse_m = _attn_fwd("xattn_fwd", qm, kvm, 0, kvm, MEM_HEADS, MEM_HEADS, MEM_HEAD_DIM, MEM_HEAD_DIM,
                          MEM_SCALE, False, blk)
    h3, n4 = _mm_nn("w_mo", om, w["w_mo"], F32, res=h2, gain=w["ffn2_norm"])
    a2, dadu2, dadg2 = _ffn_up("ffn2_up", n4, w["ffn2_w_gate"], w["ffn2_w_up"])
    dh4, dh4b, loss_vec, d_final = _ffn_down("ffn2_down", a2, w["ffn2_w_down"], h3, loss=(w["final_norm"], target))
    grads["final_norm"] = d_final

    dh3, dh3b, grads["ffn2_norm"] = _ffn_bwd("ffn2", dh4b, n4, dadg2, dadu2, a2, w["ffn2_w_gate"], w["ffn2_w_up"],
                                             w["ffn2_w_down"], grads, norm_bwd=(h3, w["ffn2_norm"], dh4),
                                             many_calls=False)

    dom, delta_m = _mm_nt("w_mo_dx", dh3b, w["w_mo"], BF16, attn_out=om, nh=MEM_HEADS, dv=MEM_HEAD_DIM)
    grads["w_mo"] = _mm_tn("w_mo_dw", om, dh3b)
    dqm, dkm, dvm = _attn_bwd("xattn_bwd", qm, kvm, 0, kvm, MEM_HEADS, dom, 0, lse_m, delta_m, MEM_HEADS,
                              MEM_HEAD_DIM, MEM_HEAD_DIM, MEM_SCALE, False, blk)
    dkvm = jnp.concatenate([dkm, dvm], axis=1)
    dh2, dh2b, grads["xattn_norm"] = _mm_nt_norm_bwd("w_mq_dx", dqm, w["w_mq"], h2, w["xattn_norm"], dh3)
    grads["w_mq"] = _mm_tn("w_mq_dw", n3, dqm)
    dmemn, grads["w_mkv"] = _mm_heads_bwd("w_mkv", dkvm, memn, w["w_mkv"])
    _, grads["mem_norm"] = _rmsnorm_bwd("mem_norm_bwd", dmemn, mem, w["mem_norm"], D_MODEL, out_dtype=BF16)

    dcat, delta = _mm_nt("w_out_dx", dh2b, w["w_out"], BF16, attn_out=att, nh=MLA_HEADS, dv=HEAD_V)
    grads["w_out"] = _mm_tn_stacked("w_out_dw", [att, pool], dh2b)
    dzp, grads["pool_w"], grads["pool_scale"] = _pool_bwd(dcat, z, w["pool_w"], w["pool_scale"])
    dqf, dkf, dvf = _attn_bwd("mla_bwd", qf, kf, 0, vf, 0, dcat, 0, lse, delta, MLA_HEADS, HEAD_QK, HEAD_V,
                              MLA_SCALE, True, blk)
    dz_lat, grads["q_norm"], grads["kv_norm"], grads["w_q_up"], grads["w_kv_up"] = _qkv_prep_bwd(
        dqf, dkf, dvf, z, qn, kvn, w["q_norm"], w["kv_norm"], w["w_q_up"], w["w_kv_up"], tc, ta, tb)
    dz = jnp.concatenate([dz_lat, dzp], axis=1)
    grads["w_in"] = _w_in_dw(dz, n2)
    dh1, dh1b, grads["mix_norm"] = _w_in_dx_norm_bwd(dz, w["w_in"], h1, w["mix_norm"], dh2)

    dn1 = _ffn_bwd("ffn1", dh1b, n1, dadg1, dadu1, a1, w["ffn1_w_gate"], w["ffn1_w_up"], w["ffn1_w_down"], grads)
    dx, grads["ffn1_norm"] = _rmsnorm_bwd("ffn1_norm_bwd", dn1, x, w["ffn1_norm"], D_MODEL, dres=dh1)
    return loss_vec[0, 0], dx


def _mesh_pos():
    x, y, c = lax.axis_index("x"), lax.axis_index("y"), lax.axis_index("c")
    chips = [(1 - x, y), (x, 1 - y), (1 - x, 1 - y)]
    chip_ids = [2 * cx + cy for cx, cy in chips]
    return x, y, c, 2 * x + y, chips, chip_ids


def _half_rows(c, rows):
    hr = rows // 2
    return pl.ds(pl.multiple_of(c * hr, 16), hr), pl.ds(pl.multiple_of((1 - c) * hr, 16), hr)


def _ag_ici_stage(shards):
    n = len(shards)

    def copies(ins, outs):
        x, y, c, me, chips, _ = _mesh_pos()
        out = []
        for k in range(n):
            mine, _ = _half_rows(c, ins[k].shape[0])
            out.append((ins[k], outs[k].at[me], None))
            for cx, cy in chips:
                out.append((ins[k].at[mine], outs[k].at[me, mine], (cx, cy, c)))
        return out

    return _Stage(shards, [jax.ShapeDtypeStruct((N_CHIPS,) + s.shape, s.dtype) for s in shards], 3 * n, n, copies)


def _quarter_rows(c, rows):
    qr = rows // 4
    return pl.ds(pl.multiple_of(c * 2 * qr, 16), qr), pl.ds(pl.multiple_of(c * 2 * qr + qr, 16), qr)


def _ag_d2d_stage(fulls):
    n = len(fulls)

    def copies(ins, outs):
        x, y, c, me, _, chip_ids = _mesh_pos()
        out = []
        for k in range(n):
            mine, _ = _half_rows(c, ins[k].shape[1])
            for j in range(3):
                out.append((ins[k].at[chip_ids[j], mine], outs[k].at[chip_ids[j], mine], (x, y, 1 - c)))
        return out

    return _Stage(fulls, [jax.ShapeDtypeStruct(f.shape, f.dtype) for f in fulls], 3 * n, 0, copies,
                  aliases={k: k for k in range(n)})


def _rs_swap_stage(grads):
    n = len(grads)

    def copies(ins, outs):
        x, y, c, _, _, _ = _mesh_pos()
        out = []
        for k in range(n):
            _, other = _half_rows(c, ins[k].shape[1])
            out.append((ins[k].at[:, other, :], outs[k], (x, y, 1 - c)))
        return out

    return _Stage(grads, [jax.ShapeDtypeStruct((N_CHIPS, g.shape[1] // 2, g.shape[2]), g.dtype) for g in grads],
                  n, 0, copies)


_REL_OF_PEER = (2, 1, 3)


def _rs_scatter_stage(sums, relative=False):
    n = len(sums)

    def copies(ins, outs):
        x, y, c, me, chips, chip_ids = _mesh_pos()
        out = []
        for k in range(n):
            mine, _ = _half_rows(c, 2 * ins[k].shape[1])
            out.append((ins[k].at[0 if relative else me], outs[k].at[0, mine, :], None))
            for j, (cx, cy) in enumerate(chips):
                slab = _REL_OF_PEER[j] if relative else chip_ids[j]
                out.append((ins[k].at[slab], outs[k].at[1 + j, mine, :], (cx, cy, c)))
        return out

    return _Stage(sums, [jax.ShapeDtypeStruct((N_CHIPS, 2 * s.shape[1], s.shape[2]), s.dtype) for s in sums],
                  3 * n, n, copies)


def _rs_mirror_stage(parts):
    n = len(parts)

    def copies(ins, outs):
        x, y, c, _, _, _ = _mesh_pos()
        out = []
        for k in range(n):
            mine, _ = _half_rows(c, ins[k].shape[1])
            out.append((ins[k].at[:, mine, :], outs[k].at[:, mine, :], (x, y, 1 - c)))
        return out

    return _Stage(parts, [jax.ShapeDtypeStruct(p.shape, p.dtype) for p in parts], n, 0, copies,
                  aliases={k: k for k in range(n)})


def _pair_add(name, gs, r1s, core):
    n = len(gs)

    def body(c_ref, *refs):
        for k in range(n):
            g_ref, r_ref, o_ref = refs[k], refs[n + k], refs[2 * n + k]
            o_ref[...] = (g_ref[...].astype(F32) + r_ref[...].astype(F32)).astype(BF16)

    def half(g):
        return pl.BlockSpec((None, g.shape[1] // 2, g.shape[2]), lambda j, c: (j, 0, 0))

    def mine(g):
        return pl.BlockSpec((None, g.shape[1] // 2, g.shape[2]), lambda j, c: (j, c[0], 0))

    return pl.pallas_call(
        body, name=name,
        grid_spec=pltpu.PrefetchScalarGridSpec(
            num_scalar_prefetch=1, grid=(N_CHIPS,),
            in_specs=[mine(g) for g in gs] + [half(g) for g in gs], out_specs=[half(g) for g in gs]),
        out_shape=[jax.ShapeDtypeStruct((N_CHIPS, g.shape[1] // 2, g.shape[2]), BF16) for g in gs],
        compiler_params=_params("parallel"),
    )(core, *gs, *r1s)


def _adamw_math(w, g, m, v):
    m = ADAM_B1 * m + (1.0 - ADAM_B1) * g
    v = ADAM_B2 * v + (1.0 - ADAM_B2) * (g * g)
    m_hat = m / (1.0 - ADAM_B1 ** ADAM_STEP)
    v_hat = v / (1.0 - ADAM_B2 ** ADAM_STEP)
    delta = -ADAM_LR * (m_hat / (jnp.sqrt(v_hat) + ADAM_EPS) + ADAM_WD * w)
    return delta, m, v


def _adamw_sum(name, ws, parts, ms, vs):
    n = len(ws)
    r, c = ws[0].shape
    assert all(w.shape == (r, c) for w in ws)
    br = r
    while br * c * 4 > (1 << 20) and br % 32 == 0:
        br //= 2

    def body(*refs):
        for k in range(n):
            w_ref, p_ref, m_ref, v_ref = refs[4 * k:4 * k + 4]
            g_ref, d_ref, nm_ref, nv_ref = refs[4 * n + 4 * k:4 * n + 4 * k + 4]
            g = p_ref[0].astype(F32)
            for j in range(1, N_CHIPS):
                g = g + p_ref[j].astype(F32)
            d, nm, nv = _adamw_math(w_ref[...], g, m_ref[...], v_ref[...])
            g_ref[...] = g
            d_ref[...] = d
            nm_ref[...] = nm
            nv_ref[...] = nv

    spec = pl.BlockSpec((br, c), lambda i: (i, 0))
    shp = jax.ShapeDtypeStruct((r, c), F32)
    args = [a for k in range(n) for a in (ws[k], parts[k], ms[k], vs[k])]
    res = _call(
        body, name=name, grid=(r // br,),
        in_specs=[spec, pl.BlockSpec((N_CHIPS, br, c), lambda i: (0, i, 0)), spec, spec] * n,
        out_specs=[spec] * (4 * n), out_shape=[shp] * (4 * n), sem=("parallel",), args=args)
    return [res[4 * k:4 * k + 4] for k in range(n)]


_SMALL_VECTORS = ("ffn1_norm", "mix_norm", "xattn_norm", "mem_norm", "ffn2_norm", "final_norm", "q_norm",
                  "kv_norm", "pool_scale")


_LOSS_ROW = 9
_VEC_ROWS = 16
_POOL_ROWS = POOL_GROUPS * POOL_CH


def _small_params_step(g, w, m, v, loss_local):
    names = list(_SMALL_VECTORS) + ["pool_w"]
    nv = len(_SMALL_VECTORS)
    widths = [g[n].shape[1] for n in _SMALL_VECTORS]
    shapes = {"vec": (_VEC_ROWS, D_MODEL), "pool": (_POOL_ROWS, POOL_CH)}

    def body(*refs):
        ins = refs[:4 * (nv + 1) + 1]
        outs = refs[len(ins):len(ins) + 4 * (nv + 1) + 1]
        vec_own, vec_sib, vec_all, pool_sib, pool_sum, pool_all, send, recv = refs[len(ins) + len(outs):]
        g_in, w_in, m_in, v_in = (ins[k * (nv + 1):(k + 1) * (nv + 1)] for k in range(4))
        loss_in = ins[-1]
        g_out, d_out, m_out, v_out = (outs[k * (nv + 1):(k + 1) * (nv + 1)] for k in range(4))
        loss_out = outs[-1]
        x, y, c, me, chips, chip_ids = _mesh_pos()
        sib = (x, y, 1 - c)

        def remote(src, dst, k, dev):
            return pltpu.make_async_remote_copy(src_ref=src, dst_ref=dst, send_sem=send.at[k], recv_sem=recv.at[k],
                                                device_id=dev, device_id_type=_MESH)

        vec_own[...] = jnp.zeros_like(vec_own)
        for i in range(nv):
            vec_own[i:i + 1, 0:widths[i]] = g_in[i][...]
        vec_own[_LOSS_ROW:_LOSS_ROW + 1, 0:128] = loss_in[...]
        swaps = [remote(vec_own, vec_sib, 0, sib), remote(g_in[nv], pool_sib, 1, sib)]
        for cp in swaps:
            cp.start()
        for cp in swaps:
            cp.wait()
        vec_all[me] = vec_own[...] + vec_sib[...]
        pool_sum[...] = g_in[nv][...] + pool_sib[...]
        pool_all[me] = pool_sum[...]
        hv, hp = _VEC_ROWS // 2, _POOL_ROWS // 2
        mine_v = pl.ds(pl.multiple_of(c * hv, 8), hv)
        mine_p = pl.ds(pl.multiple_of(c * hp, 8), hp)
        sends = []
        for j, (cx, cy) in enumerate(chips):
            sends.append(remote(vec_all.at[me, mine_v], vec_all.at[me, mine_v], 2 + j, (cx, cy, c)))
            sends.append(remote(pool_sum.at[mine_p], pool_all.at[me, mine_p], 5 + j, (cx, cy, c)))
        for cp in sends:
            cp.start()
        for cp in sends:
            cp.wait()
        mirrors = []
        for j in range(3):
            mirrors.append(remote(vec_all.at[chip_ids[j], mine_v], vec_all.at[chip_ids[j], mine_v], 8 + j, sib))
            mirrors.append(remote(pool_all.at[chip_ids[j], mine_p], pool_all.at[chip_ids[j], mine_p], 11 + j, sib))
        for cp in mirrors:
            cp.start()
        for cp in mirrors:
            cp.wait()
        vec_tot = vec_all[0]
        pool_tot = pool_all[0]
        for i in range(1, N_CHIPS):
            vec_tot = vec_tot + vec_all[i]
            pool_tot = pool_tot + pool_all[i]
        vec_sib[...] = vec_tot
        loss_out[...] = vec_sib[_LOSS_ROW:_LOSS_ROW + 1, 0:128]
        for i in range(nv + 1):
            gi = pool_tot if i == nv else vec_sib[i:i + 1, 0:widths[i]]
            d, nm, nvv = _adamw_math(w_in[i][...], gi, m_in[i][...], v_in[i][...])
            g_out[i][...] = gi
            d_out[i][...] = d
            m_out[i][...] = nm
            v_out[i][...] = nvv

    vm = pl.BlockSpec(memory_space=pltpu.VMEM)
    args = [d[n] for d in (g, w, m, v) for n in names] + [jnp.broadcast_to(loss_local.reshape(1, 1), (1, 128))]
    out_shape = [jax.ShapeDtypeStruct(g[n].shape, F32) for _ in range(4) for n in names]
    out_shape.append(jax.ShapeDtypeStruct((1, 128), F32))
    res = pl.pallas_call(
        body, name="small_params_step", in_specs=[vm] * len(args), out_specs=[vm] * len(out_shape),
        out_shape=out_shape,
        scratch_shapes=[pltpu.VMEM(shapes["vec"], F32), pltpu.VMEM(shapes["vec"], F32),
                        pltpu.VMEM((N_CHIPS,) + shapes["vec"], F32), pltpu.VMEM(shapes["pool"], F32),
                        pltpu.VMEM(shapes["pool"], F32), pltpu.VMEM((N_CHIPS,) + shapes["pool"], F32),
                        pltpu.SemaphoreType.DMA((14,)), pltpu.SemaphoreType.DMA((14,))],
        compiler_params=pltpu.CompilerParams(vmem_limit_bytes=V7X_VMEM_LIMIT_BYTES),
    )(*args)
    k = len(names)
    dicts = [dict(zip(names, res[i * k:(i + 1) * k])) for i in range(4)]
    return dicts[0], dicts[1], dicts[2], dicts[3], res[-1]


_WEIGHTS = ("ffn1_norm", "ffn1_w_gate", "ffn1_w_up", "ffn1_w_down", "mix_norm", "w_in", "q_norm", "w_q_up",
            "kv_norm", "w_kv_up", "pool_w", "pool_scale", "w_out", "xattn_norm", "mem_norm", "w_mq", "w_mkv",
            "w_mo", "ffn2_norm", "ffn2_w_gate", "ffn2_w_up", "ffn2_w_down", "final_norm")
_SHARDED = ("ffn1_w_gate", "ffn1_w_up", "ffn1_w_down", "w_in", "w_q_up", "w_kv_up", "w_out", "w_mq", "w_mkv",
            "w_mo", "ffn2_w_gate", "ffn2_w_up", "ffn2_w_down")
W_IN_SPLIT = Q_LORA + KV_LORA + ROPE_DIM


_FFN1 = ("ffn1_w_gate", "ffn1_w_up", "ffn1_w_down")
_TRANSPOSED = ("ffn1_w_gate", "ffn1_w_up", "ffn2_w_gate", "ffn2_w_up", "w_in", "w_q_up")


def _local_view(name, a):
    return jnp.swapaxes(a, 1, 2)[0] if name in _TRANSPOSED else a[0]


def _global_view(name, a):
    return jnp.swapaxes(a[None], 1, 2) if name in _TRANSPOSED else a[None]


def _pad_shard(name, a):
    if name == "w_in":
        return jnp.concatenate([a[:W_IN_SPLIT], jnp.zeros((64, a.shape[1]), a.dtype), a[W_IN_SPLIT:]], axis=0)
    if name == "w_q_up":
        return jnp.pad(a, ((0, 64), (0, 0)))
    return a


def _unpad_shard(name, a):
    if name == "w_in":
        return jnp.concatenate([a[:, :W_IN_SPLIT], a[:, W_IN_SPLIT + 64:]], axis=1)
    if name == "w_q_up":
        return a[:, :192]
    return a


def _stacked(g):
    return g if g.ndim == 3 else g.reshape(N_CHIPS, g.shape[0] // N_CHIPS, g.shape[1])


class _Plan:
    AG_UNITS = (
        (("w_in", "w_q_up", "w_kv_up"), "ffn1_up"),
        (("w_out",), "w_in"),
        (("w_mq",), "qkv_prep"),
        (("w_mkv", "w_mo", "ffn2_w_gate"), "mla_fwd"),
        (("ffn2_w_up",), "xattn_fwd"),
        (("ffn2_w_down",), "ffn2_up"),
    )
    RS_UNITS = (
        (("ffn2_w_gate", "ffn2_w_up", "ffn2_w_down"), "ffn2_dn", "mla_bwd", "qkv_prep_bwd"),
        (("w_mo", "w_mq", "w_mkv"), "w_out_dx", "mla_bwd", "qkv_prep_bwd"),
        (("w_out", "w_q_up", "w_kv_up", "w_in"), "w_in_dx", "ffn1_dact", "ffn1_dwd"),
        (("ffn1_w_down",), "ffn1_dwg", "ffn1_dwu", "ffn1_dn_a"),
        (("ffn1_w_gate",), "ffn1_dwu", "ffn1_dn_a", "ffn1_dn_b"),
        (("ffn1_w_up",), "ffn1_dn_a", "ffn1_dn_b", "adamw_w_kv_up"),
    )
    ADAMW_ORDER = (("w_kv_up",), ("ffn2_w_gate", "ffn2_w_up"), ("ffn2_w_down", "ffn1_w_down"), ("w_mo", "w_mq", "w_out"),
                   ("w_mkv",), ("w_q_up",), ("w_in",), ("ffn1_w_gate", "ffn1_w_up"))

    def __init__(self, shards, w, grads, core):
        self.shards, self.w, self.grads, self.core = shards, w, grads, core
        self.last_slab_step = 0
        self.parts = {}
        self.ag = [None for _ in self.AG_UNITS]
        self.rs = [[None, None, None, None] for _ in self.RS_UNITS]

    def pre(self, name):
        for i, (names, host) in enumerate(self.AG_UNITS):
            if name == host:
                st = _ag_ici_stage([self.shards[n] for n in names])
                st.then = _ag_d2d_stage(st.outs)
                st.start_step = self.last_slab_step if name == "ffn1_up" else 0
                self.ag[i] = _host(name, st)
        for i, (names, h1, h2, h3) in enumerate(self.RS_UNITS):
            if name == h1:
                self.rs[i][0] = _host(name, _rs_swap_stage([_stacked(self.grads[n]) for n in names]))
            if name == h2:
                self.rs[i][2] = _host(name, _rs_scatter_stage(self.rs[i][1], relative=names[0] in _FFN1))
            if name == h3:
                self.rs[i][3] = _host(name, _rs_mirror_stage(self.rs[i][2].results))

    def post(self, name):
        for i, (names, host) in enumerate(self.AG_UNITS):
            if name == host:
                for n, f in zip(names, self.ag[i].results):
                    self.w[n] = _full_weight(n, f)
        for i, (names, h1, h2, h3) in enumerate(self.RS_UNITS):
            if name == h1:
                self.rs[i][1] = list(_pair_add("pair_add_" + names[0], [_stacked(self.grads[n]) for n in names],
                                               self.rs[i][0].results, self.core))
            if name == h3:
                for n, p in zip(names, self.rs[i][3].results):
                    self.parts[n] = p


def _full_weight(name, stacked):
    if name in ("w_out", "w_mq", "w_mo"):
        return stacked.reshape(D_MODEL, D_MODEL)
    return stacked


def kernel(x, mem, positions, ffn1_norm, ffn1_w_gate, ffn1_w_up, ffn1_w_down, mix_norm, w_in, q_norm, w_q_up, kv_norm, w_kv_up, pool_w, pool_scale, w_out, xattn_norm, mem_norm, w_mq, w_mkv, w_mo, ffn2_norm, ffn2_w_gate, ffn2_w_up, ffn2_w_down, final_norm, loss_target, m_ffn1_norm, m_ffn1_w_gate, m_ffn1_w_up, m_ffn1_w_down, m_mix_norm, m_w_in, m_q_norm, m_w_q_up, m_kv_norm, m_w_kv_up, m_pool_w, m_pool_scale, m_w_out, m_xattn_norm, m_mem_norm, m_w_mq, m_w_mkv, m_w_mo, m_ffn2_norm, m_ffn2_w_gate, m_ffn2_w_up, m_ffn2_w_down, m_final_norm, v_ffn1_norm, v_ffn1_w_gate, v_ffn1_w_up, v_ffn1_w_down, v_mix_norm, v_w_in, v_q_norm, v_w_q_up, v_kv_norm, v_w_kv_up, v_pool_w, v_pool_scale, v_w_out, v_xattn_norm, v_mem_norm, v_w_mq, v_w_mkv, v_w_mo, v_ffn2_norm, v_ffn2_w_gate, v_ffn2_w_up, v_ffn2_w_down, v_final_norm):
    wts = dict(zip(_WEIGHTS, (ffn1_norm, ffn1_w_gate, ffn1_w_up, ffn1_w_down, mix_norm, w_in, q_norm, w_q_up, kv_norm, w_kv_up, pool_w, pool_scale, w_out, xattn_norm, mem_norm, w_mq, w_mkv, w_mo, ffn2_norm, ffn2_w_gate, ffn2_w_up, ffn2_w_down, final_norm)))
    mom = dict(zip(_WEIGHTS, (m_ffn1_norm, m_ffn1_w_gate, m_ffn1_w_up, m_ffn1_w_down, m_mix_norm, m_w_in, m_q_norm, m_w_q_up, m_kv_norm, m_w_kv_up, m_pool_w, m_pool_scale, m_w_out, m_xattn_norm, m_mem_norm, m_w_mq, m_w_mkv, m_w_mo, m_ffn2_norm, m_ffn2_w_gate, m_ffn2_w_up, m_ffn2_w_down, m_final_norm)))
    var = dict(zip(_WEIGHTS, (v_ffn1_norm, v_ffn1_w_gate, v_ffn1_w_up, v_ffn1_w_down, v_mix_norm, v_w_in, v_q_norm, v_w_q_up, v_kv_norm, v_w_kv_up, v_pool_w, v_pool_scale, v_w_out, v_xattn_norm, v_mem_norm, v_w_mq, v_w_mkv, v_w_mo, v_ffn2_norm, v_ffn2_w_gate, v_ffn2_w_up, v_ffn2_w_down, v_final_norm)))
    small = [n for n in _WEIGHTS if n not in _SHARDED]

    global _PLAN
    shards = {n: _pad_shard(n, _local_view(n, wts[n])).astype(BF16) for n in _SHARDED}
    w = {n: wts[n].reshape(1, -1) for n in _SMALL_VECTORS}
    w["pool_w"] = pool_w[0].astype(BF16)
    grads = {}
    core = lax.axis_index("c").astype(jnp.int32).reshape(1)
    plan = _Plan(shards, w, grads, core)
    _PLAN = plan
    try:
        w["ffn1_shards"] = tuple(shards[n] for n in _FFN1)

        loss_local, dx = _local_step(x[0], mem[0], positions[0], loss_target[0], w, grads)

        def small_view(d):
            out = {n: d[n].reshape(1, -1) for n in _SMALL_VECTORS}
            out["pool_w"] = d["pool_w"].reshape(_POOL_ROWS, POOL_CH)
            return out

        *small_res, loss_vec = _small_params_step(small_view(grads), small_view(wts), small_view(mom),
                                                  small_view(var), loss_local)
        g_out, d_out, m_out, v_out = ({n: r[n].reshape(wts[n].shape) for n in small} for r in small_res)
        loss = loss_vec[0, 0]

        for names in _Plan.ADAMW_ORDER:
            res = _adamw_sum("adamw_" + names[0], [_local_view(n, wts[n]) for n in names],
                             [_unpad_shard(n, plan.parts[n]) for n in names],
                             [_local_view(n, mom[n]) for n in names], [_local_view(n, var[n]) for n in names])
            for n, r4 in zip(names, res):
                g_out[n], d_out[n], m_out[n], v_out[n] = (_global_view(n, r) for r in r4)
    finally:
        _PLAN = None
        _PENDING.clear()

    return (loss, dx[None], *[g_out[n] for n in _WEIGHTS], *[d_out[n] for n in _WEIGHTS],
            *[m_out[n] for n in _WEIGHTS], *[v_out[n] for n in _WEIGHTS])
```

```python
import jax
import jax.numpy as jnp
from jax import lax
from jax.experimental import pallas as pl
from jax.experimental.pallas import tpu as pltpu

F32 = jnp.float32
BF16 = jnp.bfloat16

D_MODEL = 1024
D_FF = 2816
N_CHIPS = 4
FF_SHARD = D_FF // N_CHIPS
MLA_HEADS = 4
Q_LORA = 256
KV_LORA = 128
ROPE_DIM = 64
HEAD_QK = 256
HEAD_V = 128
POOL_GROUPS = 4
POOL_CH = 128
MEM_HEADS = 4
MEM_HEAD_DIM = 256
RMS_EPS = 1e-6
ROPE_BASE = 10000.0
MLA_SCALE = (128 + 64) ** -0.5
MEM_SCALE = MEM_HEAD_DIM ** -0.5

ADAM_LR = 0.001
ADAM_B1 = 0.9
ADAM_B2 = 0.999
ADAM_EPS = 1e-08
ADAM_WD = 0.01
ADAM_STEP = 10

V7X_VMEM_LIMIT_BYTES = 56 * 1024 * 1024

NN = ((1,), (0,))
NT = ((1,), (1,))
TN = ((0,), (0,))


def _params(*sem):
    return pltpu.CompilerParams(dimension_semantics=sem, vmem_limit_bytes=V7X_VMEM_LIMIT_BYTES)


_MESH = pl.DeviceIdType.MESH
_ANY = pl.BlockSpec(memory_space=pl.ANY)


class _Stage:
    def __init__(self, ins, outs, n_remote, n_local, copies, aliases=None):
        self.ins, self.outs, self.n_remote, self.n_local = list(ins), list(outs), n_remote, n_local
        self.copies, self.aliases = copies, dict(aliases or {})
        self.results = None
        self.start_step = 0
        self.then = None

    def descriptors(self, in_refs, out_refs, send, recv, loc):
        ds, ri, li = [], 0, 0
        for src, dst, dev in self.copies(in_refs, out_refs):
            if dev is None:
                ds.append(pltpu.make_async_copy(src, dst, loc.at[li]))
                li += 1
            else:
                ds.append(pltpu.make_async_remote_copy(src_ref=src, dst_ref=dst, send_sem=send.at[ri],
                                                       recv_sem=recv.at[ri], device_id=dev, device_id_type=_MESH))
                ri += 1
        assert ri == self.n_remote and li == self.n_local
        return ds


_PENDING = {}


def _host(name, stage):
    _PENDING.setdefault(name, []).append(stage)
    return stage


_PLAN = None


def _call(body, **kw):
    if _PLAN is not None:
        _PLAN.pre(kw["name"])
    res = _call_hosting(body, **kw)
    if _PLAN is not None:
        _PLAN.post(kw["name"])
    return res


def _call_hosting(body, *, name, grid, in_specs, out_specs, out_shape, sem, args, scratch_shapes=(), aliases=None):
    stages = _PENDING.pop(name, [])
    scratch_shapes = list(scratch_shapes)
    if not stages:
        return pl.pallas_call(body, name=name, grid=grid, in_specs=in_specs, out_specs=out_specs,
                              out_shape=out_shape, scratch_shapes=scratch_shapes,
                              input_output_aliases=dict(aliases or {}), compiler_params=_params(*sem))(*args)
    ni, no, ns = len(in_specs), len(out_shape), len(scratch_shapes)
    c_ins = [a for st in stages for a in st.ins]
    c_outs = [o for st in stages for o in st.outs]
    nci, nco = len(c_ins), len(c_outs)
    aliases, io, oo = dict(aliases or {}), 0, 0
    for st in stages:
        for i, j in st.aliases.items():
            aliases[ni + io + i] = no + oo + j
        io += len(st.ins)
        oo += len(st.outs)
    dma = pltpu.SemaphoreType.DMA
    sems = []
    for st in stages:
        sems += [dma((max(st.n_remote, 1),)), dma((max(st.n_remote, 1),)), dma((max(st.n_local, 1),))]
    followers = [st.then for st in stages if st.then is not None]
    for st in followers:
        sems += [dma((max(st.n_remote, 1),)), dma((max(st.n_remote, 1),)), dma((max(st.n_local, 1),))]

    def wrapped(*refs):
        ins, cin = refs[:ni], refs[ni:ni + nci]
        outs, cout = refs[ni + nci:ni + nci + no], refs[ni + nci + no:ni + nci + no + nco]
        scr = refs[ni + nci + no + nco:ni + nci + no + nco + ns]
        sem_refs = refs[ni + nci + no + nco + ns:]
        step = pl.program_id(0)
        last = pl.program_id(0) == grid[0] - 1
        for ax in range(1, len(grid)):
            step = step * grid[ax] + pl.program_id(ax)
            last = jnp.logical_and(last, pl.program_id(ax) == grid[ax] - 1)

        def descriptors(si):
            io = sum(len(st.ins) for st in stages[:si])
            oo = sum(len(st.outs) for st in stages[:si])
            st = stages[si]
            return st.descriptors(cin[io:io + len(st.ins)], cout[oo:oo + len(st.outs)], *sem_refs[3 * si:3 * si + 3])

        def follower_descriptors(fi):
            si = [k for k, st in enumerate(stages) if st.then is not None][fi]
            oo = sum(len(st.outs) for st in stages[:si])
            bufs = cout[oo:oo + len(stages[si].outs)]
            k0 = 3 * (len(stages) + fi)
            return followers[fi].descriptors(bufs, bufs, *sem_refs[k0:k0 + 3])

        def start(si):
            @pl.when(step == stages[si].start_step)
            def _():
                for d in descriptors(si):
                    d.start()

        for si, st in enumerate(stages):
            if st.start_step == 0:
                start(si)
        body(*ins, *outs, *scr)
        for si, st in enumerate(stages):
            if st.start_step != 0:
                start(si)

        @pl.when(last)
        def _():
            for si in range(len(stages)):
                for d in descriptors(si):
                    d.wait()
            for fi in range(len(followers)):
                for d in follower_descriptors(fi):
                    d.start()
            for fi in range(len(followers)):
                for d in follower_descriptors(fi):
                    d.wait()

    res = pl.pallas_call(
        wrapped, name=name, grid=grid, in_specs=list(in_specs) + [_ANY] * nci,
        out_specs=list(out_specs) + [_ANY] * nco, out_shape=list(out_shape) + c_outs,
        scratch_shapes=scratch_shapes + sems, input_output_aliases=aliases,
        compiler_params=_params(*(("arbitrary",) * len(grid))))(*args, *c_ins)
    oo = no
    for st in stages:
        st.results = list(res[oo:oo + len(st.outs)])
        oo += len(st.outs)
    return list(res[:no])


def _dot(a, b, dims):
    return lax.dot_general(a.astype(BF16), b.astype(BF16), (dims, ((), ())), preferred_element_type=F32)


_MAX_ROW_BLOCK = 1024
_ATT_BLOCK = 512


_MAX_REDUCE_BLOCK = 2048


def _row_block(s, want=1024):
    return min(want, s, _MAX_ROW_BLOCK)


def _reduce_block(s):
    return min(s, _MAX_REDUCE_BLOCK)


def _matmul(name, grid, terms, extras, outs, epilogue, acc_shape, fill=(), summed=()):
    nt, ne, no, nf = len(terms), len(extras), len(outs), len(fill)
    nk = grid[-1]
    dims = [t[4] for t in terms]

    def body(*refs):
        a_refs, b_refs = refs[:nt], refs[nt:2 * nt]
        e_refs = refs[2 * nt:2 * nt + ne]
        o_refs = refs[2 * nt + ne + nf:2 * nt + ne + nf + no]

        def finish(acc):
            vals = epilogue(acc, *[e[...] for e in e_refs])
            for idx, (o, val) in enumerate(zip(o_refs, vals)):
                if idx in summed:
                    @pl.when(pl.program_id(0) == 0)
                    def _(o=o, val=val):
                        o[...] = val.astype(o.dtype)

                    @pl.when(pl.program_id(0) > 0)
                    def _(o=o, val=val):
                        o[...] += val.astype(o.dtype)
                else:
                    o[...] = val.astype(o.dtype)

        if nk == 1:
            part = None
            for a, b, d in zip(a_refs, b_refs, dims):
                t = _dot(a[...], b[...], d)
                part = t if part is None else part + t
            finish(part)
        else:
            acc_ref = refs[-1]
            k = pl.program_id(len(grid) - 1)

            @pl.when(k == 0)
            def _():
                acc_ref[...] = jnp.zeros_like(acc_ref)

            for a, b, d in zip(a_refs, b_refs, dims):
                acc_ref[...] += _dot(a[...], b[...], d)

            @pl.when(k == nk - 1)
            def _():
                finish(acc_ref[...])

    in_specs = [t[1] for t in terms] + [t[3] for t in terms] + [e[1] for e in extras] + [_ANY] * nf
    args = [t[0] for t in terms] + [t[2] for t in terms] + [e[0] for e in extras] + list(fill)
    sem = ("arbitrary" if summed else "parallel",) * (len(grid) - 1) + ("arbitrary",)
    aliases = {2 * nt + ne + i: i for i in range(nf)}
    return _call(
        body, name=name, grid=grid, in_specs=in_specs,
        out_specs=[o[1] for o in outs], out_shape=[o[0] for o in outs],
        scratch_shapes=[pltpu.VMEM(acc_shape, F32)] if nk > 1 else [], sem=sem, args=args, aliases=aliases)


def _ident(acc):
    return (acc,)


def _rmsnorm_fwd(name, x, gain, width, col_block=0):
    s = x.shape[0]
    bm = _row_block(s)

    def body(x_ref, g_ref, o_ref):
        xf = x_ref[...]
        r = lax.rsqrt(jnp.mean(xf * xf, axis=-1, keepdims=True) + RMS_EPS)
        o_ref[...] = ((xf * r) * g_ref[...]).astype(o_ref.dtype)

    return pl.pallas_call(
        body, name=name, grid=(s // bm,),
        in_specs=[pl.BlockSpec((bm, width), lambda i: (i, col_block)), pl.BlockSpec((1, width), lambda i: (0, 0))],
        out_specs=pl.BlockSpec((bm, width), lambda i: (i, 0)),
        out_shape=jax.ShapeDtypeStruct((s, width), BF16),
        compiler_params=_params("parallel"),
    )(x, gain)


def _rms_bwd_math(dy, xf, g, width):
    r = lax.rsqrt(jnp.mean(xf * xf, axis=-1, keepdims=True) + RMS_EPS)
    dyg = dy * g
    dot = jnp.sum(dyg * xf, axis=-1, keepdims=True)
    dx = r * dyg - xf * ((r * r * r) * (dot * (1.0 / width)))
    dgain = jnp.sum(dy * (xf * r), axis=0, keepdims=True)
    return dx, dgain


def _rmsnorm_bwd(name, dy, x, gain, width, col_block=0, dres=None, out_dtype=F32):
    s = x.shape[0]
    bm = _row_block(s)
    has_res = dres is not None

    def body(*refs):
        if has_res:
            dy_ref, x_ref, g_ref, r_ref, dx_ref, dg_ref = refs
        else:
            dy_ref, x_ref, g_ref, dx_ref, dg_ref = refs
        dx, dgain = _rms_bwd_math(dy_ref[...].astype(F32), x_ref[...], g_ref[...], width)
        if has_res:
            dx = dx + r_ref[...]
        dx_ref[...] = dx.astype(dx_ref.dtype)

        @pl.when(pl.program_id(0) == 0)
        def _():
            dg_ref[...] = dgain

        @pl.when(pl.program_id(0) > 0)
        def _():
            dg_ref[...] += dgain

    row = pl.BlockSpec((bm, width), lambda i: (i, 0))
    in_specs = [row, pl.BlockSpec((bm, width), lambda i: (i, col_block)), pl.BlockSpec((1, width), lambda i: (0, 0))]
    args = [dy, x, gain]
    out_specs = [row, pl.BlockSpec((1, width), lambda i: (0, 0))]
    out_shape = [jax.ShapeDtypeStruct((s, width), out_dtype), jax.ShapeDtypeStruct((1, width), F32)]
    if has_res:
        in_specs.append(row)
        args.append(dres)
    return _call(body, name=name, grid=(s // bm,), in_specs=in_specs, out_specs=out_specs, out_shape=out_shape,
                 sem=("arbitrary",), args=args)


def _ffn_up(name, n, wg, wu):
    s = n.shape[0]
    bm = _row_block(s)

    def body(n_ref, wg_ref, wu_ref, a_ref, dadu_ref, dadg_ref):
        x = n_ref[...]
        g = _dot(x, wg_ref[...], NT)
        u = _dot(x, wu_ref[...], NT)
        sg = jax.nn.sigmoid(g)
        silu = g * sg
        a_ref[...] = (silu * u).astype(BF16)
        dadu_ref[...] = silu.astype(BF16)
        dadg_ref[...] = (u * (sg * (1.0 + g * (1.0 - sg)))).astype(BF16)

    w_spec = pl.BlockSpec((None, FF_SHARD, D_MODEL), lambda j, i: (j, 0, 0))
    o_spec = pl.BlockSpec((None, bm, FF_SHARD), lambda j, i: (j, i, 0))
    shp = jax.ShapeDtypeStruct((N_CHIPS, s, FF_SHARD), BF16)
    return _call(
        body, name=name, grid=(N_CHIPS, s // bm),
        in_specs=[pl.BlockSpec((bm, D_MODEL), lambda j, i: (i, 0)), w_spec, w_spec],
        out_specs=[o_spec, o_spec, o_spec], out_shape=[shp, shp, shp],
        sem=("parallel", "parallel"), args=[n, wg, wu])


def _ffn1_up_gather(xin, gain, g_sh, u_sh, d_sh):
    s = xin.shape[0]
    bm = _row_block(s)
    nrb = s // bm
    rows, cols = g_sh.shape

    def body(x_ref, gain_ref, gs, us, ds, n_ref, a_ref, dadu_ref, dadg_ref, wg, wu, wd, gbuf, ubuf,
             send, recv, qsend, qrecv, fsend, frecv, loc, ld):
        r, i = pl.program_id(0), pl.program_id(1)
        x, y, c = lax.axis_index("x"), lax.axis_index("y"), lax.axis_index("c")
        sib = (x, y, 1 - c)
        mine, _ = _half_rows(c, rows)
        quarters = _quarter_rows(c, rows)
        shards, fulls, bufs = (gs, us, ds), (wg, wu, wd), (gbuf, ubuf)

        def remote(src, dst, ssem, rsem, dev):
            return pltpu.make_async_remote_copy(src_ref=src, dst_ref=dst, send_sem=ssem, recv_sem=rsem,
                                                device_id=dev, device_id_type=_MESH)

        def peer(rel):
            return ((1 - x) if rel & 2 else x, (1 - y) if rel & 1 else y, c)

        def ici(k, rel, dev=sib):
            return remote(shards[k].at[mine], fulls[k].at[rel, mine], send.at[k, rel - 1], recv.at[k, rel - 1], dev)

        def quarter(k, which, dev=sib):
            slab, q = ((2, quarters[0]), (1, quarters[1]))[which]
            return remote(fulls[k].at[slab, q], fulls[k].at[3, q], qsend.at[k, which], qrecv.at[k, which], dev)

        def fwd(k, rel):
            return remote(fulls[k].at[rel, mine], fulls[k].at[rel, mine], fsend.at[k, rel - 1], frecv.at[k, rel - 1], sib)

        def own(k):
            return pltpu.make_async_copy(shards[k], fulls[k].at[0], loc.at[k])

        def load(slab):
            for k in (0, 1):
                pltpu.make_async_copy(shards[k] if slab == 0 else fulls[k].at[slab], bufs[k], ld.at[k]).start()
            for k in (0, 1):
                pltpu.make_async_copy(shards[k] if slab == 0 else fulls[k].at[slab], bufs[k], ld.at[k]).wait()

        def from_neighbour(ks, rel):
            for k in ks:
                ici(k, rel).wait_recv()
                fwd(k, rel).start()
                quarter(k, 0 if rel == 2 else 1, peer(1 if rel == 2 else 2)).start()
            for k in ks:
                fwd(k, rel).wait_recv()

        def from_diagonal(ks):
            for k in ks:
                quarter(k, 0).wait_recv()
                quarter(k, 1).wait_recv()
                fwd(k, 3).start()
            for k in ks:
                fwd(k, 3).wait_recv()

        @pl.when(jnp.logical_and(r == 0, i == 0))
        def _():
            for k in range(3):
                own(k).start()
            for rel in (1, 2):
                for k in (0, 1):
                    ici(k, rel, peer(rel)).start()
            load(0)

        @pl.when(jnp.logical_and(r == 1, i == 0))
        def _():
            from_neighbour((0, 1), 1)
            load(1)
            for rel in (1, 2):
                ici(2, rel, peer(rel)).start()

        @pl.when(jnp.logical_and(r == 2, i == 0))
        def _():
            from_neighbour((0, 1), 2)
            load(2)

        @pl.when(jnp.logical_and(r == 3, i == 0))
        def _():
            from_diagonal((0, 1))
            load(3)

        xv = _norm_bf16(x_ref[...], gain_ref[...])

        @pl.when(r == 0)
        def _():
            n_ref[...] = xv

        g = _dot(xv, gbuf[...], NT)
        u = _dot(xv, ubuf[...], NT)
        sg = jax.nn.sigmoid(g)
        silu = g * sg
        a_ref[...] = (silu * u).astype(BF16)
        dadu_ref[...] = silu.astype(BF16)
        dadg_ref[...] = (u * (sg * (1.0 + g * (1.0 - sg)))).astype(BF16)

        @pl.when(jnp.logical_and(r == 3, i == nrb - 1))
        def _():
            from_neighbour((2,), 1)
            from_neighbour((2,), 2)
            from_diagonal((2,))
            for k in range(3):
                for rel in (1, 2):
                    ici(k, rel).wait_send()
                for which in (0, 1):
                    quarter(k, which).wait_send()
                for rel in (1, 2, 3):
                    fwd(k, rel).wait_send()
                own(k).wait()

    o_spec = pl.BlockSpec((None, bm, FF_SHARD), lambda r, i: (r, i, 0))
    act = jax.ShapeDtypeStruct((N_CHIPS, s, FF_SHARD), BF16)
    full = jax.ShapeDtypeStruct((N_CHIPS, rows, cols), BF16)
    dma = pltpu.SemaphoreType.DMA
    if _PLAN is not None:
        _PLAN.last_slab_step = 3 * nrb
    n_spec = pl.BlockSpec((bm, D_MODEL), lambda r, i: (jnp.where(r == 0, i, nrb - 1), 0))
    return _call(
        body, name="ffn1_up", grid=(N_CHIPS, nrb),
        in_specs=[pl.BlockSpec((bm, D_MODEL), lambda r, i: (i, 0)), pl.BlockSpec((1, D_MODEL), lambda r, i: (0, 0)),
                  _ANY, _ANY, _ANY],
        out_specs=[n_spec, o_spec, o_spec, o_spec, _ANY, _ANY, _ANY],
        out_shape=[jax.ShapeDtypeStruct((s, D_MODEL), BF16), act, act, act, full, full, full],
        scratch_shapes=[pltpu.VMEM((rows, cols), BF16), pltpu.VMEM((rows, cols), BF16), dma((3, 2)), dma((3, 2)),
                        dma((3, 2)), dma((3, 2)), dma((3, 3)), dma((3, 3)), dma((3,)), dma((2,))],
        sem=("arbitrary", "arbitrary"), args=[xin, gain, g_sh, u_sh, d_sh])


def _residual_epilogue(alpha, with_norm):
    if not with_norm:
        return lambda acc, r: (r + alpha * acc,)

    def epilogue(acc, r, g):
        h = r + alpha * acc
        rs = lax.rsqrt(jnp.mean(h * h, axis=-1, keepdims=True) + RMS_EPS)
        return h, (h * rs) * g

    return epilogue


def _residual_outs(s, bm, gain):
    row = pl.BlockSpec((bm, D_MODEL), lambda i, k: (i, 0))
    outs = [(jax.ShapeDtypeStruct((s, D_MODEL), F32), row)]
    if gain is None:
        return [], outs
    return [(gain, pl.BlockSpec((1, D_MODEL), lambda i, k: (0, 0)))], outs + [(jax.ShapeDtypeStruct((s, D_MODEL), BF16), row)]


def _loss_epilogue(acc, res, g, target):
    d = acc.shape[-1]
    h = res + 0.5 * acc
    r = lax.rsqrt(jnp.mean(h * h, axis=-1, keepdims=True) + RMS_EPS)
    err = (h * r) * g - target
    part = 0.5 * jnp.sum(jnp.mean(err * err, axis=-1, keepdims=True), axis=0, keepdims=True)
    dx, dgain = _rms_bwd_math(err * (1.0 / d), h, g, d)
    return dx, dx, jnp.broadcast_to(part, (1, 128)), dgain


def _ffn_down(name, a, wd, res, gain=None, loss=None):
    s = a.shape[1]
    bm = _row_block(s, 512)
    row = pl.BlockSpec((bm, D_MODEL), lambda i, k: (i, 0))
    terms = [(a, pl.BlockSpec((None, bm, FF_SHARD), lambda i, k, j=j: (j, i, 0)),
              wd, pl.BlockSpec((None, FF_SHARD, D_MODEL), lambda i, k, j=j: (j, 0, 0)), NN) for j in range(N_CHIPS)]
    if loss is not None:
        vec = pl.BlockSpec((1, D_MODEL), lambda i, k: (0, 0))
        outs = [(jax.ShapeDtypeStruct((s, D_MODEL), F32), row), (jax.ShapeDtypeStruct((s, D_MODEL), BF16), row),
                (jax.ShapeDtypeStruct((1, 128), F32), pl.BlockSpec((1, 128), lambda i, k: (0, 0))),
                (jax.ShapeDtypeStruct((1, D_MODEL), F32), vec)]
        return _matmul(name, (s // bm, 1), terms, [(res, row), (loss[0], vec), (loss[1], row)], outs,
                       _loss_epilogue, None, summed=(2, 3))
    extras, outs = _residual_outs(s, bm, gain)
    res_out = _matmul(name, (s // bm, 1), terms, [(res, row)] + extras, outs,
                      _residual_epilogue(0.5, gain is not None), None)
    return res_out if gain is not None else res_out[0]


def _norm_bwd_epilogue(width):
    def epilogue(acc, h, g, dres):
        dx, dgain = _rms_bwd_math(acc, h, g, width)
        dx = dx + dres
        return dx, dx, dgain

    return epilogue


def _norm_bwd_operands(s, bm, h, gain, dres):
    row = pl.BlockSpec((bm, D_MODEL), lambda i, k: (i, 0))
    vec = pl.BlockSpec((1, D_MODEL), lambda i, k: (0, 0))
    extras = [(h, row), (gain, vec), (dres, row)]
    outs = [(jax.ShapeDtypeStruct((s, D_MODEL), F32), row), (jax.ShapeDtypeStruct((s, D_MODEL), BF16), row),
            (jax.ShapeDtypeStruct((1, D_MODEL), F32), vec)]
    return extras, outs, (2,)


def _ffn_bwd(tag, dh, n, dadg, dadu, a, wg, wu, wd, grads, norm_bwd=None, many_calls=True):
    s = dh.shape[0]
    bm = _row_block(s)
    bk = _reduce_block(s)
    nk = s // bk

    def act_bwd(acc, dg_da, du_da):
        da = 0.5 * acc
        return da * dg_da.astype(F32), da * du_da.astype(F32)

    slab = pl.BlockSpec((None, bm, FF_SHARD), lambda j, i, k: (j, i, 0))
    shp = jax.ShapeDtypeStruct((N_CHIPS, s, FF_SHARD), BF16)
    dg, du = _matmul(
        tag + "_dact", (N_CHIPS, s // bm, 1),
        [(dh, pl.BlockSpec((bm, D_MODEL), lambda j, i, k: (i, 0)),
          wd, pl.BlockSpec((None, FF_SHARD, D_MODEL), lambda j, i, k: (j, 0, 0)), NT)],
        [(dadg, slab), (dadu, slab)], [(shp, slab), (shp, slab)], act_bwd, None)

    grads[tag + "_w_down"] = _matmul(
        tag + "_dwd", (N_CHIPS, nk),
        [(a, pl.BlockSpec((None, bk, FF_SHARD), lambda j, k: (j, k, 0)),
          dh, pl.BlockSpec((bk, D_MODEL), lambda j, k: (k, 0)), TN)],
        [], [(jax.ShapeDtypeStruct((N_CHIPS, FF_SHARD, D_MODEL), BF16),
              pl.BlockSpec((None, FF_SHARD, D_MODEL), lambda j, k: (j, 0, 0)))],
        lambda acc: (0.5 * acc,), (FF_SHARD, D_MODEL))[0]

    def dw_up(nm, dact):
        return _matmul(
            nm, (N_CHIPS, nk),
            [(dact, pl.BlockSpec((None, bk, FF_SHARD), lambda j, k: (j, k, 0)),
              n, pl.BlockSpec((bk, D_MODEL), lambda j, k: (k, 0)), TN)],
            [], [(jax.ShapeDtypeStruct((N_CHIPS, FF_SHARD, D_MODEL), BF16),
                  pl.BlockSpec((None, FF_SHARD, D_MODEL), lambda j, k: (j, 0, 0)))],
            _ident, (FF_SHARD, D_MODEL))[0]

    def dw_up_pair(nm):
        def body(dg_ref, du_ref, n_ref, og_ref, ou_ref, acc_g, acc_u):
            k = pl.program_id(1)

            @pl.when(k == 0)
            def _():
                acc_g[...] = jnp.zeros_like(acc_g)
                acc_u[...] = jnp.zeros_like(acc_u)

            nv = n_ref[...]
            acc_g[...] += _dot(dg_ref[...], nv, TN)
            acc_u[...] += _dot(du_ref[...], nv, TN)

            @pl.when(k == nk - 1)
            def _():
                og_ref[...] = acc_g[...].astype(BF16)
                ou_ref[...] = acc_u[...].astype(BF16)

        act = pl.BlockSpec((None, bk, FF_SHARD), lambda j, k: (j, k, 0))
        out = pl.BlockSpec((None, FF_SHARD, D_MODEL), lambda j, k: (j, 0, 0))
        shape = jax.ShapeDtypeStruct((N_CHIPS, FF_SHARD, D_MODEL), BF16)
        return _call(body, name=nm, grid=(N_CHIPS, nk),
                     in_specs=[act, act, pl.BlockSpec((bk, D_MODEL), lambda j, k: (k, 0))], out_specs=[out, out],
                     out_shape=[shape, shape],
                     scratch_shapes=[pltpu.VMEM((FF_SHARD, D_MODEL), F32), pltpu.VMEM((FF_SHARD, D_MODEL), F32)],
                     sem=("parallel", "arbitrary"), args=[dg, du, n])

    if many_calls:
        grads[tag + "_w_gate"] = dw_up(tag + "_dwg", dg)
        grads[tag + "_w_up"] = dw_up(tag + "_dwu", du)
    else:
        grads[tag + "_w_gate"], grads[tag + "_w_up"] = dw_up_pair(tag + "_dwgu")

    bn = _row_block(s, 512)
    steps = s // bn // 2 if many_calls else s // bn
    prev, dgain = (), None
    for part, off in ((("_dn_a", 0), ("_dn_b", steps)) if many_calls else (("_dn", 0),)):
        row = pl.BlockSpec((bn, D_MODEL), lambda i, k, off=off: (i + off, 0))
        terms = []
        for j in range(N_CHIPS):
            a_slab = pl.BlockSpec((None, bn, FF_SHARD), lambda i, k, j=j, off=off: (j, i + off, 0))
            w_slab = pl.BlockSpec((None, FF_SHARD, D_MODEL), lambda i, k, j=j: (j, 0, 0))
            terms += [(dg, a_slab, wg, w_slab, NN), (du, a_slab, wu, w_slab, NN)]
        if norm_bwd is None:
            prev = _matmul(tag + part, (steps, 1), terms, [], [(jax.ShapeDtypeStruct((s, D_MODEL), F32), row)],
                           _ident, None, fill=prev)
            continue
        h, gain, dres = norm_bwd
        vec = pl.BlockSpec((1, D_MODEL), lambda i, k: (0, 0))
        res = _matmul(
            tag + part, (steps, 1), terms, [(h, row), (gain, vec), (dres, row)],
            [(jax.ShapeDtypeStruct((s, D_MODEL), F32), row), (jax.ShapeDtypeStruct((s, D_MODEL), BF16), row),
             (jax.ShapeDtypeStruct((1, D_MODEL), F32), vec)],
            _norm_bwd_epilogue(D_MODEL), None, fill=prev, summed=(2,))
        prev = res[:2]
        dgain = res[2] if dgain is None else dgain + res[2]
    return prev[0] if norm_bwd is None else (prev[0], prev[1], dgain)


def _mm_nn(name, a, b, out_dtype, res=None, gain=None):
    s, k = a.shape
    nn = b.shape[1]
    bm = _row_block(s)
    row = pl.BlockSpec((bm, nn), lambda i, kk: (i, 0))
    term = [(a, pl.BlockSpec((bm, k), lambda i, kk: (i, 0)), b, pl.BlockSpec((k, nn), lambda i, kk: (0, 0)), NN)]
    if res is None:
        return _matmul(name, (s // bm, 1), term, [], [(jax.ShapeDtypeStruct((s, nn), out_dtype), row)], _ident, None)[0]
    extras, outs = _residual_outs(s, bm, gain)
    res_out = _matmul(name, (s // bm, 1), term, [(res, row)] + extras, outs,
                      _residual_epilogue(1.0, gain is not None), None)
    return res_out if gain is not None else res_out[0]


def _mm_nt(name, a, b, out_dtype, attn_out=None, nh=0, dv=0):
    s, nn = a.shape
    k = b.shape[0]
    bm = _row_block(s)
    term = [(a, pl.BlockSpec((bm, nn), lambda i, kk: (i, 0)), b, pl.BlockSpec((k, nn), lambda i, kk: (0, 0)), NT)]
    out = (jax.ShapeDtypeStruct((s, k), out_dtype), pl.BlockSpec((bm, k), lambda i, kk: (i, 0)))
    if attn_out is None:
        return _matmul(name, (s // bm, 1), term, [], [out], _ident, None)[0]

    def with_delta(acc, o):
        do = acc.astype(out_dtype).astype(F32)
        cols = [jnp.sum(do[:, h * dv:(h + 1) * dv] * o[:, h * dv:(h + 1) * dv].astype(F32), axis=-1, keepdims=True)
                for h in range(nh)]
        return acc, jnp.stack(cols, axis=0)

    return _matmul(
        name, (s // bm, 1), term, [(attn_out, pl.BlockSpec((bm, nh * dv), lambda i, kk: (i, 0)))],
        [out, (jax.ShapeDtypeStruct((nh, s, 1), F32), pl.BlockSpec((nh, bm, 1), lambda i, kk: (0, i, 0)))],
        with_delta, None)


def _mm_nt_norm_bwd(name, a, b, h, gain, dres):
    s, nn = a.shape
    bm = _row_block(s, 512)
    extras, outs, summed = _norm_bwd_operands(s, bm, h, gain, dres)
    return _matmul(
        name, (s // bm, 1),
        [(a, pl.BlockSpec((bm, nn), lambda i, kk: (i, 0)), b, pl.BlockSpec(b.shape, lambda i, kk: (0, 0)), NT)],
        extras, outs, _norm_bwd_epilogue(D_MODEL), None, summed=summed)


def _w_in_dx_norm_bwd(dz, w_t, h, gain, dres):
    s = dz.shape[0]
    bm = _row_block(s, 512)
    epilogue = _norm_bwd_epilogue(D_MODEL)

    def body(dz_ref, w_ref, h_ref, g_ref, r_ref, dx_ref, dxb_ref, dg_ref):
        dzv = dz_ref[...]
        dn = jnp.concatenate([_dot(dzv, w_ref[j], NN) for j in range(N_CHIPS)], axis=1)
        dx, _, dgain = epilogue(dn, h_ref[...], g_ref[...], r_ref[...])
        dx_ref[...] = dx
        dxb_ref[...] = dx.astype(BF16)

        @pl.when(pl.program_id(0) == 0)
        def _():
            dg_ref[...] = dgain

        @pl.when(pl.program_id(0) > 0)
        def _():
            dg_ref[...] += dgain

    row = pl.BlockSpec((bm, D_MODEL), lambda i: (i, 0))
    vec = pl.BlockSpec((1, D_MODEL), lambda i: (0, 0))
    return _call(
        body, name="w_in_dx", grid=(s // bm,),
        in_specs=[row, pl.BlockSpec(w_t.shape, lambda i: (0, 0, 0)), row, vec, row],
        out_specs=[row, row, vec],
        out_shape=[jax.ShapeDtypeStruct((s, D_MODEL), F32), jax.ShapeDtypeStruct((s, D_MODEL), BF16),
                   jax.ShapeDtypeStruct((1, D_MODEL), F32)],
        sem=("arbitrary",), args=[dz, w_t, h, gain, dres])


def _mm_tn_stacked(name, a_list, b):
    s, nn = b.shape
    widths = [a.shape[1] for a in a_list]
    total = sum(widths)
    bk = _reduce_block(s)
    nk = s // bk
    na = len(a_list)

    def body(*refs):
        a_refs, b_ref, o_ref, acc_ref = refs[:na], refs[na], refs[na + 1], refs[na + 2]
        k = pl.program_id(0)

        @pl.when(k == 0)
        def _():
            acc_ref[...] = jnp.zeros_like(acc_ref)

        bv = b_ref[...]
        lo = 0
        for a_ref, w in zip(a_refs, widths):
            acc_ref[lo:lo + w, :] += _dot(a_ref[...], bv, TN)
            lo += w

        @pl.when(k == nk - 1)
        def _():
            o_ref[...] = acc_ref[...].astype(o_ref.dtype)

    return _call(
        body, name=name, grid=(nk,),
        in_specs=[pl.BlockSpec((bk, w), lambda k: (k, 0)) for w in widths] + [pl.BlockSpec((bk, nn), lambda k: (k, 0))],
        out_specs=[pl.BlockSpec((total, nn), lambda k: (0, 0))],
        out_shape=[jax.ShapeDtypeStruct((total, nn), BF16)],
        scratch_shapes=[pltpu.VMEM((total, nn), F32)], sem=("arbitrary",), args=list(a_list) + [b])[0]


def _mm_tn_pairs(name, pairs):
    s = pairs[0][0].shape[0]
    bk = _row_block(s)
    nk = s // bk
    npairs = len(pairs)
    shapes = [(a.shape[1], b.shape[1]) for a, b in pairs]

    def body(*refs):
        ins, outs, accs = refs[:2 * npairs], refs[2 * npairs:3 * npairs], refs[3 * npairs:]
        k = pl.program_id(0)

        @pl.when(k == 0)
        def _():
            for acc in accs:
                acc[...] = jnp.zeros_like(acc)

        for p in range(npairs):
            accs[p][...] += _dot(ins[2 * p][...], ins[2 * p + 1][...], TN)

        @pl.when(k == nk - 1)
        def _():
            for o, acc in zip(outs, accs):
                o[...] = acc[...].astype(o.dtype)

    in_specs = [pl.BlockSpec((bk, x.shape[1]), lambda k: (k, 0)) for pair in pairs for x in pair]
    return _call(
        body, name=name, grid=(nk,), in_specs=in_specs,
        out_specs=[pl.BlockSpec(shp, lambda k: (0, 0)) for shp in shapes],
        out_shape=[jax.ShapeDtypeStruct(shp, BF16) for shp in shapes],
        scratch_shapes=[pltpu.VMEM(shp, F32) for shp in shapes], sem=("arbitrary",),
        args=[x for pair in pairs for x in pair])


def _mm_heads_fwd(name, a, w, out_dtype, w_transposed=False):
    s, k = a.shape
    nh = w.shape[0]
    nn = w.shape[1] if w_transposed else w.shape[2]
    bm = _row_block(s)
    return _matmul(
        name, (nh, s // bm, 1),
        [(a, pl.BlockSpec((bm, k), lambda h, i, kk: (i, 0)),
          w, pl.BlockSpec((None,) + w.shape[1:], lambda h, i, kk: (h, 0, 0)), NT if w_transposed else NN)],
        [], [(jax.ShapeDtypeStruct((s, nh * nn), out_dtype), pl.BlockSpec((bm, nn), lambda h, i, kk: (i, h)))],
        _ident, None)[0]


def _mm_heads_bwd(name, dy, a, w, w_transposed=False):
    s, k = a.shape
    nh = w.shape[0]
    nn = w.shape[1] if w_transposed else w.shape[2]
    bm = _row_block(s)
    bk = _reduce_block(s)
    w_spec = pl.BlockSpec((None,) + w.shape[1:], lambda i, h: (h, 0, 0))
    da = _matmul(
        name + "_dx", (s // bm, nh),
        [(dy, pl.BlockSpec((bm, nn), lambda i, h: (i, h)), w, w_spec, NN if w_transposed else NT)],
        [], [(jax.ShapeDtypeStruct((s, k), F32), pl.BlockSpec((bm, k), lambda i, h: (i, 0)))], _ident, (bm, k))[0]
    a_term = (a, pl.BlockSpec((bk, k), lambda h, kk: (kk, 0)))
    dy_term = (dy, pl.BlockSpec((bk, nn), lambda h, kk: (kk, h)))
    lhs, rhs = (dy_term, a_term) if w_transposed else (a_term, dy_term)
    dw = _matmul(
        name + "_dw", (nh, s // bk), [lhs + rhs + (TN,)],
        [], [(jax.ShapeDtypeStruct(w.shape, BF16), pl.BlockSpec((None,) + w.shape[1:], lambda h, kk: (h, 0, 0)))],
        _ident, w.shape[1:])[0]
    return da, dw


def _w_in_fwd(n, w_t):
    s = n.shape[0]
    bm = _row_block(s)
    nh, nout, kin = w_t.shape
    terms = [(n, pl.BlockSpec((bm, kin), lambda i, k, j=j: (i, j)),
              w_t, pl.BlockSpec((None, nout, kin), lambda i, k, j=j: (j, 0, 0)), NT) for j in range(nh)]
    row = pl.BlockSpec((bm, nout), lambda i, k: (i, 0))
    return _matmul("w_in", (s // bm, 1), terms, [], [(jax.ShapeDtypeStruct((s, nout), F32), row)], _ident, None)[0]


def _w_in_dw(dz, n):
    s, nout = dz.shape
    kin = n.shape[1] // N_CHIPS
    bk = _reduce_block(s)
    return _matmul(
        "w_in_dw", (N_CHIPS, s // bk),
        [(dz, pl.BlockSpec((bk, nout), lambda j, k: (k, 0)), n, pl.BlockSpec((bk, kin), lambda j, k: (k, j)), TN)],
        [], [(jax.ShapeDtypeStruct((N_CHIPS, nout, kin), BF16), pl.BlockSpec((None, nout, kin), lambda j, k: (j, 0, 0)))],
        _ident, (nout, kin))[0]


def _rope_tables(positions):
    half = ROPE_DIM // 2
    freqs = 1.0 / (ROPE_BASE ** (jnp.arange(0, ROPE_DIM, 2, dtype=F32) / ROPE_DIM))
    ang = positions.astype(F32)[:, None] * freqs
    cos, sin = jnp.cos(ang), jnp.sin(ang)
    z = jnp.zeros_like(cos)
    tc = jnp.concatenate([cos, cos, z, z], axis=-1)
    ta = jnp.concatenate([-sin, z, z, z], axis=-1)
    tb = jnp.concatenate([z, sin, z, z], axis=-1)
    assert tc.shape[-1] == 4 * half
    return tc, ta, tb


def _rope(x, tc, ta, tb):
    return x * tc + pltpu.roll(x, 96, 1) * ta + pltpu.roll(x, 32, 1) * tb


def _rope_t(dy, tc, ta, tb):
    return dy * tc + pltpu.roll(dy * ta, 32, 1) + pltpu.roll(dy * tb, 96, 1)


def _norm_bf16(x, g):
    r = lax.rsqrt(jnp.mean(x * x, axis=-1, keepdims=True) + RMS_EPS)
    return ((x * r) * g).astype(BF16)


def _qkv_prep(z, q_gain, kv_gain, wq_t, wkv, tc, ta, tb):
    s = z.shape[0]
    bm = _row_block(s, 512)

    def body(zq_ref, zkv_ref, zkr_ref, qg_ref, kvg_ref, wq_ref, wkv_ref, tc_ref, ta_ref, tb_ref,
             qn_ref, kvn_ref, q_ref, k_ref, v_ref):
        c, a, b = tc_ref[...], ta_ref[...], tb_ref[...]
        qn = _norm_bf16(zq_ref[...], qg_ref[...])
        kvn = _norm_bf16(zkv_ref[...], kvg_ref[...])
        qn_ref[...] = qn
        kvn_ref[...] = kvn
        kpe = _rope(zkr_ref[...], c, a, b).astype(BF16)
        for h in range(MLA_HEADS):
            lo = h * HEAD_QK
            qp = _dot(qn, wq_ref[h], NT)
            q_ref[:, lo:lo + 128] = qp[:, :128].astype(BF16)
            q_ref[:, lo + 128:lo + 256] = _rope(qp[:, 128:], c, a, b).astype(BF16)
            kv = _dot(kvn, wkv_ref[h], NN)
            k_ref[:, lo:lo + 128] = kv[:, :128].astype(BF16)
            k_ref[:, lo + 128:lo + 256] = kpe
            v_ref[:, h * HEAD_V:(h + 1) * HEAD_V] = kv[:, 128:].astype(BF16)

    def cols(width, blk):
        return pl.BlockSpec((bm, width), lambda i: (i, blk))

    def whole(a):
        return pl.BlockSpec(a.shape, lambda i: (0,) * a.ndim)

    tab = cols(128, 0)
    return _call(
        body, name="qkv_prep", grid=(s // bm,),
        in_specs=[cols(Q_LORA, 0), cols(KV_LORA, 2), cols(128, 3), whole(q_gain), whole(kv_gain), whole(wq_t),
                  whole(wkv), tab, tab, tab],
        out_specs=[cols(Q_LORA, 0), cols(KV_LORA, 0), cols(1024, 0), cols(1024, 0), cols(512, 0)],
        out_shape=[jax.ShapeDtypeStruct((s, Q_LORA), BF16), jax.ShapeDtypeStruct((s, KV_LORA), BF16),
                   jax.ShapeDtypeStruct((s, 1024), BF16), jax.ShapeDtypeStruct((s, 1024), BF16),
                   jax.ShapeDtypeStruct((s, 512), BF16)],
        sem=("parallel",), args=[z, z, z, q_gain, kv_gain, wq_t, wkv, tc, ta, tb])


def _qkv_prep_bwd(dq, dk, dv, z, qn, kvn, q_gain, kv_gain, wq_t, wkv, tc, ta, tb):
    s = z.shape[0]
    bm = _row_block(s, 512)
    nsteps = s // bm

    def body(dq_ref, dk_ref, dv_ref, zq_ref, zkv_ref, qn_ref, kvn_ref, qg_ref, kvg_ref, wq_ref, wkv_ref,
             tc_ref, ta_ref, tb_ref, dz_ref, dqg_ref, dkvg_ref, dwq_ref, dwkv_ref, wq_acc, wkv_acc):
        i = pl.program_id(0)
        c, a, b = tc_ref[...], ta_ref[...], tb_ref[...]

        @pl.when(i == 0)
        def _():
            wq_acc[...] = jnp.zeros_like(wq_acc)
            wkv_acc[...] = jnp.zeros_like(wkv_acc)

        qn, kvn = qn_ref[...], kvn_ref[...]
        dqn = jnp.zeros((bm, Q_LORA), F32)
        dkvn = jnp.zeros((bm, KV_LORA), F32)
        dpe = jnp.zeros((bm, 128), F32)
        for h in range(MLA_HEADS):
            lo = h * HEAD_QK
            dqp = jnp.concatenate([dq_ref[:, lo:lo + 128],
                                   _rope_t(dq_ref[:, lo + 128:lo + 256].astype(F32), c, a, b).astype(BF16)], axis=1)
            dqn = dqn + _dot(dqp, wq_ref[h], NN)
            wq_acc[h] += _dot(dqp, qn, TN)
            dkv = jnp.concatenate([dk_ref[:, lo:lo + 128], dv_ref[:, h * HEAD_V:(h + 1) * HEAD_V]], axis=1)
            dkvn = dkvn + _dot(dkv, wkv_ref[h], NT)
            wkv_acc[h] += _dot(kvn, dkv, TN)
            dpe = dpe + dk_ref[:, lo + 128:lo + 256].astype(F32)
        dcq, dqg = _rms_bwd_math(dqn, zq_ref[...], qg_ref[...], Q_LORA)
        dckv, dkvg = _rms_bwd_math(dkvn, zkv_ref[...], kvg_ref[...], KV_LORA)
        dz_ref[:, 0:Q_LORA] = dcq.astype(BF16)
        dz_ref[:, Q_LORA:Q_LORA + KV_LORA] = dckv.astype(BF16)
        dz_ref[:, Q_LORA + KV_LORA:512] = _rope_t(dpe, c, a, b).astype(BF16)

        @pl.when(i == 0)
        def _():
            dqg_ref[...] = dqg
            dkvg_ref[...] = dkvg

        @pl.when(i > 0)
        def _():
            dqg_ref[...] += dqg
            dkvg_ref[...] += dkvg

        @pl.when(i == nsteps - 1)
        def _():
            dwq_ref[...] = wq_acc[...].astype(BF16)
            dwkv_ref[...] = wkv_acc[...].astype(BF16)

    def cols(width, blk):
        return pl.BlockSpec((bm, width), lambda i: (i, blk))

    def whole(shape):
        return pl.BlockSpec(shape, lambda i: (0,) * len(shape))

    tab = cols(128, 0)
    return _call(
        body, name="qkv_prep_bwd", grid=(nsteps,),
        in_specs=[cols(1024, 0), cols(1024, 0), cols(512, 0), cols(Q_LORA, 0), cols(KV_LORA, 2), cols(Q_LORA, 0),
                  cols(KV_LORA, 0), whole(q_gain.shape), whole(kv_gain.shape), whole(wq_t.shape), whole(wkv.shape),
                  tab, tab, tab],
        out_specs=[cols(512, 0), whole(q_gain.shape), whole(kv_gain.shape), whole(wq_t.shape), whole(wkv.shape)],
        out_shape=[jax.ShapeDtypeStruct((s, 512), BF16), jax.ShapeDtypeStruct(q_gain.shape, F32),
                   jax.ShapeDtypeStruct(kv_gain.shape, F32), jax.ShapeDtypeStruct(wq_t.shape, BF16),
                   jax.ShapeDtypeStruct(wkv.shape, BF16)],
        scratch_shapes=[pltpu.VMEM(wq_t.shape, F32), pltpu.VMEM(wkv.shape, F32)],
        sem=("arbitrary",), args=[dq, dk, dv, z, z, qn, kvn, q_gain, kv_gain, wq_t, wkv, tc, ta, tb])


def _causal_mask(s, row0, col0):
    rows = row0 + lax.broadcasted_iota(jnp.int32, s.shape, 0)
    cols = col0 + lax.broadcasted_iota(jnp.int32, s.shape, 1)
    return jnp.where(cols <= rows, s, -jnp.inf)


def _attn_fwd(name, q, k, k_off, v, v_off, nh, dq, dv, scale, causal, blk):
    sq, sk = q.shape[0], k.shape[0]
    bq = min(blk, sq)
    bk = min(blk, sk)
    nkv = sk // bk
    assert not causal or (sq == sk and bq == bk)

    hq = bq
    log2e = 1.4426950408889634
    c2 = scale * log2e

    def body(q_ref, k_ref, v_ref, o_ref, lse_ref):
        qi = pl.program_id(1)
        qs = (q_ref[...],)

        def step(j, carry, masked):
            rows = pl.ds(pl.multiple_of(j * bk, bk), bk)
            kb, vb = k_ref[rows, :], v_ref[rows, :]
            out = []
            for t, (m, l, acc) in enumerate(carry):
                s = _dot(qs[t], kb, NT) * c2
                if masked:
                    s = _causal_mask(s, qi * bq + t * hq, j * bk)
                m_new = jnp.maximum(m, jnp.max(s, axis=-1, keepdims=True))
                alpha = jnp.exp2(m - m_new)
                p = jnp.exp2(s - m_new)
                l = alpha * l + jnp.sum(p, axis=-1, keepdims=True)
                acc = alpha * acc + _dot(p, vb, NN)
                out.append((m_new, l, acc))
            return tuple(out)

        one = (jnp.full((hq, 1), -jnp.inf, F32), jnp.zeros((hq, 1), F32), jnp.zeros((hq, dv), F32))
        init = (one,)
        if causal:
            carry = lax.fori_loop(0, qi, lambda j, c: step(j, c, False), init)
            fin = step(qi, carry, True)
        else:
            fin = lax.fori_loop(0, nkv, lambda j, c: step(j, c, False), init)
        for t, (m, l, acc) in enumerate(fin):
            o_ref[t * hq:(t + 1) * hq, :] = (acc / l).astype(o_ref.dtype)
            lse_ref[t * hq:(t + 1) * hq, :] = m * (1.0 / log2e) + jnp.log(l)

    return _call(
        body, name=name, grid=(nh, sq // bq),
        in_specs=[pl.BlockSpec((bq, dq), lambda h, i: (i, h)),
                  pl.BlockSpec((sk, dq), lambda h, i: (0, k_off + h)),
                  pl.BlockSpec((sk, dv), lambda h, i: (0, v_off + h))],
        out_specs=[pl.BlockSpec((bq, dv), lambda h, i: (i, h)), pl.BlockSpec((None, bq, 1), lambda h, i: (h, i, 0))],
        out_shape=[jax.ShapeDtypeStruct((sq, nh * dv), BF16), jax.ShapeDtypeStruct((nh, sq, 1), F32)],
        sem=("parallel", "parallel"), args=[q, k, v])


def _attn_bwd(name, q, k, k_off, v, v_off, do, do_off, lse, delta, nh, dq, dv, scale, causal, blk):
    sq, sk = q.shape[0], k.shape[0]
    bq = min(blk, sq)
    bk = min(blk, sk)
    nq = sq // bq
    assert not causal or (sq == sk and bq == bk)

    nkv = sk // bk

    def body(q_ref, k_ref, v_ref, do_ref, lse_ref, dl_ref, dq_ref, dk_ref, dv_ref, dq_acc, dk_acc, dv_acc):
        j = pl.program_id(1)

        @pl.when(j == 0)
        def _():
            dq_acc[...] = jnp.zeros_like(dq_acc)

        dk_acc[...] = jnp.zeros_like(dk_acc)
        dv_acc[...] = jnp.zeros_like(dv_acc)
        kv = k_ref[...]
        vv = v_ref[...]

        def step(i, masked):
            rows = pl.ds(pl.multiple_of(i * bq, bq), bq)
            qv = q_ref[rows, :]
            dov = do_ref[rows, :].astype(BF16)
            s = _dot(qv, kv, NT) * scale
            if masked:
                s = _causal_mask(s, i * bq, j * bk)
            p = jnp.exp(s - lse_ref[rows, :])
            dp = _dot(dov, vv, NT)
            ds = (p * (dp - dl_ref[rows, :]) * scale).astype(BF16)
            dv_acc[...] += _dot(p, dov, TN)
            dk_acc[...] += _dot(ds, qv, TN)
            dq_acc[rows, :] += _dot(ds, kv, NN)

        if causal:
            step(j, True)

            def loop(i, c):
                step(i, False)
                return c

            lax.fori_loop(j + 1, nq, loop, 0)
        else:
            def loop(i, c):
                step(i, False)
                return c

            lax.fori_loop(0, nq, loop, 0)
        dk_ref[...] = dk_acc[...].astype(dk_ref.dtype)
        dv_ref[...] = dv_acc[...].astype(dv_ref.dtype)

        @pl.when(j == nkv - 1)
        def _():
            dq_ref[...] = dq_acc[...].astype(dq_ref.dtype)

    stat = pl.BlockSpec((None, sq, 1), lambda h, j: (h, 0, 0))
    return _call(
        body, name=name, grid=(nh, sk // bk),
        in_specs=[pl.BlockSpec((sq, dq), lambda h, j: (0, h)),
                  pl.BlockSpec((bk, dq), lambda h, j: (j, k_off + h)),
                  pl.BlockSpec((bk, dv), lambda h, j: (j, v_off + h)),
                  pl.BlockSpec((sq, dv), lambda h, j: (0, do_off + h)), stat, stat],
        out_specs=[pl.BlockSpec((sq, dq), lambda h, j: (0, h)),
                   pl.BlockSpec((bk, dq), lambda h, j: (j, h)),
                   pl.BlockSpec((bk, dv), lambda h, j: (j, h))],
        out_shape=[jax.ShapeDtypeStruct((sq, nh * dq), BF16), jax.ShapeDtypeStruct((sk, nh * dq), BF16),
                   jax.ShapeDtypeStruct((sk, nh * dv), BF16)],
        scratch_shapes=[pltpu.VMEM((sq, dq), F32), pltpu.VMEM((bk, dq), F32), pltpu.VMEM((bk, dv), F32)],
        sem=("parallel", "arbitrary"), args=[q, k, v, do, lse, delta])


def _pool_diff(z, g):
    s = z.shape[0]
    t = lax.broadcasted_iota(jnp.int32, z.shape, 0)
    acc = z
    sums = []
    for k in (1, 2, 4, 8):
        acc = acc + jnp.where(t >= k, pltpu.roll(acc, k, 0), 0.0)
        sums.append(acc)
    win = jnp.where(g == 0, sums[0], jnp.where(g == 1, sums[1], jnp.where(g == 2, sums[2], sums[3])))
    w = lax.shift_left(jnp.int32(2), g)
    count = jnp.minimum(t + 1, w).astype(F32)
    del s
    return win / count - z, count


def _pool_fwd(z, pool_w, pool_scale):
    s = z.shape[0]

    def body(z_ref, w_ref, sc_ref, o_ref):
        diff, _ = _pool_diff(z_ref[...], pl.program_id(0))
        o_ref[...] = (_dot(diff, w_ref[...], NN) * sc_ref[...]).astype(o_ref.dtype)

    return _call(
        body, name="pool_fwd", grid=(POOL_GROUPS,),
        in_specs=[pl.BlockSpec((s, POOL_CH), lambda g: (0, 4 + g)),
                  pl.BlockSpec((None, POOL_CH, POOL_CH), lambda g: (g, 0, 0)),
                  pl.BlockSpec((1, POOL_CH), lambda g: (0, g))],
        out_specs=[pl.BlockSpec((s, POOL_CH), lambda g: (0, g))],
        out_shape=[jax.ShapeDtypeStruct((s, POOL_GROUPS * POOL_CH), BF16)],
        sem=("parallel",), args=[z, pool_w, pool_scale])[0]


def _pool_bwd(dcat, z, pool_w, pool_scale):
    s = z.shape[0]

    def body(dp_ref, z_ref, w_ref, sc_ref, dz_ref, dw_ref, dsc_ref):
        g = pl.program_id(0)
        diff, count = _pool_diff(z_ref[...], g)
        dpf = dp_ref[...].astype(F32)
        u = _dot(diff, w_ref[...], NN)
        dsc_ref[...] = jnp.sum(dpf * u, axis=0, keepdims=True)
        du = (dpf * sc_ref[...]).astype(BF16)
        dw_ref[...] = _dot(diff, du, TN)
        ddiff = _dot(du, w_ref[...], NT)
        t = lax.broadcasted_iota(jnp.int32, ddiff.shape, 0)
        acc = ddiff / count
        sums = []
        for k in (1, 2, 4, 8):
            acc = acc + jnp.where(t < s - k, pltpu.roll(acc, s - k, 0), 0.0)
            sums.append(acc)
        win = jnp.where(g == 0, sums[0], jnp.where(g == 1, sums[1], jnp.where(g == 2, sums[2], sums[3])))
        dz_ref[...] = (win - ddiff).astype(dz_ref.dtype)

    return pl.pallas_call(
        body, name="pool_bwd", grid=(POOL_GROUPS,),
        in_specs=[pl.BlockSpec((s, POOL_CH), lambda g: (0, 4 + g)),
                  pl.BlockSpec((s, POOL_CH), lambda g: (0, 4 + g)),
                  pl.BlockSpec((None, POOL_CH, POOL_CH), lambda g: (g, 0, 0)),
                  pl.BlockSpec((1, POOL_CH), lambda g: (0, g))],
        out_specs=[pl.BlockSpec((s, POOL_CH), lambda g: (0, g)),
                   pl.BlockSpec((None, POOL_CH, POOL_CH), lambda g: (g, 0, 0)),
                   pl.BlockSpec((1, POOL_CH), lambda g: (0, g))],
        out_shape=[jax.ShapeDtypeStruct((s, POOL_GROUPS * POOL_CH), BF16),
                   jax.ShapeDtypeStruct((POOL_GROUPS, POOL_CH, POOL_CH), F32),
                   jax.ShapeDtypeStruct((1, POOL_GROUPS * POOL_CH), F32)],
        compiler_params=_params("parallel"),
    )(dcat, z, pool_w, pool_scale)


def _local_step(x, mem, positions, target, w, grads):
    tc, ta, tb = _rope_tables(positions)
    blk = _ATT_BLOCK

    if "ffn1_shards" in w:
        n1, a1, dadu1, dadg1, w["ffn1_w_gate"], w["ffn1_w_up"], w["ffn1_w_down"] = _ffn1_up_gather(
            x, w["ffn1_norm"], *w["ffn1_shards"])
    else:
        n1 = _rmsnorm_fwd("ffn1_norm", x, w["ffn1_norm"], D_MODEL)
        a1, dadu1, dadg1 = _ffn_up("ffn1_up", n1, w["ffn1_w_gate"], w["ffn1_w_up"])
    h1, n2 = _ffn_down("ffn1_down", a1, w["ffn1_w_down"], x, w["mix_norm"])
    z = _w_in_fwd(n2, w["w_in"])
    qn, kvn, qf, kf, vf = _qkv_prep(z, w["q_norm"], w["kv_norm"], w["w_q_up"], w["w_kv_up"], tc, ta, tb)
    att, lse = _attn_fwd("mla_fwd", qf, kf, 0, vf, 0, MLA_HEADS, HEAD_QK, HEAD_V, MLA_SCALE, True, blk)
    pool = _pool_fwd(z, w["pool_w"], w["pool_scale"])
    s = x.shape[0]
    bm = _row_block(s)
    row = pl.BlockSpec((bm, D_MODEL), lambda i, k: (i, 0))
    half = pl.BlockSpec((bm, 512), lambda i, k: (i, 0))
    h2, n3 = _matmul(
        "w_out", (s // bm, 1),
        [(att, half, w["w_out"], pl.BlockSpec((512, D_MODEL), lambda i, k: (0, 0)), NN),
         (pool, half, w["w_out"], pl.BlockSpec((512, D_MODEL), lambda i, k: (1, 0)), NN)],
        [(h1, row)] + _residual_outs(s, bm, w["xattn_norm"])[0], _residual_outs(s, bm, w["xattn_norm"])[1],
        _residual_epilogue(1.0, True), None)
    memn = _rmsnorm_fwd("mem_norm", mem, w["mem_norm"], D_MODEL)
    qm = _mm_nn("w_mq", n3, w["w_mq"], BF16)
    kvm = _mm_heads_fwd("w_mkv", memn, w["w_mkv"], BF16)
    om, lse_m = _attn_fwd("xattn_fwd", qm, kvm, 0, kvm, MEM_HEADS, MEM_HEADS, MEM_HEAD_DIM, MEM_HEAD_DIM,
                          MEM_SCALE, False, blk)
    h3, n4 = _mm_nn("w_mo", om, w["w_mo"], F32, res=h2, gain=w["ffn2_norm"])
    a2, dadu2, dadg2 = _ffn_up("ffn2_up", n4, w["ffn2_w_gate"], w["ffn2_w_up"])
    dh4, dh4b, loss_vec, d_final = _ffn_down("ffn2_down", a2, w["ffn2_w_down"], h3, loss=(w["final_norm"], target))
    grads["final_norm"] = d_final

    dh3, dh3b, grads["ffn2_norm"] = _ffn_bwd("ffn2", dh4b, n4, dadg2, dadu2, a2, w["ffn2_w_gate"], w["ffn2_w_up"],
                                             w["ffn2_w_down"], grads, norm_bwd=(h3, w["ffn2_norm"], dh4),
                                             many_calls=False)

    dom, delta_m = _mm_nt("w_mo_dx", dh3b, w["w_mo"], BF16, attn_out=om, nh=MEM_HEADS, dv=MEM_HEAD_DIM)
    dqm, dkm, dvm = _attn_bwd("xattn_bwd", qm, kvm, 0, kvm, MEM_HEADS, dom, 0, lse_m, delta_m, MEM_HEADS,
                              MEM_HEAD_DIM, MEM_HEAD_DIM, MEM_SCALE, False, blk)
    dkvm = jnp.concatenate([dkm, dvm], axis=1)
    dh2, dh2b, grads["xattn_norm"] = _mm_nt_norm_bwd("w_mq_dx", dqm, w["w_mq"], h2, w["xattn_norm"], dh3)
    grads["w_mo"], grads["w_mq"] = _mm_tn_pairs("w_mo_mq_dw", [(om, dh3b), (n3, dqm)])
    dmemn, grads["w_mkv"] = _mm_heads_bwd("w_mkv", dkvm, memn, w["w_mkv"])
    _, grads["mem_norm"] = _rmsnorm_bwd("mem_norm_bwd", dmemn, mem, w["mem_norm"], D_MODEL, out_dtype=BF16)

    dcat, delta = _mm_nt("w_out_dx", dh2b, w["w_out"], BF16, attn_out=att, nh=MLA_HEADS, dv=HEAD_V)
    grads["w_out"] = _mm_tn_stacked("w_out_dw", [att, pool], dh2b)
    dzp, grads["pool_w"], grads["pool_scale"] = _pool_bwd(dcat, z, w["pool_w"], w["pool_scale"])
    dqf, dkf, dvf = _attn_bwd("mla_bwd", qf, kf, 0, vf, 0, dcat, 0, lse, delta, MLA_HEADS, HEAD_QK, HEAD_V,
                              MLA_SCALE, True, blk)
    dz_lat, grads["q_norm"], grads["kv_norm"], grads["w_q_up"], grads["w_kv_up"] = _qkv_prep_bwd(
        dqf, dkf, dvf, z, qn, kvn, w["q_norm"], w["kv_norm"], w["w_q_up"], w["w_kv_up"], tc, ta, tb)
    dz = jnp.concatenate([dz_lat, dzp], axis=1)
    grads["w_in"] = _w_in_dw(dz, n2)
    dh1, dh1b, grads["mix_norm"] = _w_in_dx_norm_bwd(dz, w["w_in"], h1, w["mix_norm"], dh2)

    dn1 = _ffn_bwd("ffn1", dh1b, n1, dadg1, dadu1, a1, w["ffn1_w_gate"], w["ffn1_w_up"], w["ffn1_w_down"], grads)
    dx, grads["ffn1_norm"] = _rmsnorm_bwd("ffn1_norm_bwd", dn1, x, w["ffn1_norm"], D_MODEL, dres=dh1)
    return loss_vec[0, 0], dx


def _mesh_pos():
    x, y, c = lax.axis_index("x"), lax.axis_index("y"), lax.axis_index("c")
    chips = [(1 - x, y), (x, 1 - y), (1 - x, 1 - y)]
    chip_ids = [2 * cx + cy for cx, cy in chips]
    return x, y, c, 2 * x + y, chips, chip_ids


def _half_rows(c, rows):
    hr = rows // 2
    return pl.ds(pl.multiple_of(c * hr, 16), hr), pl.ds(pl.multiple_of((1 - c) * hr, 16), hr)


def _ag_ici_stage(shards):
    n = len(shards)

    def copies(ins, outs):
        x, y, c, me, chips, _ = _mesh_pos()
        out = []
        for k in range(n):
            mine, _ = _half_rows(c, ins[k].shape[0])
            out.append((ins[k], outs[k].at[me], None))
            for cx, cy in chips:
                out.append((ins[k].at[mine], outs[k].at[me, mine], (cx, cy, c)))
        return out

    return _Stage(shards, [jax.ShapeDtypeStruct((N_CHIPS,) + s.shape, s.dtype) for s in shards], 3 * n, n, copies)


def _quarter_rows(c, rows):
    qr = rows // 4
    return pl.ds(pl.multiple_of(c * 2 * qr, 16), qr), pl.ds(pl.multiple_of(c * 2 * qr + qr, 16), qr)


def _ag_d2d_stage(fulls):
    n = len(fulls)

    def copies(ins, outs):
        x, y, c, me, _, chip_ids = _mesh_pos()
        out = []
        for k in range(n):
            mine, _ = _half_rows(c, ins[k].shape[1])
            for j in range(3):
                out.append((ins[k].at[chip_ids[j], mine], outs[k].at[chip_ids[j], mine], (x, y, 1 - c)))
        return out

    return _Stage(fulls, [jax.ShapeDtypeStruct(f.shape, f.dtype) for f in fulls], 3 * n, 0, copies,
                  aliases={k: k for k in range(n)})


def _rs_swap_stage(grads):
    n = len(grads)

    def copies(ins, outs):
        x, y, c, _, _, _ = _mesh_pos()
        out = []
        for k in range(n):
            _, other = _half_rows(c, ins[k].shape[1])
            out.append((ins[k].at[:, other, :], outs[k], (x, y, 1 - c)))
        return out

    return _Stage(grads, [jax.ShapeDtypeStruct((N_CHIPS, g.shape[1] // 2, g.shape[2]), g.dtype) for g in grads],
                  n, 0, copies)


_REL_OF_PEER = (2, 1, 3)


def _rs_scatter_stage(sums, relative=False):
    n = len(sums)

    def copies(ins, outs):
        x, y, c, me, chips, chip_ids = _mesh_pos()
        out = []
        for k in range(n):
            mine, _ = _half_rows(c, 2 * ins[k].shape[1])
            out.append((ins[k].at[0 if relative else me], outs[k].at[0, mine, :], None))
            for j, (cx, cy) in enumerate(chips):
                slab = _REL_OF_PEER[j] if relative else chip_ids[j]
                out.append((ins[k].at[slab], outs[k].at[1 + j, mine, :], (cx, cy, c)))
        return out

    return _Stage(sums, [jax.ShapeDtypeStruct((N_CHIPS, 2 * s.shape[1], s.shape[2]), s.dtype) for s in sums],
                  3 * n, n, copies)


def _rs_mirror_stage(parts):
    n = len(parts)

    def copies(ins, outs):
        x, y, c, _, _, _ = _mesh_pos()
        out = []
        for k in range(n):
            mine, _ = _half_rows(c, ins[k].shape[1])
            out.append((ins[k].at[:, mine, :], outs[k].at[:, mine, :], (x, y, 1 - c)))
        return out

    return _Stage(parts, [jax.ShapeDtypeStruct(p.shape, p.dtype) for p in parts], n, 0, copies,
                  aliases={k: k for k in range(n)})


def _pair_add(name, gs, r1s, core):
    n = len(gs)

    def body(c_ref, *refs):
        for k in range(n):
            g_ref, r_ref, o_ref = refs[k], refs[n + k], refs[2 * n + k]
            o_ref[...] = (g_ref[...].astype(F32) + r_ref[...].astype(F32)).astype(BF16)

    def half(g):
        return pl.BlockSpec((None, g.shape[1] // 2, g.shape[2]), lambda j, c: (j, 0, 0))

    def mine(g):
        return pl.BlockSpec((None, g.shape[1] // 2, g.shape[2]), lambda j, c: (j, c[0], 0))

    return pl.pallas_call(
        body, name=name,
        grid_spec=pltpu.PrefetchScalarGridSpec(
            num_scalar_prefetch=1, grid=(N_CHIPS,),
            in_specs=[mine(g) for g in gs] + [half(g) for g in gs], out_specs=[half(g) for g in gs]),
        out_shape=[jax.ShapeDtypeStruct((N_CHIPS, g.shape[1] // 2, g.shape[2]), BF16) for g in gs],
        compiler_params=_params("parallel"),
    )(core, *gs, *r1s)


def _adamw_math(w, g, m, v):
    m = ADAM_B1 * m + (1.0 - ADAM_B1) * g
    v = ADAM_B2 * v + (1.0 - ADAM_B2) * (g * g)
    m_hat = m / (1.0 - ADAM_B1 ** ADAM_STEP)
    v_hat = v / (1.0 - ADAM_B2 ** ADAM_STEP)
    delta = -ADAM_LR * (m_hat / (jnp.sqrt(v_hat) + ADAM_EPS) + ADAM_WD * w)
    return delta, m, v


def _adamw_sum(name, ws, parts, ms, vs):
    n = len(ws)
    r, c = ws[0].shape
    assert all(w.shape == (r, c) for w in ws)
    br = r
    while br * c * 4 > (1 << 20) and br % 32 == 0:
        br //= 2

    def body(*refs):
        for k in range(n):
            w_ref, p_ref, m_ref, v_ref = refs[4 * k:4 * k + 4]
            g_ref, d_ref, nm_ref, nv_ref = refs[4 * n + 4 * k:4 * n + 4 * k + 4]
            g = p_ref[0].astype(F32)
            for j in range(1, N_CHIPS):
                g = g + p_ref[j].astype(F32)
            d, nm, nv = _adamw_math(w_ref[...], g, m_ref[...], v_ref[...])
            g_ref[...] = g
            d_ref[...] = d
            nm_ref[...] = nm
            nv_ref[...] = nv

    spec = pl.BlockSpec((br, c), lambda i: (i, 0))
    shp = jax.ShapeDtypeStruct((r, c), F32)
    args = [a for k in range(n) for a in (ws[k], parts[k], ms[k], vs[k])]
    res = _call(
        body, name=name, grid=(r // br,),
        in_specs=[spec, pl.BlockSpec((N_CHIPS, br, c), lambda i: (0, i, 0)), spec, spec] * n,
        out_specs=[spec] * (4 * n), out_shape=[shp] * (4 * n), sem=("parallel",), args=args)
    return [res[4 * k:4 * k + 4] for k in range(n)]


_SMALL_VECTORS = ("ffn1_norm", "mix_norm", "xattn_norm", "mem_norm", "ffn2_norm", "final_norm", "q_norm",
                  "kv_norm", "pool_scale")


_LOSS_ROW = 9
_VEC_ROWS = 16
_POOL_ROWS = POOL_GROUPS * POOL_CH


def _small_params_step(g, w, m, v, loss_local):
    names = list(_SMALL_VECTORS) + ["pool_w"]
    nv = len(_SMALL_VECTORS)
    widths = [g[n].shape[1] for n in _SMALL_VECTORS]
    shapes = {"vec": (_VEC_ROWS, D_MODEL), "pool": (_POOL_ROWS, POOL_CH)}

    def body(*refs):
        ins = refs[:4 * (nv + 1) + 1]
        outs = refs[len(ins):len(ins) + 4 * (nv + 1) + 1]
        vec_own, vec_sib, vec_all, pool_sib, pool_sum, pool_all, send, recv = refs[len(ins) + len(outs):]
        g_in, w_in, m_in, v_in = (ins[k * (nv + 1):(k + 1) * (nv + 1)] for k in range(4))
        loss_in = ins[-1]
        g_out, d_out, m_out, v_out = (outs[k * (nv + 1):(k + 1) * (nv + 1)] for k in range(4))
        loss_out = outs[-1]
        x, y, c, me, chips, chip_ids = _mesh_pos()
        sib = (x, y, 1 - c)

        def remote(src, dst, k, dev):
            return pltpu.make_async_remote_copy(src_ref=src, dst_ref=dst, send_sem=send.at[k], recv_sem=recv.at[k],
                                                device_id=dev, device_id_type=_MESH)

        vec_own[...] = jnp.zeros_like(vec_own)
        for i in range(nv):
            vec_own[i:i + 1, 0:widths[i]] = g_in[i][...]
        vec_own[_LOSS_ROW:_LOSS_ROW + 1, 0:128] = loss_in[...]
        swaps = [remote(vec_own, vec_sib, 0, sib), remote(g_in[nv], pool_sib, 1, sib)]
        for cp in swaps:
            cp.start()
        for cp in swaps:
            cp.wait()
        vec_all[me] = vec_own[...] + vec_sib[...]
        pool_sum[...] = g_in[nv][...] + pool_sib[...]
        pool_all[me] = pool_sum[...]
        hv, hp = _VEC_ROWS // 2, _POOL_ROWS // 2
        mine_v = pl.ds(pl.multiple_of(c * hv, 8), hv)
        mine_p = pl.ds(pl.multiple_of(c * hp, 8), hp)
        sends = []
        for j, (cx, cy) in enumerate(chips):
            sends.append(remote(vec_all.at[me, mine_v], vec_all.at[me, mine_v], 2 + j, (cx, cy, c)))
            sends.append(remote(pool_sum.at[mine_p], pool_all.at[me, mine_p], 5 + j, (cx, cy, c)))
        for cp in sends:
            cp.start()
        for cp in sends:
            cp.wait()
        mirrors = []
        for j in range(3):
            mirrors.append(remote(vec_all.at[chip_ids[j], mine_v], vec_all.at[chip_ids[j], mine_v], 8 + j, sib))
            mirrors.append(remote(pool_all.at[chip_ids[j], mine_p], pool_all.at[chip_ids[j], mine_p], 11 + j, sib))
        for cp in mirrors:
            cp.start()
        for cp in mirrors:
            cp.wait()
        vec_tot = vec_all[0]
        pool_tot = pool_all[0]
        for i in range(1, N_CHIPS):
            vec_tot = vec_tot + vec_all[i]
            pool_tot = pool_tot + pool_all[i]
        vec_sib[...] = vec_tot
        loss_out[...] = vec_sib[_LOSS_ROW:_LOSS_ROW + 1, 0:128]
        for i in range(nv + 1):
            gi = pool_tot if i == nv else vec_sib[i:i + 1, 0:widths[i]]
            d, nm, nvv = _adamw_math(w_in[i][...], gi, m_in[i][...], v_in[i][...])
            g_out[i][...] = gi
            d_out[i][...] = d
            m_out[i][...] = nm
            v_out[i][...] = nvv

    vm = pl.BlockSpec(memory_space=pltpu.VMEM)
    args = [d[n] for d in (g, w, m, v) for n in names] + [jnp.broadcast_to(loss_local.reshape(1, 1), (1, 128))]
    out_shape = [jax.ShapeDtypeStruct(g[n].shape, F32) for _ in range(4) for n in names]
    out_shape.append(jax.ShapeDtypeStruct((1, 128), F32))
    res = pl.pallas_call(
        body, name="small_params_step", in_specs=[vm] * len(args), out_specs=[vm] * len(out_shape),
        out_shape=out_shape,
        scratch_shapes=[pltpu.VMEM(shapes["vec"], F32), pltpu.VMEM(shapes["vec"], F32),
                        pltpu.VMEM((N_CHIPS,) + shapes["vec"], F32), pltpu.VMEM(shapes["pool"], F32),
                        pltpu.VMEM(shapes["pool"], F32), pltpu.VMEM((N_CHIPS,) + shapes["pool"], F32),
                        pltpu.SemaphoreType.DMA((14,)), pltpu.SemaphoreType.DMA((14,))],
        compiler_params=pltpu.CompilerParams(vmem_limit_bytes=V7X_VMEM_LIMIT_BYTES),
    )(*args)
    k = len(names)
    dicts = [dict(zip(names, res[i * k:(i + 1) * k])) for i in range(4)]
    return dicts[0], dicts[1], dicts[2], dicts[3], res[-1]


_WEIGHTS = ("ffn1_norm", "ffn1_w_gate", "ffn1_w_up", "ffn1_w_down", "mix_norm", "w_in", "q_norm", "w_q_up",
            "kv_norm", "w_kv_up", "pool_w", "pool_scale", "w_out", "xattn_norm", "mem_norm", "w_mq", "w_mkv",
            "w_mo", "ffn2_norm", "ffn2_w_gate", "ffn2_w_up", "ffn2_w_down", "final_norm")
_SHARDED = ("ffn1_w_gate", "ffn1_w_up", "ffn1_w_down", "w_in", "w_q_up", "w_kv_up", "w_out", "w_mq", "w_mkv",
            "w_mo", "ffn2_w_gate", "ffn2_w_up", "ffn2_w_down")
W_IN_SPLIT = Q_LORA + KV_LORA + ROPE_DIM


_FFN1 = ("ffn1_w_gate", "ffn1_w_up", "ffn1_w_down")
_TRANSPOSED = ("ffn1_w_gate", "ffn1_w_up", "ffn2_w_gate", "ffn2_w_up", "w_in", "w_q_up")


def _local_view(name, a):
    return jnp.swapaxes(a, 1, 2)[0] if name in _TRANSPOSED else a[0]


def _global_view(name, a):
    return jnp.swapaxes(a[None], 1, 2) if name in _TRANSPOSED else a[None]


def _pad_shard(name, a):
    if name == "w_in":
        return jnp.concatenate([a[:W_IN_SPLIT], jnp.zeros((64, a.shape[1]), a.dtype), a[W_IN_SPLIT:]], axis=0)
    if name == "w_q_up":
        return jnp.pad(a, ((0, 64), (0, 0)))
    return a


def _unpad_shard(name, a):
    if name == "w_in":
        return jnp.concatenate([a[:, :W_IN_SPLIT], a[:, W_IN_SPLIT + 64:]], axis=1)
    if name == "w_q_up":
        return a[:, :192]
    return a


def _stacked(g):
    return g if g.ndim == 3 else g.reshape(N_CHIPS, g.shape[0] // N_CHIPS, g.shape[1])


class _Plan:
    AG_UNITS = (
        (("w_in", "w_q_up", "w_kv_up"), "ffn1_up"),
        (("w_out",), "w_in"),
        (("w_mq",), "qkv_prep"),
        (("w_mkv", "w_mo", "ffn2_w_gate"), "mla_fwd"),
        (("ffn2_w_up",), "xattn_fwd"),
        (("ffn2_w_down",), "ffn2_up"),
    )
    RS_UNITS = (
        (("ffn2_w_gate", "ffn2_w_up", "ffn2_w_down"), "ffn2_dn", "mla_bwd", "qkv_prep_bwd"),
        (("w_mo", "w_mq", "w_mkv"), "w_out_dx", "mla_bwd", "qkv_prep_bwd"),
        (("w_out", "w_q_up", "w_kv_up", "w_in"), "w_in_dx", "ffn1_dact", "ffn1_dwd"),
        (("ffn1_w_down",), "ffn1_dwg", "ffn1_dwu", "ffn1_dn_a"),
        (("ffn1_w_gate",), "ffn1_dwu", "ffn1_dn_a", "ffn1_dn_b"),
        (("ffn1_w_up",), "ffn1_dn_a", "ffn1_dn_b", "adamw_w_kv_up"),
    )
    ADAMW_ORDER = (("w_kv_up",), ("ffn2_w_gate", "ffn2_w_up"), ("ffn2_w_down", "ffn1_w_down"), ("w_mo", "w_mq", "w_out"),
                   ("w_mkv",), ("w_q_up",), ("w_in",), ("ffn1_w_gate", "ffn1_w_up"))

    def __init__(self, shards, w, grads, core):
        self.shards, self.w, self.grads, self.core = shards, w, grads, core
        self.last_slab_step = 0
        self.parts = {}
        self.ag = [None for _ in self.AG_UNITS]
        self.rs = [[None, None, None, None] for _ in self.RS_UNITS]

    def pre(self, name):
        for i, (names, host) in enumerate(self.AG_UNITS):
            if name == host:
                st = _ag_ici_stage([self.shards[n] for n in names])
                st.then = _ag_d2d_stage(st.outs)
                st.start_step = self.last_slab_step if name == "ffn1_up" else 0
                self.ag[i] = _host(name, st)
        for i, (names, h1, h2, h3) in enumerate(self.RS_UNITS):
            if name == h1:
                self.rs[i][0] = _host(name, _rs_swap_stage([_stacked(self.grads[n]) for n in names]))
            if name == h2:
                self.rs[i][2] = _host(name, _rs_scatter_stage(self.rs[i][1], relative=names[0] in _FFN1))
            if name == h3:
                self.rs[i][3] = _host(name, _rs_mirror_stage(self.rs[i][2].results))

    def post(self, name):
        for i, (names, host) in enumerate(self.AG_UNITS):
            if name == host:
                for n, f in zip(names, self.ag[i].results):
                    self.w[n] = _full_weight(n, f)
        for i, (names, h1, h2, h3) in enumerate(self.RS_UNITS):
            if name == h1:
                self.rs[i][1] = list(_pair_add("pair_add_" + names[0], [_stacked(self.grads[n]) for n in names],
                                               self.rs[i][0].results, self.core))
            if name == h3:
                for n, p in zip(names, self.rs[i][3].results):
                    self.parts[n] = p


def _full_weight(name, stacked):
    if name in ("w_out", "w_mq", "w_mo"):
        return stacked.reshape(D_MODEL, D_MODEL)
    return stacked


def kernel(x, mem, positions, ffn1_norm, ffn1_w_gate, ffn1_w_up, ffn1_w_down, mix_norm, w_in, q_norm, w_q_up, kv_norm, w_kv_up, pool_w, pool_scale, w_out, xattn_norm, mem_norm, w_mq, w_mkv, w_mo, ffn2_norm, ffn2_w_gate, ffn2_w_up, ffn2_w_down, final_norm, loss_target, m_ffn1_norm, m_ffn1_w_gate, m_ffn1_w_up, m_ffn1_w_down, m_mix_norm, m_w_in, m_q_norm, m_w_q_up, m_kv_norm, m_w_kv_up, m_pool_w, m_pool_scale, m_w_out, m_xattn_norm, m_mem_norm, m_w_mq, m_w_mkv, m_w_mo, m_ffn2_norm, m_ffn2_w_gate, m_ffn2_w_up, m_ffn2_w_down, m_final_norm, v_ffn1_norm, v_ffn1_w_gate, v_ffn1_w_up, v_ffn1_w_down, v_mix_norm, v_w_in, v_q_norm, v_w_q_up, v_kv_norm, v_w_kv_up, v_pool_w, v_pool_scale, v_w_out, v_xattn_norm, v_mem_norm, v_w_mq, v_w_mkv, v_w_mo, v_ffn2_norm, v_ffn2_w_gate, v_ffn2_w_up, v_ffn2_w_down, v_final_norm):
    wts = dict(zip(_WEIGHTS, (ffn1_norm, ffn1_w_gate, ffn1_w_up, ffn1_w_down, mix_norm, w_in, q_norm, w_q_up, kv_norm, w_kv_up, pool_w, pool_scale, w_out, xattn_norm, mem_norm, w_mq, w_mkv, w_mo, ffn2_norm, ffn2_w_gate, ffn2_w_up, ffn2_w_down, final_norm)))
    mom = dict(zip(_WEIGHTS, (m_ffn1_norm, m_ffn1_w_gate, m_ffn1_w_up, m_ffn1_w_down, m_mix_norm, m_w_in, m_q_norm, m_w_q_up, m_kv_norm, m_w_kv_up, m_pool_w, m_pool_scale, m_w_out, m_xattn_norm, m_mem_norm, m_w_mq, m_w_mkv, m_w_mo, m_ffn2_norm, m_ffn2_w_gate, m_ffn2_w_up, m_ffn2_w_down, m_final_norm)))
    var = dict(zip(_WEIGHTS, (v_ffn1_norm, v_ffn1_w_gate, v_ffn1_w_up, v_ffn1_w_down, v_mix_norm, v_w_in, v_q_norm, v_w_q_up, v_kv_norm, v_w_kv_up, v_pool_w, v_pool_scale, v_w_out, v_xattn_norm, v_mem_norm, v_w_mq, v_w_mkv, v_w_mo, v_ffn2_norm, v_ffn2_w_gate, v_ffn2_w_up, v_ffn2_w_down, v_final_norm)))
    small = [n for n in _WEIGHTS if n not in _SHARDED]

    global _PLAN
    shards = {n: _pad_shard(n, _local_view(n, wts[n])).astype(BF16) for n in _SHARDED}
    w = {n: wts[n].reshape(1, -1) for n in _SMALL_VECTORS}
    w["pool_w"] = pool_w[0].astype(BF16)
    grads = {}
    core = lax.axis_index("c").astype(jnp.int32).reshape(1)
    plan = _Plan(shards, w, grads, core)
    _PLAN = plan
    try:
        w["ffn1_shards"] = tuple(shards[n] for n in _FFN1)

        loss_local, dx = _local_step(x[0], mem[0], positions[0], loss_target[0], w, grads)

        def small_view(d):
            out = {n: d[n].reshape(1, -1) for n in _SMALL_VECTORS}
            out["pool_w"] = d["pool_w"].reshape(_POOL_ROWS, POOL_CH)
            return out

        *small_res, loss_vec = _small_params_step(small_view(grads), small_view(wts), small_view(mom),
                                                  small_view(var), loss_local)
        g_out, d_out, m_out, v_out = ({n: r[n].reshape(wts[n].shape) for n in small} for r in small_res)
        loss = loss_vec[0, 0]

        for names in _Plan.ADAMW_ORDER:
            res = _adamw_sum("adamw_" + names[0], [_local_view(n, wts[n]) for n in names],
                             [_unpad_shard(n, plan.parts[n]) for n in names],
                             [_local_view(n, mom[n]) for n in names], [_local_view(n, var[n]) for n in names])
            for n, r4 in zip(names, res):
                g_out[n], d_out[n], m_out[n], v_out[n] = (_global_view(n, r) for r in r4)
    finally:
        _PLAN = None
        _PENDING.clear()

    return (loss, dx[None], *[g_out[n] for n in _WEIGHTS], *[d_out[n] for n in _WEIGHTS],
            *[m_out[n] for n in _WEIGHTS], *[v_out[n] for n in _WEIGHTS])
```

```python
import jax
import jax.numpy as jnp
from jax import lax
from jax.experimental import pallas as pl
from jax.experimental.pallas import tpu as pltpu

F32 = jnp.float32
BF16 = jnp.bfloat16

D_MODEL = 1024
D_FF = 2816
N_CHIPS = 4
FF_SHARD = D_FF // N_CHIPS
MLA_HEADS = 4
Q_LORA = 256
KV_LORA = 128
ROPE_DIM = 64
HEAD_QK = 256
HEAD_V = 128
POOL_GROUPS = 4
POOL_CH = 128
MEM_HEADS = 4
MEM_HEAD_DIM = 256
RMS_EPS = 1e-6
ROPE_BASE = 10000.0
MLA_SCALE = (128 + 64) ** -0.5
MEM_SCALE = MEM_HEAD_DIM ** -0.5

ADAM_LR = 0.001
ADAM_B1 = 0.9
ADAM_B2 = 0.999
ADAM_EPS = 1e-08
ADAM_WD = 0.01
ADAM_STEP = 10

V7X_VMEM_LIMIT_BYTES = 56 * 1024 * 1024

NN = ((1,), (0,))
NT = ((1,), (1,))
TN = ((0,), (0,))


def _params(*sem):
    return pltpu.CompilerParams(dimension_semantics=sem, vmem_limit_bytes=V7X_VMEM_LIMIT_BYTES)


_MESH = pl.DeviceIdType.MESH
_ANY = pl.BlockSpec(memory_space=pl.ANY)


class _Stage:
    def __init__(self, ins, outs, n_remote, n_local, copies, aliases=None):
        self.ins, self.outs, self.n_remote, self.n_local = list(ins), list(outs), n_remote, n_local
        self.copies, self.aliases = copies, dict(aliases or {})
        self.results = None
        self.start_step = 0
        self.then = None

    def descriptors(self, in_refs, out_refs, send, recv, loc):
        ds, ri, li = [], 0, 0
        for src, dst, dev in self.copies(in_refs, out_refs):
            if dev is None:
                ds.append(pltpu.make_async_copy(src, dst, loc.at[li]))
                li += 1
            else:
                ds.append(pltpu.make_async_remote_copy(src_ref=src, dst_ref=dst, send_sem=send.at[ri],
                                                       recv_sem=recv.at[ri], device_id=dev, device_id_type=_MESH))
                ri += 1
        assert ri == self.n_remote and li == self.n_local
        return ds


_PENDING = {}


def _host(name, stage):
    _PENDING.setdefault(name, []).append(stage)
    return stage


_PLAN = None


def _call(body, **kw):
    if _PLAN is not None:
        _PLAN.pre(kw["name"])
    res = _call_hosting(body, **kw)
    if _PLAN is not None:
        _PLAN.post(kw["name"])
    return res


def _call_hosting(body, *, name, grid, in_specs, out_specs, out_shape, sem, args, scratch_shapes=(), aliases=None):
    stages = _PENDING.pop(name, [])
    scratch_shapes = list(scratch_shapes)
    if not stages:
        return pl.pallas_call(body, name=name, grid=grid, in_specs=in_specs, out_specs=out_specs,
                              out_shape=out_shape, scratch_shapes=scratch_shapes,
                              input_output_aliases=dict(aliases or {}), compiler_params=_params(*sem))(*args)
    ni, no, ns = len(in_specs), len(out_shape), len(scratch_shapes)
    c_ins = [a for st in stages for a in st.ins]
    c_outs = [o for st in stages for o in st.outs]
    nci, nco = len(c_ins), len(c_outs)
    aliases, io, oo = dict(aliases or {}), 0, 0
    for st in stages:
        for i, j in st.aliases.items():
            aliases[ni + io + i] = no + oo + j
        io += len(st.ins)
        oo += len(st.outs)
    dma = pltpu.SemaphoreType.DMA
    sems = []
    for st in stages:
        sems += [dma((max(st.n_remote, 1),)), dma((max(st.n_remote, 1),)), dma((max(st.n_local, 1),))]
    followers = [st.then for st in stages if st.then is not None]
    for st in followers:
        sems += [dma((max(st.n_remote, 1),)), dma((max(st.n_remote, 1),)), dma((max(st.n_local, 1),))]

    def wrapped(*refs):
        ins, cin = refs[:ni], refs[ni:ni + nci]
        outs, cout = refs[ni + nci:ni + nci + no], refs[ni + nci + no:ni + nci + no + nco]
        scr = refs[ni + nci + no + nco:ni + nci + no + nco + ns]
        sem_refs = refs[ni + nci + no + nco + ns:]
        step = pl.program_id(0)
        last = pl.program_id(0) == grid[0] - 1
        for ax in range(1, len(grid)):
            step = step * grid[ax] + pl.program_id(ax)
            last = jnp.logical_and(last, pl.program_id(ax) == grid[ax] - 1)

        def descriptors(si):
            io = sum(len(st.ins) for st in stages[:si])
            oo = sum(len(st.outs) for st in stages[:si])
            st = stages[si]
            return st.descriptors(cin[io:io + len(st.ins)], cout[oo:oo + len(st.outs)], *sem_refs[3 * si:3 * si + 3])

        def follower_descriptors(fi):
            si = [k for k, st in enumerate(stages) if st.then is not None][fi]
            oo = sum(len(st.outs) for st in stages[:si])
            bufs = cout[oo:oo + len(stages[si].outs)]
            k0 = 3 * (len(stages) + fi)
            return followers[fi].descriptors(bufs, bufs, *sem_refs[k0:k0 + 3])

        def start(si):
            @pl.when(step == stages[si].start_step)
            def _():
                for d in descriptors(si):
                    d.start()

        for si, st in enumerate(stages):
            if st.start_step == 0:
                start(si)
        body(*ins, *outs, *scr)
        for si, st in enumerate(stages):
            if st.start_step != 0:
                start(si)

        @pl.when(last)
        def _():
            for si in range(len(stages)):
                for d in descriptors(si):
                    d.wait()
            for fi in range(len(followers)):
                for d in follower_descriptors(fi):
                    d.start()
            for fi in range(len(followers)):
                for d in follower_descriptors(fi):
                    d.wait()

    res = pl.pallas_call(
        wrapped, name=name, grid=grid, in_specs=list(in_specs) + [_ANY] * nci,
        out_specs=list(out_specs) + [_ANY] * nco, out_shape=list(out_shape) + c_outs,
        scratch_shapes=scratch_shapes + sems, input_output_aliases=aliases,
        compiler_params=_params(*(("arbitrary",) * len(grid))))(*args, *c_ins)
    oo = no
    for st in stages:
        st.results = list(res[oo:oo + len(st.outs)])
        oo += len(st.outs)
    return list(res[:no])


def _dot(a, b, dims):
    return lax.dot_general(a.astype(BF16), b.astype(BF16), (dims, ((), ())), preferred_element_type=F32)


_MAX_ROW_BLOCK = 1024
_ATT_BLOCK = 512


_MAX_REDUCE_BLOCK = 2048


def _row_block(s, want=1024):
    return min(want, s, _MAX_ROW_BLOCK)


def _reduce_block(s):
    return min(s, _MAX_REDUCE_BLOCK)


def _matmul(name, grid, terms, extras, outs, epilogue, acc_shape, fill=(), summed=()):
    nt, ne, no, nf = len(terms), len(extras), len(outs), len(fill)
    nk = grid[-1]
    dims = [t[4] for t in terms]

    def body(*refs):
        a_refs, b_refs = refs[:nt], refs[nt:2 * nt]
        e_refs = refs[2 * nt:2 * nt + ne]
        o_refs = refs[2 * nt + ne + nf:2 * nt + ne + nf + no]

        def finish(acc):
            vals = epilogue(acc, *[e[...] for e in e_refs])
            for idx, (o, val) in enumerate(zip(o_refs, vals)):
                if idx in summed:
                    @pl.when(pl.program_id(0) == 0)
                    def _(o=o, val=val):
                        o[...] = val.astype(o.dtype)

                    @pl.when(pl.program_id(0) > 0)
                    def _(o=o, val=val):
                        o[...] += val.astype(o.dtype)
                else:
                    o[...] = val.astype(o.dtype)

        if nk == 1:
            part = None
            for a, b, d in zip(a_refs, b_refs, dims):
                t = _dot(a[...], b[...], d)
                part = t if part is None else part + t
            finish(part)
        else:
            acc_ref = refs[-1]
            k = pl.program_id(len(grid) - 1)

            @pl.when(k == 0)
            def _():
                acc_ref[...] = jnp.zeros_like(acc_ref)

            for a, b, d in zip(a_refs, b_refs, dims):
                acc_ref[...] += _dot(a[...], b[...], d)

            @pl.when(k == nk - 1)
            def _():
                finish(acc_ref[...])

    in_specs = [t[1] for t in terms] + [t[3] for t in terms] + [e[1] for e in extras] + [_ANY] * nf
    args = [t[0] for t in terms] + [t[2] for t in terms] + [e[0] for e in extras] + list(fill)
    sem = ("arbitrary" if summed else "parallel",) * (len(grid) - 1) + ("arbitrary",)
    aliases = {2 * nt + ne + i: i for i in range(nf)}
    return _call(
        body, name=name, grid=grid, in_specs=in_specs,
        out_specs=[o[1] for o in outs], out_shape=[o[0] for o in outs],
        scratch_shapes=[pltpu.VMEM(acc_shape, F32)] if nk > 1 else [], sem=sem, args=args, aliases=aliases)


def _ident(acc):
    return (acc,)


def _rmsnorm_fwd(name, x, gain, width, col_block=0):
    s = x.shape[0]
    bm = _row_block(s)

    def body(x_ref, g_ref, o_ref):
        xf = x_ref[...]
        r = lax.rsqrt(jnp.mean(xf * xf, axis=-1, keepdims=True) + RMS_EPS)
        o_ref[...] = ((xf * r) * g_ref[...]).astype(o_ref.dtype)

    return pl.pallas_call(
        body, name=name, grid=(s // bm,),
        in_specs=[pl.BlockSpec((bm, width), lambda i: (i, col_block)), pl.BlockSpec((1, width), lambda i: (0, 0))],
        out_specs=pl.BlockSpec((bm, width), lambda i: (i, 0)),
        out_shape=jax.ShapeDtypeStruct((s, width), BF16),
        compiler_params=_params("parallel"),
    )(x, gain)


def _rms_bwd_math(dy, xf, g, width):
    r = lax.rsqrt(jnp.mean(xf * xf, axis=-1, keepdims=True) + RMS_EPS)
    dyg = dy * g
    dot = jnp.sum(dyg * xf, axis=-1, keepdims=True)
    dx = r * dyg - xf * ((r * r * r) * (dot * (1.0 / width)))
    dgain = jnp.sum(dy * (xf * r), axis=0, keepdims=True)
    return dx, dgain


def _rmsnorm_bwd(name, dy, x, gain, width, col_block=0, dres=None, out_dtype=F32):
    s = x.shape[0]
    bm = _row_block(s)
    has_res = dres is not None

    def body(*refs):
        if has_res:
            dy_ref, x_ref, g_ref, r_ref, dx_ref, dg_ref = refs
        else:
            dy_ref, x_ref, g_ref, dx_ref, dg_ref = refs
        dx, dgain = _rms_bwd_math(dy_ref[...].astype(F32), x_ref[...], g_ref[...], width)
        if has_res:
            dx = dx + r_ref[...]
        dx_ref[...] = dx.astype(dx_ref.dtype)

        @pl.when(pl.program_id(0) == 0)
        def _():
            dg_ref[...] = dgain

        @pl.when(pl.program_id(0) > 0)
        def _():
            dg_ref[...] += dgain

    row = pl.BlockSpec((bm, width), lambda i: (i, 0))
    in_specs = [row, pl.BlockSpec((bm, width), lambda i: (i, col_block)), pl.BlockSpec((1, width), lambda i: (0, 0))]
    args = [dy, x, gain]
    out_specs = [row, pl.BlockSpec((1, width), lambda i: (0, 0))]
    out_shape = [jax.ShapeDtypeStruct((s, width), out_dtype), jax.ShapeDtypeStruct((1, width), F32)]
    if has_res:
        in_specs.append(row)
        args.append(dres)
    return _call(body, name=name, grid=(s // bm,), in_specs=in_specs, out_specs=out_specs, out_shape=out_shape,
                 sem=("arbitrary",), args=args)


def _ffn_up(name, n, wg, wu):
    s = n.shape[0]
    bm = _row_block(s)

    def body(n_ref, wg_ref, wu_ref, a_ref, dadu_ref, dadg_ref):
        x = n_ref[...]
        g = _dot(x, wg_ref[...], NT)
        u = _dot(x, wu_ref[...], NT)
        sg = jax.nn.sigmoid(g)
        silu = g * sg
        a_ref[...] = (silu * u).astype(BF16)
        dadu_ref[...] = silu.astype(BF16)
        dadg_ref[...] = (u * (sg * (1.0 + g * (1.0 - sg)))).astype(BF16)

    w_spec = pl.BlockSpec((None, FF_SHARD, D_MODEL), lambda j, i: (j, 0, 0))
    o_spec = pl.BlockSpec((None, bm, FF_SHARD), lambda j, i: (j, i, 0))
    shp = jax.ShapeDtypeStruct((N_CHIPS, s, FF_SHARD), BF16)
    return _call(
        body, name=name, grid=(N_CHIPS, s // bm),
        in_specs=[pl.BlockSpec((bm, D_MODEL), lambda j, i: (i, 0)), w_spec, w_spec],
        out_specs=[o_spec, o_spec, o_spec], out_shape=[shp, shp, shp],
        sem=("parallel", "parallel"), args=[n, wg, wu])


def _ffn1_up_gather(xin, gain, g_sh, u_sh, d_sh):
    s = xin.shape[0]
    bm = _row_block(s)
    nrb = s // bm
    rows, cols = g_sh.shape

    def body(x_ref, gain_ref, gs, us, ds, n_ref, a_ref, dadu_ref, dadg_ref, wg, wu, wd, gbuf, ubuf,
             send, recv, qsend, qrecv, fsend, frecv, loc, ld):
        r, i = pl.program_id(0), pl.program_id(1)
        x, y, c = lax.axis_index("x"), lax.axis_index("y"), lax.axis_index("c")
        sib = (x, y, 1 - c)
        mine, _ = _half_rows(c, rows)
        quarters = _quarter_rows(c, rows)
        shards, fulls, bufs = (gs, us, ds), (wg, wu, wd), (gbuf, ubuf)

        def remote(src, dst, ssem, rsem, dev):
            return pltpu.make_async_remote_copy(src_ref=src, dst_ref=dst, send_sem=ssem, recv_sem=rsem,
                                                device_id=dev, device_id_type=_MESH)

        def peer(rel):
            return ((1 - x) if rel & 2 else x, (1 - y) if rel & 1 else y, c)

        def ici(k, rel, dev=sib):
            return remote(shards[k].at[mine], fulls[k].at[rel, mine], send.at[k, rel - 1], recv.at[k, rel - 1], dev)

        def quarter(k, which, dev=sib):
            slab, q = ((2, quarters[0]), (1, quarters[1]))[which]
            return remote(fulls[k].at[slab, q], fulls[k].at[3, q], qsend.at[k, which], qrecv.at[k, which], dev)

        def fwd(k, rel):
            return remote(fulls[k].at[rel, mine], fulls[k].at[rel, mine], fsend.at[k, rel - 1], frecv.at[k, rel - 1], sib)

        def own(k):
            return pltpu.make_async_copy(shards[k], fulls[k].at[0], loc.at[k])

        def load(slab):
            for k in (0, 1):
                pltpu.make_async_copy(shards[k] if slab == 0 else fulls[k].at[slab], bufs[k], ld.at[k]).start()
            for k in (0, 1):
                pltpu.make_async_copy(shards[k] if slab == 0 else fulls[k].at[slab], bufs[k], ld.at[k]).wait()

        def from_neighbour(ks, rel):
            for k in ks:
                ici(k, rel).wait_recv()
                fwd(k, rel).start()
                quarter(k, 0 if rel == 2 else 1, peer(1 if rel == 2 else 2)).start()
            for k in ks:
                fwd(k, rel).wait_recv()

        def from_diagonal(ks):
            for k in ks:
                quarter(k, 0).wait_recv()
                quarter(k, 1).wait_recv()
                fwd(k, 3).start()
            for k in ks:
                fwd(k, 3).wait_recv()

        @pl.when(jnp.logical_and(r == 0, i == 0))
        def _():
            for k in range(3):
                own(k).start()
            for rel in (1, 2):
                for k in (0, 1):
                    ici(k, rel, peer(rel)).start()
            load(0)

        @pl.when(jnp.logical_and(r == 1, i == 0))
        def _():
            from_neighbour((0, 1), 1)
            load(1)
            for rel in (1, 2):
                ici(2, rel, peer(rel)).start()

        @pl.when(jnp.logical_and(r == 2, i == 0))
        def _():
            from_neighbour((0, 1), 2)
            load(2)

        @pl.when(jnp.logical_and(r == 3, i == 0))
        def _():
            from_diagonal((0, 1))
            load(3)

        xv = _norm_bf16(x_ref[...], gain_ref[...])

        @pl.when(r == 0)
        def _():
            n_ref[...] = xv

        g = _dot(xv, gbuf[...], NT)
        u = _dot(xv, ubuf[...], NT)
        sg = jax.nn.sigmoid(g)
        silu = g * sg
        a_ref[...] = (silu * u).astype(BF16)
        dadu_ref[...] = silu.astype(BF16)
        dadg_ref[...] = (u * (sg * (1.0 + g * (1.0 - sg)))).astype(BF16)

        @pl.when(jnp.logical_and(r == 3, i == nrb - 1))
        def _():
            from_neighbour((2,), 1)
            from_neighbour((2,), 2)
            from_diagonal((2,))
            for k in range(3):
                for rel in (1, 2):
                    ici(k, rel).wait_send()
                for which in (0, 1):
                    quarter(k, which).wait_send()
                for rel in (1, 2, 3):
                    fwd(k, rel).wait_send()
                own(k).wait()

    o_spec = pl.BlockSpec((None, bm, FF_SHARD), lambda r, i: (r, i, 0))
    act = jax.ShapeDtypeStruct((N_CHIPS, s, FF_SHARD), BF16)
    full = jax.ShapeDtypeStruct((N_CHIPS, rows, cols), BF16)
    dma = pltpu.SemaphoreType.DMA
    if _PLAN is not None:
        _PLAN.last_slab_step = 3 * nrb
    n_spec = pl.BlockSpec((bm, D_MODEL), lambda r, i: (jnp.where(r == 0, i, nrb - 1), 0))
    return _call(
        body, name="ffn1_up", grid=(N_CHIPS, nrb),
        in_specs=[pl.BlockSpec((bm, D_MODEL), lambda r, i: (i, 0)), pl.BlockSpec((1, D_MODEL), lambda r, i: (0, 0)),
                  _ANY, _ANY, _ANY],
        out_specs=[n_spec, o_spec, o_spec, o_spec, _ANY, _ANY, _ANY],
        out_shape=[jax.ShapeDtypeStruct((s, D_MODEL), BF16), act, act, act, full, full, full],
        scratch_shapes=[pltpu.VMEM((rows, cols), BF16), pltpu.VMEM((rows, cols), BF16), dma((3, 2)), dma((3, 2)),
                        dma((3, 2)), dma((3, 2)), dma((3, 3)), dma((3, 3)), dma((3,)), dma((2,))],
        sem=("arbitrary", "arbitrary"), args=[xin, gain, g_sh, u_sh, d_sh])


def _residual_epilogue(alpha, with_norm):
    if not with_norm:
        return lambda acc, r: (r + alpha * acc,)

    def epilogue(acc, r, g):
        h = r + alpha * acc
        rs = lax.rsqrt(jnp.mean(h * h, axis=-1, keepdims=True) + RMS_EPS)
        return h, (h * rs) * g

    return epilogue


def _residual_outs(s, bm, gain):
    row = pl.BlockSpec((bm, D_MODEL), lambda i, k: (i, 0))
    outs = [(jax.ShapeDtypeStruct((s, D_MODEL), F32), row)]
    if gain is None:
        return [], outs
    return [(gain, pl.BlockSpec((1, D_MODEL), lambda i, k: (0, 0)))], outs + [(jax.ShapeDtypeStruct((s, D_MODEL), BF16), row)]


def _loss_epilogue(acc, res, g, target):
    d = acc.shape[-1]
    h = res + 0.5 * acc
    r = lax.rsqrt(jnp.mean(h * h, axis=-1, keepdims=True) + RMS_EPS)
    err = (h * r) * g - target
    part = 0.5 * jnp.sum(jnp.mean(err * err, axis=-1, keepdims=True), axis=0, keepdims=True)
    dx, dgain = _rms_bwd_math(err * (1.0 / d), h, g, d)
    return dx, dx, jnp.broadcast_to(part, (1, 128)), dgain


def _ffn_down(name, a, wd, res, gain=None, loss=None):
    s = a.shape[1]
    bm = _row_block(s, 512)
    row = pl.BlockSpec((bm, D_MODEL), lambda i, k: (i, 0))
    terms = [(a, pl.BlockSpec((None, bm, FF_SHARD), lambda i, k, j=j: (j, i, 0)),
              wd, pl.BlockSpec((None, FF_SHARD, D_MODEL), lambda i, k, j=j: (j, 0, 0)), NN) for j in range(N_CHIPS)]
    if loss is not None:
        vec = pl.BlockSpec((1, D_MODEL), lambda i, k: (0, 0))
        outs = [(jax.ShapeDtypeStruct((s, D_MODEL), F32), row), (jax.ShapeDtypeStruct((s, D_MODEL), BF16), row),
                (jax.ShapeDtypeStruct((1, 128), F32), pl.BlockSpec((1, 128), lambda i, k: (0, 0))),
                (jax.ShapeDtypeStruct((1, D_MODEL), F32), vec)]
        return _matmul(name, (s // bm, 1), terms, [(res, row), (loss[0], vec), (loss[1], row)], outs,
                       _loss_epilogue, None, summed=(2, 3))
    extras, outs = _residual_outs(s, bm, gain)
    res_out = _matmul(name, (s // bm, 1), terms, [(res, row)] + extras, outs,
                      _residual_epilogue(0.5, gain is not None), None)
    return res_out if gain is not None else res_out[0]


def _norm_bwd_epilogue(width):
    def epilogue(acc, h, g, dres):
        dx, dgain = _rms_bwd_math(acc, h, g, width)
        dx = dx + dres
        return dx, dx, dgain

    return epilogue


def _norm_bwd_operands(s, bm, h, gain, dres):
    row = pl.BlockSpec((bm, D_MODEL), lambda i, k: (i, 0))
    vec = pl.BlockSpec((1, D_MODEL), lambda i, k: (0, 0))
    extras = [(h, row), (gain, vec), (dres, row)]
    outs = [(jax.ShapeDtypeStruct((s, D_MODEL), F32), row), (jax.ShapeDtypeStruct((s, D_MODEL), BF16), row),
            (jax.ShapeDtypeStruct((1, D_MODEL), F32), vec)]
    return extras, outs, (2,)


def _ffn_bwd(tag, dh, n, dadg, dadu, a, wg, wu, wd, grads, norm_bwd=None, many_calls=True):
    s = dh.shape[0]
    bm = _row_block(s)
    bk = _reduce_block(s)
    nk = s // bk

    def act_bwd(acc, dg_da, du_da):
        da = 0.5 * acc
        return da * dg_da.astype(F32), da * du_da.astype(F32)

    slab = pl.BlockSpec((None, bm, FF_SHARD), lambda j, i, k: (j, i, 0))
    shp = jax.ShapeDtypeStruct((N_CHIPS, s, FF_SHARD), BF16)
    dg, du = _matmul(
        tag + "_dact", (N_CHIPS, s // bm, 1),
        [(dh, pl.BlockSpec((bm, D_MODEL), lambda j, i, k: (i, 0)),
          wd, pl.BlockSpec((None, FF_SHARD, D_MODEL), lambda j, i, k: (j, 0, 0)), NT)],
        [(dadg, slab), (dadu, slab)], [(shp, slab), (shp, slab)], act_bwd, None)

    grads[tag + "_w_down"] = _matmul(
        tag + "_dwd", (N_CHIPS, nk),
        [(a, pl.BlockSpec((None, bk, FF_SHARD), lambda j, k: (j, k, 0)),
          dh, pl.BlockSpec((bk, D_MODEL), lambda j, k: (k, 0)), TN)],
        [], [(jax.ShapeDtypeStruct((N_CHIPS, FF_SHARD, D_MODEL), BF16),
              pl.BlockSpec((None, FF_SHARD, D_MODEL), lambda j, k: (j, 0, 0)))],
        lambda acc: (0.5 * acc,), (FF_SHARD, D_MODEL))[0]

    def dw_up(nm, dact):
        return _matmul(
            nm, (N_CHIPS, nk),
            [(dact, pl.BlockSpec((None, bk, FF_SHARD), lambda j, k: (j, k, 0)),
              n, pl.BlockSpec((bk, D_MODEL), lambda j, k: (k, 0)), TN)],
            [], [(jax.ShapeDtypeStruct((N_CHIPS, FF_SHARD, D_MODEL), BF16),
                  pl.BlockSpec((None, FF_SHARD, D_MODEL), lambda j, k: (j, 0, 0)))],
            _ident, (FF_SHARD, D_MODEL))[0]

    def dw_up_pair(nm):
        def body(dg_ref, du_ref, n_ref, og_ref, ou_ref, acc_g, acc_u):
            k = pl.program_id(1)

            @pl.when(k == 0)
            def _():
                acc_g[...] = jnp.zeros_like(acc_g)
                acc_u[...] = jnp.zeros_like(acc_u)

            nv = n_ref[...]
            acc_g[...] += _dot(dg_ref[...], nv, TN)
            acc_u[...] += _dot(du_ref[...], nv, TN)

            @pl.when(k == nk - 1)
            def _():
                og_ref[...] = acc_g[...].astype(BF16)
                ou_ref[...] = acc_u[...].astype(BF16)

        act = pl.BlockSpec((None, bk, FF_SHARD), lambda j, k: (j, k, 0))
        out = pl.BlockSpec((None, FF_SHARD, D_MODEL), lambda j, k: (j, 0, 0))
        shape = jax.ShapeDtypeStruct((N_CHIPS, FF_SHARD, D_MODEL), BF16)
        return _call(body, name=nm, grid=(N_CHIPS, nk),
                     in_specs=[act, act, pl.BlockSpec((bk, D_MODEL), lambda j, k: (k, 0))], out_specs=[out, out],
                     out_shape=[shape, shape],
                     scratch_shapes=[pltpu.VMEM((FF_SHARD, D_MODEL), F32), pltpu.VMEM((FF_SHARD, D_MODEL), F32)],
                     sem=("parallel", "arbitrary"), args=[dg, du, n])

    if many_calls:
        grads[tag + "_w_gate"] = dw_up(tag + "_dwg", dg)
        grads[tag + "_w_up"] = dw_up(tag + "_dwu", du)
    else:
        grads[tag + "_w_gate"], grads[tag + "_w_up"] = dw_up_pair(tag + "_dwgu")

    bn = _row_block(s, 512)
    steps = s // bn // 2 if many_calls else s // bn
    prev, dgain = (), None
    for part, off in ((("_dn_a", 0), ("_dn_b", steps)) if many_calls else (("_dn", 0),)):
        row = pl.BlockSpec((bn, D_MODEL), lambda i, k, off=off: (i + off, 0))
        terms = []
        for j in range(N_CHIPS):
            a_slab = pl.BlockSpec((None, bn, FF_SHARD), lambda i, k, j=j, off=off: (j, i + off, 0))
            w_slab = pl.BlockSpec((None, FF_SHARD, D_MODEL), lambda i, k, j=j: (j, 0, 0))
            terms += [(dg, a_slab, wg, w_slab, NN), (du, a_slab, wu, w_slab, NN)]
        if norm_bwd is None:
            prev = _matmul(tag + part, (steps, 1), terms, [], [(jax.ShapeDtypeStruct((s, D_MODEL), F32), row)],
                           _ident, None, fill=prev)
            continue
        h, gain, dres = norm_bwd
        vec = pl.BlockSpec((1, D_MODEL), lambda i, k: (0, 0))
        res = _matmul(
            tag + part, (steps, 1), terms, [(h, row), (gain, vec), (dres, row)],
            [(jax.ShapeDtypeStruct((s, D_MODEL), F32), row), (jax.ShapeDtypeStruct((s, D_MODEL), BF16), row),
             (jax.ShapeDtypeStruct((1, D_MODEL), F32), vec)],
            _norm_bwd_epilogue(D_MODEL), None, fill=prev, summed=(2,))
        prev = res[:2]
        dgain = res[2] if dgain is None else dgain + res[2]
    return prev[0] if norm_bwd is None else (prev[0], prev[1], dgain)


def _mm_nn(name, a, b, out_dtype, res=None, gain=None):
    s, k = a.shape
    nn = b.shape[1]
    bm = _row_block(s)
    row = pl.BlockSpec((bm, nn), lambda i, kk: (i, 0))
    term = [(a, pl.BlockSpec((bm, k), lambda i, kk: (i, 0)), b, pl.BlockSpec((k, nn), lambda i, kk: (0, 0)), NN)]
    if res is None:
        return _matmul(name, (s // bm, 1), term, [], [(jax.ShapeDtypeStruct((s, nn), out_dtype), row)], _ident, None)[0]
    extras, outs = _residual_outs(s, bm, gain)
    res_out = _matmul(name, (s // bm, 1), term, [(res, row)] + extras, outs,
                      _residual_epilogue(1.0, gain is not None), None)
    return res_out if gain is not None else res_out[0]


def _mm_nt(name, a, b, out_dtype, attn_out=None, nh=0, dv=0):
    s, nn = a.shape
    k = b.shape[0]
    bm = _row_block(s)
    term = [(a, pl.BlockSpec((bm, nn), lambda i, kk: (i, 0)), b, pl.BlockSpec((k, nn), lambda i, kk: (0, 0)), NT)]
    out = (jax.ShapeDtypeStruct((s, k), out_dtype), pl.BlockSpec((bm, k), lambda i, kk: (i, 0)))
    if attn_out is None:
        return _matmul(name, (s // bm, 1), term, [], [out], _ident, None)[0]

    def with_delta(acc, o):
        do = acc.astype(out_dtype).astype(F32)
        cols = [jnp.sum(do[:, h * dv:(h + 1) * dv] * o[:, h * dv:(h + 1) * dv].astype(F32), axis=-1, keepdims=True)
                for h in range(nh)]
        return acc, jnp.stack(cols, axis=0)

    return _matmul(
        name, (s // bm, 1), term, [(attn_out, pl.BlockSpec((bm, nh * dv), lambda i, kk: (i, 0)))],
        [out, (jax.ShapeDtypeStruct((nh, s, 1), F32), pl.BlockSpec((nh, bm, 1), lambda i, kk: (0, i, 0)))],
        with_delta, None)


def _mm_nt_norm_bwd(name, a, b, h, gain, dres):
    s, nn = a.shape
    bm = _row_block(s, 512)
    extras, outs, summed = _norm_bwd_operands(s, bm, h, gain, dres)
    return _matmul(
        name, (s // bm, 1),
        [(a, pl.BlockSpec((bm, nn), lambda i, kk: (i, 0)), b, pl.BlockSpec(b.shape, lambda i, kk: (0, 0)), NT)],
        extras, outs, _norm_bwd_epilogue(D_MODEL), None, summed=summed)


def _w_in_dx_norm_bwd(dz, w_t, h, gain, dres):
    s = dz.shape[0]
    bm = _row_block(s, 512)
    epilogue = _norm_bwd_epilogue(D_MODEL)

    def body(dz_ref, w_ref, h_ref, g_ref, r_ref, dx_ref, dxb_ref, dg_ref):
        dzv = dz_ref[...]
        dn = jnp.concatenate([_dot(dzv, w_ref[j], NN) for j in range(N_CHIPS)], axis=1)
        dx, _, dgain = epilogue(dn, h_ref[...], g_ref[...], r_ref[...])
        dx_ref[...] = dx
        dxb_ref[...] = dx.astype(BF16)

        @pl.when(pl.program_id(0) == 0)
        def _():
            dg_ref[...] = dgain

        @pl.when(pl.program_id(0) > 0)
        def _():
            dg_ref[...] += dgain

    row = pl.BlockSpec((bm, D_MODEL), lambda i: (i, 0))
    vec = pl.BlockSpec((1, D_MODEL), lambda i: (0, 0))
    return _call(
        body, name="w_in_dx", grid=(s // bm,),
        in_specs=[row, pl.BlockSpec(w_t.shape, lambda i: (0, 0, 0)), row, vec, row],
        out_specs=[row, row, vec],
        out_shape=[jax.ShapeDtypeStruct((s, D_MODEL), F32), jax.ShapeDtypeStruct((s, D_MODEL), BF16),
                   jax.ShapeDtypeStruct((1, D_MODEL), F32)],
        sem=("arbitrary",), args=[dz, w_t, h, gain, dres])


def _mm_tn_stacked(name, a_list, b):
    s, nn = b.shape
    widths = [a.shape[1] for a in a_list]
    total = sum(widths)
    bk = _reduce_block(s)
    nk = s // bk
    na = len(a_list)

    def body(*refs):
        a_refs, b_ref, o_ref, acc_ref = refs[:na], refs[na], refs[na + 1], refs[na + 2]
        k = pl.program_id(0)

        @pl.when(k == 0)
        def _():
            acc_ref[...] = jnp.zeros_like(acc_ref)

        bv = b_ref[...]
        lo = 0
        for a_ref, w in zip(a_refs, widths):
            acc_ref[lo:lo + w, :] += _dot(a_ref[...], bv, TN)
            lo += w

        @pl.when(k == nk - 1)
        def _():
            o_ref[...] = acc_ref[...].astype(o_ref.dtype)

    return _call(
        body, name=name, grid=(nk,),
        in_specs=[pl.BlockSpec((bk, w), lambda k: (k, 0)) for w in widths] + [pl.BlockSpec((bk, nn), lambda k: (k, 0))],
        out_specs=[pl.BlockSpec((total, nn), lambda k: (0, 0))],
        out_shape=[jax.ShapeDtypeStruct((total, nn), BF16)],
        scratch_shapes=[pltpu.VMEM((total, nn), F32)], sem=("arbitrary",), args=list(a_list) + [b])[0]


def _mm_tn_pairs(name, pairs):
    s = pairs[0][0].shape[0]
    bk = _row_block(s)
    nk = s // bk
    npairs = len(pairs)
    shapes = [(a.shape[1], b.shape[1]) for a, b in pairs]

    def body(*refs):
        ins, outs, accs = refs[:2 * npairs], refs[2 * npairs:3 * npairs], refs[3 * npairs:]
        k = pl.program_id(0)

        @pl.when(k == 0)
        def _():
            for acc in accs:
                acc[...] = jnp.zeros_like(acc)

        for p in range(npairs):
            accs[p][...] += _dot(ins[2 * p][...], ins[2 * p + 1][...], TN)

        @pl.when(k == nk - 1)
        def _():
            for o, acc in zip(outs, accs):
                o[...] = acc[...].astype(o.dtype)

    in_specs = [pl.BlockSpec((bk, x.shape[1]), lambda k: (k, 0)) for pair in pairs for x in pair]
    return _call(
        body, name=name, grid=(nk,), in_specs=in_specs,
        out_specs=[pl.BlockSpec(shp, lambda k: (0, 0)) for shp in shapes],
        out_shape=[jax.ShapeDtypeStruct(shp, BF16) for shp in shapes],
        scratch_shapes=[pltpu.VMEM(shp, F32) for shp in shapes], sem=("arbitrary",),
        args=[x for pair in pairs for x in pair])


def _mm_heads_fwd(name, a, w, out_dtype, w_transposed=False):
    s, k = a.shape
    nh = w.shape[0]
    nn = w.shape[1] if w_transposed else w.shape[2]
    bm = _row_block(s)
    return _matmul(
        name, (nh, s // bm, 1),
        [(a, pl.BlockSpec((bm, k), lambda h, i, kk: (i, 0)),
          w, pl.BlockSpec((None,) + w.shape[1:], lambda h, i, kk: (h, 0, 0)), NT if w_transposed else NN)],
        [], [(jax.ShapeDtypeStruct((s, nh * nn), out_dtype), pl.BlockSpec((bm, nn), lambda h, i, kk: (i, h)))],
        _ident, None)[0]


def _mm_heads_bwd(name, dy, a, w, w_transposed=False):
    s, k = a.shape
    nh = w.shape[0]
    nn = w.shape[1] if w_transposed else w.shape[2]
    bm = _row_block(s)
    bk = _reduce_block(s)
    w_spec = pl.BlockSpec((None,) + w.shape[1:], lambda i, h: (h, 0, 0))
    da = _matmul(
        name + "_dx", (s // bm, nh),
        [(dy, pl.BlockSpec((bm, nn), lambda i, h: (i, h)), w, w_spec, NN if w_transposed else NT)],
        [], [(jax.ShapeDtypeStruct((s, k), F32), pl.BlockSpec((bm, k), lambda i, h: (i, 0)))], _ident, (bm, k))[0]
    a_term = (a, pl.BlockSpec((bk, k), lambda h, kk: (kk, 0)))
    dy_term = (dy, pl.BlockSpec((bk, nn), lambda h, kk: (kk, h)))
    lhs, rhs = (dy_term, a_term) if w_transposed else (a_term, dy_term)
    dw = _matmul(
        name + "_dw", (nh, s // bk), [lhs + rhs + (TN,)],
        [], [(jax.ShapeDtypeStruct(w.shape, BF16), pl.BlockSpec((None,) + w.shape[1:], lambda h, kk: (h, 0, 0)))],
        _ident, w.shape[1:])[0]
    return da, dw


def _w_in_fwd(n, w_t):
    s = n.shape[0]
    bm = _row_block(s)
    nh, nout, kin = w_t.shape
    terms = [(n, pl.BlockSpec((bm, kin), lambda i, k, j=j: (i, j)),
              w_t, pl.BlockSpec((None, nout, kin), lambda i, k, j=j: (j, 0, 0)), NT) for j in range(nh)]
    row = pl.BlockSpec((bm, nout), lambda i, k: (i, 0))
    return _matmul("w_in", (s // bm, 1), terms, [], [(jax.ShapeDtypeStruct((s, nout), F32), row)], _ident, None)[0]


def _w_in_dw(dz, n):
    s, nout = dz.shape
    kin = n.shape[1] // N_CHIPS
    bk = _reduce_block(s)
    return _matmul(
        "w_in_dw", (N_CHIPS, s // bk),
        [(dz, pl.BlockSpec((bk, nout), lambda j, k: (k, 0)), n, pl.BlockSpec((bk, kin), lambda j, k: (k, j)), TN)],
        [], [(jax.ShapeDtypeStruct((N_CHIPS, nout, kin), BF16), pl.BlockSpec((None, nout, kin), lambda j, k: (j, 0, 0)))],
        _ident, (nout, kin))[0]


def _rope_tables(positions):
    half = ROPE_DIM // 2
    freqs = 1.0 / (ROPE_BASE ** (jnp.arange(0, ROPE_DIM, 2, dtype=F32) / ROPE_DIM))
    ang = positions.astype(F32)[:, None] * freqs
    cos, sin = jnp.cos(ang), jnp.sin(ang)
    z = jnp.zeros_like(cos)
    tc = jnp.concatenate([cos, cos, z, z], axis=-1)
    ta = jnp.concatenate([-sin, z, z, z], axis=-1)
    tb = jnp.concatenate([z, sin, z, z], axis=-1)
    assert tc.shape[-1] == 4 * half
    return tc, ta, tb


def _rope(x, tc, ta, tb):
    return x * tc + pltpu.roll(x, 96, 1) * ta + pltpu.roll(x, 32, 1) * tb


def _rope_t(dy, tc, ta, tb):
    return dy * tc + pltpu.roll(dy * ta, 32, 1) + pltpu.roll(dy * tb, 96, 1)


def _norm_bf16(x, g):
    r = lax.rsqrt(jnp.mean(x * x, axis=-1, keepdims=True) + RMS_EPS)
    return ((x * r) * g).astype(BF16)


def _qkv_prep(z, q_gain, kv_gain, wq_t, wkv, tc, ta, tb):
    s = z.shape[0]
    bm = _row_block(s, 512)

    def body(zq_ref, zkv_ref, zkr_ref, qg_ref, kvg_ref, wq_ref, wkv_ref, tc_ref, ta_ref, tb_ref,
             qn_ref, kvn_ref, q_ref, k_ref, v_ref):
        c, a, b = tc_ref[...], ta_ref[...], tb_ref[...]
        qn = _norm_bf16(zq_ref[...], qg_ref[...])
        kvn = _norm_bf16(zkv_ref[...], kvg_ref[...])
        qn_ref[...] = qn
        kvn_ref[...] = kvn
        kpe = _rope(zkr_ref[...], c, a, b).astype(BF16)
        for h in range(MLA_HEADS):
            lo = h * HEAD_QK
            qp = _dot(qn, wq_ref[h], NT)
            q_ref[:, lo:lo + 128] = qp[:, :128].astype(BF16)
            q_ref[:, lo + 128:lo + 256] = _rope(qp[:, 128:], c, a, b).astype(BF16)
            kv = _dot(kvn, wkv_ref[h], NN)
            k_ref[:, lo:lo + 128] = kv[:, :128].astype(BF16)
            k_ref[:, lo + 128:lo + 256] = kpe
            v_ref[:, h * HEAD_V:(h + 1) * HEAD_V] = kv[:, 128:].astype(BF16)

    def cols(width, blk):
        return pl.BlockSpec((bm, width), lambda i: (i, blk))

    def whole(a):
        return pl.BlockSpec(a.shape, lambda i: (0,) * a.ndim)

    tab = cols(128, 0)
    return _call(
        body, name="qkv_prep", grid=(s // bm,),
        in_specs=[cols(Q_LORA, 0), cols(KV_LORA, 2), cols(128, 3), whole(q_gain), whole(kv_gain), whole(wq_t),
                  whole(wkv), tab, tab, tab],
        out_specs=[cols(Q_LORA, 0), cols(KV_LORA, 0), cols(1024, 0), cols(1024, 0), cols(512, 0)],
        out_shape=[jax.ShapeDtypeStruct((s, Q_LORA), BF16), jax.ShapeDtypeStruct((s, KV_LORA), BF16),
                   jax.ShapeDtypeStruct((s, 1024), BF16), jax.ShapeDtypeStruct((s, 1024), BF16),
                   jax.ShapeDtypeStruct((s, 512), BF16)],
        sem=("parallel",), args=[z, z, z, q_gain, kv_gain, wq_t, wkv, tc, ta, tb])


def _qkv_prep_bwd(dq, dk, dv, z, qn, kvn, q_gain, kv_gain, wq_t, wkv, tc, ta, tb):
    s = z.shape[0]
    bm = _row_block(s, 512)
    nsteps = s // bm

    def body(dq_ref, dk_ref, dv_ref, zq_ref, zkv_ref, qn_ref, kvn_ref, qg_ref, kvg_ref, wq_ref, wkv_ref,
             tc_ref, ta_ref, tb_ref, dz_ref, dqg_ref, dkvg_ref, dwq_ref, dwkv_ref, wq_acc, wkv_acc):
        i = pl.program_id(0)
        c, a, b = tc_ref[...], ta_ref[...], tb_ref[...]

        @pl.when(i == 0)
        def _():
            wq_acc[...] = jnp.zeros_like(wq_acc)
            wkv_acc[...] = jnp.zeros_like(wkv_acc)

        qn, kvn = qn_ref[...], kvn_ref[...]
        dqn = jnp.zeros((bm, Q_LORA), F32)
        dkvn = jnp.zeros((bm, KV_LORA), F32)
        dpe = jnp.zeros((bm, 128), F32)
        for h in range(MLA_HEADS):
            lo = h * HEAD_QK
            dqp = jnp.concatenate([dq_ref[:, lo:lo + 128],
                                   _rope_t(dq_ref[:, lo + 128:lo + 256].astype(F32), c, a, b).astype(BF16)], axis=1)
            dqn = dqn + _dot(dqp, wq_ref[h], NN)
            wq_acc[h] += _dot(dqp, qn, TN)
            dkv = jnp.concatenate([dk_ref[:, lo:lo + 128], dv_ref[:, h * HEAD_V:(h + 1) * HEAD_V]], axis=1)
            dkvn = dkvn + _dot(dkv, wkv_ref[h], NT)
            wkv_acc[h] += _dot(kvn, dkv, TN)
            dpe = dpe + dk_ref[:, lo + 128:lo + 256].astype(F32)
        dcq, dqg = _rms_bwd_math(dqn, zq_ref[...], qg_ref[...], Q_LORA)
        dckv, dkvg = _rms_bwd_math(dkvn, zkv_ref[...], kvg_ref[...], KV_LORA)
        dz_ref[:, 0:Q_LORA] = dcq.astype(BF16)
        dz_ref[:, Q_LORA:Q_LORA + KV_LORA] = dckv.astype(BF16)
        dz_ref[:, Q_LORA + KV_LORA:512] = _rope_t(dpe, c, a, b).astype(BF16)

        @pl.when(i == 0)
        def _():
            dqg_ref[...] = dqg
            dkvg_ref[...] = dkvg

        @pl.when(i > 0)
        def _():
            dqg_ref[...] += dqg
            dkvg_ref[...] += dkvg

        @pl.when(i == nsteps - 1)
        def _():
            dwq_ref[...] = wq_acc[...].astype(BF16)
            dwkv_ref[...] = wkv_acc[...].astype(BF16)

    def cols(width, blk):
        return pl.BlockSpec((bm, width), lambda i: (i, blk))

    def whole(shape):
        return pl.BlockSpec(shape, lambda i: (0,) * len(shape))

    tab = cols(128, 0)
    return _call(
        body, name="qkv_prep_bwd", grid=(nsteps,),
        in_specs=[cols(1024, 0), cols(1024, 0), cols(512, 0), cols(Q_LORA, 0), cols(KV_LORA, 2), cols(Q_LORA, 0),
                  cols(KV_LORA, 0), whole(q_gain.shape), whole(kv_gain.shape), whole(wq_t.shape), whole(wkv.shape),
                  tab, tab, tab],
        out_specs=[cols(512, 0), whole(q_gain.shape), whole(kv_gain.shape), whole(wq_t.shape), whole(wkv.shape)],
        out_shape=[jax.ShapeDtypeStruct((s, 512), BF16), jax.ShapeDtypeStruct(q_gain.shape, F32),
                   jax.ShapeDtypeStruct(kv_gain.shape, F32), jax.ShapeDtypeStruct(wq_t.shape, BF16),
                   jax.ShapeDtypeStruct(wkv.shape, BF16)],
        scratch_shapes=[pltpu.VMEM(wq_t.shape, F32), pltpu.VMEM(wkv.shape, F32)],
        sem=("arbitrary",), args=[dq, dk, dv, z, z, qn, kvn, q_gain, kv_gain, wq_t, wkv, tc, ta, tb])


def _causal_mask(s, row0, col0):
    rows = row0 + lax.broadcasted_iota(jnp.int32, s.shape, 0)
    cols = col0 + lax.broadcasted_iota(jnp.int32, s.shape, 1)
    return jnp.where(cols <= rows, s, -jnp.inf)


def _attn_fwd(name, q, k, k_off, v, v_off, nh, dq, dv, scale, causal, blk):
    sq, sk = q.shape[0], k.shape[0]
    bq = min(blk, sq)
    bk = min(blk, sk)
    nkv = sk // bk
    assert not causal or (sq == sk and bq == bk)

    hq = bq
    log2e = 1.4426950408889634
    c2 = scale * log2e

    def body(q_ref, k_ref, v_ref, o_ref, lse_ref):
        qi = pl.program_id(1)
        qs = (q_ref[...],)

        def step(j, carry, masked):
            rows = pl.ds(pl.multiple_of(j * bk, bk), bk)
            kb, vb = k_ref[rows, :], v_ref[rows, :]
            out = []
            for t, (m, l, acc) in enumerate(carry):
                s = _dot(qs[t], kb, NT) * c2
                if masked:
                    s = _causal_mask(s, qi * bq + t * hq, j * bk)
                m_new = jnp.maximum(m, jnp.max(s, axis=-1, keepdims=True))
                alpha = jnp.exp2(m - m_new)
                p = jnp.exp2(s - m_new)
                l = alpha * l + jnp.sum(p, axis=-1, keepdims=True)
                acc = alpha * acc + _dot(p, vb, NN)
                out.append((m_new, l, acc))
            return tuple(out)

        one = (jnp.full((hq, 1), -jnp.inf, F32), jnp.zeros((hq, 1), F32), jnp.zeros((hq, dv), F32))
        init = (one,)
        if causal:
            carry = lax.fori_loop(0, qi, lambda j, c: step(j, c, False), init)
            fin = step(qi, carry, True)
        else:
            fin = lax.fori_loop(0, nkv, lambda j, c: step(j, c, False), init)
        for t, (m, l, acc) in enumerate(fin):
            o_ref[t * hq:(t + 1) * hq, :] = (acc / l).astype(o_ref.dtype)
            lse_ref[t * hq:(t + 1) * hq, :] = m * (1.0 / log2e) + jnp.log(l)

    return _call(
        body, name=name, grid=(nh, sq // bq),
        in_specs=[pl.BlockSpec((bq, dq), lambda h, i: (i, h)),
                  pl.BlockSpec((sk, dq), lambda h, i: (0, k_off + h)),
                  pl.BlockSpec((sk, dv), lambda h, i: (0, v_off + h))],
        out_specs=[pl.BlockSpec((bq, dv), lambda h, i: (i, h)), pl.BlockSpec((None, bq, 1), lambda h, i: (h, i, 0))],
        out_shape=[jax.ShapeDtypeStruct((sq, nh * dv), BF16), jax.ShapeDtypeStruct((nh, sq, 1), F32)],
        sem=("parallel", "parallel"), args=[q, k, v])


def _attn_bwd(name, q, k, k_off, v, v_off, do, do_off, lse, delta, nh, dq, dv, scale, causal, blk):
    sq, sk = q.shape[0], k.shape[0]
    bq = min(blk, sq)
    bk = min(blk, sk)
    nq = sq // bq
    assert not causal or (sq == sk and bq == bk)

    nkv = sk // bk

    def body(q_ref, k_ref, v_ref, do_ref, lse_ref, dl_ref, dq_ref, dk_ref, dv_ref, dq_acc, dk_acc, dv_acc):
        j = pl.program_id(1)

        @pl.when(j == 0)
        def _():
            dq_acc[...] = jnp.zeros_like(dq_acc)

        dk_acc[...] = jnp.zeros_like(dk_acc)
        dv_acc[...] = jnp.zeros_like(dv_acc)
        kv = k_ref[...]
        vv = v_ref[...]

        def step(i, masked):
            rows = pl.ds(pl.multiple_of(i * bq, bq), bq)
            qv = q_ref[rows, :]
            dov = do_ref[rows, :].astype(BF16)
            s = _dot(qv, kv, NT) * scale
            if masked:
                s = _causal_mask(s, i * bq, j * bk)
            p = jnp.exp(s - lse_ref[rows, :])
            dp = _dot(dov, vv, NT)
            ds = (p * (dp - dl_ref[rows, :]) * scale).astype(BF16)
            dv_acc[...] += _dot(p, dov, TN)
            dk_acc[...] += _dot(ds, qv, TN)
            dq_acc[rows, :] += _dot(ds, kv, NN)

        if causal:
            step(j, True)

            def loop(i, c):
                step(i, False)
                return c

            lax.fori_loop(j + 1, nq, loop, 0)
        else:
            def loop(i, c):
                step(i, False)
                return c

            lax.fori_loop(0, nq, loop, 0)
        dk_ref[...] = dk_acc[...].astype(dk_ref.dtype)
        dv_ref[...] = dv_acc[...].astype(dv_ref.dtype)

        @pl.when(j == nkv - 1)
        def _():
            dq_ref[...] = dq_acc[...].astype(dq_ref.dtype)

    stat = pl.BlockSpec((None, sq, 1), lambda h, j: (h, 0, 0))
    return _call(
        body, name=name, grid=(nh, sk // bk),
        in_specs=[pl.BlockSpec((sq, dq), lambda h, j: (0, h)),
                  pl.BlockSpec((bk, dq), lambda h, j: (j, k_off + h)),
                  pl.BlockSpec((bk, dv), lambda h, j: (j, v_off + h)),
                  pl.BlockSpec((sq, dv), lambda h, j: (0, do_off + h)), stat, stat],
        out_specs=[pl.BlockSpec((sq, dq), lambda h, j: (0, h)),
                   pl.BlockSpec((bk, dq), lambda h, j: (j, h)),
                   pl.BlockSpec((bk, dv), lambda h, j: (j, h))],
        out_shape=[jax.ShapeDtypeStruct((sq, nh * dq), BF16), jax.ShapeDtypeStruct((sk, nh * dq), BF16),
                   jax.ShapeDtypeStruct((sk, nh * dv), BF16)],
        scratch_shapes=[pltpu.VMEM((sq, dq), F32), pltpu.VMEM((bk, dq), F32), pltpu.VMEM((bk, dv), F32)],
        sem=("parallel", "arbitrary"), args=[q, k, v, do, lse, delta])


def _pool_diff(z, g):
    s = z.shape[0]
    t = lax.broadcasted_iota(jnp.int32, z.shape, 0)
    acc = z
    sums = []
    for k in (1, 2, 4, 8):
        acc = acc + jnp.where(t >= k, pltpu.roll(acc, k, 0), 0.0)
        sums.append(acc)
    win = jnp.where(g == 0, sums[0], jnp.where(g == 1, sums[1], jnp.where(g == 2, sums[2], sums[3])))
    w = lax.shift_left(jnp.int32(2), g)
    count = jnp.minimum(t + 1, w).astype(F32)
    del s
    return win / count - z, count


def _pool_fwd(z, pool_w, pool_scale):
    s = z.shape[0]

    def body(z_ref, w_ref, sc_ref, o_ref):
        diff, _ = _pool_diff(z_ref[...], pl.program_id(0))
        o_ref[...] = (_dot(diff, w_ref[...], NN) * sc_ref[...]).astype(o_ref.dtype)

    return _call(
        body, name="pool_fwd", grid=(POOL_GROUPS,),
        in_specs=[pl.BlockSpec((s, POOL_CH), lambda g: (0, 4 + g)),
                  pl.BlockSpec((None, POOL_CH, POOL_CH), lambda g: (g, 0, 0)),
                  pl.BlockSpec((1, POOL_CH), lambda g: (0, g))],
        out_specs=[pl.BlockSpec((s, POOL_CH), lambda g: (0, g))],
        out_shape=[jax.ShapeDtypeStruct((s, POOL_GROUPS * POOL_CH), BF16)],
        sem=("parallel",), args=[z, pool_w, pool_scale])[0]


def _pool_bwd(dcat, z, pool_w, pool_scale):
    s = z.shape[0]

    def body(dp_ref, z_ref, w_ref, sc_ref, dz_ref, dw_ref, dsc_ref):
        g = pl.program_id(0)
        diff, count = _pool_diff(z_ref[...], g)
        dpf = dp_ref[...].astype(F32)
        u = _dot(diff, w_ref[...], NN)
        dsc_ref[...] = jnp.sum(dpf * u, axis=0, keepdims=True)
        du = (dpf * sc_ref[...]).astype(BF16)
        dw_ref[...] = _dot(diff, du, TN)
        ddiff = _dot(du, w_ref[...], NT)
        t = lax.broadcasted_iota(jnp.int32, ddiff.shape, 0)
        acc = ddiff / count
        sums = []
        for k in (1, 2, 4, 8):
            acc = acc + jnp.where(t < s - k, pltpu.roll(acc, s - k, 0), 0.0)
            sums.append(acc)
        win = jnp.where(g == 0, sums[0], jnp.where(g == 1, sums[1], jnp.where(g == 2, sums[2], sums[3])))
        dz_ref[...] = (win - ddiff).astype(dz_ref.dtype)

    return pl.pallas_call(
        body, name="pool_bwd", grid=(POOL_GROUPS,),
        in_specs=[pl.BlockSpec((s, POOL_CH), lambda g: (0, 4 + g)),
                  pl.BlockSpec((s, POOL_CH), lambda g: (0, 4 + g)),
                  pl.BlockSpec((None, POOL_CH, POOL_CH), lambda g: (g, 0, 0)),
                  pl.BlockSpec((1, POOL_CH), lambda g: (0, g))],
        out_specs=[pl.BlockSpec((s, POOL_CH), lambda g: (0, g)),
                   pl.BlockSpec((None, POOL_CH, POOL_CH), lambda g: (g, 0, 0)),
                   pl.BlockSpec((1, POOL_CH), lambda g: (0, g))],
        out_shape=[jax.ShapeDtypeStruct((s, POOL_GROUPS * POOL_CH), BF16),
                   jax.ShapeDtypeStruct((POOL_GROUPS, POOL_CH, POOL_CH), F32),
                   jax.ShapeDtypeStruct((1, POOL_GROUPS * POOL_CH), F32)],
        compiler_params=_params("parallel"),
    )(dcat, z, pool_w, pool_scale)


def _local_step(x, mem, positions, target, w, grads):
    tc, ta, tb = _rope_tables(positions)
    blk = _ATT_BLOCK

    if "ffn1_shards" in w:
        n1, a1, dadu1, dadg1, w["ffn1_w_gate"], w["ffn1_w_up"], w["ffn1_w_down"] = _ffn1_up_gather(
            x, w["ffn1_norm"], *w["ffn1_shards"])
    else:
        n1 = _rmsnorm_fwd("ffn1_norm", x, w["ffn1_norm"], D_MODEL)
        a1, dadu1, dadg1 = _ffn_up("ffn1_up", n1, w["ffn1_w_gate"], w["ffn1_w_up"])
    h1, n2 = _ffn_down("ffn1_down", a1, w["ffn1_w_down"], x, w["mix_norm"])
    z = _w_in_fwd(n2, w["w_in"])
    qn, kvn, qf, kf, vf = _qkv_prep(z, w["q_norm"], w["kv_norm"], w["w_q_up"], w["w_kv_up"], tc, ta, tb)
    att, lse = _attn_fwd("mla_fwd", qf, kf, 0, vf, 0, MLA_HEADS, HEAD_QK, HEAD_V, MLA_SCALE, True, blk)
    pool = _pool_fwd(z, w["pool_w"], w["pool_scale"])
    s = x.shape[0]
    bm = _row_block(s)
    row = pl.BlockSpec((bm, D_MODEL), lambda i, k: (i, 0))
    half = pl.BlockSpec((bm, 512), lambda i, k: (i, 0))
    h2, n3 = _matmul(
        "w_out", (s // bm, 1),
        [(att, half, w["w_out"], pl.BlockSpec((512, D_MODEL), lambda i, k: (0, 0)), NN),
         (pool, half, w["w_out"], pl.BlockSpec((512, D_MODEL), lambda i, k: (1, 0)), NN)],
        [(h1, row)] + _residual_outs(s, bm, w["xattn_norm"])[0], _residual_outs(s, bm, w["xattn_norm"])[1],
        _residual_epilogue(1.0, True), None)
    memn = _rmsnorm_fwd("mem_norm", mem, w["mem_norm"], D_MODEL)
    qm = _mm_nn("w_mq", n3, w["w_mq"], BF16)
    kvm = _mm_heads_fwd("w_mkv", memn, w["w_mkv"], BF16)
    om, lse_m = _attn_fwd("xattn_fwd", qm, kvm, 0, kvm, MEM_HEADS, MEM_HEADS, MEM_HEAD_DIM, MEM_HEAD_DIM,
                          MEM_SCALE, False, 2 * blk)
    h3, n4 = _mm_nn("w_mo", om, w["w_mo"], F32, res=h2, gain=w["ffn2_norm"])
    a2, dadu2, dadg2 = _ffn_up("ffn2_up", n4, w["ffn2_w_gate"], w["ffn2_w_up"])
    dh4, dh4b, loss_vec, d_final = _ffn_down("ffn2_down", a2, w["ffn2_w_down"], h3, loss=(w["final_norm"], target))
    grads["final_norm"] = d_final

    dh3, dh3b, grads["ffn2_norm"] = _ffn_bwd("ffn2", dh4b, n4, dadg2, dadu2, a2, w["ffn2_w_gate"], w["ffn2_w_up"],
                                             w["ffn2_w_down"], grads, norm_bwd=(h3, w["ffn2_norm"], dh4),
                                             many_calls=False)

    dom, delta_m = _mm_nt("w_mo_dx", dh3b, w["w_mo"], BF16, attn_out=om, nh=MEM_HEADS, dv=MEM_HEAD_DIM)
    dqm, dkm, dvm = _attn_bwd("xattn_bwd", qm, kvm, 0, kvm, MEM_HEADS, dom, 0, lse_m, delta_m, MEM_HEADS,
                              MEM_HEAD_DIM, MEM_HEAD_DIM, MEM_SCALE, False, 2 * blk)
    dkvm = jnp.concatenate([dkm, dvm], axis=1)
    dh2, dh2b, grads["xattn_norm"] = _mm_nt_norm_bwd("w_mq_dx", dqm, w["w_mq"], h2, w["xattn_norm"], dh3)
    grads["w_mo"], grads["w_mq"] = _mm_tn_pairs("w_mo_mq_dw", [(om, dh3b), (n3, dqm)])
    dmemn, grads["w_mkv"] = _mm_heads_bwd("w_mkv", dkvm, memn, w["w_mkv"])
    _, grads["mem_norm"] = _rmsnorm_bwd("mem_norm_bwd", dmemn, mem, w["mem_norm"], D_MODEL, out_dtype=BF16)

    dcat, delta = _mm_nt("w_out_dx", dh2b, w["w_out"], BF16, attn_out=att, nh=MLA_HEADS, dv=HEAD_V)
    grads["w_out"] = _mm_tn_stacked("w_out_dw", [att, pool], dh2b)
    dzp, grads["pool_w"], grads["pool_scale"] = _pool_bwd(dcat, z, w["pool_w"], w["pool_scale"])
    dqf, dkf, dvf = _attn_bwd("mla_bwd", qf, kf, 0, vf, 0, dcat, 0, lse, delta, MLA_HEADS, HEAD_QK, HEAD_V,
                              MLA_SCALE, True, blk)
    dz_lat, grads["q_norm"], grads["kv_norm"], grads["w_q_up"], grads["w_kv_up"] = _qkv_prep_bwd(
        dqf, dkf, dvf, z, qn, kvn, w["q_norm"], w["kv_norm"], w["w_q_up"], w["w_kv_up"], tc, ta, tb)
    dz = jnp.concatenate([dz_lat, dzp], axis=1)
    grads["w_in"] = _w_in_dw(dz, n2)
    dh1, dh1b, grads["mix_norm"] = _w_in_dx_norm_bwd(dz, w["w_in"], h1, w["mix_norm"], dh2)

    dn1 = _ffn_bwd("ffn1", dh1b, n1, dadg1, dadu1, a1, w["ffn1_w_gate"], w["ffn1_w_up"], w["ffn1_w_down"], grads)
    dx, grads["ffn1_norm"] = _rmsnorm_bwd("ffn1_norm_bwd", dn1, x, w["ffn1_norm"], D_MODEL, dres=dh1)
    return loss_vec[0, 0], dx


def _mesh_pos():
    x, y, c = lax.axis_index("x"), lax.axis_index("y"), lax.axis_index("c")
    chips = [(1 - x, y), (x, 1 - y), (1 - x, 1 - y)]
    chip_ids = [2 * cx + cy for cx, cy in chips]
    return x, y, c, 2 * x + y, chips, chip_ids


def _half_rows(c, rows):
    hr = rows // 2
    return pl.ds(pl.multiple_of(c * hr, 16), hr), pl.ds(pl.multiple_of((1 - c) * hr, 16), hr)


def _ag_ici_stage(shards):
    n = len(shards)

    def copies(ins, outs):
        x, y, c, me, chips, _ = _mesh_pos()
        out = []
        for k in range(n):
            mine, _ = _half_rows(c, ins[k].shape[0])
            out.append((ins[k], outs[k].at[me], None))
            for cx, cy in chips:
                out.append((ins[k].at[mine], outs[k].at[me, mine], (cx, cy, c)))
        return out

    return _Stage(shards, [jax.ShapeDtypeStruct((N_CHIPS,) + s.shape, s.dtype) for s in shards], 3 * n, n, copies)


def _quarter_rows(c, rows):
    qr = rows // 4
    return pl.ds(pl.multiple_of(c * 2 * qr, 16), qr), pl.ds(pl.multiple_of(c * 2 * qr + qr, 16), qr)


def _ag_d2d_stage(fulls):
    n = len(fulls)

    def copies(ins, outs):
        x, y, c, me, _, chip_ids = _mesh_pos()
        out = []
        for k in range(n):
            mine, _ = _half_rows(c, ins[k].shape[1])
            for j in range(3):
                out.append((ins[k].at[chip_ids[j], mine], outs[k].at[chip_ids[j], mine], (x, y, 1 - c)))
        return out

    return _Stage(fulls, [jax.ShapeDtypeStruct(f.shape, f.dtype) for f in fulls], 3 * n, 0, copies,
                  aliases={k: k for k in range(n)})


def _rs_swap_stage(grads):
    n = len(grads)

    def copies(ins, outs):
        x, y, c, _, _, _ = _mesh_pos()
        out = []
        for k in range(n):
            _, other = _half_rows(c, ins[k].shape[1])
            out.append((ins[k].at[:, other, :], outs[k], (x, y, 1 - c)))
        return out

    return _Stage(grads, [jax.ShapeDtypeStruct((N_CHIPS, g.shape[1] // 2, g.shape[2]), g.dtype) for g in grads],
                  n, 0, copies)


_REL_OF_PEER = (2, 1, 3)


def _rs_scatter_stage(sums, relative=False):
    n = len(sums)

    def copies(ins, outs):
        x, y, c, me, chips, chip_ids = _mesh_pos()
        out = []
        for k in range(n):
            mine, _ = _half_rows(c, 2 * ins[k].shape[1])
            out.append((ins[k].at[0 if relative else me], outs[k].at[0, mine, :], None))
            for j, (cx, cy) in enumerate(chips):
                slab = _REL_OF_PEER[j] if relative else chip_ids[j]
                out.append((ins[k].at[slab], outs[k].at[1 + j, mine, :], (cx, cy, c)))
        return out

    return _Stage(sums, [jax.ShapeDtypeStruct((N_CHIPS, 2 * s.shape[1], s.shape[2]), s.dtype) for s in sums],
                  3 * n, n, copies)


def _rs_mirror_stage(parts):
    n = len(parts)

    def copies(ins, outs):
        x, y, c, _, _, _ = _mesh_pos()
        out = []
        for k in range(n):
            mine, _ = _half_rows(c, ins[k].shape[1])
            out.append((ins[k].at[:, mine, :], outs[k].at[:, mine, :], (x, y, 1 - c)))
        return out

    return _Stage(parts, [jax.ShapeDtypeStruct(p.shape, p.dtype) for p in parts], n, 0, copies,
                  aliases={k: k for k in range(n)})


def _pair_add(name, gs, r1s, core):
    n = len(gs)

    def body(c_ref, *refs):
        for k in range(n):
            g_ref, r_ref, o_ref = refs[k], refs[n + k], refs[2 * n + k]
            o_ref[...] = (g_ref[...].astype(F32) + r_ref[...].astype(F32)).astype(BF16)

    def half(g):
        return pl.BlockSpec((None, g.shape[1] // 2, g.shape[2]), lambda j, c: (j, 0, 0))

    def mine(g):
        return pl.BlockSpec((None, g.shape[1] // 2, g.shape[2]), lambda j, c: (j, c[0], 0))

    return pl.pallas_call(
        body, name=name,
        grid_spec=pltpu.PrefetchScalarGridSpec(
            num_scalar_prefetch=1, grid=(N_CHIPS,),
            in_specs=[mine(g) for g in gs] + [half(g) for g in gs], out_specs=[half(g) for g in gs]),
        out_shape=[jax.ShapeDtypeStruct((N_CHIPS, g.shape[1] // 2, g.shape[2]), BF16) for g in gs],
        compiler_params=_params("parallel"),
    )(core, *gs, *r1s)


def _adamw_math(w, g, m, v):
    m = ADAM_B1 * m + (1.0 - ADAM_B1) * g
    v = ADAM_B2 * v + (1.0 - ADAM_B2) * (g * g)
    m_hat = m / (1.0 - ADAM_B1 ** ADAM_STEP)
    v_hat = v / (1.0 - ADAM_B2 ** ADAM_STEP)
    delta = -ADAM_LR * (m_hat / (jnp.sqrt(v_hat) + ADAM_EPS) + ADAM_WD * w)
    return delta, m, v


def _adamw_sum(name, ws, parts, ms, vs):
    n = len(ws)
    r, c = ws[0].shape
    assert all(w.shape == (r, c) for w in ws)
    br = r
    while br * c * 4 > (1 << 20) and br % 32 == 0:
        br //= 2

    def body(*refs):
        for k in range(n):
            w_ref, p_ref, m_ref, v_ref = refs[4 * k:4 * k + 4]
            g_ref, d_ref, nm_ref, nv_ref = refs[4 * n + 4 * k:4 * n + 4 * k + 4]
            g = p_ref[0].astype(F32)
            for j in range(1, N_CHIPS):
                g = g + p_ref[j].astype(F32)
            d, nm, nv = _adamw_math(w_ref[...], g, m_ref[...], v_ref[...])
            g_ref[...] = g
            d_ref[...] = d
            nm_ref[...] = nm
            nv_ref[...] = nv

    spec = pl.BlockSpec((br, c), lambda i: (i, 0))
    shp = jax.ShapeDtypeStruct((r, c), F32)
    args = [a for k in range(n) for a in (ws[k], parts[k], ms[k], vs[k])]
    res = _call(
        body, name=name, grid=(r // br,),
        in_specs=[spec, pl.BlockSpec((N_CHIPS, br, c), lambda i: (0, i, 0)), spec, spec] * n,
        out_specs=[spec] * (4 * n), out_shape=[shp] * (4 * n), sem=("parallel",), args=args)
    return [res[4 * k:4 * k + 4] for k in range(n)]


_SMALL_VECTORS = ("ffn1_norm", "mix_norm", "xattn_norm", "mem_norm", "ffn2_norm", "final_norm", "q_norm",
                  "kv_norm", "pool_scale")


_LOSS_ROW = 9
_VEC_ROWS = 16
_POOL_ROWS = POOL_GROUPS * POOL_CH


def _small_params_step(g, w, m, v, loss_local):
    names = list(_SMALL_VECTORS) + ["pool_w"]
    nv = len(_SMALL_VECTORS)
    widths = [g[n].shape[1] for n in _SMALL_VECTORS]
    shapes = {"vec": (_VEC_ROWS, D_MODEL), "pool": (_POOL_ROWS, POOL_CH)}

    def body(*refs):
        ins = refs[:4 * (nv + 1) + 1]
        outs = refs[len(ins):len(ins) + 4 * (nv + 1) + 1]
        vec_own, vec_sib, vec_all, pool_sib, pool_sum, pool_all, send, recv = refs[len(ins) + len(outs):]
        g_in, w_in, m_in, v_in = (ins[k * (nv + 1):(k + 1) * (nv + 1)] for k in range(4))
        loss_in = ins[-1]
        g_out, d_out, m_out, v_out = (outs[k * (nv + 1):(k + 1) * (nv + 1)] for k in range(4))
        loss_out = outs[-1]
        x, y, c, me, chips, chip_ids = _mesh_pos()
        sib = (x, y, 1 - c)

        def remote(src, dst, k, dev):
            return pltpu.make_async_remote_copy(src_ref=src, dst_ref=dst, send_sem=send.at[k], recv_sem=recv.at[k],
                                                device_id=dev, device_id_type=_MESH)

        vec_own[...] = jnp.zeros_like(vec_own)
        for i in range(nv):
            vec_own[i:i + 1, 0:widths[i]] = g_in[i][...]
        vec_own[_LOSS_ROW:_LOSS_ROW + 1, 0:128] = loss_in[...]
        swaps = [remote(vec_own, vec_sib, 0, sib), remote(g_in[nv], pool_sib, 1, sib)]
        for cp in swaps:
            cp.start()
        for cp in swaps:
            cp.wait()
        vec_all[me] = vec_own[...] + vec_sib[...]
        pool_sum[...] = g_in[nv][...] + pool_sib[...]
        pool_all[me] = pool_sum[...]
        hv, hp = _VEC_ROWS // 2, _POOL_ROWS // 2
        mine_v = pl.ds(pl.multiple_of(c * hv, 8), hv)
        mine_p = pl.ds(pl.multiple_of(c * hp, 8), hp)
        sends = []
        for j, (cx, cy) in enumerate(chips):
            sends.append(remote(vec_all.at[me, mine_v], vec_all.at[me, mine_v], 2 + j, (cx, cy, c)))
            sends.append(remote(pool_sum.at[mine_p], pool_all.at[me, mine_p], 5 + j, (cx, cy, c)))
        for cp in sends:
            cp.start()
        for cp in sends:
            cp.wait()
        mirrors = []
        for j in range(3):
            mirrors.append(remote(vec_all.at[chip_ids[j], mine_v], vec_all.at[chip_ids[j], mine_v], 8 + j, sib))
            mirrors.append(remote(pool_all.at[chip_ids[j], mine_p], pool_all.at[chip_ids[j], mine_p], 11 + j, sib))
        for cp in mirrors:
            cp.start()
        for cp in mirrors:
            cp.wait()
        vec_tot = vec_all[0]
        pool_tot = pool_all[0]
        for i in range(1, N_CHIPS):
            vec_tot = vec_tot + vec_all[i]
            pool_tot = pool_tot + pool_all[i]
        vec_sib[...] = vec_tot
        loss_out[...] = vec_sib[_LOSS_ROW:_LOSS_ROW + 1, 0:128]
        for i in range(nv + 1):
            gi = pool_tot if i == nv else vec_sib[i:i + 1, 0:widths[i]]
            d, nm, nvv = _adamw_math(w_in[i][...], gi, m_in[i][...], v_in[i][...])
            g_out[i][...] = gi
            d_out[i][...] = d
            m_out[i][...] = nm
            v_out[i][...] = nvv

    vm = pl.BlockSpec(memory_space=pltpu.VMEM)
    args = [d[n] for d in (g, w, m, v) for n in names] + [jnp.broadcast_to(loss_local.reshape(1, 1), (1, 128))]
    out_shape = [jax.ShapeDtypeStruct(g[n].shape, F32) for _ in range(4) for n in names]
    out_shape.append(jax.ShapeDtypeStruct((1, 128), F32))
    res = pl.pallas_call(
        body, name="small_params_step", in_specs=[vm] * len(args), out_specs=[vm] * len(out_shape),
        out_shape=out_shape,
        scratch_shapes=[pltpu.VMEM(shapes["vec"], F32), pltpu.VMEM(shapes["vec"], F32),
                        pltpu.VMEM((N_CHIPS,) + shapes["vec"], F32), pltpu.VMEM(shapes["pool"], F32),
                        pltpu.VMEM(shapes["pool"], F32), pltpu.VMEM((N_CHIPS,) + shapes["pool"], F32),
                        pltpu.SemaphoreType.DMA((14,)), pltpu.SemaphoreType.DMA((14,))],
        compiler_params=pltpu.CompilerParams(vmem_limit_bytes=V7X_VMEM_LIMIT_BYTES),
    )(*args)
    k = len(names)
    dicts = [dict(zip(names, res[i * k:(i + 1) * k])) for i in range(4)]
    return dicts[0], dicts[1], dicts[2], dicts[3], res[-1]


_WEIGHTS = ("ffn1_norm", "ffn1_w_gate", "ffn1_w_up", "ffn1_w_down", "mix_norm", "w_in", "q_norm", "w_q_up",
            "kv_norm", "w_kv_up", "pool_w", "pool_scale", "w_out", "xattn_norm", "mem_norm", "w_mq", "w_mkv",
            "w_mo", "ffn2_norm", "ffn2_w_gate", "ffn2_w_up", "ffn2_w_down", "final_norm")
_SHARDED = ("ffn1_w_gate", "ffn1_w_up", "ffn1_w_down", "w_in", "w_q_up", "w_kv_up", "w_out", "w_mq", "w_mkv",
            "w_mo", "ffn2_w_gate", "ffn2_w_up", "ffn2_w_down")
W_IN_SPLIT = Q_LORA + KV_LORA + ROPE_DIM


_FFN1 = ("ffn1_w_gate", "ffn1_w_up", "ffn1_w_down")
_TRANSPOSED = ("ffn1_w_gate", "ffn1_w_up", "ffn2_w_gate", "ffn2_w_up", "w_in", "w_q_up")


def _local_view(name, a):
    return jnp.swapaxes(a, 1, 2)[0] if name in _TRANSPOSED else a[0]


def _global_view(name, a):
    return jnp.swapaxes(a[None], 1, 2) if name in _TRANSPOSED else a[None]


def _pad_shard(name, a):
    if name == "w_in":
        return jnp.concatenate([a[:W_IN_SPLIT], jnp.zeros((64, a.shape[1]), a.dtype), a[W_IN_SPLIT:]], axis=0)
    if name == "w_q_up":
        return jnp.pad(a, ((0, 64), (0, 0)))
    return a


def _unpad_shard(name, a):
    if name == "w_in":
        return jnp.concatenate([a[:, :W_IN_SPLIT], a[:, W_IN_SPLIT + 64:]], axis=1)
    if name == "w_q_up":
        return a[:, :192]
    return a


def _stacked(g):
    return g if g.ndim == 3 else g.reshape(N_CHIPS, g.shape[0] // N_CHIPS, g.shape[1])


class _Plan:
    AG_UNITS = (
        (("w_in", "w_q_up", "w_kv_up"), "ffn1_up"),
        (("w_out",), "w_in"),
        (("w_mq",), "qkv_prep"),
        (("w_mkv", "w_mo", "ffn2_w_gate"), "mla_fwd"),
        (("ffn2_w_up",), "xattn_fwd"),
        (("ffn2_w_down",), "ffn2_up"),
    )
    RS_UNITS = (
        (("ffn2_w_gate", "ffn2_w_up", "ffn2_w_down"), "ffn2_dn", "mla_bwd", "qkv_prep_bwd"),
        (("w_mo", "w_mq", "w_mkv"), "w_out_dx", "mla_bwd", "qkv_prep_bwd"),
        (("w_out", "w_q_up", "w_kv_up", "w_in"), "w_in_dx", "ffn1_dact", "ffn1_dwd"),
        (("ffn1_w_down",), "ffn1_dwg", "ffn1_dwu", "ffn1_dn_a"),
        (("ffn1_w_gate",), "ffn1_dwu", "ffn1_dn_a", "ffn1_dn_b"),
        (("ffn1_w_up",), "ffn1_dn_a", "ffn1_dn_b", "adamw_w_kv_up"),
    )
    ADAMW_ORDER = (("w_kv_up",), ("ffn2_w_gate", "ffn2_w_up"), ("ffn2_w_down", "ffn1_w_down"), ("w_mo", "w_mq", "w_out"),
                   ("w_mkv",), ("w_q_up",), ("w_in",), ("ffn1_w_gate", "ffn1_w_up"))

    def __init__(self, shards, w, grads, core):
        self.shards, self.w, self.grads, self.core = shards, w, grads, core
        self.last_slab_step = 0
        self.parts = {}
        self.ag = [None for _ in self.AG_UNITS]
        self.rs = [[None, None, None, None] for _ in self.RS_UNITS]

    def pre(self, name):
        for i, (names, host) in enumerate(self.AG_UNITS):
            if name == host:
                st = _ag_ici_stage([self.shards[n] for n in names])
                st.then = _ag_d2d_stage(st.outs)
                st.start_step = self.last_slab_step if name == "ffn1_up" else 0
                self.ag[i] = _host(name, st)
        for i, (names, h1, h2, h3) in enumerate(self.RS_UNITS):
            if name == h1:
                self.rs[i][0] = _host(name, _rs_swap_stage([_stacked(self.grads[n]) for n in names]))
            if name == h2:
                self.rs[i][2] = _host(name, _rs_scatter_stage(self.rs[i][1], relative=names[0] in _FFN1))
            if name == h3:
                self.rs[i][3] = _host(name, _rs_mirror_stage(self.rs[i][2].results))

    def post(self, name):
        for i, (names, host) in enumerate(self.AG_UNITS):
            if name == host:
                for n, f in zip(names, self.ag[i].results):
                    self.w[n] = _full_weight(n, f)
        for i, (names, h1, h2, h3) in enumerate(self.RS_UNITS):
            if name == h1:
                self.rs[i][1] = list(_pair_add("pair_add_" + names[0], [_stacked(self.grads[n]) for n in names],
                                               self.rs[i][0].results, self.core))
            if name == h3:
                for n, p in zip(names, self.rs[i][3].results):
                    self.parts[n] = p


def _full_weight(name, stacked):
    if name in ("w_out", "w_mq", "w_mo"):
        return stacked.reshape(D_MODEL, D_MODEL)
    return stacked


def kernel(x, mem, positions, ffn1_norm, ffn1_w_gate, ffn1_w_up, ffn1_w_down, mix_norm, w_in, q_norm, w_q_up, kv_norm, w_kv_up, pool_w, pool_scale, w_out, xattn_norm, mem_norm, w_mq, w_mkv, w_mo, ffn2_norm, ffn2_w_gate, ffn2_w_up, ffn2_w_down, final_norm, loss_target, m_ffn1_norm, m_ffn1_w_gate, m_ffn1_w_up, m_ffn1_w_down, m_mix_norm, m_w_in, m_q_norm, m_w_q_up, m_kv_norm, m_w_kv_up, m_pool_w, m_pool_scale, m_w_out, m_xattn_norm, m_mem_norm, m_w_mq, m_w_mkv, m_w_mo, m_ffn2_norm, m_ffn2_w_gate, m_ffn2_w_up, m_ffn2_w_down, m_final_norm, v_ffn1_norm, v_ffn1_w_gate, v_ffn1_w_up, v_ffn1_w_down, v_mix_norm, v_w_in, v_q_norm, v_w_q_up, v_kv_norm, v_w_kv_up, v_pool_w, v_pool_scale, v_w_out, v_xattn_norm, v_mem_norm, v_w_mq, v_w_mkv, v_w_mo, v_ffn2_norm, v_ffn2_w_gate, v_ffn2_w_up, v_ffn2_w_down, v_final_norm):
    wts = dict(zip(_WEIGHTS, (ffn1_norm, ffn1_w_gate, ffn1_w_up, ffn1_w_down, mix_norm, w_in, q_norm, w_q_up, kv_norm, w_kv_up, pool_w, pool_scale, w_out, xattn_norm, mem_norm, w_mq, w_mkv, w_mo, ffn2_norm, ffn2_w_gate, ffn2_w_up, ffn2_w_down, final_norm)))
    mom = dict(zip(_WEIGHTS, (m_ffn1_norm, m_ffn1_w_gate, m_ffn1_w_up, m_ffn1_w_down, m_mix_norm, m_w_in, m_q_norm, m_w_q_up, m_kv_norm, m_w_kv_up, m_pool_w, m_pool_scale, m_w_out, m_xattn_norm, m_mem_norm, m_w_mq, m_w_mkv, m_w_mo, m_ffn2_norm, m_ffn2_w_gate, m_ffn2_w_up, m_ffn2_w_down, m_final_norm)))
    var = dict(zip(_WEIGHTS, (v_ffn1_norm, v_ffn1_w_gate, v_ffn1_w_up, v_ffn1_w_down, v_mix_norm, v_w_in, v_q_norm, v_w_q_up, v_kv_norm, v_w_kv_up, v_pool_w, v_pool_scale, v_w_out, v_xattn_norm, v_mem_norm, v_w_mq, v_w_mkv, v_w_mo, v_ffn2_norm, v_ffn2_w_gate, v_ffn2_w_up, v_ffn2_w_down, v_final_norm)))
    small = [n for n in _WEIGHTS if n not in _SHARDED]

    global _PLAN
    shards = {n: _pad_shard(n, _local_view(n, wts[n])).astype(BF16) for n in _SHARDED}
    w = {n: wts[n].reshape(1, -1) for n in _SMALL_VECTORS}
    w["pool_w"] = pool_w[0].astype(BF16)
    grads = {}
    core = lax.axis_index("c").astype(jnp.int32).reshape(1)
    plan = _Plan(shards, w, grads, core)
    _PLAN = plan
    try:
        w["ffn1_shards"] = tuple(shards[n] for n in _FFN1)

        loss_local, dx = _local_step(x[0], mem[0], positions[0], loss_target[0], w, grads)

        def small_view(d):
            out = {n: d[n].reshape(1, -1) for n in _SMALL_VECTORS}
            out["pool_w"] = d["pool_w"].reshape(_POOL_ROWS, POOL_CH)
            return out

        *small_res, loss_vec = _small_params_step(small_view(grads), small_view(wts), small_view(mom),
                                                  small_view(var), loss_local)
        g_out, d_out, m_out, v_out = ({n: r[n].reshape(wts[n].shape) for n in small} for r in small_res)
        loss = loss_vec[0, 0]

        for names in _Plan.ADAMW_ORDER:
            res = _adamw_sum("adamw_" + names[0], [_local_view(n, wts[n]) for n in names],
                             [_unpad_shard(n, plan.parts[n]) for n in names],
                             [_local_view(n, mom[n]) for n in names], [_local_view(n, var[n]) for n in names])
            for n, r4 in zip(names, res):
                g_out[n], d_out[n], m_out[n], v_out[n] = (_global_view(n, r) for r in r4)
    finally:
        _PLAN = None
        _PENDING.clear()

    return (loss, dx[None], *[g_out[n] for n in _WEIGHTS], *[d_out[n] for n in _WEIGHTS],
            *[m_out[n] for n in _WEIGHTS], *[v_out[n] for n in _WEIGHTS])
```

```python
import jax
import jax.numpy as jnp
from jax import lax
from jax.experimental import pallas as pl
from jax.experimental.pallas import tpu as pltpu

F32 = jnp.float32
BF16 = jnp.bfloat16

D_MODEL = 1024
D_FF = 2816
N_CHIPS = 4
FF_SHARD = D_FF // N_CHIPS
MLA_HEADS = 4
Q_LORA = 256
KV_LORA = 128
ROPE_DIM = 64
HEAD_QK = 256
HEAD_V = 128
POOL_GROUPS = 4
POOL_CH = 128
MEM_HEADS = 4
MEM_HEAD_DIM = 256
RMS_EPS = 1e-6
ROPE_BASE = 10000.0
MLA_SCALE = (128 + 64) ** -0.5
MEM_SCALE = MEM_HEAD_DIM ** -0.5

ADAM_LR = 0.001
ADAM_B1 = 0.9
ADAM_B2 = 0.999
ADAM_EPS = 1e-08
ADAM_WD = 0.01
ADAM_STEP = 10

V7X_VMEM_LIMIT_BYTES = 56 * 1024 * 1024

NN = ((1,), (0,))
NT = ((1,), (1,))
TN = ((0,), (0,))


def _params(*sem):
    return pltpu.CompilerParams(dimension_semantics=sem, vmem_limit_bytes=V7X_VMEM_LIMIT_BYTES)


_MESH = pl.DeviceIdType.MESH
_ANY = pl.BlockSpec(memory_space=pl.ANY)


class _Stage:
    def __init__(self, ins, outs, n_remote, n_local, copies, aliases=None):
        self.ins, self.outs, self.n_remote, self.n_local = list(ins), list(outs), n_remote, n_local
        self.copies, self.aliases = copies, dict(aliases or {})
        self.results = None
        self.start_step = 0
        self.then = None

    def descriptors(self, in_refs, out_refs, send, recv, loc):
        ds, ri, li = [], 0, 0
        for src, dst, dev in self.copies(in_refs, out_refs):
            if dev is None:
                ds.append(pltpu.make_async_copy(src, dst, loc.at[li]))
                li += 1
            else:
                ds.append(pltpu.make_async_remote_copy(src_ref=src, dst_ref=dst, send_sem=send.at[ri],
                                                       recv_sem=recv.at[ri], device_id=dev, device_id_type=_MESH))
                ri += 1
        assert ri == self.n_remote and li == self.n_local
        return ds


_PENDING = {}


def _host(name, stage):
    _PENDING.setdefault(name, []).append(stage)
    return stage


_PLAN = None


def _call(body, **kw):
    if _PLAN is not None:
        _PLAN.pre(kw["name"])
    res = _call_hosting(body, **kw)
    if _PLAN is not None:
        _PLAN.post(kw["name"])
    return res


def _call_hosting(body, *, name, grid, in_specs, out_specs, out_shape, sem, args, scratch_shapes=(), aliases=None):
    stages = _PENDING.pop(name, [])
    scratch_shapes = list(scratch_shapes)
    if not stages:
        return pl.pallas_call(body, name=name, grid=grid, in_specs=in_specs, out_specs=out_specs,
                              out_shape=out_shape, scratch_shapes=scratch_shapes,
                              input_output_aliases=dict(aliases or {}), compiler_params=_params(*sem))(*args)
    ni, no, ns = len(in_specs), len(out_shape), len(scratch_shapes)
    c_ins = [a for st in stages for a in st.ins]
    c_outs = [o for st in stages for o in st.outs]
    nci, nco = len(c_ins), len(c_outs)
    aliases, io, oo = dict(aliases or {}), 0, 0
    for st in stages:
        for i, j in st.aliases.items():
            aliases[ni + io + i] = no + oo + j
        io += len(st.ins)
        oo += len(st.outs)
    dma = pltpu.SemaphoreType.DMA
    sems = []
    for st in stages:
        sems += [dma((max(st.n_remote, 1),)), dma((max(st.n_remote, 1),)), dma((max(st.n_local, 1),))]
    followers = [st.then for st in stages if st.then is not None]
    for st in followers:
        sems += [dma((max(st.n_remote, 1),)), dma((max(st.n_remote, 1),)), dma((max(st.n_local, 1),))]

    def wrapped(*refs):
        ins, cin = refs[:ni], refs[ni:ni + nci]
        outs, cout = refs[ni + nci:ni + nci + no], refs[ni + nci + no:ni + nci + no + nco]
        scr = refs[ni + nci + no + nco:ni + nci + no + nco + ns]
        sem_refs = refs[ni + nci + no + nco + ns:]
        step = pl.program_id(0)
        last = pl.program_id(0) == grid[0] - 1
        for ax in range(1, len(grid)):
            step = step * grid[ax] + pl.program_id(ax)
            last = jnp.logical_and(last, pl.program_id(ax) == grid[ax] - 1)

        def descriptors(si):
            io = sum(len(st.ins) for st in stages[:si])
            oo = sum(len(st.outs) for st in stages[:si])
            st = stages[si]
            return st.descriptors(cin[io:io + len(st.ins)], cout[oo:oo + len(st.outs)], *sem_refs[3 * si:3 * si + 3])

        def follower_descriptors(fi):
            si = [k for k, st in enumerate(stages) if st.then is not None][fi]
            oo = sum(len(st.outs) for st in stages[:si])
            bufs = cout[oo:oo + len(stages[si].outs)]
            k0 = 3 * (len(stages) + fi)
            return followers[fi].descriptors(bufs, bufs, *sem_refs[k0:k0 + 3])

        def start(si):
            @pl.when(step == stages[si].start_step)
            def _():
                for d in descriptors(si):
                    d.start()

        for si, st in enumerate(stages):
            if st.start_step == 0:
                start(si)
        body(*ins, *outs, *scr)
        for si, st in enumerate(stages):
            if st.start_step != 0:
                start(si)

        @pl.when(last)
        def _():
            for si in range(len(stages)):
                for d in descriptors(si):
                    d.wait()
            for fi in range(len(followers)):
                for d in follower_descriptors(fi):
                    d.start()
            for fi in range(len(followers)):
                for d in follower_descriptors(fi):
                    d.wait()

    res = pl.pallas_call(
        wrapped, name=name, grid=grid, in_specs=list(in_specs) + [_ANY] * nci,
        out_specs=list(out_specs) + [_ANY] * nco, out_shape=list(out_shape) + c_outs,
        scratch_shapes=scratch_shapes + sems, input_output_aliases=aliases,
        compiler_params=_params(*(("arbitrary",) * len(grid))))(*args, *c_ins)
    oo = no
    for st in stages:
        st.results = list(res[oo:oo + len(st.outs)])
        oo += len(st.outs)
    return list(res[:no])


def _dot(a, b, dims):
    return lax.dot_general(a.astype(BF16), b.astype(BF16), (dims, ((), ())), preferred_element_type=F32)


_MAX_ROW_BLOCK = 1024
_ATT_BLOCK = 512


_MAX_REDUCE_BLOCK = 2048


def _row_block(s, want=1024):
    return min(want, s, _MAX_ROW_BLOCK)


def _reduce_block(s):
    return min(s, _MAX_REDUCE_BLOCK)


def _matmul(name, grid, terms, extras, outs, epilogue, acc_shape, fill=(), summed=()):
    nt, ne, no, nf = len(terms), len(extras), len(outs), len(fill)
    nk = grid[-1]
    dims = [t[4] for t in terms]

    def body(*refs):
        a_refs, b_refs = refs[:nt], refs[nt:2 * nt]
        e_refs = refs[2 * nt:2 * nt + ne]
        o_refs = refs[2 * nt + ne + nf:2 * nt + ne + nf + no]

        def finish(acc):
            vals = epilogue(acc, *[e[...] for e in e_refs])
            for idx, (o, val) in enumerate(zip(o_refs, vals)):
                if idx in summed:
                    @pl.when(pl.program_id(0) == 0)
                    def _(o=o, val=val):
                        o[...] = val.astype(o.dtype)

                    @pl.when(pl.program_id(0) > 0)
                    def _(o=o, val=val):
                        o[...] += val.astype(o.dtype)
                else:
                    o[...] = val.astype(o.dtype)

        if nk == 1:
            part = None
            for a, b, d in zip(a_refs, b_refs, dims):
                t = _dot(a[...], b[...], d)
                part = t if part is None else part + t
            finish(part)
        else:
            acc_ref = refs[-1]
            k = pl.program_id(len(grid) - 1)

            @pl.when(k == 0)
            def _():
                acc_ref[...] = jnp.zeros_like(acc_ref)

            for a, b, d in zip(a_refs, b_refs, dims):
                acc_ref[...] += _dot(a[...], b[...], d)

            @pl.when(k == nk - 1)
            def _():
                finish(acc_ref[...])

    in_specs = [t[1] for t in terms] + [t[3] for t in terms] + [e[1] for e in extras] + [_ANY] * nf
    args = [t[0] for t in terms] + [t[2] for t in terms] + [e[0] for e in extras] + list(fill)
    sem = ("arbitrary" if summed else "parallel",) * (len(grid) - 1) + ("arbitrary",)
    aliases = {2 * nt + ne + i: i for i in range(nf)}
    return _call(
        body, name=name, grid=grid, in_specs=in_specs,
        out_specs=[o[1] for o in outs], out_shape=[o[0] for o in outs],
        scratch_shapes=[pltpu.VMEM(acc_shape, F32)] if nk > 1 else [], sem=sem, args=args, aliases=aliases)


def _ident(acc):
    return (acc,)


def _rmsnorm_fwd(name, x, gain, width, col_block=0):
    s = x.shape[0]
    bm = _row_block(s)

    def body(x_ref, g_ref, o_ref):
        xf = x_ref[...]
        r = lax.rsqrt(jnp.mean(xf * xf, axis=-1, keepdims=True) + RMS_EPS)
        o_ref[...] = ((xf * r) * g_ref[...]).astype(o_ref.dtype)

    return pl.pallas_call(
        body, name=name, grid=(s // bm,),
        in_specs=[pl.BlockSpec((bm, width), lambda i: (i, col_block)), pl.BlockSpec((1, width), lambda i: (0, 0))],
        out_specs=pl.BlockSpec((bm, width), lambda i: (i, 0)),
        out_shape=jax.ShapeDtypeStruct((s, width), BF16),
        compiler_params=_params("parallel"),
    )(x, gain)


def _rms_bwd_math(dy, xf, g, width):
    r = lax.rsqrt(jnp.mean(xf * xf, axis=-1, keepdims=True) + RMS_EPS)
    dyg = dy * g
    dot = jnp.sum(dyg * xf, axis=-1, keepdims=True)
    dx = r * dyg - xf * ((r * r * r) * (dot * (1.0 / width)))
    dgain = jnp.sum(dy * (xf * r), axis=0, keepdims=True)
    return dx, dgain


def _rmsnorm_bwd(name, dy, x, gain, width, col_block=0, dres=None, out_dtype=F32):
    s = x.shape[0]
    bm = _row_block(s)
    has_res = dres is not None

    def body(*refs):
        if has_res:
            dy_ref, x_ref, g_ref, r_ref, dx_ref, dg_ref = refs
        else:
            dy_ref, x_ref, g_ref, dx_ref, dg_ref = refs
        dx, dgain = _rms_bwd_math(dy_ref[...].astype(F32), x_ref[...], g_ref[...], width)
        if has_res:
            dx = dx + r_ref[...]
        dx_ref[...] = dx.astype(dx_ref.dtype)

        @pl.when(pl.program_id(0) == 0)
        def _():
            dg_ref[...] = dgain

        @pl.when(pl.program_id(0) > 0)
        def _():
            dg_ref[...] += dgain

    row = pl.BlockSpec((bm, width), lambda i: (i, 0))
    in_specs = [row, pl.BlockSpec((bm, width), lambda i: (i, col_block)), pl.BlockSpec((1, width), lambda i: (0, 0))]
    args = [dy, x, gain]
    out_specs = [row, pl.BlockSpec((1, width), lambda i: (0, 0))]
    out_shape = [jax.ShapeDtypeStruct((s, width), out_dtype), jax.ShapeDtypeStruct((1, width), F32)]
    if has_res:
        in_specs.append(row)
        args.append(dres)
    return _call(body, name=name, grid=(s // bm,), in_specs=in_specs, out_specs=out_specs, out_shape=out_shape,
                 sem=("arbitrary",), args=args)


def _ffn_up(name, n, wg, wu):
    s = n.shape[0]
    bm = _row_block(s)

    def body(n_ref, wg_ref, wu_ref, a_ref, dadu_ref, dadg_ref):
        x = n_ref[...]
        g = _dot(x, wg_ref[...], NT)
        u = _dot(x, wu_ref[...], NT)
        sg = jax.nn.sigmoid(g)
        silu = g * sg
        a_ref[...] = (silu * u).astype(BF16)
        dadu_ref[...] = silu.astype(BF16)
        dadg_ref[...] = (u * (sg * (1.0 + g * (1.0 - sg)))).astype(BF16)

    w_spec = pl.BlockSpec((None, FF_SHARD, D_MODEL), lambda j, i: (j, 0, 0))
    o_spec = pl.BlockSpec((None, bm, FF_SHARD), lambda j, i: (j, i, 0))
    shp = jax.ShapeDtypeStruct((N_CHIPS, s, FF_SHARD), BF16)
    return _call(
        body, name=name, grid=(N_CHIPS, s // bm),
        in_specs=[pl.BlockSpec((bm, D_MODEL), lambda j, i: (i, 0)), w_spec, w_spec],
        out_specs=[o_spec, o_spec, o_spec], out_shape=[shp, shp, shp],
        sem=("parallel", "parallel"), args=[n, wg, wu])


def _ffn1_up_gather(xin, gain, g_sh, u_sh, d_sh):
    s = xin.shape[0]
    bm = _row_block(s)
    nrb = s // bm
    rows, cols = g_sh.shape

    def body(x_ref, gain_ref, gs, us, ds, n_ref, a_ref, dadu_ref, dadg_ref, wg, wu, wd, gbuf, ubuf,
             send, recv, qsend, qrecv, fsend, frecv, loc, ld):
        r, i = pl.program_id(0), pl.program_id(1)
        x, y, c = lax.axis_index("x"), lax.axis_index("y"), lax.axis_index("c")
        sib = (x, y, 1 - c)
        mine, _ = _half_rows(c, rows)
        quarters = _quarter_rows(c, rows)
        shards, fulls, bufs = (gs, us, ds), (wg, wu, wd), (gbuf, ubuf)

        def remote(src, dst, ssem, rsem, dev):
            return pltpu.make_async_remote_copy(src_ref=src, dst_ref=dst, send_sem=ssem, recv_sem=rsem,
                                                device_id=dev, device_id_type=_MESH)

        def peer(rel):
            return ((1 - x) if rel & 2 else x, (1 - y) if rel & 1 else y, c)

        def ici(k, rel, dev=sib):
            return remote(shards[k].at[mine], fulls[k].at[rel, mine], send.at[k, rel - 1], recv.at[k, rel - 1], dev)

        def quarter(k, which, dev=sib):
            slab, q = ((2, quarters[0]), (1, quarters[1]))[which]
            return remote(fulls[k].at[slab, q], fulls[k].at[3, q], qsend.at[k, which], qrecv.at[k, which], dev)

        def fwd(k, rel):
            return remote(fulls[k].at[rel, mine], fulls[k].at[rel, mine], fsend.at[k, rel - 1], frecv.at[k, rel - 1], sib)

        def own(k):
            return pltpu.make_async_copy(shards[k], fulls[k].at[0], loc.at[k])

        def load(slab):
            for k in (0, 1):
                pltpu.make_async_copy(shards[k] if slab == 0 else fulls[k].at[slab], bufs[k], ld.at[k]).start()
            for k in (0, 1):
                pltpu.make_async_copy(shards[k] if slab == 0 else fulls[k].at[slab], bufs[k], ld.at[k]).wait()

        def from_neighbour(ks, rel):
            for k in ks:
                ici(k, rel).wait_recv()
                fwd(k, rel).start()
                quarter(k, 0 if rel == 2 else 1, peer(1 if rel == 2 else 2)).start()
            for k in ks:
                fwd(k, rel).wait_recv()

        def from_diagonal(ks):
            for k in ks:
                quarter(k, 0).wait_recv()
                quarter(k, 1).wait_recv()
                fwd(k, 3).start()
            for k in ks:
                fwd(k, 3).wait_recv()

        @pl.when(jnp.logical_and(r == 0, i == 0))
        def _():
            for k in range(3):
                own(k).start()
            for rel in (1, 2):
                for k in (0, 1):
                    ici(k, rel, peer(rel)).start()
            load(0)

        @pl.when(jnp.logical_and(r == 1, i == 0))
        def _():
            from_neighbour((0, 1), 1)
            load(1)
            for rel in (1, 2):
                ici(2, rel, peer(rel)).start()

        @pl.when(jnp.logical_and(r == 2, i == 0))
        def _():
            from_neighbour((0, 1), 2)
            load(2)

        @pl.when(jnp.logical_and(r == 3, i == 0))
        def _():
            from_diagonal((0, 1))
            load(3)

        xv = _norm_bf16(x_ref[...], gain_ref[...])

        @pl.when(r == 0)
        def _():
            n_ref[...] = xv

        g = _dot(xv, gbuf[...], NT)
        u = _dot(xv, ubuf[...], NT)
        sg = jax.nn.sigmoid(g)
        silu = g * sg
        a_ref[...] = (silu * u).astype(BF16)
        dadu_ref[...] = silu.astype(BF16)
        dadg_ref[...] = (u * (sg * (1.0 + g * (1.0 - sg)))).astype(BF16)

        @pl.when(jnp.logical_and(r == 3, i == nrb - 1))
        def _():
            from_neighbour((2,), 1)
            from_neighbour((2,), 2)
            from_diagonal((2,))
            for k in range(3):
                for rel in (1, 2):
                    ici(k, rel).wait_send()
                for which in (0, 1):
                    quarter(k, which).wait_send()
                for rel in (1, 2, 3):
                    fwd(k, rel).wait_send()
                own(k).wait()

    o_spec = pl.BlockSpec((None, bm, FF_SHARD), lambda r, i: (r, i, 0))
    act = jax.ShapeDtypeStruct((N_CHIPS, s, FF_SHARD), BF16)
    full = jax.ShapeDtypeStruct((N_CHIPS, rows, cols), BF16)
    dma = pltpu.SemaphoreType.DMA
    if _PLAN is not None:
        _PLAN.last_slab_step = 3 * nrb
    n_spec = pl.BlockSpec((bm, D_MODEL), lambda r, i: (jnp.where(r == 0, i, nrb - 1), 0))
    return _call(
        body, name="ffn1_up", grid=(N_CHIPS, nrb),
        in_specs=[pl.BlockSpec((bm, D_MODEL), lambda r, i: (i, 0)), pl.BlockSpec((1, D_MODEL), lambda r, i: (0, 0)),
                  _ANY, _ANY, _ANY],
        out_specs=[n_spec, o_spec, o_spec, o_spec, _ANY, _ANY, _ANY],
        out_shape=[jax.ShapeDtypeStruct((s, D_MODEL), BF16), act, act, act, full, full, full],
        scratch_shapes=[pltpu.VMEM((rows, cols), BF16), pltpu.VMEM((rows, cols), BF16), dma((3, 2)), dma((3, 2)),
                        dma((3, 2)), dma((3, 2)), dma((3, 3)), dma((3, 3)), dma((3,)), dma((2,))],
        sem=("arbitrary", "arbitrary"), args=[xin, gain, g_sh, u_sh, d_sh])


def _residual_epilogue(alpha, with_norm):
    if not with_norm:
        return lambda acc, r: (r + alpha * acc,)

    def epilogue(acc, r, g):
        h = r + alpha * acc
        rs = lax.rsqrt(jnp.mean(h * h, axis=-1, keepdims=True) + RMS_EPS)
        return h, (h * rs) * g

    return epilogue


def _residual_outs(s, bm, gain):
    row = pl.BlockSpec((bm, D_MODEL), lambda i, k: (i, 0))
    outs = [(jax.ShapeDtypeStruct((s, D_MODEL), F32), row)]
    if gain is None:
        return [], outs
    return [(gain, pl.BlockSpec((1, D_MODEL), lambda i, k: (0, 0)))], outs + [(jax.ShapeDtypeStruct((s, D_MODEL), BF16), row)]


def _loss_epilogue(acc, res, g, target):
    d = acc.shape[-1]
    h = res + 0.5 * acc
    r = lax.rsqrt(jnp.mean(h * h, axis=-1, keepdims=True) + RMS_EPS)
    err = (h * r) * g - target
    part = 0.5 * jnp.sum(jnp.mean(err * err, axis=-1, keepdims=True), axis=0, keepdims=True)
    dx, dgain = _rms_bwd_math(err * (1.0 / d), h, g, d)
    return dx, dx, jnp.broadcast_to(part, (1, 128)), dgain


def _ffn_down(name, a, wd, res, gain=None, loss=None):
    s = a.shape[1]
    bm = _row_block(s, 512)
    row = pl.BlockSpec((bm, D_MODEL), lambda i, k: (i, 0))
    terms = [(a, pl.BlockSpec((None, bm, FF_SHARD), lambda i, k, j=j: (j, i, 0)),
              wd, pl.BlockSpec((None, FF_SHARD, D_MODEL), lambda i, k, j=j: (j, 0, 0)), NN) for j in range(N_CHIPS)]
    if loss is not None:
        vec = pl.BlockSpec((1, D_MODEL), lambda i, k: (0, 0))
        outs = [(jax.ShapeDtypeStruct((s, D_MODEL), F32), row), (jax.ShapeDtypeStruct((s, D_MODEL), BF16), row),
                (jax.ShapeDtypeStruct((1, 128), F32), pl.BlockSpec((1, 128), lambda i, k: (0, 0))),
                (jax.ShapeDtypeStruct((1, D_MODEL), F32), vec)]
        return _matmul(name, (s // bm, 1), terms, [(res, row), (loss[0], vec), (loss[1], row)], outs,
                       _loss_epilogue, None, summed=(2, 3))
    extras, outs = _residual_outs(s, bm, gain)
    res_out = _matmul(name, (s // bm, 1), terms, [(res, row)] + extras, outs,
                      _residual_epilogue(0.5, gain is not None), None)
    return res_out if gain is not None else res_out[0]


def _norm_bwd_epilogue(width):
    def epilogue(acc, h, g, dres):
        dx, dgain = _rms_bwd_math(acc, h, g, width)
        dx = dx + dres
        return dx, dx, dgain

    return epilogue


def _norm_bwd_operands(s, bm, h, gain, dres):
    row = pl.BlockSpec((bm, D_MODEL), lambda i, k: (i, 0))
    vec = pl.BlockSpec((1, D_MODEL), lambda i, k: (0, 0))
    extras = [(h, row), (gain, vec), (dres, row)]
    outs = [(jax.ShapeDtypeStruct((s, D_MODEL), F32), row), (jax.ShapeDtypeStruct((s, D_MODEL), BF16), row),
            (jax.ShapeDtypeStruct((1, D_MODEL), F32), vec)]
    return extras, outs, (2,)


def _ffn_bwd(tag, dh, n, dadg, dadu, a, wg, wu, wd, grads, norm_bwd=None, many_calls=True):
    s = dh.shape[0]
    bm = _row_block(s)
    bk = _reduce_block(s)
    nk = s // bk

    def act_bwd(acc, dg_da, du_da):
        da = 0.5 * acc
        return da * dg_da.astype(F32), da * du_da.astype(F32)

    slab = pl.BlockSpec((None, bm, FF_SHARD), lambda j, i, k: (j, i, 0))
    shp = jax.ShapeDtypeStruct((N_CHIPS, s, FF_SHARD), BF16)
    dg, du = _matmul(
        tag + "_dact", (N_CHIPS, s // bm, 1),
        [(dh, pl.BlockSpec((bm, D_MODEL), lambda j, i, k: (i, 0)),
          wd, pl.BlockSpec((None, FF_SHARD, D_MODEL), lambda j, i, k: (j, 0, 0)), NT)],
        [(dadg, slab), (dadu, slab)], [(shp, slab), (shp, slab)], act_bwd, None)

    grads[tag + "_w_down"] = _matmul(
        tag + "_dwd", (N_CHIPS, nk),
        [(a, pl.BlockSpec((None, bk, FF_SHARD), lambda j, k: (j, k, 0)),
          dh, pl.BlockSpec((bk, D_MODEL), lambda j, k: (k, 0)), TN)],
        [], [(jax.ShapeDtypeStruct((N_CHIPS, FF_SHARD, D_MODEL), BF16),
              pl.BlockSpec((None, FF_SHARD, D_MODEL), lambda j, k: (j, 0, 0)))],
        lambda acc: (0.5 * acc,), (FF_SHARD, D_MODEL))[0]

    def dw_up(nm, dact):
        return _matmul(
            nm, (N_CHIPS, nk),
            [(dact, pl.BlockSpec((None, bk, FF_SHARD), lambda j, k: (j, k, 0)),
              n, pl.BlockSpec((bk, D_MODEL), lambda j, k: (k, 0)), TN)],
            [], [(jax.ShapeDtypeStruct((N_CHIPS, FF_SHARD, D_MODEL), BF16),
                  pl.BlockSpec((None, FF_SHARD, D_MODEL), lambda j, k: (j, 0, 0)))],
            _ident, (FF_SHARD, D_MODEL))[0]

    def dw_up_pair(nm):
        def body(dg_ref, du_ref, n_ref, og_ref, ou_ref, acc_g, acc_u):
            k = pl.program_id(1)

            @pl.when(k == 0)
            def _():
                acc_g[...] = jnp.zeros_like(acc_g)
                acc_u[...] = jnp.zeros_like(acc_u)

            nv = n_ref[...]
            acc_g[...] += _dot(dg_ref[...], nv, TN)
            acc_u[...] += _dot(du_ref[...], nv, TN)

            @pl.when(k == nk - 1)
            def _():
                og_ref[...] = acc_g[...].astype(BF16)
                ou_ref[...] = acc_u[...].astype(BF16)

        act = pl.BlockSpec((None, bk, FF_SHARD), lambda j, k: (j, k, 0))
        out = pl.BlockSpec((None, FF_SHARD, D_MODEL), lambda j, k: (j, 0, 0))
        shape = jax.ShapeDtypeStruct((N_CHIPS, FF_SHARD, D_MODEL), BF16)
        return _call(body, name=nm, grid=(N_CHIPS, nk),
                     in_specs=[act, act, pl.BlockSpec((bk, D_MODEL), lambda j, k: (k, 0))], out_specs=[out, out],
                     out_shape=[shape, shape],
                     scratch_shapes=[pltpu.VMEM((FF_SHARD, D_MODEL), F32), pltpu.VMEM((FF_SHARD, D_MODEL), F32)],
                     sem=("parallel", "arbitrary"), args=[dg, du, n])

    if many_calls:
        grads[tag + "_w_gate"] = dw_up(tag + "_dwg", dg)
        grads[tag + "_w_up"] = dw_up(tag + "_dwu", du)
    else:
        grads[tag + "_w_gate"], grads[tag + "_w_up"] = dw_up_pair(tag + "_dwgu")

    bn = _row_block(s, 512)
    steps = s // bn // 2 if many_calls else s // bn
    prev, dgain = (), None
    for part, off in ((("_dn_a", 0), ("_dn_b", steps)) if many_calls else (("_dn", 0),)):
        row = pl.BlockSpec((bn, D_MODEL), lambda i, k, off=off: (i + off, 0))
        terms = []
        for j in range(N_CHIPS):
            a_slab = pl.BlockSpec((None, bn, FF_SHARD), lambda i, k, j=j, off=off: (j, i + off, 0))
            w_slab = pl.BlockSpec((None, FF_SHARD, D_MODEL), lambda i, k, j=j: (j, 0, 0))
            terms += [(dg, a_slab, wg, w_slab, NN), (du, a_slab, wu, w_slab, NN)]
        if norm_bwd is None:
            prev = _matmul(tag + part, (steps, 1), terms, [], [(jax.ShapeDtypeStruct((s, D_MODEL), F32), row)],
                           _ident, None, fill=prev)
            continue
        h, gain, dres = norm_bwd
        vec = pl.BlockSpec((1, D_MODEL), lambda i, k: (0, 0))
        res = _matmul(
            tag + part, (steps, 1), terms, [(h, row), (gain, vec), (dres, row)],
            [(jax.ShapeDtypeStruct((s, D_MODEL), F32), row), (jax.ShapeDtypeStruct((s, D_MODEL), BF16), row),
             (jax.ShapeDtypeStruct((1, D_MODEL), F32), vec)],
            _norm_bwd_epilogue(D_MODEL), None, fill=prev, summed=(2,))
        prev = res[:2]
        dgain = res[2] if dgain is None else dgain + res[2]
    return prev[0] if norm_bwd is None else (prev[0], prev[1], dgain)


def _mm_nn(name, a, b, out_dtype, res=None, gain=None):
    s, k = a.shape
    nn = b.shape[1]
    bm = _row_block(s)
    row = pl.BlockSpec((bm, nn), lambda i, kk: (i, 0))
    term = [(a, pl.BlockSpec((bm, k), lambda i, kk: (i, 0)), b, pl.BlockSpec((k, nn), lambda i, kk: (0, 0)), NN)]
    if res is None:
        return _matmul(name, (s // bm, 1), term, [], [(jax.ShapeDtypeStruct((s, nn), out_dtype), row)], _ident, None)[0]
    extras, outs = _residual_outs(s, bm, gain)
    res_out = _matmul(name, (s // bm, 1), term, [(res, row)] + extras, outs,
                      _residual_epilogue(1.0, gain is not None), None)
    return res_out if gain is not None else res_out[0]


def _mm_nt(name, a, b, out_dtype, attn_out=None, nh=0, dv=0):
    s, nn = a.shape
    k = b.shape[0]
    bm = _row_block(s)
    term = [(a, pl.BlockSpec((bm, nn), lambda i, kk: (i, 0)), b, pl.BlockSpec((k, nn), lambda i, kk: (0, 0)), NT)]
    out = (jax.ShapeDtypeStruct((s, k), out_dtype), pl.BlockSpec((bm, k), lambda i, kk: (i, 0)))
    if attn_out is None:
        return _matmul(name, (s // bm, 1), term, [], [out], _ident, None)[0]

    def with_delta(acc, o):
        do = acc.astype(out_dtype).astype(F32)
        cols = [jnp.sum(do[:, h * dv:(h + 1) * dv] * o[:, h * dv:(h + 1) * dv].astype(F32), axis=-1, keepdims=True)
                for h in range(nh)]
        return acc, jnp.stack(cols, axis=0)

    return _matmul(
        name, (s // bm, 1), term, [(attn_out, pl.BlockSpec((bm, nh * dv), lambda i, kk: (i, 0)))],
        [out, (jax.ShapeDtypeStruct((nh, s, 1), F32), pl.BlockSpec((nh, bm, 1), lambda i, kk: (0, i, 0)))],
        with_delta, None)


def _mm_nt_norm_bwd(name, a, b, h, gain, dres):
    s, nn = a.shape
    bm = _row_block(s, 512)
    extras, outs, summed = _norm_bwd_operands(s, bm, h, gain, dres)
    return _matmul(
        name, (s // bm, 1),
        [(a, pl.BlockSpec((bm, nn), lambda i, kk: (i, 0)), b, pl.BlockSpec(b.shape, lambda i, kk: (0, 0)), NT)],
        extras, outs, _norm_bwd_epilogue(D_MODEL), None, summed=summed)


def _w_in_dx_norm_bwd(dz, w_t, h, gain, dres):
    s = dz.shape[0]
    bm = _row_block(s, 512)
    epilogue = _norm_bwd_epilogue(D_MODEL)

    def body(dz_ref, w_ref, h_ref, g_ref, r_ref, dx_ref, dxb_ref, dg_ref):
        dzv = dz_ref[...]
        dn = jnp.concatenate([_dot(dzv, w_ref[j], NN) for j in range(N_CHIPS)], axis=1)
        dx, _, dgain = epilogue(dn, h_ref[...], g_ref[...], r_ref[...])
        dx_ref[...] = dx
        dxb_ref[...] = dx.astype(BF16)

        @pl.when(pl.program_id(0) == 0)
        def _():
            dg_ref[...] = dgain

        @pl.when(pl.program_id(0) > 0)
        def _():
            dg_ref[...] += dgain

    row = pl.BlockSpec((bm, D_MODEL), lambda i: (i, 0))
    vec = pl.BlockSpec((1, D_MODEL), lambda i: (0, 0))
    return _call(
        body, name="w_in_dx", grid=(s // bm,),
        in_specs=[row, pl.BlockSpec(w_t.shape, lambda i: (0, 0, 0)), row, vec, row],
        out_specs=[row, row, vec],
        out_shape=[jax.ShapeDtypeStruct((s, D_MODEL), F32), jax.ShapeDtypeStruct((s, D_MODEL), BF16),
                   jax.ShapeDtypeStruct((1, D_MODEL), F32)],
        sem=("arbitrary",), args=[dz, w_t, h, gain, dres])


def _mm_tn_stacked(name, a_list, b):
    s, nn = b.shape
    widths = [a.shape[1] for a in a_list]
    total = sum(widths)
    bk = _reduce_block(s)
    nk = s // bk
    na = len(a_list)

    def body(*refs):
        a_refs, b_ref, o_ref, acc_ref = refs[:na], refs[na], refs[na + 1], refs[na + 2]
        k = pl.program_id(0)

        @pl.when(k == 0)
        def _():
            acc_ref[...] = jnp.zeros_like(acc_ref)

        bv = b_ref[...]
        lo = 0
        for a_ref, w in zip(a_refs, widths):
            acc_ref[lo:lo + w, :] += _dot(a_ref[...], bv, TN)
            lo += w

        @pl.when(k == nk - 1)
        def _():
            o_ref[...] = acc_ref[...].astype(o_ref.dtype)

    return _call(
        body, name=name, grid=(nk,),
        in_specs=[pl.BlockSpec((bk, w), lambda k: (k, 0)) for w in widths] + [pl.BlockSpec((bk, nn), lambda k: (k, 0))],
        out_specs=[pl.BlockSpec((total, nn), lambda k: (0, 0))],
        out_shape=[jax.ShapeDtypeStruct((total, nn), BF16)],
        scratch_shapes=[pltpu.VMEM((total, nn), F32)], sem=("arbitrary",), args=list(a_list) + [b])[0]


def _mm_tn_pairs(name, pairs):
    s = pairs[0][0].shape[0]
    bk = _row_block(s)
    nk = s // bk
    npairs = len(pairs)
    shapes = [(a.shape[1], b.shape[1]) for a, b in pairs]

    def body(*refs):
        ins, outs, accs = refs[:2 * npairs], refs[2 * npairs:3 * npairs], refs[3 * npairs:]
        k = pl.program_id(0)

        @pl.when(k == 0)
        def _():
            for acc in accs:
                acc[...] = jnp.zeros_like(acc)

        for p in range(npairs):
            accs[p][...] += _dot(ins[2 * p][...], ins[2 * p + 1][...], TN)

        @pl.when(k == nk - 1)
        def _():
            for o, acc in zip(outs, accs):
                o[...] = acc[...].astype(o.dtype)

    in_specs = [pl.BlockSpec((bk, x.shape[1]), lambda k: (k, 0)) for pair in pairs for x in pair]
    return _call(
        body, name=name, grid=(nk,), in_specs=in_specs,
        out_specs=[pl.BlockSpec(shp, lambda k: (0, 0)) for shp in shapes],
        out_shape=[jax.ShapeDtypeStruct(shp, BF16) for shp in shapes],
        scratch_shapes=[pltpu.VMEM(shp, F32) for shp in shapes], sem=("arbitrary",),
        args=[x for pair in pairs for x in pair])


def _mm_heads_fwd(name, a, w, out_dtype, w_transposed=False):
    s, k = a.shape
    nh = w.shape[0]
    nn = w.shape[1] if w_transposed else w.shape[2]
    bm = _row_block(s)
    return _matmul(
        name, (nh, s // bm, 1),
        [(a, pl.BlockSpec((bm, k), lambda h, i, kk: (i, 0)),
          w, pl.BlockSpec((None,) + w.shape[1:], lambda h, i, kk: (h, 0, 0)), NT if w_transposed else NN)],
        [], [(jax.ShapeDtypeStruct((s, nh * nn), out_dtype), pl.BlockSpec((bm, nn), lambda h, i, kk: (i, h)))],
        _ident, None)[0]


def _mm_heads_bwd(name, dy, a, w, w_transposed=False):
    s, k = a.shape
    nh = w.shape[0]
    nn = w.shape[1] if w_transposed else w.shape[2]
    bm = _row_block(s)
    bk = _reduce_block(s)
    w_spec = pl.BlockSpec((None,) + w.shape[1:], lambda i, h: (h, 0, 0))
    da = _matmul(
        name + "_dx", (s // bm, nh),
        [(dy, pl.BlockSpec((bm, nn), lambda i, h: (i, h)), w, w_spec, NN if w_transposed else NT)],
        [], [(jax.ShapeDtypeStruct((s, k), F32), pl.BlockSpec((bm, k), lambda i, h: (i, 0)))], _ident, (bm, k))[0]
    a_term = (a, pl.BlockSpec((bk, k), lambda h, kk: (kk, 0)))
    dy_term = (dy, pl.BlockSpec((bk, nn), lambda h, kk: (kk, h)))
    lhs, rhs = (dy_term, a_term) if w_transposed else (a_term, dy_term)
    dw = _matmul(
        name + "_dw", (nh, s // bk), [lhs + rhs + (TN,)],
        [], [(jax.ShapeDtypeStruct(w.shape, BF16), pl.BlockSpec((None,) + w.shape[1:], lambda h, kk: (h, 0, 0)))],
        _ident, w.shape[1:])[0]
    return da, dw


def _w_in_fwd(n, w_t):
    s = n.shape[0]
    bm = _row_block(s)
    nh, nout, kin = w_t.shape
    terms = [(n, pl.BlockSpec((bm, kin), lambda i, k, j=j: (i, j)),
              w_t, pl.BlockSpec((None, nout, kin), lambda i, k, j=j: (j, 0, 0)), NT) for j in range(nh)]
    row = pl.BlockSpec((bm, nout), lambda i, k: (i, 0))
    return _matmul("w_in", (s // bm, 1), terms, [], [(jax.ShapeDtypeStruct((s, nout), F32), row)], _ident, None)[0]


def _w_in_dw(dz, n):
    s, nout = dz.shape
    kin = n.shape[1] // N_CHIPS
    bk = _reduce_block(s)
    return _matmul(
        "w_in_dw", (N_CHIPS, s // bk),
        [(dz, pl.BlockSpec((bk, nout), lambda j, k: (k, 0)), n, pl.BlockSpec((bk, kin), lambda j, k: (k, j)), TN)],
        [], [(jax.ShapeDtypeStruct((N_CHIPS, nout, kin), BF16), pl.BlockSpec((None, nout, kin), lambda j, k: (j, 0, 0)))],
        _ident, (nout, kin))[0]


def _rope_tables(positions):
    half = ROPE_DIM // 2
    freqs = 1.0 / (ROPE_BASE ** (jnp.arange(0, ROPE_DIM, 2, dtype=F32) / ROPE_DIM))
    ang = positions.astype(F32)[:, None] * freqs
    cos, sin = jnp.cos(ang), jnp.sin(ang)
    z = jnp.zeros_like(cos)
    tc = jnp.concatenate([cos, cos, z, z], axis=-1)
    ta = jnp.concatenate([-sin, z, z, z], axis=-1)
    tb = jnp.concatenate([z, sin, z, z], axis=-1)
    assert tc.shape[-1] == 4 * half
    return tc, ta, tb


def _rope(x, tc, ta, tb):
    return x * tc + pltpu.roll(x, 96, 1) * ta + pltpu.roll(x, 32, 1) * tb


def _rope_t(dy, tc, ta, tb):
    return dy * tc + pltpu.roll(dy * ta, 32, 1) + pltpu.roll(dy * tb, 96, 1)


def _norm_bf16(x, g):
    r = lax.rsqrt(jnp.mean(x * x, axis=-1, keepdims=True) + RMS_EPS)
    return ((x * r) * g).astype(BF16)


def _qkv_prep(z, q_gain, kv_gain, wq_t, wkv, tc, ta, tb):
    s = z.shape[0]
    bm = _row_block(s, 512)

    def body(zq_ref, zkv_ref, zkr_ref, qg_ref, kvg_ref, wq_ref, wkv_ref, tc_ref, ta_ref, tb_ref,
             qn_ref, kvn_ref, q_ref, k_ref, v_ref):
        c, a, b = tc_ref[...], ta_ref[...], tb_ref[...]
        qn = _norm_bf16(zq_ref[...], qg_ref[...])
        kvn = _norm_bf16(zkv_ref[...], kvg_ref[...])
        qn_ref[...] = qn
        kvn_ref[...] = kvn
        kpe = _rope(zkr_ref[...], c, a, b).astype(BF16)
        for h in range(MLA_HEADS):
            lo = h * HEAD_QK
            qp = _dot(qn, wq_ref[h], NT)
            q_ref[:, lo:lo + 128] = qp[:, :128].astype(BF16)
            q_ref[:, lo + 128:lo + 256] = _rope(qp[:, 128:], c, a, b).astype(BF16)
            kv = _dot(kvn, wkv_ref[h], NN)
            k_ref[:, lo:lo + 128] = kv[:, :128].astype(BF16)
            k_ref[:, lo + 128:lo + 256] = kpe
            v_ref[:, h * HEAD_V:(h + 1) * HEAD_V] = kv[:, 128:].astype(BF16)

    def cols(width, blk):
        return pl.BlockSpec((bm, width), lambda i: (i, blk))

    def whole(a):
        return pl.BlockSpec(a.shape, lambda i: (0,) * a.ndim)

    tab = cols(128, 0)
    return _call(
        body, name="qkv_prep", grid=(s // bm,),
        in_specs=[cols(Q_LORA, 0), cols(KV_LORA, 2), cols(128, 3), whole(q_gain), whole(kv_gain), whole(wq_t),
                  whole(wkv), tab, tab, tab],
        out_specs=[cols(Q_LORA, 0), cols(KV_LORA, 0), cols(1024, 0), cols(1024, 0), cols(512, 0)],
        out_shape=[jax.ShapeDtypeStruct((s, Q_LORA), BF16), jax.ShapeDtypeStruct((s, KV_LORA), BF16),
                   jax.ShapeDtypeStruct((s, 1024), BF16), jax.ShapeDtypeStruct((s, 1024), BF16),
                   jax.ShapeDtypeStruct((s, 512), BF16)],
        sem=("parallel",), args=[z, z, z, q_gain, kv_gain, wq_t, wkv, tc, ta, tb])


def _qkv_prep_bwd(dq, dk, dv, z, qn, kvn, q_gain, kv_gain, wq_t, wkv, tc, ta, tb):
    s = z.shape[0]
    bm = _row_block(s, 512)
    nsteps = s // bm

    def body(dq_ref, dk_ref, dv_ref, zq_ref, zkv_ref, qn_ref, kvn_ref, qg_ref, kvg_ref, wq_ref, wkv_ref,
             tc_ref, ta_ref, tb_ref, dz_ref, dqg_ref, dkvg_ref, dwq_ref, dwkv_ref, wq_acc, wkv_acc):
        i = pl.program_id(0)
        c, a, b = tc_ref[...], ta_ref[...], tb_ref[...]

        @pl.when(i == 0)
        def _():
            wq_acc[...] = jnp.zeros_like(wq_acc)
            wkv_acc[...] = jnp.zeros_like(wkv_acc)

        qn, kvn = qn_ref[...], kvn_ref[...]
        dqn = jnp.zeros((bm, Q_LORA), F32)
        dkvn = jnp.zeros((bm, KV_LORA), F32)
        dpe = jnp.zeros((bm, 128), F32)
        for h in range(MLA_HEADS):
            lo = h * HEAD_QK
            dqp = jnp.concatenate([dq_ref[:, lo:lo + 128],
                                   _rope_t(dq_ref[:, lo + 128:lo + 256].astype(F32), c, a, b).astype(BF16)], axis=1)
            dqn = dqn + _dot(dqp, wq_ref[h], NN)
            wq_acc[h] += _dot(dqp, qn, TN)
            dkv = jnp.concatenate([dk_ref[:, lo:lo + 128], dv_ref[:, h * HEAD_V:(h + 1) * HEAD_V]], axis=1)
            dkvn = dkvn + _dot(dkv, wkv_ref[h], NT)
            wkv_acc[h] += _dot(kvn, dkv, TN)
            dpe = dpe + dk_ref[:, lo + 128:lo + 256].astype(F32)
        dcq, dqg = _rms_bwd_math(dqn, zq_ref[...], qg_ref[...], Q_LORA)
        dckv, dkvg = _rms_bwd_math(dkvn, zkv_ref[...], kvg_ref[...], KV_LORA)
        dz_ref[:, 0:Q_LORA] = dcq.astype(BF16)
        dz_ref[:, Q_LORA:Q_LORA + KV_LORA] = dckv.astype(BF16)
        dz_ref[:, Q_LORA + KV_LORA:512] = _rope_t(dpe, c, a, b).astype(BF16)

        @pl.when(i == 0)
        def _():
            dqg_ref[...] = dqg
            dkvg_ref[...] = dkvg

        @pl.when(i > 0)
        def _():
            dqg_ref[...] += dqg
            dkvg_ref[...] += dkvg

        @pl.when(i == nsteps - 1)
        def _():
            dwq_ref[...] = wq_acc[...].astype(BF16)
            dwkv_ref[...] = wkv_acc[...].astype(BF16)

    def cols(width, blk):
        return pl.BlockSpec((bm, width), lambda i: (i, blk))

    def whole(shape):
        return pl.BlockSpec(shape, lambda i: (0,) * len(shape))

    tab = cols(128, 0)
    return _call(
        body, name="qkv_prep_bwd", grid=(nsteps,),
        in_specs=[cols(1024, 0), cols(1024, 0), cols(512, 0), cols(Q_LORA, 0), cols(KV_LORA, 2), cols(Q_LORA, 0),
                  cols(KV_LORA, 0), whole(q_gain.shape), whole(kv_gain.shape), whole(wq_t.shape), whole(wkv.shape),
                  tab, tab, tab],
        out_specs=[cols(512, 0), whole(q_gain.shape), whole(kv_gain.shape), whole(wq_t.shape), whole(wkv.shape)],
        out_shape=[jax.ShapeDtypeStruct((s, 512), BF16), jax.ShapeDtypeStruct(q_gain.shape, F32),
                   jax.ShapeDtypeStruct(kv_gain.shape, F32), jax.ShapeDtypeStruct(wq_t.shape, BF16),
                   jax.ShapeDtypeStruct(wkv.shape, BF16)],
        scratch_shapes=[pltpu.VMEM(wq_t.shape, F32), pltpu.VMEM(wkv.shape, F32)],
        sem=("arbitrary",), args=[dq, dk, dv, z, z, qn, kvn, q_gain, kv_gain, wq_t, wkv, tc, ta, tb])


def _causal_mask(s, row0, col0):
    rows = row0 + lax.broadcasted_iota(jnp.int32, s.shape, 0)
    cols = col0 + lax.broadcasted_iota(jnp.int32, s.shape, 1)
    return jnp.where(cols <= rows, s, -jnp.inf)


def _attn_fwd(name, q, k, k_off, v, v_off, nh, dq, dv, scale, causal, blk):
    sq, sk = q.shape[0], k.shape[0]
    bq = min(blk, sq)
    bk = min(blk, sk)
    nkv = sk // bk
    assert not causal or (sq == sk and bq == bk)

    hq = bq
    log2e = 1.4426950408889634
    c2 = scale * log2e

    def body(q_ref, k_ref, v_ref, o_ref, lse_ref):
        qi = pl.program_id(1)
        qs = (q_ref[...],)

        def step(j, carry, masked):
            rows = pl.ds(pl.multiple_of(j * bk, bk), bk)
            kb, vb = k_ref[rows, :], v_ref[rows, :]
            out = []
            for t, (m, l, acc) in enumerate(carry):
                s = _dot(qs[t], kb, NT) * c2
                if masked:
                    s = _causal_mask(s, qi * bq + t * hq, j * bk)
                m_new = jnp.maximum(m, jnp.max(s, axis=-1, keepdims=True))
                alpha = jnp.exp2(m - m_new)
                p = jnp.exp2(s - m_new)
                l = alpha * l + jnp.sum(p, axis=-1, keepdims=True)
                acc = alpha * acc + _dot(p, vb, NN)
                out.append((m_new, l, acc))
            return tuple(out)

        one = (jnp.full((hq, 1), -jnp.inf, F32), jnp.zeros((hq, 1), F32), jnp.zeros((hq, dv), F32))
        init = (one,)
        if causal:
            carry = lax.fori_loop(0, qi, lambda j, c: step(j, c, False), init)
            fin = step(qi, carry, True)
        else:
            fin = lax.fori_loop(0, nkv, lambda j, c: step(j, c, False), init)
        for t, (m, l, acc) in enumerate(fin):
            o_ref[t * hq:(t + 1) * hq, :] = (acc / l).astype(o_ref.dtype)
            lse_ref[t * hq:(t + 1) * hq, :] = m * (1.0 / log2e) + jnp.log(l)

    return _call(
        body, name=name, grid=(nh, sq // bq),
        in_specs=[pl.BlockSpec((bq, dq), lambda h, i: (i, h)),
                  pl.BlockSpec((sk, dq), lambda h, i: (0, k_off + h)),
                  pl.BlockSpec((sk, dv), lambda h, i: (0, v_off + h))],
        out_specs=[pl.BlockSpec((bq, dv), lambda h, i: (i, h)), pl.BlockSpec((None, bq, 1), lambda h, i: (h, i, 0))],
        out_shape=[jax.ShapeDtypeStruct((sq, nh * dv), BF16), jax.ShapeDtypeStruct((nh, sq, 1), F32)],
        sem=("parallel", "parallel"), args=[q, k, v])


def _attn_bwd(name, q, k, k_off, v, v_off, do, do_off, lse, delta, nh, dq, dv, scale, causal, blk):
    sq, sk = q.shape[0], k.shape[0]
    bq = min(blk, sq)
    bk = min(blk, sk)
    nq = sq // bq
    assert not causal or (sq == sk and bq == bk)

    nkv = sk // bk

    def body(q_ref, k_ref, v_ref, do_ref, lse_ref, dl_ref, dq_ref, dk_ref, dv_ref, dq_acc, dk_acc, dv_acc):
        j = pl.program_id(1)

        @pl.when(j == 0)
        def _():
            dq_acc[...] = jnp.zeros_like(dq_acc)

        dk_acc[...] = jnp.zeros_like(dk_acc)
        dv_acc[...] = jnp.zeros_like(dv_acc)
        kv = k_ref[...]
        vv = v_ref[...]

        def step(i, masked):
            rows = pl.ds(pl.multiple_of(i * bq, bq), bq)
            qv = q_ref[rows, :]
            dov = do_ref[rows, :].astype(BF16)
            s = _dot(qv, kv, NT) * scale
            if masked:
                s = _causal_mask(s, i * bq, j * bk)
            p = jnp.exp(s - lse_ref[rows, :])
            dp = _dot(dov, vv, NT)
            ds = (p * (dp - dl_ref[rows, :]) * scale).astype(BF16)
            dv_acc[...] += _dot(p, dov, TN)
            dk_acc[...] += _dot(ds, qv, TN)
            dq_acc[rows, :] += _dot(ds, kv, NN)

        if causal:
            step(j, True)

            def loop(i, c):
                step(i, False)
                return c

            lax.fori_loop(j + 1, nq, loop, 0)
        else:
            def loop(i, c):
                step(i, False)
                return c

            lax.fori_loop(0, nq, loop, 0)
        dk_ref[...] = dk_acc[...].astype(dk_ref.dtype)
        dv_ref[...] = dv_acc[...].astype(dv_ref.dtype)

        @pl.when(j == nkv - 1)
        def _():
            dq_ref[...] = dq_acc[...].astype(dq_ref.dtype)

    stat = pl.BlockSpec((None, sq, 1), lambda h, j: (h, 0, 0))
    return _call(
        body, name=name, grid=(nh, sk // bk),
        in_specs=[pl.BlockSpec((sq, dq), lambda h, j: (0, h)),
                  pl.BlockSpec((bk, dq), lambda h, j: (j, k_off + h)),
                  pl.BlockSpec((bk, dv), lambda h, j: (j, v_off + h)),
                  pl.BlockSpec((sq, dv), lambda h, j: (0, do_off + h)), stat, stat],
        out_specs=[pl.BlockSpec((sq, dq), lambda h, j: (0, h)),
                   pl.BlockSpec((bk, dq), lambda h, j: (j, h)),
                   pl.BlockSpec((bk, dv), lambda h, j: (j, h))],
        out_shape=[jax.ShapeDtypeStruct((sq, nh * dq), BF16), jax.ShapeDtypeStruct((sk, nh * dq), BF16),
                   jax.ShapeDtypeStruct((sk, nh * dv), BF16)],
        scratch_shapes=[pltpu.VMEM((sq, dq), F32), pltpu.VMEM((bk, dq), F32), pltpu.VMEM((bk, dv), F32)],
        sem=("parallel", "arbitrary"), args=[q, k, v, do, lse, delta])


def _pool_diff(z, g):
    s = z.shape[0]
    t = lax.broadcasted_iota(jnp.int32, z.shape, 0)
    acc = z
    sums = []
    for k in (1, 2, 4, 8):
        acc = acc + jnp.where(t >= k, pltpu.roll(acc, k, 0), 0.0)
        sums.append(acc)
    win = jnp.where(g == 0, sums[0], jnp.where(g == 1, sums[1], jnp.where(g == 2, sums[2], sums[3])))
    w = lax.shift_left(jnp.int32(2), g)
    count = jnp.minimum(t + 1, w).astype(F32)
    del s
    return win / count - z, count


def _pool_fwd(z, pool_w, pool_scale):
    s = z.shape[0]

    def body(z_ref, w_ref, sc_ref, o_ref):
        diff, _ = _pool_diff(z_ref[...], pl.program_id(0))
        o_ref[...] = (_dot(diff, w_ref[...], NN) * sc_ref[...]).astype(o_ref.dtype)

    return _call(
        body, name="pool_fwd", grid=(POOL_GROUPS,),
        in_specs=[pl.BlockSpec((s, POOL_CH), lambda g: (0, 4 + g)),
                  pl.BlockSpec((None, POOL_CH, POOL_CH), lambda g: (g, 0, 0)),
                  pl.BlockSpec((1, POOL_CH), lambda g: (0, g))],
        out_specs=[pl.BlockSpec((s, POOL_CH), lambda g: (0, g))],
        out_shape=[jax.ShapeDtypeStruct((s, POOL_GROUPS * POOL_CH), BF16)],
        sem=("parallel",), args=[z, pool_w, pool_scale])[0]


def _pool_bwd(dcat, z, pool_w, pool_scale):
    s = z.shape[0]

    def body(dp_ref, z_ref, w_ref, sc_ref, dz_ref, dw_ref, dsc_ref):
        g = pl.program_id(0)
        diff, count = _pool_diff(z_ref[...], g)
        dpf = dp_ref[...].astype(F32)
        u = _dot(diff, w_ref[...], NN)
        dsc_ref[...] = jnp.sum(dpf * u, axis=0, keepdims=True)
        du = (dpf * sc_ref[...]).astype(BF16)
        dw_ref[...] = _dot(diff, du, TN)
        ddiff = _dot(du, w_ref[...], NT)
        t = lax.broadcasted_iota(jnp.int32, ddiff.shape, 0)
        acc = ddiff / count
        sums = []
        for k in (1, 2, 4, 8):
            acc = acc + jnp.where(t < s - k, pltpu.roll(acc, s - k, 0), 0.0)
            sums.append(acc)
        win = jnp.where(g == 0, sums[0], jnp.where(g == 1, sums[1], jnp.where(g == 2, sums[2], sums[3])))
        dz_ref[...] = (win - ddiff).astype(dz_ref.dtype)

    return pl.pallas_call(
        body, name="pool_bwd", grid=(POOL_GROUPS,),
        in_specs=[pl.BlockSpec((s, POOL_CH), lambda g: (0, 4 + g)),
                  pl.BlockSpec((s, POOL_CH), lambda g: (0, 4 + g)),
                  pl.BlockSpec((None, POOL_CH, POOL_CH), lambda g: (g, 0, 0)),
                  pl.BlockSpec((1, POOL_CH), lambda g: (0, g))],
        out_specs=[pl.BlockSpec((s, POOL_CH), lambda g: (0, g)),
                   pl.BlockSpec((None, POOL_CH, POOL_CH), lambda g: (g, 0, 0)),
                   pl.BlockSpec((1, POOL_CH), lambda g: (0, g))],
        out_shape=[jax.ShapeDtypeStruct((s, POOL_GROUPS * POOL_CH), BF16),
                   jax.ShapeDtypeStruct((POOL_GROUPS, POOL_CH, POOL_CH), F32),
                   jax.ShapeDtypeStruct((1, POOL_GROUPS * POOL_CH), F32)],
        compiler_params=_params("parallel"),
    )(dcat, z, pool_w, pool_scale)


def _local_step(x, mem, positions, target, w, grads):
    tc, ta, tb = _rope_tables(positions)
    blk = _ATT_BLOCK

    if "ffn1_shards" in w:
        n1, a1, dadu1, dadg1, w["ffn1_w_gate"], w["ffn1_w_up"], w["ffn1_w_down"] = _ffn1_up_gather(
            x, w["ffn1_norm"], *w["ffn1_shards"])
    else:
        n1 = _rmsnorm_fwd("ffn1_norm", x, w["ffn1_norm"], D_MODEL)
        a1, dadu1, dadg1 = _ffn_up("ffn1_up", n1, w["ffn1_w_gate"], w["ffn1_w_up"])
    h1, n2 = _ffn_down("ffn1_down", a1, w["ffn1_w_down"], x, w["mix_norm"])
    z = _w_in_fwd(n2, w["w_in"])
    qn, kvn, qf, kf, vf = _qkv_prep(z, w["q_norm"], w["kv_norm"], w["w_q_up"], w["w_kv_up"], tc, ta, tb)
    att, lse = _attn_fwd("mla_fwd", qf, kf, 0, vf, 0, MLA_HEADS, HEAD_QK, HEAD_V, MLA_SCALE, True, blk)
    pool = _pool_fwd(z, w["pool_w"], w["pool_scale"])
    s = x.shape[0]
    bm = _row_block(s)
    row = pl.BlockSpec((bm, D_MODEL), lambda i, k: (i, 0))
    half = pl.BlockSpec((bm, 512), lambda i, k: (i, 0))
    h2, n3 = _matmul(
        "w_out", (s // bm, 1),
        [(att, half, w["w_out"], pl.BlockSpec((512, D_MODEL), lambda i, k: (0, 0)), NN),
         (pool, half, w["w_out"], pl.BlockSpec((512, D_MODEL), lambda i, k: (1, 0)), NN)],
        [(h1, row)] + _residual_outs(s, bm, w["xattn_norm"])[0], _residual_outs(s, bm, w["xattn_norm"])[1],
        _residual_epilogue(1.0, True), None)
    memn = _rmsnorm_fwd("mem_norm", mem, w["mem_norm"], D_MODEL)
    qm = _mm_nn("w_mq", n3, w["w_mq"], BF16)
    kvm = _mm_heads_fwd("w_mkv", memn, w["w_mkv"], BF16)
    om, lse_m = _attn_fwd("xattn_fwd", qm, kvm, 0, kvm, MEM_HEADS, MEM_HEADS, MEM_HEAD_DIM, MEM_HEAD_DIM,
                          MEM_SCALE, False, 2 * blk)
    h3, n4 = _mm_nn("w_mo", om, w["w_mo"], F32, res=h2, gain=w["ffn2_norm"])
    a2, dadu2, dadg2 = _ffn_up("ffn2_up", n4, w["ffn2_w_gate"], w["ffn2_w_up"])
    dh4, dh4b, loss_vec, d_final = _ffn_down("ffn2_down", a2, w["ffn2_w_down"], h3, loss=(w["final_norm"], target))
    grads["final_norm"] = d_final

    dh3, dh3b, grads["ffn2_norm"] = _ffn_bwd("ffn2", dh4b, n4, dadg2, dadu2, a2, w["ffn2_w_gate"], w["ffn2_w_up"],
                                             w["ffn2_w_down"], grads, norm_bwd=(h3, w["ffn2_norm"], dh4),
                                             many_calls=False)

    dom, delta_m = _mm_nt("w_mo_dx", dh3b, w["w_mo"], BF16, attn_out=om, nh=MEM_HEADS, dv=MEM_HEAD_DIM)
    dqm, dkm, dvm = _attn_bwd("xattn_bwd", qm, kvm, 0, kvm, MEM_HEADS, dom, 0, lse_m, delta_m, MEM_HEADS,
                              MEM_HEAD_DIM, MEM_HEAD_DIM, MEM_SCALE, False, 4 * blk)
    dkvm = jnp.concatenate([dkm, dvm], axis=1)
    dh2, dh2b, grads["xattn_norm"] = _mm_nt_norm_bwd("w_mq_dx", dqm, w["w_mq"], h2, w["xattn_norm"], dh3)
    grads["w_mo"], grads["w_mq"] = _mm_tn_pairs("w_mo_mq_dw", [(om, dh3b), (n3, dqm)])
    dmemn, grads["w_mkv"] = _mm_heads_bwd("w_mkv", dkvm, memn, w["w_mkv"])
    _, grads["mem_norm"] = _rmsnorm_bwd("mem_norm_bwd", dmemn, mem, w["mem_norm"], D_MODEL, out_dtype=BF16)

    dcat, delta = _mm_nt("w_out_dx", dh2b, w["w_out"], BF16, attn_out=att, nh=MLA_HEADS, dv=HEAD_V)
    grads["w_out"] = _mm_tn_stacked("w_out_dw", [att, pool], dh2b)
    dzp, grads["pool_w"], grads["pool_scale"] = _pool_bwd(dcat, z, w["pool_w"], w["pool_scale"])
    dqf, dkf, dvf = _attn_bwd("mla_bwd", qf, kf, 0, vf, 0, dcat, 0, lse, delta, MLA_HEADS, HEAD_QK, HEAD_V,
                              MLA_SCALE, True, blk)
    dz_lat, grads["q_norm"], grads["kv_norm"], grads["w_q_up"], grads["w_kv_up"] = _qkv_prep_bwd(
        dqf, dkf, dvf, z, qn, kvn, w["q_norm"], w["kv_norm"], w["w_q_up"], w["w_kv_up"], tc, ta, tb)
    dz = jnp.concatenate([dz_lat, dzp], axis=1)
    grads["w_in"] = _w_in_dw(dz, n2)
    dh1, dh1b, grads["mix_norm"] = _w_in_dx_norm_bwd(dz, w["w_in"], h1, w["mix_norm"], dh2)

    dn1 = _ffn_bwd("ffn1", dh1b, n1, dadg1, dadu1, a1, w["ffn1_w_gate"], w["ffn1_w_up"], w["ffn1_w_down"], grads)
    dx, grads["ffn1_norm"] = _rmsnorm_bwd("ffn1_norm_bwd", dn1, x, w["ffn1_norm"], D_MODEL, dres=dh1)
    return loss_vec[0, 0], dx


def _mesh_pos():
    x, y, c = lax.axis_index("x"), lax.axis_index("y"), lax.axis_index("c")
    chips = [(1 - x, y), (x, 1 - y), (1 - x, 1 - y)]
    chip_ids = [2 * cx + cy for cx, cy in chips]
    return x, y, c, 2 * x + y, chips, chip_ids


def _half_rows(c, rows):
    hr = rows // 2
    return pl.ds(pl.multiple_of(c * hr, 16), hr), pl.ds(pl.multiple_of((1 - c) * hr, 16), hr)


def _ag_ici_stage(shards):
    n = len(shards)

    def copies(ins, outs):
        x, y, c, me, chips, _ = _mesh_pos()
        out = []
        for k in range(n):
            mine, _ = _half_rows(c, ins[k].shape[0])
            out.append((ins[k], outs[k].at[me], None))
            for cx, cy in chips:
                out.append((ins[k].at[mine], outs[k].at[me, mine], (cx, cy, c)))
        return out

    return _Stage(shards, [jax.ShapeDtypeStruct((N_CHIPS,) + s.shape, s.dtype) for s in shards], 3 * n, n, copies)


def _quarter_rows(c, rows):
    qr = rows // 4
    return pl.ds(pl.multiple_of(c * 2 * qr, 16), qr), pl.ds(pl.multiple_of(c * 2 * qr + qr, 16), qr)


def _ag_d2d_stage(fulls):
    n = len(fulls)

    def copies(ins, outs):
        x, y, c, me, _, chip_ids = _mesh_pos()
        out = []
        for k in range(n):
            mine, _ = _half_rows(c, ins[k].shape[1])
            for j in range(3):
                out.append((ins[k].at[chip_ids[j], mine], outs[k].at[chip_ids[j], mine], (x, y, 1 - c)))
        return out

    return _Stage(fulls, [jax.ShapeDtypeStruct(f.shape, f.dtype) for f in fulls], 3 * n, 0, copies,
                  aliases={k: k for k in range(n)})


def _rs_swap_stage(grads):
    n = len(grads)

    def copies(ins, outs):
        x, y, c, _, _, _ = _mesh_pos()
        out = []
        for k in range(n):
            _, other = _half_rows(c, ins[k].shape[1])
            out.append((ins[k].at[:, other, :], outs[k], (x, y, 1 - c)))
        return out

    return _Stage(grads, [jax.ShapeDtypeStruct((N_CHIPS, g.shape[1] // 2, g.shape[2]), g.dtype) for g in grads],
                  n, 0, copies)


_REL_OF_PEER = (2, 1, 3)


def _rs_scatter_stage(sums, relative=False):
    n = len(sums)

    def copies(ins, outs):
        x, y, c, me, chips, chip_ids = _mesh_pos()
        out = []
        for k in range(n):
            mine, _ = _half_rows(c, 2 * ins[k].shape[1])
            out.append((ins[k].at[0 if relative else me], outs[k].at[0, mine, :], None))
            for j, (cx, cy) in enumerate(chips):
                slab = _REL_OF_PEER[j] if relative else chip_ids[j]
                out.append((ins[k].at[slab], outs[k].at[1 + j, mine, :], (cx, cy, c)))
        return out

    return _Stage(sums, [jax.ShapeDtypeStruct((N_CHIPS, 2 * s.shape[1], s.shape[2]), s.dtype) for s in sums],
                  3 * n, n, copies)


def _rs_mirror_stage(parts):
    n = len(parts)

    def copies(ins, outs):
        x, y, c, _, _, _ = _mesh_pos()
        out = []
        for k in range(n):
            mine, _ = _half_rows(c, ins[k].shape[1])
            out.append((ins[k].at[:, mine, :], outs[k].at[:, mine, :], (x, y, 1 - c)))
        return out

    return _Stage(parts, [jax.ShapeDtypeStruct(p.shape, p.dtype) for p in parts], n, 0, copies,
                  aliases={k: k for k in range(n)})


def _pair_add(name, gs, r1s, core):
    n = len(gs)

    def body(c_ref, *refs):
        for k in range(n):
            g_ref, r_ref, o_ref = refs[k], refs[n + k], refs[2 * n + k]
            o_ref[...] = (g_ref[...].astype(F32) + r_ref[...].astype(F32)).astype(BF16)

    def half(g):
        return pl.BlockSpec((None, g.shape[1] // 2, g.shape[2]), lambda j, c: (j, 0, 0))

    def mine(g):
        return pl.BlockSpec((None, g.shape[1] // 2, g.shape[2]), lambda j, c: (j, c[0], 0))

    return pl.pallas_call(
        body, name=name,
        grid_spec=pltpu.PrefetchScalarGridSpec(
            num_scalar_prefetch=1, grid=(N_CHIPS,),
            in_specs=[mine(g) for g in gs] + [half(g) for g in gs], out_specs=[half(g) for g in gs]),
        out_shape=[jax.ShapeDtypeStruct((N_CHIPS, g.shape[1] // 2, g.shape[2]), BF16) for g in gs],
        compiler_params=_params("parallel"),
    )(core, *gs, *r1s)


def _adamw_math(w, g, m, v):
    m = ADAM_B1 * m + (1.0 - ADAM_B1) * g
    v = ADAM_B2 * v + (1.0 - ADAM_B2) * (g * g)
    m_hat = m / (1.0 - ADAM_B1 ** ADAM_STEP)
    v_hat = v / (1.0 - ADAM_B2 ** ADAM_STEP)
    delta = -ADAM_LR * (m_hat / (jnp.sqrt(v_hat) + ADAM_EPS) + ADAM_WD * w)
    return delta, m, v


def _adamw_sum(name, ws, parts, ms, vs):
    n = len(ws)
    r, c = ws[0].shape
    assert all(w.shape == (r, c) for w in ws)
    br = r
    while br * c * 4 > (1 << 20) and br % 32 == 0:
        br //= 2

    def body(*refs):
        for k in range(n):
            w_ref, p_ref, m_ref, v_ref = refs[4 * k:4 * k + 4]
            g_ref, d_ref, nm_ref, nv_ref = refs[4 * n + 4 * k:4 * n + 4 * k + 4]
            g = p_ref[0].astype(F32)
            for j in range(1, N_CHIPS):
                g = g + p_ref[j].astype(F32)
            d, nm, nv = _adamw_math(w_ref[...], g, m_ref[...], v_ref[...])
            g_ref[...] = g
            d_ref[...] = d
            nm_ref[...] = nm
            nv_ref[...] = nv

    spec = pl.BlockSpec((br, c), lambda i: (i, 0))
    shp = jax.ShapeDtypeStruct((r, c), F32)
    args = [a for k in range(n) for a in (ws[k], parts[k], ms[k], vs[k])]
    res = _call(
        body, name=name, grid=(r // br,),
        in_specs=[spec, pl.BlockSpec((N_CHIPS, br, c), lambda i: (0, i, 0)), spec, spec] * n,
        out_specs=[spec] * (4 * n), out_shape=[shp] * (4 * n), sem=("parallel",), args=args)
    return [res[4 * k:4 * k + 4] for k in range(n)]


_SMALL_VECTORS = ("ffn1_norm", "mix_norm", "xattn_norm", "mem_norm", "ffn2_norm", "final_norm", "q_norm",
                  "kv_norm", "pool_scale")


_LOSS_ROW = 9
_VEC_ROWS = 16
_POOL_ROWS = POOL_GROUPS * POOL_CH


def _small_params_step(g, w, m, v, loss_local):
    names = list(_SMALL_VECTORS) + ["pool_w"]
    nv = len(_SMALL_VECTORS)
    widths = [g[n].shape[1] for n in _SMALL_VECTORS]
    shapes = {"vec": (_VEC_ROWS, D_MODEL), "pool": (_POOL_ROWS, POOL_CH)}

    def body(*refs):
        ins = refs[:4 * (nv + 1) + 1]
        outs = refs[len(ins):len(ins) + 4 * (nv + 1) + 1]
        vec_own, vec_sib, vec_all, pool_sib, pool_sum, pool_all, send, recv = refs[len(ins) + len(outs):]
        g_in, w_in, m_in, v_in = (ins[k * (nv + 1):(k + 1) * (nv + 1)] for k in range(4))
        loss_in = ins[-1]
        g_out, d_out, m_out, v_out = (outs[k * (nv + 1):(k + 1) * (nv + 1)] for k in range(4))
        loss_out = outs[-1]
        x, y, c, me, chips, chip_ids = _mesh_pos()
        sib = (x, y, 1 - c)

        def remote(src, dst, k, dev):
            return pltpu.make_async_remote_copy(src_ref=src, dst_ref=dst, send_sem=send.at[k], recv_sem=recv.at[k],
                                                device_id=dev, device_id_type=_MESH)

        vec_own[...] = jnp.zeros_like(vec_own)
        for i in range(nv):
            vec_own[i:i + 1, 0:widths[i]] = g_in[i][...]
        vec_own[_LOSS_ROW:_LOSS_ROW + 1, 0:128] = loss_in[...]
        swaps = [remote(vec_own, vec_sib, 0, sib), remote(g_in[nv], pool_sib, 1, sib)]
        for cp in swaps:
            cp.start()
        for cp in swaps:
            cp.wait()
        vec_all[me] = vec_own[...] + vec_sib[...]
        pool_sum[...] = g_in[nv][...] + pool_sib[...]
        pool_all[me] = pool_sum[...]
        hv, hp = _VEC_ROWS // 2, _POOL_ROWS // 2
        mine_v = pl.ds(pl.multiple_of(c * hv, 8), hv)
        mine_p = pl.ds(pl.multiple_of(c * hp, 8), hp)
        sends = []
        for j, (cx, cy) in enumerate(chips):
            sends.append(remote(vec_all.at[me, mine_v], vec_all.at[me, mine_v], 2 + j, (cx, cy, c)))
            sends.append(remote(pool_sum.at[mine_p], pool_all.at[me, mine_p], 5 + j, (cx, cy, c)))
        for cp in sends:
            cp.start()
        for cp in sends:
            cp.wait()
        mirrors = []
        for j in range(3):
            mirrors.append(remote(vec_all.at[chip_ids[j], mine_v], vec_all.at[chip_ids[j], mine_v], 8 + j, sib))
            mirrors.append(remote(pool_all.at[chip_ids[j], mine_p], pool_all.at[chip_ids[j], mine_p], 11 + j, sib))
        for cp in mirrors:
            cp.start()
        for cp in mirrors:
            cp.wait()
        vec_tot = vec_all[0]
        pool_tot = pool_all[0]
        for i in range(1, N_CHIPS):
            vec_tot = vec_tot + vec_all[i]
            pool_tot = pool_tot + pool_all[i]
        vec_sib[...] = vec_tot
        loss_out[...] = vec_sib[_LOSS_ROW:_LOSS_ROW + 1, 0:128]
        for i in range(nv + 1):
            gi = pool_tot if i == nv else vec_sib[i:i + 1, 0:widths[i]]
            d, nm, nvv = _adamw_math(w_in[i][...], gi, m_in[i][...], v_in[i][...])
            g_out[i][...] = gi
            d_out[i][...] = d
            m_out[i][...] = nm
            v_out[i][...] = nvv

    vm = pl.BlockSpec(memory_space=pltpu.VMEM)
    args = [d[n] for d in (g, w, m, v) for n in names] + [jnp.broadcast_to(loss_local.reshape(1, 1), (1, 128))]
    out_shape = [jax.ShapeDtypeStruct(g[n].shape, F32) for _ in range(4) for n in names]
    out_shape.append(jax.ShapeDtypeStruct((1, 128), F32))
    res = pl.pallas_call(
        body, name="small_params_step", in_specs=[vm] * len(args), out_specs=[vm] * len(out_shape),
        out_shape=out_shape,
        scratch_shapes=[pltpu.VMEM(shapes["vec"], F32), pltpu.VMEM(shapes["vec"], F32),
                        pltpu.VMEM((N_CHIPS,) + shapes["vec"], F32), pltpu.VMEM(shapes["pool"], F32),
                        pltpu.VMEM(shapes["pool"], F32), pltpu.VMEM((N_CHIPS,) + shapes["pool"], F32),
                        pltpu.SemaphoreType.DMA((14,)), pltpu.SemaphoreType.DMA((14,))],
        compiler_params=pltpu.CompilerParams(vmem_limit_bytes=V7X_VMEM_LIMIT_BYTES),
    )(*args)
    k = len(names)
    dicts = [dict(zip(names, res[i * k:(i + 1) * k])) for i in range(4)]
    return dicts[0], dicts[1], dicts[2], dicts[3], res[-1]


_WEIGHTS = ("ffn1_norm", "ffn1_w_gate", "ffn1_w_up", "ffn1_w_down", "mix_norm", "w_in", "q_norm", "w_q_up",
            "kv_norm", "w_kv_up", "pool_w", "pool_scale", "w_out", "xattn_norm", "mem_norm", "w_mq", "w_mkv",
            "w_mo", "ffn2_norm", "ffn2_w_gate", "ffn2_w_up", "ffn2_w_down", "final_norm")
_SHARDED = ("ffn1_w_gate", "ffn1_w_up", "ffn1_w_down", "w_in", "w_q_up", "w_kv_up", "w_out", "w_mq", "w_mkv",
            "w_mo", "ffn2_w_gate", "ffn2_w_up", "ffn2_w_down")
W_IN_SPLIT = Q_LORA + KV_LORA + ROPE_DIM


_FFN1 = ("ffn1_w_gate", "ffn1_w_up", "ffn1_w_down")
_TRANSPOSED = ("ffn1_w_gate", "ffn1_w_up", "ffn2_w_gate", "ffn2_w_up", "w_in", "w_q_up")


def _local_view(name, a):
    return jnp.swapaxes(a, 1, 2)[0] if name in _TRANSPOSED else a[0]


def _global_view(name, a):
    return jnp.swapaxes(a[None], 1, 2) if name in _TRANSPOSED else a[None]


def _pad_shard(name, a):
    if name == "w_in":
        return jnp.concatenate([a[:W_IN_SPLIT], jnp.zeros((64, a.shape[1]), a.dtype), a[W_IN_SPLIT:]], axis=0)
    if name == "w_q_up":
        return jnp.pad(a, ((0, 64), (0, 0)))
    return a


def _unpad_shard(name, a):
    if name == "w_in":
        return jnp.concatenate([a[:, :W_IN_SPLIT], a[:, W_IN_SPLIT + 64:]], axis=1)
    if name == "w_q_up":
        return a[:, :192]
    return a


def _stacked(g):
    return g if g.ndim == 3 else g.reshape(N_CHIPS, g.shape[0] // N_CHIPS, g.shape[1])


class _Plan:
    AG_UNITS = (
        (("w_in", "w_q_up", "w_kv_up"), "ffn1_up"),
        (("w_out",), "w_in"),
        (("w_mq",), "qkv_prep"),
        (("w_mkv", "w_mo", "ffn2_w_gate"), "mla_fwd"),
        (("ffn2_w_up",), "xattn_fwd"),
        (("ffn2_w_down",), "ffn2_up"),
    )
    RS_UNITS = (
        (("ffn2_w_gate", "ffn2_w_up", "ffn2_w_down"), "ffn2_dn", "mla_bwd", "qkv_prep_bwd"),
        (("w_mo", "w_mq", "w_mkv"), "w_out_dx", "mla_bwd", "qkv_prep_bwd"),
        (("w_out", "w_q_up", "w_kv_up", "w_in"), "w_in_dx", "ffn1_dact", "ffn1_dwd"),
        (("ffn1_w_down",), "ffn1_dwg", "ffn1_dwu", "ffn1_dn_a"),
        (("ffn1_w_gate",), "ffn1_dwu", "ffn1_dn_a", "ffn1_dn_b"),
        (("ffn1_w_up",), "ffn1_dn_a", "ffn1_dn_b", "adamw_w_kv_up"),
    )
    ADAMW_ORDER = (("w_kv_up",), ("ffn2_w_gate", "ffn2_w_up"), ("ffn2_w_down", "ffn1_w_down"), ("w_mo", "w_mq", "w_out"),
                   ("w_mkv",), ("w_q_up",), ("w_in",), ("ffn1_w_gate", "ffn1_w_up"))

    def __init__(self, shards, w, grads, core):
        self.shards, self.w, self.grads, self.core = shards, w, grads, core
        self.last_slab_step = 0
        self.parts = {}
        self.ag = [None for _ in self.AG_UNITS]
        self.rs = [[None, None, None, None] for _ in self.RS_UNITS]

    def pre(self, name):
        for i, (names, host) in enumerate(self.AG_UNITS):
            if name == host:
                st = _ag_ici_stage([self.shards[n] for n in names])
                st.then = _ag_d2d_stage(st.outs)
                st.start_step = self.last_slab_step if name == "ffn1_up" else 0
                self.ag[i] = _host(name, st)
        for i, (names, h1, h2, h3) in enumerate(self.RS_UNITS):
            if name == h1:
                self.rs[i][0] = _host(name, _rs_swap_stage([_stacked(self.grads[n]) for n in names]))
            if name == h2:
                self.rs[i][2] = _host(name, _rs_scatter_stage(self.rs[i][1], relative=names[0] in _FFN1))
            if name == h3:
                self.rs[i][3] = _host(name, _rs_mirror_stage(self.rs[i][2].results))

    def post(self, name):
        for i, (names, host) in enumerate(self.AG_UNITS):
            if name == host:
                for n, f in zip(names, self.ag[i].results):
                    self.w[n] = _full_weight(n, f)
        for i, (names, h1, h2, h3) in enumerate(self.RS_UNITS):
            if name == h1:
                self.rs[i][1] = list(_pair_add("pair_add_" + names[0], [_stacked(self.grads[n]) for n in names],
                                               self.rs[i][0].results, self.core))
            if name == h3:
                for n, p in zip(names, self.rs[i][3].results):
                    self.parts[n] = p


def _full_weight(name, stacked):
    if name in ("w_out", "w_mq", "w_mo"):
        return stacked.reshape(D_MODEL, D_MODEL)
    return stacked


def kernel(x, mem, positions, ffn1_norm, ffn1_w_gate, ffn1_w_up, ffn1_w_down, mix_norm, w_in, q_norm, w_q_up, kv_norm, w_kv_up, pool_w, pool_scale, w_out, xattn_norm, mem_norm, w_mq, w_mkv, w_mo, ffn2_norm, ffn2_w_gate, ffn2_w_up, ffn2_w_down, final_norm, loss_target, m_ffn1_norm, m_ffn1_w_gate, m_ffn1_w_up, m_ffn1_w_down, m_mix_norm, m_w_in, m_q_norm, m_w_q_up, m_kv_norm, m_w_kv_up, m_pool_w, m_pool_scale, m_w_out, m_xattn_norm, m_mem_norm, m_w_mq, m_w_mkv, m_w_mo, m_ffn2_norm, m_ffn2_w_gate, m_ffn2_w_up, m_ffn2_w_down, m_final_norm, v_ffn1_norm, v_ffn1_w_gate, v_ffn1_w_up, v_ffn1_w_down, v_mix_norm, v_w_in, v_q_norm, v_w_q_up, v_kv_norm, v_w_kv_up, v_pool_w, v_pool_scale, v_w_out, v_xattn_norm, v_mem_norm, v_w_mq, v_w_mkv, v_w_mo, v_ffn2_norm, v_ffn2_w_gate, v_ffn2_w_up, v_ffn2_w_down, v_final_norm):
    wts = dict(zip(_WEIGHTS, (ffn1_norm, ffn1_w_gate, ffn1_w_up, ffn1_w_down, mix_norm, w_in, q_norm, w_q_up, kv_norm, w_kv_up, pool_w, pool_scale, w_out, xattn_norm, mem_norm, w_mq, w_mkv, w_mo, ffn2_norm, ffn2_w_gate, ffn2_w_up, ffn2_w_down, final_norm)))
    mom = dict(zip(_WEIGHTS, (m_ffn1_norm, m_ffn1_w_gate, m_ffn1_w_up, m_ffn1_w_down, m_mix_norm, m_w_in, m_q_norm, m_w_q_up, m_kv_norm, m_w_kv_up, m_pool_w, m_pool_scale, m_w_out, m_xattn_norm, m_mem_norm, m_w_mq, m_w_mkv, m_w_mo, m_ffn2_norm, m_ffn2_w_gate, m_ffn2_w_up, m_ffn2_w_down, m_final_norm)))
    var = dict(zip(_WEIGHTS, (v_ffn1_norm, v_ffn1_w_gate, v_ffn1_w_up, v_ffn1_w_down, v_mix_norm, v_w_in, v_q_norm, v_w_q_up, v_kv_norm, v_w_kv_up, v_pool_w, v_pool_scale, v_w_out, v_xattn_norm, v_mem_norm, v_w_mq, v_w_mkv, v_w_mo, v_ffn2_norm, v_ffn2_w_gate, v_ffn2_w_up, v_ffn2_w_down, v_final_norm)))
    small = [n for n in _WEIGHTS if n not in _SHARDED]

    global _PLAN
    shards = {n: _pad_shard(n, _local_view(n, wts[n])).astype(BF16) for n in _SHARDED}
    w = {n: wts[n].reshape(1, -1) for n in _SMALL_VECTORS}
    w["pool_w"] = pool_w[0].astype(BF16)
    grads = {}
    core = lax.axis_index("c").astype(jnp.int32).reshape(1)
    plan = _Plan(shards, w, grads, core)
    _PLAN = plan
    try:
        w["ffn1_shards"] = tuple(shards[n] for n in _FFN1)

        loss_local, dx = _local_step(x[0], mem[0], positions[0], loss_target[0], w, grads)

        def small_view(d):
            out = {n: d[n].reshape(1, -1) for n in _SMALL_VECTORS}
            out["pool_w"] = d["pool_w"].reshape(_POOL_ROWS, POOL_CH)
            return out

        *small_res, loss_vec = _small_params_step(small_view(grads), small_view(wts), small_view(mom),
                                                  small_view(var), loss_local)
        g_out, d_out, m_out, v_out = ({n: r[n].reshape(wts[n].shape) for n in small} for r in small_res)
        loss = loss_vec[0, 0]

        for names in _Plan.ADAMW_ORDER:
            res = _adamw_sum("adamw_" + names[0], [_local_view(n, wts[n]) for n in names],
                             [_unpad_shard(n, plan.parts[n]) for n in names],
                             [_local_view(n, mom[n]) for n in names], [_local_view(n, var[n]) for n in names])
            for n, r4 in zip(names, res):
                g_out[n], d_out[n], m_out[n], v_out[n] = (_global_view(n, r) for r in r4)
    finally:
        _PLAN = None
        _PENDING.clear()

    return (loss, dx[None], *[g_out[n] for n in _WEIGHTS], *[d_out[n] for n in _WEIGHTS],
            *[m_out[n] for n in _WEIGHTS], *[v_out[n] for n in _WEIGHTS])
```

```python
import jax
import jax.numpy as jnp
from jax import lax
from jax.experimental import pallas as pl
from jax.experimental.pallas import tpu as pltpu

F32 = jnp.float32
BF16 = jnp.bfloat16

D_MODEL = 1024
D_FF = 2816
N_CHIPS = 4
FF_SHARD = D_FF // N_CHIPS
MLA_HEADS = 4
Q_LORA = 256
KV_LORA = 128
ROPE_DIM = 64
HEAD_QK = 256
HEAD_V = 128
POOL_GROUPS = 4
POOL_CH = 128
MEM_HEADS = 4
MEM_HEAD_DIM = 256
RMS_EPS = 1e-6
ROPE_BASE = 10000.0
MLA_SCALE = (128 + 64) ** -0.5
MEM_SCALE = MEM_HEAD_DIM ** -0.5

ADAM_LR = 0.001
ADAM_B1 = 0.9
ADAM_B2 = 0.999
ADAM_EPS = 1e-08
ADAM_WD = 0.01
ADAM_STEP = 10

V7X_VMEM_LIMIT_BYTES = 56 * 1024 * 1024

NN = ((1,), (0,))
NT = ((1,), (1,))
TN = ((0,), (0,))


def _params(*sem):
    return pltpu.CompilerParams(dimension_semantics=sem, vmem_limit_bytes=V7X_VMEM_LIMIT_BYTES)


_MESH = pl.DeviceIdType.MESH
_ANY = pl.BlockSpec(memory_space=pl.ANY)


class _Stage:
    def __init__(self, ins, outs, n_remote, n_local, copies, aliases=None):
        self.ins, self.outs, self.n_remote, self.n_local = list(ins), list(outs), n_remote, n_local
        self.copies, self.aliases = copies, dict(aliases or {})
        self.results = None
        self.start_step = 0
        self.then = None

    def descriptors(self, in_refs, out_refs, send, recv, loc):
        ds, ri, li = [], 0, 0
        for src, dst, dev in self.copies(in_refs, out_refs):
            if dev is None:
                ds.append(pltpu.make_async_copy(src, dst, loc.at[li]))
                li += 1
            else:
                ds.append(pltpu.make_async_remote_copy(src_ref=src, dst_ref=dst, send_sem=send.at[ri],
                                                       recv_sem=recv.at[ri], device_id=dev, device_id_type=_MESH))
                ri += 1
        assert ri == self.n_remote and li == self.n_local
        return ds


_PENDING = {}


def _host(name, stage):
    _PENDING.setdefault(name, []).append(stage)
    return stage


_PLAN = None


def _call(body, **kw):
    if _PLAN is not None:
        _PLAN.pre(kw["name"])
    res = _call_hosting(body, **kw)
    if _PLAN is not None:
        _PLAN.post(kw["name"])
    return res


def _call_hosting(body, *, name, grid, in_specs, out_specs, out_shape, sem, args, scratch_shapes=(), aliases=None):
    stages = _PENDING.pop(name, [])
    scratch_shapes = list(scratch_shapes)
    if not stages:
        return pl.pallas_call(body, name=name, grid=grid, in_specs=in_specs, out_specs=out_specs,
                              out_shape=out_shape, scratch_shapes=scratch_shapes,
                              input_output_aliases=dict(aliases or {}), compiler_params=_params(*sem))(*args)
    ni, no, ns = len(in_specs), len(out_shape), len(scratch_shapes)
    c_ins = [a for st in stages for a in st.ins]
    c_outs = [o for st in stages for o in st.outs]
    nci, nco = len(c_ins), len(c_outs)
    aliases, io, oo = dict(aliases or {}), 0, 0
    for st in stages:
        for i, j in st.aliases.items():
            aliases[ni + io + i] = no + oo + j
        io += len(st.ins)
        oo += len(st.outs)
    dma = pltpu.SemaphoreType.DMA
    sems = []
    for st in stages:
        sems += [dma((max(st.n_remote, 1),)), dma((max(st.n_remote, 1),)), dma((max(st.n_local, 1),))]
    followers = [st.then for st in stages if st.then is not None]
    for st in followers:
        sems += [dma((max(st.n_remote, 1),)), dma((max(st.n_remote, 1),)), dma((max(st.n_local, 1),))]

    def wrapped(*refs):
        ins, cin = refs[:ni], refs[ni:ni + nci]
        outs, cout = refs[ni + nci:ni + nci + no], refs[ni + nci + no:ni + nci + no + nco]
        scr = refs[ni + nci + no + nco:ni + nci + no + nco + ns]
        sem_refs = refs[ni + nci + no + nco + ns:]
        step = pl.program_id(0)
        last = pl.program_id(0) == grid[0] - 1
        for ax in range(1, len(grid)):
            step = step * grid[ax] + pl.program_id(ax)
            last = jnp.logical_and(last, pl.program_id(ax) == grid[ax] - 1)

        def descriptors(si):
            io = sum(len(st.ins) for st in stages[:si])
            oo = sum(len(st.outs) for st in stages[:si])
            st = stages[si]
            return st.descriptors(cin[io:io + len(st.ins)], cout[oo:oo + len(st.outs)], *sem_refs[3 * si:3 * si + 3])

        def follower_descriptors(fi):
            si = [k for k, st in enumerate(stages) if st.then is not None][fi]
            oo = sum(len(st.outs) for st in stages[:si])
            bufs = cout[oo:oo + len(stages[si].outs)]
            k0 = 3 * (len(stages) + fi)
            return followers[fi].descriptors(bufs, bufs, *sem_refs[k0:k0 + 3])

        def start(si):
            @pl.when(step == stages[si].start_step)
            def _():
                for d in descriptors(si):
                    d.start()

        for si, st in enumerate(stages):
            if st.start_step == 0:
                start(si)
        body(*ins, *outs, *scr)
        for si, st in enumerate(stages):
            if st.start_step != 0:
                start(si)

        @pl.when(last)
        def _():
            for si in range(len(stages)):
                for d in descriptors(si):
                    d.wait()
            for fi in range(len(followers)):
                for d in follower_descriptors(fi):
                    d.start()
            for fi in range(len(followers)):
                for d in follower_descriptors(fi):
                    d.wait()

    res = pl.pallas_call(
        wrapped, name=name, grid=grid, in_specs=list(in_specs) + [_ANY] * nci,
        out_specs=list(out_specs) + [_ANY] * nco, out_shape=list(out_shape) + c_outs,
        scratch_shapes=scratch_shapes + sems, input_output_aliases=aliases,
        compiler_params=_params(*(("arbitrary",) * len(grid))))(*args, *c_ins)
    oo = no
    for st in stages:
        st.results = list(res[oo:oo + len(st.outs)])
        oo += len(st.outs)
    return list(res[:no])


def _dot(a, b, dims):
    return lax.dot_general(a.astype(BF16), b.astype(BF16), (dims, ((), ())), preferred_element_type=F32)


_MAX_ROW_BLOCK = 1024
_ATT_BLOCK = 512


_MAX_REDUCE_BLOCK = 2048


def _row_block(s, want=1024):
    return min(want, s, _MAX_ROW_BLOCK)


def _reduce_block(s):
    return min(s, _MAX_REDUCE_BLOCK)


def _matmul(name, grid, terms, extras, outs, epilogue, acc_shape, fill=(), summed=()):
    nt, ne, no, nf = len(terms), len(extras), len(outs), len(fill)
    nk = grid[-1]
    dims = [t[4] for t in terms]

    def body(*refs):
        a_refs, b_refs = refs[:nt], refs[nt:2 * nt]
        e_refs = refs[2 * nt:2 * nt + ne]
        o_refs = refs[2 * nt + ne + nf:2 * nt + ne + nf + no]

        def finish(acc):
            vals = epilogue(acc, *[e[...] for e in e_refs])
            for idx, (o, val) in enumerate(zip(o_refs, vals)):
                if idx in summed:
                    @pl.when(pl.program_id(0) == 0)
                    def _(o=o, val=val):
                        o[...] = val.astype(o.dtype)

                    @pl.when(pl.program_id(0) > 0)
                    def _(o=o, val=val):
                        o[...] += val.astype(o.dtype)
                else:
                    o[...] = val.astype(o.dtype)

        if nk == 1:
            part = None
            for a, b, d in zip(a_refs, b_refs, dims):
                t = _dot(a[...], b[...], d)
                part = t if part is None else part + t
            finish(part)
        else:
            acc_ref = refs[-1]
            k = pl.program_id(len(grid) - 1)

            @pl.when(k == 0)
            def _():
                acc_ref[...] = jnp.zeros_like(acc_ref)

            for a, b, d in zip(a_refs, b_refs, dims):
                acc_ref[...] += _dot(a[...], b[...], d)

            @pl.when(k == nk - 1)
            def _():
                finish(acc_ref[...])

    in_specs = [t[1] for t in terms] + [t[3] for t in terms] + [e[1] for e in extras] + [_ANY] * nf
    args = [t[0] for t in terms] + [t[2] for t in terms] + [e[0] for e in extras] + list(fill)
    sem = ("arbitrary" if summed else "parallel",) * (len(grid) - 1) + ("arbitrary",)
    aliases = {2 * nt + ne + i: i for i in range(nf)}
    return _call(
        body, name=name, grid=grid, in_specs=in_specs,
        out_specs=[o[1] for o in outs], out_shape=[o[0] for o in outs],
        scratch_shapes=[pltpu.VMEM(acc_shape, F32)] if nk > 1 else [], sem=sem, args=args, aliases=aliases)


def _ident(acc):
    return (acc,)


def _rmsnorm_fwd(name, x, gain, width, col_block=0):
    s = x.shape[0]
    bm = _row_block(s)

    def body(x_ref, g_ref, o_ref):
        xf = x_ref[...]
        r = lax.rsqrt(jnp.mean(xf * xf, axis=-1, keepdims=True) + RMS_EPS)
        o_ref[...] = ((xf * r) * g_ref[...]).astype(o_ref.dtype)

    return pl.pallas_call(
        body, name=name, grid=(s // bm,),
        in_specs=[pl.BlockSpec((bm, width), lambda i: (i, col_block)), pl.BlockSpec((1, width), lambda i: (0, 0))],
        out_specs=pl.BlockSpec((bm, width), lambda i: (i, 0)),
        out_shape=jax.ShapeDtypeStruct((s, width), BF16),
        compiler_params=_params("parallel"),
    )(x, gain)


def _rms_bwd_math(dy, xf, g, width):
    r = lax.rsqrt(jnp.mean(xf * xf, axis=-1, keepdims=True) + RMS_EPS)
    dyg = dy * g
    dot = jnp.sum(dyg * xf, axis=-1, keepdims=True)
    dx = r * dyg - xf * ((r * r * r) * (dot * (1.0 / width)))
    dgain = jnp.sum(dy * (xf * r), axis=0, keepdims=True)
    return dx, dgain


def _rmsnorm_bwd(name, dy, x, gain, width, col_block=0, dres=None, out_dtype=F32):
    s = x.shape[0]
    bm = _row_block(s)
    has_res = dres is not None

    def body(*refs):
        if has_res:
            dy_ref, x_ref, g_ref, r_ref, dx_ref, dg_ref = refs
        else:
            dy_ref, x_ref, g_ref, dx_ref, dg_ref = refs
        dx, dgain = _rms_bwd_math(dy_ref[...].astype(F32), x_ref[...], g_ref[...], width)
        if has_res:
            dx = dx + r_ref[...]
        dx_ref[...] = dx.astype(dx_ref.dtype)

        @pl.when(pl.program_id(0) == 0)
        def _():
            dg_ref[...] = dgain

        @pl.when(pl.program_id(0) > 0)
        def _():
            dg_ref[...] += dgain

    row = pl.BlockSpec((bm, width), lambda i: (i, 0))
    in_specs = [row, pl.BlockSpec((bm, width), lambda i: (i, col_block)), pl.BlockSpec((1, width), lambda i: (0, 0))]
    args = [dy, x, gain]
    out_specs = [row, pl.BlockSpec((1, width), lambda i: (0, 0))]
    out_shape = [jax.ShapeDtypeStruct((s, width), out_dtype), jax.ShapeDtypeStruct((1, width), F32)]
    if has_res:
        in_specs.append(row)
        args.append(dres)
    return _call(body, name=name, grid=(s // bm,), in_specs=in_specs, out_specs=out_specs, out_shape=out_shape,
                 sem=("arbitrary",), args=args)


def _ffn_up(name, n, wg, wu):
    s = n.shape[0]
    bm = _row_block(s)

    def body(n_ref, wg_ref, wu_ref, a_ref, dadu_ref, dadg_ref):
        x = n_ref[...]
        g = _dot(x, wg_ref[...], NT)
        u = _dot(x, wu_ref[...], NT)
        sg = jax.nn.sigmoid(g)
        silu = g * sg
        a_ref[...] = (silu * u).astype(BF16)
        dadu_ref[...] = silu.astype(BF16)
        dadg_ref[...] = (u * (sg * (1.0 + g * (1.0 - sg)))).astype(BF16)

    w_spec = pl.BlockSpec((None, FF_SHARD, D_MODEL), lambda j, i: (j, 0, 0))
    o_spec = pl.BlockSpec((None, bm, FF_SHARD), lambda j, i: (j, i, 0))
    shp = jax.ShapeDtypeStruct((N_CHIPS, s, FF_SHARD), BF16)
    return _call(
        body, name=name, grid=(N_CHIPS, s // bm),
        in_specs=[pl.BlockSpec((bm, D_MODEL), lambda j, i: (i, 0)), w_spec, w_spec],
        out_specs=[o_spec, o_spec, o_spec], out_shape=[shp, shp, shp],
        sem=("parallel", "parallel"), args=[n, wg, wu])


def _ffn1_up_gather(xin, gain, g_sh, u_sh, d_sh):
    s = xin.shape[0]
    bm = _row_block(s)
    nrb = s // bm
    rows, cols = g_sh.shape

    def body(x_ref, gain_ref, gs, us, ds, n_ref, a_ref, dadu_ref, dadg_ref, wg, wu, wd, gbuf, ubuf,
             send, recv, qsend, qrecv, fsend, frecv, loc, ld):
        r, i = pl.program_id(0), pl.program_id(1)
        x, y, c = lax.axis_index("x"), lax.axis_index("y"), lax.axis_index("c")
        sib = (x, y, 1 - c)
        mine, _ = _half_rows(c, rows)
        quarters = _quarter_rows(c, rows)
        shards, fulls, bufs = (gs, us, ds), (wg, wu, wd), (gbuf, ubuf)

        def remote(src, dst, ssem, rsem, dev):
            return pltpu.make_async_remote_copy(src_ref=src, dst_ref=dst, send_sem=ssem, recv_sem=rsem,
                                                device_id=dev, device_id_type=_MESH)

        def peer(rel):
            return ((1 - x) if rel & 2 else x, (1 - y) if rel & 1 else y, c)

        def ici(k, rel, dev=sib):
            return remote(shards[k].at[mine], fulls[k].at[rel, mine], send.at[k, rel - 1], recv.at[k, rel - 1], dev)

        def quarter(k, which, dev=sib):
            slab, q = ((2, quarters[0]), (1, quarters[1]))[which]
            return remote(fulls[k].at[slab, q], fulls[k].at[3, q], qsend.at[k, which], qrecv.at[k, which], dev)

        def fwd(k, rel):
            return remote(fulls[k].at[rel, mine], fulls[k].at[rel, mine], fsend.at[k, rel - 1], frecv.at[k, rel - 1], sib)

        def own(k):
            return pltpu.make_async_copy(shards[k], fulls[k].at[0], loc.at[k])

        def load(slab):
            for k in (0, 1):
                pltpu.make_async_copy(shards[k] if slab == 0 else fulls[k].at[slab], bufs[k], ld.at[k]).start()
            for k in (0, 1):
                pltpu.make_async_copy(shards[k] if slab == 0 else fulls[k].at[slab], bufs[k], ld.at[k]).wait()

        def from_neighbour(ks, rel):
            for k in ks:
                ici(k, rel).wait_recv()
                fwd(k, rel).start()
                quarter(k, 0 if rel == 2 else 1, peer(1 if rel == 2 else 2)).start()
            for k in ks:
                fwd(k, rel).wait_recv()

        def from_diagonal(ks):
            for k in ks:
                quarter(k, 0).wait_recv()
                quarter(k, 1).wait_recv()
                fwd(k, 3).start()
            for k in ks:
                fwd(k, 3).wait_recv()

        @pl.when(jnp.logical_and(r == 0, i == 0))
        def _():
            for k in range(3):
                own(k).start()
            for rel in (1, 2):
                for k in (0, 1):
                    ici(k, rel, peer(rel)).start()
            load(0)

        @pl.when(jnp.logical_and(r == 1, i == 0))
        def _():
            from_neighbour((0, 1), 1)
            load(1)
            for rel in (1, 2):
                ici(2, rel, peer(rel)).start()

        @pl.when(jnp.logical_and(r == 2, i == 0))
        def _():
            from_neighbour((0, 1), 2)
            load(2)

        @pl.when(jnp.logical_and(r == 3, i == 0))
        def _():
            from_diagonal((0, 1))
            load(3)

        xv = _norm_bf16(x_ref[...], gain_ref[...])

        @pl.when(r == 0)
        def _():
            n_ref[...] = xv

        g = _dot(xv, gbuf[...], NT)
        u = _dot(xv, ubuf[...], NT)
        sg = jax.nn.sigmoid(g)
        silu = g * sg
        a_ref[...] = (silu * u).astype(BF16)
        dadu_ref[...] = silu.astype(BF16)
        dadg_ref[...] = (u * (sg * (1.0 + g * (1.0 - sg)))).astype(BF16)

        @pl.when(jnp.logical_and(r == 3, i == nrb - 1))
        def _():
            from_neighbour((2,), 1)
            from_neighbour((2,), 2)
            from_diagonal((2,))
            for k in range(3):
                for rel in (1, 2):
                    ici(k, rel).wait_send()
                for which in (0, 1):
                    quarter(k, which).wait_send()
                for rel in (1, 2, 3):
                    fwd(k, rel).wait_send()
                own(k).wait()

    o_spec = pl.BlockSpec((None, bm, FF_SHARD), lambda r, i: (r, i, 0))
    act = jax.ShapeDtypeStruct((N_CHIPS, s, FF_SHARD), BF16)
    full = jax.ShapeDtypeStruct((N_CHIPS, rows, cols), BF16)
    dma = pltpu.SemaphoreType.DMA
    if _PLAN is not None:
        _PLAN.last_slab_step = 3 * nrb
    n_spec = pl.BlockSpec((bm, D_MODEL), lambda r, i: (jnp.where(r == 0, i, nrb - 1), 0))
    return _call(
        body, name="ffn1_up", grid=(N_CHIPS, nrb),
        in_specs=[pl.BlockSpec((bm, D_MODEL), lambda r, i: (i, 0)), pl.BlockSpec((1, D_MODEL), lambda r, i: (0, 0)),
                  _ANY, _ANY, _ANY],
        out_specs=[n_spec, o_spec, o_spec, o_spec, _ANY, _ANY, _ANY],
        out_shape=[jax.ShapeDtypeStruct((s, D_MODEL), BF16), act, act, act, full, full, full],
        scratch_shapes=[pltpu.VMEM((rows, cols), BF16), pltpu.VMEM((rows, cols), BF16), dma((3, 2)), dma((3, 2)),
                        dma((3, 2)), dma((3, 2)), dma((3, 3)), dma((3, 3)), dma((3,)), dma((2,))],
        sem=("arbitrary", "arbitrary"), args=[xin, gain, g_sh, u_sh, d_sh])


def _residual_epilogue(alpha, with_norm):
    if not with_norm:
        return lambda acc, r: (r + alpha * acc,)

    def epilogue(acc, r, g):
        h = r + alpha * acc
        rs = lax.rsqrt(jnp.mean(h * h, axis=-1, keepdims=True) + RMS_EPS)
        return h, (h * rs) * g

    return epilogue


def _residual_outs(s, bm, gain):
    row = pl.BlockSpec((bm, D_MODEL), lambda i, k: (i, 0))
    outs = [(jax.ShapeDtypeStruct((s, D_MODEL), F32), row)]
    if gain is None:
        return [], outs
    return [(gain, pl.BlockSpec((1, D_MODEL), lambda i, k: (0, 0)))], outs + [(jax.ShapeDtypeStruct((s, D_MODEL), BF16), row)]


def _loss_epilogue(acc, res, g, target):
    d = acc.shape[-1]
    h = res + 0.5 * acc
    r = lax.rsqrt(jnp.mean(h * h, axis=-1, keepdims=True) + RMS_EPS)
    err = (h * r) * g - target
    part = 0.5 * jnp.sum(jnp.mean(err * err, axis=-1, keepdims=True), axis=0, keepdims=True)
    dx, dgain = _rms_bwd_math(err * (1.0 / d), h, g, d)
    return dx, dx, jnp.broadcast_to(part, (1, 128)), dgain


def _ffn_down(name, a, wd, res, gain=None, loss=None):
    s = a.shape[1]
    bm = _row_block(s, 512)
    row = pl.BlockSpec((bm, D_MODEL), lambda i, k: (i, 0))
    terms = [(a, pl.BlockSpec((None, bm, FF_SHARD), lambda i, k, j=j: (j, i, 0)),
              wd, pl.BlockSpec((None, FF_SHARD, D_MODEL), lambda i, k, j=j: (j, 0, 0)), NN) for j in range(N_CHIPS)]
    if loss is not None:
        vec = pl.BlockSpec((1, D_MODEL), lambda i, k: (0, 0))
        outs = [(jax.ShapeDtypeStruct((s, D_MODEL), F32), row), (jax.ShapeDtypeStruct((s, D_MODEL), BF16), row),
                (jax.ShapeDtypeStruct((1, 128), F32), pl.BlockSpec((1, 128), lambda i, k: (0, 0))),
                (jax.ShapeDtypeStruct((1, D_MODEL), F32), vec)]
        return _matmul(name, (s // bm, 1), terms, [(res, row), (loss[0], vec), (loss[1], row)], outs,
                       _loss_epilogue, None, summed=(2, 3))
    extras, outs = _residual_outs(s, bm, gain)
    res_out = _matmul(name, (s // bm, 1), terms, [(res, row)] + extras, outs,
                      _residual_epilogue(0.5, gain is not None), None)
    return res_out if gain is not None else res_out[0]


def _norm_bwd_epilogue(width):
    def epilogue(acc, h, g, dres):
        dx, dgain = _rms_bwd_math(acc, h, g, width)
        dx = dx + dres
        return dx, dx, dgain

    return epilogue


def _norm_bwd_operands(s, bm, h, gain, dres):
    row = pl.BlockSpec((bm, D_MODEL), lambda i, k: (i, 0))
    vec = pl.BlockSpec((1, D_MODEL), lambda i, k: (0, 0))
    extras = [(h, row), (gain, vec), (dres, row)]
    outs = [(jax.ShapeDtypeStruct((s, D_MODEL), F32), row), (jax.ShapeDtypeStruct((s, D_MODEL), BF16), row),
            (jax.ShapeDtypeStruct((1, D_MODEL), F32), vec)]
    return extras, outs, (2,)


def _ffn_bwd(tag, dh, n, dadg, dadu, a, wg, wu, wd, grads, norm_bwd=None, many_calls=True):
    s = dh.shape[0]
    bm = _row_block(s)
    bk = _reduce_block(s)
    nk = s // bk

    def act_bwd(acc, dg_da, du_da):
        da = 0.5 * acc
        return da * dg_da.astype(F32), da * du_da.astype(F32)

    slab = pl.BlockSpec((None, bm, FF_SHARD), lambda j, i, k: (j, i, 0))
    shp = jax.ShapeDtypeStruct((N_CHIPS, s, FF_SHARD), BF16)
    dg, du = _matmul(
        tag + "_dact", (N_CHIPS, s // bm, 1),
        [(dh, pl.BlockSpec((bm, D_MODEL), lambda j, i, k: (i, 0)),
          wd, pl.BlockSpec((None, FF_SHARD, D_MODEL), lambda j, i, k: (j, 0, 0)), NT)],
        [(dadg, slab), (dadu, slab)], [(shp, slab), (shp, slab)], act_bwd, None)

    grads[tag + "_w_down"] = _matmul(
        tag + "_dwd", (N_CHIPS, nk),
        [(a, pl.BlockSpec((None, bk, FF_SHARD), lambda j, k: (j, k, 0)),
          dh, pl.BlockSpec((bk, D_MODEL), lambda j, k: (k, 0)), TN)],
        [], [(jax.ShapeDtypeStruct((N_CHIPS, FF_SHARD, D_MODEL), BF16),
              pl.BlockSpec((None, FF_SHARD, D_MODEL), lambda j, k: (j, 0, 0)))],
        lambda acc: (0.5 * acc,), (FF_SHARD, D_MODEL))[0]

    def dw_up(nm, dact):
        return _matmul(
            nm, (N_CHIPS, nk),
            [(dact, pl.BlockSpec((None, bk, FF_SHARD), lambda j, k: (j, k, 0)),
              n, pl.BlockSpec((bk, D_MODEL), lambda j, k: (k, 0)), TN)],
            [], [(jax.ShapeDtypeStruct((N_CHIPS, FF_SHARD, D_MODEL), BF16),
                  pl.BlockSpec((None, FF_SHARD, D_MODEL), lambda j, k: (j, 0, 0)))],
            _ident, (FF_SHARD, D_MODEL))[0]

    def dw_up_pair(nm):
        def body(dg_ref, du_ref, n_ref, og_ref, ou_ref, acc_g, acc_u):
            k = pl.program_id(1)

            @pl.when(k == 0)
            def _():
                acc_g[...] = jnp.zeros_like(acc_g)
                acc_u[...] = jnp.zeros_like(acc_u)

            nv = n_ref[...]
            acc_g[...] += _dot(dg_ref[...], nv, TN)
            acc_u[...] += _dot(du_ref[...], nv, TN)

            @pl.when(k == nk - 1)
            def _():
                og_ref[...] = acc_g[...].astype(BF16)
                ou_ref[...] = acc_u[...].astype(BF16)

        act = pl.BlockSpec((None, bk, FF_SHARD), lambda j, k: (j, k, 0))
        out = pl.BlockSpec((None, FF_SHARD, D_MODEL), lambda j, k: (j, 0, 0))
        shape = jax.ShapeDtypeStruct((N_CHIPS, FF_SHARD, D_MODEL), BF16)
        return _call(body, name=nm, grid=(N_CHIPS, nk),
                     in_specs=[act, act, pl.BlockSpec((bk, D_MODEL), lambda j, k: (k, 0))], out_specs=[out, out],
                     out_shape=[shape, shape],
                     scratch_shapes=[pltpu.VMEM((FF_SHARD, D_MODEL), F32), pltpu.VMEM((FF_SHARD, D_MODEL), F32)],
                     sem=("parallel", "arbitrary"), args=[dg, du, n])

    if many_calls:
        grads[tag + "_w_gate"] = dw_up(tag + "_dwg", dg)
        grads[tag + "_w_up"] = dw_up(tag + "_dwu", du)
    else:
        grads[tag + "_w_gate"], grads[tag + "_w_up"] = dw_up_pair(tag + "_dwgu")

    bn = _row_block(s, 512)
    steps = s // bn // 2 if many_calls else s // bn
    prev, dgain = (), None
    for part, off in ((("_dn_a", 0), ("_dn_b", steps)) if many_calls else (("_dn", 0),)):
        row = pl.BlockSpec((bn, D_MODEL), lambda i, k, off=off: (i + off, 0))
        terms = []
        for j in range(N_CHIPS):
            a_slab = pl.BlockSpec((None, bn, FF_SHARD), lambda i, k, j=j, off=off: (j, i + off, 0))
            w_slab = pl.BlockSpec((None, FF_SHARD, D_MODEL), lambda i, k, j=j: (j, 0, 0))
            terms += [(dg, a_slab, wg, w_slab, NN), (du, a_slab, wu, w_slab, NN)]
        if norm_bwd is None:
            prev = _matmul(tag + part, (steps, 1), terms, [], [(jax.ShapeDtypeStruct((s, D_MODEL), F32), row)],
                           _ident, None, fill=prev)
            continue
        h, gain, dres = norm_bwd
        vec = pl.BlockSpec((1, D_MODEL), lambda i, k: (0, 0))
        res = _matmul(
            tag + part, (steps, 1), terms, [(h, row), (gain, vec), (dres, row)],
            [(jax.ShapeDtypeStruct((s, D_MODEL), F32), row), (jax.ShapeDtypeStruct((s, D_MODEL), BF16), row),
             (jax.ShapeDtypeStruct((1, D_MODEL), F32), vec)],
            _norm_bwd_epilogue(D_MODEL), None, fill=prev, summed=(2,))
        prev = res[:2]
        dgain = res[2] if dgain is None else dgain + res[2]
    return prev[0] if norm_bwd is None else (prev[0], prev[1], dgain)


def _mm_nn(name, a, b, out_dtype, res=None, gain=None):
    s, k = a.shape
    nn = b.shape[1]
    bm = _row_block(s)
    row = pl.BlockSpec((bm, nn), lambda i, kk: (i, 0))
    term = [(a, pl.BlockSpec((bm, k), lambda i, kk: (i, 0)), b, pl.BlockSpec((k, nn), lambda i, kk: (0, 0)), NN)]
    if res is None:
        return _matmul(name, (s // bm, 1), term, [], [(jax.ShapeDtypeStruct((s, nn), out_dtype), row)], _ident, None)[0]
    extras, outs = _residual_outs(s, bm, gain)
    res_out = _matmul(name, (s // bm, 1), term, [(res, row)] + extras, outs,
                      _residual_epilogue(1.0, gain is not None), None)
    return res_out if gain is not None else res_out[0]


def _mm_nt(name, a, b, out_dtype, attn_out=None, nh=0, dv=0):
    s, nn = a.shape
    k = b.shape[0]
    bm = _row_block(s)
    term = [(a, pl.BlockSpec((bm, nn), lambda i, kk: (i, 0)), b, pl.BlockSpec((k, nn), lambda i, kk: (0, 0)), NT)]
    out = (jax.ShapeDtypeStruct((s, k), out_dtype), pl.BlockSpec((bm, k), lambda i, kk: (i, 0)))
    if attn_out is None:
        return _matmul(name, (s // bm, 1), term, [], [out], _ident, None)[0]

    def with_delta(acc, o):
        do = acc.astype(out_dtype).astype(F32)
        cols = [jnp.sum(do[:, h * dv:(h + 1) * dv] * o[:, h * dv:(h + 1) * dv].astype(F32), axis=-1, keepdims=True)
                for h in range(nh)]
        return acc, jnp.stack(cols, axis=0)

    return _matmul(
        name, (s // bm, 1), term, [(attn_out, pl.BlockSpec((bm, nh * dv), lambda i, kk: (i, 0)))],
        [out, (jax.ShapeDtypeStruct((nh, s, 1), F32), pl.BlockSpec((nh, bm, 1), lambda i, kk: (0, i, 0)))],
        with_delta, None)


def _mm_nt_norm_bwd(name, a, b, h, gain, dres):
    s, nn = a.shape
    bm = _row_block(s, 512)
    extras, outs, summed = _norm_bwd_operands(s, bm, h, gain, dres)
    return _matmul(
        name, (s // bm, 1),
        [(a, pl.BlockSpec((bm, nn), lambda i, kk: (i, 0)), b, pl.BlockSpec(b.shape, lambda i, kk: (0, 0)), NT)],
        extras, outs, _norm_bwd_epilogue(D_MODEL), None, summed=summed)


def _w_in_dx_norm_bwd(dz, w_t, h, gain, dres):
    s = dz.shape[0]
    bm = _row_block(s, 512)
    epilogue = _norm_bwd_epilogue(D_MODEL)

    def body(dz_ref, w_ref, h_ref, g_ref, r_ref, dx_ref, dxb_ref, dg_ref):
        dzv = dz_ref[...]
        dn = jnp.concatenate([_dot(dzv, w_ref[j], NN) for j in range(N_CHIPS)], axis=1)
        dx, _, dgain = epilogue(dn, h_ref[...], g_ref[...], r_ref[...])
        dx_ref[...] = dx
        dxb_ref[...] = dx.astype(BF16)

        @pl.when(pl.program_id(0) == 0)
        def _():
            dg_ref[...] = dgain

        @pl.when(pl.program_id(0) > 0)
        def _():
            dg_ref[...] += dgain

    row = pl.BlockSpec((bm, D_MODEL), lambda i: (i, 0))
    vec = pl.BlockSpec((1, D_MODEL), lambda i: (0, 0))
    return _call(
        body, name="w_in_dx", grid=(s // bm,),
        in_specs=[row, pl.BlockSpec(w_t.shape, lambda i: (0, 0, 0)), row, vec, row],
        out_specs=[row, row, vec],
        out_shape=[jax.ShapeDtypeStruct((s, D_MODEL), F32), jax.ShapeDtypeStruct((s, D_MODEL), BF16),
                   jax.ShapeDtypeStruct((1, D_MODEL), F32)],
        sem=("arbitrary",), args=[dz, w_t, h, gain, dres])


def _mm_tn_stacked(name, a_list, b):
    s, nn = b.shape
    widths = [a.shape[1] for a in a_list]
    total = sum(widths)
    bk = _reduce_block(s)
    nk = s // bk
    na = len(a_list)

    def body(*refs):
        a_refs, b_ref, o_ref, acc_ref = refs[:na], refs[na], refs[na + 1], refs[na + 2]
        k = pl.program_id(0)

        @pl.when(k == 0)
        def _():
            acc_ref[...] = jnp.zeros_like(acc_ref)

        bv = b_ref[...]
        lo = 0
        for a_ref, w in zip(a_refs, widths):
            acc_ref[lo:lo + w, :] += _dot(a_ref[...], bv, TN)
            lo += w

        @pl.when(k == nk - 1)
        def _():
            o_ref[...] = acc_ref[...].astype(o_ref.dtype)

    return _call(
        body, name=name, grid=(nk,),
        in_specs=[pl.BlockSpec((bk, w), lambda k: (k, 0)) for w in widths] + [pl.BlockSpec((bk, nn), lambda k: (k, 0))],
        out_specs=[pl.BlockSpec((total, nn), lambda k: (0, 0))],
        out_shape=[jax.ShapeDtypeStruct((total, nn), BF16)],
        scratch_shapes=[pltpu.VMEM((total, nn), F32)], sem=("arbitrary",), args=list(a_list) + [b])[0]


def _mm_tn_pairs(name, pairs):
    s = pairs[0][0].shape[0]
    bk = _row_block(s)
    nk = s // bk
    npairs = len(pairs)
    shapes = [(a.shape[1], b.shape[1]) for a, b in pairs]

    def body(*refs):
        ins, outs, accs = refs[:2 * npairs], refs[2 * npairs:3 * npairs], refs[3 * npairs:]
        k = pl.program_id(0)

        @pl.when(k == 0)
        def _():
            for acc in accs:
                acc[...] = jnp.zeros_like(acc)

        for p in range(npairs):
            accs[p][...] += _dot(ins[2 * p][...], ins[2 * p + 1][...], TN)

        @pl.when(k == nk - 1)
        def _():
            for o, acc in zip(outs, accs):
                o[...] = acc[...].astype(o.dtype)

    in_specs = [pl.BlockSpec((bk, x.shape[1]), lambda k: (k, 0)) for pair in pairs for x in pair]
    return _call(
        body, name=name, grid=(nk,), in_specs=in_specs,
        out_specs=[pl.BlockSpec(shp, lambda k: (0, 0)) for shp in shapes],
        out_shape=[jax.ShapeDtypeStruct(shp, BF16) for shp in shapes],
        scratch_shapes=[pltpu.VMEM(shp, F32) for shp in shapes], sem=("arbitrary",),
        args=[x for pair in pairs for x in pair])


def _mm_heads_fwd(name, a, w, out_dtype, w_transposed=False):
    s, k = a.shape
    nh = w.shape[0]
    nn = w.shape[1] if w_transposed else w.shape[2]
    bm = _row_block(s)
    return _matmul(
        name, (nh, s // bm, 1),
        [(a, pl.BlockSpec((bm, k), lambda h, i, kk: (i, 0)),
          w, pl.BlockSpec((None,) + w.shape[1:], lambda h, i, kk: (h, 0, 0)), NT if w_transposed else NN)],
        [], [(jax.ShapeDtypeStruct((s, nh * nn), out_dtype), pl.BlockSpec((bm, nn), lambda h, i, kk: (i, h)))],
        _ident, None)[0]


def _mm_heads_bwd(name, dy, a, w, w_transposed=False):
    s, k = a.shape
    nh = w.shape[0]
    nn = w.shape[1] if w_transposed else w.shape[2]
    bm = _row_block(s)
    bk = _reduce_block(s)
    w_spec = pl.BlockSpec((None,) + w.shape[1:], lambda i, h: (h, 0, 0))
    da = _matmul(
        name + "_dx", (s // bm, nh),
        [(dy, pl.BlockSpec((bm, nn), lambda i, h: (i, h)), w, w_spec, NN if w_transposed else NT)],
        [], [(jax.ShapeDtypeStruct((s, k), F32), pl.BlockSpec((bm, k), lambda i, h: (i, 0)))], _ident, (bm, k))[0]
    a_term = (a, pl.BlockSpec((bk, k), lambda h, kk: (kk, 0)))
    dy_term = (dy, pl.BlockSpec((bk, nn), lambda h, kk: (kk, h)))
    lhs, rhs = (dy_term, a_term) if w_transposed else (a_term, dy_term)
    dw = _matmul(
        name + "_dw", (nh, s // bk), [lhs + rhs + (TN,)],
        [], [(jax.ShapeDtypeStruct(w.shape, BF16), pl.BlockSpec((None,) + w.shape[1:], lambda h, kk: (h, 0, 0)))],
        _ident, w.shape[1:])[0]
    return da, dw


def _w_in_fwd(n, w_t):
    s = n.shape[0]
    bm = _row_block(s)
    nh, nout, kin = w_t.shape
    terms = [(n, pl.BlockSpec((bm, kin), lambda i, k, j=j: (i, j)),
              w_t, pl.BlockSpec((None, nout, kin), lambda i, k, j=j: (j, 0, 0)), NT) for j in range(nh)]
    row = pl.BlockSpec((bm, nout), lambda i, k: (i, 0))
    return _matmul("w_in", (s // bm, 1), terms, [], [(jax.ShapeDtypeStruct((s, nout), F32), row)], _ident, None)[0]


def _w_in_dw(dz, n):
    s, nout = dz.shape
    kin = n.shape[1] // N_CHIPS
    bk = _reduce_block(s)
    return _matmul(
        "w_in_dw", (N_CHIPS, s // bk),
        [(dz, pl.BlockSpec((bk, nout), lambda j, k: (k, 0)), n, pl.BlockSpec((bk, kin), lambda j, k: (k, j)), TN)],
        [], [(jax.ShapeDtypeStruct((N_CHIPS, nout, kin), BF16), pl.BlockSpec((None, nout, kin), lambda j, k: (j, 0, 0)))],
        _ident, (nout, kin))[0]


def _rope_tables(positions):
    half = ROPE_DIM // 2
    freqs = 1.0 / (ROPE_BASE ** (jnp.arange(0, ROPE_DIM, 2, dtype=F32) / ROPE_DIM))
    ang = positions.astype(F32)[:, None] * freqs
    cos, sin = jnp.cos(ang), jnp.sin(ang)
    z = jnp.zeros_like(cos)
    tc = jnp.concatenate([cos, cos, z, z], axis=-1)
    ta = jnp.concatenate([-sin, z, z, z], axis=-1)
    tb = jnp.concatenate([z, sin, z, z], axis=-1)
    assert tc.shape[-1] == 4 * half
    return tc, ta, tb


def _rope(x, tc, ta, tb):
    return x * tc + pltpu.roll(x, 96, 1) * ta + pltpu.roll(x, 32, 1) * tb


def _rope_t(dy, tc, ta, tb):
    return dy * tc + pltpu.roll(dy * ta, 32, 1) + pltpu.roll(dy * tb, 96, 1)


def _norm_bf16(x, g):
    r = lax.rsqrt(jnp.mean(x * x, axis=-1, keepdims=True) + RMS_EPS)
    return ((x * r) * g).astype(BF16)


def _qkv_prep(z, q_gain, kv_gain, wq_t, wkv, tc, ta, tb):
    s = z.shape[0]
    bm = _row_block(s, 512)

    def body(zq_ref, zkv_ref, zkr_ref, qg_ref, kvg_ref, wq_ref, wkv_ref, tc_ref, ta_ref, tb_ref,
             qn_ref, kvn_ref, q_ref, k_ref, v_ref):
        c, a, b = tc_ref[...], ta_ref[...], tb_ref[...]
        qn = _norm_bf16(zq_ref[...], qg_ref[...])
        kvn = _norm_bf16(zkv_ref[...], kvg_ref[...])
        qn_ref[...] = qn
        kvn_ref[...] = kvn
        kpe = _rope(zkr_ref[...], c, a, b).astype(BF16)
        for h in range(MLA_HEADS):
            lo = h * HEAD_QK
            qp = _dot(qn, wq_ref[h], NT)
            q_ref[:, lo:lo + 128] = qp[:, :128].astype(BF16)
            q_ref[:, lo + 128:lo + 256] = _rope(qp[:, 128:], c, a, b).astype(BF16)
            kv = _dot(kvn, wkv_ref[h], NN)
            k_ref[:, lo:lo + 128] = kv[:, :128].astype(BF16)
            k_ref[:, lo + 128:lo + 256] = kpe
            v_ref[:, h * HEAD_V:(h + 1) * HEAD_V] = kv[:, 128:].astype(BF16)

    def cols(width, blk):
        return pl.BlockSpec((bm, width), lambda i: (i, blk))

    def whole(a):
        return pl.BlockSpec(a.shape, lambda i: (0,) * a.ndim)

    tab = cols(128, 0)
    return _call(
        body, name="qkv_prep", grid=(s // bm,),
        in_specs=[cols(Q_LORA, 0), cols(KV_LORA, 2), cols(128, 3), whole(q_gain), whole(kv_gain), whole(wq_t),
                  whole(wkv), tab, tab, tab],
        out_specs=[cols(Q_LORA, 0), cols(KV_LORA, 0), cols(1024, 0), cols(1024, 0), cols(512, 0)],
        out_shape=[jax.ShapeDtypeStruct((s, Q_LORA), BF16), jax.ShapeDtypeStruct((s, KV_LORA), BF16),
                   jax.ShapeDtypeStruct((s, 1024), BF16), jax.ShapeDtypeStruct((s, 1024), BF16),
                   jax.ShapeDtypeStruct((s, 512), BF16)],
        sem=("parallel",), args=[z, z, z, q_gain, kv_gain, wq_t, wkv, tc, ta, tb])


def _qkv_prep_bwd(dq, dk, dv, z, qn, kvn, q_gain, kv_gain, wq_t, wkv, tc, ta, tb):
    s = z.shape[0]
    bm = _row_block(s, 512)
    nsteps = s // bm

    def body(dq_ref, dk_ref, dv_ref, zq_ref, zkv_ref, qn_ref, kvn_ref, qg_ref, kvg_ref, wq_ref, wkv_ref,
             tc_ref, ta_ref, tb_ref, dz_ref, dqg_ref, dkvg_ref, dwq_ref, dwkv_ref, wq_acc, wkv_acc):
        i = pl.program_id(0)
        c, a, b = tc_ref[...], ta_ref[...], tb_ref[...]

        @pl.when(i == 0)
        def _():
            wq_acc[...] = jnp.zeros_like(wq_acc)
            wkv_acc[...] = jnp.zeros_like(wkv_acc)

        qn, kvn = qn_ref[...], kvn_ref[...]
        dqn = jnp.zeros((bm, Q_LORA), F32)
        dkvn = jnp.zeros((bm, KV_LORA), F32)
        dpe = jnp.zeros((bm, 128), F32)
        for h in range(MLA_HEADS):
            lo = h * HEAD_QK
            dqp = jnp.concatenate([dq_ref[:, lo:lo + 128],
                                   _rope_t(dq_ref[:, lo + 128:lo + 256].astype(F32), c, a, b).astype(BF16)], axis=1)
            dqn = dqn + _dot(dqp, wq_ref[h], NN)
            wq_acc[h] += _dot(dqp, qn, TN)
            dkv = jnp.concatenate([dk_ref[:, lo:lo + 128], dv_ref[:, h * HEAD_V:(h + 1) * HEAD_V]], axis=1)
            dkvn = dkvn + _dot(dkv, wkv_ref[h], NT)
            wkv_acc[h] += _dot(kvn, dkv, TN)
            dpe = dpe + dk_ref[:, lo + 128:lo + 256].astype(F32)
        dcq, dqg = _rms_bwd_math(dqn, zq_ref[...], qg_ref[...], Q_LORA)
        dckv, dkvg = _rms_bwd_math(dkvn, zkv_ref[...], kvg_ref[...], KV_LORA)
        dz_ref[:, 0:Q_LORA] = dcq.astype(BF16)
        dz_ref[:, Q_LORA:Q_LORA + KV_LORA] = dckv.astype(BF16)
        dz_ref[:, Q_LORA + KV_LORA:512] = _rope_t(dpe, c, a, b).astype(BF16)

        @pl.when(i == 0)
        def _():
            dqg_ref[...] = dqg
            dkvg_ref[...] = dkvg

        @pl.when(i > 0)
        def _():
            dqg_ref[...] += dqg
            dkvg_ref[...] += dkvg

        @pl.when(i == nsteps - 1)
        def _():
            dwq_ref[...] = wq_acc[...].astype(BF16)
            dwkv_ref[...] = wkv_acc[...].astype(BF16)

    def cols(width, blk):
        return pl.BlockSpec((bm, width), lambda i: (i, blk))

    def whole(shape):
        return pl.BlockSpec(shape, lambda i: (0,) * len(shape))

    tab = cols(128, 0)
    return _call(
        body, name="qkv_prep_bwd", grid=(nsteps,),
        in_specs=[cols(1024, 0), cols(1024, 0), cols(512, 0), cols(Q_LORA, 0), cols(KV_LORA, 2), cols(Q_LORA, 0),
                  cols(KV_LORA, 0), whole(q_gain.shape), whole(kv_gain.shape), whole(wq_t.shape), whole(wkv.shape),
                  tab, tab, tab],
        out_specs=[cols(512, 0), whole(q_gain.shape), whole(kv_gain.shape), whole(wq_t.shape), whole(wkv.shape)],
        out_shape=[jax.ShapeDtypeStruct((s, 512), BF16), jax.ShapeDtypeStruct(q_gain.shape, F32),
                   jax.ShapeDtypeStruct(kv_gain.shape, F32), jax.ShapeDtypeStruct(wq_t.shape, BF16),
                   jax.ShapeDtypeStruct(wkv.shape, BF16)],
        scratch_shapes=[pltpu.VMEM(wq_t.shape, F32), pltpu.VMEM(wkv.shape, F32)],
        sem=("arbitrary",), args=[dq, dk, dv, z, z, qn, kvn, q_gain, kv_gain, wq_t, wkv, tc, ta, tb])


def _causal_mask(s, row0, col0):
    rows = row0 + lax.broadcasted_iota(jnp.int32, s.shape, 0)
    cols = col0 + lax.broadcasted_iota(jnp.int32, s.shape, 1)
    return jnp.where(cols <= rows, s, -jnp.inf)


def _attn_fwd(name, q, k, k_off, v, v_off, nh, dq, dv, scale, causal, blk):
    sq, sk = q.shape[0], k.shape[0]
    bq = min(blk, sq)
    bk = min(blk, sk)
    nkv = sk // bk
    assert not causal or (sq == sk and bq == bk)

    hq = bq
    log2e = 1.4426950408889634
    c2 = scale * log2e

    def body(q_ref, k_ref, v_ref, o_ref, lse_ref):
        qi = pl.program_id(1)
        qs = (q_ref[...],)

        def step(j, carry, masked):
            rows = pl.ds(pl.multiple_of(j * bk, bk), bk)
            kb, vb = k_ref[rows, :], v_ref[rows, :]
            out = []
            for t, (m, l, acc) in enumerate(carry):
                s = _dot(qs[t], kb, NT) * c2
                if masked:
                    s = _causal_mask(s, qi * bq + t * hq, j * bk)
                m_new = jnp.maximum(m, jnp.max(s, axis=-1, keepdims=True))
                alpha = jnp.exp2(m - m_new)
                p = jnp.exp2(s - m_new)
                l = alpha * l + jnp.sum(p, axis=-1, keepdims=True)
                acc = alpha * acc + _dot(p, vb, NN)
                out.append((m_new, l, acc))
            return tuple(out)

        one = (jnp.full((hq, 1), -jnp.inf, F32), jnp.zeros((hq, 1), F32), jnp.zeros((hq, dv), F32))
        init = (one,)
        if causal:
            carry = lax.fori_loop(0, qi, lambda j, c: step(j, c, False), init)
            fin = step(qi, carry, True)
        else:
            fin = lax.fori_loop(0, nkv, lambda j, c: step(j, c, False), init)
        for t, (m, l, acc) in enumerate(fin):
            o_ref[t * hq:(t + 1) * hq, :] = (acc / l).astype(o_ref.dtype)
            lse_ref[t * hq:(t + 1) * hq, :] = m * (1.0 / log2e) + jnp.log(l)

    return _call(
        body, name=name, grid=(nh, sq // bq),
        in_specs=[pl.BlockSpec((bq, dq), lambda h, i: (i, h)),
                  pl.BlockSpec((sk, dq), lambda h, i: (0, k_off + h)),
                  pl.BlockSpec((sk, dv), lambda h, i: (0, v_off + h))],
        out_specs=[pl.BlockSpec((bq, dv), lambda h, i: (i, h)), pl.BlockSpec((None, bq, 1), lambda h, i: (h, i, 0))],
        out_shape=[jax.ShapeDtypeStruct((sq, nh * dv), BF16), jax.ShapeDtypeStruct((nh, sq, 1), F32)],
        sem=("parallel", "parallel"), args=[q, k, v])


def _attn_bwd(name, q, k, k_off, v, v_off, do, do_off, lse, delta, nh, dq, dv, scale, causal, blk):
    sq, sk = q.shape[0], k.shape[0]
    bq = min(blk, sq)
    bk = min(blk, sk)
    nq = sq // bq
    assert not causal or (sq == sk and bq == bk)

    nkv = sk // bk

    def body(q_ref, k_ref, v_ref, do_ref, lse_ref, dl_ref, dq_ref, dk_ref, dv_ref, dq_acc, dk_acc, dv_acc):
        j = pl.program_id(1)

        @pl.when(j == 0)
        def _():
            dq_acc[...] = jnp.zeros_like(dq_acc)

        dk_acc[...] = jnp.zeros_like(dk_acc)
        dv_acc[...] = jnp.zeros_like(dv_acc)
        kv = k_ref[...]
        vv = v_ref[...]

        def step(i, masked):
            rows = pl.ds(pl.multiple_of(i * bq, bq), bq)
            qv = q_ref[rows, :]
            dov = do_ref[rows, :].astype(BF16)
            s = _dot(qv, kv, NT) * scale
            if masked:
                s = _causal_mask(s, i * bq, j * bk)
            p = jnp.exp(s - lse_ref[rows, :])
            dp = _dot(dov, vv, NT)
            ds = (p * (dp - dl_ref[rows, :]) * scale).astype(BF16)
            dv_acc[...] += _dot(p, dov, TN)
            dk_acc[...] += _dot(ds, qv, TN)
            dq_acc[rows, :] += _dot(ds, kv, NN)

        if causal:
            step(j, True)

            def loop(i, c):
                step(i, False)
                return c

            lax.fori_loop(j + 1, nq, loop, 0)
        else:
            def loop(i, c):
                step(i, False)
                return c

            lax.fori_loop(0, nq, loop, 0)
        dk_ref[...] = dk_acc[...].astype(dk_ref.dtype)
        dv_ref[...] = dv_acc[...].astype(dv_ref.dtype)

        @pl.when(j == nkv - 1)
        def _():
            dq_ref[...] = dq_acc[...].astype(dq_ref.dtype)

    stat = pl.BlockSpec((None, sq, 1), lambda h, j: (h, 0, 0))
    return _call(
        body, name=name, grid=(nh, sk // bk),
        in_specs=[pl.BlockSpec((sq, dq), lambda h, j: (0, h)),
                  pl.BlockSpec((bk, dq), lambda h, j: (j, k_off + h)),
                  pl.BlockSpec((bk, dv), lambda h, j: (j, v_off + h)),
                  pl.BlockSpec((sq, dv), lambda h, j: (0, do_off + h)), stat, stat],
        out_specs=[pl.BlockSpec((sq, dq), lambda h, j: (0, h)),
                   pl.BlockSpec((bk, dq), lambda h, j: (j, h)),
                   pl.BlockSpec((bk, dv), lambda h, j: (j, h))],
        out_shape=[jax.ShapeDtypeStruct((sq, nh * dq), BF16), jax.ShapeDtypeStruct((sk, nh * dq), BF16),
                   jax.ShapeDtypeStruct((sk, nh * dv), BF16)],
        scratch_shapes=[pltpu.VMEM((sq, dq), F32), pltpu.VMEM((bk, dq), F32), pltpu.VMEM((bk, dv), F32)],
        sem=("parallel", "arbitrary"), args=[q, k, v, do, lse, delta])


def _pool_diff(z, g):
    s = z.shape[0]
    t = lax.broadcasted_iota(jnp.int32, z.shape, 0)
    acc = z
    sums = []
    for k in (1, 2, 4, 8):
        acc = acc + jnp.where(t >= k, pltpu.roll(acc, k, 0), 0.0)
        sums.append(acc)
    win = jnp.where(g == 0, sums[0], jnp.where(g == 1, sums[1], jnp.where(g == 2, sums[2], sums[3])))
    w = lax.shift_left(jnp.int32(2), g)
    count = jnp.minimum(t + 1, w).astype(F32)
    del s
    return win / count - z, count


def _pool_fwd(z, pool_w, pool_scale):
    s = z.shape[0]

    def body(z_ref, w_ref, sc_ref, o_ref):
        diff, _ = _pool_diff(z_ref[...], pl.program_id(0))
        o_ref[...] = (_dot(diff, w_ref[...], NN) * sc_ref[...]).astype(o_ref.dtype)

    return _call(
        body, name="pool_fwd", grid=(POOL_GROUPS,),
        in_specs=[pl.BlockSpec((s, POOL_CH), lambda g: (0, 4 + g)),
                  pl.BlockSpec((None, POOL_CH, POOL_CH), lambda g: (g, 0, 0)),
                  pl.BlockSpec((1, POOL_CH), lambda g: (0, g))],
        out_specs=[pl.BlockSpec((s, POOL_CH), lambda g: (0, g))],
        out_shape=[jax.ShapeDtypeStruct((s, POOL_GROUPS * POOL_CH), BF16)],
        sem=("parallel",), args=[z, pool_w, pool_scale])[0]


def _pool_bwd(dcat, z, pool_w, pool_scale):
    s = z.shape[0]

    def body(dp_ref, z_ref, w_ref, sc_ref, dz_ref, dw_ref, dsc_ref):
        g = pl.program_id(0)
        diff, count = _pool_diff(z_ref[...], g)
        dpf = dp_ref[...].astype(F32)
        u = _dot(diff, w_ref[...], NN)
        dsc_ref[...] = jnp.sum(dpf * u, axis=0, keepdims=True)
        du = (dpf * sc_ref[...]).astype(BF16)
        dw_ref[...] = _dot(diff, du, TN)
        ddiff = _dot(du, w_ref[...], NT)
        t = lax.broadcasted_iota(jnp.int32, ddiff.shape, 0)
        acc = ddiff / count
        sums = []
        for k in (1, 2, 4, 8):
            acc = acc + jnp.where(t < s - k, pltpu.roll(acc, s - k, 0), 0.0)
            sums.append(acc)
        win = jnp.where(g == 0, sums[0], jnp.where(g == 1, sums[1], jnp.where(g == 2, sums[2], sums[3])))
        dz_ref[...] = (win - ddiff).astype(dz_ref.dtype)

    return pl.pallas_call(
        body, name="pool_bwd", grid=(POOL_GROUPS,),
        in_specs=[pl.BlockSpec((s, POOL_CH), lambda g: (0, 4 + g)),
                  pl.BlockSpec((s, POOL_CH), lambda g: (0, 4 + g)),
                  pl.BlockSpec((None, POOL_CH, POOL_CH), lambda g: (g, 0, 0)),
                  pl.BlockSpec((1, POOL_CH), lambda g: (0, g))],
        out_specs=[pl.BlockSpec((s, POOL_CH), lambda g: (0, g)),
                   pl.BlockSpec((None, POOL_CH, POOL_CH), lambda g: (g, 0, 0)),
                   pl.BlockSpec((1, POOL_CH), lambda g: (0, g))],
        out_shape=[jax.ShapeDtypeStruct((s, POOL_GROUPS * POOL_CH), BF16),
                   jax.ShapeDtypeStruct((POOL_GROUPS, POOL_CH, POOL_CH), F32),
                   jax.ShapeDtypeStruct((1, POOL_GROUPS * POOL_CH), F32)],
        compiler_params=_params("parallel"),
    )(dcat, z, pool_w, pool_scale)


def _local_step(x, mem, positions, target, w, grads):
    tc, ta, tb = _rope_tables(positions)
    blk = _ATT_BLOCK

    if "ffn1_shards" in w:
        n1, a1, dadu1, dadg1, w["ffn1_w_gate"], w["ffn1_w_up"], w["ffn1_w_down"] = _ffn1_up_gather(
            x, w["ffn1_norm"], *w["ffn1_shards"])
    else:
        n1 = _rmsnorm_fwd("ffn1_norm", x, w["ffn1_norm"], D_MODEL)
        a1, dadu1, dadg1 = _ffn_up("ffn1_up", n1, w["ffn1_w_gate"], w["ffn1_w_up"])
    h1, n2 = _ffn_down("ffn1_down", a1, w["ffn1_w_down"], x, w["mix_norm"])
    z = _w_in_fwd(n2, w["w_in"])
    qn, kvn, qf, kf, vf = _qkv_prep(z, w["q_norm"], w["kv_norm"], w["w_q_up"], w["w_kv_up"], tc, ta, tb)
    att, lse = _attn_fwd("mla_fwd", qf, kf, 0, vf, 0, MLA_HEADS, HEAD_QK, HEAD_V, MLA_SCALE, True, blk)
    pool = _pool_fwd(z, w["pool_w"], w["pool_scale"])
    s = x.shape[0]
    bm = _row_block(s)
    row = pl.BlockSpec((bm, D_MODEL), lambda i, k: (i, 0))
    half = pl.BlockSpec((bm, 512), lambda i, k: (i, 0))
    h2, n3 = _matmul(
        "w_out", (s // bm, 1),
        [(att, half, w["w_out"], pl.BlockSpec((512, D_MODEL), lambda i, k: (0, 0)), NN),
         (pool, half, w["w_out"], pl.BlockSpec((512, D_MODEL), lambda i, k: (1, 0)), NN)],
        [(h1, row)] + _residual_outs(s, bm, w["xattn_norm"])[0], _residual_outs(s, bm, w["xattn_norm"])[1],
        _residual_epilogue(1.0, True), None)
    memn = _rmsnorm_fwd("mem_norm", mem, w["mem_norm"], D_MODEL)
    qm = _mm_nn("w_mq", n3, w["w_mq"], BF16)
    kvm = _mm_heads_fwd("w_mkv", memn, w["w_mkv"], BF16)
    om, lse_m = _attn_fwd("xattn_fwd", qm, kvm, 0, kvm, MEM_HEADS, MEM_HEADS, MEM_HEAD_DIM, MEM_HEAD_DIM,
                          MEM_SCALE, False, 2 * blk)
    h3, n4 = _mm_nn("w_mo", om, w["w_mo"], F32, res=h2, gain=w["ffn2_norm"])
    a2, dadu2, dadg2 = _ffn_up("ffn2_up", n4, w["ffn2_w_gate"], w["ffn2_w_up"])
    dh4, dh4b, loss_vec, d_final = _ffn_down("ffn2_down", a2, w["ffn2_w_down"], h3, loss=(w["final_norm"], target))
    grads["final_norm"] = d_final

    dh3, dh3b, grads["ffn2_norm"] = _ffn_bwd("ffn2", dh4b, n4, dadg2, dadu2, a2, w["ffn2_w_gate"], w["ffn2_w_up"],
                                             w["ffn2_w_down"], grads, norm_bwd=(h3, w["ffn2_norm"], dh4),
                                             many_calls=False)

    dom, delta_m = _mm_nt("w_mo_dx", dh3b, w["w_mo"], BF16, attn_out=om, nh=MEM_HEADS, dv=MEM_HEAD_DIM)
    dqm, dkm, dvm = _attn_bwd("xattn_bwd", qm, kvm, 0, kvm, MEM_HEADS, dom, 0, lse_m, delta_m, MEM_HEADS,
                              MEM_HEAD_DIM, MEM_HEAD_DIM, MEM_SCALE, False, 8 * blk)
    dkvm = jnp.concatenate([dkm, dvm], axis=1)
    dh2, dh2b, grads["xattn_norm"] = _mm_nt_norm_bwd("w_mq_dx", dqm, w["w_mq"], h2, w["xattn_norm"], dh3)
    grads["w_mo"], grads["w_mq"] = _mm_tn_pairs("w_mo_mq_dw", [(om, dh3b), (n3, dqm)])
    dmemn, grads["w_mkv"] = _mm_heads_bwd("w_mkv", dkvm, memn, w["w_mkv"])
    _, grads["mem_norm"] = _rmsnorm_bwd("mem_norm_bwd", dmemn, mem, w["mem_norm"], D_MODEL, out_dtype=BF16)

    dcat, delta = _mm_nt("w_out_dx", dh2b, w["w_out"], BF16, attn_out=att, nh=MLA_HEADS, dv=HEAD_V)
    grads["w_out"] = _mm_tn_stacked("w_out_dw", [att, pool], dh2b)
    dzp, grads["pool_w"], grads["pool_scale"] = _pool_bwd(dcat, z, w["pool_w"], w["pool_scale"])
    dqf, dkf, dvf = _attn_bwd("mla_bwd", qf, kf, 0, vf, 0, dcat, 0, lse, delta, MLA_HEADS, HEAD_QK, HEAD_V,
                              MLA_SCALE, True, blk)
    dz_lat, grads["q_norm"], grads["kv_norm"], grads["w_q_up"], grads["w_kv_up"] = _qkv_prep_bwd(
        dqf, dkf, dvf, z, qn, kvn, w["q_norm"], w["kv_norm"], w["w_q_up"], w["w_kv_up"], tc, ta, tb)
    dz = jnp.concatenate([dz_lat, dzp], axis=1)
    grads["w_in"] = _w_in_dw(dz, n2)
    dh1, dh1b, grads["mix_norm"] = _w_in_dx_norm_bwd(dz, w["w_in"], h1, w["mix_norm"], dh2)

    dn1 = _ffn_bwd("ffn1", dh1b, n1, dadg1, dadu1, a1, w["ffn1_w_gate"], w["ffn1_w_up"], w["ffn1_w_down"], grads)
    dx, grads["ffn1_norm"] = _rmsnorm_bwd("ffn1_norm_bwd", dn1, x, w["ffn1_norm"], D_MODEL, dres=dh1)
    return loss_vec[0, 0], dx


def _mesh_pos():
    x, y, c = lax.axis_index("x"), lax.axis_index("y"), lax.axis_index("c")
    chips = [(1 - x, y), (x, 1 - y), (1 - x, 1 - y)]
    chip_ids = [2 * cx + cy for cx, cy in chips]
    return x, y, c, 2 * x + y, chips, chip_ids


def _half_rows(c, rows):
    hr = rows // 2
    return pl.ds(pl.multiple_of(c * hr, 16), hr), pl.ds(pl.multiple_of((1 - c) * hr, 16), hr)


def _ag_ici_stage(shards):
    n = len(shards)

    def copies(ins, outs):
        x, y, c, me, chips, _ = _mesh_pos()
        out = []
        for k in range(n):
            mine, _ = _half_rows(c, ins[k].shape[0])
            out.append((ins[k], outs[k].at[me], None))
            for cx, cy in chips:
                out.append((ins[k].at[mine], outs[k].at[me, mine], (cx, cy, c)))
        return out

    return _Stage(shards, [jax.ShapeDtypeStruct((N_CHIPS,) + s.shape, s.dtype) for s in shards], 3 * n, n, copies)


def _quarter_rows(c, rows):
    qr = rows // 4
    return pl.ds(pl.multiple_of(c * 2 * qr, 16), qr), pl.ds(pl.multiple_of(c * 2 * qr + qr, 16), qr)


def _ag_d2d_stage(fulls):
    n = len(fulls)

    def copies(ins, outs):
        x, y, c, me, _, chip_ids = _mesh_pos()
        out = []
        for k in range(n):
            mine, _ = _half_rows(c, ins[k].shape[1])
            for j in range(3):
                out.append((ins[k].at[chip_ids[j], mine], outs[k].at[chip_ids[j], mine], (x, y, 1 - c)))
        return out

    return _Stage(fulls, [jax.ShapeDtypeStruct(f.shape, f.dtype) for f in fulls], 3 * n, 0, copies,
                  aliases={k: k for k in range(n)})


def _rs_swap_stage(grads):
    n = len(grads)

    def copies(ins, outs):
        x, y, c, _, _, _ = _mesh_pos()
        out = []
        for k in range(n):
            _, other = _half_rows(c, ins[k].shape[1])
            out.append((ins[k].at[:, other, :], outs[k], (x, y, 1 - c)))
        return out

    return _Stage(grads, [jax.ShapeDtypeStruct((N_CHIPS, g.shape[1] // 2, g.shape[2]), g.dtype) for g in grads],
                  n, 0, copies)


_REL_OF_PEER = (2, 1, 3)


def _rs_scatter_stage(sums, relative=False):
    n = len(sums)

    def copies(ins, outs):
        x, y, c, me, chips, chip_ids = _mesh_pos()
        out = []
        for k in range(n):
            mine, _ = _half_rows(c, 2 * ins[k].shape[1])
            out.append((ins[k].at[0 if relative else me], outs[k].at[0, mine, :], None))
            for j, (cx, cy) in enumerate(chips):
                slab = _REL_OF_PEER[j] if relative else chip_ids[j]
                out.append((ins[k].at[slab], outs[k].at[1 + j, mine, :], (cx, cy, c)))
        return out

    return _Stage(sums, [jax.ShapeDtypeStruct((N_CHIPS, 2 * s.shape[1], s.shape[2]), s.dtype) for s in sums],
                  3 * n, n, copies)


def _rs_mirror_stage(parts):
    n = len(parts)

    def copies(ins, outs):
        x, y, c, _, _, _ = _mesh_pos()
        out = []
        for k in range(n):
            mine, _ = _half_rows(c, ins[k].shape[1])
            out.append((ins[k].at[:, mine, :], outs[k].at[:, mine, :], (x, y, 1 - c)))
        return out

    return _Stage(parts, [jax.ShapeDtypeStruct(p.shape, p.dtype) for p in parts], n, 0, copies,
                  aliases={k: k for k in range(n)})


def _pair_add(name, gs, r1s, core):
    n = len(gs)

    def body(c_ref, *refs):
        for k in range(n):
            g_ref, r_ref, o_ref = refs[k], refs[n + k], refs[2 * n + k]
            o_ref[...] = (g_ref[...].astype(F32) + r_ref[...].astype(F32)).astype(BF16)

    def half(g):
        return pl.BlockSpec((None, g.shape[1] // 2, g.shape[2]), lambda j, c: (j, 0, 0))

    def mine(g):
        return pl.BlockSpec((None, g.shape[1] // 2, g.shape[2]), lambda j, c: (j, c[0], 0))

    return pl.pallas_call(
        body, name=name,
        grid_spec=pltpu.PrefetchScalarGridSpec(
            num_scalar_prefetch=1, grid=(N_CHIPS,),
            in_specs=[mine(g) for g in gs] + [half(g) for g in gs], out_specs=[half(g) for g in gs]),
        out_shape=[jax.ShapeDtypeStruct((N_CHIPS, g.shape[1] // 2, g.shape[2]), BF16) for g in gs],
        compiler_params=_params("parallel"),
    )(core, *gs, *r1s)


def _adamw_math(w, g, m, v):
    m = ADAM_B1 * m + (1.0 - ADAM_B1) * g
    v = ADAM_B2 * v + (1.0 - ADAM_B2) * (g * g)
    m_hat = m / (1.0 - ADAM_B1 ** ADAM_STEP)
    v_hat = v / (1.0 - ADAM_B2 ** ADAM_STEP)
    delta = -ADAM_LR * (m_hat / (jnp.sqrt(v_hat) + ADAM_EPS) + ADAM_WD * w)
    return delta, m, v


def _adamw_sum(name, ws, parts, ms, vs):
    n = len(ws)
    r, c = ws[0].shape
    assert all(w.shape == (r, c) for w in ws)
    br = r
    while br * c * 4 > (1 << 20) and br % 32 == 0:
        br //= 2

    def body(*refs):
        for k in range(n):
            w_ref, p_ref, m_ref, v_ref = refs[4 * k:4 * k + 4]
            g_ref, d_ref, nm_ref, nv_ref = refs[4 * n + 4 * k:4 * n + 4 * k + 4]
            g = p_ref[0].astype(F32)
            for j in range(1, N_CHIPS):
                g = g + p_ref[j].astype(F32)
            d, nm, nv = _adamw_math(w_ref[...], g, m_ref[...], v_ref[...])
            g_ref[...] = g
            d_ref[...] = d
            nm_ref[...] = nm
            nv_ref[...] = nv

    spec = pl.BlockSpec((br, c), lambda i: (i, 0))
    shp = jax.ShapeDtypeStruct((r, c), F32)
    args = [a for k in range(n) for a in (ws[k], parts[k], ms[k], vs[k])]
    res = _call(
        body, name=name, grid=(r // br,),
        in_specs=[spec, pl.BlockSpec((N_CHIPS, br, c), lambda i: (0, i, 0)), spec, spec] * n,
        out_specs=[spec] * (4 * n), out_shape=[shp] * (4 * n), sem=("parallel",), args=args)
    return [res[4 * k:4 * k + 4] for k in range(n)]


_SMALL_VECTORS = ("ffn1_norm", "mix_norm", "xattn_norm", "mem_norm", "ffn2_norm", "final_norm", "q_norm",
                  "kv_norm", "pool_scale")


_LOSS_ROW = 9
_VEC_ROWS = 16
_POOL_ROWS = POOL_GROUPS * POOL_CH


def _small_params_step(g, w, m, v, loss_local):
    names = list(_SMALL_VECTORS) + ["pool_w"]
    nv = len(_SMALL_VECTORS)
    widths = [g[n].shape[1] for n in _SMALL_VECTORS]
    shapes = {"vec": (_VEC_ROWS, D_MODEL), "pool": (_POOL_ROWS, POOL_CH)}

    def body(*refs):
        ins = refs[:4 * (nv + 1) + 1]
        outs = refs[len(ins):len(ins) + 4 * (nv + 1) + 1]
        vec_own, vec_sib, vec_all, pool_sib, pool_sum, pool_all, send, recv = refs[len(ins) + len(outs):]
        g_in, w_in, m_in, v_in = (ins[k * (nv + 1):(k + 1) * (nv + 1)] for k in range(4))
        loss_in = ins[-1]
        g_out, d_out, m_out, v_out = (outs[k * (nv + 1):(k + 1) * (nv + 1)] for k in range(4))
        loss_out = outs[-1]
        x, y, c, me, chips, chip_ids = _mesh_pos()
        sib = (x, y, 1 - c)

        def remote(src, dst, k, dev):
            return pltpu.make_async_remote_copy(src_ref=src, dst_ref=dst, send_sem=send.at[k], recv_sem=recv.at[k],
                                                device_id=dev, device_id_type=_MESH)

        vec_own[...] = jnp.zeros_like(vec_own)
        for i in range(nv):
            vec_own[i:i + 1, 0:widths[i]] = g_in[i][...]
        vec_own[_LOSS_ROW:_LOSS_ROW + 1, 0:128] = loss_in[...]
        swaps = [remote(vec_own, vec_sib, 0, sib), remote(g_in[nv], pool_sib, 1, sib)]
        for cp in swaps:
            cp.start()
        for cp in swaps:
            cp.wait()
        vec_all[me] = vec_own[...] + vec_sib[...]
        pool_sum[...] = g_in[nv][...] + pool_sib[...]
        pool_all[me] = pool_sum[...]
        hv, hp = _VEC_ROWS // 2, _POOL_ROWS // 2
        mine_v = pl.ds(pl.multiple_of(c * hv, 8), hv)
        mine_p = pl.ds(pl.multiple_of(c * hp, 8), hp)
        sends = []
        for j, (cx, cy) in enumerate(chips):
            sends.append(remote(vec_all.at[me, mine_v], vec_all.at[me, mine_v], 2 + j, (cx, cy, c)))
            sends.append(remote(pool_sum.at[mine_p], pool_all.at[me, mine_p], 5 + j, (cx, cy, c)))
        for cp in sends:
            cp.start()
        for cp in sends:
            cp.wait()
        mirrors = []
        for j in range(3):
            mirrors.append(remote(vec_all.at[chip_ids[j], mine_v], vec_all.at[chip_ids[j], mine_v], 8 + j, sib))
            mirrors.append(remote(pool_all.at[chip_ids[j], mine_p], pool_all.at[chip_ids[j], mine_p], 11 + j, sib))
        for cp in mirrors:
            cp.start()
        for cp in mirrors:
            cp.wait()
        vec_tot = vec_all[0]
        pool_tot = pool_all[0]
        for i in range(1, N_CHIPS):
            vec_tot = vec_tot + vec_all[i]
            pool_tot = pool_tot + pool_all[i]
        vec_sib[...] = vec_tot
        loss_out[...] = vec_sib[_LOSS_ROW:_LOSS_ROW + 1, 0:128]
        for i in range(nv + 1):
            gi = pool_tot if i == nv else vec_sib[i:i + 1, 0:widths[i]]
            d, nm, nvv = _adamw_math(w_in[i][...], gi, m_in[i][...], v_in[i][...])
            g_out[i][...] = gi
            d_out[i][...] = d
            m_out[i][...] = nm
            v_out[i][...] = nvv

    vm = pl.BlockSpec(memory_space=pltpu.VMEM)
    args = [d[n] for d in (g, w, m, v) for n in names] + [jnp.broadcast_to(loss_local.reshape(1, 1), (1, 128))]
    out_shape = [jax.ShapeDtypeStruct(g[n].shape, F32) for _ in range(4) for n in names]
    out_shape.append(jax.ShapeDtypeStruct((1, 128), F32))
    res = pl.pallas_call(
        body, name="small_params_step", in_specs=[vm] * len(args), out_specs=[vm] * len(out_shape),
        out_shape=out_shape,
        scratch_shapes=[pltpu.VMEM(shapes["vec"], F32), pltpu.VMEM(shapes["vec"], F32),
                        pltpu.VMEM((N_CHIPS,) + shapes["vec"], F32), pltpu.VMEM(shapes["pool"], F32),
                        pltpu.VMEM(shapes["pool"], F32), pltpu.VMEM((N_CHIPS,) + shapes["pool"], F32),
                        pltpu.SemaphoreType.DMA((14,)), pltpu.SemaphoreType.DMA((14,))],
        compiler_params=pltpu.CompilerParams(vmem_limit_bytes=V7X_VMEM_LIMIT_BYTES),
    )(*args)
    k = len(names)
    dicts = [dict(zip(names, res[i * k:(i + 1) * k])) for i in range(4)]
    return dicts[0], dicts[1], dicts[2], dicts[3], res[-1]


_WEIGHTS = ("ffn1_norm", "ffn1_w_gate", "ffn1_w_up", "ffn1_w_down", "mix_norm", "w_in", "q_norm", "w_q_up",
            "kv_norm", "w_kv_up", "pool_w", "pool_scale", "w_out", "xattn_norm", "mem_norm", "w_mq", "w_mkv",
            "w_mo", "ffn2_norm", "ffn2_w_gate", "ffn2_w_up", "ffn2_w_down", "final_norm")
_SHARDED = ("ffn1_w_gate", "ffn1_w_up", "ffn1_w_down", "w_in", "w_q_up", "w_kv_up", "w_out", "w_mq", "w_mkv",
            "w_mo", "ffn2_w_gate", "ffn2_w_up", "ffn2_w_down")
W_IN_SPLIT = Q_LORA + KV_LORA + ROPE_DIM


_FFN1 = ("ffn1_w_gate", "ffn1_w_up", "ffn1_w_down")
_TRANSPOSED = ("ffn1_w_gate", "ffn1_w_up", "ffn2_w_gate", "ffn2_w_up", "w_in", "w_q_up")


def _local_view(name, a):
    return jnp.swapaxes(a, 1, 2)[0] if name in _TRANSPOSED else a[0]


def _global_view(name, a):
    return jnp.swapaxes(a[None], 1, 2) if name in _TRANSPOSED else a[None]


def _pad_shard(name, a):
    if name == "w_in":
        return jnp.concatenate([a[:W_IN_SPLIT], jnp.zeros((64, a.shape[1]), a.dtype), a[W_IN_SPLIT:]], axis=0)
    if name == "w_q_up":
        return jnp.pad(a, ((0, 64), (0, 0)))
    return a


def _unpad_shard(name, a):
    if name == "w_in":
        return jnp.concatenate([a[:, :W_IN_SPLIT], a[:, W_IN_SPLIT + 64:]], axis=1)
    if name == "w_q_up":
        return a[:, :192]
    return a


def _stacked(g):
    return g if g.ndim == 3 else g.reshape(N_CHIPS, g.shape[0] // N_CHIPS, g.shape[1])


class _Plan:
    AG_UNITS = (
        (("w_in", "w_q_up", "w_kv_up"), "ffn1_up"),
        (("w_out",), "w_in"),
        (("w_mq",), "qkv_prep"),
        (("w_mkv", "w_mo", "ffn2_w_gate"), "mla_fwd"),
        (("ffn2_w_up",), "xattn_fwd"),
        (("ffn2_w_down",), "ffn2_up"),
    )
    RS_UNITS = (
        (("ffn2_w_gate", "ffn2_w_up", "ffn2_w_down"), "ffn2_dn", "mla_bwd", "qkv_prep_bwd"),
        (("w_mo", "w_mq", "w_mkv"), "w_out_dx", "mla_bwd", "qkv_prep_bwd"),
        (("w_out", "w_q_up", "w_kv_up", "w_in"), "w_in_dx", "ffn1_dact", "ffn1_dwd"),
        (("ffn1_w_down",), "ffn1_dwg", "ffn1_dwu", "ffn1_dn_a"),
        (("ffn1_w_gate",), "ffn1_dwu", "ffn1_dn_a", "ffn1_dn_b"),
        (("ffn1_w_up",), "ffn1_dn_a", "ffn1_dn_b", "adamw_w_kv_up"),
    )
    ADAMW_ORDER = (("w_kv_up",), ("ffn2_w_gate", "ffn2_w_up"), ("ffn2_w_down", "ffn1_w_down"), ("w_mo", "w_mq", "w_out"),
                   ("w_mkv",), ("w_q_up",), ("w_in",), ("ffn1_w_gate", "ffn1_w_up"))

    def __init__(self, shards, w, grads, core):
        self.shards, self.w, self.grads, self.core = shards, w, grads, core
        self.last_slab_step = 0
        self.parts = {}
        self.ag = [None for _ in self.AG_UNITS]
        self.rs = [[None, None, None, None] for _ in self.RS_UNITS]

    def pre(self, name):
        for i, (names, host) in enumerate(self.AG_UNITS):
            if name == host:
                st = _ag_ici_stage([self.shards[n] for n in names])
                st.then = _ag_d2d_stage(st.outs)
                st.start_step = self.last_slab_step if name == "ffn1_up" else 0
                self.ag[i] = _host(name, st)
        for i, (names, h1, h2, h3) in enumerate(self.RS_UNITS):
            if name == h1:
                self.rs[i][0] = _host(name, _rs_swap_stage([_stacked(self.grads[n]) for n in names]))
            if name == h2:
                self.rs[i][2] = _host(name, _rs_scatter_stage(self.rs[i][1], relative=names[0] in _FFN1))
            if name == h3:
                self.rs[i][3] = _host(name, _rs_mirror_stage(self.rs[i][2].results))

    def post(self, name):
        for i, (names, host) in enumerate(self.AG_UNITS):
            if name == host:
                for n, f in zip(names, self.ag[i].results):
                    self.w[n] = _full_weight(n, f)
        for i, (names, h1, h2, h3) in enumerate(self.RS_UNITS):
            if name == h1:
                self.rs[i][1] = list(_pair_add("pair_add_" + names[0], [_stacked(self.grads[n]) for n in names],
                                               self.rs[i][0].results, self.core))
            if name == h3:
                for n, p in zip(names, self.rs[i][3].results):
                    self.parts[n] = p


def _full_weight(name, stacked):
    if name in ("w_out", "w_mq", "w_mo"):
        return stacked.reshape(D_MODEL, D_MODEL)
    return stacked


def kernel(x, mem, positions, ffn1_norm, ffn1_w_gate, ffn1_w_up, ffn1_w_down, mix_norm, w_in, q_norm, w_q_up, kv_norm, w_kv_up, pool_w, pool_scale, w_out, xattn_norm, mem_norm, w_mq, w_mkv, w_mo, ffn2_norm, ffn2_w_gate, ffn2_w_up, ffn2_w_down, final_norm, loss_target, m_ffn1_norm, m_ffn1_w_gate, m_ffn1_w_up, m_ffn1_w_down, m_mix_norm, m_w_in, m_q_norm, m_w_q_up, m_kv_norm, m_w_kv_up, m_pool_w, m_pool_scale, m_w_out, m_xattn_norm, m_mem_norm, m_w_mq, m_w_mkv, m_w_mo, m_ffn2_norm, m_ffn2_w_gate, m_ffn2_w_up, m_ffn2_w_down, m_final_norm, v_ffn1_norm, v_ffn1_w_gate, v_ffn1_w_up, v_ffn1_w_down, v_mix_norm, v_w_in, v_q_norm, v_w_q_up, v_kv_norm, v_w_kv_up, v_pool_w, v_pool_scale, v_w_out, v_xattn_norm, v_mem_norm, v_w_mq, v_w_mkv, v_w_mo, v_ffn2_norm, v_ffn2_w_gate, v_ffn2_w_up, v_ffn2_w_down, v_final_norm):
    wts = dict(zip(_WEIGHTS, (ffn1_norm, ffn1_w_gate, ffn1_w_up, ffn1_w_down, mix_norm, w_in, q_norm, w_q_up, kv_norm, w_kv_up, pool_w, pool_scale, w_out, xattn_norm, mem_norm, w_mq, w_mkv, w_mo, ffn2_norm, ffn2_w_gate, ffn2_w_up, ffn2_w_down, final_norm)))
    mom = dict(zip(_WEIGHTS, (m_ffn1_norm, m_ffn1_w_gate, m_ffn1_w_up, m_ffn1_w_down, m_mix_norm, m_w_in, m_q_norm, m_w_q_up, m_kv_norm, m_w_kv_up, m_pool_w, m_pool_scale, m_w_out, m_xattn_norm, m_mem_norm, m_w_mq, m_w_mkv, m_w_mo, m_ffn2_norm, m_ffn2_w_gate, m_ffn2_w_up, m_ffn2_w_down, m_final_norm)))
    var = dict(zip(_WEIGHTS, (v_ffn1_norm, v_ffn1_w_gate, v_ffn1_w_up, v_ffn1_w_down, v_mix_norm, v_w_in, v_q_norm, v_w_q_up, v_kv_norm, v_w_kv_up, v_pool_w, v_pool_scale, v_w_out, v_xattn_norm, v_mem_norm, v_w_mq, v_w_mkv, v_w_mo, v_ffn2_norm, v_ffn2_w_gate, v_ffn2_w_up, v_ffn2_w_down, v_final_norm)))
    small = [n for n in _WEIGHTS if n not in _SHARDED]

    global _PLAN
    shards = {n: _pad_shard(n, _local_view(n, wts[n])).astype(BF16) for n in _SHARDED}
    w = {n: wts[n].reshape(1, -1) for n in _SMALL_VECTORS}
    w["pool_w"] = pool_w[0].astype(BF16)
    grads = {}
    core = lax.axis_index("c").astype(jnp.int32).reshape(1)
    plan = _Plan(shards, w, grads, core)
    _PLAN = plan
    try:
        w["ffn1_shards"] = tuple(shards[n] for n in _FFN1)

        loss_local, dx = _local_step(x[0], mem[0], positions[0], loss_target[0], w, grads)

        def small_view(d):
            out = {n: d[n].reshape(1, -1) for n in _SMALL_VECTORS}
            out["pool_w"] = d["pool_w"].reshape(_POOL_ROWS, POOL_CH)
            return out

        *small_res, loss_vec = _small_params_step(small_view(grads), small_view(wts), small_view(mom),
                                                  small_view(var), loss_local)
        g_out, d_out, m_out, v_out = ({n: r[n].reshape(wts[n].shape) for n in small} for r in small_res)
        loss = loss_vec[0, 0]

        for names in _Plan.ADAMW_ORDER:
            res = _adamw_sum("adamw_" + names[0], [_local_view(n, wts[n]) for n in names],
                             [_unpad_shard(n, plan.parts[n]) for n in names],
                             [_local_view(n, mom[n]) for n in names], [_local_view(n, var[n]) for n in names])
            for n, r4 in zip(names, res):
                g_out[n], d_out[n], m_out[n], v_out[n] = (_global_view(n, r) for r in r4)
    finally:
        _PLAN = None
        _PENDING.clear()

    return (loss, dx[None], *[g_out[n] for n in _WEIGHTS], *[d_out[n] for n in _WEIGHTS],
            *[m_out[n] for n in _WEIGHTS], *[v_out[n] for n in _WEIGHTS])
```

```python
import jax
import jax.numpy as jnp
from jax import lax
from jax.experimental import pallas as pl
from jax.experimental.pallas import tpu as pltpu

F32 = jnp.float32
BF16 = jnp.bfloat16

D_MODEL = 1024
D_FF = 2816
N_CHIPS = 4
FF_SHARD = D_FF // N_CHIPS
MLA_HEADS = 4
Q_LORA = 256
KV_LORA = 128
ROPE_DIM = 64
HEAD_QK = 256
HEAD_V = 128
POOL_GROUPS = 4
POOL_CH = 128
MEM_HEADS = 4
MEM_HEAD_DIM = 256
RMS_EPS = 1e-6
ROPE_BASE = 10000.0
MLA_SCALE = (128 + 64) ** -0.5
MEM_SCALE = MEM_HEAD_DIM ** -0.5

ADAM_LR = 0.001
ADAM_B1 = 0.9
ADAM_B2 = 0.999
ADAM_EPS = 1e-08
ADAM_WD = 0.01
ADAM_STEP = 10

V7X_VMEM_LIMIT_BYTES = 56 * 1024 * 1024

NN = ((1,), (0,))
NT = ((1,), (1,))
TN = ((0,), (0,))


def _params(*sem):
    return pltpu.CompilerParams(dimension_semantics=sem, vmem_limit_bytes=V7X_VMEM_LIMIT_BYTES)


_MESH = pl.DeviceIdType.MESH
_ANY = pl.BlockSpec(memory_space=pl.ANY)


class _Stage:
    def __init__(self, ins, outs, n_remote, n_local, copies, aliases=None):
        self.ins, self.outs, self.n_remote, self.n_local = list(ins), list(outs), n_remote, n_local
        self.copies, self.aliases = copies, dict(aliases or {})
        self.results = None
        self.start_step = 0
        self.then = None

    def descriptors(self, in_refs, out_refs, send, recv, loc):
        ds, ri, li = [], 0, 0
        for src, dst, dev in self.copies(in_refs, out_refs):
            if dev is None:
                ds.append(pltpu.make_async_copy(src, dst, loc.at[li]))
                li += 1
            else:
                ds.append(pltpu.make_async_remote_copy(src_ref=src, dst_ref=dst, send_sem=send.at[ri],
                                                       recv_sem=recv.at[ri], device_id=dev, device_id_type=_MESH))
                ri += 1
        assert ri == self.n_remote and li == self.n_local
        return ds


_PENDING = {}


def _host(name, stage):
    _PENDING.setdefault(name, []).append(stage)
    return stage


_PLAN = None


def _call(body, **kw):
    if _PLAN is not None:
        _PLAN.pre(kw["name"])
    res = _call_hosting(body, **kw)
    if _PLAN is not None:
        _PLAN.post(kw["name"])
    return res


def _call_hosting(body, *, name, grid, in_specs, out_specs, out_shape, sem, args, scratch_shapes=(), aliases=None):
    stages = _PENDING.pop(name, [])
    scratch_shapes = list(scratch_shapes)
    if not stages:
        return pl.pallas_call(body, name=name, grid=grid, in_specs=in_specs, out_specs=out_specs,
                              out_shape=out_shape, scratch_shapes=scratch_shapes,
                              input_output_aliases=dict(aliases or {}), compiler_params=_params(*sem))(*args)
    ni, no, ns = len(in_specs), len(out_shape), len(scratch_shapes)
    c_ins = [a for st in stages for a in st.ins]
    c_outs = [o for st in stages for o in st.outs]
    nci, nco = len(c_ins), len(c_outs)
    aliases, io, oo = dict(aliases or {}), 0, 0
    for st in stages:
        for i, j in st.aliases.items():
            aliases[ni + io + i] = no + oo + j
        io += len(st.ins)
        oo += len(st.outs)
    dma = pltpu.SemaphoreType.DMA
    sems = []
    for st in stages:
        sems += [dma((max(st.n_remote, 1),)), dma((max(st.n_remote, 1),)), dma((max(st.n_local, 1),))]
    followers = [st.then for st in stages if st.then is not None]
    for st in followers:
        sems += [dma((max(st.n_remote, 1),)), dma((max(st.n_remote, 1),)), dma((max(st.n_local, 1),))]

    def wrapped(*refs):
        ins, cin = refs[:ni], refs[ni:ni + nci]
        outs, cout = refs[ni + nci:ni + nci + no], refs[ni + nci + no:ni + nci + no + nco]
        scr = refs[ni + nci + no + nco:ni + nci + no + nco + ns]
        sem_refs = refs[ni + nci + no + nco + ns:]
        step = pl.program_id(0)
        last = pl.program_id(0) == grid[0] - 1
        for ax in range(1, len(grid)):
            step = step * grid[ax] + pl.program_id(ax)
            last = jnp.logical_and(last, pl.program_id(ax) == grid[ax] - 1)

        def descriptors(si):
            io = sum(len(st.ins) for st in stages[:si])
            oo = sum(len(st.outs) for st in stages[:si])
            st = stages[si]
            return st.descriptors(cin[io:io + len(st.ins)], cout[oo:oo + len(st.outs)], *sem_refs[3 * si:3 * si + 3])

        def follower_descriptors(fi):
            si = [k for k, st in enumerate(stages) if st.then is not None][fi]
            oo = sum(len(st.outs) for st in stages[:si])
            bufs = cout[oo:oo + len(stages[si].outs)]
            k0 = 3 * (len(stages) + fi)
            return followers[fi].descriptors(bufs, bufs, *sem_refs[k0:k0 + 3])

        def start(si):
            @pl.when(step == stages[si].start_step)
            def _():
                for d in descriptors(si):
                    d.start(priority=0 if d.is_remote else 1)

        for si, st in enumerate(stages):
            if st.start_step == 0:
                start(si)
        body(*ins, *outs, *scr)
        for si, st in enumerate(stages):
            if st.start_step != 0:
                start(si)

        @pl.when(last)
        def _():
            for si in range(len(stages)):
                for d in descriptors(si):
                    d.wait()
            for fi in range(len(followers)):
                for d in follower_descriptors(fi):
                    d.start()
            for fi in range(len(followers)):
                for d in follower_descriptors(fi):
                    d.wait()

    res = pl.pallas_call(
        wrapped, name=name, grid=grid, in_specs=list(in_specs) + [_ANY] * nci,
        out_specs=list(out_specs) + [_ANY] * nco, out_shape=list(out_shape) + c_outs,
        scratch_shapes=scratch_shapes + sems, input_output_aliases=aliases,
        compiler_params=_params(*(("arbitrary",) * len(grid))))(*args, *c_ins)
    oo = no
    for st in stages:
        st.results = list(res[oo:oo + len(st.outs)])
        oo += len(st.outs)
    return list(res[:no])


def _dot(a, b, dims):
    return lax.dot_general(a.astype(BF16), b.astype(BF16), (dims, ((), ())), preferred_element_type=F32)


_MAX_ROW_BLOCK = 1024
_ATT_BLOCK = 512


_MAX_REDUCE_BLOCK = 2048


def _row_block(s, want=1024):
    return min(want, s, _MAX_ROW_BLOCK)


def _reduce_block(s):
    return min(s, _MAX_REDUCE_BLOCK)


def _matmul(name, grid, terms, extras, outs, epilogue, acc_shape, fill=(), summed=()):
    nt, ne, no, nf = len(terms), len(extras), len(outs), len(fill)
    nk = grid[-1]
    dims = [t[4] for t in terms]

    def body(*refs):
        a_refs, b_refs = refs[:nt], refs[nt:2 * nt]
        e_refs = refs[2 * nt:2 * nt + ne]
        o_refs = refs[2 * nt + ne + nf:2 * nt + ne + nf + no]

        def finish(acc):
            vals = epilogue(acc, *[e[...] for e in e_refs])
            for idx, (o, val) in enumerate(zip(o_refs, vals)):
                if idx in summed:
                    @pl.when(pl.program_id(0) == 0)
                    def _(o=o, val=val):
                        o[...] = val.astype(o.dtype)

                    @pl.when(pl.program_id(0) > 0)
                    def _(o=o, val=val):
                        o[...] += val.astype(o.dtype)
                else:
                    o[...] = val.astype(o.dtype)

        if nk == 1:
            part = None
            for a, b, d in zip(a_refs, b_refs, dims):
                t = _dot(a[...], b[...], d)
                part = t if part is None else part + t
            finish(part)
        else:
            acc_ref = refs[-1]
            k = pl.program_id(len(grid) - 1)

            @pl.when(k == 0)
            def _():
                acc_ref[...] = jnp.zeros_like(acc_ref)

            for a, b, d in zip(a_refs, b_refs, dims):
                acc_ref[...] += _dot(a[...], b[...], d)

            @pl.when(k == nk - 1)
            def _():
                finish(acc_ref[...])

    in_specs = [t[1] for t in terms] + [t[3] for t in terms] + [e[1] for e in extras] + [_ANY] * nf
    args = [t[0] for t in terms] + [t[2] for t in terms] + [e[0] for e in extras] + list(fill)
    sem = ("arbitrary" if summed else "parallel",) * (len(grid) - 1) + ("arbitrary",)
    aliases = {2 * nt + ne + i: i for i in range(nf)}
    return _call(
        body, name=name, grid=grid, in_specs=in_specs,
        out_specs=[o[1] for o in outs], out_shape=[o[0] for o in outs],
        scratch_shapes=[pltpu.VMEM(acc_shape, F32)] if nk > 1 else [], sem=sem, args=args, aliases=aliases)


def _ident(acc):
    return (acc,)


def _rmsnorm_fwd(name, x, gain, width, col_block=0):
    s = x.shape[0]
    bm = _row_block(s)

    def body(x_ref, g_ref, o_ref):
        xf = x_ref[...]
        r = lax.rsqrt(jnp.mean(xf * xf, axis=-1, keepdims=True) + RMS_EPS)
        o_ref[...] = ((xf * r) * g_ref[...]).astype(o_ref.dtype)

    return pl.pallas_call(
        body, name=name, grid=(s // bm,),
        in_specs=[pl.BlockSpec((bm, width), lambda i: (i, col_block)), pl.BlockSpec((1, width), lambda i: (0, 0))],
        out_specs=pl.BlockSpec((bm, width), lambda i: (i, 0)),
        out_shape=jax.ShapeDtypeStruct((s, width), BF16),
        compiler_params=_params("parallel"),
    )(x, gain)


def _rms_bwd_math(dy, xf, g, width):
    r = lax.rsqrt(jnp.mean(xf * xf, axis=-1, keepdims=True) + RMS_EPS)
    dyg = dy * g
    dot = jnp.sum(dyg * xf, axis=-1, keepdims=True)
    dx = r * dyg - xf * ((r * r * r) * (dot * (1.0 / width)))
    dgain = jnp.sum(dy * (xf * r), axis=0, keepdims=True)
    return dx, dgain


def _rmsnorm_bwd(name, dy, x, gain, width, col_block=0, dres=None, out_dtype=F32):
    s = x.shape[0]
    bm = _row_block(s)
    has_res = dres is not None

    def body(*refs):
        if has_res:
            dy_ref, x_ref, g_ref, r_ref, dx_ref, dg_ref = refs
        else:
            dy_ref, x_ref, g_ref, dx_ref, dg_ref = refs
        dx, dgain = _rms_bwd_math(dy_ref[...].astype(F32), x_ref[...], g_ref[...], width)
        if has_res:
            dx = dx + r_ref[...]
        dx_ref[...] = dx.astype(dx_ref.dtype)

        @pl.when(pl.program_id(0) == 0)
        def _():
            dg_ref[...] = dgain

        @pl.when(pl.program_id(0) > 0)
        def _():
            dg_ref[...] += dgain

    row = pl.BlockSpec((bm, width), lambda i: (i, 0))
    in_specs = [row, pl.BlockSpec((bm, width), lambda i: (i, col_block)), pl.BlockSpec((1, width), lambda i: (0, 0))]
    args = [dy, x, gain]
    out_specs = [row, pl.BlockSpec((1, width), lambda i: (0, 0))]
    out_shape = [jax.ShapeDtypeStruct((s, width), out_dtype), jax.ShapeDtypeStruct((1, width), F32)]
    if has_res:
        in_specs.append(row)
        args.append(dres)
    return _call(body, name=name, grid=(s // bm,), in_specs=in_specs, out_specs=out_specs, out_shape=out_shape,
                 sem=("arbitrary",), args=args)


def _ffn_up(name, n, wg, wu):
    s = n.shape[0]
    bm = _row_block(s)

    def body(n_ref, wg_ref, wu_ref, a_ref, dadu_ref, dadg_ref):
        x = n_ref[...]
        g = _dot(x, wg_ref[...], NT)
        u = _dot(x, wu_ref[...], NT)
        sg = jax.nn.sigmoid(g)
        silu = g * sg
        a_ref[...] = (silu * u).astype(BF16)
        dadu_ref[...] = silu.astype(BF16)
        dadg_ref[...] = (u * (sg * (1.0 + g * (1.0 - sg)))).astype(BF16)

    w_spec = pl.BlockSpec((None, FF_SHARD, D_MODEL), lambda j, i: (j, 0, 0))
    o_spec = pl.BlockSpec((None, bm, FF_SHARD), lambda j, i: (j, i, 0))
    shp = jax.ShapeDtypeStruct((N_CHIPS, s, FF_SHARD), BF16)
    return _call(
        body, name=name, grid=(N_CHIPS, s // bm),
        in_specs=[pl.BlockSpec((bm, D_MODEL), lambda j, i: (i, 0)), w_spec, w_spec],
        out_specs=[o_spec, o_spec, o_spec], out_shape=[shp, shp, shp],
        sem=("parallel", "parallel"), args=[n, wg, wu])


def _ffn1_up_gather(xin, gain, g_sh, u_sh, d_sh):
    s = xin.shape[0]
    bm = _row_block(s)
    nrb = s // bm
    rows, cols = g_sh.shape

    def body(x_ref, gain_ref, gs, us, ds, n_ref, a_ref, dadu_ref, dadg_ref, wg, wu, wd, gbuf, ubuf,
             send, recv, qsend, qrecv, fsend, frecv, loc, ld):
        r, i = pl.program_id(0), pl.program_id(1)
        x, y, c = lax.axis_index("x"), lax.axis_index("y"), lax.axis_index("c")
        sib = (x, y, 1 - c)
        mine, _ = _half_rows(c, rows)
        quarters = _quarter_rows(c, rows)
        shards, fulls, bufs = (gs, us, ds), (wg, wu, wd), (gbuf, ubuf)

        def remote(src, dst, ssem, rsem, dev):
            return pltpu.make_async_remote_copy(src_ref=src, dst_ref=dst, send_sem=ssem, recv_sem=rsem,
                                                device_id=dev, device_id_type=_MESH)

        def peer(rel):
            return ((1 - x) if rel & 2 else x, (1 - y) if rel & 1 else y, c)

        def ici(k, rel, dev=sib):
            return remote(shards[k].at[mine], fulls[k].at[rel, mine], send.at[k, rel - 1], recv.at[k, rel - 1], dev)

        def quarter(k, which, dev=sib):
            slab, q = ((2, quarters[0]), (1, quarters[1]))[which]
            return remote(fulls[k].at[slab, q], fulls[k].at[3, q], qsend.at[k, which], qrecv.at[k, which], dev)

        def fwd(k, rel):
            return remote(fulls[k].at[rel, mine], fulls[k].at[rel, mine], fsend.at[k, rel - 1], frecv.at[k, rel - 1], sib)

        def own(k):
            return pltpu.make_async_copy(shards[k], fulls[k].at[0], loc.at[k])

        def load(slab):
            for k in (0, 1):
                pltpu.make_async_copy(shards[k] if slab == 0 else fulls[k].at[slab], bufs[k], ld.at[k]).start()
            for k in (0, 1):
                pltpu.make_async_copy(shards[k] if slab == 0 else fulls[k].at[slab], bufs[k], ld.at[k]).wait()

        def from_neighbour(ks, rel):
            for k in ks:
                ici(k, rel).wait_recv()
                fwd(k, rel).start()
                quarter(k, 0 if rel == 2 else 1, peer(1 if rel == 2 else 2)).start()
            for k in ks:
                fwd(k, rel).wait_recv()

        def from_diagonal(ks):
            for k in ks:
                quarter(k, 0).wait_recv()
                quarter(k, 1).wait_recv()
                fwd(k, 3).start()
            for k in ks:
                fwd(k, 3).wait_recv()

        @pl.when(jnp.logical_and(r == 0, i == 0))
        def _():
            for k in range(3):
                own(k).start(priority=1)
            for rel in (1, 2):
                for k in (0, 1):
                    ici(k, rel, peer(rel)).start()
            load(0)

        @pl.when(jnp.logical_and(r == 1, i == 0))
        def _():
            from_neighbour((0, 1), 1)
            load(1)
            for rel in (1, 2):
                ici(2, rel, peer(rel)).start()

        @pl.when(jnp.logical_and(r == 2, i == 0))
        def _():
            from_neighbour((0, 1), 2)
            load(2)

        @pl.when(jnp.logical_and(r == 3, i == 0))
        def _():
            from_diagonal((0, 1))
            load(3)

        xv = _norm_bf16(x_ref[...], gain_ref[...])

        @pl.when(r == 0)
        def _():
            n_ref[...] = xv

        g = _dot(xv, gbuf[...], NT)
        u = _dot(xv, ubuf[...], NT)
        sg = jax.nn.sigmoid(g)
        silu = g * sg
        a_ref[...] = (silu * u).astype(BF16)
        dadu_ref[...] = silu.astype(BF16)
        dadg_ref[...] = (u * (sg * (1.0 + g * (1.0 - sg)))).astype(BF16)

        @pl.when(jnp.logical_and(r == 3, i == nrb - 1))
        def _():
            from_neighbour((2,), 1)
            from_neighbour((2,), 2)
            from_diagonal((2,))
            for k in range(3):
                for rel in (1, 2):
                    ici(k, rel).wait_send()
                for which in (0, 1):
                    quarter(k, which).wait_send()
                for rel in (1, 2, 3):
                    fwd(k, rel).wait_send()
                own(k).wait()

    o_spec = pl.BlockSpec((None, bm, FF_SHARD), lambda r, i: (r, i, 0))
    act = jax.ShapeDtypeStruct((N_CHIPS, s, FF_SHARD), BF16)
    full = jax.ShapeDtypeStruct((N_CHIPS, rows, cols), BF16)
    dma = pltpu.SemaphoreType.DMA
    if _PLAN is not None:
        _PLAN.last_slab_step = 3 * nrb
    n_spec = pl.BlockSpec((bm, D_MODEL), lambda r, i: (jnp.where(r == 0, i, nrb - 1), 0))
    return _call(
        body, name="ffn1_up", grid=(N_CHIPS, nrb),
        in_specs=[pl.BlockSpec((bm, D_MODEL), lambda r, i: (i, 0)), pl.BlockSpec((1, D_MODEL), lambda r, i: (0, 0)),
                  _ANY, _ANY, _ANY],
        out_specs=[n_spec, o_spec, o_spec, o_spec, _ANY, _ANY, _ANY],
        out_shape=[jax.ShapeDtypeStruct((s, D_MODEL), BF16), act, act, act, full, full, full],
        scratch_shapes=[pltpu.VMEM((rows, cols), BF16), pltpu.VMEM((rows, cols), BF16), dma((3, 2)), dma((3, 2)),
                        dma((3, 2)), dma((3, 2)), dma((3, 3)), dma((3, 3)), dma((3,)), dma((2,))],
        sem=("arbitrary", "arbitrary"), args=[xin, gain, g_sh, u_sh, d_sh])


def _residual_epilogue(alpha, with_norm):
    if not with_norm:
        return lambda acc, r: (r + alpha * acc,)

    def epilogue(acc, r, g):
        h = r + alpha * acc
        rs = lax.rsqrt(jnp.mean(h * h, axis=-1, keepdims=True) + RMS_EPS)
        return h, (h * rs) * g

    return epilogue


def _residual_outs(s, bm, gain):
    row = pl.BlockSpec((bm, D_MODEL), lambda i, k: (i, 0))
    outs = [(jax.ShapeDtypeStruct((s, D_MODEL), F32), row)]
    if gain is None:
        return [], outs
    return [(gain, pl.BlockSpec((1, D_MODEL), lambda i, k: (0, 0)))], outs + [(jax.ShapeDtypeStruct((s, D_MODEL), BF16), row)]


def _loss_epilogue(acc, res, g, target):
    d = acc.shape[-1]
    h = res + 0.5 * acc
    r = lax.rsqrt(jnp.mean(h * h, axis=-1, keepdims=True) + RMS_EPS)
    err = (h * r) * g - target
    part = 0.5 * jnp.sum(jnp.mean(err * err, axis=-1, keepdims=True), axis=0, keepdims=True)
    dx, dgain = _rms_bwd_math(err * (1.0 / d), h, g, d)
    return dx, dx, jnp.broadcast_to(part, (1, 128)), dgain


def _ffn_down(name, a, wd, res, gain=None, loss=None):
    s = a.shape[1]
    bm = _row_block(s, 512)
    row = pl.BlockSpec((bm, D_MODEL), lambda i, k: (i, 0))
    terms = [(a, pl.BlockSpec((None, bm, FF_SHARD), lambda i, k, j=j: (j, i, 0)),
              wd, pl.BlockSpec((None, FF_SHARD, D_MODEL), lambda i, k, j=j: (j, 0, 0)), NN) for j in range(N_CHIPS)]
    if loss is not None:
        vec = pl.BlockSpec((1, D_MODEL), lambda i, k: (0, 0))
        outs = [(jax.ShapeDtypeStruct((s, D_MODEL), F32), row), (jax.ShapeDtypeStruct((s, D_MODEL), BF16), row),
                (jax.ShapeDtypeStruct((1, 128), F32), pl.BlockSpec((1, 128), lambda i, k: (0, 0))),
                (jax.ShapeDtypeStruct((1, D_MODEL), F32), vec)]
        return _matmul(name, (s // bm, 1), terms, [(res, row), (loss[0], vec), (loss[1], row)], outs,
                       _loss_epilogue, None, summed=(2, 3))
    extras, outs = _residual_outs(s, bm, gain)
    res_out = _matmul(name, (s // bm, 1), terms, [(res, row)] + extras, outs,
                      _residual_epilogue(0.5, gain is not None), None)
    return res_out if gain is not None else res_out[0]


def _norm_bwd_epilogue(width):
    def epilogue(acc, h, g, dres):
        dx, dgain = _rms_bwd_math(acc, h, g, width)
        dx = dx + dres
        return dx, dx, dgain

    return epilogue


def _norm_bwd_operands(s, bm, h, gain, dres):
    row = pl.BlockSpec((bm, D_MODEL), lambda i, k: (i, 0))
    vec = pl.BlockSpec((1, D_MODEL), lambda i, k: (0, 0))
    extras = [(h, row), (gain, vec), (dres, row)]
    outs = [(jax.ShapeDtypeStruct((s, D_MODEL), F32), row), (jax.ShapeDtypeStruct((s, D_MODEL), BF16), row),
            (jax.ShapeDtypeStruct((1, D_MODEL), F32), vec)]
    return extras, outs, (2,)


def _ffn_bwd(tag, dh, n, dadg, dadu, a, wg, wu, wd, grads, norm_bwd=None, many_calls=True):
    s = dh.shape[0]
    bm = _row_block(s)
    bk = _reduce_block(s)
    nk = s // bk

    def act_bwd(acc, dg_da, du_da):
        da = 0.5 * acc
        return da * dg_da.astype(F32), da * du_da.astype(F32)

    slab = pl.BlockSpec((None, bm, FF_SHARD), lambda j, i, k: (j, i, 0))
    shp = jax.ShapeDtypeStruct((N_CHIPS, s, FF_SHARD), BF16)
    dg, du = _matmul(
        tag + "_dact", (N_CHIPS, s // bm, 1),
        [(dh, pl.BlockSpec((bm, D_MODEL), lambda j, i, k: (i, 0)),
          wd, pl.BlockSpec((None, FF_SHARD, D_MODEL), lambda j, i, k: (j, 0, 0)), NT)],
        [(dadg, slab), (dadu, slab)], [(shp, slab), (shp, slab)], act_bwd, None)

    grads[tag + "_w_down"] = _matmul(
        tag + "_dwd", (N_CHIPS, nk),
        [(a, pl.BlockSpec((None, bk, FF_SHARD), lambda j, k: (j, k, 0)),
          dh, pl.BlockSpec((bk, D_MODEL), lambda j, k: (k, 0)), TN)],
        [], [(jax.ShapeDtypeStruct((N_CHIPS, FF_SHARD, D_MODEL), BF16),
              pl.BlockSpec((None, FF_SHARD, D_MODEL), lambda j, k: (j, 0, 0)))],
        lambda acc: (0.5 * acc,), (FF_SHARD, D_MODEL))[0]

    def dw_up(nm, dact):
        return _matmul(
            nm, (N_CHIPS, nk),
            [(dact, pl.BlockSpec((None, bk, FF_SHARD), lambda j, k: (j, k, 0)),
              n, pl.BlockSpec((bk, D_MODEL), lambda j, k: (k, 0)), TN)],
            [], [(jax.ShapeDtypeStruct((N_CHIPS, FF_SHARD, D_MODEL), BF16),
                  pl.BlockSpec((None, FF_SHARD, D_MODEL), lambda j, k: (j, 0, 0)))],
            _ident, (FF_SHARD, D_MODEL))[0]

    def dw_up_pair(nm):
        def body(dg_ref, du_ref, n_ref, og_ref, ou_ref, acc_g, acc_u):
            k = pl.program_id(1)

            @pl.when(k == 0)
            def _():
                acc_g[...] = jnp.zeros_like(acc_g)
                acc_u[...] = jnp.zeros_like(acc_u)

            nv = n_ref[...]
            acc_g[...] += _dot(dg_ref[...], nv, TN)
            acc_u[...] += _dot(du_ref[...], nv, TN)

            @pl.when(k == nk - 1)
            def _():
                og_ref[...] = acc_g[...].astype(BF16)
                ou_ref[...] = acc_u[...].astype(BF16)

        act = pl.BlockSpec((None, bk, FF_SHARD), lambda j, k: (j, k, 0))
        out = pl.BlockSpec((None, FF_SHARD, D_MODEL), lambda j, k: (j, 0, 0))
        shape = jax.ShapeDtypeStruct((N_CHIPS, FF_SHARD, D_MODEL), BF16)
        return _call(body, name=nm, grid=(N_CHIPS, nk),
                     in_specs=[act, act, pl.BlockSpec((bk, D_MODEL), lambda j, k: (k, 0))], out_specs=[out, out],
                     out_shape=[shape, shape],
                     scratch_shapes=[pltpu.VMEM((FF_SHARD, D_MODEL), F32), pltpu.VMEM((FF_SHARD, D_MODEL), F32)],
                     sem=("parallel", "arbitrary"), args=[dg, du, n])

    if many_calls:
        grads[tag + "_w_gate"] = dw_up(tag + "_dwg", dg)
        grads[tag + "_w_up"] = dw_up(tag + "_dwu", du)
    else:
        grads[tag + "_w_gate"], grads[tag + "_w_up"] = dw_up_pair(tag + "_dwgu")

    bn = _row_block(s, 512)
    steps = s // bn // 2 if many_calls else s // bn
    prev, dgain = (), None
    for part, off in ((("_dn_a", 0), ("_dn_b", steps)) if many_calls else (("_dn", 0),)):
        row = pl.BlockSpec((bn, D_MODEL), lambda i, k, off=off: (i + off, 0))
        terms = []
        for j in range(N_CHIPS):
            a_slab = pl.BlockSpec((None, bn, FF_SHARD), lambda i, k, j=j, off=off: (j, i + off, 0))
            w_slab = pl.BlockSpec((None, FF_SHARD, D_MODEL), lambda i, k, j=j: (j, 0, 0))
            terms += [(dg, a_slab, wg, w_slab, NN), (du, a_slab, wu, w_slab, NN)]
        if norm_bwd is None:
            prev = _matmul(tag + part, (steps, 1), terms, [], [(jax.ShapeDtypeStruct((s, D_MODEL), F32), row)],
                           _ident, None, fill=prev)
            continue
        h, gain, dres = norm_bwd
        vec = pl.BlockSpec((1, D_MODEL), lambda i, k: (0, 0))
        res = _matmul(
            tag + part, (steps, 1), terms, [(h, row), (gain, vec), (dres, row)],
            [(jax.ShapeDtypeStruct((s, D_MODEL), F32), row), (jax.ShapeDtypeStruct((s, D_MODEL), BF16), row),
             (jax.ShapeDtypeStruct((1, D_MODEL), F32), vec)],
            _norm_bwd_epilogue(D_MODEL), None, fill=prev, summed=(2,))
        prev = res[:2]
        dgain = res[2] if dgain is None else dgain + res[2]
    return prev[0] if norm_bwd is None else (prev[0], prev[1], dgain)


def _mm_nn(name, a, b, out_dtype, res=None, gain=None):
    s, k = a.shape
    nn = b.shape[1]
    bm = _row_block(s)
    row = pl.BlockSpec((bm, nn), lambda i, kk: (i, 0))
    term = [(a, pl.BlockSpec((bm, k), lambda i, kk: (i, 0)), b, pl.BlockSpec((k, nn), lambda i, kk: (0, 0)), NN)]
    if res is None:
        return _matmul(name, (s // bm, 1), term, [], [(jax.ShapeDtypeStruct((s, nn), out_dtype), row)], _ident, None)[0]
    extras, outs = _residual_outs(s, bm, gain)
    res_out = _matmul(name, (s // bm, 1), term, [(res, row)] + extras, outs,
                      _residual_epilogue(1.0, gain is not None), None)
    return res_out if gain is not None else res_out[0]


def _mm_nt(name, a, b, out_dtype, attn_out=None, nh=0, dv=0):
    s, nn = a.shape
    k = b.shape[0]
    bm = _row_block(s)
    term = [(a, pl.BlockSpec((bm, nn), lambda i, kk: (i, 0)), b, pl.BlockSpec((k, nn), lambda i, kk: (0, 0)), NT)]
    out = (jax.ShapeDtypeStruct((s, k), out_dtype), pl.BlockSpec((bm, k), lambda i, kk: (i, 0)))
    if attn_out is None:
        return _matmul(name, (s // bm, 1), term, [], [out], _ident, None)[0]

    def with_delta(acc, o):
        do = acc.astype(out_dtype).astype(F32)
        cols = [jnp.sum(do[:, h * dv:(h + 1) * dv] * o[:, h * dv:(h + 1) * dv].astype(F32), axis=-1, keepdims=True)
                for h in range(nh)]
        return acc, jnp.stack(cols, axis=0)

    return _matmul(
        name, (s // bm, 1), term, [(attn_out, pl.BlockSpec((bm, nh * dv), lambda i, kk: (i, 0)))],
        [out, (jax.ShapeDtypeStruct((nh, s, 1), F32), pl.BlockSpec((nh, bm, 1), lambda i, kk: (0, i, 0)))],
        with_delta, None)


def _mm_nt_norm_bwd(name, a, b, h, gain, dres):
    s, nn = a.shape
    bm = _row_block(s, 512)
    extras, outs, summed = _norm_bwd_operands(s, bm, h, gain, dres)
    return _matmul(
        name, (s // bm, 1),
        [(a, pl.BlockSpec((bm, nn), lambda i, kk: (i, 0)), b, pl.BlockSpec(b.shape, lambda i, kk: (0, 0)), NT)],
        extras, outs, _norm_bwd_epilogue(D_MODEL), None, summed=summed)


def _w_in_dx_norm_bwd(dz, w_t, h, gain, dres):
    s = dz.shape[0]
    bm = _row_block(s, 512)
    epilogue = _norm_bwd_epilogue(D_MODEL)

    def body(dz_ref, w_ref, h_ref, g_ref, r_ref, dx_ref, dxb_ref, dg_ref):
        dzv = dz_ref[...]
        dn = jnp.concatenate([_dot(dzv, w_ref[j], NN) for j in range(N_CHIPS)], axis=1)
        dx, _, dgain = epilogue(dn, h_ref[...], g_ref[...], r_ref[...])
        dx_ref[...] = dx
        dxb_ref[...] = dx.astype(BF16)

        @pl.when(pl.program_id(0) == 0)
        def _():
            dg_ref[...] = dgain

        @pl.when(pl.program_id(0) > 0)
        def _():
            dg_ref[...] += dgain

    row = pl.BlockSpec((bm, D_MODEL), lambda i: (i, 0))
    vec = pl.BlockSpec((1, D_MODEL), lambda i: (0, 0))
    return _call(
        body, name="w_in_dx", grid=(s // bm,),
        in_specs=[row, pl.BlockSpec(w_t.shape, lambda i: (0, 0, 0)), row, vec, row],
        out_specs=[row, row, vec],
        out_shape=[jax.ShapeDtypeStruct((s, D_MODEL), F32), jax.ShapeDtypeStruct((s, D_MODEL), BF16),
                   jax.ShapeDtypeStruct((1, D_MODEL), F32)],
        sem=("arbitrary",), args=[dz, w_t, h, gain, dres])


def _mm_tn_stacked(name, a_list, b):
    s, nn = b.shape
    widths = [a.shape[1] for a in a_list]
    total = sum(widths)
    bk = _reduce_block(s)
    nk = s // bk
    na = len(a_list)

    def body(*refs):
        a_refs, b_ref, o_ref, acc_ref = refs[:na], refs[na], refs[na + 1], refs[na + 2]
        k = pl.program_id(0)

        @pl.when(k == 0)
        def _():
            acc_ref[...] = jnp.zeros_like(acc_ref)

        bv = b_ref[...]
        lo = 0
        for a_ref, w in zip(a_refs, widths):
            acc_ref[lo:lo + w, :] += _dot(a_ref[...], bv, TN)
            lo += w

        @pl.when(k == nk - 1)
        def _():
            o_ref[...] = acc_ref[...].astype(o_ref.dtype)

    return _call(
        body, name=name, grid=(nk,),
        in_specs=[pl.BlockSpec((bk, w), lambda k: (k, 0)) for w in widths] + [pl.BlockSpec((bk, nn), lambda k: (k, 0))],
        out_specs=[pl.BlockSpec((total, nn), lambda k: (0, 0))],
        out_shape=[jax.ShapeDtypeStruct((total, nn), BF16)],
        scratch_shapes=[pltpu.VMEM((total, nn), F32)], sem=("arbitrary",), args=list(a_list) + [b])[0]


def _mm_tn_pairs(name, pairs):
    s = pairs[0][0].shape[0]
    bk = _row_block(s)
    nk = s // bk
    npairs = len(pairs)
    shapes = [(a.shape[1], b.shape[1]) for a, b in pairs]

    def body(*refs):
        ins, outs, accs = refs[:2 * npairs], refs[2 * npairs:3 * npairs], refs[3 * npairs:]
        k = pl.program_id(0)

        @pl.when(k == 0)
        def _():
            for acc in accs:
                acc[...] = jnp.zeros_like(acc)

        for p in range(npairs):
            accs[p][...] += _dot(ins[2 * p][...], ins[2 * p + 1][...], TN)

        @pl.when(k == nk - 1)
        def _():
            for o, acc in zip(outs, accs):
                o[...] = acc[...].astype(o.dtype)

    in_specs = [pl.BlockSpec((bk, x.shape[1]), lambda k: (k, 0)) for pair in pairs for x in pair]
    return _call(
        body, name=name, grid=(nk,), in_specs=in_specs,
        out_specs=[pl.BlockSpec(shp, lambda k: (0, 0)) for shp in shapes],
        out_shape=[jax.ShapeDtypeStruct(shp, BF16) for shp in shapes],
        scratch_shapes=[pltpu.VMEM(shp, F32) for shp in shapes], sem=("arbitrary",),
        args=[x for pair in pairs for x in pair])


def _mm_heads_fwd(name, a, w, out_dtype, w_transposed=False):
    s, k = a.shape
    nh = w.shape[0]
    nn = w.shape[1] if w_transposed else w.shape[2]
    bm = _row_block(s)
    return _matmul(
        name, (nh, s // bm, 1),
        [(a, pl.BlockSpec((bm, k), lambda h, i, kk: (i, 0)),
          w, pl.BlockSpec((None,) + w.shape[1:], lambda h, i, kk: (h, 0, 0)), NT if w_transposed else NN)],
        [], [(jax.ShapeDtypeStruct((s, nh * nn), out_dtype), pl.BlockSpec((bm, nn), lambda h, i, kk: (i, h)))],
        _ident, None)[0]


def _mm_heads_bwd(name, dy, a, w, w_transposed=False):
    s, k = a.shape
    nh = w.shape[0]
    nn = w.shape[1] if w_transposed else w.shape[2]
    bm = _row_block(s)
    bk = _reduce_block(s)
    w_spec = pl.BlockSpec((None,) + w.shape[1:], lambda i, h: (h, 0, 0))
    da = _matmul(
        name + "_dx", (s // bm, nh),
        [(dy, pl.BlockSpec((bm, nn), lambda i, h: (i, h)), w, w_spec, NN if w_transposed else NT)],
        [], [(jax.ShapeDtypeStruct((s, k), F32), pl.BlockSpec((bm, k), lambda i, h: (i, 0)))], _ident, (bm, k))[0]
    a_term = (a, pl.BlockSpec((bk, k), lambda h, kk: (kk, 0)))
    dy_term = (dy, pl.BlockSpec((bk, nn), lambda h, kk: (kk, h)))
    lhs, rhs = (dy_term, a_term) if w_transposed else (a_term, dy_term)
    dw = _matmul(
        name + "_dw", (nh, s // bk), [lhs + rhs + (TN,)],
        [], [(jax.ShapeDtypeStruct(w.shape, BF16), pl.BlockSpec((None,) + w.shape[1:], lambda h, kk: (h, 0, 0)))],
        _ident, w.shape[1:])[0]
    return da, dw


def _w_in_fwd(n, w_t):
    s = n.shape[0]
    bm = _row_block(s)
    nh, nout, kin = w_t.shape
    terms = [(n, pl.BlockSpec((bm, kin), lambda i, k, j=j: (i, j)),
              w_t, pl.BlockSpec((None, nout, kin), lambda i, k, j=j: (j, 0, 0)), NT) for j in range(nh)]
    row = pl.BlockSpec((bm, nout), lambda i, k: (i, 0))
    return _matmul("w_in", (s // bm, 1), terms, [], [(jax.ShapeDtypeStruct((s, nout), F32), row)], _ident, None)[0]


def _w_in_dw(dz, n):
    s, nout = dz.shape
    kin = n.shape[1] // N_CHIPS
    bk = _reduce_block(s)
    return _matmul(
        "w_in_dw", (N_CHIPS, s // bk),
        [(dz, pl.BlockSpec((bk, nout), lambda j, k: (k, 0)), n, pl.BlockSpec((bk, kin), lambda j, k: (k, j)), TN)],
        [], [(jax.ShapeDtypeStruct((N_CHIPS, nout, kin), BF16), pl.BlockSpec((None, nout, kin), lambda j, k: (j, 0, 0)))],
        _ident, (nout, kin))[0]


def _rope_tables(positions):
    half = ROPE_DIM // 2
    freqs = 1.0 / (ROPE_BASE ** (jnp.arange(0, ROPE_DIM, 2, dtype=F32) / ROPE_DIM))
    ang = positions.astype(F32)[:, None] * freqs
    cos, sin = jnp.cos(ang), jnp.sin(ang)
    z = jnp.zeros_like(cos)
    tc = jnp.concatenate([cos, cos, z, z], axis=-1)
    ta = jnp.concatenate([-sin, z, z, z], axis=-1)
    tb = jnp.concatenate([z, sin, z, z], axis=-1)
    assert tc.shape[-1] == 4 * half
    return tc, ta, tb


def _rope(x, tc, ta, tb):
    return x * tc + pltpu.roll(x, 96, 1) * ta + pltpu.roll(x, 32, 1) * tb


def _rope_t(dy, tc, ta, tb):
    return dy * tc + pltpu.roll(dy * ta, 32, 1) + pltpu.roll(dy * tb, 96, 1)


def _norm_bf16(x, g):
    r = lax.rsqrt(jnp.mean(x * x, axis=-1, keepdims=True) + RMS_EPS)
    return ((x * r) * g).astype(BF16)


def _qkv_prep(z, q_gain, kv_gain, wq_t, wkv, tc, ta, tb):
    s = z.shape[0]
    bm = _row_block(s, 512)

    def body(zq_ref, zkv_ref, zkr_ref, qg_ref, kvg_ref, wq_ref, wkv_ref, tc_ref, ta_ref, tb_ref,
             qn_ref, kvn_ref, q_ref, k_ref, v_ref):
        c, a, b = tc_ref[...], ta_ref[...], tb_ref[...]
        qn = _norm_bf16(zq_ref[...], qg_ref[...])
        kvn = _norm_bf16(zkv_ref[...], kvg_ref[...])
        qn_ref[...] = qn
        kvn_ref[...] = kvn
        kpe = _rope(zkr_ref[...], c, a, b).astype(BF16)
        for h in range(MLA_HEADS):
            lo = h * HEAD_QK
            qp = _dot(qn, wq_ref[h], NT)
            q_ref[:, lo:lo + 128] = qp[:, :128].astype(BF16)
            q_ref[:, lo + 128:lo + 256] = _rope(qp[:, 128:], c, a, b).astype(BF16)
            kv = _dot(kvn, wkv_ref[h], NN)
            k_ref[:, lo:lo + 128] = kv[:, :128].astype(BF16)
            k_ref[:, lo + 128:lo + 256] = kpe
            v_ref[:, h * HEAD_V:(h + 1) * HEAD_V] = kv[:, 128:].astype(BF16)

    def cols(width, blk):
        return pl.BlockSpec((bm, width), lambda i: (i, blk))

    def whole(a):
        return pl.BlockSpec(a.shape, lambda i: (0,) * a.ndim)

    tab = cols(128, 0)
    return _call(
        body, name="qkv_prep", grid=(s // bm,),
        in_specs=[cols(Q_LORA, 0), cols(KV_LORA, 2), cols(128, 3), whole(q_gain), whole(kv_gain), whole(wq_t),
                  whole(wkv), tab, tab, tab],
        out_specs=[cols(Q_LORA, 0), cols(KV_LORA, 0), cols(1024, 0), cols(1024, 0), cols(512, 0)],
        out_shape=[jax.ShapeDtypeStruct((s, Q_LORA), BF16), jax.ShapeDtypeStruct((s, KV_LORA), BF16),
                   jax.ShapeDtypeStruct((s, 1024), BF16), jax.ShapeDtypeStruct((s, 1024), BF16),
                   jax.ShapeDtypeStruct((s, 512), BF16)],
        sem=("parallel",), args=[z, z, z, q_gain, kv_gain, wq_t, wkv, tc, ta, tb])


def _qkv_prep_bwd(dq, dk, dv, z, qn, kvn, q_gain, kv_gain, wq_t, wkv, tc, ta, tb):
    s = z.shape[0]
    bm = _row_block(s, 512)
    nsteps = s // bm

    def body(dq_ref, dk_ref, dv_ref, zq_ref, zkv_ref, qn_ref, kvn_ref, qg_ref, kvg_ref, wq_ref, wkv_ref,
             tc_ref, ta_ref, tb_ref, dz_ref, dqg_ref, dkvg_ref, dwq_ref, dwkv_ref, wq_acc, wkv_acc):
        i = pl.program_id(0)
        c, a, b = tc_ref[...], ta_ref[...], tb_ref[...]

        @pl.when(i == 0)
        def _():
            wq_acc[...] = jnp.zeros_like(wq_acc)
            wkv_acc[...] = jnp.zeros_like(wkv_acc)

        qn, kvn = qn_ref[...], kvn_ref[...]
        dqn = jnp.zeros((bm, Q_LORA), F32)
        dkvn = jnp.zeros((bm, KV_LORA), F32)
        dpe = jnp.zeros((bm, 128), F32)
        for h in range(MLA_HEADS):
            lo = h * HEAD_QK
            dqp = jnp.concatenate([dq_ref[:, lo:lo + 128],
                                   _rope_t(dq_ref[:, lo + 128:lo + 256].astype(F32), c, a, b).astype(BF16)], axis=1)
            dqn = dqn + _dot(dqp, wq_ref[h], NN)
            wq_acc[h] += _dot(dqp, qn, TN)
            dkv = jnp.concatenate([dk_ref[:, lo:lo + 128], dv_ref[:, h * HEAD_V:(h + 1) * HEAD_V]], axis=1)
            dkvn = dkvn + _dot(dkv, wkv_ref[h], NT)
            wkv_acc[h] += _dot(kvn, dkv, TN)
            dpe = dpe + dk_ref[:, lo + 128:lo + 256].astype(F32)
        dcq, dqg = _rms_bwd_math(dqn, zq_ref[...], qg_ref[...], Q_LORA)
        dckv, dkvg = _rms_bwd_math(dkvn, zkv_ref[...], kvg_ref[...], KV_LORA)
        dz_ref[:, 0:Q_LORA] = dcq.astype(BF16)
        dz_ref[:, Q_LORA:Q_LORA + KV_LORA] = dckv.astype(BF16)
        dz_ref[:, Q_LORA + KV_LORA:512] = _rope_t(dpe, c, a, b).astype(BF16)

        @pl.when(i == 0)
        def _():
            dqg_ref[...] = dqg
            dkvg_ref[...] = dkvg

        @pl.when(i > 0)
        def _():
            dqg_ref[...] += dqg
            dkvg_ref[...] += dkvg

        @pl.when(i == nsteps - 1)
        def _():
            dwq_ref[...] = wq_acc[...].astype(BF16)
            dwkv_ref[...] = wkv_acc[...].astype(BF16)

    def cols(width, blk):
        return pl.BlockSpec((bm, width), lambda i: (i, blk))

    def whole(shape):
        return pl.BlockSpec(shape, lambda i: (0,) * len(shape))

    tab = cols(128, 0)
    return _call(
        body, name="qkv_prep_bwd", grid=(nsteps,),
        in_specs=[cols(1024, 0), cols(1024, 0), cols(512, 0), cols(Q_LORA, 0), cols(KV_LORA, 2), cols(Q_LORA, 0),
                  cols(KV_LORA, 0), whole(q_gain.shape), whole(kv_gain.shape), whole(wq_t.shape), whole(wkv.shape),
                  tab, tab, tab],
        out_specs=[cols(512, 0), whole(q_gain.shape), whole(kv_gain.shape), whole(wq_t.shape), whole(wkv.shape)],
        out_shape=[jax.ShapeDtypeStruct((s, 512), BF16), jax.ShapeDtypeStruct(q_gain.shape, F32),
                   jax.ShapeDtypeStruct(kv_gain.shape, F32), jax.ShapeDtypeStruct(wq_t.shape, BF16),
                   jax.ShapeDtypeStruct(wkv.shape, BF16)],
        scratch_shapes=[pltpu.VMEM(wq_t.shape, F32), pltpu.VMEM(wkv.shape, F32)],
        sem=("arbitrary",), args=[dq, dk, dv, z, z, qn, kvn, q_gain, kv_gain, wq_t, wkv, tc, ta, tb])


def _causal_mask(s, row0, col0):
    rows = row0 + lax.broadcasted_iota(jnp.int32, s.shape, 0)
    cols = col0 + lax.broadcasted_iota(jnp.int32, s.shape, 1)
    return jnp.where(cols <= rows, s, -jnp.inf)


def _attn_fwd(name, q, k, k_off, v, v_off, nh, dq, dv, scale, causal, blk):
    sq, sk = q.shape[0], k.shape[0]
    bq = min(blk, sq)
    bk = min(blk, sk)
    nkv = sk // bk
    assert not causal or (sq == sk and bq == bk)

    hq = bq
    log2e = 1.4426950408889634
    c2 = scale * log2e

    def body(q_ref, k_ref, v_ref, o_ref, lse_ref):
        qi = pl.program_id(1)
        qs = (q_ref[...],)

        def step(j, carry, masked):
            rows = pl.ds(pl.multiple_of(j * bk, bk), bk)
            kb, vb = k_ref[rows, :], v_ref[rows, :]
            out = []
            for t, (m, l, acc) in enumerate(carry):
                s = _dot(qs[t], kb, NT) * c2
                if masked:
                    s = _causal_mask(s, qi * bq + t * hq, j * bk)
                m_new = jnp.maximum(m, jnp.max(s, axis=-1, keepdims=True))
                alpha = jnp.exp2(m - m_new)
                p = jnp.exp2(s - m_new)
                l = alpha * l + jnp.sum(p, axis=-1, keepdims=True)
                acc = alpha * acc + _dot(p, vb, NN)
                out.append((m_new, l, acc))
            return tuple(out)

        one = (jnp.full((hq, 1), -jnp.inf, F32), jnp.zeros((hq, 1), F32), jnp.zeros((hq, dv), F32))
        init = (one,)
        if causal:
            carry = lax.fori_loop(0, qi, lambda j, c: step(j, c, False), init)
            fin = step(qi, carry, True)
        else:
            fin = lax.fori_loop(0, nkv, lambda j, c: step(j, c, False), init)
        for t, (m, l, acc) in enumerate(fin):
            o_ref[t * hq:(t + 1) * hq, :] = (acc / l).astype(o_ref.dtype)
            lse_ref[t * hq:(t + 1) * hq, :] = m * (1.0 / log2e) + jnp.log(l)

    return _call(
        body, name=name, grid=(nh, sq // bq),
        in_specs=[pl.BlockSpec((bq, dq), lambda h, i: (i, h)),
                  pl.BlockSpec((sk, dq), lambda h, i: (0, k_off + h)),
                  pl.BlockSpec((sk, dv), lambda h, i: (0, v_off + h))],
        out_specs=[pl.BlockSpec((bq, dv), lambda h, i: (i, h)), pl.BlockSpec((None, bq, 1), lambda h, i: (h, i, 0))],
        out_shape=[jax.ShapeDtypeStruct((sq, nh * dv), BF16), jax.ShapeDtypeStruct((nh, sq, 1), F32)],
        sem=("parallel", "parallel"), args=[q, k, v])


def _attn_bwd(name, q, k, k_off, v, v_off, do, do_off, lse, delta, nh, dq, dv, scale, causal, blk):
    sq, sk = q.shape[0], k.shape[0]
    bq = min(blk, sq)
    bk = min(blk, sk)
    nq = sq // bq
    assert not causal or (sq == sk and bq == bk)

    nkv = sk // bk

    def body(q_ref, k_ref, v_ref, do_ref, lse_ref, dl_ref, dq_ref, dk_ref, dv_ref, dq_acc, dk_acc, dv_acc):
        j = pl.program_id(1)

        @pl.when(j == 0)
        def _():
            dq_acc[...] = jnp.zeros_like(dq_acc)

        dk_acc[...] = jnp.zeros_like(dk_acc)
        dv_acc[...] = jnp.zeros_like(dv_acc)
        kv = k_ref[...]
        vv = v_ref[...]

        def step(i, masked):
            rows = pl.ds(pl.multiple_of(i * bq, bq), bq)
            qv = q_ref[rows, :]
            dov = do_ref[rows, :].astype(BF16)
            s = _dot(qv, kv, NT) * scale
            if masked:
                s = _causal_mask(s, i * bq, j * bk)
            p = jnp.exp(s - lse_ref[rows, :])
            dp = _dot(dov, vv, NT)
            ds = (p * (dp - dl_ref[rows, :]) * scale).astype(BF16)
            dv_acc[...] += _dot(p, dov, TN)
            dk_acc[...] += _dot(ds, qv, TN)
            dq_acc[rows, :] += _dot(ds, kv, NN)

        if causal:
            step(j, True)

            def loop(i, c):
                step(i, False)
                return c

            lax.fori_loop(j + 1, nq, loop, 0)
        else:
            def loop(i, c):
                step(i, False)
                return c

            lax.fori_loop(0, nq, loop, 0)
        dk_ref[...] = dk_acc[...].astype(dk_ref.dtype)
        dv_ref[...] = dv_acc[...].astype(dv_ref.dtype)

        @pl.when(j == nkv - 1)
        def _():
            dq_ref[...] = dq_acc[...].astype(dq_ref.dtype)

    stat = pl.BlockSpec((None, sq, 1), lambda h, j: (h, 0, 0))
    return _call(
        body, name=name, grid=(nh, sk // bk),
        in_specs=[pl.BlockSpec((sq, dq), lambda h, j: (0, h)),
                  pl.BlockSpec((bk, dq), lambda h, j: (j, k_off + h)),
                  pl.BlockSpec((bk, dv), lambda h, j: (j, v_off + h)),
                  pl.BlockSpec((sq, dv), lambda h, j: (0, do_off + h)), stat, stat],
        out_specs=[pl.BlockSpec((sq, dq), lambda h, j: (0, h)),
                   pl.BlockSpec((bk, dq), lambda h, j: (j, h)),
                   pl.BlockSpec((bk, dv), lambda h, j: (j, h))],
        out_shape=[jax.ShapeDtypeStruct((sq, nh * dq), BF16), jax.ShapeDtypeStruct((sk, nh * dq), BF16),
                   jax.ShapeDtypeStruct((sk, nh * dv), BF16)],
        scratch_shapes=[pltpu.VMEM((sq, dq), F32), pltpu.VMEM((bk, dq), F32), pltpu.VMEM((bk, dv), F32)],
        sem=("parallel", "arbitrary"), args=[q, k, v, do, lse, delta])


def _pool_diff(z, g):
    s = z.shape[0]
    t = lax.broadcasted_iota(jnp.int32, z.shape, 0)
    acc = z
    sums = []
    for k in (1, 2, 4, 8):
        acc = acc + jnp.where(t >= k, pltpu.roll(acc, k, 0), 0.0)
        sums.append(acc)
    win = jnp.where(g == 0, sums[0], jnp.where(g == 1, sums[1], jnp.where(g == 2, sums[2], sums[3])))
    w = lax.shift_left(jnp.int32(2), g)
    count = jnp.minimum(t + 1, w).astype(F32)
    del s
    return win / count - z, count


def _pool_fwd(z, pool_w, pool_scale):
    s = z.shape[0]

    def body(z_ref, w_ref, sc_ref, o_ref):
        diff, _ = _pool_diff(z_ref[...], pl.program_id(0))
        o_ref[...] = (_dot(diff, w_ref[...], NN) * sc_ref[...]).astype(o_ref.dtype)

    return _call(
        body, name="pool_fwd", grid=(POOL_GROUPS,),
        in_specs=[pl.BlockSpec((s, POOL_CH), lambda g: (0, 4 + g)),
                  pl.BlockSpec((None, POOL_CH, POOL_CH), lambda g: (g, 0, 0)),
                  pl.BlockSpec((1, POOL_CH), lambda g: (0, g))],
        out_specs=[pl.BlockSpec((s, POOL_CH), lambda g: (0, g))],
        out_shape=[jax.ShapeDtypeStruct((s, POOL_GROUPS * POOL_CH), BF16)],
        sem=("parallel",), args=[z, pool_w, pool_scale])[0]


def _pool_bwd(dcat, z, pool_w, pool_scale):
    s = z.shape[0]

    def body(dp_ref, z_ref, w_ref, sc_ref, dz_ref, dw_ref, dsc_ref):
        g = pl.program_id(0)
        diff, count = _pool_diff(z_ref[...], g)
        dpf = dp_ref[...].astype(F32)
        u = _dot(diff, w_ref[...], NN)
        dsc_ref[...] = jnp.sum(dpf * u, axis=0, keepdims=True)
        du = (dpf * sc_ref[...]).astype(BF16)
        dw_ref[...] = _dot(diff, du, TN)
        ddiff = _dot(du, w_ref[...], NT)
        t = lax.broadcasted_iota(jnp.int32, ddiff.shape, 0)
        acc = ddiff / count
        sums = []
        for k in (1, 2, 4, 8):
            acc = acc + jnp.where(t < s - k, pltpu.roll(acc, s - k, 0), 0.0)
            sums.append(acc)
        win = jnp.where(g == 0, sums[0], jnp.where(g == 1, sums[1], jnp.where(g == 2, sums[2], sums[3])))
        dz_ref[...] = (win - ddiff).astype(dz_ref.dtype)

    return pl.pallas_call(
        body, name="pool_bwd", grid=(POOL_GROUPS,),
        in_specs=[pl.BlockSpec((s, POOL_CH), lambda g: (0, 4 + g)),
                  pl.BlockSpec((s, POOL_CH), lambda g: (0, 4 + g)),
                  pl.BlockSpec((None, POOL_CH, POOL_CH), lambda g: (g, 0, 0)),
                  pl.BlockSpec((1, POOL_CH), lambda g: (0, g))],
        out_specs=[pl.BlockSpec((s, POOL_CH), lambda g: (0, g)),
                   pl.BlockSpec((None, POOL_CH, POOL_CH), lambda g: (g, 0, 0)),
                   pl.BlockSpec((1, POOL_CH), lambda g: (0, g))],
        out_shape=[jax.ShapeDtypeStruct((s, POOL_GROUPS * POOL_CH), BF16),
                   jax.ShapeDtypeStruct((POOL_GROUPS, POOL_CH, POOL_CH), F32),
                   jax.ShapeDtypeStruct((1, POOL_GROUPS * POOL_CH), F32)],
        compiler_params=_params("parallel"),
    )(dcat, z, pool_w, pool_scale)


def _local_step(x, mem, positions, target, w, grads):
    tc, ta, tb = _rope_tables(positions)
    blk = _ATT_BLOCK

    if "ffn1_shards" in w:
        n1, a1, dadu1, dadg1, w["ffn1_w_gate"], w["ffn1_w_up"], w["ffn1_w_down"] = _ffn1_up_gather(
            x, w["ffn1_norm"], *w["ffn1_shards"])
    else:
        n1 = _rmsnorm_fwd("ffn1_norm", x, w["ffn1_norm"], D_MODEL)
        a1, dadu1, dadg1 = _ffn_up("ffn1_up", n1, w["ffn1_w_gate"], w["ffn1_w_up"])
    h1, n2 = _ffn_down("ffn1_down", a1, w["ffn1_w_down"], x, w["mix_norm"])
    z = _w_in_fwd(n2, w["w_in"])
    qn, kvn, qf, kf, vf = _qkv_prep(z, w["q_norm"], w["kv_norm"], w["w_q_up"], w["w_kv_up"], tc, ta, tb)
    att, lse = _attn_fwd("mla_fwd", qf, kf, 0, vf, 0, MLA_HEADS, HEAD_QK, HEAD_V, MLA_SCALE, True, blk)
    pool = _pool_fwd(z, w["pool_w"], w["pool_scale"])
    s = x.shape[0]
    bm = _row_block(s)
    row = pl.BlockSpec((bm, D_MODEL), lambda i, k: (i, 0))
    half = pl.BlockSpec((bm, 512), lambda i, k: (i, 0))
    h2, n3 = _matmul(
        "w_out", (s // bm, 1),
        [(att, half, w["w_out"], pl.BlockSpec((512, D_MODEL), lambda i, k: (0, 0)), NN),
         (pool, half, w["w_out"], pl.BlockSpec((512, D_MODEL), lambda i, k: (1, 0)), NN)],
        [(h1, row)] + _residual_outs(s, bm, w["xattn_norm"])[0], _residual_outs(s, bm, w["xattn_norm"])[1],
        _residual_epilogue(1.0, True), None)
    memn = _rmsnorm_fwd("mem_norm", mem, w["mem_norm"], D_MODEL)
    qm = _mm_nn("w_mq", n3, w["w_mq"], BF16)
    kvm = _mm_heads_fwd("w_mkv", memn, w["w_mkv"], BF16)
    om, lse_m = _attn_fwd("xattn_fwd", qm, kvm, 0, kvm, MEM_HEADS, MEM_HEADS, MEM_HEAD_DIM, MEM_HEAD_DIM,
                          MEM_SCALE, False, 2 * blk)
    h3, n4 = _mm_nn("w_mo", om, w["w_mo"], F32, res=h2, gain=w["ffn2_norm"])
    a2, dadu2, dadg2 = _ffn_up("ffn2_up", n4, w["ffn2_w_gate"], w["ffn2_w_up"])
    dh4, dh4b, loss_vec, d_final = _ffn_down("ffn2_down", a2, w["ffn2_w_down"], h3, loss=(w["final_norm"], target))
    grads["final_norm"] = d_final

    dh3, dh3b, grads["ffn2_norm"] = _ffn_bwd("ffn2", dh4b, n4, dadg2, dadu2, a2, w["ffn2_w_gate"], w["ffn2_w_up"],
                                             w["ffn2_w_down"], grads, norm_bwd=(h3, w["ffn2_norm"], dh4),
                                             many_calls=False)

    dom, delta_m = _mm_nt("w_mo_dx", dh3b, w["w_mo"], BF16, attn_out=om, nh=MEM_HEADS, dv=MEM_HEAD_DIM)
    dqm, dkm, dvm = _attn_bwd("xattn_bwd", qm, kvm, 0, kvm, MEM_HEADS, dom, 0, lse_m, delta_m, MEM_HEADS,
                              MEM_HEAD_DIM, MEM_HEAD_DIM, MEM_SCALE, False, 8 * blk)
    dkvm = jnp.concatenate([dkm, dvm], axis=1)
    dh2, dh2b, grads["xattn_norm"] = _mm_nt_norm_bwd("w_mq_dx", dqm, w["w_mq"], h2, w["xattn_norm"], dh3)
    grads["w_mo"], grads["w_mq"] = _mm_tn_pairs("w_mo_mq_dw", [(om, dh3b), (n3, dqm)])
    dmemn, grads["w_mkv"] = _mm_heads_bwd("w_mkv", dkvm, memn, w["w_mkv"])
    _, grads["mem_norm"] = _rmsnorm_bwd("mem_norm_bwd", dmemn, mem, w["mem_norm"], D_MODEL, out_dtype=BF16)

    dcat, delta = _mm_nt("w_out_dx", dh2b, w["w_out"], BF16, attn_out=att, nh=MLA_HEADS, dv=HEAD_V)
    grads["w_out"] = _mm_tn_stacked("w_out_dw", [att, pool], dh2b)
    dzp, grads["pool_w"], grads["pool_scale"] = _pool_bwd(dcat, z, w["pool_w"], w["pool_scale"])
    dqf, dkf, dvf = _attn_bwd("mla_bwd", qf, kf, 0, vf, 0, dcat, 0, lse, delta, MLA_HEADS, HEAD_QK, HEAD_V,
                              MLA_SCALE, True, blk)
    dz_lat, grads["q_norm"], grads["kv_norm"], grads["w_q_up"], grads["w_kv_up"] = _qkv_prep_bwd(
        dqf, dkf, dvf, z, qn, kvn, w["q_norm"], w["kv_norm"], w["w_q_up"], w["w_kv_up"], tc, ta, tb)
    dz = jnp.concatenate([dz_lat, dzp], axis=1)
    grads["w_in"] = _w_in_dw(dz, n2)
    dh1, dh1b, grads["mix_norm"] = _w_in_dx_norm_bwd(dz, w["w_in"], h1, w["mix_norm"], dh2)

    dn1 = _ffn_bwd("ffn1", dh1b, n1, dadg1, dadu1, a1, w["ffn1_w_gate"], w["ffn1_w_up"], w["ffn1_w_down"], grads)
    dx, grads["ffn1_norm"] = _rmsnorm_bwd("ffn1_norm_bwd", dn1, x, w["ffn1_norm"], D_MODEL, dres=dh1)
    return loss_vec[0, 0], dx


def _mesh_pos():
    x, y, c = lax.axis_index("x"), lax.axis_index("y"), lax.axis_index("c")
    chips = [(1 - x, y), (x, 1 - y), (1 - x, 1 - y)]
    chip_ids = [2 * cx + cy for cx, cy in chips]
    return x, y, c, 2 * x + y, chips, chip_ids


def _half_rows(c, rows):
    hr = rows // 2
    return pl.ds(pl.multiple_of(c * hr, 16), hr), pl.ds(pl.multiple_of((1 - c) * hr, 16), hr)


def _ag_ici_stage(shards):
    n = len(shards)

    def copies(ins, outs):
        x, y, c, me, chips, _ = _mesh_pos()
        out = []
        for k in range(n):
            mine, _ = _half_rows(c, ins[k].shape[0])
            out.append((ins[k], outs[k].at[me], None))
            for cx, cy in chips:
                out.append((ins[k].at[mine], outs[k].at[me, mine], (cx, cy, c)))
        return out

    return _Stage(shards, [jax.ShapeDtypeStruct((N_CHIPS,) + s.shape, s.dtype) for s in shards], 3 * n, n, copies)


def _quarter_rows(c, rows):
    qr = rows // 4
    return pl.ds(pl.multiple_of(c * 2 * qr, 16), qr), pl.ds(pl.multiple_of(c * 2 * qr + qr, 16), qr)


def _ag_d2d_stage(fulls):
    n = len(fulls)

    def copies(ins, outs):
        x, y, c, me, _, chip_ids = _mesh_pos()
        out = []
        for k in range(n):
            mine, _ = _half_rows(c, ins[k].shape[1])
            for j in range(3):
                out.append((ins[k].at[chip_ids[j], mine], outs[k].at[chip_ids[j], mine], (x, y, 1 - c)))
        return out

    return _Stage(fulls, [jax.ShapeDtypeStruct(f.shape, f.dtype) for f in fulls], 3 * n, 0, copies,
                  aliases={k: k for k in range(n)})


def _rs_swap_stage(grads):
    n = len(grads)

    def copies(ins, outs):
        x, y, c, _, _, _ = _mesh_pos()
        out = []
        for k in range(n):
            _, other = _half_rows(c, ins[k].shape[1])
            out.append((ins[k].at[:, other, :], outs[k], (x, y, 1 - c)))
        return out

    return _Stage(grads, [jax.ShapeDtypeStruct((N_CHIPS, g.shape[1] // 2, g.shape[2]), g.dtype) for g in grads],
                  n, 0, copies)


_REL_OF_PEER = (2, 1, 3)


def _rs_scatter_stage(sums, relative=False):
    n = len(sums)

    def copies(ins, outs):
        x, y, c, me, chips, chip_ids = _mesh_pos()
        out = []
        for k in range(n):
            mine, _ = _half_rows(c, 2 * ins[k].shape[1])
            out.append((ins[k].at[0 if relative else me], outs[k].at[0, mine, :], None))
            for j, (cx, cy) in enumerate(chips):
                slab = _REL_OF_PEER[j] if relative else chip_ids[j]
                out.append((ins[k].at[slab], outs[k].at[1 + j, mine, :], (cx, cy, c)))
        return out

    return _Stage(sums, [jax.ShapeDtypeStruct((N_CHIPS, 2 * s.shape[1], s.shape[2]), s.dtype) for s in sums],
                  3 * n, n, copies)


def _rs_mirror_stage(parts):
    n = len(parts)

    def copies(ins, outs):
        x, y, c, _, _, _ = _mesh_pos()
        out = []
        for k in range(n):
            mine, _ = _half_rows(c, ins[k].shape[1])
            out.append((ins[k].at[:, mine, :], outs[k].at[:, mine, :], (x, y, 1 - c)))
        return out

    return _Stage(parts, [jax.ShapeDtypeStruct(p.shape, p.dtype) for p in parts], n, 0, copies,
                  aliases={k: k for k in range(n)})


def _pair_add(name, gs, r1s, core):
    n = len(gs)

    def body(c_ref, *refs):
        for k in range(n):
            g_ref, r_ref, o_ref = refs[k], refs[n + k], refs[2 * n + k]
            o_ref[...] = (g_ref[...].astype(F32) + r_ref[...].astype(F32)).astype(BF16)

    def half(g):
        return pl.BlockSpec((None, g.shape[1] // 2, g.shape[2]), lambda j, c: (j, 0, 0))

    def mine(g):
        return pl.BlockSpec((None, g.shape[1] // 2, g.shape[2]), lambda j, c: (j, c[0], 0))

    return pl.pallas_call(
        body, name=name,
        grid_spec=pltpu.PrefetchScalarGridSpec(
            num_scalar_prefetch=1, grid=(N_CHIPS,),
            in_specs=[mine(g) for g in gs] + [half(g) for g in gs], out_specs=[half(g) for g in gs]),
        out_shape=[jax.ShapeDtypeStruct((N_CHIPS, g.shape[1] // 2, g.shape[2]), BF16) for g in gs],
        compiler_params=_params("parallel"),
    )(core, *gs, *r1s)


def _adamw_math(w, g, m, v):
    m = ADAM_B1 * m + (1.0 - ADAM_B1) * g
    v = ADAM_B2 * v + (1.0 - ADAM_B2) * (g * g)
    m_hat = m / (1.0 - ADAM_B1 ** ADAM_STEP)
    v_hat = v / (1.0 - ADAM_B2 ** ADAM_STEP)
    delta = -ADAM_LR * (m_hat / (jnp.sqrt(v_hat) + ADAM_EPS) + ADAM_WD * w)
    return delta, m, v


def _adamw_sum(name, ws, parts, ms, vs):
    n = len(ws)
    r, c = ws[0].shape
    assert all(w.shape == (r, c) for w in ws)
    br = r
    while br * c * 4 > (1 << 20) and br % 32 == 0:
        br //= 2

    def body(*refs):
        for k in range(n):
            w_ref, p_ref, m_ref, v_ref = refs[4 * k:4 * k + 4]
            g_ref, d_ref, nm_ref, nv_ref = refs[4 * n + 4 * k:4 * n + 4 * k + 4]
            g = p_ref[0].astype(F32)
            for j in range(1, N_CHIPS):
                g = g + p_ref[j].astype(F32)
            d, nm, nv = _adamw_math(w_ref[...], g, m_ref[...], v_ref[...])
            g_ref[...] = g
            d_ref[...] = d
            nm_ref[...] = nm
            nv_ref[...] = nv

    spec = pl.BlockSpec((br, c), lambda i: (i, 0))
    shp = jax.ShapeDtypeStruct((r, c), F32)
    args = [a for k in range(n) for a in (ws[k], parts[k], ms[k], vs[k])]
    res = _call(
        body, name=name, grid=(r // br,),
        in_specs=[spec, pl.BlockSpec((N_CHIPS, br, c), lambda i: (0, i, 0)), spec, spec] * n,
        out_specs=[spec] * (4 * n), out_shape=[shp] * (4 * n), sem=("parallel",), args=args)
    return [res[4 * k:4 * k + 4] for k in range(n)]


_SMALL_VECTORS = ("ffn1_norm", "mix_norm", "xattn_norm", "mem_norm", "ffn2_norm", "final_norm", "q_norm",
                  "kv_norm", "pool_scale")


_LOSS_ROW = 9
_VEC_ROWS = 16
_POOL_ROWS = POOL_GROUPS * POOL_CH


def _small_params_step(g, w, m, v, loss_local):
    names = list(_SMALL_VECTORS) + ["pool_w"]
    nv = len(_SMALL_VECTORS)
    widths = [g[n].shape[1] for n in _SMALL_VECTORS]
    shapes = {"vec": (_VEC_ROWS, D_MODEL), "pool": (_POOL_ROWS, POOL_CH)}

    def body(*refs):
        ins = refs[:4 * (nv + 1) + 1]
        outs = refs[len(ins):len(ins) + 4 * (nv + 1) + 1]
        vec_own, vec_sib, vec_all, pool_sib, pool_sum, pool_all, send, recv = refs[len(ins) + len(outs):]
        g_in, w_in, m_in, v_in = (ins[k * (nv + 1):(k + 1) * (nv + 1)] for k in range(4))
        loss_in = ins[-1]
        g_out, d_out, m_out, v_out = (outs[k * (nv + 1):(k + 1) * (nv + 1)] for k in range(4))
        loss_out = outs[-1]
        x, y, c, me, chips, chip_ids = _mesh_pos()
        sib = (x, y, 1 - c)

        def remote(src, dst, k, dev):
            return pltpu.make_async_remote_copy(src_ref=src, dst_ref=dst, send_sem=send.at[k], recv_sem=recv.at[k],
                                                device_id=dev, device_id_type=_MESH)

        vec_own[...] = jnp.zeros_like(vec_own)
        for i in range(nv):
            vec_own[i:i + 1, 0:widths[i]] = g_in[i][...]
        vec_own[_LOSS_ROW:_LOSS_ROW + 1, 0:128] = loss_in[...]
        swaps = [remote(vec_own, vec_sib, 0, sib), remote(g_in[nv], pool_sib, 1, sib)]
        for cp in swaps:
            cp.start()
        for cp in swaps:
            cp.wait()
        vec_all[me] = vec_own[...] + vec_sib[...]
        pool_sum[...] = g_in[nv][...] + pool_sib[...]
        pool_all[me] = pool_sum[...]
        hv, hp = _VEC_ROWS // 2, _POOL_ROWS // 2
        mine_v = pl.ds(pl.multiple_of(c * hv, 8), hv)
        mine_p = pl.ds(pl.multiple_of(c * hp, 8), hp)
        sends = []
        for j, (cx, cy) in enumerate(chips):
            sends.append(remote(vec_all.at[me, mine_v], vec_all.at[me, mine_v], 2 + j, (cx, cy, c)))
            sends.append(remote(pool_sum.at[mine_p], pool_all.at[me, mine_p], 5 + j, (cx, cy, c)))
        for cp in sends:
            cp.start()
        for cp in sends:
            cp.wait()
        mirrors = []
        for j in range(3):
            mirrors.append(remote(vec_all.at[chip_ids[j], mine_v], vec_all.at[chip_ids[j], mine_v], 8 + j, sib))
            mirrors.append(remote(pool_all.at[chip_ids[j], mine_p], pool_all.at[chip_ids[j], mine_p], 11 + j, sib))
        for cp in mirrors:
            cp.start()
        for cp in mirrors:
            cp.wait()
        vec_tot = vec_all[0]
        pool_tot = pool_all[0]
        for i in range(1, N_CHIPS):
            vec_tot = vec_tot + vec_all[i]
            pool_tot = pool_tot + pool_all[i]
        vec_sib[...] = vec_tot
        loss_out[...] = vec_sib[_LOSS_ROW:_LOSS_ROW + 1, 0:128]
        for i in range(nv + 1):
            gi = pool_tot if i == nv else vec_sib[i:i + 1, 0:widths[i]]
            d, nm, nvv = _adamw_math(w_in[i][...], gi, m_in[i][...], v_in[i][...])
            g_out[i][...] = gi
            d_out[i][...] = d
            m_out[i][...] = nm
            v_out[i][...] = nvv

    vm = pl.BlockSpec(memory_space=pltpu.VMEM)
    args = [d[n] for d in (g, w, m, v) for n in names] + [jnp.broadcast_to(loss_local.reshape(1, 1), (1, 128))]
    out_shape = [jax.ShapeDtypeStruct(g[n].shape, F32) for _ in range(4) for n in names]
    out_shape.append(jax.ShapeDtypeStruct((1, 128), F32))
    res = pl.pallas_call(
        body, name="small_params_step", in_specs=[vm] * len(args), out_specs=[vm] * len(out_shape),
        out_shape=out_shape,
        scratch_shapes=[pltpu.VMEM(shapes["vec"], F32), pltpu.VMEM(shapes["vec"], F32),
                        pltpu.VMEM((N_CHIPS,) + shapes["vec"], F32), pltpu.VMEM(shapes["pool"], F32),
                        pltpu.VMEM(shapes["pool"], F32), pltpu.VMEM((N_CHIPS,) + shapes["pool"], F32),
                        pltpu.SemaphoreType.DMA((14,)), pltpu.SemaphoreType.DMA((14,))],
        compiler_params=pltpu.CompilerParams(vmem_limit_bytes=V7X_VMEM_LIMIT_BYTES),
    )(*args)
    k = len(names)
    dicts = [dict(zip(names, res[i * k:(i + 1) * k])) for i in range(4)]
    return dicts[0], dicts[1], dicts[2], dicts[3], res[-1]


_WEIGHTS = ("ffn1_norm", "ffn1_w_gate", "ffn1_w_up", "ffn1_w_down", "mix_norm", "w_in", "q_norm", "w_q_up",
            "kv_norm", "w_kv_up", "pool_w", "pool_scale", "w_out", "xattn_norm", "mem_norm", "w_mq", "w_mkv",
            "w_mo", "ffn2_norm", "ffn2_w_gate", "ffn2_w_up", "ffn2_w_down", "final_norm")
_SHARDED = ("ffn1_w_gate", "ffn1_w_up", "ffn1_w_down", "w_in", "w_q_up", "w_kv_up", "w_out", "w_mq", "w_mkv",
            "w_mo", "ffn2_w_gate", "ffn2_w_up", "ffn2_w_down")
W_IN_SPLIT = Q_LORA + KV_LORA + ROPE_DIM


_FFN1 = ("ffn1_w_gate", "ffn1_w_up", "ffn1_w_down")
_TRANSPOSED = ("ffn1_w_gate", "ffn1_w_up", "ffn2_w_gate", "ffn2_w_up", "w_in", "w_q_up")


def _local_view(name, a):
    return jnp.swapaxes(a, 1, 2)[0] if name in _TRANSPOSED else a[0]


def _global_view(name, a):
    return jnp.swapaxes(a[None], 1, 2) if name in _TRANSPOSED else a[None]


def _pad_shard(name, a):
    if name == "w_in":
        return jnp.concatenate([a[:W_IN_SPLIT], jnp.zeros((64, a.shape[1]), a.dtype), a[W_IN_SPLIT:]], axis=0)
    if name == "w_q_up":
        return jnp.pad(a, ((0, 64), (0, 0)))
    return a


def _unpad_shard(name, a):
    if name == "w_in":
        return jnp.concatenate([a[:, :W_IN_SPLIT], a[:, W_IN_SPLIT + 64:]], axis=1)
    if name == "w_q_up":
        return a[:, :192]
    return a


def _stacked(g):
    return g if g.ndim == 3 else g.reshape(N_CHIPS, g.shape[0] // N_CHIPS, g.shape[1])


class _Plan:
    AG_UNITS = (
        (("w_in", "w_q_up", "w_kv_up"), "ffn1_up"),
        (("w_out",), "w_in"),
        (("w_mq",), "qkv_prep"),
        (("w_mkv", "w_mo", "ffn2_w_gate"), "mla_fwd"),
        (("ffn2_w_up",), "xattn_fwd"),
        (("ffn2_w_down",), "ffn2_up"),
    )
    RS_UNITS = (
        (("ffn2_w_gate", "ffn2_w_up", "ffn2_w_down"), "ffn2_dn", "mla_bwd", "qkv_prep_bwd"),
        (("w_mo", "w_mq", "w_mkv"), "w_out_dx", "mla_bwd", "qkv_prep_bwd"),
        (("w_out", "w_q_up", "w_kv_up", "w_in"), "w_in_dx", "ffn1_dact", "ffn1_dwd"),
        (("ffn1_w_down",), "ffn1_dwg", "ffn1_dwu", "ffn1_dn_a"),
        (("ffn1_w_gate",), "ffn1_dwu", "ffn1_dn_a", "ffn1_dn_b"),
        (("ffn1_w_up",), "ffn1_dn_a", "ffn1_dn_b", "adamw_w_kv_up"),
    )
    ADAMW_ORDER = (("w_kv_up",), ("ffn2_w_gate", "ffn2_w_up"), ("ffn2_w_down", "ffn1_w_down"), ("w_mo", "w_mq", "w_out"),
                   ("w_mkv",), ("w_q_up",), ("w_in",), ("ffn1_w_gate", "ffn1_w_up"))

    def __init__(self, shards, w, grads, core):
        self.shards, self.w, self.grads, self.core = shards, w, grads, core
        self.last_slab_step = 0
        self.parts = {}
        self.ag = [None for _ in self.AG_UNITS]
        self.rs = [[None, None, None, None] for _ in self.RS_UNITS]

    def pre(self, name):
        for i, (names, host) in enumerate(self.AG_UNITS):
            if name == host:
                st = _ag_ici_stage([self.shards[n] for n in names])
                st.then = _ag_d2d_stage(st.outs)
                st.start_step = self.last_slab_step if name == "ffn1_up" else 0
                self.ag[i] = _host(name, st)
        for i, (names, h1, h2, h3) in enumerate(self.RS_UNITS):
            if name == h1:
                self.rs[i][0] = _host(name, _rs_swap_stage([_stacked(self.grads[n]) for n in names]))
            if name == h2:
                self.rs[i][2] = _host(name, _rs_scatter_stage(self.rs[i][1], relative=names[0] in _FFN1))
            if name == h3:
                self.rs[i][3] = _host(name, _rs_mirror_stage(self.rs[i][2].results))

    def post(self, name):
        for i, (names, host) in enumerate(self.AG_UNITS):
            if name == host:
                for n, f in zip(names, self.ag[i].results):
                    self.w[n] = _full_weight(n, f)
        for i, (names, h1, h2, h3) in enumerate(self.RS_UNITS):
            if name == h1:
                self.rs[i][1] = list(_pair_add("pair_add_" + names[0], [_stacked(self.grads[n]) for n in names],
                                               self.rs[i][0].results, self.core))
            if name == h3:
                for n, p in zip(names, self.rs[i][3].results):
                    self.parts[n] = p


def _full_weight(name, stacked):
    if name in ("w_out", "w_mq", "w_mo"):
        return stacked.reshape(D_MODEL, D_MODEL)
    return stacked


def kernel(x, mem, positions, ffn1_norm, ffn1_w_gate, ffn1_w_up, ffn1_w_down, mix_norm, w_in, q_norm, w_q_up, kv_norm, w_kv_up, pool_w, pool_scale, w_out, xattn_norm, mem_norm, w_mq, w_mkv, w_mo, ffn2_norm, ffn2_w_gate, ffn2_w_up, ffn2_w_down, final_norm, loss_target, m_ffn1_norm, m_ffn1_w_gate, m_ffn1_w_up, m_ffn1_w_down, m_mix_norm, m_w_in, m_q_norm, m_w_q_up, m_kv_norm, m_w_kv_up, m_pool_w, m_pool_scale, m_w_out, m_xattn_norm, m_mem_norm, m_w_mq, m_w_mkv, m_w_mo, m_ffn2_norm, m_ffn2_w_gate, m_ffn2_w_up, m_ffn2_w_down, m_final_norm, v_ffn1_norm, v_ffn1_w_gate, v_ffn1_w_up, v_ffn1_w_down, v_mix_norm, v_w_in, v_q_norm, v_w_q_up, v_kv_norm, v_w_kv_up, v_pool_w, v_pool_scale, v_w_out, v_xattn_norm, v_mem_norm, v_w_mq, v_w_mkv, v_w_mo, v_ffn2_norm, v_ffn2_w_gate, v_ffn2_w_up, v_ffn2_w_down, v_final_norm):
    wts = dict(zip(_WEIGHTS, (ffn1_norm, ffn1_w_gate, ffn1_w_up, ffn1_w_down, mix_norm, w_in, q_norm, w_q_up, kv_norm, w_kv_up, pool_w, pool_scale, w_out, xattn_norm, mem_norm, w_mq, w_mkv, w_mo, ffn2_norm, ffn2_w_gate, ffn2_w_up, ffn2_w_down, final_norm)))
    mom = dict(zip(_WEIGHTS, (m_ffn1_norm, m_ffn1_w_gate, m_ffn1_w_up, m_ffn1_w_down, m_mix_norm, m_w_in, m_q_norm, m_w_q_up, m_kv_norm, m_w_kv_up, m_pool_w, m_pool_scale, m_w_out, m_xattn_norm, m_mem_norm, m_w_mq, m_w_mkv, m_w_mo, m_ffn2_norm, m_ffn2_w_gate, m_ffn2_w_up, m_ffn2_w_down, m_final_norm)))
    var = dict(zip(_WEIGHTS, (v_ffn1_norm, v_ffn1_w_gate, v_ffn1_w_up, v_ffn1_w_down, v_mix_norm, v_w_in, v_q_norm, v_w_q_up, v_kv_norm, v_w_kv_up, v_pool_w, v_pool_scale, v_w_out, v_xattn_norm, v_mem_norm, v_w_mq, v_w_mkv, v_w_mo, v_ffn2_norm, v_ffn2_w_gate, v_ffn2_w_up, v_ffn2_w_down, v_final_norm)))
    small = [n for n in _WEIGHTS if n not in _SHARDED]

    global _PLAN
    shards = {n: _pad_shard(n, _local_view(n, wts[n])).astype(BF16) for n in _SHARDED}
    w = {n: wts[n].reshape(1, -1) for n in _SMALL_VECTORS}
    w["pool_w"] = pool_w[0].astype(BF16)
    grads = {}
    core = lax.axis_index("c").astype(jnp.int32).reshape(1)
    plan = _Plan(shards, w, grads, core)
    _PLAN = plan
    try:
        w["ffn1_shards"] = tuple(shards[n] for n in _FFN1)

        loss_local, dx = _local_step(x[0], mem[0], positions[0], loss_target[0], w, grads)

        def small_view(d):
            out = {n: d[n].reshape(1, -1) for n in _SMALL_VECTORS}
            out["pool_w"] = d["pool_w"].reshape(_POOL_ROWS, POOL_CH)
            return out

        *small_res, loss_vec = _small_params_step(small_view(grads), small_view(wts), small_view(mom),
                                                  small_view(var), loss_local)
        g_out, d_out, m_out, v_out = ({n: r[n].reshape(wts[n].shape) for n in small} for r in small_res)
        loss = loss_vec[0, 0]

        for names in _Plan.ADAMW_ORDER:
            res = _adamw_sum("adamw_" + names[0], [_local_view(n, wts[n]) for n in names],
                             [_unpad_shard(n, plan.parts[n]) for n in names],
                             [_local_view(n, mom[n]) for n in names], [_local_view(n, var[n]) for n in names])
            for n, r4 in zip(names, res):
                g_out[n], d_out[n], m_out[n], v_out[n] = (_global_view(n, r) for r in r4)
    finally:
        _PLAN = None
        _PENDING.clear()

    return (loss, dx[None], *[g_out[n] for n in _WEIGHTS], *[d_out[n] for n in _WEIGHTS],
            *[m_out[n] for n in _WEIGHTS], *[v_out[n] for n in _WEIGHTS])
```
